```python
import jax, jax.numpy as jnp
from jax import lax
import numpy as np

D_MODEL = 1024
BATCH = 16
SEQ = 2048
DEPTH = 1

HEAD_DIM = 64
HEADS_PER_GROUP = 4
DILATED_GROUPS = ((128, 1), (512, 4), (2048, 16))
N_GROUPS = 3
N_ATTN_HEADS = N_GROUPS * HEADS_PER_GROUP
ATTN_WIDTH = N_ATTN_HEADS * HEAD_DIM
ATTN_OUT_WIDTH = HEADS_PER_GROUP * HEAD_DIM
CONV_WIDTH = D_MODEL
CONV_K = 3
SUB_BLOCK = 128
ALIBI_MAX_EXP = 8.0
DEEPNORM_ALPHA = (2.0 * DEPTH) ** 0.25
DEEPNORM_BETA = (8.0 * DEPTH) ** -0.25
LN_EPS = 1e-5
IN_WIDTHS = (ATTN_WIDTH, ATTN_WIDTH, ATTN_WIDTH, ATTN_OUT_WIDTH,
             CONV_WIDTH, CONV_WIDTH, CONV_WIDTH, CONV_WIDTH, D_MODEL, D_MODEL)
D_IN = 3 * ATTN_WIDTH + ATTN_OUT_WIDTH + 4 * CONV_WIDTH + 2 * D_MODEL

kernel_name = "hybrid_dilated_attn_shortconv_deepnorm_adaln"


def _split_points():
    pts, acc = [], 0
    for w in IN_WIDTHS[:-1]:
        acc += w
        pts.append(acc)
    return pts


def _layer_norm(x, g, b):
    xf = x.astype(jnp.float32)
    mu = xf.mean(-1, keepdims=True)
    var = jnp.square(xf - mu).mean(-1, keepdims=True)
    y = (xf - mu) * lax.rsqrt(var + LN_EPS) * g.astype(jnp.float32) + b.astype(jnp.float32)
    return y.astype(x.dtype)


def _dilated_window_attention(q, k, v, window, dilation, slopes):
    bsz, s, h, dh = q.shape
    span = window // dilation
    n = s // dilation
    L = SUB_BLOCK
    nb = -(-n // L)
    pad = nb * L - n

    def to_sub(t):
        t = t.reshape(bsz, n, dilation, h, dh).transpose(0, 2, 1, 3, 4)
        t = jnp.pad(t, ((0, 0), (0, 0), (0, pad), (0, 0), (0, 0)))
        return t.reshape(bsz, dilation, nb, L, h, dh)

    def with_prev(t):
        prev = jnp.pad(t, ((0, 0), (0, 0), (1, 0), (0, 0), (0, 0), (0, 0)))[:, :, :-1]
        return jnp.concatenate([prev, t], axis=3)

    qs = to_sub(q)
    kb = with_prev(to_sub(k))
    vb = with_prev(to_sub(v))

    scores = jnp.einsum('brnqhd,brnkhd->brnhqk', qs, kb).astype(jnp.float32) * (dh ** -0.5)
    qi = jnp.arange(L)[:, None]
    kj = jnp.arange(2 * L)[None, :]
    delta = qi + L - kj
    key_sub = jnp.arange(nb)[:, None, None] * L + kj[None] - L
    valid = (delta >= 0) & (delta <= span) & (key_sub >= 0)
    alibi = -slopes[:, None, None] * (delta * dilation).astype(jnp.float32)[None]
    scores = scores + alibi[None, None, None]
    scores = jnp.where(valid[None, None, :, None], scores, -jnp.inf)
    m = scores.max(-1, keepdims=True)
    p = jnp.exp(scores - m)
    den = p.sum(-1, keepdims=True)
    out = jnp.einsum('brnhqk,brnkhd->brnqhd', (p / den).astype(v.dtype), vb)
    lse = jnp.swapaxes((m + jnp.log(den))[..., 0], 3, 4)

    def from_sub(t):
        t = t.reshape((bsz, dilation, nb * L) + t.shape[4:])[:, :, :n]
        return jnp.swapaxes(t, 1, 2).reshape((bsz, s) + t.shape[3:])

    return from_sub(out), from_sub(lse)


def _fwd_setup_inputs(seed: int = 0) -> dict:
    key = jax.random.key(seed)
    ks = jax.random.split(key, 16)
    f32 = jnp.float32
    x = jax.random.normal(ks[0], (BATCH, SEQ, D_MODEL), f32)
    c = jax.random.normal(ks[1], (BATCH, D_MODEL), f32)
    w_ada = jax.random.normal(ks[2], (DEPTH, D_MODEL, 3 * D_MODEL), f32) * (D_MODEL ** -0.5) * 0.5
    b_ada = jax.random.normal(ks[3], (DEPTH, 3 * D_MODEL), f32) * 0.01
    w_in = jax.random.normal(ks[4], (DEPTH, D_MODEL, D_IN), f32) * (D_MODEL ** -0.5)
    w_in = w_in.at[:, :, 2 * ATTN_WIDTH:3 * ATTN_WIDTH].multiply(DEEPNORM_BETA)
    b_in = jax.random.normal(ks[5], (DEPTH, D_IN), f32) * 0.01
    conv_w = jax.random.normal(ks[6], (DEPTH, CONV_K, CONV_WIDTH), f32) * (CONV_K ** -0.5)
    w_proj_attn = jax.random.normal(ks[7], (DEPTH, ATTN_OUT_WIDTH, D_MODEL), f32) * (ATTN_OUT_WIDTH ** -0.5) * DEEPNORM_BETA
    w_proj_conv = jax.random.normal(ks[8], (DEPTH, CONV_WIDTH, D_MODEL), f32) * (CONV_WIDTH ** -0.5) * DEEPNORM_BETA
    w_out = jax.random.normal(ks[9], (DEPTH, D_MODEL, D_MODEL), f32) * (D_MODEL ** -0.5) * DEEPNORM_BETA
    b_out = jax.random.normal(ks[10], (DEPTH, D_MODEL), f32) * 0.01
    ln_g = 1.0 + 0.02 * jax.random.normal(ks[11], (DEPTH, D_MODEL), f32)
    ln_b = 0.02 * jax.random.normal(ks[12], (DEPTH, D_MODEL), f32)
    return {"x": x, "c": c, "w_ada": w_ada, "b_ada": b_ada, "w_in": w_in, "b_in": b_in,
            "conv_w": conv_w, "w_proj_attn": w_proj_attn, "w_proj_conv": w_proj_conv,
            "w_out": w_out, "b_out": b_out, "ln_g": ln_g, "ln_b": ln_b}


def _fwd_reference(x, c, w_ada, b_ada, w_in, b_in, conv_w, w_proj_attn, w_proj_conv, w_out, b_out, ln_g, ln_b):
    bsz, s, _ = x.shape
    split_pts = _split_points()
    slopes = 2.0 ** (-ALIBI_MAX_EXP * (jnp.arange(N_ATTN_HEADS, dtype=jnp.float32) + 1.0) / N_ATTN_HEADS)
    c_act = jax.nn.silu(c)
    for layer in range(DEPTH):
        ada = (c_act @ w_ada[layer] + b_ada[layer])[:, None, :]
        shift, scale, gate = jnp.split(ada, 3, axis=-1)
        h = x * (1.0 + scale) + shift

        proj = h @ w_in[layer] + b_in[layer]
        q, k, v, z_attn, u_x, g_b, g_c, z_conv, g_mix_a, g_mix_b = jnp.split(proj, split_pts, axis=-1)
        q = q.reshape(bsz, s, N_ATTN_HEADS, HEAD_DIM)
        k = k.reshape(bsz, s, N_ATTN_HEADS, HEAD_DIM)
        v = v.reshape(bsz, s, N_ATTN_HEADS, HEAD_DIM)

        outs, lses = [], []
        for g, (window, dilation) in enumerate(DILATED_GROUPS):
            hs = slice(g * HEADS_PER_GROUP, (g + 1) * HEADS_PER_GROUP)
            o_g, lse_g = _dilated_window_attention(q[:, :, hs], k[:, :, hs], v[:, :, hs],
                                                   window, dilation, slopes[hs])
            outs.append(o_g.astype(jnp.float32))
            lses.append(lse_g)
        mix_w = jax.nn.softmax(jnp.stack(lses), axis=0)
        o_attn = jnp.einsum('gbsh,gbshd->bshd', mix_w, jnp.stack(outs)).astype(x.dtype)
        o_attn = o_attn.reshape(bsz, s, ATTN_OUT_WIDTH)
        y_attn = (o_attn * jax.nn.silu(z_attn)) @ w_proj_attn[layer]

        u = g_c * u_x
        u_pad = jnp.pad(u, ((0, 0), (CONV_K - 1, 0), (0, 0)))
        cw = conv_w[layer]
        conv = cw[0] * u_pad[:, 0:s]
        for j in range(1, CONV_K):
            conv = conv + cw[j] * u_pad[:, j:j + s]
        y_conv = (g_b * conv * jax.nn.silu(z_conv)) @ w_proj_conv[layer]

        merged = jax.nn.sigmoid(g_mix_a) * y_attn + jax.nn.sigmoid(g_mix_b) * y_conv
        sub = gate * (merged @ w_out[layer] + b_out[layer])
        x = _layer_norm(DEEPNORM_ALPHA * x + sub, ln_g[layer], ln_b[layer])
    return x


import jax as _jax
import jax.numpy as _jnp

TWIN_FORMAT = 'train_step'
FWD_PARAMS = ['x', 'c', 'w_ada', 'b_ada', 'w_in', 'b_in', 'conv_w', 'w_proj_attn', 'w_proj_conv', 'w_out', 'b_out', 'ln_g', 'ln_b']
TWIN_WEIGHTS = ['w_ada', 'b_ada', 'w_in', 'b_in', 'conv_w', 'w_proj_attn', 'w_proj_conv', 'w_out', 'b_out', 'ln_g', 'ln_b']
TWIN_DIFF_INPUT = 'x'
TWIN_INPUTS = ['x', 'c', 'w_ada', 'b_ada', 'w_in', 'b_in', 'conv_w', 'w_proj_attn', 'w_proj_conv', 'w_out', 'b_out', 'ln_g', 'ln_b', 'loss_target', 'm_w_ada', 'm_b_ada', 'm_w_in', 'm_b_in', 'm_conv_w', 'm_w_proj_attn', 'm_w_proj_conv', 'm_w_out', 'm_b_out', 'm_ln_g', 'm_ln_b', 'v_w_ada', 'v_b_ada', 'v_w_in', 'v_b_in', 'v_conv_w', 'v_w_proj_attn', 'v_w_proj_conv', 'v_w_out', 'v_b_out', 'v_ln_g', 'v_ln_b']
TWIN_OUTPUTS = ['loss', 'grad_x', 'grad_w_ada', 'grad_b_ada', 'grad_w_in', 'grad_b_in', 'grad_conv_w', 'grad_w_proj_attn', 'grad_w_proj_conv', 'grad_w_out', 'grad_b_out', 'grad_ln_g', 'grad_ln_b', 'delta_w_ada', 'delta_b_ada', 'delta_w_in', 'delta_b_in', 'delta_conv_w', 'delta_w_proj_attn', 'delta_w_proj_conv', 'delta_w_out', 'delta_b_out', 'delta_ln_g', 'delta_ln_b', 'new_m_w_ada', 'new_m_b_ada', 'new_m_w_in', 'new_m_b_in', 'new_m_conv_w', 'new_m_w_proj_attn', 'new_m_w_proj_conv', 'new_m_w_out', 'new_m_b_out', 'new_m_ln_g', 'new_m_ln_b', 'new_v_w_ada', 'new_v_b_ada', 'new_v_w_in', 'new_v_b_in', 'new_v_conv_w', 'new_v_w_proj_attn', 'new_v_w_proj_conv', 'new_v_w_out', 'new_v_b_out', 'new_v_ln_g', 'new_v_ln_b']
TWIN_LEAF_KINDS = {'loss': 'loss', 'grad_x': 'grad_x', 'grad_w_ada': 'grad_w', 'grad_b_ada': 'grad_w', 'grad_w_in': 'grad_w', 'grad_b_in': 'grad_w', 'grad_conv_w': 'grad_w', 'grad_w_proj_attn': 'grad_w', 'grad_w_proj_conv': 'grad_w', 'grad_w_out': 'grad_w', 'grad_b_out': 'grad_w', 'grad_ln_g': 'grad_w', 'grad_ln_b': 'grad_w', 'delta_w_ada': 'delta_w', 'delta_b_ada': 'delta_w', 'delta_w_in': 'delta_w', 'delta_b_in': 'delta_w', 'delta_conv_w': 'delta_w', 'delta_w_proj_attn': 'delta_w', 'delta_w_proj_conv': 'delta_w', 'delta_w_out': 'delta_w', 'delta_b_out': 'delta_w', 'delta_ln_g': 'delta_w', 'delta_ln_b': 'delta_w', 'new_m_w_ada': 'new_m', 'new_m_b_ada': 'new_m', 'new_m_w_in': 'new_m', 'new_m_b_in': 'new_m', 'new_m_conv_w': 'new_m', 'new_m_w_proj_attn': 'new_m', 'new_m_w_proj_conv': 'new_m', 'new_m_w_out': 'new_m', 'new_m_b_out': 'new_m', 'new_m_ln_g': 'new_m', 'new_m_ln_b': 'new_m', 'new_v_w_ada': 'new_v', 'new_v_b_ada': 'new_v', 'new_v_w_in': 'new_v', 'new_v_b_in': 'new_v', 'new_v_conv_w': 'new_v', 'new_v_w_proj_attn': 'new_v', 'new_v_w_proj_conv': 'new_v', 'new_v_w_out': 'new_v', 'new_v_b_out': 'new_v', 'new_v_ln_g': 'new_v', 'new_v_ln_b': 'new_v'}


def _forward(args):
    return _fwd_reference(*[args[k] for k in FWD_PARAMS])


def _output_shape():
    out = _jax.eval_shape(lambda: _forward(_fwd_setup_inputs(0)))
    return out.shape, out.dtype

N_MICROBATCH = 1
ADAM_LR = 0.001
ADAM_B1 = 0.9
ADAM_B2 = 0.999
ADAM_EPS = 1e-08
ADAM_WD = 0.01
ADAM_STEP = 10
PER_EXAMPLE_BATCH_AXIS = {'x': 0, 'c': 0, 'loss_target': 0}
SHARED_INPUTS = []
_WEIGHT_DTYPES = {'w_ada': _jnp.float32, 'b_ada': _jnp.float32, 'w_in': _jnp.float32, 'b_in': _jnp.float32, 'conv_w': _jnp.float32, 'w_proj_attn': _jnp.float32, 'w_proj_conv': _jnp.float32, 'w_out': _jnp.float32, 'b_out': _jnp.float32, 'ln_g': _jnp.float32, 'ln_b': _jnp.float32}
MOMENT_SCALE = {'w_ada': 1.199819e-02, 'b_ada': 1.996569e-02, 'w_in': 5.587172e-03, 'b_in': 5.306013e-03, 'conv_w': 8.296634e-03, 'w_proj_attn': 2.578454e-03, 'w_proj_conv': 1.302460e-02, 'w_out': 1.326019e-02, 'b_out': 6.854134e-02, 'ln_g': 3.198671e+01, 'ln_b': 4.602507e-01}


def _to_microbatches(a, axis):
    t = _jnp.moveaxis(a, axis, 0)
    t = t.reshape((N_MICROBATCH, t.shape[0] // N_MICROBATCH) + t.shape[1:])
    return _jnp.moveaxis(t, 1, axis + 1)


def setup_inputs(seed: int = 0) -> dict:
    inp = _fwd_setup_inputs(seed)
    key = _jax.random.fold_in(_jax.random.key(seed), 7919)
    shape, _ = _output_shape()
    out = dict(inp)
    out["loss_target"] = _jax.random.normal(_jax.random.fold_in(key, 0), shape, _jnp.float32)
    for i, name in enumerate(TWIN_WEIGHTS):
        w = inp[name].astype(_jnp.float32)
        if MOMENT_SCALE is None:
            s = _jnp.sqrt(_jnp.mean(_jnp.square(w)) + 1e-30)
        else:
            s = MOMENT_SCALE[name]
        km, kv = _jax.random.split(_jax.random.fold_in(key, i + 1))
        out[name] = w
        out["m_" + name] = s * _jax.random.normal(km, w.shape, _jnp.float32)
        out["v_" + name] = (s * s) * _jax.random.uniform(kv, w.shape, _jnp.float32, 0.5, 1.5)
    if N_MICROBATCH > 1:
        for name, axis in PER_EXAMPLE_BATCH_AXIS.items():
            out[name] = _to_microbatches(out[name], axis)
    return {'x': out['x'], 'c': out['c'], 'w_ada': out['w_ada'], 'b_ada': out['b_ada'], 'w_in': out['w_in'], 'b_in': out['b_in'], 'conv_w': out['conv_w'], 'w_proj_attn': out['w_proj_attn'], 'w_proj_conv': out['w_proj_conv'], 'w_out': out['w_out'], 'b_out': out['b_out'], 'ln_g': out['ln_g'], 'ln_b': out['ln_b'], 'loss_target': out['loss_target'], 'm_w_ada': out['m_w_ada'], 'm_b_ada': out['m_b_ada'], 'm_w_in': out['m_w_in'], 'm_b_in': out['m_b_in'], 'm_conv_w': out['m_conv_w'], 'm_w_proj_attn': out['m_w_proj_attn'], 'm_w_proj_conv': out['m_w_proj_conv'], 'm_w_out': out['m_w_out'], 'm_b_out': out['m_b_out'], 'm_ln_g': out['m_ln_g'], 'm_ln_b': out['m_ln_b'], 'v_w_ada': out['v_w_ada'], 'v_b_ada': out['v_b_ada'], 'v_w_in': out['v_w_in'], 'v_b_in': out['v_b_in'], 'v_conv_w': out['v_conv_w'], 'v_w_proj_attn': out['v_w_proj_attn'], 'v_w_proj_conv': out['v_w_proj_conv'], 'v_w_out': out['v_w_out'], 'v_b_out': out['v_b_out'], 'v_ln_g': out['v_ln_g'], 'v_ln_b': out['v_ln_b']}


def _loss(weights, diff, rest, loss_target):
    with _jax.named_scope("forward"):
        args = {**rest, TWIN_DIFF_INPUT: diff, **{k: w.astype(_WEIGHT_DTYPES[k]) for k, w in weights.items()}}
        y = _forward(args)
    with _jax.named_scope("loss_head"):
        err = _jnp.square(y.astype(_jnp.float32) - loss_target)
        return 0.5 * _jnp.sum(_jnp.mean(err, axis=-1)) if err.ndim else 0.5 * err


def _adamw(w, g, m, v):
    m = ADAM_B1 * m + (1.0 - ADAM_B1) * g
    v = ADAM_B2 * v + (1.0 - ADAM_B2) * _jnp.square(g)
    m_hat = m / (1.0 - ADAM_B1 ** ADAM_STEP)
    v_hat = v / (1.0 - ADAM_B2 ** ADAM_STEP)
    delta = -ADAM_LR * (m_hat / (_jnp.sqrt(v_hat) + ADAM_EPS) + ADAM_WD * w)
    return delta, m, v


def reference(x, c, w_ada, b_ada, w_in, b_in, conv_w, w_proj_attn, w_proj_conv, w_out, b_out, ln_g, ln_b, loss_target, m_w_ada, m_b_ada, m_w_in, m_b_in, m_conv_w, m_w_proj_attn, m_w_proj_conv, m_w_out, m_b_out, m_ln_g, m_ln_b, v_w_ada, v_b_ada, v_w_in, v_b_in, v_conv_w, v_w_proj_attn, v_w_proj_conv, v_w_out, v_b_out, v_ln_g, v_ln_b):
    given = dict(x=x, c=c, w_ada=w_ada, b_ada=b_ada, w_in=w_in, b_in=b_in, conv_w=conv_w, w_proj_attn=w_proj_attn, w_proj_conv=w_proj_conv, w_out=w_out, b_out=b_out, ln_g=ln_g, ln_b=ln_b, loss_target=loss_target, m_w_ada=m_w_ada, m_b_ada=m_b_ada, m_w_in=m_w_in, m_b_in=m_b_in, m_conv_w=m_conv_w, m_w_proj_attn=m_w_proj_attn, m_w_proj_conv=m_w_proj_conv, m_w_out=m_w_out, m_b_out=m_b_out, m_ln_g=m_ln_g, m_ln_b=m_ln_b, v_w_ada=v_w_ada, v_b_ada=v_b_ada, v_w_in=v_w_in, v_b_in=v_b_in, v_conv_w=v_conv_w, v_w_proj_attn=v_w_proj_attn, v_w_proj_conv=v_w_proj_conv, v_w_out=v_w_out, v_b_out=v_b_out, v_ln_g=v_ln_g, v_ln_b=v_ln_b)
    weights = {n: given[n] for n in TWIN_WEIGHTS}
    shared = {n: given[n] for n in SHARED_INPUTS}
    per_example = {n: given[n] for n in ['x', 'c']}
    grad_fn = _jax.value_and_grad(_loss, argnums=(0, 1))

    def one_microbatch(ex, loss_target):
        ex = dict(ex)
        diff = ex.pop(TWIN_DIFF_INPUT)
        return grad_fn(weights, diff, {**shared, **ex}, loss_target)

    if N_MICROBATCH == 1:
        loss, (grad_w, grad_x) = one_microbatch(per_example, given["loss_target"])
    else:
        def body(carry, xs):
            loss_sum, grad_sum = carry
            l_k, (gw_k, gx_k) = one_microbatch(xs[0], xs[1])
            with _jax.named_scope("update"):
                return (loss_sum + l_k, _jax.tree.map(_jnp.add, grad_sum, gw_k)), gx_k

        init = (_jnp.zeros((), _jnp.float32), _jax.tree.map(_jnp.zeros_like, weights))
        (loss, grad_w), grad_x = _jax.lax.scan(body, init, (per_example, given["loss_target"]))
    with _jax.named_scope("update"):
        delta_w, new_m, new_v = {}, {}, {}
        for n in TWIN_WEIGHTS:
            delta_w[n], new_m[n], new_v[n] = _adamw(weights[n], grad_w[n], given["m_" + n], given["v_" + n])
    return (loss, grad_x, *[grad_w[n] for n in TWIN_WEIGHTS], *[delta_w[n] for n in TWIN_WEIGHTS],
            *[new_m[n] for n in TWIN_WEIGHTS], *[new_v[n] for n in TWIN_WEIGHTS])
```

```python
import functools

import jax
import jax.numpy as jnp
from jax import lax
from jax.experimental import pallas as pl
from jax.experimental.pallas import tpu as pltpu

F32, BF16 = jnp.float32, jnp.bfloat16
MESH = pl.DeviceIdType.MESH
N_DEV = 8
D = 1024
SLAB = 256
N_QKV, N_REST = 9, 25
N_SLAB = N_QKV + N_REST
D_IN = N_SLAB * SLAB
DP_SLABS = 36
BLK = 128
GROUPS = ((128, 1), (512, 4), (2048, 16))
ALPHA = 2.0 ** 0.25
LN_EPS = 1e-5
LR, B1, B2, EPS, WD, STEP = 0.001, 0.9, 0.999, 1e-08, 0.01, 10
R_ZA, R_UX, R_GB, R_GC, R_ZC, R_GA, R_GBM = 0, 1, 5, 9, 13, 17, 21
P_BIN, P_BOUT, P_LNG, P_LNB, P_CONV, P_LOSS, P_DADA = 0, 8704, 9728, 10752, 11776, 14848, 14976
MIB = 1024 * 1024

_pcall = pl.pallas_call
ANY = pl.BlockSpec(memory_space=pl.ANY)
VMEM = pl.BlockSpec(memory_space=pltpu.VMEM)


def _params(vmem_mib=None, sem=None):
    kw = {}
    if vmem_mib is not None:
        kw["vmem_limit_bytes"] = vmem_mib * MIB
    if sem is not None:
        kw["dimension_semantics"] = sem
    return pltpu.CompilerParams(**kw)


def _nn(a, b):
    return jnp.dot(a, b, preferred_element_type=F32)


def _nt(a, b):
    return lax.dot_general(a, b, (((1,), (1,)), ((), ())), preferred_element_type=F32)


def _tn(a, b):
    return lax.dot_general(a, b, (((0,), (0,)), ((), ())), preferred_element_type=F32)


def _sigmoid(v):
    return 1.0 / (1.0 + jnp.exp(-v))


def _part8(v):
    return v.reshape(v.shape[0] // 8, 8, v.shape[1]).sum(axis=0)


def _my_position():
    return lax.axis_index("x"), lax.axis_index("y"), lax.axis_index("c")


def _flat(px, py, pc):
    return 4 * px + 2 * py + pc


def _peer(mask):
    x, y, c = _my_position()
    return (x ^ ((mask >> 2) & 1), y ^ ((mask >> 1) & 1), c ^ (mask & 1))


def _prep(w_in, w_pa, w_pb, w_out, c, conv_w):
    n_in = w_in.shape[1]
    full, tail = n_in // 128, n_in % 128

    def body(win_ref, wpa_ref, wpb_ref, wout_ref, c_ref, cw_ref, wint_ref, wpat_ref, wpb_o, wout_o, cact_ref, cwp_ref):
        for a in range(full):
            wint_ref[pl.ds(128 * a, 128), :] = win_ref[:, pl.ds(128 * a, 128)].T.astype(BF16)
        if tail:
            t = win_ref[:, pl.ds(n_in - 128, 128)].T
            wint_ref[pl.ds(128 * full, tail), :] = t[128 - tail:].astype(BF16)
        wpat_ref[...] = wpa_ref[...].T.astype(BF16)
        wpb_o[...] = wpb_ref[...].astype(BF16)
        wout_o[...] = wout_ref[...].astype(BF16)
        cv = c_ref[...]
        cact_ref[...] = jnp.zeros_like(cact_ref)
        cact_ref[pl.ds(0, cv.shape[0]), :] = cv * _sigmoid(cv)
        cwp_ref[...] = jnp.zeros_like(cwp_ref)
        cwp_ref[pl.ds(0, 3), :] = cw_ref[...]

    out_shape = (jax.ShapeDtypeStruct((n_in, D), BF16), jax.ShapeDtypeStruct((w_pa.shape[1], w_pa.shape[0]), BF16),
                 jax.ShapeDtypeStruct(w_pb.shape, BF16), jax.ShapeDtypeStruct(w_out.shape, BF16),
                 jax.ShapeDtypeStruct((8, D), F32), jax.ShapeDtypeStruct((8, conv_w.shape[1]), F32))
    return _pcall(body, out_shape=out_shape, in_specs=[VMEM] * 6, out_specs=(VMEM,) * 6, name="prep",
                  compiler_params=_params(32))(w_in, w_pa, w_pb, w_out, c, conv_w)


def _gather_rows(shards):
    n = len(shards)

    def body(*refs):
        srcs, outs = refs[:n], refs[n:2 * n]
        send_sems, recv_sems, local_sems = refs[2 * n:]
        x, y, c = _my_position()
        me, sibling = (x, y, c), (x, y, 1 - c)
        chips = [(1 - x, y), (x, 1 - y), (1 - x, 1 - y)]

        def rows(a, px, py, pc):
            r = shards[a].shape[0]
            return outs[a].at[pl.ds(pl.multiple_of(_flat(px, py, pc) * r, r), r), :]

        def copy(a, k, block, to, src=None):
            return pltpu.make_async_remote_copy(
                src_ref=rows(a, *block) if src is None else src, dst_ref=rows(a, *block),
                send_sem=send_sems.at[7 * a + k], recv_sem=recv_sems.at[7 * a + k], device_id=to, device_id_type=MESH)

        mine = [pltpu.make_async_copy(srcs[a], rows(a, *me), local_sems.at[a]) for a in range(n)]
        for cp in mine:
            cp.start()
        first = []
        for a in range(n):
            first.append(copy(a, 0, me, sibling, src=srcs[a]))
            first += [copy(a, 1 + j, me, (*chip, c), src=srcs[a]) for j, chip in enumerate(chips)]
        for cp in first:
            cp.start()
        passed = []
        for j, chip in enumerate(chips):
            for a in range(n):
                copy(a, 1 + j, (*chip, c), me).wait_recv()
                cp = copy(a, 4 + j, (*chip, c), sibling)
                cp.start()
                passed.append(cp)
        for a in range(n):
            copy(a, 0, sibling, me).wait_recv()
        for j, chip in enumerate(chips):
            for a in range(n):
                copy(a, 4 + j, (*chip, 1 - c), me).wait_recv()
        for cp in first + passed:
            cp.wait_send()
        for cp in mine:
            cp.wait()

    out_shape = tuple(jax.ShapeDtypeStruct((N_DEV * s.shape[0], s.shape[1]), s.dtype) for s in shards)
    return _pcall(body, out_shape=out_shape, in_specs=[ANY] * n, out_specs=(ANY,) * n, name="gather_rows",
                  scratch_shapes=[pltpu.SemaphoreType.DMA((7 * n,)), pltpu.SemaphoreType.DMA((7 * n,)),
                                  pltpu.SemaphoreType.DMA((n,))])(*shards)


def _exchange_slots(out_ref, send_sems, recv_sems):
    me = _flat(*_my_position())
    copies = []
    for mask in range(1, N_DEV):
        peer = _peer(mask)
        copies.append((mask, pltpu.make_async_remote_copy(
            src_ref=out_ref.at[me], dst_ref=out_ref.at[me], send_sem=send_sems.at[mask - 1],
            recv_sem=recv_sems.at[mask - 1], device_id=peer, device_id_type=MESH)))
    for _, cp in copies:
        cp.start()
    for mask, _ in copies:
        peer = _peer(mask)
        pltpu.make_async_remote_copy(
            src_ref=out_ref.at[_flat(*peer)], dst_ref=out_ref.at[_flat(*peer)], send_sem=send_sems.at[mask - 1],
            recv_sem=recv_sems.at[mask - 1], device_id=peer, device_id_type=MESH).wait_recv()
    for _, cp in copies:
        cp.wait_send()


def _ada_forward(cact_all, w_ada, b_ada_mine):
    nb, ncol = cact_all.shape[0], w_ada.shape[1]

    def body(c_ref, w_ref, b_ref, out_ref, send_sems, recv_sems):
        me = _flat(*_my_position())
        out_ref[me] = _nn(c_ref[...].astype(BF16), w_ref[...].astype(BF16)) + b_ref[...]
        _exchange_slots(out_ref, send_sems, recv_sems)

    return _pcall(body, out_shape=jax.ShapeDtypeStruct((N_DEV, nb, ncol), F32), in_specs=[VMEM] * 3, out_specs=VMEM,
                  scratch_shapes=[pltpu.SemaphoreType.DMA((7,)), pltpu.SemaphoreType.DMA((7,))], name="ada_forward",
                  compiler_params=_params(16))(cact_all, w_ada, b_ada_mine)


def _small_reduce(gb_rest, gb_qkv, svec, dgate, dss):
    nbat = dgate.shape[0]

    def body(gbr_ref, q0_ref, q1_ref, q2_ref, sv_ref, dg_ref, dss_ref, rows_ref, tot_ref, gbada_ref, send_sems, recv_sems):
        me = _flat(*_my_position())

        def put(off, v):
            rows_ref[me, :, pl.ds(off, v.shape[1])] = v

        def row(v):
            return jnp.sum(v, axis=0, keepdims=True)

        for g, q_ref in enumerate((q0_ref, q1_ref, q2_ref)):
            for which in range(3):
                put(P_BIN + SLAB * (3 * which + g), row(q_ref[which]))
        for s in range(N_REST):
            put(P_BIN + SLAB * (N_QKV + s), row(gbr_ref[s]))
        put(P_LNG, row(sv_ref[0]))
        put(P_LNB, row(sv_ref[1]))
        put(P_BOUT, row(sv_ref[2]))
        for j in range(3):
            put(P_CONV + D * j, row(sv_ref[3 + j]))
        loss = (0.5 / D) * jnp.sum(row(sv_ref[6]), axis=1, keepdims=True)
        put(P_LOSS, jnp.broadcast_to(loss, (1, 128)))
        for b in range(nbat):
            put(P_DADA + 3 * D * b, row(dss_ref[b, 0]))
            put(P_DADA + 3 * D * b + D, row(dss_ref[b, 1]))
            put(P_DADA + 3 * D * b + 2 * D, row(dg_ref[b]))
        _exchange_slots(rows_ref, send_sems, recv_sems)
        tot = rows_ref[0]
        for k in range(1, N_DEV):
            tot = tot + rows_ref[k]
        tot_ref[...] = tot
        gbada = tot[:, P_DADA:P_DADA + 3 * D]
        for b in range(1, nbat):
            gbada = gbada + tot[:, P_DADA + 3 * D * b:P_DADA + 3 * D * (b + 1)]
        gbada_ref[...] = gbada

    p_len = P_DADA + nbat * 3 * D
    out_shape = (jax.ShapeDtypeStruct((N_DEV, 1, p_len), F32), jax.ShapeDtypeStruct((1, p_len), F32),
                 jax.ShapeDtypeStruct((1, 3 * D), F32))
    return _pcall(body, out_shape=out_shape, in_specs=[VMEM] * 7, out_specs=(VMEM, VMEM, VMEM),
                  scratch_shapes=[pltpu.SemaphoreType.DMA((7,)), pltpu.SemaphoreType.DMA((7,))], name="small_reduce",
                  compiler_params=_params(16))(gb_rest, *gb_qkv, svec, dgate, dss)


def _make_h(x, ada, tm=512):
    t = x.shape[0]
    tps = (t // ada.shape[0]) // tm

    def body(x_ref, ada_ref, h_ref):
        h_ref[...] = (x_ref[...] * (1.0 + ada_ref[0, 1:2, :]) + ada_ref[0, 0:1, :]).astype(BF16)

    return _pcall(body, grid=(t // tm,), out_shape=jax.ShapeDtypeStruct((t, D), BF16),
                  in_specs=[pl.BlockSpec((tm, D), lambda i: (i, 0)), pl.BlockSpec((1, 3, D), lambda i: (i // tps, 0, 0))],
                  out_specs=pl.BlockSpec((tm, D), lambda i: (i, 0)), name="make_h",
                  compiler_params=_params(32, ("parallel",)))(x, ada)


def _project(h, w_in_t, b_in3, chunk=512):
    t = h.shape[0]

    def body(h_ref, w_ref, b_ref, qkv_ref, rest_ref):
        j = pl.program_id(0)

        def rows(r, carry):
            sl = pl.ds(pl.multiple_of(r * chunk, chunk), chunk)
            v = _nt(h_ref[sl, :], w_ref[...]) + b_ref[0]

            @pl.when(j < N_QKV)
            def _():
                qkv_ref[0, sl, :] = v.astype(BF16)

            @pl.when(j >= N_QKV)
            def _():
                rest_ref[0, sl, :] = v
            return carry

        lax.fori_loop(0, t // chunk, rows, 0)

    return _pcall(
        body, grid=(N_SLAB,),
        out_shape=(jax.ShapeDtypeStruct((N_QKV, t, SLAB), BF16), jax.ShapeDtypeStruct((N_REST, t, SLAB), F32)),
        in_specs=[pl.BlockSpec((t, D), lambda j: (0, 0)), pl.BlockSpec((SLAB, D), lambda j: (j, 0)),
                  pl.BlockSpec((1, 1, SLAB), lambda j: (j, 0, 0))],
        out_specs=(pl.BlockSpec((1, t, SLAB), lambda j: (jnp.minimum(j, N_QKV - 1), 0, 0)),
                   pl.BlockSpec((1, t, SLAB), lambda j: (jnp.maximum(j - N_QKV, 0), 0, 0))),
        name="project", compiler_params=_params(48, ("arbitrary",)))(h, w_in_t, b_in3)


def _bias_tables(g):
    window, dil = GROUPS[g]
    span = window // dil
    qi = jnp.arange(BLK)[:, None]
    kj = jnp.arange(2 * BLK)[None, :]
    delta = qi + BLK - kj
    valid = (delta >= 0) & (delta <= span)
    heads = jnp.arange(4, dtype=F32) + 4.0 * g
    slopes = 2.0 ** (-8.0 * (heads + 1.0) / 12.0)
    bias = -slopes[:, None, None] * (delta * dil).astype(F32)[None]
    return jnp.where(valid[None], bias, -1e30)


def _head_masks(shape):
    lane = lax.broadcasted_iota(jnp.int32, shape, 1)
    return [(lane >= 64 * h) & (lane < 64 * (h + 1)) for h in range(4)]


def _attn_forward(qkv, g, nbat):
    t = qkv.shape[1]
    dil = GROUPS[g][1]
    n = t // nbat // dil
    nblk = n // BLK
    qkv5 = qkv.reshape(3, 3, nbat, n, dil * SLAB)

    def body(qkv_ref, bias_ref, ol_ref):
        masks = _head_masks((BLK, SLAB))

        def block(i, first):
            qs = pl.ds(pl.multiple_of(i * BLK, BLK), BLK)
            if first:
                ks, nk = pl.ds(0, BLK), BLK
            else:
                ks, nk = pl.ds(pl.multiple_of((i - 1) * BLK, BLK), 2 * BLK), 2 * BLK
            q = qkv_ref[0, 0, 0, qs, :]
            kk = qkv_ref[1, 0, 0, ks, :]
            vv = qkv_ref[2, 0, 0, ks, :]
            out = jnp.zeros((BLK, SLAB), F32)
            lse = jnp.zeros((BLK, SLAB), F32)
            for h in range(4):
                qm = jnp.where(masks[h], q, jnp.zeros_like(q))
                bias = bias_ref[h, :, pl.ds(BLK, BLK)] if first else bias_ref[h]
                s = _nt(qm, kk) * 0.125 + bias
                m = jnp.max(s, axis=1, keepdims=True)
                p = jnp.exp(s - m)
                den = jnp.sum(p, axis=1, keepdims=True)
                o = _nn((p * (1.0 / den)).astype(BF16), vv)
                out = jnp.where(masks[h], o, out)
                lse = jnp.where(masks[h], m + jnp.log(den), lse)
            ol_ref[0, 0, qs, :] = out
            ol_ref[1, 0, qs, :] = lse

        block(0, True)
        if nblk > 1:
            def loop(i, carry):
                block(i, False)
                return carry
            lax.fori_loop(1, nblk, loop, 0)

    out = _pcall(
        body, grid=(nbat, dil), out_shape=jax.ShapeDtypeStruct((2, nbat, n, dil * SLAB), F32),
        in_specs=[pl.BlockSpec((3, 1, 1, n, SLAB), lambda b, r: (0, g, b, 0, r)),
                  pl.BlockSpec((4, BLK, 2 * BLK), lambda b, r: (0, 0, 0))],
        out_specs=pl.BlockSpec((2, 1, n, SLAB), lambda b, r: (0, b, 0, r)),
        name=f"attn_forward_{g}", compiler_params=_params(40, ("parallel", "parallel")))(qkv5, _bias_tables(g))
    return out.reshape(2, t, SLAB)


def _attn_backward(qkv, do_attn, ol_tot, dproj, g, nbat):
    t = qkv.shape[1]
    dil = GROUPS[g][1]
    n = t // nbat // dil
    nblk = n // BLK
    qkv5 = qkv.reshape(3, 3, nbat, n, dil * SLAB)
    do3 = do_attn.reshape(nbat, n, dil * SLAB)
    ol4 = ol_tot.reshape(2, nbat, n, dil * SLAB)
    dp5 = dproj.reshape(DP_SLABS // 3, 3, nbat, n, dil * SLAB)

    def body(qkv_ref, do_ref, ol_ref, bias_ref, dp_in, dp_ref, gb_ref, dk_acc, dv_acc):
        del dp_in
        masks = _head_masks((BLK, SLAB))
        step = pl.program_id(0) * dil + pl.program_id(1)

        @pl.when(step == 0)
        def _():
            gb_ref[...] = jnp.zeros_like(gb_ref)

        dk_acc[...] = jnp.zeros_like(dk_acc)
        dv_acc[...] = jnp.zeros_like(dv_acc)

        def block(i, first):
            qs = pl.ds(pl.multiple_of(i * BLK, BLK), BLK)
            if first:
                ks = pl.ds(0, BLK)
            else:
                ks = pl.ds(pl.multiple_of((i - 1) * BLK, BLK), 2 * BLK)
            q = qkv_ref[0, 0, 0, qs, :]
            kk = qkv_ref[1, 0, 0, ks, :]
            vv = qkv_ref[2, 0, 0, ks, :]
            do = do_ref[0, qs, :]
            prod = do.astype(F32) * ol_ref[0, 0, qs, :]
            lse = ol_ref[1, 0, qs, :]
            dq = jnp.zeros((BLK, SLAB), F32)
            for h in range(4):
                qm = jnp.where(masks[h], q, jnp.zeros_like(q))
                dom = jnp.where(masks[h], do, jnp.zeros_like(do))
                bias = bias_ref[h, :, pl.ds(BLK, BLK)] if first else bias_ref[h]
                s = _nt(qm, kk) * 0.125 + bias
                p = jnp.exp(s - lse[:, 64 * h:64 * h + 1])
                delta = jnp.sum(jnp.where(masks[h], prod, 0.0), axis=1, keepdims=True)
                ds = (p * (_nt(dom, vv) - delta)).astype(BF16)
                dv_acc[ks, :] += _tn(p.astype(BF16), dom)
                dk_acc[ks, :] += _tn(ds, qm) * 0.125
                dq = dq + jnp.where(masks[h], _nn(ds, kk), 0.0) * 0.125
            dp_ref[0, 0, 0, qs, :] = dq.astype(BF16)
            gb_ref[0] += _part8(dq)

        block(0, True)
        if nblk > 1:
            def loop(i, carry):
                block(i, False)
                return carry
            lax.fori_loop(1, nblk, loop, 0)
        dk = dk_acc[...]
        dv = dv_acc[...]
        dp_ref[1, 0, 0] = dk.astype(BF16)
        dp_ref[2, 0, 0] = dv.astype(BF16)
        gb_ref[1] += _part8(dk)
        gb_ref[2] += _part8(dv)

    dp, gb = _pcall(
        body, grid=(nbat, dil),
        out_shape=(jax.ShapeDtypeStruct(dp5.shape, BF16), jax.ShapeDtypeStruct((3, 8, SLAB), F32)),
        in_specs=[pl.BlockSpec((3, 1, 1, n, SLAB), lambda b, r: (0, g, b, 0, r)),
                  pl.BlockSpec((1, n, SLAB), lambda b, r: (b, 0, r)),
                  pl.BlockSpec((2, 1, n, SLAB), lambda b, r: (0, b, 0, r)),
                  pl.BlockSpec((4, BLK, 2 * BLK), lambda b, r: (0, 0, 0)), ANY],
        out_specs=(pl.BlockSpec((3, 1, 1, n, SLAB), lambda b, r: (DP_SLABS // 9 - 1, g, b, 0, r)),
                   pl.BlockSpec((3, 8, SLAB), lambda b, r: (0, 0, 0))),
        scratch_shapes=[pltpu.VMEM((n, SLAB), F32), pltpu.VMEM((n, SLAB), F32)],
        input_output_aliases={4: 0}, name=f"attn_backward_{g}",
        compiler_params=_params(40, ("arbitrary", "arbitrary")))(qkv5, do3, ol4, _bias_tables(g), dp5)
    return dp.reshape(DP_SLABS, t, SLAB), gb


def _mid(rest, ols, x, tgt, ada, cw, b_out, ln_g, ln_b, w_pa_t, w_pb, w_out, tm=256):
    t = x.shape[0]
    nbat = ada.shape[0]
    nt = t // tm
    tps = nt // nbat

    def body(rest_ref, halo_ref, ol0_ref, ol1_ref, ol2_ref, x_ref, t_ref, ada_ref, cw_ref, bout_ref, lng_ref, lnb_ref,
             wpat_ref, wpb_ref, wout_ref,
             dp_ref, gx0_ref, doa_ref, olt_ref, mg_ref, dof_ref, bbs_ref, dyc_ref, a_ref, dya_ref,
             gbr_ref, sv_ref, dgate_ref, carry_ref):
        i = pl.program_id(0)
        ti = nt - 1 - i
        pos = ti % tps

        @pl.when(i == 0)
        def _():
            gbr_ref[...] = jnp.zeros_like(gbr_ref)
            sv_ref[...] = jnp.zeros_like(sv_ref)

        @pl.when(pos == tps - 1)
        def _():
            dgate_ref[...] = jnp.zeros_like(dgate_ref)
            carry_ref[...] = jnp.zeros_like(carry_ref)

        row = lax.broadcasted_iota(jnp.int32, (tm, SLAB), 0)
        halo_on = (pos > 0).astype(F32)

        def cols(s):
            return pl.ds(SLAB * s, SLAB)

        l0, l1, l2 = ol0_ref[1], ol1_ref[1], ol2_ref[1]
        mx = jnp.maximum(jnp.maximum(l0, l1), l2)
        e0, e1, e2 = jnp.exp(l0 - mx), jnp.exp(l1 - mx), jnp.exp(l2 - mx)
        den = e0 + e1 + e2
        o_attn = (e0 * ol0_ref[0] + e1 * ol1_ref[0] + e2 * ol2_ref[0]) * (1.0 / den)
        olt_ref[0] = o_attn
        olt_ref[1] = mx + jnp.log(den)
        z_a = rest_ref[R_ZA]
        sg_za = _sigmoid(z_a)
        a_ref[...] = (o_attn * z_a * sg_za).astype(BF16)
        y_attn = _nt(a_ref[...], wpat_ref[...])

        def conv_parts(s):
            ux, gc = rest_ref[R_UX + s], rest_ref[R_GC + s]
            u = gc * ux
            hu = halo_ref[R_GC + s] * halo_ref[R_UX + s] * halo_on
            u1 = jnp.where(row == 0, hu[7:8], pltpu.roll(u, 1, 0))
            u2 = jnp.where(row == 0, hu[6:7], jnp.where(row == 1, hu[7:8], pltpu.roll(u, 2, 0)))
            conv = cw_ref[0:1, cols(s)] * u2 + cw_ref[1:2, cols(s)] * u1 + cw_ref[2:3, cols(s)] * u
            zc = rest_ref[R_ZC + s]
            sg = _sigmoid(zc)
            return ux, gc, u, u1, u2, conv, zc, sg

        for s in range(4):
            ux, gc, u, u1, u2, conv, zc, sg = conv_parts(s)
            bbs_ref[:, cols(s)] = (rest_ref[R_GB + s] * conv * (zc * sg)).astype(BF16)
        y_conv = _nn(bbs_ref[...], wpb_ref[...])

        for s in range(4):
            s_a, s_b = _sigmoid(rest_ref[R_GA + s]), _sigmoid(rest_ref[R_GBM + s])
            mg_ref[:, cols(s)] = (s_a * y_attn[:, SLAB * s:SLAB * (s + 1)] + s_b * y_conv[:, SLAB * s:SLAB * (s + 1)]).astype(BF16)
        o = _nn(mg_ref[...], wout_ref[...]) + bout_ref[...]
        gate = ada_ref[0, 2:3, :]
        r = ALPHA * x_ref[...] + gate * o
        mu = jnp.mean(r, axis=1, keepdims=True)
        rc = r - mu
        rstd = lax.rsqrt(jnp.mean(rc * rc, axis=1, keepdims=True) + LN_EPS)
        xhat = rc * rstd
        err = xhat * lng_ref[...] + lnb_ref[...] - t_ref[...]
        sv_ref[6] += _part8(err * err)
        dy = err * (1.0 / D)
        sv_ref[0] += _part8(dy * xhat)
        sv_ref[1] += _part8(dy)
        dxh = dy * lng_ref[...]
        dr = rstd * (dxh - jnp.mean(dxh, axis=1, keepdims=True) - xhat * jnp.mean(dxh * xhat, axis=1, keepdims=True))
        gx0_ref[...] = ALPHA * dr
        dgate_ref[0] += _part8(dr * o)
        do_ = dr * gate
        sv_ref[2] += _part8(do_)
        dof_ref[...] = do_.astype(BF16)
        dmerged = _nt(dof_ref[...], wout_ref[...])
        for s in range(4):
            s_a, s_b = _sigmoid(rest_ref[R_GA + s]), _sigmoid(rest_ref[R_GBM + s])
            dm = dmerged[:, SLAB * s:SLAB * (s + 1)]
            ya, yc = y_attn[:, SLAB * s:SLAB * (s + 1)], y_conv[:, SLAB * s:SLAB * (s + 1)]
            dya_ref[:, cols(s)] = (dm * s_a).astype(BF16)
            dyc_ref[:, cols(s)] = (dm * s_b).astype(BF16)
            dga = dm * ya * s_a * (1.0 - s_a)
            dgb = dm * yc * s_b * (1.0 - s_b)
            dp_ref[R_GA + s] = dga.astype(BF16)
            dp_ref[R_GBM + s] = dgb.astype(BF16)
            gbr_ref[R_GA + s] += _part8(dga)
            gbr_ref[R_GBM + s] += _part8(dgb)

        da = _nn(dya_ref[...], wpat_ref[...])
        doa_ref[...] = (da * z_a * sg_za).astype(BF16)
        dza = da * o_attn * (sg_za * (1.0 + z_a * (1.0 - sg_za)))
        dp_ref[R_ZA] = dza.astype(BF16)
        gbr_ref[R_ZA] += _part8(dza)

        dbb = _nt(dyc_ref[...], wpb_ref[...])
        for s in range(4):
            ux, gc, u, u1, u2, conv, zc, sg = conv_parts(s)
            gb = rest_ref[R_GB + s]
            d_b = dbb[:, SLAB * s:SLAB * (s + 1)]
            szc = zc * sg
            dgb_ = d_b * conv * szc
            dconv = d_b * gb * szc
            dzc = d_b * gb * conv * (sg * (1.0 + zc * (1.0 - sg)))
            sv_ref[3, :, cols(s)] += _part8(dconv * u2)
            sv_ref[4, :, cols(s)] += _part8(dconv * u1)
            sv_ref[5, :, cols(s)] += _part8(dconv * u)
            nxt = carry_ref[:, cols(s)]
            d1 = jnp.where(row == tm - 1, nxt[0:1], pltpu.roll(dconv, tm - 1, 0))
            d2 = jnp.where(row == tm - 1, nxt[1:2], jnp.where(row == tm - 2, nxt[0:1], pltpu.roll(dconv, tm - 2, 0)))
            carry_ref[:, cols(s)] = dconv[0:8]
            du = cw_ref[2:3, cols(s)] * dconv + cw_ref[1:2, cols(s)] * d1 + cw_ref[0:1, cols(s)] * d2
            dgc, dux = du * ux, du * gc
            for slab, val in ((R_GB + s, dgb_), (R_ZC + s, dzc), (R_GC + s, dgc), (R_UX + s, dux)):
                dp_ref[slab] = val.astype(BF16)
                gbr_ref[slab] += _part8(val)

    def tile(i):
        return nt - 1 - i

    row_blk = lambda i: (tile(i), 0)
    slab_blk = lambda i: (0, tile(i), 0)
    const2 = lambda i: (0, 0)
    const3 = lambda i: (0, 0, 0)
    in_specs = [
        pl.BlockSpec((N_REST, tm, SLAB), slab_blk),
        pl.BlockSpec((N_REST, 8, SLAB), lambda i: (0, jnp.maximum(tile(i) * (tm // 8) - 1, 0), 0)),
        pl.BlockSpec((2, tm, SLAB), slab_blk), pl.BlockSpec((2, tm, SLAB), slab_blk), pl.BlockSpec((2, tm, SLAB), slab_blk),
        pl.BlockSpec((tm, D), row_blk), pl.BlockSpec((tm, D), row_blk),
        pl.BlockSpec((1, 3, D), lambda i: (tile(i) // tps, 0, 0)),
        pl.BlockSpec((3, D), const2), pl.BlockSpec((1, D), const2), pl.BlockSpec((1, D), const2), pl.BlockSpec((1, D), const2),
        pl.BlockSpec((D, SLAB), const2), pl.BlockSpec((D, D), const2), pl.BlockSpec((D, D), const2)]
    bf_rows = lambda: jax.ShapeDtypeStruct((t, D), BF16)
    out_shape = (
        jax.ShapeDtypeStruct((DP_SLABS, t, SLAB), BF16), jax.ShapeDtypeStruct((t, D), F32),
        jax.ShapeDtypeStruct((t, SLAB), BF16), jax.ShapeDtypeStruct((2, t, SLAB), F32),
        bf_rows(), bf_rows(), bf_rows(), bf_rows(), jax.ShapeDtypeStruct((t, SLAB), BF16), bf_rows(),
        jax.ShapeDtypeStruct((N_REST, 8, SLAB), F32), jax.ShapeDtypeStruct((7, 8, D), F32),
        jax.ShapeDtypeStruct((nbat, 8, D), F32))
    out_specs = (
        pl.BlockSpec((N_REST, tm, SLAB), slab_blk), pl.BlockSpec((tm, D), row_blk),
        pl.BlockSpec((tm, SLAB), row_blk), pl.BlockSpec((2, tm, SLAB), slab_blk),
        pl.BlockSpec((tm, D), row_blk), pl.BlockSpec((tm, D), row_blk), pl.BlockSpec((tm, D), row_blk),
        pl.BlockSpec((tm, D), row_blk), pl.BlockSpec((tm, SLAB), row_blk), pl.BlockSpec((tm, D), row_blk),
        pl.BlockSpec((N_REST, 8, SLAB), const3), pl.BlockSpec((7, 8, D), const3),
        pl.BlockSpec((1, 8, D), lambda i: (tile(i) // tps, 0, 0)))
    return _pcall(body, grid=(nt,), out_shape=out_shape, in_specs=in_specs, out_specs=out_specs,
                  scratch_shapes=[pltpu.VMEM((8, D), F32)], name="mid",
                  compiler_params=_params(56, ("arbitrary",)))(
        rest, rest, *ols, x, tgt, ada, cw, b_out, ln_g, ln_b, w_pa_t, w_pb, w_out)


def _tn_matmul(lhs, rhs, lhs_spec, n_steps, out_rows, out_index, name, chunk=512):
    t, n = rhs.shape

    def body(l_ref, r_ref, o_ref):
        def rows(r, acc):
            sl = pl.ds(pl.multiple_of(r * chunk, chunk), chunk)
            lv = l_ref[0, sl, :] if len(l_ref.shape) == 3 else l_ref[sl, :]
            return acc + _tn(lv, r_ref[sl, :])
        o_ref[...] = lax.fori_loop(0, t // chunk, rows, jnp.zeros(o_ref.shape, F32))

    return _pcall(body, grid=(n_steps,), out_shape=jax.ShapeDtypeStruct((out_rows, n), F32),
                  in_specs=[lhs_spec, pl.BlockSpec((t, n), lambda j: (0, 0))],
                  out_specs=pl.BlockSpec((SLAB, n), out_index), name=name,
                  compiler_params=_params(48, ("parallel",)))(lhs, rhs)


def _grad_rows_2d(lhs, rhs, name):
    t, k = lhs.shape
    return _tn_matmul(lhs, rhs, pl.BlockSpec((t, SLAB), lambda j: (0, j)), k // SLAB, k, lambda j: (j, 0), name)


def _w_row_block(j):
    return (j + N_QKV) % N_SLAB


def _dp_slab(j):
    return jnp.where(j < N_REST, j, j + 2)


def _grad_w_in_t(dproj, h):
    t = h.shape[0]
    return _tn_matmul(dproj, h, pl.BlockSpec((1, t, SLAB), lambda j: (_dp_slab(j), 0, 0)), N_SLAB, D_IN,
                      lambda j: (_w_row_block(j), 0), "grad_w_in")


def _grad_h(dproj, w_in_t, gx0, x, ada, tm=1024):
    t = x.shape[0]
    nbat = ada.shape[0]
    tps = (t // nbat) // tm

    def body(dp_ref, w_ref, gx0_ref, x_ref, ada_ref, gx_ref, dss_ref, acc_ref):
        i, j = pl.program_id(0), pl.program_id(1)

        @pl.when(j == 0)
        def _():
            acc_ref[...] = jnp.zeros_like(acc_ref)

        acc_ref[...] += _nn(dp_ref[0], w_ref[...])

        @pl.when(j == N_SLAB - 1)
        def _():
            dh = acc_ref[...]
            gx_ref[...] = gx0_ref[...] + dh * (1.0 + ada_ref[0, 1:2, :])
            @pl.when((i % tps) == 0)
            def _():
                dss_ref[...] = jnp.zeros_like(dss_ref)

            dss_ref[0, 0] += _part8(dh)
            dss_ref[0, 1] += _part8(dh * x_ref[...])

    return _pcall(
        body, grid=(t // tm, N_SLAB),
        out_shape=(jax.ShapeDtypeStruct((t, D), F32), jax.ShapeDtypeStruct((nbat, 2, 8, D), F32)),
        in_specs=[pl.BlockSpec((1, tm, SLAB), lambda i, j: (_dp_slab(j), i, 0)),
                  pl.BlockSpec((SLAB, D), lambda i, j: (_w_row_block(j), 0)),
                  pl.BlockSpec((tm, D), lambda i, j: (i, 0)), pl.BlockSpec((tm, D), lambda i, j: (i, 0)),
                  pl.BlockSpec((1, 3, D), lambda i, j: (i // tps, 0, 0))],
        out_specs=(pl.BlockSpec((tm, D), lambda i, j: (i, 0)),
                   pl.BlockSpec((1, 2, 8, D), lambda i, j: (i // tps, 0, 0, 0))),
        scratch_shapes=[pltpu.VMEM((tm, D), F32)], name="grad_h",
        compiler_params=_params(48, ("arbitrary", "arbitrary")))(dproj, w_in_t, gx0, x, ada)


def _scatter_partials(grads):
    n = len(grads)
    rows = [g.shape[0] // N_DEV for g in grads]

    def body(*refs):
        srcs, lands = refs[:n], refs[n:2 * n]
        send_sems, recv_sems, local_sems = refs[2 * n:]
        me = _flat(*_my_position())

        def block(a, dev):
            return srcs[a].at[pl.ds(pl.multiple_of(dev * rows[a], 8), rows[a]), :]

        mine = [pltpu.make_async_copy(block(a, me), lands[a].at[me], local_sems.at[a]) for a in range(n)]
        for cp in mine:
            cp.start()
        sends = []
        for a in range(n):
            for mask in range(1, N_DEV):
                peer = _peer(mask)
                sends.append(pltpu.make_async_remote_copy(
                    src_ref=block(a, _flat(*peer)), dst_ref=lands[a].at[me], send_sem=send_sems.at[7 * a + mask - 1],
                    recv_sem=recv_sems.at[7 * a + mask - 1], device_id=peer, device_id_type=MESH))
        for cp in sends:
            cp.start()
        for a in range(n):
            for mask in range(1, N_DEV):
                peer = _peer(mask)
                pltpu.make_async_remote_copy(
                    src_ref=block(a, me), dst_ref=lands[a].at[_flat(*peer)], send_sem=send_sems.at[7 * a + mask - 1],
                    recv_sem=recv_sems.at[7 * a + mask - 1], device_id=peer, device_id_type=MESH).wait_recv()
        for cp in sends:
            cp.wait_send()
        for cp in mine:
            cp.wait()

    out_shape = tuple(jax.ShapeDtypeStruct((N_DEV, r, g.shape[1]), g.dtype) for r, g in zip(rows, grads))
    return _pcall(body, out_shape=out_shape, in_specs=[ANY] * n, out_specs=(ANY,) * n, name="scatter_partials",
                  scratch_shapes=[pltpu.SemaphoreType.DMA((7 * n,)), pltpu.SemaphoreType.DMA((7 * n,)),
                                  pltpu.SemaphoreType.DMA((n,))])(*grads)


def _sum_slots(land, n_steps, name):
    _, rows, ncol = land.shape
    rb = rows // n_steps

    def body(l_ref, o_ref):
        tot = l_ref[0]
        for k in range(1, N_DEV):
            tot = tot + l_ref[k]
        o_ref[...] = tot

    return _pcall(body, grid=(n_steps,), out_shape=jax.ShapeDtypeStruct((rows, ncol), F32),
                  in_specs=[pl.BlockSpec((N_DEV, rb, ncol), lambda i: (0, i, 0))],
                  out_specs=pl.BlockSpec((rb, ncol), lambda i: (i, 0)), name=name,
                  compiler_params=_params(32, ("parallel",)))(land)


def _adamw(w, g, m, v):
    m_new = B1 * m + (1.0 - B1) * g
    v_new = B2 * v + (1.0 - B2) * (g * g)
    m_hat = m_new / (1.0 - B1 ** STEP)
    v_hat = v_new / (1.0 - B2 ** STEP)
    delta = -LR * (m_hat / (jnp.sqrt(v_hat) + EPS) + WD * w)
    return delta, m_new, v_new


def _adam_transposed(g_t, w, m, v, name):
    n = g_t.shape[0]
    full, tail = n // 128, n % 128

    def body(gt_ref, w_ref, m_ref, v_ref, g_ref, d_ref, mo_ref, vo_ref):
        def update(g, sl):
            delta, m_new, v_new = _adamw(w_ref[:, sl], g, m_ref[:, sl], v_ref[:, sl])
            g_ref[:, sl], d_ref[:, sl], mo_ref[:, sl], vo_ref[:, sl] = g, delta, m_new, v_new

        for a in range(full):
            update(gt_ref[pl.ds(128 * a, 128), :].T, pl.ds(128 * a, 128))
        if tail:
            update(gt_ref[pl.ds(n - 128, 128), :].T[:, 128 - tail:], pl.ds(128 * full, tail))

    shape = jax.ShapeDtypeStruct(w.shape, F32)
    return _pcall(body, out_shape=(shape,) * 4, in_specs=[VMEM] * 4, out_specs=(VMEM,) * 4, name=name,
                  compiler_params=_params(56))(g_t, w, m, v)


def _adam_many(items, name):
    n = len(items)

    def body(*refs):
        ins, outs = refs[:4 * n], refs[4 * n:]
        for a in range(n):
            w_ref, g_ref, m_ref, v_ref = ins[4 * a:4 * a + 4]
            delta, m_new, v_new = _adamw(w_ref[...], g_ref[...], m_ref[...], v_ref[...])
            outs[3 * a][...], outs[3 * a + 1][...], outs[3 * a + 2][...] = delta, m_new, v_new

    out_shape = tuple(jax.ShapeDtypeStruct(it[0].shape, F32) for it in items for _ in range(3))
    flat = [arr for it in items for arr in it]
    res = _pcall(body, out_shape=out_shape, in_specs=[VMEM] * (4 * n), out_specs=(VMEM,) * (3 * n), name=name,
                 compiler_params=_params(32))(*flat)
    return [tuple(res[3 * a:3 * a + 3]) for a in range(n)]


def _adam_w_ada(cact_all, dada_mine, w, m, v):
    def body(c_ref, d_ref, w_ref, m_ref, v_ref, g_ref, dl_ref, mo_ref, vo_ref):
        g = _tn(c_ref[...].astype(BF16), d_ref[...].astype(BF16))
        delta, m_new, v_new = _adamw(w_ref[...], g, m_ref[...], v_ref[...])
        g_ref[...], dl_ref[...], mo_ref[...], vo_ref[...] = g, delta, m_new, v_new

    shape = jax.ShapeDtypeStruct(w.shape, F32)
    return _pcall(body, out_shape=(shape,) * 4, in_specs=[VMEM] * 5, out_specs=(VMEM,) * 4, name="adam_w_ada",
                  compiler_params=_params(32))(cact_all, dada_mine, w, m, v)


def kernel(x, c, w_ada, b_ada, w_in, b_in, conv_w, w_proj_attn, w_proj_conv, w_out, b_out, ln_g, ln_b, loss_target, m_w_ada, m_b_ada, m_w_in, m_b_in, m_conv_w, m_w_proj_attn, m_w_proj_conv, m_w_out, m_b_out, m_ln_g, m_ln_b, v_w_ada, v_b_ada, v_w_in, v_b_in, v_conv_w, v_w_proj_attn, v_w_proj_conv, v_w_out, v_b_out, v_ln_g, v_ln_b):
    nbat, seq, _ = x.shape
    t = nbat * seq
    me = _flat(*_my_position())
    x2, tgt2 = x.reshape(t, D), loss_target.reshape(t, D)
    sq = lambda a: a.reshape(a.shape[1:])

    w_in_t_s, w_pa_t_s, w_pb_s, w_out_s, cact_s, cw_s = _prep(sq(w_in), sq(w_proj_attn), sq(w_proj_conv), sq(w_out), c, sq(conv_w))
    w_in_t, w_pa_t, w_pb, w_o, cact_g, cw_g = _gather_rows([w_in_t_s, w_pa_t_s, w_pb_s, w_out_s, cact_s, cw_s])
    cact_all = cact_g.reshape(N_DEV, 8, D)[:, :nbat].reshape(N_DEV * nbat, D)
    cw = cw_g.reshape(N_DEV, 8, -1)[:, :3].transpose(1, 0, 2).reshape(3, D)

    ncol = w_ada.shape[2]
    b_ada_mine = lax.dynamic_slice(b_ada, (0, me * ncol), (1, ncol))
    ada_slots = _ada_forward(cact_all, sq(w_ada), b_ada_mine)
    ada_all = ada_slots.transpose(1, 0, 2).reshape(N_DEV * nbat, 3, D)
    ada = lax.dynamic_slice(ada_all, (me * nbat, 0, 0), (nbat, 3, D))

    h = _make_h(x2, ada)
    qkv, rest = _project(h, w_in_t, b_in.reshape(N_SLAB, 1, SLAB))
    ols = [_attn_forward(qkv, g, nbat) for g in range(3)]
    (dproj, gx0, do_attn, ol_tot, merged, do_f, bbs, dyc, a_bf, dya, gb_rest, svec, dgate) = _mid(
        rest, ols, x2, tgt2, ada, cw, b_out, ln_g, ln_b, w_pa_t, w_pb, w_o)

    gb_qkv = []
    for g in range(3):
        dproj, gb = _attn_backward(qkv, do_attn, ol_tot, dproj, g, nbat)
        gb_qkv.append(gb)
    grad_x, dss = _grad_h(dproj, w_in_t, gx0, x2, ada)
    g_w_in_t = _grad_w_in_t(dproj, h)
    g_w_out = _grad_rows_2d(merged, do_f, "grad_w_out")
    g_w_pb = _grad_rows_2d(bbs, dyc, "grad_w_proj_conv")
    g_w_pa_t = _grad_rows_2d(dya, a_bf, "grad_w_proj_attn")

    rows8, tot, g_bada = _small_reduce(gb_rest, gb_qkv, svec, dgate, dss)
    loss = tot[0, P_LOSS]
    dada_all = rows8[:, 0, P_DADA:].reshape(N_DEV * nbat, 3 * D)
    dada_mine = lax.dynamic_slice(dada_all, (0, me * ncol), (N_DEV * nbat, ncol))

    lands = _scatter_partials([g_w_in_t, g_w_pa_t, g_w_pb, g_w_out])
    g_in_t = _sum_slots(lands[0], 8, "sum_w_in")
    g_pa_t = _sum_slots(lands[1], 1, "sum_w_proj_attn")
    g_pb = _sum_slots(lands[2], 1, "sum_w_proj_conv")
    g_out = _sum_slots(lands[3], 1, "sum_w_out")

    g_win, d_win, nm_win, nv_win = _adam_transposed(g_in_t, sq(w_in), sq(m_w_in), sq(v_w_in), "adam_w_in")
    g_wpa, d_wpa, nm_wpa, nv_wpa = _adam_transposed(g_pa_t, sq(w_proj_attn), sq(m_w_proj_attn), sq(v_w_proj_attn), "adam_w_proj_attn")
    g_wada, d_wada, nm_wada, nv_wada = _adam_w_ada(cact_all, dada_mine, sq(w_ada), sq(m_w_ada), sq(v_w_ada))
    g_bin = tot[:, P_BIN:P_BIN + D_IN]
    g_bout = tot[:, P_BOUT:P_BOUT + D]
    g_lng = tot[:, P_LNG:P_LNG + D]
    g_lnb = tot[:, P_LNB:P_LNB + D]
    g_conv = lax.dynamic_slice(tot[:, P_CONV:P_CONV + 3 * D].reshape(3, D), (0, me * cw_s.shape[1]), (3, cw_s.shape[1]))
    upd = _adam_many([
        (sq(w_proj_conv), g_pb, sq(m_w_proj_conv), sq(v_w_proj_conv)),
        (sq(w_out), g_out, sq(m_w_out), sq(v_w_out)),
        (b_ada, g_bada, m_b_ada, v_b_ada), (b_in, g_bin, m_b_in, v_b_in), (sq(conv_w), g_conv, sq(m_conv_w), sq(v_conv_w)),
        (b_out, g_bout, m_b_out, v_b_out), (ln_g, g_lng, m_ln_g, v_ln_g), (ln_b, g_lnb, m_ln_b, v_ln_b)], "adam_rest")
    (d_wpb, nm_wpb, nv_wpb), (d_wout, nm_wout, nv_wout), (d_bada, nm_bada, nv_bada), (d_bin, nm_bin, nv_bin), \
        (d_conv, nm_conv, nv_conv), (d_bout, nm_bout, nv_bout), (d_lng, nm_lng, nv_lng), (d_lnb, nm_lnb, nv_lnb) = upd

    ex = lambda a: a.reshape((1,) + a.shape)
    grads = [ex(g_wada), g_bada, ex(g_win), g_bin, ex(g_conv), ex(g_wpa), ex(g_pb), ex(g_out), g_bout, g_lng, g_lnb]
    deltas = [ex(d_wada), d_bada, ex(d_win), d_bin, ex(d_conv), ex(d_wpa), ex(d_wpb), ex(d_wout), d_bout, d_lng, d_lnb]
    new_m = [ex(nm_wada), nm_bada, ex(nm_win), nm_bin, ex(nm_conv), ex(nm_wpa), ex(nm_wpb), ex(nm_wout), nm_bout, nm_lng, nm_lnb]
    new_v = [ex(nv_wada), nv_bada, ex(nv_win), nv_bin, ex(nv_conv), ex(nv_wpa), ex(nv_wpb), ex(nv_wout), nv_bout, nv_lng, nv_lnb]
    return (loss, grad_x.reshape(x.shape), *grads, *deltas, *new_m, *new_v)
```

```python
import functools

import jax
import jax.numpy as jnp
from jax import lax
from jax.experimental import pallas as pl
from jax.experimental.pallas import tpu as pltpu

F32, BF16 = jnp.float32, jnp.bfloat16
MESH = pl.DeviceIdType.MESH
N_DEV = 8
D = 1024
SLAB = 256
N_QKV, N_REST = 9, 25
N_SLAB = N_QKV + N_REST
D_IN = N_SLAB * SLAB
DP_SLABS = 36
BLK = 128
GROUPS = ((128, 1), (512, 4), (2048, 16))
ALPHA = 2.0 ** 0.25
LN_EPS = 1e-5
LR, B1, B2, EPS, WD, STEP = 0.001, 0.9, 0.999, 1e-08, 0.01, 10
R_ZA, R_UX, R_GB, R_GC, R_ZC, R_GA, R_GBM = 0, 1, 5, 9, 13, 17, 21
P_BIN, P_BOUT, P_LNG, P_LNB, P_CONV, P_LOSS, P_DADA = 0, 8704, 9728, 10752, 11776, 14848, 14976
MIB = 1024 * 1024

_pcall = pl.pallas_call
ANY = pl.BlockSpec(memory_space=pl.ANY)
VMEM = pl.BlockSpec(memory_space=pltpu.VMEM)


def _params(vmem_mib=None, sem=None):
    kw = {}
    if vmem_mib is not None:
        kw["vmem_limit_bytes"] = vmem_mib * MIB
    if sem is not None:
        kw["dimension_semantics"] = sem
    return pltpu.CompilerParams(**kw)


def _nn(a, b):
    return jnp.dot(a, b, preferred_element_type=F32)


def _nt(a, b):
    return lax.dot_general(a, b, (((1,), (1,)), ((), ())), preferred_element_type=F32)


def _tn(a, b):
    return lax.dot_general(a, b, (((0,), (0,)), ((), ())), preferred_element_type=F32)


def _sigmoid(v):
    return 1.0 / (1.0 + jnp.exp(-v))


def _part8(v):
    return v.reshape(v.shape[0] // 8, 8, v.shape[1]).sum(axis=0)


def _my_position():
    return lax.axis_index("x"), lax.axis_index("y"), lax.axis_index("c")


def _flat(px, py, pc):
    return 4 * px + 2 * py + pc


def _peer(mask):
    x, y, c = _my_position()
    return (x ^ ((mask >> 2) & 1), y ^ ((mask >> 1) & 1), c ^ (mask & 1))


def _prep(w_in, w_pa, w_pb, w_out, c, conv_w):
    n_in = w_in.shape[1]
    full, tail = n_in // 128, n_in % 128

    def body(win_ref, wpa_ref, wpb_ref, wout_ref, c_ref, cw_ref, wint_ref, wpat_ref, wpb_o, wout_o, cact_ref, cwp_ref):
        for a in range(full):
            wint_ref[pl.ds(128 * a, 128), :] = win_ref[:, pl.ds(128 * a, 128)].T.astype(BF16)
        if tail:
            t = win_ref[:, pl.ds(n_in - 128, 128)].T
            wint_ref[pl.ds(128 * full, tail), :] = t[128 - tail:].astype(BF16)
        wpat_ref[...] = wpa_ref[...].T.astype(BF16)
        wpb_o[...] = wpb_ref[...].astype(BF16)
        wout_o[...] = wout_ref[...].astype(BF16)
        cv = c_ref[...]
        cact_ref[...] = jnp.zeros_like(cact_ref)
        cact_ref[pl.ds(0, cv.shape[0]), :] = cv * _sigmoid(cv)
        cwp_ref[...] = jnp.zeros_like(cwp_ref)
        cwp_ref[pl.ds(0, 3), :] = cw_ref[...]

    out_shape = (jax.ShapeDtypeStruct((n_in, D), BF16), jax.ShapeDtypeStruct((w_pa.shape[1], w_pa.shape[0]), BF16),
                 jax.ShapeDtypeStruct(w_pb.shape, BF16), jax.ShapeDtypeStruct(w_out.shape, BF16),
                 jax.ShapeDtypeStruct((8, D), F32), jax.ShapeDtypeStruct((8, conv_w.shape[1]), F32))
    return _pcall(body, out_shape=out_shape, in_specs=[VMEM] * 6, out_specs=(VMEM,) * 6, name="prep",
                  compiler_params=_params(32))(w_in, w_pa, w_pb, w_out, c, conv_w)


def _gather_rows(shards):
    n = len(shards)

    def body(*refs):
        srcs, outs = refs[:n], refs[n:2 * n]
        send_sems, recv_sems, local_sems = refs[2 * n:]
        x, y, c = _my_position()
        me, sibling = (x, y, c), (x, y, 1 - c)
        chips = [(1 - x, y), (x, 1 - y), (1 - x, 1 - y)]

        def rows(a, px, py, pc):
            r = shards[a].shape[0]
            return outs[a].at[pl.ds(pl.multiple_of(_flat(px, py, pc) * r, r), r), :]

        def copy(a, k, block, to, src=None):
            return pltpu.make_async_remote_copy(
                src_ref=rows(a, *block) if src is None else src, dst_ref=rows(a, *block),
                send_sem=send_sems.at[7 * a + k], recv_sem=recv_sems.at[7 * a + k], device_id=to, device_id_type=MESH)

        mine = [pltpu.make_async_copy(srcs[a], rows(a, *me), local_sems.at[a]) for a in range(n)]
        for cp in mine:
            cp.start()
        first = []
        for a in range(n):
            first.append(copy(a, 0, me, sibling, src=srcs[a]))
            first += [copy(a, 1 + j, me, (*chip, c), src=srcs[a]) for j, chip in enumerate(chips)]
        for cp in first:
            cp.start()
        passed = []
        for j, chip in enumerate(chips):
            for a in range(n):
                copy(a, 1 + j, (*chip, c), me).wait_recv()
                cp = copy(a, 4 + j, (*chip, c), sibling)
                cp.start()
                passed.append(cp)
        for a in range(n):
            copy(a, 0, sibling, me).wait_recv()
        for j, chip in enumerate(chips):
            for a in range(n):
                copy(a, 4 + j, (*chip, 1 - c), me).wait_recv()
        for cp in first + passed:
            cp.wait_send()
        for cp in mine:
            cp.wait()

    out_shape = tuple(jax.ShapeDtypeStruct((N_DEV * s.shape[0], s.shape[1]), s.dtype) for s in shards)
    return _pcall(body, out_shape=out_shape, in_specs=[ANY] * n, out_specs=(ANY,) * n, name="gather_rows",
                  scratch_shapes=[pltpu.SemaphoreType.DMA((7 * n,)), pltpu.SemaphoreType.DMA((7 * n,)),
                                  pltpu.SemaphoreType.DMA((n,))])(*shards)


def _exchange_slots(out_ref, send_sems, recv_sems):
    me = _flat(*_my_position())
    copies = []
    for mask in range(1, N_DEV):
        peer = _peer(mask)
        copies.append((mask, pltpu.make_async_remote_copy(
            src_ref=out_ref.at[me], dst_ref=out_ref.at[me], send_sem=send_sems.at[mask - 1],
            recv_sem=recv_sems.at[mask - 1], device_id=peer, device_id_type=MESH)))
    for _, cp in copies:
        cp.start()
    for mask, _ in copies:
        peer = _peer(mask)
        pltpu.make_async_remote_copy(
            src_ref=out_ref.at[_flat(*peer)], dst_ref=out_ref.at[_flat(*peer)], send_sem=send_sems.at[mask - 1],
            recv_sem=recv_sems.at[mask - 1], device_id=peer, device_id_type=MESH).wait_recv()
    for _, cp in copies:
        cp.wait_send()


def _ada_forward(cact_all, w_ada, b_ada_mine):
    nb, ncol = cact_all.shape[0], w_ada.shape[1]

    def body(c_ref, w_ref, b_ref, out_ref, send_sems, recv_sems):
        me = _flat(*_my_position())
        out_ref[me] = _nn(c_ref[...].astype(BF16), w_ref[...].astype(BF16)) + b_ref[...]
        _exchange_slots(out_ref, send_sems, recv_sems)

    return _pcall(body, out_shape=jax.ShapeDtypeStruct((N_DEV, nb, ncol), F32), in_specs=[VMEM] * 3, out_specs=VMEM,
                  scratch_shapes=[pltpu.SemaphoreType.DMA((7,)), pltpu.SemaphoreType.DMA((7,))], name="ada_forward",
                  compiler_params=_params(16))(cact_all, w_ada, b_ada_mine)


def _small_reduce(gb_rest, gb_qkv, svec, dgate, dss):
    nbat = dgate.shape[0]

    def body(gbr_ref, q0_ref, q1_ref, q2_ref, sv_ref, dg_ref, dss_ref, rows_ref, tot_ref, gbada_ref, send_sems, recv_sems):
        me = _flat(*_my_position())

        def put(off, v):
            rows_ref[me, :, pl.ds(off, v.shape[1])] = v

        def row(v):
            return jnp.sum(v, axis=0, keepdims=True)

        for g, q_ref in enumerate((q0_ref, q1_ref, q2_ref)):
            for which in range(3):
                put(P_BIN + SLAB * (3 * which + g), row(q_ref[which]))
        for s in range(N_REST):
            put(P_BIN + SLAB * (N_QKV + s), row(gbr_ref[s]))
        put(P_LNG, row(sv_ref[0]))
        put(P_LNB, row(sv_ref[1]))
        put(P_BOUT, row(sv_ref[2]))
        for j in range(3):
            put(P_CONV + D * j, row(sv_ref[3 + j]))
        loss = (0.5 / D) * jnp.sum(row(sv_ref[6]), axis=1, keepdims=True)
        put(P_LOSS, jnp.broadcast_to(loss, (1, 128)))
        for b in range(nbat):
            put(P_DADA + 3 * D * b, row(dss_ref[b, 0]))
            put(P_DADA + 3 * D * b + D, row(dss_ref[b, 1]))
            put(P_DADA + 3 * D * b + 2 * D, row(dg_ref[b]))
        _exchange_slots(rows_ref, send_sems, recv_sems)
        tot = rows_ref[0]
        for k in range(1, N_DEV):
            tot = tot + rows_ref[k]
        tot_ref[...] = tot
        gbada = tot[:, P_DADA:P_DADA + 3 * D]
        for b in range(1, nbat):
            gbada = gbada + tot[:, P_DADA + 3 * D * b:P_DADA + 3 * D * (b + 1)]
        gbada_ref[...] = gbada

    p_len = P_DADA + nbat * 3 * D
    out_shape = (jax.ShapeDtypeStruct((N_DEV, 1, p_len), F32), jax.ShapeDtypeStruct((1, p_len), F32),
                 jax.ShapeDtypeStruct((1, 3 * D), F32))
    return _pcall(body, out_shape=out_shape, in_specs=[VMEM] * 7, out_specs=(VMEM, VMEM, VMEM),
                  scratch_shapes=[pltpu.SemaphoreType.DMA((7,)), pltpu.SemaphoreType.DMA((7,))], name="small_reduce",
                  compiler_params=_params(16))(gb_rest, *gb_qkv, svec, dgate, dss)


def _make_h(x, ada, tm=512):
    t = x.shape[0]
    tps = (t // ada.shape[0]) // tm

    def body(x_ref, ada_ref, h_ref):
        h_ref[...] = (x_ref[...] * (1.0 + ada_ref[0, 1:2, :]) + ada_ref[0, 0:1, :]).astype(BF16)

    return _pcall(body, grid=(t // tm,), out_shape=jax.ShapeDtypeStruct((t, D), BF16),
                  in_specs=[pl.BlockSpec((tm, D), lambda i: (i, 0)), pl.BlockSpec((1, 3, D), lambda i: (i // tps, 0, 0))],
                  out_specs=pl.BlockSpec((tm, D), lambda i: (i, 0)), name="make_h",
                  compiler_params=_params(32, ("parallel",)))(x, ada)


def _project(h, w_in_t, b_in3, chunk=512):
    t = h.shape[0]

    def body(h_ref, w_ref, b_ref, qkv_ref, rest_ref):
        j = pl.program_id(0)

        def rows(r, carry):
            sl = pl.ds(pl.multiple_of(r * chunk, chunk), chunk)
            v = _nt(h_ref[sl, :], w_ref[...]) + b_ref[0]

            @pl.when(j < N_QKV)
            def _():
                qkv_ref[0, sl, :] = v.astype(BF16)

            @pl.when(j >= N_QKV)
            def _():
                rest_ref[0, sl, :] = v
            return carry

        lax.fori_loop(0, t // chunk, rows, 0)

    return _pcall(
        body, grid=(N_SLAB,),
        out_shape=(jax.ShapeDtypeStruct((N_QKV, t, SLAB), BF16), jax.ShapeDtypeStruct((N_REST, t, SLAB), F32)),
        in_specs=[pl.BlockSpec((t, D), lambda j: (0, 0)), pl.BlockSpec((SLAB, D), lambda j: (j, 0)),
                  pl.BlockSpec((1, 1, SLAB), lambda j: (j, 0, 0))],
        out_specs=(pl.BlockSpec((1, t, SLAB), lambda j: (jnp.minimum(j, N_QKV - 1), 0, 0)),
                   pl.BlockSpec((1, t, SLAB), lambda j: (jnp.maximum(j - N_QKV, 0), 0, 0))),
        name="project", compiler_params=_params(48, ("arbitrary",)))(h, w_in_t, b_in3)


def _bias_tables(g):
    window, dil = GROUPS[g]
    span = window // dil
    qi = jnp.arange(BLK)[:, None]
    kj = jnp.arange(2 * BLK)[None, :]
    delta = qi + BLK - kj
    valid = (delta >= 0) & (delta <= span)
    heads = jnp.arange(4, dtype=F32) + 4.0 * g
    slopes = 2.0 ** (-8.0 * (heads + 1.0) / 12.0)
    bias = -slopes[:, None, None] * (delta * dil).astype(F32)[None]
    return jnp.where(valid[None], bias, -1e30)


def _head_masks(shape):
    lane = lax.broadcasted_iota(jnp.int32, shape, 1)
    return [(lane >= 64 * h) & (lane < 64 * (h + 1)) for h in range(4)]


def _regroup(load_half, dst_ref, stage_ref, n, dil):
    for hlf in range(2):
        stage_ref[hlf] = load_half(hlf)

    def residue(r, carry):
        for hlf in range(2):
            dst_ref[pl.ds(pl.multiple_of(r * n, BLK), n), pl.ds(128 * hlf, 128)] = (
                stage_ref[hlf, pl.ds(r, n, stride=dil), :].astype(dst_ref.dtype))
        return carry

    lax.fori_loop(0, dil, residue, 0)


def _store_block(nat_ref, r, i, val, dil):
    for hlf in range(2):
        nat_ref[hlf, pl.ds(r + dil * BLK * i, BLK, stride=dil), :] = val[:, 128 * hlf:128 * (hlf + 1)]


def _for_blocks(block, dil, nblk):
    def residue(r, carry):
        block(r, 0, True)
        if nblk > 1:
            def loop(i, c):
                block(r, i, False)
                return c
            lax.fori_loop(1, nblk, loop, 0)
        return carry

    if dil == 1:
        residue(0, 0)
    else:
        lax.fori_loop(0, dil, residue, 0)


def _attn_forward(qkv, g, nbat):
    t = qkv.shape[1]
    seq = t // nbat
    dil = GROUPS[g][1]
    n = seq // dil
    nblk = n // BLK
    qkv4 = qkv.reshape(3, 3, t, SLAB)

    def body(qkv_ref, bias_ref, ol_ref, *scratch):
        masks = _head_masks((BLK, SLAB))
        if dil > 1:
            stage, qd, kd, vd, nat_o, nat_l = scratch
            for which, dst in enumerate((qd, kd, vd)):
                _regroup(lambda hlf, which=which: qkv_ref[which, 0, :, pl.ds(128 * hlf, 128)].astype(F32), dst, stage, n, dil)
        else:
            qd, kd, vd = qkv_ref.at[0, 0], qkv_ref.at[1, 0], qkv_ref.at[2, 0]

        def block(r, i, first):
            base = r * n
            qs = pl.ds(pl.multiple_of(base + i * BLK, BLK), BLK)
            ks = pl.ds(pl.multiple_of(base, BLK), BLK) if first else pl.ds(pl.multiple_of(base + (i - 1) * BLK, BLK), 2 * BLK)
            q, kk, vv = qd[qs, :], kd[ks, :], vd[ks, :]
            out = jnp.zeros((BLK, SLAB), F32)
            lse = jnp.zeros((BLK, SLAB), F32)
            for h in range(4):
                qm = jnp.where(masks[h], q, jnp.zeros_like(q))
                bias = bias_ref[h, :, pl.ds(BLK, BLK)] if first else bias_ref[h]
                s = _nt(qm, kk) * 0.125 + bias
                m = jnp.max(s, axis=1, keepdims=True)
                p = jnp.exp(s - m)
                den = jnp.sum(p, axis=1, keepdims=True)
                o = _nn((p * (1.0 / den)).astype(BF16), vv)
                out = jnp.where(masks[h], o, out)
                lse = jnp.where(masks[h], m + jnp.log(den), lse)
            if dil > 1:
                _store_block(nat_o, r, i, out, dil)
                _store_block(nat_l, r, i, lse, dil)
            else:
                ol_ref[0, qs, :] = out
                ol_ref[1, qs, :] = lse

        _for_blocks(block, dil, nblk)
        if dil > 1:
            for hlf in range(2):
                ol_ref[0, :, pl.ds(128 * hlf, 128)] = nat_o[hlf]
                ol_ref[1, :, pl.ds(128 * hlf, 128)] = nat_l[hlf]

    scratch = []
    if dil > 1:
        scratch = [pltpu.VMEM((2, seq, 128), F32)] + [pltpu.VMEM((seq, SLAB), BF16)] * 3 + [pltpu.VMEM((2, seq, 128), F32)] * 2
    return _pcall(
        body, grid=(nbat,), out_shape=jax.ShapeDtypeStruct((2, t, SLAB), F32),
        in_specs=[pl.BlockSpec((3, 1, seq, SLAB), lambda b: (0, g, b, 0)),
                  pl.BlockSpec((4, BLK, 2 * BLK), lambda b: (0, 0, 0))],
        out_specs=pl.BlockSpec((2, seq, SLAB), lambda b: (0, b, 0)), scratch_shapes=scratch,
        name=f"attn_forward_{g}", compiler_params=_params(40, ("parallel",)))(qkv4, _bias_tables(g))


def _attn_backward(qkv, do_attn, ol_tot, dproj, g, nbat):
    t = qkv.shape[1]
    seq = t // nbat
    dil = GROUPS[g][1]
    n = seq // dil
    nblk = n // BLK
    qkv4 = qkv.reshape(3, 3, t, SLAB)
    dp4 = dproj.reshape(DP_SLABS // 3, 3, t, SLAB)

    def body(qkv_ref, do_ref, ol_ref, bias_ref, dp_in, dp_ref, gb_ref, dk_acc, dv_acc, *scratch):
        del dp_in
        masks = _head_masks((BLK, SLAB))

        @pl.when(pl.program_id(0) == 0)
        def _():
            gb_ref[...] = jnp.zeros_like(gb_ref)

        dk_acc[...] = jnp.zeros_like(dk_acc)
        dv_acc[...] = jnp.zeros_like(dv_acc)
        if dil > 1:
            stage, qd, kd, vd, dod, prodd, lsed, nat = scratch
            lanes = lambda hlf: pl.ds(128 * hlf, 128)
            for which, dst in enumerate((qd, kd, vd)):
                _regroup(lambda hlf, which=which: qkv_ref[which, 0, :, lanes(hlf)].astype(F32), dst, stage, n, dil)
            _regroup(lambda hlf: do_ref[:, lanes(hlf)].astype(F32), dod, stage, n, dil)
            _regroup(lambda hlf: do_ref[:, lanes(hlf)].astype(F32) * ol_ref[0, :, lanes(hlf)], prodd, stage, n, dil)
            _regroup(lambda hlf: ol_ref[1, :, lanes(hlf)], lsed, stage, n, dil)
        else:
            qd, kd, vd = qkv_ref.at[0, 0], qkv_ref.at[1, 0], qkv_ref.at[2, 0]

        def block(r, i, first):
            base = r * n
            qs = pl.ds(pl.multiple_of(base + i * BLK, BLK), BLK)
            ks = pl.ds(pl.multiple_of(base, BLK), BLK) if first else pl.ds(pl.multiple_of(base + (i - 1) * BLK, BLK), 2 * BLK)
            q, kk, vv = qd[qs, :], kd[ks, :], vd[ks, :]
            if dil > 1:
                do, prod, lse = dod[qs, :], prodd[qs, :], lsed[qs, :]
            else:
                do = do_ref[qs, :]
                prod = do.astype(F32) * ol_ref[0, qs, :]
                lse = ol_ref[1, qs, :]
            dq = jnp.zeros((BLK, SLAB), F32)
            for h in range(4):
                qm = jnp.where(masks[h], q, jnp.zeros_like(q))
                dom = jnp.where(masks[h], do, jnp.zeros_like(do))
                bias = bias_ref[h, :, pl.ds(BLK, BLK)] if first else bias_ref[h]
                s = _nt(qm, kk) * 0.125 + bias
                p = jnp.exp(s - lse[:, 64 * h:64 * h + 1])
                delta = jnp.sum(jnp.where(masks[h], prod, 0.0), axis=1, keepdims=True)
                ds = (p * (_nt(dom, vv) - delta)).astype(BF16)
                dv_acc[ks, :] += _tn(p.astype(BF16), dom)
                dk_acc[ks, :] += _tn(ds, qm) * 0.125
                dq = dq + jnp.where(masks[h], _nn(ds, kk), 0.0) * 0.125
            if dil > 1:
                _store_block(nat, r, i, dq, dil)
            else:
                dp_ref[0, 0, qs, :] = dq.astype(BF16)
            gb_ref[0] += _part8(dq)

        _for_blocks(block, dil, nblk)
        gb_ref[1] += _part8(dk_acc[...])
        gb_ref[2] += _part8(dv_acc[...])
        if dil > 1:
            def flush(which):
                for hlf in range(2):
                    dp_ref[which, 0, :, pl.ds(128 * hlf, 128)] = nat[hlf].astype(BF16)

            def to_token_order(acc_ref):
                def residue(r, carry):
                    for hlf in range(2):
                        nat[hlf, pl.ds(r, n, stride=dil), :] = acc_ref[pl.ds(pl.multiple_of(r * n, BLK), n), pl.ds(128 * hlf, 128)]
                    return carry
                lax.fori_loop(0, dil, residue, 0)

            flush(0)
            to_token_order(dk_acc)
            flush(1)
            to_token_order(dv_acc)
            flush(2)
        else:
            dp_ref[1, 0] = dk_acc[...].astype(BF16)
            dp_ref[2, 0] = dv_acc[...].astype(BF16)

    scratch = [pltpu.VMEM((seq, SLAB), F32)] * 2
    if dil > 1:
        scratch += ([pltpu.VMEM((2, seq, 128), F32)] + [pltpu.VMEM((seq, SLAB), BF16)] * 4 + [pltpu.VMEM((seq, SLAB), F32)] * 2
                    + [pltpu.VMEM((2, seq, 128), F32)])
    dp, gb = _pcall(
        body, grid=(nbat,),
        out_shape=(jax.ShapeDtypeStruct(dp4.shape, BF16), jax.ShapeDtypeStruct((3, 8, SLAB), F32)),
        in_specs=[pl.BlockSpec((3, 1, seq, SLAB), lambda b: (0, g, b, 0)),
                  pl.BlockSpec((seq, SLAB), lambda b: (b, 0)),
                  pl.BlockSpec((2, seq, SLAB), lambda b: (0, b, 0)),
                  pl.BlockSpec((4, BLK, 2 * BLK), lambda b: (0, 0, 0)), ANY],
        out_specs=(pl.BlockSpec((3, 1, seq, SLAB), lambda b: (DP_SLABS // 9 - 1, g, b, 0)),
                   pl.BlockSpec((3, 8, SLAB), lambda b: (0, 0, 0))),
        scratch_shapes=scratch, input_output_aliases={4: 0}, name=f"attn_backward_{g}",
        compiler_params=_params(48, ("arbitrary",)))(qkv4, do_attn, ol_tot, _bias_tables(g), dp4)
    return dp.reshape(DP_SLABS, t, SLAB), gb


def _mid(rest, ols, x, tgt, ada, cw, b_out, ln_g, ln_b, w_pa_t, w_pb, w_out, tm=256):
    t = x.shape[0]
    nbat = ada.shape[0]
    nt = t // tm
    tps = nt // nbat

    def body(rest_ref, halo_ref, ol0_ref, ol1_ref, ol2_ref, x_ref, t_ref, ada_ref, cw_ref, bout_ref, lng_ref, lnb_ref,
             wpat_ref, wpb_ref, wout_ref,
             dp_ref, gx0_ref, doa_ref, olt_ref, mg_ref, dof_ref, bbs_ref, dyc_ref, a_ref, dya_ref,
             gbr_ref, sv_ref, dgate_ref, carry_ref):
        i = pl.program_id(0)
        ti = nt - 1 - i
        pos = ti % tps

        @pl.when(i == 0)
        def _():
            gbr_ref[...] = jnp.zeros_like(gbr_ref)
            sv_ref[...] = jnp.zeros_like(sv_ref)

        @pl.when(pos == tps - 1)
        def _():
            dgate_ref[...] = jnp.zeros_like(dgate_ref)
            carry_ref[...] = jnp.zeros_like(carry_ref)

        row = lax.broadcasted_iota(jnp.int32, (tm, SLAB), 0)
        halo_on = (pos > 0).astype(F32)

        def cols(s):
            return pl.ds(SLAB * s, SLAB)

        l0, l1, l2 = ol0_ref[1], ol1_ref[1], ol2_ref[1]
        mx = jnp.maximum(jnp.maximum(l0, l1), l2)
        e0, e1, e2 = jnp.exp(l0 - mx), jnp.exp(l1 - mx), jnp.exp(l2 - mx)
        den = e0 + e1 + e2
        o_attn = (e0 * ol0_ref[0] + e1 * ol1_ref[0] + e2 * ol2_ref[0]) * (1.0 / den)
        olt_ref[0] = o_attn
        olt_ref[1] = mx + jnp.log(den)
        z_a = rest_ref[R_ZA]
        sg_za = _sigmoid(z_a)
        a_ref[...] = (o_attn * z_a * sg_za).astype(BF16)
        y_attn = _nt(a_ref[...], wpat_ref[...])

        def conv_parts(s):
            ux, gc = rest_ref[R_UX + s], rest_ref[R_GC + s]
            u = gc * ux
            hu = halo_ref[R_GC + s] * halo_ref[R_UX + s] * halo_on
            u1 = jnp.where(row == 0, hu[7:8], pltpu.roll(u, 1, 0))
            u2 = jnp.where(row == 0, hu[6:7], jnp.where(row == 1, hu[7:8], pltpu.roll(u, 2, 0)))
            conv = cw_ref[0:1, cols(s)] * u2 + cw_ref[1:2, cols(s)] * u1 + cw_ref[2:3, cols(s)] * u
            zc = rest_ref[R_ZC + s]
            sg = _sigmoid(zc)
            return ux, gc, u, u1, u2, conv, zc, sg

        for s in range(4):
            ux, gc, u, u1, u2, conv, zc, sg = conv_parts(s)
            bbs_ref[:, cols(s)] = (rest_ref[R_GB + s] * conv * (zc * sg)).astype(BF16)
        y_conv = _nn(bbs_ref[...], wpb_ref[...])

        for s in range(4):
            s_a, s_b = _sigmoid(rest_ref[R_GA + s]), _sigmoid(rest_ref[R_GBM + s])
            mg_ref[:, cols(s)] = (s_a * y_attn[:, SLAB * s:SLAB * (s + 1)] + s_b * y_conv[:, SLAB * s:SLAB * (s + 1)]).astype(BF16)
        o = _nn(mg_ref[...], wout_ref[...]) + bout_ref[...]
        gate = ada_ref[0, 2:3, :]
        r = ALPHA * x_ref[...] + gate * o
        mu = jnp.mean(r, axis=1, keepdims=True)
        rc = r - mu
        rstd = lax.rsqrt(jnp.mean(rc * rc, axis=1, keepdims=True) + LN_EPS)
        xhat = rc * rstd
        err = xhat * lng_ref[...] + lnb_ref[...] - t_ref[...]
        sv_ref[6] += _part8(err * err)
        dy = err * (1.0 / D)
        sv_ref[0] += _part8(dy * xhat)
        sv_ref[1] += _part8(dy)
        dxh = dy * lng_ref[...]
        dr = rstd * (dxh - jnp.mean(dxh, axis=1, keepdims=True) - xhat * jnp.mean(dxh * xhat, axis=1, keepdims=True))
        gx0_ref[...] = ALPHA * dr
        dgate_ref[0] += _part8(dr * o)
        do_ = dr * gate
        sv_ref[2] += _part8(do_)
        dof_ref[...] = do_.astype(BF16)
        dmerged = _nt(dof_ref[...], wout_ref[...])
        for s in range(4):
            s_a, s_b = _sigmoid(rest_ref[R_GA + s]), _sigmoid(rest_ref[R_GBM + s])
            dm = dmerged[:, SLAB * s:SLAB * (s + 1)]
            ya, yc = y_attn[:, SLAB * s:SLAB * (s + 1)], y_conv[:, SLAB * s:SLAB * (s + 1)]
            dya_ref[:, cols(s)] = (dm * s_a).astype(BF16)
            dyc_ref[:, cols(s)] = (dm * s_b).astype(BF16)
            dga = dm * ya * s_a * (1.0 - s_a)
            dgb = dm * yc * s_b * (1.0 - s_b)
            dp_ref[R_GA + s] = dga.astype(BF16)
            dp_ref[R_GBM + s] = dgb.astype(BF16)
            gbr_ref[R_GA + s] += _part8(dga)
            gbr_ref[R_GBM + s] += _part8(dgb)

        da = _nn(dya_ref[...], wpat_ref[...])
        doa_ref[...] = (da * z_a * sg_za).astype(BF16)
        dza = da * o_attn * (sg_za * (1.0 + z_a * (1.0 - sg_za)))
        dp_ref[R_ZA] = dza.astype(BF16)
        gbr_ref[R_ZA] += _part8(dza)

        dbb = _nt(dyc_ref[...], wpb_ref[...])
        for s in range(4):
            ux, gc, u, u1, u2, conv, zc, sg = conv_parts(s)
            gb = rest_ref[R_GB + s]
            d_b = dbb[:, SLAB * s:SLAB * (s + 1)]
            szc = zc * sg
            dgb_ = d_b * conv * szc
            dconv = d_b * gb * szc
            dzc = d_b * gb * conv * (sg * (1.0 + zc * (1.0 - sg)))
            sv_ref[3, :, cols(s)] += _part8(dconv * u2)
            sv_ref[4, :, cols(s)] += _part8(dconv * u1)
            sv_ref[5, :, cols(s)] += _part8(dconv * u)
            nxt = carry_ref[:, cols(s)]
            d1 = jnp.where(row == tm - 1, nxt[0:1], pltpu.roll(dconv, tm - 1, 0))
            d2 = jnp.where(row == tm - 1, nxt[1:2], jnp.where(row == tm - 2, nxt[0:1], pltpu.roll(dconv, tm - 2, 0)))
            carry_ref[:, cols(s)] = dconv[0:8]
            du = cw_ref[2:3, cols(s)] * dconv + cw_ref[1:2, cols(s)] * d1 + cw_ref[0:1, cols(s)] * d2
            dgc, dux = du * ux, du * gc
            for slab, val in ((R_GB + s, dgb_), (R_ZC + s, dzc), (R_GC + s, dgc), (R_UX + s, dux)):
                dp_ref[slab] = val.astype(BF16)
                gbr_ref[slab] += _part8(val)

    def tile(i):
        return nt - 1 - i

    row_blk = lambda i: (tile(i), 0)
    slab_blk = lambda i: (0, tile(i), 0)
    const2 = lambda i: (0, 0)
    const3 = lambda i: (0, 0, 0)
    in_specs = [
        pl.BlockSpec((N_REST, tm, SLAB), slab_blk),
        pl.BlockSpec((N_REST, 8, SLAB), lambda i: (0, jnp.maximum(tile(i) * (tm // 8) - 1, 0), 0)),
        pl.BlockSpec((2, tm, SLAB), slab_blk), pl.BlockSpec((2, tm, SLAB), slab_blk), pl.BlockSpec((2, tm, SLAB), slab_blk),
        pl.BlockSpec((tm, D), row_blk), pl.BlockSpec((tm, D), row_blk),
        pl.BlockSpec((1, 3, D), lambda i: (tile(i) // tps, 0, 0)),
        pl.BlockSpec((3, D), const2), pl.BlockSpec((1, D), const2), pl.BlockSpec((1, D), const2), pl.BlockSpec((1, D), const2),
        pl.BlockSpec((D, SLAB), const2), pl.BlockSpec((D, D), const2), pl.BlockSpec((D, D), const2)]
    bf_rows = lambda: jax.ShapeDtypeStruct((t, D), BF16)
    out_shape = (
        jax.ShapeDtypeStruct((DP_SLABS, t, SLAB), BF16), jax.ShapeDtypeStruct((t, D), F32),
        jax.ShapeDtypeStruct((t, SLAB), BF16), jax.ShapeDtypeStruct((2, t, SLAB), F32),
        bf_rows(), bf_rows(), bf_rows(), bf_rows(), jax.ShapeDtypeStruct((t, SLAB), BF16), bf_rows(),
        jax.ShapeDtypeStruct((N_REST, 8, SLAB), F32), jax.ShapeDtypeStruct((7, 8, D), F32),
        jax.ShapeDtypeStruct((nbat, 8, D), F32))
    out_specs = (
        pl.BlockSpec((N_REST, tm, SLAB), slab_blk), pl.BlockSpec((tm, D), row_blk),
        pl.BlockSpec((tm, SLAB), row_blk), pl.BlockSpec((2, tm, SLAB), slab_blk),
        pl.BlockSpec((tm, D), row_blk), pl.BlockSpec((tm, D), row_blk), pl.BlockSpec((tm, D), row_blk),
        pl.BlockSpec((tm, D), row_blk), pl.BlockSpec((tm, SLAB), row_blk), pl.BlockSpec((tm, D), row_blk),
        pl.BlockSpec((N_REST, 8, SLAB), const3), pl.BlockSpec((7, 8, D), const3),
        pl.BlockSpec((1, 8, D), lambda i: (tile(i) // tps, 0, 0)))
    return _pcall(body, grid=(nt,), out_shape=out_shape, in_specs=in_specs, out_specs=out_specs,
                  scratch_shapes=[pltpu.VMEM((8, D), F32)], name="mid",
                  compiler_params=_params(56, ("arbitrary",)))(
        rest, rest, *ols, x, tgt, ada, cw, b_out, ln_g, ln_b, w_pa_t, w_pb, w_out)


def _tn_matmul(lhs, rhs, lhs_spec, n_steps, out_rows, out_index, name, chunk=512):
    t, n = rhs.shape

    def body(l_ref, r_ref, o_ref):
        def rows(r, acc):
            sl = pl.ds(pl.multiple_of(r * chunk, chunk), chunk)
            lv = l_ref[0, sl, :] if len(l_ref.shape) == 3 else l_ref[sl, :]
            return acc + _tn(lv, r_ref[sl, :])
        o_ref[...] = lax.fori_loop(0, t // chunk, rows, jnp.zeros(o_ref.shape, F32))

    return _pcall(body, grid=(n_steps,), out_shape=jax.ShapeDtypeStruct((out_rows, n), F32),
                  in_specs=[lhs_spec, pl.BlockSpec((t, n), lambda j: (0, 0))],
                  out_specs=pl.BlockSpec((SLAB, n), out_index), name=name,
                  compiler_params=_params(48, ("parallel",)))(lhs, rhs)


def _grad_rows_2d(lhs, rhs, name):
    t, k = lhs.shape
    return _tn_matmul(lhs, rhs, pl.BlockSpec((t, SLAB), lambda j: (0, j)), k // SLAB, k, lambda j: (j, 0), name)


def _w_row_block(j):
    return (j + N_QKV) % N_SLAB


def _dp_slab(j):
    return jnp.where(j < N_REST, j, j + 2)


def _grad_w_in_t(dproj, h):
    t = h.shape[0]
    return _tn_matmul(dproj, h, pl.BlockSpec((1, t, SLAB), lambda j: (_dp_slab(j), 0, 0)), N_SLAB, D_IN,
                      lambda j: (_w_row_block(j), 0), "grad_w_in")


def _grad_h(dproj, w_in_t, gx0, x, ada, tm=1024):
    t = x.shape[0]
    nbat = ada.shape[0]
    tps = (t // nbat) // tm

    def body(dp_ref, w_ref, gx0_ref, x_ref, ada_ref, gx_ref, dss_ref, acc_ref):
        i, j = pl.program_id(0), pl.program_id(1)

        @pl.when(j == 0)
        def _():
            acc_ref[...] = jnp.zeros_like(acc_ref)

        acc_ref[...] += _nn(dp_ref[0], w_ref[...])

        @pl.when(j == N_SLAB - 1)
        def _():
            dh = acc_ref[...]
            gx_ref[...] = gx0_ref[...] + dh * (1.0 + ada_ref[0, 1:2, :])
            @pl.when((i % tps) == 0)
            def _():
                dss_ref[...] = jnp.zeros_like(dss_ref)

            dss_ref[0, 0] += _part8(dh)
            dss_ref[0, 1] += _part8(dh * x_ref[...])

    return _pcall(
        body, grid=(t // tm, N_SLAB),
        out_shape=(jax.ShapeDtypeStruct((t, D), F32), jax.ShapeDtypeStruct((nbat, 2, 8, D), F32)),
        in_specs=[pl.BlockSpec((1, tm, SLAB), lambda i, j: (_dp_slab(j), i, 0)),
                  pl.BlockSpec((SLAB, D), lambda i, j: (_w_row_block(j), 0)),
                  pl.BlockSpec((tm, D), lambda i, j: (i, 0)), pl.BlockSpec((tm, D), lambda i, j: (i, 0)),
                  pl.BlockSpec((1, 3, D), lambda i, j: (i // tps, 0, 0))],
        out_specs=(pl.BlockSpec((tm, D), lambda i, j: (i, 0)),
                   pl.BlockSpec((1, 2, 8, D), lambda i, j: (i // tps, 0, 0, 0))),
        scratch_shapes=[pltpu.VMEM((tm, D), F32)], name="grad_h",
        compiler_params=_params(48, ("arbitrary", "arbitrary")))(dproj, w_in_t, gx0, x, ada)


def _chip(m):
    x, y, _ = _my_position()
    return (x ^ ((m >> 1) & 1), y ^ (m & 1))


def _exchange_siblings(grads):
    n = len(grads)
    rows = [g.shape[0] // N_DEV for g in grads]

    def body(*refs):
        srcs, owns, lands = refs[:n], refs[n:2 * n], refs[2 * n:3 * n]
        send_sems, recv_sems, local_sems = refs[3 * n:]
        x, y, c = _my_position()

        def block(a, m, core):
            dev = _flat(*_chip(m), core)
            return srcs[a].at[pl.ds(pl.multiple_of(dev * rows[a], 8), rows[a]), :]

        local, sends = [], []
        for a in range(n):
            for m in range(4):
                local.append(pltpu.make_async_copy(block(a, m, c), owns[a].at[m], local_sems.at[4 * a + m]))
                sends.append(pltpu.make_async_remote_copy(
                    src_ref=block(a, m, 1 - c), dst_ref=lands[a].at[m], send_sem=send_sems.at[4 * a + m],
                    recv_sem=recv_sems.at[4 * a + m], device_id=(x, y, 1 - c), device_id_type=MESH))
        for cp in sends + local:
            cp.start()
        for cp in sends:
            cp.wait()
        for cp in local:
            cp.wait()

    shapes = tuple(jax.ShapeDtypeStruct((4, r, g.shape[1]), g.dtype) for r, g in zip(rows, grads))
    res = _pcall(body, out_shape=shapes + shapes, in_specs=[ANY] * n, out_specs=(ANY,) * (2 * n), name="exchange_siblings",
                 scratch_shapes=[pltpu.SemaphoreType.DMA((4 * n,))] * 3)(*grads)
    return res[:n], res[n:]


def _exchange_chips(bufs):
    n = len(bufs)

    def body(*refs):
        srcs, lands = refs[:n], refs[n:2 * n]
        send_sems, recv_sems = refs[2 * n:]
        _, _, c = _my_position()
        copies = []
        for a in range(n):
            for m in range(1, 4):
                copies.append(pltpu.make_async_remote_copy(
                    src_ref=srcs[a].at[m - 1], dst_ref=lands[a].at[m - 1], send_sem=send_sems.at[3 * a + m - 1],
                    recv_sem=recv_sems.at[3 * a + m - 1], device_id=(*_chip(m), c), device_id_type=MESH))
        for cp in copies:
            cp.start()
        for cp in copies:
            cp.wait()

    shapes = tuple(jax.ShapeDtypeStruct(b.shape, b.dtype) for b in bufs)
    return _pcall(body, out_shape=shapes, in_specs=[ANY] * n, out_specs=(ANY,) * n, name="exchange_chips",
                  scratch_shapes=[pltpu.SemaphoreType.DMA((3 * n,))] * 2)(*bufs)


def _pair_sums(owns, lands, n_steps, name):
    n = len(owns)
    rbs = [o.shape[1] // n_steps for o in owns]

    def body(*refs):
        own_refs, land_refs, outs = refs[:n], refs[n:2 * n], refs[2 * n:]
        for a in range(n):
            outs[2 * a][...] = own_refs[a][0] + land_refs[a][0]
            for m in range(1, 4):
                outs[2 * a + 1][m - 1] = (own_refs[a][m] + land_refs[a][m]).astype(BF16)

    in_specs = [pl.BlockSpec((4, rb, o.shape[2]), lambda i: (0, i, 0)) for rb, o in zip(rbs, owns)] * 2
    out_shape, out_specs = [], []
    for rb, o in zip(rbs, owns):
        out_shape += [jax.ShapeDtypeStruct(o.shape[1:], F32), jax.ShapeDtypeStruct((3,) + o.shape[1:], BF16)]
        out_specs += [pl.BlockSpec((rb, o.shape[2]), lambda i: (i, 0)), pl.BlockSpec((3, rb, o.shape[2]), lambda i: (0, i, 0))]
    res = _pcall(body, grid=(n_steps,), out_shape=tuple(out_shape), in_specs=in_specs, out_specs=tuple(out_specs), name=name,
                 compiler_params=_params(48, ("parallel",)))(*owns, *lands)
    return res[0::2], res[1::2]


def _final_sums(mine, lands, n_steps, name):
    n = len(mine)
    rbs = [o.shape[0] // n_steps for o in mine]

    def body(*refs):
        mine_refs, land_refs, outs = refs[:n], refs[n:2 * n], refs[2 * n:]
        for a in range(n):
            tot = mine_refs[a][...]
            for m in range(3):
                tot = tot + land_refs[a][m].astype(F32)
            outs[a][...] = tot

    in_specs = ([pl.BlockSpec((rb, o.shape[1]), lambda i: (i, 0)) for rb, o in zip(rbs, mine)]
                + [pl.BlockSpec((3, rb, o.shape[1]), lambda i: (0, i, 0)) for rb, o in zip(rbs, mine)])
    out_specs = tuple(pl.BlockSpec((rb, o.shape[1]), lambda i: (i, 0)) for rb, o in zip(rbs, mine))
    out_shape = tuple(jax.ShapeDtypeStruct(o.shape, F32) for o in mine)
    return _pcall(body, grid=(n_steps,), out_shape=out_shape, in_specs=in_specs, out_specs=out_specs, name=name,
                  compiler_params=_params(32, ("parallel",)))(*mine, *lands)


def _reduce_scatter(grads):
    owns, lands = _exchange_siblings(grads)
    big_mine, big_send = _pair_sums(owns[:1], lands[:1], 4, "pair_sums_w_in")
    small_mine, small_send = _pair_sums(owns[1:], lands[1:], 1, "pair_sums_rest")
    got = _exchange_chips(list(big_send) + list(small_send))
    big = _final_sums(big_mine, got[:1], 4, "final_sums_w_in")
    small = _final_sums(small_mine, got[1:], 1, "final_sums_rest")
    return list(big) + list(small)


def _adamw(w, g, m, v):
    m_new = B1 * m + (1.0 - B1) * g
    v_new = B2 * v + (1.0 - B2) * (g * g)
    m_hat = m_new / (1.0 - B1 ** STEP)
    v_hat = v_new / (1.0 - B2 ** STEP)
    delta = -LR * (m_hat / (jnp.sqrt(v_hat) + EPS) + WD * w)
    return delta, m_new, v_new


def _adam_transposed(g_t, w, m, v, name):
    n = g_t.shape[0]
    full, tail = n // 128, n % 128

    def body(gt_ref, w_ref, m_ref, v_ref, g_ref, d_ref, mo_ref, vo_ref):
        def update(g, sl):
            delta, m_new, v_new = _adamw(w_ref[:, sl], g, m_ref[:, sl], v_ref[:, sl])
            g_ref[:, sl], d_ref[:, sl], mo_ref[:, sl], vo_ref[:, sl] = g, delta, m_new, v_new

        for a in range(full):
            update(gt_ref[pl.ds(128 * a, 128), :].T, pl.ds(128 * a, 128))
        if tail:
            update(gt_ref[pl.ds(n - 128, 128), :].T[:, 128 - tail:], pl.ds(128 * full, tail))

    shape = jax.ShapeDtypeStruct(w.shape, F32)
    return _pcall(body, out_shape=(shape,) * 4, in_specs=[VMEM] * 4, out_specs=(VMEM,) * 4, name=name,
                  compiler_params=_params(56))(g_t, w, m, v)


def _adam_many(items, name):
    n = len(items)

    def body(*refs):
        ins, outs = refs[:4 * n], refs[4 * n:]
        for a in range(n):
            w_ref, g_ref, m_ref, v_ref = ins[4 * a:4 * a + 4]
            delta, m_new, v_new = _adamw(w_ref[...], g_ref[...], m_ref[...], v_ref[...])
            outs[3 * a][...], outs[3 * a + 1][...], outs[3 * a + 2][...] = delta, m_new, v_new

    out_shape = tuple(jax.ShapeDtypeStruct(it[0].shape, F32) for it in items for _ in range(3))
    flat = [arr for it in items for arr in it]
    res = _pcall(body, out_shape=out_shape, in_specs=[VMEM] * (4 * n), out_specs=(VMEM,) * (3 * n), name=name,
                 compiler_params=_params(32))(*flat)
    return [tuple(res[3 * a:3 * a + 3]) for a in range(n)]


def _adam_w_ada(cact_all, dada_mine, w, m, v):
    def body(c_ref, d_ref, w_ref, m_ref, v_ref, g_ref, dl_ref, mo_ref, vo_ref):
        g = _tn(c_ref[...].astype(BF16), d_ref[...].astype(BF16))
        delta, m_new, v_new = _adamw(w_ref[...], g, m_ref[...], v_ref[...])
        g_ref[...], dl_ref[...], mo_ref[...], vo_ref[...] = g, delta, m_new, v_new

    shape = jax.ShapeDtypeStruct(w.shape, F32)
    return _pcall(body, out_shape=(shape,) * 4, in_specs=[VMEM] * 5, out_specs=(VMEM,) * 4, name="adam_w_ada",
                  compiler_params=_params(32))(cact_all, dada_mine, w, m, v)


def kernel(x, c, w_ada, b_ada, w_in, b_in, conv_w, w_proj_attn, w_proj_conv, w_out, b_out, ln_g, ln_b, loss_target, m_w_ada, m_b_ada, m_w_in, m_b_in, m_conv_w, m_w_proj_attn, m_w_proj_conv, m_w_out, m_b_out, m_ln_g, m_ln_b, v_w_ada, v_b_ada, v_w_in, v_b_in, v_conv_w, v_w_proj_attn, v_w_proj_conv, v_w_out, v_b_out, v_ln_g, v_ln_b):
    nbat, seq, _ = x.shape
    t = nbat * seq
    me = _flat(*_my_position())
    x2, tgt2 = x.reshape(t, D), loss_target.reshape(t, D)
    sq = lambda a: a.reshape(a.shape[1:])

    w_in_t_s, w_pa_t_s, w_pb_s, w_out_s, cact_s, cw_s = _prep(sq(w_in), sq(w_proj_attn), sq(w_proj_conv), sq(w_out), c, sq(conv_w))
    w_in_t, w_pa_t, w_pb, w_o, cact_g, cw_g = _gather_rows([w_in_t_s, w_pa_t_s, w_pb_s, w_out_s, cact_s, cw_s])
    cact_all = cact_g.reshape(N_DEV, 8, D)[:, :nbat].reshape(N_DEV * nbat, D)
    cw = cw_g.reshape(N_DEV, 8, -1)[:, :3].transpose(1, 0, 2).reshape(3, D)

    ncol = w_ada.shape[2]
    b_ada_mine = lax.dynamic_slice(b_ada, (0, me * ncol), (1, ncol))
    ada_slots = _ada_forward(cact_all, sq(w_ada), b_ada_mine)
    ada_all = ada_slots.transpose(1, 0, 2).reshape(N_DEV * nbat, 3, D)
    ada = lax.dynamic_slice(ada_all, (me * nbat, 0, 0), (nbat, 3, D))

    h = _make_h(x2, ada)
    qkv, rest = _project(h, w_in_t, b_in.reshape(N_SLAB, 1, SLAB))
    ols = [_attn_forward(qkv, g, nbat) for g in range(3)]
    (dproj, gx0, do_attn, ol_tot, merged, do_f, bbs, dyc, a_bf, dya, gb_rest, svec, dgate) = _mid(
        rest, ols, x2, tgt2, ada, cw, b_out, ln_g, ln_b, w_pa_t, w_pb, w_o)

    gb_qkv = []
    for g in range(3):
        dproj, gb = _attn_backward(qkv, do_attn, ol_tot, dproj, g, nbat)
        gb_qkv.append(gb)
    grad_x, dss = _grad_h(dproj, w_in_t, gx0, x2, ada)
    g_w_in_t = _grad_w_in_t(dproj, h)
    g_w_out = _grad_rows_2d(merged, do_f, "grad_w_out")
    g_w_pb = _grad_rows_2d(bbs, dyc, "grad_w_proj_conv")
    g_w_pa_t = _grad_rows_2d(dya, a_bf, "grad_w_proj_attn")

    rows8, tot, g_bada = _small_reduce(gb_rest, gb_qkv, svec, dgate, dss)
    loss = tot[0, P_LOSS]
    dada_all = rows8[:, 0, P_DADA:].reshape(N_DEV * nbat, 3 * D)
    dada_mine = lax.dynamic_slice(dada_all, (0, me * ncol), (N_DEV * nbat, ncol))

    g_in_t, g_pa_t, g_pb, g_out = _reduce_scatter([g_w_in_t, g_w_pa_t, g_w_pb, g_w_out])

    g_win, d_win, nm_win, nv_win = _adam_transposed(g_in_t, sq(w_in), sq(m_w_in), sq(v_w_in), "adam_w_in")
    g_wpa, d_wpa, nm_wpa, nv_wpa = _adam_transposed(g_pa_t, sq(w_proj_attn), sq(m_w_proj_attn), sq(v_w_proj_attn), "adam_w_proj_attn")
    g_wada, d_wada, nm_wada, nv_wada = _adam_w_ada(cact_all, dada_mine, sq(w_ada), sq(m_w_ada), sq(v_w_ada))
    g_bin = tot[:, P_BIN:P_BIN + D_IN]
    g_bout = tot[:, P_BOUT:P_BOUT + D]
    g_lng = tot[:, P_LNG:P_LNG + D]
    g_lnb = tot[:, P_LNB:P_LNB + D]
    g_conv = lax.dynamic_slice(tot[:, P_CONV:P_CONV + 3 * D].reshape(3, D), (0, me * cw_s.shape[1]), (3, cw_s.shape[1]))
    upd = _adam_many([
        (sq(w_proj_conv), g_pb, sq(m_w_proj_conv), sq(v_w_proj_conv)),
        (sq(w_out), g_out, sq(m_w_out), sq(v_w_out)),
        (b_ada, g_bada, m_b_ada, v_b_ada), (b_in, g_bin, m_b_in, v_b_in), (sq(conv_w), g_conv, sq(m_conv_w), sq(v_conv_w)),
        (b_out, g_bout, m_b_out, v_b_out), (ln_g, g_lng, m_ln_g, v_ln_g), (ln_b, g_lnb, m_ln_b, v_ln_b)], "adam_rest")
    (d_wpb, nm_wpb, nv_wpb), (d_wout, nm_wout, nv_wout), (d_bada, nm_bada, nv_bada), (d_bin, nm_bin, nv_bin), \
        (d_conv, nm_conv, nv_conv), (d_bout, nm_bout, nv_bout), (d_lng, nm_lng, nv_lng), (d_lnb, nm_lnb, nv_lnb) = upd

    ex = lambda a: a.reshape((1,) + a.shape)
    grads = [ex(g_wada), g_bada, ex(g_win), g_bin, ex(g_conv), ex(g_wpa), ex(g_pb), ex(g_out), g_bout, g_lng, g_lnb]
    deltas = [ex(d_wada), d_bada, ex(d_win), d_bin, ex(d_conv), ex(d_wpa), ex(d_wpb), ex(d_wout), d_bout, d_lng, d_lnb]
    new_m = [ex(nm_wada), nm_bada, ex(nm_win), nm_bin, ex(nm_conv), ex(nm_wpa), ex(nm_wpb), ex(nm_wout), nm_bout, nm_lng, nm_lnb]
    new_v = [ex(nv_wada), nv_bada, ex(nv_win), nv_bin, ex(nv_conv), ex(nv_wpa), ex(nv_wpb), ex(nv_wout), nv_bout, nv_lng, nv_lnb]
    return (loss, grad_x.reshape(x.shape), *grads, *deltas, *new_m, *new_v)
```

```python
import functools

import jax
import jax.numpy as jnp
from jax import lax
from jax.experimental import pallas as pl
from jax.experimental.pallas import tpu as pltpu

F32, BF16 = jnp.float32, jnp.bfloat16
MESH = pl.DeviceIdType.MESH
N_DEV = 8
D = 1024
SLAB = 256
N_QKV, N_REST = 9, 25
N_SLAB = N_QKV + N_REST
D_IN = N_SLAB * SLAB
DP_SLABS = 36
BLK = 128
GROUPS = ((128, 1), (512, 4), (2048, 16))
ALPHA = 2.0 ** 0.25
LN_EPS = 1e-5
LR, B1, B2, EPS, WD, STEP = 0.001, 0.9, 0.999, 1e-08, 0.01, 10
R_ZA, R_UX, R_GB, R_GC, R_ZC, R_GA, R_GBM = 0, 1, 5, 9, 13, 17, 21
P_BIN, P_BOUT, P_LNG, P_LNB, P_CONV, P_LOSS, P_DADA = 0, 8704, 9728, 10752, 11776, 14848, 14976
MIB = 1024 * 1024

_pcall = pl.pallas_call
ANY = pl.BlockSpec(memory_space=pl.ANY)
VMEM = pl.BlockSpec(memory_space=pltpu.VMEM)


def _params(vmem_mib=None, sem=None):
    kw = {}
    if vmem_mib is not None:
        kw["vmem_limit_bytes"] = vmem_mib * MIB
    if sem is not None:
        kw["dimension_semantics"] = sem
    return pltpu.CompilerParams(**kw)


def _nn(a, b):
    return jnp.dot(a, b, preferred_element_type=F32)


def _nt(a, b):
    return lax.dot_general(a, b, (((1,), (1,)), ((), ())), preferred_element_type=F32)


def _tn(a, b):
    return lax.dot_general(a, b, (((0,), (0,)), ((), ())), preferred_element_type=F32)


def _sigmoid(v):
    return 1.0 / (1.0 + jnp.exp(-v))


def _part8(v):
    return v.reshape(v.shape[0] // 8, 8, v.shape[1]).sum(axis=0)


def _my_position():
    return lax.axis_index("x"), lax.axis_index("y"), lax.axis_index("c")


def _flat(px, py, pc):
    return 4 * px + 2 * py + pc


def _peer(mask):
    x, y, c = _my_position()
    return (x ^ ((mask >> 2) & 1), y ^ ((mask >> 1) & 1), c ^ (mask & 1))


def _prep(w_in, w_pa, w_pb, w_out, c, conv_w):
    n_in = w_in.shape[1]
    full, tail = n_in // 128, n_in % 128

    def body(win_ref, wpa_ref, wpb_ref, wout_ref, c_ref, cw_ref, wint_ref, wpat_ref, wpb_o, wout_o, cact_ref, cwp_ref):
        for a in range(full):
            wint_ref[pl.ds(128 * a, 128), :] = win_ref[:, pl.ds(128 * a, 128)].T.astype(BF16)
        if tail:
            t = win_ref[:, pl.ds(n_in - 128, 128)].T
            wint_ref[pl.ds(128 * full, tail), :] = t[128 - tail:].astype(BF16)
        wpat_ref[...] = wpa_ref[...].T.astype(BF16)
        wpb_o[...] = wpb_ref[...].astype(BF16)
        wout_o[...] = wout_ref[...].astype(BF16)
        cv = c_ref[...]
        cact_ref[...] = jnp.zeros_like(cact_ref)
        cact_ref[pl.ds(0, cv.shape[0]), :] = cv * _sigmoid(cv)
        cwp_ref[...] = jnp.zeros_like(cwp_ref)
        cwp_ref[pl.ds(0, 3), :] = cw_ref[...]

    out_shape = (jax.ShapeDtypeStruct((n_in, D), BF16), jax.ShapeDtypeStruct((w_pa.shape[1], w_pa.shape[0]), BF16),
                 jax.ShapeDtypeStruct(w_pb.shape, BF16), jax.ShapeDtypeStruct(w_out.shape, BF16),
                 jax.ShapeDtypeStruct((8, D), F32), jax.ShapeDtypeStruct((8, conv_w.shape[1]), F32))
    return _pcall(body, out_shape=out_shape, in_specs=[VMEM] * 6, out_specs=(VMEM,) * 6, name="prep",
                  compiler_params=_params(32))(w_in, w_pa, w_pb, w_out, c, conv_w)


def _gather_rows(shards):
    n = len(shards)

    def body(*refs):
        srcs, outs = refs[:n], refs[n:2 * n]
        send_sems, recv_sems, local_sems = refs[2 * n:]
        x, y, c = _my_position()
        me, sibling = (x, y, c), (x, y, 1 - c)
        chips = [(1 - x, y), (x, 1 - y), (1 - x, 1 - y)]

        def rows(a, px, py, pc):
            r = shards[a].shape[0]
            return outs[a].at[pl.ds(pl.multiple_of(_flat(px, py, pc) * r, r), r), :]

        def copy(a, k, block, to, src=None):
            return pltpu.make_async_remote_copy(
                src_ref=rows(a, *block) if src is None else src, dst_ref=rows(a, *block),
                send_sem=send_sems.at[7 * a + k], recv_sem=recv_sems.at[7 * a + k], device_id=to, device_id_type=MESH)

        mine = [pltpu.make_async_copy(srcs[a], rows(a, *me), local_sems.at[a]) for a in range(n)]
        for cp in mine:
            cp.start()
        first = []
        for a in range(n):
            first.append(copy(a, 0, me, sibling, src=srcs[a]))
            first += [copy(a, 1 + j, me, (*chip, c), src=srcs[a]) for j, chip in enumerate(chips)]
        for cp in first:
            cp.start()
        passed = []
        for j, chip in enumerate(chips):
            for a in range(n):
                copy(a, 1 + j, (*chip, c), me).wait_recv()
                cp = copy(a, 4 + j, (*chip, c), sibling)
                cp.start()
                passed.append(cp)
        for a in range(n):
            copy(a, 0, sibling, me).wait_recv()
        for j, chip in enumerate(chips):
            for a in range(n):
                copy(a, 4 + j, (*chip, 1 - c), me).wait_recv()
        for cp in first + passed:
            cp.wait_send()
        for cp in mine:
            cp.wait()

    out_shape = tuple(jax.ShapeDtypeStruct((N_DEV * s.shape[0], s.shape[1]), s.dtype) for s in shards)
    return _pcall(body, out_shape=out_shape, in_specs=[ANY] * n, out_specs=(ANY,) * n, name="gather_rows",
                  scratch_shapes=[pltpu.SemaphoreType.DMA((7 * n,)), pltpu.SemaphoreType.DMA((7 * n,)),
                                  pltpu.SemaphoreType.DMA((n,))])(*shards)


def _exchange_slots(out_ref, send_sems, recv_sems):
    me = _flat(*_my_position())
    copies = []
    for mask in range(1, N_DEV):
        peer = _peer(mask)
        copies.append((mask, pltpu.make_async_remote_copy(
            src_ref=out_ref.at[me], dst_ref=out_ref.at[me], send_sem=send_sems.at[mask - 1],
            recv_sem=recv_sems.at[mask - 1], device_id=peer, device_id_type=MESH)))
    for _, cp in copies:
        cp.start()
    for mask, _ in copies:
        peer = _peer(mask)
        pltpu.make_async_remote_copy(
            src_ref=out_ref.at[_flat(*peer)], dst_ref=out_ref.at[_flat(*peer)], send_sem=send_sems.at[mask - 1],
            recv_sem=recv_sems.at[mask - 1], device_id=peer, device_id_type=MESH).wait_recv()
    for _, cp in copies:
        cp.wait_send()


def _ada_forward(cact_all, w_ada, b_ada_mine):
    nb, ncol = cact_all.shape[0], w_ada.shape[1]

    def body(c_ref, w_ref, b_ref, out_ref, send_sems, recv_sems):
        me = _flat(*_my_position())
        out_ref[me] = _nn(c_ref[...].astype(BF16), w_ref[...].astype(BF16)) + b_ref[...]
        _exchange_slots(out_ref, send_sems, recv_sems)

    return _pcall(body, out_shape=jax.ShapeDtypeStruct((N_DEV, nb, ncol), F32), in_specs=[VMEM] * 3, out_specs=VMEM,
                  scratch_shapes=[pltpu.SemaphoreType.DMA((7,)), pltpu.SemaphoreType.DMA((7,))], name="ada_forward",
                  compiler_params=_params(16))(cact_all, w_ada, b_ada_mine)


def _small_reduce(gb_rest, gb_qkv, svec, dgate, dss):
    nbat = dgate.shape[0]

    def body(gbr_ref, q0_ref, q1_ref, q2_ref, sv_ref, dg_ref, dss_ref, rows_ref, tot_ref, gbada_ref, send_sems, recv_sems):
        me = _flat(*_my_position())

        def put(off, v):
            rows_ref[me, :, pl.ds(off, v.shape[1])] = v

        def row(v):
            return jnp.sum(v, axis=0, keepdims=True)

        for g, q_ref in enumerate((q0_ref, q1_ref, q2_ref)):
            for which in range(3):
                put(P_BIN + SLAB * (3 * which + g), row(q_ref[which]))
        for s in range(N_REST):
            put(P_BIN + SLAB * (N_QKV + s), row(gbr_ref[s]))
        put(P_LNG, row(sv_ref[0]))
        put(P_LNB, row(sv_ref[1]))
        put(P_BOUT, row(sv_ref[2]))
        for j in range(3):
            put(P_CONV + D * j, row(sv_ref[3 + j]))
        loss = (0.5 / D) * jnp.sum(row(sv_ref[6]), axis=1, keepdims=True)
        put(P_LOSS, jnp.broadcast_to(loss, (1, 128)))
        for b in range(nbat):
            put(P_DADA + 3 * D * b, row(dss_ref[b, 0]))
            put(P_DADA + 3 * D * b + D, row(dss_ref[b, 1]))
            put(P_DADA + 3 * D * b + 2 * D, row(dg_ref[b]))
        _exchange_slots(rows_ref, send_sems, recv_sems)
        tot = rows_ref[0]
        for k in range(1, N_DEV):
            tot = tot + rows_ref[k]
        tot_ref[...] = tot
        gbada = tot[:, P_DADA:P_DADA + 3 * D]
        for b in range(1, nbat):
            gbada = gbada + tot[:, P_DADA + 3 * D * b:P_DADA + 3 * D * (b + 1)]
        gbada_ref[...] = gbada

    p_len = P_DADA + nbat * 3 * D
    out_shape = (jax.ShapeDtypeStruct((N_DEV, 1, p_len), F32), jax.ShapeDtypeStruct((1, p_len), F32),
                 jax.ShapeDtypeStruct((1, 3 * D), F32))
    return _pcall(body, out_shape=out_shape, in_specs=[VMEM] * 7, out_specs=(VMEM, VMEM, VMEM),
                  scratch_shapes=[pltpu.SemaphoreType.DMA((7,)), pltpu.SemaphoreType.DMA((7,))], name="small_reduce",
                  compiler_params=_params(16))(gb_rest, *gb_qkv, svec, dgate, dss)


def _make_h(x, ada, tm=512):
    t = x.shape[0]
    tps = (t // ada.shape[0]) // tm

    def body(x_ref, ada_ref, h_ref):
        h_ref[...] = (x_ref[...] * (1.0 + ada_ref[0, 1:2, :]) + ada_ref[0, 0:1, :]).astype(BF16)

    return _pcall(body, grid=(t // tm,), out_shape=jax.ShapeDtypeStruct((t, D), BF16),
                  in_specs=[pl.BlockSpec((tm, D), lambda i: (i, 0)), pl.BlockSpec((1, 3, D), lambda i: (i // tps, 0, 0))],
                  out_specs=pl.BlockSpec((tm, D), lambda i: (i, 0)), name="make_h",
                  compiler_params=_params(32, ("parallel",)))(x, ada)


def _project(h, w_in_t, b_in3):
    t = h.shape[0]

    def body(h_ref, w_ref, b_ref, qkv_ref, rest_ref):
        j = pl.program_id(0)
        v = _nt(h_ref[...], w_ref[...]) + b_ref[0]

        @pl.when(j < N_QKV)
        def _():
            qkv_ref[0] = v.astype(BF16)

        @pl.when(j >= N_QKV)
        def _():
            rest_ref[0] = v

    return _pcall(
        body, grid=(N_SLAB,),
        out_shape=(jax.ShapeDtypeStruct((N_QKV, t, SLAB), BF16), jax.ShapeDtypeStruct((N_REST, t, SLAB), F32)),
        in_specs=[pl.BlockSpec((t, D), lambda j: (0, 0)), pl.BlockSpec((SLAB, D), lambda j: (j, 0)),
                  pl.BlockSpec((1, 1, SLAB), lambda j: (j, 0, 0))],
        out_specs=(pl.BlockSpec((1, t, SLAB), lambda j: (jnp.minimum(j, N_QKV - 1), 0, 0)),
                   pl.BlockSpec((1, t, SLAB), lambda j: (jnp.maximum(j - N_QKV, 0), 0, 0))),
        name="project", compiler_params=_params(48, ("arbitrary",)))(h, w_in_t, b_in3)


def _bias_tables(g):
    window, dil = GROUPS[g]
    span = window // dil
    qi = jnp.arange(BLK)[:, None]
    kj = jnp.arange(2 * BLK)[None, :]
    delta = qi + BLK - kj
    valid = (delta >= 0) & (delta <= span)
    heads = jnp.arange(4, dtype=F32) + 4.0 * g
    slopes = 2.0 ** (-8.0 * (heads + 1.0) / 12.0)
    bias = -slopes[:, None, None] * (delta * dil).astype(F32)[None]
    return jnp.where(valid[None], bias, -1e30).reshape(4 * BLK, 2 * BLK)


def _head_masks(shape):
    lane = lax.broadcasted_iota(jnp.int32, shape, 1)
    return [(lane >= 64 * h) & (lane < 64 * (h + 1)) for h in range(4)]


def _stack_heads(v, masks):
    return jnp.concatenate([jnp.where(masks[h], v, jnp.zeros_like(v)) for h in range(4)], axis=0)


def _unstack_heads(v4, masks):
    out = jnp.where(masks[0], v4[0:BLK], 0.0)
    for h in range(1, 4):
        out = jnp.where(masks[h], v4[BLK * h:BLK * (h + 1)], out)
    return out


def _regroup(load_half, dst_ref, stage_ref, n, dil):
    for hlf in range(2):
        stage_ref[hlf] = load_half(hlf)

    def residue(r, carry):
        for hlf in range(2):
            dst_ref[pl.ds(pl.multiple_of(r * n, BLK), n), pl.ds(128 * hlf, 128)] = (
                stage_ref[hlf, pl.ds(r, n, stride=dil), :].astype(dst_ref.dtype))
        return carry

    lax.fori_loop(0, dil, residue, 0)


def _store_block(nat_ref, r, i, val, dil):
    for hlf in range(2):
        nat_ref[hlf, pl.ds(r + dil * BLK * i, BLK, stride=dil), :] = val[:, 128 * hlf:128 * (hlf + 1)]


def _for_blocks(block, dil, nblk):
    def residue(r, carry):
        block(r, 0, True)
        if nblk > 1:
            def loop(i, c):
                block(r, i, False)
                return c
            lax.fori_loop(1, nblk, loop, 0)
        return carry

    if dil == 1:
        residue(0, 0)
    else:
        lax.fori_loop(0, dil, residue, 0)


def _attn_forward(qkv, g, nbat):
    t = qkv.shape[1]
    seq = t // nbat
    dil = GROUPS[g][1]
    n = seq // dil
    nblk = n // BLK
    qkv4 = qkv.reshape(3, 3, t, SLAB)

    def body(qkv_ref, bias_ref, ol_ref, *scratch):
        masks = _head_masks((BLK, SLAB))
        if dil > 1:
            stage, qd, kd, vd, nat_o, nat_l = scratch
            for which, dst in enumerate((qd, kd, vd)):
                _regroup(lambda hlf, which=which: qkv_ref[which, 0, :, pl.ds(128 * hlf, 128)].astype(F32), dst, stage, n, dil)
        else:
            qd, kd, vd = qkv_ref.at[0, 0], qkv_ref.at[1, 0], qkv_ref.at[2, 0]

        def block(r, i, first):
            base = r * n
            qs = pl.ds(pl.multiple_of(base + i * BLK, BLK), BLK)
            ks = pl.ds(pl.multiple_of(base, BLK), BLK) if first else pl.ds(pl.multiple_of(base + (i - 1) * BLK, BLK), 2 * BLK)
            q, kk, vv = qd[qs, :], kd[ks, :], vd[ks, :]
            bias = bias_ref[:, pl.ds(BLK, BLK)] if first else bias_ref[...]
            s = _nt(_stack_heads(q, masks), kk) * 0.125 + bias
            m = jnp.max(s, axis=1, keepdims=True)
            p = jnp.exp(s - m)
            den = jnp.sum(p, axis=1, keepdims=True)
            out = _unstack_heads(_nn((p * (1.0 / den)).astype(BF16), vv), masks)
            lse = _unstack_heads(jnp.broadcast_to(m + jnp.log(den), (4 * BLK, SLAB)), masks)
            if dil > 1:
                _store_block(nat_o, r, i, out, dil)
                _store_block(nat_l, r, i, lse, dil)
            else:
                ol_ref[0, qs, :] = out
                ol_ref[1, qs, :] = lse

        _for_blocks(block, dil, nblk)
        if dil > 1:
            for hlf in range(2):
                ol_ref[0, :, pl.ds(128 * hlf, 128)] = nat_o[hlf]
                ol_ref[1, :, pl.ds(128 * hlf, 128)] = nat_l[hlf]

    scratch = []
    if dil > 1:
        scratch = [pltpu.VMEM((2, seq, 128), F32)] + [pltpu.VMEM((seq, SLAB), BF16)] * 3 + [pltpu.VMEM((2, seq, 128), F32)] * 2
    return _pcall(
        body, grid=(nbat,), out_shape=jax.ShapeDtypeStruct((2, t, SLAB), F32),
        in_specs=[pl.BlockSpec((3, 1, seq, SLAB), lambda b: (0, g, b, 0)),
                  pl.BlockSpec((4 * BLK, 2 * BLK), lambda b: (0, 0))],
        out_specs=pl.BlockSpec((2, seq, SLAB), lambda b: (0, b, 0)), scratch_shapes=scratch,
        name=f"attn_forward_{g}", compiler_params=_params(40, ("parallel",)))(qkv4, _bias_tables(g))


def _attn_backward(qkv, do_attn, ol_tot, dproj, g, nbat):
    t = qkv.shape[1]
    seq = t // nbat
    dil = GROUPS[g][1]
    n = seq // dil
    nblk = n // BLK
    qkv4 = qkv.reshape(3, 3, t, SLAB)
    dp4 = dproj.reshape(DP_SLABS // 3, 3, t, SLAB)

    def body(qkv_ref, do_ref, ol_ref, bias_ref, dp_in, dp_ref, gb_ref, dk_acc, dv_acc, *scratch):
        del dp_in
        masks = _head_masks((BLK, SLAB))

        @pl.when(pl.program_id(0) == 0)
        def _():
            gb_ref[...] = jnp.zeros_like(gb_ref)

        dk_acc[...] = jnp.zeros_like(dk_acc)
        dv_acc[...] = jnp.zeros_like(dv_acc)
        if dil > 1:
            stage, qd, kd, vd, dod, prodd, lsed, nat = scratch
            lanes = lambda hlf: pl.ds(128 * hlf, 128)
            for which, dst in enumerate((qd, kd, vd)):
                _regroup(lambda hlf, which=which: qkv_ref[which, 0, :, lanes(hlf)].astype(F32), dst, stage, n, dil)
            _regroup(lambda hlf: do_ref[:, lanes(hlf)].astype(F32), dod, stage, n, dil)
            _regroup(lambda hlf: do_ref[:, lanes(hlf)].astype(F32) * ol_ref[0, :, lanes(hlf)], prodd, stage, n, dil)
            _regroup(lambda hlf: ol_ref[1, :, lanes(hlf)], lsed, stage, n, dil)
        else:
            qd, kd, vd = qkv_ref.at[0, 0], qkv_ref.at[1, 0], qkv_ref.at[2, 0]

        def block(r, i, first):
            base = r * n
            qs = pl.ds(pl.multiple_of(base + i * BLK, BLK), BLK)
            ks = pl.ds(pl.multiple_of(base, BLK), BLK) if first else pl.ds(pl.multiple_of(base + (i - 1) * BLK, BLK), 2 * BLK)
            q, kk, vv = qd[qs, :], kd[ks, :], vd[ks, :]
            if dil > 1:
                do, prod, lse = dod[qs, :], prodd[qs, :], lsed[qs, :]
            else:
                do = do_ref[qs, :]
                prod = do.astype(F32) * ol_ref[0, qs, :]
                lse = ol_ref[1, qs, :]
            q4, do4 = _stack_heads(q, masks), _stack_heads(do, masks)
            bias = bias_ref[:, pl.ds(BLK, BLK)] if first else bias_ref[...]
            lse4 = jnp.concatenate([lse[:, 64 * h:64 * h + 1] for h in range(4)], axis=0)
            delta4 = jnp.concatenate([jnp.sum(jnp.where(masks[h], prod, 0.0), axis=1, keepdims=True) for h in range(4)], axis=0)
            p = jnp.exp(_nt(q4, kk) * 0.125 + bias - lse4)
            ds = (p * (_nt(do4, vv) - delta4)).astype(BF16)
            dv_acc[ks, :] += _tn(p.astype(BF16), do4)
            dk_acc[ks, :] += _tn(ds, q4) * 0.125
            dq = _unstack_heads(_nn(ds, kk), masks) * 0.125
            if dil > 1:
                _store_block(nat, r, i, dq, dil)
            else:
                dp_ref[0, 0, qs, :] = dq.astype(BF16)
            gb_ref[0] += _part8(dq)

        _for_blocks(block, dil, nblk)
        gb_ref[1] += _part8(dk_acc[...])
        gb_ref[2] += _part8(dv_acc[...])
        if dil > 1:
            def flush(which):
                for hlf in range(2):
                    dp_ref[which, 0, :, pl.ds(128 * hlf, 128)] = nat[hlf].astype(BF16)

            def to_token_order(acc_ref):
                def residue(r, carry):
                    for hlf in range(2):
                        nat[hlf, pl.ds(r, n, stride=dil), :] = acc_ref[pl.ds(pl.multiple_of(r * n, BLK), n), pl.ds(128 * hlf, 128)]
                    return carry
                lax.fori_loop(0, dil, residue, 0)

            flush(0)
            to_token_order(dk_acc)
            flush(1)
            to_token_order(dv_acc)
            flush(2)
        else:
            dp_ref[1, 0] = dk_acc[...].astype(BF16)
            dp_ref[2, 0] = dv_acc[...].astype(BF16)

    scratch = [pltpu.VMEM((seq, SLAB), F32)] * 2
    if dil > 1:
        scratch += ([pltpu.VMEM((2, seq, 128), F32)] + [pltpu.VMEM((seq, SLAB), BF16)] * 4 + [pltpu.VMEM((seq, SLAB), F32)] * 2
                    + [pltpu.VMEM((2, seq, 128), F32)])
    dp, gb = _pcall(
        body, grid=(nbat,),
        out_shape=(jax.ShapeDtypeStruct(dp4.shape, BF16), jax.ShapeDtypeStruct((3, 8, SLAB), F32)),
        in_specs=[pl.BlockSpec((3, 1, seq, SLAB), lambda b: (0, g, b, 0)),
                  pl.BlockSpec((seq, SLAB), lambda b: (b, 0)),
                  pl.BlockSpec((2, seq, SLAB), lambda b: (0, b, 0)),
                  pl.BlockSpec((4 * BLK, 2 * BLK), lambda b: (0, 0)), ANY],
        out_specs=(pl.BlockSpec((3, 1, seq, SLAB), lambda b: (DP_SLABS // 9 - 1, g, b, 0)),
                   pl.BlockSpec((3, 8, SLAB), lambda b: (0, 0, 0))),
        scratch_shapes=scratch, input_output_aliases={4: 0}, name=f"attn_backward_{g}",
        compiler_params=_params(48, ("arbitrary",)))(qkv4, do_attn, ol_tot, _bias_tables(g), dp4)
    return dp.reshape(DP_SLABS, t, SLAB), gb


def _mid(rest, ols, x, tgt, ada, cw, b_out, ln_g, ln_b, w_pa_t, w_pb, w_out, tm=256):
    t = x.shape[0]
    nbat = ada.shape[0]
    nt = t // tm
    tps = nt // nbat

    def body(rest_ref, halo_ref, ol0_ref, ol1_ref, ol2_ref, x_ref, t_ref, ada_ref, cw_ref, bout_ref, lng_ref, lnb_ref,
             wpat_ref, wpb_ref, wout_ref,
             dp_ref, gx0_ref, doa_ref, olt_ref, mg_ref, dof_ref, bbs_ref, dyc_ref, a_ref, dya_ref,
             gbr_ref, sv_ref, dgate_ref, carry_ref):
        i = pl.program_id(0)
        ti = nt - 1 - i
        pos = ti % tps

        @pl.when(i == 0)
        def _():
            gbr_ref[...] = jnp.zeros_like(gbr_ref)
            sv_ref[...] = jnp.zeros_like(sv_ref)

        @pl.when(pos == tps - 1)
        def _():
            dgate_ref[...] = jnp.zeros_like(dgate_ref)
            carry_ref[...] = jnp.zeros_like(carry_ref)

        row = lax.broadcasted_iota(jnp.int32, (tm, SLAB), 0)
        halo_on = (pos > 0).astype(F32)

        def cols(s):
            return pl.ds(SLAB * s, SLAB)

        l0, l1, l2 = ol0_ref[1], ol1_ref[1], ol2_ref[1]
        mx = jnp.maximum(jnp.maximum(l0, l1), l2)
        e0, e1, e2 = jnp.exp(l0 - mx), jnp.exp(l1 - mx), jnp.exp(l2 - mx)
        den = e0 + e1 + e2
        o_attn = (e0 * ol0_ref[0] + e1 * ol1_ref[0] + e2 * ol2_ref[0]) * (1.0 / den)
        olt_ref[0] = o_attn
        olt_ref[1] = mx + jnp.log(den)
        z_a = rest_ref[R_ZA]
        sg_za = _sigmoid(z_a)
        a_ref[...] = (o_attn * z_a * sg_za).astype(BF16)
        y_attn = _nt(a_ref[...], wpat_ref[...])

        def conv_parts(s):
            ux, gc = rest_ref[R_UX + s], rest_ref[R_GC + s]
            u = gc * ux
            hu = halo_ref[R_GC + s] * halo_ref[R_UX + s] * halo_on
            u1 = jnp.where(row == 0, hu[7:8], pltpu.roll(u, 1, 0))
            u2 = jnp.where(row == 0, hu[6:7], jnp.where(row == 1, hu[7:8], pltpu.roll(u, 2, 0)))
            conv = cw_ref[0:1, cols(s)] * u2 + cw_ref[1:2, cols(s)] * u1 + cw_ref[2:3, cols(s)] * u
            zc = rest_ref[R_ZC + s]
            sg = _sigmoid(zc)
            return ux, gc, u, u1, u2, conv, zc, sg

        for s in range(4):
            ux, gc, u, u1, u2, conv, zc, sg = conv_parts(s)
            bbs_ref[:, cols(s)] = (rest_ref[R_GB + s] * conv * (zc * sg)).astype(BF16)
        y_conv = _nn(bbs_ref[...], wpb_ref[...])

        for s in range(4):
            s_a, s_b = _sigmoid(rest_ref[R_GA + s]), _sigmoid(rest_ref[R_GBM + s])
            mg_ref[:, cols(s)] = (s_a * y_attn[:, SLAB * s:SLAB * (s + 1)] + s_b * y_conv[:, SLAB * s:SLAB * (s + 1)]).astype(BF16)
        o = _nn(mg_ref[...], wout_ref[...]) + bout_ref[...]
        gate = ada_ref[0, 2:3, :]
        r = ALPHA * x_ref[...] + gate * o
        mu = jnp.mean(r, axis=1, keepdims=True)
        rc = r - mu
        rstd = lax.rsqrt(jnp.mean(rc * rc, axis=1, keepdims=True) + LN_EPS)
        xhat = rc * rstd
        err = xhat * lng_ref[...] + lnb_ref[...] - t_ref[...]
        sv_ref[6] += _part8(err * err)
        dy = err * (1.0 / D)
        sv_ref[0] += _part8(dy * xhat)
        sv_ref[1] += _part8(dy)
        dxh = dy * lng_ref[...]
        dr = rstd * (dxh - jnp.mean(dxh, axis=1, keepdims=True) - xhat * jnp.mean(dxh * xhat, axis=1, keepdims=True))
        gx0_ref[...] = ALPHA * dr
        dgate_ref[0] += _part8(dr * o)
        do_ = dr * gate
        sv_ref[2] += _part8(do_)
        dof_ref[...] = do_.astype(BF16)
        dmerged = _nt(dof_ref[...], wout_ref[...])
        for s in range(4):
            s_a, s_b = _sigmoid(rest_ref[R_GA + s]), _sigmoid(rest_ref[R_GBM + s])
            dm = dmerged[:, SLAB * s:SLAB * (s + 1)]
            ya, yc = y_attn[:, SLAB * s:SLAB * (s + 1)], y_conv[:, SLAB * s:SLAB * (s + 1)]
            dya_ref[:, cols(s)] = (dm * s_a).astype(BF16)
            dyc_ref[:, cols(s)] = (dm * s_b).astype(BF16)
            dga = dm * ya * s_a * (1.0 - s_a)
            dgb = dm * yc * s_b * (1.0 - s_b)
            dp_ref[R_GA + s] = dga.astype(BF16)
            dp_ref[R_GBM + s] = dgb.astype(BF16)
            gbr_ref[R_GA + s] += _part8(dga)
            gbr_ref[R_GBM + s] += _part8(dgb)

        da = _nn(dya_ref[...], wpat_ref[...])
        doa_ref[...] = (da * z_a * sg_za).astype(BF16)
        dza = da * o_attn * (sg_za * (1.0 + z_a * (1.0 - sg_za)))
        dp_ref[R_ZA] = dza.astype(BF16)
        gbr_ref[R_ZA] += _part8(dza)

        dbb = _nt(dyc_ref[...], wpb_ref[...])
        for s in range(4):
            ux, gc, u, u1, u2, conv, zc, sg = conv_parts(s)
            gb = rest_ref[R_GB + s]
            d_b = dbb[:, SLAB * s:SLAB * (s + 1)]
            szc = zc * sg
            dgb_ = d_b * conv * szc
            dconv = d_b * gb * szc
            dzc = d_b * gb * conv * (sg * (1.0 + zc * (1.0 - sg)))
            sv_ref[3, :, cols(s)] += _part8(dconv * u2)
            sv_ref[4, :, cols(s)] += _part8(dconv * u1)
            sv_ref[5, :, cols(s)] += _part8(dconv * u)
            nxt = carry_ref[:, cols(s)]
            d1 = jnp.where(row == tm - 1, nxt[0:1], pltpu.roll(dconv, tm - 1, 0))
            d2 = jnp.where(row == tm - 1, nxt[1:2], jnp.where(row == tm - 2, nxt[0:1], pltpu.roll(dconv, tm - 2, 0)))
            carry_ref[:, cols(s)] = dconv[0:8]
            du = cw_ref[2:3, cols(s)] * dconv + cw_ref[1:2, cols(s)] * d1 + cw_ref[0:1, cols(s)] * d2
            dgc, dux = du * ux, du * gc
            for slab, val in ((R_GB + s, dgb_), (R_ZC + s, dzc), (R_GC + s, dgc), (R_UX + s, dux)):
                dp_ref[slab] = val.astype(BF16)
                gbr_ref[slab] += _part8(val)

    def tile(i):
        return nt - 1 - i

    row_blk = lambda i: (tile(i), 0)
    slab_blk = lambda i: (0, tile(i), 0)
    const2 = lambda i: (0, 0)
    const3 = lambda i: (0, 0, 0)
    in_specs = [
        pl.BlockSpec((N_REST, tm, SLAB), slab_blk),
        pl.BlockSpec((N_REST, 8, SLAB), lambda i: (0, jnp.maximum(tile(i) * (tm // 8) - 1, 0), 0)),
        pl.BlockSpec((2, tm, SLAB), slab_blk), pl.BlockSpec((2, tm, SLAB), slab_blk), pl.BlockSpec((2, tm, SLAB), slab_blk),
        pl.BlockSpec((tm, D), row_blk), pl.BlockSpec((tm, D), row_blk),
        pl.BlockSpec((1, 3, D), lambda i: (tile(i) // tps, 0, 0)),
        pl.BlockSpec((3, D), const2), pl.BlockSpec((1, D), const2), pl.BlockSpec((1, D), const2), pl.BlockSpec((1, D), const2),
        pl.BlockSpec((D, SLAB), const2), pl.BlockSpec((D, D), const2), pl.BlockSpec((D, D), const2)]
    bf_rows = lambda: jax.ShapeDtypeStruct((t, D), BF16)
    out_shape = (
        jax.ShapeDtypeStruct((DP_SLABS, t, SLAB), BF16), jax.ShapeDtypeStruct((t, D), F32),
        jax.ShapeDtypeStruct((t, SLAB), BF16), jax.ShapeDtypeStruct((2, t, SLAB), F32),
        bf_rows(), bf_rows(), bf_rows(), bf_rows(), jax.ShapeDtypeStruct((t, SLAB), BF16), bf_rows(),
        jax.ShapeDtypeStruct((N_REST, 8, SLAB), F32), jax.ShapeDtypeStruct((7, 8, D), F32),
        jax.ShapeDtypeStruct((nbat, 8, D), F32))
    out_specs = (
        pl.BlockSpec((N_REST, tm, SLAB), slab_blk), pl.BlockSpec((tm, D), row_blk),
        pl.BlockSpec((tm, SLAB), row_blk), pl.BlockSpec((2, tm, SLAB), slab_blk),
        pl.BlockSpec((tm, D), row_blk), pl.BlockSpec((tm, D), row_blk), pl.BlockSpec((tm, D), row_blk),
        pl.BlockSpec((tm, D), row_blk), pl.BlockSpec((tm, SLAB), row_blk), pl.BlockSpec((tm, D), row_blk),
        pl.BlockSpec((N_REST, 8, SLAB), const3), pl.BlockSpec((7, 8, D), const3),
        pl.BlockSpec((1, 8, D), lambda i: (tile(i) // tps, 0, 0)))
    return _pcall(body, grid=(nt,), out_shape=out_shape, in_specs=in_specs, out_specs=out_specs,
                  scratch_shapes=[pltpu.VMEM((8, D), F32)], name="mid",
                  compiler_params=_params(56, ("arbitrary",)))(
        rest, rest, *ols, x, tgt, ada, cw, b_out, ln_g, ln_b, w_pa_t, w_pb, w_out)


def _tn_matmul(lhs, rhs, lhs_spec, n_steps, out_rows, out_index, name):
    t, n = rhs.shape

    def body(l_ref, r_ref, o_ref):
        o_ref[...] = _tn(l_ref[0] if len(l_ref.shape) == 3 else l_ref[...], r_ref[...])

    return _pcall(body, grid=(n_steps,), out_shape=jax.ShapeDtypeStruct((out_rows, n), F32),
                  in_specs=[lhs_spec, pl.BlockSpec((t, n), lambda j: (0, 0))],
                  out_specs=pl.BlockSpec((SLAB, n), out_index), name=name,
                  compiler_params=_params(48, ("parallel",)))(lhs, rhs)


def _grad_rows_2d(lhs, rhs, name):
    t, k = lhs.shape
    return _tn_matmul(lhs, rhs, pl.BlockSpec((t, SLAB), lambda j: (0, j)), k // SLAB, k, lambda j: (j, 0), name)


def _w_row_block(j):
    return (j + N_QKV) % N_SLAB


def _dp_slab(j):
    return jnp.where(j < N_REST, j, j + 2)


def _grad_w_in_t(dproj, h):
    t = h.shape[0]
    return _tn_matmul(dproj, h, pl.BlockSpec((1, t, SLAB), lambda j: (_dp_slab(j), 0, 0)), N_SLAB, D_IN,
                      lambda j: (_w_row_block(j), 0), "grad_w_in")


def _grad_h(dproj, w_in_t, gx0, x, ada, tm=512):
    t = x.shape[0]
    nbat = ada.shape[0]
    tps = (t // nbat) // tm

    def body(dp_ref, w_ref, gx0_ref, x_ref, ada_ref, gx_ref, dss_ref):
        i = pl.program_id(0)
        dh = None
        for j in range(N_SLAB):
            slab = j if j < N_REST else j + 2
            part = _nn(dp_ref[slab], w_ref[pl.ds(SLAB * ((j + N_QKV) % N_SLAB), SLAB), :])
            dh = part if dh is None else dh + part
        gx_ref[...] = gx0_ref[...] + dh * (1.0 + ada_ref[0, 1:2, :])

        @pl.when((i % tps) == 0)
        def _():
            dss_ref[...] = jnp.zeros_like(dss_ref)

        dss_ref[0, 0] += _part8(dh)
        dss_ref[0, 1] += _part8(dh * x_ref[...])

    return _pcall(
        body, grid=(t // tm,),
        out_shape=(jax.ShapeDtypeStruct((t, D), F32), jax.ShapeDtypeStruct((nbat, 2, 8, D), F32)),
        in_specs=[pl.BlockSpec((DP_SLABS, tm, SLAB), lambda i: (0, i, 0)),
                  pl.BlockSpec((D_IN, D), lambda i: (0, 0), pipeline_mode=pl.Buffered(1)),
                  pl.BlockSpec((tm, D), lambda i: (i, 0)), pl.BlockSpec((tm, D), lambda i: (i, 0)),
                  pl.BlockSpec((1, 3, D), lambda i: (i // tps, 0, 0))],
        out_specs=(pl.BlockSpec((tm, D), lambda i: (i, 0)),
                   pl.BlockSpec((1, 2, 8, D), lambda i: (i // tps, 0, 0, 0))),
        name="grad_h", compiler_params=_params(60, ("arbitrary",)))(dproj, w_in_t, gx0, x, ada)


def _chip(m):
    x, y, _ = _my_position()
    return (x ^ ((m >> 1) & 1), y ^ (m & 1))


def _exchange_siblings(grads):
    n = len(grads)
    rows = [g.shape[0] // N_DEV for g in grads]

    def body(*refs):
        srcs, lands = refs[:n], refs[n:2 * n]
        send_sems, recv_sems = refs[2 * n:]
        x, y, c = _my_position()
        sends = []
        for a in range(n):
            for m in range(4):
                dev = _flat(*_chip(m), 1 - c)
                sends.append(pltpu.make_async_remote_copy(
                    src_ref=srcs[a].at[pl.ds(pl.multiple_of(dev * rows[a], 8), rows[a]), :], dst_ref=lands[a].at[m],
                    send_sem=send_sems.at[4 * a + m], recv_sem=recv_sems.at[4 * a + m], device_id=(x, y, 1 - c),
                    device_id_type=MESH))
        for cp in sends:
            cp.start()
        for cp in sends:
            cp.wait()

    shapes = tuple(jax.ShapeDtypeStruct((4, r, g.shape[1]), g.dtype) for r, g in zip(rows, grads))
    return _pcall(body, out_shape=shapes, in_specs=[ANY] * n, out_specs=(ANY,) * n, name="exchange_siblings",
                  scratch_shapes=[pltpu.SemaphoreType.DMA((4 * n,))] * 2)(*grads)


def _exchange_chips(bufs):
    n = len(bufs)

    def body(*refs):
        srcs, lands = refs[:n], refs[n:2 * n]
        send_sems, recv_sems = refs[2 * n:]
        _, _, c = _my_position()
        copies = []
        for a in range(n):
            for m in range(1, 4):
                copies.append(pltpu.make_async_remote_copy(
                    src_ref=srcs[a].at[m - 1], dst_ref=lands[a].at[m - 1], send_sem=send_sems.at[3 * a + m - 1],
                    recv_sem=recv_sems.at[3 * a + m - 1], device_id=(*_chip(m), c), device_id_type=MESH))
        for cp in copies:
            cp.start()
        for cp in copies:
            cp.wait()

    shapes = tuple(jax.ShapeDtypeStruct(b.shape, b.dtype) for b in bufs)
    return _pcall(body, out_shape=shapes, in_specs=[ANY] * n, out_specs=(ANY,) * n, name="exchange_chips",
                  scratch_shapes=[pltpu.SemaphoreType.DMA((3 * n,))] * 2)(*bufs)


def _pair_sums(devs, grads, lands, n_steps, name):
    n = len(grads)
    rows = [l.shape[1] for l in lands]
    rbs = [r // n_steps for r in rows]

    def body(devs_ref, *refs):
        del devs_ref
        g_refs, land_refs, outs = refs[:4 * n], refs[4 * n:5 * n], refs[5 * n:]
        for a in range(n):
            outs[2 * a][...] = g_refs[4 * a][...] + land_refs[a][0]
            for m in range(1, 4):
                outs[2 * a + 1][m - 1] = (g_refs[4 * a + m][...] + land_refs[a][m]).astype(BF16)

    def block_of(m, per_dev):
        return lambda i, devs_ref: (devs_ref[m] * per_dev + i, 0)

    in_specs = [pl.BlockSpec((rb, l.shape[2]), block_of(m, n_steps)) for rb, l in zip(rbs, lands) for m in range(4)]
    in_specs += [pl.BlockSpec((4, rb, l.shape[2]), lambda i, devs_ref: (0, i, 0)) for rb, l in zip(rbs, lands)]
    out_shape, out_specs = [], []
    for rb, l in zip(rbs, lands):
        out_shape += [jax.ShapeDtypeStruct(l.shape[1:], F32), jax.ShapeDtypeStruct((3,) + l.shape[1:], BF16)]
        out_specs += [pl.BlockSpec((rb, l.shape[2]), lambda i, devs_ref: (i, 0)),
                      pl.BlockSpec((3, rb, l.shape[2]), lambda i, devs_ref: (0, i, 0))]
    grid_spec = pltpu.PrefetchScalarGridSpec(num_scalar_prefetch=1, grid=(n_steps,), in_specs=in_specs, out_specs=tuple(out_specs))
    res = _pcall(body, grid_spec=grid_spec, out_shape=tuple(out_shape), name=name,
                 compiler_params=_params(48, ("parallel",)))(devs, *[g for g in grads for _ in range(4)], *lands)
    return res[0::2], res[1::2]


def _final_sums(mine, lands, n_steps, name):
    n = len(mine)
    rbs = [o.shape[0] // n_steps for o in mine]

    def body(*refs):
        mine_refs, land_refs, outs = refs[:n], refs[n:2 * n], refs[2 * n:]
        for a in range(n):
            tot = mine_refs[a][...]
            for m in range(3):
                tot = tot + land_refs[a][m].astype(F32)
            outs[a][...] = tot

    in_specs = ([pl.BlockSpec((rb, o.shape[1]), lambda i: (i, 0)) for rb, o in zip(rbs, mine)]
                + [pl.BlockSpec((3, rb, o.shape[1]), lambda i: (0, i, 0)) for rb, o in zip(rbs, mine)])
    out_specs = tuple(pl.BlockSpec((rb, o.shape[1]), lambda i: (i, 0)) for rb, o in zip(rbs, mine))
    out_shape = tuple(jax.ShapeDtypeStruct(o.shape, F32) for o in mine)
    return _pcall(body, grid=(n_steps,), out_shape=out_shape, in_specs=in_specs, out_specs=out_specs, name=name,
                  compiler_params=_params(32, ("parallel",)))(*mine, *lands)


def _reduce_scatter(grads):
    lands = _exchange_siblings(grads)
    c = lax.axis_index("c")
    devs = jnp.stack([_flat(*_chip(m), c) for m in range(4)]).astype(jnp.int32)
    big_mine, big_send = _pair_sums(devs, grads[:1], lands[:1], 4, "pair_sums_w_in")
    small_mine, small_send = _pair_sums(devs, grads[1:], lands[1:], 1, "pair_sums_rest")
    got = _exchange_chips(list(big_send) + list(small_send))
    big = _final_sums(big_mine, got[:1], 4, "final_sums_w_in")
    small = _final_sums(small_mine, got[1:], 1, "final_sums_rest")
    return list(big) + list(small)


def _adamw(w, g, m, v):
    m_new = B1 * m + (1.0 - B1) * g
    v_new = B2 * v + (1.0 - B2) * (g * g)
    m_hat = m_new / (1.0 - B1 ** STEP)
    v_hat = v_new / (1.0 - B2 ** STEP)
    delta = -LR * (m_hat / (jnp.sqrt(v_hat) + EPS) + WD * w)
    return delta, m_new, v_new


def _adam_transposed(g_t, w, m, v, name):
    n = g_t.shape[0]
    full, tail = n // 128, n % 128

    def body(gt_ref, w_ref, m_ref, v_ref, g_ref, d_ref, mo_ref, vo_ref):
        def update(g, sl):
            delta, m_new, v_new = _adamw(w_ref[:, sl], g, m_ref[:, sl], v_ref[:, sl])
            g_ref[:, sl], d_ref[:, sl], mo_ref[:, sl], vo_ref[:, sl] = g, delta, m_new, v_new

        for a in range(full):
            update(gt_ref[pl.ds(128 * a, 128), :].T, pl.ds(128 * a, 128))
        if tail:
            update(gt_ref[pl.ds(n - 128, 128), :].T[:, 128 - tail:], pl.ds(128 * full, tail))

    shape = jax.ShapeDtypeStruct(w.shape, F32)
    return _pcall(body, out_shape=(shape,) * 4, in_specs=[VMEM] * 4, out_specs=(VMEM,) * 4, name=name,
                  compiler_params=_params(56))(g_t, w, m, v)


def _adam_many(items, name):
    n = len(items)

    def body(*refs):
        ins, outs = refs[:4 * n], refs[4 * n:]
        for a in range(n):
            w_ref, g_ref, m_ref, v_ref = ins[4 * a:4 * a + 4]
            delta, m_new, v_new = _adamw(w_ref[...], g_ref[...], m_ref[...], v_ref[...])
            outs[3 * a][...], outs[3 * a + 1][...], outs[3 * a + 2][...] = delta, m_new, v_new

    out_shape = tuple(jax.ShapeDtypeStruct(it[0].shape, F32) for it in items for _ in range(3))
    flat = [arr for it in items for arr in it]
    res = _pcall(body, out_shape=out_shape, in_specs=[VMEM] * (4 * n), out_specs=(VMEM,) * (3 * n), name=name,
                 compiler_params=_params(32))(*flat)
    return [tuple(res[3 * a:3 * a + 3]) for a in range(n)]


def _adam_w_ada(cact_all, dada_mine, w, m, v):
    def body(c_ref, d_ref, w_ref, m_ref, v_ref, g_ref, dl_ref, mo_ref, vo_ref):
        g = _tn(c_ref[...].astype(BF16), d_ref[...].astype(BF16))
        delta, m_new, v_new = _adamw(w_ref[...], g, m_ref[...], v_ref[...])
        g_ref[...], dl_ref[...], mo_ref[...], vo_ref[...] = g, delta, m_new, v_new

    shape = jax.ShapeDtypeStruct(w.shape, F32)
    return _pcall(body, out_shape=(shape,) * 4, in_specs=[VMEM] * 5, out_specs=(VMEM,) * 4, name="adam_w_ada",
                  compiler_params=_params(32))(cact_all, dada_mine, w, m, v)


def kernel(x, c, w_ada, b_ada, w_in, b_in, conv_w, w_proj_attn, w_proj_conv, w_out, b_out, ln_g, ln_b, loss_target, m_w_ada, m_b_ada, m_w_in, m_b_in, m_conv_w, m_w_proj_attn, m_w_proj_conv, m_w_out, m_b_out, m_ln_g, m_ln_b, v_w_ada, v_b_ada, v_w_in, v_b_in, v_conv_w, v_w_proj_attn, v_w_proj_conv, v_w_out, v_b_out, v_ln_g, v_ln_b):
    nbat, seq, _ = x.shape
    t = nbat * seq
    me = _flat(*_my_position())
    x2, tgt2 = x.reshape(t, D), loss_target.reshape(t, D)
    sq = lambda a: a.reshape(a.shape[1:])

    w_in_t_s, w_pa_t_s, w_pb_s, w_out_s, cact_s, cw_s = _prep(sq(w_in), sq(w_proj_attn), sq(w_proj_conv), sq(w_out), c, sq(conv_w))
    w_in_t, w_pa_t, w_pb, w_o, cact_g, cw_g = _gather_rows([w_in_t_s, w_pa_t_s, w_pb_s, w_out_s, cact_s, cw_s])
    cact_all = cact_g.reshape(N_DEV, 8, D)[:, :nbat].reshape(N_DEV * nbat, D)
    cw = cw_g.reshape(N_DEV, 8, -1)[:, :3].transpose(1, 0, 2).reshape(3, D)

    ncol = w_ada.shape[2]
    b_ada_mine = lax.dynamic_slice(b_ada, (0, me * ncol), (1, ncol))
    ada_slots = _ada_forward(cact_all, sq(w_ada), b_ada_mine)
    ada_all = ada_slots.transpose(1, 0, 2).reshape(N_DEV * nbat, 3, D)
    ada = lax.dynamic_slice(ada_all, (me * nbat, 0, 0), (nbat, 3, D))

    h = _make_h(x2, ada)
    qkv, rest = _project(h, w_in_t, b_in.reshape(N_SLAB, 1, SLAB))
    ols = [_attn_forward(qkv, g, nbat) for g in range(3)]
    (dproj, gx0, do_attn, ol_tot, merged, do_f, bbs, dyc, a_bf, dya, gb_rest, svec, dgate) = _mid(
        rest, ols, x2, tgt2, ada, cw, b_out, ln_g, ln_b, w_pa_t, w_pb, w_o)

    gb_qkv = []
    for g in range(3):
        dproj, gb = _attn_backward(qkv, do_attn, ol_tot, dproj, g, nbat)
        gb_qkv.append(gb)
    grad_x, dss = _grad_h(dproj, w_in_t, gx0, x2, ada)
    g_w_in_t = _grad_w_in_t(dproj, h)
    g_w_out = _grad_rows_2d(merged, do_f, "grad_w_out")
    g_w_pb = _grad_rows_2d(bbs, dyc, "grad_w_proj_conv")
    g_w_pa_t = _grad_rows_2d(dya, a_bf, "grad_w_proj_attn")

    rows8, tot, g_bada = _small_reduce(gb_rest, gb_qkv, svec, dgate, dss)
    loss = tot[0, P_LOSS]
    dada_all = rows8[:, 0, P_DADA:].reshape(N_DEV * nbat, 3 * D)
    dada_mine = lax.dynamic_slice(dada_all, (0, me * ncol), (N_DEV * nbat, ncol))

    g_in_t, g_pa_t, g_pb, g_out = _reduce_scatter([g_w_in_t, g_w_pa_t, g_w_pb, g_w_out])

    g_win, d_win, nm_win, nv_win = _adam_transposed(g_in_t, sq(w_in), sq(m_w_in), sq(v_w_in), "adam_w_in")
    g_wpa, d_wpa, nm_wpa, nv_wpa = _adam_transposed(g_pa_t, sq(w_proj_attn), sq(m_w_proj_attn), sq(v_w_proj_attn), "adam_w_proj_attn")
    g_wada, d_wada, nm_wada, nv_wada = _adam_w_ada(cact_all, dada_mine, sq(w_ada), sq(m_w_ada), sq(v_w_ada))
    g_bin = tot[:, P_BIN:P_BIN + D_IN]
    g_bout = tot[:, P_BOUT:P_BOUT + D]
    g_lng = tot[:, P_LNG:P_LNG + D]
    g_lnb = tot[:, P_LNB:P_LNB + D]
    g_conv = lax.dynamic_slice(tot[:, P_CONV:P_CONV + 3 * D].reshape(3, D), (0, me * cw_s.shape[1]), (3, cw_s.shape[1]))
    upd = _adam_many([
        (sq(w_proj_conv), g_pb, sq(m_w_proj_conv), sq(v_w_proj_conv)),
        (sq(w_out), g_out, sq(m_w_out), sq(v_w_out)),
        (b_ada, g_bada, m_b_ada, v_b_ada), (b_in, g_bin, m_b_in, v_b_in), (sq(conv_w), g_conv, sq(m_conv_w), sq(v_conv_w)),
        (b_out, g_bout, m_b_out, v_b_out), (ln_g, g_lng, m_ln_g, v_ln_g), (ln_b, g_lnb, m_ln_b, v_ln_b)], "adam_rest")
    (d_wpb, nm_wpb, nv_wpb), (d_wout, nm_wout, nv_wout), (d_bada, nm_bada, nv_bada), (d_bin, nm_bin, nv_bin), \
        (d_conv, nm_conv, nv_conv), (d_bout, nm_bout, nv_bout), (d_lng, nm_lng, nv_lng), (d_lnb, nm_lnb, nv_lnb) = upd

    ex = lambda a: a.reshape((1,) + a.shape)
    grads = [ex(g_wada), g_bada, ex(g_win), g_bin, ex(g_conv), ex(g_wpa), ex(g_pb), ex(g_out), g_bout, g_lng, g_lnb]
    deltas = [ex(d_wada), d_bada, ex(d_win), d_bin, ex(d_conv), ex(d_wpa), ex(d_wpb), ex(d_wout), d_bout, d_lng, d_lnb]
    new_m = [ex(nm_wada), nm_bada, ex(nm_win), nm_bin, ex(nm_conv), ex(nm_wpa), ex(nm_wpb), ex(nm_wout), nm_bout, nm_lng, nm_lnb]
    new_v = [ex(nv_wada), nv_bada, ex(nv_win), nv_bin, ex(nv_conv), ex(nv_wpa), ex(nv_wpb), ex(nv_wout), nv_bout, nv_lng, nv_lnb]
    return (loss, grad_x.reshape(x.shape), *grads, *deltas, *new_m, *new_v)
```

```python
import functools

import jax
import jax.numpy as jnp
from jax import lax
from jax.experimental import pallas as pl
from jax.experimental.pallas import tpu as pltpu

F32, BF16 = jnp.float32, jnp.bfloat16
MESH = pl.DeviceIdType.MESH
N_DEV = 8
D = 1024
SLAB = 256
N_QKV, N_REST = 9, 25
N_SLAB = N_QKV + N_REST
D_IN = N_SLAB * SLAB
DP_SLABS = 36
BLK = 128
GROUPS = ((128, 1), (512, 4), (2048, 16))
ALPHA = 2.0 ** 0.25
LN_EPS = 1e-5
LR, B1, B2, EPS, WD, STEP = 0.001, 0.9, 0.999, 1e-08, 0.01, 10
R_ZA, R_UX, R_GB, R_GC, R_ZC, R_GA, R_GBM = 0, 1, 5, 9, 13, 17, 21
P_BIN, P_BOUT, P_LNG, P_LNB, P_CONV, P_LOSS, P_DADA = 0, 8704, 9728, 10752, 11776, 14848, 14976
MIB = 1024 * 1024

_pcall = pl.pallas_call
ANY = pl.BlockSpec(memory_space=pl.ANY)
VMEM = pl.BlockSpec(memory_space=pltpu.VMEM)


def _params(vmem_mib=None, sem=None):
    kw = {}
    if vmem_mib is not None:
        kw["vmem_limit_bytes"] = vmem_mib * MIB
    if sem is not None:
        kw["dimension_semantics"] = sem
    return pltpu.CompilerParams(**kw)


def _nn(a, b):
    return jnp.dot(a, b, preferred_element_type=F32)


def _nt(a, b):
    return lax.dot_general(a, b, (((1,), (1,)), ((), ())), preferred_element_type=F32)


def _tn(a, b):
    return lax.dot_general(a, b, (((0,), (0,)), ((), ())), preferred_element_type=F32)


def _sigmoid(v):
    return 1.0 / (1.0 + jnp.exp(-v))


def _part8(v):
    return v.reshape(v.shape[0] // 8, 8, v.shape[1]).sum(axis=0)


def _my_position():
    return lax.axis_index("x"), lax.axis_index("y"), lax.axis_index("c")


def _flat(px, py, pc):
    return 4 * px + 2 * py + pc


def _peer(mask):
    x, y, c = _my_position()
    return (x ^ ((mask >> 2) & 1), y ^ ((mask >> 1) & 1), c ^ (mask & 1))


def _column_chunks(n):
    chunks = [(128 * a, 0, 128 * a, 128) for a in range(n // 128)]
    if n % 128:
        chunks.append((n - 128, 128 - n % 128, 128 * (n // 128), n % 128))
    return chunks


def _transpose_w_in(w_in, rb=128):
    k, n = w_in.shape

    def body(w_ref, o_ref):
        for src, skip, dst, size in _column_chunks(n):
            o_ref[pl.ds(dst, size), :] = w_ref[:, pl.ds(src, 128)].T[skip:].astype(BF16)

    return _pcall(body, grid=(k // rb,), out_shape=jax.ShapeDtypeStruct((n, k), BF16),
                  in_specs=[pl.BlockSpec((rb, n), lambda i: (i, 0))], out_specs=pl.BlockSpec((n, rb), lambda i: (0, i)),
                  name="transpose_w_in", compiler_params=_params(16, ("parallel",)))(w_in)


def _prep(w_pa, w_pb, w_out, c, conv_w):
    def body(wpa_ref, wpb_ref, wout_ref, c_ref, cw_ref, wpat_ref, wpb_o, wout_o, cact_ref, cwp_ref):
        wpat_ref[...] = wpa_ref[...].T.astype(BF16)
        wpb_o[...] = wpb_ref[...].astype(BF16)
        wout_o[...] = wout_ref[...].astype(BF16)
        cv = c_ref[...]
        cact_ref[...] = jnp.zeros_like(cact_ref)
        cact_ref[pl.ds(0, cv.shape[0]), :] = cv * _sigmoid(cv)
        cwp_ref[...] = jnp.zeros_like(cwp_ref)
        cwp_ref[pl.ds(0, 3), :] = cw_ref[...]

    out_shape = (jax.ShapeDtypeStruct((w_pa.shape[1], w_pa.shape[0]), BF16),
                 jax.ShapeDtypeStruct(w_pb.shape, BF16), jax.ShapeDtypeStruct(w_out.shape, BF16),
                 jax.ShapeDtypeStruct((8, D), F32), jax.ShapeDtypeStruct((8, conv_w.shape[1]), F32))
    return _pcall(body, out_shape=out_shape, in_specs=[VMEM] * 5, out_specs=(VMEM,) * 5, name="prep",
                  compiler_params=_params(16))(w_pa, w_pb, w_out, c, conv_w)


def _gather_rows(shards):
    n = len(shards)

    def body(*refs):
        srcs, outs = refs[:n], refs[n:2 * n]
        send_sems, recv_sems, local_sems = refs[2 * n:]
        x, y, c = _my_position()
        me, sibling = (x, y, c), (x, y, 1 - c)
        chips = [(1 - x, y), (x, 1 - y), (1 - x, 1 - y)]

        def rows(a, px, py, pc):
            r = shards[a].shape[0]
            return outs[a].at[pl.ds(pl.multiple_of(_flat(px, py, pc) * r, r), r), :]

        def copy(a, k, block, to, src=None):
            return pltpu.make_async_remote_copy(
                src_ref=rows(a, *block) if src is None else src, dst_ref=rows(a, *block),
                send_sem=send_sems.at[7 * a + k], recv_sem=recv_sems.at[7 * a + k], device_id=to, device_id_type=MESH)

        mine = [pltpu.make_async_copy(srcs[a], rows(a, *me), local_sems.at[a]) for a in range(n)]
        for cp in mine:
            cp.start()
        first = []
        for a in range(n):
            first.append(copy(a, 0, me, sibling, src=srcs[a]))
            first += [copy(a, 1 + j, me, (*chip, c), src=srcs[a]) for j, chip in enumerate(chips)]
        for cp in first:
            cp.start()
        passed = []
        for j, chip in enumerate(chips):
            for a in range(n):
                copy(a, 1 + j, (*chip, c), me).wait_recv()
                cp = copy(a, 4 + j, (*chip, c), sibling)
                cp.start()
                passed.append(cp)
        for a in range(n):
            copy(a, 0, sibling, me).wait_recv()
        for j, chip in enumerate(chips):
            for a in range(n):
                copy(a, 4 + j, (*chip, 1 - c), me).wait_recv()
        for cp in first + passed:
            cp.wait_send()
        for cp in mine:
            cp.wait()

    out_shape = tuple(jax.ShapeDtypeStruct((N_DEV * s.shape[0], s.shape[1]), s.dtype) for s in shards)
    return _pcall(body, out_shape=out_shape, in_specs=[ANY] * n, out_specs=(ANY,) * n, name="gather_rows",
                  scratch_shapes=[pltpu.SemaphoreType.DMA((7 * n,)), pltpu.SemaphoreType.DMA((7 * n,)),
                                  pltpu.SemaphoreType.DMA((n,))])(*shards)


def _exchange_slots(out_ref, send_sems, recv_sems):
    me = _flat(*_my_position())
    copies = []
    for mask in range(1, N_DEV):
        peer = _peer(mask)
        copies.append((mask, pltpu.make_async_remote_copy(
            src_ref=out_ref.at[me], dst_ref=out_ref.at[me], send_sem=send_sems.at[mask - 1],
            recv_sem=recv_sems.at[mask - 1], device_id=peer, device_id_type=MESH)))
    for _, cp in copies:
        cp.start()
    for mask, _ in copies:
        peer = _peer(mask)
        pltpu.make_async_remote_copy(
            src_ref=out_ref.at[_flat(*peer)], dst_ref=out_ref.at[_flat(*peer)], send_sem=send_sems.at[mask - 1],
            recv_sem=recv_sems.at[mask - 1], device_id=peer, device_id_type=MESH).wait_recv()
    for _, cp in copies:
        cp.wait_send()


def _ada_forward(cact_all, w_ada, b_ada_mine):
    nb, ncol = cact_all.shape[0], w_ada.shape[1]

    def body(c_ref, w_ref, b_ref, out_ref, send_sems, recv_sems):
        me = _flat(*_my_position())
        out_ref[me] = _nn(c_ref[...].astype(BF16), w_ref[...].astype(BF16)) + b_ref[...]
        _exchange_slots(out_ref, send_sems, recv_sems)

    return _pcall(body, out_shape=jax.ShapeDtypeStruct((N_DEV, nb, ncol), F32), in_specs=[VMEM] * 3, out_specs=VMEM,
                  scratch_shapes=[pltpu.SemaphoreType.DMA((7,)), pltpu.SemaphoreType.DMA((7,))], name="ada_forward",
                  compiler_params=_params(16))(cact_all, w_ada, b_ada_mine)


def _small_reduce(gb_rest, gb_qkv, svec, dgate, dss):
    nbat = dgate.shape[0]

    def body(gbr_ref, q0_ref, q1_ref, q2_ref, sv_ref, dg_ref, dss_ref, rows_ref, tot_ref, gbada_ref, send_sems, recv_sems):
        me = _flat(*_my_position())

        def put(off, v):
            rows_ref[me, :, pl.ds(off, v.shape[1])] = v

        def row(v):
            return jnp.sum(v, axis=0, keepdims=True)

        for g, q_ref in enumerate((q0_ref, q1_ref, q2_ref)):
            for which in range(3):
                put(P_BIN + SLAB * (3 * which + g), row(q_ref[which]))
        for s in range(N_REST):
            put(P_BIN + SLAB * (N_QKV + s), row(gbr_ref[s]))
        put(P_LNG, row(sv_ref[0]))
        put(P_LNB, row(sv_ref[1]))
        put(P_BOUT, row(sv_ref[2]))
        for j in range(3):
            put(P_CONV + D * j, row(sv_ref[3 + j]))
        loss = (0.5 / D) * jnp.sum(row(sv_ref[6]), axis=1, keepdims=True)
        put(P_LOSS, jnp.broadcast_to(loss, (1, 128)))
        for b in range(nbat):
            put(P_DADA + 3 * D * b, row(dss_ref[b, 0]))
            put(P_DADA + 3 * D * b + D, row(dss_ref[b, 1]))
            put(P_DADA + 3 * D * b + 2 * D, row(dg_ref[b]))
        _exchange_slots(rows_ref, send_sems, recv_sems)
        tot = rows_ref[0]
        for k in range(1, N_DEV):
            tot = tot + rows_ref[k]
        tot_ref[...] = tot
        gbada = tot[:, P_DADA:P_DADA + 3 * D]
        for b in range(1, nbat):
            gbada = gbada + tot[:, P_DADA + 3 * D * b:P_DADA + 3 * D * (b + 1)]
        gbada_ref[...] = gbada

    p_len = P_DADA + nbat * 3 * D
    out_shape = (jax.ShapeDtypeStruct((N_DEV, 1, p_len), F32), jax.ShapeDtypeStruct((1, p_len), F32),
                 jax.ShapeDtypeStruct((1, 3 * D), F32))
    return _pcall(body, out_shape=out_shape, in_specs=[VMEM] * 7, out_specs=(VMEM, VMEM, VMEM),
                  scratch_shapes=[pltpu.SemaphoreType.DMA((7,)), pltpu.SemaphoreType.DMA((7,))], name="small_reduce",
                  compiler_params=_params(16))(gb_rest, *gb_qkv, svec, dgate, dss)


def _make_h(x, ada, tm=512):
    t = x.shape[0]
    tps = (t // ada.shape[0]) // tm

    def body(x_ref, ada_ref, h_ref):
        h_ref[...] = (x_ref[...] * (1.0 + ada_ref[0, 1:2, :]) + ada_ref[0, 0:1, :]).astype(BF16)

    return _pcall(body, grid=(t // tm,), out_shape=jax.ShapeDtypeStruct((t, D), BF16),
                  in_specs=[pl.BlockSpec((tm, D), lambda i: (i, 0)), pl.BlockSpec((1, 3, D), lambda i: (i // tps, 0, 0))],
                  out_specs=pl.BlockSpec((tm, D), lambda i: (i, 0)), name="make_h",
                  compiler_params=_params(32, ("parallel",)))(x, ada)


def _project(h, w_in_t, b_in3):
    t = h.shape[0]

    def body(h_ref, w_ref, b_ref, qkv_ref, rest_ref):
        j = pl.program_id(0)
        v = _nt(h_ref[...], w_ref[...]) + b_ref[0]

        @pl.when(j < N_QKV)
        def _():
            qkv_ref[0] = v.astype(BF16)

        @pl.when(j >= N_QKV)
        def _():
            rest_ref[0] = v

    return _pcall(
        body, grid=(N_SLAB,),
        out_shape=(jax.ShapeDtypeStruct((N_QKV, t, SLAB), BF16), jax.ShapeDtypeStruct((N_REST, t, SLAB), F32)),
        in_specs=[pl.BlockSpec((t, D), lambda j: (0, 0)), pl.BlockSpec((SLAB, D), lambda j: (j, 0)),
                  pl.BlockSpec((1, 1, SLAB), lambda j: (j, 0, 0))],
        out_specs=(pl.BlockSpec((1, t, SLAB), lambda j: (jnp.minimum(j, N_QKV - 1), 0, 0)),
                   pl.BlockSpec((1, t, SLAB), lambda j: (jnp.maximum(j - N_QKV, 0), 0, 0))),
        name="project", compiler_params=_params(48, ("arbitrary",)))(h, w_in_t, b_in3)


def _bias_tables(g):
    window, dil = GROUPS[g]
    span = window // dil
    qi = jnp.arange(BLK)[:, None]
    kj = jnp.arange(2 * BLK)[None, :]
    delta = qi + BLK - kj
    valid = (delta >= 0) & (delta <= span)
    heads = jnp.arange(4, dtype=F32) + 4.0 * g
    slopes = 2.0 ** (-8.0 * (heads + 1.0) / 12.0)
    bias = -slopes[:, None, None] * (delta * dil).astype(F32)[None]
    return jnp.where(valid[None], bias, -1e30).reshape(4 * BLK, 2 * BLK)


def _head_masks(shape):
    lane = lax.broadcasted_iota(jnp.int32, shape, 1)
    return [(lane >= 64 * h) & (lane < 64 * (h + 1)) for h in range(4)]


def _stack_heads(v, masks):
    return jnp.concatenate([jnp.where(masks[h], v, jnp.zeros_like(v)) for h in range(4)], axis=0)


def _unstack_heads(v4, masks):
    out = jnp.where(masks[0], v4[0:BLK], 0.0)
    for h in range(1, 4):
        out = jnp.where(masks[h], v4[BLK * h:BLK * (h + 1)], out)
    return out


def _regroup(load_half, dst_ref, stage_ref, n, dil):
    for hlf in range(2):
        stage_ref[hlf] = load_half(hlf)

    def residue(r, carry):
        for hlf in range(2):
            dst_ref[pl.ds(pl.multiple_of(r * n, BLK), n), pl.ds(128 * hlf, 128)] = (
                stage_ref[hlf, pl.ds(r, n, stride=dil), :].astype(dst_ref.dtype))
        return carry

    lax.fori_loop(0, dil, residue, 0)


def _store_block(nat_ref, r, i, val, dil):
    for hlf in range(2):
        nat_ref[hlf, pl.ds(r + dil * BLK * i, BLK, stride=dil), :] = val[:, 128 * hlf:128 * (hlf + 1)]


def _for_blocks(block, dil, nblk):
    def residue(r, carry):
        block(r, 0, True)
        if nblk > 1:
            def loop(i, c):
                block(r, i, False)
                return c
            lax.fori_loop(1, nblk, loop, 0)
        return carry

    if dil == 1:
        residue(0, 0)
    else:
        lax.fori_loop(0, dil, residue, 0)


def _attn_forward(qkv, g, nbat):
    t = qkv.shape[1]
    seq = t // nbat
    dil = GROUPS[g][1]
    n = seq // dil
    nblk = n // BLK
    qkv4 = qkv.reshape(3, 3, t, SLAB)

    def body(qkv_ref, bias_ref, ol_ref, *scratch):
        masks = _head_masks((BLK, SLAB))
        if dil > 1:
            stage, qd, kd, vd, nat_o, nat_l = scratch
            for which, dst in enumerate((qd, kd, vd)):
                _regroup(lambda hlf, which=which: qkv_ref[which, 0, :, pl.ds(128 * hlf, 128)].astype(F32), dst, stage, n, dil)
        else:
            qd, kd, vd = qkv_ref.at[0, 0], qkv_ref.at[1, 0], qkv_ref.at[2, 0]

        def block(r, i, first):
            base = r * n
            qs = pl.ds(pl.multiple_of(base + i * BLK, BLK), BLK)
            ks = pl.ds(pl.multiple_of(base, BLK), BLK) if first else pl.ds(pl.multiple_of(base + (i - 1) * BLK, BLK), 2 * BLK)
            q, kk, vv = qd[qs, :], kd[ks, :], vd[ks, :]
            bias = bias_ref[:, pl.ds(BLK, BLK)] if first else bias_ref[...]
            s = _nt(_stack_heads(q, masks), kk) * 0.125 + bias
            m = jnp.max(s, axis=1, keepdims=True)
            p = jnp.exp(s - m)
            den = jnp.sum(p, axis=1, keepdims=True)
            out = _unstack_heads(_nn((p * (1.0 / den)).astype(BF16), vv), masks)
            lse = _unstack_heads(jnp.broadcast_to(m + jnp.log(den), (4 * BLK, SLAB)), masks)
            if dil > 1:
                _store_block(nat_o, r, i, out, dil)
                _store_block(nat_l, r, i, lse, dil)
            else:
                ol_ref[0, qs, :] = out
                ol_ref[1, qs, :] = lse

        _for_blocks(block, dil, nblk)
        if dil > 1:
            for hlf in range(2):
                ol_ref[0, :, pl.ds(128 * hlf, 128)] = nat_o[hlf]
                ol_ref[1, :, pl.ds(128 * hlf, 128)] = nat_l[hlf]

    scratch = []
    if dil > 1:
        scratch = [pltpu.VMEM((2, seq, 128), F32)] + [pltpu.VMEM((seq, SLAB), BF16)] * 3 + [pltpu.VMEM((2, seq, 128), F32)] * 2
    return _pcall(
        body, grid=(nbat,), out_shape=jax.ShapeDtypeStruct((2, t, SLAB), F32),
        in_specs=[pl.BlockSpec((3, 1, seq, SLAB), lambda b: (0, g, b, 0)),
                  pl.BlockSpec((4 * BLK, 2 * BLK), lambda b: (0, 0))],
        out_specs=pl.BlockSpec((2, seq, SLAB), lambda b: (0, b, 0)), scratch_shapes=scratch,
        name=f"attn_forward_{g}", compiler_params=_params(40, ("parallel",)))(qkv4, _bias_tables(g))


def _attn_backward(qkv, do_attn, ol_tot, dproj, g, nbat):
    t = qkv.shape[1]
    seq = t // nbat
    dil = GROUPS[g][1]
    n = seq // dil
    nblk = n // BLK
    qkv4 = qkv.reshape(3, 3, t, SLAB)
    dp4 = dproj.reshape(DP_SLABS // 3, 3, t, SLAB)

    def body(qkv_ref, do_ref, ol_ref, bias_ref, dp_in, dp_ref, gb_ref, dk_acc, dv_acc, *scratch):
        del dp_in
        masks = _head_masks((BLK, SLAB))

        @pl.when(pl.program_id(0) == 0)
        def _():
            gb_ref[...] = jnp.zeros_like(gb_ref)

        dk_acc[...] = jnp.zeros_like(dk_acc)
        dv_acc[...] = jnp.zeros_like(dv_acc)
        if dil > 1:
            stage, qd, kd, vd, dod, prodd, lsed, nat = scratch
            lanes = lambda hlf: pl.ds(128 * hlf, 128)
            for which, dst in enumerate((qd, kd, vd)):
                _regroup(lambda hlf, which=which: qkv_ref[which, 0, :, lanes(hlf)].astype(F32), dst, stage, n, dil)
            _regroup(lambda hlf: do_ref[:, lanes(hlf)].astype(F32), dod, stage, n, dil)
            _regroup(lambda hlf: do_ref[:, lanes(hlf)].astype(F32) * ol_ref[0, :, lanes(hlf)], prodd, stage, n, dil)
            _regroup(lambda hlf: ol_ref[1, :, lanes(hlf)], lsed, stage, n, dil)
        else:
            qd, kd, vd = qkv_ref.at[0, 0], qkv_ref.at[1, 0], qkv_ref.at[2, 0]

        def block(r, i, first):
            base = r * n
            qs = pl.ds(pl.multiple_of(base + i * BLK, BLK), BLK)
            ks = pl.ds(pl.multiple_of(base, BLK), BLK) if first else pl.ds(pl.multiple_of(base + (i - 1) * BLK, BLK), 2 * BLK)
            q, kk, vv = qd[qs, :], kd[ks, :], vd[ks, :]
            if dil > 1:
                do, prod, lse = dod[qs, :], prodd[qs, :], lsed[qs, :]
            else:
                do = do_ref[qs, :]
                prod = do.astype(F32) * ol_ref[0, qs, :]
                lse = ol_ref[1, qs, :]
            q4, do4 = _stack_heads(q, masks), _stack_heads(do, masks)
            bias = bias_ref[:, pl.ds(BLK, BLK)] if first else bias_ref[...]
            lse4 = jnp.concatenate([lse[:, 64 * h:64 * h + 1] for h in range(4)], axis=0)
            delta4 = jnp.concatenate([jnp.sum(jnp.where(masks[h], prod, 0.0), axis=1, keepdims=True) for h in range(4)], axis=0)
            p = jnp.exp(_nt(q4, kk) * 0.125 + bias - lse4)
            ds = (p * (_nt(do4, vv) - delta4)).astype(BF16)
            dv_acc[ks, :] += _tn(p.astype(BF16), do4)
            dk_acc[ks, :] += _tn(ds, q4) * 0.125
            dq = _unstack_heads(_nn(ds, kk), masks) * 0.125
            if dil > 1:
                _store_block(nat, r, i, dq, dil)
            else:
                dp_ref[0, 0, qs, :] = dq.astype(BF16)
            gb_ref[0] += _part8(dq)

        _for_blocks(block, dil, nblk)
        gb_ref[1] += _part8(dk_acc[...])
        gb_ref[2] += _part8(dv_acc[...])
        if dil > 1:
            def flush(which):
                for hlf in range(2):
                    dp_ref[which, 0, :, pl.ds(128 * hlf, 128)] = nat[hlf].astype(BF16)

            def to_token_order(acc_ref):
                def residue(r, carry):
                    for hlf in range(2):
                        nat[hlf, pl.ds(r, n, stride=dil), :] = acc_ref[pl.ds(pl.multiple_of(r * n, BLK), n), pl.ds(128 * hlf, 128)]
                    return carry
                lax.fori_loop(0, dil, residue, 0)

            flush(0)
            to_token_order(dk_acc)
            flush(1)
            to_token_order(dv_acc)
            flush(2)
        else:
            dp_ref[1, 0] = dk_acc[...].astype(BF16)
            dp_ref[2, 0] = dv_acc[...].astype(BF16)

    scratch = [pltpu.VMEM((seq, SLAB), F32)] * 2
    if dil > 1:
        scratch += ([pltpu.VMEM((2, seq, 128), F32)] + [pltpu.VMEM((seq, SLAB), BF16)] * 4 + [pltpu.VMEM((seq, SLAB), F32)] * 2
                    + [pltpu.VMEM((2, seq, 128), F32)])
    dp, gb = _pcall(
        body, grid=(nbat,),
        out_shape=(jax.ShapeDtypeStruct(dp4.shape, BF16), jax.ShapeDtypeStruct((3, 8, SLAB), F32)),
        in_specs=[pl.BlockSpec((3, 1, seq, SLAB), lambda b: (0, g, b, 0)),
                  pl.BlockSpec((seq, SLAB), lambda b: (b, 0)),
                  pl.BlockSpec((2, seq, SLAB), lambda b: (0, b, 0)),
                  pl.BlockSpec((4 * BLK, 2 * BLK), lambda b: (0, 0)), ANY],
        out_specs=(pl.BlockSpec((3, 1, seq, SLAB), lambda b: (DP_SLABS // 9 - 1, g, b, 0)),
                   pl.BlockSpec((3, 8, SLAB), lambda b: (0, 0, 0))),
        scratch_shapes=scratch, input_output_aliases={4: 0}, name=f"attn_backward_{g}",
        compiler_params=_params(48, ("arbitrary",)))(qkv4, do_attn, ol_tot, _bias_tables(g), dp4)
    return dp.reshape(DP_SLABS, t, SLAB), gb


def _mid(rest, ols, x, tgt, ada, cw, b_out, ln_g, ln_b, w_pa_t, w_pb, w_out, tm=256):
    t = x.shape[0]
    nbat = ada.shape[0]
    nt = t // tm
    tps = nt // nbat

    def body(rest_ref, halo_ref, ol0_ref, ol1_ref, ol2_ref, x_ref, t_ref, ada_ref, cw_ref, bout_ref, lng_ref, lnb_ref,
             wpat_ref, wpb_ref, wout_ref,
             dp_ref, gx0_ref, doa_ref, olt_ref, mg_ref, dof_ref, bbs_ref, dyc_ref, a_ref, dya_ref,
             gbr_ref, sv_ref, dgate_ref, carry_ref):
        i = pl.program_id(0)
        ti = nt - 1 - i
        pos = ti % tps

        @pl.when(i == 0)
        def _():
            gbr_ref[...] = jnp.zeros_like(gbr_ref)
            sv_ref[...] = jnp.zeros_like(sv_ref)

        @pl.when(pos == tps - 1)
        def _():
            dgate_ref[...] = jnp.zeros_like(dgate_ref)
            carry_ref[...] = jnp.zeros_like(carry_ref)

        row = lax.broadcasted_iota(jnp.int32, (tm, SLAB), 0)
        halo_on = (pos > 0).astype(F32)

        def cols(s):
            return pl.ds(SLAB * s, SLAB)

        l0, l1, l2 = ol0_ref[1], ol1_ref[1], ol2_ref[1]
        mx = jnp.maximum(jnp.maximum(l0, l1), l2)
        e0, e1, e2 = jnp.exp(l0 - mx), jnp.exp(l1 - mx), jnp.exp(l2 - mx)
        den = e0 + e1 + e2
        o_attn = (e0 * ol0_ref[0] + e1 * ol1_ref[0] + e2 * ol2_ref[0]) * (1.0 / den)
        olt_ref[0] = o_attn
        olt_ref[1] = mx + jnp.log(den)
        z_a = rest_ref[R_ZA]
        sg_za = _sigmoid(z_a)
        a_ref[...] = (o_attn * z_a * sg_za).astype(BF16)
        y_attn = _nt(a_ref[...], wpat_ref[...])

        def conv_parts(s):
            ux, gc = rest_ref[R_UX + s], rest_ref[R_GC + s]
            u = gc * ux
            hu = halo_ref[R_GC + s] * halo_ref[R_UX + s] * halo_on
            u1 = jnp.where(row == 0, hu[7:8], pltpu.roll(u, 1, 0))
            u2 = jnp.where(row == 0, hu[6:7], jnp.where(row == 1, hu[7:8], pltpu.roll(u, 2, 0)))
            conv = cw_ref[0:1, cols(s)] * u2 + cw_ref[1:2, cols(s)] * u1 + cw_ref[2:3, cols(s)] * u
            zc = rest_ref[R_ZC + s]
            sg = _sigmoid(zc)
            return ux, gc, u, u1, u2, conv, zc, sg

        for s in range(4):
            ux, gc, u, u1, u2, conv, zc, sg = conv_parts(s)
            bbs_ref[:, cols(s)] = (rest_ref[R_GB + s] * conv * (zc * sg)).astype(BF16)
        y_conv = _nn(bbs_ref[...], wpb_ref[...])

        for s in range(4):
            s_a, s_b = _sigmoid(rest_ref[R_GA + s]), _sigmoid(rest_ref[R_GBM + s])
            mg_ref[:, cols(s)] = (s_a * y_attn[:, SLAB * s:SLAB * (s + 1)] + s_b * y_conv[:, SLAB * s:SLAB * (s + 1)]).astype(BF16)
        o = _nn(mg_ref[...], wout_ref[...]) + bout_ref[...]
        gate = ada_ref[0, 2:3, :]
        r = ALPHA * x_ref[...] + gate * o
        mu = jnp.mean(r, axis=1, keepdims=True)
        rc = r - mu
        rstd = lax.rsqrt(jnp.mean(rc * rc, axis=1, keepdims=True) + LN_EPS)
        xhat = rc * rstd
        err = xhat * lng_ref[...] + lnb_ref[...] - t_ref[...]
        sv_ref[6] += _part8(err * err)
        dy = err * (1.0 / D)
        sv_ref[0] += _part8(dy * xhat)
        sv_ref[1] += _part8(dy)
        dxh = dy * lng_ref[...]
        dr = rstd * (dxh - jnp.mean(dxh, axis=1, keepdims=True) - xhat * jnp.mean(dxh * xhat, axis=1, keepdims=True))
        gx0_ref[...] = ALPHA * dr
        dgate_ref[0] += _part8(dr * o)
        do_ = dr * gate
        sv_ref[2] += _part8(do_)
        dof_ref[...] = do_.astype(BF16)
        dmerged = _nt(dof_ref[...], wout_ref[...])
        for s in range(4):
            s_a, s_b = _sigmoid(rest_ref[R_GA + s]), _sigmoid(rest_ref[R_GBM + s])
            dm = dmerged[:, SLAB * s:SLAB * (s + 1)]
            ya, yc = y_attn[:, SLAB * s:SLAB * (s + 1)], y_conv[:, SLAB * s:SLAB * (s + 1)]
            dya_ref[:, cols(s)] = (dm * s_a).astype(BF16)
            dyc_ref[:, cols(s)] = (dm * s_b).astype(BF16)
            dga = dm * ya * s_a * (1.0 - s_a)
            dgb = dm * yc * s_b * (1.0 - s_b)
            dp_ref[R_GA + s] = dga.astype(BF16)
            dp_ref[R_GBM + s] = dgb.astype(BF16)
            gbr_ref[R_GA + s] += _part8(dga)
            gbr_ref[R_GBM + s] += _part8(dgb)

        da = _nn(dya_ref[...], wpat_ref[...])
        doa_ref[...] = (da * z_a * sg_za).astype(BF16)
        dza = da * o_attn * (sg_za * (1.0 + z_a * (1.0 - sg_za)))
        dp_ref[R_ZA] = dza.astype(BF16)
        gbr_ref[R_ZA] += _part8(dza)

        dbb = _nt(dyc_ref[...], wpb_ref[...])
        for s in range(4):
            ux, gc, u, u1, u2, conv, zc, sg = conv_parts(s)
            gb = rest_ref[R_GB + s]
            d_b = dbb[:, SLAB * s:SLAB * (s + 1)]
            szc = zc * sg
            dgb_ = d_b * conv * szc
            dconv = d_b * gb * szc
            dzc = d_b * gb * conv * (sg * (1.0 + zc * (1.0 - sg)))
            sv_ref[3, :, cols(s)] += _part8(dconv * u2)
            sv_ref[4, :, cols(s)] += _part8(dconv * u1)
            sv_ref[5, :, cols(s)] += _part8(dconv * u)
            nxt = carry_ref[:, cols(s)]
            d1 = jnp.where(row == tm - 1, nxt[0:1], pltpu.roll(dconv, tm - 1, 0))
            d2 = jnp.where(row == tm - 1, nxt[1:2], jnp.where(row == tm - 2, nxt[0:1], pltpu.roll(dconv, tm - 2, 0)))
            carry_ref[:, cols(s)] = dconv[0:8]
            du = cw_ref[2:3, cols(s)] * dconv + cw_ref[1:2, cols(s)] * d1 + cw_ref[0:1, cols(s)] * d2
            dgc, dux = du * ux, du * gc
            for slab, val in ((R_GB + s, dgb_), (R_ZC + s, dzc), (R_GC + s, dgc), (R_UX + s, dux)):
                dp_ref[slab] = val.astype(BF16)
                gbr_ref[slab] += _part8(val)

    def tile(i):
        return nt - 1 - i

    row_blk = lambda i: (tile(i), 0)
    slab_blk = lambda i: (0, tile(i), 0)
    const2 = lambda i: (0, 0)
    const3 = lambda i: (0, 0, 0)
    in_specs = [
        pl.BlockSpec((N_REST, tm, SLAB), slab_blk),
        pl.BlockSpec((N_REST, 8, SLAB), lambda i: (0, jnp.maximum(tile(i) * (tm // 8) - 1, 0), 0)),
        pl.BlockSpec((2, tm, SLAB), slab_blk), pl.BlockSpec((2, tm, SLAB), slab_blk), pl.BlockSpec((2, tm, SLAB), slab_blk),
        pl.BlockSpec((tm, D), row_blk), pl.BlockSpec((tm, D), row_blk),
        pl.BlockSpec((1, 3, D), lambda i: (tile(i) // tps, 0, 0)),
        pl.BlockSpec((3, D), const2), pl.BlockSpec((1, D), const2), pl.BlockSpec((1, D), const2), pl.BlockSpec((1, D), const2),
        pl.BlockSpec((D, SLAB), const2), pl.BlockSpec((D, D), const2), pl.BlockSpec((D, D), const2)]
    bf_rows = lambda: jax.ShapeDtypeStruct((t, D), BF16)
    out_shape = (
        jax.ShapeDtypeStruct((DP_SLABS, t, SLAB), BF16), jax.ShapeDtypeStruct((t, D), F32),
        jax.ShapeDtypeStruct((t, SLAB), BF16), jax.ShapeDtypeStruct((2, t, SLAB), F32),
        bf_rows(), bf_rows(), bf_rows(), bf_rows(), jax.ShapeDtypeStruct((t, SLAB), BF16), bf_rows(),
        jax.ShapeDtypeStruct((N_REST, 8, SLAB), F32), jax.ShapeDtypeStruct((7, 8, D), F32),
        jax.ShapeDtypeStruct((nbat, 8, D), F32))
    out_specs = (
        pl.BlockSpec((N_REST, tm, SLAB), slab_blk), pl.BlockSpec((tm, D), row_blk),
        pl.BlockSpec((tm, SLAB), row_blk), pl.BlockSpec((2, tm, SLAB), slab_blk),
        pl.BlockSpec((tm, D), row_blk), pl.BlockSpec((tm, D), row_blk), pl.BlockSpec((tm, D), row_blk),
        pl.BlockSpec((tm, D), row_blk), pl.BlockSpec((tm, SLAB), row_blk), pl.BlockSpec((tm, D), row_blk),
        pl.BlockSpec((N_REST, 8, SLAB), const3), pl.BlockSpec((7, 8, D), const3),
        pl.BlockSpec((1, 8, D), lambda i: (tile(i) // tps, 0, 0)))
    return _pcall(body, grid=(nt,), out_shape=out_shape, in_specs=in_specs, out_specs=out_specs,
                  scratch_shapes=[pltpu.VMEM((8, D), F32)], name="mid",
                  compiler_params=_params(56, ("arbitrary",)))(
        rest, rest, *ols, x, tgt, ada, cw, b_out, ln_g, ln_b, w_pa_t, w_pb, w_out)


def _tn_matmul(lhs, rhs, lhs_spec, n_steps, out_rows, out_index, name):
    t, n = rhs.shape

    def body(l_ref, r_ref, o_ref):
        o_ref[...] = _tn(l_ref[0] if len(l_ref.shape) == 3 else l_ref[...], r_ref[...])

    return _pcall(body, grid=(n_steps,), out_shape=jax.ShapeDtypeStruct((out_rows, n), F32),
                  in_specs=[lhs_spec, pl.BlockSpec((t, n), lambda j: (0, 0))],
                  out_specs=pl.BlockSpec((SLAB, n), out_index), name=name,
                  compiler_params=_params(48, ("parallel",)))(lhs, rhs)


def _grad_rows_2d(lhs, rhs, name):
    t, k = lhs.shape
    return _tn_matmul(lhs, rhs, pl.BlockSpec((t, SLAB), lambda j: (0, j)), k // SLAB, k, lambda j: (j, 0), name)


def _w_row_block(j):
    return (j + N_QKV) % N_SLAB


def _dp_slab(j):
    return jnp.where(j < N_REST, j, j + 2)


def _grad_w_in_t(dproj, h):
    t = h.shape[0]
    return _tn_matmul(dproj, h, pl.BlockSpec((1, t, SLAB), lambda j: (_dp_slab(j), 0, 0)), N_SLAB, D_IN,
                      lambda j: (_w_row_block(j), 0), "grad_w_in")


def _grad_h(dproj, w_in_t, gx0, x, ada, tm=512):
    t = x.shape[0]
    nbat = ada.shape[0]
    tps = (t // nbat) // tm

    def body(dp_ref, w_ref, gx0_ref, x_ref, ada_ref, gx_ref, dss_ref):
        i = pl.program_id(0)
        dh = None
        for j in range(N_SLAB):
            slab = j if j < N_REST else j + 2
            part = _nn(dp_ref[slab], w_ref[pl.ds(SLAB * ((j + N_QKV) % N_SLAB), SLAB), :])
            dh = part if dh is None else dh + part
        gx_ref[...] = gx0_ref[...] + dh * (1.0 + ada_ref[0, 1:2, :])

        @pl.when((i % tps) == 0)
        def _():
            dss_ref[...] = jnp.zeros_like(dss_ref)

        dss_ref[0, 0] += _part8(dh)
        dss_ref[0, 1] += _part8(dh * x_ref[...])

    return _pcall(
        body, grid=(t // tm,),
        out_shape=(jax.ShapeDtypeStruct((t, D), F32), jax.ShapeDtypeStruct((nbat, 2, 8, D), F32)),
        in_specs=[pl.BlockSpec((DP_SLABS, tm, SLAB), lambda i: (0, i, 0)),
                  pl.BlockSpec((D_IN, D), lambda i: (0, 0), pipeline_mode=pl.Buffered(1)),
                  pl.BlockSpec((tm, D), lambda i: (i, 0)), pl.BlockSpec((tm, D), lambda i: (i, 0)),
                  pl.BlockSpec((1, 3, D), lambda i: (i // tps, 0, 0))],
        out_specs=(pl.BlockSpec((tm, D), lambda i: (i, 0)),
                   pl.BlockSpec((1, 2, 8, D), lambda i: (i // tps, 0, 0, 0))),
        name="grad_h", compiler_params=_params(60, ("arbitrary",)))(dproj, w_in_t, gx0, x, ada)


def _chip(m):
    x, y, _ = _my_position()
    return (x ^ ((m >> 1) & 1), y ^ (m & 1))


def _exchange_siblings(grads):
    n = len(grads)
    rows = [g.shape[0] // N_DEV for g in grads]

    def body(*refs):
        srcs, lands = refs[:n], refs[n:2 * n]
        send_sems, recv_sems = refs[2 * n:]
        x, y, c = _my_position()
        sends = []
        for a in range(n):
            for m in range(4):
                dev = _flat(*_chip(m), 1 - c)
                sends.append(pltpu.make_async_remote_copy(
                    src_ref=srcs[a].at[pl.ds(pl.multiple_of(dev * rows[a], 8), rows[a]), :], dst_ref=lands[a].at[m],
                    send_sem=send_sems.at[4 * a + m], recv_sem=recv_sems.at[4 * a + m], device_id=(x, y, 1 - c),
                    device_id_type=MESH))
        for cp in sends:
            cp.start()
        for cp in sends:
            cp.wait()

    shapes = tuple(jax.ShapeDtypeStruct((4, r, g.shape[1]), g.dtype) for r, g in zip(rows, grads))
    return _pcall(body, out_shape=shapes, in_specs=[ANY] * n, out_specs=(ANY,) * n, name="exchange_siblings",
                  scratch_shapes=[pltpu.SemaphoreType.DMA((4 * n,))] * 2)(*grads)


HBM = pl.BlockSpec(memory_space=pltpu.HBM)
SEM = pl.BlockSpec(memory_space=pltpu.SEMAPHORE)


def _chip_copies(srcs, lands, send_sems, recv_sems):
    _, _, c = _my_position()
    return [pltpu.make_async_remote_copy(
        src_ref=srcs[a].at[m - 1], dst_ref=lands[a].at[m - 1], send_sem=send_sems.at[3 * a + m - 1],
        recv_sem=recv_sems.at[3 * a + m - 1], device_id=(*_chip(m), c), device_id_type=MESH)
        for a in range(len(srcs)) for m in range(1, 4)]


def _exchange_chips_start(bufs):
    n = len(bufs)

    def body(*refs):
        srcs, lands = refs[:n], refs[n:2 * n]
        send_sems, recv_sems = refs[2 * n], refs[2 * n + 1]
        token = refs[-1]
        for cp in _chip_copies(srcs, lands, send_sems, recv_sems):
            cp.start()
        token[...] = jnp.zeros_like(token)

    hbm = tuple(pltpu.HBM(b.shape, b.dtype) for b in bufs)
    out_shape = (pltpu.SemaphoreType.DMA((3 * n,)), pltpu.SemaphoreType.DMA((3 * n,))) + hbm + hbm + (jax.ShapeDtypeStruct((8, 128), F32),)
    operands = [pltpu.with_memory_space_constraint(b, pltpu.HBM) for b in bufs]
    operands += [pltpu.with_memory_space_constraint(lax.empty(b.shape, b.dtype), pltpu.HBM) for b in bufs]
    res = _pcall(body, out_shape=out_shape, in_specs=[HBM] * (2 * n), out_specs=(SEM, SEM) + (HBM,) * (2 * n) + (VMEM,),
                 input_output_aliases={i: 2 + i for i in range(2 * n)}, name="exchange_chips_start",
                 compiler_params=pltpu.CompilerParams(has_side_effects=pltpu.SideEffectType.DATAFLOW_SIDE_EFFECTING))(*operands)
    return res[0], res[1], res[2:2 + n], res[2 + n:2 + 2 * n], res[-1]


def _exchange_chips_wait(send_sems, recv_sems, bufs, lands, after):
    n = len(bufs)

    def body(*refs):
        srcs, lnds = refs[:n], refs[n:2 * n]
        s_sems, r_sems = refs[2 * n], refs[2 * n + 1]
        for cp in _chip_copies(srcs, lnds, s_sems, r_sems):
            cp.wait_send()
            cp.wait_recv()

    hbm = tuple(pltpu.HBM(b.shape, b.dtype) for b in bufs)
    res = _pcall(body, out_shape=hbm + hbm, in_specs=[HBM] * (2 * n) + [SEM, SEM, ANY], out_specs=(HBM,) * (2 * n),
                 input_output_aliases={i: i for i in range(2 * n)}, name="exchange_chips_wait",
                 compiler_params=pltpu.CompilerParams(has_side_effects=pltpu.SideEffectType.DATAFLOW_SIDE_EFFECTING))(
        *bufs, *lands, send_sems, recv_sems, after)
    return res[n:]


def _pair_sums(devs, grads, lands, n_steps, name):
    n = len(grads)
    rows = [l.shape[1] for l in lands]
    rbs = [r // n_steps for r in rows]

    def body(devs_ref, *refs):
        del devs_ref
        g_refs, land_refs, outs = refs[:4 * n], refs[4 * n:5 * n], refs[5 * n:]
        for a in range(n):
            outs[2 * a][...] = g_refs[4 * a][...] + land_refs[a][0]
            for m in range(1, 4):
                outs[2 * a + 1][m - 1] = (g_refs[4 * a + m][...] + land_refs[a][m]).astype(BF16)

    def block_of(m, per_dev):
        return lambda i, devs_ref: (devs_ref[m] * per_dev + i, 0)

    in_specs = [pl.BlockSpec((rb, l.shape[2]), block_of(m, n_steps)) for rb, l in zip(rbs, lands) for m in range(4)]
    in_specs += [pl.BlockSpec((4, rb, l.shape[2]), lambda i, devs_ref: (0, i, 0)) for rb, l in zip(rbs, lands)]
    out_shape, out_specs = [], []
    for rb, l in zip(rbs, lands):
        out_shape += [jax.ShapeDtypeStruct(l.shape[1:], F32), jax.ShapeDtypeStruct((3,) + l.shape[1:], BF16)]
        out_specs += [pl.BlockSpec((rb, l.shape[2]), lambda i, devs_ref: (i, 0)),
                      pl.BlockSpec((3, rb, l.shape[2]), lambda i, devs_ref: (0, i, 0))]
    grid_spec = pltpu.PrefetchScalarGridSpec(num_scalar_prefetch=1, grid=(n_steps,), in_specs=in_specs, out_specs=tuple(out_specs))
    res = _pcall(body, grid_spec=grid_spec, out_shape=tuple(out_shape), name=name,
                 compiler_params=_params(48, ("parallel",)))(devs, *[g for g in grads for _ in range(4)], *lands)
    return res[0::2], res[1::2]


def _final_sums(mine, lands, n_steps, name):
    n = len(mine)
    rbs = [o.shape[0] // n_steps for o in mine]

    def body(*refs):
        mine_refs, land_refs, outs = refs[:n], refs[n:2 * n], refs[2 * n:]
        for a in range(n):
            tot = mine_refs[a][...]
            for m in range(3):
                tot = tot + land_refs[a][m].astype(F32)
            outs[a][...] = tot

    in_specs = ([pl.BlockSpec((rb, o.shape[1]), lambda i: (i, 0)) for rb, o in zip(rbs, mine)]
                + [pl.BlockSpec((3, rb, o.shape[1]), lambda i: (0, i, 0)) for rb, o in zip(rbs, mine)])
    out_specs = tuple(pl.BlockSpec((rb, o.shape[1]), lambda i: (i, 0)) for rb, o in zip(rbs, mine))
    out_shape = tuple(jax.ShapeDtypeStruct(o.shape, F32) for o in mine)
    return _pcall(body, grid=(n_steps,), out_shape=out_shape, in_specs=in_specs, out_specs=out_specs, name=name,
                  compiler_params=_params(32, ("parallel",)))(*mine, *lands)


def _reduce_scatter_begin(grads):
    lands = _exchange_siblings(grads)
    c = lax.axis_index("c")
    devs = jnp.stack([_flat(*_chip(m), c) for m in range(4)]).astype(jnp.int32)
    big_mine, big_send = _pair_sums(devs, grads[:1], lands[:1], 4, "pair_sums_w_in")
    small_mine, small_send = _pair_sums(devs, grads[1:], lands[1:], 1, "pair_sums_rest")
    send_sems, recv_sems, bufs, zones, token = _exchange_chips_start(list(big_send) + list(small_send))
    return (send_sems, recv_sems, bufs, zones, list(big_mine) + list(small_mine)), token


def _reduce_scatter_end(state, after):
    send_sems, recv_sems, bufs, zones, mine = state
    got = _exchange_chips_wait(send_sems, recv_sems, bufs, zones, after)
    big = _final_sums(mine[:1], got[:1], 4, "final_sums_w_in")
    small = _final_sums(mine[1:], got[1:], 1, "final_sums_rest")
    return list(big) + list(small)


def _adamw(w, g, m, v):
    m_new = B1 * m + (1.0 - B1) * g
    v_new = B2 * v + (1.0 - B2) * (g * g)
    m_hat = m_new / (1.0 - B1 ** STEP)
    v_hat = v_new / (1.0 - B2 ** STEP)
    delta = -LR * (m_hat / (jnp.sqrt(v_hat) + EPS) + WD * w)
    return delta, m_new, v_new


def _adam_transposed(g_t, w, m, v, name):
    n, k = g_t.shape
    rb = min(k, 128)

    def body(gt_ref, w_ref, m_ref, v_ref, g_ref, d_ref, mo_ref, vo_ref):
        for src, skip, dst, size in _column_chunks(n):
            sl = pl.ds(dst, size)
            g = gt_ref[pl.ds(src, 128), :].T[:, skip:]
            delta, m_new, v_new = _adamw(w_ref[:, sl], g, m_ref[:, sl], v_ref[:, sl])
            g_ref[:, sl], d_ref[:, sl], mo_ref[:, sl], vo_ref[:, sl] = g, delta, m_new, v_new

    shape = jax.ShapeDtypeStruct(w.shape, F32)
    rows = pl.BlockSpec((rb, n), lambda i: (i, 0))
    return _pcall(body, grid=(k // rb,), out_shape=(shape,) * 4,
                  in_specs=[pl.BlockSpec((n, rb), lambda i: (0, i)), rows, rows, rows], out_specs=(rows,) * 4, name=name,
                  compiler_params=_params(32, ("parallel",)))(g_t, w, m, v)


def _adam_many(items, name):
    n = len(items)

    def body(*refs):
        ins, outs = refs[:4 * n], refs[4 * n:]
        for a in range(n):
            w_ref, g_ref, m_ref, v_ref = ins[4 * a:4 * a + 4]
            delta, m_new, v_new = _adamw(w_ref[...], g_ref[...], m_ref[...], v_ref[...])
            outs[3 * a][...], outs[3 * a + 1][...], outs[3 * a + 2][...] = delta, m_new, v_new

    out_shape = tuple(jax.ShapeDtypeStruct(it[0].shape, F32) for it in items for _ in range(3))
    flat = [arr for it in items for arr in it]
    res = _pcall(body, out_shape=out_shape, in_specs=[VMEM] * (4 * n), out_specs=(VMEM,) * (3 * n), name=name,
                 compiler_params=_params(32))(*flat)
    return [tuple(res[3 * a:3 * a + 3]) for a in range(n)]


def _adam_w_ada(cact_all, dada_mine, w, m, v):
    def body(c_ref, d_ref, w_ref, m_ref, v_ref, g_ref, dl_ref, mo_ref, vo_ref):
        g = _tn(c_ref[...].astype(BF16), d_ref[...].astype(BF16))
        delta, m_new, v_new = _adamw(w_ref[...], g, m_ref[...], v_ref[...])
        g_ref[...], dl_ref[...], mo_ref[...], vo_ref[...] = g, delta, m_new, v_new

    shape = jax.ShapeDtypeStruct(w.shape, F32)
    return _pcall(body, out_shape=(shape,) * 4, in_specs=[VMEM] * 5, out_specs=(VMEM,) * 4, name="adam_w_ada",
                  compiler_params=_params(32))(cact_all, dada_mine, w, m, v)


def kernel(x, c, w_ada, b_ada, w_in, b_in, conv_w, w_proj_attn, w_proj_conv, w_out, b_out, ln_g, ln_b, loss_target, m_w_ada, m_b_ada, m_w_in, m_b_in, m_conv_w, m_w_proj_attn, m_w_proj_conv, m_w_out, m_b_out, m_ln_g, m_ln_b, v_w_ada, v_b_ada, v_w_in, v_b_in, v_conv_w, v_w_proj_attn, v_w_proj_conv, v_w_out, v_b_out, v_ln_g, v_ln_b):
    nbat, seq, _ = x.shape
    t = nbat * seq
    me = _flat(*_my_position())
    x2, tgt2 = x.reshape(t, D), loss_target.reshape(t, D)
    sq = lambda a: a.reshape(a.shape[1:])

    w_in_t_s = _transpose_w_in(sq(w_in))
    w_pa_t_s, w_pb_s, w_out_s, cact_s, cw_s = _prep(sq(w_proj_attn), sq(w_proj_conv), sq(w_out), c, sq(conv_w))
    w_in_t, w_pa_t, w_pb, w_o, cact_g, cw_g = _gather_rows([w_in_t_s, w_pa_t_s, w_pb_s, w_out_s, cact_s, cw_s])
    cact_all = cact_g.reshape(N_DEV, 8, D)[:, :nbat].reshape(N_DEV * nbat, D)
    cw = cw_g.reshape(N_DEV, 8, -1)[:, :3].transpose(1, 0, 2).reshape(3, D)

    ncol = w_ada.shape[2]
    b_ada_mine = lax.dynamic_slice(b_ada, (0, me * ncol), (1, ncol))
    ada_slots = _ada_forward(cact_all, sq(w_ada), b_ada_mine)
    ada_all = ada_slots.transpose(1, 0, 2).reshape(N_DEV * nbat, 3, D)
    ada = lax.dynamic_slice(ada_all, (me * nbat, 0, 0), (nbat, 3, D))

    h = _make_h(x2, ada)
    qkv, rest = _project(h, w_in_t, b_in.reshape(N_SLAB, 1, SLAB))
    ols = [_attn_forward(qkv, g, nbat) for g in range(3)]
    (dproj, gx0, do_attn, ol_tot, merged, do_f, bbs, dyc, a_bf, dya, gb_rest, svec, dgate) = _mid(
        rest, ols, x2, tgt2, ada, cw, b_out, ln_g, ln_b, w_pa_t, w_pb, w_o)

    gb_qkv = []
    for g in range(3):
        dproj, gb = _attn_backward(qkv, do_attn, ol_tot, dproj, g, nbat)
        gb_qkv.append(gb)
    g_w_in_t = _grad_w_in_t(dproj, h)
    g_w_out = _grad_rows_2d(merged, do_f, "grad_w_out")
    g_w_pb = _grad_rows_2d(bbs, dyc, "grad_w_proj_conv")
    g_w_pa_t = _grad_rows_2d(dya, a_bf, "grad_w_proj_attn")

    rs_state, token = _reduce_scatter_begin([g_w_in_t, g_w_pa_t, g_w_pb, g_w_out])
    grad_x, dss = _grad_h(dproj, w_in_t, gx0, x2, ada + token[0, 0])

    rows8, tot, g_bada = _small_reduce(gb_rest, gb_qkv, svec, dgate, dss)
    g_in_t, g_pa_t, g_pb, g_out = _reduce_scatter_end(rs_state, tot)
    loss = tot[0, P_LOSS]
    dada_all = rows8[:, 0, P_DADA:].reshape(N_DEV * nbat, 3 * D)
    dada_mine = lax.dynamic_slice(dada_all, (0, me * ncol), (N_DEV * nbat, ncol))

    g_win, d_win, nm_win, nv_win = _adam_transposed(g_in_t, sq(w_in), sq(m_w_in), sq(v_w_in), "adam_w_in")
    g_wpa, d_wpa, nm_wpa, nv_wpa = _adam_transposed(g_pa_t, sq(w_proj_attn), sq(m_w_proj_attn), sq(v_w_proj_attn), "adam_w_proj_attn")
    g_wada, d_wada, nm_wada, nv_wada = _adam_w_ada(cact_all, dada_mine, sq(w_ada), sq(m_w_ada), sq(v_w_ada))
    g_bin = tot[:, P_BIN:P_BIN + D_IN]
    g_bout = tot[:, P_BOUT:P_BOUT + D]
    g_lng = tot[:, P_LNG:P_LNG + D]
    g_lnb = tot[:, P_LNB:P_LNB + D]
    g_conv = lax.dynamic_slice(tot[:, P_CONV:P_CONV + 3 * D].reshape(3, D), (0, me * cw_s.shape[1]), (3, cw_s.shape[1]))
    upd = _adam_many([
        (sq(w_proj_conv), g_pb, sq(m_w_proj_conv), sq(v_w_proj_conv)),
        (sq(w_out), g_out, sq(m_w_out), sq(v_w_out)),
        (b_ada, g_bada, m_b_ada, v_b_ada), (b_in, g_bin, m_b_in, v_b_in), (sq(conv_w), g_conv, sq(m_conv_w), sq(v_conv_w)),
        (b_out, g_bout, m_b_out, v_b_out), (ln_g, g_lng, m_ln_g, v_ln_g), (ln_b, g_lnb, m_ln_b, v_ln_b)], "adam_rest")
    (d_wpb, nm_wpb, nv_wpb), (d_wout, nm_wout, nv_wout), (d_bada, nm_bada, nv_bada), (d_bin, nm_bin, nv_bin), \
        (d_conv, nm_conv, nv_conv), (d_bout, nm_bout, nv_bout), (d_lng, nm_lng, nv_lng), (d_lnb, nm_lnb, nv_lnb) = upd

    ex = lambda a: a.reshape((1,) + a.shape)
    grads = [ex(g_wada), g_bada, ex(g_win), g_bin, ex(g_conv), ex(g_wpa), ex(g_pb), ex(g_out), g_bout, g_lng, g_lnb]
    deltas = [ex(d_wada), d_bada, ex(d_win), d_bin, ex(d_conv), ex(d_wpa), ex(d_wpb), ex(d_wout), d_bout, d_lng, d_lnb]
    new_m = [ex(nm_wada), nm_bada, ex(nm_win), nm_bin, ex(nm_conv), ex(nm_wpa), ex(nm_wpb), ex(nm_wout), nm_bout, nm_lng, nm_lnb]
    new_v = [ex(nv_wada), nv_bada, ex(nv_win), nv_bin, ex(nv_conv), ex(nv_wpa), ex(nv_wpb), ex(nv_wout), nv_bout, nv_lng, nv_lnb]
    return (loss, grad_x.reshape(x.shape), *grads, *deltas, *new_m, *new_v)
```

```python
import functools

import jax
import jax.numpy as jnp
from jax import lax
from jax.experimental import pallas as pl
from jax.experimental.pallas import tpu as pltpu

F32, BF16 = jnp.float32, jnp.bfloat16
MESH = pl.DeviceIdType.MESH
N_DEV = 8
D = 1024
SLAB = 256
N_QKV, N_REST = 9, 25
N_SLAB = N_QKV + N_REST
D_IN = N_SLAB * SLAB
DP_SLABS = 36
BLK = 128
GROUPS = ((128, 1), (512, 4), (2048, 16))
ALPHA = 2.0 ** 0.25
LN_EPS = 1e-5
LR, B1, B2, EPS, WD, STEP = 0.001, 0.9, 0.999, 1e-08, 0.01, 10
R_ZA, R_UX, R_GB, R_GC, R_ZC, R_GA, R_GBM = 0, 1, 5, 9, 13, 17, 21
P_BIN, P_BOUT, P_LNG, P_LNB, P_CONV, P_LOSS, P_DADA = 0, 8704, 9728, 10752, 11776, 14848, 14976
MIB = 1024 * 1024

_pcall = pl.pallas_call
ANY = pl.BlockSpec(memory_space=pl.ANY)
VMEM = pl.BlockSpec(memory_space=pltpu.VMEM)


def _params(vmem_mib=None, sem=None):
    kw = {}
    if vmem_mib is not None:
        kw["vmem_limit_bytes"] = vmem_mib * MIB
    if sem is not None:
        kw["dimension_semantics"] = sem
    return pltpu.CompilerParams(**kw)


def _nn(a, b):
    return jnp.dot(a, b, preferred_element_type=F32)


def _nt(a, b):
    return lax.dot_general(a, b, (((1,), (1,)), ((), ())), preferred_element_type=F32)


def _tn(a, b):
    return lax.dot_general(a, b, (((0,), (0,)), ((), ())), preferred_element_type=F32)


def _sigmoid(v):
    return 1.0 / (1.0 + jnp.exp(-v))


def _part8(v):
    return v.reshape(v.shape[0] // 8, 8, v.shape[1]).sum(axis=0)


def _my_position():
    return lax.axis_index("x"), lax.axis_index("y"), lax.axis_index("c")


def _flat(px, py, pc):
    return 4 * px + 2 * py + pc


def _peer(mask):
    x, y, c = _my_position()
    return (x ^ ((mask >> 2) & 1), y ^ ((mask >> 1) & 1), c ^ (mask & 1))


def _column_chunks(n):
    chunks = [(128 * a, 0, 128 * a, 128) for a in range(n // 128)]
    if n % 128:
        chunks.append((n - 128, 128 - n % 128, 128 * (n // 128), n % 128))
    return chunks


def _cast_rows(w, n_steps, name):
    rows, ncol = w.shape
    blk = pl.BlockSpec((rows // n_steps, ncol), lambda i: (i, 0))

    def body(w_ref, o_ref):
        o_ref[...] = w_ref[...].astype(BF16)

    return _pcall(body, grid=(n_steps,), out_shape=jax.ShapeDtypeStruct(w.shape, BF16), in_specs=[blk], out_specs=blk,
                  name=name, compiler_params=_params(16, ("parallel",)))(w)


def _prep(w_pa, w_pb, w_out, c, conv_w):
    def body(wpa_ref, wpb_ref, wout_ref, c_ref, cw_ref, wpat_ref, wpb_o, wout_o, cact_ref, cwp_ref):
        wpat_ref[...] = wpa_ref[...].T.astype(BF16)
        wpb_o[...] = wpb_ref[...].astype(BF16)
        wout_o[...] = wout_ref[...].astype(BF16)
        cv = c_ref[...]
        cact_ref[...] = jnp.zeros_like(cact_ref)
        cact_ref[pl.ds(0, cv.shape[0]), :] = cv * _sigmoid(cv)
        cwp_ref[...] = jnp.zeros_like(cwp_ref)
        cwp_ref[pl.ds(0, 3), :] = cw_ref[...]

    out_shape = (jax.ShapeDtypeStruct((w_pa.shape[1], w_pa.shape[0]), BF16),
                 jax.ShapeDtypeStruct(w_pb.shape, BF16), jax.ShapeDtypeStruct(w_out.shape, BF16),
                 jax.ShapeDtypeStruct((8, D), F32), jax.ShapeDtypeStruct((8, conv_w.shape[1]), F32))
    return _pcall(body, out_shape=out_shape, in_specs=[VMEM] * 5, out_specs=(VMEM,) * 5, name="prep",
                  compiler_params=_params(16))(w_pa, w_pb, w_out, c, conv_w)


def _gather_rows(shards):
    n = len(shards)

    def body(*refs):
        srcs, outs = refs[:n], refs[n:2 * n]
        send_sems, recv_sems, local_sems = refs[2 * n:]
        x, y, c = _my_position()
        me, sibling = (x, y, c), (x, y, 1 - c)
        chips = [(1 - x, y), (x, 1 - y), (1 - x, 1 - y)]

        def rows(a, px, py, pc):
            r = shards[a].shape[0]
            return outs[a].at[pl.ds(pl.multiple_of(_flat(px, py, pc) * r, r), r), :]

        def copy(a, k, block, to, src=None):
            return pltpu.make_async_remote_copy(
                src_ref=rows(a, *block) if src is None else src, dst_ref=rows(a, *block),
                send_sem=send_sems.at[7 * a + k], recv_sem=recv_sems.at[7 * a + k], device_id=to, device_id_type=MESH)

        mine = [pltpu.make_async_copy(srcs[a], rows(a, *me), local_sems.at[a]) for a in range(n)]
        for cp in mine:
            cp.start()
        first = []
        for a in range(n):
            first.append(copy(a, 0, me, sibling, src=srcs[a]))
            first += [copy(a, 1 + j, me, (*chip, c), src=srcs[a]) for j, chip in enumerate(chips)]
        for cp in first:
            cp.start()
        passed = []
        for j, chip in enumerate(chips):
            for a in range(n):
                copy(a, 1 + j, (*chip, c), me).wait_recv()
                cp = copy(a, 4 + j, (*chip, c), sibling)
                cp.start()
                passed.append(cp)
        for a in range(n):
            copy(a, 0, sibling, me).wait_recv()
        for j, chip in enumerate(chips):
            for a in range(n):
                copy(a, 4 + j, (*chip, 1 - c), me).wait_recv()
        for cp in first + passed:
            cp.wait_send()
        for cp in mine:
            cp.wait()

    out_shape = tuple(jax.ShapeDtypeStruct((N_DEV * s.shape[0], s.shape[1]), s.dtype) for s in shards)
    return _pcall(body, out_shape=out_shape, in_specs=[ANY] * n, out_specs=(ANY,) * n, name="gather_rows",
                  scratch_shapes=[pltpu.SemaphoreType.DMA((7 * n,)), pltpu.SemaphoreType.DMA((7 * n,)),
                                  pltpu.SemaphoreType.DMA((n,))])(*shards)


def _exchange_slots(out_ref, send_sems, recv_sems):
    me = _flat(*_my_position())
    copies = []
    for mask in range(1, N_DEV):
        peer = _peer(mask)
        copies.append((mask, pltpu.make_async_remote_copy(
            src_ref=out_ref.at[me], dst_ref=out_ref.at[me], send_sem=send_sems.at[mask - 1],
            recv_sem=recv_sems.at[mask - 1], device_id=peer, device_id_type=MESH)))
    for _, cp in copies:
        cp.start()
    for mask, _ in copies:
        peer = _peer(mask)
        pltpu.make_async_remote_copy(
            src_ref=out_ref.at[_flat(*peer)], dst_ref=out_ref.at[_flat(*peer)], send_sem=send_sems.at[mask - 1],
            recv_sem=recv_sems.at[mask - 1], device_id=peer, device_id_type=MESH).wait_recv()
    for _, cp in copies:
        cp.wait_send()


def _ada_forward(cact_all, w_ada, b_ada_mine):
    nb, ncol = cact_all.shape[0], w_ada.shape[1]

    def body(c_ref, w_ref, b_ref, out_ref, send_sems, recv_sems):
        me = _flat(*_my_position())
        out_ref[me] = _nn(c_ref[...].astype(BF16), w_ref[...].astype(BF16)) + b_ref[...]
        _exchange_slots(out_ref, send_sems, recv_sems)

    return _pcall(body, out_shape=jax.ShapeDtypeStruct((N_DEV, nb, ncol), F32), in_specs=[VMEM] * 3, out_specs=VMEM,
                  scratch_shapes=[pltpu.SemaphoreType.DMA((7,)), pltpu.SemaphoreType.DMA((7,))], name="ada_forward",
                  compiler_params=_params(16))(cact_all, w_ada, b_ada_mine)


def _small_reduce(gb_rest, gb_qkv, svec, dgate, dss):
    nbat = dgate.shape[0]

    def body(gbr_ref, q0_ref, q1_ref, q2_ref, sv_ref, dg_ref, dss_ref, rows_ref, tot_ref, gbada_ref, send_sems, recv_sems):
        me = _flat(*_my_position())

        def put(off, v):
            rows_ref[me, :, pl.ds(off, v.shape[1])] = v

        def row(v):
            return jnp.sum(v, axis=0, keepdims=True)

        for g, q_ref in enumerate((q0_ref, q1_ref, q2_ref)):
            for which in range(3):
                put(P_BIN + SLAB * (3 * which + g), row(q_ref[which]))
        for s in range(N_REST):
            put(P_BIN + SLAB * (N_QKV + s), row(gbr_ref[s]))
        put(P_LNG, row(sv_ref[0]))
        put(P_LNB, row(sv_ref[1]))
        put(P_BOUT, row(sv_ref[2]))
        for j in range(3):
            put(P_CONV + D * j, row(sv_ref[3 + j]))
        loss = (0.5 / D) * jnp.sum(row(sv_ref[6]), axis=1, keepdims=True)
        put(P_LOSS, jnp.broadcast_to(loss, (1, 128)))
        for b in range(nbat):
            put(P_DADA + 3 * D * b, row(dss_ref[b, 0]))
            put(P_DADA + 3 * D * b + D, row(dss_ref[b, 1]))
            put(P_DADA + 3 * D * b + 2 * D, row(dg_ref[b]))
        _exchange_slots(rows_ref, send_sems, recv_sems)
        tot = rows_ref[0]
        for k in range(1, N_DEV):
            tot = tot + rows_ref[k]
        tot_ref[...] = tot
        gbada = tot[:, P_DADA:P_DADA + 3 * D]
        for b in range(1, nbat):
            gbada = gbada + tot[:, P_DADA + 3 * D * b:P_DADA + 3 * D * (b + 1)]
        gbada_ref[...] = gbada

    p_len = P_DADA + nbat * 3 * D
    out_shape = (jax.ShapeDtypeStruct((N_DEV, 1, p_len), F32), jax.ShapeDtypeStruct((1, p_len), F32),
                 jax.ShapeDtypeStruct((1, 3 * D), F32))
    return _pcall(body, out_shape=out_shape, in_specs=[VMEM] * 7, out_specs=(VMEM, VMEM, VMEM),
                  scratch_shapes=[pltpu.SemaphoreType.DMA((7,)), pltpu.SemaphoreType.DMA((7,))], name="small_reduce",
                  compiler_params=_params(16))(gb_rest, *gb_qkv, svec, dgate, dss)


def _make_h(x, ada, tm=512):
    t = x.shape[0]
    tps = (t // ada.shape[0]) // tm

    def body(x_ref, ada_ref, h_ref):
        h_ref[...] = (x_ref[...] * (1.0 + ada_ref[0, 1:2, :]) + ada_ref[0, 0:1, :]).astype(BF16)

    return _pcall(body, grid=(t // tm,), out_shape=jax.ShapeDtypeStruct((t, D), BF16),
                  in_specs=[pl.BlockSpec((tm, D), lambda i: (i, 0)), pl.BlockSpec((1, 3, D), lambda i: (i // tps, 0, 0))],
                  out_specs=pl.BlockSpec((tm, D), lambda i: (i, 0)), name="make_h",
                  compiler_params=_params(32, ("parallel",)))(x, ada)


def _project(h, w_in_t, b_in3):
    t = h.shape[0]

    def body(h_ref, w_ref, b_ref, qkv_ref, rest_ref):
        j = pl.program_id(0)
        v = _nt(h_ref[...], w_ref[...]) + b_ref[0]

        @pl.when(j < N_QKV)
        def _():
            qkv_ref[0] = v.astype(BF16)

        @pl.when(j >= N_QKV)
        def _():
            rest_ref[0] = v

    return _pcall(
        body, grid=(N_SLAB,),
        out_shape=(jax.ShapeDtypeStruct((N_QKV, t, SLAB), BF16), jax.ShapeDtypeStruct((N_REST, t, SLAB), F32)),
        in_specs=[pl.BlockSpec((t, D), lambda j: (0, 0)), pl.BlockSpec((SLAB, D), lambda j: (j, 0)),
                  pl.BlockSpec((1, 1, SLAB), lambda j: (j, 0, 0))],
        out_specs=(pl.BlockSpec((1, t, SLAB), lambda j: (jnp.minimum(j, N_QKV - 1), 0, 0)),
                   pl.BlockSpec((1, t, SLAB), lambda j: (jnp.maximum(j - N_QKV, 0), 0, 0))),
        name="project", compiler_params=_params(48, ("arbitrary",)))(h, w_in_t, b_in3)


def _bias_tables(g):
    window, dil = GROUPS[g]
    span = window // dil
    qi = jnp.arange(BLK)[:, None]
    kj = jnp.arange(2 * BLK)[None, :]
    delta = qi + BLK - kj
    valid = (delta >= 0) & (delta <= span)
    heads = jnp.arange(4, dtype=F32) + 4.0 * g
    slopes = 2.0 ** (-8.0 * (heads + 1.0) / 12.0)
    bias = -slopes[:, None, None] * (delta * dil).astype(F32)[None]
    return jnp.where(valid[None], bias, -1e30).reshape(4 * BLK, 2 * BLK)


def _head_masks(shape):
    lane = lax.broadcasted_iota(jnp.int32, shape, 1)
    return [(lane >= 64 * h) & (lane < 64 * (h + 1)) for h in range(4)]


def _stack_heads(v, masks):
    return jnp.concatenate([jnp.where(masks[h], v, jnp.zeros_like(v)) for h in range(4)], axis=0)


def _unstack_heads(v4, masks):
    out = jnp.where(masks[0], v4[0:BLK], 0.0)
    for h in range(1, 4):
        out = jnp.where(masks[h], v4[BLK * h:BLK * (h + 1)], out)
    return out


def _regroup(load_half, dst_ref, stage_ref, n, dil):
    for hlf in range(2):
        stage_ref[hlf] = load_half(hlf)

    def residue(r, carry):
        for hlf in range(2):
            dst_ref[pl.ds(pl.multiple_of(r * n, BLK), n), pl.ds(128 * hlf, 128)] = (
                stage_ref[hlf, pl.ds(r, n, stride=dil), :].astype(dst_ref.dtype))
        return carry

    lax.fori_loop(0, dil, residue, 0)


def _store_block(nat_ref, r, i, val, dil):
    for hlf in range(2):
        nat_ref[hlf, pl.ds(r + dil * BLK * i, BLK, stride=dil), :] = val[:, 128 * hlf:128 * (hlf + 1)]


def _for_blocks(block, dil, nblk):
    def residue(r, carry):
        block(r, 0, True)
        if nblk > 1:
            def loop(i, c):
                block(r, i, False)
                return c
            lax.fori_loop(1, nblk, loop, 0)
        return carry

    if dil == 1:
        residue(0, 0)
    else:
        lax.fori_loop(0, dil, residue, 0)


def _attn_forward(qkv, g, nbat):
    t = qkv.shape[1]
    seq = t // nbat
    dil = GROUPS[g][1]
    n = seq // dil
    nblk = n // BLK
    qkv4 = qkv.reshape(3, 3, t, SLAB)

    def body(qkv_ref, bias_ref, ol_ref, *scratch):
        masks = _head_masks((BLK, SLAB))
        if dil > 1:
            stage, qd, kd, vd, nat_o, nat_l = scratch
            for which, dst in enumerate((qd, kd, vd)):
                _regroup(lambda hlf, which=which: qkv_ref[which, 0, :, pl.ds(128 * hlf, 128)].astype(F32), dst, stage, n, dil)
        else:
            qd, kd, vd = qkv_ref.at[0, 0], qkv_ref.at[1, 0], qkv_ref.at[2, 0]

        def block(r, i, first):
            base = r * n
            qs = pl.ds(pl.multiple_of(base + i * BLK, BLK), BLK)
            ks = pl.ds(pl.multiple_of(base, BLK), BLK) if first else pl.ds(pl.multiple_of(base + (i - 1) * BLK, BLK), 2 * BLK)
            q, kk, vv = qd[qs, :], kd[ks, :], vd[ks, :]
            bias = bias_ref[:, pl.ds(BLK, BLK)] if first else bias_ref[...]
            s = _nt(_stack_heads(q, masks), kk) * 0.125 + bias
            m = jnp.max(s, axis=1, keepdims=True)
            p = jnp.exp(s - m)
            den = jnp.sum(p, axis=1, keepdims=True)
            out = _unstack_heads(_nn((p * (1.0 / den)).astype(BF16), vv), masks)
            lse = _unstack_heads(jnp.broadcast_to(m + jnp.log(den), (4 * BLK, SLAB)), masks)
            if dil > 1:
                _store_block(nat_o, r, i, out, dil)
                _store_block(nat_l, r, i, lse, dil)
            else:
                ol_ref[0, qs, :] = out
                ol_ref[1, qs, :] = lse

        _for_blocks(block, dil, nblk)
        if dil > 1:
            for hlf in range(2):
                ol_ref[0, :, pl.ds(128 * hlf, 128)] = nat_o[hlf]
                ol_ref[1, :, pl.ds(128 * hlf, 128)] = nat_l[hlf]

    scratch = []
    if dil > 1:
        scratch = [pltpu.VMEM((2, seq, 128), F32)] + [pltpu.VMEM((seq, SLAB), BF16)] * 3 + [pltpu.VMEM((2, seq, 128), F32)] * 2
    return _pcall(
        body, grid=(nbat,), out_shape=jax.ShapeDtypeStruct((2, t, SLAB), F32),
        in_specs=[pl.BlockSpec((3, 1, seq, SLAB), lambda b: (0, g, b, 0)),
                  pl.BlockSpec((4 * BLK, 2 * BLK), lambda b: (0, 0))],
        out_specs=pl.BlockSpec((2, seq, SLAB), lambda b: (0, b, 0)), scratch_shapes=scratch,
        name=f"attn_forward_{g}", compiler_params=_params(40, ("parallel",)))(qkv4, _bias_tables(g))


def _attn_backward(qkv, do_attn, ol_tot, dproj, g, nbat):
    t = qkv.shape[1]
    seq = t // nbat
    dil = GROUPS[g][1]
    n = seq // dil
    nblk = n // BLK
    qkv4 = qkv.reshape(3, 3, t, SLAB)
    dp4 = dproj.reshape(DP_SLABS // 3, 3, t, SLAB)

    def body(qkv_ref, do_ref, ol_ref, bias_ref, dp_in, dp_ref, gb_ref, dk_acc, dv_acc, *scratch):
        del dp_in
        masks = _head_masks((BLK, SLAB))

        @pl.when(pl.program_id(0) == 0)
        def _():
            gb_ref[...] = jnp.zeros_like(gb_ref)

        dk_acc[...] = jnp.zeros_like(dk_acc)
        dv_acc[...] = jnp.zeros_like(dv_acc)
        if dil > 1:
            stage, qd, kd, vd, dod, prodd, lsed, nat = scratch
            lanes = lambda hlf: pl.ds(128 * hlf, 128)
            for which, dst in enumerate((qd, kd, vd)):
                _regroup(lambda hlf, which=which: qkv_ref[which, 0, :, lanes(hlf)].astype(F32), dst, stage, n, dil)
            _regroup(lambda hlf: do_ref[:, lanes(hlf)].astype(F32), dod, stage, n, dil)
            _regroup(lambda hlf: do_ref[:, lanes(hlf)].astype(F32) * ol_ref[0, :, lanes(hlf)], prodd, stage, n, dil)
            _regroup(lambda hlf: ol_ref[1, :, lanes(hlf)], lsed, stage, n, dil)
        else:
            qd, kd, vd = qkv_ref.at[0, 0], qkv_ref.at[1, 0], qkv_ref.at[2, 0]

        def block(r, i, first):
            base = r * n
            qs = pl.ds(pl.multiple_of(base + i * BLK, BLK), BLK)
            ks = pl.ds(pl.multiple_of(base, BLK), BLK) if first else pl.ds(pl.multiple_of(base + (i - 1) * BLK, BLK), 2 * BLK)
            q, kk, vv = qd[qs, :], kd[ks, :], vd[ks, :]
            if dil > 1:
                do, prod, lse = dod[qs, :], prodd[qs, :], lsed[qs, :]
            else:
                do = do_ref[qs, :]
                prod = do.astype(F32) * ol_ref[0, qs, :]
                lse = ol_ref[1, qs, :]
            q4, do4 = _stack_heads(q, masks), _stack_heads(do, masks)
            bias = bias_ref[:, pl.ds(BLK, BLK)] if first else bias_ref[...]
            lse4 = jnp.concatenate([lse[:, 64 * h:64 * h + 1] for h in range(4)], axis=0)
            delta4 = jnp.concatenate([jnp.sum(jnp.where(masks[h], prod, 0.0), axis=1, keepdims=True) for h in range(4)], axis=0)
            p = jnp.exp(_nt(q4, kk) * 0.125 + bias - lse4)
            ds = (p * (_nt(do4, vv) - delta4)).astype(BF16)
            dv_acc[ks, :] += _tn(p.astype(BF16), do4)
            dk_acc[ks, :] += _tn(ds, q4) * 0.125
            dq = _unstack_heads(_nn(ds, kk), masks) * 0.125
            if dil > 1:
                _store_block(nat, r, i, dq, dil)
            else:
                dp_ref[0, 0, qs, :] = dq.astype(BF16)
            gb_ref[0] += _part8(dq)

        _for_blocks(block, dil, nblk)
        gb_ref[1] += _part8(dk_acc[...])
        gb_ref[2] += _part8(dv_acc[...])
        if dil > 1:
            def flush(which):
                for hlf in range(2):
                    dp_ref[which, 0, :, pl.ds(128 * hlf, 128)] = nat[hlf].astype(BF16)

            def to_token_order(acc_ref):
                def residue(r, carry):
                    for hlf in range(2):
                        nat[hlf, pl.ds(r, n, stride=dil), :] = acc_ref[pl.ds(pl.multiple_of(r * n, BLK), n), pl.ds(128 * hlf, 128)]
                    return carry
                lax.fori_loop(0, dil, residue, 0)

            flush(0)
            to_token_order(dk_acc)
            flush(1)
            to_token_order(dv_acc)
            flush(2)
        else:
            dp_ref[1, 0] = dk_acc[...].astype(BF16)
            dp_ref[2, 0] = dv_acc[...].astype(BF16)

    scratch = [pltpu.VMEM((seq, SLAB), F32)] * 2
    if dil > 1:
        scratch += ([pltpu.VMEM((2, seq, 128), F32)] + [pltpu.VMEM((seq, SLAB), BF16)] * 4 + [pltpu.VMEM((seq, SLAB), F32)] * 2
                    + [pltpu.VMEM((2, seq, 128), F32)])
    dp, gb = _pcall(
        body, grid=(nbat,),
        out_shape=(jax.ShapeDtypeStruct(dp4.shape, BF16), jax.ShapeDtypeStruct((3, 8, SLAB), F32)),
        in_specs=[pl.BlockSpec((3, 1, seq, SLAB), lambda b: (0, g, b, 0)),
                  pl.BlockSpec((seq, SLAB), lambda b: (b, 0)),
                  pl.BlockSpec((2, seq, SLAB), lambda b: (0, b, 0)),
                  pl.BlockSpec((4 * BLK, 2 * BLK), lambda b: (0, 0)), ANY],
        out_specs=(pl.BlockSpec((3, 1, seq, SLAB), lambda b: (DP_SLABS // 9 - 1, g, b, 0)),
                   pl.BlockSpec((3, 8, SLAB), lambda b: (0, 0, 0))),
        scratch_shapes=scratch, input_output_aliases={4: 0}, name=f"attn_backward_{g}",
        compiler_params=_params(48, ("arbitrary",)))(qkv4, do_attn, ol_tot, _bias_tables(g), dp4)
    return dp.reshape(DP_SLABS, t, SLAB), gb


def _mid(rest, ols, x, tgt, ada, cw, b_out, ln_g, ln_b, w_pa_t, w_pb, w_out, tm=256):
    t = x.shape[0]
    nbat = ada.shape[0]
    nt = t // tm
    tps = nt // nbat

    def body(rest_ref, halo_ref, ol0_ref, ol1_ref, ol2_ref, x_ref, t_ref, ada_ref, cw_ref, bout_ref, lng_ref, lnb_ref,
             wpat_ref, wpb_ref, wout_ref,
             dp_ref, gx0_ref, doa_ref, olt_ref, mg_ref, dof_ref, bbs_ref, dyc_ref, a_ref, dya_ref,
             gbr_ref, sv_ref, dgate_ref, carry_ref):
        i = pl.program_id(0)
        ti = nt - 1 - i
        pos = ti % tps

        @pl.when(i == 0)
        def _():
            gbr_ref[...] = jnp.zeros_like(gbr_ref)
            sv_ref[...] = jnp.zeros_like(sv_ref)

        @pl.when(pos == tps - 1)
        def _():
            dgate_ref[...] = jnp.zeros_like(dgate_ref)
            carry_ref[...] = jnp.zeros_like(carry_ref)

        row = lax.broadcasted_iota(jnp.int32, (tm, SLAB), 0)
        halo_on = (pos > 0).astype(F32)

        def cols(s):
            return pl.ds(SLAB * s, SLAB)

        l0, l1, l2 = ol0_ref[1], ol1_ref[1], ol2_ref[1]
        mx = jnp.maximum(jnp.maximum(l0, l1), l2)
        e0, e1, e2 = jnp.exp(l0 - mx), jnp.exp(l1 - mx), jnp.exp(l2 - mx)
        den = e0 + e1 + e2
        o_attn = (e0 * ol0_ref[0] + e1 * ol1_ref[0] + e2 * ol2_ref[0]) * (1.0 / den)
        olt_ref[0] = o_attn
        olt_ref[1] = mx + jnp.log(den)
        z_a = rest_ref[R_ZA]
        sg_za = _sigmoid(z_a)
        a_ref[...] = (o_attn * z_a * sg_za).astype(BF16)
        y_attn = _nt(a_ref[...], wpat_ref[...])

        def conv_parts(s):
            ux, gc = rest_ref[R_UX + s], rest_ref[R_GC + s]
            u = gc * ux
            hu = halo_ref[R_GC + s] * halo_ref[R_UX + s] * halo_on
            u1 = jnp.where(row == 0, hu[7:8], pltpu.roll(u, 1, 0))
            u2 = jnp.where(row == 0, hu[6:7], jnp.where(row == 1, hu[7:8], pltpu.roll(u, 2, 0)))
            conv = cw_ref[0:1, cols(s)] * u2 + cw_ref[1:2, cols(s)] * u1 + cw_ref[2:3, cols(s)] * u
            zc = rest_ref[R_ZC + s]
            sg = _sigmoid(zc)
            return ux, gc, u, u1, u2, conv, zc, sg

        for s in range(4):
            ux, gc, u, u1, u2, conv, zc, sg = conv_parts(s)
            bbs_ref[:, cols(s)] = (rest_ref[R_GB + s] * conv * (zc * sg)).astype(BF16)
        y_conv = _nn(bbs_ref[...], wpb_ref[...])

        for s in range(4):
            s_a, s_b = _sigmoid(rest_ref[R_GA + s]), _sigmoid(rest_ref[R_GBM + s])
            mg_ref[:, cols(s)] = (s_a * y_attn[:, SLAB * s:SLAB * (s + 1)] + s_b * y_conv[:, SLAB * s:SLAB * (s + 1)]).astype(BF16)
        o = _nn(mg_ref[...], wout_ref[...]) + bout_ref[...]
        gate = ada_ref[0, 2:3, :]
        r = ALPHA * x_ref[...] + gate * o
        mu = jnp.mean(r, axis=1, keepdims=True)
        rc = r - mu
        rstd = lax.rsqrt(jnp.mean(rc * rc, axis=1, keepdims=True) + LN_EPS)
        xhat = rc * rstd
        err = xhat * lng_ref[...] + lnb_ref[...] - t_ref[...]
        sv_ref[6] += _part8(err * err)
        dy = err * (1.0 / D)
        sv_ref[0] += _part8(dy * xhat)
        sv_ref[1] += _part8(dy)
        dxh = dy * lng_ref[...]
        dr = rstd * (dxh - jnp.mean(dxh, axis=1, keepdims=True) - xhat * jnp.mean(dxh * xhat, axis=1, keepdims=True))
        gx0_ref[...] = ALPHA * dr
        dgate_ref[0] += _part8(dr * o)
        do_ = dr * gate
        sv_ref[2] += _part8(do_)
        dof_ref[...] = do_.astype(BF16)
        dmerged = _nt(dof_ref[...], wout_ref[...])
        for s in range(4):
            s_a, s_b = _sigmoid(rest_ref[R_GA + s]), _sigmoid(rest_ref[R_GBM + s])
            dm = dmerged[:, SLAB * s:SLAB * (s + 1)]
            ya, yc = y_attn[:, SLAB * s:SLAB * (s + 1)], y_conv[:, SLAB * s:SLAB * (s + 1)]
            dya_ref[:, cols(s)] = (dm * s_a).astype(BF16)
            dyc_ref[:, cols(s)] = (dm * s_b).astype(BF16)
            dga = dm * ya * s_a * (1.0 - s_a)
            dgb = dm * yc * s_b * (1.0 - s_b)
            dp_ref[R_GA + s] = dga.astype(BF16)
            dp_ref[R_GBM + s] = dgb.astype(BF16)
            gbr_ref[R_GA + s] += _part8(dga)
            gbr_ref[R_GBM + s] += _part8(dgb)

        da = _nn(dya_ref[...], wpat_ref[...])
        doa_ref[...] = (da * z_a * sg_za).astype(BF16)
        dza = da * o_attn * (sg_za * (1.0 + z_a * (1.0 - sg_za)))
        dp_ref[R_ZA] = dza.astype(BF16)
        gbr_ref[R_ZA] += _part8(dza)

        dbb = _nt(dyc_ref[...], wpb_ref[...])
        for s in range(4):
            ux, gc, u, u1, u2, conv, zc, sg = conv_parts(s)
            gb = rest_ref[R_GB + s]
            d_b = dbb[:, SLAB * s:SLAB * (s + 1)]
            szc = zc * sg
            dgb_ = d_b * conv * szc
            dconv = d_b * gb * szc
            dzc = d_b * gb * conv * (sg * (1.0 + zc * (1.0 - sg)))
            sv_ref[3, :, cols(s)] += _part8(dconv * u2)
            sv_ref[4, :, cols(s)] += _part8(dconv * u1)
            sv_ref[5, :, cols(s)] += _part8(dconv * u)
            nxt = carry_ref[:, cols(s)]
            d1 = jnp.where(row == tm - 1, nxt[0:1], pltpu.roll(dconv, tm - 1, 0))
            d2 = jnp.where(row == tm - 1, nxt[1:2], jnp.where(row == tm - 2, nxt[0:1], pltpu.roll(dconv, tm - 2, 0)))
            carry_ref[:, cols(s)] = dconv[0:8]
            du = cw_ref[2:3, cols(s)] * dconv + cw_ref[1:2, cols(s)] * d1 + cw_ref[0:1, cols(s)] * d2
            dgc, dux = du * ux, du * gc
            for slab, val in ((R_GB + s, dgb_), (R_ZC + s, dzc), (R_GC + s, dgc), (R_UX + s, dux)):
                dp_ref[slab] = val.astype(BF16)
                gbr_ref[slab] += _part8(val)

    def tile(i):
        return nt - 1 - i

    row_blk = lambda i: (tile(i), 0)
    slab_blk = lambda i: (0, tile(i), 0)
    const2 = lambda i: (0, 0)
    const3 = lambda i: (0, 0, 0)
    in_specs = [
        pl.BlockSpec((N_REST, tm, SLAB), slab_blk),
        pl.BlockSpec((N_REST, 8, SLAB), lambda i: (0, jnp.maximum(tile(i) * (tm // 8) - 1, 0), 0)),
        pl.BlockSpec((2, tm, SLAB), slab_blk), pl.BlockSpec((2, tm, SLAB), slab_blk), pl.BlockSpec((2, tm, SLAB), slab_blk),
        pl.BlockSpec((tm, D), row_blk), pl.BlockSpec((tm, D), row_blk),
        pl.BlockSpec((1, 3, D), lambda i: (tile(i) // tps, 0, 0)),
        pl.BlockSpec((3, D), const2), pl.BlockSpec((1, D), const2), pl.BlockSpec((1, D), const2), pl.BlockSpec((1, D), const2),
        pl.BlockSpec((D, SLAB), const2), pl.BlockSpec((D, D), const2), pl.BlockSpec((D, D), const2)]
    bf_rows = lambda: jax.ShapeDtypeStruct((t, D), BF16)
    out_shape = (
        jax.ShapeDtypeStruct((DP_SLABS, t, SLAB), BF16), jax.ShapeDtypeStruct((t, D), F32),
        jax.ShapeDtypeStruct((t, SLAB), BF16), jax.ShapeDtypeStruct((2, t, SLAB), F32),
        bf_rows(), bf_rows(), bf_rows(), bf_rows(), jax.ShapeDtypeStruct((t, SLAB), BF16), bf_rows(),
        jax.ShapeDtypeStruct((N_REST, 8, SLAB), F32), jax.ShapeDtypeStruct((7, 8, D), F32),
        jax.ShapeDtypeStruct((nbat, 8, D), F32))
    out_specs = (
        pl.BlockSpec((N_REST, tm, SLAB), slab_blk), pl.BlockSpec((tm, D), row_blk),
        pl.BlockSpec((tm, SLAB), row_blk), pl.BlockSpec((2, tm, SLAB), slab_blk),
        pl.BlockSpec((tm, D), row_blk), pl.BlockSpec((tm, D), row_blk), pl.BlockSpec((tm, D), row_blk),
        pl.BlockSpec((tm, D), row_blk), pl.BlockSpec((tm, SLAB), row_blk), pl.BlockSpec((tm, D), row_blk),
        pl.BlockSpec((N_REST, 8, SLAB), const3), pl.BlockSpec((7, 8, D), const3),
        pl.BlockSpec((1, 8, D), lambda i: (tile(i) // tps, 0, 0)))
    return _pcall(body, grid=(nt,), out_shape=out_shape, in_specs=in_specs, out_specs=out_specs,
                  scratch_shapes=[pltpu.VMEM((8, D), F32)], name="mid",
                  compiler_params=_params(56, ("arbitrary",)))(
        rest, rest, *ols, x, tgt, ada, cw, b_out, ln_g, ln_b, w_pa_t, w_pb, w_out)


def _tn_matmul(lhs, rhs, lhs_spec, n_steps, out_rows, out_index, name, after):
    t, n = rhs.shape

    def body(l_ref, r_ref, after_ref, o_ref):
        del after_ref
        o_ref[...] = _tn(l_ref[0] if len(l_ref.shape) == 3 else l_ref[...], r_ref[...])

    return _pcall(body, grid=(n_steps,), out_shape=jax.ShapeDtypeStruct((out_rows, n), F32),
                  in_specs=[lhs_spec, pl.BlockSpec((t, n), lambda j: (0, 0)), ANY],
                  out_specs=pl.BlockSpec((SLAB, n), out_index), name=name,
                  compiler_params=_params(48, ("parallel",)))(lhs, rhs, after)


def _grad_rows_2d(lhs, rhs, name, after):
    t, k = lhs.shape
    return _tn_matmul(lhs, rhs, pl.BlockSpec((t, SLAB), lambda j: (0, j)), k // SLAB, k, lambda j: (j, 0), name, after)


def _w_row_block(j):
    return (j + N_QKV) % N_SLAB


def _dp_slab(j):
    return jnp.where(j < N_REST, j, j + 2)


def _grad_w_in_t(dproj, h):
    t = h.shape[0]
    return _tn_matmul(dproj, h, pl.BlockSpec((1, t, SLAB), lambda j: (_dp_slab(j), 0, 0)), N_SLAB, D_IN,
                      lambda j: (_w_row_block(j), 0), "grad_w_in", h)


def _grad_h(dproj, w_in_t, gx0, x, ada, after, tm=512):
    t = x.shape[0]
    nbat = ada.shape[0]
    tps = (t // nbat) // tm

    def body(dp_ref, w_ref, gx0_ref, x_ref, ada_ref, after_ref, gx_ref, dss_ref):
        del after_ref
        i = pl.program_id(0)
        dh = None
        for j in range(N_SLAB):
            slab = j if j < N_REST else j + 2
            part = _nn(dp_ref[slab], w_ref[pl.ds(SLAB * ((j + N_QKV) % N_SLAB), SLAB), :])
            dh = part if dh is None else dh + part
        gx_ref[...] = gx0_ref[...] + dh * (1.0 + ada_ref[0, 1:2, :])

        @pl.when((i % tps) == 0)
        def _():
            dss_ref[...] = jnp.zeros_like(dss_ref)

        dss_ref[0, 0] += _part8(dh)
        dss_ref[0, 1] += _part8(dh * x_ref[...])

    return _pcall(
        body, grid=(t // tm,),
        out_shape=(jax.ShapeDtypeStruct((t, D), F32), jax.ShapeDtypeStruct((nbat, 2, 8, D), F32)),
        in_specs=[pl.BlockSpec((DP_SLABS, tm, SLAB), lambda i: (0, i, 0)),
                  pl.BlockSpec((D_IN, D), lambda i: (0, 0), pipeline_mode=pl.Buffered(1)),
                  pl.BlockSpec((tm, D), lambda i: (i, 0)), pl.BlockSpec((tm, D), lambda i: (i, 0)),
                  pl.BlockSpec((1, 3, D), lambda i: (i // tps, 0, 0)), ANY],
        out_specs=(pl.BlockSpec((tm, D), lambda i: (i, 0)),
                   pl.BlockSpec((1, 2, 8, D), lambda i: (i // tps, 0, 0, 0))),
        name="grad_h", compiler_params=_params(60, ("arbitrary",)))(dproj, w_in_t, gx0, x, ada, after)


def _chip(m):
    x, y, _ = _my_position()
    return (x ^ ((m >> 1) & 1), y ^ (m & 1))


def _exchange_siblings(grads):
    n = len(grads)

    def body(*refs):
        copies = _sibling_copies(refs[:n], refs[n:2 * n], refs[2 * n], refs[2 * n + 1])
        for cp in copies:
            cp.start()
        for cp in copies:
            cp.wait()

    return _pcall(body, out_shape=tuple(_sibling_zones(grads)), in_specs=[ANY] * n, out_specs=(ANY,) * n,
                  name="exchange_siblings", scratch_shapes=[pltpu.SemaphoreType.DMA((4 * n,))] * 2)(*grads)


def _sibling_zones(grads):
    return [jax.ShapeDtypeStruct((4, g.shape[0] // N_DEV, g.shape[1]), g.dtype) for g in grads]


def _sibling_copies(srcs, lands, send_sems, recv_sems):
    x, y, c = _my_position()
    copies = []
    for a, (src, land) in enumerate(zip(srcs, lands)):
        rows = land.shape[1]
        for m in range(4):
            dev = _flat(*_chip(m), 1 - c)
            copies.append(pltpu.make_async_remote_copy(
                src_ref=src.at[pl.ds(pl.multiple_of(dev * rows, 8), rows), :], dst_ref=land.at[m],
                send_sem=send_sems.at[4 * a + m], recv_sem=recv_sems.at[4 * a + m], device_id=(x, y, 1 - c),
                device_id_type=MESH))
    return copies


def _chip_copies(srcs, lands, send_sems, recv_sems):
    _, _, c = _my_position()
    return [pltpu.make_async_remote_copy(
        src_ref=srcs[a].at[m - 1], dst_ref=lands[a].at[m - 1], send_sem=send_sems.at[3 * a + m - 1],
        recv_sem=recv_sems.at[3 * a + m - 1], device_id=(*_chip(m), c), device_id_type=MESH)
        for a in range(len(srcs)) for m in range(1, 4)]


HBM = pl.BlockSpec(memory_space=pltpu.HBM)
SEM = pl.BlockSpec(memory_space=pltpu.SEMAPHORE)
SPLIT_COPY = pltpu.CompilerParams(has_side_effects=pltpu.SideEffectType.DATAFLOW_SIDE_EFFECTING)


def _start_copies(make_copies, n_sems, srcs, zones, name):
    n = len(srcs)

    def body(*refs):
        for cp in make_copies(refs[:n], refs[n:2 * n], refs[2 * n], refs[2 * n + 1]):
            cp.start()
        refs[-1][...] = jnp.zeros_like(refs[-1])

    hbm = tuple(pltpu.HBM(b.shape, b.dtype) for b in list(srcs) + list(zones))
    out_shape = (pltpu.SemaphoreType.DMA((n_sems,)), pltpu.SemaphoreType.DMA((n_sems,))) + hbm + (jax.ShapeDtypeStruct((8, 128), F32),)
    operands = [pltpu.with_memory_space_constraint(b, pltpu.HBM) for b in srcs]
    operands += [pltpu.with_memory_space_constraint(lax.empty(z.shape, z.dtype), pltpu.HBM) for z in zones]
    res = _pcall(body, out_shape=out_shape, in_specs=[HBM] * (2 * n), out_specs=(SEM, SEM) + (HBM,) * (2 * n) + (VMEM,),
                 input_output_aliases={i: 2 + i for i in range(2 * n)}, name=name, compiler_params=SPLIT_COPY)(*operands)
    return (res[0], res[1], res[2:2 + n], res[2 + n:2 + 2 * n]), res[-1]


def _wait_copies(make_copies, flight, after, name):
    send_sems, recv_sems, srcs, zones = flight
    n = len(srcs)

    def body(*refs):
        for cp in make_copies(refs[:n], refs[n:2 * n], refs[2 * n], refs[2 * n + 1]):
            cp.wait_send()
            cp.wait_recv()

    hbm = tuple(pltpu.HBM(b.shape, b.dtype) for b in list(srcs) + list(zones))
    res = _pcall(body, out_shape=hbm, in_specs=[HBM] * (2 * n) + [SEM, SEM, ANY], out_specs=(HBM,) * (2 * n),
                 input_output_aliases={i: i for i in range(2 * n)}, name=name, compiler_params=SPLIT_COPY)(
        *srcs, *zones, send_sems, recv_sems, after)
    return res[:n], res[n:]


def _pair_sums(devs, grads, lands, n_steps, name):
    n = len(grads)
    rows = [l.shape[1] for l in lands]
    rbs = [r // n_steps for r in rows]

    def body(devs_ref, *refs):
        del devs_ref
        g_refs, land_refs, outs = refs[:4 * n], refs[4 * n:5 * n], refs[5 * n:]
        for a in range(n):
            outs[2 * a][...] = g_refs[4 * a][...] + land_refs[a][0]
            for m in range(1, 4):
                outs[2 * a + 1][m - 1] = (g_refs[4 * a + m][...] + land_refs[a][m]).astype(BF16)

    def block_of(m, per_dev):
        return lambda i, devs_ref: (devs_ref[m] * per_dev + i, 0)

    in_specs = [pl.BlockSpec((rb, l.shape[2]), block_of(m, n_steps)) for rb, l in zip(rbs, lands) for m in range(4)]
    in_specs += [pl.BlockSpec((4, rb, l.shape[2]), lambda i, devs_ref: (0, i, 0)) for rb, l in zip(rbs, lands)]
    out_shape, out_specs = [], []
    for rb, l in zip(rbs, lands):
        out_shape += [jax.ShapeDtypeStruct(l.shape[1:], F32), jax.ShapeDtypeStruct((3,) + l.shape[1:], BF16)]
        out_specs += [pl.BlockSpec((rb, l.shape[2]), lambda i, devs_ref: (i, 0)),
                      pl.BlockSpec((3, rb, l.shape[2]), lambda i, devs_ref: (0, i, 0))]
    grid_spec = pltpu.PrefetchScalarGridSpec(num_scalar_prefetch=1, grid=(n_steps,), in_specs=in_specs, out_specs=tuple(out_specs))
    res = _pcall(body, grid_spec=grid_spec, out_shape=tuple(out_shape), name=name,
                 compiler_params=_params(48, ("parallel",)))(devs, *[g for g in grads for _ in range(4)], *lands)
    return res[0::2], res[1::2]


def _final_sums(mine, lands, n_steps, name):
    n = len(mine)
    rbs = [o.shape[0] // n_steps for o in mine]

    def body(*refs):
        mine_refs, land_refs, outs = refs[:n], refs[n:2 * n], refs[2 * n:]
        for a in range(n):
            tot = mine_refs[a][...]
            for m in range(3):
                tot = tot + land_refs[a][m].astype(F32)
            outs[a][...] = tot

    in_specs = ([pl.BlockSpec((rb, o.shape[1]), lambda i: (i, 0)) for rb, o in zip(rbs, mine)]
                + [pl.BlockSpec((3, rb, o.shape[1]), lambda i: (0, i, 0)) for rb, o in zip(rbs, mine)])
    out_specs = tuple(pl.BlockSpec((rb, o.shape[1]), lambda i: (i, 0)) for rb, o in zip(rbs, mine))
    out_shape = tuple(jax.ShapeDtypeStruct(o.shape, F32) for o in mine)
    return _pcall(body, grid=(n_steps,), out_shape=out_shape, in_specs=in_specs, out_specs=out_specs, name=name,
                  compiler_params=_params(32, ("parallel",)))(*mine, *lands)


def _reduce_scatter_begin(big, small_after_start):
    flight, token = _start_copies(_sibling_copies, 4, [big], _sibling_zones([big]), "siblings_start")
    small = small_after_start(token)
    (big,), big_lands = _wait_copies(_sibling_copies, flight, small[-1], "siblings_wait")
    small_lands = _exchange_siblings(small)
    c = lax.axis_index("c")
    devs = jnp.stack([_flat(*_chip(m), c) for m in range(4)]).astype(jnp.int32)
    big_mine, big_send = _pair_sums(devs, [big], big_lands, 4, "pair_sums_w_in")
    small_mine, small_send = _pair_sums(devs, small, small_lands, 1, "pair_sums_rest")
    bufs = list(big_send) + list(small_send)
    flight, token = _start_copies(_chip_copies, 3 * len(bufs), bufs, bufs, "chips_start")
    return (flight, list(big_mine) + list(small_mine)), token


def _reduce_scatter_end(state, after):
    flight, mine = state
    _, got = _wait_copies(_chip_copies, flight, after, "chips_wait")
    big = _final_sums(mine[:1], got[:1], 4, "final_sums_w_in")
    small = _final_sums(mine[1:], got[1:], 1, "final_sums_rest")
    return list(big) + list(small)


def _adamw(w, g, m, v):
    m_new = B1 * m + (1.0 - B1) * g
    v_new = B2 * v + (1.0 - B2) * (g * g)
    m_hat = m_new / (1.0 - B1 ** STEP)
    v_hat = v_new / (1.0 - B2 ** STEP)
    delta = -LR * (m_hat / (jnp.sqrt(v_hat) + EPS) + WD * w)
    return delta, m_new, v_new


def _adam_rows(g, w, m, v, n_steps, name):
    rows, ncol = w.shape
    blk = pl.BlockSpec((rows // n_steps, ncol), lambda i: (i, 0))

    def body(g_ref, w_ref, m_ref, v_ref, d_ref, mo_ref, vo_ref):
        d_ref[...], mo_ref[...], vo_ref[...] = _adamw(w_ref[...], g_ref[...], m_ref[...], v_ref[...])

    shape = jax.ShapeDtypeStruct(w.shape, F32)
    return _pcall(body, grid=(n_steps,), out_shape=(shape,) * 3, in_specs=[blk] * 4, out_specs=(blk,) * 3, name=name,
                  compiler_params=_params(32, ("parallel",)))(g, w, m, v)


def _adam_transposed(g_t, w, m, v, name):
    n, k = g_t.shape
    rb = min(k, 128)

    def body(gt_ref, w_ref, m_ref, v_ref, g_ref, d_ref, mo_ref, vo_ref):
        for src, skip, dst, size in _column_chunks(n):
            sl = pl.ds(dst, size)
            g = gt_ref[pl.ds(src, 128), :].T[:, skip:]
            delta, m_new, v_new = _adamw(w_ref[:, sl], g, m_ref[:, sl], v_ref[:, sl])
            g_ref[:, sl], d_ref[:, sl], mo_ref[:, sl], vo_ref[:, sl] = g, delta, m_new, v_new

    shape = jax.ShapeDtypeStruct(w.shape, F32)
    rows = pl.BlockSpec((rb, n), lambda i: (i, 0))
    return _pcall(body, grid=(k // rb,), out_shape=(shape,) * 4,
                  in_specs=[pl.BlockSpec((n, rb), lambda i: (0, i)), rows, rows, rows], out_specs=(rows,) * 4, name=name,
                  compiler_params=_params(32, ("parallel",)))(g_t, w, m, v)


def _adam_many(items, name):
    n = len(items)

    def body(*refs):
        ins, outs = refs[:4 * n], refs[4 * n:]
        for a in range(n):
            w_ref, g_ref, m_ref, v_ref = ins[4 * a:4 * a + 4]
            delta, m_new, v_new = _adamw(w_ref[...], g_ref[...], m_ref[...], v_ref[...])
            outs[3 * a][...], outs[3 * a + 1][...], outs[3 * a + 2][...] = delta, m_new, v_new

    out_shape = tuple(jax.ShapeDtypeStruct(it[0].shape, F32) for it in items for _ in range(3))
    flat = [arr for it in items for arr in it]
    res = _pcall(body, out_shape=out_shape, in_specs=[VMEM] * (4 * n), out_specs=(VMEM,) * (3 * n), name=name,
                 compiler_params=_params(32))(*flat)
    return [tuple(res[3 * a:3 * a + 3]) for a in range(n)]


def _adam_w_ada(cact_all, dada_mine, w, m, v):
    def body(c_ref, d_ref, w_ref, m_ref, v_ref, g_ref, dl_ref, mo_ref, vo_ref):
        g = _tn(c_ref[...].astype(BF16), d_ref[...].astype(BF16))
        delta, m_new, v_new = _adamw(w_ref[...], g, m_ref[...], v_ref[...])
        g_ref[...], dl_ref[...], mo_ref[...], vo_ref[...] = g, delta, m_new, v_new

    shape = jax.ShapeDtypeStruct(w.shape, F32)
    return _pcall(body, out_shape=(shape,) * 4, in_specs=[VMEM] * 5, out_specs=(VMEM,) * 4, name="adam_w_ada",
                  compiler_params=_params(32))(cact_all, dada_mine, w, m, v)


def kernel(x, c, w_ada, b_ada, w_in, b_in, conv_w, w_proj_attn, w_proj_conv, w_out, b_out, ln_g, ln_b, loss_target, m_w_ada, m_b_ada, m_w_in, m_b_in, m_conv_w, m_w_proj_attn, m_w_proj_conv, m_w_out, m_b_out, m_ln_g, m_ln_b, v_w_ada, v_b_ada, v_w_in, v_b_in, v_conv_w, v_w_proj_attn, v_w_proj_conv, v_w_out, v_b_out, v_ln_g, v_ln_b):
    nbat, seq, _ = x.shape
    t = nbat * seq
    me = _flat(*_my_position())
    x2, tgt2 = x.reshape(t, D), loss_target.reshape(t, D)
    sq = lambda a: a.reshape(a.shape[1:])

    tr = lambda a: a[0].T
    w_in_rows = tr(w_in)
    w_in_t_s = _cast_rows(w_in_rows, 4, "cast_w_in")
    w_pa_t_s, w_pb_s, w_out_s, cact_s, cw_s = _prep(sq(w_proj_attn), sq(w_proj_conv), sq(w_out), c, sq(conv_w))
    w_in_t, w_pa_t, w_pb, w_o, cact_g, cw_g = _gather_rows([w_in_t_s, w_pa_t_s, w_pb_s, w_out_s, cact_s, cw_s])
    cact_all = cact_g.reshape(N_DEV, 8, D)[:, :nbat].reshape(N_DEV * nbat, D)
    cw = cw_g.reshape(N_DEV, 8, -1)[:, :3].transpose(1, 0, 2).reshape(3, D)

    ncol = w_ada.shape[2]
    b_ada_mine = lax.dynamic_slice(b_ada, (0, me * ncol), (1, ncol))
    ada_slots = _ada_forward(cact_all, sq(w_ada), b_ada_mine)
    ada_all = ada_slots.transpose(1, 0, 2).reshape(N_DEV * nbat, 3, D)
    ada = lax.dynamic_slice(ada_all, (me * nbat, 0, 0), (nbat, 3, D))

    h = _make_h(x2, ada)
    qkv, rest = _project(h, w_in_t, b_in.reshape(N_SLAB, 1, SLAB))
    ols = [_attn_forward(qkv, g, nbat) for g in range(3)]
    (dproj, gx0, do_attn, ol_tot, merged, do_f, bbs, dyc, a_bf, dya, gb_rest, svec, dgate) = _mid(
        rest, ols, x2, tgt2, ada, cw, b_out, ln_g, ln_b, w_pa_t, w_pb, w_o)

    gb_qkv = []
    for g in range(3):
        dproj, gb = _attn_backward(qkv, do_attn, ol_tot, dproj, g, nbat)
        gb_qkv.append(gb)
    g_w_in_t = _grad_w_in_t(dproj, h)

    def small_grads(token):
        g_w_out = _grad_rows_2d(merged, do_f, "grad_w_out", token)
        g_w_pb = _grad_rows_2d(bbs, dyc, "grad_w_proj_conv", g_w_out)
        g_w_pa_t = _grad_rows_2d(dya, a_bf, "grad_w_proj_attn", g_w_pb)
        return [g_w_out, g_w_pb, g_w_pa_t]

    rs_state, token = _reduce_scatter_begin(g_w_in_t, small_grads)
    grad_x, dss = _grad_h(dproj, w_in_t, gx0, x2, ada, token)

    rows8, tot, g_bada = _small_reduce(gb_rest, gb_qkv, svec, dgate, dss)
    g_in_t, g_out, g_pb, g_pa_t = _reduce_scatter_end(rs_state, tot)
    loss = tot[0, P_LOSS]
    dada_all = rows8[:, 0, P_DADA:].reshape(N_DEV * nbat, 3 * D)
    dada_mine = lax.dynamic_slice(dada_all, (0, me * ncol), (N_DEV * nbat, ncol))

    d_win_t, nm_win_t, nv_win_t = _adam_rows(g_in_t, w_in_rows, tr(m_w_in), tr(v_w_in), 8, "adam_w_in")
    g_win, d_win, nm_win, nv_win = g_in_t.T, d_win_t.T, nm_win_t.T, nv_win_t.T
    g_wpa, d_wpa, nm_wpa, nv_wpa = _adam_transposed(g_pa_t, sq(w_proj_attn), sq(m_w_proj_attn), sq(v_w_proj_attn), "adam_w_proj_attn")
    g_wada, d_wada, nm_wada, nv_wada = _adam_w_ada(cact_all, dada_mine, sq(w_ada), sq(m_w_ada), sq(v_w_ada))
    g_bin = tot[:, P_BIN:P_BIN + D_IN]
    g_bout = tot[:, P_BOUT:P_BOUT + D]
    g_lng = tot[:, P_LNG:P_LNG + D]
    g_lnb = tot[:, P_LNB:P_LNB + D]
    g_conv = lax.dynamic_slice(tot[:, P_CONV:P_CONV + 3 * D].reshape(3, D), (0, me * cw_s.shape[1]), (3, cw_s.shape[1]))
    upd = _adam_many([
        (sq(w_proj_conv), g_pb, sq(m_w_proj_conv), sq(v_w_proj_conv)),
        (sq(w_out), g_out, sq(m_w_out), sq(v_w_out)),
        (b_ada, g_bada, m_b_ada, v_b_ada), (b_in, g_bin, m_b_in, v_b_in), (sq(conv_w), g_conv, sq(m_conv_w), sq(v_conv_w)),
        (b_out, g_bout, m_b_out, v_b_out), (ln_g, g_lng, m_ln_g, v_ln_g), (ln_b, g_lnb, m_ln_b, v_ln_b)], "adam_rest")
    (d_wpb, nm_wpb, nv_wpb), (d_wout, nm_wout, nv_wout), (d_bada, nm_bada, nv_bada), (d_bin, nm_bin, nv_bin), \
        (d_conv, nm_conv, nv_conv), (d_bout, nm_bout, nv_bout), (d_lng, nm_lng, nv_lng), (d_lnb, nm_lnb, nv_lnb) = upd

    ex = lambda a: a.reshape((1,) + a.shape)
    grads = [ex(g_wada), g_bada, ex(g_win), g_bin, ex(g_conv), ex(g_wpa), ex(g_pb), ex(g_out), g_bout, g_lng, g_lnb]
    deltas = [ex(d_wada), d_bada, ex(d_win), d_bin, ex(d_conv), ex(d_wpa), ex(d_wpb), ex(d_wout), d_bout, d_lng, d_lnb]
    new_m = [ex(nm_wada), nm_bada, ex(nm_win), nm_bin, ex(nm_conv), ex(nm_wpa), ex(nm_wpb), ex(nm_wout), nm_bout, nm_lng, nm_lnb]
    new_v = [ex(nv_wada), nv_bada, ex(nv_win), nv_bin, ex(nv_conv), ex(nv_wpa), ex(nv_wpb), ex(nv_wout), nv_bout, nv_lng, nv_lnb]
    return (loss, grad_x.reshape(x.shape), *grads, *deltas, *new_m, *new_v)
```

```python
import functools

import jax
import jax.numpy as jnp
from jax import lax
from jax.experimental import pallas as pl
from jax.experimental.pallas import tpu as pltpu

F32, BF16 = jnp.float32, jnp.bfloat16
MESH = pl.DeviceIdType.MESH
N_DEV = 8
D = 1024
SLAB = 256
N_QKV, N_REST = 9, 25
N_SLAB = N_QKV + N_REST
D_IN = N_SLAB * SLAB
DP_SLABS = 36
BLK = 128
GROUPS = ((128, 1), (512, 4), (2048, 16))
ALPHA = 2.0 ** 0.25
LN_EPS = 1e-5
LR, B1, B2, EPS, WD, STEP = 0.001, 0.9, 0.999, 1e-08, 0.01, 10
R_ZA, R_UX, R_GB, R_GC, R_ZC, R_GA, R_GBM = 0, 1, 5, 9, 13, 17, 21
P_BIN, P_BOUT, P_LNG, P_LNB, P_CONV, P_LOSS, P_DADA = 0, 8704, 9728, 10752, 11776, 14848, 14976
MIB = 1024 * 1024

_pcall = pl.pallas_call
ANY = pl.BlockSpec(memory_space=pl.ANY)
VMEM = pl.BlockSpec(memory_space=pltpu.VMEM)


def _whole(a):
    return pl.BlockSpec(a.shape, lambda i: (0,) * len(a.shape))


def _params(vmem_mib=None, sem=None):
    kw = {}
    if vmem_mib is not None:
        kw["vmem_limit_bytes"] = vmem_mib * MIB
    if sem is not None:
        kw["dimension_semantics"] = sem
    return pltpu.CompilerParams(**kw)


def _nn(a, b):
    return jnp.dot(a, b, preferred_element_type=F32)


def _nt(a, b):
    return lax.dot_general(a, b, (((1,), (1,)), ((), ())), preferred_element_type=F32)


def _tn(a, b):
    return lax.dot_general(a, b, (((0,), (0,)), ((), ())), preferred_element_type=F32)


def _sigmoid(v):
    return 1.0 / (1.0 + jnp.exp(-v))


def _part8(v):
    return v.reshape(v.shape[0] // 8, 8, v.shape[1]).sum(axis=0)


def _my_position():
    return lax.axis_index("x"), lax.axis_index("y"), lax.axis_index("c")


def _flat(px, py, pc):
    return 4 * px + 2 * py + pc


def _peer(mask):
    x, y, c = _my_position()
    return (x ^ ((mask >> 2) & 1), y ^ ((mask >> 1) & 1), c ^ (mask & 1))


def _column_chunks(n):
    chunks = [(128 * a, 0, 128 * a, 128) for a in range(n // 128)]
    if n % 128:
        chunks.append((n - 128, 128 - n % 128, 128 * (n // 128), n % 128))
    return chunks


def _cast_rows(w, n_steps, name):
    rows, ncol = w.shape
    blk = pl.BlockSpec((rows // n_steps, ncol), lambda i: (i, 0))

    def body(w_ref, o_ref):
        o_ref[...] = w_ref[...].astype(BF16)

    return _pcall(body, grid=(n_steps,), out_shape=jax.ShapeDtypeStruct(w.shape, BF16), in_specs=[blk], out_specs=blk,
                  name=name, compiler_params=_params(16, ("parallel",)))(w)


def _prep(w_pa, w_pb, w_out, c, conv_w):
    def body(wpa_ref, wpb_ref, wout_ref, c_ref, cw_ref, wpat_ref, wpb_o, wout_o, cact_ref, cwp_ref):
        wpat_ref[...] = wpa_ref[...].T.astype(BF16)
        wpb_o[...] = wpb_ref[...].astype(BF16)
        wout_o[...] = wout_ref[...].astype(BF16)
        cv = c_ref[...]
        cact_ref[...] = jnp.zeros_like(cact_ref)
        cact_ref[pl.ds(0, cv.shape[0]), :] = cv * _sigmoid(cv)
        cwp_ref[...] = jnp.zeros_like(cwp_ref)
        cwp_ref[pl.ds(0, 3), :] = cw_ref[...]

    out_shape = (jax.ShapeDtypeStruct((w_pa.shape[1], w_pa.shape[0]), BF16),
                 jax.ShapeDtypeStruct(w_pb.shape, BF16), jax.ShapeDtypeStruct(w_out.shape, BF16),
                 jax.ShapeDtypeStruct((8, D), F32), jax.ShapeDtypeStruct((8, conv_w.shape[1]), F32))
    operands = (w_pa, w_pb, w_out, c, conv_w)
    return _pcall(body, grid=(1,), out_shape=out_shape, in_specs=[_whole(a) for a in operands],
                  out_specs=tuple(_whole(o) for o in out_shape), name="prep", compiler_params=_params(16))(*operands)


def _gather_rows(shards):
    n = len(shards)

    def body(*refs):
        srcs, outs = refs[:n], refs[n:2 * n]
        send_sems, recv_sems, local_sems = refs[2 * n:]
        x, y, c = _my_position()
        me, sibling = (x, y, c), (x, y, 1 - c)
        chips = [(1 - x, y), (x, 1 - y), (1 - x, 1 - y)]

        def rows(a, px, py, pc):
            r = shards[a].shape[0]
            return outs[a].at[pl.ds(pl.multiple_of(_flat(px, py, pc) * r, r), r), :]

        def copy(a, k, block, to, src=None):
            return pltpu.make_async_remote_copy(
                src_ref=rows(a, *block) if src is None else src, dst_ref=rows(a, *block),
                send_sem=send_sems.at[7 * a + k], recv_sem=recv_sems.at[7 * a + k], device_id=to, device_id_type=MESH)

        mine = [pltpu.make_async_copy(srcs[a], rows(a, *me), local_sems.at[a]) for a in range(n)]
        for cp in mine:
            cp.start()
        first = []
        for a in range(n):
            first.append(copy(a, 0, me, sibling, src=srcs[a]))
            first += [copy(a, 1 + j, me, (*chip, c), src=srcs[a]) for j, chip in enumerate(chips)]
        for cp in first:
            cp.start()
        passed = []
        for j, chip in enumerate(chips):
            for a in range(n):
                copy(a, 1 + j, (*chip, c), me).wait_recv()
                cp = copy(a, 4 + j, (*chip, c), sibling)
                cp.start()
                passed.append(cp)
        for a in range(n):
            copy(a, 0, sibling, me).wait_recv()
        for j, chip in enumerate(chips):
            for a in range(n):
                copy(a, 4 + j, (*chip, 1 - c), me).wait_recv()
        for cp in first + passed:
            cp.wait_send()
        for cp in mine:
            cp.wait()

    out_shape = tuple(jax.ShapeDtypeStruct((N_DEV * s.shape[0], s.shape[1]), s.dtype) for s in shards)
    return _pcall(body, out_shape=out_shape, in_specs=[ANY] * n, out_specs=(ANY,) * n, name="gather_rows",
                  scratch_shapes=[pltpu.SemaphoreType.DMA((7 * n,)), pltpu.SemaphoreType.DMA((7 * n,)),
                                  pltpu.SemaphoreType.DMA((n,))])(*shards)


def _exchange_slots(out_ref, send_sems, recv_sems):
    me = _flat(*_my_position())
    copies = []
    for mask in range(1, N_DEV):
        peer = _peer(mask)
        copies.append((mask, pltpu.make_async_remote_copy(
            src_ref=out_ref.at[me], dst_ref=out_ref.at[me], send_sem=send_sems.at[mask - 1],
            recv_sem=recv_sems.at[mask - 1], device_id=peer, device_id_type=MESH)))
    for _, cp in copies:
        cp.start()
    for mask, _ in copies:
        peer = _peer(mask)
        pltpu.make_async_remote_copy(
            src_ref=out_ref.at[_flat(*peer)], dst_ref=out_ref.at[_flat(*peer)], send_sem=send_sems.at[mask - 1],
            recv_sem=recv_sems.at[mask - 1], device_id=peer, device_id_type=MESH).wait_recv()
    for _, cp in copies:
        cp.wait_send()


def _ada_forward(cact_all, w_ada, b_ada_mine):
    nb, ncol = cact_all.shape[0], w_ada.shape[1]

    def body(c_ref, w_ref, b_ref, out_ref, send_sems, recv_sems):
        me = _flat(*_my_position())
        out_ref[me] = _nn(c_ref[...].astype(BF16), w_ref[...].astype(BF16)) + b_ref[...]
        _exchange_slots(out_ref, send_sems, recv_sems)

    operands = (cact_all, w_ada, b_ada_mine)
    return _pcall(body, grid=(1,), out_shape=jax.ShapeDtypeStruct((N_DEV, nb, ncol), F32),
                  in_specs=[_whole(a) for a in operands], out_specs=VMEM,
                  scratch_shapes=[pltpu.SemaphoreType.DMA((7,)), pltpu.SemaphoreType.DMA((7,))], name="ada_forward",
                  compiler_params=_params(16))(*operands)


def _small_reduce(gb_rest, gb_qkv, svec, dgate, dss):
    nbat = dgate.shape[0]

    def body(gbr_ref, q0_ref, q1_ref, q2_ref, sv_ref, dg_ref, dss_ref, rows_ref, tot_ref, gbada_ref, send_sems, recv_sems):
        me = _flat(*_my_position())

        def put(off, v):
            rows_ref[me, :, pl.ds(off, v.shape[1])] = v

        def row(v):
            return jnp.sum(v, axis=0, keepdims=True)

        for g, q_ref in enumerate((q0_ref, q1_ref, q2_ref)):
            for which in range(3):
                put(P_BIN + SLAB * (3 * which + g), row(q_ref[which]))
        for s in range(N_REST):
            put(P_BIN + SLAB * (N_QKV + s), row(gbr_ref[s]))
        put(P_LNG, row(sv_ref[0]))
        put(P_LNB, row(sv_ref[1]))
        put(P_BOUT, row(sv_ref[2]))
        for j in range(3):
            put(P_CONV + D * j, row(sv_ref[3 + j]))
        loss = (0.5 / D) * jnp.sum(row(sv_ref[6]), axis=1, keepdims=True)
        put(P_LOSS, jnp.broadcast_to(loss, (1, 128)))
        for b in range(nbat):
            put(P_DADA + 3 * D * b, row(dss_ref[b, 0]))
            put(P_DADA + 3 * D * b + D, row(dss_ref[b, 1]))
            put(P_DADA + 3 * D * b + 2 * D, row(dg_ref[b]))
        _exchange_slots(rows_ref, send_sems, recv_sems)
        tot = rows_ref[0]
        for k in range(1, N_DEV):
            tot = tot + rows_ref[k]
        tot_ref[...] = tot
        gbada = tot[:, P_DADA:P_DADA + 3 * D]
        for b in range(1, nbat):
            gbada = gbada + tot[:, P_DADA + 3 * D * b:P_DADA + 3 * D * (b + 1)]
        gbada_ref[...] = gbada

    p_len = P_DADA + nbat * 3 * D
    out_shape = (jax.ShapeDtypeStruct((N_DEV, 1, p_len), F32), jax.ShapeDtypeStruct((1, p_len), F32),
                 jax.ShapeDtypeStruct((1, 3 * D), F32))
    operands = (gb_rest, *gb_qkv, svec, dgate, dss)
    return _pcall(body, grid=(1,), out_shape=out_shape, in_specs=[_whole(a) for a in operands],
                  out_specs=(VMEM, _whole(out_shape[1]), _whole(out_shape[2])),
                  scratch_shapes=[pltpu.SemaphoreType.DMA((7,)), pltpu.SemaphoreType.DMA((7,))], name="small_reduce",
                  compiler_params=_params(16))(*operands)


def _make_h(x, ada, tm=512):
    t = x.shape[0]
    tps = (t // ada.shape[0]) // tm

    def body(x_ref, ada_ref, h_ref):
        h_ref[...] = (x_ref[...] * (1.0 + ada_ref[0, 1:2, :]) + ada_ref[0, 0:1, :]).astype(BF16)

    return _pcall(body, grid=(t // tm,), out_shape=jax.ShapeDtypeStruct((t, D), BF16),
                  in_specs=[pl.BlockSpec((tm, D), lambda i: (i, 0)), pl.BlockSpec((1, 3, D), lambda i: (i // tps, 0, 0))],
                  out_specs=pl.BlockSpec((tm, D), lambda i: (i, 0)), name="make_h",
                  compiler_params=_params(32, ("parallel",)))(x, ada)


def _project(h, w_in_t, b_in3):
    t = h.shape[0]

    def body(h_ref, w_ref, b_ref, qkv_ref, rest_ref):
        j = pl.program_id(0)
        v = _nt(h_ref[...], w_ref[...]) + b_ref[0]

        @pl.when(j < N_QKV)
        def _():
            qkv_ref[0] = v.astype(BF16)

        @pl.when(j >= N_QKV)
        def _():
            rest_ref[0] = v

    return _pcall(
        body, grid=(N_SLAB,),
        out_shape=(jax.ShapeDtypeStruct((N_QKV, t, SLAB), BF16), jax.ShapeDtypeStruct((N_REST, t, SLAB), F32)),
        in_specs=[pl.BlockSpec((t, D), lambda j: (0, 0)), pl.BlockSpec((SLAB, D), lambda j: (j, 0)),
                  pl.BlockSpec((1, 1, SLAB), lambda j: (j, 0, 0))],
        out_specs=(pl.BlockSpec((1, t, SLAB), lambda j: (jnp.minimum(j, N_QKV - 1), 0, 0)),
                   pl.BlockSpec((1, t, SLAB), lambda j: (jnp.maximum(j - N_QKV, 0), 0, 0))),
        name="project", compiler_params=_params(48, ("arbitrary",)))(h, w_in_t, b_in3)


def _bias_tables(g):
    window, dil = GROUPS[g]
    span = window // dil
    qi = jnp.arange(BLK)[:, None]
    kj = jnp.arange(2 * BLK)[None, :]
    delta = qi + BLK - kj
    valid = (delta >= 0) & (delta <= span)
    heads = jnp.arange(4, dtype=F32) + 4.0 * g
    slopes = 2.0 ** (-8.0 * (heads + 1.0) / 12.0)
    bias = -slopes[:, None, None] * (delta * dil).astype(F32)[None]
    return jnp.where(valid[None], bias, -1e30).reshape(4 * BLK, 2 * BLK)


def _head_masks(shape):
    lane = lax.broadcasted_iota(jnp.int32, shape, 1)
    return [(lane >= 64 * h) & (lane < 64 * (h + 1)) for h in range(4)]


def _stack_heads(v, masks):
    return jnp.concatenate([jnp.where(masks[h], v, jnp.zeros_like(v)) for h in range(4)], axis=0)


def _unstack_heads(v4, masks):
    out = jnp.where(masks[0], v4[0:BLK], 0.0)
    for h in range(1, 4):
        out = jnp.where(masks[h], v4[BLK * h:BLK * (h + 1)], out)
    return out


def _regroup(load_half, dst_ref, stage_ref, n, dil):
    for hlf in range(2):
        stage_ref[hlf] = load_half(hlf)

    def residue(r, carry):
        for hlf in range(2):
            dst_ref[pl.ds(pl.multiple_of(r * n, BLK), n), pl.ds(128 * hlf, 128)] = (
                stage_ref[hlf, pl.ds(r, n, stride=dil), :].astype(dst_ref.dtype))
        return carry

    lax.fori_loop(0, dil, residue, 0)


def _store_block(nat_ref, r, i, val, dil):
    for hlf in range(2):
        nat_ref[hlf, pl.ds(r + dil * BLK * i, BLK, stride=dil), :] = val[:, 128 * hlf:128 * (hlf + 1)]


def _for_blocks(block, dil, nblk):
    if dil == 1:
        block(0, 0, True)
        block(0, 1, False)

        def pair(k, carry):
            block(0, 2 * k, False)
            block(0, 2 * k + 1, False)
            return carry

        lax.fori_loop(1, nblk // 2, pair, 0)
    else:
        def residues(k, carry):
            block(2 * k, 0, True)
            block(2 * k + 1, 0, True)
            if nblk > 1:
                def loop(i, c):
                    block(2 * k, i, False)
                    block(2 * k + 1, i, False)
                    return c
                lax.fori_loop(1, nblk, loop, 0)
            return carry

        lax.fori_loop(0, dil // 2, residues, 0)


def _attn_forward(qkv, g, nbat):
    t = qkv.shape[1]
    seq = t // nbat
    dil = GROUPS[g][1]
    n = seq // dil
    nblk = n // BLK
    qkv4 = qkv.reshape(3, 3, t, SLAB)

    def body(qkv_ref, bias_ref, ol_ref, *scratch):
        masks = _head_masks((BLK, SLAB))
        if dil > 1:
            stage, qd, kd, vd, nat_o, nat_l = scratch
            for which, dst in enumerate((qd, kd, vd)):
                _regroup(lambda hlf, which=which: qkv_ref[which, 0, :, pl.ds(128 * hlf, 128)].astype(F32), dst, stage, n, dil)
        else:
            qd, kd, vd = qkv_ref.at[0, 0], qkv_ref.at[1, 0], qkv_ref.at[2, 0]

        def block(r, i, first):
            base = r * n
            qs = pl.ds(pl.multiple_of(base + i * BLK, BLK), BLK)
            ks = pl.ds(pl.multiple_of(base, BLK), BLK) if first else pl.ds(pl.multiple_of(base + (i - 1) * BLK, BLK), 2 * BLK)
            q, kk, vv = qd[qs, :], kd[ks, :], vd[ks, :]
            bias = bias_ref[:, pl.ds(BLK, BLK)] if first else bias_ref[...]
            s = _nt(_stack_heads(q, masks), kk) * 0.125 + bias
            m = jnp.max(s, axis=1, keepdims=True)
            p = jnp.exp(s - m)
            den = jnp.sum(p, axis=1, keepdims=True)
            out = _unstack_heads(_nn((p * (1.0 / den)).astype(BF16), vv), masks)
            lse = _unstack_heads(jnp.broadcast_to(m + jnp.log(den), (4 * BLK, SLAB)), masks)
            if dil > 1:
                _store_block(nat_o, r, i, out, dil)
                _store_block(nat_l, r, i, lse, dil)
            else:
                ol_ref[0, qs, :] = out
                ol_ref[1, qs, :] = lse

        _for_blocks(block, dil, nblk)
        if dil > 1:
            for hlf in range(2):
                ol_ref[0, :, pl.ds(128 * hlf, 128)] = nat_o[hlf]
                ol_ref[1, :, pl.ds(128 * hlf, 128)] = nat_l[hlf]

    scratch = []
    if dil > 1:
        scratch = [pltpu.VMEM((2, seq, 128), F32)] + [pltpu.VMEM((seq, SLAB), BF16)] * 3 + [pltpu.VMEM((2, seq, 128), F32)] * 2
    return _pcall(
        body, grid=(nbat,), out_shape=jax.ShapeDtypeStruct((2, t, SLAB), F32),
        in_specs=[pl.BlockSpec((3, 1, seq, SLAB), lambda b: (0, g, b, 0)),
                  pl.BlockSpec((4 * BLK, 2 * BLK), lambda b: (0, 0))],
        out_specs=pl.BlockSpec((2, seq, SLAB), lambda b: (0, b, 0)), scratch_shapes=scratch,
        name=f"attn_forward_{g}", compiler_params=_params(40, ("parallel",)))(qkv4, _bias_tables(g))


def _attn_backward(qkv, do_attn, ol_tot, dproj, g, nbat):
    t = qkv.shape[1]
    seq = t // nbat
    dil = GROUPS[g][1]
    n = seq // dil
    nblk = n // BLK
    qkv4 = qkv.reshape(3, 3, t, SLAB)
    dp4 = dproj.reshape(DP_SLABS // 3, 3, t, SLAB)

    def body(qkv_ref, do_ref, ol_ref, bias_ref, dp_in, dp_ref, gb_ref, dk_acc, dv_acc, *scratch):
        del dp_in
        masks = _head_masks((BLK, SLAB))

        @pl.when(pl.program_id(0) == 0)
        def _():
            gb_ref[...] = jnp.zeros_like(gb_ref)

        dk_acc[...] = jnp.zeros_like(dk_acc)
        dv_acc[...] = jnp.zeros_like(dv_acc)
        if dil > 1:
            stage, qd, kd, vd, dod, prodd, lsed, nat = scratch
            lanes = lambda hlf: pl.ds(128 * hlf, 128)
            for which, dst in enumerate((qd, kd, vd)):
                _regroup(lambda hlf, which=which: qkv_ref[which, 0, :, lanes(hlf)].astype(F32), dst, stage, n, dil)
            _regroup(lambda hlf: do_ref[:, lanes(hlf)].astype(F32), dod, stage, n, dil)
            _regroup(lambda hlf: do_ref[:, lanes(hlf)].astype(F32) * ol_ref[0, :, lanes(hlf)], prodd, stage, n, dil)
            _regroup(lambda hlf: ol_ref[1, :, lanes(hlf)], lsed, stage, n, dil)
        else:
            qd, kd, vd = qkv_ref.at[0, 0], qkv_ref.at[1, 0], qkv_ref.at[2, 0]

        def block(r, i, first):
            base = r * n
            qs = pl.ds(pl.multiple_of(base + i * BLK, BLK), BLK)
            ks = pl.ds(pl.multiple_of(base, BLK), BLK) if first else pl.ds(pl.multiple_of(base + (i - 1) * BLK, BLK), 2 * BLK)
            q, kk, vv = qd[qs, :], kd[ks, :], vd[ks, :]
            if dil > 1:
                do, prod, lse = dod[qs, :], prodd[qs, :], lsed[qs, :]
            else:
                do = do_ref[qs, :]
                prod = do.astype(F32) * ol_ref[0, qs, :]
                lse = ol_ref[1, qs, :]
            q4, do4 = _stack_heads(q, masks), _stack_heads(do, masks)
            bias = bias_ref[:, pl.ds(BLK, BLK)] if first else bias_ref[...]
            lse4 = jnp.concatenate([lse[:, 64 * h:64 * h + 1] for h in range(4)], axis=0)
            delta4 = jnp.concatenate([jnp.sum(jnp.where(masks[h], prod, 0.0), axis=1, keepdims=True) for h in range(4)], axis=0)
            p = jnp.exp(_nt(q4, kk) * 0.125 + bias - lse4)
            ds = (p * (_nt(do4, vv) - delta4)).astype(BF16)
            dv_acc[ks, :] += _tn(p.astype(BF16), do4)
            dk_acc[ks, :] += _tn(ds, q4) * 0.125
            dq = _unstack_heads(_nn(ds, kk), masks) * 0.125
            if dil > 1:
                _store_block(nat, r, i, dq, dil)
            else:
                dp_ref[0, 0, qs, :] = dq.astype(BF16)
            gb_ref[0] += _part8(dq)

        _for_blocks(block, dil, nblk)
        gb_ref[1] += _part8(dk_acc[...])
        gb_ref[2] += _part8(dv_acc[...])
        if dil > 1:
            def flush(which):
                for hlf in range(2):
                    dp_ref[which, 0, :, pl.ds(128 * hlf, 128)] = nat[hlf].astype(BF16)

            def to_token_order(acc_ref):
                def residue(r, carry):
                    for hlf in range(2):
                        nat[hlf, pl.ds(r, n, stride=dil), :] = acc_ref[pl.ds(pl.multiple_of(r * n, BLK), n), pl.ds(128 * hlf, 128)]
                    return carry
                lax.fori_loop(0, dil, residue, 0)

            flush(0)
            to_token_order(dk_acc)
            flush(1)
            to_token_order(dv_acc)
            flush(2)
        else:
            dp_ref[1, 0] = dk_acc[...].astype(BF16)
            dp_ref[2, 0] = dv_acc[...].astype(BF16)

    scratch = [pltpu.VMEM((seq, SLAB), F32)] * 2
    if dil > 1:
        scratch += ([pltpu.VMEM((2, seq, 128), F32)] + [pltpu.VMEM((seq, SLAB), BF16)] * 4 + [pltpu.VMEM((seq, SLAB), F32)] * 2
                    + [pltpu.VMEM((2, seq, 128), F32)])
    dp, gb = _pcall(
        body, grid=(nbat,),
        out_shape=(jax.ShapeDtypeStruct(dp4.shape, BF16), jax.ShapeDtypeStruct((3, 8, SLAB), F32)),
        in_specs=[pl.BlockSpec((3, 1, seq, SLAB), lambda b: (0, g, b, 0)),
                  pl.BlockSpec((seq, SLAB), lambda b: (b, 0)),
                  pl.BlockSpec((2, seq, SLAB), lambda b: (0, b, 0)),
                  pl.BlockSpec((4 * BLK, 2 * BLK), lambda b: (0, 0)), ANY],
        out_specs=(pl.BlockSpec((3, 1, seq, SLAB), lambda b: (DP_SLABS // 9 - 1, g, b, 0)),
                   pl.BlockSpec((3, 8, SLAB), lambda b: (0, 0, 0))),
        scratch_shapes=scratch, input_output_aliases={4: 0}, name=f"attn_backward_{g}",
        compiler_params=_params(48, ("arbitrary",)))(qkv4, do_attn, ol_tot, _bias_tables(g), dp4)
    return dp.reshape(DP_SLABS, t, SLAB), gb


def _mid(rest, ols, x, tgt, ada, cw, b_out, ln_g, ln_b, w_pa_t, w_pb, w_out, tm=256):
    t = x.shape[0]
    nbat = ada.shape[0]
    nt = t // tm
    tps = nt // nbat

    def body(rest_ref, halo_ref, ol0_ref, ol1_ref, ol2_ref, x_ref, t_ref, ada_ref, cw_ref, bout_ref, lng_ref, lnb_ref,
             wpat_ref, wpb_ref, wout_ref,
             dp_ref, gx0_ref, doa_ref, olt_ref, mg_ref, dof_ref, bbs_ref, dyc_ref, a_ref, dya_ref,
             gbr_ref, sv_ref, dgate_ref, carry_ref, keep_ref):
        i = pl.program_id(0)
        ti = nt - 1 - i
        pos = ti % tps

        @pl.when(i == 0)
        def _():
            gbr_ref[...] = jnp.zeros_like(gbr_ref)
            sv_ref[...] = jnp.zeros_like(sv_ref)

        @pl.when(pos == tps - 1)
        def _():
            dgate_ref[...] = jnp.zeros_like(dgate_ref)
            carry_ref[...] = jnp.zeros_like(carry_ref)

        row = lax.broadcasted_iota(jnp.int32, (tm, SLAB), 0)
        halo_on = (pos > 0).astype(F32)

        def cols(s):
            return pl.ds(SLAB * s, SLAB)

        l0, l1, l2 = ol0_ref[1], ol1_ref[1], ol2_ref[1]
        mx = jnp.maximum(jnp.maximum(l0, l1), l2)
        e0, e1, e2 = jnp.exp(l0 - mx), jnp.exp(l1 - mx), jnp.exp(l2 - mx)
        den = e0 + e1 + e2
        o_attn = (e0 * ol0_ref[0] + e1 * ol1_ref[0] + e2 * ol2_ref[0]) * (1.0 / den)
        olt_ref[0] = o_attn
        olt_ref[1] = mx + jnp.log(den)
        z_a = rest_ref[R_ZA]
        sg_za = _sigmoid(z_a)
        a_ref[...] = (o_attn * z_a * sg_za).astype(BF16)
        y_attn = _nt(a_ref[...], wpat_ref[...])

        for s in range(4):
            u = rest_ref[R_GC + s] * rest_ref[R_UX + s]
            hu = halo_ref[R_GC + s] * halo_ref[R_UX + s] * halo_on
            u1 = jnp.where(row == 0, hu[7:8], pltpu.roll(u, 1, 0))
            u2 = jnp.where(row == 0, hu[6:7], jnp.where(row == 1, hu[7:8], pltpu.roll(u, 2, 0)))
            conv = cw_ref[0:1, cols(s)] * u2 + cw_ref[1:2, cols(s)] * u1 + cw_ref[2:3, cols(s)] * u
            zc = rest_ref[R_ZC + s]
            sg = _sigmoid(zc)
            keep_ref[2, :, cols(s)], keep_ref[3, :, cols(s)], keep_ref[4, :, cols(s)], keep_ref[5, :, cols(s)] = u1, u2, conv, sg
            bbs_ref[:, cols(s)] = (rest_ref[R_GB + s] * conv * (zc * sg)).astype(BF16)
        y_conv = _nn(bbs_ref[...], wpb_ref[...])

        for s in range(4):
            s_a, s_b = _sigmoid(rest_ref[R_GA + s]), _sigmoid(rest_ref[R_GBM + s])
            keep_ref[0, :, cols(s)], keep_ref[1, :, cols(s)] = s_a, s_b
            mg_ref[:, cols(s)] = (s_a * y_attn[:, SLAB * s:SLAB * (s + 1)] + s_b * y_conv[:, SLAB * s:SLAB * (s + 1)]).astype(BF16)
        o = _nn(mg_ref[...], wout_ref[...]) + bout_ref[...]
        gate = ada_ref[0, 2:3, :]
        r = ALPHA * x_ref[...] + gate * o
        mu = jnp.mean(r, axis=1, keepdims=True)
        rc = r - mu
        rstd = lax.rsqrt(jnp.mean(rc * rc, axis=1, keepdims=True) + LN_EPS)
        xhat = rc * rstd
        err = xhat * lng_ref[...] + lnb_ref[...] - t_ref[...]
        sv_ref[6] += _part8(err * err)
        dy = err * (1.0 / D)
        sv_ref[0] += _part8(dy * xhat)
        sv_ref[1] += _part8(dy)
        dxh = dy * lng_ref[...]
        dr = rstd * (dxh - jnp.mean(dxh, axis=1, keepdims=True) - xhat * jnp.mean(dxh * xhat, axis=1, keepdims=True))
        gx0_ref[...] = ALPHA * dr
        dgate_ref[0] += _part8(dr * o)
        do_ = dr * gate
        sv_ref[2] += _part8(do_)
        dof_ref[...] = do_.astype(BF16)
        dmerged = _nt(dof_ref[...], wout_ref[...])
        for s in range(4):
            s_a, s_b = keep_ref[0, :, cols(s)], keep_ref[1, :, cols(s)]
            dm = dmerged[:, SLAB * s:SLAB * (s + 1)]
            ya, yc = y_attn[:, SLAB * s:SLAB * (s + 1)], y_conv[:, SLAB * s:SLAB * (s + 1)]
            dya_ref[:, cols(s)] = (dm * s_a).astype(BF16)
            dyc_ref[:, cols(s)] = (dm * s_b).astype(BF16)
            dga = dm * ya * s_a * (1.0 - s_a)
            dgb = dm * yc * s_b * (1.0 - s_b)
            dp_ref[R_GA + s] = dga.astype(BF16)
            dp_ref[R_GBM + s] = dgb.astype(BF16)
            gbr_ref[R_GA + s] += _part8(dga)
            gbr_ref[R_GBM + s] += _part8(dgb)

        da = _nn(dya_ref[...], wpat_ref[...])
        doa_ref[...] = (da * z_a * sg_za).astype(BF16)
        dza = da * o_attn * (sg_za * (1.0 + z_a * (1.0 - sg_za)))
        dp_ref[R_ZA] = dza.astype(BF16)
        gbr_ref[R_ZA] += _part8(dza)

        dbb = _nt(dyc_ref[...], wpb_ref[...])
        for s in range(4):
            ux, gc, zc = rest_ref[R_UX + s], rest_ref[R_GC + s], rest_ref[R_ZC + s]
            u = gc * ux
            u1, u2, conv, sg = keep_ref[2, :, cols(s)], keep_ref[3, :, cols(s)], keep_ref[4, :, cols(s)], keep_ref[5, :, cols(s)]
            gb = rest_ref[R_GB + s]
            d_b = dbb[:, SLAB * s:SLAB * (s + 1)]
            szc = zc * sg
            dgb_ = d_b * conv * szc
            dconv = d_b * gb * szc
            dzc = d_b * gb * conv * (sg * (1.0 + zc * (1.0 - sg)))
            sv_ref[3, :, cols(s)] += _part8(dconv * u2)
            sv_ref[4, :, cols(s)] += _part8(dconv * u1)
            sv_ref[5, :, cols(s)] += _part8(dconv * u)
            nxt = carry_ref[:, cols(s)]
            d1 = jnp.where(row == tm - 1, nxt[0:1], pltpu.roll(dconv, tm - 1, 0))
            d2 = jnp.where(row == tm - 1, nxt[1:2], jnp.where(row == tm - 2, nxt[0:1], pltpu.roll(dconv, tm - 2, 0)))
            carry_ref[:, cols(s)] = dconv[0:8]
            du = cw_ref[2:3, cols(s)] * dconv + cw_ref[1:2, cols(s)] * d1 + cw_ref[0:1, cols(s)] * d2
            dgc, dux = du * ux, du * gc
            for slab, val in ((R_GB + s, dgb_), (R_ZC + s, dzc), (R_GC + s, dgc), (R_UX + s, dux)):
                dp_ref[slab] = val.astype(BF16)
                gbr_ref[slab] += _part8(val)

    def tile(i):
        return nt - 1 - i

    row_blk = lambda i: (tile(i), 0)
    slab_blk = lambda i: (0, tile(i), 0)
    const2 = lambda i: (0, 0)
    const3 = lambda i: (0, 0, 0)
    in_specs = [
        pl.BlockSpec((N_REST, tm, SLAB), slab_blk),
        pl.BlockSpec((N_REST, 8, SLAB), lambda i: (0, jnp.maximum(tile(i) * (tm // 8) - 1, 0), 0)),
        pl.BlockSpec((2, tm, SLAB), slab_blk), pl.BlockSpec((2, tm, SLAB), slab_blk), pl.BlockSpec((2, tm, SLAB), slab_blk),
        pl.BlockSpec((tm, D), row_blk), pl.BlockSpec((tm, D), row_blk),
        pl.BlockSpec((1, 3, D), lambda i: (tile(i) // tps, 0, 0)),
        pl.BlockSpec((3, D), const2), pl.BlockSpec((1, D), const2), pl.BlockSpec((1, D), const2), pl.BlockSpec((1, D), const2),
        pl.BlockSpec((D, SLAB), const2), pl.BlockSpec((D, D), const2), pl.BlockSpec((D, D), const2)]
    bf_rows = lambda: jax.ShapeDtypeStruct((t, D), BF16)
    out_shape = (
        jax.ShapeDtypeStruct((DP_SLABS, t, SLAB), BF16), jax.ShapeDtypeStruct((t, D), F32),
        jax.ShapeDtypeStruct((t, SLAB), BF16), jax.ShapeDtypeStruct((2, t, SLAB), F32),
        bf_rows(), bf_rows(), bf_rows(), bf_rows(), jax.ShapeDtypeStruct((t, SLAB), BF16), bf_rows(),
        jax.ShapeDtypeStruct((N_REST, 8, SLAB), F32), jax.ShapeDtypeStruct((7, 8, D), F32),
        jax.ShapeDtypeStruct((nbat, 8, D), F32))
    out_specs = (
        pl.BlockSpec((N_REST, tm, SLAB), slab_blk), pl.BlockSpec((tm, D), row_blk),
        pl.BlockSpec((tm, SLAB), row_blk), pl.BlockSpec((2, tm, SLAB), slab_blk),
        pl.BlockSpec((tm, D), row_blk), pl.BlockSpec((tm, D), row_blk), pl.BlockSpec((tm, D), row_blk),
        pl.BlockSpec((tm, D), row_blk), pl.BlockSpec((tm, SLAB), row_blk), pl.BlockSpec((tm, D), row_blk),
        pl.BlockSpec((N_REST, 8, SLAB), const3), pl.BlockSpec((7, 8, D), const3),
        pl.BlockSpec((1, 8, D), lambda i: (tile(i) // tps, 0, 0)))
    return _pcall(body, grid=(nt,), out_shape=out_shape, in_specs=in_specs, out_specs=out_specs,
                  scratch_shapes=[pltpu.VMEM((8, D), F32), pltpu.VMEM((6, tm, D), F32)], name="mid",
                  compiler_params=_params(56, ("arbitrary",)))(
        rest, rest, *ols, x, tgt, ada, cw, b_out, ln_g, ln_b, w_pa_t, w_pb, w_out)


def _tn_matmul(lhs, rhs, lhs_spec, n_steps, out_rows, out_index, name, after):
    t, n = rhs.shape

    def body(l_ref, r_ref, after_ref, o_ref):
        del after_ref
        o_ref[...] = _tn(l_ref[0] if len(l_ref.shape) == 3 else l_ref[...], r_ref[...])

    return _pcall(body, grid=(n_steps,), out_shape=jax.ShapeDtypeStruct((out_rows, n), F32),
                  in_specs=[lhs_spec, pl.BlockSpec((t, n), lambda j: (0, 0)), ANY],
                  out_specs=pl.BlockSpec((SLAB, n), out_index), name=name,
                  compiler_params=_params(48, ("parallel",)))(lhs, rhs, after)


def _grad_rows_2d(lhs, rhs, name, after):
    t, k = lhs.shape
    return _tn_matmul(lhs, rhs, pl.BlockSpec((t, SLAB), lambda j: (0, j)), k // SLAB, k, lambda j: (j, 0), name, after)


def _w_row_block(j):
    return (j + N_QKV) % N_SLAB


def _dp_slab(j):
    return jnp.where(j < N_REST, j, j + 2)


def _grad_w_in_t(dproj, h):
    t = h.shape[0]
    return _tn_matmul(dproj, h, pl.BlockSpec((1, t, SLAB), lambda j: (_dp_slab(j), 0, 0)), N_SLAB, D_IN,
                      lambda j: (_w_row_block(j), 0), "grad_w_in", h)


def _grad_h(dproj, w_in_t, gx0, x, ada, after, tm=512):
    t = x.shape[0]
    nbat = ada.shape[0]
    tps = (t // nbat) // tm

    def body(dp_ref, w_ref, gx0_ref, x_ref, ada_ref, after_ref, gx_ref, dss_ref):
        del after_ref
        i = pl.program_id(0)
        dh = None
        for j in range(N_SLAB):
            slab = j if j < N_REST else j + 2
            part = _nn(dp_ref[slab], w_ref[pl.ds(SLAB * ((j + N_QKV) % N_SLAB), SLAB), :])
            dh = part if dh is None else dh + part
        gx_ref[...] = gx0_ref[...] + dh * (1.0 + ada_ref[0, 1:2, :])

        @pl.when((i % tps) == 0)
        def _():
            dss_ref[...] = jnp.zeros_like(dss_ref)

        dss_ref[0, 0] += _part8(dh)
        dss_ref[0, 1] += _part8(dh * x_ref[...])

    return _pcall(
        body, grid=(t // tm,),
        out_shape=(jax.ShapeDtypeStruct((t, D), F32), jax.ShapeDtypeStruct((nbat, 2, 8, D), F32)),
        in_specs=[pl.BlockSpec((DP_SLABS, tm, SLAB), lambda i: (0, i, 0)),
                  pl.BlockSpec((D_IN, D), lambda i: (0, 0), pipeline_mode=pl.Buffered(1)),
                  pl.BlockSpec((tm, D), lambda i: (i, 0)), pl.BlockSpec((tm, D), lambda i: (i, 0)),
                  pl.BlockSpec((1, 3, D), lambda i: (i // tps, 0, 0)), ANY],
        out_specs=(pl.BlockSpec((tm, D), lambda i: (i, 0)),
                   pl.BlockSpec((1, 2, 8, D), lambda i: (i // tps, 0, 0, 0))),
        name="grad_h", compiler_params=_params(60, ("arbitrary",)))(dproj, w_in_t, gx0, x, ada, after)


def _chip(m):
    x, y, _ = _my_position()
    return (x ^ ((m >> 1) & 1), y ^ (m & 1))


def _exchange_siblings(grads):
    n = len(grads)

    def body(*refs):
        copies = _sibling_copies(refs[:n], refs[n:2 * n], refs[2 * n], refs[2 * n + 1])
        for cp in copies:
            cp.start()
        for cp in copies:
            cp.wait()

    return _pcall(body, out_shape=tuple(_sibling_zones(grads)), in_specs=[ANY] * n, out_specs=(ANY,) * n,
                  name="exchange_siblings", scratch_shapes=[pltpu.SemaphoreType.DMA((4 * n,))] * 2)(*grads)


def _sibling_zones(grads):
    return [jax.ShapeDtypeStruct((4, g.shape[0] // N_DEV, g.shape[1]), g.dtype) for g in grads]


def _sibling_copies(srcs, lands, send_sems, recv_sems):
    x, y, c = _my_position()
    copies = []
    for a, (src, land) in enumerate(zip(srcs, lands)):
        rows = land.shape[1]
        for m in range(4):
            dev = _flat(*_chip(m), 1 - c)
            copies.append(pltpu.make_async_remote_copy(
                src_ref=src.at[pl.ds(pl.multiple_of(dev * rows, 8), rows), :], dst_ref=land.at[m],
                send_sem=send_sems.at[4 * a + m], recv_sem=recv_sems.at[4 * a + m], device_id=(x, y, 1 - c),
                device_id_type=MESH))
    return copies


def _chip_copies(srcs, lands, send_sems, recv_sems):
    _, _, c = _my_position()
    return [pltpu.make_async_remote_copy(
        src_ref=srcs[a].at[m - 1], dst_ref=lands[a].at[m - 1], send_sem=send_sems.at[3 * a + m - 1],
        recv_sem=recv_sems.at[3 * a + m - 1], device_id=(*_chip(m), c), device_id_type=MESH)
        for a in range(len(srcs)) for m in range(1, 4)]


HBM = pl.BlockSpec(memory_space=pltpu.HBM)
SEM = pl.BlockSpec(memory_space=pltpu.SEMAPHORE)
SPLIT_COPY = pltpu.CompilerParams(has_side_effects=pltpu.SideEffectType.DATAFLOW_SIDE_EFFECTING)


def _start_copies(make_copies, n_sems, srcs, zones, name):
    n = len(srcs)

    def body(*refs):
        for cp in make_copies(refs[:n], refs[n:2 * n], refs[2 * n], refs[2 * n + 1]):
            cp.start()
        refs[-1][...] = jnp.zeros_like(refs[-1])

    hbm = tuple(pltpu.HBM(b.shape, b.dtype) for b in list(srcs) + list(zones))
    out_shape = (pltpu.SemaphoreType.DMA((n_sems,)), pltpu.SemaphoreType.DMA((n_sems,))) + hbm + (jax.ShapeDtypeStruct((8, 128), F32),)
    operands = [pltpu.with_memory_space_constraint(b, pltpu.HBM) for b in srcs]
    operands += [pltpu.with_memory_space_constraint(lax.empty(z.shape, z.dtype), pltpu.HBM) for z in zones]
    res = _pcall(body, out_shape=out_shape, in_specs=[HBM] * (2 * n), out_specs=(SEM, SEM) + (HBM,) * (2 * n) + (VMEM,),
                 input_output_aliases={i: 2 + i for i in range(2 * n)}, name=name, compiler_params=SPLIT_COPY)(*operands)
    return (res[0], res[1], res[2:2 + n], res[2 + n:2 + 2 * n]), res[-1]


def _wait_copies(make_copies, flight, after, name):
    send_sems, recv_sems, srcs, zones = flight
    n = len(srcs)

    def body(*refs):
        for cp in make_copies(refs[:n], refs[n:2 * n], refs[2 * n], refs[2 * n + 1]):
            cp.wait_send()
            cp.wait_recv()

    hbm = tuple(pltpu.HBM(b.shape, b.dtype) for b in list(srcs) + list(zones))
    res = _pcall(body, out_shape=hbm, in_specs=[HBM] * (2 * n) + [SEM, SEM, ANY], out_specs=(HBM,) * (2 * n),
                 input_output_aliases={i: i for i in range(2 * n)}, name=name, compiler_params=SPLIT_COPY)(
        *srcs, *zones, send_sems, recv_sems, after)
    return res[:n], res[n:]


def _pair_sums(devs, grads, lands, n_steps, name):
    n = len(grads)
    rows = [l.shape[1] for l in lands]
    rbs = [r // n_steps for r in rows]

    def body(devs_ref, *refs):
        del devs_ref
        g_refs, land_refs, outs = refs[:4 * n], refs[4 * n:5 * n], refs[5 * n:]
        for a in range(n):
            outs[2 * a][...] = g_refs[4 * a][...] + land_refs[a][0]
            for m in range(1, 4):
                outs[2 * a + 1][m - 1] = (g_refs[4 * a + m][...] + land_refs[a][m]).astype(BF16)

    def block_of(m, per_dev):
        return lambda i, devs_ref: (devs_ref[m] * per_dev + i, 0)

    in_specs = [pl.BlockSpec((rb, l.shape[2]), block_of(m, n_steps)) for rb, l in zip(rbs, lands) for m in range(4)]
    in_specs += [pl.BlockSpec((4, rb, l.shape[2]), lambda i, devs_ref: (0, i, 0)) for rb, l in zip(rbs, lands)]
    out_shape, out_specs = [], []
    for rb, l in zip(rbs, lands):
        out_shape += [jax.ShapeDtypeStruct(l.shape[1:], F32), jax.ShapeDtypeStruct((3,) + l.shape[1:], BF16)]
        out_specs += [pl.BlockSpec((rb, l.shape[2]), lambda i, devs_ref: (i, 0)),
                      pl.BlockSpec((3, rb, l.shape[2]), lambda i, devs_ref: (0, i, 0))]
    grid_spec = pltpu.PrefetchScalarGridSpec(num_scalar_prefetch=1, grid=(n_steps,), in_specs=in_specs, out_specs=tuple(out_specs))
    res = _pcall(body, grid_spec=grid_spec, out_shape=tuple(out_shape), name=name,
                 compiler_params=_params(48, ("parallel",)))(devs, *[g for g in grads for _ in range(4)], *lands)
    return res[0::2], res[1::2]


def _final_sums(mine, lands, n_steps, name):
    n = len(mine)
    rbs = [o.shape[0] // n_steps for o in mine]

    def body(*refs):
        mine_refs, land_refs, outs = refs[:n], refs[n:2 * n], refs[2 * n:]
        for a in range(n):
            tot = mine_refs[a][...]
            for m in range(3):
                tot = tot + land_refs[a][m].astype(F32)
            outs[a][...] = tot

    in_specs = ([pl.BlockSpec((rb, o.shape[1]), lambda i: (i, 0)) for rb, o in zip(rbs, mine)]
                + [pl.BlockSpec((3, rb, o.shape[1]), lambda i: (0, i, 0)) for rb, o in zip(rbs, mine)])
    out_specs = tuple(pl.BlockSpec((rb, o.shape[1]), lambda i: (i, 0)) for rb, o in zip(rbs, mine))
    out_shape = tuple(jax.ShapeDtypeStruct(o.shape, F32) for o in mine)
    return _pcall(body, grid=(n_steps,), out_shape=out_shape, in_specs=in_specs, out_specs=out_specs, name=name,
                  compiler_params=_params(32, ("parallel",)))(*mine, *lands)


def _reduce_scatter_begin(big, small_after_start):
    flight, token = _start_copies(_sibling_copies, 4, [big], _sibling_zones([big]), "siblings_start")
    small = small_after_start(token)
    (big,), big_lands = _wait_copies(_sibling_copies, flight, small[-1], "siblings_wait")
    small_lands = _exchange_siblings(small)
    c = lax.axis_index("c")
    devs = jnp.stack([_flat(*_chip(m), c) for m in range(4)]).astype(jnp.int32)
    big_mine, big_send = _pair_sums(devs, [big], big_lands, 4, "pair_sums_w_in")
    small_mine, small_send = _pair_sums(devs, small, small_lands, 1, "pair_sums_rest")
    bufs = list(big_send) + list(small_send)
    flight, token = _start_copies(_chip_copies, 3 * len(bufs), bufs, bufs, "chips_start")
    return (flight, list(big_mine) + list(small_mine)), token


def _reduce_scatter_end(state, after):
    flight, mine = state
    _, got = _wait_copies(_chip_copies, flight, after, "chips_wait")
    big = _final_sums(mine[:1], got[:1], 4, "final_sums_w_in")
    small = _final_sums(mine[1:], got[1:], 1, "final_sums_rest")
    return list(big) + list(small)


def _adamw(w, g, m, v):
    m_new = B1 * m + (1.0 - B1) * g
    v_new = B2 * v + (1.0 - B2) * (g * g)
    m_hat = m_new / (1.0 - B1 ** STEP)
    v_hat = v_new / (1.0 - B2 ** STEP)
    delta = -LR * (m_hat / (jnp.sqrt(v_hat) + EPS) + WD * w)
    return delta, m_new, v_new


def _adam_rows(g, w, m, v, n_steps, name):
    rows, ncol = w.shape
    blk = pl.BlockSpec((rows // n_steps, ncol), lambda i: (i, 0))

    def body(g_ref, w_ref, m_ref, v_ref, d_ref, mo_ref, vo_ref):
        d_ref[...], mo_ref[...], vo_ref[...] = _adamw(w_ref[...], g_ref[...], m_ref[...], v_ref[...])

    shape = jax.ShapeDtypeStruct(w.shape, F32)
    return _pcall(body, grid=(n_steps,), out_shape=(shape,) * 3, in_specs=[blk] * 4, out_specs=(blk,) * 3, name=name,
                  compiler_params=_params(32, ("parallel",)))(g, w, m, v)


def _adam_transposed(g_t, w, m, v, name):
    n, k = g_t.shape
    rb = min(k, 128)

    def body(gt_ref, w_ref, m_ref, v_ref, g_ref, d_ref, mo_ref, vo_ref):
        for src, skip, dst, size in _column_chunks(n):
            sl = pl.ds(dst, size)
            g = gt_ref[pl.ds(src, 128), :].T[:, skip:]
            delta, m_new, v_new = _adamw(w_ref[:, sl], g, m_ref[:, sl], v_ref[:, sl])
            g_ref[:, sl], d_ref[:, sl], mo_ref[:, sl], vo_ref[:, sl] = g, delta, m_new, v_new

    shape = jax.ShapeDtypeStruct(w.shape, F32)
    rows = pl.BlockSpec((rb, n), lambda i: (i, 0))
    return _pcall(body, grid=(k // rb,), out_shape=(shape,) * 4,
                  in_specs=[pl.BlockSpec((n, rb), lambda i: (0, i)), rows, rows, rows], out_specs=(rows,) * 4, name=name,
                  compiler_params=_params(32, ("parallel",)))(g_t, w, m, v)


def _adam_many(items, name):
    n = len(items)

    def body(*refs):
        ins, outs = refs[:4 * n], refs[4 * n:]
        for a in range(n):
            w_ref, g_ref, m_ref, v_ref = ins[4 * a:4 * a + 4]
            delta, m_new, v_new = _adamw(w_ref[...], g_ref[...], m_ref[...], v_ref[...])
            outs[3 * a][...], outs[3 * a + 1][...], outs[3 * a + 2][...] = delta, m_new, v_new

    out_shape = tuple(jax.ShapeDtypeStruct(it[0].shape, F32) for it in items for _ in range(3))
    flat = [arr for it in items for arr in it]
    res = _pcall(body, grid=(1,), out_shape=out_shape, in_specs=[_whole(a) for a in flat],
                 out_specs=tuple(_whole(o) for o in out_shape), name=name, compiler_params=_params(32))(*flat)
    return [tuple(res[3 * a:3 * a + 3]) for a in range(n)]


def _adam_w_ada(cact_all, dada_mine, w, m, v):
    def body(c_ref, d_ref, w_ref, m_ref, v_ref, g_ref, dl_ref, mo_ref, vo_ref):
        g = _tn(c_ref[...].astype(BF16), d_ref[...].astype(BF16))
        delta, m_new, v_new = _adamw(w_ref[...], g, m_ref[...], v_ref[...])
        g_ref[...], dl_ref[...], mo_ref[...], vo_ref[...] = g, delta, m_new, v_new

    shape = jax.ShapeDtypeStruct(w.shape, F32)
    operands = (cact_all, dada_mine, w, m, v)
    return _pcall(body, grid=(1,), out_shape=(shape,) * 4, in_specs=[_whole(a) for a in operands],
                  out_specs=(_whole(w),) * 4, name="adam_w_ada", compiler_params=_params(32))(*operands)


def kernel(x, c, w_ada, b_ada, w_in, b_in, conv_w, w_proj_attn, w_proj_conv, w_out, b_out, ln_g, ln_b, loss_target, m_w_ada, m_b_ada, m_w_in, m_b_in, m_conv_w, m_w_proj_attn, m_w_proj_conv, m_w_out, m_b_out, m_ln_g, m_ln_b, v_w_ada, v_b_ada, v_w_in, v_b_in, v_conv_w, v_w_proj_attn, v_w_proj_conv, v_w_out, v_b_out, v_ln_g, v_ln_b):
    nbat, seq, _ = x.shape
    t = nbat * seq
    me = _flat(*_my_position())
    x2, tgt2 = x.reshape(t, D), loss_target.reshape(t, D)
    sq = lambda a: a.reshape(a.shape[1:])

    tr = lambda a: a[0].T
    w_in_rows = tr(w_in)
    w_in_t_s = _cast_rows(w_in_rows, 4, "cast_w_in")
    w_pa_t_s, w_pb_s, w_out_s, cact_s, cw_s = _prep(sq(w_proj_attn), sq(w_proj_conv), sq(w_out), c, sq(conv_w))
    w_in_t, w_pa_t, w_pb, w_o, cact_g, cw_g = _gather_rows([w_in_t_s, w_pa_t_s, w_pb_s, w_out_s, cact_s, cw_s])
    cact_all = cact_g.reshape(N_DEV, 8, D)[:, :nbat].reshape(N_DEV * nbat, D)
    cw = cw_g.reshape(N_DEV, 8, -1)[:, :3].transpose(1, 0, 2).reshape(3, D)

    ncol = w_ada.shape[2]
    b_ada_mine = lax.dynamic_slice(b_ada, (0, me * ncol), (1, ncol))
    ada_slots = _ada_forward(cact_all, sq(w_ada), b_ada_mine)
    ada_all = ada_slots.transpose(1, 0, 2).reshape(N_DEV * nbat, 3, D)
    ada = lax.dynamic_slice(ada_all, (me * nbat, 0, 0), (nbat, 3, D))

    h = _make_h(x2, ada)
    qkv, rest = _project(h, w_in_t, b_in.reshape(N_SLAB, 1, SLAB))
    ols = [_attn_forward(qkv, g, nbat) for g in range(3)]
    (dproj, gx0, do_attn, ol_tot, merged, do_f, bbs, dyc, a_bf, dya, gb_rest, svec, dgate) = _mid(
        rest, ols, x2, tgt2, ada, cw, b_out, ln_g, ln_b, w_pa_t, w_pb, w_o)

    gb_qkv = []
    for g in range(3):
        dproj, gb = _attn_backward(qkv, do_attn, ol_tot, dproj, g, nbat)
        gb_qkv.append(gb)
    g_w_in_t = _grad_w_in_t(dproj, h)

    def small_grads(token):
        g_w_out = _grad_rows_2d(merged, do_f, "grad_w_out", token)
        g_w_pb = _grad_rows_2d(bbs, dyc, "grad_w_proj_conv", g_w_out)
        g_w_pa_t = _grad_rows_2d(dya, a_bf, "grad_w_proj_attn", g_w_pb)
        return [g_w_out, g_w_pb, g_w_pa_t]

    rs_state, token = _reduce_scatter_begin(g_w_in_t, small_grads)
    grad_x, dss = _grad_h(dproj, w_in_t, gx0, x2, ada, token)

    rows8, tot, g_bada = _small_reduce(gb_rest, gb_qkv, svec, dgate, dss)
    g_in_t, g_out, g_pb, g_pa_t = _reduce_scatter_end(rs_state, tot)
    loss = tot[0, P_LOSS]
    dada_all = rows8[:, 0, P_DADA:].reshape(N_DEV * nbat, 3 * D)
    dada_mine = lax.dynamic_slice(dada_all, (0, me * ncol), (N_DEV * nbat, ncol))

    d_win_t, nm_win_t, nv_win_t = _adam_rows(g_in_t, w_in_rows, tr(m_w_in), tr(v_w_in), 8, "adam_w_in")
    g_win, d_win, nm_win, nv_win = g_in_t.T, d_win_t.T, nm_win_t.T, nv_win_t.T
    g_wpa, d_wpa, nm_wpa, nv_wpa = _adam_transposed(g_pa_t, sq(w_proj_attn), sq(m_w_proj_attn), sq(v_w_proj_attn), "adam_w_proj_attn")
    g_wada, d_wada, nm_wada, nv_wada = _adam_w_ada(cact_all, dada_mine, sq(w_ada), sq(m_w_ada), sq(v_w_ada))
    g_bin = tot[:, P_BIN:P_BIN + D_IN]
    g_bout = tot[:, P_BOUT:P_BOUT + D]
    g_lng = tot[:, P_LNG:P_LNG + D]
    g_lnb = tot[:, P_LNB:P_LNB + D]
    g_conv = lax.dynamic_slice(tot[:, P_CONV:P_CONV + 3 * D].reshape(3, D), (0, me * cw_s.shape[1]), (3, cw_s.shape[1]))
    upd = _adam_many([
        (sq(w_proj_conv), g_pb, sq(m_w_proj_conv), sq(v_w_proj_conv)),
        (sq(w_out), g_out, sq(m_w_out), sq(v_w_out)),
        (b_ada, g_bada, m_b_ada, v_b_ada), (b_in, g_bin, m_b_in, v_b_in), (sq(conv_w), g_conv, sq(m_conv_w), sq(v_conv_w)),
        (b_out, g_bout, m_b_out, v_b_out), (ln_g, g_lng, m_ln_g, v_ln_g), (ln_b, g_lnb, m_ln_b, v_ln_b)], "adam_rest")
    (d_wpb, nm_wpb, nv_wpb), (d_wout, nm_wout, nv_wout), (d_bada, nm_bada, nv_bada), (d_bin, nm_bin, nv_bin), \
        (d_conv, nm_conv, nv_conv), (d_bout, nm_bout, nv_bout), (d_lng, nm_lng, nv_lng), (d_lnb, nm_lnb, nv_lnb) = upd

    ex = lambda a: a.reshape((1,) + a.shape)
    grads = [ex(g_wada), g_bada, ex(g_win), g_bin, ex(g_conv), ex(g_wpa), ex(g_pb), ex(g_out), g_bout, g_lng, g_lnb]
    deltas = [ex(d_wada), d_bada, ex(d_win), d_bin, ex(d_conv), ex(d_wpa), ex(d_wpb), ex(d_wout), d_bout, d_lng, d_lnb]
    new_m = [ex(nm_wada), nm_bada, ex(nm_win), nm_bin, ex(nm_conv), ex(nm_wpa), ex(nm_wpb), ex(nm_wout), nm_bout, nm_lng, nm_lnb]
    new_v = [ex(nv_wada), nv_bada, ex(nv_win), nv_bin, ex(nv_conv), ex(nv_wpa), ex(nv_wpb), ex(nv_wout), nv_bout, nv_lng, nv_lnb]
    return (loss, grad_x.reshape(x.shape), *grads, *deltas, *new_m, *new_v)
```

```python
import functools

import jax
import jax.numpy as jnp
from jax import lax
from jax.experimental import pallas as pl
from jax.experimental.pallas import tpu as pltpu

F32, BF16 = jnp.float32, jnp.bfloat16
MESH = pl.DeviceIdType.MESH
N_DEV = 8
D = 1024
SLAB = 256
N_QKV, N_REST = 9, 25
N_SLAB = N_QKV + N_REST
D_IN = N_SLAB * SLAB
DP_SLABS = 36
BLK = 128
GROUPS = ((128, 1), (512, 4), (2048, 16))
ALPHA = 2.0 ** 0.25
LN_EPS = 1e-5
LR, B1, B2, EPS, WD, STEP = 0.001, 0.9, 0.999, 1e-08, 0.01, 10
R_ZA, R_UX, R_GB, R_GC, R_ZC, R_GA, R_GBM = 0, 1, 5, 9, 13, 17, 21
P_BIN, P_BOUT, P_LNG, P_LNB, P_CONV, P_LOSS, P_DADA = 0, 8704, 9728, 10752, 11776, 14848, 14976
MIB = 1024 * 1024


def _pcall(body, *, out_shape, out_specs=None, **kw):
    def pin_out(shape, spec):
        blocked = isinstance(shape, jax.ShapeDtypeStruct) and getattr(spec, "block_shape", None) is not None
        return pltpu.HBM(shape.shape, shape.dtype) if blocked else shape

    if out_specs is None:
        grid_spec = kw["grid_spec"]
        specs = grid_spec.out_specs
    else:
        kw["out_specs"] = specs = out_specs
    if isinstance(out_shape, (tuple, list)):
        out_shape = tuple(pin_out(s, p) for s, p in zip(out_shape, specs))
    else:
        out_shape = pin_out(out_shape, specs)
    call = pl.pallas_call(body, out_shape=out_shape, **kw)

    def run(*operands):
        def pin(o):
            is_data = jnp.issubdtype(o.dtype, jnp.floating) or jnp.issubdtype(o.dtype, jnp.integer)
            return pltpu.with_memory_space_constraint(o, pltpu.HBM) if is_data else o
        return call(*[pin(o) for o in operands])

    return run

ANY = pl.BlockSpec(memory_space=pl.ANY)
VMEM = pl.BlockSpec(memory_space=pltpu.VMEM)


def _whole(a):
    return pl.BlockSpec(a.shape, lambda i: (0,) * len(a.shape))


def _params(vmem_mib=None, sem=None):
    kw = {}
    if vmem_mib is not None:
        kw["vmem_limit_bytes"] = vmem_mib * MIB
    if sem is not None:
        kw["dimension_semantics"] = sem
    return pltpu.CompilerParams(**kw)


def _nn(a, b):
    return jnp.dot(a, b, preferred_element_type=F32)


def _nt(a, b):
    return lax.dot_general(a, b, (((1,), (1,)), ((), ())), preferred_element_type=F32)


def _tn(a, b):
    return lax.dot_general(a, b, (((0,), (0,)), ((), ())), preferred_element_type=F32)


def _sigmoid(v):
    return 1.0 / (1.0 + jnp.exp(-v))


def _part8(v):
    return v.reshape(v.shape[0] // 8, 8, v.shape[1]).sum(axis=0)


def _my_position():
    return lax.axis_index("x"), lax.axis_index("y"), lax.axis_index("c")


def _flat(px, py, pc):
    return 4 * px + 2 * py + pc


def _peer(mask):
    x, y, c = _my_position()
    return (x ^ ((mask >> 2) & 1), y ^ ((mask >> 1) & 1), c ^ (mask & 1))


def _column_chunks(n):
    chunks = [(128 * a, 0, 128 * a, 128) for a in range(n // 128)]
    if n % 128:
        chunks.append((n - 128, 128 - n % 128, 128 * (n // 128), n % 128))
    return chunks


def _cast_rows(w, n_steps, name):
    rows, ncol = w.shape
    blk = pl.BlockSpec((rows // n_steps, ncol), lambda i: (i, 0))

    def body(w_ref, o_ref):
        o_ref[...] = w_ref[...].astype(BF16)

    return _pcall(body, grid=(n_steps,), out_shape=jax.ShapeDtypeStruct(w.shape, BF16), in_specs=[blk], out_specs=blk,
                  name=name, compiler_params=_params(16, ("parallel",)))(w)


def _prep(w_pa, w_pb, w_out, c, conv_w):
    def body(wpa_ref, wpb_ref, wout_ref, c_ref, cw_ref, wpat_ref, wpb_o, wout_o, cact_ref, cwp_ref):
        wpat_ref[...] = wpa_ref[...].T.astype(BF16)
        wpb_o[...] = wpb_ref[...].astype(BF16)
        wout_o[...] = wout_ref[...].astype(BF16)
        cv = c_ref[...]
        cact_ref[...] = jnp.zeros_like(cact_ref)
        cact_ref[pl.ds(0, cv.shape[0]), :] = cv * _sigmoid(cv)
        cwp_ref[...] = jnp.zeros_like(cwp_ref)
        cwp_ref[pl.ds(0, 3), :] = cw_ref[...]

    out_shape = (jax.ShapeDtypeStruct((w_pa.shape[1], w_pa.shape[0]), BF16),
                 jax.ShapeDtypeStruct(w_pb.shape, BF16), jax.ShapeDtypeStruct(w_out.shape, BF16),
                 jax.ShapeDtypeStruct((8, D), F32), jax.ShapeDtypeStruct((8, conv_w.shape[1]), F32))
    operands = (w_pa, w_pb, w_out, c, conv_w)
    return _pcall(body, grid=(1,), out_shape=out_shape, in_specs=[_whole(a) for a in operands],
                  out_specs=tuple(_whole(o) for o in out_shape), name="prep", compiler_params=_params(16))(*operands)


def _gather_rows(shards):
    n = len(shards)

    def body(*refs):
        srcs, outs = refs[:n], refs[n:2 * n]
        send_sems, recv_sems, local_sems = refs[2 * n:]
        x, y, c = _my_position()
        me, sibling = (x, y, c), (x, y, 1 - c)
        chips = [(1 - x, y), (x, 1 - y), (1 - x, 1 - y)]

        def rows(a, px, py, pc):
            r = shards[a].shape[0]
            return outs[a].at[pl.ds(pl.multiple_of(_flat(px, py, pc) * r, r), r), :]

        def copy(a, k, block, to, src=None):
            return pltpu.make_async_remote_copy(
                src_ref=rows(a, *block) if src is None else src, dst_ref=rows(a, *block),
                send_sem=send_sems.at[7 * a + k], recv_sem=recv_sems.at[7 * a + k], device_id=to, device_id_type=MESH)

        mine = [pltpu.make_async_copy(srcs[a], rows(a, *me), local_sems.at[a]) for a in range(n)]
        for cp in mine:
            cp.start()
        first = []
        for a in range(n):
            first.append(copy(a, 0, me, sibling, src=srcs[a]))
            first += [copy(a, 1 + j, me, (*chip, c), src=srcs[a]) for j, chip in enumerate(chips)]
        for cp in first:
            cp.start()
        passed = []
        for j, chip in enumerate(chips):
            for a in range(n):
                copy(a, 1 + j, (*chip, c), me).wait_recv()
                cp = copy(a, 4 + j, (*chip, c), sibling)
                cp.start()
                passed.append(cp)
        for a in range(n):
            copy(a, 0, sibling, me).wait_recv()
        for j, chip in enumerate(chips):
            for a in range(n):
                copy(a, 4 + j, (*chip, 1 - c), me).wait_recv()
        for cp in first + passed:
            cp.wait_send()
        for cp in mine:
            cp.wait()

    out_shape = tuple(jax.ShapeDtypeStruct((N_DEV * s.shape[0], s.shape[1]), s.dtype) for s in shards)
    return _pcall(body, out_shape=out_shape, in_specs=[ANY] * n, out_specs=(ANY,) * n, name="gather_rows",
                  scratch_shapes=[pltpu.SemaphoreType.DMA((7 * n,)), pltpu.SemaphoreType.DMA((7 * n,)),
                                  pltpu.SemaphoreType.DMA((n,))])(*shards)


def _exchange_slots(out_ref, send_sems, recv_sems):
    me = _flat(*_my_position())
    copies = []
    for mask in range(1, N_DEV):
        peer = _peer(mask)
        copies.append((mask, pltpu.make_async_remote_copy(
            src_ref=out_ref.at[me], dst_ref=out_ref.at[me], send_sem=send_sems.at[mask - 1],
            recv_sem=recv_sems.at[mask - 1], device_id=peer, device_id_type=MESH)))
    for _, cp in copies:
        cp.start()
    for mask, _ in copies:
        peer = _peer(mask)
        pltpu.make_async_remote_copy(
            src_ref=out_ref.at[_flat(*peer)], dst_ref=out_ref.at[_flat(*peer)], send_sem=send_sems.at[mask - 1],
            recv_sem=recv_sems.at[mask - 1], device_id=peer, device_id_type=MESH).wait_recv()
    for _, cp in copies:
        cp.wait_send()


def _ada_forward(cact_all, w_ada, b_ada_mine):
    nb, ncol = cact_all.shape[0], w_ada.shape[1]

    def body(c_ref, w_ref, b_ref, out_ref, send_sems, recv_sems):
        me = _flat(*_my_position())
        out_ref[me] = _nn(c_ref[...].astype(BF16), w_ref[...].astype(BF16)) + b_ref[...]
        _exchange_slots(out_ref, send_sems, recv_sems)

    operands = (cact_all, w_ada, b_ada_mine)
    return _pcall(body, grid=(1,), out_shape=jax.ShapeDtypeStruct((N_DEV, nb, ncol), F32),
                  in_specs=[_whole(a) for a in operands], out_specs=VMEM,
                  scratch_shapes=[pltpu.SemaphoreType.DMA((7,)), pltpu.SemaphoreType.DMA((7,))], name="ada_forward",
                  compiler_params=_params(16))(*operands)


def _small_reduce(gb_rest, gb_qkv, svec, dgate, dss):
    nbat = dgate.shape[0]

    def body(gbr_ref, q0_ref, q1_ref, q2_ref, sv_ref, dg_ref, dss_ref, rows_ref, tot_ref, gbada_ref, send_sems, recv_sems):
        me = _flat(*_my_position())

        def put(off, v):
            rows_ref[me, :, pl.ds(off, v.shape[1])] = v

        def row(v):
            return jnp.sum(v, axis=0, keepdims=True)

        for g, q_ref in enumerate((q0_ref, q1_ref, q2_ref)):
            for which in range(3):
                put(P_BIN + SLAB * (3 * which + g), row(q_ref[which]))
        for s in range(N_REST):
            put(P_BIN + SLAB * (N_QKV + s), row(gbr_ref[s]))
        put(P_LNG, row(sv_ref[0]))
        put(P_LNB, row(sv_ref[1]))
        put(P_BOUT, row(sv_ref[2]))
        for j in range(3):
            put(P_CONV + D * j, row(sv_ref[3 + j]))
        loss = (0.5 / D) * jnp.sum(row(sv_ref[6]), axis=1, keepdims=True)
        put(P_LOSS, jnp.broadcast_to(loss, (1, 128)))
        for b in range(nbat):
            put(P_DADA + 3 * D * b, row(dss_ref[b, 0]))
            put(P_DADA + 3 * D * b + D, row(dss_ref[b, 1]))
            put(P_DADA + 3 * D * b + 2 * D, row(dg_ref[b]))
        _exchange_slots(rows_ref, send_sems, recv_sems)
        tot = rows_ref[0]
        for k in range(1, N_DEV):
            tot = tot + rows_ref[k]
        tot_ref[...] = tot
        gbada = tot[:, P_DADA:P_DADA + 3 * D]
        for b in range(1, nbat):
            gbada = gbada + tot[:, P_DADA + 3 * D * b:P_DADA + 3 * D * (b + 1)]
        gbada_ref[...] = gbada

    p_len = P_DADA + nbat * 3 * D
    out_shape = (jax.ShapeDtypeStruct((N_DEV, 1, p_len), F32), jax.ShapeDtypeStruct((1, p_len), F32),
                 jax.ShapeDtypeStruct((1, 3 * D), F32))
    operands = (gb_rest, *gb_qkv, svec, dgate, dss)
    return _pcall(body, grid=(1,), out_shape=out_shape, in_specs=[_whole(a) for a in operands],
                  out_specs=(VMEM, _whole(out_shape[1]), _whole(out_shape[2])),
                  scratch_shapes=[pltpu.SemaphoreType.DMA((7,)), pltpu.SemaphoreType.DMA((7,))], name="small_reduce",
                  compiler_params=_params(16))(*operands)


def _make_h(x, ada, tm=512):
    t = x.shape[0]
    tps = (t // ada.shape[0]) // tm

    def body(x_ref, ada_ref, h_ref):
        h_ref[...] = (x_ref[...] * (1.0 + ada_ref[0, 1:2, :]) + ada_ref[0, 0:1, :]).astype(BF16)

    return _pcall(body, grid=(t // tm,), out_shape=jax.ShapeDtypeStruct((t, D), BF16),
                  in_specs=[pl.BlockSpec((tm, D), lambda i: (i, 0)), pl.BlockSpec((1, 3, D), lambda i: (i // tps, 0, 0))],
                  out_specs=pl.BlockSpec((tm, D), lambda i: (i, 0)), name="make_h",
                  compiler_params=_params(32, ("parallel",)))(x, ada)


def _project(h, w_in_t, b_in3):
    t = h.shape[0]

    def body(h_ref, w_ref, b_ref, qkv_ref, rest_ref):
        j = pl.program_id(0)
        v = _nt(h_ref[...], w_ref[...]) + b_ref[0]

        @pl.when(j < N_QKV)
        def _():
            qkv_ref[0] = v.astype(BF16)

        @pl.when(j >= N_QKV)
        def _():
            rest_ref[0] = v

    return _pcall(
        body, grid=(N_SLAB,),
        out_shape=(jax.ShapeDtypeStruct((N_QKV, t, SLAB), BF16), jax.ShapeDtypeStruct((N_REST, t, SLAB), F32)),
        in_specs=[pl.BlockSpec((t, D), lambda j: (0, 0)), pl.BlockSpec((SLAB, D), lambda j: (j, 0)),
                  pl.BlockSpec((1, 1, SLAB), lambda j: (j, 0, 0))],
        out_specs=(pl.BlockSpec((1, t, SLAB), lambda j: (jnp.minimum(j, N_QKV - 1), 0, 0)),
                   pl.BlockSpec((1, t, SLAB), lambda j: (jnp.maximum(j - N_QKV, 0), 0, 0))),
        name="project", compiler_params=_params(48, ("arbitrary",)))(h, w_in_t, b_in3)


def _bias_tables(g):
    window, dil = GROUPS[g]
    span = window // dil
    qi = jnp.arange(BLK)[:, None]
    kj = jnp.arange(2 * BLK)[None, :]
    delta = qi + BLK - kj
    valid = (delta >= 0) & (delta <= span)
    heads = jnp.arange(4, dtype=F32) + 4.0 * g
    slopes = 2.0 ** (-8.0 * (heads + 1.0) / 12.0)
    bias = -slopes[:, None, None] * (delta * dil).astype(F32)[None]
    return jnp.where(valid[None], bias, -1e30).reshape(4 * BLK, 2 * BLK)


def _head_masks(shape):
    lane = lax.broadcasted_iota(jnp.int32, shape, 1)
    return [(lane >= 64 * h) & (lane < 64 * (h + 1)) for h in range(4)]


def _stack_heads(v, masks):
    return jnp.concatenate([jnp.where(masks[h], v, jnp.zeros_like(v)) for h in range(4)], axis=0)


def _unstack_heads(v4, masks):
    out = jnp.where(masks[0], v4[0:BLK], 0.0)
    for h in range(1, 4):
        out = jnp.where(masks[h], v4[BLK * h:BLK * (h + 1)], out)
    return out


def _regroup(load_half, dst_ref, stage_ref, n, dil):
    for hlf in range(2):
        stage_ref[hlf] = load_half(hlf)

    def residue(r, carry):
        for hlf in range(2):
            dst_ref[pl.ds(pl.multiple_of(r * n, BLK), n), pl.ds(128 * hlf, 128)] = (
                stage_ref[hlf, pl.ds(r, n, stride=dil), :].astype(dst_ref.dtype))
        return carry

    lax.fori_loop(0, dil, residue, 0)


def _store_block(nat_ref, r, i, val, dil):
    for hlf in range(2):
        nat_ref[hlf, pl.ds(r + dil * BLK * i, BLK, stride=dil), :] = val[:, 128 * hlf:128 * (hlf + 1)]


def _for_blocks(block, dil, nblk):
    if dil == 1:
        block(0, 0, True)
        block(0, 1, False)

        def pair(k, carry):
            block(0, 2 * k, False)
            block(0, 2 * k + 1, False)
            return carry

        lax.fori_loop(1, nblk // 2, pair, 0)
    else:
        def residues(k, carry):
            block(2 * k, 0, True)
            block(2 * k + 1, 0, True)
            if nblk > 1:
                def loop(i, c):
                    block(2 * k, i, False)
                    block(2 * k + 1, i, False)
                    return c
                lax.fori_loop(1, nblk, loop, 0)
            return carry

        lax.fori_loop(0, dil // 2, residues, 0)


def _attn_forward(qkv, g, nbat):
    t = qkv.shape[1]
    seq = t // nbat
    dil = GROUPS[g][1]
    n = seq // dil
    nblk = n // BLK
    qkv4 = qkv.reshape(3, 3, t, SLAB)

    def body(qkv_ref, bias_ref, ol_ref, *scratch):
        masks = _head_masks((BLK, SLAB))
        if dil > 1:
            stage, qd, kd, vd, nat_o, nat_l = scratch
            for which, dst in enumerate((qd, kd, vd)):
                _regroup(lambda hlf, which=which: qkv_ref[which, 0, :, pl.ds(128 * hlf, 128)].astype(F32), dst, stage, n, dil)
        else:
            qd, kd, vd = qkv_ref.at[0, 0], qkv_ref.at[1, 0], qkv_ref.at[2, 0]

        def block(r, i, first):
            base = r * n
            qs = pl.ds(pl.multiple_of(base + i * BLK, BLK), BLK)
            ks = pl.ds(pl.multiple_of(base, BLK), BLK) if first else pl.ds(pl.multiple_of(base + (i - 1) * BLK, BLK), 2 * BLK)
            q, kk, vv = qd[qs, :], kd[ks, :], vd[ks, :]
            bias = bias_ref[:, pl.ds(BLK, BLK)] if first else bias_ref[...]
            s = _nt(_stack_heads(q, masks), kk) * 0.125 + bias
            m = jnp.max(s, axis=1, keepdims=True)
            p = jnp.exp(s - m)
            den = jnp.sum(p, axis=1, keepdims=True)
            out = _unstack_heads(_nn((p * (1.0 / den)).astype(BF16), vv), masks)
            lse = _unstack_heads(jnp.broadcast_to(m + jnp.log(den), (4 * BLK, SLAB)), masks)
            if dil > 1:
                _store_block(nat_o, r, i, out, dil)
                _store_block(nat_l, r, i, lse, dil)
            else:
                ol_ref[0, qs, :] = out
                ol_ref[1, qs, :] = lse

        _for_blocks(block, dil, nblk)
        if dil > 1:
            for hlf in range(2):
                ol_ref[0, :, pl.ds(128 * hlf, 128)] = nat_o[hlf]
                ol_ref[1, :, pl.ds(128 * hlf, 128)] = nat_l[hlf]

    scratch = []
    if dil > 1:
        scratch = [pltpu.VMEM((2, seq, 128), F32)] + [pltpu.VMEM((seq, SLAB), BF16)] * 3 + [pltpu.VMEM((2, seq, 128), F32)] * 2
    return _pcall(
        body, grid=(nbat,), out_shape=jax.ShapeDtypeStruct((2, t, SLAB), F32),
        in_specs=[pl.BlockSpec((3, 1, seq, SLAB), lambda b: (0, g, b, 0)),
                  pl.BlockSpec((4 * BLK, 2 * BLK), lambda b: (0, 0))],
        out_specs=pl.BlockSpec((2, seq, SLAB), lambda b: (0, b, 0)), scratch_shapes=scratch,
        name=f"attn_forward_{g}", compiler_params=_params(40, ("parallel",)))(qkv4, _bias_tables(g))


def _attn_backward(qkv, do_attn, ol_tot, dproj, g, nbat):
    t = qkv.shape[1]
    seq = t // nbat
    dil = GROUPS[g][1]
    n = seq // dil
    nblk = n // BLK
    qkv4 = qkv.reshape(3, 3, t, SLAB)
    dp4 = dproj.reshape(DP_SLABS // 3, 3, t, SLAB)

    def body(qkv_ref, do_ref, ol_ref, bias_ref, dp_in, dp_ref, gb_ref, dk_acc, dv_acc, *scratch):
        del dp_in
        masks = _head_masks((BLK, SLAB))

        @pl.when(pl.program_id(0) == 0)
        def _():
            gb_ref[...] = jnp.zeros_like(gb_ref)

        dk_acc[...] = jnp.zeros_like(dk_acc)
        dv_acc[...] = jnp.zeros_like(dv_acc)
        if dil > 1:
            stage, qd, kd, vd, dod, prodd, lsed, nat = scratch
            lanes = lambda hlf: pl.ds(128 * hlf, 128)
            for which, dst in enumerate((qd, kd, vd)):
                _regroup(lambda hlf, which=which: qkv_ref[which, 0, :, lanes(hlf)].astype(F32), dst, stage, n, dil)
            _regroup(lambda hlf: do_ref[:, lanes(hlf)].astype(F32), dod, stage, n, dil)
            _regroup(lambda hlf: do_ref[:, lanes(hlf)].astype(F32) * ol_ref[0, :, lanes(hlf)], prodd, stage, n, dil)
            _regroup(lambda hlf: ol_ref[1, :, lanes(hlf)], lsed, stage, n, dil)
        else:
            qd, kd, vd = qkv_ref.at[0, 0], qkv_ref.at[1, 0], qkv_ref.at[2, 0]

        def block(r, i, first):
            base = r * n
            qs = pl.ds(pl.multiple_of(base + i * BLK, BLK), BLK)
            ks = pl.ds(pl.multiple_of(base, BLK), BLK) if first else pl.ds(pl.multiple_of(base + (i - 1) * BLK, BLK), 2 * BLK)
            q, kk, vv = qd[qs, :], kd[ks, :], vd[ks, :]
            if dil > 1:
                do, prod, lse = dod[qs, :], prodd[qs, :], lsed[qs, :]
            else:
                do = do_ref[qs, :]
                prod = do.astype(F32) * ol_ref[0, qs, :]
                lse = ol_ref[1, qs, :]
            q4, do4 = _stack_heads(q, masks), _stack_heads(do, masks)
            bias = bias_ref[:, pl.ds(BLK, BLK)] if first else bias_ref[...]
            lse4 = jnp.concatenate([lse[:, 64 * h:64 * h + 1] for h in range(4)], axis=0)
            delta4 = jnp.concatenate([jnp.sum(jnp.where(masks[h], prod, 0.0), axis=1, keepdims=True) for h in range(4)], axis=0)
            p = jnp.exp(_nt(q4, kk) * 0.125 + bias - lse4)
            ds = (p * (_nt(do4, vv) - delta4)).astype(BF16)
            dv_acc[ks, :] += _tn(p.astype(BF16), do4)
            dk_acc[ks, :] += _tn(ds, q4) * 0.125
            dq = _unstack_heads(_nn(ds, kk), masks) * 0.125
            if dil > 1:
                _store_block(nat, r, i, dq, dil)
            else:
                dp_ref[0, 0, qs, :] = dq.astype(BF16)
            gb_ref[0] += _part8(dq)

        _for_blocks(block, dil, nblk)
        gb_ref[1] += _part8(dk_acc[...])
        gb_ref[2] += _part8(dv_acc[...])
        if dil > 1:
            def flush(which):
                for hlf in range(2):
                    dp_ref[which, 0, :, pl.ds(128 * hlf, 128)] = nat[hlf].astype(BF16)

            def to_token_order(acc_ref):
                def residue(r, carry):
                    for hlf in range(2):
                        nat[hlf, pl.ds(r, n, stride=dil), :] = acc_ref[pl.ds(pl.multiple_of(r * n, BLK), n), pl.ds(128 * hlf, 128)]
                    return carry
                lax.fori_loop(0, dil, residue, 0)

            flush(0)
            to_token_order(dk_acc)
            flush(1)
            to_token_order(dv_acc)
            flush(2)
        else:
            dp_ref[1, 0] = dk_acc[...].astype(BF16)
            dp_ref[2, 0] = dv_acc[...].astype(BF16)

    scratch = [pltpu.VMEM((seq, SLAB), F32)] * 2
    if dil > 1:
        scratch += ([pltpu.VMEM((2, seq, 128), F32)] + [pltpu.VMEM((seq, SLAB), BF16)] * 4 + [pltpu.VMEM((seq, SLAB), F32)] * 2
                    + [pltpu.VMEM((2, seq, 128), F32)])
    dp, gb = _pcall(
        body, grid=(nbat,),
        out_shape=(jax.ShapeDtypeStruct(dp4.shape, BF16), jax.ShapeDtypeStruct((3, 8, SLAB), F32)),
        in_specs=[pl.BlockSpec((3, 1, seq, SLAB), lambda b: (0, g, b, 0)),
                  pl.BlockSpec((seq, SLAB), lambda b: (b, 0)),
                  pl.BlockSpec((2, seq, SLAB), lambda b: (0, b, 0)),
                  pl.BlockSpec((4 * BLK, 2 * BLK), lambda b: (0, 0)), ANY],
        out_specs=(pl.BlockSpec((3, 1, seq, SLAB), lambda b: (DP_SLABS // 9 - 1, g, b, 0)),
                   pl.BlockSpec((3, 8, SLAB), lambda b: (0, 0, 0))),
        scratch_shapes=scratch, input_output_aliases={4: 0}, name=f"attn_backward_{g}",
        compiler_params=_params(48, ("arbitrary",)))(qkv4, do_attn, ol_tot, _bias_tables(g), dp4)
    return dp.reshape(DP_SLABS, t, SLAB), gb


def _mid(rest, ols, x, tgt, ada, cw, b_out, ln_g, ln_b, w_pa_t, w_pb, w_out, tm=256):
    t = x.shape[0]
    nbat = ada.shape[0]
    nt = t // tm
    tps = nt // nbat

    def body(rest_ref, halo_ref, ol0_ref, ol1_ref, ol2_ref, x_ref, t_ref, ada_ref, cw_ref, bout_ref, lng_ref, lnb_ref,
             wpat_ref, wpb_ref, wout_ref,
             dp_ref, gx0_ref, doa_ref, olt_ref, mg_ref, dof_ref, bbs_ref, dyc_ref, a_ref, dya_ref,
             gbr_ref, sv_ref, dgate_ref, carry_ref, keep_ref):
        i = pl.program_id(0)
        ti = nt - 1 - i
        pos = ti % tps

        @pl.when(i == 0)
        def _():
            gbr_ref[...] = jnp.zeros_like(gbr_ref)
            sv_ref[...] = jnp.zeros_like(sv_ref)

        @pl.when(pos == tps - 1)
        def _():
            dgate_ref[...] = jnp.zeros_like(dgate_ref)
            carry_ref[...] = jnp.zeros_like(carry_ref)

        row = lax.broadcasted_iota(jnp.int32, (tm, SLAB), 0)
        halo_on = (pos > 0).astype(F32)

        def cols(s):
            return pl.ds(SLAB * s, SLAB)

        l0, l1, l2 = ol0_ref[1], ol1_ref[1], ol2_ref[1]
        mx = jnp.maximum(jnp.maximum(l0, l1), l2)
        e0, e1, e2 = jnp.exp(l0 - mx), jnp.exp(l1 - mx), jnp.exp(l2 - mx)
        den = e0 + e1 + e2
        o_attn = (e0 * ol0_ref[0] + e1 * ol1_ref[0] + e2 * ol2_ref[0]) * (1.0 / den)
        olt_ref[0] = o_attn
        olt_ref[1] = mx + jnp.log(den)
        z_a = rest_ref[R_ZA]
        sg_za = _sigmoid(z_a)
        a_ref[...] = (o_attn * z_a * sg_za).astype(BF16)
        y_attn = _nt(a_ref[...], wpat_ref[...])

        for s in range(4):
            u = rest_ref[R_GC + s] * rest_ref[R_UX + s]
            hu = halo_ref[R_GC + s] * halo_ref[R_UX + s] * halo_on
            u1 = jnp.where(row == 0, hu[7:8], pltpu.roll(u, 1, 0))
            u2 = jnp.where(row == 0, hu[6:7], jnp.where(row == 1, hu[7:8], pltpu.roll(u, 2, 0)))
            conv = cw_ref[0:1, cols(s)] * u2 + cw_ref[1:2, cols(s)] * u1 + cw_ref[2:3, cols(s)] * u
            zc = rest_ref[R_ZC + s]
            sg = _sigmoid(zc)
            keep_ref[2, :, cols(s)], keep_ref[3, :, cols(s)], keep_ref[4, :, cols(s)], keep_ref[5, :, cols(s)] = u1, u2, conv, sg
            bbs_ref[:, cols(s)] = (rest_ref[R_GB + s] * conv * (zc * sg)).astype(BF16)
        y_conv = _nn(bbs_ref[...], wpb_ref[...])

        for s in range(4):
            s_a, s_b = _sigmoid(rest_ref[R_GA + s]), _sigmoid(rest_ref[R_GBM + s])
            keep_ref[0, :, cols(s)], keep_ref[1, :, cols(s)] = s_a, s_b
            mg_ref[:, cols(s)] = (s_a * y_attn[:, SLAB * s:SLAB * (s + 1)] + s_b * y_conv[:, SLAB * s:SLAB * (s + 1)]).astype(BF16)
        o = _nn(mg_ref[...], wout_ref[...]) + bout_ref[...]
        gate = ada_ref[0, 2:3, :]
        r = ALPHA * x_ref[...] + gate * o
        mu = jnp.mean(r, axis=1, keepdims=True)
        rc = r - mu
        rstd = lax.rsqrt(jnp.mean(rc * rc, axis=1, keepdims=True) + LN_EPS)
        xhat = rc * rstd
        err = xhat * lng_ref[...] + lnb_ref[...] - t_ref[...]
        sv_ref[6] += _part8(err * err)
        dy = err * (1.0 / D)
        sv_ref[0] += _part8(dy * xhat)
        sv_ref[1] += _part8(dy)
        dxh = dy * lng_ref[...]
        dr = rstd * (dxh - jnp.mean(dxh, axis=1, keepdims=True) - xhat * jnp.mean(dxh * xhat, axis=1, keepdims=True))
        gx0_ref[...] = ALPHA * dr
        dgate_ref[0] += _part8(dr * o)
        do_ = dr * gate
        sv_ref[2] += _part8(do_)
        dof_ref[...] = do_.astype(BF16)
        dmerged = _nt(dof_ref[...], wout_ref[...])
        for s in range(4):
            s_a, s_b = keep_ref[0, :, cols(s)], keep_ref[1, :, cols(s)]
            dm = dmerged[:, SLAB * s:SLAB * (s + 1)]
            ya, yc = y_attn[:, SLAB * s:SLAB * (s + 1)], y_conv[:, SLAB * s:SLAB * (s + 1)]
            dya_ref[:, cols(s)] = (dm * s_a).astype(BF16)
            dyc_ref[:, cols(s)] = (dm * s_b).astype(BF16)
            dga = dm * ya * s_a * (1.0 - s_a)
            dgb = dm * yc * s_b * (1.0 - s_b)
            dp_ref[R_GA + s] = dga.astype(BF16)
            dp_ref[R_GBM + s] = dgb.astype(BF16)
            gbr_ref[R_GA + s] += _part8(dga)
            gbr_ref[R_GBM + s] += _part8(dgb)

        da = _nn(dya_ref[...], wpat_ref[...])
        doa_ref[...] = (da * z_a * sg_za).astype(BF16)
        dza = da * o_attn * (sg_za * (1.0 + z_a * (1.0 - sg_za)))
        dp_ref[R_ZA] = dza.astype(BF16)
        gbr_ref[R_ZA] += _part8(dza)

        dbb = _nt(dyc_ref[...], wpb_ref[...])
        for s in range(4):
            ux, gc, zc = rest_ref[R_UX + s], rest_ref[R_GC + s], rest_ref[R_ZC + s]
            u = gc * ux
            u1, u2, conv, sg = keep_ref[2, :, cols(s)], keep_ref[3, :, cols(s)], keep_ref[4, :, cols(s)], keep_ref[5, :, cols(s)]
            gb = rest_ref[R_GB + s]
            d_b = dbb[:, SLAB * s:SLAB * (s + 1)]
            szc = zc * sg
            dgb_ = d_b * conv * szc
            dconv = d_b * gb * szc
            dzc = d_b * gb * conv * (sg * (1.0 + zc * (1.0 - sg)))
            sv_ref[3, :, cols(s)] += _part8(dconv * u2)
            sv_ref[4, :, cols(s)] += _part8(dconv * u1)
            sv_ref[5, :, cols(s)] += _part8(dconv * u)
            nxt = carry_ref[:, cols(s)]
            d1 = jnp.where(row == tm - 1, nxt[0:1], pltpu.roll(dconv, tm - 1, 0))
            d2 = jnp.where(row == tm - 1, nxt[1:2], jnp.where(row == tm - 2, nxt[0:1], pltpu.roll(dconv, tm - 2, 0)))
            carry_ref[:, cols(s)] = dconv[0:8]
            du = cw_ref[2:3, cols(s)] * dconv + cw_ref[1:2, cols(s)] * d1 + cw_ref[0:1, cols(s)] * d2
            dgc, dux = du * ux, du * gc
            for slab, val in ((R_GB + s, dgb_), (R_ZC + s, dzc), (R_GC + s, dgc), (R_UX + s, dux)):
                dp_ref[slab] = val.astype(BF16)
                gbr_ref[slab] += _part8(val)

    def tile(i):
        return nt - 1 - i

    row_blk = lambda i: (tile(i), 0)
    slab_blk = lambda i: (0, tile(i), 0)
    const2 = lambda i: (0, 0)
    const3 = lambda i: (0, 0, 0)
    in_specs = [
        pl.BlockSpec((N_REST, tm, SLAB), slab_blk),
        pl.BlockSpec((N_REST, 8, SLAB), lambda i: (0, jnp.maximum(tile(i) * (tm // 8) - 1, 0), 0)),
        pl.BlockSpec((2, tm, SLAB), slab_blk), pl.BlockSpec((2, tm, SLAB), slab_blk), pl.BlockSpec((2, tm, SLAB), slab_blk),
        pl.BlockSpec((tm, D), row_blk), pl.BlockSpec((tm, D), row_blk),
        pl.BlockSpec((1, 3, D), lambda i: (tile(i) // tps, 0, 0)),
        pl.BlockSpec((3, D), const2), pl.BlockSpec((1, D), const2), pl.BlockSpec((1, D), const2), pl.BlockSpec((1, D), const2),
        pl.BlockSpec((D, SLAB), const2), pl.BlockSpec((D, D), const2), pl.BlockSpec((D, D), const2)]
    bf_rows = lambda: jax.ShapeDtypeStruct((t, D), BF16)
    out_shape = (
        jax.ShapeDtypeStruct((DP_SLABS, t, SLAB), BF16), jax.ShapeDtypeStruct((t, D), F32),
        jax.ShapeDtypeStruct((t, SLAB), BF16), jax.ShapeDtypeStruct((2, t, SLAB), F32),
        bf_rows(), bf_rows(), bf_rows(), bf_rows(), jax.ShapeDtypeStruct((t, SLAB), BF16), bf_rows(),
        jax.ShapeDtypeStruct((N_REST, 8, SLAB), F32), jax.ShapeDtypeStruct((7, 8, D), F32),
        jax.ShapeDtypeStruct((nbat, 8, D), F32))
    out_specs = (
        pl.BlockSpec((N_REST, tm, SLAB), slab_blk), pl.BlockSpec((tm, D), row_blk),
        pl.BlockSpec((tm, SLAB), row_blk), pl.BlockSpec((2, tm, SLAB), slab_blk),
        pl.BlockSpec((tm, D), row_blk), pl.BlockSpec((tm, D), row_blk), pl.BlockSpec((tm, D), row_blk),
        pl.BlockSpec((tm, D), row_blk), pl.BlockSpec((tm, SLAB), row_blk), pl.BlockSpec((tm, D), row_blk),
        pl.BlockSpec((N_REST, 8, SLAB), const3), pl.BlockSpec((7, 8, D), const3),
        pl.BlockSpec((1, 8, D), lambda i: (tile(i) // tps, 0, 0)))
    return _pcall(body, grid=(nt,), out_shape=out_shape, in_specs=in_specs, out_specs=out_specs,
                  scratch_shapes=[pltpu.VMEM((8, D), F32), pltpu.VMEM((6, tm, D), F32)], name="mid",
                  compiler_params=_params(56, ("arbitrary",)))(
        rest, rest, *ols, x, tgt, ada, cw, b_out, ln_g, ln_b, w_pa_t, w_pb, w_out)


def _tn_matmul(lhs, rhs, lhs_spec, n_steps, out_rows, out_index, name, after):
    t, n = rhs.shape

    def body(l_ref, r_ref, after_ref, o_ref):
        del after_ref
        o_ref[...] = _tn(l_ref[0] if len(l_ref.shape) == 3 else l_ref[...], r_ref[...])

    return _pcall(body, grid=(n_steps,), out_shape=jax.ShapeDtypeStruct((out_rows, n), F32),
                  in_specs=[lhs_spec, pl.BlockSpec((t, n), lambda j: (0, 0)), ANY],
                  out_specs=pl.BlockSpec((SLAB, n), out_index), name=name,
                  compiler_params=_params(48, ("parallel",)))(lhs, rhs, after)


def _grad_rows_2d(lhs, rhs, name, after):
    t, k = lhs.shape
    return _tn_matmul(lhs, rhs, pl.BlockSpec((t, SLAB), lambda j: (0, j)), k // SLAB, k, lambda j: (j, 0), name, after)


def _w_row_block(j):
    return (j + N_QKV) % N_SLAB


def _dp_slab(j):
    return jnp.where(j < N_REST, j, j + 2)


def _grad_w_in_t(dproj, h):
    t = h.shape[0]
    return _tn_matmul(dproj, h, pl.BlockSpec((1, t, SLAB), lambda j: (_dp_slab(j), 0, 0)), N_SLAB, D_IN,
                      lambda j: (_w_row_block(j), 0), "grad_w_in", h)


def _grad_h(dproj, w_in_t, gx0, x, ada, after, tm=512):
    t = x.shape[0]
    nbat = ada.shape[0]
    tps = (t // nbat) // tm

    def body(dp_ref, w_ref, gx0_ref, x_ref, ada_ref, after_ref, gx_ref, dss_ref):
        del after_ref
        i = pl.program_id(0)
        dh = None
        for j in range(N_SLAB):
            slab = j if j < N_REST else j + 2
            part = _nn(dp_ref[slab], w_ref[pl.ds(SLAB * ((j + N_QKV) % N_SLAB), SLAB), :])
            dh = part if dh is None else dh + part
        gx_ref[...] = gx0_ref[...] + dh * (1.0 + ada_ref[0, 1:2, :])

        @pl.when((i % tps) == 0)
        def _():
            dss_ref[...] = jnp.zeros_like(dss_ref)

        dss_ref[0, 0] += _part8(dh)
        dss_ref[0, 1] += _part8(dh * x_ref[...])

    return _pcall(
        body, grid=(t // tm,),
        out_shape=(jax.ShapeDtypeStruct((t, D), F32), jax.ShapeDtypeStruct((nbat, 2, 8, D), F32)),
        in_specs=[pl.BlockSpec((DP_SLABS, tm, SLAB), lambda i: (0, i, 0)),
                  pl.BlockSpec((D_IN, D), lambda i: (0, 0), pipeline_mode=pl.Buffered(1)),
                  pl.BlockSpec((tm, D), lambda i: (i, 0)), pl.BlockSpec((tm, D), lambda i: (i, 0)),
                  pl.BlockSpec((1, 3, D), lambda i: (i // tps, 0, 0)), ANY],
        out_specs=(pl.BlockSpec((tm, D), lambda i: (i, 0)),
                   pl.BlockSpec((1, 2, 8, D), lambda i: (i // tps, 0, 0, 0))),
        name="grad_h", compiler_params=_params(60, ("arbitrary",)))(dproj, w_in_t, gx0, x, ada, after)


def _chip(m):
    x, y, _ = _my_position()
    return (x ^ ((m >> 1) & 1), y ^ (m & 1))


def _exchange_siblings(grads):
    n = len(grads)

    def body(*refs):
        copies = _sibling_copies(refs[:n], refs[n:2 * n], refs[2 * n], refs[2 * n + 1])
        for cp in copies:
            cp.start()
        for cp in copies:
            cp.wait()

    return _pcall(body, out_shape=tuple(_sibling_zones(grads)), in_specs=[ANY] * n, out_specs=(ANY,) * n,
                  name="exchange_siblings", scratch_shapes=[pltpu.SemaphoreType.DMA((4 * n,))] * 2)(*grads)


def _sibling_zones(grads):
    return [jax.ShapeDtypeStruct((4, g.shape[0] // N_DEV, g.shape[1]), g.dtype) for g in grads]


def _sibling_copies(srcs, lands, send_sems, recv_sems):
    x, y, c = _my_position()
    copies = []
    for a, (src, land) in enumerate(zip(srcs, lands)):
        rows = land.shape[1]
        for m in range(4):
            dev = _flat(*_chip(m), 1 - c)
            copies.append(pltpu.make_async_remote_copy(
                src_ref=src.at[pl.ds(pl.multiple_of(dev * rows, 8), rows), :], dst_ref=land.at[m],
                send_sem=send_sems.at[4 * a + m], recv_sem=recv_sems.at[4 * a + m], device_id=(x, y, 1 - c),
                device_id_type=MESH))
    return copies


def _chip_copies(srcs, lands, send_sems, recv_sems):
    _, _, c = _my_position()
    return [pltpu.make_async_remote_copy(
        src_ref=srcs[a].at[m - 1], dst_ref=lands[a].at[m - 1], send_sem=send_sems.at[3 * a + m - 1],
        recv_sem=recv_sems.at[3 * a + m - 1], device_id=(*_chip(m), c), device_id_type=MESH)
        for a in range(len(srcs)) for m in range(1, 4)]


HBM = pl.BlockSpec(memory_space=pltpu.HBM)
SEM = pl.BlockSpec(memory_space=pltpu.SEMAPHORE)
SPLIT_COPY = pltpu.CompilerParams(has_side_effects=pltpu.SideEffectType.DATAFLOW_SIDE_EFFECTING)


def _start_copies(make_copies, n_sems, srcs, zones, name):
    n = len(srcs)

    def body(*refs):
        for cp in make_copies(refs[:n], refs[n:2 * n], refs[2 * n], refs[2 * n + 1]):
            cp.start()
        refs[-1][...] = jnp.zeros_like(refs[-1])

    hbm = tuple(pltpu.HBM(b.shape, b.dtype) for b in list(srcs) + list(zones))
    out_shape = (pltpu.SemaphoreType.DMA((n_sems,)), pltpu.SemaphoreType.DMA((n_sems,))) + hbm + (jax.ShapeDtypeStruct((8, 128), F32),)
    operands = [pltpu.with_memory_space_constraint(b, pltpu.HBM) for b in srcs]
    operands += [pltpu.with_memory_space_constraint(lax.empty(z.shape, z.dtype), pltpu.HBM) for z in zones]
    res = _pcall(body, out_shape=out_shape, in_specs=[HBM] * (2 * n), out_specs=(SEM, SEM) + (HBM,) * (2 * n) + (VMEM,),
                 input_output_aliases={i: 2 + i for i in range(2 * n)}, name=name, compiler_params=SPLIT_COPY)(*operands)
    return (res[0], res[1], res[2:2 + n], res[2 + n:2 + 2 * n]), res[-1]


def _wait_copies(make_copies, flight, after, name):
    send_sems, recv_sems, srcs, zones = flight
    n = len(srcs)

    def body(*refs):
        for cp in make_copies(refs[:n], refs[n:2 * n], refs[2 * n], refs[2 * n + 1]):
            cp.wait_send()
            cp.wait_recv()

    hbm = tuple(pltpu.HBM(b.shape, b.dtype) for b in list(srcs) + list(zones))
    res = _pcall(body, out_shape=hbm, in_specs=[HBM] * (2 * n) + [SEM, SEM, ANY], out_specs=(HBM,) * (2 * n),
                 input_output_aliases={i: i for i in range(2 * n)}, name=name, compiler_params=SPLIT_COPY)(
        *srcs, *zones, send_sems, recv_sems, after)
    return res[:n], res[n:]


def _pair_sums(devs, grads, lands, n_steps, name):
    n = len(grads)
    rows = [l.shape[1] for l in lands]
    rbs = [r // n_steps for r in rows]

    def body(devs_ref, *refs):
        del devs_ref
        g_refs, land_refs, outs = refs[:4 * n], refs[4 * n:5 * n], refs[5 * n:]
        for a in range(n):
            outs[2 * a][...] = g_refs[4 * a][...] + land_refs[a][0]
            for m in range(1, 4):
                outs[2 * a + 1][m - 1] = (g_refs[4 * a + m][...] + land_refs[a][m]).astype(BF16)

    def block_of(m, per_dev):
        return lambda i, devs_ref: (devs_ref[m] * per_dev + i, 0)

    in_specs = [pl.BlockSpec((rb, l.shape[2]), block_of(m, n_steps)) for rb, l in zip(rbs, lands) for m in range(4)]
    in_specs += [pl.BlockSpec((4, rb, l.shape[2]), lambda i, devs_ref: (0, i, 0)) for rb, l in zip(rbs, lands)]
    out_shape, out_specs = [], []
    for rb, l in zip(rbs, lands):
        out_shape += [jax.ShapeDtypeStruct(l.shape[1:], F32), jax.ShapeDtypeStruct((3,) + l.shape[1:], BF16)]
        out_specs += [pl.BlockSpec((rb, l.shape[2]), lambda i, devs_ref: (i, 0)),
                      pl.BlockSpec((3, rb, l.shape[2]), lambda i, devs_ref: (0, i, 0))]
    grid_spec = pltpu.PrefetchScalarGridSpec(num_scalar_prefetch=1, grid=(n_steps,), in_specs=in_specs, out_specs=tuple(out_specs))
    res = _pcall(body, grid_spec=grid_spec, out_shape=tuple(out_shape), name=name,
                 compiler_params=_params(48, ("parallel",)))(devs, *[g for g in grads for _ in range(4)], *lands)
    return res[0::2], res[1::2]


def _final_sums(mine, lands, n_steps, name):
    n = len(mine)
    rbs = [o.shape[0] // n_steps for o in mine]

    def body(*refs):
        mine_refs, land_refs, outs = refs[:n], refs[n:2 * n], refs[2 * n:]
        for a in range(n):
            tot = mine_refs[a][...]
            for m in range(3):
                tot = tot + land_refs[a][m].astype(F32)
            outs[a][...] = tot

    in_specs = ([pl.BlockSpec((rb, o.shape[1]), lambda i: (i, 0)) for rb, o in zip(rbs, mine)]
                + [pl.BlockSpec((3, rb, o.shape[1]), lambda i: (0, i, 0)) for rb, o in zip(rbs, mine)])
    out_specs = tuple(pl.BlockSpec((rb, o.shape[1]), lambda i: (i, 0)) for rb, o in zip(rbs, mine))
    out_shape = tuple(jax.ShapeDtypeStruct(o.shape, F32) for o in mine)
    return _pcall(body, grid=(n_steps,), out_shape=out_shape, in_specs=in_specs, out_specs=out_specs, name=name,
                  compiler_params=_params(32, ("parallel",)))(*mine, *lands)


def _reduce_scatter_begin(big, small_after_start):
    flight, token = _start_copies(_sibling_copies, 4, [big], _sibling_zones([big]), "siblings_start")
    small = small_after_start(token)
    (big,), big_lands = _wait_copies(_sibling_copies, flight, small[-1], "siblings_wait")
    small_lands = _exchange_siblings(small)
    c = lax.axis_index("c")
    devs = jnp.stack([_flat(*_chip(m), c) for m in range(4)]).astype(jnp.int32)
    big_mine, big_send = _pair_sums(devs, [big], big_lands, 4, "pair_sums_w_in")
    small_mine, small_send = _pair_sums(devs, small, small_lands, 1, "pair_sums_rest")
    bufs = list(big_send) + list(small_send)
    flight, token = _start_copies(_chip_copies, 3 * len(bufs), bufs, bufs, "chips_start")
    return (flight, list(big_mine) + list(small_mine)), token


def _reduce_scatter_end(state, after):
    flight, mine = state
    _, got = _wait_copies(_chip_copies, flight, after, "chips_wait")
    big = _final_sums(mine[:1], got[:1], 4, "final_sums_w_in")
    small = _final_sums(mine[1:], got[1:], 1, "final_sums_rest")
    return list(big) + list(small)


def _adamw(w, g, m, v):
    m_new = B1 * m + (1.0 - B1) * g
    v_new = B2 * v + (1.0 - B2) * (g * g)
    m_hat = m_new / (1.0 - B1 ** STEP)
    v_hat = v_new / (1.0 - B2 ** STEP)
    delta = -LR * (m_hat / (jnp.sqrt(v_hat) + EPS) + WD * w)
    return delta, m_new, v_new


def _adam_rows(g, w, m, v, n_steps, name):
    rows, ncol = w.shape
    blk = pl.BlockSpec((rows // n_steps, ncol), lambda i: (i, 0))

    def body(g_ref, w_ref, m_ref, v_ref, d_ref, mo_ref, vo_ref):
        d_ref[...], mo_ref[...], vo_ref[...] = _adamw(w_ref[...], g_ref[...], m_ref[...], v_ref[...])

    shape = jax.ShapeDtypeStruct(w.shape, F32)
    return _pcall(body, grid=(n_steps,), out_shape=(shape,) * 3, in_specs=[blk] * 4, out_specs=(blk,) * 3, name=name,
                  compiler_params=_params(32, ("parallel",)))(g, w, m, v)


def _adam_transposed(g_t, w, m, v, name):
    n, k = g_t.shape
    rb = min(k, 128)

    def body(gt_ref, w_ref, m_ref, v_ref, g_ref, d_ref, mo_ref, vo_ref):
        for src, skip, dst, size in _column_chunks(n):
            sl = pl.ds(dst, size)
            g = gt_ref[pl.ds(src, 128), :].T[:, skip:]
            delta, m_new, v_new = _adamw(w_ref[:, sl], g, m_ref[:, sl], v_ref[:, sl])
            g_ref[:, sl], d_ref[:, sl], mo_ref[:, sl], vo_ref[:, sl] = g, delta, m_new, v_new

    shape = jax.ShapeDtypeStruct(w.shape, F32)
    rows = pl.BlockSpec((rb, n), lambda i: (i, 0))
    return _pcall(body, grid=(k // rb,), out_shape=(shape,) * 4,
                  in_specs=[pl.BlockSpec((n, rb), lambda i: (0, i)), rows, rows, rows], out_specs=(rows,) * 4, name=name,
                  compiler_params=_params(32, ("parallel",)))(g_t, w, m, v)


def _adam_many(items, name):
    n = len(items)

    def body(*refs):
        ins, outs = refs[:4 * n], refs[4 * n:]
        for a in range(n):
            w_ref, g_ref, m_ref, v_ref = ins[4 * a:4 * a + 4]
            delta, m_new, v_new = _adamw(w_ref[...], g_ref[...], m_ref[...], v_ref[...])
            outs[3 * a][...], outs[3 * a + 1][...], outs[3 * a + 2][...] = delta, m_new, v_new

    out_shape = tuple(jax.ShapeDtypeStruct(it[0].shape, F32) for it in items for _ in range(3))
    flat = [arr for it in items for arr in it]
    res = _pcall(body, grid=(1,), out_shape=out_shape, in_specs=[_whole(a) for a in flat],
                 out_specs=tuple(_whole(o) for o in out_shape), name=name, compiler_params=_params(32))(*flat)
    return [tuple(res[3 * a:3 * a + 3]) for a in range(n)]


def _adam_w_ada(cact_all, dada_mine, w, m, v):
    def body(c_ref, d_ref, w_ref, m_ref, v_ref, g_ref, dl_ref, mo_ref, vo_ref):
        g = _tn(c_ref[...].astype(BF16), d_ref[...].astype(BF16))
        delta, m_new, v_new = _adamw(w_ref[...], g, m_ref[...], v_ref[...])
        g_ref[...], dl_ref[...], mo_ref[...], vo_ref[...] = g, delta, m_new, v_new

    shape = jax.ShapeDtypeStruct(w.shape, F32)
    operands = (cact_all, dada_mine, w, m, v)
    return _pcall(body, grid=(1,), out_shape=(shape,) * 4, in_specs=[_whole(a) for a in operands],
                  out_specs=(_whole(w),) * 4, name="adam_w_ada", compiler_params=_params(32))(*operands)


def kernel(x, c, w_ada, b_ada, w_in, b_in, conv_w, w_proj_attn, w_proj_conv, w_out, b_out, ln_g, ln_b, loss_target, m_w_ada, m_b_ada, m_w_in, m_b_in, m_conv_w, m_w_proj_attn, m_w_proj_conv, m_w_out, m_b_out, m_ln_g, m_ln_b, v_w_ada, v_b_ada, v_w_in, v_b_in, v_conv_w, v_w_proj_attn, v_w_proj_conv, v_w_out, v_b_out, v_ln_g, v_ln_b):
    nbat, seq, _ = x.shape
    t = nbat * seq
    me = _flat(*_my_position())
    x2, tgt2 = x.reshape(t, D), loss_target.reshape(t, D)
    sq = lambda a: a.reshape(a.shape[1:])

    tr = lambda a: a[0].T
    w_in_rows = tr(w_in)
    w_in_t_s = _cast_rows(w_in_rows, 4, "cast_w_in")
    w_pa_t_s, w_pb_s, w_out_s, cact_s, cw_s = _prep(sq(w_proj_attn), sq(w_proj_conv), sq(w_out), c, sq(conv_w))
    w_in_t, w_pa_t, w_pb, w_o, cact_g, cw_g = _gather_rows([w_in_t_s, w_pa_t_s, w_pb_s, w_out_s, cact_s, cw_s])
    cact_all = cact_g.reshape(N_DEV, 8, D)[:, :nbat].reshape(N_DEV * nbat, D)
    cw = cw_g.reshape(N_DEV, 8, -1)[:, :3].transpose(1, 0, 2).reshape(3, D)

    ncol = w_ada.shape[2]
    b_ada_mine = lax.dynamic_slice(b_ada, (0, me * ncol), (1, ncol))
    ada_slots = _ada_forward(cact_all, sq(w_ada), b_ada_mine)
    ada_all = ada_slots.transpose(1, 0, 2).reshape(N_DEV * nbat, 3, D)
    ada = lax.dynamic_slice(ada_all, (me * nbat, 0, 0), (nbat, 3, D))

    h = _make_h(x2, ada)
    qkv, rest = _project(h, w_in_t, b_in.reshape(N_SLAB, 1, SLAB))
    ols = [_attn_forward(qkv, g, nbat) for g in range(3)]
    (dproj, gx0, do_attn, ol_tot, merged, do_f, bbs, dyc, a_bf, dya, gb_rest, svec, dgate) = _mid(
        rest, ols, x2, tgt2, ada, cw, b_out, ln_g, ln_b, w_pa_t, w_pb, w_o)

    gb_qkv = []
    for g in range(3):
        dproj, gb = _attn_backward(qkv, do_attn, ol_tot, dproj, g, nbat)
        gb_qkv.append(gb)
    g_w_in_t = _grad_w_in_t(dproj, h)

    def small_grads(token):
        g_w_out = _grad_rows_2d(merged, do_f, "grad_w_out", token)
        g_w_pb = _grad_rows_2d(bbs, dyc, "grad_w_proj_conv", g_w_out)
        g_w_pa_t = _grad_rows_2d(dya, a_bf, "grad_w_proj_attn", g_w_pb)
        return [g_w_out, g_w_pb, g_w_pa_t]

    rs_state, token = _reduce_scatter_begin(g_w_in_t, small_grads)
    grad_x, dss = _grad_h(dproj, w_in_t, gx0, x2, ada, token)

    rows8, tot, g_bada = _small_reduce(gb_rest, gb_qkv, svec, dgate, dss)
    g_in_t, g_out, g_pb, g_pa_t = _reduce_scatter_end(rs_state, tot)
    loss = tot[0, P_LOSS]
    dada_all = rows8[:, 0, P_DADA:].reshape(N_DEV * nbat, 3 * D)
    dada_mine = lax.dynamic_slice(dada_all, (0, me * ncol), (N_DEV * nbat, ncol))

    d_win_t, nm_win_t, nv_win_t = _adam_rows(g_in_t, w_in_rows, tr(m_w_in), tr(v_w_in), 8, "adam_w_in")
    g_win, d_win, nm_win, nv_win = g_in_t.T, d_win_t.T, nm_win_t.T, nv_win_t.T
    g_wpa, d_wpa, nm_wpa, nv_wpa = _adam_transposed(g_pa_t, sq(w_proj_attn), sq(m_w_proj_attn), sq(v_w_proj_attn), "adam_w_proj_attn")
    g_wada, d_wada, nm_wada, nv_wada = _adam_w_ada(cact_all, dada_mine, sq(w_ada), sq(m_w_ada), sq(v_w_ada))
    g_bin = tot[:, P_BIN:P_BIN + D_IN]
    g_bout = tot[:, P_BOUT:P_BOUT + D]
    g_lng = tot[:, P_LNG:P_LNG + D]
    g_lnb = tot[:, P_LNB:P_LNB + D]
    g_conv = lax.dynamic_slice(tot[:, P_CONV:P_CONV + 3 * D].reshape(3, D), (0, me * cw_s.shape[1]), (3, cw_s.shape[1]))
    upd = _adam_many([
        (sq(w_proj_conv), g_pb, sq(m_w_proj_conv), sq(v_w_proj_conv)),
        (sq(w_out), g_out, sq(m_w_out), sq(v_w_out)),
        (b_ada, g_bada, m_b_ada, v_b_ada), (b_in, g_bin, m_b_in, v_b_in), (sq(conv_w), g_conv, sq(m_conv_w), sq(v_conv_w)),
        (b_out, g_bout, m_b_out, v_b_out), (ln_g, g_lng, m_ln_g, v_ln_g), (ln_b, g_lnb, m_ln_b, v_ln_b)], "adam_rest")
    (d_wpb, nm_wpb, nv_wpb), (d_wout, nm_wout, nv_wout), (d_bada, nm_bada, nv_bada), (d_bin, nm_bin, nv_bin), \
        (d_conv, nm_conv, nv_conv), (d_bout, nm_bout, nv_bout), (d_lng, nm_lng, nv_lng), (d_lnb, nm_lnb, nv_lnb) = upd

    ex = lambda a: a.reshape((1,) + a.shape)
    grads = [ex(g_wada), g_bada, ex(g_win), g_bin, ex(g_conv), ex(g_wpa), ex(g_pb), ex(g_out), g_bout, g_lng, g_lnb]
    deltas = [ex(d_wada), d_bada, ex(d_win), d_bin, ex(d_conv), ex(d_wpa), ex(d_wpb), ex(d_wout), d_bout, d_lng, d_lnb]
    new_m = [ex(nm_wada), nm_bada, ex(nm_win), nm_bin, ex(nm_conv), ex(nm_wpa), ex(nm_wpb), ex(nm_wout), nm_bout, nm_lng, nm_lnb]
    new_v = [ex(nv_wada), nv_bada, ex(nv_win), nv_bin, ex(nv_conv), ex(nv_wpa), ex(nv_wpb), ex(nv_wout), nv_bout, nv_lng, nv_lnb]
    return (loss, grad_x.reshape(x.shape), *grads, *deltas, *new_m, *new_v)
```

```python
import functools

import jax
import jax.numpy as jnp
from jax import lax
from jax.experimental import pallas as pl
from jax.experimental.pallas import tpu as pltpu

F32, BF16 = jnp.float32, jnp.bfloat16
MESH = pl.DeviceIdType.MESH
N_DEV = 8
D = 1024
SLAB = 256
N_QKV, N_REST = 9, 25
N_SLAB = N_QKV + N_REST
D_IN = N_SLAB * SLAB
DP_SLABS = 36
BLK = 128
GROUPS = ((128, 1), (512, 4), (2048, 16))
ALPHA = 2.0 ** 0.25
LN_EPS = 1e-5
LR, B1, B2, EPS, WD, STEP = 0.001, 0.9, 0.999, 1e-08, 0.01, 10
R_ZA, R_UX, R_GB, R_GC, R_ZC, R_GA, R_GBM = 0, 1, 5, 9, 13, 17, 21
P_BIN, P_BOUT, P_LNG, P_LNB, P_CONV, P_LOSS, P_DADA = 0, 8704, 9728, 10752, 11776, 14848, 14976
MIB = 1024 * 1024


def _pcall(body, *, out_shape, out_specs=None, **kw):
    def pin_out(shape, spec):
        blocked = isinstance(shape, jax.ShapeDtypeStruct) and getattr(spec, "block_shape", None) is not None
        return pltpu.HBM(shape.shape, shape.dtype) if blocked else shape

    n_scalar = 0
    if out_specs is None:
        specs = kw["grid_spec"].out_specs
        n_scalar = kw["grid_spec"].num_scalar_prefetch
    else:
        kw["out_specs"] = specs = out_specs
    if isinstance(out_shape, (tuple, list)):
        out_shape = tuple(pin_out(s, p) for s, p in zip(out_shape, specs))
    else:
        out_shape = pin_out(out_shape, specs)
    call = pl.pallas_call(body, out_shape=out_shape, **kw)

    def run(*operands):
        def pin(o):
            is_data = jnp.issubdtype(o.dtype, jnp.floating) or jnp.issubdtype(o.dtype, jnp.integer)
            return pltpu.with_memory_space_constraint(o, pltpu.HBM) if is_data else o
        return call(*operands[:n_scalar], *[pin(o) for o in operands[n_scalar:]])

    return run

ANY = pl.BlockSpec(memory_space=pl.ANY)
VMEM = pl.BlockSpec(memory_space=pltpu.VMEM)


def _whole(a):
    return pl.BlockSpec(a.shape, lambda i: (0,) * len(a.shape))


def _params(vmem_mib=None, sem=None):
    kw = {}
    if vmem_mib is not None:
        kw["vmem_limit_bytes"] = vmem_mib * MIB
    if sem is not None:
        kw["dimension_semantics"] = sem
    return pltpu.CompilerParams(**kw)


def _nn(a, b):
    return jnp.dot(a, b, preferred_element_type=F32)


def _nt(a, b):
    return lax.dot_general(a, b, (((1,), (1,)), ((), ())), preferred_element_type=F32)


def _tn(a, b):
    return lax.dot_general(a, b, (((0,), (0,)), ((), ())), preferred_element_type=F32)


def _sigmoid(v):
    return 1.0 / (1.0 + jnp.exp(-v))


def _part8(v):
    return v.reshape(v.shape[0] // 8, 8, v.shape[1]).sum(axis=0)


def _my_position():
    return lax.axis_index("x"), lax.axis_index("y"), lax.axis_index("c")


def _flat(px, py, pc):
    return 4 * px + 2 * py + pc


def _peer(mask):
    x, y, c = _my_position()
    return (x ^ ((mask >> 2) & 1), y ^ ((mask >> 1) & 1), c ^ (mask & 1))


def _column_chunks(n):
    chunks = [(128 * a, 0, 128 * a, 128) for a in range(n // 128)]
    if n % 128:
        chunks.append((n - 128, 128 - n % 128, 128 * (n // 128), n % 128))
    return chunks


def _cast_rows(w, n_steps, name):
    rows, ncol = w.shape
    blk = pl.BlockSpec((rows // n_steps, ncol), lambda i: (i, 0))

    def body(w_ref, o_ref):
        o_ref[...] = w_ref[...].astype(BF16)

    return _pcall(body, grid=(n_steps,), out_shape=jax.ShapeDtypeStruct(w.shape, BF16), in_specs=[blk], out_specs=blk,
                  name=name, compiler_params=_params(16, ("parallel",)))(w)


def _prep(w_pa, w_pb, w_out, c, conv_w):
    def body(wpa_ref, wpb_ref, wout_ref, c_ref, cw_ref, wpat_ref, wpb_o, wout_o, cact_ref, cwp_ref):
        wpat_ref[...] = wpa_ref[...].T.astype(BF16)
        wpb_o[...] = wpb_ref[...].astype(BF16)
        wout_o[...] = wout_ref[...].astype(BF16)
        cv = c_ref[...]
        cact_ref[...] = jnp.zeros_like(cact_ref)
        cact_ref[pl.ds(0, cv.shape[0]), :] = cv * _sigmoid(cv)
        cwp_ref[...] = jnp.zeros_like(cwp_ref)
        cwp_ref[pl.ds(0, 3), :] = cw_ref[...]

    out_shape = (jax.ShapeDtypeStruct((w_pa.shape[1], w_pa.shape[0]), BF16),
                 jax.ShapeDtypeStruct(w_pb.shape, BF16), jax.ShapeDtypeStruct(w_out.shape, BF16),
                 jax.ShapeDtypeStruct((8, D), F32), jax.ShapeDtypeStruct((8, conv_w.shape[1]), F32))
    operands = (w_pa, w_pb, w_out, c, conv_w)
    return _pcall(body, grid=(1,), out_shape=out_shape, in_specs=[_whole(a) for a in operands],
                  out_specs=tuple(_whole(o) for o in out_shape), name="prep", compiler_params=_params(16))(*operands)


def _gather_rows(shards):
    n = len(shards)

    def body(*refs):
        srcs, outs = refs[:n], refs[n:2 * n]
        send_sems, recv_sems, local_sems = refs[2 * n:]
        x, y, c = _my_position()
        me, sibling = (x, y, c), (x, y, 1 - c)
        chips = [(1 - x, y), (x, 1 - y), (1 - x, 1 - y)]

        def rows(a, px, py, pc):
            r = shards[a].shape[0]
            return outs[a].at[pl.ds(pl.multiple_of(_flat(px, py, pc) * r, r), r), :]

        def copy(a, k, block, to, src=None):
            return pltpu.make_async_remote_copy(
                src_ref=rows(a, *block) if src is None else src, dst_ref=rows(a, *block),
                send_sem=send_sems.at[7 * a + k], recv_sem=recv_sems.at[7 * a + k], device_id=to, device_id_type=MESH)

        mine = [pltpu.make_async_copy(srcs[a], rows(a, *me), local_sems.at[a]) for a in range(n)]
        for cp in mine:
            cp.start()
        first = []
        for a in range(n):
            first.append(copy(a, 0, me, sibling, src=srcs[a]))
            first += [copy(a, 1 + j, me, (*chip, c), src=srcs[a]) for j, chip in enumerate(chips)]
        for cp in first:
            cp.start()
        passed = []
        for j, chip in enumerate(chips):
            for a in range(n):
                copy(a, 1 + j, (*chip, c), me).wait_recv()
                cp = copy(a, 4 + j, (*chip, c), sibling)
                cp.start()
                passed.append(cp)
        for a in range(n):
            copy(a, 0, sibling, me).wait_recv()
        for j, chip in enumerate(chips):
            for a in range(n):
                copy(a, 4 + j, (*chip, 1 - c), me).wait_recv()
        for cp in first + passed:
            cp.wait_send()
        for cp in mine:
            cp.wait()

    out_shape = tuple(jax.ShapeDtypeStruct((N_DEV * s.shape[0], s.shape[1]), s.dtype) for s in shards)
    return _pcall(body, out_shape=out_shape, in_specs=[ANY] * n, out_specs=(ANY,) * n, name="gather_rows",
                  scratch_shapes=[pltpu.SemaphoreType.DMA((7 * n,)), pltpu.SemaphoreType.DMA((7 * n,)),
                                  pltpu.SemaphoreType.DMA((n,))])(*shards)


def _exchange_slots(out_ref, send_sems, recv_sems):
    me = _flat(*_my_position())
    copies = []
    for mask in range(1, N_DEV):
        peer = _peer(mask)
        copies.append((mask, pltpu.make_async_remote_copy(
            src_ref=out_ref.at[me], dst_ref=out_ref.at[me], send_sem=send_sems.at[mask - 1],
            recv_sem=recv_sems.at[mask - 1], device_id=peer, device_id_type=MESH)))
    for _, cp in copies:
        cp.start()
    for mask, _ in copies:
        peer = _peer(mask)
        pltpu.make_async_remote_copy(
            src_ref=out_ref.at[_flat(*peer)], dst_ref=out_ref.at[_flat(*peer)], send_sem=send_sems.at[mask - 1],
            recv_sem=recv_sems.at[mask - 1], device_id=peer, device_id_type=MESH).wait_recv()
    for _, cp in copies:
        cp.wait_send()


def _ada_forward(cact_all, w_ada, b_ada_mine):
    nb, ncol = cact_all.shape[0], w_ada.shape[1]

    def body(c_ref, w_ref, b_ref, out_ref, send_sems, recv_sems):
        me = _flat(*_my_position())
        out_ref[me] = _nn(c_ref[...].astype(BF16), w_ref[...].astype(BF16)) + b_ref[...]
        _exchange_slots(out_ref, send_sems, recv_sems)

    operands = (cact_all, w_ada, b_ada_mine)
    return _pcall(body, grid=(1,), out_shape=jax.ShapeDtypeStruct((N_DEV, nb, ncol), F32),
                  in_specs=[_whole(a) for a in operands], out_specs=VMEM,
                  scratch_shapes=[pltpu.SemaphoreType.DMA((7,)), pltpu.SemaphoreType.DMA((7,))], name="ada_forward",
                  compiler_params=_params(16))(*operands)


def _small_reduce(gb_rest, gb_qkv, svec, dgate, dss):
    nbat = dgate.shape[0]

    def body(gbr_ref, q0_ref, q1_ref, q2_ref, sv_ref, dg_ref, dss_ref, rows_ref, tot_ref, gbada_ref, send_sems, recv_sems):
        me = _flat(*_my_position())

        def put(off, v):
            rows_ref[me, :, pl.ds(off, v.shape[1])] = v

        def row(v):
            return jnp.sum(v, axis=0, keepdims=True)

        for g, q_ref in enumerate((q0_ref, q1_ref, q2_ref)):
            for which in range(3):
                put(P_BIN + SLAB * (3 * which + g), row(q_ref[which]))
        for s in range(N_REST):
            put(P_BIN + SLAB * (N_QKV + s), row(gbr_ref[s]))
        put(P_LNG, row(sv_ref[0]))
        put(P_LNB, row(sv_ref[1]))
        put(P_BOUT, row(sv_ref[2]))
        for j in range(3):
            put(P_CONV + D * j, row(sv_ref[3 + j]))
        loss = (0.5 / D) * jnp.sum(row(sv_ref[6]), axis=1, keepdims=True)
        put(P_LOSS, jnp.broadcast_to(loss, (1, 128)))
        for b in range(nbat):
            put(P_DADA + 3 * D * b, row(dss_ref[b, 0]))
            put(P_DADA + 3 * D * b + D, row(dss_ref[b, 1]))
            put(P_DADA + 3 * D * b + 2 * D, row(dg_ref[b]))
        _exchange_slots(rows_ref, send_sems, recv_sems)
        tot = rows_ref[0]
        for k in range(1, N_DEV):
            tot = tot + rows_ref[k]
        tot_ref[...] = tot
        gbada = tot[:, P_DADA:P_DADA + 3 * D]
        for b in range(1, nbat):
            gbada = gbada + tot[:, P_DADA + 3 * D * b:P_DADA + 3 * D * (b + 1)]
        gbada_ref[...] = gbada

    p_len = P_DADA + nbat * 3 * D
    out_shape = (jax.ShapeDtypeStruct((N_DEV, 1, p_len), F32), jax.ShapeDtypeStruct((1, p_len), F32),
                 jax.ShapeDtypeStruct((1, 3 * D), F32))
    operands = (gb_rest, *gb_qkv, svec, dgate, dss)
    return _pcall(body, grid=(1,), out_shape=out_shape, in_specs=[_whole(a) for a in operands],
                  out_specs=(VMEM, _whole(out_shape[1]), _whole(out_shape[2])),
                  scratch_shapes=[pltpu.SemaphoreType.DMA((7,)), pltpu.SemaphoreType.DMA((7,))], name="small_reduce",
                  compiler_params=_params(16))(*operands)


def _make_h(x, ada, tm=512):
    t = x.shape[0]
    tps = (t // ada.shape[0]) // tm

    def body(x_ref, ada_ref, h_ref):
        h_ref[...] = (x_ref[...] * (1.0 + ada_ref[0, 1:2, :]) + ada_ref[0, 0:1, :]).astype(BF16)

    return _pcall(body, grid=(t // tm,), out_shape=jax.ShapeDtypeStruct((t, D), BF16),
                  in_specs=[pl.BlockSpec((tm, D), lambda i: (i, 0)), pl.BlockSpec((1, 3, D), lambda i: (i // tps, 0, 0))],
                  out_specs=pl.BlockSpec((tm, D), lambda i: (i, 0)), name="make_h",
                  compiler_params=_params(32, ("parallel",)))(x, ada)


PIECE = 64
ARRIVAL_RANK = (0, 1, 3, 5, 2, 4, 6, 7)
SLOT_MASK = (1, 4, 2, 6, 5, 3, 7)


def _arrival_tables(shard_rows):
    import numpy as np
    table = np.zeros((N_DEV, N_SLAB + 7), np.int32)
    lo = [(SLAB * j) // shard_rows for j in range(N_SLAB)]
    hi = [(SLAB * j + SLAB - 1) // shard_rows for j in range(N_SLAB)]
    for k in range(N_DEV):
        rank = [ARRIVAL_RANK[s ^ k] for s in range(N_DEV)]
        order = sorted(range(N_SLAB), key=lambda j: (max(rank[lo[j]], rank[hi[j]]), j))
        table[k, :N_SLAB] = order
        for slot, mask in enumerate(SLOT_MASK):
            s = k ^ mask
            table[k, N_SLAB + slot] = min(t for t, j in enumerate(order) if s in (lo[j], hi[j]))
    return table


def _project_gather(shard, h, b_in3, others):
    t = h.shape[0]
    n_o = len(others)
    srows = shard.shape[0]
    shards = [shard] + list(others)
    table = jnp.asarray(_arrival_tables(srows))

    def body(tbl_ref, *refs):
        srcs = [refs[0]] + list(refs[3:3 + n_o])
        h_ref, b_ref = refs[1], refs[2]
        outs = [refs[3 + n_o]] + list(refs[6 + n_o:6 + 2 * n_o])
        qkv_ref, rest_ref = refs[4 + n_o], refs[5 + n_o]
        (wtile, obf, of32, send_sems, recv_sems, local_sems, tile_sems, obf_sems, of32_sems) = refs[6 + 2 * n_o:]
        w_full = outs[0]
        x, y, c = _my_position()
        k = _flat(x, y, c)
        me, sibling = (x, y, c), (x, y, 1 - c)
        chips = [(1 - x, y), (x, 1 - y), (1 - x, 1 - y)]

        def rows(a, px, py, pc):
            r = shards[a].shape[0]
            return outs[a].at[pl.ds(pl.multiple_of(_flat(px, py, pc) * r, r), r), :]

        def copy(a, slot, block, to, src=None):
            return pltpu.make_async_remote_copy(
                src_ref=rows(a, *block) if src is None else src, dst_ref=rows(a, *block),
                send_sem=send_sems.at[7 * a + slot], recv_sem=recv_sems.at[7 * a + slot], device_id=to, device_id_type=MESH)

        mine = [pltpu.make_async_copy(srcs[a], rows(a, *me), local_sems.at[a]) for a in range(1 + n_o)]
        first = []
        for a in range(1 + n_o):
            first.append(copy(a, 0, me, sibling, src=srcs[a]))
            first += [copy(a, 1 + j, me, (*chip, c), src=srcs[a]) for j, chip in enumerate(chips)]
        for cp in mine + first:
            cp.start()

        def arrive(a, slot):
            if slot == 0:
                copy(a, 0, sibling, me).wait_recv()
            elif slot < 4:
                copy(a, slot, (*chips[slot - 1], c), me).wait_recv()
                copy(a, slot + 3, (*chips[slot - 1], c), sibling).start()
            else:
                copy(a, slot, (*chips[slot - 4], 1 - c), me).wait_recv()

        def arrive_for(step):
            for slot in range(7):
                @pl.when(tbl_ref[k, N_SLAB + slot] == step)
                def _():
                    arrive(0, slot)

        def fetch(step, buf):
            slab = tbl_ref[k, step]
            for p in range(SLAB // PIECE):
                g0 = slab * SLAB + PIECE * p
                own = (g0 >= k * srows) & (g0 < (k + 1) * srows)
                dst = wtile.at[buf, pl.ds(PIECE * p, PIECE), :]

                @pl.when(own)
                def _():
                    pltpu.make_async_copy(srcs[0].at[pl.ds(pl.multiple_of(g0 - k * srows, PIECE), PIECE), :], dst, tile_sems.at[buf]).start()

                @pl.when(jnp.logical_not(own))
                def _():
                    pltpu.make_async_copy(w_full.at[pl.ds(pl.multiple_of(g0, PIECE), PIECE), :], dst, tile_sems.at[buf]).start()

        def wait_tile(buf):
            pltpu.make_async_copy(w_full.at[pl.ds(0, SLAB), :], wtile.at[buf], tile_sems.at[buf]).wait()

        def put(buf_ref, sems, dst_ref, count, value):
            b = count % 2

            @pl.when(count >= 2)
            def _():
                pltpu.make_async_copy(buf_ref.at[b], dst_ref, sems.at[b]).wait()

            buf_ref[b] = value
            pltpu.make_async_copy(buf_ref.at[b], dst_ref, sems.at[b]).start()

        def drain(buf_ref, sems, dst_ref, count):
            for back in (1, 2):
                @pl.when(count >= back)
                def _():
                    pltpu.make_async_copy(buf_ref.at[(count - back) % 2], dst_ref, sems.at[(count - back) % 2]).wait()

        arrive_for(0)
        fetch(0, 0)

        def step(s, carry):
            n_bf, n_f32 = carry
            buf = s % 2

            @pl.when(s + 1 < N_SLAB)
            def _():
                arrive_for(s + 1)
                fetch(s + 1, 1 - buf)

            wait_tile(buf)
            slab = tbl_ref[k, s]
            v = _nt(h_ref[...], wtile[buf]) + b_ref[slab]
            is_qkv = slab < N_QKV

            @pl.when(is_qkv)
            def _():
                put(obf, obf_sems, qkv_ref.at[jnp.minimum(slab, N_QKV - 1)], n_bf, v.astype(BF16))

            @pl.when(jnp.logical_not(is_qkv))
            def _():
                put(of32, of32_sems, rest_ref.at[jnp.maximum(slab - N_QKV, 0)], n_f32, v)

            return n_bf + is_qkv.astype(jnp.int32), n_f32 + 1 - is_qkv.astype(jnp.int32)

        n_bf, n_f32 = lax.fori_loop(0, N_SLAB, step, (jnp.int32(0), jnp.int32(0)))
        drain(obf, obf_sems, qkv_ref.at[0], n_bf)
        drain(of32, of32_sems, rest_ref.at[0], n_f32)

        for slots in ((1, 2, 3), (0, 4, 5, 6)):
            for a in range(1, 1 + n_o):
                for slot in slots:
                    arrive(a, slot)
        for cp in first:
            cp.wait_send()
        for a in range(1 + n_o):
            for j, chip in enumerate(chips):
                copy(a, 4 + j, (*chip, c), sibling).wait_send()
        for cp in mine:
            cp.wait()

    out_shape = ((jax.ShapeDtypeStruct((N_DEV * srows, D), BF16), jax.ShapeDtypeStruct((N_QKV, t, SLAB), BF16),
                  jax.ShapeDtypeStruct((N_REST, t, SLAB), F32))
                 + tuple(jax.ShapeDtypeStruct((N_DEV * o.shape[0], o.shape[1]), o.dtype) for o in others))
    n_all = 1 + n_o
    grid_spec = pltpu.PrefetchScalarGridSpec(
        num_scalar_prefetch=1, grid=(1,),
        in_specs=[ANY, pl.BlockSpec((t, D), lambda i, tbl: (0, 0), pipeline_mode=pl.Buffered(1)),
                  pl.BlockSpec((N_SLAB, 1, SLAB), lambda i, tbl: (0, 0, 0))] + [ANY] * n_o,
        out_specs=(ANY,) * (3 + n_o),
        scratch_shapes=[pltpu.VMEM((2, SLAB, D), BF16), pltpu.VMEM((2, t, SLAB), BF16), pltpu.VMEM((2, t, SLAB), F32),
                        pltpu.SemaphoreType.DMA((7 * n_all,)), pltpu.SemaphoreType.DMA((7 * n_all,)),
                        pltpu.SemaphoreType.DMA((n_all,)), pltpu.SemaphoreType.DMA((2,)), pltpu.SemaphoreType.DMA((2,)),
                        pltpu.SemaphoreType.DMA((2,))])
    res = _pcall(body, grid_spec=grid_spec, out_shape=out_shape, name="project_gather",
                 compiler_params=_params(48, ("arbitrary",)))(table, shard, h, b_in3, *others)
    return res[0], res[1], res[2], list(res[3:])


def _bias_tables(g):
    window, dil = GROUPS[g]
    span = window // dil
    qi = jnp.arange(BLK)[:, None]
    kj = jnp.arange(2 * BLK)[None, :]
    delta = qi + BLK - kj
    valid = (delta >= 0) & (delta <= span)
    heads = jnp.arange(4, dtype=F32) + 4.0 * g
    slopes = 2.0 ** (-8.0 * (heads + 1.0) / 12.0)
    bias = -slopes[:, None, None] * (delta * dil).astype(F32)[None]
    return jnp.where(valid[None], bias, -1e30).reshape(4 * BLK, 2 * BLK)


def _head_masks(shape):
    lane = lax.broadcasted_iota(jnp.int32, shape, 1)
    return [(lane >= 64 * h) & (lane < 64 * (h + 1)) for h in range(4)]


def _stack_heads(v, masks):
    return jnp.concatenate([jnp.where(masks[h], v, jnp.zeros_like(v)) for h in range(4)], axis=0)


def _unstack_heads(v4, masks):
    out = jnp.where(masks[0], v4[0:BLK], 0.0)
    for h in range(1, 4):
        out = jnp.where(masks[h], v4[BLK * h:BLK * (h + 1)], out)
    return out


def _regroup(load_half, dst_ref, stage_ref, n, dil):
    for hlf in range(2):
        stage_ref[hlf] = load_half(hlf)

    def residue(r, carry):
        for hlf in range(2):
            dst_ref[pl.ds(pl.multiple_of(r * n, BLK), n), pl.ds(128 * hlf, 128)] = (
                stage_ref[hlf, pl.ds(r, n, stride=dil), :].astype(dst_ref.dtype))
        return carry

    lax.fori_loop(0, dil, residue, 0)


def _store_block(nat_ref, r, i, val, dil):
    for hlf in range(2):
        nat_ref[hlf, pl.ds(r + dil * BLK * i, BLK, stride=dil), :] = val[:, 128 * hlf:128 * (hlf + 1)]


def _for_blocks(block, dil, nblk):
    if dil == 1:
        block(0, 0, True)
        block(0, 1, False)

        def pair(k, carry):
            block(0, 2 * k, False)
            block(0, 2 * k + 1, False)
            return carry

        lax.fori_loop(1, nblk // 2, pair, 0)
    else:
        def residues(k, carry):
            block(2 * k, 0, True)
            block(2 * k + 1, 0, True)
            if nblk > 1:
                def loop(i, c):
                    block(2 * k, i, False)
                    block(2 * k + 1, i, False)
                    return c
                lax.fori_loop(1, nblk, loop, 0)
            return carry

        lax.fori_loop(0, dil // 2, residues, 0)


def _attn_forward(qkv, g, nbat):
    t = qkv.shape[1]
    seq = t // nbat
    dil = GROUPS[g][1]
    n = seq // dil
    nblk = n // BLK
    qkv4 = qkv.reshape(3, 3, t, SLAB)

    def body(qkv_ref, bias_ref, ol_ref, *scratch):
        masks = _head_masks((BLK, SLAB))
        if dil > 1:
            stage, qd, kd, vd, nat_o, nat_l = scratch
            for which, dst in enumerate((qd, kd, vd)):
                _regroup(lambda hlf, which=which: qkv_ref[which, 0, :, pl.ds(128 * hlf, 128)].astype(F32), dst, stage, n, dil)
        else:
            qd, kd, vd = qkv_ref.at[0, 0], qkv_ref.at[1, 0], qkv_ref.at[2, 0]

        def block(r, i, first):
            base = r * n
            qs = pl.ds(pl.multiple_of(base + i * BLK, BLK), BLK)
            ks = pl.ds(pl.multiple_of(base, BLK), BLK) if first else pl.ds(pl.multiple_of(base + (i - 1) * BLK, BLK), 2 * BLK)
            q, kk, vv = qd[qs, :], kd[ks, :], vd[ks, :]
            bias = bias_ref[:, pl.ds(BLK, BLK)] if first else bias_ref[...]
            s = _nt(_stack_heads(q, masks), kk) * 0.125 + bias
            m = jnp.max(s, axis=1, keepdims=True)
            p = jnp.exp(s - m)
            den = jnp.sum(p, axis=1, keepdims=True)
            out = _unstack_heads(_nn((p * (1.0 / den)).astype(BF16), vv), masks)
            lse = _unstack_heads(jnp.broadcast_to(m + jnp.log(den), (4 * BLK, SLAB)), masks)
            if dil > 1:
                _store_block(nat_o, r, i, out, dil)
                _store_block(nat_l, r, i, lse, dil)
            else:
                ol_ref[0, qs, :] = out
                ol_ref[1, qs, :] = lse

        _for_blocks(block, dil, nblk)
        if dil > 1:
            for hlf in range(2):
                ol_ref[0, :, pl.ds(128 * hlf, 128)] = nat_o[hlf]
                ol_ref[1, :, pl.ds(128 * hlf, 128)] = nat_l[hlf]

    scratch = []
    if dil > 1:
        scratch = [pltpu.VMEM((2, seq, 128), F32)] + [pltpu.VMEM((seq, SLAB), BF16)] * 3 + [pltpu.VMEM((2, seq, 128), F32)] * 2
    return _pcall(
        body, grid=(nbat,), out_shape=jax.ShapeDtypeStruct((2, t, SLAB), F32),
        in_specs=[pl.BlockSpec((3, 1, seq, SLAB), lambda b: (0, g, b, 0)),
                  pl.BlockSpec((4 * BLK, 2 * BLK), lambda b: (0, 0))],
        out_specs=pl.BlockSpec((2, seq, SLAB), lambda b: (0, b, 0)), scratch_shapes=scratch,
        name=f"attn_forward_{g}", compiler_params=_params(40, ("parallel",)))(qkv4, _bias_tables(g))


def _attn_backward(qkv, do_attn, ol_tot, dproj, g, nbat):
    t = qkv.shape[1]
    seq = t // nbat
    dil = GROUPS[g][1]
    n = seq // dil
    nblk = n // BLK
    qkv4 = qkv.reshape(3, 3, t, SLAB)
    dp4 = dproj.reshape(DP_SLABS // 3, 3, t, SLAB)

    def body(qkv_ref, do_ref, ol_ref, bias_ref, dp_in, dp_ref, gb_ref, dk_acc, dv_acc, *scratch):
        del dp_in
        masks = _head_masks((BLK, SLAB))

        @pl.when(pl.program_id(0) == 0)
        def _():
            gb_ref[...] = jnp.zeros_like(gb_ref)

        dk_acc[...] = jnp.zeros_like(dk_acc)
        dv_acc[...] = jnp.zeros_like(dv_acc)
        if dil > 1:
            stage, qd, kd, vd, dod, prodd, lsed, nat = scratch
            lanes = lambda hlf: pl.ds(128 * hlf, 128)
            for which, dst in enumerate((qd, kd, vd)):
                _regroup(lambda hlf, which=which: qkv_ref[which, 0, :, lanes(hlf)].astype(F32), dst, stage, n, dil)
            _regroup(lambda hlf: do_ref[:, lanes(hlf)].astype(F32), dod, stage, n, dil)
            _regroup(lambda hlf: do_ref[:, lanes(hlf)].astype(F32) * ol_ref[0, :, lanes(hlf)], prodd, stage, n, dil)
            _regroup(lambda hlf: ol_ref[1, :, lanes(hlf)], lsed, stage, n, dil)
        else:
            qd, kd, vd = qkv_ref.at[0, 0], qkv_ref.at[1, 0], qkv_ref.at[2, 0]

        def block(r, i, first):
            base = r * n
            qs = pl.ds(pl.multiple_of(base + i * BLK, BLK), BLK)
            ks = pl.ds(pl.multiple_of(base, BLK), BLK) if first else pl.ds(pl.multiple_of(base + (i - 1) * BLK, BLK), 2 * BLK)
            q, kk, vv = qd[qs, :], kd[ks, :], vd[ks, :]
            if dil > 1:
                do, prod, lse = dod[qs, :], prodd[qs, :], lsed[qs, :]
            else:
                do = do_ref[qs, :]
                prod = do.astype(F32) * ol_ref[0, qs, :]
                lse = ol_ref[1, qs, :]
            q4, do4 = _stack_heads(q, masks), _stack_heads(do, masks)
            bias = bias_ref[:, pl.ds(BLK, BLK)] if first else bias_ref[...]
            lse4 = jnp.concatenate([lse[:, 64 * h:64 * h + 1] for h in range(4)], axis=0)
            delta4 = jnp.concatenate([jnp.sum(jnp.where(masks[h], prod, 0.0), axis=1, keepdims=True) for h in range(4)], axis=0)
            p = jnp.exp(_nt(q4, kk) * 0.125 + bias - lse4)
            ds = (p * (_nt(do4, vv) - delta4)).astype(BF16)
            dv_acc[ks, :] += _tn(p.astype(BF16), do4)
            dk_acc[ks, :] += _tn(ds, q4) * 0.125
            dq = _unstack_heads(_nn(ds, kk), masks) * 0.125
            if dil > 1:
                _store_block(nat, r, i, dq, dil)
            else:
                dp_ref[0, 0, qs, :] = dq.astype(BF16)
            gb_ref[0] += _part8(dq)

        _for_blocks(block, dil, nblk)
        gb_ref[1] += _part8(dk_acc[...])
        gb_ref[2] += _part8(dv_acc[...])
        if dil > 1:
            def flush(which):
                for hlf in range(2):
                    dp_ref[which, 0, :, pl.ds(128 * hlf, 128)] = nat[hlf].astype(BF16)

            def to_token_order(acc_ref):
                def residue(r, carry):
                    for hlf in range(2):
                        nat[hlf, pl.ds(r, n, stride=dil), :] = acc_ref[pl.ds(pl.multiple_of(r * n, BLK), n), pl.ds(128 * hlf, 128)]
                    return carry
                lax.fori_loop(0, dil, residue, 0)

            flush(0)
            to_token_order(dk_acc)
            flush(1)
            to_token_order(dv_acc)
            flush(2)
        else:
            dp_ref[1, 0] = dk_acc[...].astype(BF16)
            dp_ref[2, 0] = dv_acc[...].astype(BF16)

    scratch = [pltpu.VMEM((seq, SLAB), F32)] * 2
    if dil > 1:
        scratch += ([pltpu.VMEM((2, seq, 128), F32)] + [pltpu.VMEM((seq, SLAB), BF16)] * 4 + [pltpu.VMEM((seq, SLAB), F32)] * 2
                    + [pltpu.VMEM((2, seq, 128), F32)])
    dp, gb = _pcall(
        body, grid=(nbat,),
        out_shape=(jax.ShapeDtypeStruct(dp4.shape, BF16), jax.ShapeDtypeStruct((3, 8, SLAB), F32)),
        in_specs=[pl.BlockSpec((3, 1, seq, SLAB), lambda b: (0, g, b, 0)),
                  pl.BlockSpec((seq, SLAB), lambda b: (b, 0)),
                  pl.BlockSpec((2, seq, SLAB), lambda b: (0, b, 0)),
                  pl.BlockSpec((4 * BLK, 2 * BLK), lambda b: (0, 0)), ANY],
        out_specs=(pl.BlockSpec((3, 1, seq, SLAB), lambda b: (DP_SLABS // 9 - 1, g, b, 0)),
                   pl.BlockSpec((3, 8, SLAB), lambda b: (0, 0, 0))),
        scratch_shapes=scratch, input_output_aliases={4: 0}, name=f"attn_backward_{g}",
        compiler_params=_params(48, ("arbitrary",)))(qkv4, do_attn, ol_tot, _bias_tables(g), dp4)
    return dp.reshape(DP_SLABS, t, SLAB), gb


def _mid(rest, ols, x, tgt, ada, cw, b_out, ln_g, ln_b, w_pa_t, w_pb, w_out, tm=256):
    t = x.shape[0]
    nbat = ada.shape[0]
    nt = t // tm
    tps = nt // nbat

    def body(rest_ref, halo_ref, ol0_ref, ol1_ref, ol2_ref, x_ref, t_ref, ada_ref, cw_ref, bout_ref, lng_ref, lnb_ref,
             wpat_ref, wpb_ref, wout_ref,
             dp_ref, gx0_ref, doa_ref, olt_ref, mg_ref, dof_ref, bbs_ref, dyc_ref, a_ref, dya_ref,
             gbr_ref, sv_ref, dgate_ref, carry_ref, keep_ref):
        i = pl.program_id(0)
        ti = nt - 1 - i
        pos = ti % tps

        @pl.when(i == 0)
        def _():
            gbr_ref[...] = jnp.zeros_like(gbr_ref)
            sv_ref[...] = jnp.zeros_like(sv_ref)

        @pl.when(pos == tps - 1)
        def _():
            dgate_ref[...] = jnp.zeros_like(dgate_ref)
            carry_ref[...] = jnp.zeros_like(carry_ref)

        row = lax.broadcasted_iota(jnp.int32, (tm, SLAB), 0)
        halo_on = (pos > 0).astype(F32)

        def cols(s):
            return pl.ds(SLAB * s, SLAB)

        l0, l1, l2 = ol0_ref[1], ol1_ref[1], ol2_ref[1]
        mx = jnp.maximum(jnp.maximum(l0, l1), l2)
        e0, e1, e2 = jnp.exp(l0 - mx), jnp.exp(l1 - mx), jnp.exp(l2 - mx)
        den = e0 + e1 + e2
        o_attn = (e0 * ol0_ref[0] + e1 * ol1_ref[0] + e2 * ol2_ref[0]) * (1.0 / den)
        olt_ref[0] = o_attn
        olt_ref[1] = mx + jnp.log(den)
        z_a = rest_ref[R_ZA]
        sg_za = _sigmoid(z_a)
        a_ref[...] = (o_attn * z_a * sg_za).astype(BF16)
        y_attn = _nt(a_ref[...], wpat_ref[...])

        for s in range(4):
            u = rest_ref[R_GC + s] * rest_ref[R_UX + s]
            hu = halo_ref[R_GC + s] * halo_ref[R_UX + s] * halo_on
            u1 = jnp.where(row == 0, hu[7:8], pltpu.roll(u, 1, 0))
            u2 = jnp.where(row == 0, hu[6:7], jnp.where(row == 1, hu[7:8], pltpu.roll(u, 2, 0)))
            conv = cw_ref[0:1, cols(s)] * u2 + cw_ref[1:2, cols(s)] * u1 + cw_ref[2:3, cols(s)] * u
            zc = rest_ref[R_ZC + s]
            sg = _sigmoid(zc)
            keep_ref[2, :, cols(s)], keep_ref[3, :, cols(s)], keep_ref[4, :, cols(s)], keep_ref[5, :, cols(s)] = u1, u2, conv, sg
            bbs_ref[:, cols(s)] = (rest_ref[R_GB + s] * conv * (zc * sg)).astype(BF16)
        y_conv = _nn(bbs_ref[...], wpb_ref[...])

        for s in range(4):
            s_a, s_b = _sigmoid(rest_ref[R_GA + s]), _sigmoid(rest_ref[R_GBM + s])
            keep_ref[0, :, cols(s)], keep_ref[1, :, cols(s)] = s_a, s_b
            mg_ref[:, cols(s)] = (s_a * y_attn[:, SLAB * s:SLAB * (s + 1)] + s_b * y_conv[:, SLAB * s:SLAB * (s + 1)]).astype(BF16)
        o = _nn(mg_ref[...], wout_ref[...]) + bout_ref[...]
        gate = ada_ref[0, 2:3, :]
        r = ALPHA * x_ref[...] + gate * o
        mu = jnp.mean(r, axis=1, keepdims=True)
        rc = r - mu
        rstd = lax.rsqrt(jnp.mean(rc * rc, axis=1, keepdims=True) + LN_EPS)
        xhat = rc * rstd
        err = xhat * lng_ref[...] + lnb_ref[...] - t_ref[...]
        sv_ref[6] += _part8(err * err)
        dy = err * (1.0 / D)
        sv_ref[0] += _part8(dy * xhat)
        sv_ref[1] += _part8(dy)
        dxh = dy * lng_ref[...]
        dr = rstd * (dxh - jnp.mean(dxh, axis=1, keepdims=True) - xhat * jnp.mean(dxh * xhat, axis=1, keepdims=True))
        gx0_ref[...] = ALPHA * dr
        dgate_ref[0] += _part8(dr * o)
        do_ = dr * gate
        sv_ref[2] += _part8(do_)
        dof_ref[...] = do_.astype(BF16)
        dmerged = _nt(dof_ref[...], wout_ref[...])
        for s in range(4):
            s_a, s_b = keep_ref[0, :, cols(s)], keep_ref[1, :, cols(s)]
            dm = dmerged[:, SLAB * s:SLAB * (s + 1)]
            ya, yc = y_attn[:, SLAB * s:SLAB * (s + 1)], y_conv[:, SLAB * s:SLAB * (s + 1)]
            dya_ref[:, cols(s)] = (dm * s_a).astype(BF16)
            dyc_ref[:, cols(s)] = (dm * s_b).astype(BF16)
            dga = dm * ya * s_a * (1.0 - s_a)
            dgb = dm * yc * s_b * (1.0 - s_b)
            dp_ref[R_GA + s] = dga.astype(BF16)
            dp_ref[R_GBM + s] = dgb.astype(BF16)
            gbr_ref[R_GA + s] += _part8(dga)
            gbr_ref[R_GBM + s] += _part8(dgb)

        da = _nn(dya_ref[...], wpat_ref[...])
        doa_ref[...] = (da * z_a * sg_za).astype(BF16)
        dza = da * o_attn * (sg_za * (1.0 + z_a * (1.0 - sg_za)))
        dp_ref[R_ZA] = dza.astype(BF16)
        gbr_ref[R_ZA] += _part8(dza)

        dbb = _nt(dyc_ref[...], wpb_ref[...])
        for s in range(4):
            ux, gc, zc = rest_ref[R_UX + s], rest_ref[R_GC + s], rest_ref[R_ZC + s]
            u = gc * ux
            u1, u2, conv, sg = keep_ref[2, :, cols(s)], keep_ref[3, :, cols(s)], keep_ref[4, :, cols(s)], keep_ref[5, :, cols(s)]
            gb = rest_ref[R_GB + s]
            d_b = dbb[:, SLAB * s:SLAB * (s + 1)]
            szc = zc * sg
            dgb_ = d_b * conv * szc
            dconv = d_b * gb * szc
            dzc = d_b * gb * conv * (sg * (1.0 + zc * (1.0 - sg)))
            sv_ref[3, :, cols(s)] += _part8(dconv * u2)
            sv_ref[4, :, cols(s)] += _part8(dconv * u1)
            sv_ref[5, :, cols(s)] += _part8(dconv * u)
            nxt = carry_ref[:, cols(s)]
            d1 = jnp.where(row == tm - 1, nxt[0:1], pltpu.roll(dconv, tm - 1, 0))
            d2 = jnp.where(row == tm - 1, nxt[1:2], jnp.where(row == tm - 2, nxt[0:1], pltpu.roll(dconv, tm - 2, 0)))
            carry_ref[:, cols(s)] = dconv[0:8]
            du = cw_ref[2:3, cols(s)] * dconv + cw_ref[1:2, cols(s)] * d1 + cw_ref[0:1, cols(s)] * d2
            dgc, dux = du * ux, du * gc
            for slab, val in ((R_GB + s, dgb_), (R_ZC + s, dzc), (R_GC + s, dgc), (R_UX + s, dux)):
                dp_ref[slab] = val.astype(BF16)
                gbr_ref[slab] += _part8(val)

    def tile(i):
        return nt - 1 - i

    row_blk = lambda i: (tile(i), 0)
    slab_blk = lambda i: (0, tile(i), 0)
    const2 = lambda i: (0, 0)
    const3 = lambda i: (0, 0, 0)
    in_specs = [
        pl.BlockSpec((N_REST, tm, SLAB), slab_blk),
        pl.BlockSpec((N_REST, 8, SLAB), lambda i: (0, jnp.maximum(tile(i) * (tm // 8) - 1, 0), 0)),
        pl.BlockSpec((2, tm, SLAB), slab_blk), pl.BlockSpec((2, tm, SLAB), slab_blk), pl.BlockSpec((2, tm, SLAB), slab_blk),
        pl.BlockSpec((tm, D), row_blk), pl.BlockSpec((tm, D), row_blk),
        pl.BlockSpec((1, 3, D), lambda i: (tile(i) // tps, 0, 0)),
        pl.BlockSpec((3, D), const2), pl.BlockSpec((1, D), const2), pl.BlockSpec((1, D), const2), pl.BlockSpec((1, D), const2),
        pl.BlockSpec((D, SLAB), const2), pl.BlockSpec((D, D), const2), pl.BlockSpec((D, D), const2)]
    bf_rows = lambda: jax.ShapeDtypeStruct((t, D), BF16)
    out_shape = (
        jax.ShapeDtypeStruct((DP_SLABS, t, SLAB), BF16), jax.ShapeDtypeStruct((t, D), F32),
        jax.ShapeDtypeStruct((t, SLAB), BF16), jax.ShapeDtypeStruct((2, t, SLAB), F32),
        bf_rows(), bf_rows(), bf_rows(), bf_rows(), jax.ShapeDtypeStruct((t, SLAB), BF16), bf_rows(),
        jax.ShapeDtypeStruct((N_REST, 8, SLAB), F32), jax.ShapeDtypeStruct((7, 8, D), F32),
        jax.ShapeDtypeStruct((nbat, 8, D), F32))
    out_specs = (
        pl.BlockSpec((N_REST, tm, SLAB), slab_blk), pl.BlockSpec((tm, D), row_blk),
        pl.BlockSpec((tm, SLAB), row_blk), pl.BlockSpec((2, tm, SLAB), slab_blk),
        pl.BlockSpec((tm, D), row_blk), pl.BlockSpec((tm, D), row_blk), pl.BlockSpec((tm, D), row_blk),
        pl.BlockSpec((tm, D), row_blk), pl.BlockSpec((tm, SLAB), row_blk), pl.BlockSpec((tm, D), row_blk),
        pl.BlockSpec((N_REST, 8, SLAB), const3), pl.BlockSpec((7, 8, D), const3),
        pl.BlockSpec((1, 8, D), lambda i: (tile(i) // tps, 0, 0)))
    return _pcall(body, grid=(nt,), out_shape=out_shape, in_specs=in_specs, out_specs=out_specs,
                  scratch_shapes=[pltpu.VMEM((8, D), F32), pltpu.VMEM((6, tm, D), F32)], name="mid",
                  compiler_params=_params(56, ("arbitrary",)))(
        rest, rest, *ols, x, tgt, ada, cw, b_out, ln_g, ln_b, w_pa_t, w_pb, w_out)


def _tn_matmul(lhs, rhs, lhs_spec, n_steps, out_rows, out_index, name, after):
    t, n = rhs.shape

    def body(l_ref, r_ref, after_ref, o_ref):
        del after_ref
        o_ref[...] = _tn(l_ref[0] if len(l_ref.shape) == 3 else l_ref[...], r_ref[...])

    return _pcall(body, grid=(n_steps,), out_shape=jax.ShapeDtypeStruct((out_rows, n), F32),
                  in_specs=[lhs_spec, pl.BlockSpec((t, n), lambda j: (0, 0)), ANY],
                  out_specs=pl.BlockSpec((SLAB, n), out_index), name=name,
                  compiler_params=_params(48, ("parallel",)))(lhs, rhs, after)


def _grad_rows_2d(lhs, rhs, name, after):
    t, k = lhs.shape
    return _tn_matmul(lhs, rhs, pl.BlockSpec((t, SLAB), lambda j: (0, j)), k // SLAB, k, lambda j: (j, 0), name, after)


def _w_row_block(j):
    return (j + N_QKV) % N_SLAB


def _dp_slab(j):
    return jnp.where(j < N_REST, j, j + 2)


def _grad_w_in_t(dproj, h):
    t = h.shape[0]
    return _tn_matmul(dproj, h, pl.BlockSpec((1, t, SLAB), lambda j: (_dp_slab(j), 0, 0)), N_SLAB, D_IN,
                      lambda j: (_w_row_block(j), 0), "grad_w_in", h)


def _grad_h(dproj, w_in_t, gx0, x, ada, after, tm=512):
    t = x.shape[0]
    nbat = ada.shape[0]
    tps = (t // nbat) // tm

    def body(dp_ref, w_ref, gx0_ref, x_ref, ada_ref, after_ref, gx_ref, dss_ref):
        del after_ref
        i = pl.program_id(0)
        dh = None
        for j in range(N_SLAB):
            slab = j if j < N_REST else j + 2
            part = _nn(dp_ref[slab], w_ref[pl.ds(SLAB * ((j + N_QKV) % N_SLAB), SLAB), :])
            dh = part if dh is None else dh + part
        gx_ref[...] = gx0_ref[...] + dh * (1.0 + ada_ref[0, 1:2, :])

        @pl.when((i % tps) == 0)
        def _():
            dss_ref[...] = jnp.zeros_like(dss_ref)

        dss_ref[0, 0] += _part8(dh)
        dss_ref[0, 1] += _part8(dh * x_ref[...])

    return _pcall(
        body, grid=(t // tm,),
        out_shape=(jax.ShapeDtypeStruct((t, D), F32), jax.ShapeDtypeStruct((nbat, 2, 8, D), F32)),
        in_specs=[pl.BlockSpec((DP_SLABS, tm, SLAB), lambda i: (0, i, 0)),
                  pl.BlockSpec((D_IN, D), lambda i: (0, 0), pipeline_mode=pl.Buffered(1)),
                  pl.BlockSpec((tm, D), lambda i: (i, 0)), pl.BlockSpec((tm, D), lambda i: (i, 0)),
                  pl.BlockSpec((1, 3, D), lambda i: (i // tps, 0, 0)), ANY],
        out_specs=(pl.BlockSpec((tm, D), lambda i: (i, 0)),
                   pl.BlockSpec((1, 2, 8, D), lambda i: (i // tps, 0, 0, 0))),
        name="grad_h", compiler_params=_params(60, ("arbitrary",)))(dproj, w_in_t, gx0, x, ada, after)


def _chip(m):
    x, y, _ = _my_position()
    return (x ^ ((m >> 1) & 1), y ^ (m & 1))


def _exchange_siblings(grads):
    n = len(grads)

    def body(*refs):
        copies = _sibling_copies(refs[:n], refs[n:2 * n], refs[2 * n], refs[2 * n + 1])
        for cp in copies:
            cp.start()
        for cp in copies:
            cp.wait()

    return _pcall(body, out_shape=tuple(_sibling_zones(grads)), in_specs=[ANY] * n, out_specs=(ANY,) * n,
                  name="exchange_siblings", scratch_shapes=[pltpu.SemaphoreType.DMA((4 * n,))] * 2)(*grads)


def _sibling_zones(grads):
    return [jax.ShapeDtypeStruct((4, g.shape[0] // N_DEV, g.shape[1]), g.dtype) for g in grads]


def _sibling_copies(srcs, lands, send_sems, recv_sems):
    x, y, c = _my_position()
    copies = []
    for a, (src, land) in enumerate(zip(srcs, lands)):
        rows = land.shape[1]
        for m in range(4):
            dev = _flat(*_chip(m), 1 - c)
            copies.append(pltpu.make_async_remote_copy(
                src_ref=src.at[pl.ds(pl.multiple_of(dev * rows, 8), rows), :], dst_ref=land.at[m],
                send_sem=send_sems.at[4 * a + m], recv_sem=recv_sems.at[4 * a + m], device_id=(x, y, 1 - c),
                device_id_type=MESH))
    return copies


def _chip_copies(srcs, lands, send_sems, recv_sems):
    _, _, c = _my_position()
    return [pltpu.make_async_remote_copy(
        src_ref=srcs[a].at[m - 1], dst_ref=lands[a].at[m - 1], send_sem=send_sems.at[3 * a + m - 1],
        recv_sem=recv_sems.at[3 * a + m - 1], device_id=(*_chip(m), c), device_id_type=MESH)
        for a in range(len(srcs)) for m in range(1, 4)]


HBM = pl.BlockSpec(memory_space=pltpu.HBM)
SEM = pl.BlockSpec(memory_space=pltpu.SEMAPHORE)
SPLIT_COPY = pltpu.CompilerParams(has_side_effects=pltpu.SideEffectType.DATAFLOW_SIDE_EFFECTING)


def _start_copies(make_copies, n_sems, srcs, zones, name):
    n = len(srcs)

    def body(*refs):
        for cp in make_copies(refs[:n], refs[n:2 * n], refs[2 * n], refs[2 * n + 1]):
            cp.start()
        refs[-1][...] = jnp.zeros_like(refs[-1])

    hbm = tuple(pltpu.HBM(b.shape, b.dtype) for b in list(srcs) + list(zones))
    out_shape = (pltpu.SemaphoreType.DMA((n_sems,)), pltpu.SemaphoreType.DMA((n_sems,))) + hbm + (jax.ShapeDtypeStruct((8, 128), F32),)
    operands = [pltpu.with_memory_space_constraint(b, pltpu.HBM) for b in srcs]
    operands += [pltpu.with_memory_space_constraint(lax.empty(z.shape, z.dtype), pltpu.HBM) for z in zones]
    res = _pcall(body, out_shape=out_shape, in_specs=[HBM] * (2 * n), out_specs=(SEM, SEM) + (HBM,) * (2 * n) + (VMEM,),
                 input_output_aliases={i: 2 + i for i in range(2 * n)}, name=name, compiler_params=SPLIT_COPY)(*operands)
    return (res[0], res[1], res[2:2 + n], res[2 + n:2 + 2 * n]), res[-1]


def _wait_copies(make_copies, flight, after, name):
    send_sems, recv_sems, srcs, zones = flight
    n = len(srcs)

    def body(*refs):
        for cp in make_copies(refs[:n], refs[n:2 * n], refs[2 * n], refs[2 * n + 1]):
            cp.wait_send()
            cp.wait_recv()

    hbm = tuple(pltpu.HBM(b.shape, b.dtype) for b in list(srcs) + list(zones))
    res = _pcall(body, out_shape=hbm, in_specs=[HBM] * (2 * n) + [SEM, SEM, ANY], out_specs=(HBM,) * (2 * n),
                 input_output_aliases={i: i for i in range(2 * n)}, name=name, compiler_params=SPLIT_COPY)(
        *srcs, *zones, send_sems, recv_sems, after)
    return res[:n], res[n:]


def _pair_sums(devs, grads, lands, n_steps, name):
    n = len(grads)
    rows = [l.shape[1] for l in lands]
    rbs = [r // n_steps for r in rows]

    def body(devs_ref, *refs):
        del devs_ref
        g_refs, land_refs, outs = refs[:4 * n], refs[4 * n:5 * n], refs[5 * n:]
        for a in range(n):
            outs[2 * a][...] = g_refs[4 * a][...] + land_refs[a][0]
            for m in range(1, 4):
                outs[2 * a + 1][m - 1] = (g_refs[4 * a + m][...] + land_refs[a][m]).astype(BF16)

    def block_of(m, per_dev):
        return lambda i, devs_ref: (devs_ref[m] * per_dev + i, 0)

    in_specs = [pl.BlockSpec((rb, l.shape[2]), block_of(m, n_steps)) for rb, l in zip(rbs, lands) for m in range(4)]
    in_specs += [pl.BlockSpec((4, rb, l.shape[2]), lambda i, devs_ref: (0, i, 0)) for rb, l in zip(rbs, lands)]
    out_shape, out_specs = [], []
    for rb, l in zip(rbs, lands):
        out_shape += [jax.ShapeDtypeStruct(l.shape[1:], F32), jax.ShapeDtypeStruct((3,) + l.shape[1:], BF16)]
        out_specs += [pl.BlockSpec((rb, l.shape[2]), lambda i, devs_ref: (i, 0)),
                      pl.BlockSpec((3, rb, l.shape[2]), lambda i, devs_ref: (0, i, 0))]
    grid_spec = pltpu.PrefetchScalarGridSpec(num_scalar_prefetch=1, grid=(n_steps,), in_specs=in_specs, out_specs=tuple(out_specs))
    res = _pcall(body, grid_spec=grid_spec, out_shape=tuple(out_shape), name=name,
                 compiler_params=_params(48, ("parallel",)))(devs, *[g for g in grads for _ in range(4)], *lands)
    return res[0::2], res[1::2]


def _final_sums(mine, lands, n_steps, name):
    n = len(mine)
    rbs = [o.shape[0] // n_steps for o in mine]

    def body(*refs):
        mine_refs, land_refs, outs = refs[:n], refs[n:2 * n], refs[2 * n:]
        for a in range(n):
            tot = mine_refs[a][...]
            for m in range(3):
                tot = tot + land_refs[a][m].astype(F32)
            outs[a][...] = tot

    in_specs = ([pl.BlockSpec((rb, o.shape[1]), lambda i: (i, 0)) for rb, o in zip(rbs, mine)]
                + [pl.BlockSpec((3, rb, o.shape[1]), lambda i: (0, i, 0)) for rb, o in zip(rbs, mine)])
    out_specs = tuple(pl.BlockSpec((rb, o.shape[1]), lambda i: (i, 0)) for rb, o in zip(rbs, mine))
    out_shape = tuple(jax.ShapeDtypeStruct(o.shape, F32) for o in mine)
    return _pcall(body, grid=(n_steps,), out_shape=out_shape, in_specs=in_specs, out_specs=out_specs, name=name,
                  compiler_params=_params(32, ("parallel",)))(*mine, *lands)


def _reduce_scatter_begin(big, small_after_start):
    flight, token = _start_copies(_sibling_copies, 4, [big], _sibling_zones([big]), "siblings_start")
    small = small_after_start(token)
    (big,), big_lands = _wait_copies(_sibling_copies, flight, small[-1], "siblings_wait")
    small_lands = _exchange_siblings(small)
    c = lax.axis_index("c")
    devs = jnp.stack([_flat(*_chip(m), c) for m in range(4)]).astype(jnp.int32)
    big_mine, big_send = _pair_sums(devs, [big], big_lands, 4, "pair_sums_w_in")
    small_mine, small_send = _pair_sums(devs, small, small_lands, 1, "pair_sums_rest")
    bufs = list(big_send) + list(small_send)
    flight, token = _start_copies(_chip_copies, 3 * len(bufs), bufs, bufs, "chips_start")
    return (flight, list(big_mine) + list(small_mine)), token


def _reduce_scatter_end(state, after):
    flight, mine = state
    _, got = _wait_copies(_chip_copies, flight, after, "chips_wait")
    big = _final_sums(mine[:1], got[:1], 4, "final_sums_w_in")
    small = _final_sums(mine[1:], got[1:], 1, "final_sums_rest")
    return list(big) + list(small)


def _adamw(w, g, m, v):
    m_new = B1 * m + (1.0 - B1) * g
    v_new = B2 * v + (1.0 - B2) * (g * g)
    m_hat = m_new / (1.0 - B1 ** STEP)
    v_hat = v_new / (1.0 - B2 ** STEP)
    delta = -LR * (m_hat / (jnp.sqrt(v_hat) + EPS) + WD * w)
    return delta, m_new, v_new


def _adam_rows(g, w, m, v, n_steps, name):
    rows, ncol = w.shape
    blk = pl.BlockSpec((rows // n_steps, ncol), lambda i: (i, 0))

    def body(g_ref, w_ref, m_ref, v_ref, d_ref, mo_ref, vo_ref):
        d_ref[...], mo_ref[...], vo_ref[...] = _adamw(w_ref[...], g_ref[...], m_ref[...], v_ref[...])

    shape = jax.ShapeDtypeStruct(w.shape, F32)
    return _pcall(body, grid=(n_steps,), out_shape=(shape,) * 3, in_specs=[blk] * 4, out_specs=(blk,) * 3, name=name,
                  compiler_params=_params(32, ("parallel",)))(g, w, m, v)


def _adam_transposed(g_t, w, m, v, name):
    n, k = g_t.shape
    rb = min(k, 128)

    def body(gt_ref, w_ref, m_ref, v_ref, g_ref, d_ref, mo_ref, vo_ref):
        for src, skip, dst, size in _column_chunks(n):
            sl = pl.ds(dst, size)
            g = gt_ref[pl.ds(src, 128), :].T[:, skip:]
            delta, m_new, v_new = _adamw(w_ref[:, sl], g, m_ref[:, sl], v_ref[:, sl])
            g_ref[:, sl], d_ref[:, sl], mo_ref[:, sl], vo_ref[:, sl] = g, delta, m_new, v_new

    shape = jax.ShapeDtypeStruct(w.shape, F32)
    rows = pl.BlockSpec((rb, n), lambda i: (i, 0))
    return _pcall(body, grid=(k // rb,), out_shape=(shape,) * 4,
                  in_specs=[pl.BlockSpec((n, rb), lambda i: (0, i)), rows, rows, rows], out_specs=(rows,) * 4, name=name,
                  compiler_params=_params(32, ("parallel",)))(g_t, w, m, v)


def _adam_many(items, name):
    n = len(items)

    def body(*refs):
        ins, outs = refs[:4 * n], refs[4 * n:]
        for a in range(n):
            w_ref, g_ref, m_ref, v_ref = ins[4 * a:4 * a + 4]
            delta, m_new, v_new = _adamw(w_ref[...], g_ref[...], m_ref[...], v_ref[...])
            outs[3 * a][...], outs[3 * a + 1][...], outs[3 * a + 2][...] = delta, m_new, v_new

    out_shape = tuple(jax.ShapeDtypeStruct(it[0].shape, F32) for it in items for _ in range(3))
    flat = [arr for it in items for arr in it]
    res = _pcall(body, grid=(1,), out_shape=out_shape, in_specs=[_whole(a) for a in flat],
                 out_specs=tuple(_whole(o) for o in out_shape), name=name, compiler_params=_params(32))(*flat)
    return [tuple(res[3 * a:3 * a + 3]) for a in range(n)]


def _adam_w_ada(cact_all, dada_mine, w, m, v):
    def body(c_ref, d_ref, w_ref, m_ref, v_ref, g_ref, dl_ref, mo_ref, vo_ref):
        g = _tn(c_ref[...].astype(BF16), d_ref[...].astype(BF16))
        delta, m_new, v_new = _adamw(w_ref[...], g, m_ref[...], v_ref[...])
        g_ref[...], dl_ref[...], mo_ref[...], vo_ref[...] = g, delta, m_new, v_new

    shape = jax.ShapeDtypeStruct(w.shape, F32)
    operands = (cact_all, dada_mine, w, m, v)
    return _pcall(body, grid=(1,), out_shape=(shape,) * 4, in_specs=[_whole(a) for a in operands],
                  out_specs=(_whole(w),) * 4, name="adam_w_ada", compiler_params=_params(32))(*operands)


def kernel(x, c, w_ada, b_ada, w_in, b_in, conv_w, w_proj_attn, w_proj_conv, w_out, b_out, ln_g, ln_b, loss_target, m_w_ada, m_b_ada, m_w_in, m_b_in, m_conv_w, m_w_proj_attn, m_w_proj_conv, m_w_out, m_b_out, m_ln_g, m_ln_b, v_w_ada, v_b_ada, v_w_in, v_b_in, v_conv_w, v_w_proj_attn, v_w_proj_conv, v_w_out, v_b_out, v_ln_g, v_ln_b):
    nbat, seq, _ = x.shape
    t = nbat * seq
    me = _flat(*_my_position())
    x2, tgt2 = x.reshape(t, D), loss_target.reshape(t, D)
    sq = lambda a: a.reshape(a.shape[1:])

    tr = lambda a: a[0].T
    w_in_rows = tr(w_in)
    w_in_t_s = _cast_rows(w_in_rows, 4, "cast_w_in")
    w_pa_t_s, w_pb_s, w_out_s, cact_s, cw_s = _prep(sq(w_proj_attn), sq(w_proj_conv), sq(w_out), c, sq(conv_w))
    cact_g, cw_g = _gather_rows([cact_s, cw_s])
    cact_all = cact_g.reshape(N_DEV, 8, D)[:, :nbat].reshape(N_DEV * nbat, D)
    cw = cw_g.reshape(N_DEV, 8, -1)[:, :3].transpose(1, 0, 2).reshape(3, D)

    ncol = w_ada.shape[2]
    b_ada_mine = lax.dynamic_slice(b_ada, (0, me * ncol), (1, ncol))
    ada_slots = _ada_forward(cact_all, sq(w_ada), b_ada_mine)
    ada_all = ada_slots.transpose(1, 0, 2).reshape(N_DEV * nbat, 3, D)
    ada = lax.dynamic_slice(ada_all, (me * nbat, 0, 0), (nbat, 3, D))

    h = _make_h(x2, ada)
    w_in_t, qkv, rest, (w_pa_t, w_pb, w_o) = _project_gather(w_in_t_s, h, b_in.reshape(N_SLAB, 1, SLAB), [w_pa_t_s, w_pb_s, w_out_s])
    ols = [_attn_forward(qkv, g, nbat) for g in range(3)]
    (dproj, gx0, do_attn, ol_tot, merged, do_f, bbs, dyc, a_bf, dya, gb_rest, svec, dgate) = _mid(
        rest, ols, x2, tgt2, ada, cw, b_out, ln_g, ln_b, w_pa_t, w_pb, w_o)

    gb_qkv = []
    for g in range(3):
        dproj, gb = _attn_backward(qkv, do_attn, ol_tot, dproj, g, nbat)
        gb_qkv.append(gb)
    g_w_in_t = _grad_w_in_t(dproj, h)

    def small_grads(token):
        g_w_out = _grad_rows_2d(merged, do_f, "grad_w_out", token)
        g_w_pb = _grad_rows_2d(bbs, dyc, "grad_w_proj_conv", g_w_out)
        g_w_pa_t = _grad_rows_2d(dya, a_bf, "grad_w_proj_attn", g_w_pb)
        return [g_w_out, g_w_pb, g_w_pa_t]

    rs_state, token = _reduce_scatter_begin(g_w_in_t, small_grads)
    grad_x, dss = _grad_h(dproj, w_in_t, gx0, x2, ada, token)

    rows8, tot, g_bada = _small_reduce(gb_rest, gb_qkv, svec, dgate, dss)
    g_in_t, g_out, g_pb, g_pa_t = _reduce_scatter_end(rs_state, tot)
    loss = tot[0, P_LOSS]
    dada_all = rows8[:, 0, P_DADA:].reshape(N_DEV * nbat, 3 * D)
    dada_mine = lax.dynamic_slice(dada_all, (0, me * ncol), (N_DEV * nbat, ncol))

    d_win_t, nm_win_t, nv_win_t = _adam_rows(g_in_t, w_in_rows, tr(m_w_in), tr(v_w_in), 8, "adam_w_in")
    g_win, d_win, nm_win, nv_win = g_in_t.T, d_win_t.T, nm_win_t.T, nv_win_t.T
    g_wpa, d_wpa, nm_wpa, nv_wpa = _adam_transposed(g_pa_t, sq(w_proj_attn), sq(m_w_proj_attn), sq(v_w_proj_attn), "adam_w_proj_attn")
    g_wada, d_wada, nm_wada, nv_wada = _adam_w_ada(cact_all, dada_mine, sq(w_ada), sq(m_w_ada), sq(v_w_ada))
    g_bin = tot[:, P_BIN:P_BIN + D_IN]
    g_bout = tot[:, P_BOUT:P_BOUT + D]
    g_lng = tot[:, P_LNG:P_LNG + D]
    g_lnb = tot[:, P_LNB:P_LNB + D]
    g_conv = lax.dynamic_slice(tot[:, P_CONV:P_CONV + 3 * D].reshape(3, D), (0, me * cw_s.shape[1]), (3, cw_s.shape[1]))
    upd = _adam_many([
        (sq(w_proj_conv), g_pb, sq(m_w_proj_conv), sq(v_w_proj_conv)),
        (sq(w_out), g_out, sq(m_w_out), sq(v_w_out)),
        (b_ada, g_bada, m_b_ada, v_b_ada), (b_in, g_bin, m_b_in, v_b_in), (sq(conv_w), g_conv, sq(m_conv_w), sq(v_conv_w)),
        (b_out, g_bout, m_b_out, v_b_out), (ln_g, g_lng, m_ln_g, v_ln_g), (ln_b, g_lnb, m_ln_b, v_ln_b)], "adam_rest")
    (d_wpb, nm_wpb, nv_wpb), (d_wout, nm_wout, nv_wout), (d_bada, nm_bada, nv_bada), (d_bin, nm_bin, nv_bin), \
        (d_conv, nm_conv, nv_conv), (d_bout, nm_bout, nv_bout), (d_lng, nm_lng, nv_lng), (d_lnb, nm_lnb, nv_lnb) = upd

    ex = lambda a: a.reshape((1,) + a.shape)
    grads = [ex(g_wada), g_bada, ex(g_win), g_bin, ex(g_conv), ex(g_wpa), ex(g_pb), ex(g_out), g_bout, g_lng, g_lnb]
    deltas = [ex(d_wada), d_bada, ex(d_win), d_bin, ex(d_conv), ex(d_wpa), ex(d_wpb), ex(d_wout), d_bout, d_lng, d_lnb]
    new_m = [ex(nm_wada), nm_bada, ex(nm_win), nm_bin, ex(nm_conv), ex(nm_wpa), ex(nm_wpb), ex(nm_wout), nm_bout, nm_lng, nm_lnb]
    new_v = [ex(nv_wada), nv_bada, ex(nv_win), nv_bin, ex(nv_conv), ex(nv_wpa), ex(nv_wpb), ex(nv_wout), nv_bout, nv_lng, nv_lnb]
    return (loss, grad_x.reshape(x.shape), *grads, *deltas, *new_m, *new_v)
```

```python
import functools

import jax
import jax.numpy as jnp
from jax import lax
from jax.experimental import pallas as pl
from jax.experimental.pallas import tpu as pltpu

F32, BF16 = jnp.float32, jnp.bfloat16
MESH = pl.DeviceIdType.MESH
N_DEV = 8
D = 1024
SLAB = 256
N_QKV, N_REST = 9, 25
N_SLAB = N_QKV + N_REST
D_IN = N_SLAB * SLAB
DP_SLABS = 36
BLK = 128
GROUPS = ((128, 1), (512, 4), (2048, 16))
ALPHA = 2.0 ** 0.25
LN_EPS = 1e-5
LR, B1, B2, EPS, WD, STEP = 0.001, 0.9, 0.999, 1e-08, 0.01, 10
R_ZA, R_UX, R_GB, R_GC, R_ZC, R_GA, R_GBM = 0, 1, 5, 9, 13, 17, 21
P_BIN, P_BOUT, P_LNG, P_LNB, P_CONV, P_LOSS, P_DADA = 0, 8704, 9728, 10752, 11776, 14848, 14976
MIB = 1024 * 1024


def _pcall(body, *, out_shape, out_specs=None, **kw):
    def pin_out(shape, spec):
        blocked = isinstance(shape, jax.ShapeDtypeStruct) and getattr(spec, "block_shape", None) is not None
        return pltpu.HBM(shape.shape, shape.dtype) if blocked else shape

    n_scalar = 0
    if out_specs is None:
        specs = kw["grid_spec"].out_specs
        n_scalar = kw["grid_spec"].num_scalar_prefetch
    else:
        kw["out_specs"] = specs = out_specs
    if isinstance(out_shape, (tuple, list)):
        out_shape = tuple(pin_out(s, p) for s, p in zip(out_shape, specs))
    else:
        out_shape = pin_out(out_shape, specs)
    call = pl.pallas_call(body, out_shape=out_shape, **kw)

    def run(*operands):
        def pin(o):
            is_data = jnp.issubdtype(o.dtype, jnp.floating) or jnp.issubdtype(o.dtype, jnp.integer)
            return pltpu.with_memory_space_constraint(o, pltpu.HBM) if is_data else o
        return call(*operands[:n_scalar], *[pin(o) for o in operands[n_scalar:]])

    return run

ANY = pl.BlockSpec(memory_space=pl.ANY)
VMEM = pl.BlockSpec(memory_space=pltpu.VMEM)


def _whole(a):
    return pl.BlockSpec(a.shape, lambda i: (0,) * len(a.shape))


def _params(vmem_mib=None, sem=None):
    kw = {}
    if vmem_mib is not None:
        kw["vmem_limit_bytes"] = vmem_mib * MIB
    if sem is not None:
        kw["dimension_semantics"] = sem
    return pltpu.CompilerParams(**kw)


def _nn(a, b):
    return jnp.dot(a, b, preferred_element_type=F32)


def _nt(a, b):
    return lax.dot_general(a, b, (((1,), (1,)), ((), ())), preferred_element_type=F32)


def _tn(a, b):
    return lax.dot_general(a, b, (((0,), (0,)), ((), ())), preferred_element_type=F32)


def _sigmoid(v):
    return 1.0 / (1.0 + jnp.exp(-v))


def _part8(v):
    return v.reshape(v.shape[0] // 8, 8, v.shape[1]).sum(axis=0)


def _my_position():
    return lax.axis_index("x"), lax.axis_index("y"), lax.axis_index("c")


def _flat(px, py, pc):
    return 4 * px + 2 * py + pc


def _peer(mask):
    x, y, c = _my_position()
    return (x ^ ((mask >> 2) & 1), y ^ ((mask >> 1) & 1), c ^ (mask & 1))


def _column_chunks(n):
    chunks = [(128 * a, 0, 128 * a, 128) for a in range(n // 128)]
    if n % 128:
        chunks.append((n - 128, 128 - n % 128, 128 * (n // 128), n % 128))
    return chunks


def _cast_rows(w, n_steps, name):
    rows, ncol = w.shape
    blk = pl.BlockSpec((rows // n_steps, ncol), lambda i: (i, 0))

    def body(w_ref, o_ref):
        o_ref[...] = w_ref[...].astype(BF16)

    return _pcall(body, grid=(n_steps,), out_shape=jax.ShapeDtypeStruct(w.shape, BF16), in_specs=[blk], out_specs=blk,
                  name=name, compiler_params=_params(16, ("parallel",)))(w)


def _prep(w_pa, w_pb, w_out, c, conv_w):
    def body(wpa_ref, wpb_ref, wout_ref, c_ref, cw_ref, wpat_ref, wpb_o, wout_o, cact_ref, cwp_ref):
        wpat_ref[...] = wpa_ref[...].T.astype(BF16)
        wpb_o[...] = wpb_ref[...].astype(BF16)
        wout_o[...] = wout_ref[...].astype(BF16)
        cv = c_ref[...]
        cact_ref[...] = jnp.zeros_like(cact_ref)
        cact_ref[pl.ds(0, cv.shape[0]), :] = cv * _sigmoid(cv)
        cwp_ref[...] = jnp.zeros_like(cwp_ref)
        cwp_ref[pl.ds(0, 3), :] = cw_ref[...]

    out_shape = (jax.ShapeDtypeStruct((w_pa.shape[1], w_pa.shape[0]), BF16),
                 jax.ShapeDtypeStruct(w_pb.shape, BF16), jax.ShapeDtypeStruct(w_out.shape, BF16),
                 jax.ShapeDtypeStruct((8, D), F32), jax.ShapeDtypeStruct((8, conv_w.shape[1]), F32))
    operands = (w_pa, w_pb, w_out, c, conv_w)
    return _pcall(body, grid=(1,), out_shape=out_shape, in_specs=[_whole(a) for a in operands],
                  out_specs=tuple(_whole(o) for o in out_shape), name="prep", compiler_params=_params(16))(*operands)


def _gather_rows(shards):
    n = len(shards)

    def body(*refs):
        srcs, outs = refs[:n], refs[n:2 * n]
        send_sems, recv_sems, local_sems = refs[2 * n:]
        x, y, c = _my_position()
        me, sibling = (x, y, c), (x, y, 1 - c)
        chips = [(1 - x, y), (x, 1 - y), (1 - x, 1 - y)]

        def rows(a, px, py, pc):
            r = shards[a].shape[0]
            return outs[a].at[pl.ds(pl.multiple_of(_flat(px, py, pc) * r, r), r), :]

        def copy(a, k, block, to, src=None):
            return pltpu.make_async_remote_copy(
                src_ref=rows(a, *block) if src is None else src, dst_ref=rows(a, *block),
                send_sem=send_sems.at[7 * a + k], recv_sem=recv_sems.at[7 * a + k], device_id=to, device_id_type=MESH)

        mine = [pltpu.make_async_copy(srcs[a], rows(a, *me), local_sems.at[a]) for a in range(n)]
        for cp in mine:
            cp.start()
        first = []
        for a in range(n):
            first.append(copy(a, 0, me, sibling, src=srcs[a]))
            first += [copy(a, 1 + j, me, (*chip, c), src=srcs[a]) for j, chip in enumerate(chips)]
        for cp in first:
            cp.start()
        passed = []
        for j, chip in enumerate(chips):
            for a in range(n):
                copy(a, 1 + j, (*chip, c), me).wait_recv()
                cp = copy(a, 4 + j, (*chip, c), sibling)
                cp.start()
                passed.append(cp)
        for a in range(n):
            copy(a, 0, sibling, me).wait_recv()
        for j, chip in enumerate(chips):
            for a in range(n):
                copy(a, 4 + j, (*chip, 1 - c), me).wait_recv()
        for cp in first + passed:
            cp.wait_send()
        for cp in mine:
            cp.wait()

    out_shape = tuple(jax.ShapeDtypeStruct((N_DEV * s.shape[0], s.shape[1]), s.dtype) for s in shards)
    return _pcall(body, out_shape=out_shape, in_specs=[ANY] * n, out_specs=(ANY,) * n, name="gather_rows",
                  scratch_shapes=[pltpu.SemaphoreType.DMA((7 * n,)), pltpu.SemaphoreType.DMA((7 * n,)),
                                  pltpu.SemaphoreType.DMA((n,))])(*shards)


def _exchange_slots(out_ref, send_sems, recv_sems):
    me = _flat(*_my_position())
    copies = []
    for mask in range(1, N_DEV):
        peer = _peer(mask)
        copies.append((mask, pltpu.make_async_remote_copy(
            src_ref=out_ref.at[me], dst_ref=out_ref.at[me], send_sem=send_sems.at[mask - 1],
            recv_sem=recv_sems.at[mask - 1], device_id=peer, device_id_type=MESH)))
    for _, cp in copies:
        cp.start()
    for mask, _ in copies:
        peer = _peer(mask)
        pltpu.make_async_remote_copy(
            src_ref=out_ref.at[_flat(*peer)], dst_ref=out_ref.at[_flat(*peer)], send_sem=send_sems.at[mask - 1],
            recv_sem=recv_sems.at[mask - 1], device_id=peer, device_id_type=MESH).wait_recv()
    for _, cp in copies:
        cp.wait_send()


def _ada_forward(cact_all, w_ada, b_ada_mine):
    nb, ncol = cact_all.shape[0], w_ada.shape[1]

    def body(c_ref, w_ref, b_ref, out_ref, send_sems, recv_sems):
        me = _flat(*_my_position())
        out_ref[me] = _nn(c_ref[...].astype(BF16), w_ref[...].astype(BF16)) + b_ref[...]
        _exchange_slots(out_ref, send_sems, recv_sems)

    operands = (cact_all, w_ada, b_ada_mine)
    return _pcall(body, grid=(1,), out_shape=jax.ShapeDtypeStruct((N_DEV, nb, ncol), F32),
                  in_specs=[_whole(a) for a in operands], out_specs=VMEM,
                  scratch_shapes=[pltpu.SemaphoreType.DMA((7,)), pltpu.SemaphoreType.DMA((7,))], name="ada_forward",
                  compiler_params=_params(16))(*operands)


def _small_reduce(gb_rest, gb_qkv, svec, dgate, dss):
    nbat = dgate.shape[0]

    def body(gbr_ref, q0_ref, q1_ref, q2_ref, sv_ref, dg_ref, dss_ref, rows_ref, tot_ref, gbada_ref, send_sems, recv_sems):
        me = _flat(*_my_position())

        def put(off, v):
            rows_ref[me, :, pl.ds(off, v.shape[1])] = v

        def row(v):
            return jnp.sum(v, axis=0, keepdims=True)

        for g, q_ref in enumerate((q0_ref, q1_ref, q2_ref)):
            for which in range(3):
                put(P_BIN + SLAB * (3 * which + g), row(q_ref[which]))
        for s in range(N_REST):
            put(P_BIN + SLAB * (N_QKV + s), row(gbr_ref[s]))
        put(P_LNG, row(sv_ref[0]))
        put(P_LNB, row(sv_ref[1]))
        put(P_BOUT, row(sv_ref[2]))
        for j in range(3):
            put(P_CONV + D * j, row(sv_ref[3 + j]))
        loss = (0.5 / D) * jnp.sum(row(sv_ref[6]), axis=1, keepdims=True)
        put(P_LOSS, jnp.broadcast_to(loss, (1, 128)))
        for b in range(nbat):
            put(P_DADA + 3 * D * b, row(dss_ref[b, 0]))
            put(P_DADA + 3 * D * b + D, row(dss_ref[b, 1]))
            put(P_DADA + 3 * D * b + 2 * D, row(dg_ref[b]))
        _exchange_slots(rows_ref, send_sems, recv_sems)
        tot = rows_ref[0]
        for k in range(1, N_DEV):
            tot = tot + rows_ref[k]
        tot_ref[...] = tot
        gbada = tot[:, P_DADA:P_DADA + 3 * D]
        for b in range(1, nbat):
            gbada = gbada + tot[:, P_DADA + 3 * D * b:P_DADA + 3 * D * (b + 1)]
        gbada_ref[...] = gbada

    p_len = P_DADA + nbat * 3 * D
    out_shape = (jax.ShapeDtypeStruct((N_DEV, 1, p_len), F32), jax.ShapeDtypeStruct((1, p_len), F32),
                 jax.ShapeDtypeStruct((1, 3 * D), F32))
    operands = (gb_rest, *gb_qkv, svec, dgate, dss)
    return _pcall(body, grid=(1,), out_shape=out_shape, in_specs=[_whole(a) for a in operands],
                  out_specs=(VMEM, _whole(out_shape[1]), _whole(out_shape[2])),
                  scratch_shapes=[pltpu.SemaphoreType.DMA((7,)), pltpu.SemaphoreType.DMA((7,))], name="small_reduce",
                  compiler_params=_params(16))(*operands)


def _make_h(x, ada, tm=512):
    t = x.shape[0]
    tps = (t // ada.shape[0]) // tm

    def body(x_ref, ada_ref, h_ref):
        h_ref[...] = (x_ref[...] * (1.0 + ada_ref[0, 1:2, :]) + ada_ref[0, 0:1, :]).astype(BF16)

    return _pcall(body, grid=(t // tm,), out_shape=jax.ShapeDtypeStruct((t, D), BF16),
                  in_specs=[pl.BlockSpec((tm, D), lambda i: (i, 0)), pl.BlockSpec((1, 3, D), lambda i: (i // tps, 0, 0))],
                  out_specs=pl.BlockSpec((tm, D), lambda i: (i, 0)), name="make_h",
                  compiler_params=_params(32, ("parallel",)))(x, ada)


PIECE = 64
N_CHUNK = 4
ARRIVAL_RANK = (0, 1, 3, 5, 2, 4, 6, 7)
SLOT_MASK = (1, 4, 2, 6, 5, 3, 7)


def _arrival_tables(shard_rows):
    import numpy as np
    crow = shard_rows // N_CHUNK
    table = np.zeros((N_DEV, N_SLAB + 7 * N_CHUNK), np.int32)
    lo = [(SLAB * j) // crow for j in range(N_SLAB)]
    hi = [(SLAB * j + SLAB - 1) // crow for j in range(N_SLAB)]
    for k in range(N_DEV):
        def rank(chunk):
            shard_rank = ARRIVAL_RANK[(chunk // N_CHUNK) ^ k]
            return shard_rank if shard_rank < 2 else 2 + 8 * (chunk % N_CHUNK) + shard_rank
        order = sorted(range(N_SLAB), key=lambda j: (max(rank(lo[j]), rank(hi[j])), j))
        table[k, :N_SLAB] = order
        for slot, mask in enumerate(SLOT_MASK):
            for ch in range(N_CHUNK):
                chunk = (k ^ mask) * N_CHUNK + ch
                table[k, N_SLAB + slot * N_CHUNK + ch] = min(t for t, j in enumerate(order) if lo[j] <= chunk <= hi[j])
    return table


def _project_gather(shard, h, b_in3, others):
    t = h.shape[0]
    n_o = len(others)
    srows = shard.shape[0]
    crow = srows // N_CHUNK
    shards = [shard] + list(others)
    table = jnp.asarray(_arrival_tables(srows))

    def body(tbl_ref, *refs):
        srcs = [refs[0]] + list(refs[3:3 + n_o])
        h_ref, b_ref = refs[1], refs[2]
        outs = [refs[3 + n_o]] + list(refs[6 + n_o:6 + 2 * n_o])
        qkv_ref, rest_ref = refs[4 + n_o], refs[5 + n_o]
        (wtile, obf, of32, send_sems, recv_sems, local_sems, tile_sems, obf_sems, of32_sems) = refs[6 + 2 * n_o:]
        w_full = outs[0]
        x, y, c = _my_position()
        k = _flat(x, y, c)
        me, sibling = (x, y, c), (x, y, 1 - c)
        chips = [(1 - x, y), (x, 1 - y), (1 - x, 1 - y)]

        def rows(a, px, py, pc, ch):
            r = shards[a].shape[0]
            if ch is None:
                return outs[a].at[pl.ds(pl.multiple_of(_flat(px, py, pc) * r, r), r), :]
            return outs[a].at[pl.ds(pl.multiple_of(_flat(px, py, pc) * r + ch * crow, crow), crow), :]

        def copy(a, slot, block, to, ch=None, src=None):
            sem = slot * N_CHUNK + ch if a == 0 else 7 * (N_CHUNK - 1 + a) + slot
            if src is not None and ch is not None:
                src = src.at[pl.ds(ch * crow, crow), :]
            return pltpu.make_async_remote_copy(
                src_ref=rows(a, *block, ch) if src is None else src, dst_ref=rows(a, *block, ch),
                send_sem=send_sems.at[sem], recv_sem=recv_sems.at[sem], device_id=to, device_id_type=MESH)

        mine = [pltpu.make_async_copy(srcs[a], rows(a, *me, None), local_sems.at[a]) for a in range(1 + n_o)]
        first = []
        for ch in range(N_CHUNK):
            first.append(copy(0, 0, me, sibling, ch, src=srcs[0]))
            first += [copy(0, 1 + j, me, (*chip, c), ch, src=srcs[0]) for j, chip in enumerate(chips)]
        for a in range(1, 1 + n_o):
            first.append(copy(a, 0, me, sibling, src=srcs[a]))
            first += [copy(a, 1 + j, me, (*chip, c), src=srcs[a]) for j, chip in enumerate(chips)]
        for cp in mine + first:
            cp.start()

        def arrive(a, slot, ch=None):
            if slot == 0:
                copy(a, 0, sibling, me, ch).wait_recv()
            elif slot < 4:
                copy(a, slot, (*chips[slot - 1], c), me, ch).wait_recv()
                copy(a, slot + 3, (*chips[slot - 1], c), sibling, ch).start()
            else:
                copy(a, slot, (*chips[slot - 4], 1 - c), me, ch).wait_recv()

        def arrive_for(step):
            for slot in range(7):
                for ch in range(N_CHUNK):
                    @pl.when(tbl_ref[k, N_SLAB + slot * N_CHUNK + ch] == step)
                    def _():
                        arrive(0, slot, ch)

        def fetch(step, buf):
            slab = tbl_ref[k, step]
            for p in range(SLAB // PIECE):
                g0 = slab * SLAB + PIECE * p
                own = (g0 >= k * srows) & (g0 < (k + 1) * srows)
                dst = wtile.at[buf, pl.ds(PIECE * p, PIECE), :]

                @pl.when(own)
                def _():
                    pltpu.make_async_copy(srcs[0].at[pl.ds(pl.multiple_of(g0 - k * srows, PIECE), PIECE), :], dst, tile_sems.at[buf]).start()

                @pl.when(jnp.logical_not(own))
                def _():
                    pltpu.make_async_copy(w_full.at[pl.ds(pl.multiple_of(g0, PIECE), PIECE), :], dst, tile_sems.at[buf]).start()

        def wait_tile(buf):
            pltpu.make_async_copy(w_full.at[pl.ds(0, SLAB), :], wtile.at[buf], tile_sems.at[buf]).wait()

        def put(buf_ref, sems, dst_ref, count, value):
            b = count % 2

            @pl.when(count >= 2)
            def _():
                pltpu.make_async_copy(buf_ref.at[b], dst_ref, sems.at[b]).wait()

            buf_ref[b] = value
            pltpu.make_async_copy(buf_ref.at[b], dst_ref, sems.at[b]).start()

        def drain(buf_ref, sems, dst_ref, count):
            for back in (1, 2):
                @pl.when(count >= back)
                def _():
                    pltpu.make_async_copy(buf_ref.at[(count - back) % 2], dst_ref, sems.at[(count - back) % 2]).wait()

        arrive_for(0)
        fetch(0, 0)

        def step(s, carry):
            n_bf, n_f32 = carry
            buf = s % 2

            @pl.when(s + 1 < N_SLAB)
            def _():
                arrive_for(s + 1)
                fetch(s + 1, 1 - buf)

            wait_tile(buf)
            slab = tbl_ref[k, s]
            v = _nt(h_ref[...], wtile[buf]) + b_ref[slab]
            is_qkv = slab < N_QKV

            @pl.when(is_qkv)
            def _():
                put(obf, obf_sems, qkv_ref.at[jnp.minimum(slab, N_QKV - 1)], n_bf, v.astype(BF16))

            @pl.when(jnp.logical_not(is_qkv))
            def _():
                put(of32, of32_sems, rest_ref.at[jnp.maximum(slab - N_QKV, 0)], n_f32, v)

            return n_bf + is_qkv.astype(jnp.int32), n_f32 + 1 - is_qkv.astype(jnp.int32)

        n_bf, n_f32 = lax.fori_loop(0, N_SLAB, step, (jnp.int32(0), jnp.int32(0)))
        drain(obf, obf_sems, qkv_ref.at[0], n_bf)
        drain(of32, of32_sems, rest_ref.at[0], n_f32)

        for slots in ((1, 2, 3), (0, 4, 5, 6)):
            for a in range(1, 1 + n_o):
                for slot in slots:
                    arrive(a, slot)
        for cp in first:
            cp.wait_send()
        for j, chip in enumerate(chips):
            for ch in range(N_CHUNK):
                copy(0, 4 + j, (*chip, c), sibling, ch).wait_send()
            for a in range(1, 1 + n_o):
                copy(a, 4 + j, (*chip, c), sibling).wait_send()
        for cp in mine:
            cp.wait()

    out_shape = ((jax.ShapeDtypeStruct((N_DEV * srows, D), BF16), jax.ShapeDtypeStruct((N_QKV, t, SLAB), BF16),
                  jax.ShapeDtypeStruct((N_REST, t, SLAB), F32))
                 + tuple(jax.ShapeDtypeStruct((N_DEV * o.shape[0], o.shape[1]), o.dtype) for o in others))
    n_all = 1 + n_o
    n_sems = 7 * (N_CHUNK + n_o)
    grid_spec = pltpu.PrefetchScalarGridSpec(
        num_scalar_prefetch=1, grid=(1,),
        in_specs=[ANY, pl.BlockSpec((t, D), lambda i, tbl: (0, 0), pipeline_mode=pl.Buffered(1)),
                  pl.BlockSpec((N_SLAB, 1, SLAB), lambda i, tbl: (0, 0, 0))] + [ANY] * n_o,
        out_specs=(ANY,) * (3 + n_o),
        scratch_shapes=[pltpu.VMEM((2, SLAB, D), BF16), pltpu.VMEM((2, t, SLAB), BF16), pltpu.VMEM((2, t, SLAB), F32),
                        pltpu.SemaphoreType.DMA((n_sems,)), pltpu.SemaphoreType.DMA((n_sems,)),
                        pltpu.SemaphoreType.DMA((n_all,)), pltpu.SemaphoreType.DMA((2,)), pltpu.SemaphoreType.DMA((2,)),
                        pltpu.SemaphoreType.DMA((2,))])
    res = _pcall(body, grid_spec=grid_spec, out_shape=out_shape, name="project_gather",
                 compiler_params=_params(48, ("arbitrary",)))(table, shard, h, b_in3, *others)
    return res[0], res[1], res[2], list(res[3:])


def _bias_tables(g):
    window, dil = GROUPS[g]
    span = window // dil
    qi = jnp.arange(BLK)[:, None]
    kj = jnp.arange(2 * BLK)[None, :]
    delta = qi + BLK - kj
    valid = (delta >= 0) & (delta <= span)
    heads = jnp.arange(4, dtype=F32) + 4.0 * g
    slopes = 2.0 ** (-8.0 * (heads + 1.0) / 12.0)
    bias = -slopes[:, None, None] * (delta * dil).astype(F32)[None]
    return jnp.where(valid[None], bias, -1e30).reshape(4 * BLK, 2 * BLK)


def _head_masks(shape):
    lane = lax.broadcasted_iota(jnp.int32, shape, 1)
    return [(lane >= 64 * h) & (lane < 64 * (h + 1)) for h in range(4)]


def _stack_heads(v, masks):
    return jnp.concatenate([jnp.where(masks[h], v, jnp.zeros_like(v)) for h in range(4)], axis=0)


def _unstack_heads(v4, masks):
    out = jnp.where(masks[0], v4[0:BLK], 0.0)
    for h in range(1, 4):
        out = jnp.where(masks[h], v4[BLK * h:BLK * (h + 1)], out)
    return out


def _regroup(load_half, dst_ref, stage_ref, n, dil):
    for hlf in range(2):
        stage_ref[hlf] = load_half(hlf)

    def residue(r, carry):
        for hlf in range(2):
            dst_ref[pl.ds(pl.multiple_of(r * n, BLK), n), pl.ds(128 * hlf, 128)] = (
                stage_ref[hlf, pl.ds(r, n, stride=dil), :].astype(dst_ref.dtype))
        return carry

    lax.fori_loop(0, dil, residue, 0)


def _store_block(nat_ref, r, i, val, dil):
    for hlf in range(2):
        nat_ref[hlf, pl.ds(r + dil * BLK * i, BLK, stride=dil), :] = val[:, 128 * hlf:128 * (hlf + 1)]


def _for_blocks(block, dil, nblk):
    if dil == 1:
        block(0, 0, True)
        block(0, 1, False)

        def pair(k, carry):
            block(0, 2 * k, False)
            block(0, 2 * k + 1, False)
            return carry

        lax.fori_loop(1, nblk // 2, pair, 0)
    else:
        def residues(k, carry):
            block(2 * k, 0, True)
            block(2 * k + 1, 0, True)
            if nblk > 1:
                def loop(i, c):
                    block(2 * k, i, False)
                    block(2 * k + 1, i, False)
                    return c
                lax.fori_loop(1, nblk, loop, 0)
            return carry

        lax.fori_loop(0, dil // 2, residues, 0)


def _attn_forward(qkv, g, nbat):
    t = qkv.shape[1]
    seq = t // nbat
    dil = GROUPS[g][1]
    n = seq // dil
    nblk = n // BLK
    qkv4 = qkv.reshape(3, 3, t, SLAB)

    def body(qkv_ref, bias_ref, ol_ref, *scratch):
        masks = _head_masks((BLK, SLAB))
        if dil > 1:
            stage, qd, kd, vd, nat_o, nat_l = scratch
            for which, dst in enumerate((qd, kd, vd)):
                _regroup(lambda hlf, which=which: qkv_ref[which, 0, :, pl.ds(128 * hlf, 128)].astype(F32), dst, stage, n, dil)
        else:
            qd, kd, vd = qkv_ref.at[0, 0], qkv_ref.at[1, 0], qkv_ref.at[2, 0]

        def block(r, i, first):
            base = r * n
            qs = pl.ds(pl.multiple_of(base + i * BLK, BLK), BLK)
            ks = pl.ds(pl.multiple_of(base, BLK), BLK) if first else pl.ds(pl.multiple_of(base + (i - 1) * BLK, BLK), 2 * BLK)
            q, kk, vv = qd[qs, :], kd[ks, :], vd[ks, :]
            bias = bias_ref[:, pl.ds(BLK, BLK)] if first else bias_ref[...]
            s = _nt(_stack_heads(q, masks), kk) * 0.125 + bias
            m = jnp.max(s, axis=1, keepdims=True)
            p = jnp.exp(s - m)
            den = jnp.sum(p, axis=1, keepdims=True)
            out = _unstack_heads(_nn((p * (1.0 / den)).astype(BF16), vv), masks)
            lse = _unstack_heads(jnp.broadcast_to(m + jnp.log(den), (4 * BLK, SLAB)), masks)
            if dil > 1:
                _store_block(nat_o, r, i, out, dil)
                _store_block(nat_l, r, i, lse, dil)
            else:
                ol_ref[0, qs, :] = out
                ol_ref[1, qs, :] = lse

        _for_blocks(block, dil, nblk)
        if dil > 1:
            for hlf in range(2):
                ol_ref[0, :, pl.ds(128 * hlf, 128)] = nat_o[hlf]
                ol_ref[1, :, pl.ds(128 * hlf, 128)] = nat_l[hlf]

    scratch = []
    if dil > 1:
        scratch = [pltpu.VMEM((2, seq, 128), F32)] + [pltpu.VMEM((seq, SLAB), BF16)] * 3 + [pltpu.VMEM((2, seq, 128), F32)] * 2
    return _pcall(
        body, grid=(nbat,), out_shape=jax.ShapeDtypeStruct((2, t, SLAB), F32),
        in_specs=[pl.BlockSpec((3, 1, seq, SLAB), lambda b: (0, g, b, 0)),
                  pl.BlockSpec((4 * BLK, 2 * BLK), lambda b: (0, 0))],
        out_specs=pl.BlockSpec((2, seq, SLAB), lambda b: (0, b, 0)), scratch_shapes=scratch,
        name=f"attn_forward_{g}", compiler_params=_params(40, ("parallel",)))(qkv4, _bias_tables(g))


def _attn_backward(qkv, do_attn, ol_tot, dproj, g, nbat):
    t = qkv.shape[1]
    seq = t // nbat
    dil = GROUPS[g][1]
    n = seq // dil
    nblk = n // BLK
    qkv4 = qkv.reshape(3, 3, t, SLAB)
    dp4 = dproj.reshape(DP_SLABS // 3, 3, t, SLAB)

    def body(qkv_ref, do_ref, ol_ref, bias_ref, dp_in, dp_ref, gb_ref, dk_acc, dv_acc, *scratch):
        del dp_in
        masks = _head_masks((BLK, SLAB))

        @pl.when(pl.program_id(0) == 0)
        def _():
            gb_ref[...] = jnp.zeros_like(gb_ref)

        dk_acc[...] = jnp.zeros_like(dk_acc)
        dv_acc[...] = jnp.zeros_like(dv_acc)
        if dil > 1:
            stage, qd, kd, vd, dod, prodd, lsed, nat = scratch
            lanes = lambda hlf: pl.ds(128 * hlf, 128)
            for which, dst in enumerate((qd, kd, vd)):
                _regroup(lambda hlf, which=which: qkv_ref[which, 0, :, lanes(hlf)].astype(F32), dst, stage, n, dil)
            _regroup(lambda hlf: do_ref[:, lanes(hlf)].astype(F32), dod, stage, n, dil)
            _regroup(lambda hlf: do_ref[:, lanes(hlf)].astype(F32) * ol_ref[0, :, lanes(hlf)], prodd, stage, n, dil)
            _regroup(lambda hlf: ol_ref[1, :, lanes(hlf)], lsed, stage, n, dil)
        else:
            qd, kd, vd = qkv_ref.at[0, 0], qkv_ref.at[1, 0], qkv_ref.at[2, 0]

        def block(r, i, first):
            base = r * n
            qs = pl.ds(pl.multiple_of(base + i * BLK, BLK), BLK)
            ks = pl.ds(pl.multiple_of(base, BLK), BLK) if first else pl.ds(pl.multiple_of(base + (i - 1) * BLK, BLK), 2 * BLK)
            q, kk, vv = qd[qs, :], kd[ks, :], vd[ks, :]
            if dil > 1:
                do, prod, lse = dod[qs, :], prodd[qs, :], lsed[qs, :]
            else:
                do = do_ref[qs, :]
                prod = do.astype(F32) * ol_ref[0, qs, :]
                lse = ol_ref[1, qs, :]
            q4, do4 = _stack_heads(q, masks), _stack_heads(do, masks)
            bias = bias_ref[:, pl.ds(BLK, BLK)] if first else bias_ref[...]
            lse4 = jnp.concatenate([lse[:, 64 * h:64 * h + 1] for h in range(4)], axis=0)
            delta4 = jnp.concatenate([jnp.sum(jnp.where(masks[h], prod, 0.0), axis=1, keepdims=True) for h in range(4)], axis=0)
            p = jnp.exp(_nt(q4, kk) * 0.125 + bias - lse4)
            ds = (p * (_nt(do4, vv) - delta4)).astype(BF16)
            dv_acc[ks, :] += _tn(p.astype(BF16), do4)
            dk_acc[ks, :] += _tn(ds, q4) * 0.125
            dq = _unstack_heads(_nn(ds, kk), masks) * 0.125
            if dil > 1:
                _store_block(nat, r, i, dq, dil)
            else:
                dp_ref[0, 0, qs, :] = dq.astype(BF16)
            gb_ref[0] += _part8(dq)

        _for_blocks(block, dil, nblk)
        gb_ref[1] += _part8(dk_acc[...])
        gb_ref[2] += _part8(dv_acc[...])
        if dil > 1:
            def flush(which):
                for hlf in range(2):
                    dp_ref[which, 0, :, pl.ds(128 * hlf, 128)] = nat[hlf].astype(BF16)

            def to_token_order(acc_ref):
                def residue(r, carry):
                    for hlf in range(2):
                        nat[hlf, pl.ds(r, n, stride=dil), :] = acc_ref[pl.ds(pl.multiple_of(r * n, BLK), n), pl.ds(128 * hlf, 128)]
                    return carry
                lax.fori_loop(0, dil, residue, 0)

            flush(0)
            to_token_order(dk_acc)
            flush(1)
            to_token_order(dv_acc)
            flush(2)
        else:
            dp_ref[1, 0] = dk_acc[...].astype(BF16)
            dp_ref[2, 0] = dv_acc[...].astype(BF16)

    scratch = [pltpu.VMEM((seq, SLAB), F32)] * 2
    if dil > 1:
        scratch += ([pltpu.VMEM((2, seq, 128), F32)] + [pltpu.VMEM((seq, SLAB), BF16)] * 4 + [pltpu.VMEM((seq, SLAB), F32)] * 2
                    + [pltpu.VMEM((2, seq, 128), F32)])
    dp, gb = _pcall(
        body, grid=(nbat,),
        out_shape=(jax.ShapeDtypeStruct(dp4.shape, BF16), jax.ShapeDtypeStruct((3, 8, SLAB), F32)),
        in_specs=[pl.BlockSpec((3, 1, seq, SLAB), lambda b: (0, g, b, 0)),
                  pl.BlockSpec((seq, SLAB), lambda b: (b, 0)),
                  pl.BlockSpec((2, seq, SLAB), lambda b: (0, b, 0)),
                  pl.BlockSpec((4 * BLK, 2 * BLK), lambda b: (0, 0)), ANY],
        out_specs=(pl.BlockSpec((3, 1, seq, SLAB), lambda b: (DP_SLABS // 9 - 1, g, b, 0)),
                   pl.BlockSpec((3, 8, SLAB), lambda b: (0, 0, 0))),
        scratch_shapes=scratch, input_output_aliases={4: 0}, name=f"attn_backward_{g}",
        compiler_params=_params(48, ("arbitrary",)))(qkv4, do_attn, ol_tot, _bias_tables(g), dp4)
    return dp.reshape(DP_SLABS, t, SLAB), gb


def _mid(rest, ols, x, tgt, ada, cw, b_out, ln_g, ln_b, w_pa_t, w_pb, w_out, tm=256):
    t = x.shape[0]
    nbat = ada.shape[0]
    nt = t // tm
    tps = nt // nbat

    def body(rest_ref, halo_ref, ol0_ref, ol1_ref, ol2_ref, x_ref, t_ref, ada_ref, cw_ref, bout_ref, lng_ref, lnb_ref,
             wpat_ref, wpb_ref, wout_ref,
             dp_ref, gx0_ref, doa_ref, olt_ref, mg_ref, dof_ref, bbs_ref, dyc_ref, a_ref, dya_ref,
             gbr_ref, sv_ref, dgate_ref, carry_ref, keep_ref):
        i = pl.program_id(0)
        ti = nt - 1 - i
        pos = ti % tps

        @pl.when(i == 0)
        def _():
            gbr_ref[...] = jnp.zeros_like(gbr_ref)
            sv_ref[...] = jnp.zeros_like(sv_ref)

        @pl.when(pos == tps - 1)
        def _():
            dgate_ref[...] = jnp.zeros_like(dgate_ref)
            carry_ref[...] = jnp.zeros_like(carry_ref)

        row = lax.broadcasted_iota(jnp.int32, (tm, SLAB), 0)
        halo_on = (pos > 0).astype(F32)

        def cols(s):
            return pl.ds(SLAB * s, SLAB)

        l0, l1, l2 = ol0_ref[1], ol1_ref[1], ol2_ref[1]
        mx = jnp.maximum(jnp.maximum(l0, l1), l2)
        e0, e1, e2 = jnp.exp(l0 - mx), jnp.exp(l1 - mx), jnp.exp(l2 - mx)
        den = e0 + e1 + e2
        o_attn = (e0 * ol0_ref[0] + e1 * ol1_ref[0] + e2 * ol2_ref[0]) * (1.0 / den)
        olt_ref[0] = o_attn
        olt_ref[1] = mx + jnp.log(den)
        z_a = rest_ref[R_ZA]
        sg_za = _sigmoid(z_a)
        a_ref[...] = (o_attn * z_a * sg_za).astype(BF16)
        y_attn = _nt(a_ref[...], wpat_ref[...])

        for s in range(4):
            u = rest_ref[R_GC + s] * rest_ref[R_UX + s]
            hu = halo_ref[R_GC + s] * halo_ref[R_UX + s] * halo_on
            u1 = jnp.where(row == 0, hu[7:8], pltpu.roll(u, 1, 0))
            u2 = jnp.where(row == 0, hu[6:7], jnp.where(row == 1, hu[7:8], pltpu.roll(u, 2, 0)))
            conv = cw_ref[0:1, cols(s)] * u2 + cw_ref[1:2, cols(s)] * u1 + cw_ref[2:3, cols(s)] * u
            zc = rest_ref[R_ZC + s]
            sg = _sigmoid(zc)
            keep_ref[2, :, cols(s)], keep_ref[3, :, cols(s)], keep_ref[4, :, cols(s)], keep_ref[5, :, cols(s)] = u1, u2, conv, sg
            bbs_ref[:, cols(s)] = (rest_ref[R_GB + s] * conv * (zc * sg)).astype(BF16)
        y_conv = _nn(bbs_ref[...], wpb_ref[...])

        for s in range(4):
            s_a, s_b = _sigmoid(rest_ref[R_GA + s]), _sigmoid(rest_ref[R_GBM + s])
            keep_ref[0, :, cols(s)], keep_ref[1, :, cols(s)] = s_a, s_b
            mg_ref[:, cols(s)] = (s_a * y_attn[:, SLAB * s:SLAB * (s + 1)] + s_b * y_conv[:, SLAB * s:SLAB * (s + 1)]).astype(BF16)
        o = _nn(mg_ref[...], wout_ref[...]) + bout_ref[...]
        gate = ada_ref[0, 2:3, :]
        r = ALPHA * x_ref[...] + gate * o
        mu = jnp.mean(r, axis=1, keepdims=True)
        rc = r - mu
        rstd = lax.rsqrt(jnp.mean(rc * rc, axis=1, keepdims=True) + LN_EPS)
        xhat = rc * rstd
        err = xhat * lng_ref[...] + lnb_ref[...] - t_ref[...]
        sv_ref[6] += _part8(err * err)
        dy = err * (1.0 / D)
        sv_ref[0] += _part8(dy * xhat)
        sv_ref[1] += _part8(dy)
        dxh = dy * lng_ref[...]
        dr = rstd * (dxh - jnp.mean(dxh, axis=1, keepdims=True) - xhat * jnp.mean(dxh * xhat, axis=1, keepdims=True))
        gx0_ref[...] = ALPHA * dr
        dgate_ref[0] += _part8(dr * o)
        do_ = dr * gate
        sv_ref[2] += _part8(do_)
        dof_ref[...] = do_.astype(BF16)
        dmerged = _nt(dof_ref[...], wout_ref[...])
        for s in range(4):
            s_a, s_b = keep_ref[0, :, cols(s)], keep_ref[1, :, cols(s)]
            dm = dmerged[:, SLAB * s:SLAB * (s + 1)]
            ya, yc = y_attn[:, SLAB * s:SLAB * (s + 1)], y_conv[:, SLAB * s:SLAB * (s + 1)]
            dya_ref[:, cols(s)] = (dm * s_a).astype(BF16)
            dyc_ref[:, cols(s)] = (dm * s_b).astype(BF16)
            dga = dm * ya * s_a * (1.0 - s_a)
            dgb = dm * yc * s_b * (1.0 - s_b)
            dp_ref[R_GA + s] = dga.astype(BF16)
            dp_ref[R_GBM + s] = dgb.astype(BF16)
            gbr_ref[R_GA + s] += _part8(dga)
            gbr_ref[R_GBM + s] += _part8(dgb)

        da = _nn(dya_ref[...], wpat_ref[...])
        doa_ref[...] = (da * z_a * sg_za).astype(BF16)
        dza = da * o_attn * (sg_za * (1.0 + z_a * (1.0 - sg_za)))
        dp_ref[R_ZA] = dza.astype(BF16)
        gbr_ref[R_ZA] += _part8(dza)

        dbb = _nt(dyc_ref[...], wpb_ref[...])
        for s in range(4):
            ux, gc, zc = rest_ref[R_UX + s], rest_ref[R_GC + s], rest_ref[R_ZC + s]
            u = gc * ux
            u1, u2, conv, sg = keep_ref[2, :, cols(s)], keep_ref[3, :, cols(s)], keep_ref[4, :, cols(s)], keep_ref[5, :, cols(s)]
            gb = rest_ref[R_GB + s]
            d_b = dbb[:, SLAB * s:SLAB * (s + 1)]
            szc = zc * sg
            dgb_ = d_b * conv * szc
            dconv = d_b * gb * szc
            dzc = d_b * gb * conv * (sg * (1.0 + zc * (1.0 - sg)))
            sv_ref[3, :, cols(s)] += _part8(dconv * u2)
            sv_ref[4, :, cols(s)] += _part8(dconv * u1)
            sv_ref[5, :, cols(s)] += _part8(dconv * u)
            nxt = carry_ref[:, cols(s)]
            d1 = jnp.where(row == tm - 1, nxt[0:1], pltpu.roll(dconv, tm - 1, 0))
            d2 = jnp.where(row == tm - 1, nxt[1:2], jnp.where(row == tm - 2, nxt[0:1], pltpu.roll(dconv, tm - 2, 0)))
            carry_ref[:, cols(s)] = dconv[0:8]
            du = cw_ref[2:3, cols(s)] * dconv + cw_ref[1:2, cols(s)] * d1 + cw_ref[0:1, cols(s)] * d2
            dgc, dux = du * ux, du * gc
            for slab, val in ((R_GB + s, dgb_), (R_ZC + s, dzc), (R_GC + s, dgc), (R_UX + s, dux)):
                dp_ref[slab] = val.astype(BF16)
                gbr_ref[slab] += _part8(val)

    def tile(i):
        return nt - 1 - i

    row_blk = lambda i: (tile(i), 0)
    slab_blk = lambda i: (0, tile(i), 0)
    const2 = lambda i: (0, 0)
    const3 = lambda i: (0, 0, 0)
    in_specs = [
        pl.BlockSpec((N_REST, tm, SLAB), slab_blk),
        pl.BlockSpec((N_REST, 8, SLAB), lambda i: (0, jnp.maximum(tile(i) * (tm // 8) - 1, 0), 0)),
        pl.BlockSpec((2, tm, SLAB), slab_blk), pl.BlockSpec((2, tm, SLAB), slab_blk), pl.BlockSpec((2, tm, SLAB), slab_blk),
        pl.BlockSpec((tm, D), row_blk), pl.BlockSpec((tm, D), row_blk),
        pl.BlockSpec((1, 3, D), lambda i: (tile(i) // tps, 0, 0)),
        pl.BlockSpec((3, D), const2), pl.BlockSpec((1, D), const2), pl.BlockSpec((1, D), const2), pl.BlockSpec((1, D), const2),
        pl.BlockSpec((D, SLAB), const2), pl.BlockSpec((D, D), const2), pl.BlockSpec((D, D), const2)]
    bf_rows = lambda: jax.ShapeDtypeStruct((t, D), BF16)
    out_shape = (
        jax.ShapeDtypeStruct((DP_SLABS, t, SLAB), BF16), jax.ShapeDtypeStruct((t, D), F32),
        jax.ShapeDtypeStruct((t, SLAB), BF16), jax.ShapeDtypeStruct((2, t, SLAB), F32),
        bf_rows(), bf_rows(), bf_rows(), bf_rows(), jax.ShapeDtypeStruct((t, SLAB), BF16), bf_rows(),
        jax.ShapeDtypeStruct((N_REST, 8, SLAB), F32), jax.ShapeDtypeStruct((7, 8, D), F32),
        jax.ShapeDtypeStruct((nbat, 8, D), F32))
    out_specs = (
        pl.BlockSpec((N_REST, tm, SLAB), slab_blk), pl.BlockSpec((tm, D), row_blk),
        pl.BlockSpec((tm, SLAB), row_blk), pl.BlockSpec((2, tm, SLAB), slab_blk),
        pl.BlockSpec((tm, D), row_blk), pl.BlockSpec((tm, D), row_blk), pl.BlockSpec((tm, D), row_blk),
        pl.BlockSpec((tm, D), row_blk), pl.BlockSpec((tm, SLAB), row_blk), pl.BlockSpec((tm, D), row_blk),
        pl.BlockSpec((N_REST, 8, SLAB), const3), pl.BlockSpec((7, 8, D), const3),
        pl.BlockSpec((1, 8, D), lambda i: (tile(i) // tps, 0, 0)))
    return _pcall(body, grid=(nt,), out_shape=out_shape, in_specs=in_specs, out_specs=out_specs,
                  scratch_shapes=[pltpu.VMEM((8, D), F32), pltpu.VMEM((6, tm, D), F32)], name="mid",
                  compiler_params=_params(56, ("arbitrary",)))(
        rest, rest, *ols, x, tgt, ada, cw, b_out, ln_g, ln_b, w_pa_t, w_pb, w_out)


def _tn_matmul(lhs, rhs, lhs_spec, n_steps, out_rows, out_index, name, after):
    t, n = rhs.shape

    def body(l_ref, r_ref, after_ref, o_ref):
        del after_ref
        o_ref[...] = _tn(l_ref[0] if len(l_ref.shape) == 3 else l_ref[...], r_ref[...])

    return _pcall(body, grid=(n_steps,), out_shape=jax.ShapeDtypeStruct((out_rows, n), F32),
                  in_specs=[lhs_spec, pl.BlockSpec((t, n), lambda j: (0, 0)), ANY],
                  out_specs=pl.BlockSpec((SLAB, n), out_index), name=name,
                  compiler_params=_params(48, ("parallel",)))(lhs, rhs, after)


def _grad_rows_2d(lhs, rhs, name, after):
    t, k = lhs.shape
    return _tn_matmul(lhs, rhs, pl.BlockSpec((t, SLAB), lambda j: (0, j)), k // SLAB, k, lambda j: (j, 0), name, after)


def _w_row_block(j):
    return (j + N_QKV) % N_SLAB


def _dp_slab(j):
    return jnp.where(j < N_REST, j, j + 2)


def _grad_w_in_t(dproj, h):
    t = h.shape[0]
    return _tn_matmul(dproj, h, pl.BlockSpec((1, t, SLAB), lambda j: (_dp_slab(j), 0, 0)), N_SLAB, D_IN,
                      lambda j: (_w_row_block(j), 0), "grad_w_in", h)


def _grad_h(dproj, w_in_t, gx0, x, ada, after, tm=512):
    t = x.shape[0]
    nbat = ada.shape[0]
    tps = (t // nbat) // tm

    def body(dp_ref, w_ref, gx0_ref, x_ref, ada_ref, after_ref, gx_ref, dss_ref):
        del after_ref
        i = pl.program_id(0)
        dh = None
        for j in range(N_SLAB):
            slab = j if j < N_REST else j + 2
            part = _nn(dp_ref[slab], w_ref[pl.ds(SLAB * ((j + N_QKV) % N_SLAB), SLAB), :])
            dh = part if dh is None else dh + part
        gx_ref[...] = gx0_ref[...] + dh * (1.0 + ada_ref[0, 1:2, :])

        @pl.when((i % tps) == 0)
        def _():
            dss_ref[...] = jnp.zeros_like(dss_ref)

        dss_ref[0, 0] += _part8(dh)
        dss_ref[0, 1] += _part8(dh * x_ref[...])

    return _pcall(
        body, grid=(t // tm,),
        out_shape=(jax.ShapeDtypeStruct((t, D), F32), jax.ShapeDtypeStruct((nbat, 2, 8, D), F32)),
        in_specs=[pl.BlockSpec((DP_SLABS, tm, SLAB), lambda i: (0, i, 0)),
                  pl.BlockSpec((D_IN, D), lambda i: (0, 0), pipeline_mode=pl.Buffered(1)),
                  pl.BlockSpec((tm, D), lambda i: (i, 0)), pl.BlockSpec((tm, D), lambda i: (i, 0)),
                  pl.BlockSpec((1, 3, D), lambda i: (i // tps, 0, 0)), ANY],
        out_specs=(pl.BlockSpec((tm, D), lambda i: (i, 0)),
                   pl.BlockSpec((1, 2, 8, D), lambda i: (i // tps, 0, 0, 0))),
        name="grad_h", compiler_params=_params(60, ("arbitrary",)))(dproj, w_in_t, gx0, x, ada, after)


def _chip(m):
    x, y, _ = _my_position()
    return (x ^ ((m >> 1) & 1), y ^ (m & 1))


def _exchange_siblings(grads):
    n = len(grads)

    def body(*refs):
        copies = _sibling_copies(refs[:n], refs[n:2 * n], refs[2 * n], refs[2 * n + 1])
        for cp in copies:
            cp.start()
        for cp in copies:
            cp.wait()

    return _pcall(body, out_shape=tuple(_sibling_zones(grads)), in_specs=[ANY] * n, out_specs=(ANY,) * n,
                  name="exchange_siblings", scratch_shapes=[pltpu.SemaphoreType.DMA((4 * n,))] * 2)(*grads)


def _sibling_zones(grads):
    return [jax.ShapeDtypeStruct((4, g.shape[0] // N_DEV, g.shape[1]), g.dtype) for g in grads]


def _sibling_copies(srcs, lands, send_sems, recv_sems):
    x, y, c = _my_position()
    copies = []
    for a, (src, land) in enumerate(zip(srcs, lands)):
        rows = land.shape[1]
        for m in range(4):
            dev = _flat(*_chip(m), 1 - c)
            copies.append(pltpu.make_async_remote_copy(
                src_ref=src.at[pl.ds(pl.multiple_of(dev * rows, 8), rows), :], dst_ref=land.at[m],
                send_sem=send_sems.at[4 * a + m], recv_sem=recv_sems.at[4 * a + m], device_id=(x, y, 1 - c),
                device_id_type=MESH))
    return copies


def _chip_copies(srcs, lands, send_sems, recv_sems):
    _, _, c = _my_position()
    return [pltpu.make_async_remote_copy(
        src_ref=srcs[a].at[m - 1], dst_ref=lands[a].at[m - 1], send_sem=send_sems.at[3 * a + m - 1],
        recv_sem=recv_sems.at[3 * a + m - 1], device_id=(*_chip(m), c), device_id_type=MESH)
        for a in range(len(srcs)) for m in range(1, 4)]


HBM = pl.BlockSpec(memory_space=pltpu.HBM)
SEM = pl.BlockSpec(memory_space=pltpu.SEMAPHORE)
SPLIT_COPY = pltpu.CompilerParams(has_side_effects=pltpu.SideEffectType.DATAFLOW_SIDE_EFFECTING)


def _start_copies(make_copies, n_sems, srcs, zones, name):
    n = len(srcs)

    def body(*refs):
        for cp in make_copies(refs[:n], refs[n:2 * n], refs[2 * n], refs[2 * n + 1]):
            cp.start()
        refs[-1][...] = jnp.zeros_like(refs[-1])

    hbm = tuple(pltpu.HBM(b.shape, b.dtype) for b in list(srcs) + list(zones))
    out_shape = (pltpu.SemaphoreType.DMA((n_sems,)), pltpu.SemaphoreType.DMA((n_sems,))) + hbm + (jax.ShapeDtypeStruct((8, 128), F32),)
    operands = [pltpu.with_memory_space_constraint(b, pltpu.HBM) for b in srcs]
    operands += [pltpu.with_memory_space_constraint(lax.empty(z.shape, z.dtype), pltpu.HBM) for z in zones]
    res = _pcall(body, out_shape=out_shape, in_specs=[HBM] * (2 * n), out_specs=(SEM, SEM) + (HBM,) * (2 * n) + (VMEM,),
                 input_output_aliases={i: 2 + i for i in range(2 * n)}, name=name, compiler_params=SPLIT_COPY)(*operands)
    return (res[0], res[1], res[2:2 + n], res[2 + n:2 + 2 * n]), res[-1]


def _wait_copies(make_copies, flight, after, name):
    send_sems, recv_sems, srcs, zones = flight
    n = len(srcs)

    def body(*refs):
        for cp in make_copies(refs[:n], refs[n:2 * n], refs[2 * n], refs[2 * n + 1]):
            cp.wait_send()
            cp.wait_recv()

    hbm = tuple(pltpu.HBM(b.shape, b.dtype) for b in list(srcs) + list(zones))
    res = _pcall(body, out_shape=hbm, in_specs=[HBM] * (2 * n) + [SEM, SEM, ANY], out_specs=(HBM,) * (2 * n),
                 input_output_aliases={i: i for i in range(2 * n)}, name=name, compiler_params=SPLIT_COPY)(
        *srcs, *zones, send_sems, recv_sems, after)
    return res[:n], res[n:]


def _pair_sums(devs, grads, lands, n_steps, name):
    n = len(grads)
    rows = [l.shape[1] for l in lands]
    rbs = [r // n_steps for r in rows]

    def body(devs_ref, *refs):
        del devs_ref
        g_refs, land_refs, outs = refs[:4 * n], refs[4 * n:5 * n], refs[5 * n:]
        for a in range(n):
            outs[2 * a][...] = g_refs[4 * a][...] + land_refs[a][0]
            for m in range(1, 4):
                outs[2 * a + 1][m - 1] = (g_refs[4 * a + m][...] + land_refs[a][m]).astype(BF16)

    def block_of(m, per_dev):
        return lambda i, devs_ref: (devs_ref[m] * per_dev + i, 0)

    in_specs = [pl.BlockSpec((rb, l.shape[2]), block_of(m, n_steps)) for rb, l in zip(rbs, lands) for m in range(4)]
    in_specs += [pl.BlockSpec((4, rb, l.shape[2]), lambda i, devs_ref: (0, i, 0)) for rb, l in zip(rbs, lands)]
    out_shape, out_specs = [], []
    for rb, l in zip(rbs, lands):
        out_shape += [jax.ShapeDtypeStruct(l.shape[1:], F32), jax.ShapeDtypeStruct((3,) + l.shape[1:], BF16)]
        out_specs += [pl.BlockSpec((rb, l.shape[2]), lambda i, devs_ref: (i, 0)),
                      pl.BlockSpec((3, rb, l.shape[2]), lambda i, devs_ref: (0, i, 0))]
    grid_spec = pltpu.PrefetchScalarGridSpec(num_scalar_prefetch=1, grid=(n_steps,), in_specs=in_specs, out_specs=tuple(out_specs))
    res = _pcall(body, grid_spec=grid_spec, out_shape=tuple(out_shape), name=name,
                 compiler_params=_params(48, ("parallel",)))(devs, *[g for g in grads for _ in range(4)], *lands)
    return res[0::2], res[1::2]


def _final_sums(mine, lands, n_steps, name):
    n = len(mine)
    rbs = [o.shape[0] // n_steps for o in mine]

    def body(*refs):
        mine_refs, land_refs, outs = refs[:n], refs[n:2 * n], refs[2 * n:]
        for a in range(n):
            tot = mine_refs[a][...]
            for m in range(3):
                tot = tot + land_refs[a][m].astype(F32)
            outs[a][...] = tot

    in_specs = ([pl.BlockSpec((rb, o.shape[1]), lambda i: (i, 0)) for rb, o in zip(rbs, mine)]
                + [pl.BlockSpec((3, rb, o.shape[1]), lambda i: (0, i, 0)) for rb, o in zip(rbs, mine)])
    out_specs = tuple(pl.BlockSpec((rb, o.shape[1]), lambda i: (i, 0)) for rb, o in zip(rbs, mine))
    out_shape = tuple(jax.ShapeDtypeStruct(o.shape, F32) for o in mine)
    return _pcall(body, grid=(n_steps,), out_shape=out_shape, in_specs=in_specs, out_specs=out_specs, name=name,
                  compiler_params=_params(32, ("parallel",)))(*mine, *lands)


def _reduce_scatter_begin(big, small_after_start):
    flight, token = _start_copies(_sibling_copies, 4, [big], _sibling_zones([big]), "siblings_start")
    small = small_after_start(token)
    (big,), big_lands = _wait_copies(_sibling_copies, flight, small[-1], "siblings_wait")
    small_lands = _exchange_siblings(small)
    c = lax.axis_index("c")
    devs = jnp.stack([_flat(*_chip(m), c) for m in range(4)]).astype(jnp.int32)
    big_mine, big_send = _pair_sums(devs, [big], big_lands, 4, "pair_sums_w_in")
    small_mine, small_send = _pair_sums(devs, small, small_lands, 1, "pair_sums_rest")
    bufs = list(big_send) + list(small_send)
    flight, token = _start_copies(_chip_copies, 3 * len(bufs), bufs, bufs, "chips_start")
    return (flight, list(big_mine) + list(small_mine)), token


def _reduce_scatter_end(state, after):
    flight, mine = state
    _, got = _wait_copies(_chip_copies, flight, after, "chips_wait")
    big = _final_sums(mine[:1], got[:1], 4, "final_sums_w_in")
    small = _final_sums(mine[1:], got[1:], 1, "final_sums_rest")
    return list(big) + list(small)


def _adamw(w, g, m, v):
    m_new = B1 * m + (1.0 - B1) * g
    v_new = B2 * v + (1.0 - B2) * (g * g)
    m_hat = m_new / (1.0 - B1 ** STEP)
    v_hat = v_new / (1.0 - B2 ** STEP)
    delta = -LR * (m_hat / (jnp.sqrt(v_hat) + EPS) + WD * w)
    return delta, m_new, v_new


def _adam_rows(g, w, m, v, n_steps, name):
    rows, ncol = w.shape
    blk = pl.BlockSpec((rows // n_steps, ncol), lambda i: (i, 0))

    def body(g_ref, w_ref, m_ref, v_ref, d_ref, mo_ref, vo_ref):
        d_ref[...], mo_ref[...], vo_ref[...] = _adamw(w_ref[...], g_ref[...], m_ref[...], v_ref[...])

    shape = jax.ShapeDtypeStruct(w.shape, F32)
    return _pcall(body, grid=(n_steps,), out_shape=(shape,) * 3, in_specs=[blk] * 4, out_specs=(blk,) * 3, name=name,
                  compiler_params=_params(32, ("parallel",)))(g, w, m, v)


def _adam_transposed(g_t, w, m, v, name):
    n, k = g_t.shape
    rb = min(k, 128)

    def body(gt_ref, w_ref, m_ref, v_ref, g_ref, d_ref, mo_ref, vo_ref):
        for src, skip, dst, size in _column_chunks(n):
            sl = pl.ds(dst, size)
            g = gt_ref[pl.ds(src, 128), :].T[:, skip:]
            delta, m_new, v_new = _adamw(w_ref[:, sl], g, m_ref[:, sl], v_ref[:, sl])
            g_ref[:, sl], d_ref[:, sl], mo_ref[:, sl], vo_ref[:, sl] = g, delta, m_new, v_new

    shape = jax.ShapeDtypeStruct(w.shape, F32)
    rows = pl.BlockSpec((rb, n), lambda i: (i, 0))
    return _pcall(body, grid=(k // rb,), out_shape=(shape,) * 4,
                  in_specs=[pl.BlockSpec((n, rb), lambda i: (0, i)), rows, rows, rows], out_specs=(rows,) * 4, name=name,
                  compiler_params=_params(32, ("parallel",)))(g_t, w, m, v)


def _adam_many(items, name):
    n = len(items)

    def body(*refs):
        ins, outs = refs[:4 * n], refs[4 * n:]
        for a in range(n):
            w_ref, g_ref, m_ref, v_ref = ins[4 * a:4 * a + 4]
            delta, m_new, v_new = _adamw(w_ref[...], g_ref[...], m_ref[...], v_ref[...])
            outs[3 * a][...], outs[3 * a + 1][...], outs[3 * a + 2][...] = delta, m_new, v_new

    out_shape = tuple(jax.ShapeDtypeStruct(it[0].shape, F32) for it in items for _ in range(3))
    flat = [arr for it in items for arr in it]
    res = _pcall(body, grid=(1,), out_shape=out_shape, in_specs=[_whole(a) for a in flat],
                 out_specs=tuple(_whole(o) for o in out_shape), name=name, compiler_params=_params(32))(*flat)
    return [tuple(res[3 * a:3 * a + 3]) for a in range(n)]


def _adam_w_ada(cact_all, dada_mine, w, m, v):
    def body(c_ref, d_ref, w_ref, m_ref, v_ref, g_ref, dl_ref, mo_ref, vo_ref):
        g = _tn(c_ref[...].astype(BF16), d_ref[...].astype(BF16))
        delta, m_new, v_new = _adamw(w_ref[...], g, m_ref[...], v_ref[...])
        g_ref[...], dl_ref[...], mo_ref[...], vo_ref[...] = g, delta, m_new, v_new

    shape = jax.ShapeDtypeStruct(w.shape, F32)
    operands = (cact_all, dada_mine, w, m, v)
    return _pcall(body, grid=(1,), out_shape=(shape,) * 4, in_specs=[_whole(a) for a in operands],
                  out_specs=(_whole(w),) * 4, name="adam_w_ada", compiler_params=_params(32))(*operands)


def kernel(x, c, w_ada, b_ada, w_in, b_in, conv_w, w_proj_attn, w_proj_conv, w_out, b_out, ln_g, ln_b, loss_target, m_w_ada, m_b_ada, m_w_in, m_b_in, m_conv_w, m_w_proj_attn, m_w_proj_conv, m_w_out, m_b_out, m_ln_g, m_ln_b, v_w_ada, v_b_ada, v_w_in, v_b_in, v_conv_w, v_w_proj_attn, v_w_proj_conv, v_w_out, v_b_out, v_ln_g, v_ln_b):
    nbat, seq, _ = x.shape
    t = nbat * seq
    me = _flat(*_my_position())
    x2, tgt2 = x.reshape(t, D), loss_target.reshape(t, D)
    sq = lambda a: a.reshape(a.shape[1:])

    tr = lambda a: a[0].T
    w_in_rows = tr(w_in)
    w_in_t_s = _cast_rows(w_in_rows, 4, "cast_w_in")
    w_pa_t_s, w_pb_s, w_out_s, cact_s, cw_s = _prep(sq(w_proj_attn), sq(w_proj_conv), sq(w_out), c, sq(conv_w))
    cact_g, cw_g = _gather_rows([cact_s, cw_s])
    cact_all = cact_g.reshape(N_DEV, 8, D)[:, :nbat].reshape(N_DEV * nbat, D)
    cw = cw_g.reshape(N_DEV, 8, -1)[:, :3].transpose(1, 0, 2).reshape(3, D)

    ncol = w_ada.shape[2]
    b_ada_mine = lax.dynamic_slice(b_ada, (0, me * ncol), (1, ncol))
    ada_slots = _ada_forward(cact_all, sq(w_ada), b_ada_mine)
    ada_all = ada_slots.transpose(1, 0, 2).reshape(N_DEV * nbat, 3, D)
    ada = lax.dynamic_slice(ada_all, (me * nbat, 0, 0), (nbat, 3, D))

    h = _make_h(x2, ada)
    w_in_t, qkv, rest, (w_pa_t, w_pb, w_o) = _project_gather(w_in_t_s, h, b_in.reshape(N_SLAB, 1, SLAB), [w_pa_t_s, w_pb_s, w_out_s])
    ols = [_attn_forward(qkv, g, nbat) for g in range(3)]
    (dproj, gx0, do_attn, ol_tot, merged, do_f, bbs, dyc, a_bf, dya, gb_rest, svec, dgate) = _mid(
        rest, ols, x2, tgt2, ada, cw, b_out, ln_g, ln_b, w_pa_t, w_pb, w_o)

    gb_qkv = []
    for g in range(3):
        dproj, gb = _attn_backward(qkv, do_attn, ol_tot, dproj, g, nbat)
        gb_qkv.append(gb)
    g_w_in_t = _grad_w_in_t(dproj, h)

    def small_grads(token):
        g_w_out = _grad_rows_2d(merged, do_f, "grad_w_out", token)
        g_w_pb = _grad_rows_2d(bbs, dyc, "grad_w_proj_conv", g_w_out)
        g_w_pa_t = _grad_rows_2d(dya, a_bf, "grad_w_proj_attn", g_w_pb)
        return [g_w_out, g_w_pb, g_w_pa_t]

    rs_state, token = _reduce_scatter_begin(g_w_in_t, small_grads)
    grad_x, dss = _grad_h(dproj, w_in_t, gx0, x2, ada, token)

    rows8, tot, g_bada = _small_reduce(gb_rest, gb_qkv, svec, dgate, dss)
    g_in_t, g_out, g_pb, g_pa_t = _reduce_scatter_end(rs_state, tot)
    loss = tot[0, P_LOSS]
    dada_all = rows8[:, 0, P_DADA:].reshape(N_DEV * nbat, 3 * D)
    dada_mine = lax.dynamic_slice(dada_all, (0, me * ncol), (N_DEV * nbat, ncol))

    d_win_t, nm_win_t, nv_win_t = _adam_rows(g_in_t, w_in_rows, tr(m_w_in), tr(v_w_in), 8, "adam_w_in")
    g_win, d_win, nm_win, nv_win = g_in_t.T, d_win_t.T, nm_win_t.T, nv_win_t.T
    g_wpa, d_wpa, nm_wpa, nv_wpa = _adam_transposed(g_pa_t, sq(w_proj_attn), sq(m_w_proj_attn), sq(v_w_proj_attn), "adam_w_proj_attn")
    g_wada, d_wada, nm_wada, nv_wada = _adam_w_ada(cact_all, dada_mine, sq(w_ada), sq(m_w_ada), sq(v_w_ada))
    g_bin = tot[:, P_BIN:P_BIN + D_IN]
    g_bout = tot[:, P_BOUT:P_BOUT + D]
    g_lng = tot[:, P_LNG:P_LNG + D]
    g_lnb = tot[:, P_LNB:P_LNB + D]
    g_conv = lax.dynamic_slice(tot[:, P_CONV:P_CONV + 3 * D].reshape(3, D), (0, me * cw_s.shape[1]), (3, cw_s.shape[1]))
    upd = _adam_many([
        (sq(w_proj_conv), g_pb, sq(m_w_proj_conv), sq(v_w_proj_conv)),
        (sq(w_out), g_out, sq(m_w_out), sq(v_w_out)),
        (b_ada, g_bada, m_b_ada, v_b_ada), (b_in, g_bin, m_b_in, v_b_in), (sq(conv_w), g_conv, sq(m_conv_w), sq(v_conv_w)),
        (b_out, g_bout, m_b_out, v_b_out), (ln_g, g_lng, m_ln_g, v_ln_g), (ln_b, g_lnb, m_ln_b, v_ln_b)], "adam_rest")
    (d_wpb, nm_wpb, nv_wpb), (d_wout, nm_wout, nv_wout), (d_bada, nm_bada, nv_bada), (d_bin, nm_bin, nv_bin), \
        (d_conv, nm_conv, nv_conv), (d_bout, nm_bout, nv_bout), (d_lng, nm_lng, nv_lng), (d_lnb, nm_lnb, nv_lnb) = upd

    ex = lambda a: a.reshape((1,) + a.shape)
    grads = [ex(g_wada), g_bada, ex(g_win), g_bin, ex(g_conv), ex(g_wpa), ex(g_pb), ex(g_out), g_bout, g_lng, g_lnb]
    deltas = [ex(d_wada), d_bada, ex(d_win), d_bin, ex(d_conv), ex(d_wpa), ex(d_wpb), ex(d_wout), d_bout, d_lng, d_lnb]
    new_m = [ex(nm_wada), nm_bada, ex(nm_win), nm_bin, ex(nm_conv), ex(nm_wpa), ex(nm_wpb), ex(nm_wout), nm_bout, nm_lng, nm_lnb]
    new_v = [ex(nv_wada), nv_bada, ex(nv_win), nv_bin, ex(nv_conv), ex(nv_wpa), ex(nv_wpb), ex(nv_wout), nv_bout, nv_lng, nv_lnb]
    return (loss, grad_x.reshape(x.shape), *grads, *deltas, *new_m, *new_v)
```

```python
import functools

import jax
import jax.numpy as jnp
from jax import lax
from jax.experimental import pallas as pl
from jax.experimental.pallas import tpu as pltpu

F32, BF16 = jnp.float32, jnp.bfloat16
MESH = pl.DeviceIdType.MESH
N_DEV = 8
D = 1024
SLAB = 256
N_QKV, N_REST = 9, 25
N_SLAB = N_QKV + N_REST
D_IN = N_SLAB * SLAB
DP_SLABS = 36
BLK = 128
GROUPS = ((128, 1), (512, 4), (2048, 16))
ALPHA = 2.0 ** 0.25
LN_EPS = 1e-5
LR, B1, B2, EPS, WD, STEP = 0.001, 0.9, 0.999, 1e-08, 0.01, 10
R_ZA, R_UX, R_GB, R_GC, R_ZC, R_GA, R_GBM = 0, 1, 5, 9, 13, 17, 21
P_BIN, P_BOUT, P_LNG, P_LNB, P_CONV, P_LOSS, P_DADA = 0, 8704, 9728, 10752, 11776, 14848, 14976
MIB = 1024 * 1024


def _pcall(body, *, out_shape, out_specs=None, **kw):
    def pin_out(shape, spec):
        blocked = isinstance(shape, jax.ShapeDtypeStruct) and getattr(spec, "block_shape", None) is not None
        return pltpu.HBM(shape.shape, shape.dtype) if blocked else shape

    n_scalar = 0
    if out_specs is None:
        specs = kw["grid_spec"].out_specs
        n_scalar = kw["grid_spec"].num_scalar_prefetch
    else:
        kw["out_specs"] = specs = out_specs
    if isinstance(out_shape, (tuple, list)):
        out_shape = tuple(pin_out(s, p) for s, p in zip(out_shape, specs))
    else:
        out_shape = pin_out(out_shape, specs)
    call = pl.pallas_call(body, out_shape=out_shape, **kw)

    def run(*operands):
        def pin(o):
            is_data = jnp.issubdtype(o.dtype, jnp.floating) or jnp.issubdtype(o.dtype, jnp.integer)
            return pltpu.with_memory_space_constraint(o, pltpu.HBM) if is_data else o
        return call(*operands[:n_scalar], *[pin(o) for o in operands[n_scalar:]])

    return run

ANY = pl.BlockSpec(memory_space=pl.ANY)
VMEM = pl.BlockSpec(memory_space=pltpu.VMEM)


def _whole(a):
    return pl.BlockSpec(a.shape, lambda i: (0,) * len(a.shape))


def _params(vmem_mib=None, sem=None):
    kw = {}
    if vmem_mib is not None:
        kw["vmem_limit_bytes"] = vmem_mib * MIB
    if sem is not None:
        kw["dimension_semantics"] = sem
    return pltpu.CompilerParams(**kw)


def _nn(a, b):
    return jnp.dot(a, b, preferred_element_type=F32)


def _nt(a, b):
    return lax.dot_general(a, b, (((1,), (1,)), ((), ())), preferred_element_type=F32)


def _tn(a, b):
    return lax.dot_general(a, b, (((0,), (0,)), ((), ())), preferred_element_type=F32)


def _sigmoid(v):
    return 1.0 / (1.0 + jnp.exp(-v))


def _part8(v):
    return v.reshape(v.shape[0] // 8, 8, v.shape[1]).sum(axis=0)


def _my_position():
    return lax.axis_index("x"), lax.axis_index("y"), lax.axis_index("c")


def _flat(px, py, pc):
    return 4 * px + 2 * py + pc


def _peer(mask):
    x, y, c = _my_position()
    return (x ^ ((mask >> 2) & 1), y ^ ((mask >> 1) & 1), c ^ (mask & 1))


def _column_chunks(n):
    chunks = [(128 * a, 0, 128 * a, 128) for a in range(n // 128)]
    if n % 128:
        chunks.append((n - 128, 128 - n % 128, 128 * (n // 128), n % 128))
    return chunks


def _cast_rows(w, n_steps, name):
    rows, ncol = w.shape
    blk = pl.BlockSpec((rows // n_steps, ncol), lambda i: (i, 0))

    def body(w_ref, o_ref):
        o_ref[...] = w_ref[...].astype(BF16)

    return _pcall(body, grid=(n_steps,), out_shape=jax.ShapeDtypeStruct(w.shape, BF16), in_specs=[blk], out_specs=blk,
                  name=name, compiler_params=_params(16, ("parallel",)))(w)


def _prep(w_pa, w_pb, w_out, c, conv_w):
    def body(wpa_ref, wpb_ref, wout_ref, c_ref, cw_ref, wpat_ref, wpb_o, wout_o, cact_ref, cwp_ref):
        wpat_ref[...] = wpa_ref[...].T.astype(BF16)
        wpb_o[...] = wpb_ref[...].astype(BF16)
        wout_o[...] = wout_ref[...].astype(BF16)
        cv = c_ref[...]
        cact_ref[...] = jnp.zeros_like(cact_ref)
        cact_ref[pl.ds(0, cv.shape[0]), :] = cv * _sigmoid(cv)
        cwp_ref[...] = jnp.zeros_like(cwp_ref)
        cwp_ref[pl.ds(0, 3), :] = cw_ref[...]

    out_shape = (jax.ShapeDtypeStruct((w_pa.shape[1], w_pa.shape[0]), BF16),
                 jax.ShapeDtypeStruct(w_pb.shape, BF16), jax.ShapeDtypeStruct(w_out.shape, BF16),
                 jax.ShapeDtypeStruct((8, D), F32), jax.ShapeDtypeStruct((8, conv_w.shape[1]), F32))
    operands = (w_pa, w_pb, w_out, c, conv_w)
    return _pcall(body, grid=(1,), out_shape=out_shape, in_specs=[_whole(a) for a in operands],
                  out_specs=tuple(_whole(o) for o in out_shape), name="prep", compiler_params=_params(16))(*operands)


def _exchange_slots(out_refs, send_sems, recv_sems, base=0):
    me = _flat(*_my_position())

    def copy(a, mask, slot):
        return pltpu.make_async_remote_copy(
            src_ref=out_refs[a].at[slot], dst_ref=out_refs[a].at[slot], send_sem=send_sems.at[base + 7 * a + mask - 1],
            recv_sem=recv_sems.at[base + 7 * a + mask - 1], device_id=_peer(mask), device_id_type=MESH)

    pairs = [(a, mask) for a in range(len(out_refs)) for mask in range(1, N_DEV)]
    for a, mask in pairs:
        copy(a, mask, me).start()
    for a, mask in pairs:
        copy(a, mask, _flat(*_peer(mask))).wait_recv()
    for a, mask in pairs:
        copy(a, mask, me).wait_send()


def _ada_forward(cact_mine, cw_mine, w_ada, b_ada_mine):
    ncol = w_ada.shape[1]

    def body(c_ref, cw_ref, w_ref, b_ref, out_ref, call_ref, cwall_ref, send_sems, recv_sems):
        me = _flat(*_my_position())
        call_ref[me] = c_ref[...]
        cwall_ref[me] = cw_ref[...]
        _exchange_slots([call_ref, cwall_ref], send_sems, recv_sems)
        c_all = call_ref[...].reshape(N_DEV * 8, D).astype(BF16)
        out_ref[me] = (_nn(c_all, w_ref[...].astype(BF16)) + b_ref[...]).reshape(N_DEV, 8, ncol)
        _exchange_slots([out_ref], send_sems, recv_sems, base=14)

    operands = (cact_mine, cw_mine, w_ada, b_ada_mine)
    out_shape = (jax.ShapeDtypeStruct((N_DEV, N_DEV, 8, ncol), F32), jax.ShapeDtypeStruct((N_DEV, 8, D), F32),
                 jax.ShapeDtypeStruct((N_DEV,) + cw_mine.shape, F32))
    return _pcall(body, grid=(1,), out_shape=out_shape, in_specs=[_whole(a) for a in operands], out_specs=(VMEM,) * 3,
                  scratch_shapes=[pltpu.SemaphoreType.DMA((21,)), pltpu.SemaphoreType.DMA((21,))], name="ada_forward",
                  compiler_params=_params(16))(*operands)


def _small_reduce(gb_rest, gb_qkv, svec, dgate, dss):
    nbat = dgate.shape[0]

    def body(gbr_ref, q0_ref, q1_ref, q2_ref, sv_ref, dg_ref, dss_ref, rows_ref, tot_ref, gbada_ref, send_sems, recv_sems):
        me = _flat(*_my_position())

        def put(off, v):
            rows_ref[me, :, pl.ds(off, v.shape[1])] = v

        def row(v):
            return jnp.sum(v, axis=0, keepdims=True)

        for g, q_ref in enumerate((q0_ref, q1_ref, q2_ref)):
            for which in range(3):
                put(P_BIN + SLAB * (3 * which + g), row(q_ref[which]))
        for s in range(N_REST):
            put(P_BIN + SLAB * (N_QKV + s), row(gbr_ref[s]))
        put(P_LNG, row(sv_ref[0]))
        put(P_LNB, row(sv_ref[1]))
        put(P_BOUT, row(sv_ref[2]))
        for j in range(3):
            put(P_CONV + D * j, row(sv_ref[3 + j]))
        loss = (0.5 / D) * jnp.sum(row(sv_ref[6]), axis=1, keepdims=True)
        put(P_LOSS, jnp.broadcast_to(loss, (1, 128)))
        for b in range(nbat):
            put(P_DADA + 3 * D * b, row(dss_ref[b, 0]))
            put(P_DADA + 3 * D * b + D, row(dss_ref[b, 1]))
            put(P_DADA + 3 * D * b + 2 * D, row(dg_ref[b]))
        _exchange_slots([rows_ref], send_sems, recv_sems)
        tot = rows_ref[0]
        for k in range(1, N_DEV):
            tot = tot + rows_ref[k]
        tot_ref[...] = tot
        gbada = tot[:, P_DADA:P_DADA + 3 * D]
        for b in range(1, nbat):
            gbada = gbada + tot[:, P_DADA + 3 * D * b:P_DADA + 3 * D * (b + 1)]
        gbada_ref[...] = gbada

    p_len = P_DADA + nbat * 3 * D
    out_shape = (jax.ShapeDtypeStruct((N_DEV, 1, p_len), F32), jax.ShapeDtypeStruct((1, p_len), F32),
                 jax.ShapeDtypeStruct((1, 3 * D), F32))
    operands = (gb_rest, *gb_qkv, svec, dgate, dss)
    return _pcall(body, grid=(1,), out_shape=out_shape, in_specs=[_whole(a) for a in operands],
                  out_specs=(VMEM, _whole(out_shape[1]), _whole(out_shape[2])),
                  scratch_shapes=[pltpu.SemaphoreType.DMA((7,)), pltpu.SemaphoreType.DMA((7,))], name="small_reduce",
                  compiler_params=_params(16))(*operands)


def _make_h(x, ada, tm=512):
    t = x.shape[0]
    tps = (t // ada.shape[0]) // tm

    def body(x_ref, ada_ref, h_ref):
        h_ref[...] = (x_ref[...] * (1.0 + ada_ref[0, 1:2, :]) + ada_ref[0, 0:1, :]).astype(BF16)

    return _pcall(body, grid=(t // tm,), out_shape=jax.ShapeDtypeStruct((t, D), BF16),
                  in_specs=[pl.BlockSpec((tm, D), lambda i: (i, 0)), pl.BlockSpec((1, 3, D), lambda i: (i // tps, 0, 0))],
                  out_specs=pl.BlockSpec((tm, D), lambda i: (i, 0)), name="make_h",
                  compiler_params=_params(32, ("parallel",)))(x, ada)


PIECE = 64
N_CHUNK = 4
ARRIVAL_RANK = (0, 1, 3, 5, 2, 4, 6, 7)
SLOT_MASK = (1, 4, 2, 6, 5, 3, 7)


def _arrival_tables(shard_rows):
    import numpy as np
    crow = shard_rows // N_CHUNK
    table = np.zeros((N_DEV, N_SLAB + 7 * N_CHUNK), np.int32)
    lo = [(SLAB * j) // crow for j in range(N_SLAB)]
    hi = [(SLAB * j + SLAB - 1) // crow for j in range(N_SLAB)]
    for k in range(N_DEV):
        def rank(chunk):
            shard_rank = ARRIVAL_RANK[(chunk // N_CHUNK) ^ k]
            return shard_rank if shard_rank < 2 else 2 + 8 * (chunk % N_CHUNK) + shard_rank
        order = sorted(range(N_SLAB), key=lambda j: (max(rank(lo[j]), rank(hi[j])), j))
        table[k, :N_SLAB] = order
        for slot, mask in enumerate(SLOT_MASK):
            for ch in range(N_CHUNK):
                chunk = (k ^ mask) * N_CHUNK + ch
                table[k, N_SLAB + slot * N_CHUNK + ch] = min(t for t, j in enumerate(order) if lo[j] <= chunk <= hi[j])
    return table


def _project_gather(shard, h, b_in3, others):
    t = h.shape[0]
    n_o = len(others)
    srows = shard.shape[0]
    crow = srows // N_CHUNK
    shards = [shard] + list(others)
    table = jnp.asarray(_arrival_tables(srows))

    def body(tbl_ref, *refs):
        srcs = [refs[0]] + list(refs[3:3 + n_o])
        h_ref, b_ref = refs[1], refs[2]
        outs = [refs[3 + n_o]] + list(refs[6 + n_o:6 + 2 * n_o])
        qkv_ref, rest_ref = refs[4 + n_o], refs[5 + n_o]
        (wtile, obf, of32, send_sems, recv_sems, local_sems, tile_sems, obf_sems, of32_sems) = refs[6 + 2 * n_o:]
        w_full = outs[0]
        x, y, c = _my_position()
        k = _flat(x, y, c)
        me, sibling = (x, y, c), (x, y, 1 - c)
        chips = [(1 - x, y), (x, 1 - y), (1 - x, 1 - y)]

        def rows(a, px, py, pc, ch):
            r = shards[a].shape[0]
            if ch is None:
                return outs[a].at[pl.ds(pl.multiple_of(_flat(px, py, pc) * r, r), r), :]
            return outs[a].at[pl.ds(pl.multiple_of(_flat(px, py, pc) * r + ch * crow, crow), crow), :]

        def copy(a, slot, block, to, ch=None, src=None):
            sem = slot * N_CHUNK + ch if a == 0 else 7 * (N_CHUNK - 1 + a) + slot
            if src is not None and ch is not None:
                src = src.at[pl.ds(ch * crow, crow), :]
            return pltpu.make_async_remote_copy(
                src_ref=rows(a, *block, ch) if src is None else src, dst_ref=rows(a, *block, ch),
                send_sem=send_sems.at[sem], recv_sem=recv_sems.at[sem], device_id=to, device_id_type=MESH)

        mine = [pltpu.make_async_copy(srcs[a], rows(a, *me, None), local_sems.at[a]) for a in range(1 + n_o)]
        first = []
        for ch in range(N_CHUNK):
            first.append(copy(0, 0, me, sibling, ch, src=srcs[0]))
            first += [copy(0, 1 + j, me, (*chip, c), ch, src=srcs[0]) for j, chip in enumerate(chips)]
            if ch == 0:
                for a in range(1, 1 + n_o):
                    first.append(copy(a, 0, me, sibling, src=srcs[a]))
                    first += [copy(a, 1 + j, me, (*chip, c), src=srcs[a]) for j, chip in enumerate(chips)]
        for cp in mine + first:
            cp.start()

        def arrive(a, slot, ch=None):
            if slot == 0:
                copy(a, 0, sibling, me, ch).wait_recv()
            elif slot < 4:
                copy(a, slot, (*chips[slot - 1], c), me, ch).wait_recv()
                copy(a, slot + 3, (*chips[slot - 1], c), sibling, ch).start()
            else:
                copy(a, slot, (*chips[slot - 4], 1 - c), me, ch).wait_recv()

        def arrive_for(step):
            for slot in range(7):
                for ch in range(N_CHUNK):
                    @pl.when(tbl_ref[k, N_SLAB + slot * N_CHUNK + ch] == step)
                    def _():
                        arrive(0, slot, ch)

        def fetch(step, buf):
            slab = tbl_ref[k, step]
            for p in range(SLAB // PIECE):
                g0 = slab * SLAB + PIECE * p
                own = (g0 >= k * srows) & (g0 < (k + 1) * srows)
                dst = wtile.at[buf, pl.ds(PIECE * p, PIECE), :]

                @pl.when(own)
                def _():
                    pltpu.make_async_copy(srcs[0].at[pl.ds(pl.multiple_of(g0 - k * srows, PIECE), PIECE), :], dst, tile_sems.at[buf]).start()

                @pl.when(jnp.logical_not(own))
                def _():
                    pltpu.make_async_copy(w_full.at[pl.ds(pl.multiple_of(g0, PIECE), PIECE), :], dst, tile_sems.at[buf]).start()

        def wait_tile(buf):
            pltpu.make_async_copy(w_full.at[pl.ds(0, SLAB), :], wtile.at[buf], tile_sems.at[buf]).wait()

        def put(buf_ref, sems, dst_ref, count, value):
            b = count % 2

            @pl.when(count >= 2)
            def _():
                pltpu.make_async_copy(buf_ref.at[b], dst_ref, sems.at[b]).wait()

            buf_ref[b] = value
            pltpu.make_async_copy(buf_ref.at[b], dst_ref, sems.at[b]).start()

        def drain(buf_ref, sems, dst_ref, count):
            for back in (1, 2):
                @pl.when(count >= back)
                def _():
                    pltpu.make_async_copy(buf_ref.at[(count - back) % 2], dst_ref, sems.at[(count - back) % 2]).wait()

        arrive_for(0)
        fetch(0, 0)

        def step(s, carry):
            n_bf, n_f32 = carry
            buf = s % 2

            @pl.when(s + 1 < N_SLAB)
            def _():
                arrive_for(s + 1)
                fetch(s + 1, 1 - buf)

            wait_tile(buf)
            slab = tbl_ref[k, s]
            v = _nt(h_ref[...], wtile[buf]) + b_ref[slab]
            is_qkv = slab < N_QKV

            @pl.when(is_qkv)
            def _():
                put(obf, obf_sems, qkv_ref.at[jnp.minimum(slab, N_QKV - 1)], n_bf, v.astype(BF16))

            @pl.when(jnp.logical_not(is_qkv))
            def _():
                put(of32, of32_sems, rest_ref.at[jnp.maximum(slab - N_QKV, 0)], n_f32, v)

            return n_bf + is_qkv.astype(jnp.int32), n_f32 + 1 - is_qkv.astype(jnp.int32)

        n_bf, n_f32 = lax.fori_loop(0, N_SLAB, step, (jnp.int32(0), jnp.int32(0)))
        drain(obf, obf_sems, qkv_ref.at[0], n_bf)
        drain(of32, of32_sems, rest_ref.at[0], n_f32)

        for slots in ((1, 2, 3), (0, 4, 5, 6)):
            for a in range(1, 1 + n_o):
                for slot in slots:
                    arrive(a, slot)
        for cp in first:
            cp.wait_send()
        for j, chip in enumerate(chips):
            for ch in range(N_CHUNK):
                copy(0, 4 + j, (*chip, c), sibling, ch).wait_send()
            for a in range(1, 1 + n_o):
                copy(a, 4 + j, (*chip, c), sibling).wait_send()
        for cp in mine:
            cp.wait()

    out_shape = ((jax.ShapeDtypeStruct((N_DEV * srows, D), BF16), jax.ShapeDtypeStruct((N_QKV, t, SLAB), BF16),
                  jax.ShapeDtypeStruct((N_REST, t, SLAB), F32))
                 + tuple(jax.ShapeDtypeStruct((N_DEV * o.shape[0], o.shape[1]), o.dtype) for o in others))
    n_all = 1 + n_o
    n_sems = 7 * (N_CHUNK + n_o)
    grid_spec = pltpu.PrefetchScalarGridSpec(
        num_scalar_prefetch=1, grid=(1,),
        in_specs=[ANY, pl.BlockSpec((t, D), lambda i, tbl: (0, 0), pipeline_mode=pl.Buffered(1)),
                  pl.BlockSpec((N_SLAB, 1, SLAB), lambda i, tbl: (0, 0, 0))] + [ANY] * n_o,
        out_specs=(ANY,) * (3 + n_o),
        scratch_shapes=[pltpu.VMEM((2, SLAB, D), BF16), pltpu.VMEM((2, t, SLAB), BF16), pltpu.VMEM((2, t, SLAB), F32),
                        pltpu.SemaphoreType.DMA((n_sems,)), pltpu.SemaphoreType.DMA((n_sems,)),
                        pltpu.SemaphoreType.DMA((n_all,)), pltpu.SemaphoreType.DMA((2,)), pltpu.SemaphoreType.DMA((2,)),
                        pltpu.SemaphoreType.DMA((2,))])
    res = _pcall(body, grid_spec=grid_spec, out_shape=out_shape, name="project_gather",
                 compiler_params=_params(48, ("arbitrary",)))(table, shard, h, b_in3, *others)
    return res[0], res[1], res[2], list(res[3:])


def _bias_tables(g):
    window, dil = GROUPS[g]
    span = window // dil
    qi = jnp.arange(BLK)[:, None]
    kj = jnp.arange(2 * BLK)[None, :]
    delta = qi + BLK - kj
    valid = (delta >= 0) & (delta <= span)
    heads = jnp.arange(4, dtype=F32) + 4.0 * g
    slopes = 2.0 ** (-8.0 * (heads + 1.0) / 12.0)
    bias = -slopes[:, None, None] * (delta * dil).astype(F32)[None]
    return jnp.where(valid[None], bias, -1e30).reshape(4 * BLK, 2 * BLK)


def _head_masks(shape):
    lane = lax.broadcasted_iota(jnp.int32, shape, 1)
    return [(lane >= 64 * h) & (lane < 64 * (h + 1)) for h in range(4)]


def _stack_heads(v, masks):
    return jnp.concatenate([jnp.where(masks[h], v, jnp.zeros_like(v)) for h in range(4)], axis=0)


def _unstack_heads(v4, masks):
    out = jnp.where(masks[0], v4[0:BLK], 0.0)
    for h in range(1, 4):
        out = jnp.where(masks[h], v4[BLK * h:BLK * (h + 1)], out)
    return out


def _regroup(load_half, dst_ref, stage_ref, n, dil):
    for hlf in range(2):
        stage_ref[hlf] = load_half(hlf)

    def residue(r, carry):
        for hlf in range(2):
            dst_ref[pl.ds(pl.multiple_of(r * n, BLK), n), pl.ds(128 * hlf, 128)] = (
                stage_ref[hlf, pl.ds(r, n, stride=dil), :].astype(dst_ref.dtype))
        return carry

    lax.fori_loop(0, dil, residue, 0)


def _store_block(nat_ref, r, i, val, dil):
    for hlf in range(2):
        nat_ref[hlf, pl.ds(r + dil * BLK * i, BLK, stride=dil), :] = val[:, 128 * hlf:128 * (hlf + 1)]


def _for_blocks(block, dil, nblk):
    if dil == 1:
        block(0, 0, True)
        block(0, 1, False)

        def pair(k, carry):
            block(0, 2 * k, False)
            block(0, 2 * k + 1, False)
            return carry

        lax.fori_loop(1, nblk // 2, pair, 0)
    else:
        def residues(k, carry):
            block(2 * k, 0, True)
            block(2 * k + 1, 0, True)
            if nblk > 1:
                def loop(i, c):
                    block(2 * k, i, False)
                    block(2 * k + 1, i, False)
                    return c
                lax.fori_loop(1, nblk, loop, 0)
            return carry

        lax.fori_loop(0, dil // 2, residues, 0)


def _attn_forward(qkv, g, nbat):
    t = qkv.shape[1]
    seq = t // nbat
    dil = GROUPS[g][1]
    n = seq // dil
    nblk = n // BLK
    qkv4 = qkv.reshape(3, 3, t, SLAB)

    def body(qkv_ref, bias_ref, ol_ref, *scratch):
        masks = _head_masks((BLK, SLAB))
        if dil > 1:
            stage, qd, kd, vd, nat_o, nat_l = scratch
            for which, dst in enumerate((qd, kd, vd)):
                _regroup(lambda hlf, which=which: qkv_ref[which, 0, :, pl.ds(128 * hlf, 128)].astype(F32), dst, stage, n, dil)
        else:
            qd, kd, vd = qkv_ref.at[0, 0], qkv_ref.at[1, 0], qkv_ref.at[2, 0]

        def block(r, i, first):
            base = r * n
            qs = pl.ds(pl.multiple_of(base + i * BLK, BLK), BLK)
            ks = pl.ds(pl.multiple_of(base, BLK), BLK) if first else pl.ds(pl.multiple_of(base + (i - 1) * BLK, BLK), 2 * BLK)
            q, kk, vv = qd[qs, :], kd[ks, :], vd[ks, :]
            bias = bias_ref[:, pl.ds(BLK, BLK)] if first else bias_ref[...]
            s = _nt(_stack_heads(q, masks), kk) * 0.125 + bias
            m = jnp.max(s, axis=1, keepdims=True)
            p = jnp.exp(s - m)
            den = jnp.sum(p, axis=1, keepdims=True)
            out = _unstack_heads(_nn((p * (1.0 / den)).astype(BF16), vv), masks)
            lse = _unstack_heads(jnp.broadcast_to(m + jnp.log(den), (4 * BLK, SLAB)), masks)
            if dil > 1:
                _store_block(nat_o, r, i, out, dil)
                _store_block(nat_l, r, i, lse, dil)
            else:
                ol_ref[0, qs, :] = out
                ol_ref[1, qs, :] = lse

        _for_blocks(block, dil, nblk)
        if dil > 1:
            for hlf in range(2):
                ol_ref[0, :, pl.ds(128 * hlf, 128)] = nat_o[hlf]
                ol_ref[1, :, pl.ds(128 * hlf, 128)] = nat_l[hlf]

    scratch = []
    if dil > 1:
        scratch = [pltpu.VMEM((2, seq, 128), F32)] + [pltpu.VMEM((seq, SLAB), BF16)] * 3 + [pltpu.VMEM((2, seq, 128), F32)] * 2
    return _pcall(
        body, grid=(nbat,), out_shape=jax.ShapeDtypeStruct((2, t, SLAB), F32),
        in_specs=[pl.BlockSpec((3, 1, seq, SLAB), lambda b: (0, g, b, 0)),
                  pl.BlockSpec((4 * BLK, 2 * BLK), lambda b: (0, 0))],
        out_specs=pl.BlockSpec((2, seq, SLAB), lambda b: (0, b, 0)), scratch_shapes=scratch,
        name=f"attn_forward_{g}", compiler_params=_params(40, ("parallel",)))(qkv4, _bias_tables(g))


def _attn_backward(qkv, do_attn, ol_tot, dproj, g, nbat):
    t = qkv.shape[1]
    seq = t // nbat
    dil = GROUPS[g][1]
    n = seq // dil
    nblk = n // BLK
    qkv4 = qkv.reshape(3, 3, t, SLAB)
    dp4 = dproj.reshape(DP_SLABS // 3, 3, t, SLAB)

    def body(qkv_ref, do_ref, ol_ref, bias_ref, dp_in, dp_ref, gb_ref, dk_acc, dv_acc, *scratch):
        del dp_in
        masks = _head_masks((BLK, SLAB))

        @pl.when(pl.program_id(0) == 0)
        def _():
            gb_ref[...] = jnp.zeros_like(gb_ref)

        dk_acc[...] = jnp.zeros_like(dk_acc)
        dv_acc[...] = jnp.zeros_like(dv_acc)
        if dil > 1:
            stage, qd, kd, vd, dod, prodd, lsed, nat = scratch
            lanes = lambda hlf: pl.ds(128 * hlf, 128)
            for which, dst in enumerate((qd, kd, vd)):
                _regroup(lambda hlf, which=which: qkv_ref[which, 0, :, lanes(hlf)].astype(F32), dst, stage, n, dil)
            _regroup(lambda hlf: do_ref[:, lanes(hlf)].astype(F32), dod, stage, n, dil)
            _regroup(lambda hlf: do_ref[:, lanes(hlf)].astype(F32) * ol_ref[0, :, lanes(hlf)], prodd, stage, n, dil)
            _regroup(lambda hlf: ol_ref[1, :, lanes(hlf)], lsed, stage, n, dil)
        else:
            qd, kd, vd = qkv_ref.at[0, 0], qkv_ref.at[1, 0], qkv_ref.at[2, 0]

        def block(r, i, first):
            base = r * n
            qs = pl.ds(pl.multiple_of(base + i * BLK, BLK), BLK)
            ks = pl.ds(pl.multiple_of(base, BLK), BLK) if first else pl.ds(pl.multiple_of(base + (i - 1) * BLK, BLK), 2 * BLK)
            q, kk, vv = qd[qs, :], kd[ks, :], vd[ks, :]
            if dil > 1:
                do, prod, lse = dod[qs, :], prodd[qs, :], lsed[qs, :]
            else:
                do = do_ref[qs, :]
                prod = do.astype(F32) * ol_ref[0, qs, :]
                lse = ol_ref[1, qs, :]
            q4, do4 = _stack_heads(q, masks), _stack_heads(do, masks)
            bias = bias_ref[:, pl.ds(BLK, BLK)] if first else bias_ref[...]
            lse4 = jnp.concatenate([lse[:, 64 * h:64 * h + 1] for h in range(4)], axis=0)
            delta4 = jnp.concatenate([jnp.sum(jnp.where(masks[h], prod, 0.0), axis=1, keepdims=True) for h in range(4)], axis=0)
            p = jnp.exp(_nt(q4, kk) * 0.125 + bias - lse4)
            ds = (p * (_nt(do4, vv) - delta4)).astype(BF16)
            dv_acc[ks, :] += _tn(p.astype(BF16), do4)
            dk_acc[ks, :] += _tn(ds, q4) * 0.125
            dq = _unstack_heads(_nn(ds, kk), masks) * 0.125
            if dil > 1:
                _store_block(nat, r, i, dq, dil)
            else:
                dp_ref[0, 0, qs, :] = dq.astype(BF16)
            gb_ref[0] += _part8(dq)

        _for_blocks(block, dil, nblk)
        gb_ref[1] += _part8(dk_acc[...])
        gb_ref[2] += _part8(dv_acc[...])
        if dil > 1:
            def flush(which):
                for hlf in range(2):
                    dp_ref[which, 0, :, pl.ds(128 * hlf, 128)] = nat[hlf].astype(BF16)

            def to_token_order(acc_ref):
                def residue(r, carry):
                    for hlf in range(2):
                        nat[hlf, pl.ds(r, n, stride=dil), :] = acc_ref[pl.ds(pl.multiple_of(r * n, BLK), n), pl.ds(128 * hlf, 128)]
                    return carry
                lax.fori_loop(0, dil, residue, 0)

            flush(0)
            to_token_order(dk_acc)
            flush(1)
            to_token_order(dv_acc)
            flush(2)
        else:
            dp_ref[1, 0] = dk_acc[...].astype(BF16)
            dp_ref[2, 0] = dv_acc[...].astype(BF16)

    scratch = [pltpu.VMEM((seq, SLAB), F32)] * 2
    if dil > 1:
        scratch += ([pltpu.VMEM((2, seq, 128), F32)] + [pltpu.VMEM((seq, SLAB), BF16)] * 4 + [pltpu.VMEM((seq, SLAB), F32)] * 2
                    + [pltpu.VMEM((2, seq, 128), F32)])
    dp, gb = _pcall(
        body, grid=(nbat,),
        out_shape=(jax.ShapeDtypeStruct(dp4.shape, BF16), jax.ShapeDtypeStruct((3, 8, SLAB), F32)),
        in_specs=[pl.BlockSpec((3, 1, seq, SLAB), lambda b: (0, g, b, 0)),
                  pl.BlockSpec((seq, SLAB), lambda b: (b, 0)),
                  pl.BlockSpec((2, seq, SLAB), lambda b: (0, b, 0)),
                  pl.BlockSpec((4 * BLK, 2 * BLK), lambda b: (0, 0)), ANY],
        out_specs=(pl.BlockSpec((3, 1, seq, SLAB), lambda b: (DP_SLABS // 9 - 1, g, b, 0)),
                   pl.BlockSpec((3, 8, SLAB), lambda b: (0, 0, 0))),
        scratch_shapes=scratch, input_output_aliases={4: 0}, name=f"attn_backward_{g}",
        compiler_params=_params(48, ("arbitrary",)))(qkv4, do_attn, ol_tot, _bias_tables(g), dp4)
    return dp.reshape(DP_SLABS, t, SLAB), gb


def _mid(rest, ols, x, tgt, ada, cw, b_out, ln_g, ln_b, w_pa_t, w_pb, w_out, tm=256):
    t = x.shape[0]
    nbat = ada.shape[0]
    nt = t // tm
    tps = nt // nbat

    def body(rest_ref, halo_ref, ol0_ref, ol1_ref, ol2_ref, x_ref, t_ref, ada_ref, cw_ref, bout_ref, lng_ref, lnb_ref,
             wpat_ref, wpb_ref, wout_ref,
             dp_ref, gx0_ref, doa_ref, olt_ref, mg_ref, dof_ref, bbs_ref, dyc_ref, a_ref, dya_ref,
             gbr_ref, sv_ref, dgate_ref, carry_ref, keep_ref):
        i = pl.program_id(0)
        ti = nt - 1 - i
        pos = ti % tps

        @pl.when(i == 0)
        def _():
            gbr_ref[...] = jnp.zeros_like(gbr_ref)
            sv_ref[...] = jnp.zeros_like(sv_ref)

        @pl.when(pos == tps - 1)
        def _():
            dgate_ref[...] = jnp.zeros_like(dgate_ref)
            carry_ref[...] = jnp.zeros_like(carry_ref)

        row = lax.broadcasted_iota(jnp.int32, (tm, SLAB), 0)
        halo_on = (pos > 0).astype(F32)

        def cols(s):
            return pl.ds(SLAB * s, SLAB)

        l0, l1, l2 = ol0_ref[1], ol1_ref[1], ol2_ref[1]
        mx = jnp.maximum(jnp.maximum(l0, l1), l2)
        e0, e1, e2 = jnp.exp(l0 - mx), jnp.exp(l1 - mx), jnp.exp(l2 - mx)
        den = e0 + e1 + e2
        o_attn = (e0 * ol0_ref[0] + e1 * ol1_ref[0] + e2 * ol2_ref[0]) * (1.0 / den)
        olt_ref[0] = o_attn
        olt_ref[1] = mx + jnp.log(den)
        z_a = rest_ref[R_ZA]
        sg_za = _sigmoid(z_a)
        a_ref[...] = (o_attn * z_a * sg_za).astype(BF16)
        y_attn = _nt(a_ref[...], wpat_ref[...])

        for s in range(4):
            u = rest_ref[R_GC + s] * rest_ref[R_UX + s]
            hu = halo_ref[R_GC + s] * halo_ref[R_UX + s] * halo_on
            u1 = jnp.where(row == 0, hu[7:8], pltpu.roll(u, 1, 0))
            u2 = jnp.where(row == 0, hu[6:7], jnp.where(row == 1, hu[7:8], pltpu.roll(u, 2, 0)))
            conv = cw_ref[0:1, cols(s)] * u2 + cw_ref[1:2, cols(s)] * u1 + cw_ref[2:3, cols(s)] * u
            zc = rest_ref[R_ZC + s]
            sg = _sigmoid(zc)
            keep_ref[2, :, cols(s)], keep_ref[3, :, cols(s)], keep_ref[4, :, cols(s)], keep_ref[5, :, cols(s)] = u1, u2, conv, sg
            bbs_ref[:, cols(s)] = (rest_ref[R_GB + s] * conv * (zc * sg)).astype(BF16)
        y_conv = _nn(bbs_ref[...], wpb_ref[...])

        for s in range(4):
            s_a, s_b = _sigmoid(rest_ref[R_GA + s]), _sigmoid(rest_ref[R_GBM + s])
            keep_ref[0, :, cols(s)], keep_ref[1, :, cols(s)] = s_a, s_b
            mg_ref[:, cols(s)] = (s_a * y_attn[:, SLAB * s:SLAB * (s + 1)] + s_b * y_conv[:, SLAB * s:SLAB * (s + 1)]).astype(BF16)
        o = _nn(mg_ref[...], wout_ref[...]) + bout_ref[...]
        gate = ada_ref[0, 2:3, :]
        r = ALPHA * x_ref[...] + gate * o
        mu = jnp.mean(r, axis=1, keepdims=True)
        rc = r - mu
        rstd = lax.rsqrt(jnp.mean(rc * rc, axis=1, keepdims=True) + LN_EPS)
        xhat = rc * rstd
        err = xhat * lng_ref[...] + lnb_ref[...] - t_ref[...]
        sv_ref[6] += _part8(err * err)
        dy = err * (1.0 / D)
        sv_ref[0] += _part8(dy * xhat)
        sv_ref[1] += _part8(dy)
        dxh = dy * lng_ref[...]
        dr = rstd * (dxh - jnp.mean(dxh, axis=1, keepdims=True) - xhat * jnp.mean(dxh * xhat, axis=1, keepdims=True))
        gx0_ref[...] = ALPHA * dr
        dgate_ref[0] += _part8(dr * o)
        do_ = dr * gate
        sv_ref[2] += _part8(do_)
        dof_ref[...] = do_.astype(BF16)
        dmerged = _nt(dof_ref[...], wout_ref[...])
        for s in range(4):
            s_a, s_b = keep_ref[0, :, cols(s)], keep_ref[1, :, cols(s)]
            dm = dmerged[:, SLAB * s:SLAB * (s + 1)]
            ya, yc = y_attn[:, SLAB * s:SLAB * (s + 1)], y_conv[:, SLAB * s:SLAB * (s + 1)]
            dya_ref[:, cols(s)] = (dm * s_a).astype(BF16)
            dyc_ref[:, cols(s)] = (dm * s_b).astype(BF16)
            dga = dm * ya * s_a * (1.0 - s_a)
            dgb = dm * yc * s_b * (1.0 - s_b)
            dp_ref[R_GA + s] = dga.astype(BF16)
            dp_ref[R_GBM + s] = dgb.astype(BF16)
            gbr_ref[R_GA + s] += _part8(dga)
            gbr_ref[R_GBM + s] += _part8(dgb)

        da = _nn(dya_ref[...], wpat_ref[...])
        doa_ref[...] = (da * z_a * sg_za).astype(BF16)
        dza = da * o_attn * (sg_za * (1.0 + z_a * (1.0 - sg_za)))
        dp_ref[R_ZA] = dza.astype(BF16)
        gbr_ref[R_ZA] += _part8(dza)

        dbb = _nt(dyc_ref[...], wpb_ref[...])
        for s in range(4):
            ux, gc, zc = rest_ref[R_UX + s], rest_ref[R_GC + s], rest_ref[R_ZC + s]
            u = gc * ux
            u1, u2, conv, sg = keep_ref[2, :, cols(s)], keep_ref[3, :, cols(s)], keep_ref[4, :, cols(s)], keep_ref[5, :, cols(s)]
            gb = rest_ref[R_GB + s]
            d_b = dbb[:, SLAB * s:SLAB * (s + 1)]
            szc = zc * sg
            dgb_ = d_b * conv * szc
            dconv = d_b * gb * szc
            dzc = d_b * gb * conv * (sg * (1.0 + zc * (1.0 - sg)))
            sv_ref[3, :, cols(s)] += _part8(dconv * u2)
            sv_ref[4, :, cols(s)] += _part8(dconv * u1)
            sv_ref[5, :, cols(s)] += _part8(dconv * u)
            nxt = carry_ref[:, cols(s)]
            d1 = jnp.where(row == tm - 1, nxt[0:1], pltpu.roll(dconv, tm - 1, 0))
            d2 = jnp.where(row == tm - 1, nxt[1:2], jnp.where(row == tm - 2, nxt[0:1], pltpu.roll(dconv, tm - 2, 0)))
            carry_ref[:, cols(s)] = dconv[0:8]
            du = cw_ref[2:3, cols(s)] * dconv + cw_ref[1:2, cols(s)] * d1 + cw_ref[0:1, cols(s)] * d2
            dgc, dux = du * ux, du * gc
            for slab, val in ((R_GB + s, dgb_), (R_ZC + s, dzc), (R_GC + s, dgc), (R_UX + s, dux)):
                dp_ref[slab] = val.astype(BF16)
                gbr_ref[slab] += _part8(val)

    def tile(i):
        return nt - 1 - i

    row_blk = lambda i: (tile(i), 0)
    slab_blk = lambda i: (0, tile(i), 0)
    const2 = lambda i: (0, 0)
    const3 = lambda i: (0, 0, 0)
    in_specs = [
        pl.BlockSpec((N_REST, tm, SLAB), slab_blk),
        pl.BlockSpec((N_REST, 8, SLAB), lambda i: (0, jnp.maximum(tile(i) * (tm // 8) - 1, 0), 0)),
        pl.BlockSpec((2, tm, SLAB), slab_blk), pl.BlockSpec((2, tm, SLAB), slab_blk), pl.BlockSpec((2, tm, SLAB), slab_blk),
        pl.BlockSpec((tm, D), row_blk), pl.BlockSpec((tm, D), row_blk),
        pl.BlockSpec((1, 3, D), lambda i: (tile(i) // tps, 0, 0)),
        pl.BlockSpec((3, D), const2), pl.BlockSpec((1, D), const2), pl.BlockSpec((1, D), const2), pl.BlockSpec((1, D), const2),
        pl.BlockSpec((D, SLAB), const2), pl.BlockSpec((D, D), const2), pl.BlockSpec((D, D), const2)]
    bf_rows = lambda: jax.ShapeDtypeStruct((t, D), BF16)
    out_shape = (
        jax.ShapeDtypeStruct((DP_SLABS, t, SLAB), BF16), jax.ShapeDtypeStruct((t, D), F32),
        jax.ShapeDtypeStruct((t, SLAB), BF16), jax.ShapeDtypeStruct((2, t, SLAB), F32),
        bf_rows(), bf_rows(), bf_rows(), bf_rows(), jax.ShapeDtypeStruct((t, SLAB), BF16), bf_rows(),
        jax.ShapeDtypeStruct((N_REST, 8, SLAB), F32), jax.ShapeDtypeStruct((7, 8, D), F32),
        jax.ShapeDtypeStruct((nbat, 8, D), F32))
    out_specs = (
        pl.BlockSpec((N_REST, tm, SLAB), slab_blk), pl.BlockSpec((tm, D), row_blk),
        pl.BlockSpec((tm, SLAB), row_blk), pl.BlockSpec((2, tm, SLAB), slab_blk),
        pl.BlockSpec((tm, D), row_blk), pl.BlockSpec((tm, D), row_blk), pl.BlockSpec((tm, D), row_blk),
        pl.BlockSpec((tm, D), row_blk), pl.BlockSpec((tm, SLAB), row_blk), pl.BlockSpec((tm, D), row_blk),
        pl.BlockSpec((N_REST, 8, SLAB), const3), pl.BlockSpec((7, 8, D), const3),
        pl.BlockSpec((1, 8, D), lambda i: (tile(i) // tps, 0, 0)))
    return _pcall(body, grid=(nt,), out_shape=out_shape, in_specs=in_specs, out_specs=out_specs,
                  scratch_shapes=[pltpu.VMEM((8, D), F32), pltpu.VMEM((6, tm, D), F32)], name="mid",
                  compiler_params=_params(56, ("arbitrary",)))(
        rest, rest, *ols, x, tgt, ada, cw, b_out, ln_g, ln_b, w_pa_t, w_pb, w_out)


def _tn_matmul(lhs, rhs, lhs_spec, n_steps, out_rows, out_index, name, after):
    t, n = rhs.shape

    def body(l_ref, r_ref, after_ref, o_ref):
        del after_ref
        o_ref[...] = _tn(l_ref[0] if len(l_ref.shape) == 3 else l_ref[...], r_ref[...])

    return _pcall(body, grid=(n_steps,), out_shape=jax.ShapeDtypeStruct((out_rows, n), F32),
                  in_specs=[lhs_spec, pl.BlockSpec((t, n), lambda j: (0, 0)), ANY],
                  out_specs=pl.BlockSpec((SLAB, n), out_index), name=name,
                  compiler_params=_params(48, ("parallel",)))(lhs, rhs, after)


def _grad_rows_2d(lhs, rhs, name, after):
    t, k = lhs.shape
    return _tn_matmul(lhs, rhs, pl.BlockSpec((t, SLAB), lambda j: (0, j)), k // SLAB, k, lambda j: (j, 0), name, after)


def _w_row_block(j):
    return (j + N_QKV) % N_SLAB


def _dp_slab(j):
    return jnp.where(j < N_REST, j, j + 2)


def _grad_w_in_t(dproj, h):
    t = h.shape[0]
    return _tn_matmul(dproj, h, pl.BlockSpec((1, t, SLAB), lambda j: (_dp_slab(j), 0, 0)), N_SLAB, D_IN,
                      lambda j: (_w_row_block(j), 0), "grad_w_in", h)


def _grad_h(dproj, w_in_t, gx0, x, ada, after, tm=512):
    t = x.shape[0]
    nbat = ada.shape[0]
    tps = (t // nbat) // tm

    def body(dp_ref, w_ref, gx0_ref, x_ref, ada_ref, after_ref, gx_ref, dss_ref):
        del after_ref
        i = pl.program_id(0)
        dh = None
        for j in range(N_SLAB):
            slab = j if j < N_REST else j + 2
            part = _nn(dp_ref[slab], w_ref[pl.ds(SLAB * ((j + N_QKV) % N_SLAB), SLAB), :])
            dh = part if dh is None else dh + part
        gx_ref[...] = gx0_ref[...] + dh * (1.0 + ada_ref[0, 1:2, :])

        @pl.when((i % tps) == 0)
        def _():
            dss_ref[...] = jnp.zeros_like(dss_ref)

        dss_ref[0, 0] += _part8(dh)
        dss_ref[0, 1] += _part8(dh * x_ref[...])

    return _pcall(
        body, grid=(t // tm,),
        out_shape=(jax.ShapeDtypeStruct((t, D), F32), jax.ShapeDtypeStruct((nbat, 2, 8, D), F32)),
        in_specs=[pl.BlockSpec((DP_SLABS, tm, SLAB), lambda i: (0, i, 0)),
                  pl.BlockSpec((D_IN, D), lambda i: (0, 0), pipeline_mode=pl.Buffered(1)),
                  pl.BlockSpec((tm, D), lambda i: (i, 0)), pl.BlockSpec((tm, D), lambda i: (i, 0)),
                  pl.BlockSpec((1, 3, D), lambda i: (i // tps, 0, 0)), ANY],
        out_specs=(pl.BlockSpec((tm, D), lambda i: (i, 0)),
                   pl.BlockSpec((1, 2, 8, D), lambda i: (i // tps, 0, 0, 0))),
        name="grad_h", compiler_params=_params(60, ("arbitrary",)))(dproj, w_in_t, gx0, x, ada, after)


def _chip(m):
    x, y, _ = _my_position()
    return (x ^ ((m >> 1) & 1), y ^ (m & 1))


def _exchange_siblings(grads, name):
    n = len(grads)

    def body(*refs):
        copies = _sibling_copies(refs[:n], refs[n:2 * n], refs[2 * n], refs[2 * n + 1])
        for cp in copies:
            cp.start()
        for cp in copies:
            cp.wait()

    return _pcall(body, out_shape=tuple(_sibling_zones(grads)), in_specs=[ANY] * n, out_specs=(ANY,) * n,
                  name=name, scratch_shapes=[pltpu.SemaphoreType.DMA((4 * n,))] * 2)(*grads)


def _sibling_zones(grads):
    return [jax.ShapeDtypeStruct((4, g.shape[0] // N_DEV, g.shape[1]), g.dtype) for g in grads]


def _sibling_copies(srcs, lands, send_sems, recv_sems):
    x, y, c = _my_position()
    copies = []
    for a, (src, land) in enumerate(zip(srcs, lands)):
        rows = land.shape[1]
        for m in range(4):
            dev = _flat(*_chip(m), 1 - c)
            copies.append(pltpu.make_async_remote_copy(
                src_ref=src.at[pl.ds(pl.multiple_of(dev * rows, 8), rows), :], dst_ref=land.at[m],
                send_sem=send_sems.at[4 * a + m], recv_sem=recv_sems.at[4 * a + m], device_id=(x, y, 1 - c),
                device_id_type=MESH))
    return copies


def _chip_copies(srcs, lands, send_sems, recv_sems):
    _, _, c = _my_position()
    return [pltpu.make_async_remote_copy(
        src_ref=srcs[a].at[m - 1], dst_ref=lands[a].at[m - 1], send_sem=send_sems.at[3 * a + m - 1],
        recv_sem=recv_sems.at[3 * a + m - 1], device_id=(*_chip(m), c), device_id_type=MESH)
        for a in range(len(srcs)) for m in range(1, 4)]


HBM = pl.BlockSpec(memory_space=pltpu.HBM)
SEM = pl.BlockSpec(memory_space=pltpu.SEMAPHORE)
SPLIT_COPY = pltpu.CompilerParams(has_side_effects=pltpu.SideEffectType.DATAFLOW_SIDE_EFFECTING)


def _start_copies(make_copies, n_sems, srcs, zones, name):
    n = len(srcs)

    def body(*refs):
        for cp in make_copies(refs[:n], refs[n:2 * n], refs[2 * n], refs[2 * n + 1]):
            cp.start()
        refs[-1][...] = jnp.zeros_like(refs[-1])

    hbm = tuple(pltpu.HBM(b.shape, b.dtype) for b in list(srcs) + list(zones))
    out_shape = (pltpu.SemaphoreType.DMA((n_sems,)), pltpu.SemaphoreType.DMA((n_sems,))) + hbm + (jax.ShapeDtypeStruct((8, 128), F32),)
    operands = [pltpu.with_memory_space_constraint(b, pltpu.HBM) for b in srcs]
    operands += [pltpu.with_memory_space_constraint(lax.empty(z.shape, z.dtype), pltpu.HBM) for z in zones]
    res = _pcall(body, out_shape=out_shape, in_specs=[HBM] * (2 * n), out_specs=(SEM, SEM) + (HBM,) * (2 * n) + (VMEM,),
                 input_output_aliases={i: 2 + i for i in range(2 * n)}, name=name, compiler_params=SPLIT_COPY)(*operands)
    return (res[0], res[1], res[2:2 + n], res[2 + n:2 + 2 * n]), res[-1]


def _wait_copies(make_copies, flight, after, name):
    send_sems, recv_sems, srcs, zones = flight
    n = len(srcs)

    def body(*refs):
        for cp in make_copies(refs[:n], refs[n:2 * n], refs[2 * n], refs[2 * n + 1]):
            cp.wait_send()
            cp.wait_recv()

    hbm = tuple(pltpu.HBM(b.shape, b.dtype) for b in list(srcs) + list(zones))
    res = _pcall(body, out_shape=hbm, in_specs=[HBM] * (2 * n) + [SEM, SEM, ANY], out_specs=(HBM,) * (2 * n),
                 input_output_aliases={i: i for i in range(2 * n)}, name=name, compiler_params=SPLIT_COPY)(
        *srcs, *zones, send_sems, recv_sems, after)
    return res[:n], res[n:]


def _pair_sums(devs, grads, lands, n_steps, name):
    n = len(grads)
    rows = [l.shape[1] for l in lands]
    rbs = [r // n_steps for r in rows]

    def body(devs_ref, *refs):
        del devs_ref
        g_refs, land_refs, outs = refs[:4 * n], refs[4 * n:5 * n], refs[5 * n:]
        for a in range(n):
            outs[2 * a][...] = g_refs[4 * a][...] + land_refs[a][0]
            for m in range(1, 4):
                outs[2 * a + 1][m - 1] = (g_refs[4 * a + m][...] + land_refs[a][m]).astype(BF16)

    def block_of(m, per_dev):
        return lambda i, devs_ref: (devs_ref[m] * per_dev + i, 0)

    in_specs = [pl.BlockSpec((rb, l.shape[2]), block_of(m, n_steps)) for rb, l in zip(rbs, lands) for m in range(4)]
    in_specs += [pl.BlockSpec((4, rb, l.shape[2]), lambda i, devs_ref: (0, i, 0)) for rb, l in zip(rbs, lands)]
    out_shape, out_specs = [], []
    for rb, l in zip(rbs, lands):
        out_shape += [jax.ShapeDtypeStruct(l.shape[1:], F32), jax.ShapeDtypeStruct((3,) + l.shape[1:], BF16)]
        out_specs += [pl.BlockSpec((rb, l.shape[2]), lambda i, devs_ref: (i, 0)),
                      pl.BlockSpec((3, rb, l.shape[2]), lambda i, devs_ref: (0, i, 0))]
    grid_spec = pltpu.PrefetchScalarGridSpec(num_scalar_prefetch=1, grid=(n_steps,), in_specs=in_specs, out_specs=tuple(out_specs))
    res = _pcall(body, grid_spec=grid_spec, out_shape=tuple(out_shape), name=name,
                 compiler_params=_params(48, ("parallel",)))(devs, *[g for g in grads for _ in range(4)], *lands)
    return res[0::2], res[1::2]


def _final_sums(mine, lands, n_steps, name):
    n = len(mine)
    rbs = [o.shape[0] // n_steps for o in mine]

    def body(*refs):
        mine_refs, land_refs, outs = refs[:n], refs[n:2 * n], refs[2 * n:]
        for a in range(n):
            tot = mine_refs[a][...]
            for m in range(3):
                tot = tot + land_refs[a][m].astype(F32)
            outs[a][...] = tot

    in_specs = ([pl.BlockSpec((rb, o.shape[1]), lambda i: (i, 0)) for rb, o in zip(rbs, mine)]
                + [pl.BlockSpec((3, rb, o.shape[1]), lambda i: (0, i, 0)) for rb, o in zip(rbs, mine)])
    out_specs = tuple(pl.BlockSpec((rb, o.shape[1]), lambda i: (i, 0)) for rb, o in zip(rbs, mine))
    out_shape = tuple(jax.ShapeDtypeStruct(o.shape, F32) for o in mine)
    return _pcall(body, grid=(n_steps,), out_shape=out_shape, in_specs=in_specs, out_specs=out_specs, name=name,
                  compiler_params=_params(32, ("parallel",)))(*mine, *lands)


def _reduce_scatter_begin(big, small_after_start):
    c = lax.axis_index("c")
    devs = jnp.stack([_flat(*_chip(m), c) for m in range(4)]).astype(jnp.int32)
    big_mine, big_send = _pair_sums(devs, [big], _exchange_siblings([big], "exchange_siblings_w_in"), 4, "pair_sums_w_in")
    big_flight, token = _start_copies(_chip_copies, 3, list(big_send), list(big_send), "chips_start_w_in")
    small = small_after_start(token)
    small_mine, small_send = _pair_sums(devs, small, _exchange_siblings(small, "exchange_siblings_rest"), 1, "pair_sums_rest")
    small_flight, token = _start_copies(_chip_copies, 3 * len(small), list(small_send), list(small_send), "chips_start_rest")
    return (big_flight, small_flight, list(big_mine) + list(small_mine)), token


def _reduce_scatter_end(state, after):
    big_flight, small_flight, mine = state
    _, big_got = _wait_copies(_chip_copies, big_flight, after, "chips_wait_w_in")
    _, small_got = _wait_copies(_chip_copies, small_flight, after, "chips_wait_rest")
    small = _final_sums(mine[1:], small_got, 1, "final_sums_rest")
    return (mine[0], big_got[0]), list(small)


def _adamw(w, g, m, v):
    m_new = B1 * m + (1.0 - B1) * g
    v_new = B2 * v + (1.0 - B2) * (g * g)
    m_hat = m_new / (1.0 - B1 ** STEP)
    v_hat = v_new / (1.0 - B2 ** STEP)
    delta = -LR * (m_hat / (jnp.sqrt(v_hat) + EPS) + WD * w)
    return delta, m_new, v_new


def _final_sum_adam_rows(mine, land, w, m, v, n_steps, name):
    rows, ncol = w.shape
    blk = pl.BlockSpec((rows // n_steps, ncol), lambda i: (i, 0))

    def body(mine_ref, land_ref, w_ref, m_ref, v_ref, g_ref, d_ref, mo_ref, vo_ref):
        g = mine_ref[...]
        for k in range(3):
            g = g + land_ref[k].astype(F32)
        g_ref[...] = g
        d_ref[...], mo_ref[...], vo_ref[...] = _adamw(w_ref[...], g, m_ref[...], v_ref[...])

    shape = jax.ShapeDtypeStruct(w.shape, F32)
    return _pcall(body, grid=(n_steps,), out_shape=(shape,) * 4,
                  in_specs=[blk, pl.BlockSpec((3, rows // n_steps, ncol), lambda i: (0, i, 0)), blk, blk, blk],
                  out_specs=(blk,) * 4, name=name, compiler_params=_params(32, ("parallel",)))(mine, land, w, m, v)


def _adam_transposed(g_t, w, m, v, name):
    n, k = g_t.shape
    rb = min(k, 128)

    def body(gt_ref, w_ref, m_ref, v_ref, g_ref, d_ref, mo_ref, vo_ref):
        for src, skip, dst, size in _column_chunks(n):
            sl = pl.ds(dst, size)
            g = gt_ref[pl.ds(src, 128), :].T[:, skip:]
            delta, m_new, v_new = _adamw(w_ref[:, sl], g, m_ref[:, sl], v_ref[:, sl])
            g_ref[:, sl], d_ref[:, sl], mo_ref[:, sl], vo_ref[:, sl] = g, delta, m_new, v_new

    shape = jax.ShapeDtypeStruct(w.shape, F32)
    rows = pl.BlockSpec((rb, n), lambda i: (i, 0))
    return _pcall(body, grid=(k // rb,), out_shape=(shape,) * 4,
                  in_specs=[pl.BlockSpec((n, rb), lambda i: (0, i)), rows, rows, rows], out_specs=(rows,) * 4, name=name,
                  compiler_params=_params(32, ("parallel",)))(g_t, w, m, v)


def _adam_many(items, name):
    n = len(items)

    def body(*refs):
        ins, outs = refs[:4 * n], refs[4 * n:]
        for a in range(n):
            w_ref, g_ref, m_ref, v_ref = ins[4 * a:4 * a + 4]
            delta, m_new, v_new = _adamw(w_ref[...], g_ref[...], m_ref[...], v_ref[...])
            outs[3 * a][...], outs[3 * a + 1][...], outs[3 * a + 2][...] = delta, m_new, v_new

    out_shape = tuple(jax.ShapeDtypeStruct(it[0].shape, F32) for it in items for _ in range(3))
    flat = [arr for it in items for arr in it]
    res = _pcall(body, grid=(1,), out_shape=out_shape, in_specs=[_whole(a) for a in flat],
                 out_specs=tuple(_whole(o) for o in out_shape), name=name, compiler_params=_params(32))(*flat)
    return [tuple(res[3 * a:3 * a + 3]) for a in range(n)]


def _adam_w_ada(cact_all, dada_mine, w, m, v):
    def body(c_ref, d_ref, w_ref, m_ref, v_ref, g_ref, dl_ref, mo_ref, vo_ref):
        g = _tn(c_ref[...].astype(BF16), d_ref[...].astype(BF16))
        delta, m_new, v_new = _adamw(w_ref[...], g, m_ref[...], v_ref[...])
        g_ref[...], dl_ref[...], mo_ref[...], vo_ref[...] = g, delta, m_new, v_new

    shape = jax.ShapeDtypeStruct(w.shape, F32)
    operands = (cact_all, dada_mine, w, m, v)
    return _pcall(body, grid=(1,), out_shape=(shape,) * 4, in_specs=[_whole(a) for a in operands],
                  out_specs=(_whole(w),) * 4, name="adam_w_ada", compiler_params=_params(32))(*operands)


def kernel(x, c, w_ada, b_ada, w_in, b_in, conv_w, w_proj_attn, w_proj_conv, w_out, b_out, ln_g, ln_b, loss_target, m_w_ada, m_b_ada, m_w_in, m_b_in, m_conv_w, m_w_proj_attn, m_w_proj_conv, m_w_out, m_b_out, m_ln_g, m_ln_b, v_w_ada, v_b_ada, v_w_in, v_b_in, v_conv_w, v_w_proj_attn, v_w_proj_conv, v_w_out, v_b_out, v_ln_g, v_ln_b):
    nbat, seq, _ = x.shape
    t = nbat * seq
    me = _flat(*_my_position())
    x2, tgt2 = x.reshape(t, D), loss_target.reshape(t, D)
    sq = lambda a: a.reshape(a.shape[1:])

    tr = lambda a: a[0].T
    w_in_rows = tr(w_in)
    w_in_t_s = _cast_rows(w_in_rows, 4, "cast_w_in")
    w_pa_t_s, w_pb_s, w_out_s, cact_s, cw_s = _prep(sq(w_proj_attn), sq(w_proj_conv), sq(w_out), c, sq(conv_w))

    ncol = w_ada.shape[2]
    b_ada_mine = lax.dynamic_slice(b_ada, (0, me * ncol), (1, ncol))
    ada_slots, cact_slots, cw_slots = _ada_forward(cact_s, cw_s, sq(w_ada), b_ada_mine)
    cact_all = cact_slots[:, :nbat].reshape(N_DEV * nbat, D)
    cw = cw_slots[:, :3].transpose(1, 0, 2).reshape(3, D)
    ada_all = ada_slots[:, :, :nbat].transpose(1, 2, 0, 3).reshape(N_DEV * nbat, 3, D)
    ada = lax.dynamic_slice(ada_all, (me * nbat, 0, 0), (nbat, 3, D))

    h = _make_h(x2, ada)
    w_in_t, qkv, rest, (w_pa_t, w_pb, w_o) = _project_gather(w_in_t_s, h, b_in.reshape(N_SLAB, 1, SLAB), [w_pa_t_s, w_pb_s, w_out_s])
    ols = [_attn_forward(qkv, g, nbat) for g in range(3)]
    (dproj, gx0, do_attn, ol_tot, merged, do_f, bbs, dyc, a_bf, dya, gb_rest, svec, dgate) = _mid(
        rest, ols, x2, tgt2, ada, cw, b_out, ln_g, ln_b, w_pa_t, w_pb, w_o)

    gb_qkv = []
    for g in range(3):
        dproj, gb = _attn_backward(qkv, do_attn, ol_tot, dproj, g, nbat)
        gb_qkv.append(gb)
    g_w_in_t = _grad_w_in_t(dproj, h)

    def small_grads(token):
        g_w_out = _grad_rows_2d(merged, do_f, "grad_w_out", token)
        g_w_pb = _grad_rows_2d(bbs, dyc, "grad_w_proj_conv", g_w_out)
        g_w_pa_t = _grad_rows_2d(dya, a_bf, "grad_w_proj_attn", g_w_pb)
        return [g_w_out, g_w_pb, g_w_pa_t]

    rs_state, token = _reduce_scatter_begin(g_w_in_t, small_grads)
    grad_x, dss = _grad_h(dproj, w_in_t, gx0, x2, ada, token)

    rows8, tot, g_bada = _small_reduce(gb_rest, gb_qkv, svec, dgate, dss)
    (g_in_mine, g_in_got), (g_out, g_pb, g_pa_t) = _reduce_scatter_end(rs_state, tot)
    loss = tot[0, P_LOSS]
    dada_all = rows8[:, 0, P_DADA:].reshape(N_DEV * nbat, 3 * D)
    dada_mine = lax.dynamic_slice(dada_all, (0, me * ncol), (N_DEV * nbat, ncol))

    g_in_t, d_win_t, nm_win_t, nv_win_t = _final_sum_adam_rows(g_in_mine, g_in_got, w_in_rows, tr(m_w_in), tr(v_w_in), 4, "adam_w_in")
    g_win, d_win, nm_win, nv_win = g_in_t.T, d_win_t.T, nm_win_t.T, nv_win_t.T
    g_wpa, d_wpa, nm_wpa, nv_wpa = _adam_transposed(g_pa_t, sq(w_proj_attn), sq(m_w_proj_attn), sq(v_w_proj_attn), "adam_w_proj_attn")
    g_wada, d_wada, nm_wada, nv_wada = _adam_w_ada(cact_all, dada_mine, sq(w_ada), sq(m_w_ada), sq(v_w_ada))
    g_bin = tot[:, P_BIN:P_BIN + D_IN]
    g_bout = tot[:, P_BOUT:P_BOUT + D]
    g_lng = tot[:, P_LNG:P_LNG + D]
    g_lnb = tot[:, P_LNB:P_LNB + D]
    g_conv = lax.dynamic_slice(tot[:, P_CONV:P_CONV + 3 * D].reshape(3, D), (0, me * cw_s.shape[1]), (3, cw_s.shape[1]))
    upd = _adam_many([
        (sq(w_proj_conv), g_pb, sq(m_w_proj_conv), sq(v_w_proj_conv)),
        (sq(w_out), g_out, sq(m_w_out), sq(v_w_out)),
        (b_ada, g_bada, m_b_ada, v_b_ada), (b_in, g_bin, m_b_in, v_b_in), (sq(conv_w), g_conv, sq(m_conv_w), sq(v_conv_w)),
        (b_out, g_bout, m_b_out, v_b_out), (ln_g, g_lng, m_ln_g, v_ln_g), (ln_b, g_lnb, m_ln_b, v_ln_b)], "adam_rest")
    (d_wpb, nm_wpb, nv_wpb), (d_wout, nm_wout, nv_wout), (d_bada, nm_bada, nv_bada), (d_bin, nm_bin, nv_bin), \
        (d_conv, nm_conv, nv_conv), (d_bout, nm_bout, nv_bout), (d_lng, nm_lng, nv_lng), (d_lnb, nm_lnb, nv_lnb) = upd

    ex = lambda a: a.reshape((1,) + a.shape)
    grads = [ex(g_wada), g_bada, ex(g_win), g_bin, ex(g_conv), ex(g_wpa), ex(g_pb), ex(g_out), g_bout, g_lng, g_lnb]
    deltas = [ex(d_wada), d_bada, ex(d_win), d_bin, ex(d_conv), ex(d_wpa), ex(d_wpb), ex(d_wout), d_bout, d_lng, d_lnb]
    new_m = [ex(nm_wada), nm_bada, ex(nm_win), nm_bin, ex(nm_conv), ex(nm_wpa), ex(nm_wpb), ex(nm_wout), nm_bout, nm_lng, nm_lnb]
    new_v = [ex(nv_wada), nv_bada, ex(nv_win), nv_bin, ex(nv_conv), ex(nv_wpa), ex(nv_wpb), ex(nv_wout), nv_bout, nv_lng, nv_lnb]
    return (loss, grad_x.reshape(x.shape), *grads, *deltas, *new_m, *new_v)
```

```python
import functools

import jax
import jax.numpy as jnp
from jax import lax
from jax.experimental import pallas as pl
from jax.experimental.pallas import tpu as pltpu

F32, BF16 = jnp.float32, jnp.bfloat16
MESH = pl.DeviceIdType.MESH
N_DEV = 8
D = 1024
SLAB = 256
N_QKV, N_REST = 9, 25
N_SLAB = N_QKV + N_REST
D_IN = N_SLAB * SLAB
DP_SLABS = 36
BLK = 128
GROUPS = ((128, 1), (512, 4), (2048, 16))
ALPHA = 2.0 ** 0.25
LN_EPS = 1e-5
LR, B1, B2, EPS, WD, STEP = 0.001, 0.9, 0.999, 1e-08, 0.01, 10
R_ZA, R_UX, R_GB, R_GC, R_ZC, R_GA, R_GBM = 0, 1, 5, 9, 13, 17, 21
P_BIN, P_BOUT, P_LNG, P_LNB, P_CONV, P_LOSS, P_DADA = 0, 8704, 9728, 10752, 11776, 14848, 14976
MIB = 1024 * 1024


def _pcall(body, *, out_shape, out_specs=None, **kw):
    def pin_out(shape, spec):
        blocked = isinstance(shape, jax.ShapeDtypeStruct) and getattr(spec, "block_shape", None) is not None
        return pltpu.HBM(shape.shape, shape.dtype) if blocked else shape

    n_scalar = 0
    if out_specs is None:
        specs = kw["grid_spec"].out_specs
        n_scalar = kw["grid_spec"].num_scalar_prefetch
    else:
        kw["out_specs"] = specs = out_specs
    if isinstance(out_shape, (tuple, list)):
        out_shape = tuple(pin_out(s, p) for s, p in zip(out_shape, specs))
    else:
        out_shape = pin_out(out_shape, specs)
    call = pl.pallas_call(body, out_shape=out_shape, **kw)

    def run(*operands):
        def pin(o):
            is_data = jnp.issubdtype(o.dtype, jnp.floating) or jnp.issubdtype(o.dtype, jnp.integer)
            return pltpu.with_memory_space_constraint(o, pltpu.HBM) if is_data else o
        return call(*operands[:n_scalar], *[pin(o) for o in operands[n_scalar:]])

    return run

ANY = pl.BlockSpec(memory_space=pl.ANY)
VMEM = pl.BlockSpec(memory_space=pltpu.VMEM)


def _whole(a):
    return pl.BlockSpec(a.shape, lambda i: (0,) * len(a.shape))


def _params(vmem_mib=None, sem=None):
    kw = {}
    if vmem_mib is not None:
        kw["vmem_limit_bytes"] = vmem_mib * MIB
    if sem is not None:
        kw["dimension_semantics"] = sem
    return pltpu.CompilerParams(**kw)


def _nn(a, b):
    return jnp.dot(a, b, preferred_element_type=F32)


def _nt(a, b):
    return lax.dot_general(a, b, (((1,), (1,)), ((), ())), preferred_element_type=F32)


def _tn(a, b):
    return lax.dot_general(a, b, (((0,), (0,)), ((), ())), preferred_element_type=F32)


def _sigmoid(v):
    return 1.0 / (1.0 + jnp.exp(-v))


def _part8(v):
    return v.reshape(v.shape[0] // 8, 8, v.shape[1]).sum(axis=0)


def _my_position():
    return lax.axis_index("x"), lax.axis_index("y"), lax.axis_index("c")


def _flat(px, py, pc):
    return 4 * px + 2 * py + pc


def _peer(mask):
    x, y, c = _my_position()
    return (x ^ ((mask >> 2) & 1), y ^ ((mask >> 1) & 1), c ^ (mask & 1))


def _column_chunks(n):
    chunks = [(128 * a, 0, 128 * a, 128) for a in range(n // 128)]
    if n % 128:
        chunks.append((n - 128, 128 - n % 128, 128 * (n // 128), n % 128))
    return chunks


def _cast_rows(w, n_steps, name):
    rows, ncol = w.shape
    blk = pl.BlockSpec((rows // n_steps, ncol), lambda i: (i, 0))

    def body(w_ref, o_ref):
        o_ref[...] = w_ref[...].astype(BF16)

    return _pcall(body, grid=(n_steps,), out_shape=jax.ShapeDtypeStruct(w.shape, BF16), in_specs=[blk], out_specs=blk,
                  name=name, compiler_params=_params(16, ("parallel",)))(w)


def _prep(w_pa, w_pb, w_out, c, conv_w):
    def body(wpa_ref, wpb_ref, wout_ref, c_ref, cw_ref, wpat_ref, wpb_o, wout_o, cact_ref, cwp_ref):
        wpat_ref[...] = wpa_ref[...].T.astype(BF16)
        wpb_o[...] = wpb_ref[...].astype(BF16)
        wout_o[...] = wout_ref[...].astype(BF16)
        cv = c_ref[...]
        cact_ref[...] = jnp.zeros_like(cact_ref)
        cact_ref[pl.ds(0, cv.shape[0]), :] = cv * _sigmoid(cv)
        cwp_ref[...] = jnp.zeros_like(cwp_ref)
        cwp_ref[pl.ds(0, 3), :] = cw_ref[...]

    out_shape = (jax.ShapeDtypeStruct((w_pa.shape[1], w_pa.shape[0]), BF16),
                 jax.ShapeDtypeStruct(w_pb.shape, BF16), jax.ShapeDtypeStruct(w_out.shape, BF16),
                 jax.ShapeDtypeStruct((8, D), F32), jax.ShapeDtypeStruct((8, conv_w.shape[1]), F32))
    operands = (w_pa, w_pb, w_out, c, conv_w)
    return _pcall(body, grid=(1,), out_shape=out_shape, in_specs=[_whole(a) for a in operands],
                  out_specs=tuple(_whole(o) for o in out_shape), name="prep", compiler_params=_params(16))(*operands)


def _exchange_slots(out_refs, send_sems, recv_sems, base=0):
    me = _flat(*_my_position())

    def copy(a, mask, slot):
        return pltpu.make_async_remote_copy(
            src_ref=out_refs[a].at[slot], dst_ref=out_refs[a].at[slot], send_sem=send_sems.at[base + 7 * a + mask - 1],
            recv_sem=recv_sems.at[base + 7 * a + mask - 1], device_id=_peer(mask), device_id_type=MESH)

    pairs = [(a, mask) for a in range(len(out_refs)) for mask in range(1, N_DEV)]
    for a, mask in pairs:
        copy(a, mask, me).start()
    for a, mask in pairs:
        copy(a, mask, _flat(*_peer(mask))).wait_recv()
    for a, mask in pairs:
        copy(a, mask, me).wait_send()


def _ada_forward(cact_mine, cw_mine, w_ada, b_ada_mine):
    ncol = w_ada.shape[1]

    def body(c_ref, cw_ref, w_ref, b_ref, out_ref, call_ref, cwall_ref, send_sems, recv_sems):
        me = _flat(*_my_position())
        call_ref[me] = c_ref[...]
        cwall_ref[me] = cw_ref[...]
        _exchange_slots([call_ref, cwall_ref], send_sems, recv_sems)
        c_all = call_ref[...].reshape(N_DEV * 8, D).astype(BF16)
        out_ref[me] = (_nn(c_all, w_ref[...].astype(BF16)) + b_ref[...]).reshape(N_DEV, 8, ncol)
        _exchange_slots([out_ref], send_sems, recv_sems, base=14)

    operands = (cact_mine, cw_mine, w_ada, b_ada_mine)
    out_shape = (jax.ShapeDtypeStruct((N_DEV, N_DEV, 8, ncol), F32), jax.ShapeDtypeStruct((N_DEV, 8, D), F32),
                 jax.ShapeDtypeStruct((N_DEV,) + cw_mine.shape, F32))
    return _pcall(body, grid=(1,), out_shape=out_shape, in_specs=[_whole(a) for a in operands], out_specs=(VMEM,) * 3,
                  scratch_shapes=[pltpu.SemaphoreType.DMA((21,)), pltpu.SemaphoreType.DMA((21,))], name="ada_forward",
                  compiler_params=_params(16))(*operands)


def _small_reduce(gb_rest, gb_qkv, svec, dgate, dss):
    nbat = dgate.shape[0]

    def body(gbr_ref, q0_ref, q1_ref, q2_ref, sv_ref, dg_ref, dss_ref, rows_ref, tot_ref, gbada_ref, send_sems, recv_sems):
        me = _flat(*_my_position())

        def put(off, v):
            rows_ref[me, :, pl.ds(off, v.shape[1])] = v

        def row(v):
            return jnp.sum(v, axis=0, keepdims=True)

        for g, q_ref in enumerate((q0_ref, q1_ref, q2_ref)):
            for which in range(3):
                put(P_BIN + SLAB * (3 * which + g), row(q_ref[which]))
        for s in range(N_REST):
            put(P_BIN + SLAB * (N_QKV + s), row(gbr_ref[s]))
        put(P_LNG, row(sv_ref[0]))
        put(P_LNB, row(sv_ref[1]))
        put(P_BOUT, row(sv_ref[2]))
        for j in range(3):
            put(P_CONV + D * j, row(sv_ref[3 + j]))
        loss = (0.5 / D) * jnp.sum(row(sv_ref[6]), axis=1, keepdims=True)
        put(P_LOSS, jnp.broadcast_to(loss, (1, 128)))
        for b in range(nbat):
            put(P_DADA + 3 * D * b, row(dss_ref[b, 0]))
            put(P_DADA + 3 * D * b + D, row(dss_ref[b, 1]))
            put(P_DADA + 3 * D * b + 2 * D, row(dg_ref[b]))
        _exchange_slots([rows_ref], send_sems, recv_sems)
        tot = rows_ref[0]
        for k in range(1, N_DEV):
            tot = tot + rows_ref[k]
        tot_ref[...] = tot
        gbada = tot[:, P_DADA:P_DADA + 3 * D]
        for b in range(1, nbat):
            gbada = gbada + tot[:, P_DADA + 3 * D * b:P_DADA + 3 * D * (b + 1)]
        gbada_ref[...] = gbada

    p_len = P_DADA + nbat * 3 * D
    out_shape = (jax.ShapeDtypeStruct((N_DEV, 1, p_len), F32), jax.ShapeDtypeStruct((1, p_len), F32),
                 jax.ShapeDtypeStruct((1, 3 * D), F32))
    operands = (gb_rest, *gb_qkv, svec, dgate, dss)
    return _pcall(body, grid=(1,), out_shape=out_shape, in_specs=[_whole(a) for a in operands],
                  out_specs=(VMEM, _whole(out_shape[1]), _whole(out_shape[2])),
                  scratch_shapes=[pltpu.SemaphoreType.DMA((7,)), pltpu.SemaphoreType.DMA((7,))], name="small_reduce",
                  compiler_params=_params(16))(*operands)


def _make_h(x, ada, tm=512):
    t = x.shape[0]
    tps = (t // ada.shape[0]) // tm

    def body(x_ref, ada_ref, h_ref):
        h_ref[...] = (x_ref[...] * (1.0 + ada_ref[0, 1:2, :]) + ada_ref[0, 0:1, :]).astype(BF16)

    return _pcall(body, grid=(t // tm,), out_shape=jax.ShapeDtypeStruct((t, D), BF16),
                  in_specs=[pl.BlockSpec((tm, D), lambda i: (i, 0)), pl.BlockSpec((1, 3, D), lambda i: (i // tps, 0, 0))],
                  out_specs=pl.BlockSpec((tm, D), lambda i: (i, 0)), name="make_h",
                  compiler_params=_params(32, ("parallel",)))(x, ada)


PIECE = 64
N_CHUNK = 4
ARRIVAL_RANK = (0, 1, 3, 5, 2, 4, 6, 7)
SLOT_MASK = (1, 4, 2, 6, 5, 3, 7)


def _arrival_tables(shard_rows):
    import numpy as np
    crow = shard_rows // N_CHUNK
    table = np.zeros((N_DEV, N_SLAB + 7 * N_CHUNK), np.int32)
    lo = [(SLAB * j) // crow for j in range(N_SLAB)]
    hi = [(SLAB * j + SLAB - 1) // crow for j in range(N_SLAB)]
    for k in range(N_DEV):
        def rank(chunk):
            shard_rank = ARRIVAL_RANK[(chunk // N_CHUNK) ^ k]
            return shard_rank if shard_rank < 2 else 2 + 8 * (chunk % N_CHUNK) + shard_rank
        order = sorted(range(N_SLAB), key=lambda j: (max(rank(lo[j]), rank(hi[j])), j))
        table[k, :N_SLAB] = order
        for slot, mask in enumerate(SLOT_MASK):
            for ch in range(N_CHUNK):
                chunk = (k ^ mask) * N_CHUNK + ch
                table[k, N_SLAB + slot * N_CHUNK + ch] = min(t for t, j in enumerate(order) if lo[j] <= chunk <= hi[j])
    return table


def _project_gather(shard, h, b_in3, others):
    t = h.shape[0]
    n_o = len(others)
    srows = shard.shape[0]
    crow = srows // N_CHUNK
    shards = [shard] + list(others)
    table = jnp.asarray(_arrival_tables(srows))

    def body(tbl_ref, *refs):
        srcs = [refs[0]] + list(refs[3:3 + n_o])
        h_ref, b_ref = refs[1], refs[2]
        outs = [refs[3 + n_o]] + list(refs[6 + n_o:6 + 2 * n_o])
        qkv_ref, rest_ref = refs[4 + n_o], refs[5 + n_o]
        (wtile, obf, of32, send_sems, recv_sems, local_sems, tile_sems, obf_sems, of32_sems) = refs[6 + 2 * n_o:]
        w_full = outs[0]
        x, y, c = _my_position()
        k = _flat(x, y, c)
        me, sibling = (x, y, c), (x, y, 1 - c)
        chips = [(1 - x, y), (x, 1 - y), (1 - x, 1 - y)]

        def rows(a, px, py, pc, ch):
            r = shards[a].shape[0]
            if ch is None:
                return outs[a].at[pl.ds(pl.multiple_of(_flat(px, py, pc) * r, r), r), :]
            return outs[a].at[pl.ds(pl.multiple_of(_flat(px, py, pc) * r + ch * crow, crow), crow), :]

        def copy(a, slot, block, to, ch=None, src=None):
            sem = slot * N_CHUNK + ch if a == 0 else 7 * (N_CHUNK - 1 + a) + slot
            if src is not None and ch is not None:
                src = src.at[pl.ds(ch * crow, crow), :]
            return pltpu.make_async_remote_copy(
                src_ref=rows(a, *block, ch) if src is None else src, dst_ref=rows(a, *block, ch),
                send_sem=send_sems.at[sem], recv_sem=recv_sems.at[sem], device_id=to, device_id_type=MESH)

        mine = [pltpu.make_async_copy(srcs[a], rows(a, *me, None), local_sems.at[a]) for a in range(1 + n_o)]
        first = []
        for ch in range(N_CHUNK):
            first.append(copy(0, 0, me, sibling, ch, src=srcs[0]))
            first += [copy(0, 1 + j, me, (*chip, c), ch, src=srcs[0]) for j, chip in enumerate(chips)]
        for a in range(1, 1 + n_o):
            first.append(copy(a, 0, me, sibling, src=srcs[a]))
            first += [copy(a, 1 + j, me, (*chip, c), src=srcs[a]) for j, chip in enumerate(chips)]
        for cp in mine + first:
            cp.start()

        def arrive(a, slot, ch=None):
            if slot == 0:
                copy(a, 0, sibling, me, ch).wait_recv()
            elif slot < 4:
                copy(a, slot, (*chips[slot - 1], c), me, ch).wait_recv()
                copy(a, slot + 3, (*chips[slot - 1], c), sibling, ch).start()
            else:
                copy(a, slot, (*chips[slot - 4], 1 - c), me, ch).wait_recv()

        def arrive_for(step):
            for slot in range(7):
                for ch in range(N_CHUNK):
                    @pl.when(tbl_ref[k, N_SLAB + slot * N_CHUNK + ch] == step)
                    def _():
                        arrive(0, slot, ch)

        def fetch(step, buf):
            slab = tbl_ref[k, step]
            for p in range(SLAB // PIECE):
                g0 = slab * SLAB + PIECE * p
                own = (g0 >= k * srows) & (g0 < (k + 1) * srows)
                dst = wtile.at[buf, pl.ds(PIECE * p, PIECE), :]

                @pl.when(own)
                def _():
                    pltpu.make_async_copy(srcs[0].at[pl.ds(pl.multiple_of(g0 - k * srows, PIECE), PIECE), :], dst, tile_sems.at[buf]).start()

                @pl.when(jnp.logical_not(own))
                def _():
                    pltpu.make_async_copy(w_full.at[pl.ds(pl.multiple_of(g0, PIECE), PIECE), :], dst, tile_sems.at[buf]).start()

        def wait_tile(buf):
            pltpu.make_async_copy(w_full.at[pl.ds(0, SLAB), :], wtile.at[buf], tile_sems.at[buf]).wait()

        def put(buf_ref, sems, dst_ref, count, value):
            b = count % 2

            @pl.when(count >= 2)
            def _():
                pltpu.make_async_copy(buf_ref.at[b], dst_ref, sems.at[b]).wait()

            buf_ref[b] = value
            pltpu.make_async_copy(buf_ref.at[b], dst_ref, sems.at[b]).start()

        def drain(buf_ref, sems, dst_ref, count):
            for back in (1, 2):
                @pl.when(count >= back)
                def _():
                    pltpu.make_async_copy(buf_ref.at[(count - back) % 2], dst_ref, sems.at[(count - back) % 2]).wait()

        arrive_for(0)
        fetch(0, 0)

        def step(s, carry):
            n_bf, n_f32 = carry
            buf = s % 2

            @pl.when(s + 1 < N_SLAB)
            def _():
                arrive_for(s + 1)
                fetch(s + 1, 1 - buf)

            wait_tile(buf)
            slab = tbl_ref[k, s]
            v = _nt(h_ref[...], wtile[buf]) + b_ref[slab]
            is_qkv = slab < N_QKV

            @pl.when(is_qkv)
            def _():
                put(obf, obf_sems, qkv_ref.at[jnp.minimum(slab, N_QKV - 1)], n_bf, v.astype(BF16))

            @pl.when(jnp.logical_not(is_qkv))
            def _():
                put(of32, of32_sems, rest_ref.at[jnp.maximum(slab - N_QKV, 0)], n_f32, v)

            return n_bf + is_qkv.astype(jnp.int32), n_f32 + 1 - is_qkv.astype(jnp.int32)

        n_bf, n_f32 = lax.fori_loop(0, N_SLAB, step, (jnp.int32(0), jnp.int32(0)))
        drain(obf, obf_sems, qkv_ref.at[0], n_bf)
        drain(of32, of32_sems, rest_ref.at[0], n_f32)

        for slots in ((1, 2, 3), (0, 4, 5, 6)):
            for a in range(1, 1 + n_o):
                for slot in slots:
                    arrive(a, slot)
        for cp in first:
            cp.wait_send()
        for j, chip in enumerate(chips):
            for ch in range(N_CHUNK):
                copy(0, 4 + j, (*chip, c), sibling, ch).wait_send()
            for a in range(1, 1 + n_o):
                copy(a, 4 + j, (*chip, c), sibling).wait_send()
        for cp in mine:
            cp.wait()

    out_shape = ((jax.ShapeDtypeStruct((N_DEV * srows, D), BF16), jax.ShapeDtypeStruct((N_QKV, t, SLAB), BF16),
                  jax.ShapeDtypeStruct((N_REST, t, SLAB), F32))
                 + tuple(jax.ShapeDtypeStruct((N_DEV * o.shape[0], o.shape[1]), o.dtype) for o in others))
    n_all = 1 + n_o
    n_sems = 7 * (N_CHUNK + n_o)
    grid_spec = pltpu.PrefetchScalarGridSpec(
        num_scalar_prefetch=1, grid=(1,),
        in_specs=[ANY, pl.BlockSpec((t, D), lambda i, tbl: (0, 0), pipeline_mode=pl.Buffered(1)),
                  pl.BlockSpec((N_SLAB, 1, SLAB), lambda i, tbl: (0, 0, 0))] + [ANY] * n_o,
        out_specs=(ANY,) * (3 + n_o),
        scratch_shapes=[pltpu.VMEM((2, SLAB, D), BF16), pltpu.VMEM((2, t, SLAB), BF16), pltpu.VMEM((2, t, SLAB), F32),
                        pltpu.SemaphoreType.DMA((n_sems,)), pltpu.SemaphoreType.DMA((n_sems,)),
                        pltpu.SemaphoreType.DMA((n_all,)), pltpu.SemaphoreType.DMA((2,)), pltpu.SemaphoreType.DMA((2,)),
                        pltpu.SemaphoreType.DMA((2,))])
    res = _pcall(body, grid_spec=grid_spec, out_shape=out_shape, name="project_gather",
                 compiler_params=_params(48, ("arbitrary",)))(table, shard, h, b_in3, *others)
    return res[0], res[1], res[2], list(res[3:])


def _bias_tables(g):
    window, dil = GROUPS[g]
    span = window // dil
    qi = jnp.arange(BLK)[:, None]
    kj = jnp.arange(2 * BLK)[None, :]
    delta = qi + BLK - kj
    valid = (delta >= 0) & (delta <= span)
    heads = jnp.arange(4, dtype=F32) + 4.0 * g
    slopes = 2.0 ** (-8.0 * (heads + 1.0) / 12.0)
    bias = -slopes[:, None, None] * (delta * dil).astype(F32)[None]
    return jnp.where(valid[None], bias, -1e30).reshape(4 * BLK, 2 * BLK)


def _head_masks(shape):
    lane = lax.broadcasted_iota(jnp.int32, shape, 1)
    return [(lane >= 64 * h) & (lane < 64 * (h + 1)) for h in range(4)]


def _stack_heads(v, masks):
    return jnp.concatenate([jnp.where(masks[h], v, jnp.zeros_like(v)) for h in range(4)], axis=0)


def _unstack_heads(v4, masks):
    out = jnp.where(masks[0], v4[0:BLK], 0.0)
    for h in range(1, 4):
        out = jnp.where(masks[h], v4[BLK * h:BLK * (h + 1)], out)
    return out


def _regroup(load_half, dst_ref, stage_ref, n, dil):
    for hlf in range(2):
        stage_ref[hlf] = load_half(hlf)

    def residue(r, carry):
        for hlf in range(2):
            dst_ref[pl.ds(pl.multiple_of(r * n, BLK), n), pl.ds(128 * hlf, 128)] = (
                stage_ref[hlf, pl.ds(r, n, stride=dil), :].astype(dst_ref.dtype))
        return carry

    lax.fori_loop(0, dil, residue, 0)


def _store_block(nat_ref, r, i, val, dil):
    for hlf in range(2):
        nat_ref[hlf, pl.ds(r + dil * BLK * i, BLK, stride=dil), :] = val[:, 128 * hlf:128 * (hlf + 1)]


def _for_blocks(block, dil, nblk):
    if dil == 1:
        block(0, 0, True)
        block(0, 1, False)

        def pair(k, carry):
            block(0, 2 * k, False)
            block(0, 2 * k + 1, False)
            return carry

        lax.fori_loop(1, nblk // 2, pair, 0)
    else:
        def residues(k, carry):
            block(2 * k, 0, True)
            block(2 * k + 1, 0, True)
            if nblk > 1:
                def loop(i, c):
                    block(2 * k, i, False)
                    block(2 * k + 1, i, False)
                    return c
                lax.fori_loop(1, nblk, loop, 0)
            return carry

        lax.fori_loop(0, dil // 2, residues, 0)


def _attn_forward(qkv, g, nbat):
    t = qkv.shape[1]
    seq = t // nbat
    dil = GROUPS[g][1]
    n = seq // dil
    nblk = n // BLK
    qkv4 = qkv.reshape(3, 3, t, SLAB)

    def body(qkv_ref, bias_ref, ol_ref, *scratch):
        masks = _head_masks((BLK, SLAB))
        if dil > 1:
            stage, qd, kd, vd, nat_o, nat_l = scratch
            for which, dst in enumerate((qd, kd, vd)):
                _regroup(lambda hlf, which=which: qkv_ref[which, 0, :, pl.ds(128 * hlf, 128)].astype(F32), dst, stage, n, dil)
        else:
            qd, kd, vd = qkv_ref.at[0, 0], qkv_ref.at[1, 0], qkv_ref.at[2, 0]

        def block(r, i, first):
            base = r * n
            qs = pl.ds(pl.multiple_of(base + i * BLK, BLK), BLK)
            ks = pl.ds(pl.multiple_of(base, BLK), BLK) if first else pl.ds(pl.multiple_of(base + (i - 1) * BLK, BLK), 2 * BLK)
            q, kk, vv = qd[qs, :], kd[ks, :], vd[ks, :]
            bias = bias_ref[:, pl.ds(BLK, BLK)] if first else bias_ref[...]
            s = _nt(_stack_heads(q, masks), kk) * 0.125 + bias
            m = jnp.max(s, axis=1, keepdims=True)
            p = jnp.exp(s - m)
            den = jnp.sum(p, axis=1, keepdims=True)
            out = _unstack_heads(_nn((p * (1.0 / den)).astype(BF16), vv), masks)
            lse = _unstack_heads(jnp.broadcast_to(m + jnp.log(den), (4 * BLK, SLAB)), masks)
            if dil > 1:
                _store_block(nat_o, r, i, out, dil)
                _store_block(nat_l, r, i, lse, dil)
            else:
                ol_ref[0, qs, :] = out
                ol_ref[1, qs, :] = lse

        _for_blocks(block, dil, nblk)
        if dil > 1:
            for hlf in range(2):
                ol_ref[0, :, pl.ds(128 * hlf, 128)] = nat_o[hlf]
                ol_ref[1, :, pl.ds(128 * hlf, 128)] = nat_l[hlf]

    scratch = []
    if dil > 1:
        scratch = [pltpu.VMEM((2, seq, 128), F32)] + [pltpu.VMEM((seq, SLAB), BF16)] * 3 + [pltpu.VMEM((2, seq, 128), F32)] * 2
    return _pcall(
        body, grid=(nbat,), out_shape=jax.ShapeDtypeStruct((2, t, SLAB), F32),
        in_specs=[pl.BlockSpec((3, 1, seq, SLAB), lambda b: (0, g, b, 0)),
                  pl.BlockSpec((4 * BLK, 2 * BLK), lambda b: (0, 0))],
        out_specs=pl.BlockSpec((2, seq, SLAB), lambda b: (0, b, 0)), scratch_shapes=scratch,
        name=f"attn_forward_{g}", compiler_params=_params(40, ("parallel",)))(qkv4, _bias_tables(g))


def _attn_backward(qkv, do_attn, ol_tot, dproj, g, nbat):
    t = qkv.shape[1]
    seq = t // nbat
    dil = GROUPS[g][1]
    n = seq // dil
    nblk = n // BLK
    qkv4 = qkv.reshape(3, 3, t, SLAB)
    dp4 = dproj.reshape(DP_SLABS // 3, 3, t, SLAB)

    def body(qkv_ref, do_ref, ol_ref, bias_ref, dp_in, dp_ref, gb_ref, dk_acc, dv_acc, *scratch):
        del dp_in
        masks = _head_masks((BLK, SLAB))

        @pl.when(pl.program_id(0) == 0)
        def _():
            gb_ref[...] = jnp.zeros_like(gb_ref)

        dk_acc[...] = jnp.zeros_like(dk_acc)
        dv_acc[...] = jnp.zeros_like(dv_acc)
        if dil > 1:
            stage, qd, kd, vd, dod, prodd, lsed, nat = scratch
            lanes = lambda hlf: pl.ds(128 * hlf, 128)
            for which, dst in enumerate((qd, kd, vd)):
                _regroup(lambda hlf, which=which: qkv_ref[which, 0, :, lanes(hlf)].astype(F32), dst, stage, n, dil)
            _regroup(lambda hlf: do_ref[:, lanes(hlf)].astype(F32), dod, stage, n, dil)
            _regroup(lambda hlf: do_ref[:, lanes(hlf)].astype(F32) * ol_ref[0, :, lanes(hlf)], prodd, stage, n, dil)
            _regroup(lambda hlf: ol_ref[1, :, lanes(hlf)], lsed, stage, n, dil)
        else:
            qd, kd, vd = qkv_ref.at[0, 0], qkv_ref.at[1, 0], qkv_ref.at[2, 0]

        def block(r, i, first):
            base = r * n
            qs = pl.ds(pl.multiple_of(base + i * BLK, BLK), BLK)
            ks = pl.ds(pl.multiple_of(base, BLK), BLK) if first else pl.ds(pl.multiple_of(base + (i - 1) * BLK, BLK), 2 * BLK)
            q, kk, vv = qd[qs, :], kd[ks, :], vd[ks, :]
            if dil > 1:
                do, prod, lse = dod[qs, :], prodd[qs, :], lsed[qs, :]
            else:
                do = do_ref[qs, :]
                prod = do.astype(F32) * ol_ref[0, qs, :]
                lse = ol_ref[1, qs, :]
            q4, do4 = _stack_heads(q, masks), _stack_heads(do, masks)
            bias = bias_ref[:, pl.ds(BLK, BLK)] if first else bias_ref[...]
            lse4 = jnp.concatenate([lse[:, 64 * h:64 * h + 1] for h in range(4)], axis=0)
            delta4 = jnp.concatenate([jnp.sum(jnp.where(masks[h], prod, 0.0), axis=1, keepdims=True) for h in range(4)], axis=0)
            p = jnp.exp(_nt(q4, kk) * 0.125 + bias - lse4)
            ds = (p * (_nt(do4, vv) - delta4)).astype(BF16)
            dv_acc[ks, :] += _tn(p.astype(BF16), do4)
            dk_acc[ks, :] += _tn(ds, q4) * 0.125
            dq = _unstack_heads(_nn(ds, kk), masks) * 0.125
            if dil > 1:
                _store_block(nat, r, i, dq, dil)
            else:
                dp_ref[0, 0, qs, :] = dq.astype(BF16)
            gb_ref[0] += _part8(dq)

        _for_blocks(block, dil, nblk)
        gb_ref[1] += _part8(dk_acc[...])
        gb_ref[2] += _part8(dv_acc[...])
        if dil > 1:
            def flush(which):
                for hlf in range(2):
                    dp_ref[which, 0, :, pl.ds(128 * hlf, 128)] = nat[hlf].astype(BF16)

            def to_token_order(acc_ref):
                def residue(r, carry):
                    for hlf in range(2):
                        nat[hlf, pl.ds(r, n, stride=dil), :] = acc_ref[pl.ds(pl.multiple_of(r * n, BLK), n), pl.ds(128 * hlf, 128)]
                    return carry
                lax.fori_loop(0, dil, residue, 0)

            flush(0)
            to_token_order(dk_acc)
            flush(1)
            to_token_order(dv_acc)
            flush(2)
        else:
            dp_ref[1, 0] = dk_acc[...].astype(BF16)
            dp_ref[2, 0] = dv_acc[...].astype(BF16)

    scratch = [pltpu.VMEM((seq, SLAB), F32)] * 2
    if dil > 1:
        scratch += ([pltpu.VMEM((2, seq, 128), F32)] + [pltpu.VMEM((seq, SLAB), BF16)] * 4 + [pltpu.VMEM((seq, SLAB), F32)] * 2
                    + [pltpu.VMEM((2, seq, 128), F32)])
    dp, gb = _pcall(
        body, grid=(nbat,),
        out_shape=(jax.ShapeDtypeStruct(dp4.shape, BF16), jax.ShapeDtypeStruct((3, 8, SLAB), F32)),
        in_specs=[pl.BlockSpec((3, 1, seq, SLAB), lambda b: (0, g, b, 0)),
                  pl.BlockSpec((seq, SLAB), lambda b: (b, 0)),
                  pl.BlockSpec((2, seq, SLAB), lambda b: (0, b, 0)),
                  pl.BlockSpec((4 * BLK, 2 * BLK), lambda b: (0, 0)), ANY],
        out_specs=(pl.BlockSpec((3, 1, seq, SLAB), lambda b: (DP_SLABS // 9 - 1, g, b, 0)),
                   pl.BlockSpec((3, 8, SLAB), lambda b: (0, 0, 0))),
        scratch_shapes=scratch, input_output_aliases={4: 0}, name=f"attn_backward_{g}",
        compiler_params=_params(48, ("arbitrary",)))(qkv4, do_attn, ol_tot, _bias_tables(g), dp4)
    return dp.reshape(DP_SLABS, t, SLAB), gb


def _mid(rest, ols, x, tgt, ada, cw, b_out, ln_g, ln_b, w_pa_t, w_pb, w_out, tm=256):
    t = x.shape[0]
    nbat = ada.shape[0]
    nt = t // tm
    tps = nt // nbat

    def body(rest_ref, halo_ref, ol0_ref, ol1_ref, ol2_ref, x_ref, t_ref, ada_ref, cw_ref, bout_ref, lng_ref, lnb_ref,
             wpat_ref, wpb_ref, wout_ref,
             dp_ref, gx0_ref, doa_ref, olt_ref, mg_ref, dof_ref, bbs_ref, dyc_ref, a_ref, dya_ref,
             gbr_ref, sv_ref, dgate_ref, carry_ref, keep_ref):
        i = pl.program_id(0)
        ti = nt - 1 - i
        pos = ti % tps

        @pl.when(i == 0)
        def _():
            gbr_ref[...] = jnp.zeros_like(gbr_ref)
            sv_ref[...] = jnp.zeros_like(sv_ref)

        @pl.when(pos == tps - 1)
        def _():
            dgate_ref[...] = jnp.zeros_like(dgate_ref)
            carry_ref[...] = jnp.zeros_like(carry_ref)

        row = lax.broadcasted_iota(jnp.int32, (tm, SLAB), 0)
        halo_on = (pos > 0).astype(F32)

        def cols(s):
            return pl.ds(SLAB * s, SLAB)

        l0, l1, l2 = ol0_ref[1], ol1_ref[1], ol2_ref[1]
        mx = jnp.maximum(jnp.maximum(l0, l1), l2)
        e0, e1, e2 = jnp.exp(l0 - mx), jnp.exp(l1 - mx), jnp.exp(l2 - mx)
        den = e0 + e1 + e2
        o_attn = (e0 * ol0_ref[0] + e1 * ol1_ref[0] + e2 * ol2_ref[0]) * (1.0 / den)
        olt_ref[0] = o_attn
        olt_ref[1] = mx + jnp.log(den)
        z_a = rest_ref[R_ZA]
        sg_za = _sigmoid(z_a)
        a_ref[...] = (o_attn * z_a * sg_za).astype(BF16)
        y_attn = _nt(a_ref[...], wpat_ref[...])

        for s in range(4):
            u = rest_ref[R_GC + s] * rest_ref[R_UX + s]
            hu = halo_ref[R_GC + s] * halo_ref[R_UX + s] * halo_on
            u1 = jnp.where(row == 0, hu[7:8], pltpu.roll(u, 1, 0))
            u2 = jnp.where(row == 0, hu[6:7], jnp.where(row == 1, hu[7:8], pltpu.roll(u, 2, 0)))
            conv = cw_ref[0:1, cols(s)] * u2 + cw_ref[1:2, cols(s)] * u1 + cw_ref[2:3, cols(s)] * u
            zc = rest_ref[R_ZC + s]
            sg = _sigmoid(zc)
            keep_ref[2, :, cols(s)], keep_ref[3, :, cols(s)], keep_ref[4, :, cols(s)], keep_ref[5, :, cols(s)] = u1, u2, conv, sg
            bbs_ref[:, cols(s)] = (rest_ref[R_GB + s] * conv * (zc * sg)).astype(BF16)
        y_conv = _nn(bbs_ref[...], wpb_ref[...])

        for s in range(4):
            s_a, s_b = _sigmoid(rest_ref[R_GA + s]), _sigmoid(rest_ref[R_GBM + s])
            keep_ref[0, :, cols(s)], keep_ref[1, :, cols(s)] = s_a, s_b
            mg_ref[:, cols(s)] = (s_a * y_attn[:, SLAB * s:SLAB * (s + 1)] + s_b * y_conv[:, SLAB * s:SLAB * (s + 1)]).astype(BF16)
        o = _nn(mg_ref[...], wout_ref[...]) + bout_ref[...]
        gate = ada_ref[0, 2:3, :]
        r = ALPHA * x_ref[...] + gate * o
        mu = jnp.mean(r, axis=1, keepdims=True)
        rc = r - mu
        rstd = lax.rsqrt(jnp.mean(rc * rc, axis=1, keepdims=True) + LN_EPS)
        xhat = rc * rstd
        err = xhat * lng_ref[...] + lnb_ref[...] - t_ref[...]
        sv_ref[6] += _part8(err * err)
        dy = err * (1.0 / D)
        sv_ref[0] += _part8(dy * xhat)
        sv_ref[1] += _part8(dy)
        dxh = dy * lng_ref[...]
        dr = rstd * (dxh - jnp.mean(dxh, axis=1, keepdims=True) - xhat * jnp.mean(dxh * xhat, axis=1, keepdims=True))
        gx0_ref[...] = ALPHA * dr
        dgate_ref[0] += _part8(dr * o)
        do_ = dr * gate
        sv_ref[2] += _part8(do_)
        dof_ref[...] = do_.astype(BF16)
        dmerged = _nt(dof_ref[...], wout_ref[...])
        for s in range(4):
            s_a, s_b = keep_ref[0, :, cols(s)], keep_ref[1, :, cols(s)]
            dm = dmerged[:, SLAB * s:SLAB * (s + 1)]
            ya, yc = y_attn[:, SLAB * s:SLAB * (s + 1)], y_conv[:, SLAB * s:SLAB * (s + 1)]
            dya_ref[:, cols(s)] = (dm * s_a).astype(BF16)
            dyc_ref[:, cols(s)] = (dm * s_b).astype(BF16)
            dga = dm * ya * s_a * (1.0 - s_a)
            dgb = dm * yc * s_b * (1.0 - s_b)
            dp_ref[R_GA + s] = dga.astype(BF16)
            dp_ref[R_GBM + s] = dgb.astype(BF16)
            gbr_ref[R_GA + s] += _part8(dga)
            gbr_ref[R_GBM + s] += _part8(dgb)

        da = _nn(dya_ref[...], wpat_ref[...])
        doa_ref[...] = (da * z_a * sg_za).astype(BF16)
        dza = da * o_attn * (sg_za * (1.0 + z_a * (1.0 - sg_za)))
        dp_ref[R_ZA] = dza.astype(BF16)
        gbr_ref[R_ZA] += _part8(dza)

        dbb = _nt(dyc_ref[...], wpb_ref[...])
        for s in range(4):
            ux, gc, zc = rest_ref[R_UX + s], rest_ref[R_GC + s], rest_ref[R_ZC + s]
            u = gc * ux
            u1, u2, conv, sg = keep_ref[2, :, cols(s)], keep_ref[3, :, cols(s)], keep_ref[4, :, cols(s)], keep_ref[5, :, cols(s)]
            gb = rest_ref[R_GB + s]
            d_b = dbb[:, SLAB * s:SLAB * (s + 1)]
            szc = zc * sg
            dgb_ = d_b * conv * szc
            dconv = d_b * gb * szc
            dzc = d_b * gb * conv * (sg * (1.0 + zc * (1.0 - sg)))
            sv_ref[3, :, cols(s)] += _part8(dconv * u2)
            sv_ref[4, :, cols(s)] += _part8(dconv * u1)
            sv_ref[5, :, cols(s)] += _part8(dconv * u)
            nxt = carry_ref[:, cols(s)]
            d1 = jnp.where(row == tm - 1, nxt[0:1], pltpu.roll(dconv, tm - 1, 0))
            d2 = jnp.where(row == tm - 1, nxt[1:2], jnp.where(row == tm - 2, nxt[0:1], pltpu.roll(dconv, tm - 2, 0)))
            carry_ref[:, cols(s)] = dconv[0:8]
            du = cw_ref[2:3, cols(s)] * dconv + cw_ref[1:2, cols(s)] * d1 + cw_ref[0:1, cols(s)] * d2
            dgc, dux = du * ux, du * gc
            for slab, val in ((R_GB + s, dgb_), (R_ZC + s, dzc), (R_GC + s, dgc), (R_UX + s, dux)):
                dp_ref[slab] = val.astype(BF16)
                gbr_ref[slab] += _part8(val)

    def tile(i):
        return nt - 1 - i

    row_blk = lambda i: (tile(i), 0)
    slab_blk = lambda i: (0, tile(i), 0)
    const2 = lambda i: (0, 0)
    const3 = lambda i: (0, 0, 0)
    in_specs = [
        pl.BlockSpec((N_REST, tm, SLAB), slab_blk),
        pl.BlockSpec((N_REST, 8, SLAB), lambda i: (0, jnp.maximum(tile(i) * (tm // 8) - 1, 0), 0)),
        pl.BlockSpec((2, tm, SLAB), slab_blk), pl.BlockSpec((2, tm, SLAB), slab_blk), pl.BlockSpec((2, tm, SLAB), slab_blk),
        pl.BlockSpec((tm, D), row_blk), pl.BlockSpec((tm, D), row_blk),
        pl.BlockSpec((1, 3, D), lambda i: (tile(i) // tps, 0, 0)),
        pl.BlockSpec((3, D), const2), pl.BlockSpec((1, D), const2), pl.BlockSpec((1, D), const2), pl.BlockSpec((1, D), const2),
        pl.BlockSpec((D, SLAB), const2), pl.BlockSpec((D, D), const2), pl.BlockSpec((D, D), const2)]
    bf_rows = lambda: jax.ShapeDtypeStruct((t, D), BF16)
    out_shape = (
        jax.ShapeDtypeStruct((DP_SLABS, t, SLAB), BF16), jax.ShapeDtypeStruct((t, D), F32),
        jax.ShapeDtypeStruct((t, SLAB), BF16), jax.ShapeDtypeStruct((2, t, SLAB), F32),
        bf_rows(), bf_rows(), bf_rows(), bf_rows(), jax.ShapeDtypeStruct((t, SLAB), BF16), bf_rows(),
        jax.ShapeDtypeStruct((N_REST, 8, SLAB), F32), jax.ShapeDtypeStruct((7, 8, D), F32),
        jax.ShapeDtypeStruct((nbat, 8, D), F32))
    out_specs = (
        pl.BlockSpec((N_REST, tm, SLAB), slab_blk), pl.BlockSpec((tm, D), row_blk),
        pl.BlockSpec((tm, SLAB), row_blk), pl.BlockSpec((2, tm, SLAB), slab_blk),
        pl.BlockSpec((tm, D), row_blk), pl.BlockSpec((tm, D), row_blk), pl.BlockSpec((tm, D), row_blk),
        pl.BlockSpec((tm, D), row_blk), pl.BlockSpec((tm, SLAB), row_blk), pl.BlockSpec((tm, D), row_blk),
        pl.BlockSpec((N_REST, 8, SLAB), const3), pl.BlockSpec((7, 8, D), const3),
        pl.BlockSpec((1, 8, D), lambda i: (tile(i) // tps, 0, 0)))
    return _pcall(body, grid=(nt,), out_shape=out_shape, in_specs=in_specs, out_specs=out_specs,
                  scratch_shapes=[pltpu.VMEM((8, D), F32), pltpu.VMEM((6, tm, D), F32)], name="mid",
                  compiler_params=_params(56, ("arbitrary",)))(
        rest, rest, *ols, x, tgt, ada, cw, b_out, ln_g, ln_b, w_pa_t, w_pb, w_out)


def _tn_matmul(lhs, rhs, lhs_spec, n_steps, out_rows, out_index, name, after):
    t, n = rhs.shape

    def body(l_ref, r_ref, after_ref, o_ref):
        del after_ref
        o_ref[...] = _tn(l_ref[0] if len(l_ref.shape) == 3 else l_ref[...], r_ref[...])

    return _pcall(body, grid=(n_steps,), out_shape=jax.ShapeDtypeStruct((out_rows, n), F32),
                  in_specs=[lhs_spec, pl.BlockSpec((t, n), lambda j: (0, 0)), ANY],
                  out_specs=pl.BlockSpec((SLAB, n), out_index), name=name,
                  compiler_params=_params(48, ("parallel",)))(lhs, rhs, after)


def _grad_rows_2d(lhs, rhs, name, after):
    t, k = lhs.shape
    return _tn_matmul(lhs, rhs, pl.BlockSpec((t, SLAB), lambda j: (0, j)), k // SLAB, k, lambda j: (j, 0), name, after)


def _w_row_block(j):
    return (j + N_QKV) % N_SLAB


def _dp_slab(j):
    return jnp.where(j < N_REST, j, j + 2)


def _grad_w_in_t(dproj, h):
    t = h.shape[0]
    return _tn_matmul(dproj, h, pl.BlockSpec((1, t, SLAB), lambda j: (_dp_slab(j), 0, 0)), N_SLAB, D_IN,
                      lambda j: (_w_row_block(j), 0), "grad_w_in", h)


def _grad_h(dproj, w_in_t, gx0, x, ada, after, tm=512):
    t = x.shape[0]
    nbat = ada.shape[0]
    tps = (t // nbat) // tm

    def body(dp_ref, w_ref, gx0_ref, x_ref, ada_ref, after_ref, gx_ref, dss_ref):
        del after_ref
        i = pl.program_id(0)
        dh = None
        for j in range(N_SLAB):
            slab = j if j < N_REST else j + 2
            part = _nn(dp_ref[slab], w_ref[pl.ds(SLAB * ((j + N_QKV) % N_SLAB), SLAB), :])
            dh = part if dh is None else dh + part
        gx_ref[...] = gx0_ref[...] + dh * (1.0 + ada_ref[0, 1:2, :])

        @pl.when((i % tps) == 0)
        def _():
            dss_ref[...] = jnp.zeros_like(dss_ref)

        dss_ref[0, 0] += _part8(dh)
        dss_ref[0, 1] += _part8(dh * x_ref[...])

    return _pcall(
        body, grid=(t // tm,),
        out_shape=(jax.ShapeDtypeStruct((t, D), F32), jax.ShapeDtypeStruct((nbat, 2, 8, D), F32)),
        in_specs=[pl.BlockSpec((DP_SLABS, tm, SLAB), lambda i: (0, i, 0)),
                  pl.BlockSpec((D_IN, D), lambda i: (0, 0), pipeline_mode=pl.Buffered(1)),
                  pl.BlockSpec((tm, D), lambda i: (i, 0)), pl.BlockSpec((tm, D), lambda i: (i, 0)),
                  pl.BlockSpec((1, 3, D), lambda i: (i // tps, 0, 0)), ANY],
        out_specs=(pl.BlockSpec((tm, D), lambda i: (i, 0)),
                   pl.BlockSpec((1, 2, 8, D), lambda i: (i // tps, 0, 0, 0))),
        name="grad_h", compiler_params=_params(60, ("arbitrary",)))(dproj, w_in_t, gx0, x, ada, after)


def _chip(m):
    x, y, _ = _my_position()
    return (x ^ ((m >> 1) & 1), y ^ (m & 1))


def _exchange_siblings(grads, after, name):
    n = len(grads)

    def body(*refs):
        copies = _sibling_copies(refs[:n], refs[n + 1:2 * n + 1], refs[2 * n + 1], refs[2 * n + 2])
        for cp in copies:
            cp.start()
        for cp in copies:
            cp.wait()

    return _pcall(body, out_shape=tuple(_sibling_zones(grads)), in_specs=[ANY] * (n + 1), out_specs=(ANY,) * n,
                  name=name, scratch_shapes=[pltpu.SemaphoreType.DMA((4 * n,))] * 2)(*grads, after)


def _sibling_zones(grads):
    return [jax.ShapeDtypeStruct((4, g.shape[0] // N_DEV, g.shape[1]), g.dtype) for g in grads]


def _sibling_copies(srcs, lands, send_sems, recv_sems):
    x, y, c = _my_position()
    copies = []
    for a, (src, land) in enumerate(zip(srcs, lands)):
        rows = land.shape[1]
        for m in range(4):
            dev = _flat(*_chip(m), 1 - c)
            copies.append(pltpu.make_async_remote_copy(
                src_ref=src.at[pl.ds(pl.multiple_of(dev * rows, 8), rows), :], dst_ref=land.at[m],
                send_sem=send_sems.at[4 * a + m], recv_sem=recv_sems.at[4 * a + m], device_id=(x, y, 1 - c),
                device_id_type=MESH))
    return copies


def _chip_copies(srcs, lands, send_sems, recv_sems):
    _, _, c = _my_position()
    return [pltpu.make_async_remote_copy(
        src_ref=srcs[a].at[m - 1], dst_ref=lands[a].at[m - 1], send_sem=send_sems.at[3 * a + m - 1],
        recv_sem=recv_sems.at[3 * a + m - 1], device_id=(*_chip(m), c), device_id_type=MESH)
        for a in range(len(srcs)) for m in range(1, 4)]


HBM = pl.BlockSpec(memory_space=pltpu.HBM)
SEM = pl.BlockSpec(memory_space=pltpu.SEMAPHORE)
SPLIT_COPY = pltpu.CompilerParams(has_side_effects=pltpu.SideEffectType.DATAFLOW_SIDE_EFFECTING)


def _start_copies(make_copies, n_sems, srcs, zones, name):
    n = len(srcs)

    def body(*refs):
        for cp in make_copies(refs[:n], refs[n:2 * n], refs[2 * n], refs[2 * n + 1]):
            cp.start()
        refs[-1][...] = jnp.zeros_like(refs[-1])

    hbm = tuple(pltpu.HBM(b.shape, b.dtype) for b in list(srcs) + list(zones))
    out_shape = (pltpu.SemaphoreType.DMA((n_sems,)), pltpu.SemaphoreType.DMA((n_sems,))) + hbm + (jax.ShapeDtypeStruct((8, 128), F32),)
    operands = [pltpu.with_memory_space_constraint(b, pltpu.HBM) for b in srcs]
    operands += [pltpu.with_memory_space_constraint(lax.empty(z.shape, z.dtype), pltpu.HBM) for z in zones]
    res = _pcall(body, out_shape=out_shape, in_specs=[HBM] * (2 * n), out_specs=(SEM, SEM) + (HBM,) * (2 * n) + (VMEM,),
                 input_output_aliases={i: 2 + i for i in range(2 * n)}, name=name, compiler_params=SPLIT_COPY)(*operands)
    return (res[0], res[1], res[2:2 + n], res[2 + n:2 + 2 * n]), res[-1]


def _wait_copies(make_copies, flight, after, name):
    send_sems, recv_sems, srcs, zones = flight
    n = len(srcs)

    def body(*refs):
        for cp in make_copies(refs[:n], refs[n:2 * n], refs[2 * n], refs[2 * n + 1]):
            cp.wait_send()
            cp.wait_recv()

    hbm = tuple(pltpu.HBM(b.shape, b.dtype) for b in list(srcs) + list(zones))
    res = _pcall(body, out_shape=hbm, in_specs=[HBM] * (2 * n) + [SEM, SEM, ANY], out_specs=(HBM,) * (2 * n),
                 input_output_aliases={i: i for i in range(2 * n)}, name=name, compiler_params=SPLIT_COPY)(
        *srcs, *zones, send_sems, recv_sems, after)
    return res[:n], res[n:]


def _pair_sums(devs, grads, lands, n_steps, name):
    n = len(grads)
    rows = [l.shape[1] for l in lands]
    rbs = [r // n_steps for r in rows]

    def body(devs_ref, *refs):
        del devs_ref
        g_refs, land_refs, outs = refs[:4 * n], refs[4 * n:5 * n], refs[5 * n:]
        for a in range(n):
            outs[2 * a][...] = g_refs[4 * a][...] + land_refs[a][0]
            for m in range(1, 4):
                outs[2 * a + 1][m - 1] = (g_refs[4 * a + m][...] + land_refs[a][m]).astype(BF16)

    def block_of(m, per_dev):
        return lambda i, devs_ref: (devs_ref[m] * per_dev + i, 0)

    in_specs = [pl.BlockSpec((rb, l.shape[2]), block_of(m, n_steps)) for rb, l in zip(rbs, lands) for m in range(4)]
    in_specs += [pl.BlockSpec((4, rb, l.shape[2]), lambda i, devs_ref: (0, i, 0)) for rb, l in zip(rbs, lands)]
    out_shape, out_specs = [], []
    for rb, l in zip(rbs, lands):
        out_shape += [jax.ShapeDtypeStruct(l.shape[1:], F32), jax.ShapeDtypeStruct((3,) + l.shape[1:], BF16)]
        out_specs += [pl.BlockSpec((rb, l.shape[2]), lambda i, devs_ref: (i, 0)),
                      pl.BlockSpec((3, rb, l.shape[2]), lambda i, devs_ref: (0, i, 0))]
    grid_spec = pltpu.PrefetchScalarGridSpec(num_scalar_prefetch=1, grid=(n_steps,), in_specs=in_specs, out_specs=tuple(out_specs))
    res = _pcall(body, grid_spec=grid_spec, out_shape=tuple(out_shape), name=name,
                 compiler_params=_params(48, ("parallel",)))(devs, *[g for g in grads for _ in range(4)], *lands)
    return res[0::2], res[1::2]


def _final_sums(mine, lands, n_steps, name):
    n = len(mine)
    rbs = [o.shape[0] // n_steps for o in mine]

    def body(*refs):
        mine_refs, land_refs, outs = refs[:n], refs[n:2 * n], refs[2 * n:]
        for a in range(n):
            tot = mine_refs[a][...]
            for m in range(3):
                tot = tot + land_refs[a][m].astype(F32)
            outs[a][...] = tot

    in_specs = ([pl.BlockSpec((rb, o.shape[1]), lambda i: (i, 0)) for rb, o in zip(rbs, mine)]
                + [pl.BlockSpec((3, rb, o.shape[1]), lambda i: (0, i, 0)) for rb, o in zip(rbs, mine)])
    out_specs = tuple(pl.BlockSpec((rb, o.shape[1]), lambda i: (i, 0)) for rb, o in zip(rbs, mine))
    out_shape = tuple(jax.ShapeDtypeStruct(o.shape, F32) for o in mine)
    return _pcall(body, grid=(n_steps,), out_shape=out_shape, in_specs=in_specs, out_specs=out_specs, name=name,
                  compiler_params=_params(32, ("parallel",)))(*mine, *lands)


def _reduce_scatter_begin(big, small_after_start):
    c = lax.axis_index("c")
    devs = jnp.stack([_flat(*_chip(m), c) for m in range(4)]).astype(jnp.int32)
    flight, token = _start_copies(_sibling_copies, 4, [big], _sibling_zones([big]), "siblings_start")
    small = small_after_start(token)
    (big,), big_lands = _wait_copies(_sibling_copies, flight, small[-1], "siblings_wait")
    big_mine, big_send = _pair_sums(devs, [big], big_lands, 4, "pair_sums_w_in")
    big_flight, token = _start_copies(_chip_copies, 3, list(big_send), list(big_send), "chips_start_w_in")
    small_lands = _exchange_siblings(small, token, "exchange_siblings_rest")
    small_mine, small_send = _pair_sums(devs, small, small_lands, 1, "pair_sums_rest")
    small_flight, token = _start_copies(_chip_copies, 3 * len(small), list(small_send), list(small_send), "chips_start_rest")
    return (big_flight, small_flight, list(big_mine) + list(small_mine)), token


def _reduce_scatter_end(state, after):
    big_flight, small_flight, mine = state
    _, big_got = _wait_copies(_chip_copies, big_flight, after, "chips_wait_w_in")
    _, small_got = _wait_copies(_chip_copies, small_flight, after, "chips_wait_rest")
    small = _final_sums(mine[1:], small_got, 1, "final_sums_rest")
    return (mine[0], big_got[0]), list(small)


def _adamw(w, g, m, v):
    m_new = B1 * m + (1.0 - B1) * g
    v_new = B2 * v + (1.0 - B2) * (g * g)
    m_hat = m_new / (1.0 - B1 ** STEP)
    v_hat = v_new / (1.0 - B2 ** STEP)
    delta = -LR * (m_hat / (jnp.sqrt(v_hat) + EPS) + WD * w)
    return delta, m_new, v_new


def _final_sum_adam_rows(mine, land, w, m, v, n_steps, name):
    rows, ncol = w.shape
    blk = pl.BlockSpec((rows // n_steps, ncol), lambda i: (i, 0))

    def body(mine_ref, land_ref, w_ref, m_ref, v_ref, g_ref, d_ref, mo_ref, vo_ref):
        g = mine_ref[...]
        for k in range(3):
            g = g + land_ref[k].astype(F32)
        g_ref[...] = g
        d_ref[...], mo_ref[...], vo_ref[...] = _adamw(w_ref[...], g, m_ref[...], v_ref[...])

    shape = jax.ShapeDtypeStruct(w.shape, F32)
    return _pcall(body, grid=(n_steps,), out_shape=(shape,) * 4,
                  in_specs=[blk, pl.BlockSpec((3, rows // n_steps, ncol), lambda i: (0, i, 0)), blk, blk, blk],
                  out_specs=(blk,) * 4, name=name, compiler_params=_params(32, ("parallel",)))(mine, land, w, m, v)


def _adam_transposed(g_t, w, m, v, name):
    n, k = g_t.shape
    rb = min(k, 128)

    def body(gt_ref, w_ref, m_ref, v_ref, g_ref, d_ref, mo_ref, vo_ref):
        for src, skip, dst, size in _column_chunks(n):
            sl = pl.ds(dst, size)
            g = gt_ref[pl.ds(src, 128), :].T[:, skip:]
            delta, m_new, v_new = _adamw(w_ref[:, sl], g, m_ref[:, sl], v_ref[:, sl])
            g_ref[:, sl], d_ref[:, sl], mo_ref[:, sl], vo_ref[:, sl] = g, delta, m_new, v_new

    shape = jax.ShapeDtypeStruct(w.shape, F32)
    rows = pl.BlockSpec((rb, n), lambda i: (i, 0))
    return _pcall(body, grid=(k // rb,), out_shape=(shape,) * 4,
                  in_specs=[pl.BlockSpec((n, rb), lambda i: (0, i)), rows, rows, rows], out_specs=(rows,) * 4, name=name,
                  compiler_params=_params(32, ("parallel",)))(g_t, w, m, v)


def _adam_many(items, name):
    n = len(items)

    def body(*refs):
        ins, outs = refs[:4 * n], refs[4 * n:]
        for a in range(n):
            w_ref, g_ref, m_ref, v_ref = ins[4 * a:4 * a + 4]
            delta, m_new, v_new = _adamw(w_ref[...], g_ref[...], m_ref[...], v_ref[...])
            outs[3 * a][...], outs[3 * a + 1][...], outs[3 * a + 2][...] = delta, m_new, v_new

    out_shape = tuple(jax.ShapeDtypeStruct(it[0].shape, F32) for it in items for _ in range(3))
    flat = [arr for it in items for arr in it]
    res = _pcall(body, grid=(1,), out_shape=out_shape, in_specs=[_whole(a) for a in flat],
                 out_specs=tuple(_whole(o) for o in out_shape), name=name, compiler_params=_params(32))(*flat)
    return [tuple(res[3 * a:3 * a + 3]) for a in range(n)]


def _adam_w_ada(cact_all, dada_mine, w, m, v):
    def body(c_ref, d_ref, w_ref, m_ref, v_ref, g_ref, dl_ref, mo_ref, vo_ref):
        g = _tn(c_ref[...].astype(BF16), d_ref[...].astype(BF16))
        delta, m_new, v_new = _adamw(w_ref[...], g, m_ref[...], v_ref[...])
        g_ref[...], dl_ref[...], mo_ref[...], vo_ref[...] = g, delta, m_new, v_new

    shape = jax.ShapeDtypeStruct(w.shape, F32)
    operands = (cact_all, dada_mine, w, m, v)
    return _pcall(body, grid=(1,), out_shape=(shape,) * 4, in_specs=[_whole(a) for a in operands],
                  out_specs=(_whole(w),) * 4, name="adam_w_ada", compiler_params=_params(32))(*operands)


def kernel(x, c, w_ada, b_ada, w_in, b_in, conv_w, w_proj_attn, w_proj_conv, w_out, b_out, ln_g, ln_b, loss_target, m_w_ada, m_b_ada, m_w_in, m_b_in, m_conv_w, m_w_proj_attn, m_w_proj_conv, m_w_out, m_b_out, m_ln_g, m_ln_b, v_w_ada, v_b_ada, v_w_in, v_b_in, v_conv_w, v_w_proj_attn, v_w_proj_conv, v_w_out, v_b_out, v_ln_g, v_ln_b):
    nbat, seq, _ = x.shape
    t = nbat * seq
    me = _flat(*_my_position())
    x2, tgt2 = x.reshape(t, D), loss_target.reshape(t, D)
    sq = lambda a: a.reshape(a.shape[1:])

    tr = lambda a: a[0].T
    w_in_rows = tr(w_in)
    w_in_t_s = _cast_rows(w_in_rows, 4, "cast_w_in")
    w_pa_t_s, w_pb_s, w_out_s, cact_s, cw_s = _prep(sq(w_proj_attn), sq(w_proj_conv), sq(w_out), c, sq(conv_w))

    ncol = w_ada.shape[2]
    b_ada_mine = lax.dynamic_slice(b_ada, (0, me * ncol), (1, ncol))
    ada_slots, cact_slots, cw_slots = _ada_forward(cact_s, cw_s, sq(w_ada), b_ada_mine)
    cact_all = cact_slots[:, :nbat].reshape(N_DEV * nbat, D)
    cw = cw_slots[:, :3].transpose(1, 0, 2).reshape(3, D)
    ada_all = ada_slots[:, :, :nbat].transpose(1, 2, 0, 3).reshape(N_DEV * nbat, 3, D)
    ada = lax.dynamic_slice(ada_all, (me * nbat, 0, 0), (nbat, 3, D))

    h = _make_h(x2, ada)
    w_in_t, qkv, rest, (w_pa_t, w_pb, w_o) = _project_gather(w_in_t_s, h, b_in.reshape(N_SLAB, 1, SLAB), [w_pa_t_s, w_pb_s, w_out_s])
    ols = [_attn_forward(qkv, g, nbat) for g in range(3)]
    (dproj, gx0, do_attn, ol_tot, merged, do_f, bbs, dyc, a_bf, dya, gb_rest, svec, dgate) = _mid(
        rest, ols, x2, tgt2, ada, cw, b_out, ln_g, ln_b, w_pa_t, w_pb, w_o)

    gb_qkv = []
    for g in range(3):
        dproj, gb = _attn_backward(qkv, do_attn, ol_tot, dproj, g, nbat)
        gb_qkv.append(gb)
    g_w_in_t = _grad_w_in_t(dproj, h)

    def small_grads(token):
        g_w_out = _grad_rows_2d(merged, do_f, "grad_w_out", token)
        g_w_pb = _grad_rows_2d(bbs, dyc, "grad_w_proj_conv", g_w_out)
        g_w_pa_t = _grad_rows_2d(dya, a_bf, "grad_w_proj_attn", g_w_pb)
        return [g_w_out, g_w_pb, g_w_pa_t]

    rs_state, token = _reduce_scatter_begin(g_w_in_t, small_grads)
    grad_x, dss = _grad_h(dproj, w_in_t, gx0, x2, ada, token)

    rows8, tot, g_bada = _small_reduce(gb_rest, gb_qkv, svec, dgate, dss)
    (g_in_mine, g_in_got), (g_out, g_pb, g_pa_t) = _reduce_scatter_end(rs_state, tot)
    loss = tot[0, P_LOSS]
    dada_all = rows8[:, 0, P_DADA:].reshape(N_DEV * nbat, 3 * D)
    dada_mine = lax.dynamic_slice(dada_all, (0, me * ncol), (N_DEV * nbat, ncol))

    g_in_t, d_win_t, nm_win_t, nv_win_t = _final_sum_adam_rows(g_in_mine, g_in_got, w_in_rows, tr(m_w_in), tr(v_w_in), 4, "adam_w_in")
    g_win, d_win, nm_win, nv_win = g_in_t.T, d_win_t.T, nm_win_t.T, nv_win_t.T
    g_wpa, d_wpa, nm_wpa, nv_wpa = _adam_transposed(g_pa_t, sq(w_proj_attn), sq(m_w_proj_attn), sq(v_w_proj_attn), "adam_w_proj_attn")
    g_wada, d_wada, nm_wada, nv_wada = _adam_w_ada(cact_all, dada_mine, sq(w_ada), sq(m_w_ada), sq(v_w_ada))
    g_bin = tot[:, P_BIN:P_BIN + D_IN]
    g_bout = tot[:, P_BOUT:P_BOUT + D]
    g_lng = tot[:, P_LNG:P_LNG + D]
    g_lnb = tot[:, P_LNB:P_LNB + D]
    g_conv = lax.dynamic_slice(tot[:, P_CONV:P_CONV + 3 * D].reshape(3, D), (0, me * cw_s.shape[1]), (3, cw_s.shape[1]))
    upd = _adam_many([
        (sq(w_proj_conv), g_pb, sq(m_w_proj_conv), sq(v_w_proj_conv)),
        (sq(w_out), g_out, sq(m_w_out), sq(v_w_out)),
        (b_ada, g_bada, m_b_ada, v_b_ada), (b_in, g_bin, m_b_in, v_b_in), (sq(conv_w), g_conv, sq(m_conv_w), sq(v_conv_w)),
        (b_out, g_bout, m_b_out, v_b_out), (ln_g, g_lng, m_ln_g, v_ln_g), (ln_b, g_lnb, m_ln_b, v_ln_b)], "adam_rest")
    (d_wpb, nm_wpb, nv_wpb), (d_wout, nm_wout, nv_wout), (d_bada, nm_bada, nv_bada), (d_bin, nm_bin, nv_bin), \
        (d_conv, nm_conv, nv_conv), (d_bout, nm_bout, nv_bout), (d_lng, nm_lng, nv_lng), (d_lnb, nm_lnb, nv_lnb) = upd

    ex = lambda a: a.reshape((1,) + a.shape)
    grads = [ex(g_wada), g_bada, ex(g_win), g_bin, ex(g_conv), ex(g_wpa), ex(g_pb), ex(g_out), g_bout, g_lng, g_lnb]
    deltas = [ex(d_wada), d_bada, ex(d_win), d_bin, ex(d_conv), ex(d_wpa), ex(d_wpb), ex(d_wout), d_bout, d_lng, d_lnb]
    new_m = [ex(nm_wada), nm_bada, ex(nm_win), nm_bin, ex(nm_conv), ex(nm_wpa), ex(nm_wpb), ex(nm_wout), nm_bout, nm_lng, nm_lnb]
    new_v = [ex(nv_wada), nv_bada, ex(nv_win), nv_bin, ex(nv_conv), ex(nv_wpa), ex(nv_wpb), ex(nv_wout), nv_bout, nv_lng, nv_lnb]
    return (loss, grad_x.reshape(x.shape), *grads, *deltas, *new_m, *new_v)
```

```python
import functools

import jax
import jax.numpy as jnp
from jax import lax
from jax.experimental import pallas as pl
from jax.experimental.pallas import tpu as pltpu

F32, BF16 = jnp.float32, jnp.bfloat16
MESH = pl.DeviceIdType.MESH
N_DEV = 8
D = 1024
SLAB = 256
N_QKV, N_REST = 9, 25
N_SLAB = N_QKV + N_REST
D_IN = N_SLAB * SLAB
DP_SLABS = 36
BLK = 128
GROUPS = ((128, 1), (512, 4), (2048, 16))
ALPHA = 2.0 ** 0.25
LN_EPS = 1e-5
LR, B1, B2, EPS, WD, STEP = 0.001, 0.9, 0.999, 1e-08, 0.01, 10
R_ZA, R_UX, R_GB, R_GC, R_ZC, R_GA, R_GBM = 0, 1, 5, 9, 13, 17, 21
P_BIN, P_BOUT, P_LNG, P_LNB, P_CONV, P_LOSS, P_DADA = 0, 8704, 9728, 10752, 11776, 14848, 14976
MIB = 1024 * 1024


def _pcall(body, *, out_shape, out_specs=None, **kw):
    def pin_out(shape, spec):
        blocked = isinstance(shape, jax.ShapeDtypeStruct) and getattr(spec, "block_shape", None) is not None
        return pltpu.HBM(shape.shape, shape.dtype) if blocked else shape

    n_scalar = 0
    if out_specs is None:
        specs = kw["grid_spec"].out_specs
        n_scalar = kw["grid_spec"].num_scalar_prefetch
    else:
        kw["out_specs"] = specs = out_specs
    if isinstance(out_shape, (tuple, list)):
        out_shape = tuple(pin_out(s, p) for s, p in zip(out_shape, specs))
    else:
        out_shape = pin_out(out_shape, specs)
    call = pl.pallas_call(body, out_shape=out_shape, **kw)

    def run(*operands):
        def pin(o):
            is_data = jnp.issubdtype(o.dtype, jnp.floating) or jnp.issubdtype(o.dtype, jnp.integer)
            return pltpu.with_memory_space_constraint(o, pltpu.HBM) if is_data else o
        return call(*operands[:n_scalar], *[pin(o) for o in operands[n_scalar:]])

    return run

ANY = pl.BlockSpec(memory_space=pl.ANY)
VMEM = pl.BlockSpec(memory_space=pltpu.VMEM)


def _whole(a):
    return pl.BlockSpec(a.shape, lambda i: (0,) * len(a.shape))


def _params(vmem_mib=None, sem=None):
    kw = {}
    if vmem_mib is not None:
        kw["vmem_limit_bytes"] = vmem_mib * MIB
    if sem is not None:
        kw["dimension_semantics"] = sem
    return pltpu.CompilerParams(**kw)


def _nn(a, b):
    return jnp.dot(a, b, preferred_element_type=F32)


def _nt(a, b):
    return lax.dot_general(a, b, (((1,), (1,)), ((), ())), preferred_element_type=F32)


def _tn(a, b):
    return lax.dot_general(a, b, (((0,), (0,)), ((), ())), preferred_element_type=F32)


def _sigmoid(v):
    return 1.0 / (1.0 + jnp.exp(-v))


def _part8(v):
    return v.reshape(v.shape[0] // 8, 8, v.shape[1]).sum(axis=0)


def _my_position():
    return lax.axis_index("x"), lax.axis_index("y"), lax.axis_index("c")


def _flat(px, py, pc):
    return 4 * px + 2 * py + pc


def _peer(mask):
    x, y, c = _my_position()
    return (x ^ ((mask >> 2) & 1), y ^ ((mask >> 1) & 1), c ^ (mask & 1))


def _column_chunks(n):
    chunks = [(128 * a, 0, 128 * a, 128) for a in range(n // 128)]
    if n % 128:
        chunks.append((n - 128, 128 - n % 128, 128 * (n // 128), n % 128))
    return chunks


def _cast_rows(w, n_steps, name):
    rows, ncol = w.shape
    blk = pl.BlockSpec((rows // n_steps, ncol), lambda i: (i, 0))

    def body(w_ref, o_ref):
        o_ref[...] = w_ref[...].astype(BF16)

    return _pcall(body, grid=(n_steps,), out_shape=jax.ShapeDtypeStruct(w.shape, BF16), in_specs=[blk], out_specs=blk,
                  name=name, compiler_params=_params(16, ("parallel",)))(w)


def _prep(w_pa, w_pb, w_out, c, conv_w):
    def body(wpa_ref, wpb_ref, wout_ref, c_ref, cw_ref, wpat_ref, wpb_o, wout_o, cact_ref, cwp_ref):
        wpat_ref[...] = wpa_ref[...].T.astype(BF16)
        wpb_o[...] = wpb_ref[...].astype(BF16)
        wout_o[...] = wout_ref[...].astype(BF16)
        cv = c_ref[...]
        cact_ref[...] = jnp.zeros_like(cact_ref)
        cact_ref[pl.ds(0, cv.shape[0]), :] = cv * _sigmoid(cv)
        cwp_ref[...] = jnp.zeros_like(cwp_ref)
        cwp_ref[pl.ds(0, 3), :] = cw_ref[...]

    out_shape = (jax.ShapeDtypeStruct((w_pa.shape[1], w_pa.shape[0]), BF16),
                 jax.ShapeDtypeStruct(w_pb.shape, BF16), jax.ShapeDtypeStruct(w_out.shape, BF16),
                 jax.ShapeDtypeStruct((8, D), F32), jax.ShapeDtypeStruct((8, conv_w.shape[1]), F32))
    operands = (w_pa, w_pb, w_out, c, conv_w)
    return _pcall(body, grid=(1,), out_shape=out_shape, in_specs=[_whole(a) for a in operands],
                  out_specs=tuple(_whole(o) for o in out_shape), name="prep", compiler_params=_params(16))(*operands)


def _exchange_slots(out_refs, send_sems, recv_sems, base=0):
    me = _flat(*_my_position())

    def copy(a, mask, slot):
        return pltpu.make_async_remote_copy(
            src_ref=out_refs[a].at[slot], dst_ref=out_refs[a].at[slot], send_sem=send_sems.at[base + 7 * a + mask - 1],
            recv_sem=recv_sems.at[base + 7 * a + mask - 1], device_id=_peer(mask), device_id_type=MESH)

    pairs = [(a, mask) for a in range(len(out_refs)) for mask in range(1, N_DEV)]
    for a, mask in pairs:
        copy(a, mask, me).start()
    for a, mask in pairs:
        copy(a, mask, _flat(*_peer(mask))).wait_recv()
    for a, mask in pairs:
        copy(a, mask, me).wait_send()


def _ada_forward(cact_mine, cw_mine, w_ada, b_ada_mine):
    ncol = w_ada.shape[1]

    def body(c_ref, cw_ref, w_ref, b_ref, out_ref, call_ref, cwall_ref, send_sems, recv_sems):
        me = _flat(*_my_position())
        call_ref[me] = c_ref[...]
        cwall_ref[me] = cw_ref[...]
        _exchange_slots([call_ref, cwall_ref], send_sems, recv_sems)
        c_all = call_ref[...].reshape(N_DEV * 8, D).astype(BF16)
        out_ref[me] = (_nn(c_all, w_ref[...].astype(BF16)) + b_ref[...]).reshape(N_DEV, 8, ncol)
        _exchange_slots([out_ref], send_sems, recv_sems, base=14)

    operands = (cact_mine, cw_mine, w_ada, b_ada_mine)
    out_shape = (jax.ShapeDtypeStruct((N_DEV, N_DEV, 8, ncol), F32), jax.ShapeDtypeStruct((N_DEV, 8, D), F32),
                 jax.ShapeDtypeStruct((N_DEV,) + cw_mine.shape, F32))
    return _pcall(body, grid=(1,), out_shape=out_shape, in_specs=[_whole(a) for a in operands], out_specs=(VMEM,) * 3,
                  scratch_shapes=[pltpu.SemaphoreType.DMA((21,)), pltpu.SemaphoreType.DMA((21,))], name="ada_forward",
                  compiler_params=_params(16))(*operands)


def _small_reduce(gb_rest, gb_qkv, svec, dgate, dss):
    nbat = dgate.shape[0]

    def body(gbr_ref, q0_ref, q1_ref, q2_ref, sv_ref, dg_ref, dss_ref, rows_ref, tot_ref, gbada_ref, send_sems, recv_sems):
        me = _flat(*_my_position())

        def put(off, v):
            rows_ref[me, :, pl.ds(off, v.shape[1])] = v

        def row(v):
            return jnp.sum(v, axis=0, keepdims=True)

        for g, q_ref in enumerate((q0_ref, q1_ref, q2_ref)):
            for which in range(3):
                put(P_BIN + SLAB * (3 * which + g), row(q_ref[which]))
        for s in range(N_REST):
            put(P_BIN + SLAB * (N_QKV + s), row(gbr_ref[s]))
        put(P_LNG, row(sv_ref[0]))
        put(P_LNB, row(sv_ref[1]))
        put(P_BOUT, row(sv_ref[2]))
        for j in range(3):
            put(P_CONV + D * j, row(sv_ref[3 + j]))
        loss = (0.5 / D) * jnp.sum(row(sv_ref[6]), axis=1, keepdims=True)
        put(P_LOSS, jnp.broadcast_to(loss, (1, 128)))
        for b in range(nbat):
            put(P_DADA + 3 * D * b, row(dss_ref[b, 0]))
            put(P_DADA + 3 * D * b + D, row(dss_ref[b, 1]))
            put(P_DADA + 3 * D * b + 2 * D, row(dg_ref[b]))
        _exchange_slots([rows_ref], send_sems, recv_sems)
        tot = rows_ref[0]
        for k in range(1, N_DEV):
            tot = tot + rows_ref[k]
        tot_ref[...] = tot
        gbada = tot[:, P_DADA:P_DADA + 3 * D]
        for b in range(1, nbat):
            gbada = gbada + tot[:, P_DADA + 3 * D * b:P_DADA + 3 * D * (b + 1)]
        gbada_ref[...] = gbada

    p_len = P_DADA + nbat * 3 * D
    out_shape = (jax.ShapeDtypeStruct((N_DEV, 1, p_len), F32), jax.ShapeDtypeStruct((1, p_len), F32),
                 jax.ShapeDtypeStruct((1, 3 * D), F32))
    operands = (gb_rest, *gb_qkv, svec, dgate, dss)
    return _pcall(body, grid=(1,), out_shape=out_shape, in_specs=[_whole(a) for a in operands],
                  out_specs=(VMEM, _whole(out_shape[1]), _whole(out_shape[2])),
                  scratch_shapes=[pltpu.SemaphoreType.DMA((7,)), pltpu.SemaphoreType.DMA((7,))], name="small_reduce",
                  compiler_params=_params(16))(*operands)


def _make_h(x, ada, tm=512):
    t = x.shape[0]
    tps = (t // ada.shape[0]) // tm

    def body(x_ref, ada_ref, h_ref):
        h_ref[...] = (x_ref[...] * (1.0 + ada_ref[0, 1:2, :]) + ada_ref[0, 0:1, :]).astype(BF16)

    return _pcall(body, grid=(t // tm,), out_shape=jax.ShapeDtypeStruct((t, D), BF16),
                  in_specs=[pl.BlockSpec((tm, D), lambda i: (i, 0)), pl.BlockSpec((1, 3, D), lambda i: (i // tps, 0, 0))],
                  out_specs=pl.BlockSpec((tm, D), lambda i: (i, 0)), name="make_h",
                  compiler_params=_params(32, ("parallel",)))(x, ada)


PIECE = 64
N_CHUNK = 4
ARRIVAL_RANK = (0, 1, 3, 5, 2, 4, 6, 7)
SLOT_MASK = (1, 4, 2, 6, 5, 3, 7)


def _arrival_tables(shard_rows):
    import numpy as np
    crow = shard_rows // N_CHUNK
    table = np.zeros((N_DEV, N_SLAB + 7 * N_CHUNK), np.int32)
    lo = [(SLAB * j) // crow for j in range(N_SLAB)]
    hi = [(SLAB * j + SLAB - 1) // crow for j in range(N_SLAB)]
    for k in range(N_DEV):
        def rank(chunk):
            shard_rank = ARRIVAL_RANK[(chunk // N_CHUNK) ^ k]
            return shard_rank if shard_rank < 2 else 2 + 8 * (chunk % N_CHUNK) + shard_rank
        order = sorted(range(N_SLAB), key=lambda j: (max(rank(lo[j]), rank(hi[j])), j))
        table[k, :N_SLAB] = order
        for slot, mask in enumerate(SLOT_MASK):
            for ch in range(N_CHUNK):
                chunk = (k ^ mask) * N_CHUNK + ch
                table[k, N_SLAB + slot * N_CHUNK + ch] = min(t for t, j in enumerate(order) if lo[j] <= chunk <= hi[j])
    return table


def _project_gather(shard, h, b_in3, others):
    t = h.shape[0]
    n_o = len(others)
    srows = shard.shape[0]
    crow = srows // N_CHUNK
    shards = [shard] + list(others)
    table = jnp.asarray(_arrival_tables(srows))

    def body(tbl_ref, *refs):
        srcs = [refs[0]] + list(refs[3:3 + n_o])
        h_ref, b_ref = refs[1], refs[2]
        outs = [refs[3 + n_o]] + list(refs[6 + n_o:6 + 2 * n_o])
        qkv_ref, rest_ref = refs[4 + n_o], refs[5 + n_o]
        (wtile, obf, of32, send_sems, recv_sems, local_sems, tile_sems, obf_sems, of32_sems) = refs[6 + 2 * n_o:]
        w_full = outs[0]
        x, y, c = _my_position()
        k = _flat(x, y, c)
        me, sibling = (x, y, c), (x, y, 1 - c)
        chips = [(1 - x, y), (x, 1 - y), (1 - x, 1 - y)]

        def rows(a, px, py, pc, ch):
            r = shards[a].shape[0]
            if ch is None:
                return outs[a].at[pl.ds(pl.multiple_of(_flat(px, py, pc) * r, r), r), :]
            return outs[a].at[pl.ds(pl.multiple_of(_flat(px, py, pc) * r + ch * crow, crow), crow), :]

        def copy(a, slot, block, to, ch=None, src=None):
            sem = slot * N_CHUNK + ch if a == 0 else 7 * (N_CHUNK - 1 + a) + slot
            if src is not None and ch is not None:
                src = src.at[pl.ds(ch * crow, crow), :]
            return pltpu.make_async_remote_copy(
                src_ref=rows(a, *block, ch) if src is None else src, dst_ref=rows(a, *block, ch),
                send_sem=send_sems.at[sem], recv_sem=recv_sems.at[sem], device_id=to, device_id_type=MESH)

        def relay(slot, ch):
            block = (*chips[slot - 1], c)
            return pltpu.make_async_remote_copy(
                src_ref=rows(0, *block, ch), dst_ref=rows(0, *block, ch), send_sem=send_sems.at[n_sems + ch],
                recv_sem=recv_sems.at[3 * N_CHUNK + ch], device_id=(*chips[2 - slot], c), device_id_type=MESH)

        def relays(slot):
            return c == slot - 1

        mine = [pltpu.make_async_copy(srcs[a], rows(a, *me, None), local_sems.at[a]) for a in range(1 + n_o)]
        first = []
        for ch in range(N_CHUNK):
            first.append(copy(0, 0, me, sibling, ch, src=srcs[0]))
            first += [copy(0, 1 + j, me, (*chip, c), ch, src=srcs[0]) for j, chip in enumerate(chips[:2])]
        for a in range(1, 1 + n_o):
            first.append(copy(a, 0, me, sibling, src=srcs[a]))
            first += [copy(a, 1 + j, me, (*chip, c), src=srcs[a]) for j, chip in enumerate(chips)]
        for cp in mine + first:
            cp.start()

        def arrive(a, slot, ch=None):
            if slot == 0:
                copy(a, 0, sibling, me, ch).wait_recv()
            elif slot < 4:
                copy(a, slot, (*chips[slot - 1], c), me, ch).wait_recv()
                copy(a, slot + 3, (*chips[slot - 1], c), sibling, ch).start()
                if a == 0 and slot < 3:
                    @pl.when(relays(slot))
                    def _():
                        relay(slot, ch).start()
            else:
                copy(a, slot, (*chips[slot - 4], 1 - c), me, ch).wait_recv()

        def arrive_for(step):
            for slot in range(7):
                for ch in range(N_CHUNK):
                    @pl.when(tbl_ref[k, N_SLAB + slot * N_CHUNK + ch] == step)
                    def _():
                        arrive(0, slot, ch)

        def fetch(step, buf):
            slab = tbl_ref[k, step]
            for p in range(SLAB // PIECE):
                g0 = slab * SLAB + PIECE * p
                own = (g0 >= k * srows) & (g0 < (k + 1) * srows)
                dst = wtile.at[buf, pl.ds(PIECE * p, PIECE), :]

                @pl.when(own)
                def _():
                    pltpu.make_async_copy(srcs[0].at[pl.ds(pl.multiple_of(g0 - k * srows, PIECE), PIECE), :], dst, tile_sems.at[buf]).start()

                @pl.when(jnp.logical_not(own))
                def _():
                    pltpu.make_async_copy(w_full.at[pl.ds(pl.multiple_of(g0, PIECE), PIECE), :], dst, tile_sems.at[buf]).start()

        def wait_tile(buf):
            pltpu.make_async_copy(w_full.at[pl.ds(0, SLAB), :], wtile.at[buf], tile_sems.at[buf]).wait()

        def put(buf_ref, sems, dst_ref, count, value):
            b = count % 2

            @pl.when(count >= 2)
            def _():
                pltpu.make_async_copy(buf_ref.at[b], dst_ref, sems.at[b]).wait()

            buf_ref[b] = value
            pltpu.make_async_copy(buf_ref.at[b], dst_ref, sems.at[b]).start()

        def drain(buf_ref, sems, dst_ref, count):
            for back in (1, 2):
                @pl.when(count >= back)
                def _():
                    pltpu.make_async_copy(buf_ref.at[(count - back) % 2], dst_ref, sems.at[(count - back) % 2]).wait()

        arrive_for(0)
        fetch(0, 0)

        def step(s, carry):
            n_bf, n_f32 = carry
            buf = s % 2

            @pl.when(s + 1 < N_SLAB)
            def _():
                arrive_for(s + 1)
                fetch(s + 1, 1 - buf)

            wait_tile(buf)
            slab = tbl_ref[k, s]
            v = _nt(h_ref[...], wtile[buf]) + b_ref[slab]
            is_qkv = slab < N_QKV

            @pl.when(is_qkv)
            def _():
                put(obf, obf_sems, qkv_ref.at[jnp.minimum(slab, N_QKV - 1)], n_bf, v.astype(BF16))

            @pl.when(jnp.logical_not(is_qkv))
            def _():
                put(of32, of32_sems, rest_ref.at[jnp.maximum(slab - N_QKV, 0)], n_f32, v)

            return n_bf + is_qkv.astype(jnp.int32), n_f32 + 1 - is_qkv.astype(jnp.int32)

        n_bf, n_f32 = lax.fori_loop(0, N_SLAB, step, (jnp.int32(0), jnp.int32(0)))
        drain(obf, obf_sems, qkv_ref.at[0], n_bf)
        drain(of32, of32_sems, rest_ref.at[0], n_f32)

        for slots in ((1, 2, 3), (0, 4, 5, 6)):
            for a in range(1, 1 + n_o):
                for slot in slots:
                    arrive(a, slot)
        for cp in first:
            cp.wait_send()
        for slot in (1, 2):
            @pl.when(relays(slot))
            def _():
                for ch in range(N_CHUNK):
                    relay(slot, ch).wait_send()
        for j, chip in enumerate(chips):
            for ch in range(N_CHUNK):
                copy(0, 4 + j, (*chip, c), sibling, ch).wait_send()
            for a in range(1, 1 + n_o):
                copy(a, 4 + j, (*chip, c), sibling).wait_send()
        for cp in mine:
            cp.wait()

    out_shape = ((jax.ShapeDtypeStruct((N_DEV * srows, D), BF16), jax.ShapeDtypeStruct((N_QKV, t, SLAB), BF16),
                  jax.ShapeDtypeStruct((N_REST, t, SLAB), F32))
                 + tuple(jax.ShapeDtypeStruct((N_DEV * o.shape[0], o.shape[1]), o.dtype) for o in others))
    n_all = 1 + n_o
    n_sems = 7 * (N_CHUNK + n_o)
    grid_spec = pltpu.PrefetchScalarGridSpec(
        num_scalar_prefetch=1, grid=(1,),
        in_specs=[ANY, pl.BlockSpec((t, D), lambda i, tbl: (0, 0), pipeline_mode=pl.Buffered(1)),
                  pl.BlockSpec((N_SLAB, 1, SLAB), lambda i, tbl: (0, 0, 0))] + [ANY] * n_o,
        out_specs=(ANY,) * (3 + n_o),
        scratch_shapes=[pltpu.VMEM((2, SLAB, D), BF16), pltpu.VMEM((2, t, SLAB), BF16), pltpu.VMEM((2, t, SLAB), F32),
                        pltpu.SemaphoreType.DMA((n_sems + N_CHUNK,)), pltpu.SemaphoreType.DMA((n_sems,)),
                        pltpu.SemaphoreType.DMA((n_all,)), pltpu.SemaphoreType.DMA((2,)), pltpu.SemaphoreType.DMA((2,)),
                        pltpu.SemaphoreType.DMA((2,))])
    res = _pcall(body, grid_spec=grid_spec, out_shape=out_shape, name="project_gather",
                 compiler_params=_params(48, ("arbitrary",)))(table, shard, h, b_in3, *others)
    return res[0], res[1], res[2], list(res[3:])


def _bias_tables(g):
    window, dil = GROUPS[g]
    span = window // dil
    qi = jnp.arange(BLK)[:, None]
    kj = jnp.arange(2 * BLK)[None, :]
    delta = qi + BLK - kj
    valid = (delta >= 0) & (delta <= span)
    heads = jnp.arange(4, dtype=F32) + 4.0 * g
    slopes = 2.0 ** (-8.0 * (heads + 1.0) / 12.0)
    bias = -slopes[:, None, None] * (delta * dil).astype(F32)[None]
    return jnp.where(valid[None], bias, -1e30).reshape(4 * BLK, 2 * BLK)


def _head_masks(shape):
    lane = lax.broadcasted_iota(jnp.int32, shape, 1)
    return [(lane >= 64 * h) & (lane < 64 * (h + 1)) for h in range(4)]


def _stack_heads(v, masks):
    return jnp.concatenate([jnp.where(masks[h], v, jnp.zeros_like(v)) for h in range(4)], axis=0)


def _unstack_heads(v4, masks):
    out = jnp.where(masks[0], v4[0:BLK], 0.0)
    for h in range(1, 4):
        out = jnp.where(masks[h], v4[BLK * h:BLK * (h + 1)], out)
    return out


def _regroup(load_half, dst_ref, stage_ref, n, dil):
    for hlf in range(2):
        stage_ref[hlf] = load_half(hlf)

    def residue(r, carry):
        for hlf in range(2):
            dst_ref[pl.ds(pl.multiple_of(r * n, BLK), n), pl.ds(128 * hlf, 128)] = (
                stage_ref[hlf, pl.ds(r, n, stride=dil), :].astype(dst_ref.dtype))
        return carry

    lax.fori_loop(0, dil, residue, 0)


def _store_block(nat_ref, r, i, val, dil):
    for hlf in range(2):
        nat_ref[hlf, pl.ds(r + dil * BLK * i, BLK, stride=dil), :] = val[:, 128 * hlf:128 * (hlf + 1)]


def _for_blocks(block, dil, nblk):
    if dil == 1:
        block(0, 0, True)
        block(0, 1, False)

        def pair(k, carry):
            block(0, 2 * k, False)
            block(0, 2 * k + 1, False)
            return carry

        lax.fori_loop(1, nblk // 2, pair, 0)
    else:
        def residues(k, carry):
            block(2 * k, 0, True)
            block(2 * k + 1, 0, True)
            if nblk > 1:
                def loop(i, c):
                    block(2 * k, i, False)
                    block(2 * k + 1, i, False)
                    return c
                lax.fori_loop(1, nblk, loop, 0)
            return carry

        lax.fori_loop(0, dil // 2, residues, 0)


def _attn_forward(qkv, g, nbat):
    t = qkv.shape[1]
    seq = t // nbat
    dil = GROUPS[g][1]
    n = seq // dil
    nblk = n // BLK
    qkv4 = qkv.reshape(3, 3, t, SLAB)

    def body(qkv_ref, bias_ref, ol_ref, *scratch):
        masks = _head_masks((BLK, SLAB))
        if dil > 1:
            stage, qd, kd, vd, nat_o, nat_l = scratch
            for which, dst in enumerate((qd, kd, vd)):
                _regroup(lambda hlf, which=which: qkv_ref[which, 0, :, pl.ds(128 * hlf, 128)].astype(F32), dst, stage, n, dil)
        else:
            qd, kd, vd = qkv_ref.at[0, 0], qkv_ref.at[1, 0], qkv_ref.at[2, 0]

        def block(r, i, first):
            base = r * n
            qs = pl.ds(pl.multiple_of(base + i * BLK, BLK), BLK)
            ks = pl.ds(pl.multiple_of(base, BLK), BLK) if first else pl.ds(pl.multiple_of(base + (i - 1) * BLK, BLK), 2 * BLK)
            q, kk, vv = qd[qs, :], kd[ks, :], vd[ks, :]
            bias = bias_ref[:, pl.ds(BLK, BLK)] if first else bias_ref[...]
            s = _nt(_stack_heads(q, masks), kk) * 0.125 + bias
            m = jnp.max(s, axis=1, keepdims=True)
            p = jnp.exp(s - m)
            den = jnp.sum(p, axis=1, keepdims=True)
            out = _unstack_heads(_nn((p * (1.0 / den)).astype(BF16), vv), masks)
            lse = _unstack_heads(jnp.broadcast_to(m + jnp.log(den), (4 * BLK, SLAB)), masks)
            if dil > 1:
                _store_block(nat_o, r, i, out, dil)
                _store_block(nat_l, r, i, lse, dil)
            else:
                ol_ref[0, qs, :] = out
                ol_ref[1, qs, :] = lse

        _for_blocks(block, dil, nblk)
        if dil > 1:
            for hlf in range(2):
                ol_ref[0, :, pl.ds(128 * hlf, 128)] = nat_o[hlf]
                ol_ref[1, :, pl.ds(128 * hlf, 128)] = nat_l[hlf]

    scratch = []
    if dil > 1:
        scratch = [pltpu.VMEM((2, seq, 128), F32)] + [pltpu.VMEM((seq, SLAB), BF16)] * 3 + [pltpu.VMEM((2, seq, 128), F32)] * 2
    return _pcall(
        body, grid=(nbat,), out_shape=jax.ShapeDtypeStruct((2, t, SLAB), F32),
        in_specs=[pl.BlockSpec((3, 1, seq, SLAB), lambda b: (0, g, b, 0)),
                  pl.BlockSpec((4 * BLK, 2 * BLK), lambda b: (0, 0))],
        out_specs=pl.BlockSpec((2, seq, SLAB), lambda b: (0, b, 0)), scratch_shapes=scratch,
        name=f"attn_forward_{g}", compiler_params=_params(40, ("parallel",)))(qkv4, _bias_tables(g))


def _attn_backward(qkv, do_attn, ol_tot, dproj, g, nbat):
    t = qkv.shape[1]
    seq = t // nbat
    dil = GROUPS[g][1]
    n = seq // dil
    nblk = n // BLK
    qkv4 = qkv.reshape(3, 3, t, SLAB)
    dp4 = dproj.reshape(DP_SLABS // 3, 3, t, SLAB)

    def body(qkv_ref, do_ref, ol_ref, bias_ref, dp_in, dp_ref, gb_ref, dk_acc, dv_acc, *scratch):
        del dp_in
        masks = _head_masks((BLK, SLAB))

        @pl.when(pl.program_id(0) == 0)
        def _():
            gb_ref[...] = jnp.zeros_like(gb_ref)

        dk_acc[...] = jnp.zeros_like(dk_acc)
        dv_acc[...] = jnp.zeros_like(dv_acc)
        if dil > 1:
            stage, qd, kd, vd, dod, prodd, lsed, nat = scratch
            lanes = lambda hlf: pl.ds(128 * hlf, 128)
            for which, dst in enumerate((qd, kd, vd)):
                _regroup(lambda hlf, which=which: qkv_ref[which, 0, :, lanes(hlf)].astype(F32), dst, stage, n, dil)
            _regroup(lambda hlf: do_ref[:, lanes(hlf)].astype(F32), dod, stage, n, dil)
            _regroup(lambda hlf: do_ref[:, lanes(hlf)].astype(F32) * ol_ref[0, :, lanes(hlf)], prodd, stage, n, dil)
            _regroup(lambda hlf: ol_ref[1, :, lanes(hlf)], lsed, stage, n, dil)
        else:
            qd, kd, vd = qkv_ref.at[0, 0], qkv_ref.at[1, 0], qkv_ref.at[2, 0]

        def block(r, i, first):
            base = r * n
            qs = pl.ds(pl.multiple_of(base + i * BLK, BLK), BLK)
            ks = pl.ds(pl.multiple_of(base, BLK), BLK) if first else pl.ds(pl.multiple_of(base + (i - 1) * BLK, BLK), 2 * BLK)
            q, kk, vv = qd[qs, :], kd[ks, :], vd[ks, :]
            if dil > 1:
                do, prod, lse = dod[qs, :], prodd[qs, :], lsed[qs, :]
            else:
                do = do_ref[qs, :]
                prod = do.astype(F32) * ol_ref[0, qs, :]
                lse = ol_ref[1, qs, :]
            q4, do4 = _stack_heads(q, masks), _stack_heads(do, masks)
            bias = bias_ref[:, pl.ds(BLK, BLK)] if first else bias_ref[...]
            lse4 = jnp.concatenate([lse[:, 64 * h:64 * h + 1] for h in range(4)], axis=0)
            delta4 = jnp.concatenate([jnp.sum(jnp.where(masks[h], prod, 0.0), axis=1, keepdims=True) for h in range(4)], axis=0)
            p = jnp.exp(_nt(q4, kk) * 0.125 + bias - lse4)
            ds = (p * (_nt(do4, vv) - delta4)).astype(BF16)
            dv_acc[ks, :] += _tn(p.astype(BF16), do4)
            dk_acc[ks, :] += _tn(ds, q4) * 0.125
            dq = _unstack_heads(_nn(ds, kk), masks) * 0.125
            if dil > 1:
                _store_block(nat, r, i, dq, dil)
            else:
                dp_ref[0, 0, qs, :] = dq.astype(BF16)
            gb_ref[0] += _part8(dq)

        _for_blocks(block, dil, nblk)
        gb_ref[1] += _part8(dk_acc[...])
        gb_ref[2] += _part8(dv_acc[...])
        if dil > 1:
            def flush(which):
                for hlf in range(2):
                    dp_ref[which, 0, :, pl.ds(128 * hlf, 128)] = nat[hlf].astype(BF16)

            def to_token_order(acc_ref):
                def residue(r, carry):
                    for hlf in range(2):
                        nat[hlf, pl.ds(r, n, stride=dil), :] = acc_ref[pl.ds(pl.multiple_of(r * n, BLK), n), pl.ds(128 * hlf, 128)]
                    return carry
                lax.fori_loop(0, dil, residue, 0)

            flush(0)
            to_token_order(dk_acc)
            flush(1)
            to_token_order(dv_acc)
            flush(2)
        else:
            dp_ref[1, 0] = dk_acc[...].astype(BF16)
            dp_ref[2, 0] = dv_acc[...].astype(BF16)

    scratch = [pltpu.VMEM((seq, SLAB), F32)] * 2
    if dil > 1:
        scratch += ([pltpu.VMEM((2, seq, 128), F32)] + [pltpu.VMEM((seq, SLAB), BF16)] * 4 + [pltpu.VMEM((seq, SLAB), F32)] * 2
                    + [pltpu.VMEM((2, seq, 128), F32)])
    dp, gb = _pcall(
        body, grid=(nbat,),
        out_shape=(jax.ShapeDtypeStruct(dp4.shape, BF16), jax.ShapeDtypeStruct((3, 8, SLAB), F32)),
        in_specs=[pl.BlockSpec((3, 1, seq, SLAB), lambda b: (0, g, b, 0)),
                  pl.BlockSpec((seq, SLAB), lambda b: (b, 0)),
                  pl.BlockSpec((2, seq, SLAB), lambda b: (0, b, 0)),
                  pl.BlockSpec((4 * BLK, 2 * BLK), lambda b: (0, 0)), ANY],
        out_specs=(pl.BlockSpec((3, 1, seq, SLAB), lambda b: (DP_SLABS // 9 - 1, g, b, 0)),
                   pl.BlockSpec((3, 8, SLAB), lambda b: (0, 0, 0))),
        scratch_shapes=scratch, input_output_aliases={4: 0}, name=f"attn_backward_{g}",
        compiler_params=_params(48, ("arbitrary",)))(qkv4, do_attn, ol_tot, _bias_tables(g), dp4)
    return dp.reshape(DP_SLABS, t, SLAB), gb


def _mid(rest, ols, x, tgt, ada, cw, b_out, ln_g, ln_b, w_pa_t, w_pb, w_out, tm=256):
    t = x.shape[0]
    nbat = ada.shape[0]
    nt = t // tm
    tps = nt // nbat

    def body(rest_ref, halo_ref, ol0_ref, ol1_ref, ol2_ref, x_ref, t_ref, ada_ref, cw_ref, bout_ref, lng_ref, lnb_ref,
             wpat_ref, wpb_ref, wout_ref,
             dp_ref, gx0_ref, doa_ref, olt_ref, mg_ref, dof_ref, bbs_ref, dyc_ref, a_ref, dya_ref,
             gbr_ref, sv_ref, dgate_ref, carry_ref, keep_ref):
        i = pl.program_id(0)
        ti = nt - 1 - i
        pos = ti % tps

        @pl.when(i == 0)
        def _():
            gbr_ref[...] = jnp.zeros_like(gbr_ref)
            sv_ref[...] = jnp.zeros_like(sv_ref)

        @pl.when(pos == tps - 1)
        def _():
            dgate_ref[...] = jnp.zeros_like(dgate_ref)
            carry_ref[...] = jnp.zeros_like(carry_ref)

        row = lax.broadcasted_iota(jnp.int32, (tm, SLAB), 0)
        halo_on = (pos > 0).astype(F32)

        def cols(s):
            return pl.ds(SLAB * s, SLAB)

        l0, l1, l2 = ol0_ref[1], ol1_ref[1], ol2_ref[1]
        mx = jnp.maximum(jnp.maximum(l0, l1), l2)
        e0, e1, e2 = jnp.exp(l0 - mx), jnp.exp(l1 - mx), jnp.exp(l2 - mx)
        den = e0 + e1 + e2
        o_attn = (e0 * ol0_ref[0] + e1 * ol1_ref[0] + e2 * ol2_ref[0]) * (1.0 / den)
        olt_ref[0] = o_attn
        olt_ref[1] = mx + jnp.log(den)
        z_a = rest_ref[R_ZA]
        sg_za = _sigmoid(z_a)
        a_ref[...] = (o_attn * z_a * sg_za).astype(BF16)
        y_attn = _nt(a_ref[...], wpat_ref[...])

        for s in range(4):
            u = rest_ref[R_GC + s] * rest_ref[R_UX + s]
            hu = halo_ref[R_GC + s] * halo_ref[R_UX + s] * halo_on
            u1 = jnp.where(row == 0, hu[7:8], pltpu.roll(u, 1, 0))
            u2 = jnp.where(row == 0, hu[6:7], jnp.where(row == 1, hu[7:8], pltpu.roll(u, 2, 0)))
            conv = cw_ref[0:1, cols(s)] * u2 + cw_ref[1:2, cols(s)] * u1 + cw_ref[2:3, cols(s)] * u
            zc = rest_ref[R_ZC + s]
            sg = _sigmoid(zc)
            keep_ref[2, :, cols(s)], keep_ref[3, :, cols(s)], keep_ref[4, :, cols(s)], keep_ref[5, :, cols(s)] = u1, u2, conv, sg
            bbs_ref[:, cols(s)] = (rest_ref[R_GB + s] * conv * (zc * sg)).astype(BF16)
        y_conv = _nn(bbs_ref[...], wpb_ref[...])

        for s in range(4):
            s_a, s_b = _sigmoid(rest_ref[R_GA + s]), _sigmoid(rest_ref[R_GBM + s])
            keep_ref[0, :, cols(s)], keep_ref[1, :, cols(s)] = s_a, s_b
            mg_ref[:, cols(s)] = (s_a * y_attn[:, SLAB * s:SLAB * (s + 1)] + s_b * y_conv[:, SLAB * s:SLAB * (s + 1)]).astype(BF16)
        o = _nn(mg_ref[...], wout_ref[...]) + bout_ref[...]
        gate = ada_ref[0, 2:3, :]
        r = ALPHA * x_ref[...] + gate * o
        mu = jnp.mean(r, axis=1, keepdims=True)
        rc = r - mu
        rstd = lax.rsqrt(jnp.mean(rc * rc, axis=1, keepdims=True) + LN_EPS)
        xhat = rc * rstd
        err = xhat * lng_ref[...] + lnb_ref[...] - t_ref[...]
        sv_ref[6] += _part8(err * err)
        dy = err * (1.0 / D)
        sv_ref[0] += _part8(dy * xhat)
        sv_ref[1] += _part8(dy)
        dxh = dy * lng_ref[...]
        dr = rstd * (dxh - jnp.mean(dxh, axis=1, keepdims=True) - xhat * jnp.mean(dxh * xhat, axis=1, keepdims=True))
        gx0_ref[...] = ALPHA * dr
        dgate_ref[0] += _part8(dr * o)
        do_ = dr * gate
        sv_ref[2] += _part8(do_)
        dof_ref[...] = do_.astype(BF16)
        dmerged = _nt(dof_ref[...], wout_ref[...])
        for s in range(4):
            s_a, s_b = keep_ref[0, :, cols(s)], keep_ref[1, :, cols(s)]
            dm = dmerged[:, SLAB * s:SLAB * (s + 1)]
            ya, yc = y_attn[:, SLAB * s:SLAB * (s + 1)], y_conv[:, SLAB * s:SLAB * (s + 1)]
            dya_ref[:, cols(s)] = (dm * s_a).astype(BF16)
            dyc_ref[:, cols(s)] = (dm * s_b).astype(BF16)
            dga = dm * ya * s_a * (1.0 - s_a)
            dgb = dm * yc * s_b * (1.0 - s_b)
            dp_ref[R_GA + s] = dga.astype(BF16)
            dp_ref[R_GBM + s] = dgb.astype(BF16)
            gbr_ref[R_GA + s] += _part8(dga)
            gbr_ref[R_GBM + s] += _part8(dgb)

        da = _nn(dya_ref[...], wpat_ref[...])
        doa_ref[...] = (da * z_a * sg_za).astype(BF16)
        dza = da * o_attn * (sg_za * (1.0 + z_a * (1.0 - sg_za)))
        dp_ref[R_ZA] = dza.astype(BF16)
        gbr_ref[R_ZA] += _part8(dza)

        dbb = _nt(dyc_ref[...], wpb_ref[...])
        for s in range(4):
            ux, gc, zc = rest_ref[R_UX + s], rest_ref[R_GC + s], rest_ref[R_ZC + s]
            u = gc * ux
            u1, u2, conv, sg = keep_ref[2, :, cols(s)], keep_ref[3, :, cols(s)], keep_ref[4, :, cols(s)], keep_ref[5, :, cols(s)]
            gb = rest_ref[R_GB + s]
            d_b = dbb[:, SLAB * s:SLAB * (s + 1)]
            szc = zc * sg
            dgb_ = d_b * conv * szc
            dconv = d_b * gb * szc
            dzc = d_b * gb * conv * (sg * (1.0 + zc * (1.0 - sg)))
            sv_ref[3, :, cols(s)] += _part8(dconv * u2)
            sv_ref[4, :, cols(s)] += _part8(dconv * u1)
            sv_ref[5, :, cols(s)] += _part8(dconv * u)
            nxt = carry_ref[:, cols(s)]
            d1 = jnp.where(row == tm - 1, nxt[0:1], pltpu.roll(dconv, tm - 1, 0))
            d2 = jnp.where(row == tm - 1, nxt[1:2], jnp.where(row == tm - 2, nxt[0:1], pltpu.roll(dconv, tm - 2, 0)))
            carry_ref[:, cols(s)] = dconv[0:8]
            du = cw_ref[2:3, cols(s)] * dconv + cw_ref[1:2, cols(s)] * d1 + cw_ref[0:1, cols(s)] * d2
            dgc, dux = du * ux, du * gc
            for slab, val in ((R_GB + s, dgb_), (R_ZC + s, dzc), (R_GC + s, dgc), (R_UX + s, dux)):
                dp_ref[slab] = val.astype(BF16)
                gbr_ref[slab] += _part8(val)

    def tile(i):
        return nt - 1 - i

    row_blk = lambda i: (tile(i), 0)
    slab_blk = lambda i: (0, tile(i), 0)
    const2 = lambda i: (0, 0)
    const3 = lambda i: (0, 0, 0)
    in_specs = [
        pl.BlockSpec((N_REST, tm, SLAB), slab_blk),
        pl.BlockSpec((N_REST, 8, SLAB), lambda i: (0, jnp.maximum(tile(i) * (tm // 8) - 1, 0), 0)),
        pl.BlockSpec((2, tm, SLAB), slab_blk), pl.BlockSpec((2, tm, SLAB), slab_blk), pl.BlockSpec((2, tm, SLAB), slab_blk),
        pl.BlockSpec((tm, D), row_blk), pl.BlockSpec((tm, D), row_blk),
        pl.BlockSpec((1, 3, D), lambda i: (tile(i) // tps, 0, 0)),
        pl.BlockSpec((3, D), const2), pl.BlockSpec((1, D), const2), pl.BlockSpec((1, D), const2), pl.BlockSpec((1, D), const2),
        pl.BlockSpec((D, SLAB), const2), pl.BlockSpec((D, D), const2), pl.BlockSpec((D, D), const2)]
    bf_rows = lambda: jax.ShapeDtypeStruct((t, D), BF16)
    out_shape = (
        jax.ShapeDtypeStruct((DP_SLABS, t, SLAB), BF16), jax.ShapeDtypeStruct((t, D), F32),
        jax.ShapeDtypeStruct((t, SLAB), BF16), jax.ShapeDtypeStruct((2, t, SLAB), F32),
        bf_rows(), bf_rows(), bf_rows(), bf_rows(), jax.ShapeDtypeStruct((t, SLAB), BF16), bf_rows(),
        jax.ShapeDtypeStruct((N_REST, 8, SLAB), F32), jax.ShapeDtypeStruct((7, 8, D), F32),
        jax.ShapeDtypeStruct((nbat, 8, D), F32))
    out_specs = (
        pl.BlockSpec((N_REST, tm, SLAB), slab_blk), pl.BlockSpec((tm, D), row_blk),
        pl.BlockSpec((tm, SLAB), row_blk), pl.BlockSpec((2, tm, SLAB), slab_blk),
        pl.BlockSpec((tm, D), row_blk), pl.BlockSpec((tm, D), row_blk), pl.BlockSpec((tm, D), row_blk),
        pl.BlockSpec((tm, D), row_blk), pl.BlockSpec((tm, SLAB), row_blk), pl.BlockSpec((tm, D), row_blk),
        pl.BlockSpec((N_REST, 8, SLAB), const3), pl.BlockSpec((7, 8, D), const3),
        pl.BlockSpec((1, 8, D), lambda i: (tile(i) // tps, 0, 0)))
    return _pcall(body, grid=(nt,), out_shape=out_shape, in_specs=in_specs, out_specs=out_specs,
                  scratch_shapes=[pltpu.VMEM((8, D), F32), pltpu.VMEM((6, tm, D), F32)], name="mid",
                  compiler_params=_params(56, ("arbitrary",)))(
        rest, rest, *ols, x, tgt, ada, cw, b_out, ln_g, ln_b, w_pa_t, w_pb, w_out)


def _tn_matmul(lhs, rhs, lhs_spec, n_steps, out_rows, out_index, name, after):
    t, n = rhs.shape

    def body(l_ref, r_ref, after_ref, o_ref):
        del after_ref
        o_ref[...] = _tn(l_ref[0] if len(l_ref.shape) == 3 else l_ref[...], r_ref[...])

    return _pcall(body, grid=(n_steps,), out_shape=jax.ShapeDtypeStruct((out_rows, n), F32),
                  in_specs=[lhs_spec, pl.BlockSpec((t, n), lambda j: (0, 0)), ANY],
                  out_specs=pl.BlockSpec((SLAB, n), out_index), name=name,
                  compiler_params=_params(48, ("parallel",)))(lhs, rhs, after)


def _grad_rows_2d(lhs, rhs, name, after):
    t, k = lhs.shape
    return _tn_matmul(lhs, rhs, pl.BlockSpec((t, SLAB), lambda j: (0, j)), k // SLAB, k, lambda j: (j, 0), name, after)


def _w_row_block(j):
    return (j + N_QKV) % N_SLAB


def _dp_slab(j):
    return jnp.where(j < N_REST, j, j + 2)


def _grad_w_in_t(dproj, h):
    t = h.shape[0]
    return _tn_matmul(dproj, h, pl.BlockSpec((1, t, SLAB), lambda j: (_dp_slab(j), 0, 0)), N_SLAB, D_IN,
                      lambda j: (_w_row_block(j), 0), "grad_w_in", h)


def _grad_h(dproj, w_in_t, gx0, x, ada, after, tm=512):
    t = x.shape[0]
    nbat = ada.shape[0]
    tps = (t // nbat) // tm

    def body(dp_ref, w_ref, gx0_ref, x_ref, ada_ref, after_ref, gx_ref, dss_ref):
        del after_ref
        i = pl.program_id(0)
        dh = None
        for j in range(N_SLAB):
            slab = j if j < N_REST else j + 2
            part = _nn(dp_ref[slab], w_ref[pl.ds(SLAB * ((j + N_QKV) % N_SLAB), SLAB), :])
            dh = part if dh is None else dh + part
        gx_ref[...] = gx0_ref[...] + dh * (1.0 + ada_ref[0, 1:2, :])

        @pl.when((i % tps) == 0)
        def _():
            dss_ref[...] = jnp.zeros_like(dss_ref)

        dss_ref[0, 0] += _part8(dh)
        dss_ref[0, 1] += _part8(dh * x_ref[...])

    return _pcall(
        body, grid=(t // tm,),
        out_shape=(jax.ShapeDtypeStruct((t, D), F32), jax.ShapeDtypeStruct((nbat, 2, 8, D), F32)),
        in_specs=[pl.BlockSpec((DP_SLABS, tm, SLAB), lambda i: (0, i, 0)),
                  pl.BlockSpec((D_IN, D), lambda i: (0, 0), pipeline_mode=pl.Buffered(1)),
                  pl.BlockSpec((tm, D), lambda i: (i, 0)), pl.BlockSpec((tm, D), lambda i: (i, 0)),
                  pl.BlockSpec((1, 3, D), lambda i: (i // tps, 0, 0)), ANY],
        out_specs=(pl.BlockSpec((tm, D), lambda i: (i, 0)),
                   pl.BlockSpec((1, 2, 8, D), lambda i: (i // tps, 0, 0, 0))),
        name="grad_h", compiler_params=_params(60, ("arbitrary",)))(dproj, w_in_t, gx0, x, ada, after)


def _chip(m):
    x, y, _ = _my_position()
    return (x ^ ((m >> 1) & 1), y ^ (m & 1))


def _exchange_siblings(grads, after, name):
    n = len(grads)

    def body(*refs):
        copies = _sibling_copies(refs[:n], refs[n + 1:2 * n + 1], refs[2 * n + 1], refs[2 * n + 2])
        for cp in copies:
            cp.start()
        for cp in copies:
            cp.wait()

    return _pcall(body, out_shape=tuple(_sibling_zones(grads)), in_specs=[ANY] * (n + 1), out_specs=(ANY,) * n,
                  name=name, scratch_shapes=[pltpu.SemaphoreType.DMA((4 * n,))] * 2)(*grads, after)


def _sibling_zones(grads):
    return [jax.ShapeDtypeStruct((4, g.shape[0] // N_DEV, g.shape[1]), g.dtype) for g in grads]


def _sibling_copies(srcs, lands, send_sems, recv_sems):
    x, y, c = _my_position()
    copies = []
    for a, (src, land) in enumerate(zip(srcs, lands)):
        rows = land.shape[1]
        for m in range(4):
            dev = _flat(*_chip(m), 1 - c)
            copies.append(pltpu.make_async_remote_copy(
                src_ref=src.at[pl.ds(pl.multiple_of(dev * rows, 8), rows), :], dst_ref=land.at[m],
                send_sem=send_sems.at[4 * a + m], recv_sem=recv_sems.at[4 * a + m], device_id=(x, y, 1 - c),
                device_id_type=MESH))
    return copies


def _chip_copies(srcs, lands, send_sems, recv_sems):
    _, _, c = _my_position()
    return [pltpu.make_async_remote_copy(
        src_ref=srcs[a].at[m - 1], dst_ref=lands[a].at[m - 1], send_sem=send_sems.at[3 * a + m - 1],
        recv_sem=recv_sems.at[3 * a + m - 1], device_id=(*_chip(m), c), device_id_type=MESH)
        for a in range(len(srcs)) for m in range(1, 4)]


HBM = pl.BlockSpec(memory_space=pltpu.HBM)
SEM = pl.BlockSpec(memory_space=pltpu.SEMAPHORE)
SPLIT_COPY = pltpu.CompilerParams(has_side_effects=pltpu.SideEffectType.DATAFLOW_SIDE_EFFECTING)


def _start_copies(make_copies, n_sems, srcs, zones, name):
    n = len(srcs)

    def body(*refs):
        for cp in make_copies(refs[:n], refs[n:2 * n], refs[2 * n], refs[2 * n + 1]):
            cp.start()
        refs[-1][...] = jnp.zeros_like(refs[-1])

    hbm = tuple(pltpu.HBM(b.shape, b.dtype) for b in list(srcs) + list(zones))
    out_shape = (pltpu.SemaphoreType.DMA((n_sems,)), pltpu.SemaphoreType.DMA((n_sems,))) + hbm + (jax.ShapeDtypeStruct((8, 128), F32),)
    operands = [pltpu.with_memory_space_constraint(b, pltpu.HBM) for b in srcs]
    operands += [pltpu.with_memory_space_constraint(lax.empty(z.shape, z.dtype), pltpu.HBM) for z in zones]
    res = _pcall(body, out_shape=out_shape, in_specs=[HBM] * (2 * n), out_specs=(SEM, SEM) + (HBM,) * (2 * n) + (VMEM,),
                 input_output_aliases={i: 2 + i for i in range(2 * n)}, name=name, compiler_params=SPLIT_COPY)(*operands)
    return (res[0], res[1], res[2:2 + n], res[2 + n:2 + 2 * n]), res[-1]


def _wait_copies(make_copies, flight, after, name):
    send_sems, recv_sems, srcs, zones = flight
    n = len(srcs)

    def body(*refs):
        for cp in make_copies(refs[:n], refs[n:2 * n], refs[2 * n], refs[2 * n + 1]):
            cp.wait_send()
            cp.wait_recv()

    hbm = tuple(pltpu.HBM(b.shape, b.dtype) for b in list(srcs) + list(zones))
    res = _pcall(body, out_shape=hbm, in_specs=[HBM] * (2 * n) + [SEM, SEM, ANY], out_specs=(HBM,) * (2 * n),
                 input_output_aliases={i: i for i in range(2 * n)}, name=name, compiler_params=SPLIT_COPY)(
        *srcs, *zones, send_sems, recv_sems, after)
    return res[:n], res[n:]


def _pair_sums(devs, grads, lands, n_steps, name):
    n = len(grads)
    rows = [l.shape[1] for l in lands]
    rbs = [r // n_steps for r in rows]

    def body(devs_ref, *refs):
        del devs_ref
        g_refs, land_refs, outs = refs[:4 * n], refs[4 * n:5 * n], refs[5 * n:]
        for a in range(n):
            outs[2 * a][...] = g_refs[4 * a][...] + land_refs[a][0]
            for m in range(1, 4):
                outs[2 * a + 1][m - 1] = (g_refs[4 * a + m][...] + land_refs[a][m]).astype(BF16)

    def block_of(m, per_dev):
        return lambda i, devs_ref: (devs_ref[m] * per_dev + i, 0)

    in_specs = [pl.BlockSpec((rb, l.shape[2]), block_of(m, n_steps)) for rb, l in zip(rbs, lands) for m in range(4)]
    in_specs += [pl.BlockSpec((4, rb, l.shape[2]), lambda i, devs_ref: (0, i, 0)) for rb, l in zip(rbs, lands)]
    out_shape, out_specs = [], []
    for rb, l in zip(rbs, lands):
        out_shape += [jax.ShapeDtypeStruct(l.shape[1:], F32), jax.ShapeDtypeStruct((3,) + l.shape[1:], BF16)]
        out_specs += [pl.BlockSpec((rb, l.shape[2]), lambda i, devs_ref: (i, 0)),
                      pl.BlockSpec((3, rb, l.shape[2]), lambda i, devs_ref: (0, i, 0))]
    grid_spec = pltpu.PrefetchScalarGridSpec(num_scalar_prefetch=1, grid=(n_steps,), in_specs=in_specs, out_specs=tuple(out_specs))
    res = _pcall(body, grid_spec=grid_spec, out_shape=tuple(out_shape), name=name,
                 compiler_params=_params(48, ("parallel",)))(devs, *[g for g in grads for _ in range(4)], *lands)
    return res[0::2], res[1::2]


def _final_sums(mine, lands, n_steps, name):
    n = len(mine)
    rbs = [o.shape[0] // n_steps for o in mine]

    def body(*refs):
        mine_refs, land_refs, outs = refs[:n], refs[n:2 * n], refs[2 * n:]
        for a in range(n):
            tot = mine_refs[a][...]
            for m in range(3):
                tot = tot + land_refs[a][m].astype(F32)
            outs[a][...] = tot

    in_specs = ([pl.BlockSpec((rb, o.shape[1]), lambda i: (i, 0)) for rb, o in zip(rbs, mine)]
                + [pl.BlockSpec((3, rb, o.shape[1]), lambda i: (0, i, 0)) for rb, o in zip(rbs, mine)])
    out_specs = tuple(pl.BlockSpec((rb, o.shape[1]), lambda i: (i, 0)) for rb, o in zip(rbs, mine))
    out_shape = tuple(jax.ShapeDtypeStruct(o.shape, F32) for o in mine)
    return _pcall(body, grid=(n_steps,), out_shape=out_shape, in_specs=in_specs, out_specs=out_specs, name=name,
                  compiler_params=_params(32, ("parallel",)))(*mine, *lands)


def _reduce_scatter_begin(big, small_after_start):
    c = lax.axis_index("c")
    devs = jnp.stack([_flat(*_chip(m), c) for m in range(4)]).astype(jnp.int32)
    flight, token = _start_copies(_sibling_copies, 4, [big], _sibling_zones([big]), "siblings_start")
    small = small_after_start(token)
    (big,), big_lands = _wait_copies(_sibling_copies, flight, small[-1], "siblings_wait")
    big_mine, big_send = _pair_sums(devs, [big], big_lands, 4, "pair_sums_w_in")
    big_flight, token = _start_copies(_chip_copies, 3, list(big_send), list(big_send), "chips_start_w_in")
    small_lands = _exchange_siblings(small, token, "exchange_siblings_rest")
    small_mine, small_send = _pair_sums(devs, small, small_lands, 1, "pair_sums_rest")
    small_flight, token = _start_copies(_chip_copies, 3 * len(small), list(small_send), list(small_send), "chips_start_rest")
    return (big_flight, small_flight, list(big_mine) + list(small_mine)), token


def _reduce_scatter_end(state, after):
    big_flight, small_flight, mine = state
    _, big_got = _wait_copies(_chip_copies, big_flight, after, "chips_wait_w_in")
    _, small_got = _wait_copies(_chip_copies, small_flight, after, "chips_wait_rest")
    small = _final_sums(mine[1:], small_got, 1, "final_sums_rest")
    return (mine[0], big_got[0]), list(small)


def _adamw(w, g, m, v):
    m_new = B1 * m + (1.0 - B1) * g
    v_new = B2 * v + (1.0 - B2) * (g * g)
    m_hat = m_new / (1.0 - B1 ** STEP)
    v_hat = v_new / (1.0 - B2 ** STEP)
    delta = -LR * (m_hat / (jnp.sqrt(v_hat) + EPS) + WD * w)
    return delta, m_new, v_new


def _final_sum_adam_rows(mine, land, w, m, v, n_steps, name):
    rows, ncol = w.shape
    blk = pl.BlockSpec((rows // n_steps, ncol), lambda i: (i, 0))

    def body(mine_ref, land_ref, w_ref, m_ref, v_ref, g_ref, d_ref, mo_ref, vo_ref):
        g = mine_ref[...]
        for k in range(3):
            g = g + land_ref[k].astype(F32)
        g_ref[...] = g
        d_ref[...], mo_ref[...], vo_ref[...] = _adamw(w_ref[...], g, m_ref[...], v_ref[...])

    shape = jax.ShapeDtypeStruct(w.shape, F32)
    return _pcall(body, grid=(n_steps,), out_shape=(shape,) * 4,
                  in_specs=[blk, pl.BlockSpec((3, rows // n_steps, ncol), lambda i: (0, i, 0)), blk, blk, blk],
                  out_specs=(blk,) * 4, name=name, compiler_params=_params(32, ("parallel",)))(mine, land, w, m, v)


def _adam_transposed(g_t, w, m, v, name):
    n, k = g_t.shape
    rb = min(k, 128)

    def body(gt_ref, w_ref, m_ref, v_ref, g_ref, d_ref, mo_ref, vo_ref):
        for src, skip, dst, size in _column_chunks(n):
            sl = pl.ds(dst, size)
            g = gt_ref[pl.ds(src, 128), :].T[:, skip:]
            delta, m_new, v_new = _adamw(w_ref[:, sl], g, m_ref[:, sl], v_ref[:, sl])
            g_ref[:, sl], d_ref[:, sl], mo_ref[:, sl], vo_ref[:, sl] = g, delta, m_new, v_new

    shape = jax.ShapeDtypeStruct(w.shape, F32)
    rows = pl.BlockSpec((rb, n), lambda i: (i, 0))
    return _pcall(body, grid=(k // rb,), out_shape=(shape,) * 4,
                  in_specs=[pl.BlockSpec((n, rb), lambda i: (0, i)), rows, rows, rows], out_specs=(rows,) * 4, name=name,
                  compiler_params=_params(32, ("parallel",)))(g_t, w, m, v)


def _adam_many(items, name):
    n = len(items)

    def body(*refs):
        ins, outs = refs[:4 * n], refs[4 * n:]
        for a in range(n):
            w_ref, g_ref, m_ref, v_ref = ins[4 * a:4 * a + 4]
            delta, m_new, v_new = _adamw(w_ref[...], g_ref[...], m_ref[...], v_ref[...])
            outs[3 * a][...], outs[3 * a + 1][...], outs[3 * a + 2][...] = delta, m_new, v_new

    out_shape = tuple(jax.ShapeDtypeStruct(it[0].shape, F32) for it in items for _ in range(3))
    flat = [arr for it in items for arr in it]
    res = _pcall(body, grid=(1,), out_shape=out_shape, in_specs=[_whole(a) for a in flat],
                 out_specs=tuple(_whole(o) for o in out_shape), name=name, compiler_params=_params(32))(*flat)
    return [tuple(res[3 * a:3 * a + 3]) for a in range(n)]


def _adam_w_ada(cact_all, dada_mine, w, m, v):
    def body(c_ref, d_ref, w_ref, m_ref, v_ref, g_ref, dl_ref, mo_ref, vo_ref):
        g = _tn(c_ref[...].astype(BF16), d_ref[...].astype(BF16))
        delta, m_new, v_new = _adamw(w_ref[...], g, m_ref[...], v_ref[...])
        g_ref[...], dl_ref[...], mo_ref[...], vo_ref[...] = g, delta, m_new, v_new

    shape = jax.ShapeDtypeStruct(w.shape, F32)
    operands = (cact_all, dada_mine, w, m, v)
    return _pcall(body, grid=(1,), out_shape=(shape,) * 4, in_specs=[_whole(a) for a in operands],
                  out_specs=(_whole(w),) * 4, name="adam_w_ada", compiler_params=_params(32))(*operands)


def kernel(x, c, w_ada, b_ada, w_in, b_in, conv_w, w_proj_attn, w_proj_conv, w_out, b_out, ln_g, ln_b, loss_target, m_w_ada, m_b_ada, m_w_in, m_b_in, m_conv_w, m_w_proj_attn, m_w_proj_conv, m_w_out, m_b_out, m_ln_g, m_ln_b, v_w_ada, v_b_ada, v_w_in, v_b_in, v_conv_w, v_w_proj_attn, v_w_proj_conv, v_w_out, v_b_out, v_ln_g, v_ln_b):
    nbat, seq, _ = x.shape
    t = nbat * seq
    me = _flat(*_my_position())
    x2, tgt2 = x.reshape(t, D), loss_target.reshape(t, D)
    sq = lambda a: a.reshape(a.shape[1:])

    tr = lambda a: a[0].T
    w_in_rows = tr(w_in)
    w_in_t_s = _cast_rows(w_in_rows, 4, "cast_w_in")
    w_pa_t_s, w_pb_s, w_out_s, cact_s, cw_s = _prep(sq(w_proj_attn), sq(w_proj_conv), sq(w_out), c, sq(conv_w))

    ncol = w_ada.shape[2]
    b_ada_mine = lax.dynamic_slice(b_ada, (0, me * ncol), (1, ncol))
    ada_slots, cact_slots, cw_slots = _ada_forward(cact_s, cw_s, sq(w_ada), b_ada_mine)
    cact_all = cact_slots[:, :nbat].reshape(N_DEV * nbat, D)
    cw = cw_slots[:, :3].transpose(1, 0, 2).reshape(3, D)
    ada_all = ada_slots[:, :, :nbat].transpose(1, 2, 0, 3).reshape(N_DEV * nbat, 3, D)
    ada = lax.dynamic_slice(ada_all, (me * nbat, 0, 0), (nbat, 3, D))

    h = _make_h(x2, ada)
    w_in_t, qkv, rest, (w_pa_t, w_pb, w_o) = _project_gather(w_in_t_s, h, b_in.reshape(N_SLAB, 1, SLAB), [w_pa_t_s, w_pb_s, w_out_s])
    ols = [_attn_forward(qkv, g, nbat) for g in range(3)]
    (dproj, gx0, do_attn, ol_tot, merged, do_f, bbs, dyc, a_bf, dya, gb_rest, svec, dgate) = _mid(
        rest, ols, x2, tgt2, ada, cw, b_out, ln_g, ln_b, w_pa_t, w_pb, w_o)

    gb_qkv = []
    for g in range(3):
        dproj, gb = _attn_backward(qkv, do_attn, ol_tot, dproj, g, nbat)
        gb_qkv.append(gb)
    g_w_in_t = _grad_w_in_t(dproj, h)

    def small_grads(token):
        g_w_out = _grad_rows_2d(merged, do_f, "grad_w_out", token)
        g_w_pb = _grad_rows_2d(bbs, dyc, "grad_w_proj_conv", g_w_out)
        g_w_pa_t = _grad_rows_2d(dya, a_bf, "grad_w_proj_attn", g_w_pb)
        return [g_w_out, g_w_pb, g_w_pa_t]

    rs_state, token = _reduce_scatter_begin(g_w_in_t, small_grads)
    grad_x, dss = _grad_h(dproj, w_in_t, gx0, x2, ada, token)

    rows8, tot, g_bada = _small_reduce(gb_rest, gb_qkv, svec, dgate, dss)
    (g_in_mine, g_in_got), (g_out, g_pb, g_pa_t) = _reduce_scatter_end(rs_state, tot)
    loss = tot[0, P_LOSS]
    dada_all = rows8[:, 0, P_DADA:].reshape(N_DEV * nbat, 3 * D)
    dada_mine = lax.dynamic_slice(dada_all, (0, me * ncol), (N_DEV * nbat, ncol))

    g_in_t, d_win_t, nm_win_t, nv_win_t = _final_sum_adam_rows(g_in_mine, g_in_got, w_in_rows, tr(m_w_in), tr(v_w_in), 4, "adam_w_in")
    g_win, d_win, nm_win, nv_win = g_in_t.T, d_win_t.T, nm_win_t.T, nv_win_t.T
    g_wpa, d_wpa, nm_wpa, nv_wpa = _adam_transposed(g_pa_t, sq(w_proj_attn), sq(m_w_proj_attn), sq(v_w_proj_attn), "adam_w_proj_attn")
    g_wada, d_wada, nm_wada, nv_wada = _adam_w_ada(cact_all, dada_mine, sq(w_ada), sq(m_w_ada), sq(v_w_ada))
    g_bin = tot[:, P_BIN:P_BIN + D_IN]
    g_bout = tot[:, P_BOUT:P_BOUT + D]
    g_lng = tot[:, P_LNG:P_LNG + D]
    g_lnb = tot[:, P_LNB:P_LNB + D]
    g_conv = lax.dynamic_slice(tot[:, P_CONV:P_CONV + 3 * D].reshape(3, D), (0, me * cw_s.shape[1]), (3, cw_s.shape[1]))
    upd = _adam_many([
        (sq(w_proj_conv), g_pb, sq(m_w_proj_conv), sq(v_w_proj_conv)),
        (sq(w_out), g_out, sq(m_w_out), sq(v_w_out)),
        (b_ada, g_bada, m_b_ada, v_b_ada), (b_in, g_bin, m_b_in, v_b_in), (sq(conv_w), g_conv, sq(m_conv_w), sq(v_conv_w)),
        (b_out, g_bout, m_b_out, v_b_out), (ln_g, g_lng, m_ln_g, v_ln_g), (ln_b, g_lnb, m_ln_b, v_ln_b)], "adam_rest")
    (d_wpb, nm_wpb, nv_wpb), (d_wout, nm_wout, nv_wout), (d_bada, nm_bada, nv_bada), (d_bin, nm_bin, nv_bin), \
        (d_conv, nm_conv, nv_conv), (d_bout, nm_bout, nv_bout), (d_lng, nm_lng, nv_lng), (d_lnb, nm_lnb, nv_lnb) = upd

    ex = lambda a: a.reshape((1,) + a.shape)
    grads = [ex(g_wada), g_bada, ex(g_win), g_bin, ex(g_conv), ex(g_wpa), ex(g_pb), ex(g_out), g_bout, g_lng, g_lnb]
    deltas = [ex(d_wada), d_bada, ex(d_win), d_bin, ex(d_conv), ex(d_wpa), ex(d_wpb), ex(d_wout), d_bout, d_lng, d_lnb]
    new_m = [ex(nm_wada), nm_bada, ex(nm_win), nm_bin, ex(nm_conv), ex(nm_wpa), ex(nm_wpb), ex(nm_wout), nm_bout, nm_lng, nm_lnb]
    new_v = [ex(nv_wada), nv_bada, ex(nv_win), nv_bin, ex(nv_conv), ex(nv_wpa), ex(nv_wpb), ex(nv_wout), nv_bout, nv_lng, nv_lnb]
    return (loss, grad_x.reshape(x.shape), *grads, *deltas, *new_m, *new_v)
```

```python
import functools

import jax
import jax.numpy as jnp
from jax import lax
from jax.experimental import pallas as pl
from jax.experimental.pallas import tpu as pltpu

F32, BF16 = jnp.float32, jnp.bfloat16
MESH = pl.DeviceIdType.MESH
N_DEV = 8
D = 1024
SLAB = 256
N_QKV, N_REST = 9, 25
N_SLAB = N_QKV + N_REST
D_IN = N_SLAB * SLAB
DP_SLABS = 36
BLK = 128
GROUPS = ((128, 1), (512, 4), (2048, 16))
ALPHA = 2.0 ** 0.25
LN_EPS = 1e-5
LR, B1, B2, EPS, WD, STEP = 0.001, 0.9, 0.999, 1e-08, 0.01, 10
R_ZA, R_UX, R_GB, R_GC, R_ZC, R_GA, R_GBM = 0, 1, 5, 9, 13, 17, 21
P_BIN, P_BOUT, P_LNG, P_LNB, P_CONV, P_LOSS, P_DADA = 0, 8704, 9728, 10752, 11776, 14848, 14976
MIB = 1024 * 1024


def _pcall(body, *, out_shape, out_specs=None, **kw):
    def pin_out(shape, spec):
        blocked = isinstance(shape, jax.ShapeDtypeStruct) and getattr(spec, "block_shape", None) is not None
        return pltpu.HBM(shape.shape, shape.dtype) if blocked else shape

    n_scalar = 0
    if out_specs is None:
        specs = kw["grid_spec"].out_specs
        n_scalar = kw["grid_spec"].num_scalar_prefetch
    else:
        kw["out_specs"] = specs = out_specs
    if isinstance(out_shape, (tuple, list)):
        out_shape = tuple(pin_out(s, p) for s, p in zip(out_shape, specs))
    else:
        out_shape = pin_out(out_shape, specs)
    call = pl.pallas_call(body, out_shape=out_shape, **kw)

    def run(*operands):
        def pin(o):
            is_data = jnp.issubdtype(o.dtype, jnp.floating) or jnp.issubdtype(o.dtype, jnp.integer)
            return pltpu.with_memory_space_constraint(o, pltpu.HBM) if is_data else o
        return call(*operands[:n_scalar], *[pin(o) for o in operands[n_scalar:]])

    return run

ANY = pl.BlockSpec(memory_space=pl.ANY)
VMEM = pl.BlockSpec(memory_space=pltpu.VMEM)


def _whole(a):
    return pl.BlockSpec(a.shape, lambda i: (0,) * len(a.shape))


def _params(vmem_mib=None, sem=None):
    kw = {}
    if vmem_mib is not None:
        kw["vmem_limit_bytes"] = vmem_mib * MIB
    if sem is not None:
        kw["dimension_semantics"] = sem
    return pltpu.CompilerParams(**kw)


def _nn(a, b):
    return jnp.dot(a, b, preferred_element_type=F32)


def _nt(a, b):
    return lax.dot_general(a, b, (((1,), (1,)), ((), ())), preferred_element_type=F32)


def _tn(a, b):
    return lax.dot_general(a, b, (((0,), (0,)), ((), ())), preferred_element_type=F32)


def _sigmoid(v):
    return 1.0 / (1.0 + jnp.exp(-v))


def _part8(v):
    return v.reshape(v.shape[0] // 8, 8, v.shape[1]).sum(axis=0)


def _my_position():
    return lax.axis_index("x"), lax.axis_index("y"), lax.axis_index("c")


def _flat(px, py, pc):
    return 4 * px + 2 * py + pc


def _peer(mask):
    x, y, c = _my_position()
    return (x ^ ((mask >> 2) & 1), y ^ ((mask >> 1) & 1), c ^ (mask & 1))


def _column_chunks(n):
    chunks = [(128 * a, 0, 128 * a, 128) for a in range(n // 128)]
    if n % 128:
        chunks.append((n - 128, 128 - n % 128, 128 * (n // 128), n % 128))
    return chunks


def _cast_rows(w, n_steps, name):
    rows, ncol = w.shape
    blk = pl.BlockSpec((rows // n_steps, ncol), lambda i: (i, 0))

    def body(w_ref, o_ref):
        o_ref[...] = w_ref[...].astype(BF16)

    return _pcall(body, grid=(n_steps,), out_shape=jax.ShapeDtypeStruct(w.shape, BF16), in_specs=[blk], out_specs=blk,
                  name=name, compiler_params=_params(16, ("parallel",)))(w)


def _prep(w_pa, w_pb, w_out, c, conv_w):
    def body(wpa_ref, wpb_ref, wout_ref, c_ref, cw_ref, wpat_ref, wpb_o, wout_o, cact_ref, cwp_ref):
        wpat_ref[...] = wpa_ref[...].T.astype(BF16)
        wpb_o[...] = wpb_ref[...].astype(BF16)
        wout_o[...] = wout_ref[...].astype(BF16)
        cv = c_ref[...]
        cact_ref[...] = jnp.zeros_like(cact_ref)
        cact_ref[pl.ds(0, cv.shape[0]), :] = cv * _sigmoid(cv)
        cwp_ref[...] = jnp.zeros_like(cwp_ref)
        cwp_ref[pl.ds(0, 3), :] = cw_ref[...]

    out_shape = (jax.ShapeDtypeStruct((w_pa.shape[1], w_pa.shape[0]), BF16),
                 jax.ShapeDtypeStruct(w_pb.shape, BF16), jax.ShapeDtypeStruct(w_out.shape, BF16),
                 jax.ShapeDtypeStruct((8, D), F32), jax.ShapeDtypeStruct((8, conv_w.shape[1]), F32))
    operands = (w_pa, w_pb, w_out, c, conv_w)
    return _pcall(body, grid=(1,), out_shape=out_shape, in_specs=[_whole(a) for a in operands],
                  out_specs=tuple(_whole(o) for o in out_shape), name="prep", compiler_params=_params(16))(*operands)


def _exchange_slots(out_refs, send_sems, recv_sems, base=0):
    me = _flat(*_my_position())

    def copy(a, mask, slot):
        return pltpu.make_async_remote_copy(
            src_ref=out_refs[a].at[slot], dst_ref=out_refs[a].at[slot], send_sem=send_sems.at[base + 7 * a + mask - 1],
            recv_sem=recv_sems.at[base + 7 * a + mask - 1], device_id=_peer(mask), device_id_type=MESH)

    pairs = [(a, mask) for a in range(len(out_refs)) for mask in range(1, N_DEV)]
    for a, mask in pairs:
        copy(a, mask, me).start()
    for a, mask in pairs:
        copy(a, mask, _flat(*_peer(mask))).wait_recv()
    for a, mask in pairs:
        copy(a, mask, me).wait_send()


def _ada_forward(cact_mine, cw_mine, w_ada, b_ada_mine):
    ncol = w_ada.shape[1]

    def body(c_ref, cw_ref, w_ref, b_ref, out_ref, call_ref, cwall_ref, send_sems, recv_sems):
        me = _flat(*_my_position())
        call_ref[me] = c_ref[...]
        cwall_ref[me] = cw_ref[...]
        _exchange_slots([call_ref, cwall_ref], send_sems, recv_sems)
        c_all = call_ref[...].reshape(N_DEV * 8, D).astype(BF16)
        out_ref[me] = (_nn(c_all, w_ref[...].astype(BF16)) + b_ref[...]).reshape(N_DEV, 8, ncol)
        _exchange_slots([out_ref], send_sems, recv_sems, base=14)

    operands = (cact_mine, cw_mine, w_ada, b_ada_mine)
    out_shape = (jax.ShapeDtypeStruct((N_DEV, N_DEV, 8, ncol), F32), jax.ShapeDtypeStruct((N_DEV, 8, D), F32),
                 jax.ShapeDtypeStruct((N_DEV,) + cw_mine.shape, F32))
    return _pcall(body, grid=(1,), out_shape=out_shape, in_specs=[_whole(a) for a in operands], out_specs=(VMEM,) * 3,
                  scratch_shapes=[pltpu.SemaphoreType.DMA((21,)), pltpu.SemaphoreType.DMA((21,))], name="ada_forward",
                  compiler_params=_params(16))(*operands)


def _small_reduce(gb_rest, gb_qkv, svec, dgate, dss):
    nbat = dgate.shape[0]

    def body(gbr_ref, q0_ref, q1_ref, q2_ref, sv_ref, dg_ref, dss_ref, rows_ref, tot_ref, gbada_ref, send_sems, recv_sems):
        me = _flat(*_my_position())

        def put(off, v):
            rows_ref[me, :, pl.ds(off, v.shape[1])] = v

        def row(v):
            return jnp.sum(v, axis=0, keepdims=True)

        for g, q_ref in enumerate((q0_ref, q1_ref, q2_ref)):
            for which in range(3):
                put(P_BIN + SLAB * (3 * which + g), row(q_ref[which]))
        for s in range(N_REST):
            put(P_BIN + SLAB * (N_QKV + s), row(gbr_ref[s]))
        put(P_LNG, row(sv_ref[0]))
        put(P_LNB, row(sv_ref[1]))
        put(P_BOUT, row(sv_ref[2]))
        for j in range(3):
            put(P_CONV + D * j, row(sv_ref[3 + j]))
        loss = (0.5 / D) * jnp.sum(row(sv_ref[6]), axis=1, keepdims=True)
        put(P_LOSS, jnp.broadcast_to(loss, (1, 128)))
        for b in range(nbat):
            put(P_DADA + 3 * D * b, row(dss_ref[b, 0]))
            put(P_DADA + 3 * D * b + D, row(dss_ref[b, 1]))
            put(P_DADA + 3 * D * b + 2 * D, row(dg_ref[b]))
        _exchange_slots([rows_ref], send_sems, recv_sems)
        tot = rows_ref[0]
        for k in range(1, N_DEV):
            tot = tot + rows_ref[k]
        tot_ref[...] = tot
        gbada = tot[:, P_DADA:P_DADA + 3 * D]
        for b in range(1, nbat):
            gbada = gbada + tot[:, P_DADA + 3 * D * b:P_DADA + 3 * D * (b + 1)]
        gbada_ref[...] = gbada

    p_len = P_DADA + nbat * 3 * D
    out_shape = (jax.ShapeDtypeStruct((N_DEV, 1, p_len), F32), jax.ShapeDtypeStruct((1, p_len), F32),
                 jax.ShapeDtypeStruct((1, 3 * D), F32))
    operands = (gb_rest, *gb_qkv, svec, dgate, dss)
    return _pcall(body, grid=(1,), out_shape=out_shape, in_specs=[_whole(a) for a in operands],
                  out_specs=(VMEM, _whole(out_shape[1]), _whole(out_shape[2])),
                  scratch_shapes=[pltpu.SemaphoreType.DMA((7,)), pltpu.SemaphoreType.DMA((7,))], name="small_reduce",
                  compiler_params=_params(16))(*operands)


def _make_h(x, ada, tm=512):
    t = x.shape[0]
    tps = (t // ada.shape[0]) // tm

    def body(x_ref, ada_ref, h_ref):
        h_ref[...] = (x_ref[...] * (1.0 + ada_ref[0, 1:2, :]) + ada_ref[0, 0:1, :]).astype(BF16)

    return _pcall(body, grid=(t // tm,), out_shape=jax.ShapeDtypeStruct((t, D), BF16),
                  in_specs=[pl.BlockSpec((tm, D), lambda i: (i, 0)), pl.BlockSpec((1, 3, D), lambda i: (i // tps, 0, 0))],
                  out_specs=pl.BlockSpec((tm, D), lambda i: (i, 0)), name="make_h",
                  compiler_params=_params(32, ("parallel",)))(x, ada)


PIECE = 64
N_CHUNK = 4
ARRIVAL_RANK = (0, 1, 3, 5, 2, 4, 6, 7)
SLOT_MASK = (1, 4, 2, 6, 5, 3, 7)


def _arrival_tables(shard_rows):
    import numpy as np
    crow = shard_rows // N_CHUNK
    table = np.zeros((N_DEV, N_SLAB + 7 * N_CHUNK), np.int32)
    lo = [(SLAB * j) // crow for j in range(N_SLAB)]
    hi = [(SLAB * j + SLAB - 1) // crow for j in range(N_SLAB)]
    for k in range(N_DEV):
        def rank(chunk):
            shard_rank = ARRIVAL_RANK[(chunk // N_CHUNK) ^ k]
            return shard_rank if shard_rank < 2 else 2 + 8 * (chunk % N_CHUNK) + shard_rank
        order = sorted(range(N_SLAB), key=lambda j: (max(rank(lo[j]), rank(hi[j])), j))
        table[k, :N_SLAB] = order
        for slot, mask in enumerate(SLOT_MASK):
            for ch in range(N_CHUNK):
                chunk = (k ^ mask) * N_CHUNK + ch
                table[k, N_SLAB + slot * N_CHUNK + ch] = min(t for t, j in enumerate(order) if lo[j] <= chunk <= hi[j])
    return table


def _project_gather(shard, h, b_in3, others):
    t = h.shape[0]
    n_o = len(others)
    srows = shard.shape[0]
    crow = srows // N_CHUNK
    shards = [shard] + list(others)
    table = jnp.asarray(_arrival_tables(srows))

    def body(tbl_ref, *refs):
        srcs = [refs[0]] + list(refs[3:3 + n_o])
        h_ref, b_ref = refs[1], refs[2]
        outs = [refs[3 + n_o]] + list(refs[6 + n_o:6 + 2 * n_o])
        qkv_ref, rest_ref = refs[4 + n_o], refs[5 + n_o]
        (wtile, obf, of32, send_sems, recv_sems, local_sems, tile_sems, obf_sems, of32_sems) = refs[6 + 2 * n_o:]
        w_full = outs[0]
        x, y, c = _my_position()
        k = _flat(x, y, c)
        me, sibling = (x, y, c), (x, y, 1 - c)
        chips = [(1 - x, y), (x, 1 - y), (1 - x, 1 - y)]

        def rows(a, px, py, pc, ch):
            r = shards[a].shape[0]
            if ch is None:
                return outs[a].at[pl.ds(pl.multiple_of(_flat(px, py, pc) * r, r), r), :]
            return outs[a].at[pl.ds(pl.multiple_of(_flat(px, py, pc) * r + ch * crow, crow), crow), :]

        def copy(a, slot, block, to, ch=None, src=None):
            sem = slot * N_CHUNK + ch if a == 0 else 7 * (N_CHUNK - 1 + a) + slot
            if src is not None and ch is not None:
                src = src.at[pl.ds(ch * crow, crow), :]
            return pltpu.make_async_remote_copy(
                src_ref=rows(a, *block, ch) if src is None else src, dst_ref=rows(a, *block, ch),
                send_sem=send_sems.at[sem], recv_sem=recv_sems.at[sem], device_id=to, device_id_type=MESH)

        mine = [pltpu.make_async_copy(srcs[a], rows(a, *me, None), local_sems.at[a]) for a in range(1 + n_o)]
        first = []
        for ch in range(N_CHUNK):
            first.append(copy(0, 0, me, sibling, ch, src=srcs[0]))
            first += [copy(0, 1 + j, me, (*chip, c), ch, src=srcs[0]) for j, chip in enumerate(chips)]
        for a in range(1, 1 + n_o):
            first.append(copy(a, 0, me, sibling, src=srcs[a]))
            first += [copy(a, 1 + j, me, (*chip, c), src=srcs[a]) for j, chip in enumerate(chips)]
        for cp in mine + first:
            cp.start()

        def arrive(a, slot, ch=None):
            if slot == 0:
                copy(a, 0, sibling, me, ch).wait_recv()
            elif slot < 4:
                copy(a, slot, (*chips[slot - 1], c), me, ch).wait_recv()
                copy(a, slot + 3, (*chips[slot - 1], c), sibling, ch).start()
            else:
                copy(a, slot, (*chips[slot - 4], 1 - c), me, ch).wait_recv()

        def arrive_for(step):
            for slot in range(7):
                for ch in range(N_CHUNK):
                    @pl.when(tbl_ref[k, N_SLAB + slot * N_CHUNK + ch] == step)
                    def _():
                        arrive(0, slot, ch)

        def fetch(step, buf):
            slab = tbl_ref[k, step]
            for p in range(SLAB // PIECE):
                g0 = slab * SLAB + PIECE * p
                own = (g0 >= k * srows) & (g0 < (k + 1) * srows)
                dst = wtile.at[buf, pl.ds(PIECE * p, PIECE), :]

                @pl.when(own)
                def _():
                    pltpu.make_async_copy(srcs[0].at[pl.ds(pl.multiple_of(g0 - k * srows, PIECE), PIECE), :], dst, tile_sems.at[buf]).start()

                @pl.when(jnp.logical_not(own))
                def _():
                    pltpu.make_async_copy(w_full.at[pl.ds(pl.multiple_of(g0, PIECE), PIECE), :], dst, tile_sems.at[buf]).start()

        def wait_tile(buf):
            pltpu.make_async_copy(w_full.at[pl.ds(0, SLAB), :], wtile.at[buf], tile_sems.at[buf]).wait()

        def put(buf_ref, sems, dst_ref, count, value):
            b = count % 2

            @pl.when(count >= 2)
            def _():
                pltpu.make_async_copy(buf_ref.at[b], dst_ref, sems.at[b]).wait()

            buf_ref[b] = value
            pltpu.make_async_copy(buf_ref.at[b], dst_ref, sems.at[b]).start()

        def drain(buf_ref, sems, dst_ref, count):
            for back in (1, 2):
                @pl.when(count >= back)
                def _():
                    pltpu.make_async_copy(buf_ref.at[(count - back) % 2], dst_ref, sems.at[(count - back) % 2]).wait()

        arrive_for(0)
        fetch(0, 0)

        def step(s, carry):
            n_bf, n_f32 = carry
            buf = s % 2

            @pl.when(s + 1 < N_SLAB)
            def _():
                arrive_for(s + 1)
                fetch(s + 1, 1 - buf)

            wait_tile(buf)
            slab = tbl_ref[k, s]
            v = _nt(h_ref[...], wtile[buf]) + b_ref[slab]
            is_qkv = slab < N_QKV

            @pl.when(is_qkv)
            def _():
                put(obf, obf_sems, qkv_ref.at[jnp.minimum(slab, N_QKV - 1)], n_bf, v.astype(BF16))

            @pl.when(jnp.logical_not(is_qkv))
            def _():
                put(of32, of32_sems, rest_ref.at[jnp.maximum(slab - N_QKV, 0)], n_f32, v)

            return n_bf + is_qkv.astype(jnp.int32), n_f32 + 1 - is_qkv.astype(jnp.int32)

        n_bf, n_f32 = lax.fori_loop(0, N_SLAB, step, (jnp.int32(0), jnp.int32(0)))
        drain(obf, obf_sems, qkv_ref.at[0], n_bf)
        drain(of32, of32_sems, rest_ref.at[0], n_f32)

        for slots in ((1, 2, 3), (0, 4, 5, 6)):
            for a in range(1, 1 + n_o):
                for slot in slots:
                    arrive(a, slot)
        for cp in first:
            cp.wait_send()
        for j, chip in enumerate(chips):
            for ch in range(N_CHUNK):
                copy(0, 4 + j, (*chip, c), sibling, ch).wait_send()
            for a in range(1, 1 + n_o):
                copy(a, 4 + j, (*chip, c), sibling).wait_send()
        for cp in mine:
            cp.wait()

    out_shape = ((jax.ShapeDtypeStruct((N_DEV * srows, D), BF16), jax.ShapeDtypeStruct((N_QKV, t, SLAB), BF16),
                  jax.ShapeDtypeStruct((N_REST, t, SLAB), F32))
                 + tuple(jax.ShapeDtypeStruct((N_DEV * o.shape[0], o.shape[1]), o.dtype) for o in others))
    n_all = 1 + n_o
    n_sems = 7 * (N_CHUNK + n_o)
    grid_spec = pltpu.PrefetchScalarGridSpec(
        num_scalar_prefetch=1, grid=(1,),
        in_specs=[ANY, pl.BlockSpec((t, D), lambda i, tbl: (0, 0), pipeline_mode=pl.Buffered(1)),
                  pl.BlockSpec((N_SLAB, 1, SLAB), lambda i, tbl: (0, 0, 0))] + [ANY] * n_o,
        out_specs=(ANY,) * (3 + n_o),
        scratch_shapes=[pltpu.VMEM((2, SLAB, D), BF16), pltpu.VMEM((2, t, SLAB), BF16), pltpu.VMEM((2, t, SLAB), F32),
                        pltpu.SemaphoreType.DMA((n_sems,)), pltpu.SemaphoreType.DMA((n_sems,)),
                        pltpu.SemaphoreType.DMA((n_all,)), pltpu.SemaphoreType.DMA((2,)), pltpu.SemaphoreType.DMA((2,)),
                        pltpu.SemaphoreType.DMA((2,))])
    res = _pcall(body, grid_spec=grid_spec, out_shape=out_shape, name="project_gather",
                 compiler_params=_params(48, ("arbitrary",)))(table, shard, h, b_in3, *others)
    return res[0], res[1], res[2], list(res[3:])


def _bias_tables(g):
    window, dil = GROUPS[g]
    span = window // dil
    qi = jnp.arange(BLK)[:, None]
    kj = jnp.arange(2 * BLK)[None, :]
    delta = qi + BLK - kj
    valid = (delta >= 0) & (delta <= span)
    heads = jnp.arange(4, dtype=F32) + 4.0 * g
    slopes = 2.0 ** (-8.0 * (heads + 1.0) / 12.0)
    bias = -slopes[:, None, None] * (delta * dil).astype(F32)[None]
    return jnp.where(valid[None], bias, -1e30).reshape(4 * BLK, 2 * BLK)


def _head_masks(shape):
    lane = lax.broadcasted_iota(jnp.int32, shape, 1)
    return [(lane >= 64 * h) & (lane < 64 * (h + 1)) for h in range(4)]


def _stack_heads(v, masks):
    return jnp.concatenate([jnp.where(masks[h], v, jnp.zeros_like(v)) for h in range(4)], axis=0)


def _unstack_heads(v4, masks):
    out = jnp.where(masks[0], v4[0:BLK], 0.0)
    for h in range(1, 4):
        out = jnp.where(masks[h], v4[BLK * h:BLK * (h + 1)], out)
    return out


def _regroup(load_half, dst_ref, stage_ref, n, dil):
    for hlf in range(2):
        stage_ref[hlf] = load_half(hlf)

    def residue(r, carry):
        for hlf in range(2):
            dst_ref[pl.ds(pl.multiple_of(r * n, BLK), n), pl.ds(128 * hlf, 128)] = (
                stage_ref[hlf, pl.ds(r, n, stride=dil), :].astype(dst_ref.dtype))
        return carry

    lax.fori_loop(0, dil, residue, 0)


def _store_block(nat_ref, r, i, val, dil):
    for hlf in range(2):
        nat_ref[hlf, pl.ds(r + dil * BLK * i, BLK, stride=dil), :] = val[:, 128 * hlf:128 * (hlf + 1)]


def _for_blocks(block, dil, nblk):
    if dil == 1:
        block(0, 0, True)
        block(0, 1, False)

        def pair(k, carry):
            block(0, 2 * k, False)
            block(0, 2 * k + 1, False)
            return carry

        lax.fori_loop(1, nblk // 2, pair, 0)
    else:
        def residues(k, carry):
            block(2 * k, 0, True)
            block(2 * k + 1, 0, True)
            if nblk > 1:
                def loop(i, c):
                    block(2 * k, i, False)
                    block(2 * k + 1, i, False)
                    return c
                lax.fori_loop(1, nblk, loop, 0)
            return carry

        lax.fori_loop(0, dil // 2, residues, 0)


def _attn_forward(qkv, g, nbat, after):
    t = qkv.shape[1]
    seq = t // nbat
    dil = GROUPS[g][1]
    n = seq // dil
    nblk = n // BLK
    qkv4 = qkv.reshape(3, 3, t, SLAB)

    def body(qkv_ref, bias_ref, after_ref, ol_ref, *scratch):
        del after_ref
        masks = _head_masks((BLK, SLAB))
        if dil > 1:
            stage, qd, kd, vd, nat_o, nat_l = scratch
            for which, dst in enumerate((qd, kd, vd)):
                _regroup(lambda hlf, which=which: qkv_ref[which, 0, :, pl.ds(128 * hlf, 128)].astype(F32), dst, stage, n, dil)
        else:
            qd, kd, vd = qkv_ref.at[0, 0], qkv_ref.at[1, 0], qkv_ref.at[2, 0]

        def block(r, i, first):
            base = r * n
            qs = pl.ds(pl.multiple_of(base + i * BLK, BLK), BLK)
            ks = pl.ds(pl.multiple_of(base, BLK), BLK) if first else pl.ds(pl.multiple_of(base + (i - 1) * BLK, BLK), 2 * BLK)
            q, kk, vv = qd[qs, :], kd[ks, :], vd[ks, :]
            bias = bias_ref[:, pl.ds(BLK, BLK)] if first else bias_ref[...]
            s = _nt(_stack_heads(q, masks), kk) * 0.125 + bias
            m = jnp.max(s, axis=1, keepdims=True)
            p = jnp.exp(s - m)
            den = jnp.sum(p, axis=1, keepdims=True)
            out = _unstack_heads(_nn((p * (1.0 / den)).astype(BF16), vv), masks)
            lse = _unstack_heads(jnp.broadcast_to(m + jnp.log(den), (4 * BLK, SLAB)), masks)
            if dil > 1:
                _store_block(nat_o, r, i, out, dil)
                _store_block(nat_l, r, i, lse, dil)
            else:
                ol_ref[0, qs, :] = out
                ol_ref[1, qs, :] = lse

        _for_blocks(block, dil, nblk)
        if dil > 1:
            for hlf in range(2):
                ol_ref[0, :, pl.ds(128 * hlf, 128)] = nat_o[hlf]
                ol_ref[1, :, pl.ds(128 * hlf, 128)] = nat_l[hlf]

    scratch = []
    if dil > 1:
        scratch = [pltpu.VMEM((2, seq, 128), F32)] + [pltpu.VMEM((seq, SLAB), BF16)] * 3 + [pltpu.VMEM((2, seq, 128), F32)] * 2
    return _pcall(
        body, grid=(nbat,), out_shape=jax.ShapeDtypeStruct((2, t, SLAB), F32),
        in_specs=[pl.BlockSpec((3, 1, seq, SLAB), lambda b: (0, g, b, 0)),
                  pl.BlockSpec((4 * BLK, 2 * BLK), lambda b: (0, 0)), ANY],
        out_specs=pl.BlockSpec((2, seq, SLAB), lambda b: (0, b, 0)), scratch_shapes=scratch,
        name=f"attn_forward_{g}", compiler_params=_params(40, ("parallel",)))(qkv4, _bias_tables(g), after)


def _attn_backward(qkv, do_attn, ol_tot, dproj, g, nbat):
    t = qkv.shape[1]
    seq = t // nbat
    dil = GROUPS[g][1]
    n = seq // dil
    nblk = n // BLK
    qkv4 = qkv.reshape(3, 3, t, SLAB)
    dp4 = dproj.reshape(DP_SLABS // 3, 3, t, SLAB)

    def body(qkv_ref, do_ref, ol_ref, bias_ref, dp_in, dp_ref, gb_ref, dk_acc, dv_acc, *scratch):
        del dp_in
        masks = _head_masks((BLK, SLAB))

        @pl.when(pl.program_id(0) == 0)
        def _():
            gb_ref[...] = jnp.zeros_like(gb_ref)

        dk_acc[...] = jnp.zeros_like(dk_acc)
        dv_acc[...] = jnp.zeros_like(dv_acc)
        if dil > 1:
            stage, qd, kd, vd, dod, prodd, lsed, nat = scratch
            lanes = lambda hlf: pl.ds(128 * hlf, 128)
            for which, dst in enumerate((qd, kd, vd)):
                _regroup(lambda hlf, which=which: qkv_ref[which, 0, :, lanes(hlf)].astype(F32), dst, stage, n, dil)
            _regroup(lambda hlf: do_ref[:, lanes(hlf)].astype(F32), dod, stage, n, dil)
            _regroup(lambda hlf: do_ref[:, lanes(hlf)].astype(F32) * ol_ref[0, :, lanes(hlf)], prodd, stage, n, dil)
            _regroup(lambda hlf: ol_ref[1, :, lanes(hlf)], lsed, stage, n, dil)
        else:
            qd, kd, vd = qkv_ref.at[0, 0], qkv_ref.at[1, 0], qkv_ref.at[2, 0]

        def block(r, i, first):
            base = r * n
            qs = pl.ds(pl.multiple_of(base + i * BLK, BLK), BLK)
            ks = pl.ds(pl.multiple_of(base, BLK), BLK) if first else pl.ds(pl.multiple_of(base + (i - 1) * BLK, BLK), 2 * BLK)
            q, kk, vv = qd[qs, :], kd[ks, :], vd[ks, :]
            if dil > 1:
                do, prod, lse = dod[qs, :], prodd[qs, :], lsed[qs, :]
            else:
                do = do_ref[qs, :]
                prod = do.astype(F32) * ol_ref[0, qs, :]
                lse = ol_ref[1, qs, :]
            q4, do4 = _stack_heads(q, masks), _stack_heads(do, masks)
            bias = bias_ref[:, pl.ds(BLK, BLK)] if first else bias_ref[...]
            lse4 = jnp.concatenate([lse[:, 64 * h:64 * h + 1] for h in range(4)], axis=0)
            delta4 = jnp.concatenate([jnp.sum(jnp.where(masks[h], prod, 0.0), axis=1, keepdims=True) for h in range(4)], axis=0)
            p = jnp.exp(_nt(q4, kk) * 0.125 + bias - lse4)
            ds = (p * (_nt(do4, vv) - delta4)).astype(BF16)
            dv_acc[ks, :] += _tn(p.astype(BF16), do4)
            dk_acc[ks, :] += _tn(ds, q4) * 0.125
            dq = _unstack_heads(_nn(ds, kk), masks) * 0.125
            if dil > 1:
                _store_block(nat, r, i, dq, dil)
            else:
                dp_ref[0, 0, qs, :] = dq.astype(BF16)
            gb_ref[0] += _part8(dq)

        _for_blocks(block, dil, nblk)
        gb_ref[1] += _part8(dk_acc[...])
        gb_ref[2] += _part8(dv_acc[...])
        if dil > 1:
            def flush(which):
                for hlf in range(2):
                    dp_ref[which, 0, :, pl.ds(128 * hlf, 128)] = nat[hlf].astype(BF16)

            def to_token_order(acc_ref):
                def residue(r, carry):
                    for hlf in range(2):
                        nat[hlf, pl.ds(r, n, stride=dil), :] = acc_ref[pl.ds(pl.multiple_of(r * n, BLK), n), pl.ds(128 * hlf, 128)]
                    return carry
                lax.fori_loop(0, dil, residue, 0)

            flush(0)
            to_token_order(dk_acc)
            flush(1)
            to_token_order(dv_acc)
            flush(2)
        else:
            dp_ref[1, 0] = dk_acc[...].astype(BF16)
            dp_ref[2, 0] = dv_acc[...].astype(BF16)

    scratch = [pltpu.VMEM((seq, SLAB), F32)] * 2
    if dil > 1:
        scratch += ([pltpu.VMEM((2, seq, 128), F32)] + [pltpu.VMEM((seq, SLAB), BF16)] * 4 + [pltpu.VMEM((seq, SLAB), F32)] * 2
                    + [pltpu.VMEM((2, seq, 128), F32)])
    dp, gb = _pcall(
        body, grid=(nbat,),
        out_shape=(jax.ShapeDtypeStruct(dp4.shape, BF16), jax.ShapeDtypeStruct((3, 8, SLAB), F32)),
        in_specs=[pl.BlockSpec((3, 1, seq, SLAB), lambda b: (0, g, b, 0)),
                  pl.BlockSpec((seq, SLAB), lambda b: (b, 0)),
                  pl.BlockSpec((2, seq, SLAB), lambda b: (0, b, 0)),
                  pl.BlockSpec((4 * BLK, 2 * BLK), lambda b: (0, 0)), ANY],
        out_specs=(pl.BlockSpec((3, 1, seq, SLAB), lambda b: (DP_SLABS // 9 - 1, g, b, 0)),
                   pl.BlockSpec((3, 8, SLAB), lambda b: (0, 0, 0))),
        scratch_shapes=scratch, input_output_aliases={4: 0}, name=f"attn_backward_{g}",
        compiler_params=_params(48, ("arbitrary",)))(qkv4, do_attn, ol_tot, _bias_tables(g), dp4)
    return dp.reshape(DP_SLABS, t, SLAB), gb


def _mid(rest, ols, x, tgt, ada, cw, b_out, ln_g, ln_b, w_pa_t, w_pb, w_out, tm=256):
    t = x.shape[0]
    nbat = ada.shape[0]
    nt = t // tm
    tps = nt // nbat

    def body(rest_ref, halo_ref, ol0_ref, ol1_ref, ol2_ref, x_ref, t_ref, ada_ref, cw_ref, bout_ref, lng_ref, lnb_ref,
             wpat_ref, wpb_ref, wout_ref,
             dp_ref, gx0_ref, doa_ref, olt_ref, mg_ref, dof_ref, bbs_ref, dyc_ref, a_ref, dya_ref,
             gbr_ref, sv_ref, dgate_ref, carry_ref, keep_ref):
        i = pl.program_id(0)
        ti = nt - 1 - i
        pos = ti % tps

        @pl.when(i == 0)
        def _():
            gbr_ref[...] = jnp.zeros_like(gbr_ref)
            sv_ref[...] = jnp.zeros_like(sv_ref)

        @pl.when(pos == tps - 1)
        def _():
            dgate_ref[...] = jnp.zeros_like(dgate_ref)
            carry_ref[...] = jnp.zeros_like(carry_ref)

        row = lax.broadcasted_iota(jnp.int32, (tm, SLAB), 0)
        halo_on = (pos > 0).astype(F32)

        def cols(s):
            return pl.ds(SLAB * s, SLAB)

        l0, l1, l2 = ol0_ref[1], ol1_ref[1], ol2_ref[1]
        mx = jnp.maximum(jnp.maximum(l0, l1), l2)
        e0, e1, e2 = jnp.exp(l0 - mx), jnp.exp(l1 - mx), jnp.exp(l2 - mx)
        den = e0 + e1 + e2
        o_attn = (e0 * ol0_ref[0] + e1 * ol1_ref[0] + e2 * ol2_ref[0]) * (1.0 / den)
        olt_ref[0] = o_attn
        olt_ref[1] = mx + jnp.log(den)
        z_a = rest_ref[R_ZA]
        sg_za = _sigmoid(z_a)
        a_ref[...] = (o_attn * z_a * sg_za).astype(BF16)
        y_attn = _nt(a_ref[...], wpat_ref[...])

        for s in range(4):
            u = rest_ref[R_GC + s] * rest_ref[R_UX + s]
            hu = halo_ref[R_GC + s] * halo_ref[R_UX + s] * halo_on
            u1 = jnp.where(row == 0, hu[7:8], pltpu.roll(u, 1, 0))
            u2 = jnp.where(row == 0, hu[6:7], jnp.where(row == 1, hu[7:8], pltpu.roll(u, 2, 0)))
            conv = cw_ref[0:1, cols(s)] * u2 + cw_ref[1:2, cols(s)] * u1 + cw_ref[2:3, cols(s)] * u
            zc = rest_ref[R_ZC + s]
            sg = _sigmoid(zc)
            keep_ref[2, :, cols(s)], keep_ref[3, :, cols(s)], keep_ref[4, :, cols(s)], keep_ref[5, :, cols(s)] = u1, u2, conv, sg
            bbs_ref[:, cols(s)] = (rest_ref[R_GB + s] * conv * (zc * sg)).astype(BF16)
        y_conv = _nn(bbs_ref[...], wpb_ref[...])

        for s in range(4):
            s_a, s_b = _sigmoid(rest_ref[R_GA + s]), _sigmoid(rest_ref[R_GBM + s])
            keep_ref[0, :, cols(s)], keep_ref[1, :, cols(s)] = s_a, s_b
            mg_ref[:, cols(s)] = (s_a * y_attn[:, SLAB * s:SLAB * (s + 1)] + s_b * y_conv[:, SLAB * s:SLAB * (s + 1)]).astype(BF16)
        o = _nn(mg_ref[...], wout_ref[...]) + bout_ref[...]
        gate = ada_ref[0, 2:3, :]
        r = ALPHA * x_ref[...] + gate * o
        mu = jnp.mean(r, axis=1, keepdims=True)
        rc = r - mu
        rstd = lax.rsqrt(jnp.mean(rc * rc, axis=1, keepdims=True) + LN_EPS)
        xhat = rc * rstd
        err = xhat * lng_ref[...] + lnb_ref[...] - t_ref[...]
        sv_ref[6] += _part8(err * err)
        dy = err * (1.0 / D)
        sv_ref[0] += _part8(dy * xhat)
        sv_ref[1] += _part8(dy)
        dxh = dy * lng_ref[...]
        dr = rstd * (dxh - jnp.mean(dxh, axis=1, keepdims=True) - xhat * jnp.mean(dxh * xhat, axis=1, keepdims=True))
        gx0_ref[...] = ALPHA * dr
        dgate_ref[0] += _part8(dr * o)
        do_ = dr * gate
        sv_ref[2] += _part8(do_)
        dof_ref[...] = do_.astype(BF16)
        dmerged = _nt(dof_ref[...], wout_ref[...])
        for s in range(4):
            s_a, s_b = keep_ref[0, :, cols(s)], keep_ref[1, :, cols(s)]
            dm = dmerged[:, SLAB * s:SLAB * (s + 1)]
            ya, yc = y_attn[:, SLAB * s:SLAB * (s + 1)], y_conv[:, SLAB * s:SLAB * (s + 1)]
            dya_ref[:, cols(s)] = (dm * s_a).astype(BF16)
            dyc_ref[:, cols(s)] = (dm * s_b).astype(BF16)
            dga = dm * ya * s_a * (1.0 - s_a)
            dgb = dm * yc * s_b * (1.0 - s_b)
            dp_ref[R_GA + s] = dga.astype(BF16)
            dp_ref[R_GBM + s] = dgb.astype(BF16)
            gbr_ref[R_GA + s] += _part8(dga)
            gbr_ref[R_GBM + s] += _part8(dgb)

        da = _nn(dya_ref[...], wpat_ref[...])
        doa_ref[...] = (da * z_a * sg_za).astype(BF16)
        dza = da * o_attn * (sg_za * (1.0 + z_a * (1.0 - sg_za)))
        dp_ref[R_ZA] = dza.astype(BF16)
        gbr_ref[R_ZA] += _part8(dza)

        dbb = _nt(dyc_ref[...], wpb_ref[...])
        for s in range(4):
            ux, gc, zc = rest_ref[R_UX + s], rest_ref[R_GC + s], rest_ref[R_ZC + s]
            u = gc * ux
            u1, u2, conv, sg = keep_ref[2, :, cols(s)], keep_ref[3, :, cols(s)], keep_ref[4, :, cols(s)], keep_ref[5, :, cols(s)]
            gb = rest_ref[R_GB + s]
            d_b = dbb[:, SLAB * s:SLAB * (s + 1)]
            szc = zc * sg
            dgb_ = d_b * conv * szc
            dconv = d_b * gb * szc
            dzc = d_b * gb * conv * (sg * (1.0 + zc * (1.0 - sg)))
            sv_ref[3, :, cols(s)] += _part8(dconv * u2)
            sv_ref[4, :, cols(s)] += _part8(dconv * u1)
            sv_ref[5, :, cols(s)] += _part8(dconv * u)
            nxt = carry_ref[:, cols(s)]
            d1 = jnp.where(row == tm - 1, nxt[0:1], pltpu.roll(dconv, tm - 1, 0))
            d2 = jnp.where(row == tm - 1, nxt[1:2], jnp.where(row == tm - 2, nxt[0:1], pltpu.roll(dconv, tm - 2, 0)))
            carry_ref[:, cols(s)] = dconv[0:8]
            du = cw_ref[2:3, cols(s)] * dconv + cw_ref[1:2, cols(s)] * d1 + cw_ref[0:1, cols(s)] * d2
            dgc, dux = du * ux, du * gc
            for slab, val in ((R_GB + s, dgb_), (R_ZC + s, dzc), (R_GC + s, dgc), (R_UX + s, dux)):
                dp_ref[slab] = val.astype(BF16)
                gbr_ref[slab] += _part8(val)

    def tile(i):
        return nt - 1 - i

    row_blk = lambda i: (tile(i), 0)
    slab_blk = lambda i: (0, tile(i), 0)
    const2 = lambda i: (0, 0)
    const3 = lambda i: (0, 0, 0)
    in_specs = [
        pl.BlockSpec((N_REST, tm, SLAB), slab_blk),
        pl.BlockSpec((N_REST, 8, SLAB), lambda i: (0, jnp.maximum(tile(i) * (tm // 8) - 1, 0), 0)),
        pl.BlockSpec((2, tm, SLAB), slab_blk), pl.BlockSpec((2, tm, SLAB), slab_blk), pl.BlockSpec((2, tm, SLAB), slab_blk),
        pl.BlockSpec((tm, D), row_blk), pl.BlockSpec((tm, D), row_blk),
        pl.BlockSpec((1, 3, D), lambda i: (tile(i) // tps, 0, 0)),
        pl.BlockSpec((3, D), const2), pl.BlockSpec((1, D), const2), pl.BlockSpec((1, D), const2), pl.BlockSpec((1, D), const2),
        pl.BlockSpec((D, SLAB), const2), pl.BlockSpec((D, D), const2), pl.BlockSpec((D, D), const2)]
    bf_rows = lambda: jax.ShapeDtypeStruct((t, D), BF16)
    out_shape = (
        jax.ShapeDtypeStruct((DP_SLABS, t, SLAB), BF16), jax.ShapeDtypeStruct((t, D), F32),
        jax.ShapeDtypeStruct((t, SLAB), BF16), jax.ShapeDtypeStruct((2, t, SLAB), F32),
        bf_rows(), bf_rows(), bf_rows(), bf_rows(), jax.ShapeDtypeStruct((t, SLAB), BF16), bf_rows(),
        jax.ShapeDtypeStruct((N_REST, 8, SLAB), F32), jax.ShapeDtypeStruct((7, 8, D), F32),
        jax.ShapeDtypeStruct((nbat, 8, D), F32))
    out_specs = (
        pl.BlockSpec((N_REST, tm, SLAB), slab_blk), pl.BlockSpec((tm, D), row_blk),
        pl.BlockSpec((tm, SLAB), row_blk), pl.BlockSpec((2, tm, SLAB), slab_blk),
        pl.BlockSpec((tm, D), row_blk), pl.BlockSpec((tm, D), row_blk), pl.BlockSpec((tm, D), row_blk),
        pl.BlockSpec((tm, D), row_blk), pl.BlockSpec((tm, SLAB), row_blk), pl.BlockSpec((tm, D), row_blk),
        pl.BlockSpec((N_REST, 8, SLAB), const3), pl.BlockSpec((7, 8, D), const3),
        pl.BlockSpec((1, 8, D), lambda i: (tile(i) // tps, 0, 0)))
    return _pcall(body, grid=(nt,), out_shape=out_shape, in_specs=in_specs, out_specs=out_specs,
                  scratch_shapes=[pltpu.VMEM((8, D), F32), pltpu.VMEM((6, tm, D), F32)], name="mid",
                  compiler_params=_params(56, ("arbitrary",)))(
        rest, rest, *ols, x, tgt, ada, cw, b_out, ln_g, ln_b, w_pa_t, w_pb, w_out)


def _tn_matmul(lhs, rhs, lhs_spec, n_steps, out_rows, out_index, name, after):
    t, n = rhs.shape

    def body(l_ref, r_ref, after_ref, o_ref):
        del after_ref
        o_ref[...] = _tn(l_ref[0] if len(l_ref.shape) == 3 else l_ref[...], r_ref[...])

    return _pcall(body, grid=(n_steps,), out_shape=jax.ShapeDtypeStruct((out_rows, n), F32),
                  in_specs=[lhs_spec, pl.BlockSpec((t, n), lambda j: (0, 0)), ANY],
                  out_specs=pl.BlockSpec((SLAB, n), out_index), name=name,
                  compiler_params=_params(48, ("parallel",)))(lhs, rhs, after)


def _grad_rows_2d(lhs, rhs, name, after):
    t, k = lhs.shape
    return _tn_matmul(lhs, rhs, pl.BlockSpec((t, SLAB), lambda j: (0, j)), k // SLAB, k, lambda j: (j, 0), name, after)


def _w_row_block(j):
    return (j + N_QKV) % N_SLAB


def _dp_slab(j):
    return jnp.where(j < N_REST, j, j + 2)


def _grad_w_in_t(dproj, h):
    t = h.shape[0]
    return _tn_matmul(dproj, h, pl.BlockSpec((1, t, SLAB), lambda j: (_dp_slab(j), 0, 0)), N_SLAB, D_IN,
                      lambda j: (_w_row_block(j), 0), "grad_w_in", h)


def _grad_h(dproj, w_in_t, gx0, x, ada, after, tm=512):
    t = x.shape[0]
    nbat = ada.shape[0]
    tps = (t // nbat) // tm

    def body(dp_ref, w_ref, gx0_ref, x_ref, ada_ref, after_ref, gx_ref, dss_ref):
        del after_ref
        i = pl.program_id(0)
        dh = None
        for j in range(N_SLAB):
            slab = j if j < N_REST else j + 2
            part = _nn(dp_ref[slab], w_ref[pl.ds(SLAB * ((j + N_QKV) % N_SLAB), SLAB), :])
            dh = part if dh is None else dh + part
        gx_ref[...] = gx0_ref[...] + dh * (1.0 + ada_ref[0, 1:2, :])

        @pl.when((i % tps) == 0)
        def _():
            dss_ref[...] = jnp.zeros_like(dss_ref)

        dss_ref[0, 0] += _part8(dh)
        dss_ref[0, 1] += _part8(dh * x_ref[...])

    return _pcall(
        body, grid=(t // tm,),
        out_shape=(jax.ShapeDtypeStruct((t, D), F32), jax.ShapeDtypeStruct((nbat, 2, 8, D), F32)),
        in_specs=[pl.BlockSpec((DP_SLABS, tm, SLAB), lambda i: (0, i, 0)),
                  pl.BlockSpec((D_IN, D), lambda i: (0, 0), pipeline_mode=pl.Buffered(1)),
                  pl.BlockSpec((tm, D), lambda i: (i, 0)), pl.BlockSpec((tm, D), lambda i: (i, 0)),
                  pl.BlockSpec((1, 3, D), lambda i: (i // tps, 0, 0)), ANY],
        out_specs=(pl.BlockSpec((tm, D), lambda i: (i, 0)),
                   pl.BlockSpec((1, 2, 8, D), lambda i: (i // tps, 0, 0, 0))),
        name="grad_h", compiler_params=_params(60, ("arbitrary",)))(dproj, w_in_t, gx0, x, ada, after)


def _chip(m):
    x, y, _ = _my_position()
    return (x ^ ((m >> 1) & 1), y ^ (m & 1))


def _exchange_siblings(grads, after, name):
    n = len(grads)

    def body(*refs):
        copies = _sibling_copies(refs[:n], refs[n + 1:2 * n + 1], refs[2 * n + 1], refs[2 * n + 2])
        for cp in copies:
            cp.start()
        for cp in copies:
            cp.wait()

    return _pcall(body, out_shape=tuple(_sibling_zones(grads)), in_specs=[ANY] * (n + 1), out_specs=(ANY,) * n,
                  name=name, scratch_shapes=[pltpu.SemaphoreType.DMA((4 * n,))] * 2)(*grads, after)


def _sibling_zones(grads):
    return [jax.ShapeDtypeStruct((4, g.shape[0] // N_DEV, g.shape[1]), g.dtype) for g in grads]


def _sibling_copies(srcs, lands, send_sems, recv_sems):
    x, y, c = _my_position()
    copies = []
    for a, (src, land) in enumerate(zip(srcs, lands)):
        rows = land.shape[1]
        for m in range(4):
            dev = _flat(*_chip(m), 1 - c)
            copies.append(pltpu.make_async_remote_copy(
                src_ref=src.at[pl.ds(pl.multiple_of(dev * rows, 8), rows), :], dst_ref=land.at[m],
                send_sem=send_sems.at[4 * a + m], recv_sem=recv_sems.at[4 * a + m], device_id=(x, y, 1 - c),
                device_id_type=MESH))
    return copies


def _chip_copies(srcs, lands, send_sems, recv_sems):
    _, _, c = _my_position()
    return [pltpu.make_async_remote_copy(
        src_ref=srcs[a].at[m - 1], dst_ref=lands[a].at[m - 1], send_sem=send_sems.at[3 * a + m - 1],
        recv_sem=recv_sems.at[3 * a + m - 1], device_id=(*_chip(m), c), device_id_type=MESH)
        for a in range(len(srcs)) for m in range(1, 4)]


def _slot_copies(srcs, lands, send_sems, recv_sems):
    me = _flat(*_my_position())
    return [pltpu.make_async_remote_copy(
        src_ref=srcs[a], dst_ref=lands[a].at[me], send_sem=send_sems.at[7 * a + mask - 1],
        recv_sem=recv_sems.at[7 * a + mask - 1], device_id=_peer(mask), device_id_type=MESH)
        for a in range(len(srcs)) for mask in range(1, N_DEV)]


HBM = pl.BlockSpec(memory_space=pltpu.HBM)
SEM = pl.BlockSpec(memory_space=pltpu.SEMAPHORE)
SPLIT_COPY = pltpu.CompilerParams(has_side_effects=pltpu.SideEffectType.DATAFLOW_SIDE_EFFECTING)


def _start_copies(make_copies, n_sems, srcs, zones, name, after):
    n = len(srcs)

    def body(*refs):
        for cp in make_copies(refs[:n], refs[n:2 * n], refs[2 * n + 1], refs[2 * n + 2]):
            cp.start()
        refs[-1][...] = jnp.zeros_like(refs[-1])

    hbm = tuple(pltpu.HBM(b.shape, b.dtype) for b in list(srcs) + list(zones))
    out_shape = (pltpu.SemaphoreType.DMA((n_sems,)), pltpu.SemaphoreType.DMA((n_sems,))) + hbm + (jax.ShapeDtypeStruct((8, 128), F32),)
    operands = [pltpu.with_memory_space_constraint(b, pltpu.HBM) for b in srcs]
    operands += [pltpu.with_memory_space_constraint(lax.empty(z.shape, z.dtype), pltpu.HBM) for z in zones]
    res = _pcall(body, out_shape=out_shape, in_specs=[HBM] * (2 * n) + [ANY], out_specs=(SEM, SEM) + (HBM,) * (2 * n) + (VMEM,),
                 input_output_aliases={i: 2 + i for i in range(2 * n)}, name=name, compiler_params=SPLIT_COPY)(*operands, after)
    return (res[0], res[1], res[2:2 + n], res[2 + n:2 + 2 * n]), res[-1]


def _wait_copies(make_copies, flight, after, name):
    send_sems, recv_sems, srcs, zones = flight
    n = len(srcs)

    def body(*refs):
        for cp in make_copies(refs[:n], refs[n:2 * n], refs[2 * n], refs[2 * n + 1]):
            cp.wait_send()
            cp.wait_recv()

    hbm = tuple(pltpu.HBM(b.shape, b.dtype) for b in list(srcs) + list(zones))
    res = _pcall(body, out_shape=hbm, in_specs=[HBM] * (2 * n) + [SEM, SEM, ANY], out_specs=(HBM,) * (2 * n),
                 input_output_aliases={i: i for i in range(2 * n)}, name=name, compiler_params=SPLIT_COPY)(
        *srcs, *zones, send_sems, recv_sems, after)
    return res[:n], res[n:]


def _pair_sums(devs, grads, lands, n_steps, name):
    n = len(grads)
    rows = [l.shape[1] for l in lands]
    rbs = [r // n_steps for r in rows]

    def body(devs_ref, *refs):
        del devs_ref
        g_refs, land_refs, outs = refs[:4 * n], refs[4 * n:5 * n], refs[5 * n:]
        for a in range(n):
            outs[2 * a][...] = g_refs[4 * a][...] + land_refs[a][0]
            for m in range(1, 4):
                outs[2 * a + 1][m - 1] = (g_refs[4 * a + m][...] + land_refs[a][m]).astype(BF16)

    def block_of(m, per_dev):
        return lambda i, devs_ref: (devs_ref[m] * per_dev + i, 0)

    in_specs = [pl.BlockSpec((rb, l.shape[2]), block_of(m, n_steps)) for rb, l in zip(rbs, lands) for m in range(4)]
    in_specs += [pl.BlockSpec((4, rb, l.shape[2]), lambda i, devs_ref: (0, i, 0)) for rb, l in zip(rbs, lands)]
    out_shape, out_specs = [], []
    for rb, l in zip(rbs, lands):
        out_shape += [jax.ShapeDtypeStruct(l.shape[1:], F32), jax.ShapeDtypeStruct((3,) + l.shape[1:], BF16)]
        out_specs += [pl.BlockSpec((rb, l.shape[2]), lambda i, devs_ref: (i, 0)),
                      pl.BlockSpec((3, rb, l.shape[2]), lambda i, devs_ref: (0, i, 0))]
    grid_spec = pltpu.PrefetchScalarGridSpec(num_scalar_prefetch=1, grid=(n_steps,), in_specs=in_specs, out_specs=tuple(out_specs))
    res = _pcall(body, grid_spec=grid_spec, out_shape=tuple(out_shape), name=name,
                 compiler_params=_params(48, ("parallel",)))(devs, *[g for g in grads for _ in range(4)], *lands)
    return res[0::2], res[1::2]


def _final_sums(mine, lands, n_steps, name):
    n = len(mine)
    rbs = [o.shape[0] // n_steps for o in mine]

    def body(*refs):
        mine_refs, land_refs, outs = refs[:n], refs[n:2 * n], refs[2 * n:]
        for a in range(n):
            tot = mine_refs[a][...]
            for m in range(3):
                tot = tot + land_refs[a][m].astype(F32)
            outs[a][...] = tot

    in_specs = ([pl.BlockSpec((rb, o.shape[1]), lambda i: (i, 0)) for rb, o in zip(rbs, mine)]
                + [pl.BlockSpec((3, rb, o.shape[1]), lambda i: (0, i, 0)) for rb, o in zip(rbs, mine)])
    out_specs = tuple(pl.BlockSpec((rb, o.shape[1]), lambda i: (i, 0)) for rb, o in zip(rbs, mine))
    out_shape = tuple(jax.ShapeDtypeStruct(o.shape, F32) for o in mine)
    return _pcall(body, grid=(n_steps,), out_shape=out_shape, in_specs=in_specs, out_specs=out_specs, name=name,
                  compiler_params=_params(32, ("parallel",)))(*mine, *lands)


def _reduce_scatter_begin(big, small_after_start):
    c = lax.axis_index("c")
    devs = jnp.stack([_flat(*_chip(m), c) for m in range(4)]).astype(jnp.int32)
    flight, token = _start_copies(_sibling_copies, 4, [big], _sibling_zones([big]), "siblings_start", big)
    small = small_after_start(token)
    (big,), big_lands = _wait_copies(_sibling_copies, flight, small[-1], "siblings_wait")
    big_mine, big_send = _pair_sums(devs, [big], big_lands, 4, "pair_sums_w_in")
    big_flight, token = _start_copies(_chip_copies, 3, list(big_send), list(big_send), "chips_start_w_in", big_send[0])
    small_lands = _exchange_siblings(small, token, "exchange_siblings_rest")
    small_mine, small_send = _pair_sums(devs, small, small_lands, 1, "pair_sums_rest")
    small_flight, token = _start_copies(_chip_copies, 3 * len(small), list(small_send), list(small_send), "chips_start_rest",
                                        small_send[0])
    return (big_flight, small_flight, list(big_mine) + list(small_mine)), token


def _reduce_scatter_end(state, after):
    big_flight, small_flight, mine = state
    _, big_got = _wait_copies(_chip_copies, big_flight, after, "chips_wait_w_in")
    _, small_got = _wait_copies(_chip_copies, small_flight, after, "chips_wait_rest")
    small = _final_sums(mine[1:], small_got, 1, "final_sums_rest")
    return (mine[0], big_got[0]), list(small)


def _adamw(w, g, m, v):
    m_new = B1 * m + (1.0 - B1) * g
    v_new = B2 * v + (1.0 - B2) * (g * g)
    m_hat = m_new / (1.0 - B1 ** STEP)
    v_hat = v_new / (1.0 - B2 ** STEP)
    delta = -LR * (m_hat / (jnp.sqrt(v_hat) + EPS) + WD * w)
    return delta, m_new, v_new


def _final_sum_adam_rows(mine, land, w, m, v, n_steps, name):
    rows, ncol = w.shape
    blk = pl.BlockSpec((rows // n_steps, ncol), lambda i: (i, 0))

    def body(mine_ref, land_ref, w_ref, m_ref, v_ref, g_ref, d_ref, mo_ref, vo_ref):
        g = mine_ref[...]
        for k in range(3):
            g = g + land_ref[k].astype(F32)
        g_ref[...] = g
        d_ref[...], mo_ref[...], vo_ref[...] = _adamw(w_ref[...], g, m_ref[...], v_ref[...])

    shape = jax.ShapeDtypeStruct(w.shape, F32)
    return _pcall(body, grid=(n_steps,), out_shape=(shape,) * 4,
                  in_specs=[blk, pl.BlockSpec((3, rows // n_steps, ncol), lambda i: (0, i, 0)), blk, blk, blk],
                  out_specs=(blk,) * 4, name=name, compiler_params=_params(32, ("parallel",)))(mine, land, w, m, v)


def _adam_transposed(g_t, w, m, v, name):
    n, k = g_t.shape
    rb = min(k, 128)

    def body(gt_ref, w_ref, m_ref, v_ref, g_ref, d_ref, mo_ref, vo_ref):
        for src, skip, dst, size in _column_chunks(n):
            sl = pl.ds(dst, size)
            g = gt_ref[pl.ds(src, 128), :].T[:, skip:]
            delta, m_new, v_new = _adamw(w_ref[:, sl], g, m_ref[:, sl], v_ref[:, sl])
            g_ref[:, sl], d_ref[:, sl], mo_ref[:, sl], vo_ref[:, sl] = g, delta, m_new, v_new

    shape = jax.ShapeDtypeStruct(w.shape, F32)
    rows = pl.BlockSpec((rb, n), lambda i: (i, 0))
    return _pcall(body, grid=(k // rb,), out_shape=(shape,) * 4,
                  in_specs=[pl.BlockSpec((n, rb), lambda i: (0, i)), rows, rows, rows], out_specs=(rows,) * 4, name=name,
                  compiler_params=_params(32, ("parallel",)))(g_t, w, m, v)


def _adam_many(items, name):
    n = len(items)

    def body(*refs):
        ins, outs = refs[:4 * n], refs[4 * n:]
        for a in range(n):
            w_ref, g_ref, m_ref, v_ref = ins[4 * a:4 * a + 4]
            delta, m_new, v_new = _adamw(w_ref[...], g_ref[...], m_ref[...], v_ref[...])
            outs[3 * a][...], outs[3 * a + 1][...], outs[3 * a + 2][...] = delta, m_new, v_new

    out_shape = tuple(jax.ShapeDtypeStruct(it[0].shape, F32) for it in items for _ in range(3))
    flat = [arr for it in items for arr in it]
    res = _pcall(body, grid=(1,), out_shape=out_shape, in_specs=[_whole(a) for a in flat],
                 out_specs=tuple(_whole(o) for o in out_shape), name=name, compiler_params=_params(32))(*flat)
    return [tuple(res[3 * a:3 * a + 3]) for a in range(n)]


def _adam_w_ada(cact_all, dada_mine, w, m, v):
    def body(c_ref, d_ref, w_ref, m_ref, v_ref, g_ref, dl_ref, mo_ref, vo_ref):
        g = _tn(c_ref[...].astype(BF16), d_ref[...].astype(BF16))
        delta, m_new, v_new = _adamw(w_ref[...], g, m_ref[...], v_ref[...])
        g_ref[...], dl_ref[...], mo_ref[...], vo_ref[...] = g, delta, m_new, v_new

    shape = jax.ShapeDtypeStruct(w.shape, F32)
    operands = (cact_all, dada_mine, w, m, v)
    return _pcall(body, grid=(1,), out_shape=(shape,) * 4, in_specs=[_whole(a) for a in operands],
                  out_specs=(_whole(w),) * 4, name="adam_w_ada", compiler_params=_params(32))(*operands)


def kernel(x, c, w_ada, b_ada, w_in, b_in, conv_w, w_proj_attn, w_proj_conv, w_out, b_out, ln_g, ln_b, loss_target, m_w_ada, m_b_ada, m_w_in, m_b_in, m_conv_w, m_w_proj_attn, m_w_proj_conv, m_w_out, m_b_out, m_ln_g, m_ln_b, v_w_ada, v_b_ada, v_w_in, v_b_in, v_conv_w, v_w_proj_attn, v_w_proj_conv, v_w_out, v_b_out, v_ln_g, v_ln_b):
    nbat, seq, _ = x.shape
    t = nbat * seq
    me = _flat(*_my_position())
    x2, tgt2 = x.reshape(t, D), loss_target.reshape(t, D)
    sq = lambda a: a.reshape(a.shape[1:])

    tr = lambda a: a[0].T
    w_in_rows = tr(w_in)
    w_in_t_s = _cast_rows(w_in_rows, 4, "cast_w_in")
    w_pa_t_s, w_pb_s, w_out_s, cact_s, cw_s = _prep(sq(w_proj_attn), sq(w_proj_conv), sq(w_out), c, sq(conv_w))

    ncol = w_ada.shape[2]
    b_ada_mine = lax.dynamic_slice(b_ada, (0, me * ncol), (1, ncol))
    ada_slots, cact_slots, cw_slots = _ada_forward(cact_s, cw_s, sq(w_ada), b_ada_mine)
    cact_all = cact_slots[:, :nbat].reshape(N_DEV * nbat, D)
    cw = cw_slots[:, :3].transpose(1, 0, 2).reshape(3, D)
    ada_all = ada_slots[:, :, :nbat].transpose(1, 2, 0, 3).reshape(N_DEV * nbat, 3, D)
    ada = lax.dynamic_slice(ada_all, (me * nbat, 0, 0), (nbat, 3, D))

    h = _make_h(x2, ada)
    w_in_t, qkv, rest, _ = _project_gather(w_in_t_s, h, b_in.reshape(N_SLAB, 1, SLAB), [])

    small_shards = [w_pa_t_s, w_pb_s, w_out_s]
    zones = [jax.ShapeDtypeStruct((N_DEV,) + s.shape, s.dtype) for s in small_shards]
    flight, token = _start_copies(_slot_copies, 7 * len(small_shards), small_shards, zones, "weights_start", qkv)
    ols = []
    for g in range(3):
        ols.append(_attn_forward(qkv, g, nbat, token if g == 0 else ols[-1]))
    _, slots = _wait_copies(_slot_copies, flight, ols[-1], "weights_wait")
    w_pa_t, w_pb, w_o = [lax.dynamic_update_slice(z.reshape(N_DEV * s.shape[0], s.shape[1]), s, (me * s.shape[0], 0))
                         for z, s in zip(slots, small_shards)]
    (dproj, gx0, do_attn, ol_tot, merged, do_f, bbs, dyc, a_bf, dya, gb_rest, svec, dgate) = _mid(
        rest, ols, x2, tgt2, ada, cw, b_out, ln_g, ln_b, w_pa_t, w_pb, w_o)

    gb_qkv = []
    for g in range(3):
        dproj, gb = _attn_backward(qkv, do_attn, ol_tot, dproj, g, nbat)
        gb_qkv.append(gb)
    g_w_in_t = _grad_w_in_t(dproj, h)

    def small_grads(token):
        g_w_out = _grad_rows_2d(merged, do_f, "grad_w_out", token)
        g_w_pb = _grad_rows_2d(bbs, dyc, "grad_w_proj_conv", g_w_out)
        g_w_pa_t = _grad_rows_2d(dya, a_bf, "grad_w_proj_attn", g_w_pb)
        return [g_w_out, g_w_pb, g_w_pa_t]

    rs_state, token = _reduce_scatter_begin(g_w_in_t, small_grads)
    grad_x, dss = _grad_h(dproj, w_in_t, gx0, x2, ada, token)

    rows8, tot, g_bada = _small_reduce(gb_rest, gb_qkv, svec, dgate, dss)
    (g_in_mine, g_in_got), (g_out, g_pb, g_pa_t) = _reduce_scatter_end(rs_state, tot)
    loss = tot[0, P_LOSS]
    dada_all = rows8[:, 0, P_DADA:].reshape(N_DEV * nbat, 3 * D)
    dada_mine = lax.dynamic_slice(dada_all, (0, me * ncol), (N_DEV * nbat, ncol))

    g_in_t, d_win_t, nm_win_t, nv_win_t = _final_sum_adam_rows(g_in_mine, g_in_got, w_in_rows, tr(m_w_in), tr(v_w_in), 4, "adam_w_in")
    g_win, d_win, nm_win, nv_win = g_in_t.T, d_win_t.T, nm_win_t.T, nv_win_t.T
    g_wpa, d_wpa, nm_wpa, nv_wpa = _adam_transposed(g_pa_t, sq(w_proj_attn), sq(m_w_proj_attn), sq(v_w_proj_attn), "adam_w_proj_attn")
    g_wada, d_wada, nm_wada, nv_wada = _adam_w_ada(cact_all, dada_mine, sq(w_ada), sq(m_w_ada), sq(v_w_ada))
    g_bin = tot[:, P_BIN:P_BIN + D_IN]
    g_bout = tot[:, P_BOUT:P_BOUT + D]
    g_lng = tot[:, P_LNG:P_LNG + D]
    g_lnb = tot[:, P_LNB:P_LNB + D]
    g_conv = lax.dynamic_slice(tot[:, P_CONV:P_CONV + 3 * D].reshape(3, D), (0, me * cw_s.shape[1]), (3, cw_s.shape[1]))
    upd = _adam_many([
        (sq(w_proj_conv), g_pb, sq(m_w_proj_conv), sq(v_w_proj_conv)),
        (sq(w_out), g_out, sq(m_w_out), sq(v_w_out)),
        (b_ada, g_bada, m_b_ada, v_b_ada), (b_in, g_bin, m_b_in, v_b_in), (sq(conv_w), g_conv, sq(m_conv_w), sq(v_conv_w)),
        (b_out, g_bout, m_b_out, v_b_out), (ln_g, g_lng, m_ln_g, v_ln_g), (ln_b, g_lnb, m_ln_b, v_ln_b)], "adam_rest")
    (d_wpb, nm_wpb, nv_wpb), (d_wout, nm_wout, nv_wout), (d_bada, nm_bada, nv_bada), (d_bin, nm_bin, nv_bin), \
        (d_conv, nm_conv, nv_conv), (d_bout, nm_bout, nv_bout), (d_lng, nm_lng, nv_lng), (d_lnb, nm_lnb, nv_lnb) = upd

    ex = lambda a: a.reshape((1,) + a.shape)
    grads = [ex(g_wada), g_bada, ex(g_win), g_bin, ex(g_conv), ex(g_wpa), ex(g_pb), ex(g_out), g_bout, g_lng, g_lnb]
    deltas = [ex(d_wada), d_bada, ex(d_win), d_bin, ex(d_conv), ex(d_wpa), ex(d_wpb), ex(d_wout), d_bout, d_lng, d_lnb]
    new_m = [ex(nm_wada), nm_bada, ex(nm_win), nm_bin, ex(nm_conv), ex(nm_wpa), ex(nm_wpb), ex(nm_wout), nm_bout, nm_lng, nm_lnb]
    new_v = [ex(nv_wada), nv_bada, ex(nv_win), nv_bin, ex(nv_conv), ex(nv_wpa), ex(nv_wpb), ex(nv_wout), nv_bout, nv_lng, nv_lnb]
    return (loss, grad_x.reshape(x.shape), *grads, *deltas, *new_m, *new_v)
```

```python
import functools

import jax
import jax.numpy as jnp
from jax import lax
from jax.experimental import pallas as pl
from jax.experimental.pallas import tpu as pltpu

F32, BF16 = jnp.float32, jnp.bfloat16
MESH = pl.DeviceIdType.MESH
N_DEV = 8
D = 1024
SLAB = 256
N_QKV, N_REST = 9, 25
N_SLAB = N_QKV + N_REST
D_IN = N_SLAB * SLAB
DP_SLABS = 36
BLK = 128
GROUPS = ((128, 1), (512, 4), (2048, 16))
ALPHA = 2.0 ** 0.25
LN_EPS = 1e-5
LR, B1, B2, EPS, WD, STEP = 0.001, 0.9, 0.999, 1e-08, 0.01, 10
R_ZA, R_UX, R_GB, R_GC, R_ZC, R_GA, R_GBM = 0, 1, 5, 9, 13, 17, 21
P_BIN, P_BOUT, P_LNG, P_LNB, P_CONV, P_LOSS, P_DADA = 0, 8704, 9728, 10752, 11776, 14848, 14976
MIB = 1024 * 1024


def _pcall(body, *, out_shape, out_specs=None, **kw):
    def pin_out(shape, spec):
        blocked = isinstance(shape, jax.ShapeDtypeStruct) and getattr(spec, "block_shape", None) is not None
        return pltpu.HBM(shape.shape, shape.dtype) if blocked else shape

    n_scalar = 0
    if out_specs is None:
        specs = kw["grid_spec"].out_specs
        n_scalar = kw["grid_spec"].num_scalar_prefetch
    else:
        kw["out_specs"] = specs = out_specs
    if isinstance(out_shape, (tuple, list)):
        out_shape = tuple(pin_out(s, p) for s, p in zip(out_shape, specs))
    else:
        out_shape = pin_out(out_shape, specs)
    call = pl.pallas_call(body, out_shape=out_shape, **kw)

    def run(*operands):
        def pin(o):
            is_data = jnp.issubdtype(o.dtype, jnp.floating) or jnp.issubdtype(o.dtype, jnp.integer)
            return pltpu.with_memory_space_constraint(o, pltpu.HBM) if is_data else o
        return call(*operands[:n_scalar], *[pin(o) for o in operands[n_scalar:]])

    return run

ANY = pl.BlockSpec(memory_space=pl.ANY)
VMEM = pl.BlockSpec(memory_space=pltpu.VMEM)


def _whole(a):
    return pl.BlockSpec(a.shape, lambda i: (0,) * len(a.shape))


def _params(vmem_mib=None, sem=None):
    kw = {}
    if vmem_mib is not None:
        kw["vmem_limit_bytes"] = vmem_mib * MIB
    if sem is not None:
        kw["dimension_semantics"] = sem
    return pltpu.CompilerParams(**kw)


def _nn(a, b):
    return jnp.dot(a, b, preferred_element_type=F32)


def _nt(a, b):
    return lax.dot_general(a, b, (((1,), (1,)), ((), ())), preferred_element_type=F32)


def _tn(a, b):
    return lax.dot_general(a, b, (((0,), (0,)), ((), ())), preferred_element_type=F32)


def _sigmoid(v):
    return 1.0 / (1.0 + jnp.exp(-v))


def _part8(v):
    return v.reshape(v.shape[0] // 8, 8, v.shape[1]).sum(axis=0)


def _my_position():
    return lax.axis_index("x"), lax.axis_index("y"), lax.axis_index("c")


def _flat(px, py, pc):
    return 4 * px + 2 * py + pc


def _peer(mask):
    x, y, c = _my_position()
    return (x ^ ((mask >> 2) & 1), y ^ ((mask >> 1) & 1), c ^ (mask & 1))


def _column_chunks(n):
    chunks = [(128 * a, 0, 128 * a, 128) for a in range(n // 128)]
    if n % 128:
        chunks.append((n - 128, 128 - n % 128, 128 * (n // 128), n % 128))
    return chunks


def _cast_rows(w, n_steps, name):
    rows, ncol = w.shape
    blk = pl.BlockSpec((rows // n_steps, ncol), lambda i: (i, 0))

    def body(w_ref, o_ref):
        o_ref[...] = w_ref[...].astype(BF16)

    return _pcall(body, grid=(n_steps,), out_shape=jax.ShapeDtypeStruct(w.shape, BF16), in_specs=[blk], out_specs=blk,
                  name=name, compiler_params=_params(16, ("parallel",)))(w)


def _prep(w_pa, w_pb, w_out, c, conv_w):
    def body(wpa_ref, wpb_ref, wout_ref, c_ref, cw_ref, wpat_ref, wpb_o, wout_o, cact_ref, cwp_ref):
        wpat_ref[...] = wpa_ref[...].T.astype(BF16)
        wpb_o[...] = wpb_ref[...].astype(BF16)
        wout_o[...] = wout_ref[...].astype(BF16)
        cv = c_ref[...]
        cact_ref[...] = jnp.zeros_like(cact_ref)
        cact_ref[pl.ds(0, cv.shape[0]), :] = cv * _sigmoid(cv)
        cwp_ref[...] = jnp.zeros_like(cwp_ref)
        cwp_ref[pl.ds(0, 3), :] = cw_ref[...]

    out_shape = (jax.ShapeDtypeStruct((w_pa.shape[1], w_pa.shape[0]), BF16),
                 jax.ShapeDtypeStruct(w_pb.shape, BF16), jax.ShapeDtypeStruct(w_out.shape, BF16),
                 jax.ShapeDtypeStruct((8, D), F32), jax.ShapeDtypeStruct((8, conv_w.shape[1]), F32))
    operands = (w_pa, w_pb, w_out, c, conv_w)
    return _pcall(body, grid=(1,), out_shape=out_shape, in_specs=[_whole(a) for a in operands],
                  out_specs=tuple(_whole(o) for o in out_shape), name="prep", compiler_params=_params(16))(*operands)


def _exchange_slots(out_refs, send_sems, recv_sems, base=0):
    me = _flat(*_my_position())

    def copy(a, mask, slot):
        return pltpu.make_async_remote_copy(
            src_ref=out_refs[a].at[slot], dst_ref=out_refs[a].at[slot], send_sem=send_sems.at[base + 7 * a + mask - 1],
            recv_sem=recv_sems.at[base + 7 * a + mask - 1], device_id=_peer(mask), device_id_type=MESH)

    pairs = [(a, mask) for a in range(len(out_refs)) for mask in range(1, N_DEV)]
    for a, mask in pairs:
        copy(a, mask, me).start()
    for a, mask in pairs:
        copy(a, mask, _flat(*_peer(mask))).wait_recv()
    for a, mask in pairs:
        copy(a, mask, me).wait_send()


def _ada_forward(cact_mine, cw_mine, w_ada, b_ada_mine):
    ncol = w_ada.shape[1]

    def body(c_ref, cw_ref, w_ref, b_ref, out_ref, call_ref, cwall_ref, send_sems, recv_sems):
        me = _flat(*_my_position())
        call_ref[me] = c_ref[...]
        cwall_ref[me] = cw_ref[...]
        _exchange_slots([call_ref, cwall_ref], send_sems, recv_sems)
        c_all = call_ref[...].reshape(N_DEV * 8, D).astype(BF16)
        out_ref[me] = (_nn(c_all, w_ref[...].astype(BF16)) + b_ref[...]).reshape(N_DEV, 8, ncol)
        _exchange_slots([out_ref], send_sems, recv_sems, base=14)

    operands = (cact_mine, cw_mine, w_ada, b_ada_mine)
    out_shape = (jax.ShapeDtypeStruct((N_DEV, N_DEV, 8, ncol), F32), jax.ShapeDtypeStruct((N_DEV, 8, D), F32),
                 jax.ShapeDtypeStruct((N_DEV,) + cw_mine.shape, F32))
    return _pcall(body, grid=(1,), out_shape=out_shape, in_specs=[_whole(a) for a in operands], out_specs=(VMEM,) * 3,
                  scratch_shapes=[pltpu.SemaphoreType.DMA((21,)), pltpu.SemaphoreType.DMA((21,))], name="ada_forward",
                  compiler_params=_params(16))(*operands)


def _small_reduce(gb_rest, gb_qkv, svec, dgate, dss):
    nbat = dgate.shape[0]

    def body(gbr_ref, q0_ref, q1_ref, q2_ref, sv_ref, dg_ref, dss_ref, rows_ref, tot_ref, gbada_ref, send_sems, recv_sems):
        me = _flat(*_my_position())

        def put(off, v):
            rows_ref[me, :, pl.ds(off, v.shape[1])] = v

        def row(v):
            return jnp.sum(v, axis=0, keepdims=True)

        for g, q_ref in enumerate((q0_ref, q1_ref, q2_ref)):
            for which in range(3):
                put(P_BIN + SLAB * (3 * which + g), row(q_ref[which]))
        for s in range(N_REST):
            put(P_BIN + SLAB * (N_QKV + s), row(gbr_ref[s]))
        put(P_LNG, row(sv_ref[0]))
        put(P_LNB, row(sv_ref[1]))
        put(P_BOUT, row(sv_ref[2]))
        for j in range(3):
            put(P_CONV + D * j, row(sv_ref[3 + j]))
        loss = (0.5 / D) * jnp.sum(row(sv_ref[6]), axis=1, keepdims=True)
        put(P_LOSS, jnp.broadcast_to(loss, (1, 128)))
        for b in range(nbat):
            put(P_DADA + 3 * D * b, row(dss_ref[b, 0]))
            put(P_DADA + 3 * D * b + D, row(dss_ref[b, 1]))
            put(P_DADA + 3 * D * b + 2 * D, row(dg_ref[b]))
        _exchange_slots([rows_ref], send_sems, recv_sems)
        tot = rows_ref[0]
        for k in range(1, N_DEV):
            tot = tot + rows_ref[k]
        tot_ref[...] = tot
        gbada = tot[:, P_DADA:P_DADA + 3 * D]
        for b in range(1, nbat):
            gbada = gbada + tot[:, P_DADA + 3 * D * b:P_DADA + 3 * D * (b + 1)]
        gbada_ref[...] = gbada

    p_len = P_DADA + nbat * 3 * D
    out_shape = (jax.ShapeDtypeStruct((N_DEV, 1, p_len), F32), jax.ShapeDtypeStruct((1, p_len), F32),
                 jax.ShapeDtypeStruct((1, 3 * D), F32))
    operands = (gb_rest, *gb_qkv, svec, dgate, dss)
    return _pcall(body, grid=(1,), out_shape=out_shape, in_specs=[_whole(a) for a in operands],
                  out_specs=(VMEM, _whole(out_shape[1]), _whole(out_shape[2])),
                  scratch_shapes=[pltpu.SemaphoreType.DMA((7,)), pltpu.SemaphoreType.DMA((7,))], name="small_reduce",
                  compiler_params=_params(16))(*operands)


def _make_h(x, ada, tm=512):
    t = x.shape[0]
    tps = (t // ada.shape[0]) // tm

    def body(x_ref, ada_ref, h_ref):
        h_ref[...] = (x_ref[...] * (1.0 + ada_ref[0, 1:2, :]) + ada_ref[0, 0:1, :]).astype(BF16)

    return _pcall(body, grid=(t // tm,), out_shape=jax.ShapeDtypeStruct((t, D), BF16),
                  in_specs=[pl.BlockSpec((tm, D), lambda i: (i, 0)), pl.BlockSpec((1, 3, D), lambda i: (i // tps, 0, 0))],
                  out_specs=pl.BlockSpec((tm, D), lambda i: (i, 0)), name="make_h",
                  compiler_params=_params(32, ("parallel",)))(x, ada)


PIECE = 64
N_CHUNK = 4
ARRIVAL_RANK = (0, 1, 3, 5, 2, 4, 6, 7)
SLOT_MASK = (1, 4, 2, 6, 5, 3, 7)


def _arrival_tables(shard_rows):
    import numpy as np
    crow = shard_rows // N_CHUNK
    table = np.zeros((N_DEV, N_SLAB + 7 * N_CHUNK), np.int32)
    lo = [(SLAB * j) // crow for j in range(N_SLAB)]
    hi = [(SLAB * j + SLAB - 1) // crow for j in range(N_SLAB)]
    for k in range(N_DEV):
        def rank(chunk):
            shard_rank = ARRIVAL_RANK[(chunk // N_CHUNK) ^ k]
            return shard_rank if shard_rank < 2 else 2 + 8 * (chunk % N_CHUNK) + shard_rank
        order = sorted(range(N_SLAB), key=lambda j: (max(rank(lo[j]), rank(hi[j])), j))
        table[k, :N_SLAB] = order
        for slot, mask in enumerate(SLOT_MASK):
            for ch in range(N_CHUNK):
                chunk = (k ^ mask) * N_CHUNK + ch
                table[k, N_SLAB + slot * N_CHUNK + ch] = min(t for t, j in enumerate(order) if lo[j] <= chunk <= hi[j])
    return table


def _project_gather(shard, h, b_in3, others):
    t = h.shape[0]
    n_o = len(others)
    srows = shard.shape[0]
    crow = srows // N_CHUNK
    shards = [shard] + list(others)
    table = jnp.asarray(_arrival_tables(srows))

    def body(tbl_ref, *refs):
        srcs = [refs[0]] + list(refs[3:3 + n_o])
        h_ref, b_ref = refs[1], refs[2]
        outs = [refs[3 + n_o]] + list(refs[6 + n_o:6 + 2 * n_o])
        qkv_ref, rest_ref = refs[4 + n_o], refs[5 + n_o]
        (wtile, obf, of32, send_sems, recv_sems, local_sems, tile_sems, obf_sems, of32_sems) = refs[6 + 2 * n_o:]
        w_full = outs[0]
        x, y, c = _my_position()
        k = _flat(x, y, c)
        me, sibling = (x, y, c), (x, y, 1 - c)
        chips = [(1 - x, y), (x, 1 - y), (1 - x, 1 - y)]

        def rows(a, px, py, pc, ch):
            r = shards[a].shape[0]
            if ch is None:
                return outs[a].at[pl.ds(pl.multiple_of(_flat(px, py, pc) * r, r), r), :]
            return outs[a].at[pl.ds(pl.multiple_of(_flat(px, py, pc) * r + ch * crow, crow), crow), :]

        def copy(a, slot, block, to, ch=None, src=None):
            sem = slot * N_CHUNK + ch if a == 0 else 7 * (N_CHUNK - 1 + a) + slot
            if src is not None and ch is not None:
                src = src.at[pl.ds(ch * crow, crow), :]
            return pltpu.make_async_remote_copy(
                src_ref=rows(a, *block, ch) if src is None else src, dst_ref=rows(a, *block, ch),
                send_sem=send_sems.at[sem], recv_sem=recv_sems.at[sem], device_id=to, device_id_type=MESH)

        mine = [pltpu.make_async_copy(srcs[a], rows(a, *me, None), local_sems.at[a]) for a in range(1 + n_o)]
        first = []
        for ch in range(N_CHUNK):
            first.append(copy(0, 0, me, sibling, ch, src=srcs[0]))
            first += [copy(0, 1 + j, me, (*chip, c), ch, src=srcs[0]) for j, chip in enumerate(chips)]
        for a in range(1, 1 + n_o):
            first.append(copy(a, 0, me, sibling, src=srcs[a]))
            first += [copy(a, 1 + j, me, (*chip, c), src=srcs[a]) for j, chip in enumerate(chips)]
        for cp in mine + first:
            cp.start()

        def arrive(a, slot, ch=None):
            if slot == 0:
                copy(a, 0, sibling, me, ch).wait_recv()
            elif slot < 4:
                copy(a, slot, (*chips[slot - 1], c), me, ch).wait_recv()
                copy(a, slot + 3, (*chips[slot - 1], c), sibling, ch).start()
            else:
                copy(a, slot, (*chips[slot - 4], 1 - c), me, ch).wait_recv()

        def arrive_for(step):
            for slot in range(7):
                for ch in range(N_CHUNK):
                    @pl.when(tbl_ref[k, N_SLAB + slot * N_CHUNK + ch] == step)
                    def _():
                        arrive(0, slot, ch)

        def fetch(step, buf):
            slab = tbl_ref[k, step]
            for p in range(SLAB // PIECE):
                g0 = slab * SLAB + PIECE * p
                own = (g0 >= k * srows) & (g0 < (k + 1) * srows)
                dst = wtile.at[buf, pl.ds(PIECE * p, PIECE), :]

                @pl.when(own)
                def _():
                    pltpu.make_async_copy(srcs[0].at[pl.ds(pl.multiple_of(g0 - k * srows, PIECE), PIECE), :], dst, tile_sems.at[buf]).start()

                @pl.when(jnp.logical_not(own))
                def _():
                    pltpu.make_async_copy(w_full.at[pl.ds(pl.multiple_of(g0, PIECE), PIECE), :], dst, tile_sems.at[buf]).start()

        def wait_tile(buf):
            pltpu.make_async_copy(w_full.at[pl.ds(0, SLAB), :], wtile.at[buf], tile_sems.at[buf]).wait()

        def put(buf_ref, sems, dst_ref, count, value):
            b = count % 2

            @pl.when(count >= 2)
            def _():
                pltpu.make_async_copy(buf_ref.at[b], dst_ref, sems.at[b]).wait()

            buf_ref[b] = value
            pltpu.make_async_copy(buf_ref.at[b], dst_ref, sems.at[b]).start()

        def drain(buf_ref, sems, dst_ref, count):
            for back in (1, 2):
                @pl.when(count >= back)
                def _():
                    pltpu.make_async_copy(buf_ref.at[(count - back) % 2], dst_ref, sems.at[(count - back) % 2]).wait()

        arrive_for(0)
        fetch(0, 0)

        def step(s, carry):
            n_bf, n_f32 = carry
            buf = s % 2

            @pl.when(s + 1 < N_SLAB)
            def _():
                arrive_for(s + 1)
                fetch(s + 1, 1 - buf)

            wait_tile(buf)
            slab = tbl_ref[k, s]
            v = _nt(h_ref[...], wtile[buf]) + b_ref[slab]
            is_qkv = slab < N_QKV

            @pl.when(is_qkv)
            def _():
                put(obf, obf_sems, qkv_ref.at[jnp.minimum(slab, N_QKV - 1)], n_bf, v.astype(BF16))

            @pl.when(jnp.logical_not(is_qkv))
            def _():
                put(of32, of32_sems, rest_ref.at[jnp.maximum(slab - N_QKV, 0)], n_f32, v)

            return n_bf + is_qkv.astype(jnp.int32), n_f32 + 1 - is_qkv.astype(jnp.int32)

        n_bf, n_f32 = lax.fori_loop(0, N_SLAB, step, (jnp.int32(0), jnp.int32(0)))
        drain(obf, obf_sems, qkv_ref.at[0], n_bf)
        drain(of32, of32_sems, rest_ref.at[0], n_f32)

        for slots in ((1, 2, 3), (0, 4, 5, 6)):
            for a in range(1, 1 + n_o):
                for slot in slots:
                    arrive(a, slot)
        for cp in first:
            cp.wait_send()
        for j, chip in enumerate(chips):
            for ch in range(N_CHUNK):
                copy(0, 4 + j, (*chip, c), sibling, ch).wait_send()
            for a in range(1, 1 + n_o):
                copy(a, 4 + j, (*chip, c), sibling).wait_send()
        for cp in mine:
            cp.wait()

    out_shape = ((jax.ShapeDtypeStruct((N_DEV * srows, D), BF16), jax.ShapeDtypeStruct((N_QKV, t, SLAB), BF16),
                  jax.ShapeDtypeStruct((N_REST, t, SLAB), F32))
                 + tuple(jax.ShapeDtypeStruct((N_DEV * o.shape[0], o.shape[1]), o.dtype) for o in others))
    n_all = 1 + n_o
    n_sems = 7 * (N_CHUNK + n_o)
    grid_spec = pltpu.PrefetchScalarGridSpec(
        num_scalar_prefetch=1, grid=(1,),
        in_specs=[ANY, pl.BlockSpec((t, D), lambda i, tbl: (0, 0), pipeline_mode=pl.Buffered(1)),
                  pl.BlockSpec((N_SLAB, 1, SLAB), lambda i, tbl: (0, 0, 0))] + [ANY] * n_o,
        out_specs=(ANY,) * (3 + n_o),
        scratch_shapes=[pltpu.VMEM((2, SLAB, D), BF16), pltpu.VMEM((2, t, SLAB), BF16), pltpu.VMEM((2, t, SLAB), F32),
                        pltpu.SemaphoreType.DMA((n_sems,)), pltpu.SemaphoreType.DMA((n_sems,)),
                        pltpu.SemaphoreType.DMA((n_all,)), pltpu.SemaphoreType.DMA((2,)), pltpu.SemaphoreType.DMA((2,)),
                        pltpu.SemaphoreType.DMA((2,))])
    res = _pcall(body, grid_spec=grid_spec, out_shape=out_shape, name="project_gather",
                 compiler_params=_params(48, ("arbitrary",)))(table, shard, h, b_in3, *others)
    return res[0], res[1], res[2], list(res[3:])


def _bias_tables(g):
    window, dil = GROUPS[g]
    span = window // dil
    qi = jnp.arange(BLK)[:, None]
    kj = jnp.arange(2 * BLK)[None, :]
    delta = qi + BLK - kj
    valid = (delta >= 0) & (delta <= span)
    heads = jnp.arange(4, dtype=F32) + 4.0 * g
    slopes = 2.0 ** (-8.0 * (heads + 1.0) / 12.0)
    bias = -slopes[:, None, None] * (delta * dil).astype(F32)[None]
    return jnp.where(valid[None], bias, -1e30).reshape(4 * BLK, 2 * BLK)


def _head_masks(shape):
    lane = lax.broadcasted_iota(jnp.int32, shape, 1)
    return [(lane >= 64 * h) & (lane < 64 * (h + 1)) for h in range(4)]


def _stack_heads(v, masks):
    return jnp.concatenate([jnp.where(masks[h], v, jnp.zeros_like(v)) for h in range(4)], axis=0)


def _unstack_heads(v4, masks):
    out = jnp.where(masks[0], v4[0:BLK], 0.0)
    for h in range(1, 4):
        out = jnp.where(masks[h], v4[BLK * h:BLK * (h + 1)], out)
    return out


def _regroup(load_half, dst_ref, stage_ref, n, dil):
    for hlf in range(2):
        stage_ref[hlf] = load_half(hlf)

    def residue(r, carry):
        for hlf in range(2):
            dst_ref[pl.ds(pl.multiple_of(r * n, BLK), n), pl.ds(128 * hlf, 128)] = (
                stage_ref[hlf, pl.ds(r, n, stride=dil), :].astype(dst_ref.dtype))
        return carry

    lax.fori_loop(0, dil, residue, 0)


def _store_block(nat_ref, r, i, val, dil):
    for hlf in range(2):
        nat_ref[hlf, pl.ds(r + dil * BLK * i, BLK, stride=dil), :] = val[:, 128 * hlf:128 * (hlf + 1)]


def _for_blocks(block, dil, nblk):
    if dil == 1:
        block(0, 0, True)
        block(0, 1, False)

        def pair(k, carry):
            block(0, 2 * k, False)
            block(0, 2 * k + 1, False)
            return carry

        lax.fori_loop(1, nblk // 2, pair, 0)
    else:
        def residues(k, carry):
            block(2 * k, 0, True)
            block(2 * k + 1, 0, True)
            if nblk > 1:
                def loop(i, c):
                    block(2 * k, i, False)
                    block(2 * k + 1, i, False)
                    return c
                lax.fori_loop(1, nblk, loop, 0)
            return carry

        lax.fori_loop(0, dil // 2, residues, 0)


def _attn_forward(qkv, nbat):
    t = qkv.shape[1]
    seq = t // nbat
    n_grp = len(GROUPS)

    def body(qkv_ref, b0_ref, b1_ref, b2_ref, ol_ref, stage, qs_ref, ks_ref, vs_ref, *nat):
        masks = _head_masks((BLK, SLAB))
        bias_refs = (b0_ref, b1_ref, b2_ref)
        for g, (_, dil) in enumerate(GROUPS):
            n = seq // dil
            bias_ref, nat_o, nat_l = bias_refs[g], nat[2 * g], nat[2 * g + 1]
            if dil > 1:
                qd, kd, vd = qs_ref, ks_ref, vs_ref
                for which, dst in enumerate((qd, kd, vd)):
                    _regroup(lambda hlf, which=which, g=g: qkv_ref[3 * which + g, :, pl.ds(128 * hlf, 128)].astype(F32), dst, stage, n, dil)
            else:
                qd, kd, vd = qkv_ref.at[g], qkv_ref.at[3 + g], qkv_ref.at[6 + g]

            def block(r, i, first, n=n, dil=dil, qd=qd, kd=kd, vd=vd, bias_ref=bias_ref, nat_o=nat_o, nat_l=nat_l):
                base = r * n
                qs = pl.ds(pl.multiple_of(base + i * BLK, BLK), BLK)
                ks = pl.ds(pl.multiple_of(base, BLK), BLK) if first else pl.ds(pl.multiple_of(base + (i - 1) * BLK, BLK), 2 * BLK)
                q, kk, vv = qd[qs, :], kd[ks, :], vd[ks, :]
                bias = bias_ref[:, pl.ds(BLK, BLK)] if first else bias_ref[...]
                s = _nt(_stack_heads(q, masks), kk) * 0.125 + bias
                m = jnp.max(s, axis=1, keepdims=True)
                p = jnp.exp(s - m)
                den = jnp.sum(p, axis=1, keepdims=True)
                out = _unstack_heads(_nn((p * (1.0 / den)).astype(BF16), vv), masks)
                lse = _unstack_heads(jnp.broadcast_to(m + jnp.log(den), (4 * BLK, SLAB)), masks)
                _store_block(nat_o, r, i, out, dil)
                _store_block(nat_l, r, i, lse, dil)

            _for_blocks(block, dil, n // BLK)

        for hlf in range(2):
            l0, l1, l2 = nat[1][hlf], nat[3][hlf], nat[5][hlf]
            mx = jnp.maximum(jnp.maximum(l0, l1), l2)
            e0, e1, e2 = jnp.exp(l0 - mx), jnp.exp(l1 - mx), jnp.exp(l2 - mx)
            den = e0 + e1 + e2
            ol_ref[0, :, pl.ds(128 * hlf, 128)] = (e0 * nat[0][hlf] + e1 * nat[2][hlf] + e2 * nat[4][hlf]) * (1.0 / den)
            ol_ref[1, :, pl.ds(128 * hlf, 128)] = mx + jnp.log(den)

    halves = pltpu.VMEM((2, seq, 128), F32)
    bias_spec = pl.BlockSpec((4 * BLK, 2 * BLK), lambda b: (0, 0))
    return _pcall(
        body, grid=(nbat,), out_shape=jax.ShapeDtypeStruct((2, t, SLAB), F32),
        in_specs=[pl.BlockSpec((N_QKV, seq, SLAB), lambda b: (0, b, 0))] + [bias_spec] * n_grp,
        out_specs=pl.BlockSpec((2, seq, SLAB), lambda b: (0, b, 0)),
        scratch_shapes=[halves] + [pltpu.VMEM((seq, SLAB), BF16)] * 3 + [halves] * (2 * n_grp),
        name="attn_forward", compiler_params=_params(56, ("parallel",)))(qkv, *[_bias_tables(g) for g in range(n_grp)])


def _attn_backward(qkv, do_attn, ol_tot, dproj, g, nbat):
    t = qkv.shape[1]
    seq = t // nbat
    dil = GROUPS[g][1]
    n = seq // dil
    nblk = n // BLK
    qkv4 = qkv.reshape(3, 3, t, SLAB)
    dp4 = dproj.reshape(DP_SLABS // 3, 3, t, SLAB)

    def body(qkv_ref, do_ref, ol_ref, bias_ref, dp_in, dp_ref, gb_ref, dk_acc, dv_acc, *scratch):
        del dp_in
        masks = _head_masks((BLK, SLAB))

        @pl.when(pl.program_id(0) == 0)
        def _():
            gb_ref[...] = jnp.zeros_like(gb_ref)

        dk_acc[...] = jnp.zeros_like(dk_acc)
        dv_acc[...] = jnp.zeros_like(dv_acc)
        if dil > 1:
            stage, qd, kd, vd, dod, prodd, lsed, nat = scratch
            lanes = lambda hlf: pl.ds(128 * hlf, 128)
            for which, dst in enumerate((qd, kd, vd)):
                _regroup(lambda hlf, which=which: qkv_ref[which, 0, :, lanes(hlf)].astype(F32), dst, stage, n, dil)
            _regroup(lambda hlf: do_ref[:, lanes(hlf)].astype(F32), dod, stage, n, dil)
            _regroup(lambda hlf: do_ref[:, lanes(hlf)].astype(F32) * ol_ref[0, :, lanes(hlf)], prodd, stage, n, dil)
            _regroup(lambda hlf: ol_ref[1, :, lanes(hlf)], lsed, stage, n, dil)
        else:
            qd, kd, vd = qkv_ref.at[0, 0], qkv_ref.at[1, 0], qkv_ref.at[2, 0]

        def block(r, i, first):
            base = r * n
            qs = pl.ds(pl.multiple_of(base + i * BLK, BLK), BLK)
            ks = pl.ds(pl.multiple_of(base, BLK), BLK) if first else pl.ds(pl.multiple_of(base + (i - 1) * BLK, BLK), 2 * BLK)
            q, kk, vv = qd[qs, :], kd[ks, :], vd[ks, :]
            if dil > 1:
                do, prod, lse = dod[qs, :], prodd[qs, :], lsed[qs, :]
            else:
                do = do_ref[qs, :]
                prod = do.astype(F32) * ol_ref[0, qs, :]
                lse = ol_ref[1, qs, :]
            q4, do4 = _stack_heads(q, masks), _stack_heads(do, masks)
            bias = bias_ref[:, pl.ds(BLK, BLK)] if first else bias_ref[...]
            lse4 = jnp.concatenate([lse[:, 64 * h:64 * h + 1] for h in range(4)], axis=0)
            delta4 = jnp.concatenate([jnp.sum(jnp.where(masks[h], prod, 0.0), axis=1, keepdims=True) for h in range(4)], axis=0)
            p = jnp.exp(_nt(q4, kk) * 0.125 + bias - lse4)
            ds = (p * (_nt(do4, vv) - delta4)).astype(BF16)
            dv_acc[ks, :] += _tn(p.astype(BF16), do4)
            dk_acc[ks, :] += _tn(ds, q4) * 0.125
            dq = _unstack_heads(_nn(ds, kk), masks) * 0.125
            if dil > 1:
                _store_block(nat, r, i, dq, dil)
            else:
                dp_ref[0, 0, qs, :] = dq.astype(BF16)
            gb_ref[0] += _part8(dq)

        _for_blocks(block, dil, nblk)
        gb_ref[1] += _part8(dk_acc[...])
        gb_ref[2] += _part8(dv_acc[...])
        if dil > 1:
            def flush(which):
                for hlf in range(2):
                    dp_ref[which, 0, :, pl.ds(128 * hlf, 128)] = nat[hlf].astype(BF16)

            def to_token_order(acc_ref):
                def residue(r, carry):
                    for hlf in range(2):
                        nat[hlf, pl.ds(r, n, stride=dil), :] = acc_ref[pl.ds(pl.multiple_of(r * n, BLK), n), pl.ds(128 * hlf, 128)]
                    return carry
                lax.fori_loop(0, dil, residue, 0)

            flush(0)
            to_token_order(dk_acc)
            flush(1)
            to_token_order(dv_acc)
            flush(2)
        else:
            dp_ref[1, 0] = dk_acc[...].astype(BF16)
            dp_ref[2, 0] = dv_acc[...].astype(BF16)

    scratch = [pltpu.VMEM((seq, SLAB), F32)] * 2
    if dil > 1:
        scratch += ([pltpu.VMEM((2, seq, 128), F32)] + [pltpu.VMEM((seq, SLAB), BF16)] * 4 + [pltpu.VMEM((seq, SLAB), F32)] * 2
                    + [pltpu.VMEM((2, seq, 128), F32)])
    dp, gb = _pcall(
        body, grid=(nbat,),
        out_shape=(jax.ShapeDtypeStruct(dp4.shape, BF16), jax.ShapeDtypeStruct((3, 8, SLAB), F32)),
        in_specs=[pl.BlockSpec((3, 1, seq, SLAB), lambda b: (0, g, b, 0)),
                  pl.BlockSpec((seq, SLAB), lambda b: (b, 0)),
                  pl.BlockSpec((2, seq, SLAB), lambda b: (0, b, 0)),
                  pl.BlockSpec((4 * BLK, 2 * BLK), lambda b: (0, 0)), ANY],
        out_specs=(pl.BlockSpec((3, 1, seq, SLAB), lambda b: (DP_SLABS // 9 - 1, g, b, 0)),
                   pl.BlockSpec((3, 8, SLAB), lambda b: (0, 0, 0))),
        scratch_shapes=scratch, input_output_aliases={4: 0}, name=f"attn_backward_{g}",
        compiler_params=_params(48, ("arbitrary",)))(qkv4, do_attn, ol_tot, _bias_tables(g), dp4)
    return dp.reshape(DP_SLABS, t, SLAB), gb


def _mid(rest, ol_tot, x, tgt, ada, cw, b_out, ln_g, ln_b, w_pa_t, w_pb, w_out, tm=256):
    t = x.shape[0]
    nbat = ada.shape[0]
    nt = t // tm
    tps = nt // nbat

    def body(rest_ref, halo_ref, ol_ref, x_ref, t_ref, ada_ref, cw_ref, bout_ref, lng_ref, lnb_ref,
             wpat_ref, wpb_ref, wout_ref,
             dp_ref, gx0_ref, doa_ref, mg_ref, dof_ref, bbs_ref, dyc_ref, a_ref, dya_ref,
             gbr_ref, sv_ref, dgate_ref, carry_ref, keep_ref):
        i = pl.program_id(0)
        ti = nt - 1 - i
        pos = ti % tps

        @pl.when(i == 0)
        def _():
            gbr_ref[...] = jnp.zeros_like(gbr_ref)
            sv_ref[...] = jnp.zeros_like(sv_ref)

        @pl.when(pos == tps - 1)
        def _():
            dgate_ref[...] = jnp.zeros_like(dgate_ref)
            carry_ref[...] = jnp.zeros_like(carry_ref)

        row = lax.broadcasted_iota(jnp.int32, (tm, SLAB), 0)
        halo_on = (pos > 0).astype(F32)

        def cols(s):
            return pl.ds(SLAB * s, SLAB)

        o_attn = ol_ref[0]
        z_a = rest_ref[R_ZA]
        sg_za = _sigmoid(z_a)
        a_ref[...] = (o_attn * z_a * sg_za).astype(BF16)
        y_attn = _nt(a_ref[...], wpat_ref[...])

        for s in range(4):
            u = rest_ref[R_GC + s] * rest_ref[R_UX + s]
            hu = halo_ref[R_GC + s] * halo_ref[R_UX + s] * halo_on
            u1 = jnp.where(row == 0, hu[7:8], pltpu.roll(u, 1, 0))
            u2 = jnp.where(row == 0, hu[6:7], jnp.where(row == 1, hu[7:8], pltpu.roll(u, 2, 0)))
            conv = cw_ref[0:1, cols(s)] * u2 + cw_ref[1:2, cols(s)] * u1 + cw_ref[2:3, cols(s)] * u
            zc = rest_ref[R_ZC + s]
            sg = _sigmoid(zc)
            keep_ref[2, :, cols(s)], keep_ref[3, :, cols(s)], keep_ref[4, :, cols(s)], keep_ref[5, :, cols(s)] = u1, u2, conv, sg
            bbs_ref[:, cols(s)] = (rest_ref[R_GB + s] * conv * (zc * sg)).astype(BF16)
        y_conv = _nn(bbs_ref[...], wpb_ref[...])

        for s in range(4):
            s_a, s_b = _sigmoid(rest_ref[R_GA + s]), _sigmoid(rest_ref[R_GBM + s])
            keep_ref[0, :, cols(s)], keep_ref[1, :, cols(s)] = s_a, s_b
            mg_ref[:, cols(s)] = (s_a * y_attn[:, SLAB * s:SLAB * (s + 1)] + s_b * y_conv[:, SLAB * s:SLAB * (s + 1)]).astype(BF16)
        o = _nn(mg_ref[...], wout_ref[...]) + bout_ref[...]
        gate = ada_ref[0, 2:3, :]
        r = ALPHA * x_ref[...] + gate * o
        mu = jnp.mean(r, axis=1, keepdims=True)
        rc = r - mu
        rstd = lax.rsqrt(jnp.mean(rc * rc, axis=1, keepdims=True) + LN_EPS)
        xhat = rc * rstd
        err = xhat * lng_ref[...] + lnb_ref[...] - t_ref[...]
        sv_ref[6] += _part8(err * err)
        dy = err * (1.0 / D)
        sv_ref[0] += _part8(dy * xhat)
        sv_ref[1] += _part8(dy)
        dxh = dy * lng_ref[...]
        dr = rstd * (dxh - jnp.mean(dxh, axis=1, keepdims=True) - xhat * jnp.mean(dxh * xhat, axis=1, keepdims=True))
        gx0_ref[...] = ALPHA * dr
        dgate_ref[0] += _part8(dr * o)
        do_ = dr * gate
        sv_ref[2] += _part8(do_)
        dof_ref[...] = do_.astype(BF16)
        dmerged = _nt(dof_ref[...], wout_ref[...])
        for s in range(4):
            s_a, s_b = keep_ref[0, :, cols(s)], keep_ref[1, :, cols(s)]
            dm = dmerged[:, SLAB * s:SLAB * (s + 1)]
            ya, yc = y_attn[:, SLAB * s:SLAB * (s + 1)], y_conv[:, SLAB * s:SLAB * (s + 1)]
            dya_ref[:, cols(s)] = (dm * s_a).astype(BF16)
            dyc_ref[:, cols(s)] = (dm * s_b).astype(BF16)
            dga = dm * ya * s_a * (1.0 - s_a)
            dgb = dm * yc * s_b * (1.0 - s_b)
            dp_ref[R_GA + s] = dga.astype(BF16)
            dp_ref[R_GBM + s] = dgb.astype(BF16)
            gbr_ref[R_GA + s] += _part8(dga)
            gbr_ref[R_GBM + s] += _part8(dgb)

        da = _nn(dya_ref[...], wpat_ref[...])
        doa_ref[...] = (da * z_a * sg_za).astype(BF16)
        dza = da * o_attn * (sg_za * (1.0 + z_a * (1.0 - sg_za)))
        dp_ref[R_ZA] = dza.astype(BF16)
        gbr_ref[R_ZA] += _part8(dza)

        dbb = _nt(dyc_ref[...], wpb_ref[...])
        for s in range(4):
            ux, gc, zc = rest_ref[R_UX + s], rest_ref[R_GC + s], rest_ref[R_ZC + s]
            u = gc * ux
            u1, u2, conv, sg = keep_ref[2, :, cols(s)], keep_ref[3, :, cols(s)], keep_ref[4, :, cols(s)], keep_ref[5, :, cols(s)]
            gb = rest_ref[R_GB + s]
            d_b = dbb[:, SLAB * s:SLAB * (s + 1)]
            szc = zc * sg
            dgb_ = d_b * conv * szc
            dconv = d_b * gb * szc
            dzc = d_b * gb * conv * (sg * (1.0 + zc * (1.0 - sg)))
            sv_ref[3, :, cols(s)] += _part8(dconv * u2)
            sv_ref[4, :, cols(s)] += _part8(dconv * u1)
            sv_ref[5, :, cols(s)] += _part8(dconv * u)
            nxt = carry_ref[:, cols(s)]
            d1 = jnp.where(row == tm - 1, nxt[0:1], pltpu.roll(dconv, tm - 1, 0))
            d2 = jnp.where(row == tm - 1, nxt[1:2], jnp.where(row == tm - 2, nxt[0:1], pltpu.roll(dconv, tm - 2, 0)))
            carry_ref[:, cols(s)] = dconv[0:8]
            du = cw_ref[2:3, cols(s)] * dconv + cw_ref[1:2, cols(s)] * d1 + cw_ref[0:1, cols(s)] * d2
            dgc, dux = du * ux, du * gc
            for slab, val in ((R_GB + s, dgb_), (R_ZC + s, dzc), (R_GC + s, dgc), (R_UX + s, dux)):
                dp_ref[slab] = val.astype(BF16)
                gbr_ref[slab] += _part8(val)

    def tile(i):
        return nt - 1 - i

    row_blk = lambda i: (tile(i), 0)
    slab_blk = lambda i: (0, tile(i), 0)
    const2 = lambda i: (0, 0)
    const3 = lambda i: (0, 0, 0)
    in_specs = [
        pl.BlockSpec((N_REST, tm, SLAB), slab_blk),
        pl.BlockSpec((N_REST, 8, SLAB), lambda i: (0, jnp.maximum(tile(i) * (tm // 8) - 1, 0), 0)),
        pl.BlockSpec((1, tm, SLAB), slab_blk),
        pl.BlockSpec((tm, D), row_blk), pl.BlockSpec((tm, D), row_blk),
        pl.BlockSpec((1, 3, D), lambda i: (tile(i) // tps, 0, 0)),
        pl.BlockSpec((3, D), const2), pl.BlockSpec((1, D), const2), pl.BlockSpec((1, D), const2), pl.BlockSpec((1, D), const2),
        pl.BlockSpec((D, SLAB), const2), pl.BlockSpec((D, D), const2), pl.BlockSpec((D, D), const2)]
    bf_rows = lambda: jax.ShapeDtypeStruct((t, D), BF16)
    out_shape = (
        jax.ShapeDtypeStruct((DP_SLABS, t, SLAB), BF16), jax.ShapeDtypeStruct((t, D), F32),
        jax.ShapeDtypeStruct((t, SLAB), BF16),
        bf_rows(), bf_rows(), bf_rows(), bf_rows(), jax.ShapeDtypeStruct((t, SLAB), BF16), bf_rows(),
        jax.ShapeDtypeStruct((N_REST, 8, SLAB), F32), jax.ShapeDtypeStruct((7, 8, D), F32),
        jax.ShapeDtypeStruct((nbat, 8, D), F32))
    out_specs = (
        pl.BlockSpec((N_REST, tm, SLAB), slab_blk), pl.BlockSpec((tm, D), row_blk),
        pl.BlockSpec((tm, SLAB), row_blk),
        pl.BlockSpec((tm, D), row_blk), pl.BlockSpec((tm, D), row_blk), pl.BlockSpec((tm, D), row_blk),
        pl.BlockSpec((tm, D), row_blk), pl.BlockSpec((tm, SLAB), row_blk), pl.BlockSpec((tm, D), row_blk),
        pl.BlockSpec((N_REST, 8, SLAB), const3), pl.BlockSpec((7, 8, D), const3),
        pl.BlockSpec((1, 8, D), lambda i: (tile(i) // tps, 0, 0)))
    return _pcall(body, grid=(nt,), out_shape=out_shape, in_specs=in_specs, out_specs=out_specs,
                  scratch_shapes=[pltpu.VMEM((8, D), F32), pltpu.VMEM((6, tm, D), F32)], name="mid",
                  compiler_params=_params(56, ("arbitrary",)))(
        rest, rest, ol_tot, x, tgt, ada, cw, b_out, ln_g, ln_b, w_pa_t, w_pb, w_out)


def _tn_matmul(lhs, rhs, lhs_spec, n_steps, out_rows, out_index, name, after):
    t, n = rhs.shape

    def body(l_ref, r_ref, after_ref, o_ref):
        del after_ref
        o_ref[...] = _tn(l_ref[0] if len(l_ref.shape) == 3 else l_ref[...], r_ref[...])

    return _pcall(body, grid=(n_steps,), out_shape=jax.ShapeDtypeStruct((out_rows, n), F32),
                  in_specs=[lhs_spec, pl.BlockSpec((t, n), lambda j: (0, 0)), ANY],
                  out_specs=pl.BlockSpec((SLAB, n), out_index), name=name,
                  compiler_params=_params(48, ("parallel",)))(lhs, rhs, after)


def _grad_rows_2d(lhs, rhs, name, after):
    t, k = lhs.shape
    return _tn_matmul(lhs, rhs, pl.BlockSpec((t, SLAB), lambda j: (0, j)), k // SLAB, k, lambda j: (j, 0), name, after)


def _w_row_block(j):
    return (j + N_QKV) % N_SLAB


def _dp_slab(j):
    return jnp.where(j < N_REST, j, j + 2)


def _grad_w_in_t(dproj, h):
    t = h.shape[0]
    return _tn_matmul(dproj, h, pl.BlockSpec((1, t, SLAB), lambda j: (_dp_slab(j), 0, 0)), N_SLAB, D_IN,
                      lambda j: (_w_row_block(j), 0), "grad_w_in", h)


def _grad_h(dproj, w_in_t, gx0, x, ada, after, tm=512):
    t = x.shape[0]
    nbat = ada.shape[0]
    tps = (t // nbat) // tm

    def body(dp_ref, w_ref, gx0_ref, x_ref, ada_ref, after_ref, gx_ref, dss_ref):
        del after_ref
        i = pl.program_id(0)
        dh = None
        for j in range(N_SLAB):
            slab = j if j < N_REST else j + 2
            part = _nn(dp_ref[slab], w_ref[pl.ds(SLAB * ((j + N_QKV) % N_SLAB), SLAB), :])
            dh = part if dh is None else dh + part
        gx_ref[...] = gx0_ref[...] + dh * (1.0 + ada_ref[0, 1:2, :])

        @pl.when((i % tps) == 0)
        def _():
            dss_ref[...] = jnp.zeros_like(dss_ref)

        dss_ref[0, 0] += _part8(dh)
        dss_ref[0, 1] += _part8(dh * x_ref[...])

    return _pcall(
        body, grid=(t // tm,),
        out_shape=(jax.ShapeDtypeStruct((t, D), F32), jax.ShapeDtypeStruct((nbat, 2, 8, D), F32)),
        in_specs=[pl.BlockSpec((DP_SLABS, tm, SLAB), lambda i: (0, i, 0)),
                  pl.BlockSpec((D_IN, D), lambda i: (0, 0), pipeline_mode=pl.Buffered(1)),
                  pl.BlockSpec((tm, D), lambda i: (i, 0)), pl.BlockSpec((tm, D), lambda i: (i, 0)),
                  pl.BlockSpec((1, 3, D), lambda i: (i // tps, 0, 0)), ANY],
        out_specs=(pl.BlockSpec((tm, D), lambda i: (i, 0)),
                   pl.BlockSpec((1, 2, 8, D), lambda i: (i // tps, 0, 0, 0))),
        name="grad_h", compiler_params=_params(60, ("arbitrary",)))(dproj, w_in_t, gx0, x, ada, after)


def _chip(m):
    x, y, _ = _my_position()
    return (x ^ ((m >> 1) & 1), y ^ (m & 1))


def _exchange_siblings(grads, after, name):
    n = len(grads)

    def body(*refs):
        copies = _sibling_copies(refs[:n], refs[n + 1:2 * n + 1], refs[2 * n + 1], refs[2 * n + 2])
        for cp in copies:
            cp.start()
        for cp in copies:
            cp.wait()

    return _pcall(body, out_shape=tuple(_sibling_zones(grads)), in_specs=[ANY] * (n + 1), out_specs=(ANY,) * n,
                  name=name, scratch_shapes=[pltpu.SemaphoreType.DMA((4 * n,))] * 2)(*grads, after)


def _sibling_zones(grads):
    return [jax.ShapeDtypeStruct((4, g.shape[0] // N_DEV, g.shape[1]), g.dtype) for g in grads]


def _sibling_copies(srcs, lands, send_sems, recv_sems):
    x, y, c = _my_position()
    copies = []
    for a, (src, land) in enumerate(zip(srcs, lands)):
        rows = land.shape[1]
        for m in range(4):
            dev = _flat(*_chip(m), 1 - c)
            copies.append(pltpu.make_async_remote_copy(
                src_ref=src.at[pl.ds(pl.multiple_of(dev * rows, 8), rows), :], dst_ref=land.at[m],
                send_sem=send_sems.at[4 * a + m], recv_sem=recv_sems.at[4 * a + m], device_id=(x, y, 1 - c),
                device_id_type=MESH))
    return copies


def _chip_copies(srcs, lands, send_sems, recv_sems):
    _, _, c = _my_position()
    return [pltpu.make_async_remote_copy(
        src_ref=srcs[a].at[m - 1], dst_ref=lands[a].at[m - 1], send_sem=send_sems.at[3 * a + m - 1],
        recv_sem=recv_sems.at[3 * a + m - 1], device_id=(*_chip(m), c), device_id_type=MESH)
        for a in range(len(srcs)) for m in range(1, 4)]


HBM = pl.BlockSpec(memory_space=pltpu.HBM)
SEM = pl.BlockSpec(memory_space=pltpu.SEMAPHORE)
SPLIT_COPY = pltpu.CompilerParams(has_side_effects=pltpu.SideEffectType.DATAFLOW_SIDE_EFFECTING)


def _start_copies(make_copies, n_sems, srcs, zones, name):
    n = len(srcs)

    def body(*refs):
        for cp in make_copies(refs[:n], refs[n:2 * n], refs[2 * n], refs[2 * n + 1]):
            cp.start()
        refs[-1][...] = jnp.zeros_like(refs[-1])

    hbm = tuple(pltpu.HBM(b.shape, b.dtype) for b in list(srcs) + list(zones))
    out_shape = (pltpu.SemaphoreType.DMA((n_sems,)), pltpu.SemaphoreType.DMA((n_sems,))) + hbm + (jax.ShapeDtypeStruct((8, 128), F32),)
    operands = [pltpu.with_memory_space_constraint(b, pltpu.HBM) for b in srcs]
    operands += [pltpu.with_memory_space_constraint(lax.empty(z.shape, z.dtype), pltpu.HBM) for z in zones]
    res = _pcall(body, out_shape=out_shape, in_specs=[HBM] * (2 * n), out_specs=(SEM, SEM) + (HBM,) * (2 * n) + (VMEM,),
                 input_output_aliases={i: 2 + i for i in range(2 * n)}, name=name, compiler_params=SPLIT_COPY)(*operands)
    return (res[0], res[1], res[2:2 + n], res[2 + n:2 + 2 * n]), res[-1]


def _wait_copies(make_copies, flight, after, name):
    send_sems, recv_sems, srcs, zones = flight
    n = len(srcs)

    def body(*refs):
        for cp in make_copies(refs[:n], refs[n:2 * n], refs[2 * n], refs[2 * n + 1]):
            cp.wait_send()
            cp.wait_recv()

    hbm = tuple(pltpu.HBM(b.shape, b.dtype) for b in list(srcs) + list(zones))
    res = _pcall(body, out_shape=hbm, in_specs=[HBM] * (2 * n) + [SEM, SEM, ANY], out_specs=(HBM,) * (2 * n),
                 input_output_aliases={i: i for i in range(2 * n)}, name=name, compiler_params=SPLIT_COPY)(
        *srcs, *zones, send_sems, recv_sems, after)
    return res[:n], res[n:]


def _pair_sums(devs, grads, lands, n_steps, name):
    n = len(grads)
    rows = [l.shape[1] for l in lands]
    rbs = [r // n_steps for r in rows]

    def body(devs_ref, *refs):
        del devs_ref
        g_refs, land_refs, outs = refs[:4 * n], refs[4 * n:5 * n], refs[5 * n:]
        for a in range(n):
            outs[2 * a][...] = g_refs[4 * a][...] + land_refs[a][0]
            for m in range(1, 4):
                outs[2 * a + 1][m - 1] = (g_refs[4 * a + m][...] + land_refs[a][m]).astype(BF16)

    def block_of(m, per_dev):
        return lambda i, devs_ref: (devs_ref[m] * per_dev + i, 0)

    in_specs = [pl.BlockSpec((rb, l.shape[2]), block_of(m, n_steps)) for rb, l in zip(rbs, lands) for m in range(4)]
    in_specs += [pl.BlockSpec((4, rb, l.shape[2]), lambda i, devs_ref: (0, i, 0)) for rb, l in zip(rbs, lands)]
    out_shape, out_specs = [], []
    for rb, l in zip(rbs, lands):
        out_shape += [jax.ShapeDtypeStruct(l.shape[1:], F32), jax.ShapeDtypeStruct((3,) + l.shape[1:], BF16)]
        out_specs += [pl.BlockSpec((rb, l.shape[2]), lambda i, devs_ref: (i, 0)),
                      pl.BlockSpec((3, rb, l.shape[2]), lambda i, devs_ref: (0, i, 0))]
    grid_spec = pltpu.PrefetchScalarGridSpec(num_scalar_prefetch=1, grid=(n_steps,), in_specs=in_specs, out_specs=tuple(out_specs))
    res = _pcall(body, grid_spec=grid_spec, out_shape=tuple(out_shape), name=name,
                 compiler_params=_params(48, ("parallel",)))(devs, *[g for g in grads for _ in range(4)], *lands)
    return res[0::2], res[1::2]


def _final_sums(mine, lands, n_steps, name):
    n = len(mine)
    rbs = [o.shape[0] // n_steps for o in mine]

    def body(*refs):
        mine_refs, land_refs, outs = refs[:n], refs[n:2 * n], refs[2 * n:]
        for a in range(n):
            tot = mine_refs[a][...]
            for m in range(3):
                tot = tot + land_refs[a][m].astype(F32)
            outs[a][...] = tot

    in_specs = ([pl.BlockSpec((rb, o.shape[1]), lambda i: (i, 0)) for rb, o in zip(rbs, mine)]
                + [pl.BlockSpec((3, rb, o.shape[1]), lambda i: (0, i, 0)) for rb, o in zip(rbs, mine)])
    out_specs = tuple(pl.BlockSpec((rb, o.shape[1]), lambda i: (i, 0)) for rb, o in zip(rbs, mine))
    out_shape = tuple(jax.ShapeDtypeStruct(o.shape, F32) for o in mine)
    return _pcall(body, grid=(n_steps,), out_shape=out_shape, in_specs=in_specs, out_specs=out_specs, name=name,
                  compiler_params=_params(32, ("parallel",)))(*mine, *lands)


def _reduce_scatter_begin(big, small_after_start):
    c = lax.axis_index("c")
    devs = jnp.stack([_flat(*_chip(m), c) for m in range(4)]).astype(jnp.int32)
    flight, token = _start_copies(_sibling_copies, 4, [big], _sibling_zones([big]), "siblings_start")
    small = small_after_start(token)
    (big,), big_lands = _wait_copies(_sibling_copies, flight, small[-1], "siblings_wait")
    big_mine, big_send = _pair_sums(devs, [big], big_lands, 4, "pair_sums_w_in")
    big_flight, token = _start_copies(_chip_copies, 3, list(big_send), list(big_send), "chips_start_w_in")
    small_lands = _exchange_siblings(small, token, "exchange_siblings_rest")
    small_mine, small_send = _pair_sums(devs, small, small_lands, 1, "pair_sums_rest")
    small_flight, token = _start_copies(_chip_copies, 3 * len(small), list(small_send), list(small_send), "chips_start_rest")
    return (big_flight, small_flight, list(big_mine) + list(small_mine)), token


def _reduce_scatter_end(state, after):
    big_flight, small_flight, mine = state
    _, big_got = _wait_copies(_chip_copies, big_flight, after, "chips_wait_w_in")
    _, small_got = _wait_copies(_chip_copies, small_flight, after, "chips_wait_rest")
    small = _final_sums(mine[1:], small_got, 1, "final_sums_rest")
    return (mine[0], big_got[0]), list(small)


def _adamw(w, g, m, v):
    m_new = B1 * m + (1.0 - B1) * g
    v_new = B2 * v + (1.0 - B2) * (g * g)
    m_hat = m_new / (1.0 - B1 ** STEP)
    v_hat = v_new / (1.0 - B2 ** STEP)
    delta = -LR * (m_hat / (jnp.sqrt(v_hat) + EPS) + WD * w)
    return delta, m_new, v_new


def _final_sum_adam_rows(mine, land, w, m, v, n_steps, name):
    rows, ncol = w.shape
    blk = pl.BlockSpec((rows // n_steps, ncol), lambda i: (i, 0))

    def body(mine_ref, land_ref, w_ref, m_ref, v_ref, g_ref, d_ref, mo_ref, vo_ref):
        g = mine_ref[...]
        for k in range(3):
            g = g + land_ref[k].astype(F32)
        g_ref[...] = g
        d_ref[...], mo_ref[...], vo_ref[...] = _adamw(w_ref[...], g, m_ref[...], v_ref[...])

    shape = jax.ShapeDtypeStruct(w.shape, F32)
    return _pcall(body, grid=(n_steps,), out_shape=(shape,) * 4,
                  in_specs=[blk, pl.BlockSpec((3, rows // n_steps, ncol), lambda i: (0, i, 0)), blk, blk, blk],
                  out_specs=(blk,) * 4, name=name, compiler_params=_params(32, ("parallel",)))(mine, land, w, m, v)


def _adam_transposed(g_t, w, m, v, name):
    n, k = g_t.shape
    rb = min(k, 128)

    def body(gt_ref, w_ref, m_ref, v_ref, g_ref, d_ref, mo_ref, vo_ref):
        for src, skip, dst, size in _column_chunks(n):
            sl = pl.ds(dst, size)
            g = gt_ref[pl.ds(src, 128), :].T[:, skip:]
            delta, m_new, v_new = _adamw(w_ref[:, sl], g, m_ref[:, sl], v_ref[:, sl])
            g_ref[:, sl], d_ref[:, sl], mo_ref[:, sl], vo_ref[:, sl] = g, delta, m_new, v_new

    shape = jax.ShapeDtypeStruct(w.shape, F32)
    rows = pl.BlockSpec((rb, n), lambda i: (i, 0))
    return _pcall(body, grid=(k // rb,), out_shape=(shape,) * 4,
                  in_specs=[pl.BlockSpec((n, rb), lambda i: (0, i)), rows, rows, rows], out_specs=(rows,) * 4, name=name,
                  compiler_params=_params(32, ("parallel",)))(g_t, w, m, v)


def _adam_many(items, name):
    n = len(items)

    def body(*refs):
        ins, outs = refs[:4 * n], refs[4 * n:]
        for a in range(n):
            w_ref, g_ref, m_ref, v_ref = ins[4 * a:4 * a + 4]
            delta, m_new, v_new = _adamw(w_ref[...], g_ref[...], m_ref[...], v_ref[...])
            outs[3 * a][...], outs[3 * a + 1][...], outs[3 * a + 2][...] = delta, m_new, v_new

    out_shape = tuple(jax.ShapeDtypeStruct(it[0].shape, F32) for it in items for _ in range(3))
    flat = [arr for it in items for arr in it]
    res = _pcall(body, grid=(1,), out_shape=out_shape, in_specs=[_whole(a) for a in flat],
                 out_specs=tuple(_whole(o) for o in out_shape), name=name, compiler_params=_params(32))(*flat)
    return [tuple(res[3 * a:3 * a + 3]) for a in range(n)]


def _adam_w_ada(cact_all, dada_mine, w, m, v):
    def body(c_ref, d_ref, w_ref, m_ref, v_ref, g_ref, dl_ref, mo_ref, vo_ref):
        g = _tn(c_ref[...].astype(BF16), d_ref[...].astype(BF16))
        delta, m_new, v_new = _adamw(w_ref[...], g, m_ref[...], v_ref[...])
        g_ref[...], dl_ref[...], mo_ref[...], vo_ref[...] = g, delta, m_new, v_new

    shape = jax.ShapeDtypeStruct(w.shape, F32)
    operands = (cact_all, dada_mine, w, m, v)
    return _pcall(body, grid=(1,), out_shape=(shape,) * 4, in_specs=[_whole(a) for a in operands],
                  out_specs=(_whole(w),) * 4, name="adam_w_ada", compiler_params=_params(32))(*operands)


def kernel(x, c, w_ada, b_ada, w_in, b_in, conv_w, w_proj_attn, w_proj_conv, w_out, b_out, ln_g, ln_b, loss_target, m_w_ada, m_b_ada, m_w_in, m_b_in, m_conv_w, m_w_proj_attn, m_w_proj_conv, m_w_out, m_b_out, m_ln_g, m_ln_b, v_w_ada, v_b_ada, v_w_in, v_b_in, v_conv_w, v_w_proj_attn, v_w_proj_conv, v_w_out, v_b_out, v_ln_g, v_ln_b):
    nbat, seq, _ = x.shape
    t = nbat * seq
    me = _flat(*_my_position())
    x2, tgt2 = x.reshape(t, D), loss_target.reshape(t, D)
    sq = lambda a: a.reshape(a.shape[1:])

    tr = lambda a: a[0].T
    w_in_rows = tr(w_in)
    w_in_t_s = _cast_rows(w_in_rows, 4, "cast_w_in")
    w_pa_t_s, w_pb_s, w_out_s, cact_s, cw_s = _prep(sq(w_proj_attn), sq(w_proj_conv), sq(w_out), c, sq(conv_w))

    ncol = w_ada.shape[2]
    b_ada_mine = lax.dynamic_slice(b_ada, (0, me * ncol), (1, ncol))
    ada_slots, cact_slots, cw_slots = _ada_forward(cact_s, cw_s, sq(w_ada), b_ada_mine)
    cact_all = cact_slots[:, :nbat].reshape(N_DEV * nbat, D)
    cw = cw_slots[:, :3].transpose(1, 0, 2).reshape(3, D)
    ada_all = ada_slots[:, :, :nbat].transpose(1, 2, 0, 3).reshape(N_DEV * nbat, 3, D)
    ada = lax.dynamic_slice(ada_all, (me * nbat, 0, 0), (nbat, 3, D))

    h = _make_h(x2, ada)
    w_in_t, qkv, rest, (w_pa_t, w_pb, w_o) = _project_gather(w_in_t_s, h, b_in.reshape(N_SLAB, 1, SLAB), [w_pa_t_s, w_pb_s, w_out_s])
    ol_tot = _attn_forward(qkv, nbat)
    (dproj, gx0, do_attn, merged, do_f, bbs, dyc, a_bf, dya, gb_rest, svec, dgate) = _mid(
        rest, ol_tot, x2, tgt2, ada, cw, b_out, ln_g, ln_b, w_pa_t, w_pb, w_o)

    gb_qkv = []
    for g in range(3):
        dproj, gb = _attn_backward(qkv, do_attn, ol_tot, dproj, g, nbat)
        gb_qkv.append(gb)
    g_w_in_t = _grad_w_in_t(dproj, h)

    def small_grads(token):
        g_w_out = _grad_rows_2d(merged, do_f, "grad_w_out", token)
        g_w_pb = _grad_rows_2d(bbs, dyc, "grad_w_proj_conv", g_w_out)
        g_w_pa_t = _grad_rows_2d(dya, a_bf, "grad_w_proj_attn", g_w_pb)
        return [g_w_out, g_w_pb, g_w_pa_t]

    rs_state, token = _reduce_scatter_begin(g_w_in_t, small_grads)
    grad_x, dss = _grad_h(dproj, w_in_t, gx0, x2, ada, token)

    rows8, tot, g_bada = _small_reduce(gb_rest, gb_qkv, svec, dgate, dss)
    (g_in_mine, g_in_got), (g_out, g_pb, g_pa_t) = _reduce_scatter_end(rs_state, tot)
    loss = tot[0, P_LOSS]
    dada_all = rows8[:, 0, P_DADA:].reshape(N_DEV * nbat, 3 * D)
    dada_mine = lax.dynamic_slice(dada_all, (0, me * ncol), (N_DEV * nbat, ncol))

    g_in_t, d_win_t, nm_win_t, nv_win_t = _final_sum_adam_rows(g_in_mine, g_in_got, w_in_rows, tr(m_w_in), tr(v_w_in), 4, "adam_w_in")
    g_win, d_win, nm_win, nv_win = g_in_t.T, d_win_t.T, nm_win_t.T, nv_win_t.T
    g_wpa, d_wpa, nm_wpa, nv_wpa = _adam_transposed(g_pa_t, sq(w_proj_attn), sq(m_w_proj_attn), sq(v_w_proj_attn), "adam_w_proj_attn")
    g_wada, d_wada, nm_wada, nv_wada = _adam_w_ada(cact_all, dada_mine, sq(w_ada), sq(m_w_ada), sq(v_w_ada))
    g_bin = tot[:, P_BIN:P_BIN + D_IN]
    g_bout = tot[:, P_BOUT:P_BOUT + D]
    g_lng = tot[:, P_LNG:P_LNG + D]
    g_lnb = tot[:, P_LNB:P_LNB + D]
    g_conv = lax.dynamic_slice(tot[:, P_CONV:P_CONV + 3 * D].reshape(3, D), (0, me * cw_s.shape[1]), (3, cw_s.shape[1]))
    upd = _adam_many([
        (sq(w_proj_conv), g_pb, sq(m_w_proj_conv), sq(v_w_proj_conv)),
        (sq(w_out), g_out, sq(m_w_out), sq(v_w_out)),
        (b_ada, g_bada, m_b_ada, v_b_ada), (b_in, g_bin, m_b_in, v_b_in), (sq(conv_w), g_conv, sq(m_conv_w), sq(v_conv_w)),
        (b_out, g_bout, m_b_out, v_b_out), (ln_g, g_lng, m_ln_g, v_ln_g), (ln_b, g_lnb, m_ln_b, v_ln_b)], "adam_rest")
    (d_wpb, nm_wpb, nv_wpb), (d_wout, nm_wout, nv_wout), (d_bada, nm_bada, nv_bada), (d_bin, nm_bin, nv_bin), \
        (d_conv, nm_conv, nv_conv), (d_bout, nm_bout, nv_bout), (d_lng, nm_lng, nv_lng), (d_lnb, nm_lnb, nv_lnb) = upd

    ex = lambda a: a.reshape((1,) + a.shape)
    grads = [ex(g_wada), g_bada, ex(g_win), g_bin, ex(g_conv), ex(g_wpa), ex(g_pb), ex(g_out), g_bout, g_lng, g_lnb]
    deltas = [ex(d_wada), d_bada, ex(d_win), d_bin, ex(d_conv), ex(d_wpa), ex(d_wpb), ex(d_wout), d_bout, d_lng, d_lnb]
    new_m = [ex(nm_wada), nm_bada, ex(nm_win), nm_bin, ex(nm_conv), ex(nm_wpa), ex(nm_wpb), ex(nm_wout), nm_bout, nm_lng, nm_lnb]
    new_v = [ex(nv_wada), nv_bada, ex(nv_win), nv_bin, ex(nv_conv), ex(nv_wpa), ex(nv_wpb), ex(nv_wout), nv_bout, nv_lng, nv_lnb]
    return (loss, grad_x.reshape(x.shape), *grads, *deltas, *new_m, *new_v)
```

```python
import functools

import jax
import jax.numpy as jnp
from jax import lax
from jax.experimental import pallas as pl
from jax.experimental.pallas import tpu as pltpu

F32, BF16 = jnp.float32, jnp.bfloat16
MESH = pl.DeviceIdType.MESH
N_DEV = 8
D = 1024
SLAB = 256
N_QKV, N_REST = 9, 25
N_SLAB = N_QKV + N_REST
D_IN = N_SLAB * SLAB
DP_SLABS = 36
BLK = 128
GROUPS = ((128, 1), (512, 4), (2048, 16))
ALPHA = 2.0 ** 0.25
LN_EPS = 1e-5
LR, B1, B2, EPS, WD, STEP = 0.001, 0.9, 0.999, 1e-08, 0.01, 10
R_ZA, R_UX, R_GB, R_GC, R_ZC, R_GA, R_GBM = 0, 1, 5, 9, 13, 17, 21
P_BIN, P_BOUT, P_LNG, P_LNB, P_CONV, P_LOSS, P_DADA = 0, 8704, 9728, 10752, 11776, 14848, 14976
MIB = 1024 * 1024


def _pcall(body, *, out_shape, out_specs=None, **kw):
    def pin_out(shape, spec):
        blocked = isinstance(shape, jax.ShapeDtypeStruct) and getattr(spec, "block_shape", None) is not None
        return pltpu.HBM(shape.shape, shape.dtype) if blocked else shape

    n_scalar = 0
    if out_specs is None:
        specs = kw["grid_spec"].out_specs
        n_scalar = kw["grid_spec"].num_scalar_prefetch
    else:
        kw["out_specs"] = specs = out_specs
    if isinstance(out_shape, (tuple, list)):
        out_shape = tuple(pin_out(s, p) for s, p in zip(out_shape, specs))
    else:
        out_shape = pin_out(out_shape, specs)
    call = pl.pallas_call(body, out_shape=out_shape, **kw)

    def run(*operands):
        def pin(o):
            is_data = jnp.issubdtype(o.dtype, jnp.floating) or jnp.issubdtype(o.dtype, jnp.integer)
            return pltpu.with_memory_space_constraint(o, pltpu.HBM) if is_data else o
        return call(*operands[:n_scalar], *[pin(o) for o in operands[n_scalar:]])

    return run

ANY = pl.BlockSpec(memory_space=pl.ANY)
VMEM = pl.BlockSpec(memory_space=pltpu.VMEM)


def _whole(a):
    return pl.BlockSpec(a.shape, lambda i: (0,) * len(a.shape))


def _params(vmem_mib=None, sem=None):
    kw = {}
    if vmem_mib is not None:
        kw["vmem_limit_bytes"] = vmem_mib * MIB
    if sem is not None:
        kw["dimension_semantics"] = sem
    return pltpu.CompilerParams(**kw)


def _nn(a, b):
    return jnp.dot(a, b, preferred_element_type=F32)


def _nt(a, b):
    return lax.dot_general(a, b, (((1,), (1,)), ((), ())), preferred_element_type=F32)


def _tn(a, b):
    return lax.dot_general(a, b, (((0,), (0,)), ((), ())), preferred_element_type=F32)


def _sigmoid(v):
    return 1.0 / (1.0 + jnp.exp(-v))


def _part8(v):
    return v.reshape(v.shape[0] // 8, 8, v.shape[1]).sum(axis=0)


def _my_position():
    return lax.axis_index("x"), lax.axis_index("y"), lax.axis_index("c")


def _flat(px, py, pc):
    return 4 * px + 2 * py + pc


def _peer(mask):
    x, y, c = _my_position()
    return (x ^ ((mask >> 2) & 1), y ^ ((mask >> 1) & 1), c ^ (mask & 1))


def _column_chunks(n):
    chunks = [(128 * a, 0, 128 * a, 128) for a in range(n // 128)]
    if n % 128:
        chunks.append((n - 128, 128 - n % 128, 128 * (n // 128), n % 128))
    return chunks


def _cast_rows(w, n_steps, name):
    rows, ncol = w.shape
    blk = pl.BlockSpec((rows // n_steps, ncol), lambda i: (i, 0))

    def body(w_ref, o_ref):
        o_ref[...] = w_ref[...].astype(BF16)

    return _pcall(body, grid=(n_steps,), out_shape=jax.ShapeDtypeStruct(w.shape, BF16), in_specs=[blk], out_specs=blk,
                  name=name, compiler_params=_params(16, ("parallel",)))(w)


def _prep(w_pa, w_pb, w_out, c, conv_w):
    def body(wpa_ref, wpb_ref, wout_ref, c_ref, cw_ref, wpat_ref, wpb_o, wout_o, cact_ref, cwp_ref):
        wpat_ref[...] = wpa_ref[...].T.astype(BF16)
        wpb_o[...] = wpb_ref[...].astype(BF16)
        wout_o[...] = wout_ref[...].astype(BF16)
        cv = c_ref[...]
        cact_ref[...] = jnp.zeros_like(cact_ref)
        cact_ref[pl.ds(0, cv.shape[0]), :] = cv * _sigmoid(cv)
        cwp_ref[...] = jnp.zeros_like(cwp_ref)
        cwp_ref[pl.ds(0, 3), :] = cw_ref[...]

    out_shape = (jax.ShapeDtypeStruct((w_pa.shape[1], w_pa.shape[0]), BF16),
                 jax.ShapeDtypeStruct(w_pb.shape, BF16), jax.ShapeDtypeStruct(w_out.shape, BF16),
                 jax.ShapeDtypeStruct((8, D), F32), jax.ShapeDtypeStruct((8, conv_w.shape[1]), F32))
    operands = (w_pa, w_pb, w_out, c, conv_w)
    return _pcall(body, grid=(1,), out_shape=out_shape, in_specs=[_whole(a) for a in operands],
                  out_specs=tuple(_whole(o) for o in out_shape), name="prep", compiler_params=_params(16))(*operands)


def _exchange_slots(out_refs, send_sems, recv_sems, base=0):
    me = _flat(*_my_position())

    def copy(a, mask, slot):
        return pltpu.make_async_remote_copy(
            src_ref=out_refs[a].at[slot], dst_ref=out_refs[a].at[slot], send_sem=send_sems.at[base + 7 * a + mask - 1],
            recv_sem=recv_sems.at[base + 7 * a + mask - 1], device_id=_peer(mask), device_id_type=MESH)

    pairs = [(a, mask) for a in range(len(out_refs)) for mask in range(1, N_DEV)]
    for a, mask in pairs:
        copy(a, mask, me).start()
    for a, mask in pairs:
        copy(a, mask, _flat(*_peer(mask))).wait_recv()
    for a, mask in pairs:
        copy(a, mask, me).wait_send()


def _ada_forward(cact_mine, cw_mine, w_ada, b_ada_mine):
    ncol = w_ada.shape[1]

    def body(c_ref, cw_ref, w_ref, b_ref, out_ref, call_ref, cwall_ref, send_sems, recv_sems):
        me = _flat(*_my_position())
        call_ref[me] = c_ref[...]
        cwall_ref[me] = cw_ref[...]
        _exchange_slots([call_ref, cwall_ref], send_sems, recv_sems)
        c_all = call_ref[...].reshape(N_DEV * 8, D).astype(BF16)
        out_ref[me] = (_nn(c_all, w_ref[...].astype(BF16)) + b_ref[...]).reshape(N_DEV, 8, ncol)
        _exchange_slots([out_ref], send_sems, recv_sems, base=14)

    operands = (cact_mine, cw_mine, w_ada, b_ada_mine)
    out_shape = (jax.ShapeDtypeStruct((N_DEV, N_DEV, 8, ncol), F32), jax.ShapeDtypeStruct((N_DEV, 8, D), F32),
                 jax.ShapeDtypeStruct((N_DEV,) + cw_mine.shape, F32))
    return _pcall(body, grid=(1,), out_shape=out_shape, in_specs=[_whole(a) for a in operands], out_specs=(VMEM,) * 3,
                  scratch_shapes=[pltpu.SemaphoreType.DMA((21,)), pltpu.SemaphoreType.DMA((21,))], name="ada_forward",
                  compiler_params=_params(16))(*operands)


def _small_reduce(gb_rest, gb_qkv, svec, dgate, dss):
    nbat = dgate.shape[0]

    def body(gbr_ref, q0_ref, q1_ref, q2_ref, sv_ref, dg_ref, dss_ref, rows_ref, tot_ref, gbada_ref, send_sems, recv_sems):
        me = _flat(*_my_position())

        def put(off, v):
            rows_ref[me, :, pl.ds(off, v.shape[1])] = v

        def row(v):
            return jnp.sum(v, axis=0, keepdims=True)

        for g, q_ref in enumerate((q0_ref, q1_ref, q2_ref)):
            for which in range(3):
                put(P_BIN + SLAB * (3 * which + g), row(q_ref[which]))
        for s in range(N_REST):
            put(P_BIN + SLAB * (N_QKV + s), row(gbr_ref[s]))
        put(P_LNG, row(sv_ref[0]))
        put(P_LNB, row(sv_ref[1]))
        put(P_BOUT, row(sv_ref[2]))
        for j in range(3):
            put(P_CONV + D * j, row(sv_ref[3 + j]))
        loss = (0.5 / D) * jnp.sum(row(sv_ref[6]), axis=1, keepdims=True)
        put(P_LOSS, jnp.broadcast_to(loss, (1, 128)))
        for b in range(nbat):
            put(P_DADA + 3 * D * b, row(dss_ref[b, 0]))
            put(P_DADA + 3 * D * b + D, row(dss_ref[b, 1]))
            put(P_DADA + 3 * D * b + 2 * D, row(dg_ref[b]))
        _exchange_slots([rows_ref], send_sems, recv_sems)
        tot = rows_ref[0]
        for k in range(1, N_DEV):
            tot = tot + rows_ref[k]
        tot_ref[...] = tot
        gbada = tot[:, P_DADA:P_DADA + 3 * D]
        for b in range(1, nbat):
            gbada = gbada + tot[:, P_DADA + 3 * D * b:P_DADA + 3 * D * (b + 1)]
        gbada_ref[...] = gbada

    p_len = P_DADA + nbat * 3 * D
    out_shape = (jax.ShapeDtypeStruct((N_DEV, 1, p_len), F32), jax.ShapeDtypeStruct((1, p_len), F32),
                 jax.ShapeDtypeStruct((1, 3 * D), F32))
    operands = (gb_rest, *gb_qkv, svec, dgate, dss)
    return _pcall(body, grid=(1,), out_shape=out_shape, in_specs=[_whole(a) for a in operands],
                  out_specs=(VMEM, _whole(out_shape[1]), _whole(out_shape[2])),
                  scratch_shapes=[pltpu.SemaphoreType.DMA((7,)), pltpu.SemaphoreType.DMA((7,))], name="small_reduce",
                  compiler_params=_params(16))(*operands)


PIECE = 64
N_CHUNK = 4
ARRIVAL_RANK = (0, 1, 3, 5, 2, 4, 6, 7)
SLOT_MASK = (1, 4, 2, 6, 5, 3, 7)


def _arrival_tables(shard_rows):
    import numpy as np
    crow = shard_rows // N_CHUNK
    table = np.zeros((N_DEV, N_SLAB + 7 * N_CHUNK), np.int32)
    lo = [(SLAB * j) // crow for j in range(N_SLAB)]
    hi = [(SLAB * j + SLAB - 1) // crow for j in range(N_SLAB)]
    for k in range(N_DEV):
        def rank(chunk):
            shard_rank = ARRIVAL_RANK[(chunk // N_CHUNK) ^ k]
            return shard_rank if shard_rank < 2 else 2 + 8 * (chunk % N_CHUNK) + shard_rank
        order = sorted(range(N_SLAB), key=lambda j: (max(rank(lo[j]), rank(hi[j])), j))
        table[k, :N_SLAB] = order
        for slot, mask in enumerate(SLOT_MASK):
            for ch in range(N_CHUNK):
                chunk = (k ^ mask) * N_CHUNK + ch
                table[k, N_SLAB + slot * N_CHUNK + ch] = min(t for t, j in enumerate(order) if lo[j] <= chunk <= hi[j])
    return table


def _project_gather(shard, x, ada, b_in3, others, xt=512):
    t = x.shape[0]
    n_o = len(others)
    srows = shard.shape[0]
    crow = srows // N_CHUNK
    shards = [shard] + list(others)
    table = jnp.asarray(_arrival_tables(srows))
    seq_tiles = (t // ada.shape[0]) // xt

    def body(tbl_ref, *refs):
        srcs = [refs[0]] + list(refs[4:4 + n_o])
        x_ref, ada_ref, b_ref = refs[1], refs[2], refs[3]
        outs = [refs[4 + n_o]] + list(refs[8 + n_o:8 + 2 * n_o])
        qkv_ref, rest_ref, h_out = refs[5 + n_o], refs[6 + n_o], refs[7 + n_o]
        (wtile, obf, of32, h_ref, xbuf, send_sems, recv_sems, local_sems, tile_sems, obf_sems, of32_sems, x_sems,
         h_sems) = refs[8 + 2 * n_o:]
        w_full = outs[0]
        x, y, c = _my_position()
        k = _flat(x, y, c)
        me, sibling = (x, y, c), (x, y, 1 - c)
        chips = [(1 - x, y), (x, 1 - y), (1 - x, 1 - y)]

        def rows(a, px, py, pc, ch):
            r = shards[a].shape[0]
            if ch is None:
                return outs[a].at[pl.ds(pl.multiple_of(_flat(px, py, pc) * r, r), r), :]
            return outs[a].at[pl.ds(pl.multiple_of(_flat(px, py, pc) * r + ch * crow, crow), crow), :]

        def copy(a, slot, block, to, ch=None, src=None):
            sem = slot * N_CHUNK + ch if a == 0 else 7 * (N_CHUNK - 1 + a) + slot
            if src is not None and ch is not None:
                src = src.at[pl.ds(ch * crow, crow), :]
            return pltpu.make_async_remote_copy(
                src_ref=rows(a, *block, ch) if src is None else src, dst_ref=rows(a, *block, ch),
                send_sem=send_sems.at[sem], recv_sem=recv_sems.at[sem], device_id=to, device_id_type=MESH)

        mine = [pltpu.make_async_copy(srcs[a], rows(a, *me, None), local_sems.at[a]) for a in range(1 + n_o)]
        first = []
        for ch in range(N_CHUNK):
            first.append(copy(0, 0, me, sibling, ch, src=srcs[0]))
            first += [copy(0, 1 + j, me, (*chip, c), ch, src=srcs[0]) for j, chip in enumerate(chips)]
        for a in range(1, 1 + n_o):
            first.append(copy(a, 0, me, sibling, src=srcs[a]))
            first += [copy(a, 1 + j, me, (*chip, c), src=srcs[a]) for j, chip in enumerate(chips)]
        for cp in mine + first:
            cp.start()

        def arrive(a, slot, ch=None):
            if slot == 0:
                copy(a, 0, sibling, me, ch).wait_recv()
            elif slot < 4:
                copy(a, slot, (*chips[slot - 1], c), me, ch).wait_recv()
                copy(a, slot + 3, (*chips[slot - 1], c), sibling, ch).start()
            else:
                copy(a, slot, (*chips[slot - 4], 1 - c), me, ch).wait_recv()

        def arrive_for(step):
            for slot in range(7):
                for ch in range(N_CHUNK):
                    @pl.when(tbl_ref[k, N_SLAB + slot * N_CHUNK + ch] == step)
                    def _():
                        arrive(0, slot, ch)

        def fetch(step, buf):
            slab = tbl_ref[k, step]
            for p in range(SLAB // PIECE):
                g0 = slab * SLAB + PIECE * p
                own = (g0 >= k * srows) & (g0 < (k + 1) * srows)
                dst = wtile.at[buf, pl.ds(PIECE * p, PIECE), :]

                @pl.when(own)
                def _():
                    pltpu.make_async_copy(srcs[0].at[pl.ds(pl.multiple_of(g0 - k * srows, PIECE), PIECE), :], dst, tile_sems.at[buf]).start()

                @pl.when(jnp.logical_not(own))
                def _():
                    pltpu.make_async_copy(w_full.at[pl.ds(pl.multiple_of(g0, PIECE), PIECE), :], dst, tile_sems.at[buf]).start()

        def wait_tile(buf):
            pltpu.make_async_copy(w_full.at[pl.ds(0, SLAB), :], wtile.at[buf], tile_sems.at[buf]).wait()

        def put(buf_ref, sems, dst_ref, count, value):
            b = count % 2

            @pl.when(count >= 2)
            def _():
                pltpu.make_async_copy(buf_ref.at[b], dst_ref, sems.at[b]).wait()

            buf_ref[b] = value
            pltpu.make_async_copy(buf_ref.at[b], dst_ref, sems.at[b]).start()

        def drain(buf_ref, sems, dst_ref, count):
            for back in (1, 2):
                @pl.when(count >= back)
                def _():
                    pltpu.make_async_copy(buf_ref.at[(count - back) % 2], dst_ref, sems.at[(count - back) % 2]).wait()

        def x_copy(i):
            return pltpu.make_async_copy(x_ref.at[pl.ds(xt * i, xt), :], xbuf.at[i % 2], x_sems.at[i % 2])

        def h_copy(i):
            return pltpu.make_async_copy(h_ref.at[pl.ds(xt * i, xt), :], h_out.at[pl.ds(xt * i, xt), :], h_sems.at[i % 2])

        x_copy(0).start()
        for i in range(t // xt):
            if i + 1 < t // xt:
                x_copy(i + 1).start()
            x_copy(i).wait()
            b = i // seq_tiles
            h_ref[pl.ds(xt * i, xt), :] = (xbuf[i % 2] * (1.0 + ada_ref[b, 1:2, :]) + ada_ref[b, 0:1, :]).astype(BF16)
            if i >= 2:
                h_copy(i - 2).wait()
            h_copy(i).start()
        for i in range(max(t // xt - 2, 0), t // xt):
            h_copy(i).wait()

        arrive_for(0)
        fetch(0, 0)

        def step(s, carry):
            n_bf, n_f32 = carry
            buf = s % 2

            @pl.when(s + 1 < N_SLAB)
            def _():
                arrive_for(s + 1)
                fetch(s + 1, 1 - buf)

            wait_tile(buf)
            slab = tbl_ref[k, s]
            v = _nt(h_ref[...], wtile[buf]) + b_ref[slab]
            is_qkv = slab < N_QKV

            @pl.when(is_qkv)
            def _():
                put(obf, obf_sems, qkv_ref.at[jnp.minimum(slab, N_QKV - 1)], n_bf, v.astype(BF16))

            @pl.when(jnp.logical_not(is_qkv))
            def _():
                put(of32, of32_sems, rest_ref.at[jnp.maximum(slab - N_QKV, 0)], n_f32, v)

            return n_bf + is_qkv.astype(jnp.int32), n_f32 + 1 - is_qkv.astype(jnp.int32)

        n_bf, n_f32 = lax.fori_loop(0, N_SLAB, step, (jnp.int32(0), jnp.int32(0)))
        drain(obf, obf_sems, qkv_ref.at[0], n_bf)
        drain(of32, of32_sems, rest_ref.at[0], n_f32)

        for slots in ((1, 2, 3), (0, 4, 5, 6)):
            for a in range(1, 1 + n_o):
                for slot in slots:
                    arrive(a, slot)
        for cp in first:
            cp.wait_send()
        for j, chip in enumerate(chips):
            for ch in range(N_CHUNK):
                copy(0, 4 + j, (*chip, c), sibling, ch).wait_send()
            for a in range(1, 1 + n_o):
                copy(a, 4 + j, (*chip, c), sibling).wait_send()
        for cp in mine:
            cp.wait()

    out_shape = ((jax.ShapeDtypeStruct((N_DEV * srows, D), BF16), jax.ShapeDtypeStruct((N_QKV, t, SLAB), BF16),
                  jax.ShapeDtypeStruct((N_REST, t, SLAB), F32), jax.ShapeDtypeStruct((t, D), BF16))
                 + tuple(jax.ShapeDtypeStruct((N_DEV * o.shape[0], o.shape[1]), o.dtype) for o in others))
    n_all = 1 + n_o
    n_sems = 7 * (N_CHUNK + n_o)
    pair = pltpu.SemaphoreType.DMA((2,))
    grid_spec = pltpu.PrefetchScalarGridSpec(
        num_scalar_prefetch=1, grid=(1,),
        in_specs=[ANY, ANY, pl.BlockSpec(ada.shape, lambda i, tbl: (0, 0, 0)),
                  pl.BlockSpec((N_SLAB, 1, SLAB), lambda i, tbl: (0, 0, 0))] + [ANY] * n_o,
        out_specs=(ANY,) * (4 + n_o),
        scratch_shapes=[pltpu.VMEM((2, SLAB, D), BF16), pltpu.VMEM((2, t, SLAB), BF16), pltpu.VMEM((2, t, SLAB), F32),
                        pltpu.VMEM((t, D), BF16), pltpu.VMEM((2, xt, D), F32),
                        pltpu.SemaphoreType.DMA((n_sems,)), pltpu.SemaphoreType.DMA((n_sems,)),
                        pltpu.SemaphoreType.DMA((n_all,)), pair, pair, pair, pair, pair])
    res = _pcall(body, grid_spec=grid_spec, out_shape=out_shape, name="project_gather",
                 compiler_params=_params(48, ("arbitrary",)))(table, shard, x, ada, b_in3, *others)
    return res[0], res[1], res[2], res[3], list(res[4:])


def _bias_tables(g):
    window, dil = GROUPS[g]
    span = window // dil
    qi = jnp.arange(BLK)[:, None]
    kj = jnp.arange(2 * BLK)[None, :]
    delta = qi + BLK - kj
    valid = (delta >= 0) & (delta <= span)
    heads = jnp.arange(4, dtype=F32) + 4.0 * g
    slopes = 2.0 ** (-8.0 * (heads + 1.0) / 12.0)
    bias = -slopes[:, None, None] * (delta * dil).astype(F32)[None]
    return jnp.where(valid[None], bias, -1e30).reshape(4 * BLK, 2 * BLK)


def _head_masks(shape):
    lane = lax.broadcasted_iota(jnp.int32, shape, 1)
    return [(lane >= 64 * h) & (lane < 64 * (h + 1)) for h in range(4)]


def _stack_heads(v, masks):
    return jnp.concatenate([jnp.where(masks[h], v, jnp.zeros_like(v)) for h in range(4)], axis=0)


def _unstack_heads(v4, masks):
    out = jnp.where(masks[0], v4[0:BLK], 0.0)
    for h in range(1, 4):
        out = jnp.where(masks[h], v4[BLK * h:BLK * (h + 1)], out)
    return out


def _regroup(load_half, dst_ref, stage_ref, n, dil):
    for hlf in range(2):
        stage_ref[hlf] = load_half(hlf)

    def residue(r, carry):
        for hlf in range(2):
            dst_ref[pl.ds(pl.multiple_of(r * n, BLK), n), pl.ds(128 * hlf, 128)] = (
                stage_ref[hlf, pl.ds(r, n, stride=dil), :].astype(dst_ref.dtype))
        return carry

    lax.fori_loop(0, dil, residue, 0)


def _store_block(nat_ref, r, i, val, dil):
    for hlf in range(2):
        nat_ref[hlf, pl.ds(r + dil * BLK * i, BLK, stride=dil), :] = val[:, 128 * hlf:128 * (hlf + 1)]


def _for_blocks(block, dil, nblk):
    if dil == 1:
        block(0, 0, True)
        block(0, 1, False)

        def pair(k, carry):
            block(0, 2 * k, False)
            block(0, 2 * k + 1, False)
            return carry

        lax.fori_loop(1, nblk // 2, pair, 0)
    else:
        def residues(k, carry):
            block(2 * k, 0, True)
            block(2 * k + 1, 0, True)
            if nblk > 1:
                def loop(i, c):
                    block(2 * k, i, False)
                    block(2 * k + 1, i, False)
                    return c
                lax.fori_loop(1, nblk, loop, 0)
            return carry

        lax.fori_loop(0, dil // 2, residues, 0)


def _attn_forward(qkv, nbat):
    t = qkv.shape[1]
    seq = t // nbat
    n_grp = len(GROUPS)

    def body(qkv_ref, b0_ref, b1_ref, b2_ref, ol_ref, stage, qs_ref, ks_ref, vs_ref, *nat):
        masks = _head_masks((BLK, SLAB))
        bias_refs = (b0_ref, b1_ref, b2_ref)
        for g, (_, dil) in enumerate(GROUPS):
            n = seq // dil
            bias_ref, nat_o, nat_l = bias_refs[g], nat[2 * g], nat[2 * g + 1]
            if dil > 1:
                qd, kd, vd = qs_ref, ks_ref, vs_ref
                for which, dst in enumerate((qd, kd, vd)):
                    _regroup(lambda hlf, which=which, g=g: qkv_ref[3 * which + g, :, pl.ds(128 * hlf, 128)].astype(F32), dst, stage, n, dil)
            else:
                qd, kd, vd = qkv_ref.at[g], qkv_ref.at[3 + g], qkv_ref.at[6 + g]

            def block(r, i, first, n=n, dil=dil, qd=qd, kd=kd, vd=vd, bias_ref=bias_ref, nat_o=nat_o, nat_l=nat_l):
                base = r * n
                qs = pl.ds(pl.multiple_of(base + i * BLK, BLK), BLK)
                ks = pl.ds(pl.multiple_of(base, BLK), BLK) if first else pl.ds(pl.multiple_of(base + (i - 1) * BLK, BLK), 2 * BLK)
                q, kk, vv = qd[qs, :], kd[ks, :], vd[ks, :]
                bias = bias_ref[:, pl.ds(BLK, BLK)] if first else bias_ref[...]
                s = _nt(_stack_heads(q, masks), kk) * 0.125 + bias
                m = jnp.max(s, axis=1, keepdims=True)
                p = jnp.exp(s - m)
                den = jnp.sum(p, axis=1, keepdims=True)
                out = _unstack_heads(_nn((p * (1.0 / den)).astype(BF16), vv), masks)
                lse = _unstack_heads(jnp.broadcast_to(m + jnp.log(den), (4 * BLK, SLAB)), masks)
                _store_block(nat_o, r, i, out, dil)
                _store_block(nat_l, r, i, lse, dil)

            _for_blocks(block, dil, n // BLK)

        for hlf in range(2):
            l0, l1, l2 = nat[1][hlf], nat[3][hlf], nat[5][hlf]
            mx = jnp.maximum(jnp.maximum(l0, l1), l2)
            e0, e1, e2 = jnp.exp(l0 - mx), jnp.exp(l1 - mx), jnp.exp(l2 - mx)
            den = e0 + e1 + e2
            ol_ref[0, :, pl.ds(128 * hlf, 128)] = (e0 * nat[0][hlf] + e1 * nat[2][hlf] + e2 * nat[4][hlf]) * (1.0 / den)
            ol_ref[1, :, pl.ds(128 * hlf, 128)] = mx + jnp.log(den)

    halves = pltpu.VMEM((2, seq, 128), F32)
    bias_spec = pl.BlockSpec((4 * BLK, 2 * BLK), lambda b: (0, 0))
    return _pcall(
        body, grid=(nbat,), out_shape=jax.ShapeDtypeStruct((2, t, SLAB), F32),
        in_specs=[pl.BlockSpec((N_QKV, seq, SLAB), lambda b: (0, b, 0))] + [bias_spec] * n_grp,
        out_specs=pl.BlockSpec((2, seq, SLAB), lambda b: (0, b, 0)),
        scratch_shapes=[halves] + [pltpu.VMEM((seq, SLAB), BF16)] * 3 + [halves] * (2 * n_grp),
        name="attn_forward", compiler_params=_params(56, ("parallel",)))(qkv, *[_bias_tables(g) for g in range(n_grp)])


def _attn_backward(qkv, do_attn, ol_tot, dproj, g, nbat):
    t = qkv.shape[1]
    seq = t // nbat
    dil = GROUPS[g][1]
    n = seq // dil
    nblk = n // BLK
    qkv4 = qkv.reshape(3, 3, t, SLAB)
    dp4 = dproj.reshape(DP_SLABS // 3, 3, t, SLAB)

    def body(qkv_ref, do_ref, ol_ref, bias_ref, dp_in, dp_ref, gb_ref, dk_acc, dv_acc, *scratch):
        del dp_in
        masks = _head_masks((BLK, SLAB))

        @pl.when(pl.program_id(0) == 0)
        def _():
            gb_ref[...] = jnp.zeros_like(gb_ref)

        dk_acc[...] = jnp.zeros_like(dk_acc)
        dv_acc[...] = jnp.zeros_like(dv_acc)
        if dil > 1:
            stage, qd, kd, vd, dod, prodd, lsed, nat = scratch
            lanes = lambda hlf: pl.ds(128 * hlf, 128)
            for which, dst in enumerate((qd, kd, vd)):
                _regroup(lambda hlf, which=which: qkv_ref[which, 0, :, lanes(hlf)].astype(F32), dst, stage, n, dil)
            _regroup(lambda hlf: do_ref[:, lanes(hlf)].astype(F32), dod, stage, n, dil)
            _regroup(lambda hlf: do_ref[:, lanes(hlf)].astype(F32) * ol_ref[0, :, lanes(hlf)], prodd, stage, n, dil)
            _regroup(lambda hlf: ol_ref[1, :, lanes(hlf)], lsed, stage, n, dil)
        else:
            qd, kd, vd = qkv_ref.at[0, 0], qkv_ref.at[1, 0], qkv_ref.at[2, 0]

        def block(r, i, first):
            base = r * n
            qs = pl.ds(pl.multiple_of(base + i * BLK, BLK), BLK)
            ks = pl.ds(pl.multiple_of(base, BLK), BLK) if first else pl.ds(pl.multiple_of(base + (i - 1) * BLK, BLK), 2 * BLK)
            q, kk, vv = qd[qs, :], kd[ks, :], vd[ks, :]
            if dil > 1:
                do, prod, lse = dod[qs, :], prodd[qs, :], lsed[qs, :]
            else:
                do = do_ref[qs, :]
                prod = do.astype(F32) * ol_ref[0, qs, :]
                lse = ol_ref[1, qs, :]
            q4, do4 = _stack_heads(q, masks), _stack_heads(do, masks)
            bias = bias_ref[:, pl.ds(BLK, BLK)] if first else bias_ref[...]
            lse4 = jnp.concatenate([lse[:, 64 * h:64 * h + 1] for h in range(4)], axis=0)
            delta4 = jnp.concatenate([jnp.sum(jnp.where(masks[h], prod, 0.0), axis=1, keepdims=True) for h in range(4)], axis=0)
            p = jnp.exp(_nt(q4, kk) * 0.125 + bias - lse4)
            ds = (p * (_nt(do4, vv) - delta4)).astype(BF16)
            dv_acc[ks, :] += _tn(p.astype(BF16), do4)
            dk_acc[ks, :] += _tn(ds, q4) * 0.125
            dq = _unstack_heads(_nn(ds, kk), masks) * 0.125
            if dil > 1:
                _store_block(nat, r, i, dq, dil)
            else:
                dp_ref[0, 0, qs, :] = dq.astype(BF16)
            gb_ref[0] += _part8(dq)

        _for_blocks(block, dil, nblk)
        gb_ref[1] += _part8(dk_acc[...])
        gb_ref[2] += _part8(dv_acc[...])
        if dil > 1:
            def flush(which):
                for hlf in range(2):
                    dp_ref[which, 0, :, pl.ds(128 * hlf, 128)] = nat[hlf].astype(BF16)

            def to_token_order(acc_ref):
                def residue(r, carry):
                    for hlf in range(2):
                        nat[hlf, pl.ds(r, n, stride=dil), :] = acc_ref[pl.ds(pl.multiple_of(r * n, BLK), n), pl.ds(128 * hlf, 128)]
                    return carry
                lax.fori_loop(0, dil, residue, 0)

            flush(0)
            to_token_order(dk_acc)
            flush(1)
            to_token_order(dv_acc)
            flush(2)
        else:
            dp_ref[1, 0] = dk_acc[...].astype(BF16)
            dp_ref[2, 0] = dv_acc[...].astype(BF16)

    scratch = [pltpu.VMEM((seq, SLAB), F32)] * 2
    if dil > 1:
        scratch += ([pltpu.VMEM((2, seq, 128), F32)] + [pltpu.VMEM((seq, SLAB), BF16)] * 4 + [pltpu.VMEM((seq, SLAB), F32)] * 2
                    + [pltpu.VMEM((2, seq, 128), F32)])
    dp, gb = _pcall(
        body, grid=(nbat,),
        out_shape=(jax.ShapeDtypeStruct(dp4.shape, BF16), jax.ShapeDtypeStruct((3, 8, SLAB), F32)),
        in_specs=[pl.BlockSpec((3, 1, seq, SLAB), lambda b: (0, g, b, 0)),
                  pl.BlockSpec((seq, SLAB), lambda b: (b, 0)),
                  pl.BlockSpec((2, seq, SLAB), lambda b: (0, b, 0)),
                  pl.BlockSpec((4 * BLK, 2 * BLK), lambda b: (0, 0)), ANY],
        out_specs=(pl.BlockSpec((3, 1, seq, SLAB), lambda b: (DP_SLABS // 9 - 1, g, b, 0)),
                   pl.BlockSpec((3, 8, SLAB), lambda b: (0, 0, 0))),
        scratch_shapes=scratch, input_output_aliases={4: 0}, name=f"attn_backward_{g}",
        compiler_params=_params(48, ("arbitrary",)))(qkv4, do_attn, ol_tot, _bias_tables(g), dp4)
    return dp.reshape(DP_SLABS, t, SLAB), gb


def _mid(rest, ol_tot, x, tgt, ada, cw, b_out, ln_g, ln_b, w_pa_t, w_pb, w_out, tm=256):
    t = x.shape[0]
    nbat = ada.shape[0]
    nt = t // tm
    tps = nt // nbat

    def body(rest_ref, halo_ref, ol_ref, x_ref, t_ref, ada_ref, cw_ref, bout_ref, lng_ref, lnb_ref,
             wpat_ref, wpb_ref, wout_ref,
             dp_ref, gx0_ref, doa_ref, mg_ref, dof_ref, bbs_ref, dyc_ref, a_ref, dya_ref,
             gbr_ref, sv_ref, dgate_ref, carry_ref, keep_ref):
        i = pl.program_id(0)
        ti = nt - 1 - i
        pos = ti % tps

        @pl.when(i == 0)
        def _():
            gbr_ref[...] = jnp.zeros_like(gbr_ref)
            sv_ref[...] = jnp.zeros_like(sv_ref)

        @pl.when(pos == tps - 1)
        def _():
            dgate_ref[...] = jnp.zeros_like(dgate_ref)
            carry_ref[...] = jnp.zeros_like(carry_ref)

        row = lax.broadcasted_iota(jnp.int32, (tm, SLAB), 0)
        halo_on = (pos > 0).astype(F32)

        def cols(s):
            return pl.ds(SLAB * s, SLAB)

        o_attn = ol_ref[0]
        z_a = rest_ref[R_ZA]
        sg_za = _sigmoid(z_a)
        a_ref[...] = (o_attn * z_a * sg_za).astype(BF16)
        y_attn = _nt(a_ref[...], wpat_ref[...])

        for s in range(4):
            u = rest_ref[R_GC + s] * rest_ref[R_UX + s]
            hu = halo_ref[R_GC + s] * halo_ref[R_UX + s] * halo_on
            u1 = jnp.where(row == 0, hu[7:8], pltpu.roll(u, 1, 0))
            u2 = jnp.where(row == 0, hu[6:7], jnp.where(row == 1, hu[7:8], pltpu.roll(u, 2, 0)))
            conv = cw_ref[0:1, cols(s)] * u2 + cw_ref[1:2, cols(s)] * u1 + cw_ref[2:3, cols(s)] * u
            zc = rest_ref[R_ZC + s]
            sg = _sigmoid(zc)
            keep_ref[2, :, cols(s)], keep_ref[3, :, cols(s)], keep_ref[4, :, cols(s)], keep_ref[5, :, cols(s)] = u1, u2, conv, sg
            bbs_ref[:, cols(s)] = (rest_ref[R_GB + s] * conv * (zc * sg)).astype(BF16)
        y_conv = _nn(bbs_ref[...], wpb_ref[...])

        for s in range(4):
            s_a, s_b = _sigmoid(rest_ref[R_GA + s]), _sigmoid(rest_ref[R_GBM + s])
            keep_ref[0, :, cols(s)], keep_ref[1, :, cols(s)] = s_a, s_b
            mg_ref[:, cols(s)] = (s_a * y_attn[:, SLAB * s:SLAB * (s + 1)] + s_b * y_conv[:, SLAB * s:SLAB * (s + 1)]).astype(BF16)
        o = _nn(mg_ref[...], wout_ref[...]) + bout_ref[...]
        gate = ada_ref[0, 2:3, :]
        r = ALPHA * x_ref[...] + gate * o
        mu = jnp.mean(r, axis=1, keepdims=True)
        rc = r - mu
        rstd = lax.rsqrt(jnp.mean(rc * rc, axis=1, keepdims=True) + LN_EPS)
        xhat = rc * rstd
        err = xhat * lng_ref[...] + lnb_ref[...] - t_ref[...]
        sv_ref[6] += _part8(err * err)
        dy = err * (1.0 / D)
        sv_ref[0] += _part8(dy * xhat)
        sv_ref[1] += _part8(dy)
        dxh = dy * lng_ref[...]
        dr = rstd * (dxh - jnp.mean(dxh, axis=1, keepdims=True) - xhat * jnp.mean(dxh * xhat, axis=1, keepdims=True))
        gx0_ref[...] = ALPHA * dr
        dgate_ref[0] += _part8(dr * o)
        do_ = dr * gate
        sv_ref[2] += _part8(do_)
        dof_ref[...] = do_.astype(BF16)
        dmerged = _nt(dof_ref[...], wout_ref[...])
        for s in range(4):
            s_a, s_b = keep_ref[0, :, cols(s)], keep_ref[1, :, cols(s)]
            dm = dmerged[:, SLAB * s:SLAB * (s + 1)]
            ya, yc = y_attn[:, SLAB * s:SLAB * (s + 1)], y_conv[:, SLAB * s:SLAB * (s + 1)]
            dya_ref[:, cols(s)] = (dm * s_a).astype(BF16)
            dyc_ref[:, cols(s)] = (dm * s_b).astype(BF16)
            dga = dm * ya * s_a * (1.0 - s_a)
            dgb = dm * yc * s_b * (1.0 - s_b)
            dp_ref[R_GA + s] = dga.astype(BF16)
            dp_ref[R_GBM + s] = dgb.astype(BF16)
            gbr_ref[R_GA + s] += _part8(dga)
            gbr_ref[R_GBM + s] += _part8(dgb)

        da = _nn(dya_ref[...], wpat_ref[...])
        doa_ref[...] = (da * z_a * sg_za).astype(BF16)
        dza = da * o_attn * (sg_za * (1.0 + z_a * (1.0 - sg_za)))
        dp_ref[R_ZA] = dza.astype(BF16)
        gbr_ref[R_ZA] += _part8(dza)

        dbb = _nt(dyc_ref[...], wpb_ref[...])
        for s in range(4):
            ux, gc, zc = rest_ref[R_UX + s], rest_ref[R_GC + s], rest_ref[R_ZC + s]
            u = gc * ux
            u1, u2, conv, sg = keep_ref[2, :, cols(s)], keep_ref[3, :, cols(s)], keep_ref[4, :, cols(s)], keep_ref[5, :, cols(s)]
            gb = rest_ref[R_GB + s]
            d_b = dbb[:, SLAB * s:SLAB * (s + 1)]
            szc = zc * sg
            dgb_ = d_b * conv * szc
            dconv = d_b * gb * szc
            dzc = d_b * gb * conv * (sg * (1.0 + zc * (1.0 - sg)))
            sv_ref[3, :, cols(s)] += _part8(dconv * u2)
            sv_ref[4, :, cols(s)] += _part8(dconv * u1)
            sv_ref[5, :, cols(s)] += _part8(dconv * u)
            nxt = carry_ref[:, cols(s)]
            d1 = jnp.where(row == tm - 1, nxt[0:1], pltpu.roll(dconv, tm - 1, 0))
            d2 = jnp.where(row == tm - 1, nxt[1:2], jnp.where(row == tm - 2, nxt[0:1], pltpu.roll(dconv, tm - 2, 0)))
            carry_ref[:, cols(s)] = dconv[0:8]
            du = cw_ref[2:3, cols(s)] * dconv + cw_ref[1:2, cols(s)] * d1 + cw_ref[0:1, cols(s)] * d2
            dgc, dux = du * ux, du * gc
            for slab, val in ((R_GB + s, dgb_), (R_ZC + s, dzc), (R_GC + s, dgc), (R_UX + s, dux)):
                dp_ref[slab] = val.astype(BF16)
                gbr_ref[slab] += _part8(val)

    def tile(i):
        return nt - 1 - i

    row_blk = lambda i: (tile(i), 0)
    slab_blk = lambda i: (0, tile(i), 0)
    const2 = lambda i: (0, 0)
    const3 = lambda i: (0, 0, 0)
    in_specs = [
        pl.BlockSpec((N_REST, tm, SLAB), slab_blk),
        pl.BlockSpec((N_REST, 8, SLAB), lambda i: (0, jnp.maximum(tile(i) * (tm // 8) - 1, 0), 0)),
        pl.BlockSpec((1, tm, SLAB), slab_blk),
        pl.BlockSpec((tm, D), row_blk), pl.BlockSpec((tm, D), row_blk),
        pl.BlockSpec((1, 3, D), lambda i: (tile(i) // tps, 0, 0)),
        pl.BlockSpec((3, D), const2), pl.BlockSpec((1, D), const2), pl.BlockSpec((1, D), const2), pl.BlockSpec((1, D), const2),
        pl.BlockSpec((D, SLAB), const2), pl.BlockSpec((D, D), const2), pl.BlockSpec((D, D), const2)]
    bf_rows = lambda: jax.ShapeDtypeStruct((t, D), BF16)
    out_shape = (
        jax.ShapeDtypeStruct((DP_SLABS, t, SLAB), BF16), jax.ShapeDtypeStruct((t, D), F32),
        jax.ShapeDtypeStruct((t, SLAB), BF16),
        bf_rows(), bf_rows(), bf_rows(), bf_rows(), jax.ShapeDtypeStruct((t, SLAB), BF16), bf_rows(),
        jax.ShapeDtypeStruct((N_REST, 8, SLAB), F32), jax.ShapeDtypeStruct((7, 8, D), F32),
        jax.ShapeDtypeStruct((nbat, 8, D), F32))
    out_specs = (
        pl.BlockSpec((N_REST, tm, SLAB), slab_blk), pl.BlockSpec((tm, D), row_blk),
        pl.BlockSpec((tm, SLAB), row_blk),
        pl.BlockSpec((tm, D), row_blk), pl.BlockSpec((tm, D), row_blk), pl.BlockSpec((tm, D), row_blk),
        pl.BlockSpec((tm, D), row_blk), pl.BlockSpec((tm, SLAB), row_blk), pl.BlockSpec((tm, D), row_blk),
        pl.BlockSpec((N_REST, 8, SLAB), const3), pl.BlockSpec((7, 8, D), const3),
        pl.BlockSpec((1, 8, D), lambda i: (tile(i) // tps, 0, 0)))
    return _pcall(body, grid=(nt,), out_shape=out_shape, in_specs=in_specs, out_specs=out_specs,
                  scratch_shapes=[pltpu.VMEM((8, D), F32), pltpu.VMEM((6, tm, D), F32)], name="mid",
                  compiler_params=_params(56, ("arbitrary",)))(
        rest, rest, ol_tot, x, tgt, ada, cw, b_out, ln_g, ln_b, w_pa_t, w_pb, w_out)


def _tn_matmul(lhs, rhs, lhs_spec, n_steps, out_rows, out_index, name, after):
    t, n = rhs.shape

    def body(l_ref, r_ref, after_ref, o_ref):
        del after_ref
        o_ref[...] = _tn(l_ref[0] if len(l_ref.shape) == 3 else l_ref[...], r_ref[...])

    return _pcall(body, grid=(n_steps,), out_shape=jax.ShapeDtypeStruct((out_rows, n), F32),
                  in_specs=[lhs_spec, pl.BlockSpec((t, n), lambda j: (0, 0)), ANY],
                  out_specs=pl.BlockSpec((SLAB, n), out_index), name=name,
                  compiler_params=_params(48, ("parallel",)))(lhs, rhs, after)


def _grad_rows_2d(lhs, rhs, name, after):
    t, k = lhs.shape
    return _tn_matmul(lhs, rhs, pl.BlockSpec((t, SLAB), lambda j: (0, j)), k // SLAB, k, lambda j: (j, 0), name, after)


def _w_row_block(j):
    return (j + N_QKV) % N_SLAB


def _dp_slab(j):
    return jnp.where(j < N_REST, j, j + 2)


def _grad_w_in_t(dproj, h):
    t = h.shape[0]
    return _tn_matmul(dproj, h, pl.BlockSpec((1, t, SLAB), lambda j: (_dp_slab(j), 0, 0)), N_SLAB, D_IN,
                      lambda j: (_w_row_block(j), 0), "grad_w_in", h)


def _grad_h(dproj, w_in_t, gx0, x, ada, after, tm=512):
    t = x.shape[0]
    nbat = ada.shape[0]
    tps = (t // nbat) // tm

    def body(dp_ref, w_ref, gx0_ref, x_ref, ada_ref, after_ref, gx_ref, dss_ref):
        del after_ref
        i = pl.program_id(0)
        dh = None
        for j in range(N_SLAB):
            slab = j if j < N_REST else j + 2
            part = _nn(dp_ref[slab], w_ref[pl.ds(SLAB * ((j + N_QKV) % N_SLAB), SLAB), :])
            dh = part if dh is None else dh + part
        gx_ref[...] = gx0_ref[...] + dh * (1.0 + ada_ref[0, 1:2, :])

        @pl.when((i % tps) == 0)
        def _():
            dss_ref[...] = jnp.zeros_like(dss_ref)

        dss_ref[0, 0] += _part8(dh)
        dss_ref[0, 1] += _part8(dh * x_ref[...])

    return _pcall(
        body, grid=(t // tm,),
        out_shape=(jax.ShapeDtypeStruct((t, D), F32), jax.ShapeDtypeStruct((nbat, 2, 8, D), F32)),
        in_specs=[pl.BlockSpec((DP_SLABS, tm, SLAB), lambda i: (0, i, 0)),
                  pl.BlockSpec((D_IN, D), lambda i: (0, 0), pipeline_mode=pl.Buffered(1)),
                  pl.BlockSpec((tm, D), lambda i: (i, 0)), pl.BlockSpec((tm, D), lambda i: (i, 0)),
                  pl.BlockSpec((1, 3, D), lambda i: (i // tps, 0, 0)), ANY],
        out_specs=(pl.BlockSpec((tm, D), lambda i: (i, 0)),
                   pl.BlockSpec((1, 2, 8, D), lambda i: (i // tps, 0, 0, 0))),
        name="grad_h", compiler_params=_params(60, ("arbitrary",)))(dproj, w_in_t, gx0, x, ada, after)


def _chip(m):
    x, y, _ = _my_position()
    return (x ^ ((m >> 1) & 1), y ^ (m & 1))


def _exchange_siblings(grads, after, name):
    n = len(grads)

    def body(*refs):
        copies = _sibling_copies(refs[:n], refs[n + 1:2 * n + 1], refs[2 * n + 1], refs[2 * n + 2])
        for cp in copies:
            cp.start()
        for cp in copies:
            cp.wait()

    return _pcall(body, out_shape=tuple(_sibling_zones(grads)), in_specs=[ANY] * (n + 1), out_specs=(ANY,) * n,
                  name=name, scratch_shapes=[pltpu.SemaphoreType.DMA((4 * n,))] * 2)(*grads, after)


def _sibling_zones(grads):
    return [jax.ShapeDtypeStruct((4, g.shape[0] // N_DEV, g.shape[1]), g.dtype) for g in grads]


def _sibling_copies(srcs, lands, send_sems, recv_sems):
    x, y, c = _my_position()
    copies = []
    for a, (src, land) in enumerate(zip(srcs, lands)):
        rows = land.shape[1]
        for m in range(4):
            dev = _flat(*_chip(m), 1 - c)
            copies.append(pltpu.make_async_remote_copy(
                src_ref=src.at[pl.ds(pl.multiple_of(dev * rows, 8), rows), :], dst_ref=land.at[m],
                send_sem=send_sems.at[4 * a + m], recv_sem=recv_sems.at[4 * a + m], device_id=(x, y, 1 - c),
                device_id_type=MESH))
    return copies


def _chip_copies(srcs, lands, send_sems, recv_sems):
    _, _, c = _my_position()
    return [pltpu.make_async_remote_copy(
        src_ref=srcs[a].at[m - 1], dst_ref=lands[a].at[m - 1], send_sem=send_sems.at[3 * a + m - 1],
        recv_sem=recv_sems.at[3 * a + m - 1], device_id=(*_chip(m), c), device_id_type=MESH)
        for a in range(len(srcs)) for m in range(1, 4)]


HBM = pl.BlockSpec(memory_space=pltpu.HBM)
SEM = pl.BlockSpec(memory_space=pltpu.SEMAPHORE)
SPLIT_COPY = pltpu.CompilerParams(has_side_effects=pltpu.SideEffectType.DATAFLOW_SIDE_EFFECTING)


def _start_copies(make_copies, n_sems, srcs, zones, name):
    n = len(srcs)

    def body(*refs):
        for cp in make_copies(refs[:n], refs[n:2 * n], refs[2 * n], refs[2 * n + 1]):
            cp.start()
        refs[-1][...] = jnp.zeros_like(refs[-1])

    hbm = tuple(pltpu.HBM(b.shape, b.dtype) for b in list(srcs) + list(zones))
    out_shape = (pltpu.SemaphoreType.DMA((n_sems,)), pltpu.SemaphoreType.DMA((n_sems,))) + hbm + (jax.ShapeDtypeStruct((8, 128), F32),)
    operands = [pltpu.with_memory_space_constraint(b, pltpu.HBM) for b in srcs]
    operands += [pltpu.with_memory_space_constraint(lax.empty(z.shape, z.dtype), pltpu.HBM) for z in zones]
    res = _pcall(body, out_shape=out_shape, in_specs=[HBM] * (2 * n), out_specs=(SEM, SEM) + (HBM,) * (2 * n) + (VMEM,),
                 input_output_aliases={i: 2 + i for i in range(2 * n)}, name=name, compiler_params=SPLIT_COPY)(*operands)
    return (res[0], res[1], res[2:2 + n], res[2 + n:2 + 2 * n]), res[-1]


def _wait_copies(make_copies, flight, after, name):
    send_sems, recv_sems, srcs, zones = flight
    n = len(srcs)

    def body(*refs):
        for cp in make_copies(refs[:n], refs[n:2 * n], refs[2 * n], refs[2 * n + 1]):
            cp.wait_send()
            cp.wait_recv()

    hbm = tuple(pltpu.HBM(b.shape, b.dtype) for b in list(srcs) + list(zones))
    res = _pcall(body, out_shape=hbm, in_specs=[HBM] * (2 * n) + [SEM, SEM, ANY], out_specs=(HBM,) * (2 * n),
                 input_output_aliases={i: i for i in range(2 * n)}, name=name, compiler_params=SPLIT_COPY)(
        *srcs, *zones, send_sems, recv_sems, after)
    return res[:n], res[n:]


def _pair_sums(devs, grads, lands, n_steps, name):
    n = len(grads)
    rows = [l.shape[1] for l in lands]
    rbs = [r // n_steps for r in rows]

    def body(devs_ref, *refs):
        del devs_ref
        g_refs, land_refs, outs = refs[:4 * n], refs[4 * n:5 * n], refs[5 * n:]
        for a in range(n):
            outs[2 * a][...] = g_refs[4 * a][...] + land_refs[a][0]
            for m in range(1, 4):
                outs[2 * a + 1][m - 1] = (g_refs[4 * a + m][...] + land_refs[a][m]).astype(BF16)

    def block_of(m, per_dev):
        return lambda i, devs_ref: (devs_ref[m] * per_dev + i, 0)

    in_specs = [pl.BlockSpec((rb, l.shape[2]), block_of(m, n_steps)) for rb, l in zip(rbs, lands) for m in range(4)]
    in_specs += [pl.BlockSpec((4, rb, l.shape[2]), lambda i, devs_ref: (0, i, 0)) for rb, l in zip(rbs, lands)]
    out_shape, out_specs = [], []
    for rb, l in zip(rbs, lands):
        out_shape += [jax.ShapeDtypeStruct(l.shape[1:], F32), jax.ShapeDtypeStruct((3,) + l.shape[1:], BF16)]
        out_specs += [pl.BlockSpec((rb, l.shape[2]), lambda i, devs_ref: (i, 0)),
                      pl.BlockSpec((3, rb, l.shape[2]), lambda i, devs_ref: (0, i, 0))]
    grid_spec = pltpu.PrefetchScalarGridSpec(num_scalar_prefetch=1, grid=(n_steps,), in_specs=in_specs, out_specs=tuple(out_specs))
    res = _pcall(body, grid_spec=grid_spec, out_shape=tuple(out_shape), name=name,
                 compiler_params=_params(48, ("parallel",)))(devs, *[g for g in grads for _ in range(4)], *lands)
    return res[0::2], res[1::2]


def _final_sums(mine, lands, n_steps, name):
    n = len(mine)
    rbs = [o.shape[0] // n_steps for o in mine]

    def body(*refs):
        mine_refs, land_refs, outs = refs[:n], refs[n:2 * n], refs[2 * n:]
        for a in range(n):
            tot = mine_refs[a][...]
            for m in range(3):
                tot = tot + land_refs[a][m].astype(F32)
            outs[a][...] = tot

    in_specs = ([pl.BlockSpec((rb, o.shape[1]), lambda i: (i, 0)) for rb, o in zip(rbs, mine)]
                + [pl.BlockSpec((3, rb, o.shape[1]), lambda i: (0, i, 0)) for rb, o in zip(rbs, mine)])
    out_specs = tuple(pl.BlockSpec((rb, o.shape[1]), lambda i: (i, 0)) for rb, o in zip(rbs, mine))
    out_shape = tuple(jax.ShapeDtypeStruct(o.shape, F32) for o in mine)
    return _pcall(body, grid=(n_steps,), out_shape=out_shape, in_specs=in_specs, out_specs=out_specs, name=name,
                  compiler_params=_params(32, ("parallel",)))(*mine, *lands)


def _reduce_scatter_begin(big, small_after_start):
    c = lax.axis_index("c")
    devs = jnp.stack([_flat(*_chip(m), c) for m in range(4)]).astype(jnp.int32)
    flight, token = _start_copies(_sibling_copies, 4, [big], _sibling_zones([big]), "siblings_start")
    small = small_after_start(token)
    (big,), big_lands = _wait_copies(_sibling_copies, flight, small[-1], "siblings_wait")
    big_mine, big_send = _pair_sums(devs, [big], big_lands, 4, "pair_sums_w_in")
    big_flight, token = _start_copies(_chip_copies, 3, list(big_send), list(big_send), "chips_start_w_in")
    small_lands = _exchange_siblings(small, token, "exchange_siblings_rest")
    small_mine, small_send = _pair_sums(devs, small, small_lands, 1, "pair_sums_rest")
    small_flight, token = _start_copies(_chip_copies, 3 * len(small), list(small_send), list(small_send), "chips_start_rest")
    return (big_flight, small_flight, list(big_mine) + list(small_mine)), token


def _reduce_scatter_end(state, after):
    big_flight, small_flight, mine = state
    _, big_got = _wait_copies(_chip_copies, big_flight, after, "chips_wait_w_in")
    _, small_got = _wait_copies(_chip_copies, small_flight, after, "chips_wait_rest")
    small = _final_sums(mine[1:], small_got, 1, "final_sums_rest")
    return (mine[0], big_got[0]), list(small)


def _adamw(w, g, m, v):
    m_new = B1 * m + (1.0 - B1) * g
    v_new = B2 * v + (1.0 - B2) * (g * g)
    m_hat = m_new / (1.0 - B1 ** STEP)
    v_hat = v_new / (1.0 - B2 ** STEP)
    delta = -LR * (m_hat / (jnp.sqrt(v_hat) + EPS) + WD * w)
    return delta, m_new, v_new


def _final_sum_adam_rows(mine, land, w, m, v, n_steps, name):
    rows, ncol = w.shape
    blk = pl.BlockSpec((rows // n_steps, ncol), lambda i: (i, 0))

    def body(mine_ref, land_ref, w_ref, m_ref, v_ref, g_ref, d_ref, mo_ref, vo_ref):
        g = mine_ref[...]
        for k in range(3):
            g = g + land_ref[k].astype(F32)
        g_ref[...] = g
        d_ref[...], mo_ref[...], vo_ref[...] = _adamw(w_ref[...], g, m_ref[...], v_ref[...])

    shape = jax.ShapeDtypeStruct(w.shape, F32)
    return _pcall(body, grid=(n_steps,), out_shape=(shape,) * 4,
                  in_specs=[blk, pl.BlockSpec((3, rows // n_steps, ncol), lambda i: (0, i, 0)), blk, blk, blk],
                  out_specs=(blk,) * 4, name=name, compiler_params=_params(32, ("parallel",)))(mine, land, w, m, v)


def _adam_transposed(g_t, w, m, v, name):
    n, k = g_t.shape
    rb = min(k, 128)

    def body(gt_ref, w_ref, m_ref, v_ref, g_ref, d_ref, mo_ref, vo_ref):
        for src, skip, dst, size in _column_chunks(n):
            sl = pl.ds(dst, size)
            g = gt_ref[pl.ds(src, 128), :].T[:, skip:]
            delta, m_new, v_new = _adamw(w_ref[:, sl], g, m_ref[:, sl], v_ref[:, sl])
            g_ref[:, sl], d_ref[:, sl], mo_ref[:, sl], vo_ref[:, sl] = g, delta, m_new, v_new

    shape = jax.ShapeDtypeStruct(w.shape, F32)
    rows = pl.BlockSpec((rb, n), lambda i: (i, 0))
    return _pcall(body, grid=(k // rb,), out_shape=(shape,) * 4,
                  in_specs=[pl.BlockSpec((n, rb), lambda i: (0, i)), rows, rows, rows], out_specs=(rows,) * 4, name=name,
                  compiler_params=_params(32, ("parallel",)))(g_t, w, m, v)


def _adam_many(items, name):
    n = len(items)

    def body(*refs):
        ins, outs = refs[:4 * n], refs[4 * n:]
        for a in range(n):
            w_ref, g_ref, m_ref, v_ref = ins[4 * a:4 * a + 4]
            delta, m_new, v_new = _adamw(w_ref[...], g_ref[...], m_ref[...], v_ref[...])
            outs[3 * a][...], outs[3 * a + 1][...], outs[3 * a + 2][...] = delta, m_new, v_new

    out_shape = tuple(jax.ShapeDtypeStruct(it[0].shape, F32) for it in items for _ in range(3))
    flat = [arr for it in items for arr in it]
    res = _pcall(body, grid=(1,), out_shape=out_shape, in_specs=[_whole(a) for a in flat],
                 out_specs=tuple(_whole(o) for o in out_shape), name=name, compiler_params=_params(32))(*flat)
    return [tuple(res[3 * a:3 * a + 3]) for a in range(n)]


def _adam_w_ada(cact_all, dada_mine, w, m, v):
    def body(c_ref, d_ref, w_ref, m_ref, v_ref, g_ref, dl_ref, mo_ref, vo_ref):
        g = _tn(c_ref[...].astype(BF16), d_ref[...].astype(BF16))
        delta, m_new, v_new = _adamw(w_ref[...], g, m_ref[...], v_ref[...])
        g_ref[...], dl_ref[...], mo_ref[...], vo_ref[...] = g, delta, m_new, v_new

    shape = jax.ShapeDtypeStruct(w.shape, F32)
    operands = (cact_all, dada_mine, w, m, v)
    return _pcall(body, grid=(1,), out_shape=(shape,) * 4, in_specs=[_whole(a) for a in operands],
                  out_specs=(_whole(w),) * 4, name="adam_w_ada", compiler_params=_params(32))(*operands)


def kernel(x, c, w_ada, b_ada, w_in, b_in, conv_w, w_proj_attn, w_proj_conv, w_out, b_out, ln_g, ln_b, loss_target, m_w_ada, m_b_ada, m_w_in, m_b_in, m_conv_w, m_w_proj_attn, m_w_proj_conv, m_w_out, m_b_out, m_ln_g, m_ln_b, v_w_ada, v_b_ada, v_w_in, v_b_in, v_conv_w, v_w_proj_attn, v_w_proj_conv, v_w_out, v_b_out, v_ln_g, v_ln_b):
    nbat, seq, _ = x.shape
    t = nbat * seq
    me = _flat(*_my_position())
    x2, tgt2 = x.reshape(t, D), loss_target.reshape(t, D)
    sq = lambda a: a.reshape(a.shape[1:])

    tr = lambda a: a[0].T
    w_in_rows = tr(w_in)
    w_in_t_s = _cast_rows(w_in_rows, 4, "cast_w_in")
    w_pa_t_s, w_pb_s, w_out_s, cact_s, cw_s = _prep(sq(w_proj_attn), sq(w_proj_conv), sq(w_out), c, sq(conv_w))

    ncol = w_ada.shape[2]
    b_ada_mine = lax.dynamic_slice(b_ada, (0, me * ncol), (1, ncol))
    ada_slots, cact_slots, cw_slots = _ada_forward(cact_s, cw_s, sq(w_ada), b_ada_mine)
    cact_all = cact_slots[:, :nbat].reshape(N_DEV * nbat, D)
    cw = cw_slots[:, :3].transpose(1, 0, 2).reshape(3, D)
    ada_all = ada_slots[:, :, :nbat].transpose(1, 2, 0, 3).reshape(N_DEV * nbat, 3, D)
    ada = lax.dynamic_slice(ada_all, (me * nbat, 0, 0), (nbat, 3, D))

    w_in_t, qkv, rest, h, (w_pa_t, w_pb, w_o) = _project_gather(
        w_in_t_s, x2, ada, b_in.reshape(N_SLAB, 1, SLAB), [w_pa_t_s, w_pb_s, w_out_s])
    ol_tot = _attn_forward(qkv, nbat)
    (dproj, gx0, do_attn, merged, do_f, bbs, dyc, a_bf, dya, gb_rest, svec, dgate) = _mid(
        rest, ol_tot, x2, tgt2, ada, cw, b_out, ln_g, ln_b, w_pa_t, w_pb, w_o)

    gb_qkv = []
    for g in range(3):
        dproj, gb = _attn_backward(qkv, do_attn, ol_tot, dproj, g, nbat)
        gb_qkv.append(gb)
    g_w_in_t = _grad_w_in_t(dproj, h)

    def small_grads(token):
        g_w_out = _grad_rows_2d(merged, do_f, "grad_w_out", token)
        g_w_pb = _grad_rows_2d(bbs, dyc, "grad_w_proj_conv", g_w_out)
        g_w_pa_t = _grad_rows_2d(dya, a_bf, "grad_w_proj_attn", g_w_pb)
        return [g_w_out, g_w_pb, g_w_pa_t]

    rs_state, token = _reduce_scatter_begin(g_w_in_t, small_grads)
    grad_x, dss = _grad_h(dproj, w_in_t, gx0, x2, ada, token)

    rows8, tot, g_bada = _small_reduce(gb_rest, gb_qkv, svec, dgate, dss)
    (g_in_mine, g_in_got), (g_out, g_pb, g_pa_t) = _reduce_scatter_end(rs_state, tot)
    loss = tot[0, P_LOSS]
    dada_all = rows8[:, 0, P_DADA:].reshape(N_DEV * nbat, 3 * D)
    dada_mine = lax.dynamic_slice(dada_all, (0, me * ncol), (N_DEV * nbat, ncol))

    g_in_t, d_win_t, nm_win_t, nv_win_t = _final_sum_adam_rows(g_in_mine, g_in_got, w_in_rows, tr(m_w_in), tr(v_w_in), 4, "adam_w_in")
    g_win, d_win, nm_win, nv_win = g_in_t.T, d_win_t.T, nm_win_t.T, nv_win_t.T
    g_wpa, d_wpa, nm_wpa, nv_wpa = _adam_transposed(g_pa_t, sq(w_proj_attn), sq(m_w_proj_attn), sq(v_w_proj_attn), "adam_w_proj_attn")
    g_wada, d_wada, nm_wada, nv_wada = _adam_w_ada(cact_all, dada_mine, sq(w_ada), sq(m_w_ada), sq(v_w_ada))
    g_bin = tot[:, P_BIN:P_BIN + D_IN]
    g_bout = tot[:, P_BOUT:P_BOUT + D]
    g_lng = tot[:, P_LNG:P_LNG + D]
    g_lnb = tot[:, P_LNB:P_LNB + D]
    g_conv = lax.dynamic_slice(tot[:, P_CONV:P_CONV + 3 * D].reshape(3, D), (0, me * cw_s.shape[1]), (3, cw_s.shape[1]))
    upd = _adam_many([
        (sq(w_proj_conv), g_pb, sq(m_w_proj_conv), sq(v_w_proj_conv)),
        (sq(w_out), g_out, sq(m_w_out), sq(v_w_out)),
        (b_ada, g_bada, m_b_ada, v_b_ada), (b_in, g_bin, m_b_in, v_b_in), (sq(conv_w), g_conv, sq(m_conv_w), sq(v_conv_w)),
        (b_out, g_bout, m_b_out, v_b_out), (ln_g, g_lng, m_ln_g, v_ln_g), (ln_b, g_lnb, m_ln_b, v_ln_b)], "adam_rest")
    (d_wpb, nm_wpb, nv_wpb), (d_wout, nm_wout, nv_wout), (d_bada, nm_bada, nv_bada), (d_bin, nm_bin, nv_bin), \
        (d_conv, nm_conv, nv_conv), (d_bout, nm_bout, nv_bout), (d_lng, nm_lng, nv_lng), (d_lnb, nm_lnb, nv_lnb) = upd

    ex = lambda a: a.reshape((1,) + a.shape)
    grads = [ex(g_wada), g_bada, ex(g_win), g_bin, ex(g_conv), ex(g_wpa), ex(g_pb), ex(g_out), g_bout, g_lng, g_lnb]
    deltas = [ex(d_wada), d_bada, ex(d_win), d_bin, ex(d_conv), ex(d_wpa), ex(d_wpb), ex(d_wout), d_bout, d_lng, d_lnb]
    new_m = [ex(nm_wada), nm_bada, ex(nm_win), nm_bin, ex(nm_conv), ex(nm_wpa), ex(nm_wpb), ex(nm_wout), nm_bout, nm_lng, nm_lnb]
    new_v = [ex(nv_wada), nv_bada, ex(nv_win), nv_bin, ex(nv_conv), ex(nv_wpa), ex(nv_wpb), ex(nv_wout), nv_bout, nv_lng, nv_lnb]
    return (loss, grad_x.reshape(x.shape), *grads, *deltas, *new_m, *new_v)
```

```python
import functools

import jax
import jax.numpy as jnp
from jax import lax
from jax.experimental import pallas as pl
from jax.experimental.pallas import tpu as pltpu

F32, BF16 = jnp.float32, jnp.bfloat16
MESH = pl.DeviceIdType.MESH
N_DEV = 8
D = 1024
SLAB = 256
N_QKV, N_REST = 9, 25
N_SLAB = N_QKV + N_REST
D_IN = N_SLAB * SLAB
DP_SLABS = 36
BLK = 128
HALO = 16
GROUPS = ((128, 1), (512, 4), (2048, 16))
ALPHA = 2.0 ** 0.25
LN_EPS = 1e-5
LR, B1, B2, EPS, WD, STEP = 0.001, 0.9, 0.999, 1e-08, 0.01, 10
R_ZA, R_UX, R_GB, R_GC, R_ZC, R_GA, R_GBM = 0, 1, 5, 9, 13, 17, 21
P_BIN, P_BOUT, P_LNG, P_LNB, P_CONV, P_LOSS, P_DADA = 0, 8704, 9728, 10752, 11776, 14848, 14976
MIB = 1024 * 1024


def _pcall(body, *, out_shape, out_specs=None, **kw):
    def pin_out(shape, spec):
        blocked = isinstance(shape, jax.ShapeDtypeStruct) and getattr(spec, "block_shape", None) is not None
        return pltpu.HBM(shape.shape, shape.dtype) if blocked else shape

    n_scalar = 0
    if out_specs is None:
        specs = kw["grid_spec"].out_specs
        n_scalar = kw["grid_spec"].num_scalar_prefetch
    else:
        kw["out_specs"] = specs = out_specs
    if isinstance(out_shape, (tuple, list)):
        out_shape = tuple(pin_out(s, p) for s, p in zip(out_shape, specs))
    else:
        out_shape = pin_out(out_shape, specs)
    call = pl.pallas_call(body, out_shape=out_shape, **kw)

    def run(*operands):
        def pin(o):
            is_data = jnp.issubdtype(o.dtype, jnp.floating) or jnp.issubdtype(o.dtype, jnp.integer)
            return pltpu.with_memory_space_constraint(o, pltpu.HBM) if is_data else o
        return call(*operands[:n_scalar], *[pin(o) for o in operands[n_scalar:]])

    return run

ANY = pl.BlockSpec(memory_space=pl.ANY)
VMEM = pl.BlockSpec(memory_space=pltpu.VMEM)


def _whole(a):
    return pl.BlockSpec(a.shape, lambda i: (0,) * len(a.shape))


def _params(vmem_mib=None, sem=None):
    kw = {}
    if vmem_mib is not None:
        kw["vmem_limit_bytes"] = vmem_mib * MIB
    if sem is not None:
        kw["dimension_semantics"] = sem
    return pltpu.CompilerParams(**kw)


def _nn(a, b):
    return jnp.dot(a, b, preferred_element_type=F32)


def _nt(a, b):
    return lax.dot_general(a, b, (((1,), (1,)), ((), ())), preferred_element_type=F32)


def _tn(a, b):
    return lax.dot_general(a, b, (((0,), (0,)), ((), ())), preferred_element_type=F32)


def _sigmoid(v):
    return 1.0 / (1.0 + jnp.exp(-v))


def _part8(v):
    return v.reshape(v.shape[0] // 8, 8, v.shape[1]).sum(axis=0)


def _my_position():
    return lax.axis_index("x"), lax.axis_index("y"), lax.axis_index("c")


def _flat(px, py, pc):
    return 4 * px + 2 * py + pc


def _peer(mask):
    x, y, c = _my_position()
    return (x ^ ((mask >> 2) & 1), y ^ ((mask >> 1) & 1), c ^ (mask & 1))


def _column_chunks(n):
    chunks = [(128 * a, 0, 128 * a, 128) for a in range(n // 128)]
    if n % 128:
        chunks.append((n - 128, 128 - n % 128, 128 * (n // 128), n % 128))
    return chunks


def _cast_rows(w, n_steps, name):
    rows, ncol = w.shape
    blk = pl.BlockSpec((rows // n_steps, ncol), lambda i: (i, 0))

    def body(w_ref, o_ref):
        o_ref[...] = w_ref[...].astype(BF16)

    return _pcall(body, grid=(n_steps,), out_shape=jax.ShapeDtypeStruct(w.shape, BF16), in_specs=[blk], out_specs=blk,
                  name=name, compiler_params=_params(16, ("parallel",)))(w)


def _prep(w_pa, w_pb, w_out, c, conv_w):
    def body(wpa_ref, wpb_ref, wout_ref, c_ref, cw_ref, wpat_ref, wpb_o, wout_o, cact_ref, cwp_ref):
        wpat_ref[...] = wpa_ref[...].T.astype(BF16)
        wpb_o[...] = wpb_ref[...].astype(BF16)
        wout_o[...] = wout_ref[...].astype(BF16)
        cv = c_ref[...]
        cact_ref[...] = jnp.zeros_like(cact_ref)
        cact_ref[pl.ds(0, cv.shape[0]), :] = cv * _sigmoid(cv)
        cwp_ref[...] = jnp.zeros_like(cwp_ref)
        cwp_ref[pl.ds(0, 3), :] = cw_ref[...]

    out_shape = (jax.ShapeDtypeStruct((w_pa.shape[1], w_pa.shape[0]), BF16),
                 jax.ShapeDtypeStruct(w_pb.shape, BF16), jax.ShapeDtypeStruct(w_out.shape, BF16),
                 jax.ShapeDtypeStruct((8, D), F32), jax.ShapeDtypeStruct((8, conv_w.shape[1]), F32))
    operands = (w_pa, w_pb, w_out, c, conv_w)
    return _pcall(body, grid=(1,), out_shape=out_shape, in_specs=[_whole(a) for a in operands],
                  out_specs=tuple(_whole(o) for o in out_shape), name="prep", compiler_params=_params(16))(*operands)


def _exchange_slots(out_refs, send_sems, recv_sems, base=0):
    me = _flat(*_my_position())

    def copy(a, mask, slot):
        return pltpu.make_async_remote_copy(
            src_ref=out_refs[a].at[slot], dst_ref=out_refs[a].at[slot], send_sem=send_sems.at[base + 7 * a + mask - 1],
            recv_sem=recv_sems.at[base + 7 * a + mask - 1], device_id=_peer(mask), device_id_type=MESH)

    pairs = [(a, mask) for a in range(len(out_refs)) for mask in range(1, N_DEV)]
    for a, mask in pairs:
        copy(a, mask, me).start()
    for a, mask in pairs:
        copy(a, mask, _flat(*_peer(mask))).wait_recv()
    for a, mask in pairs:
        copy(a, mask, me).wait_send()


def _ada_forward(cact_mine, cw_mine, w_ada, b_ada_mine):
    ncol = w_ada.shape[1]

    def body(c_ref, cw_ref, w_ref, b_ref, out_ref, call_ref, cwall_ref, send_sems, recv_sems):
        me = _flat(*_my_position())
        call_ref[me] = c_ref[...]
        cwall_ref[me] = cw_ref[...]
        _exchange_slots([call_ref, cwall_ref], send_sems, recv_sems)
        c_all = call_ref[...].reshape(N_DEV * 8, D).astype(BF16)
        out_ref[me] = (_nn(c_all, w_ref[...].astype(BF16)) + b_ref[...]).reshape(N_DEV, 8, ncol)
        _exchange_slots([out_ref], send_sems, recv_sems, base=14)

    operands = (cact_mine, cw_mine, w_ada, b_ada_mine)
    out_shape = (jax.ShapeDtypeStruct((N_DEV, N_DEV, 8, ncol), F32), jax.ShapeDtypeStruct((N_DEV, 8, D), F32),
                 jax.ShapeDtypeStruct((N_DEV,) + cw_mine.shape, F32))
    return _pcall(body, grid=(1,), out_shape=out_shape, in_specs=[_whole(a) for a in operands], out_specs=(VMEM,) * 3,
                  scratch_shapes=[pltpu.SemaphoreType.DMA((21,)), pltpu.SemaphoreType.DMA((21,))], name="ada_forward",
                  compiler_params=_params(16))(*operands)


def _small_reduce(gb_rest, gb_qkv, svec, dgate, dss):
    nbat = dgate.shape[0]

    def body(gbr_ref, q0_ref, q1_ref, q2_ref, sv_ref, dg_ref, dss_ref, rows_ref, tot_ref, gbada_ref, send_sems, recv_sems):
        me = _flat(*_my_position())

        def put(off, v):
            rows_ref[me, :, pl.ds(off, v.shape[1])] = v

        def row(v):
            return jnp.sum(v, axis=0, keepdims=True)

        for g, q_ref in enumerate((q0_ref, q1_ref, q2_ref)):
            for which in range(3):
                put(P_BIN + SLAB * (3 * which + g), row(q_ref[which]))
        for s in range(N_REST):
            put(P_BIN + SLAB * (N_QKV + s), row(gbr_ref[s]))
        put(P_LNG, row(sv_ref[0]))
        put(P_LNB, row(sv_ref[1]))
        put(P_BOUT, row(sv_ref[2]))
        for j in range(3):
            put(P_CONV + D * j, row(sv_ref[3 + j]))
        loss = (0.5 / D) * jnp.sum(row(sv_ref[6]), axis=1, keepdims=True)
        put(P_LOSS, jnp.broadcast_to(loss, (1, 128)))
        for b in range(nbat):
            put(P_DADA + 3 * D * b, row(dss_ref[b, 0]))
            put(P_DADA + 3 * D * b + D, row(dss_ref[b, 1]))
            put(P_DADA + 3 * D * b + 2 * D, row(dg_ref[b]))
        _exchange_slots([rows_ref], send_sems, recv_sems)
        tot = rows_ref[0]
        for k in range(1, N_DEV):
            tot = tot + rows_ref[k]
        tot_ref[...] = tot
        gbada = tot[:, P_DADA:P_DADA + 3 * D]
        for b in range(1, nbat):
            gbada = gbada + tot[:, P_DADA + 3 * D * b:P_DADA + 3 * D * (b + 1)]
        gbada_ref[...] = gbada

    p_len = P_DADA + nbat * 3 * D
    out_shape = (jax.ShapeDtypeStruct((N_DEV, 1, p_len), F32), jax.ShapeDtypeStruct((1, p_len), F32),
                 jax.ShapeDtypeStruct((1, 3 * D), F32))
    operands = (gb_rest, *gb_qkv, svec, dgate, dss)
    return _pcall(body, grid=(1,), out_shape=out_shape, in_specs=[_whole(a) for a in operands],
                  out_specs=(VMEM, _whole(out_shape[1]), _whole(out_shape[2])),
                  scratch_shapes=[pltpu.SemaphoreType.DMA((7,)), pltpu.SemaphoreType.DMA((7,))], name="small_reduce",
                  compiler_params=_params(16))(*operands)


PIECE = 64
N_CHUNK = 4
ARRIVAL_RANK = (0, 1, 3, 5, 2, 4, 6, 7)
SLOT_MASK = (1, 4, 2, 6, 5, 3, 7)


def _arrival_tables(shard_rows):
    import numpy as np
    crow = shard_rows // N_CHUNK
    table = np.zeros((N_DEV, N_SLAB + 7 * N_CHUNK), np.int32)
    lo = [(SLAB * j) // crow for j in range(N_SLAB)]
    hi = [(SLAB * j + SLAB - 1) // crow for j in range(N_SLAB)]
    for k in range(N_DEV):
        def rank(chunk):
            shard_rank = ARRIVAL_RANK[(chunk // N_CHUNK) ^ k]
            return shard_rank if shard_rank < 2 else 2 + 8 * (chunk % N_CHUNK) + shard_rank
        order = sorted(range(N_SLAB), key=lambda j: (max(rank(lo[j]), rank(hi[j])), j))
        table[k, :N_SLAB] = order
        for slot, mask in enumerate(SLOT_MASK):
            for ch in range(N_CHUNK):
                chunk = (k ^ mask) * N_CHUNK + ch
                table[k, N_SLAB + slot * N_CHUNK + ch] = min(t for t, j in enumerate(order) if lo[j] <= chunk <= hi[j])
    return table


def _project_gather(shard, x, ada, b_in3, others, xt=512):
    t = x.shape[0]
    n_o = len(others)
    srows = shard.shape[0]
    crow = srows // N_CHUNK
    shards = [shard] + list(others)
    table = jnp.asarray(_arrival_tables(srows))
    seq_tiles = (t // ada.shape[0]) // xt

    def body(tbl_ref, *refs):
        srcs = [refs[0]] + list(refs[4:4 + n_o])
        x_ref, ada_ref, b_ref = refs[1], refs[2], refs[3]
        outs = [refs[4 + n_o]] + list(refs[8 + n_o:8 + 2 * n_o])
        qkv_ref, rest_ref, h_out = refs[5 + n_o], refs[6 + n_o], refs[7 + n_o]
        (wtile, obf, orest, h_ref, xbuf, send_sems, recv_sems, local_sems, tile_sems, obf_sems, orest_sems, x_sems,
         h_sems) = refs[8 + 2 * n_o:]
        w_full = outs[0]
        x, y, c = _my_position()
        k = _flat(x, y, c)
        me, sibling = (x, y, c), (x, y, 1 - c)
        chips = [(1 - x, y), (x, 1 - y), (1 - x, 1 - y)]

        def rows(a, px, py, pc, ch):
            r = shards[a].shape[0]
            if ch is None:
                return outs[a].at[pl.ds(pl.multiple_of(_flat(px, py, pc) * r, r), r), :]
            return outs[a].at[pl.ds(pl.multiple_of(_flat(px, py, pc) * r + ch * crow, crow), crow), :]

        def copy(a, slot, block, to, ch=None, src=None):
            sem = slot * N_CHUNK + ch if a == 0 else 7 * (N_CHUNK - 1 + a) + slot
            if src is not None and ch is not None:
                src = src.at[pl.ds(ch * crow, crow), :]
            return pltpu.make_async_remote_copy(
                src_ref=rows(a, *block, ch) if src is None else src, dst_ref=rows(a, *block, ch),
                send_sem=send_sems.at[sem], recv_sem=recv_sems.at[sem], device_id=to, device_id_type=MESH)

        mine = [pltpu.make_async_copy(srcs[a], rows(a, *me, None), local_sems.at[a]) for a in range(1 + n_o)]
        first = []
        for ch in range(N_CHUNK):
            first.append(copy(0, 0, me, sibling, ch, src=srcs[0]))
            first += [copy(0, 1 + j, me, (*chip, c), ch, src=srcs[0]) for j, chip in enumerate(chips)]
        for a in range(1, 1 + n_o):
            first.append(copy(a, 0, me, sibling, src=srcs[a]))
            first += [copy(a, 1 + j, me, (*chip, c), src=srcs[a]) for j, chip in enumerate(chips)]
        for cp in mine + first:
            cp.start()

        def arrive(a, slot, ch=None):
            if slot == 0:
                copy(a, 0, sibling, me, ch).wait_recv()
            elif slot < 4:
                copy(a, slot, (*chips[slot - 1], c), me, ch).wait_recv()
                copy(a, slot + 3, (*chips[slot - 1], c), sibling, ch).start()
            else:
                copy(a, slot, (*chips[slot - 4], 1 - c), me, ch).wait_recv()

        def arrive_for(step):
            for slot in range(7):
                for ch in range(N_CHUNK):
                    @pl.when(tbl_ref[k, N_SLAB + slot * N_CHUNK + ch] == step)
                    def _():
                        arrive(0, slot, ch)

        def fetch(step, buf):
            slab = tbl_ref[k, step]
            for p in range(SLAB // PIECE):
                g0 = slab * SLAB + PIECE * p
                own = (g0 >= k * srows) & (g0 < (k + 1) * srows)
                dst = wtile.at[buf, pl.ds(PIECE * p, PIECE), :]

                @pl.when(own)
                def _():
                    pltpu.make_async_copy(srcs[0].at[pl.ds(pl.multiple_of(g0 - k * srows, PIECE), PIECE), :], dst, tile_sems.at[buf]).start()

                @pl.when(jnp.logical_not(own))
                def _():
                    pltpu.make_async_copy(w_full.at[pl.ds(pl.multiple_of(g0, PIECE), PIECE), :], dst, tile_sems.at[buf]).start()

        def wait_tile(buf):
            pltpu.make_async_copy(w_full.at[pl.ds(0, SLAB), :], wtile.at[buf], tile_sems.at[buf]).wait()

        def put(buf_ref, sems, dst_ref, count, value):
            b = count % 2

            @pl.when(count >= 2)
            def _():
                pltpu.make_async_copy(buf_ref.at[b], dst_ref, sems.at[b]).wait()

            buf_ref[b] = value
            pltpu.make_async_copy(buf_ref.at[b], dst_ref, sems.at[b]).start()

        def drain(buf_ref, sems, dst_ref, count):
            for back in (1, 2):
                @pl.when(count >= back)
                def _():
                    pltpu.make_async_copy(buf_ref.at[(count - back) % 2], dst_ref, sems.at[(count - back) % 2]).wait()

        def x_copy(i):
            return pltpu.make_async_copy(x_ref.at[pl.ds(xt * i, xt), :], xbuf.at[i % 2], x_sems.at[i % 2])

        def h_copy(i):
            return pltpu.make_async_copy(h_ref.at[pl.ds(xt * i, xt), :], h_out.at[pl.ds(xt * i, xt), :], h_sems.at[i % 2])

        x_copy(0).start()
        for i in range(t // xt):
            if i + 1 < t // xt:
                x_copy(i + 1).start()
            x_copy(i).wait()
            b = i // seq_tiles
            h_ref[pl.ds(xt * i, xt), :] = (xbuf[i % 2] * (1.0 + ada_ref[b, 1:2, :]) + ada_ref[b, 0:1, :]).astype(BF16)
            if i >= 2:
                h_copy(i - 2).wait()
            h_copy(i).start()
        for i in range(max(t // xt - 2, 0), t // xt):
            h_copy(i).wait()

        arrive_for(0)
        fetch(0, 0)

        def step(s, carry):
            n_bf, n_rest = carry
            buf = s % 2

            @pl.when(s + 1 < N_SLAB)
            def _():
                arrive_for(s + 1)
                fetch(s + 1, 1 - buf)

            wait_tile(buf)
            slab = tbl_ref[k, s]
            v = _nt(h_ref[...], wtile[buf]) + b_ref[slab]
            is_qkv = slab < N_QKV

            @pl.when(is_qkv)
            def _():
                put(obf, obf_sems, qkv_ref.at[jnp.minimum(slab, N_QKV - 1)], n_bf, v.astype(BF16))

            @pl.when(jnp.logical_not(is_qkv))
            def _():
                put(orest, orest_sems, rest_ref.at[jnp.maximum(slab - N_QKV, 0)], n_rest, v.astype(BF16))

            return n_bf + is_qkv.astype(jnp.int32), n_rest + 1 - is_qkv.astype(jnp.int32)

        n_bf, n_rest = lax.fori_loop(0, N_SLAB, step, (jnp.int32(0), jnp.int32(0)))
        drain(obf, obf_sems, qkv_ref.at[0], n_bf)
        drain(orest, orest_sems, rest_ref.at[0], n_rest)

        for slots in ((1, 2, 3), (0, 4, 5, 6)):
            for a in range(1, 1 + n_o):
                for slot in slots:
                    arrive(a, slot)
        for cp in first:
            cp.wait_send()
        for j, chip in enumerate(chips):
            for ch in range(N_CHUNK):
                copy(0, 4 + j, (*chip, c), sibling, ch).wait_send()
            for a in range(1, 1 + n_o):
                copy(a, 4 + j, (*chip, c), sibling).wait_send()
        for cp in mine:
            cp.wait()

    out_shape = ((jax.ShapeDtypeStruct((N_DEV * srows, D), BF16), jax.ShapeDtypeStruct((N_QKV, t, SLAB), BF16),
                  jax.ShapeDtypeStruct((N_REST, t, SLAB), BF16), jax.ShapeDtypeStruct((t, D), BF16))
                 + tuple(jax.ShapeDtypeStruct((N_DEV * o.shape[0], o.shape[1]), o.dtype) for o in others))
    n_all = 1 + n_o
    n_sems = 7 * (N_CHUNK + n_o)
    pair = pltpu.SemaphoreType.DMA((2,))
    grid_spec = pltpu.PrefetchScalarGridSpec(
        num_scalar_prefetch=1, grid=(1,),
        in_specs=[ANY, ANY, pl.BlockSpec(ada.shape, lambda i, tbl: (0, 0, 0)),
                  pl.BlockSpec((N_SLAB, 1, SLAB), lambda i, tbl: (0, 0, 0))] + [ANY] * n_o,
        out_specs=(ANY,) * (4 + n_o),
        scratch_shapes=[pltpu.VMEM((2, SLAB, D), BF16), pltpu.VMEM((2, t, SLAB), BF16), pltpu.VMEM((2, t, SLAB), BF16),
                        pltpu.VMEM((t, D), BF16), pltpu.VMEM((2, xt, D), F32),
                        pltpu.SemaphoreType.DMA((n_sems,)), pltpu.SemaphoreType.DMA((n_sems,)),
                        pltpu.SemaphoreType.DMA((n_all,)), pair, pair, pair, pair, pair])
    res = _pcall(body, grid_spec=grid_spec, out_shape=out_shape, name="project_gather",
                 compiler_params=_params(48, ("arbitrary",)))(table, shard, x, ada, b_in3, *others)
    return res[0], res[1], res[2], res[3], list(res[4:])


def _bias_tables(g):
    window, dil = GROUPS[g]
    span = window // dil
    qi = jnp.arange(BLK)[:, None]
    kj = jnp.arange(2 * BLK)[None, :]
    delta = qi + BLK - kj
    valid = (delta >= 0) & (delta <= span)
    heads = jnp.arange(4, dtype=F32) + 4.0 * g
    slopes = 2.0 ** (-8.0 * (heads + 1.0) / 12.0)
    bias = -slopes[:, None, None] * (delta * dil).astype(F32)[None]
    return jnp.where(valid[None], bias, -1e30).reshape(4 * BLK, 2 * BLK)


def _head_masks(shape):
    lane = lax.broadcasted_iota(jnp.int32, shape, 1)
    return [(lane >= 64 * h) & (lane < 64 * (h + 1)) for h in range(4)]


def _stack_heads(v, masks):
    return jnp.concatenate([jnp.where(masks[h], v, jnp.zeros_like(v)) for h in range(4)], axis=0)


def _unstack_heads(v4, masks):
    out = jnp.where(masks[0], v4[0:BLK], 0.0)
    for h in range(1, 4):
        out = jnp.where(masks[h], v4[BLK * h:BLK * (h + 1)], out)
    return out


def _regroup(load_half, dst_ref, stage_ref, n, dil):
    for hlf in range(2):
        stage_ref[hlf] = load_half(hlf)

    def residue(r, carry):
        for hlf in range(2):
            dst_ref[pl.ds(pl.multiple_of(r * n, BLK), n), pl.ds(128 * hlf, 128)] = (
                stage_ref[hlf, pl.ds(r, n, stride=dil), :].astype(dst_ref.dtype))
        return carry

    lax.fori_loop(0, dil, residue, 0)


def _store_block(nat_ref, r, i, val, dil):
    for hlf in range(2):
        nat_ref[hlf, pl.ds(r + dil * BLK * i, BLK, stride=dil), :] = val[:, 128 * hlf:128 * (hlf + 1)]


def _for_blocks(block, dil, nblk):
    if dil == 1:
        block(0, 0, True)
        block(0, 1, False)

        def pair(k, carry):
            block(0, 2 * k, False)
            block(0, 2 * k + 1, False)
            return carry

        lax.fori_loop(1, nblk // 2, pair, 0)
    else:
        def residues(k, carry):
            block(2 * k, 0, True)
            block(2 * k + 1, 0, True)
            if nblk > 1:
                def loop(i, c):
                    block(2 * k, i, False)
                    block(2 * k + 1, i, False)
                    return c
                lax.fori_loop(1, nblk, loop, 0)
            return carry

        lax.fori_loop(0, dil // 2, residues, 0)


def _attn_forward(qkv, nbat):
    t = qkv.shape[1]
    seq = t // nbat
    n_grp = len(GROUPS)

    def body(qkv_ref, b0_ref, b1_ref, b2_ref, ol_ref, stage, qs_ref, ks_ref, vs_ref, *nat):
        masks = _head_masks((BLK, SLAB))
        bias_refs = (b0_ref, b1_ref, b2_ref)
        for g, (_, dil) in enumerate(GROUPS):
            n = seq // dil
            bias_ref, nat_o, nat_l = bias_refs[g], nat[2 * g], nat[2 * g + 1]
            if dil > 1:
                qd, kd, vd = qs_ref, ks_ref, vs_ref
                for which, dst in enumerate((qd, kd, vd)):
                    _regroup(lambda hlf, which=which, g=g: qkv_ref[3 * which + g, :, pl.ds(128 * hlf, 128)].astype(F32), dst, stage, n, dil)
            else:
                qd, kd, vd = qkv_ref.at[g], qkv_ref.at[3 + g], qkv_ref.at[6 + g]

            def block(r, i, first, n=n, dil=dil, qd=qd, kd=kd, vd=vd, bias_ref=bias_ref, nat_o=nat_o, nat_l=nat_l):
                base = r * n
                qs = pl.ds(pl.multiple_of(base + i * BLK, BLK), BLK)
                ks = pl.ds(pl.multiple_of(base, BLK), BLK) if first else pl.ds(pl.multiple_of(base + (i - 1) * BLK, BLK), 2 * BLK)
                q, kk, vv = qd[qs, :], kd[ks, :], vd[ks, :]
                bias = bias_ref[:, pl.ds(BLK, BLK)] if first else bias_ref[...]
                s = _nt(_stack_heads(q, masks), kk) * 0.125 + bias
                m = jnp.max(s, axis=1, keepdims=True)
                p = jnp.exp(s - m)
                den = jnp.sum(p, axis=1, keepdims=True)
                out = _unstack_heads(_nn((p * (1.0 / den)).astype(BF16), vv), masks)
                lse = _unstack_heads(jnp.broadcast_to(m + jnp.log(den), (4 * BLK, SLAB)), masks)
                _store_block(nat_o, r, i, out, dil)
                _store_block(nat_l, r, i, lse, dil)

            _for_blocks(block, dil, n // BLK)

        for hlf in range(2):
            l0, l1, l2 = nat[1][hlf], nat[3][hlf], nat[5][hlf]
            mx = jnp.maximum(jnp.maximum(l0, l1), l2)
            e0, e1, e2 = jnp.exp(l0 - mx), jnp.exp(l1 - mx), jnp.exp(l2 - mx)
            den = e0 + e1 + e2
            ol_ref[0, :, pl.ds(128 * hlf, 128)] = (e0 * nat[0][hlf] + e1 * nat[2][hlf] + e2 * nat[4][hlf]) * (1.0 / den)
            ol_ref[1, :, pl.ds(128 * hlf, 128)] = mx + jnp.log(den)

    halves = pltpu.VMEM((2, seq, 128), F32)
    bias_spec = pl.BlockSpec((4 * BLK, 2 * BLK), lambda b: (0, 0))
    return _pcall(
        body, grid=(nbat,), out_shape=jax.ShapeDtypeStruct((2, t, SLAB), F32),
        in_specs=[pl.BlockSpec((N_QKV, seq, SLAB), lambda b: (0, b, 0))] + [bias_spec] * n_grp,
        out_specs=pl.BlockSpec((2, seq, SLAB), lambda b: (0, b, 0)),
        scratch_shapes=[halves] + [pltpu.VMEM((seq, SLAB), BF16)] * 3 + [halves] * (2 * n_grp),
        name="attn_forward", compiler_params=_params(56, ("parallel",)))(qkv, *[_bias_tables(g) for g in range(n_grp)])


def _attn_backward(qkv, do_attn, ol_tot, dproj, g, nbat):
    t = qkv.shape[1]
    seq = t // nbat
    dil = GROUPS[g][1]
    n = seq // dil
    nblk = n // BLK
    qkv4 = qkv.reshape(3, 3, t, SLAB)
    dp4 = dproj.reshape(DP_SLABS // 3, 3, t, SLAB)

    def body(qkv_ref, do_ref, ol_ref, bias_ref, dp_in, dp_ref, gb_ref, dk_acc, dv_acc, *scratch):
        del dp_in
        masks = _head_masks((BLK, SLAB))

        @pl.when(pl.program_id(0) == 0)
        def _():
            gb_ref[...] = jnp.zeros_like(gb_ref)

        dk_acc[...] = jnp.zeros_like(dk_acc)
        dv_acc[...] = jnp.zeros_like(dv_acc)
        if dil > 1:
            stage, qd, kd, vd, dod, prodd, lsed, nat = scratch
            lanes = lambda hlf: pl.ds(128 * hlf, 128)
            for which, dst in enumerate((qd, kd, vd)):
                _regroup(lambda hlf, which=which: qkv_ref[which, 0, :, lanes(hlf)].astype(F32), dst, stage, n, dil)
            _regroup(lambda hlf: do_ref[:, lanes(hlf)].astype(F32), dod, stage, n, dil)
            _regroup(lambda hlf: do_ref[:, lanes(hlf)].astype(F32) * ol_ref[0, :, lanes(hlf)], prodd, stage, n, dil)
            _regroup(lambda hlf: ol_ref[1, :, lanes(hlf)], lsed, stage, n, dil)
        else:
            qd, kd, vd = qkv_ref.at[0, 0], qkv_ref.at[1, 0], qkv_ref.at[2, 0]

        def block(r, i, first):
            base = r * n
            qs = pl.ds(pl.multiple_of(base + i * BLK, BLK), BLK)
            ks = pl.ds(pl.multiple_of(base, BLK), BLK) if first else pl.ds(pl.multiple_of(base + (i - 1) * BLK, BLK), 2 * BLK)
            q, kk, vv = qd[qs, :], kd[ks, :], vd[ks, :]
            if dil > 1:
                do, prod, lse = dod[qs, :], prodd[qs, :], lsed[qs, :]
            else:
                do = do_ref[qs, :]
                prod = do.astype(F32) * ol_ref[0, qs, :]
                lse = ol_ref[1, qs, :]
            q4, do4 = _stack_heads(q, masks), _stack_heads(do, masks)
            bias = bias_ref[:, pl.ds(BLK, BLK)] if first else bias_ref[...]
            lse4 = jnp.concatenate([lse[:, 64 * h:64 * h + 1] for h in range(4)], axis=0)
            delta4 = jnp.concatenate([jnp.sum(jnp.where(masks[h], prod, 0.0), axis=1, keepdims=True) for h in range(4)], axis=0)
            p = jnp.exp(_nt(q4, kk) * 0.125 + bias - lse4)
            ds = (p * (_nt(do4, vv) - delta4)).astype(BF16)
            dv_acc[ks, :] += _tn(p.astype(BF16), do4)
            dk_acc[ks, :] += _tn(ds, q4) * 0.125
            dq = _unstack_heads(_nn(ds, kk), masks) * 0.125
            if dil > 1:
                _store_block(nat, r, i, dq, dil)
            else:
                dp_ref[0, 0, qs, :] = dq.astype(BF16)
            gb_ref[0] += _part8(dq)

        _for_blocks(block, dil, nblk)
        gb_ref[1] += _part8(dk_acc[...])
        gb_ref[2] += _part8(dv_acc[...])
        if dil > 1:
            def flush(which):
                for hlf in range(2):
                    dp_ref[which, 0, :, pl.ds(128 * hlf, 128)] = nat[hlf].astype(BF16)

            def to_token_order(acc_ref):
                def residue(r, carry):
                    for hlf in range(2):
                        nat[hlf, pl.ds(r, n, stride=dil), :] = acc_ref[pl.ds(pl.multiple_of(r * n, BLK), n), pl.ds(128 * hlf, 128)]
                    return carry
                lax.fori_loop(0, dil, residue, 0)

            flush(0)
            to_token_order(dk_acc)
            flush(1)
            to_token_order(dv_acc)
            flush(2)
        else:
            dp_ref[1, 0] = dk_acc[...].astype(BF16)
            dp_ref[2, 0] = dv_acc[...].astype(BF16)

    scratch = [pltpu.VMEM((seq, SLAB), F32)] * 2
    if dil > 1:
        scratch += ([pltpu.VMEM((2, seq, 128), F32)] + [pltpu.VMEM((seq, SLAB), BF16)] * 4 + [pltpu.VMEM((seq, SLAB), F32)] * 2
                    + [pltpu.VMEM((2, seq, 128), F32)])
    dp, gb = _pcall(
        body, grid=(nbat,),
        out_shape=(jax.ShapeDtypeStruct(dp4.shape, BF16), jax.ShapeDtypeStruct((3, 8, SLAB), F32)),
        in_specs=[pl.BlockSpec((3, 1, seq, SLAB), lambda b: (0, g, b, 0)),
                  pl.BlockSpec((seq, SLAB), lambda b: (b, 0)),
                  pl.BlockSpec((2, seq, SLAB), lambda b: (0, b, 0)),
                  pl.BlockSpec((4 * BLK, 2 * BLK), lambda b: (0, 0)), ANY],
        out_specs=(pl.BlockSpec((3, 1, seq, SLAB), lambda b: (DP_SLABS // 9 - 1, g, b, 0)),
                   pl.BlockSpec((3, 8, SLAB), lambda b: (0, 0, 0))),
        scratch_shapes=scratch, input_output_aliases={4: 0}, name=f"attn_backward_{g}",
        compiler_params=_params(48, ("arbitrary",)))(qkv4, do_attn, ol_tot, _bias_tables(g), dp4)
    return dp.reshape(DP_SLABS, t, SLAB), gb


def _mid(rest, ol_tot, x, tgt, ada, cw, b_out, ln_g, ln_b, w_pa_t, w_pb, w_out, tm=256):
    t = x.shape[0]
    nbat = ada.shape[0]
    nt = t // tm
    tps = nt // nbat

    def body(rest_ref, halo_ref, ol_ref, x_ref, t_ref, ada_ref, cw_ref, bout_ref, lng_ref, lnb_ref,
             wpat_ref, wpb_ref, wout_ref,
             dp_ref, gx0_ref, doa_ref, mg_ref, dof_ref, bbs_ref, dyc_ref, a_ref, dya_ref,
             gbr_ref, sv_ref, dgate_ref, carry_ref, keep_ref):
        i = pl.program_id(0)
        ti = nt - 1 - i
        pos = ti % tps

        @pl.when(i == 0)
        def _():
            gbr_ref[...] = jnp.zeros_like(gbr_ref)
            sv_ref[...] = jnp.zeros_like(sv_ref)

        @pl.when(pos == tps - 1)
        def _():
            dgate_ref[...] = jnp.zeros_like(dgate_ref)
            carry_ref[...] = jnp.zeros_like(carry_ref)

        row = lax.broadcasted_iota(jnp.int32, (tm, SLAB), 0)
        halo_on = (pos > 0).astype(F32)

        def cols(s):
            return pl.ds(SLAB * s, SLAB)

        def rest(slab):
            return rest_ref[slab].astype(F32)

        o_attn = ol_ref[0]
        z_a = rest(R_ZA)
        sg_za = _sigmoid(z_a)
        a_ref[...] = (o_attn * z_a * sg_za).astype(BF16)
        y_attn = _nt(a_ref[...], wpat_ref[...])

        for s in range(4):
            u = rest(R_GC + s) * rest(R_UX + s)
            hu = halo_ref[R_GC + s].astype(F32) * halo_ref[R_UX + s].astype(F32) * halo_on
            u1 = jnp.where(row == 0, hu[HALO - 1:HALO], pltpu.roll(u, 1, 0))
            u2 = jnp.where(row == 0, hu[HALO - 2:HALO - 1], jnp.where(row == 1, hu[HALO - 1:HALO], pltpu.roll(u, 2, 0)))
            conv = cw_ref[0:1, cols(s)] * u2 + cw_ref[1:2, cols(s)] * u1 + cw_ref[2:3, cols(s)] * u
            zc = rest(R_ZC + s)
            sg = _sigmoid(zc)
            keep_ref[2, :, cols(s)], keep_ref[3, :, cols(s)], keep_ref[4, :, cols(s)], keep_ref[5, :, cols(s)] = u1, u2, conv, sg
            bbs_ref[:, cols(s)] = (rest(R_GB + s) * conv * (zc * sg)).astype(BF16)
        y_conv = _nn(bbs_ref[...], wpb_ref[...])

        for s in range(4):
            s_a, s_b = _sigmoid(rest(R_GA + s)), _sigmoid(rest(R_GBM + s))
            keep_ref[0, :, cols(s)], keep_ref[1, :, cols(s)] = s_a, s_b
            mg_ref[:, cols(s)] = (s_a * y_attn[:, SLAB * s:SLAB * (s + 1)] + s_b * y_conv[:, SLAB * s:SLAB * (s + 1)]).astype(BF16)
        o = _nn(mg_ref[...], wout_ref[...]) + bout_ref[...]
        gate = ada_ref[0, 2:3, :]
        r = ALPHA * x_ref[...] + gate * o
        mu = jnp.mean(r, axis=1, keepdims=True)
        rc = r - mu
        rstd = lax.rsqrt(jnp.mean(rc * rc, axis=1, keepdims=True) + LN_EPS)
        xhat = rc * rstd
        err = xhat * lng_ref[...] + lnb_ref[...] - t_ref[...]
        sv_ref[6] += _part8(err * err)
        dy = err * (1.0 / D)
        sv_ref[0] += _part8(dy * xhat)
        sv_ref[1] += _part8(dy)
        dxh = dy * lng_ref[...]
        dr = rstd * (dxh - jnp.mean(dxh, axis=1, keepdims=True) - xhat * jnp.mean(dxh * xhat, axis=1, keepdims=True))
        gx0_ref[...] = ALPHA * dr
        dgate_ref[0] += _part8(dr * o)
        do_ = dr * gate
        sv_ref[2] += _part8(do_)
        dof_ref[...] = do_.astype(BF16)
        dmerged = _nt(dof_ref[...], wout_ref[...])
        for s in range(4):
            s_a, s_b = keep_ref[0, :, cols(s)], keep_ref[1, :, cols(s)]
            dm = dmerged[:, SLAB * s:SLAB * (s + 1)]
            ya, yc = y_attn[:, SLAB * s:SLAB * (s + 1)], y_conv[:, SLAB * s:SLAB * (s + 1)]
            dya_ref[:, cols(s)] = (dm * s_a).astype(BF16)
            dyc_ref[:, cols(s)] = (dm * s_b).astype(BF16)
            dga = dm * ya * s_a * (1.0 - s_a)
            dgb = dm * yc * s_b * (1.0 - s_b)
            dp_ref[R_GA + s] = dga.astype(BF16)
            dp_ref[R_GBM + s] = dgb.astype(BF16)
            gbr_ref[R_GA + s] += _part8(dga)
            gbr_ref[R_GBM + s] += _part8(dgb)

        da = _nn(dya_ref[...], wpat_ref[...])
        doa_ref[...] = (da * z_a * sg_za).astype(BF16)
        dza = da * o_attn * (sg_za * (1.0 + z_a * (1.0 - sg_za)))
        dp_ref[R_ZA] = dza.astype(BF16)
        gbr_ref[R_ZA] += _part8(dza)

        dbb = _nt(dyc_ref[...], wpb_ref[...])
        for s in range(4):
            ux, gc, zc = rest(R_UX + s), rest(R_GC + s), rest(R_ZC + s)
            u = gc * ux
            u1, u2, conv, sg = keep_ref[2, :, cols(s)], keep_ref[3, :, cols(s)], keep_ref[4, :, cols(s)], keep_ref[5, :, cols(s)]
            gb = rest(R_GB + s)
            d_b = dbb[:, SLAB * s:SLAB * (s + 1)]
            szc = zc * sg
            dgb_ = d_b * conv * szc
            dconv = d_b * gb * szc
            dzc = d_b * gb * conv * (sg * (1.0 + zc * (1.0 - sg)))
            sv_ref[3, :, cols(s)] += _part8(dconv * u2)
            sv_ref[4, :, cols(s)] += _part8(dconv * u1)
            sv_ref[5, :, cols(s)] += _part8(dconv * u)
            nxt = carry_ref[:, cols(s)]
            d1 = jnp.where(row == tm - 1, nxt[0:1], pltpu.roll(dconv, tm - 1, 0))
            d2 = jnp.where(row == tm - 1, nxt[1:2], jnp.where(row == tm - 2, nxt[0:1], pltpu.roll(dconv, tm - 2, 0)))
            carry_ref[:, cols(s)] = dconv[0:8]
            du = cw_ref[2:3, cols(s)] * dconv + cw_ref[1:2, cols(s)] * d1 + cw_ref[0:1, cols(s)] * d2
            dgc, dux = du * ux, du * gc
            for slab, val in ((R_GB + s, dgb_), (R_ZC + s, dzc), (R_GC + s, dgc), (R_UX + s, dux)):
                dp_ref[slab] = val.astype(BF16)
                gbr_ref[slab] += _part8(val)

    def tile(i):
        return nt - 1 - i

    row_blk = lambda i: (tile(i), 0)
    slab_blk = lambda i: (0, tile(i), 0)
    const2 = lambda i: (0, 0)
    const3 = lambda i: (0, 0, 0)
    in_specs = [
        pl.BlockSpec((N_REST, tm, SLAB), slab_blk),
        pl.BlockSpec((N_REST, HALO, SLAB), lambda i: (0, jnp.maximum(tile(i) * (tm // HALO) - 1, 0), 0)),
        pl.BlockSpec((1, tm, SLAB), slab_blk),
        pl.BlockSpec((tm, D), row_blk), pl.BlockSpec((tm, D), row_blk),
        pl.BlockSpec((1, 3, D), lambda i: (tile(i) // tps, 0, 0)),
        pl.BlockSpec((3, D), const2), pl.BlockSpec((1, D), const2), pl.BlockSpec((1, D), const2), pl.BlockSpec((1, D), const2),
        pl.BlockSpec((D, SLAB), const2), pl.BlockSpec((D, D), const2), pl.BlockSpec((D, D), const2)]
    bf_rows = lambda: jax.ShapeDtypeStruct((t, D), BF16)
    out_shape = (
        jax.ShapeDtypeStruct((DP_SLABS, t, SLAB), BF16), jax.ShapeDtypeStruct((t, D), F32),
        jax.ShapeDtypeStruct((t, SLAB), BF16),
        bf_rows(), bf_rows(), bf_rows(), bf_rows(), jax.ShapeDtypeStruct((t, SLAB), BF16), bf_rows(),
        jax.ShapeDtypeStruct((N_REST, 8, SLAB), F32), jax.ShapeDtypeStruct((7, 8, D), F32),
        jax.ShapeDtypeStruct((nbat, 8, D), F32))
    out_specs = (
        pl.BlockSpec((N_REST, tm, SLAB), slab_blk), pl.BlockSpec((tm, D), row_blk),
        pl.BlockSpec((tm, SLAB), row_blk),
        pl.BlockSpec((tm, D), row_blk), pl.BlockSpec((tm, D), row_blk), pl.BlockSpec((tm, D), row_blk),
        pl.BlockSpec((tm, D), row_blk), pl.BlockSpec((tm, SLAB), row_blk), pl.BlockSpec((tm, D), row_blk),
        pl.BlockSpec((N_REST, 8, SLAB), const3), pl.BlockSpec((7, 8, D), const3),
        pl.BlockSpec((1, 8, D), lambda i: (tile(i) // tps, 0, 0)))
    return _pcall(body, grid=(nt,), out_shape=out_shape, in_specs=in_specs, out_specs=out_specs,
                  scratch_shapes=[pltpu.VMEM((8, D), F32), pltpu.VMEM((6, tm, D), F32)], name="mid",
                  compiler_params=_params(56, ("arbitrary",)))(
        rest, rest, ol_tot, x, tgt, ada, cw, b_out, ln_g, ln_b, w_pa_t, w_pb, w_out)


def _tn_matmul(lhs, rhs, lhs_spec, n_steps, out_rows, out_index, name, after):
    t, n = rhs.shape

    def body(l_ref, r_ref, after_ref, o_ref):
        del after_ref
        o_ref[...] = _tn(l_ref[0] if len(l_ref.shape) == 3 else l_ref[...], r_ref[...])

    return _pcall(body, grid=(n_steps,), out_shape=jax.ShapeDtypeStruct((out_rows, n), F32),
                  in_specs=[lhs_spec, pl.BlockSpec((t, n), lambda j: (0, 0)), ANY],
                  out_specs=pl.BlockSpec((SLAB, n), out_index), name=name,
                  compiler_params=_params(48, ("parallel",)))(lhs, rhs, after)


def _grad_rows_2d(lhs, rhs, name, after):
    t, k = lhs.shape
    return _tn_matmul(lhs, rhs, pl.BlockSpec((t, SLAB), lambda j: (0, j)), k // SLAB, k, lambda j: (j, 0), name, after)


def _w_row_block(j):
    return (j + N_QKV) % N_SLAB


def _dp_slab(j):
    return jnp.where(j < N_REST, j, j + 2)


def _grad_w_in_t(dproj, h):
    t = h.shape[0]
    return _tn_matmul(dproj, h, pl.BlockSpec((1, t, SLAB), lambda j: (_dp_slab(j), 0, 0)), N_SLAB, D_IN,
                      lambda j: (_w_row_block(j), 0), "grad_w_in", h)


def _grad_h(dproj, w_in_t, gx0, x, ada, after, tm=512):
    t = x.shape[0]
    nbat = ada.shape[0]
    tps = (t // nbat) // tm

    def body(dp_ref, w_ref, gx0_ref, x_ref, ada_ref, after_ref, gx_ref, dss_ref):
        del after_ref
        i = pl.program_id(0)
        dh = None
        for j in range(N_SLAB):
            slab = j if j < N_REST else j + 2
            part = _nn(dp_ref[slab], w_ref[pl.ds(SLAB * ((j + N_QKV) % N_SLAB), SLAB), :])
            dh = part if dh is None else dh + part
        gx_ref[...] = gx0_ref[...] + dh * (1.0 + ada_ref[0, 1:2, :])

        @pl.when((i % tps) == 0)
        def _():
            dss_ref[...] = jnp.zeros_like(dss_ref)

        dss_ref[0, 0] += _part8(dh)
        dss_ref[0, 1] += _part8(dh * x_ref[...])

    return _pcall(
        body, grid=(t // tm,),
        out_shape=(jax.ShapeDtypeStruct((t, D), F32), jax.ShapeDtypeStruct((nbat, 2, 8, D), F32)),
        in_specs=[pl.BlockSpec((DP_SLABS, tm, SLAB), lambda i: (0, i, 0)),
                  pl.BlockSpec((D_IN, D), lambda i: (0, 0), pipeline_mode=pl.Buffered(1)),
                  pl.BlockSpec((tm, D), lambda i: (i, 0)), pl.BlockSpec((tm, D), lambda i: (i, 0)),
                  pl.BlockSpec((1, 3, D), lambda i: (i // tps, 0, 0)), ANY],
        out_specs=(pl.BlockSpec((tm, D), lambda i: (i, 0)),
                   pl.BlockSpec((1, 2, 8, D), lambda i: (i // tps, 0, 0, 0))),
        name="grad_h", compiler_params=_params(60, ("arbitrary",)))(dproj, w_in_t, gx0, x, ada, after)


def _chip(m):
    x, y, _ = _my_position()
    return (x ^ ((m >> 1) & 1), y ^ (m & 1))


def _exchange_siblings(grads, after, name):
    n = len(grads)

    def body(*refs):
        copies = _sibling_copies(refs[:n], refs[n + 1:2 * n + 1], refs[2 * n + 1], refs[2 * n + 2])
        for cp in copies:
            cp.start()
        for cp in copies:
            cp.wait()

    return _pcall(body, out_shape=tuple(_sibling_zones(grads)), in_specs=[ANY] * (n + 1), out_specs=(ANY,) * n,
                  name=name, scratch_shapes=[pltpu.SemaphoreType.DMA((4 * n,))] * 2)(*grads, after)


def _sibling_zones(grads):
    return [jax.ShapeDtypeStruct((4, g.shape[0] // N_DEV, g.shape[1]), g.dtype) for g in grads]


def _sibling_copies(srcs, lands, send_sems, recv_sems):
    x, y, c = _my_position()
    copies = []
    for a, (src, land) in enumerate(zip(srcs, lands)):
        rows = land.shape[1]
        for m in range(4):
            dev = _flat(*_chip(m), 1 - c)
            copies.append(pltpu.make_async_remote_copy(
                src_ref=src.at[pl.ds(pl.multiple_of(dev * rows, 8), rows), :], dst_ref=land.at[m],
                send_sem=send_sems.at[4 * a + m], recv_sem=recv_sems.at[4 * a + m], device_id=(x, y, 1 - c),
                device_id_type=MESH))
    return copies


def _chip_copies(srcs, lands, send_sems, recv_sems):
    _, _, c = _my_position()
    return [pltpu.make_async_remote_copy(
        src_ref=srcs[a].at[m - 1], dst_ref=lands[a].at[m - 1], send_sem=send_sems.at[3 * a + m - 1],
        recv_sem=recv_sems.at[3 * a + m - 1], device_id=(*_chip(m), c), device_id_type=MESH)
        for a in range(len(srcs)) for m in range(1, 4)]


HBM = pl.BlockSpec(memory_space=pltpu.HBM)
SEM = pl.BlockSpec(memory_space=pltpu.SEMAPHORE)
SPLIT_COPY = pltpu.CompilerParams(has_side_effects=pltpu.SideEffectType.DATAFLOW_SIDE_EFFECTING)


def _start_copies(make_copies, n_sems, srcs, zones, name):
    n = len(srcs)

    def body(*refs):
        for cp in make_copies(refs[:n], refs[n:2 * n], refs[2 * n], refs[2 * n + 1]):
            cp.start()
        refs[-1][...] = jnp.zeros_like(refs[-1])

    hbm = tuple(pltpu.HBM(b.shape, b.dtype) for b in list(srcs) + list(zones))
    out_shape = (pltpu.SemaphoreType.DMA((n_sems,)), pltpu.SemaphoreType.DMA((n_sems,))) + hbm + (jax.ShapeDtypeStruct((8, 128), F32),)
    operands = [pltpu.with_memory_space_constraint(b, pltpu.HBM) for b in srcs]
    operands += [pltpu.with_memory_space_constraint(lax.empty(z.shape, z.dtype), pltpu.HBM) for z in zones]
    res = _pcall(body, out_shape=out_shape, in_specs=[HBM] * (2 * n), out_specs=(SEM, SEM) + (HBM,) * (2 * n) + (VMEM,),
                 input_output_aliases={i: 2 + i for i in range(2 * n)}, name=name, compiler_params=SPLIT_COPY)(*operands)
    return (res[0], res[1], res[2:2 + n], res[2 + n:2 + 2 * n]), res[-1]


def _wait_copies(make_copies, flight, after, name):
    send_sems, recv_sems, srcs, zones = flight
    n = len(srcs)

    def body(*refs):
        for cp in make_copies(refs[:n], refs[n:2 * n], refs[2 * n], refs[2 * n + 1]):
            cp.wait_send()
            cp.wait_recv()

    hbm = tuple(pltpu.HBM(b.shape, b.dtype) for b in list(srcs) + list(zones))
    res = _pcall(body, out_shape=hbm, in_specs=[HBM] * (2 * n) + [SEM, SEM, ANY], out_specs=(HBM,) * (2 * n),
                 input_output_aliases={i: i for i in range(2 * n)}, name=name, compiler_params=SPLIT_COPY)(
        *srcs, *zones, send_sems, recv_sems, after)
    return res[:n], res[n:]


def _pair_sums(devs, grads, lands, n_steps, name):
    n = len(grads)
    rows = [l.shape[1] for l in lands]
    rbs = [r // n_steps for r in rows]

    def body(devs_ref, *refs):
        del devs_ref
        g_refs, land_refs, outs = refs[:4 * n], refs[4 * n:5 * n], refs[5 * n:]
        for a in range(n):
            outs[2 * a][...] = g_refs[4 * a][...] + land_refs[a][0]
            for m in range(1, 4):
                outs[2 * a + 1][m - 1] = (g_refs[4 * a + m][...] + land_refs[a][m]).astype(BF16)

    def block_of(m, per_dev):
        return lambda i, devs_ref: (devs_ref[m] * per_dev + i, 0)

    in_specs = [pl.BlockSpec((rb, l.shape[2]), block_of(m, n_steps)) for rb, l in zip(rbs, lands) for m in range(4)]
    in_specs += [pl.BlockSpec((4, rb, l.shape[2]), lambda i, devs_ref: (0, i, 0)) for rb, l in zip(rbs, lands)]
    out_shape, out_specs = [], []
    for rb, l in zip(rbs, lands):
        out_shape += [jax.ShapeDtypeStruct(l.shape[1:], F32), jax.ShapeDtypeStruct((3,) + l.shape[1:], BF16)]
        out_specs += [pl.BlockSpec((rb, l.shape[2]), lambda i, devs_ref: (i, 0)),
                      pl.BlockSpec((3, rb, l.shape[2]), lambda i, devs_ref: (0, i, 0))]
    grid_spec = pltpu.PrefetchScalarGridSpec(num_scalar_prefetch=1, grid=(n_steps,), in_specs=in_specs, out_specs=tuple(out_specs))
    res = _pcall(body, grid_spec=grid_spec, out_shape=tuple(out_shape), name=name,
                 compiler_params=_params(48, ("parallel",)))(devs, *[g for g in grads for _ in range(4)], *lands)
    return res[0::2], res[1::2]


def _final_sums(mine, lands, n_steps, name):
    n = len(mine)
    rbs = [o.shape[0] // n_steps for o in mine]

    def body(*refs):
        mine_refs, land_refs, outs = refs[:n], refs[n:2 * n], refs[2 * n:]
        for a in range(n):
            tot = mine_refs[a][...]
            for m in range(3):
                tot = tot + land_refs[a][m].astype(F32)
            outs[a][...] = tot

    in_specs = ([pl.BlockSpec((rb, o.shape[1]), lambda i: (i, 0)) for rb, o in zip(rbs, mine)]
                + [pl.BlockSpec((3, rb, o.shape[1]), lambda i: (0, i, 0)) for rb, o in zip(rbs, mine)])
    out_specs = tuple(pl.BlockSpec((rb, o.shape[1]), lambda i: (i, 0)) for rb, o in zip(rbs, mine))
    out_shape = tuple(jax.ShapeDtypeStruct(o.shape, F32) for o in mine)
    return _pcall(body, grid=(n_steps,), out_shape=out_shape, in_specs=in_specs, out_specs=out_specs, name=name,
                  compiler_params=_params(32, ("parallel",)))(*mine, *lands)


def _reduce_scatter_begin(big, small_after_start):
    c = lax.axis_index("c")
    devs = jnp.stack([_flat(*_chip(m), c) for m in range(4)]).astype(jnp.int32)
    flight, token = _start_copies(_sibling_copies, 4, [big], _sibling_zones([big]), "siblings_start")
    small = small_after_start(token)
    (big,), big_lands = _wait_copies(_sibling_copies, flight, small[-1], "siblings_wait")
    big_mine, big_send = _pair_sums(devs, [big], big_lands, 4, "pair_sums_w_in")
    big_flight, token = _start_copies(_chip_copies, 3, list(big_send), list(big_send), "chips_start_w_in")
    small_lands = _exchange_siblings(small, token, "exchange_siblings_rest")
    small_mine, small_send = _pair_sums(devs, small, small_lands, 1, "pair_sums_rest")
    small_flight, token = _start_copies(_chip_copies, 3 * len(small), list(small_send), list(small_send), "chips_start_rest")
    return (big_flight, small_flight, list(big_mine) + list(small_mine)), token


def _reduce_scatter_end(state, after):
    big_flight, small_flight, mine = state
    _, big_got = _wait_copies(_chip_copies, big_flight, after, "chips_wait_w_in")
    _, small_got = _wait_copies(_chip_copies, small_flight, after, "chips_wait_rest")
    small = _final_sums(mine[1:], small_got, 1, "final_sums_rest")
    return (mine[0], big_got[0]), list(small)


def _adamw(w, g, m, v):
    m_new = B1 * m + (1.0 - B1) * g
    v_new = B2 * v + (1.0 - B2) * (g * g)
    m_hat = m_new / (1.0 - B1 ** STEP)
    v_hat = v_new / (1.0 - B2 ** STEP)
    delta = -LR * (m_hat / (jnp.sqrt(v_hat) + EPS) + WD * w)
    return delta, m_new, v_new


def _final_sum_adam_rows(mine, land, w, m, v, n_steps, name):
    rows, ncol = w.shape
    blk = pl.BlockSpec((rows // n_steps, ncol), lambda i: (i, 0))

    def body(mine_ref, land_ref, w_ref, m_ref, v_ref, g_ref, d_ref, mo_ref, vo_ref):
        g = mine_ref[...]
        for k in range(3):
            g = g + land_ref[k].astype(F32)
        g_ref[...] = g
        d_ref[...], mo_ref[...], vo_ref[...] = _adamw(w_ref[...], g, m_ref[...], v_ref[...])

    shape = jax.ShapeDtypeStruct(w.shape, F32)
    return _pcall(body, grid=(n_steps,), out_shape=(shape,) * 4,
                  in_specs=[blk, pl.BlockSpec((3, rows // n_steps, ncol), lambda i: (0, i, 0)), blk, blk, blk],
                  out_specs=(blk,) * 4, name=name, compiler_params=_params(32, ("parallel",)))(mine, land, w, m, v)


def _adam_transposed(g_t, w, m, v, name):
    n, k = g_t.shape
    rb = min(k, 128)

    def body(gt_ref, w_ref, m_ref, v_ref, g_ref, d_ref, mo_ref, vo_ref):
        for src, skip, dst, size in _column_chunks(n):
            sl = pl.ds(dst, size)
            g = gt_ref[pl.ds(src, 128), :].T[:, skip:]
            delta, m_new, v_new = _adamw(w_ref[:, sl], g, m_ref[:, sl], v_ref[:, sl])
            g_ref[:, sl], d_ref[:, sl], mo_ref[:, sl], vo_ref[:, sl] = g, delta, m_new, v_new

    shape = jax.ShapeDtypeStruct(w.shape, F32)
    rows = pl.BlockSpec((rb, n), lambda i: (i, 0))
    return _pcall(body, grid=(k // rb,), out_shape=(shape,) * 4,
                  in_specs=[pl.BlockSpec((n, rb), lambda i: (0, i)), rows, rows, rows], out_specs=(rows,) * 4, name=name,
                  compiler_params=_params(32, ("parallel",)))(g_t, w, m, v)


def _adam_many(items, name):
    n = len(items)

    def body(*refs):
        ins, outs = refs[:4 * n], refs[4 * n:]
        for a in range(n):
            w_ref, g_ref, m_ref, v_ref = ins[4 * a:4 * a + 4]
            delta, m_new, v_new = _adamw(w_ref[...], g_ref[...], m_ref[...], v_ref[...])
            outs[3 * a][...], outs[3 * a + 1][...], outs[3 * a + 2][...] = delta, m_new, v_new

    out_shape = tuple(jax.ShapeDtypeStruct(it[0].shape, F32) for it in items for _ in range(3))
    flat = [arr for it in items for arr in it]
    res = _pcall(body, grid=(1,), out_shape=out_shape, in_specs=[_whole(a) for a in flat],
                 out_specs=tuple(_whole(o) for o in out_shape), name=name, compiler_params=_params(32))(*flat)
    return [tuple(res[3 * a:3 * a + 3]) for a in range(n)]


def _adam_w_ada(cact_all, dada_mine, w, m, v):
    def body(c_ref, d_ref, w_ref, m_ref, v_ref, g_ref, dl_ref, mo_ref, vo_ref):
        g = _tn(c_ref[...].astype(BF16), d_ref[...].astype(BF16))
        delta, m_new, v_new = _adamw(w_ref[...], g, m_ref[...], v_ref[...])
        g_ref[...], dl_ref[...], mo_ref[...], vo_ref[...] = g, delta, m_new, v_new

    shape = jax.ShapeDtypeStruct(w.shape, F32)
    operands = (cact_all, dada_mine, w, m, v)
    return _pcall(body, grid=(1,), out_shape=(shape,) * 4, in_specs=[_whole(a) for a in operands],
                  out_specs=(_whole(w),) * 4, name="adam_w_ada", compiler_params=_params(32))(*operands)


def kernel(x, c, w_ada, b_ada, w_in, b_in, conv_w, w_proj_attn, w_proj_conv, w_out, b_out, ln_g, ln_b, loss_target, m_w_ada, m_b_ada, m_w_in, m_b_in, m_conv_w, m_w_proj_attn, m_w_proj_conv, m_w_out, m_b_out, m_ln_g, m_ln_b, v_w_ada, v_b_ada, v_w_in, v_b_in, v_conv_w, v_w_proj_attn, v_w_proj_conv, v_w_out, v_b_out, v_ln_g, v_ln_b):
    nbat, seq, _ = x.shape
    t = nbat * seq
    me = _flat(*_my_position())
    x2, tgt2 = x.reshape(t, D), loss_target.reshape(t, D)
    sq = lambda a: a.reshape(a.shape[1:])

    tr = lambda a: a[0].T
    w_in_rows = tr(w_in)
    w_in_t_s = _cast_rows(w_in_rows, 4, "cast_w_in")
    w_pa_t_s, w_pb_s, w_out_s, cact_s, cw_s = _prep(sq(w_proj_attn), sq(w_proj_conv), sq(w_out), c, sq(conv_w))

    ncol = w_ada.shape[2]
    b_ada_mine = lax.dynamic_slice(b_ada, (0, me * ncol), (1, ncol))
    ada_slots, cact_slots, cw_slots = _ada_forward(cact_s, cw_s, sq(w_ada), b_ada_mine)
    cact_all = cact_slots[:, :nbat].reshape(N_DEV * nbat, D)
    cw = cw_slots[:, :3].transpose(1, 0, 2).reshape(3, D)
    ada_all = ada_slots[:, :, :nbat].transpose(1, 2, 0, 3).reshape(N_DEV * nbat, 3, D)
    ada = lax.dynamic_slice(ada_all, (me * nbat, 0, 0), (nbat, 3, D))

    w_in_t, qkv, rest, h, (w_pa_t, w_pb, w_o) = _project_gather(
        w_in_t_s, x2, ada, b_in.reshape(N_SLAB, 1, SLAB), [w_pa_t_s, w_pb_s, w_out_s])
    ol_tot = _attn_forward(qkv, nbat)
    (dproj, gx0, do_attn, merged, do_f, bbs, dyc, a_bf, dya, gb_rest, svec, dgate) = _mid(
        rest, ol_tot, x2, tgt2, ada, cw, b_out, ln_g, ln_b, w_pa_t, w_pb, w_o)

    gb_qkv = []
    for g in range(3):
        dproj, gb = _attn_backward(qkv, do_attn, ol_tot, dproj, g, nbat)
        gb_qkv.append(gb)
    g_w_in_t = _grad_w_in_t(dproj, h)

    def small_grads(token):
        g_w_out = _grad_rows_2d(merged, do_f, "grad_w_out", token)
        g_w_pb = _grad_rows_2d(bbs, dyc, "grad_w_proj_conv", g_w_out)
        g_w_pa_t = _grad_rows_2d(dya, a_bf, "grad_w_proj_attn", g_w_pb)
        return [g_w_out, g_w_pb, g_w_pa_t]

    rs_state, token = _reduce_scatter_begin(g_w_in_t, small_grads)
    grad_x, dss = _grad_h(dproj, w_in_t, gx0, x2, ada, token)

    rows8, tot, g_bada = _small_reduce(gb_rest, gb_qkv, svec, dgate, dss)
    (g_in_mine, g_in_got), (g_out, g_pb, g_pa_t) = _reduce_scatter_end(rs_state, tot)
    loss = tot[0, P_LOSS]
    dada_all = rows8[:, 0, P_DADA:].reshape(N_DEV * nbat, 3 * D)
    dada_mine = lax.dynamic_slice(dada_all, (0, me * ncol), (N_DEV * nbat, ncol))

    g_in_t, d_win_t, nm_win_t, nv_win_t = _final_sum_adam_rows(g_in_mine, g_in_got, w_in_rows, tr(m_w_in), tr(v_w_in), 4, "adam_w_in")
    g_win, d_win, nm_win, nv_win = g_in_t.T, d_win_t.T, nm_win_t.T, nv_win_t.T
    g_wpa, d_wpa, nm_wpa, nv_wpa = _adam_transposed(g_pa_t, sq(w_proj_attn), sq(m_w_proj_attn), sq(v_w_proj_attn), "adam_w_proj_attn")
    g_wada, d_wada, nm_wada, nv_wada = _adam_w_ada(cact_all, dada_mine, sq(w_ada), sq(m_w_ada), sq(v_w_ada))
    g_bin = tot[:, P_BIN:P_BIN + D_IN]
    g_bout = tot[:, P_BOUT:P_BOUT + D]
    g_lng = tot[:, P_LNG:P_LNG + D]
    g_lnb = tot[:, P_LNB:P_LNB + D]
    g_conv = lax.dynamic_slice(tot[:, P_CONV:P_CONV + 3 * D].reshape(3, D), (0, me * cw_s.shape[1]), (3, cw_s.shape[1]))
    upd = _adam_many([
        (sq(w_proj_conv), g_pb, sq(m_w_proj_conv), sq(v_w_proj_conv)),
        (sq(w_out), g_out, sq(m_w_out), sq(v_w_out)),
        (b_ada, g_bada, m_b_ada, v_b_ada), (b_in, g_bin, m_b_in, v_b_in), (sq(conv_w), g_conv, sq(m_conv_w), sq(v_conv_w)),
        (b_out, g_bout, m_b_out, v_b_out), (ln_g, g_lng, m_ln_g, v_ln_g), (ln_b, g_lnb, m_ln_b, v_ln_b)], "adam_rest")
    (d_wpb, nm_wpb, nv_wpb), (d_wout, nm_wout, nv_wout), (d_bada, nm_bada, nv_bada), (d_bin, nm_bin, nv_bin), \
        (d_conv, nm_conv, nv_conv), (d_bout, nm_bout, nv_bout), (d_lng, nm_lng, nv_lng), (d_lnb, nm_lnb, nv_lnb) = upd

    ex = lambda a: a.reshape((1,) + a.shape)
    grads = [ex(g_wada), g_bada, ex(g_win), g_bin, ex(g_conv), ex(g_wpa), ex(g_pb), ex(g_out), g_bout, g_lng, g_lnb]
    deltas = [ex(d_wada), d_bada, ex(d_win), d_bin, ex(d_conv), ex(d_wpa), ex(d_wpb), ex(d_wout), d_bout, d_lng, d_lnb]
    new_m = [ex(nm_wada), nm_bada, ex(nm_win), nm_bin, ex(nm_conv), ex(nm_wpa), ex(nm_wpb), ex(nm_wout), nm_bout, nm_lng, nm_lnb]
    new_v = [ex(nv_wada), nv_bada, ex(nv_win), nv_bin, ex(nv_conv), ex(nv_wpa), ex(nv_wpb), ex(nv_wout), nv_bout, nv_lng, nv_lnb]
    return (loss, grad_x.reshape(x.shape), *grads, *deltas, *new_m, *new_v)
```

```python
import functools

import jax
import jax.numpy as jnp
from jax import lax
from jax.experimental import pallas as pl
from jax.experimental.pallas import tpu as pltpu

F32, BF16 = jnp.float32, jnp.bfloat16
MESH = pl.DeviceIdType.MESH
N_DEV = 8
D = 1024
SLAB = 256
N_QKV, N_REST = 9, 25
N_SLAB = N_QKV + N_REST
D_IN = N_SLAB * SLAB
DP_SLABS = 36
BLK = 128
GROUPS = ((128, 1), (512, 4), (2048, 16))
ALPHA = 2.0 ** 0.25
LN_EPS = 1e-5
LR, B1, B2, EPS, WD, STEP = 0.001, 0.9, 0.999, 1e-08, 0.01, 10
R_ZA, R_UX, R_GB, R_GC, R_ZC, R_GA, R_GBM = 0, 1, 5, 9, 13, 17, 21
P_BIN, P_BOUT, P_LNG, P_LNB, P_CONV, P_LOSS, P_DADA = 0, 8704, 9728, 10752, 11776, 14848, 14976
MIB = 1024 * 1024


def _pcall(body, *, out_shape, out_specs=None, **kw):
    def pin_out(shape, spec):
        blocked = isinstance(shape, jax.ShapeDtypeStruct) and getattr(spec, "block_shape", None) is not None
        return pltpu.HBM(shape.shape, shape.dtype) if blocked else shape

    n_scalar = 0
    if out_specs is None:
        specs = kw["grid_spec"].out_specs
        n_scalar = kw["grid_spec"].num_scalar_prefetch
    else:
        kw["out_specs"] = specs = out_specs
    if isinstance(out_shape, (tuple, list)):
        out_shape = tuple(pin_out(s, p) for s, p in zip(out_shape, specs))
    else:
        out_shape = pin_out(out_shape, specs)
    call = pl.pallas_call(body, out_shape=out_shape, **kw)

    def run(*operands):
        def pin(o):
            is_data = jnp.issubdtype(o.dtype, jnp.floating) or jnp.issubdtype(o.dtype, jnp.integer)
            return pltpu.with_memory_space_constraint(o, pltpu.HBM) if is_data else o
        return call(*operands[:n_scalar], *[pin(o) for o in operands[n_scalar:]])

    return run

ANY = pl.BlockSpec(memory_space=pl.ANY)
VMEM = pl.BlockSpec(memory_space=pltpu.VMEM)


def _whole(a):
    return pl.BlockSpec(a.shape, lambda i: (0,) * len(a.shape))


def _params(vmem_mib=None, sem=None):
    kw = {}
    if vmem_mib is not None:
        kw["vmem_limit_bytes"] = vmem_mib * MIB
    if sem is not None:
        kw["dimension_semantics"] = sem
    return pltpu.CompilerParams(**kw)


def _nn(a, b):
    return jnp.dot(a, b, preferred_element_type=F32)


def _nt(a, b):
    return lax.dot_general(a, b, (((1,), (1,)), ((), ())), preferred_element_type=F32)


def _tn(a, b):
    return lax.dot_general(a, b, (((0,), (0,)), ((), ())), preferred_element_type=F32)


def _sigmoid(v):
    return 0.5 * jnp.tanh(0.5 * v) + 0.5


def _part8(v):
    return v.reshape(v.shape[0] // 8, 8, v.shape[1]).sum(axis=0)


def _my_position():
    return lax.axis_index("x"), lax.axis_index("y"), lax.axis_index("c")


def _flat(px, py, pc):
    return 4 * px + 2 * py + pc


def _peer(mask):
    x, y, c = _my_position()
    return (x ^ ((mask >> 2) & 1), y ^ ((mask >> 1) & 1), c ^ (mask & 1))


def _column_chunks(n):
    chunks = [(128 * a, 0, 128 * a, 128) for a in range(n // 128)]
    if n % 128:
        chunks.append((n - 128, 128 - n % 128, 128 * (n // 128), n % 128))
    return chunks


def _cast_rows(w, n_steps, name):
    rows, ncol = w.shape
    blk = pl.BlockSpec((rows // n_steps, ncol), lambda i: (i, 0))

    def body(w_ref, o_ref):
        o_ref[...] = w_ref[...].astype(BF16)

    return _pcall(body, grid=(n_steps,), out_shape=jax.ShapeDtypeStruct(w.shape, BF16), in_specs=[blk], out_specs=blk,
                  name=name, compiler_params=_params(16, ("parallel",)))(w)


def _prep(w_pa, w_pb, w_out, c, conv_w):
    def body(wpa_ref, wpb_ref, wout_ref, c_ref, cw_ref, wpat_ref, wpb_o, wout_o, cact_ref, cwp_ref):
        wpat_ref[...] = wpa_ref[...].T.astype(BF16)
        wpb_o[...] = wpb_ref[...].astype(BF16)
        wout_o[...] = wout_ref[...].astype(BF16)
        cv = c_ref[...]
        cact_ref[...] = jnp.zeros_like(cact_ref)
        cact_ref[pl.ds(0, cv.shape[0]), :] = cv * _sigmoid(cv)
        cwp_ref[...] = jnp.zeros_like(cwp_ref)
        cwp_ref[pl.ds(0, 3), :] = cw_ref[...]

    out_shape = (jax.ShapeDtypeStruct((w_pa.shape[1], w_pa.shape[0]), BF16),
                 jax.ShapeDtypeStruct(w_pb.shape, BF16), jax.ShapeDtypeStruct(w_out.shape, BF16),
                 jax.ShapeDtypeStruct((8, D), F32), jax.ShapeDtypeStruct((8, conv_w.shape[1]), F32))
    operands = (w_pa, w_pb, w_out, c, conv_w)
    return _pcall(body, grid=(1,), out_shape=out_shape, in_specs=[_whole(a) for a in operands],
                  out_specs=tuple(_whole(o) for o in out_shape), name="prep", compiler_params=_params(16))(*operands)


def _exchange_slots(out_refs, send_sems, recv_sems, base=0):
    me = _flat(*_my_position())

    def copy(a, mask, slot):
        return pltpu.make_async_remote_copy(
            src_ref=out_refs[a].at[slot], dst_ref=out_refs[a].at[slot], send_sem=send_sems.at[base + 7 * a + mask - 1],
            recv_sem=recv_sems.at[base + 7 * a + mask - 1], device_id=_peer(mask), device_id_type=MESH)

    pairs = [(a, mask) for a in range(len(out_refs)) for mask in range(1, N_DEV)]
    for a, mask in pairs:
        copy(a, mask, me).start()
    for a, mask in pairs:
        copy(a, mask, _flat(*_peer(mask))).wait_recv()
    for a, mask in pairs:
        copy(a, mask, me).wait_send()


def _ada_forward(cact_mine, cw_mine, w_ada, b_ada_mine):
    ncol = w_ada.shape[1]

    def body(c_ref, cw_ref, w_ref, b_ref, out_ref, call_ref, cwall_ref, send_sems, recv_sems):
        me = _flat(*_my_position())
        call_ref[me] = c_ref[...]
        cwall_ref[me] = cw_ref[...]
        _exchange_slots([call_ref, cwall_ref], send_sems, recv_sems)
        c_all = call_ref[...].reshape(N_DEV * 8, D).astype(BF16)
        out_ref[me] = (_nn(c_all, w_ref[...].astype(BF16)) + b_ref[...]).reshape(N_DEV, 8, ncol)
        _exchange_slots([out_ref], send_sems, recv_sems, base=14)

    operands = (cact_mine, cw_mine, w_ada, b_ada_mine)
    out_shape = (jax.ShapeDtypeStruct((N_DEV, N_DEV, 8, ncol), F32), jax.ShapeDtypeStruct((N_DEV, 8, D), F32),
                 jax.ShapeDtypeStruct((N_DEV,) + cw_mine.shape, F32))
    return _pcall(body, grid=(1,), out_shape=out_shape, in_specs=[_whole(a) for a in operands], out_specs=(VMEM,) * 3,
                  scratch_shapes=[pltpu.SemaphoreType.DMA((21,)), pltpu.SemaphoreType.DMA((21,))], name="ada_forward",
                  compiler_params=_params(16))(*operands)


def _small_reduce(gb_rest, gb_qkv, svec, dgate, dss):
    nbat = dgate.shape[0]

    def body(gbr_ref, q0_ref, q1_ref, q2_ref, sv_ref, dg_ref, dss_ref, rows_ref, tot_ref, gbada_ref, send_sems, recv_sems):
        me = _flat(*_my_position())

        def put(off, v):
            rows_ref[me, :, pl.ds(off, v.shape[1])] = v

        def row(v):
            return jnp.sum(v, axis=0, keepdims=True)

        for g, q_ref in enumerate((q0_ref, q1_ref, q2_ref)):
            for which in range(3):
                put(P_BIN + SLAB * (3 * which + g), row(q_ref[which]))
        for s in range(N_REST):
            put(P_BIN + SLAB * (N_QKV + s), row(gbr_ref[s]))
        put(P_LNG, row(sv_ref[0]))
        put(P_LNB, row(sv_ref[1]))
        put(P_BOUT, row(sv_ref[2]))
        for j in range(3):
            put(P_CONV + D * j, row(sv_ref[3 + j]))
        loss = (0.5 / D) * jnp.sum(row(sv_ref[6]), axis=1, keepdims=True)
        put(P_LOSS, jnp.broadcast_to(loss, (1, 128)))
        for b in range(nbat):
            put(P_DADA + 3 * D * b, row(dss_ref[b, 0]))
            put(P_DADA + 3 * D * b + D, row(dss_ref[b, 1]))
            put(P_DADA + 3 * D * b + 2 * D, row(dg_ref[b]))
        _exchange_slots([rows_ref], send_sems, recv_sems)
        tot = rows_ref[0]
        for k in range(1, N_DEV):
            tot = tot + rows_ref[k]
        tot_ref[...] = tot
        gbada = tot[:, P_DADA:P_DADA + 3 * D]
        for b in range(1, nbat):
            gbada = gbada + tot[:, P_DADA + 3 * D * b:P_DADA + 3 * D * (b + 1)]
        gbada_ref[...] = gbada

    p_len = P_DADA + nbat * 3 * D
    out_shape = (jax.ShapeDtypeStruct((N_DEV, 1, p_len), F32), jax.ShapeDtypeStruct((1, p_len), F32),
                 jax.ShapeDtypeStruct((1, 3 * D), F32))
    operands = (gb_rest, *gb_qkv, svec, dgate, dss)
    return _pcall(body, grid=(1,), out_shape=out_shape, in_specs=[_whole(a) for a in operands],
                  out_specs=(VMEM, _whole(out_shape[1]), _whole(out_shape[2])),
                  scratch_shapes=[pltpu.SemaphoreType.DMA((7,)), pltpu.SemaphoreType.DMA((7,))], name="small_reduce",
                  compiler_params=_params(16))(*operands)


PIECE = 64
N_CHUNK = 4
ARRIVAL_RANK = (0, 1, 3, 5, 2, 4, 6, 7)
SLOT_MASK = (1, 4, 2, 6, 5, 3, 7)


def _arrival_tables(shard_rows):
    import numpy as np
    crow = shard_rows // N_CHUNK
    table = np.zeros((N_DEV, N_SLAB + 7 * N_CHUNK), np.int32)
    lo = [(SLAB * j) // crow for j in range(N_SLAB)]
    hi = [(SLAB * j + SLAB - 1) // crow for j in range(N_SLAB)]
    for k in range(N_DEV):
        def rank(chunk):
            shard_rank = ARRIVAL_RANK[(chunk // N_CHUNK) ^ k]
            return shard_rank if shard_rank < 2 else 2 + 8 * (chunk % N_CHUNK) + shard_rank
        order = sorted(range(N_SLAB), key=lambda j: (max(rank(lo[j]), rank(hi[j])), j))
        table[k, :N_SLAB] = order
        for slot, mask in enumerate(SLOT_MASK):
            for ch in range(N_CHUNK):
                chunk = (k ^ mask) * N_CHUNK + ch
                table[k, N_SLAB + slot * N_CHUNK + ch] = min(t for t, j in enumerate(order) if lo[j] <= chunk <= hi[j])
    return table


def _project_gather(shard, x, ada, b_in3, others, xt=512):
    t = x.shape[0]
    n_o = len(others)
    srows = shard.shape[0]
    crow = srows // N_CHUNK
    shards = [shard] + list(others)
    table = jnp.asarray(_arrival_tables(srows))
    seq_tiles = (t // ada.shape[0]) // xt

    def body(tbl_ref, *refs):
        srcs = [refs[0]] + list(refs[4:4 + n_o])
        x_ref, ada_ref, b_ref = refs[1], refs[2], refs[3]
        outs = [refs[4 + n_o]] + list(refs[8 + n_o:8 + 2 * n_o])
        qkv_ref, rest_ref, h_out = refs[5 + n_o], refs[6 + n_o], refs[7 + n_o]
        (wtile, obf, of32, h_ref, xbuf, send_sems, recv_sems, local_sems, tile_sems, obf_sems, of32_sems, x_sems,
         h_sems) = refs[8 + 2 * n_o:]
        w_full = outs[0]
        x, y, c = _my_position()
        k = _flat(x, y, c)
        me, sibling = (x, y, c), (x, y, 1 - c)
        chips = [(1 - x, y), (x, 1 - y), (1 - x, 1 - y)]

        def rows(a, px, py, pc, ch):
            r = shards[a].shape[0]
            if ch is None:
                return outs[a].at[pl.ds(pl.multiple_of(_flat(px, py, pc) * r, r), r), :]
            return outs[a].at[pl.ds(pl.multiple_of(_flat(px, py, pc) * r + ch * crow, crow), crow), :]

        def copy(a, slot, block, to, ch=None, src=None):
            sem = slot * N_CHUNK + ch if a == 0 else 7 * (N_CHUNK - 1 + a) + slot
            if src is not None and ch is not None:
                src = src.at[pl.ds(ch * crow, crow), :]
            return pltpu.make_async_remote_copy(
                src_ref=rows(a, *block, ch) if src is None else src, dst_ref=rows(a, *block, ch),
                send_sem=send_sems.at[sem], recv_sem=recv_sems.at[sem], device_id=to, device_id_type=MESH)

        mine = [pltpu.make_async_copy(srcs[a], rows(a, *me, None), local_sems.at[a]) for a in range(1 + n_o)]
        first = []
        for ch in range(N_CHUNK):
            first.append(copy(0, 0, me, sibling, ch, src=srcs[0]))
            first += [copy(0, 1 + j, me, (*chip, c), ch, src=srcs[0]) for j, chip in enumerate(chips)]
        for a in range(1, 1 + n_o):
            first.append(copy(a, 0, me, sibling, src=srcs[a]))
            first += [copy(a, 1 + j, me, (*chip, c), src=srcs[a]) for j, chip in enumerate(chips)]
        for cp in mine + first:
            cp.start()

        def arrive(a, slot, ch=None):
            if slot == 0:
                copy(a, 0, sibling, me, ch).wait_recv()
            elif slot < 4:
                copy(a, slot, (*chips[slot - 1], c), me, ch).wait_recv()
                copy(a, slot + 3, (*chips[slot - 1], c), sibling, ch).start()
            else:
                copy(a, slot, (*chips[slot - 4], 1 - c), me, ch).wait_recv()

        def arrive_for(step):
            for slot in range(7):
                for ch in range(N_CHUNK):
                    @pl.when(tbl_ref[k, N_SLAB + slot * N_CHUNK + ch] == step)
                    def _():
                        arrive(0, slot, ch)

        def fetch(step, buf):
            slab = tbl_ref[k, step]
            for p in range(SLAB // PIECE):
                g0 = slab * SLAB + PIECE * p
                own = (g0 >= k * srows) & (g0 < (k + 1) * srows)
                dst = wtile.at[buf, pl.ds(PIECE * p, PIECE), :]

                @pl.when(own)
                def _():
                    pltpu.make_async_copy(srcs[0].at[pl.ds(pl.multiple_of(g0 - k * srows, PIECE), PIECE), :], dst, tile_sems.at[buf]).start()

                @pl.when(jnp.logical_not(own))
                def _():
                    pltpu.make_async_copy(w_full.at[pl.ds(pl.multiple_of(g0, PIECE), PIECE), :], dst, tile_sems.at[buf]).start()

        def wait_tile(buf):
            pltpu.make_async_copy(w_full.at[pl.ds(0, SLAB), :], wtile.at[buf], tile_sems.at[buf]).wait()

        def put(buf_ref, sems, dst_ref, count, value):
            b = count % 2

            @pl.when(count >= 2)
            def _():
                pltpu.make_async_copy(buf_ref.at[b], dst_ref, sems.at[b]).wait()

            buf_ref[b] = value
            pltpu.make_async_copy(buf_ref.at[b], dst_ref, sems.at[b]).start()

        def drain(buf_ref, sems, dst_ref, count):
            for back in (1, 2):
                @pl.when(count >= back)
                def _():
                    pltpu.make_async_copy(buf_ref.at[(count - back) % 2], dst_ref, sems.at[(count - back) % 2]).wait()

        def x_copy(i):
            return pltpu.make_async_copy(x_ref.at[pl.ds(xt * i, xt), :], xbuf.at[i % 2], x_sems.at[i % 2])

        def h_copy(i):
            return pltpu.make_async_copy(h_ref.at[pl.ds(xt * i, xt), :], h_out.at[pl.ds(xt * i, xt), :], h_sems.at[i % 2])

        x_copy(0).start()
        for i in range(t // xt):
            if i + 1 < t // xt:
                x_copy(i + 1).start()
            x_copy(i).wait()
            b = i // seq_tiles
            h_ref[pl.ds(xt * i, xt), :] = (xbuf[i % 2] * (1.0 + ada_ref[b, 1:2, :]) + ada_ref[b, 0:1, :]).astype(BF16)
            if i >= 2:
                h_copy(i - 2).wait()
            h_copy(i).start()
        for i in range(max(t // xt - 2, 0), t // xt):
            h_copy(i).wait()

        arrive_for(0)
        fetch(0, 0)

        def step(s, carry):
            n_bf, n_f32 = carry
            buf = s % 2

            @pl.when(s + 1 < N_SLAB)
            def _():
                arrive_for(s + 1)
                fetch(s + 1, 1 - buf)

            wait_tile(buf)
            slab = tbl_ref[k, s]
            v = _nt(h_ref[...], wtile[buf]) + b_ref[slab]
            is_qkv = slab < N_QKV

            @pl.when(is_qkv)
            def _():
                put(obf, obf_sems, qkv_ref.at[jnp.minimum(slab, N_QKV - 1)], n_bf, v.astype(BF16))

            @pl.when(jnp.logical_not(is_qkv))
            def _():
                put(of32, of32_sems, rest_ref.at[jnp.maximum(slab - N_QKV, 0)], n_f32, v)

            return n_bf + is_qkv.astype(jnp.int32), n_f32 + 1 - is_qkv.astype(jnp.int32)

        n_bf, n_f32 = lax.fori_loop(0, N_SLAB, step, (jnp.int32(0), jnp.int32(0)))
        drain(obf, obf_sems, qkv_ref.at[0], n_bf)
        drain(of32, of32_sems, rest_ref.at[0], n_f32)

        for slots in ((1, 2, 3), (0, 4, 5, 6)):
            for a in range(1, 1 + n_o):
                for slot in slots:
                    arrive(a, slot)
        for cp in first:
            cp.wait_send()
        for j, chip in enumerate(chips):
            for ch in range(N_CHUNK):
                copy(0, 4 + j, (*chip, c), sibling, ch).wait_send()
            for a in range(1, 1 + n_o):
                copy(a, 4 + j, (*chip, c), sibling).wait_send()
        for cp in mine:
            cp.wait()

    out_shape = ((jax.ShapeDtypeStruct((N_DEV * srows, D), BF16), jax.ShapeDtypeStruct((N_QKV, t, SLAB), BF16),
                  jax.ShapeDtypeStruct((N_REST, t, SLAB), F32), jax.ShapeDtypeStruct((t, D), BF16))
                 + tuple(jax.ShapeDtypeStruct((N_DEV * o.shape[0], o.shape[1]), o.dtype) for o in others))
    n_all = 1 + n_o
    n_sems = 7 * (N_CHUNK + n_o)
    pair = pltpu.SemaphoreType.DMA((2,))
    grid_spec = pltpu.PrefetchScalarGridSpec(
        num_scalar_prefetch=1, grid=(1,),
        in_specs=[ANY, ANY, pl.BlockSpec(ada.shape, lambda i, tbl: (0, 0, 0)),
                  pl.BlockSpec((N_SLAB, 1, SLAB), lambda i, tbl: (0, 0, 0))] + [ANY] * n_o,
        out_specs=(ANY,) * (4 + n_o),
        scratch_shapes=[pltpu.VMEM((2, SLAB, D), BF16), pltpu.VMEM((2, t, SLAB), BF16), pltpu.VMEM((2, t, SLAB), F32),
                        pltpu.VMEM((t, D), BF16), pltpu.VMEM((2, xt, D), F32),
                        pltpu.SemaphoreType.DMA((n_sems,)), pltpu.SemaphoreType.DMA((n_sems,)),
                        pltpu.SemaphoreType.DMA((n_all,)), pair, pair, pair, pair, pair])
    res = _pcall(body, grid_spec=grid_spec, out_shape=out_shape, name="project_gather",
                 compiler_params=_params(48, ("arbitrary",)))(table, shard, x, ada, b_in3, *others)
    return res[0], res[1], res[2], res[3], list(res[4:])


def _bias_tables(g):
    window, dil = GROUPS[g]
    span = window // dil
    qi = jnp.arange(BLK)[:, None]
    kj = jnp.arange(2 * BLK)[None, :]
    delta = qi + BLK - kj
    valid = (delta >= 0) & (delta <= span)
    heads = jnp.arange(4, dtype=F32) + 4.0 * g
    slopes = 2.0 ** (-8.0 * (heads + 1.0) / 12.0)
    bias = -slopes[:, None, None] * (delta * dil).astype(F32)[None]
    return jnp.where(valid[None], bias, -1e30).reshape(4 * BLK, 2 * BLK)


def _head_masks(shape):
    lane = lax.broadcasted_iota(jnp.int32, shape, 1)
    return [(lane >= 64 * h) & (lane < 64 * (h + 1)) for h in range(4)]


def _stack_heads(v, masks):
    return jnp.concatenate([jnp.where(masks[h], v, jnp.zeros_like(v)) for h in range(4)], axis=0)


def _unstack_heads(v4, masks):
    out = jnp.where(masks[0], v4[0:BLK], 0.0)
    for h in range(1, 4):
        out = jnp.where(masks[h], v4[BLK * h:BLK * (h + 1)], out)
    return out


def _regroup(load_half, dst_ref, stage_ref, n, dil):
    for hlf in range(2):
        stage_ref[hlf] = load_half(hlf)

    def residue(r, carry):
        for hlf in range(2):
            dst_ref[pl.ds(pl.multiple_of(r * n, BLK), n), pl.ds(128 * hlf, 128)] = (
                stage_ref[hlf, pl.ds(r, n, stride=dil), :].astype(dst_ref.dtype))
        return carry

    lax.fori_loop(0, dil, residue, 0)


def _store_block(nat_ref, r, i, val, dil):
    for hlf in range(2):
        nat_ref[hlf, pl.ds(r + dil * BLK * i, BLK, stride=dil), :] = val[:, 128 * hlf:128 * (hlf + 1)]


def _for_blocks(block, dil, nblk):
    if dil == 1:
        block(0, 0, True)
        block(0, 1, False)

        def pair(k, carry):
            block(0, 2 * k, False)
            block(0, 2 * k + 1, False)
            return carry

        lax.fori_loop(1, nblk // 2, pair, 0)
    else:
        def residues(k, carry):
            block(2 * k, 0, True)
            block(2 * k + 1, 0, True)
            if nblk > 1:
                def loop(i, c):
                    block(2 * k, i, False)
                    block(2 * k + 1, i, False)
                    return c
                lax.fori_loop(1, nblk, loop, 0)
            return carry

        lax.fori_loop(0, dil // 2, residues, 0)


def _attn_forward(qkv, nbat):
    t = qkv.shape[1]
    seq = t // nbat
    n_grp = len(GROUPS)

    def body(qkv_ref, b0_ref, b1_ref, b2_ref, ol_ref, stage, qs_ref, ks_ref, vs_ref, *nat):
        masks = _head_masks((BLK, SLAB))
        bias_refs = (b0_ref, b1_ref, b2_ref)
        for g, (_, dil) in enumerate(GROUPS):
            n = seq // dil
            bias_ref, nat_o, nat_l = bias_refs[g], nat[2 * g], nat[2 * g + 1]
            if dil > 1:
                qd, kd, vd = qs_ref, ks_ref, vs_ref
                for which, dst in enumerate((qd, kd, vd)):
                    _regroup(lambda hlf, which=which, g=g: qkv_ref[3 * which + g, :, pl.ds(128 * hlf, 128)].astype(F32), dst, stage, n, dil)
            else:
                qd, kd, vd = qkv_ref.at[g], qkv_ref.at[3 + g], qkv_ref.at[6 + g]

            def block(r, i, first, n=n, dil=dil, qd=qd, kd=kd, vd=vd, bias_ref=bias_ref, nat_o=nat_o, nat_l=nat_l):
                base = r * n
                qs = pl.ds(pl.multiple_of(base + i * BLK, BLK), BLK)
                ks = pl.ds(pl.multiple_of(base, BLK), BLK) if first else pl.ds(pl.multiple_of(base + (i - 1) * BLK, BLK), 2 * BLK)
                q, kk, vv = qd[qs, :], kd[ks, :], vd[ks, :]
                bias = bias_ref[:, pl.ds(BLK, BLK)] if first else bias_ref[...]
                s = _nt(_stack_heads(q, masks), kk) * 0.125 + bias
                m = jnp.max(s, axis=1, keepdims=True)
                p = jnp.exp(s - m)
                den = jnp.sum(p, axis=1, keepdims=True)
                out = _unstack_heads(_nn((p * (1.0 / den)).astype(BF16), vv), masks)
                lse = _unstack_heads(jnp.broadcast_to(m + jnp.log(den), (4 * BLK, SLAB)), masks)
                _store_block(nat_o, r, i, out, dil)
                _store_block(nat_l, r, i, lse, dil)

            _for_blocks(block, dil, n // BLK)

        for hlf in range(2):
            l0, l1, l2 = nat[1][hlf], nat[3][hlf], nat[5][hlf]
            mx = jnp.maximum(jnp.maximum(l0, l1), l2)
            e0, e1, e2 = jnp.exp(l0 - mx), jnp.exp(l1 - mx), jnp.exp(l2 - mx)
            den = e0 + e1 + e2
            ol_ref[0, :, pl.ds(128 * hlf, 128)] = (e0 * nat[0][hlf] + e1 * nat[2][hlf] + e2 * nat[4][hlf]) * (1.0 / den)
            ol_ref[1, :, pl.ds(128 * hlf, 128)] = mx + jnp.log(den)

    halves = pltpu.VMEM((2, seq, 128), F32)
    bias_spec = pl.BlockSpec((4 * BLK, 2 * BLK), lambda b: (0, 0))
    return _pcall(
        body, grid=(nbat,), out_shape=jax.ShapeDtypeStruct((2, t, SLAB), F32),
        in_specs=[pl.BlockSpec((N_QKV, seq, SLAB), lambda b: (0, b, 0))] + [bias_spec] * n_grp,
        out_specs=pl.BlockSpec((2, seq, SLAB), lambda b: (0, b, 0)),
        scratch_shapes=[halves] + [pltpu.VMEM((seq, SLAB), BF16)] * 3 + [halves] * (2 * n_grp),
        name="attn_forward", compiler_params=_params(56, ("parallel",)))(qkv, *[_bias_tables(g) for g in range(n_grp)])


def _attn_backward(qkv, do_attn, ol_tot, dproj, g, nbat):
    t = qkv.shape[1]
    seq = t // nbat
    dil = GROUPS[g][1]
    n = seq // dil
    nblk = n // BLK
    qkv4 = qkv.reshape(3, 3, t, SLAB)
    dp4 = dproj.reshape(DP_SLABS // 3, 3, t, SLAB)

    def body(qkv_ref, do_ref, ol_ref, bias_ref, dp_in, dp_ref, gb_ref, dk_acc, dv_acc, *scratch):
        del dp_in
        masks = _head_masks((BLK, SLAB))

        @pl.when(pl.program_id(0) == 0)
        def _():
            gb_ref[...] = jnp.zeros_like(gb_ref)

        dk_acc[...] = jnp.zeros_like(dk_acc)
        dv_acc[...] = jnp.zeros_like(dv_acc)
        if dil > 1:
            stage, qd, kd, vd, dod, prodd, lsed, nat = scratch
            lanes = lambda hlf: pl.ds(128 * hlf, 128)
            for which, dst in enumerate((qd, kd, vd)):
                _regroup(lambda hlf, which=which: qkv_ref[which, 0, :, lanes(hlf)].astype(F32), dst, stage, n, dil)
            _regroup(lambda hlf: do_ref[:, lanes(hlf)].astype(F32), dod, stage, n, dil)
            _regroup(lambda hlf: do_ref[:, lanes(hlf)].astype(F32) * ol_ref[0, :, lanes(hlf)], prodd, stage, n, dil)
            _regroup(lambda hlf: ol_ref[1, :, lanes(hlf)], lsed, stage, n, dil)
        else:
            qd, kd, vd = qkv_ref.at[0, 0], qkv_ref.at[1, 0], qkv_ref.at[2, 0]

        def block(r, i, first):
            base = r * n
            qs = pl.ds(pl.multiple_of(base + i * BLK, BLK), BLK)
            ks = pl.ds(pl.multiple_of(base, BLK), BLK) if first else pl.ds(pl.multiple_of(base + (i - 1) * BLK, BLK), 2 * BLK)
            q, kk, vv = qd[qs, :], kd[ks, :], vd[ks, :]
            if dil > 1:
                do, prod, lse = dod[qs, :], prodd[qs, :], lsed[qs, :]
            else:
                do = do_ref[qs, :]
                prod = do.astype(F32) * ol_ref[0, qs, :]
                lse = ol_ref[1, qs, :]
            q4, do4 = _stack_heads(q, masks), _stack_heads(do, masks)
            bias = bias_ref[:, pl.ds(BLK, BLK)] if first else bias_ref[...]
            lse4 = jnp.concatenate([lse[:, 64 * h:64 * h + 1] for h in range(4)], axis=0)
            delta4 = jnp.concatenate([jnp.sum(jnp.where(masks[h], prod, 0.0), axis=1, keepdims=True) for h in range(4)], axis=0)
            p = jnp.exp(_nt(q4, kk) * 0.125 + bias - lse4)
            ds = (p * (_nt(do4, vv) - delta4)).astype(BF16)
            dv_acc[ks, :] += _tn(p.astype(BF16), do4)
            dk_acc[ks, :] += _tn(ds, q4) * 0.125
            dq = _unstack_heads(_nn(ds, kk), masks) * 0.125
            if dil > 1:
                _store_block(nat, r, i, dq, dil)
            else:
                dp_ref[0, 0, qs, :] = dq.astype(BF16)
            gb_ref[0] += _part8(dq)

        _for_blocks(block, dil, nblk)
        gb_ref[1] += _part8(dk_acc[...])
        gb_ref[2] += _part8(dv_acc[...])
        if dil > 1:
            def flush(which):
                for hlf in range(2):
                    dp_ref[which, 0, :, pl.ds(128 * hlf, 128)] = nat[hlf].astype(BF16)

            def to_token_order(acc_ref):
                def residue(r, carry):
                    for hlf in range(2):
                        nat[hlf, pl.ds(r, n, stride=dil), :] = acc_ref[pl.ds(pl.multiple_of(r * n, BLK), n), pl.ds(128 * hlf, 128)]
                    return carry
                lax.fori_loop(0, dil, residue, 0)

            flush(0)
            to_token_order(dk_acc)
            flush(1)
            to_token_order(dv_acc)
            flush(2)
        else:
            dp_ref[1, 0] = dk_acc[...].astype(BF16)
            dp_ref[2, 0] = dv_acc[...].astype(BF16)

    scratch = [pltpu.VMEM((seq, SLAB), F32)] * 2
    if dil > 1:
        scratch += ([pltpu.VMEM((2, seq, 128), F32)] + [pltpu.VMEM((seq, SLAB), BF16)] * 4 + [pltpu.VMEM((seq, SLAB), F32)] * 2
                    + [pltpu.VMEM((2, seq, 128), F32)])
    dp, gb = _pcall(
        body, grid=(nbat,),
        out_shape=(jax.ShapeDtypeStruct(dp4.shape, BF16), jax.ShapeDtypeStruct((3, 8, SLAB), F32)),
        in_specs=[pl.BlockSpec((3, 1, seq, SLAB), lambda b: (0, g, b, 0)),
                  pl.BlockSpec((seq, SLAB), lambda b: (b, 0)),
                  pl.BlockSpec((2, seq, SLAB), lambda b: (0, b, 0)),
                  pl.BlockSpec((4 * BLK, 2 * BLK), lambda b: (0, 0)), ANY],
        out_specs=(pl.BlockSpec((3, 1, seq, SLAB), lambda b: (DP_SLABS // 9 - 1, g, b, 0)),
                   pl.BlockSpec((3, 8, SLAB), lambda b: (0, 0, 0))),
        scratch_shapes=scratch, input_output_aliases={4: 0}, name=f"attn_backward_{g}",
        compiler_params=_params(48, ("arbitrary",)))(qkv4, do_attn, ol_tot, _bias_tables(g), dp4)
    return dp.reshape(DP_SLABS, t, SLAB), gb


def _mid(rest, ol_tot, x, tgt, ada, cw, b_out, ln_g, ln_b, w_pa_t, w_pb, w_out, tm=256):
    t = x.shape[0]
    nbat = ada.shape[0]
    nt = t // tm
    tps = nt // nbat

    def body(rest_ref, halo_ref, ol_ref, x_ref, t_ref, ada_ref, cw_ref, bout_ref, lng_ref, lnb_ref,
             wpat_ref, wpb_ref, wout_ref,
             dp_ref, gx0_ref, doa_ref, mg_ref, dof_ref, bbs_ref, dyc_ref, a_ref, dya_ref,
             gbr_ref, sv_ref, dgate_ref, carry_ref, keep_ref):
        i = pl.program_id(0)
        ti = nt - 1 - i
        pos = ti % tps

        @pl.when(i == 0)
        def _():
            gbr_ref[...] = jnp.zeros_like(gbr_ref)
            sv_ref[...] = jnp.zeros_like(sv_ref)

        @pl.when(pos == tps - 1)
        def _():
            dgate_ref[...] = jnp.zeros_like(dgate_ref)
            carry_ref[...] = jnp.zeros_like(carry_ref)

        row = lax.broadcasted_iota(jnp.int32, (tm, SLAB), 0)
        halo_on = (pos > 0).astype(F32)

        def cols(s):
            return pl.ds(SLAB * s, SLAB)

        o_attn = ol_ref[0]
        z_a = rest_ref[R_ZA]
        sg_za = _sigmoid(z_a)
        a_ref[...] = (o_attn * z_a * sg_za).astype(BF16)
        y_attn = _nt(a_ref[...], wpat_ref[...])

        for s in range(4):
            u = rest_ref[R_GC + s] * rest_ref[R_UX + s]
            hu = halo_ref[R_GC + s] * halo_ref[R_UX + s] * halo_on
            u1 = jnp.where(row == 0, hu[7:8], pltpu.roll(u, 1, 0))
            u2 = jnp.where(row == 0, hu[6:7], jnp.where(row == 1, hu[7:8], pltpu.roll(u, 2, 0)))
            conv = cw_ref[0:1, cols(s)] * u2 + cw_ref[1:2, cols(s)] * u1 + cw_ref[2:3, cols(s)] * u
            zc = rest_ref[R_ZC + s]
            sg = _sigmoid(zc)
            keep_ref[2, :, cols(s)], keep_ref[3, :, cols(s)], keep_ref[4, :, cols(s)], keep_ref[5, :, cols(s)] = u1, u2, conv, sg
            bbs_ref[:, cols(s)] = (rest_ref[R_GB + s] * conv * (zc * sg)).astype(BF16)
        y_conv = _nn(bbs_ref[...], wpb_ref[...])

        for s in range(4):
            s_a, s_b = _sigmoid(rest_ref[R_GA + s]), _sigmoid(rest_ref[R_GBM + s])
            keep_ref[0, :, cols(s)], keep_ref[1, :, cols(s)] = s_a, s_b
            mg_ref[:, cols(s)] = (s_a * y_attn[:, SLAB * s:SLAB * (s + 1)] + s_b * y_conv[:, SLAB * s:SLAB * (s + 1)]).astype(BF16)
        o = _nn(mg_ref[...], wout_ref[...]) + bout_ref[...]
        gate = ada_ref[0, 2:3, :]
        r = ALPHA * x_ref[...] + gate * o
        mu = jnp.mean(r, axis=1, keepdims=True)
        rc = r - mu
        rstd = lax.rsqrt(jnp.mean(rc * rc, axis=1, keepdims=True) + LN_EPS)
        xhat = rc * rstd
        err = xhat * lng_ref[...] + lnb_ref[...] - t_ref[...]
        sv_ref[6] += _part8(err * err)
        dy = err * (1.0 / D)
        sv_ref[0] += _part8(dy * xhat)
        sv_ref[1] += _part8(dy)
        dxh = dy * lng_ref[...]
        dr = rstd * (dxh - jnp.mean(dxh, axis=1, keepdims=True) - xhat * jnp.mean(dxh * xhat, axis=1, keepdims=True))
        gx0_ref[...] = ALPHA * dr
        dgate_ref[0] += _part8(dr * o)
        do_ = dr * gate
        sv_ref[2] += _part8(do_)
        dof_ref[...] = do_.astype(BF16)
        dmerged = _nt(dof_ref[...], wout_ref[...])
        for s in range(4):
            s_a, s_b = keep_ref[0, :, cols(s)], keep_ref[1, :, cols(s)]
            dm = dmerged[:, SLAB * s:SLAB * (s + 1)]
            ya, yc = y_attn[:, SLAB * s:SLAB * (s + 1)], y_conv[:, SLAB * s:SLAB * (s + 1)]
            dya_ref[:, cols(s)] = (dm * s_a).astype(BF16)
            dyc_ref[:, cols(s)] = (dm * s_b).astype(BF16)
            dga = dm * ya * s_a * (1.0 - s_a)
            dgb = dm * yc * s_b * (1.0 - s_b)
            dp_ref[R_GA + s] = dga.astype(BF16)
            dp_ref[R_GBM + s] = dgb.astype(BF16)
            gbr_ref[R_GA + s] += _part8(dga)
            gbr_ref[R_GBM + s] += _part8(dgb)

        da = _nn(dya_ref[...], wpat_ref[...])
        doa_ref[...] = (da * z_a * sg_za).astype(BF16)
        dza = da * o_attn * (sg_za * (1.0 + z_a * (1.0 - sg_za)))
        dp_ref[R_ZA] = dza.astype(BF16)
        gbr_ref[R_ZA] += _part8(dza)

        dbb = _nt(dyc_ref[...], wpb_ref[...])
        for s in range(4):
            ux, gc, zc = rest_ref[R_UX + s], rest_ref[R_GC + s], rest_ref[R_ZC + s]
            u = gc * ux
            u1, u2, conv, sg = keep_ref[2, :, cols(s)], keep_ref[3, :, cols(s)], keep_ref[4, :, cols(s)], keep_ref[5, :, cols(s)]
            gb = rest_ref[R_GB + s]
            d_b = dbb[:, SLAB * s:SLAB * (s + 1)]
            szc = zc * sg
            dgb_ = d_b * conv * szc
            dconv = d_b * gb * szc
            dzc = d_b * gb * conv * (sg * (1.0 + zc * (1.0 - sg)))
            sv_ref[3, :, cols(s)] += _part8(dconv * u2)
            sv_ref[4, :, cols(s)] += _part8(dconv * u1)
            sv_ref[5, :, cols(s)] += _part8(dconv * u)
            nxt = carry_ref[:, cols(s)]
            d1 = jnp.where(row == tm - 1, nxt[0:1], pltpu.roll(dconv, tm - 1, 0))
            d2 = jnp.where(row == tm - 1, nxt[1:2], jnp.where(row == tm - 2, nxt[0:1], pltpu.roll(dconv, tm - 2, 0)))
            carry_ref[:, cols(s)] = dconv[0:8]
            du = cw_ref[2:3, cols(s)] * dconv + cw_ref[1:2, cols(s)] * d1 + cw_ref[0:1, cols(s)] * d2
            dgc, dux = du * ux, du * gc
            for slab, val in ((R_GB + s, dgb_), (R_ZC + s, dzc), (R_GC + s, dgc), (R_UX + s, dux)):
                dp_ref[slab] = val.astype(BF16)
                gbr_ref[slab] += _part8(val)

    def tile(i):
        return nt - 1 - i

    row_blk = lambda i: (tile(i), 0)
    slab_blk = lambda i: (0, tile(i), 0)
    const2 = lambda i: (0, 0)
    const3 = lambda i: (0, 0, 0)
    in_specs = [
        pl.BlockSpec((N_REST, tm, SLAB), slab_blk),
        pl.BlockSpec((N_REST, 8, SLAB), lambda i: (0, jnp.maximum(tile(i) * (tm // 8) - 1, 0), 0)),
        pl.BlockSpec((1, tm, SLAB), slab_blk),
        pl.BlockSpec((tm, D), row_blk), pl.BlockSpec((tm, D), row_blk),
        pl.BlockSpec((1, 3, D), lambda i: (tile(i) // tps, 0, 0)),
        pl.BlockSpec((3, D), const2), pl.BlockSpec((1, D), const2), pl.BlockSpec((1, D), const2), pl.BlockSpec((1, D), const2),
        pl.BlockSpec((D, SLAB), const2), pl.BlockSpec((D, D), const2), pl.BlockSpec((D, D), const2)]
    bf_rows = lambda: jax.ShapeDtypeStruct((t, D), BF16)
    out_shape = (
        jax.ShapeDtypeStruct((DP_SLABS, t, SLAB), BF16), jax.ShapeDtypeStruct((t, D), F32),
        jax.ShapeDtypeStruct((t, SLAB), BF16),
        bf_rows(), bf_rows(), bf_rows(), bf_rows(), jax.ShapeDtypeStruct((t, SLAB), BF16), bf_rows(),
        jax.ShapeDtypeStruct((N_REST, 8, SLAB), F32), jax.ShapeDtypeStruct((7, 8, D), F32),
        jax.ShapeDtypeStruct((nbat, 8, D), F32))
    out_specs = (
        pl.BlockSpec((N_REST, tm, SLAB), slab_blk), pl.BlockSpec((tm, D), row_blk),
        pl.BlockSpec((tm, SLAB), row_blk),
        pl.BlockSpec((tm, D), row_blk), pl.BlockSpec((tm, D), row_blk), pl.BlockSpec((tm, D), row_blk),
        pl.BlockSpec((tm, D), row_blk), pl.BlockSpec((tm, SLAB), row_blk), pl.BlockSpec((tm, D), row_blk),
        pl.BlockSpec((N_REST, 8, SLAB), const3), pl.BlockSpec((7, 8, D), const3),
        pl.BlockSpec((1, 8, D), lambda i: (tile(i) // tps, 0, 0)))
    return _pcall(body, grid=(nt,), out_shape=out_shape, in_specs=in_specs, out_specs=out_specs,
                  scratch_shapes=[pltpu.VMEM((8, D), F32), pltpu.VMEM((6, tm, D), F32)], name="mid",
                  compiler_params=_params(56, ("arbitrary",)))(
        rest, rest, ol_tot, x, tgt, ada, cw, b_out, ln_g, ln_b, w_pa_t, w_pb, w_out)


def _tn_matmul(lhs, rhs, lhs_spec, n_steps, out_rows, out_index, name, after):
    t, n = rhs.shape

    def body(l_ref, r_ref, after_ref, o_ref):
        del after_ref
        o_ref[...] = _tn(l_ref[0] if len(l_ref.shape) == 3 else l_ref[...], r_ref[...])

    return _pcall(body, grid=(n_steps,), out_shape=jax.ShapeDtypeStruct((out_rows, n), F32),
                  in_specs=[lhs_spec, pl.BlockSpec((t, n), lambda j: (0, 0)), ANY],
                  out_specs=pl.BlockSpec((SLAB, n), out_index), name=name,
                  compiler_params=_params(48, ("parallel",)))(lhs, rhs, after)


def _grad_rows_2d(lhs, rhs, name, after, tc=1024):
    t, k = lhs.shape
    n = rhs.shape[1]

    def body(l_ref, r_ref, after_ref, o_ref):
        del after_ref
        part = _tn(l_ref[...], r_ref[...])

        @pl.when(pl.program_id(0) == 0)
        def _():
            o_ref[...] = part

        @pl.when(pl.program_id(0) > 0)
        def _():
            o_ref[...] += part

    return _pcall(body, grid=(t // tc,), out_shape=jax.ShapeDtypeStruct((k, n), F32),
                  in_specs=[pl.BlockSpec((tc, k), lambda i: (i, 0)), pl.BlockSpec((tc, n), lambda i: (i, 0)), ANY],
                  out_specs=pl.BlockSpec((k, n), lambda i: (0, 0)), name=name,
                  compiler_params=_params(32, ("arbitrary",)))(lhs, rhs, after)


def _w_row_block(j):
    return (j + N_QKV) % N_SLAB


def _dp_slab(j):
    return jnp.where(j < N_REST, j, j + 2)


def _grad_w_in_t(dproj, h):
    t = h.shape[0]
    return _tn_matmul(dproj, h, pl.BlockSpec((1, t, SLAB), lambda j: (_dp_slab(j), 0, 0)), N_SLAB, D_IN,
                      lambda j: (_w_row_block(j), 0), "grad_w_in", h)


def _grad_h(dproj, w_in_t, gx0, x, ada, after, tm=512):
    t = x.shape[0]
    nbat = ada.shape[0]
    tps = (t // nbat) // tm

    def body(dp_ref, w_ref, gx0_ref, x_ref, ada_ref, after_ref, gx_ref, dss_ref):
        del after_ref
        i = pl.program_id(0)
        dh = None
        for j in range(N_SLAB):
            slab = j if j < N_REST else j + 2
            part = _nn(dp_ref[slab], w_ref[pl.ds(SLAB * ((j + N_QKV) % N_SLAB), SLAB), :])
            dh = part if dh is None else dh + part
        gx_ref[...] = gx0_ref[...] + dh * (1.0 + ada_ref[0, 1:2, :])

        @pl.when((i % tps) == 0)
        def _():
            dss_ref[...] = jnp.zeros_like(dss_ref)

        dss_ref[0, 0] += _part8(dh)
        dss_ref[0, 1] += _part8(dh * x_ref[...])

    return _pcall(
        body, grid=(t // tm,),
        out_shape=(jax.ShapeDtypeStruct((t, D), F32), jax.ShapeDtypeStruct((nbat, 2, 8, D), F32)),
        in_specs=[pl.BlockSpec((DP_SLABS, tm, SLAB), lambda i: (0, i, 0)),
                  pl.BlockSpec((D_IN, D), lambda i: (0, 0), pipeline_mode=pl.Buffered(1)),
                  pl.BlockSpec((tm, D), lambda i: (i, 0)), pl.BlockSpec((tm, D), lambda i: (i, 0)),
                  pl.BlockSpec((1, 3, D), lambda i: (i // tps, 0, 0)), ANY],
        out_specs=(pl.BlockSpec((tm, D), lambda i: (i, 0)),
                   pl.BlockSpec((1, 2, 8, D), lambda i: (i // tps, 0, 0, 0))),
        name="grad_h", compiler_params=_params(60, ("arbitrary",)))(dproj, w_in_t, gx0, x, ada, after)


def _chip(m):
    x, y, _ = _my_position()
    return (x ^ ((m >> 1) & 1), y ^ (m & 1))


def _exchange_siblings(grads, after, name):
    n = len(grads)

    def body(*refs):
        copies = _sibling_copies(refs[:n], refs[n + 1:2 * n + 1], refs[2 * n + 1], refs[2 * n + 2])
        for cp in copies:
            cp.start()
        for cp in copies:
            cp.wait()

    return _pcall(body, out_shape=tuple(_sibling_zones(grads)), in_specs=[ANY] * (n + 1), out_specs=(ANY,) * n,
                  name=name, scratch_shapes=[pltpu.SemaphoreType.DMA((4 * n,))] * 2)(*grads, after)


def _sibling_zones(grads):
    return [jax.ShapeDtypeStruct((4, g.shape[0] // N_DEV, g.shape[1]), g.dtype) for g in grads]


def _sibling_copies(srcs, lands, send_sems, recv_sems):
    x, y, c = _my_position()
    copies = []
    for a, (src, land) in enumerate(zip(srcs, lands)):
        rows = land.shape[1]
        for m in range(4):
            dev = _flat(*_chip(m), 1 - c)
            copies.append(pltpu.make_async_remote_copy(
                src_ref=src.at[pl.ds(pl.multiple_of(dev * rows, 8), rows), :], dst_ref=land.at[m],
                send_sem=send_sems.at[4 * a + m], recv_sem=recv_sems.at[4 * a + m], device_id=(x, y, 1 - c),
                device_id_type=MESH))
    return copies


def _chip_copies(srcs, lands, send_sems, recv_sems):
    _, _, c = _my_position()
    return [pltpu.make_async_remote_copy(
        src_ref=srcs[a].at[m - 1], dst_ref=lands[a].at[m - 1], send_sem=send_sems.at[3 * a + m - 1],
        recv_sem=recv_sems.at[3 * a + m - 1], device_id=(*_chip(m), c), device_id_type=MESH)
        for a in range(len(srcs)) for m in range(1, 4)]


HBM = pl.BlockSpec(memory_space=pltpu.HBM)
SEM = pl.BlockSpec(memory_space=pltpu.SEMAPHORE)
SPLIT_COPY = pltpu.CompilerParams(has_side_effects=pltpu.SideEffectType.DATAFLOW_SIDE_EFFECTING)


def _start_copies(make_copies, n_sems, srcs, zones, name):
    n = len(srcs)

    def body(*refs):
        for cp in make_copies(refs[:n], refs[n:2 * n], refs[2 * n], refs[2 * n + 1]):
            cp.start()
        refs[-1][...] = jnp.zeros_like(refs[-1])

    hbm = tuple(pltpu.HBM(b.shape, b.dtype) for b in list(srcs) + list(zones))
    out_shape = (pltpu.SemaphoreType.DMA((n_sems,)), pltpu.SemaphoreType.DMA((n_sems,))) + hbm + (jax.ShapeDtypeStruct((8, 128), F32),)
    operands = [pltpu.with_memory_space_constraint(b, pltpu.HBM) for b in srcs]
    operands += [pltpu.with_memory_space_constraint(lax.empty(z.shape, z.dtype), pltpu.HBM) for z in zones]
    res = _pcall(body, out_shape=out_shape, in_specs=[HBM] * (2 * n), out_specs=(SEM, SEM) + (HBM,) * (2 * n) + (VMEM,),
                 input_output_aliases={i: 2 + i for i in range(2 * n)}, name=name, compiler_params=SPLIT_COPY)(*operands)
    return (res[0], res[1], res[2:2 + n], res[2 + n:2 + 2 * n]), res[-1]


def _wait_copies(make_copies, flight, after, name):
    send_sems, recv_sems, srcs, zones = flight
    n = len(srcs)

    def body(*refs):
        for cp in make_copies(refs[:n], refs[n:2 * n], refs[2 * n], refs[2 * n + 1]):
            cp.wait_send()
            cp.wait_recv()

    hbm = tuple(pltpu.HBM(b.shape, b.dtype) for b in list(srcs) + list(zones))
    res = _pcall(body, out_shape=hbm, in_specs=[HBM] * (2 * n) + [SEM, SEM, ANY], out_specs=(HBM,) * (2 * n),
                 input_output_aliases={i: i for i in range(2 * n)}, name=name, compiler_params=SPLIT_COPY)(
        *srcs, *zones, send_sems, recv_sems, after)
    return res[:n], res[n:]


def _pair_sums(devs, grads, lands, n_steps, name):
    n = len(grads)
    rows = [l.shape[1] for l in lands]
    rbs = [r // n_steps for r in rows]

    def body(devs_ref, *refs):
        del devs_ref
        g_refs, land_refs, outs = refs[:4 * n], refs[4 * n:5 * n], refs[5 * n:]
        for a in range(n):
            outs[2 * a][...] = g_refs[4 * a][...] + land_refs[a][0]
            for m in range(1, 4):
                outs[2 * a + 1][m - 1] = (g_refs[4 * a + m][...] + land_refs[a][m]).astype(BF16)

    def block_of(m, per_dev):
        return lambda i, devs_ref: (devs_ref[m] * per_dev + i, 0)

    in_specs = [pl.BlockSpec((rb, l.shape[2]), block_of(m, n_steps)) for rb, l in zip(rbs, lands) for m in range(4)]
    in_specs += [pl.BlockSpec((4, rb, l.shape[2]), lambda i, devs_ref: (0, i, 0)) for rb, l in zip(rbs, lands)]
    out_shape, out_specs = [], []
    for rb, l in zip(rbs, lands):
        out_shape += [jax.ShapeDtypeStruct(l.shape[1:], F32), jax.ShapeDtypeStruct((3,) + l.shape[1:], BF16)]
        out_specs += [pl.BlockSpec((rb, l.shape[2]), lambda i, devs_ref: (i, 0)),
                      pl.BlockSpec((3, rb, l.shape[2]), lambda i, devs_ref: (0, i, 0))]
    grid_spec = pltpu.PrefetchScalarGridSpec(num_scalar_prefetch=1, grid=(n_steps,), in_specs=in_specs, out_specs=tuple(out_specs))
    res = _pcall(body, grid_spec=grid_spec, out_shape=tuple(out_shape), name=name,
                 compiler_params=_params(48, ("parallel",)))(devs, *[g for g in grads for _ in range(4)], *lands)
    return res[0::2], res[1::2]


def _final_sums(mine, lands, n_steps, name):
    n = len(mine)
    rbs = [o.shape[0] // n_steps for o in mine]

    def body(*refs):
        mine_refs, land_refs, outs = refs[:n], refs[n:2 * n], refs[2 * n:]
        for a in range(n):
            tot = mine_refs[a][...]
            for m in range(3):
                tot = tot + land_refs[a][m].astype(F32)
            outs[a][...] = tot

    in_specs = ([pl.BlockSpec((rb, o.shape[1]), lambda i: (i, 0)) for rb, o in zip(rbs, mine)]
                + [pl.BlockSpec((3, rb, o.shape[1]), lambda i: (0, i, 0)) for rb, o in zip(rbs, mine)])
    out_specs = tuple(pl.BlockSpec((rb, o.shape[1]), lambda i: (i, 0)) for rb, o in zip(rbs, mine))
    out_shape = tuple(jax.ShapeDtypeStruct(o.shape, F32) for o in mine)
    return _pcall(body, grid=(n_steps,), out_shape=out_shape, in_specs=in_specs, out_specs=out_specs, name=name,
                  compiler_params=_params(32, ("parallel",)))(*mine, *lands)


def _reduce_scatter_begin(big, small_after_start):
    c = lax.axis_index("c")
    devs = jnp.stack([_flat(*_chip(m), c) for m in range(4)]).astype(jnp.int32)
    flight, token = _start_copies(_sibling_copies, 4, [big], _sibling_zones([big]), "siblings_start")
    small = small_after_start(token)
    (big,), big_lands = _wait_copies(_sibling_copies, flight, small[-1], "siblings_wait")
    big_mine, big_send = _pair_sums(devs, [big], big_lands, 4, "pair_sums_w_in")
    big_flight, token = _start_copies(_chip_copies, 3, list(big_send), list(big_send), "chips_start_w_in")
    small_lands = _exchange_siblings(small, token, "exchange_siblings_rest")
    small_mine, small_send = _pair_sums(devs, small, small_lands, 1, "pair_sums_rest")
    small_flight, token = _start_copies(_chip_copies, 3 * len(small), list(small_send), list(small_send), "chips_start_rest")
    return (big_flight, small_flight, list(big_mine) + list(small_mine)), token


def _reduce_scatter_end(state, after):
    big_flight, small_flight, mine = state
    _, big_got = _wait_copies(_chip_copies, big_flight, after, "chips_wait_w_in")
    _, small_got = _wait_copies(_chip_copies, small_flight, after, "chips_wait_rest")
    small = _final_sums(mine[1:], small_got, 1, "final_sums_rest")
    return (mine[0], big_got[0]), list(small)


def _adamw(w, g, m, v):
    m_new = B1 * m + (1.0 - B1) * g
    v_new = B2 * v + (1.0 - B2) * (g * g)
    m_hat = m_new / (1.0 - B1 ** STEP)
    v_hat = v_new / (1.0 - B2 ** STEP)
    delta = -LR * (m_hat / (jnp.sqrt(v_hat) + EPS) + WD * w)
    return delta, m_new, v_new


def _final_sum_adam_rows(mine, land, w, m, v, n_steps, name):
    rows, ncol = w.shape
    blk = pl.BlockSpec((rows // n_steps, ncol), lambda i: (i, 0))

    def body(mine_ref, land_ref, w_ref, m_ref, v_ref, g_ref, d_ref, mo_ref, vo_ref):
        g = mine_ref[...]
        for k in range(3):
            g = g + land_ref[k].astype(F32)
        g_ref[...] = g
        d_ref[...], mo_ref[...], vo_ref[...] = _adamw(w_ref[...], g, m_ref[...], v_ref[...])

    shape = jax.ShapeDtypeStruct(w.shape, F32)
    return _pcall(body, grid=(n_steps,), out_shape=(shape,) * 4,
                  in_specs=[blk, pl.BlockSpec((3, rows // n_steps, ncol), lambda i: (0, i, 0)), blk, blk, blk],
                  out_specs=(blk,) * 4, name=name, compiler_params=_params(32, ("parallel",)))(mine, land, w, m, v)


def _adam_transposed(g_t, w, m, v, name):
    n, k = g_t.shape
    rb = min(k, 128)

    def body(gt_ref, w_ref, m_ref, v_ref, g_ref, d_ref, mo_ref, vo_ref):
        for src, skip, dst, size in _column_chunks(n):
            sl = pl.ds(dst, size)
            g = gt_ref[pl.ds(src, 128), :].T[:, skip:]
            delta, m_new, v_new = _adamw(w_ref[:, sl], g, m_ref[:, sl], v_ref[:, sl])
            g_ref[:, sl], d_ref[:, sl], mo_ref[:, sl], vo_ref[:, sl] = g, delta, m_new, v_new

    shape = jax.ShapeDtypeStruct(w.shape, F32)
    rows = pl.BlockSpec((rb, n), lambda i: (i, 0))
    return _pcall(body, grid=(k // rb,), out_shape=(shape,) * 4,
                  in_specs=[pl.BlockSpec((n, rb), lambda i: (0, i)), rows, rows, rows], out_specs=(rows,) * 4, name=name,
                  compiler_params=_params(32, ("parallel",)))(g_t, w, m, v)


def _adam_many(items, name):
    n = len(items)

    def body(*refs):
        ins, outs = refs[:4 * n], refs[4 * n:]
        for a in range(n):
            w_ref, g_ref, m_ref, v_ref = ins[4 * a:4 * a + 4]
            delta, m_new, v_new = _adamw(w_ref[...], g_ref[...], m_ref[...], v_ref[...])
            outs[3 * a][...], outs[3 * a + 1][...], outs[3 * a + 2][...] = delta, m_new, v_new

    out_shape = tuple(jax.ShapeDtypeStruct(it[0].shape, F32) for it in items for _ in range(3))
    flat = [arr for it in items for arr in it]
    res = _pcall(body, grid=(1,), out_shape=out_shape, in_specs=[_whole(a) for a in flat],
                 out_specs=tuple(_whole(o) for o in out_shape), name=name, compiler_params=_params(32))(*flat)
    return [tuple(res[3 * a:3 * a + 3]) for a in range(n)]


def _adam_w_ada(cact_all, dada_mine, w, m, v):
    def body(c_ref, d_ref, w_ref, m_ref, v_ref, g_ref, dl_ref, mo_ref, vo_ref):
        g = _tn(c_ref[...].astype(BF16), d_ref[...].astype(BF16))
        delta, m_new, v_new = _adamw(w_ref[...], g, m_ref[...], v_ref[...])
        g_ref[...], dl_ref[...], mo_ref[...], vo_ref[...] = g, delta, m_new, v_new

    shape = jax.ShapeDtypeStruct(w.shape, F32)
    operands = (cact_all, dada_mine, w, m, v)
    return _pcall(body, grid=(1,), out_shape=(shape,) * 4, in_specs=[_whole(a) for a in operands],
                  out_specs=(_whole(w),) * 4, name="adam_w_ada", compiler_params=_params(32))(*operands)


def kernel(x, c, w_ada, b_ada, w_in, b_in, conv_w, w_proj_attn, w_proj_conv, w_out, b_out, ln_g, ln_b, loss_target, m_w_ada, m_b_ada, m_w_in, m_b_in, m_conv_w, m_w_proj_attn, m_w_proj_conv, m_w_out, m_b_out, m_ln_g, m_ln_b, v_w_ada, v_b_ada, v_w_in, v_b_in, v_conv_w, v_w_proj_attn, v_w_proj_conv, v_w_out, v_b_out, v_ln_g, v_ln_b):
    nbat, seq, _ = x.shape
    t = nbat * seq
    me = _flat(*_my_position())
    x2, tgt2 = x.reshape(t, D), loss_target.reshape(t, D)
    sq = lambda a: a.reshape(a.shape[1:])

    tr = lambda a: a[0].T
    w_in_rows = tr(w_in)
    w_in_t_s = _cast_rows(w_in_rows, 4, "cast_w_in")
    w_pa_t_s, w_pb_s, w_out_s, cact_s, cw_s = _prep(sq(w_proj_attn), sq(w_proj_conv), sq(w_out), c, sq(conv_w))

    ncol = w_ada.shape[2]
    b_ada_mine = lax.dynamic_slice(b_ada, (0, me * ncol), (1, ncol))
    ada_slots, cact_slots, cw_slots = _ada_forward(cact_s, cw_s, sq(w_ada), b_ada_mine)
    cact_all = cact_slots[:, :nbat].reshape(N_DEV * nbat, D)
    cw = cw_slots[:, :3].transpose(1, 0, 2).reshape(3, D)
    ada_all = ada_slots[:, :, :nbat].transpose(1, 2, 0, 3).reshape(N_DEV * nbat, 3, D)
    ada = lax.dynamic_slice(ada_all, (me * nbat, 0, 0), (nbat, 3, D))

    w_in_t, qkv, rest, h, (w_pa_t, w_pb, w_o) = _project_gather(
        w_in_t_s, x2, ada, b_in.reshape(N_SLAB, 1, SLAB), [w_pa_t_s, w_pb_s, w_out_s])
    ol_tot = _attn_forward(qkv, nbat)
    (dproj, gx0, do_attn, merged, do_f, bbs, dyc, a_bf, dya, gb_rest, svec, dgate) = _mid(
        rest, ol_tot, x2, tgt2, ada, cw, b_out, ln_g, ln_b, w_pa_t, w_pb, w_o)

    gb_qkv = []
    for g in range(3):
        dproj, gb = _attn_backward(qkv, do_attn, ol_tot, dproj, g, nbat)
        gb_qkv.append(gb)
    g_w_in_t = _grad_w_in_t(dproj, h)

    def small_grads(token):
        g_w_out = _grad_rows_2d(merged, do_f, "grad_w_out", token)
        g_w_pb = _grad_rows_2d(bbs, dyc, "grad_w_proj_conv", g_w_out)
        g_w_pa_t = _grad_rows_2d(dya, a_bf, "grad_w_proj_attn", g_w_pb)
        return [g_w_out, g_w_pb, g_w_pa_t]

    rs_state, token = _reduce_scatter_begin(g_w_in_t, small_grads)
    grad_x, dss = _grad_h(dproj, w_in_t, gx0, x2, ada, token)

    rows8, tot, g_bada = _small_reduce(gb_rest, gb_qkv, svec, dgate, dss)
    (g_in_mine, g_in_got), (g_out, g_pb, g_pa_t) = _reduce_scatter_end(rs_state, tot)
    loss = tot[0, P_LOSS]
    dada_all = rows8[:, 0, P_DADA:].reshape(N_DEV * nbat, 3 * D)
    dada_mine = lax.dynamic_slice(dada_all, (0, me * ncol), (N_DEV * nbat, ncol))

    g_in_t, d_win_t, nm_win_t, nv_win_t = _final_sum_adam_rows(g_in_mine, g_in_got, w_in_rows, tr(m_w_in), tr(v_w_in), 4, "adam_w_in")
    g_win, d_win, nm_win, nv_win = g_in_t.T, d_win_t.T, nm_win_t.T, nv_win_t.T
    g_wpa, d_wpa, nm_wpa, nv_wpa = _adam_transposed(g_pa_t, sq(w_proj_attn), sq(m_w_proj_attn), sq(v_w_proj_attn), "adam_w_proj_attn")
    g_wada, d_wada, nm_wada, nv_wada = _adam_w_ada(cact_all, dada_mine, sq(w_ada), sq(m_w_ada), sq(v_w_ada))
    g_bin = tot[:, P_BIN:P_BIN + D_IN]
    g_bout = tot[:, P_BOUT:P_BOUT + D]
    g_lng = tot[:, P_LNG:P_LNG + D]
    g_lnb = tot[:, P_LNB:P_LNB + D]
    g_conv = lax.dynamic_slice(tot[:, P_CONV:P_CONV + 3 * D].reshape(3, D), (0, me * cw_s.shape[1]), (3, cw_s.shape[1]))
    upd = _adam_many([
        (sq(w_proj_conv), g_pb, sq(m_w_proj_conv), sq(v_w_proj_conv)),
        (sq(w_out), g_out, sq(m_w_out), sq(v_w_out)),
        (b_ada, g_bada, m_b_ada, v_b_ada), (b_in, g_bin, m_b_in, v_b_in), (sq(conv_w), g_conv, sq(m_conv_w), sq(v_conv_w)),
        (b_out, g_bout, m_b_out, v_b_out), (ln_g, g_lng, m_ln_g, v_ln_g), (ln_b, g_lnb, m_ln_b, v_ln_b)], "adam_rest")
    (d_wpb, nm_wpb, nv_wpb), (d_wout, nm_wout, nv_wout), (d_bada, nm_bada, nv_bada), (d_bin, nm_bin, nv_bin), \
        (d_conv, nm_conv, nv_conv), (d_bout, nm_bout, nv_bout), (d_lng, nm_lng, nv_lng), (d_lnb, nm_lnb, nv_lnb) = upd

    ex = lambda a: a.reshape((1,) + a.shape)
    grads = [ex(g_wada), g_bada, ex(g_win), g_bin, ex(g_conv), ex(g_wpa), ex(g_pb), ex(g_out), g_bout, g_lng, g_lnb]
    deltas = [ex(d_wada), d_bada, ex(d_win), d_bin, ex(d_conv), ex(d_wpa), ex(d_wpb), ex(d_wout), d_bout, d_lng, d_lnb]
    new_m = [ex(nm_wada), nm_bada, ex(nm_win), nm_bin, ex(nm_conv), ex(nm_wpa), ex(nm_wpb), ex(nm_wout), nm_bout, nm_lng, nm_lnb]
    new_v = [ex(nv_wada), nv_bada, ex(nv_win), nv_bin, ex(nv_conv), ex(nv_wpa), ex(nv_wpb), ex(nv_wout), nv_bout, nv_lng, nv_lnb]
    return (loss, grad_x.reshape(x.shape), *grads, *deltas, *new_m, *new_v)
```

```python
import jax
import jax.numpy as jnp
from jax import lax
from jax.experimental import pallas as pl
from jax.experimental.pallas import tpu as pltpu

F32, BF16 = jnp.float32, jnp.bfloat16
MESH = pl.DeviceIdType.MESH
N_DEV = 8
D = 1024
SLAB = 256
N_QKV, N_REST = 9, 25
N_SLAB = N_QKV + N_REST
D_IN = N_SLAB * SLAB
DP_SLABS = 36
BLK = 128
GROUPS = ((128, 1), (512, 4), (2048, 16))
ALPHA = 2.0 ** 0.25
LN_EPS = 1e-5
LR, B1, B2, EPS, WD, STEP = 0.001, 0.9, 0.999, 1e-08, 0.01, 10
R_ZA, R_UX, R_GB, R_GC, R_ZC, R_GA, R_GBM = 0, 1, 5, 9, 13, 17, 21
P_BIN, P_BOUT, P_LNG, P_LNB, P_CONV, P_LOSS, P_DADA = 0, 8704, 9728, 10752, 11776, 14848, 14976
MIB = 1024 * 1024


def _pcall(body, *, out_shape, out_specs=None, **kw):
    def pin_out(shape, spec):
        in_hbm = getattr(spec, "block_shape", None) is not None or getattr(spec, "memory_space", None) is pl.ANY
        return pltpu.HBM(shape.shape, shape.dtype) if in_hbm and isinstance(shape, jax.ShapeDtypeStruct) else shape

    n_scalar = 0
    if out_specs is None:
        specs = kw["grid_spec"].out_specs
        n_scalar = kw["grid_spec"].num_scalar_prefetch
    else:
        kw["out_specs"] = specs = out_specs
    if isinstance(out_shape, (tuple, list)):
        out_shape = tuple(pin_out(s, p) for s, p in zip(out_shape, specs))
    else:
        out_shape = pin_out(out_shape, specs)
    call = pl.pallas_call(body, out_shape=out_shape, **kw)

    def run(*operands):
        def pin(o):
            is_data = jnp.issubdtype(o.dtype, jnp.floating) or jnp.issubdtype(o.dtype, jnp.integer)
            return pltpu.with_memory_space_constraint(o, pltpu.HBM) if is_data else o
        return call(*operands[:n_scalar], *[pin(o) for o in operands[n_scalar:]])

    return run

ANY = pl.BlockSpec(memory_space=pl.ANY)
VMEM = pl.BlockSpec(memory_space=pltpu.VMEM)


def _whole(a):
    return pl.BlockSpec(a.shape, lambda i: (0,) * len(a.shape))


def _params(vmem_mib=None, sem=None):
    kw = {}
    if vmem_mib is not None:
        kw["vmem_limit_bytes"] = vmem_mib * MIB
    if sem is not None:
        kw["dimension_semantics"] = sem
    return pltpu.CompilerParams(**kw)


def _nn(a, b):
    return jnp.dot(a, b, preferred_element_type=F32)


def _nt(a, b):
    return lax.dot_general(a, b, (((1,), (1,)), ((), ())), preferred_element_type=F32)


def _tn(a, b):
    return lax.dot_general(a, b, (((0,), (0,)), ((), ())), preferred_element_type=F32)


def _sigmoid(v):
    return 0.5 * jnp.tanh(0.5 * v) + 0.5


def _part8(v):
    return v.reshape(v.shape[0] // 8, 8, v.shape[1]).sum(axis=0)


def _my_position():
    return lax.axis_index("x"), lax.axis_index("y"), lax.axis_index("c")


def _flat(px, py, pc):
    return 4 * px + 2 * py + pc


def _peer(mask):
    x, y, c = _my_position()
    return (x ^ ((mask >> 2) & 1), y ^ ((mask >> 1) & 1), c ^ (mask & 1))


def _column_chunks(n):
    chunks = [(128 * a, 0, 128 * a, 128) for a in range(n // 128)]
    if n % 128:
        chunks.append((n - 128, 128 - n % 128, 128 * (n // 128), n % 128))
    return chunks


def _cast_rows(w, n_steps, name):
    rows, ncol = w.shape
    blk = pl.BlockSpec((rows // n_steps, ncol), lambda i: (i, 0))

    def body(w_ref, o_ref):
        o_ref[...] = w_ref[...].astype(BF16)

    return _pcall(body, grid=(n_steps,), out_shape=jax.ShapeDtypeStruct(w.shape, BF16), in_specs=[blk], out_specs=blk,
                  name=name, compiler_params=_params(16, ("parallel",)))(w)


def _prep(w_pa, w_pb, w_out, c, conv_w):
    def body(wpa_ref, wpb_ref, wout_ref, c_ref, cw_ref, wpat_ref, wpb_o, wout_o, cact_ref, cwp_ref):
        wpat_ref[...] = wpa_ref[...].T.astype(BF16)
        wpb_o[...] = wpb_ref[...].astype(BF16)
        wout_o[...] = wout_ref[...].astype(BF16)
        cv = c_ref[...]
        cact_ref[...] = jnp.zeros_like(cact_ref)
        cact_ref[pl.ds(0, cv.shape[0]), :] = cv * _sigmoid(cv)
        cwp_ref[...] = jnp.zeros_like(cwp_ref)
        cwp_ref[pl.ds(0, 3), :] = cw_ref[...]

    out_shape = (jax.ShapeDtypeStruct((w_pa.shape[1], w_pa.shape[0]), BF16),
                 jax.ShapeDtypeStruct(w_pb.shape, BF16), jax.ShapeDtypeStruct(w_out.shape, BF16),
                 jax.ShapeDtypeStruct((8, D), F32), jax.ShapeDtypeStruct((8, conv_w.shape[1]), F32))
    operands = (w_pa, w_pb, w_out, c, conv_w)
    return _pcall(body, grid=(1,), out_shape=out_shape, in_specs=[_whole(a) for a in operands],
                  out_specs=tuple(_whole(o) for o in out_shape), name="prep", compiler_params=_params(16))(*operands)


def _exchange_slots(out_refs, send_sems, recv_sems, base=0):
    me = _flat(*_my_position())

    def copy(a, mask, slot):
        return pltpu.make_async_remote_copy(
            src_ref=out_refs[a].at[slot], dst_ref=out_refs[a].at[slot], send_sem=send_sems.at[base + 7 * a + mask - 1],
            recv_sem=recv_sems.at[base + 7 * a + mask - 1], device_id=_peer(mask), device_id_type=MESH)

    pairs = [(a, mask) for a in range(len(out_refs)) for mask in range(1, N_DEV)]
    for a, mask in pairs:
        copy(a, mask, me).start()
    for a, mask in pairs:
        copy(a, mask, _flat(*_peer(mask))).wait_recv()
    for a, mask in pairs:
        copy(a, mask, me).wait_send()


def _ada_forward(cact_mine, cw_mine, w_ada, b_ada_mine):
    ncol = w_ada.shape[1]

    def body(c_ref, cw_ref, w_ref, b_ref, out_ref, call_ref, cwall_ref, send_sems, recv_sems):
        me = _flat(*_my_position())
        call_ref[me] = c_ref[...]
        cwall_ref[me] = cw_ref[...]
        _exchange_slots([call_ref, cwall_ref], send_sems, recv_sems)
        c_all = call_ref[...].reshape(N_DEV * 8, D).astype(BF16)
        out_ref[me] = (_nn(c_all, w_ref[...].astype(BF16)) + b_ref[...]).reshape(N_DEV, 8, ncol)
        _exchange_slots([out_ref], send_sems, recv_sems, base=14)

    operands = (cact_mine, cw_mine, w_ada, b_ada_mine)
    out_shape = (jax.ShapeDtypeStruct((N_DEV, N_DEV, 8, ncol), F32), jax.ShapeDtypeStruct((N_DEV, 8, D), F32),
                 jax.ShapeDtypeStruct((N_DEV,) + cw_mine.shape, F32))
    return _pcall(body, grid=(1,), out_shape=out_shape, in_specs=[_whole(a) for a in operands], out_specs=(VMEM,) * 3,
                  scratch_shapes=[pltpu.SemaphoreType.DMA((21,)), pltpu.SemaphoreType.DMA((21,))], name="ada_forward",
                  compiler_params=_params(16))(*operands)


def _small_reduce(gb_rest, gb_qkv, svec, dgate, dss):
    nbat = dgate.shape[0]

    def body(gbr_ref, q0_ref, q1_ref, q2_ref, sv_ref, dg_ref, dss_ref, rows_ref, tot_ref, gbada_ref, send_sems, recv_sems):
        me = _flat(*_my_position())

        def put(off, v):
            rows_ref[me, :, pl.ds(off, v.shape[1])] = v

        def row(v):
            return jnp.sum(v, axis=0, keepdims=True)

        for g, q_ref in enumerate((q0_ref, q1_ref, q2_ref)):
            for which in range(3):
                put(P_BIN + SLAB * (3 * which + g), row(q_ref[which]))
        for s in range(N_REST):
            put(P_BIN + SLAB * (N_QKV + s), row(gbr_ref[s]))
        put(P_LNG, row(sv_ref[0]))
        put(P_LNB, row(sv_ref[1]))
        put(P_BOUT, row(sv_ref[2]))
        for j in range(3):
            put(P_CONV + D * j, row(sv_ref[3 + j]))
        loss = (0.5 / D) * jnp.sum(row(sv_ref[6]), axis=1, keepdims=True)
        put(P_LOSS, jnp.broadcast_to(loss, (1, 128)))
        for b in range(nbat):
            put(P_DADA + 3 * D * b, row(dss_ref[b, 0]))
            put(P_DADA + 3 * D * b + D, row(dss_ref[b, 1]))
            put(P_DADA + 3 * D * b + 2 * D, row(dg_ref[b]))
        _exchange_slots([rows_ref], send_sems, recv_sems)
        tot = rows_ref[0]
        for k in range(1, N_DEV):
            tot = tot + rows_ref[k]
        tot_ref[...] = tot
        gbada = tot[:, P_DADA:P_DADA + 3 * D]
        for b in range(1, nbat):
            gbada = gbada + tot[:, P_DADA + 3 * D * b:P_DADA + 3 * D * (b + 1)]
        gbada_ref[...] = gbada

    p_len = P_DADA + nbat * 3 * D
    out_shape = (jax.ShapeDtypeStruct((N_DEV, 1, p_len), F32), jax.ShapeDtypeStruct((1, p_len), F32),
                 jax.ShapeDtypeStruct((1, 3 * D), F32))
    operands = (gb_rest, *gb_qkv, svec, dgate, dss)
    return _pcall(body, grid=(1,), out_shape=out_shape, in_specs=[_whole(a) for a in operands],
                  out_specs=(VMEM, _whole(out_shape[1]), _whole(out_shape[2])),
                  scratch_shapes=[pltpu.SemaphoreType.DMA((7,)), pltpu.SemaphoreType.DMA((7,))], name="small_reduce",
                  compiler_params=_params(16))(*operands)


PIECE = 64
N_CHUNK = 4
ARRIVAL_RANK = (0, 1, 3, 5, 2, 4, 6, 7)
SLOT_MASK = (1, 4, 2, 6, 5, 3, 7)


def _arrival_tables(shard_rows):
    import numpy as np
    crow = shard_rows // N_CHUNK
    table = np.zeros((N_DEV, N_SLAB + 7 * N_CHUNK), np.int32)
    lo = [(SLAB * j) // crow for j in range(N_SLAB)]
    hi = [(SLAB * j + SLAB - 1) // crow for j in range(N_SLAB)]
    for k in range(N_DEV):
        def rank(chunk):
            shard_rank = ARRIVAL_RANK[(chunk // N_CHUNK) ^ k]
            return shard_rank if shard_rank < 2 else 2 + 8 * (chunk % N_CHUNK) + shard_rank
        order = sorted(range(N_SLAB), key=lambda j: (max(rank(lo[j]), rank(hi[j])), j))
        table[k, :N_SLAB] = order
        for slot, mask in enumerate(SLOT_MASK):
            for ch in range(N_CHUNK):
                chunk = (k ^ mask) * N_CHUNK + ch
                table[k, N_SLAB + slot * N_CHUNK + ch] = min(t for t, j in enumerate(order) if lo[j] <= chunk <= hi[j])
    return table


def _project_gather(shard, x, ada, b_in3, others, xt=512):
    t = x.shape[0]
    n_o = len(others)
    srows = shard.shape[0]
    crow = srows // N_CHUNK
    shards = [shard] + list(others)
    table = jnp.asarray(_arrival_tables(srows))
    seq_tiles = (t // ada.shape[0]) // xt

    def body(tbl_ref, *refs):
        srcs = [refs[0]] + list(refs[4:4 + n_o])
        x_ref, ada_ref, b_ref = refs[1], refs[2], refs[3]
        outs = [refs[4 + n_o]] + list(refs[8 + n_o:8 + 2 * n_o])
        qkv_ref, rest_ref, h_out = refs[5 + n_o], refs[6 + n_o], refs[7 + n_o]
        (wtile, obf, of32, h_ref, xbuf, send_sems, recv_sems, local_sems, tile_sems, obf_sems, of32_sems, x_sems,
         h_sems) = refs[8 + 2 * n_o:]
        w_full = outs[0]
        x, y, c = _my_position()
        k = _flat(x, y, c)
        me, sibling = (x, y, c), (x, y, 1 - c)
        chips = [(1 - x, y), (x, 1 - y), (1 - x, 1 - y)]

        def rows(a, px, py, pc, ch):
            r = shards[a].shape[0]
            if ch is None:
                return outs[a].at[pl.ds(pl.multiple_of(_flat(px, py, pc) * r, r), r), :]
            return outs[a].at[pl.ds(pl.multiple_of(_flat(px, py, pc) * r + ch * crow, crow), crow), :]

        def copy(a, slot, block, to, ch=None, src=None):
            sem = slot * N_CHUNK + ch if a == 0 else 7 * (N_CHUNK - 1 + a) + slot
            if src is not None and ch is not None:
                src = src.at[pl.ds(ch * crow, crow), :]
            return pltpu.make_async_remote_copy(
                src_ref=rows(a, *block, ch) if src is None else src, dst_ref=rows(a, *block, ch),
                send_sem=send_sems.at[sem], recv_sem=recv_sems.at[sem], device_id=to, device_id_type=MESH)

        mine = [pltpu.make_async_copy(srcs[a], rows(a, *me, None), local_sems.at[a]) for a in range(1 + n_o)]
        first = []
        for ch in range(N_CHUNK):
            first.append(copy(0, 0, me, sibling, ch, src=srcs[0]))
            first += [copy(0, 1 + j, me, (*chip, c), ch, src=srcs[0]) for j, chip in enumerate(chips)]
        for a in range(1, 1 + n_o):
            first.append(copy(a, 0, me, sibling, src=srcs[a]))
            first += [copy(a, 1 + j, me, (*chip, c), src=srcs[a]) for j, chip in enumerate(chips)]
        for cp in mine + first:
            cp.start()

        def arrive(a, slot, ch=None):
            if slot == 0:
                copy(a, 0, sibling, me, ch).wait_recv()
            elif slot < 4:
                copy(a, slot, (*chips[slot - 1], c), me, ch).wait_recv()
                copy(a, slot + 3, (*chips[slot - 1], c), sibling, ch).start()
            else:
                copy(a, slot, (*chips[slot - 4], 1 - c), me, ch).wait_recv()

        def arrive_for(step):
            for slot in range(7):
                for ch in range(N_CHUNK):
                    @pl.when(tbl_ref[k, N_SLAB + slot * N_CHUNK + ch] == step)
                    def _():
                        arrive(0, slot, ch)

        def fetch(step, buf):
            slab = tbl_ref[k, step]
            for p in range(SLAB // PIECE):
                g0 = slab * SLAB + PIECE * p
                own = (g0 >= k * srows) & (g0 < (k + 1) * srows)
                dst = wtile.at[buf, pl.ds(PIECE * p, PIECE), :]

                @pl.when(own)
                def _():
                    pltpu.make_async_copy(srcs[0].at[pl.ds(pl.multiple_of(g0 - k * srows, PIECE), PIECE), :], dst, tile_sems.at[buf]).start()

                @pl.when(jnp.logical_not(own))
                def _():
                    pltpu.make_async_copy(w_full.at[pl.ds(pl.multiple_of(g0, PIECE), PIECE), :], dst, tile_sems.at[buf]).start()

        def wait_tile(buf):
            pltpu.make_async_copy(w_full.at[pl.ds(0, SLAB), :], wtile.at[buf], tile_sems.at[buf]).wait()

        def put(buf_ref, sems, dst_ref, count, value):
            b = count % 2

            @pl.when(count >= 2)
            def _():
                pltpu.make_async_copy(buf_ref.at[b], dst_ref, sems.at[b]).wait()

            buf_ref[b] = value
            pltpu.make_async_copy(buf_ref.at[b], dst_ref, sems.at[b]).start()

        def drain(buf_ref, sems, dst_ref, count):
            for back in (1, 2):
                @pl.when(count >= back)
                def _():
                    pltpu.make_async_copy(buf_ref.at[(count - back) % 2], dst_ref, sems.at[(count - back) % 2]).wait()

        def x_copy(i):
            return pltpu.make_async_copy(x_ref.at[pl.ds(xt * i, xt), :], xbuf.at[i % 2], x_sems.at[i % 2])

        def h_copy(i):
            return pltpu.make_async_copy(h_ref.at[pl.ds(xt * i, xt), :], h_out.at[pl.ds(xt * i, xt), :], h_sems.at[i % 2])

        x_copy(0).start()
        for i in range(t // xt):
            if i + 1 < t // xt:
                x_copy(i + 1).start()
            x_copy(i).wait()
            b = i // seq_tiles
            h_ref[pl.ds(xt * i, xt), :] = (xbuf[i % 2] * (1.0 + ada_ref[b, 1:2, :]) + ada_ref[b, 0:1, :]).astype(BF16)
            if i >= 2:
                h_copy(i - 2).wait()
            h_copy(i).start()
        for i in range(max(t // xt - 2, 0), t // xt):
            h_copy(i).wait()

        arrive_for(0)
        fetch(0, 0)

        def step(s, carry):
            n_bf, n_f32 = carry
            buf = s % 2

            @pl.when(s + 1 < N_SLAB)
            def _():
                arrive_for(s + 1)
                fetch(s + 1, 1 - buf)

            wait_tile(buf)
            slab = tbl_ref[k, s]
            v = _nt(h_ref[...], wtile[buf]) + b_ref[slab]
            is_qkv = slab < N_QKV

            @pl.when(is_qkv)
            def _():
                put(obf, obf_sems, qkv_ref.at[jnp.minimum(slab, N_QKV - 1)], n_bf, v.astype(BF16))

            @pl.when(jnp.logical_not(is_qkv))
            def _():
                put(of32, of32_sems, rest_ref.at[jnp.maximum(slab - N_QKV, 0)], n_f32, v)

            return n_bf + is_qkv.astype(jnp.int32), n_f32 + 1 - is_qkv.astype(jnp.int32)

        n_bf, n_f32 = lax.fori_loop(0, N_SLAB, step, (jnp.int32(0), jnp.int32(0)))
        drain(obf, obf_sems, qkv_ref.at[0], n_bf)
        drain(of32, of32_sems, rest_ref.at[0], n_f32)

        for slots in ((1, 2, 3), (0, 4, 5, 6)):
            for a in range(1, 1 + n_o):
                for slot in slots:
                    arrive(a, slot)
        for cp in first:
            cp.wait_send()
        for j, chip in enumerate(chips):
            for ch in range(N_CHUNK):
                copy(0, 4 + j, (*chip, c), sibling, ch).wait_send()
            for a in range(1, 1 + n_o):
                copy(a, 4 + j, (*chip, c), sibling).wait_send()
        for cp in mine:
            cp.wait()

    out_shape = ((jax.ShapeDtypeStruct((N_DEV * srows, D), BF16), jax.ShapeDtypeStruct((N_QKV, t, SLAB), BF16),
                  jax.ShapeDtypeStruct((N_REST, t, SLAB), F32), jax.ShapeDtypeStruct((t, D), BF16))
                 + tuple(jax.ShapeDtypeStruct((N_DEV * o.shape[0], o.shape[1]), o.dtype) for o in others))
    n_all = 1 + n_o
    n_sems = 7 * (N_CHUNK + n_o)
    pair = pltpu.SemaphoreType.DMA((2,))
    grid_spec = pltpu.PrefetchScalarGridSpec(
        num_scalar_prefetch=1, grid=(1,),
        in_specs=[ANY, ANY, pl.BlockSpec(ada.shape, lambda i, tbl: (0, 0, 0)),
                  pl.BlockSpec((N_SLAB, 1, SLAB), lambda i, tbl: (0, 0, 0))] + [ANY] * n_o,
        out_specs=(ANY,) * (4 + n_o),
        scratch_shapes=[pltpu.VMEM((2, SLAB, D), BF16), pltpu.VMEM((2, t, SLAB), BF16), pltpu.VMEM((2, t, SLAB), F32),
                        pltpu.VMEM((t, D), BF16), pltpu.VMEM((2, xt, D), F32),
                        pltpu.SemaphoreType.DMA((n_sems,)), pltpu.SemaphoreType.DMA((n_sems,)),
                        pltpu.SemaphoreType.DMA((n_all,)), pair, pair, pair, pair, pair])
    res = _pcall(body, grid_spec=grid_spec, out_shape=out_shape, name="project_gather",
                 compiler_params=_params(48, ("arbitrary",)))(table, shard, x, ada, b_in3, *others)
    return res[0], res[1], res[2], res[3], list(res[4:])


def _bias_tables(g):
    window, dil = GROUPS[g]
    span = window // dil
    qi = jnp.arange(BLK)[:, None]
    kj = jnp.arange(2 * BLK)[None, :]
    delta = qi + BLK - kj
    valid = (delta >= 0) & (delta <= span)
    heads = jnp.arange(4, dtype=F32) + 4.0 * g
    slopes = 2.0 ** (-8.0 * (heads + 1.0) / 12.0)
    bias = -slopes[:, None, None] * (delta * dil).astype(F32)[None]
    return jnp.where(valid[None], bias, -1e30).reshape(4 * BLK, 2 * BLK)


def _head_masks(shape):
    lane = lax.broadcasted_iota(jnp.int32, shape, 1)
    return [(lane >= 64 * h) & (lane < 64 * (h + 1)) for h in range(4)]


def _stack_heads(v, masks):
    return jnp.concatenate([jnp.where(masks[h], v, jnp.zeros_like(v)) for h in range(4)], axis=0)


def _unstack_heads(v4, masks):
    out = jnp.where(masks[0], v4[0:BLK], 0.0)
    for h in range(1, 4):
        out = jnp.where(masks[h], v4[BLK * h:BLK * (h + 1)], out)
    return out


def _regroup(load_half, dst_ref, stage_ref, n, dil):
    for hlf in range(2):
        stage_ref[hlf] = load_half(hlf)

    def residue(r, carry):
        for hlf in range(2):
            dst_ref[pl.ds(pl.multiple_of(r * n, BLK), n), pl.ds(128 * hlf, 128)] = (
                stage_ref[hlf, pl.ds(r, n, stride=dil), :].astype(dst_ref.dtype))
        return carry

    lax.fori_loop(0, dil, residue, 0)


def _store_block(nat_ref, r, i, val, dil):
    for hlf in range(2):
        nat_ref[hlf, pl.ds(r + dil * BLK * i, BLK, stride=dil), :] = val[:, 128 * hlf:128 * (hlf + 1)]


def _for_blocks(block, dil, nblk):
    if dil == 1:
        block(0, 0, True)
        block(0, 1, False)

        def pair(k, carry):
            block(0, 2 * k, False)
            block(0, 2 * k + 1, False)
            return carry

        lax.fori_loop(1, nblk // 2, pair, 0)
    else:
        def residues(k, carry):
            block(2 * k, 0, True)
            block(2 * k + 1, 0, True)
            if nblk > 1:
                def loop(i, c):
                    block(2 * k, i, False)
                    block(2 * k + 1, i, False)
                    return c
                lax.fori_loop(1, nblk, loop, 0)
            return carry

        lax.fori_loop(0, dil // 2, residues, 0)


def _attn_forward(qkv, nbat):
    t = qkv.shape[1]
    seq = t // nbat
    n_grp = len(GROUPS)

    def body(qkv_ref, b0_ref, b1_ref, b2_ref, ol_ref, stage, qs_ref, ks_ref, vs_ref, *nat):
        masks = _head_masks((BLK, SLAB))
        bias_refs = (b0_ref, b1_ref, b2_ref)
        for g, (_, dil) in enumerate(GROUPS):
            n = seq // dil
            bias_ref, nat_o, nat_l = bias_refs[g], nat[2 * g], nat[2 * g + 1]
            if dil > 1:
                qd, kd, vd = qs_ref, ks_ref, vs_ref
                for which, dst in enumerate((qd, kd, vd)):
                    _regroup(lambda hlf, which=which, g=g: qkv_ref[3 * which + g, :, pl.ds(128 * hlf, 128)].astype(F32), dst, stage, n, dil)
            else:
                qd, kd, vd = qkv_ref.at[g], qkv_ref.at[3 + g], qkv_ref.at[6 + g]

            def block(r, i, first, n=n, dil=dil, qd=qd, kd=kd, vd=vd, bias_ref=bias_ref, nat_o=nat_o, nat_l=nat_l):
                base = r * n
                qs = pl.ds(pl.multiple_of(base + i * BLK, BLK), BLK)
                ks = pl.ds(pl.multiple_of(base, BLK), BLK) if first else pl.ds(pl.multiple_of(base + (i - 1) * BLK, BLK), 2 * BLK)
                q, kk, vv = qd[qs, :], kd[ks, :], vd[ks, :]
                bias = bias_ref[:, pl.ds(BLK, BLK)] if first else bias_ref[...]
                s = _nt(_stack_heads(q, masks), kk) * 0.125 + bias
                m = jnp.max(s, axis=1, keepdims=True)
                p = jnp.exp(s - m)
                den = jnp.sum(p, axis=1, keepdims=True)
                out = _unstack_heads(_nn((p * (1.0 / den)).astype(BF16), vv), masks)
                lse = _unstack_heads(jnp.broadcast_to(m + jnp.log(den), (4 * BLK, SLAB)), masks)
                _store_block(nat_o, r, i, out, dil)
                _store_block(nat_l, r, i, lse, dil)

            _for_blocks(block, dil, n // BLK)

        for hlf in range(2):
            l0, l1, l2 = nat[1][hlf], nat[3][hlf], nat[5][hlf]
            mx = jnp.maximum(jnp.maximum(l0, l1), l2)
            e0, e1, e2 = jnp.exp(l0 - mx), jnp.exp(l1 - mx), jnp.exp(l2 - mx)
            den = e0 + e1 + e2
            ol_ref[0, :, pl.ds(128 * hlf, 128)] = (e0 * nat[0][hlf] + e1 * nat[2][hlf] + e2 * nat[4][hlf]) * (1.0 / den)
            ol_ref[1, :, pl.ds(128 * hlf, 128)] = mx + jnp.log(den)

    halves = pltpu.VMEM((2, seq, 128), F32)
    bias_spec = pl.BlockSpec((4 * BLK, 2 * BLK), lambda b: (0, 0))
    return _pcall(
        body, grid=(nbat,), out_shape=jax.ShapeDtypeStruct((2, t, SLAB), F32),
        in_specs=[pl.BlockSpec((N_QKV, seq, SLAB), lambda b: (0, b, 0))] + [bias_spec] * n_grp,
        out_specs=pl.BlockSpec((2, seq, SLAB), lambda b: (0, b, 0)),
        scratch_shapes=[halves] + [pltpu.VMEM((seq, SLAB), BF16)] * 3 + [halves] * (2 * n_grp),
        name="attn_forward", compiler_params=_params(56, ("parallel",)))(qkv, *[_bias_tables(g) for g in range(n_grp)])


def _attn_backward(qkv, do_attn, ol_tot, dproj, g, nbat):
    t = qkv.shape[1]
    seq = t // nbat
    dil = GROUPS[g][1]
    n = seq // dil
    nblk = n // BLK
    qkv4 = qkv.reshape(3, 3, t, SLAB)
    dp4 = dproj.reshape(DP_SLABS // 3, 3, t, SLAB)

    def body(qkv_ref, do_ref, ol_ref, bias_ref, dp_in, dp_ref, gb_ref, dk_acc, dv_acc, *scratch):
        del dp_in
        masks = _head_masks((BLK, SLAB))

        @pl.when(pl.program_id(0) == 0)
        def _():
            gb_ref[...] = jnp.zeros_like(gb_ref)

        dk_acc[...] = jnp.zeros_like(dk_acc)
        dv_acc[...] = jnp.zeros_like(dv_acc)
        if dil > 1:
            stage, qd, kd, vd, dod, prodd, lsed, nat = scratch
            lanes = lambda hlf: pl.ds(128 * hlf, 128)
            for which, dst in enumerate((qd, kd, vd)):
                _regroup(lambda hlf, which=which: qkv_ref[which, 0, :, lanes(hlf)].astype(F32), dst, stage, n, dil)
            _regroup(lambda hlf: do_ref[:, lanes(hlf)].astype(F32), dod, stage, n, dil)
            _regroup(lambda hlf: do_ref[:, lanes(hlf)].astype(F32) * ol_ref[0, :, lanes(hlf)], prodd, stage, n, dil)
            _regroup(lambda hlf: ol_ref[1, :, lanes(hlf)], lsed, stage, n, dil)
        else:
            qd, kd, vd = qkv_ref.at[0, 0], qkv_ref.at[1, 0], qkv_ref.at[2, 0]

        def block(r, i, first):
            base = r * n
            qs = pl.ds(pl.multiple_of(base + i * BLK, BLK), BLK)
            ks = pl.ds(pl.multiple_of(base, BLK), BLK) if first else pl.ds(pl.multiple_of(base + (i - 1) * BLK, BLK), 2 * BLK)
            q, kk, vv = qd[qs, :], kd[ks, :], vd[ks, :]
            if dil > 1:
                do, prod, lse = dod[qs, :], prodd[qs, :], lsed[qs, :]
            else:
                do = do_ref[qs, :]
                prod = do.astype(F32) * ol_ref[0, qs, :]
                lse = ol_ref[1, qs, :]
            q4, do4 = _stack_heads(q, masks), _stack_heads(do, masks)
            bias = bias_ref[:, pl.ds(BLK, BLK)] if first else bias_ref[...]
            lse4 = jnp.concatenate([lse[:, 64 * h:64 * h + 1] for h in range(4)], axis=0)
            delta4 = jnp.concatenate([jnp.sum(jnp.where(masks[h], prod, 0.0), axis=1, keepdims=True) for h in range(4)], axis=0)
            p = jnp.exp(_nt(q4, kk) * 0.125 + bias - lse4)
            ds = (p * (_nt(do4, vv) - delta4)).astype(BF16)
            dv_acc[ks, :] += _tn(p.astype(BF16), do4)
            dk_acc[ks, :] += _tn(ds, q4) * 0.125
            dq = _unstack_heads(_nn(ds, kk), masks) * 0.125
            if dil > 1:
                _store_block(nat, r, i, dq, dil)
            else:
                dp_ref[0, 0, qs, :] = dq.astype(BF16)
            gb_ref[0] += _part8(dq)

        _for_blocks(block, dil, nblk)
        gb_ref[1] += _part8(dk_acc[...])
        gb_ref[2] += _part8(dv_acc[...])
        if dil > 1:
            def flush(which):
                for hlf in range(2):
                    dp_ref[which, 0, :, pl.ds(128 * hlf, 128)] = nat[hlf].astype(BF16)

            def to_token_order(acc_ref):
                def residue(r, carry):
                    for hlf in range(2):
                        nat[hlf, pl.ds(r, n, stride=dil), :] = acc_ref[pl.ds(pl.multiple_of(r * n, BLK), n), pl.ds(128 * hlf, 128)]
                    return carry
                lax.fori_loop(0, dil, residue, 0)

            flush(0)
            to_token_order(dk_acc)
            flush(1)
            to_token_order(dv_acc)
            flush(2)
        else:
            dp_ref[1, 0] = dk_acc[...].astype(BF16)
            dp_ref[2, 0] = dv_acc[...].astype(BF16)

    scratch = [pltpu.VMEM((seq, SLAB), F32)] * 2
    if dil > 1:
        scratch += ([pltpu.VMEM((2, seq, 128), F32)] + [pltpu.VMEM((seq, SLAB), BF16)] * 4 + [pltpu.VMEM((seq, SLAB), F32)] * 2
                    + [pltpu.VMEM((2, seq, 128), F32)])
    dp, gb = _pcall(
        body, grid=(nbat,),
        out_shape=(jax.ShapeDtypeStruct(dp4.shape, BF16), jax.ShapeDtypeStruct((3, 8, SLAB), F32)),
        in_specs=[pl.BlockSpec((3, 1, seq, SLAB), lambda b: (0, g, b, 0)),
                  pl.BlockSpec((seq, SLAB), lambda b: (b, 0)),
                  pl.BlockSpec((2, seq, SLAB), lambda b: (0, b, 0)),
                  pl.BlockSpec((4 * BLK, 2 * BLK), lambda b: (0, 0)), ANY],
        out_specs=(pl.BlockSpec((3, 1, seq, SLAB), lambda b: (DP_SLABS // 9 - 1, g, b, 0)),
                   pl.BlockSpec((3, 8, SLAB), lambda b: (0, 0, 0))),
        scratch_shapes=scratch, input_output_aliases={4: 0}, name=f"attn_backward_{g}",
        compiler_params=_params(48, ("arbitrary",)))(qkv4, do_attn, ol_tot, _bias_tables(g), dp4)
    return dp.reshape(DP_SLABS, t, SLAB), gb


def _mid(rest, ol_tot, x, tgt, ada, cw, b_out, ln_g, ln_b, w_pa_t, w_pb, w_out, tm=256):
    t = x.shape[0]
    nbat = ada.shape[0]
    nt = t // tm
    tps = nt // nbat

    def body(rest_ref, halo_ref, ol_ref, x_ref, t_ref, ada_ref, cw_ref, bout_ref, lng_ref, lnb_ref,
             wpat_ref, wpb_ref, wout_ref,
             dp_ref, gx0_ref, doa_ref, mg_ref, dof_ref, bbs_ref, dyc_ref, a_ref, dya_ref,
             gbr_ref, sv_ref, dgate_ref, carry_ref, keep_ref):
        i = pl.program_id(0)
        ti = nt - 1 - i
        pos = ti % tps

        @pl.when(i == 0)
        def _():
            gbr_ref[...] = jnp.zeros_like(gbr_ref)
            sv_ref[...] = jnp.zeros_like(sv_ref)

        @pl.when(pos == tps - 1)
        def _():
            dgate_ref[...] = jnp.zeros_like(dgate_ref)
            carry_ref[...] = jnp.zeros_like(carry_ref)

        row = lax.broadcasted_iota(jnp.int32, (tm, SLAB), 0)
        halo_on = (pos > 0).astype(F32)

        def cols(s):
            return pl.ds(SLAB * s, SLAB)

        o_attn = ol_ref[0]
        z_a = rest_ref[R_ZA]
        sg_za = _sigmoid(z_a)
        a_ref[...] = (o_attn * z_a * sg_za).astype(BF16)
        y_attn = _nt(a_ref[...], wpat_ref[...])

        for s in range(4):
            u = rest_ref[R_GC + s] * rest_ref[R_UX + s]
            hu = halo_ref[R_GC + s] * halo_ref[R_UX + s] * halo_on
            u1 = jnp.where(row == 0, hu[7:8], pltpu.roll(u, 1, 0))
            u2 = jnp.where(row == 0, hu[6:7], jnp.where(row == 1, hu[7:8], pltpu.roll(u, 2, 0)))
            conv = cw_ref[0:1, cols(s)] * u2 + cw_ref[1:2, cols(s)] * u1 + cw_ref[2:3, cols(s)] * u
            zc = rest_ref[R_ZC + s]
            sg = _sigmoid(zc)
            keep_ref[2, :, cols(s)], keep_ref[3, :, cols(s)], keep_ref[4, :, cols(s)], keep_ref[5, :, cols(s)] = u1, u2, conv, sg
            bbs_ref[:, cols(s)] = (rest_ref[R_GB + s] * conv * (zc * sg)).astype(BF16)
        y_conv = _nn(bbs_ref[...], wpb_ref[...])

        for s in range(4):
            s_a, s_b = _sigmoid(rest_ref[R_GA + s]), _sigmoid(rest_ref[R_GBM + s])
            keep_ref[0, :, cols(s)], keep_ref[1, :, cols(s)] = s_a, s_b
            mg_ref[:, cols(s)] = (s_a * y_attn[:, SLAB * s:SLAB * (s + 1)] + s_b * y_conv[:, SLAB * s:SLAB * (s + 1)]).astype(BF16)
        o = _nn(mg_ref[...], wout_ref[...]) + bout_ref[...]
        gate = ada_ref[0, 2:3, :]
        r = ALPHA * x_ref[...] + gate * o
        mu = jnp.mean(r, axis=1, keepdims=True)
        rc = r - mu
        rstd = lax.rsqrt(jnp.mean(rc * rc, axis=1, keepdims=True) + LN_EPS)
        xhat = rc * rstd
        err = xhat * lng_ref[...] + lnb_ref[...] - t_ref[...]
        sv_ref[6] += _part8(err * err)
        dy = err * (1.0 / D)
        sv_ref[0] += _part8(dy * xhat)
        sv_ref[1] += _part8(dy)
        dxh = dy * lng_ref[...]
        dr = rstd * (dxh - jnp.mean(dxh, axis=1, keepdims=True) - xhat * jnp.mean(dxh * xhat, axis=1, keepdims=True))
        gx0_ref[...] = ALPHA * dr
        dgate_ref[0] += _part8(dr * o)
        do_ = dr * gate
        sv_ref[2] += _part8(do_)
        dof_ref[...] = do_.astype(BF16)
        dmerged = _nt(dof_ref[...], wout_ref[...])
        for s in range(4):
            s_a, s_b = keep_ref[0, :, cols(s)], keep_ref[1, :, cols(s)]
            dm = dmerged[:, SLAB * s:SLAB * (s + 1)]
            ya, yc = y_attn[:, SLAB * s:SLAB * (s + 1)], y_conv[:, SLAB * s:SLAB * (s + 1)]
            dya_ref[:, cols(s)] = (dm * s_a).astype(BF16)
            dyc_ref[:, cols(s)] = (dm * s_b).astype(BF16)
            dga = dm * ya * s_a * (1.0 - s_a)
            dgb = dm * yc * s_b * (1.0 - s_b)
            dp_ref[R_GA + s] = dga.astype(BF16)
            dp_ref[R_GBM + s] = dgb.astype(BF16)
            gbr_ref[R_GA + s] += _part8(dga)
            gbr_ref[R_GBM + s] += _part8(dgb)

        da = _nn(dya_ref[...], wpat_ref[...])
        doa_ref[...] = (da * z_a * sg_za).astype(BF16)
        dza = da * o_attn * (sg_za * (1.0 + z_a * (1.0 - sg_za)))
        dp_ref[R_ZA] = dza.astype(BF16)
        gbr_ref[R_ZA] += _part8(dza)

        dbb = _nt(dyc_ref[...], wpb_ref[...])
        for s in range(4):
            ux, gc, zc = rest_ref[R_UX + s], rest_ref[R_GC + s], rest_ref[R_ZC + s]
            u = gc * ux
            u1, u2, conv, sg = keep_ref[2, :, cols(s)], keep_ref[3, :, cols(s)], keep_ref[4, :, cols(s)], keep_ref[5, :, cols(s)]
            gb = rest_ref[R_GB + s]
            d_b = dbb[:, SLAB * s:SLAB * (s + 1)]
            szc = zc * sg
            dgb_ = d_b * conv * szc
            dconv = d_b * gb * szc
            dzc = d_b * gb * conv * (sg * (1.0 + zc * (1.0 - sg)))
            sv_ref[3, :, cols(s)] += _part8(dconv * u2)
            sv_ref[4, :, cols(s)] += _part8(dconv * u1)
            sv_ref[5, :, cols(s)] += _part8(dconv * u)
            nxt = carry_ref[:, cols(s)]
            d1 = jnp.where(row == tm - 1, nxt[0:1], pltpu.roll(dconv, tm - 1, 0))
            d2 = jnp.where(row == tm - 1, nxt[1:2], jnp.where(row == tm - 2, nxt[0:1], pltpu.roll(dconv, tm - 2, 0)))
            carry_ref[:, cols(s)] = dconv[0:8]
            du = cw_ref[2:3, cols(s)] * dconv + cw_ref[1:2, cols(s)] * d1 + cw_ref[0:1, cols(s)] * d2
            dgc, dux = du * ux, du * gc
            for slab, val in ((R_GB + s, dgb_), (R_ZC + s, dzc), (R_GC + s, dgc), (R_UX + s, dux)):
                dp_ref[slab] = val.astype(BF16)
                gbr_ref[slab] += _part8(val)

    def tile(i):
        return nt - 1 - i

    row_blk = lambda i: (tile(i), 0)
    slab_blk = lambda i: (0, tile(i), 0)
    const2 = lambda i: (0, 0)
    const3 = lambda i: (0, 0, 0)
    in_specs = [
        pl.BlockSpec((N_REST, tm, SLAB), slab_blk),
        pl.BlockSpec((N_REST, 8, SLAB), lambda i: (0, jnp.maximum(tile(i) * (tm // 8) - 1, 0), 0)),
        pl.BlockSpec((1, tm, SLAB), slab_blk),
        pl.BlockSpec((tm, D), row_blk), pl.BlockSpec((tm, D), row_blk),
        pl.BlockSpec((1, 3, D), lambda i: (tile(i) // tps, 0, 0)),
        pl.BlockSpec((3, D), const2), pl.BlockSpec((1, D), const2), pl.BlockSpec((1, D), const2), pl.BlockSpec((1, D), const2),
        pl.BlockSpec((D, SLAB), const2), pl.BlockSpec((D, D), const2), pl.BlockSpec((D, D), const2)]
    bf_rows = lambda: jax.ShapeDtypeStruct((t, D), BF16)
    out_shape = (
        jax.ShapeDtypeStruct((DP_SLABS, t, SLAB), BF16), jax.ShapeDtypeStruct((t, D), F32),
        jax.ShapeDtypeStruct((t, SLAB), BF16),
        bf_rows(), bf_rows(), bf_rows(), bf_rows(), jax.ShapeDtypeStruct((t, SLAB), BF16), bf_rows(),
        jax.ShapeDtypeStruct((N_REST, 8, SLAB), F32), jax.ShapeDtypeStruct((7, 8, D), F32),
        jax.ShapeDtypeStruct((nbat, 8, D), F32))
    out_specs = (
        pl.BlockSpec((N_REST, tm, SLAB), slab_blk), pl.BlockSpec((tm, D), row_blk),
        pl.BlockSpec((tm, SLAB), row_blk),
        pl.BlockSpec((tm, D), row_blk), pl.BlockSpec((tm, D), row_blk), pl.BlockSpec((tm, D), row_blk),
        pl.BlockSpec((tm, D), row_blk), pl.BlockSpec((tm, SLAB), row_blk), pl.BlockSpec((tm, D), row_blk),
        pl.BlockSpec((N_REST, 8, SLAB), const3), pl.BlockSpec((7, 8, D), const3),
        pl.BlockSpec((1, 8, D), lambda i: (tile(i) // tps, 0, 0)))
    return _pcall(body, grid=(nt,), out_shape=out_shape, in_specs=in_specs, out_specs=out_specs,
                  scratch_shapes=[pltpu.VMEM((8, D), F32), pltpu.VMEM((6, tm, D), F32)], name="mid",
                  compiler_params=_params(56, ("arbitrary",)))(
        rest, rest, ol_tot, x, tgt, ada, cw, b_out, ln_g, ln_b, w_pa_t, w_pb, w_out)


def _tn_matmul(lhs, rhs, lhs_spec, n_steps, out_rows, out_index, name, after):
    t, n = rhs.shape

    def body(l_ref, r_ref, after_ref, o_ref):
        del after_ref
        o_ref[...] = _tn(l_ref[0] if len(l_ref.shape) == 3 else l_ref[...], r_ref[...])

    return _pcall(body, grid=(n_steps,), out_shape=jax.ShapeDtypeStruct((out_rows, n), F32),
                  in_specs=[lhs_spec, pl.BlockSpec((t, n), lambda j: (0, 0)), ANY],
                  out_specs=pl.BlockSpec((SLAB, n), out_index), name=name,
                  compiler_params=_params(48, ("parallel",)))(lhs, rhs, after)


def _grad_rows_2d(lhs, rhs, name, after, tc=1024):
    t, k = lhs.shape
    n = rhs.shape[1]

    def body(l_ref, r_ref, after_ref, o_ref):
        del after_ref
        part = _tn(l_ref[...], r_ref[...])

        @pl.when(pl.program_id(0) == 0)
        def _():
            o_ref[...] = part

        @pl.when(pl.program_id(0) > 0)
        def _():
            o_ref[...] += part

    return _pcall(body, grid=(t // tc,), out_shape=jax.ShapeDtypeStruct((k, n), F32),
                  in_specs=[pl.BlockSpec((tc, k), lambda i: (i, 0)), pl.BlockSpec((tc, n), lambda i: (i, 0)), ANY],
                  out_specs=pl.BlockSpec((k, n), lambda i: (0, 0)), name=name,
                  compiler_params=_params(32, ("arbitrary",)))(lhs, rhs, after)


def _w_row_block(j):
    return (j + N_QKV) % N_SLAB


def _dp_slab(j):
    return jnp.where(j < N_REST, j, j + 2)


def _grad_w_in_t(dproj, h):
    t = h.shape[0]
    return _tn_matmul(dproj, h, pl.BlockSpec((1, t, SLAB), lambda j: (_dp_slab(j), 0, 0)), N_SLAB, D_IN,
                      lambda j: (_w_row_block(j), 0), "grad_w_in", h)


def _grad_h(dproj, w_in_t, gx0, x, ada, after, tm=512):
    t = x.shape[0]
    nbat = ada.shape[0]
    tps = (t // nbat) // tm

    def body(dp_ref, w_ref, gx0_ref, x_ref, ada_ref, after_ref, gx_ref, dss_ref):
        del after_ref
        i = pl.program_id(0)
        dh = None
        for j in range(N_SLAB):
            slab = j if j < N_REST else j + 2
            part = _nn(dp_ref[slab], w_ref[pl.ds(SLAB * ((j + N_QKV) % N_SLAB), SLAB), :])
            dh = part if dh is None else dh + part
        gx_ref[...] = gx0_ref[...] + dh * (1.0 + ada_ref[0, 1:2, :])

        @pl.when((i % tps) == 0)
        def _():
            dss_ref[...] = jnp.zeros_like(dss_ref)

        dss_ref[0, 0] += _part8(dh)
        dss_ref[0, 1] += _part8(dh * x_ref[...])

    return _pcall(
        body, grid=(t // tm,),
        out_shape=(jax.ShapeDtypeStruct((t, D), F32), jax.ShapeDtypeStruct((nbat, 2, 8, D), F32)),
        in_specs=[pl.BlockSpec((DP_SLABS, tm, SLAB), lambda i: (0, i, 0)),
                  pl.BlockSpec((D_IN, D), lambda i: (0, 0), pipeline_mode=pl.Buffered(1)),
                  pl.BlockSpec((tm, D), lambda i: (i, 0)), pl.BlockSpec((tm, D), lambda i: (i, 0)),
                  pl.BlockSpec((1, 3, D), lambda i: (i // tps, 0, 0)), ANY],
        out_specs=(pl.BlockSpec((tm, D), lambda i: (i, 0)),
                   pl.BlockSpec((1, 2, 8, D), lambda i: (i // tps, 0, 0, 0))),
        name="grad_h", compiler_params=_params(60, ("arbitrary",)))(dproj, w_in_t, gx0, x, ada, after)


def _chip(m):
    x, y, _ = _my_position()
    return (x ^ ((m >> 1) & 1), y ^ (m & 1))


def _exchange_siblings(grads, after, name):
    n = len(grads)

    def body(*refs):
        copies = _sibling_copies(refs[:n], refs[n + 1:2 * n + 1], refs[2 * n + 1], refs[2 * n + 2])
        for cp in copies:
            cp.start()
        for cp in copies:
            cp.wait()

    return _pcall(body, out_shape=tuple(_sibling_zones(grads)), in_specs=[ANY] * (n + 1), out_specs=(ANY,) * n,
                  name=name, scratch_shapes=[pltpu.SemaphoreType.DMA((4 * n,))] * 2)(*grads, after)


def _sibling_zones(grads):
    return [jax.ShapeDtypeStruct((4, g.shape[0] // N_DEV, g.shape[1]), g.dtype) for g in grads]


def _sibling_copies(srcs, lands, send_sems, recv_sems):
    x, y, c = _my_position()
    copies = []
    for a, (src, land) in enumerate(zip(srcs, lands)):
        rows = land.shape[1]
        for m in range(4):
            dev = _flat(*_chip(m), 1 - c)
            copies.append(pltpu.make_async_remote_copy(
                src_ref=src.at[pl.ds(pl.multiple_of(dev * rows, 8), rows), :], dst_ref=land.at[m],
                send_sem=send_sems.at[4 * a + m], recv_sem=recv_sems.at[4 * a + m], device_id=(x, y, 1 - c),
                device_id_type=MESH))
    return copies


def _chip_copies(srcs, lands, send_sems, recv_sems):
    _, _, c = _my_position()
    return [pltpu.make_async_remote_copy(
        src_ref=srcs[a].at[m - 1], dst_ref=lands[a].at[m - 1], send_sem=send_sems.at[3 * a + m - 1],
        recv_sem=recv_sems.at[3 * a + m - 1], device_id=(*_chip(m), c), device_id_type=MESH)
        for a in range(len(srcs)) for m in range(1, 4)]


HBM = pl.BlockSpec(memory_space=pltpu.HBM)
SEM = pl.BlockSpec(memory_space=pltpu.SEMAPHORE)
SPLIT_COPY = pltpu.CompilerParams(has_side_effects=pltpu.SideEffectType.DATAFLOW_SIDE_EFFECTING)


def _start_copies(make_copies, n_sems, srcs, zones, name):
    n = len(srcs)

    def body(*refs):
        for cp in make_copies(refs[:n], refs[n:2 * n], refs[2 * n], refs[2 * n + 1]):
            cp.start()
        refs[-1][...] = jnp.zeros_like(refs[-1])

    hbm = tuple(pltpu.HBM(b.shape, b.dtype) for b in list(srcs) + list(zones))
    out_shape = (pltpu.SemaphoreType.DMA((n_sems,)), pltpu.SemaphoreType.DMA((n_sems,))) + hbm + (jax.ShapeDtypeStruct((8, 128), F32),)
    operands = [pltpu.with_memory_space_constraint(b, pltpu.HBM) for b in srcs]
    operands += [pltpu.with_memory_space_constraint(lax.empty(z.shape, z.dtype), pltpu.HBM) for z in zones]
    res = _pcall(body, out_shape=out_shape, in_specs=[HBM] * (2 * n), out_specs=(SEM, SEM) + (HBM,) * (2 * n) + (VMEM,),
                 input_output_aliases={i: 2 + i for i in range(2 * n)}, name=name, compiler_params=SPLIT_COPY)(*operands)
    return (res[0], res[1], res[2:2 + n], res[2 + n:2 + 2 * n]), res[-1]


def _wait_copies(make_copies, flight, after, name):
    send_sems, recv_sems, srcs, zones = flight
    n = len(srcs)

    def body(*refs):
        for cp in make_copies(refs[:n], refs[n:2 * n], refs[2 * n], refs[2 * n + 1]):
            cp.wait_send()
            cp.wait_recv()

    hbm = tuple(pltpu.HBM(b.shape, b.dtype) for b in list(srcs) + list(zones))
    res = _pcall(body, out_shape=hbm, in_specs=[HBM] * (2 * n) + [SEM, SEM, ANY], out_specs=(HBM,) * (2 * n),
                 input_output_aliases={i: i for i in range(2 * n)}, name=name, compiler_params=SPLIT_COPY)(
        *srcs, *zones, send_sems, recv_sems, after)
    return res[:n], res[n:]


def _pair_sums(devs, grads, lands, n_steps, name):
    n = len(grads)
    rows = [l.shape[1] for l in lands]
    rbs = [r // n_steps for r in rows]

    def body(devs_ref, *refs):
        del devs_ref
        g_refs, land_refs, outs = refs[:4 * n], refs[4 * n:5 * n], refs[5 * n:]
        for a in range(n):
            outs[2 * a][...] = g_refs[4 * a][...] + land_refs[a][0]
            for m in range(1, 4):
                outs[2 * a + 1][m - 1] = (g_refs[4 * a + m][...] + land_refs[a][m]).astype(BF16)

    def block_of(m, per_dev):
        return lambda i, devs_ref: (devs_ref[m] * per_dev + i, 0)

    in_specs = [pl.BlockSpec((rb, l.shape[2]), block_of(m, n_steps)) for rb, l in zip(rbs, lands) for m in range(4)]
    in_specs += [pl.BlockSpec((4, rb, l.shape[2]), lambda i, devs_ref: (0, i, 0)) for rb, l in zip(rbs, lands)]
    out_shape, out_specs = [], []
    for rb, l in zip(rbs, lands):
        out_shape += [jax.ShapeDtypeStruct(l.shape[1:], F32), jax.ShapeDtypeStruct((3,) + l.shape[1:], BF16)]
        out_specs += [pl.BlockSpec((rb, l.shape[2]), lambda i, devs_ref: (i, 0)),
                      pl.BlockSpec((3, rb, l.shape[2]), lambda i, devs_ref: (0, i, 0))]
    grid_spec = pltpu.PrefetchScalarGridSpec(num_scalar_prefetch=1, grid=(n_steps,), in_specs=in_specs, out_specs=tuple(out_specs))
    res = _pcall(body, grid_spec=grid_spec, out_shape=tuple(out_shape), name=name,
                 compiler_params=_params(48, ("parallel",)))(devs, *[g for g in grads for _ in range(4)], *lands)
    return res[0::2], res[1::2]


def _final_sums(mine, lands, n_steps, name):
    n = len(mine)
    rbs = [o.shape[0] // n_steps for o in mine]

    def body(*refs):
        mine_refs, land_refs, outs = refs[:n], refs[n:2 * n], refs[2 * n:]
        for a in range(n):
            tot = mine_refs[a][...]
            for m in range(3):
                tot = tot + land_refs[a][m].astype(F32)
            outs[a][...] = tot

    in_specs = ([pl.BlockSpec((rb, o.shape[1]), lambda i: (i, 0)) for rb, o in zip(rbs, mine)]
                + [pl.BlockSpec((3, rb, o.shape[1]), lambda i: (0, i, 0)) for rb, o in zip(rbs, mine)])
    out_specs = tuple(pl.BlockSpec((rb, o.shape[1]), lambda i: (i, 0)) for rb, o in zip(rbs, mine))
    out_shape = tuple(jax.ShapeDtypeStruct(o.shape, F32) for o in mine)
    return _pcall(body, grid=(n_steps,), out_shape=out_shape, in_specs=in_specs, out_specs=out_specs, name=name,
                  compiler_params=_params(32, ("parallel",)))(*mine, *lands)


def _reduce_scatter_begin(big, small_after_start):
    c = lax.axis_index("c")
    devs = jnp.stack([_flat(*_chip(m), c) for m in range(4)]).astype(jnp.int32)
    flight, token = _start_copies(_sibling_copies, 4, [big], _sibling_zones([big]), "siblings_start")
    small = small_after_start(token)
    (big,), big_lands = _wait_copies(_sibling_copies, flight, small[-1], "siblings_wait")
    big_mine, big_send = _pair_sums(devs, [big], big_lands, 4, "pair_sums_w_in")
    big_flight, token = _start_copies(_chip_copies, 3, list(big_send), list(big_send), "chips_start_w_in")
    small_lands = _exchange_siblings(small, token, "exchange_siblings_rest")
    small_mine, small_send = _pair_sums(devs, small, small_lands, 1, "pair_sums_rest")
    small_flight, token = _start_copies(_chip_copies, 3 * len(small), list(small_send), list(small_send), "chips_start_rest")
    return (big_flight, small_flight, list(big_mine) + list(small_mine)), token


def _reduce_scatter_end(state, after):
    big_flight, small_flight, mine = state
    _, big_got = _wait_copies(_chip_copies, big_flight, after, "chips_wait_w_in")
    _, small_got = _wait_copies(_chip_copies, small_flight, after, "chips_wait_rest")
    small = _final_sums(mine[1:], small_got, 1, "final_sums_rest")
    return (mine[0], big_got[0]), list(small)


def _adamw(w, g, m, v):
    m_new = B1 * m + (1.0 - B1) * g
    v_new = B2 * v + (1.0 - B2) * (g * g)
    m_hat = m_new / (1.0 - B1 ** STEP)
    v_hat = v_new / (1.0 - B2 ** STEP)
    delta = -LR * (m_hat / (jnp.sqrt(v_hat) + EPS) + WD * w)
    return delta, m_new, v_new


def _final_sum_adam_rows(mine, land, w, m, v, n_steps, name):
    rows, ncol = w.shape
    blk = pl.BlockSpec((rows // n_steps, ncol), lambda i: (i, 0))

    def body(mine_ref, land_ref, w_ref, m_ref, v_ref, g_ref, d_ref, mo_ref, vo_ref):
        g = mine_ref[...]
        for k in range(3):
            g = g + land_ref[k].astype(F32)
        g_ref[...] = g
        d_ref[...], mo_ref[...], vo_ref[...] = _adamw(w_ref[...], g, m_ref[...], v_ref[...])

    shape = jax.ShapeDtypeStruct(w.shape, F32)
    return _pcall(body, grid=(n_steps,), out_shape=(shape,) * 4,
                  in_specs=[blk, pl.BlockSpec((3, rows // n_steps, ncol), lambda i: (0, i, 0)), blk, blk, blk],
                  out_specs=(blk,) * 4, name=name, compiler_params=_params(32, ("parallel",)))(mine, land, w, m, v)


def _adam_transposed(g_t, w, m, v, name):
    n, k = g_t.shape
    rb = min(k, 128)

    def body(gt_ref, w_ref, m_ref, v_ref, g_ref, d_ref, mo_ref, vo_ref):
        for src, skip, dst, size in _column_chunks(n):
            sl = pl.ds(dst, size)
            g = gt_ref[pl.ds(src, 128), :].T[:, skip:]
            delta, m_new, v_new = _adamw(w_ref[:, sl], g, m_ref[:, sl], v_ref[:, sl])
            g_ref[:, sl], d_ref[:, sl], mo_ref[:, sl], vo_ref[:, sl] = g, delta, m_new, v_new

    shape = jax.ShapeDtypeStruct(w.shape, F32)
    rows = pl.BlockSpec((rb, n), lambda i: (i, 0))
    return _pcall(body, grid=(k // rb,), out_shape=(shape,) * 4,
                  in_specs=[pl.BlockSpec((n, rb), lambda i: (0, i)), rows, rows, rows], out_specs=(rows,) * 4, name=name,
                  compiler_params=_params(32, ("parallel",)))(g_t, w, m, v)


def _adam_many(items, name):
    n = len(items)

    def body(*refs):
        ins, outs = refs[:4 * n], refs[4 * n:]
        for a in range(n):
            w_ref, g_ref, m_ref, v_ref = ins[4 * a:4 * a + 4]
            delta, m_new, v_new = _adamw(w_ref[...], g_ref[...], m_ref[...], v_ref[...])
            outs[3 * a][...], outs[3 * a + 1][...], outs[3 * a + 2][...] = delta, m_new, v_new

    out_shape = tuple(jax.ShapeDtypeStruct(it[0].shape, F32) for it in items for _ in range(3))
    flat = [arr for it in items for arr in it]
    res = _pcall(body, grid=(1,), out_shape=out_shape, in_specs=[_whole(a) for a in flat],
                 out_specs=tuple(_whole(o) for o in out_shape), name=name, compiler_params=_params(32))(*flat)
    return [tuple(res[3 * a:3 * a + 3]) for a in range(n)]


def _adam_w_ada(cact_all, dada_mine, w, m, v):
    def body(c_ref, d_ref, w_ref, m_ref, v_ref, g_ref, dl_ref, mo_ref, vo_ref):
        g = _tn(c_ref[...].astype(BF16), d_ref[...].astype(BF16))
        delta, m_new, v_new = _adamw(w_ref[...], g, m_ref[...], v_ref[...])
        g_ref[...], dl_ref[...], mo_ref[...], vo_ref[...] = g, delta, m_new, v_new

    shape = jax.ShapeDtypeStruct(w.shape, F32)
    operands = (cact_all, dada_mine, w, m, v)
    return _pcall(body, grid=(1,), out_shape=(shape,) * 4, in_specs=[_whole(a) for a in operands],
                  out_specs=(_whole(w),) * 4, name="adam_w_ada", compiler_params=_params(32))(*operands)


def kernel(x, c, w_ada, b_ada, w_in, b_in, conv_w, w_proj_attn, w_proj_conv, w_out, b_out, ln_g, ln_b, loss_target, m_w_ada, m_b_ada, m_w_in, m_b_in, m_conv_w, m_w_proj_attn, m_w_proj_conv, m_w_out, m_b_out, m_ln_g, m_ln_b, v_w_ada, v_b_ada, v_w_in, v_b_in, v_conv_w, v_w_proj_attn, v_w_proj_conv, v_w_out, v_b_out, v_ln_g, v_ln_b):
    nbat, seq, _ = x.shape
    t = nbat * seq
    me = _flat(*_my_position())
    x2, tgt2 = x.reshape(t, D), loss_target.reshape(t, D)
    sq = lambda a: a.reshape(a.shape[1:])

    tr = lambda a: a[0].T
    w_in_rows = tr(w_in)
    w_in_t_s = _cast_rows(w_in_rows, 4, "cast_w_in")
    w_pa_t_s, w_pb_s, w_out_s, cact_s, cw_s = _prep(sq(w_proj_attn), sq(w_proj_conv), sq(w_out), c, sq(conv_w))

    ncol = w_ada.shape[2]
    b_ada_mine = lax.dynamic_slice(b_ada, (0, me * ncol), (1, ncol))
    ada_slots, cact_slots, cw_slots = _ada_forward(cact_s, cw_s, sq(w_ada), b_ada_mine)
    cact_all = cact_slots[:, :nbat].reshape(N_DEV * nbat, D)
    cw = cw_slots[:, :3].transpose(1, 0, 2).reshape(3, D)
    ada_all = ada_slots[:, :, :nbat].transpose(1, 2, 0, 3).reshape(N_DEV * nbat, 3, D)
    ada = lax.dynamic_slice(ada_all, (me * nbat, 0, 0), (nbat, 3, D))

    w_in_t, qkv, rest, h, (w_pa_t, w_pb, w_o) = _project_gather(
        w_in_t_s, x2, ada, b_in.reshape(N_SLAB, 1, SLAB), [w_pa_t_s, w_pb_s, w_out_s])
    ol_tot = _attn_forward(qkv, nbat)
    (dproj, gx0, do_attn, merged, do_f, bbs, dyc, a_bf, dya, gb_rest, svec, dgate) = _mid(
        rest, ol_tot, x2, tgt2, ada, cw, b_out, ln_g, ln_b, w_pa_t, w_pb, w_o)

    gb_qkv = []
    for g in range(3):
        dproj, gb = _attn_backward(qkv, do_attn, ol_tot, dproj, g, nbat)
        gb_qkv.append(gb)
    g_w_in_t = _grad_w_in_t(dproj, h)

    def small_grads(token):
        g_w_out = _grad_rows_2d(merged, do_f, "grad_w_out", token)
        g_w_pb = _grad_rows_2d(bbs, dyc, "grad_w_proj_conv", g_w_out)
        g_w_pa_t = _grad_rows_2d(dya, a_bf, "grad_w_proj_attn", g_w_pb)
        return [g_w_out, g_w_pb, g_w_pa_t]

    rs_state, token = _reduce_scatter_begin(g_w_in_t, small_grads)
    grad_x, dss = _grad_h(dproj, w_in_t, gx0, x2, ada, token)

    rows8, tot, g_bada = _small_reduce(gb_rest, gb_qkv, svec, dgate, dss)
    (g_in_mine, g_in_got), (g_out, g_pb, g_pa_t) = _reduce_scatter_end(rs_state, tot)
    loss = tot[0, P_LOSS]
    dada_all = rows8[:, 0, P_DADA:].reshape(N_DEV * nbat, 3 * D)
    dada_mine = lax.dynamic_slice(dada_all, (0, me * ncol), (N_DEV * nbat, ncol))

    g_in_t, d_win_t, nm_win_t, nv_win_t = _final_sum_adam_rows(g_in_mine, g_in_got, w_in_rows, tr(m_w_in), tr(v_w_in), 4, "adam_w_in")
    g_win, d_win, nm_win, nv_win = g_in_t.T, d_win_t.T, nm_win_t.T, nv_win_t.T
    g_wpa, d_wpa, nm_wpa, nv_wpa = _adam_transposed(g_pa_t, sq(w_proj_attn), sq(m_w_proj_attn), sq(v_w_proj_attn), "adam_w_proj_attn")
    g_wada, d_wada, nm_wada, nv_wada = _adam_w_ada(cact_all, dada_mine, sq(w_ada), sq(m_w_ada), sq(v_w_ada))
    g_bin = tot[:, P_BIN:P_BIN + D_IN]
    g_bout = tot[:, P_BOUT:P_BOUT + D]
    g_lng = tot[:, P_LNG:P_LNG + D]
    g_lnb = tot[:, P_LNB:P_LNB + D]
    g_conv = lax.dynamic_slice(tot[:, P_CONV:P_CONV + 3 * D].reshape(3, D), (0, me * cw_s.shape[1]), (3, cw_s.shape[1]))
    upd = _adam_many([
        (sq(w_proj_conv), g_pb, sq(m_w_proj_conv), sq(v_w_proj_conv)),
        (sq(w_out), g_out, sq(m_w_out), sq(v_w_out)),
        (b_ada, g_bada, m_b_ada, v_b_ada), (b_in, g_bin, m_b_in, v_b_in), (sq(conv_w), g_conv, sq(m_conv_w), sq(v_conv_w)),
        (b_out, g_bout, m_b_out, v_b_out), (ln_g, g_lng, m_ln_g, v_ln_g), (ln_b, g_lnb, m_ln_b, v_ln_b)], "adam_rest")
    (d_wpb, nm_wpb, nv_wpb), (d_wout, nm_wout, nv_wout), (d_bada, nm_bada, nv_bada), (d_bin, nm_bin, nv_bin), \
        (d_conv, nm_conv, nv_conv), (d_bout, nm_bout, nv_bout), (d_lng, nm_lng, nv_lng), (d_lnb, nm_lnb, nv_lnb) = upd

    ex = lambda a: a.reshape((1,) + a.shape)
    grads = [ex(g_wada), g_bada, ex(g_win), g_bin, ex(g_conv), ex(g_wpa), ex(g_pb), ex(g_out), g_bout, g_lng, g_lnb]
    deltas = [ex(d_wada), d_bada, ex(d_win), d_bin, ex(d_conv), ex(d_wpa), ex(d_wpb), ex(d_wout), d_bout, d_lng, d_lnb]
    new_m = [ex(nm_wada), nm_bada, ex(nm_win), nm_bin, ex(nm_conv), ex(nm_wpa), ex(nm_wpb), ex(nm_wout), nm_bout, nm_lng, nm_lnb]
    new_v = [ex(nv_wada), nv_bada, ex(nv_win), nv_bin, ex(nv_conv), ex(nv_wpa), ex(nv_wpb), ex(nv_wout), nv_bout, nv_lng, nv_lnb]
    return (loss, grad_x.reshape(x.shape), *grads, *deltas, *new_m, *new_v)
```

```python
import jax
import jax.numpy as jnp
from jax import lax
from jax.experimental import pallas as pl
from jax.experimental.pallas import tpu as pltpu

F32, BF16 = jnp.float32, jnp.bfloat16
MESH = pl.DeviceIdType.MESH
N_DEV = 8
D = 1024
SLAB = 256
N_QKV, N_REST = 9, 25
N_SLAB = N_QKV + N_REST
D_IN = N_SLAB * SLAB
DP_SLABS = 36
BLK = 128
WAYS = 4
GROUPS = ((128, 1), (512, 4), (2048, 16))
ALPHA = 2.0 ** 0.25
LN_EPS = 1e-5
LR, B1, B2, EPS, WD, STEP = 0.001, 0.9, 0.999, 1e-08, 0.01, 10
R_ZA, R_UX, R_GB, R_GC, R_ZC, R_GA, R_GBM = 0, 1, 5, 9, 13, 17, 21
P_BIN, P_BOUT, P_LNG, P_LNB, P_CONV, P_LOSS, P_DADA = 0, 8704, 9728, 10752, 11776, 14848, 14976
MIB = 1024 * 1024


def _pcall(body, *, out_shape, out_specs=None, **kw):
    def pin_out(shape, spec):
        in_hbm = getattr(spec, "block_shape", None) is not None or getattr(spec, "memory_space", None) is pl.ANY
        return pltpu.HBM(shape.shape, shape.dtype) if in_hbm and isinstance(shape, jax.ShapeDtypeStruct) else shape

    n_scalar = 0
    if out_specs is None:
        specs = kw["grid_spec"].out_specs
        n_scalar = kw["grid_spec"].num_scalar_prefetch
    else:
        kw["out_specs"] = specs = out_specs
    if isinstance(out_shape, (tuple, list)):
        out_shape = tuple(pin_out(s, p) for s, p in zip(out_shape, specs))
    else:
        out_shape = pin_out(out_shape, specs)
    call = pl.pallas_call(body, out_shape=out_shape, **kw)

    def run(*operands):
        def pin(o):
            is_data = jnp.issubdtype(o.dtype, jnp.floating) or jnp.issubdtype(o.dtype, jnp.integer)
            return pltpu.with_memory_space_constraint(o, pltpu.HBM) if is_data else o
        return call(*operands[:n_scalar], *[pin(o) for o in operands[n_scalar:]])

    return run

ANY = pl.BlockSpec(memory_space=pl.ANY)
VMEM = pl.BlockSpec(memory_space=pltpu.VMEM)


def _whole(a):
    return pl.BlockSpec(a.shape, lambda i: (0,) * len(a.shape))


def _params(vmem_mib=None, sem=None):
    kw = {}
    if vmem_mib is not None:
        kw["vmem_limit_bytes"] = vmem_mib * MIB
    if sem is not None:
        kw["dimension_semantics"] = sem
    return pltpu.CompilerParams(**kw)


def _nn(a, b):
    return jnp.dot(a, b, preferred_element_type=F32)


def _nt(a, b):
    return lax.dot_general(a, b, (((1,), (1,)), ((), ())), preferred_element_type=F32)


def _tn(a, b):
    return lax.dot_general(a, b, (((0,), (0,)), ((), ())), preferred_element_type=F32)


def _sigmoid(v):
    return 0.5 * jnp.tanh(0.5 * v) + 0.5


def _part8(v):
    return v.reshape(v.shape[0] // 8, 8, v.shape[1]).sum(axis=0)


def _my_position():
    return lax.axis_index("x"), lax.axis_index("y"), lax.axis_index("c")


def _flat(px, py, pc):
    return 4 * px + 2 * py + pc


def _peer(mask):
    x, y, c = _my_position()
    return (x ^ ((mask >> 2) & 1), y ^ ((mask >> 1) & 1), c ^ (mask & 1))


def _column_chunks(n):
    chunks = [(128 * a, 0, 128 * a, 128) for a in range(n // 128)]
    if n % 128:
        chunks.append((n - 128, 128 - n % 128, 128 * (n // 128), n % 128))
    return chunks


def _cast_rows(w, n_steps, name):
    rows, ncol = w.shape
    blk = pl.BlockSpec((rows // n_steps, ncol), lambda i: (i, 0))

    def body(w_ref, o_ref):
        o_ref[...] = w_ref[...].astype(BF16)

    return _pcall(body, grid=(n_steps,), out_shape=jax.ShapeDtypeStruct(w.shape, BF16), in_specs=[blk], out_specs=blk,
                  name=name, compiler_params=_params(16, ("parallel",)))(w)


def _prep(w_pa, w_pb, w_out, c, conv_w):
    def body(wpa_ref, wpb_ref, wout_ref, c_ref, cw_ref, wpat_ref, wpb_o, wout_o, cact_ref, cwp_ref):
        wpat_ref[...] = wpa_ref[...].T.astype(BF16)
        wpb_o[...] = wpb_ref[...].astype(BF16)
        wout_o[...] = wout_ref[...].astype(BF16)
        cv = c_ref[...]
        cact_ref[...] = jnp.zeros_like(cact_ref)
        cact_ref[pl.ds(0, cv.shape[0]), :] = cv * _sigmoid(cv)
        cwp_ref[...] = jnp.zeros_like(cwp_ref)
        cwp_ref[pl.ds(0, 3), :] = cw_ref[...]

    out_shape = (jax.ShapeDtypeStruct((w_pa.shape[1], w_pa.shape[0]), BF16),
                 jax.ShapeDtypeStruct(w_pb.shape, BF16), jax.ShapeDtypeStruct(w_out.shape, BF16),
                 jax.ShapeDtypeStruct((8, D), F32), jax.ShapeDtypeStruct((8, conv_w.shape[1]), F32))
    operands = (w_pa, w_pb, w_out, c, conv_w)
    return _pcall(body, grid=(1,), out_shape=out_shape, in_specs=[_whole(a) for a in operands],
                  out_specs=tuple(_whole(o) for o in out_shape), name="prep", compiler_params=_params(16))(*operands)


def _exchange_slots(out_refs, send_sems, recv_sems, base=0):
    me = _flat(*_my_position())

    def copy(a, mask, slot):
        return pltpu.make_async_remote_copy(
            src_ref=out_refs[a].at[slot], dst_ref=out_refs[a].at[slot], send_sem=send_sems.at[base + 7 * a + mask - 1],
            recv_sem=recv_sems.at[base + 7 * a + mask - 1], device_id=_peer(mask), device_id_type=MESH)

    pairs = [(a, mask) for a in range(len(out_refs)) for mask in range(1, N_DEV)]
    for a, mask in pairs:
        copy(a, mask, me).start()
    for a, mask in pairs:
        copy(a, mask, _flat(*_peer(mask))).wait_recv()
    for a, mask in pairs:
        copy(a, mask, me).wait_send()


def _ada_forward(cact_mine, cw_mine, w_ada, b_ada_mine):
    ncol = w_ada.shape[1]

    def body(c_ref, cw_ref, w_ref, b_ref, out_ref, call_ref, cwall_ref, send_sems, recv_sems):
        me = _flat(*_my_position())
        call_ref[me] = c_ref[...]
        cwall_ref[me] = cw_ref[...]
        _exchange_slots([call_ref, cwall_ref], send_sems, recv_sems)
        c_all = call_ref[...].reshape(N_DEV * 8, D).astype(BF16)
        out_ref[me] = (_nn(c_all, w_ref[...].astype(BF16)) + b_ref[...]).reshape(N_DEV, 8, ncol)
        _exchange_slots([out_ref], send_sems, recv_sems, base=14)

    operands = (cact_mine, cw_mine, w_ada, b_ada_mine)
    out_shape = (jax.ShapeDtypeStruct((N_DEV, N_DEV, 8, ncol), F32), jax.ShapeDtypeStruct((N_DEV, 8, D), F32),
                 jax.ShapeDtypeStruct((N_DEV,) + cw_mine.shape, F32))
    return _pcall(body, grid=(1,), out_shape=out_shape, in_specs=[_whole(a) for a in operands], out_specs=(VMEM,) * 3,
                  scratch_shapes=[pltpu.SemaphoreType.DMA((21,)), pltpu.SemaphoreType.DMA((21,))], name="ada_forward",
                  compiler_params=_params(16))(*operands)


def _small_reduce(gb_rest, gb_qkv, svec, dgate, dss):
    nbat = dgate.shape[0]

    def body(gbr_ref, q0_ref, q1_ref, q2_ref, sv_ref, dg_ref, dss_ref, rows_ref, tot_ref, gbada_ref, send_sems, recv_sems):
        me = _flat(*_my_position())

        def put(off, v):
            rows_ref[me, :, pl.ds(off, v.shape[1])] = v

        def row(v):
            return jnp.sum(v, axis=0, keepdims=True)

        for g, q_ref in enumerate((q0_ref, q1_ref, q2_ref)):
            for which in range(3):
                put(P_BIN + SLAB * (3 * which + g), row(q_ref[which]))
        for s in range(N_REST):
            put(P_BIN + SLAB * (N_QKV + s), row(gbr_ref[s]))
        put(P_LNG, row(sv_ref[0]))
        put(P_LNB, row(sv_ref[1]))
        put(P_BOUT, row(sv_ref[2]))
        for j in range(3):
            put(P_CONV + D * j, row(sv_ref[3 + j]))
        loss = (0.5 / D) * jnp.sum(row(sv_ref[6]), axis=1, keepdims=True)
        put(P_LOSS, jnp.broadcast_to(loss, (1, 128)))
        for b in range(nbat):
            put(P_DADA + 3 * D * b, row(dss_ref[b, 0]))
            put(P_DADA + 3 * D * b + D, row(dss_ref[b, 1]))
            put(P_DADA + 3 * D * b + 2 * D, row(dg_ref[b]))
        _exchange_slots([rows_ref], send_sems, recv_sems)
        tot = rows_ref[0]
        for k in range(1, N_DEV):
            tot = tot + rows_ref[k]
        tot_ref[...] = tot
        gbada = tot[:, P_DADA:P_DADA + 3 * D]
        for b in range(1, nbat):
            gbada = gbada + tot[:, P_DADA + 3 * D * b:P_DADA + 3 * D * (b + 1)]
        gbada_ref[...] = gbada

    p_len = P_DADA + nbat * 3 * D
    out_shape = (jax.ShapeDtypeStruct((N_DEV, 1, p_len), F32), jax.ShapeDtypeStruct((1, p_len), F32),
                 jax.ShapeDtypeStruct((1, 3 * D), F32))
    operands = (gb_rest, *gb_qkv, svec, dgate, dss)
    return _pcall(body, grid=(1,), out_shape=out_shape, in_specs=[_whole(a) for a in operands],
                  out_specs=(VMEM, _whole(out_shape[1]), _whole(out_shape[2])),
                  scratch_shapes=[pltpu.SemaphoreType.DMA((7,)), pltpu.SemaphoreType.DMA((7,))], name="small_reduce",
                  compiler_params=_params(16))(*operands)


PIECE = 64
N_CHUNK = 4
ARRIVAL_RANK = (0, 1, 3, 5, 2, 4, 6, 7)
SLOT_MASK = (1, 4, 2, 6, 5, 3, 7)


def _arrival_tables(shard_rows):
    import numpy as np
    crow = shard_rows // N_CHUNK
    table = np.zeros((N_DEV, N_SLAB + 7 * N_CHUNK), np.int32)
    lo = [(SLAB * j) // crow for j in range(N_SLAB)]
    hi = [(SLAB * j + SLAB - 1) // crow for j in range(N_SLAB)]
    for k in range(N_DEV):
        def rank(chunk):
            shard_rank = ARRIVAL_RANK[(chunk // N_CHUNK) ^ k]
            return shard_rank if shard_rank < 2 else 2 + 8 * (chunk % N_CHUNK) + shard_rank
        order = sorted(range(N_SLAB), key=lambda j: (max(rank(lo[j]), rank(hi[j])), j))
        table[k, :N_SLAB] = order
        for slot, mask in enumerate(SLOT_MASK):
            for ch in range(N_CHUNK):
                chunk = (k ^ mask) * N_CHUNK + ch
                table[k, N_SLAB + slot * N_CHUNK + ch] = min(t for t, j in enumerate(order) if lo[j] <= chunk <= hi[j])
    return table


def _project_gather(shard, x, ada, b_in3, others, xt=512):
    t = x.shape[0]
    n_o = len(others)
    srows = shard.shape[0]
    crow = srows // N_CHUNK
    shards = [shard] + list(others)
    table = jnp.asarray(_arrival_tables(srows))
    seq_tiles = (t // ada.shape[0]) // xt

    def body(tbl_ref, *refs):
        srcs = [refs[0]] + list(refs[4:4 + n_o])
        x_ref, ada_ref, b_ref = refs[1], refs[2], refs[3]
        outs = [refs[4 + n_o]] + list(refs[8 + n_o:8 + 2 * n_o])
        qkv_ref, rest_ref, h_out = refs[5 + n_o], refs[6 + n_o], refs[7 + n_o]
        (wtile, obf, of32, h_ref, xbuf, send_sems, recv_sems, local_sems, tile_sems, obf_sems, of32_sems, x_sems,
         h_sems) = refs[8 + 2 * n_o:]
        w_full = outs[0]
        x, y, c = _my_position()
        k = _flat(x, y, c)
        me, sibling = (x, y, c), (x, y, 1 - c)
        chips = [(1 - x, y), (x, 1 - y), (1 - x, 1 - y)]

        def rows(a, px, py, pc, ch):
            r = shards[a].shape[0]
            if ch is None:
                return outs[a].at[pl.ds(pl.multiple_of(_flat(px, py, pc) * r, r), r), :]
            return outs[a].at[pl.ds(pl.multiple_of(_flat(px, py, pc) * r + ch * crow, crow), crow), :]

        def copy(a, slot, block, to, ch=None, src=None):
            sem = slot * N_CHUNK + ch if a == 0 else 7 * (N_CHUNK - 1 + a) + slot
            if src is not None and ch is not None:
                src = src.at[pl.ds(ch * crow, crow), :]
            return pltpu.make_async_remote_copy(
                src_ref=rows(a, *block, ch) if src is None else src, dst_ref=rows(a, *block, ch),
                send_sem=send_sems.at[sem], recv_sem=recv_sems.at[sem], device_id=to, device_id_type=MESH)

        mine = [pltpu.make_async_copy(srcs[a], rows(a, *me, None), local_sems.at[a]) for a in range(1 + n_o)]
        first = []
        for ch in range(N_CHUNK):
            first.append(copy(0, 0, me, sibling, ch, src=srcs[0]))
            first += [copy(0, 1 + j, me, (*chip, c), ch, src=srcs[0]) for j, chip in enumerate(chips)]
        for a in range(1, 1 + n_o):
            first.append(copy(a, 0, me, sibling, src=srcs[a]))
            first += [copy(a, 1 + j, me, (*chip, c), src=srcs[a]) for j, chip in enumerate(chips)]
        for cp in mine + first:
            cp.start()

        def arrive(a, slot, ch=None):
            if slot == 0:
                copy(a, 0, sibling, me, ch).wait_recv()
            elif slot < 4:
                copy(a, slot, (*chips[slot - 1], c), me, ch).wait_recv()
                copy(a, slot + 3, (*chips[slot - 1], c), sibling, ch).start()
            else:
                copy(a, slot, (*chips[slot - 4], 1 - c), me, ch).wait_recv()

        def arrive_for(step):
            for slot in range(7):
                for ch in range(N_CHUNK):
                    @pl.when(tbl_ref[k, N_SLAB + slot * N_CHUNK + ch] == step)
                    def _():
                        arrive(0, slot, ch)

        def fetch(step, buf):
            slab = tbl_ref[k, step]
            for p in range(SLAB // PIECE):
                g0 = slab * SLAB + PIECE * p
                own = (g0 >= k * srows) & (g0 < (k + 1) * srows)
                dst = wtile.at[buf, pl.ds(PIECE * p, PIECE), :]

                @pl.when(own)
                def _():
                    pltpu.make_async_copy(srcs[0].at[pl.ds(pl.multiple_of(g0 - k * srows, PIECE), PIECE), :], dst, tile_sems.at[buf]).start()

                @pl.when(jnp.logical_not(own))
                def _():
                    pltpu.make_async_copy(w_full.at[pl.ds(pl.multiple_of(g0, PIECE), PIECE), :], dst, tile_sems.at[buf]).start()

        def wait_tile(buf):
            pltpu.make_async_copy(w_full.at[pl.ds(0, SLAB), :], wtile.at[buf], tile_sems.at[buf]).wait()

        def put(buf_ref, sems, dst_ref, count, value):
            b = count % 2

            @pl.when(count >= 2)
            def _():
                pltpu.make_async_copy(buf_ref.at[b], dst_ref, sems.at[b]).wait()

            buf_ref[b] = value
            pltpu.make_async_copy(buf_ref.at[b], dst_ref, sems.at[b]).start()

        def drain(buf_ref, sems, dst_ref, count):
            for back in (1, 2):
                @pl.when(count >= back)
                def _():
                    pltpu.make_async_copy(buf_ref.at[(count - back) % 2], dst_ref, sems.at[(count - back) % 2]).wait()

        def x_copy(i):
            return pltpu.make_async_copy(x_ref.at[pl.ds(xt * i, xt), :], xbuf.at[i % 2], x_sems.at[i % 2])

        def h_copy(i):
            return pltpu.make_async_copy(h_ref.at[pl.ds(xt * i, xt), :], h_out.at[pl.ds(xt * i, xt), :], h_sems.at[i % 2])

        x_copy(0).start()
        for i in range(t // xt):
            if i + 1 < t // xt:
                x_copy(i + 1).start()
            x_copy(i).wait()
            b = i // seq_tiles
            h_ref[pl.ds(xt * i, xt), :] = (xbuf[i % 2] * (1.0 + ada_ref[b, 1:2, :]) + ada_ref[b, 0:1, :]).astype(BF16)
            if i >= 2:
                h_copy(i - 2).wait()
            h_copy(i).start()
        for i in range(max(t // xt - 2, 0), t // xt):
            h_copy(i).wait()

        arrive_for(0)
        fetch(0, 0)

        def step(s, carry):
            n_bf, n_f32 = carry
            buf = s % 2

            @pl.when(s + 1 < N_SLAB)
            def _():
                arrive_for(s + 1)
                fetch(s + 1, 1 - buf)

            wait_tile(buf)
            slab = tbl_ref[k, s]
            v = _nt(h_ref[...], wtile[buf]) + b_ref[slab]
            is_qkv = slab < N_QKV

            @pl.when(is_qkv)
            def _():
                put(obf, obf_sems, qkv_ref.at[jnp.minimum(slab, N_QKV - 1)], n_bf, v.astype(BF16))

            @pl.when(jnp.logical_not(is_qkv))
            def _():
                put(of32, of32_sems, rest_ref.at[jnp.maximum(slab - N_QKV, 0)], n_f32, v)

            return n_bf + is_qkv.astype(jnp.int32), n_f32 + 1 - is_qkv.astype(jnp.int32)

        n_bf, n_f32 = lax.fori_loop(0, N_SLAB, step, (jnp.int32(0), jnp.int32(0)))
        drain(obf, obf_sems, qkv_ref.at[0], n_bf)
        drain(of32, of32_sems, rest_ref.at[0], n_f32)

        for slots in ((1, 2, 3), (0, 4, 5, 6)):
            for a in range(1, 1 + n_o):
                for slot in slots:
                    arrive(a, slot)
        for cp in first:
            cp.wait_send()
        for j, chip in enumerate(chips):
            for ch in range(N_CHUNK):
                copy(0, 4 + j, (*chip, c), sibling, ch).wait_send()
            for a in range(1, 1 + n_o):
                copy(a, 4 + j, (*chip, c), sibling).wait_send()
        for cp in mine:
            cp.wait()

    out_shape = ((jax.ShapeDtypeStruct((N_DEV * srows, D), BF16), jax.ShapeDtypeStruct((N_QKV, t, SLAB), BF16),
                  jax.ShapeDtypeStruct((N_REST, t, SLAB), F32), jax.ShapeDtypeStruct((t, D), BF16))
                 + tuple(jax.ShapeDtypeStruct((N_DEV * o.shape[0], o.shape[1]), o.dtype) for o in others))
    n_all = 1 + n_o
    n_sems = 7 * (N_CHUNK + n_o)
    pair = pltpu.SemaphoreType.DMA((2,))
    grid_spec = pltpu.PrefetchScalarGridSpec(
        num_scalar_prefetch=1, grid=(1,),
        in_specs=[ANY, ANY, pl.BlockSpec(ada.shape, lambda i, tbl: (0, 0, 0)),
                  pl.BlockSpec((N_SLAB, 1, SLAB), lambda i, tbl: (0, 0, 0))] + [ANY] * n_o,
        out_specs=(ANY,) * (4 + n_o),
        scratch_shapes=[pltpu.VMEM((2, SLAB, D), BF16), pltpu.VMEM((2, t, SLAB), BF16), pltpu.VMEM((2, t, SLAB), F32),
                        pltpu.VMEM((t, D), BF16), pltpu.VMEM((2, xt, D), F32),
                        pltpu.SemaphoreType.DMA((n_sems,)), pltpu.SemaphoreType.DMA((n_sems,)),
                        pltpu.SemaphoreType.DMA((n_all,)), pair, pair, pair, pair, pair])
    res = _pcall(body, grid_spec=grid_spec, out_shape=out_shape, name="project_gather",
                 compiler_params=_params(48, ("arbitrary",)))(table, shard, x, ada, b_in3, *others)
    return res[0], res[1], res[2], res[3], list(res[4:])


def _bias_tables(g):
    window, dil = GROUPS[g]
    span = window // dil
    qi = jnp.arange(BLK)[:, None]
    kj = jnp.arange(2 * BLK)[None, :]
    delta = qi + BLK - kj
    valid = (delta >= 0) & (delta <= span)
    heads = jnp.arange(4, dtype=F32) + 4.0 * g
    slopes = 2.0 ** (-8.0 * (heads + 1.0) / 12.0)
    bias = -slopes[:, None, None] * (delta * dil).astype(F32)[None]
    return jnp.where(valid[None], bias, -1e30).reshape(4 * BLK, 2 * BLK)


def _head_masks(shape):
    lane = lax.broadcasted_iota(jnp.int32, shape, 1)
    return [(lane >= 64 * h) & (lane < 64 * (h + 1)) for h in range(4)]


def _stack_heads(v, masks):
    return jnp.concatenate([jnp.where(masks[h], v, jnp.zeros_like(v)) for h in range(4)], axis=0)


def _unstack_heads(v4, masks):
    out = jnp.where(masks[0], v4[0:BLK], 0.0)
    for h in range(1, 4):
        out = jnp.where(masks[h], v4[BLK * h:BLK * (h + 1)], out)
    return out


def _regroup(load_half, dst_ref, stage_ref, n, dil):
    for hlf in range(2):
        stage_ref[hlf] = load_half(hlf)

    def residue(r, carry):
        for hlf in range(2):
            dst_ref[pl.ds(pl.multiple_of(r * n, BLK), n), pl.ds(128 * hlf, 128)] = (
                stage_ref[hlf, pl.ds(r, n, stride=dil), :].astype(dst_ref.dtype))
        return carry

    lax.fori_loop(0, dil, residue, 0)


def _store_block(nat_ref, r, i, val, dil):
    for hlf in range(2):
        nat_ref[hlf, pl.ds(r + dil * BLK * i, BLK, stride=dil), :] = val[:, 128 * hlf:128 * (hlf + 1)]


def _for_blocks(block, dil, nblk):
    if dil == 1:
        for i in range(WAYS):
            block(0, i, i == 0)

        def step(k, carry):
            for j in range(WAYS):
                block(0, WAYS * k + j, False)
            return carry

        lax.fori_loop(1, nblk // WAYS, step, 0)
    else:
        def residues(k, carry):
            for j in range(WAYS):
                block(WAYS * k + j, 0, True)
            if nblk > 1:
                def loop(i, c):
                    for j in range(WAYS):
                        block(WAYS * k + j, i, False)
                    return c
                lax.fori_loop(1, nblk, loop, 0)
            return carry

        lax.fori_loop(0, dil // WAYS, residues, 0)


def _attn_forward(qkv, nbat):
    t = qkv.shape[1]
    seq = t // nbat
    n_grp = len(GROUPS)

    def body(qkv_ref, b0_ref, b1_ref, b2_ref, ol_ref, stage, qs_ref, ks_ref, vs_ref, *nat):
        masks = _head_masks((BLK, SLAB))
        bias_refs = (b0_ref, b1_ref, b2_ref)
        for g, (_, dil) in enumerate(GROUPS):
            n = seq // dil
            bias_ref, nat_o, nat_l = bias_refs[g], nat[2 * g], nat[2 * g + 1]
            if dil > 1:
                qd, kd, vd = qs_ref, ks_ref, vs_ref
                for which, dst in enumerate((qd, kd, vd)):
                    _regroup(lambda hlf, which=which, g=g: qkv_ref[3 * which + g, :, pl.ds(128 * hlf, 128)].astype(F32), dst, stage, n, dil)
            else:
                qd, kd, vd = qkv_ref.at[g], qkv_ref.at[3 + g], qkv_ref.at[6 + g]

            def block(r, i, first, n=n, dil=dil, qd=qd, kd=kd, vd=vd, bias_ref=bias_ref, nat_o=nat_o, nat_l=nat_l):
                base = r * n
                qs = pl.ds(pl.multiple_of(base + i * BLK, BLK), BLK)
                ks = pl.ds(pl.multiple_of(base, BLK), BLK) if first else pl.ds(pl.multiple_of(base + (i - 1) * BLK, BLK), 2 * BLK)
                q, kk, vv = qd[qs, :], kd[ks, :], vd[ks, :]
                bias = bias_ref[:, pl.ds(BLK, BLK)] if first else bias_ref[...]
                s = _nt(_stack_heads(q, masks), kk) * 0.125 + bias
                m = jnp.max(s, axis=1, keepdims=True)
                p = jnp.exp(s - m)
                den = jnp.sum(p, axis=1, keepdims=True)
                out = _unstack_heads(_nn((p * (1.0 / den)).astype(BF16), vv), masks)
                lse = _unstack_heads(jnp.broadcast_to(m + jnp.log(den), (4 * BLK, SLAB)), masks)
                _store_block(nat_o, r, i, out, dil)
                _store_block(nat_l, r, i, lse, dil)

            _for_blocks(block, dil, n // BLK)

        for hlf in range(2):
            l0, l1, l2 = nat[1][hlf], nat[3][hlf], nat[5][hlf]
            mx = jnp.maximum(jnp.maximum(l0, l1), l2)
            e0, e1, e2 = jnp.exp(l0 - mx), jnp.exp(l1 - mx), jnp.exp(l2 - mx)
            den = e0 + e1 + e2
            ol_ref[0, :, pl.ds(128 * hlf, 128)] = (e0 * nat[0][hlf] + e1 * nat[2][hlf] + e2 * nat[4][hlf]) * (1.0 / den)
            ol_ref[1, :, pl.ds(128 * hlf, 128)] = mx + jnp.log(den)

    halves = pltpu.VMEM((2, seq, 128), F32)
    bias_spec = pl.BlockSpec((4 * BLK, 2 * BLK), lambda b: (0, 0))
    return _pcall(
        body, grid=(nbat,), out_shape=jax.ShapeDtypeStruct((2, t, SLAB), F32),
        in_specs=[pl.BlockSpec((N_QKV, seq, SLAB), lambda b: (0, b, 0))] + [bias_spec] * n_grp,
        out_specs=pl.BlockSpec((2, seq, SLAB), lambda b: (0, b, 0)),
        scratch_shapes=[halves] + [pltpu.VMEM((seq, SLAB), BF16)] * 3 + [halves] * (2 * n_grp),
        name="attn_forward", compiler_params=_params(56, ("parallel",)))(qkv, *[_bias_tables(g) for g in range(n_grp)])


def _attn_backward(qkv, do_attn, ol_tot, dproj, g, nbat):
    t = qkv.shape[1]
    seq = t // nbat
    dil = GROUPS[g][1]
    n = seq // dil
    nblk = n // BLK
    qkv4 = qkv.reshape(3, 3, t, SLAB)
    dp4 = dproj.reshape(DP_SLABS // 3, 3, t, SLAB)

    def body(qkv_ref, do_ref, ol_ref, bias_ref, dp_in, dp_ref, gb_ref, dk_acc, dv_acc, *scratch):
        del dp_in
        masks = _head_masks((BLK, SLAB))

        @pl.when(pl.program_id(0) == 0)
        def _():
            gb_ref[...] = jnp.zeros_like(gb_ref)

        dk_acc[...] = jnp.zeros_like(dk_acc)
        dv_acc[...] = jnp.zeros_like(dv_acc)
        if dil > 1:
            stage, qd, kd, vd, dod, prodd, lsed, nat = scratch
            lanes = lambda hlf: pl.ds(128 * hlf, 128)
            for which, dst in enumerate((qd, kd, vd)):
                _regroup(lambda hlf, which=which: qkv_ref[which, 0, :, lanes(hlf)].astype(F32), dst, stage, n, dil)
            _regroup(lambda hlf: do_ref[:, lanes(hlf)].astype(F32), dod, stage, n, dil)
            _regroup(lambda hlf: do_ref[:, lanes(hlf)].astype(F32) * ol_ref[0, :, lanes(hlf)], prodd, stage, n, dil)
            _regroup(lambda hlf: ol_ref[1, :, lanes(hlf)], lsed, stage, n, dil)
        else:
            qd, kd, vd = qkv_ref.at[0, 0], qkv_ref.at[1, 0], qkv_ref.at[2, 0]

        def block(r, i, first):
            base = r * n
            qs = pl.ds(pl.multiple_of(base + i * BLK, BLK), BLK)
            ks = pl.ds(pl.multiple_of(base, BLK), BLK) if first else pl.ds(pl.multiple_of(base + (i - 1) * BLK, BLK), 2 * BLK)
            q, kk, vv = qd[qs, :], kd[ks, :], vd[ks, :]
            if dil > 1:
                do, prod, lse = dod[qs, :], prodd[qs, :], lsed[qs, :]
            else:
                do = do_ref[qs, :]
                prod = do.astype(F32) * ol_ref[0, qs, :]
                lse = ol_ref[1, qs, :]
            q4, do4 = _stack_heads(q, masks), _stack_heads(do, masks)
            bias = bias_ref[:, pl.ds(BLK, BLK)] if first else bias_ref[...]
            lse4 = jnp.concatenate([lse[:, 64 * h:64 * h + 1] for h in range(4)], axis=0)
            delta4 = jnp.concatenate([jnp.sum(jnp.where(masks[h], prod, 0.0), axis=1, keepdims=True) for h in range(4)], axis=0)
            p = jnp.exp(_nt(q4, kk) * 0.125 + bias - lse4)
            ds = (p * (_nt(do4, vv) - delta4)).astype(BF16)
            dv_acc[ks, :] += _tn(p.astype(BF16), do4)
            dk_acc[ks, :] += _tn(ds, q4) * 0.125
            dq = _unstack_heads(_nn(ds, kk), masks) * 0.125
            if dil > 1:
                _store_block(nat, r, i, dq, dil)
            else:
                dp_ref[0, 0, qs, :] = dq.astype(BF16)
            gb_ref[0] += _part8(dq)

        _for_blocks(block, dil, nblk)
        gb_ref[1] += _part8(dk_acc[...])
        gb_ref[2] += _part8(dv_acc[...])
        if dil > 1:
            def flush(which):
                for hlf in range(2):
                    dp_ref[which, 0, :, pl.ds(128 * hlf, 128)] = nat[hlf].astype(BF16)

            def to_token_order(acc_ref):
                def residue(r, carry):
                    for hlf in range(2):
                        nat[hlf, pl.ds(r, n, stride=dil), :] = acc_ref[pl.ds(pl.multiple_of(r * n, BLK), n), pl.ds(128 * hlf, 128)]
                    return carry
                lax.fori_loop(0, dil, residue, 0)

            flush(0)
            to_token_order(dk_acc)
            flush(1)
            to_token_order(dv_acc)
            flush(2)
        else:
            dp_ref[1, 0] = dk_acc[...].astype(BF16)
            dp_ref[2, 0] = dv_acc[...].astype(BF16)

    scratch = [pltpu.VMEM((seq, SLAB), F32)] * 2
    if dil > 1:
        scratch += ([pltpu.VMEM((2, seq, 128), F32)] + [pltpu.VMEM((seq, SLAB), BF16)] * 4 + [pltpu.VMEM((seq, SLAB), F32)] * 2
                    + [pltpu.VMEM((2, seq, 128), F32)])
    dp, gb = _pcall(
        body, grid=(nbat,),
        out_shape=(jax.ShapeDtypeStruct(dp4.shape, BF16), jax.ShapeDtypeStruct((3, 8, SLAB), F32)),
        in_specs=[pl.BlockSpec((3, 1, seq, SLAB), lambda b: (0, g, b, 0)),
                  pl.BlockSpec((seq, SLAB), lambda b: (b, 0)),
                  pl.BlockSpec((2, seq, SLAB), lambda b: (0, b, 0)),
                  pl.BlockSpec((4 * BLK, 2 * BLK), lambda b: (0, 0)), ANY],
        out_specs=(pl.BlockSpec((3, 1, seq, SLAB), lambda b: (DP_SLABS // 9 - 1, g, b, 0)),
                   pl.BlockSpec((3, 8, SLAB), lambda b: (0, 0, 0))),
        scratch_shapes=scratch, input_output_aliases={4: 0}, name=f"attn_backward_{g}",
        compiler_params=_params(48, ("arbitrary",)))(qkv4, do_attn, ol_tot, _bias_tables(g), dp4)
    return dp.reshape(DP_SLABS, t, SLAB), gb


def _mid(rest, ol_tot, x, tgt, ada, cw, b_out, ln_g, ln_b, w_pa_t, w_pb, w_out, tm=256):
    t = x.shape[0]
    nbat = ada.shape[0]
    nt = t // tm
    tps = nt // nbat

    def body(rest_ref, halo_ref, ol_ref, x_ref, t_ref, ada_ref, cw_ref, bout_ref, lng_ref, lnb_ref,
             wpat_ref, wpb_ref, wout_ref,
             dp_ref, gx0_ref, doa_ref, mg_ref, dof_ref, bbs_ref, dyc_ref, a_ref, dya_ref,
             gbr_ref, sv_ref, dgate_ref, carry_ref, keep_ref):
        i = pl.program_id(0)
        ti = nt - 1 - i
        pos = ti % tps

        @pl.when(i == 0)
        def _():
            gbr_ref[...] = jnp.zeros_like(gbr_ref)
            sv_ref[...] = jnp.zeros_like(sv_ref)

        @pl.when(pos == tps - 1)
        def _():
            dgate_ref[...] = jnp.zeros_like(dgate_ref)
            carry_ref[...] = jnp.zeros_like(carry_ref)

        row = lax.broadcasted_iota(jnp.int32, (tm, SLAB), 0)
        halo_on = (pos > 0).astype(F32)

        def cols(s):
            return pl.ds(SLAB * s, SLAB)

        o_attn = ol_ref[0]
        z_a = rest_ref[R_ZA]
        sg_za = _sigmoid(z_a)
        a_ref[...] = (o_attn * z_a * sg_za).astype(BF16)
        y_attn = _nt(a_ref[...], wpat_ref[...])

        for s in range(4):
            u = rest_ref[R_GC + s] * rest_ref[R_UX + s]
            hu = halo_ref[R_GC + s] * halo_ref[R_UX + s] * halo_on
            u1 = jnp.where(row == 0, hu[7:8], pltpu.roll(u, 1, 0))
            u2 = jnp.where(row == 0, hu[6:7], jnp.where(row == 1, hu[7:8], pltpu.roll(u, 2, 0)))
            conv = cw_ref[0:1, cols(s)] * u2 + cw_ref[1:2, cols(s)] * u1 + cw_ref[2:3, cols(s)] * u
            zc = rest_ref[R_ZC + s]
            sg = _sigmoid(zc)
            keep_ref[2, :, cols(s)], keep_ref[3, :, cols(s)], keep_ref[4, :, cols(s)], keep_ref[5, :, cols(s)] = u1, u2, conv, sg
            bbs_ref[:, cols(s)] = (rest_ref[R_GB + s] * conv * (zc * sg)).astype(BF16)
        y_conv = _nn(bbs_ref[...], wpb_ref[...])

        for s in range(4):
            s_a, s_b = _sigmoid(rest_ref[R_GA + s]), _sigmoid(rest_ref[R_GBM + s])
            keep_ref[0, :, cols(s)], keep_ref[1, :, cols(s)] = s_a, s_b
            mg_ref[:, cols(s)] = (s_a * y_attn[:, SLAB * s:SLAB * (s + 1)] + s_b * y_conv[:, SLAB * s:SLAB * (s + 1)]).astype(BF16)
        o = _nn(mg_ref[...], wout_ref[...]) + bout_ref[...]
        gate = ada_ref[0, 2:3, :]
        r = ALPHA * x_ref[...] + gate * o
        mu = jnp.mean(r, axis=1, keepdims=True)
        rc = r - mu
        rstd = lax.rsqrt(jnp.mean(rc * rc, axis=1, keepdims=True) + LN_EPS)
        xhat = rc * rstd
        err = xhat * lng_ref[...] + lnb_ref[...] - t_ref[...]
        sv_ref[6] += _part8(err * err)
        dy = err * (1.0 / D)
        sv_ref[0] += _part8(dy * xhat)
        sv_ref[1] += _part8(dy)
        dxh = dy * lng_ref[...]
        dr = rstd * (dxh - jnp.mean(dxh, axis=1, keepdims=True) - xhat * jnp.mean(dxh * xhat, axis=1, keepdims=True))
        gx0_ref[...] = ALPHA * dr
        dgate_ref[0] += _part8(dr * o)
        do_ = dr * gate
        sv_ref[2] += _part8(do_)
        dof_ref[...] = do_.astype(BF16)
        dmerged = _nt(dof_ref[...], wout_ref[...])
        for s in range(4):
            s_a, s_b = keep_ref[0, :, cols(s)], keep_ref[1, :, cols(s)]
            dm = dmerged[:, SLAB * s:SLAB * (s + 1)]
            ya, yc = y_attn[:, SLAB * s:SLAB * (s + 1)], y_conv[:, SLAB * s:SLAB * (s + 1)]
            dya_ref[:, cols(s)] = (dm * s_a).astype(BF16)
            dyc_ref[:, cols(s)] = (dm * s_b).astype(BF16)
            dga = dm * ya * s_a * (1.0 - s_a)
            dgb = dm * yc * s_b * (1.0 - s_b)
            dp_ref[R_GA + s] = dga.astype(BF16)
            dp_ref[R_GBM + s] = dgb.astype(BF16)
            gbr_ref[R_GA + s] += _part8(dga)
            gbr_ref[R_GBM + s] += _part8(dgb)

        da = _nn(dya_ref[...], wpat_ref[...])
        doa_ref[...] = (da * z_a * sg_za).astype(BF16)
        dza = da * o_attn * (sg_za * (1.0 + z_a * (1.0 - sg_za)))
        dp_ref[R_ZA] = dza.astype(BF16)
        gbr_ref[R_ZA] += _part8(dza)

        dbb = _nt(dyc_ref[...], wpb_ref[...])
        for s in range(4):
            ux, gc, zc = rest_ref[R_UX + s], rest_ref[R_GC + s], rest_ref[R_ZC + s]
            u = gc * ux
            u1, u2, conv, sg = keep_ref[2, :, cols(s)], keep_ref[3, :, cols(s)], keep_ref[4, :, cols(s)], keep_ref[5, :, cols(s)]
            gb = rest_ref[R_GB + s]
            d_b = dbb[:, SLAB * s:SLAB * (s + 1)]
            szc = zc * sg
            dgb_ = d_b * conv * szc
            dconv = d_b * gb * szc
            dzc = d_b * gb * conv * (sg * (1.0 + zc * (1.0 - sg)))
            sv_ref[3, :, cols(s)] += _part8(dconv * u2)
            sv_ref[4, :, cols(s)] += _part8(dconv * u1)
            sv_ref[5, :, cols(s)] += _part8(dconv * u)
            nxt = carry_ref[:, cols(s)]
            d1 = jnp.where(row == tm - 1, nxt[0:1], pltpu.roll(dconv, tm - 1, 0))
            d2 = jnp.where(row == tm - 1, nxt[1:2], jnp.where(row == tm - 2, nxt[0:1], pltpu.roll(dconv, tm - 2, 0)))
            carry_ref[:, cols(s)] = dconv[0:8]
            du = cw_ref[2:3, cols(s)] * dconv + cw_ref[1:2, cols(s)] * d1 + cw_ref[0:1, cols(s)] * d2
            dgc, dux = du * ux, du * gc
            for slab, val in ((R_GB + s, dgb_), (R_ZC + s, dzc), (R_GC + s, dgc), (R_UX + s, dux)):
                dp_ref[slab] = val.astype(BF16)
                gbr_ref[slab] += _part8(val)

    def tile(i):
        return nt - 1 - i

    row_blk = lambda i: (tile(i), 0)
    slab_blk = lambda i: (0, tile(i), 0)
    const2 = lambda i: (0, 0)
    const3 = lambda i: (0, 0, 0)
    in_specs = [
        pl.BlockSpec((N_REST, tm, SLAB), slab_blk),
        pl.BlockSpec((N_REST, 8, SLAB), lambda i: (0, jnp.maximum(tile(i) * (tm // 8) - 1, 0), 0)),
        pl.BlockSpec((1, tm, SLAB), slab_blk),
        pl.BlockSpec((tm, D), row_blk), pl.BlockSpec((tm, D), row_blk),
        pl.BlockSpec((1, 3, D), lambda i: (tile(i) // tps, 0, 0)),
        pl.BlockSpec((3, D), const2), pl.BlockSpec((1, D), const2), pl.BlockSpec((1, D), const2), pl.BlockSpec((1, D), const2),
        pl.BlockSpec((D, SLAB), const2), pl.BlockSpec((D, D), const2), pl.BlockSpec((D, D), const2)]
    bf_rows = lambda: jax.ShapeDtypeStruct((t, D), BF16)
    out_shape = (
        jax.ShapeDtypeStruct((DP_SLABS, t, SLAB), BF16), jax.ShapeDtypeStruct((t, D), F32),
        jax.ShapeDtypeStruct((t, SLAB), BF16),
        bf_rows(), bf_rows(), bf_rows(), bf_rows(), jax.ShapeDtypeStruct((t, SLAB), BF16), bf_rows(),
        jax.ShapeDtypeStruct((N_REST, 8, SLAB), F32), jax.ShapeDtypeStruct((7, 8, D), F32),
        jax.ShapeDtypeStruct((nbat, 8, D), F32))
    out_specs = (
        pl.BlockSpec((N_REST, tm, SLAB), slab_blk), pl.BlockSpec((tm, D), row_blk),
        pl.BlockSpec((tm, SLAB), row_blk),
        pl.BlockSpec((tm, D), row_blk), pl.BlockSpec((tm, D), row_blk), pl.BlockSpec((tm, D), row_blk),
        pl.BlockSpec((tm, D), row_blk), pl.BlockSpec((tm, SLAB), row_blk), pl.BlockSpec((tm, D), row_blk),
        pl.BlockSpec((N_REST, 8, SLAB), const3), pl.BlockSpec((7, 8, D), const3),
        pl.BlockSpec((1, 8, D), lambda i: (tile(i) // tps, 0, 0)))
    return _pcall(body, grid=(nt,), out_shape=out_shape, in_specs=in_specs, out_specs=out_specs,
                  scratch_shapes=[pltpu.VMEM((8, D), F32), pltpu.VMEM((6, tm, D), F32)], name="mid",
                  compiler_params=_params(56, ("arbitrary",)))(
        rest, rest, ol_tot, x, tgt, ada, cw, b_out, ln_g, ln_b, w_pa_t, w_pb, w_out)


def _tn_matmul(lhs, rhs, lhs_spec, n_steps, out_rows, out_index, name, after):
    t, n = rhs.shape

    def body(l_ref, r_ref, after_ref, o_ref):
        del after_ref
        o_ref[...] = _tn(l_ref[0] if len(l_ref.shape) == 3 else l_ref[...], r_ref[...])

    return _pcall(body, grid=(n_steps,), out_shape=jax.ShapeDtypeStruct((out_rows, n), F32),
                  in_specs=[lhs_spec, pl.BlockSpec((t, n), lambda j: (0, 0)), ANY],
                  out_specs=pl.BlockSpec((SLAB, n), out_index), name=name,
                  compiler_params=_params(48, ("parallel",)))(lhs, rhs, after)


def _grad_rows_2d(lhs, rhs, name, after, tc=1024):
    t, k = lhs.shape
    n = rhs.shape[1]

    def body(l_ref, r_ref, after_ref, o_ref):
        del after_ref
        part = _tn(l_ref[...], r_ref[...])

        @pl.when(pl.program_id(0) == 0)
        def _():
            o_ref[...] = part

        @pl.when(pl.program_id(0) > 0)
        def _():
            o_ref[...] += part

    return _pcall(body, grid=(t // tc,), out_shape=jax.ShapeDtypeStruct((k, n), F32),
                  in_specs=[pl.BlockSpec((tc, k), lambda i: (i, 0)), pl.BlockSpec((tc, n), lambda i: (i, 0)), ANY],
                  out_specs=pl.BlockSpec((k, n), lambda i: (0, 0)), name=name,
                  compiler_params=_params(32, ("arbitrary",)))(lhs, rhs, after)


def _w_row_block(j):
    return (j + N_QKV) % N_SLAB


def _dp_slab(j):
    return jnp.where(j < N_REST, j, j + 2)


def _grad_w_in_t(dproj, h):
    t = h.shape[0]
    return _tn_matmul(dproj, h, pl.BlockSpec((1, t, SLAB), lambda j: (_dp_slab(j), 0, 0)), N_SLAB, D_IN,
                      lambda j: (_w_row_block(j), 0), "grad_w_in", h)


def _grad_h(dproj, w_in_t, gx0, x, ada, after, tm=512):
    t = x.shape[0]
    nbat = ada.shape[0]
    tps = (t // nbat) // tm

    def body(dp_ref, w_ref, gx0_ref, x_ref, ada_ref, after_ref, gx_ref, dss_ref):
        del after_ref
        i = pl.program_id(0)
        dh = None
        for j in range(N_SLAB):
            slab = j if j < N_REST else j + 2
            part = _nn(dp_ref[slab], w_ref[pl.ds(SLAB * ((j + N_QKV) % N_SLAB), SLAB), :])
            dh = part if dh is None else dh + part
        gx_ref[...] = gx0_ref[...] + dh * (1.0 + ada_ref[0, 1:2, :])

        @pl.when((i % tps) == 0)
        def _():
            dss_ref[...] = jnp.zeros_like(dss_ref)

        dss_ref[0, 0] += _part8(dh)
        dss_ref[0, 1] += _part8(dh * x_ref[...])

    return _pcall(
        body, grid=(t // tm,),
        out_shape=(jax.ShapeDtypeStruct((t, D), F32), jax.ShapeDtypeStruct((nbat, 2, 8, D), F32)),
        in_specs=[pl.BlockSpec((DP_SLABS, tm, SLAB), lambda i: (0, i, 0)),
                  pl.BlockSpec((D_IN, D), lambda i: (0, 0), pipeline_mode=pl.Buffered(1)),
                  pl.BlockSpec((tm, D), lambda i: (i, 0)), pl.BlockSpec((tm, D), lambda i: (i, 0)),
                  pl.BlockSpec((1, 3, D), lambda i: (i // tps, 0, 0)), ANY],
        out_specs=(pl.BlockSpec((tm, D), lambda i: (i, 0)),
                   pl.BlockSpec((1, 2, 8, D), lambda i: (i // tps, 0, 0, 0))),
        name="grad_h", compiler_params=_params(60, ("arbitrary",)))(dproj, w_in_t, gx0, x, ada, after)


def _chip(m):
    x, y, _ = _my_position()
    return (x ^ ((m >> 1) & 1), y ^ (m & 1))


def _exchange_siblings(grads, after, name):
    n = len(grads)

    def body(*refs):
        copies = _sibling_copies(refs[:n], refs[n + 1:2 * n + 1], refs[2 * n + 1], refs[2 * n + 2])
        for cp in copies:
            cp.start()
        for cp in copies:
            cp.wait()

    return _pcall(body, out_shape=tuple(_sibling_zones(grads)), in_specs=[ANY] * (n + 1), out_specs=(ANY,) * n,
                  name=name, scratch_shapes=[pltpu.SemaphoreType.DMA((4 * n,))] * 2)(*grads, after)


def _sibling_zones(grads):
    return [jax.ShapeDtypeStruct((4, g.shape[0] // N_DEV, g.shape[1]), g.dtype) for g in grads]


def _sibling_copies(srcs, lands, send_sems, recv_sems):
    x, y, c = _my_position()
    copies = []
    for a, (src, land) in enumerate(zip(srcs, lands)):
        rows = land.shape[1]
        for m in range(4):
            dev = _flat(*_chip(m), 1 - c)
            copies.append(pltpu.make_async_remote_copy(
                src_ref=src.at[pl.ds(pl.multiple_of(dev * rows, 8), rows), :], dst_ref=land.at[m],
                send_sem=send_sems.at[4 * a + m], recv_sem=recv_sems.at[4 * a + m], device_id=(x, y, 1 - c),
                device_id_type=MESH))
    return copies


def _chip_copies(srcs, lands, send_sems, recv_sems):
    _, _, c = _my_position()
    return [pltpu.make_async_remote_copy(
        src_ref=srcs[a].at[m - 1], dst_ref=lands[a].at[m - 1], send_sem=send_sems.at[3 * a + m - 1],
        recv_sem=recv_sems.at[3 * a + m - 1], device_id=(*_chip(m), c), device_id_type=MESH)
        for a in range(len(srcs)) for m in range(1, 4)]


HBM = pl.BlockSpec(memory_space=pltpu.HBM)
SEM = pl.BlockSpec(memory_space=pltpu.SEMAPHORE)
SPLIT_COPY = pltpu.CompilerParams(has_side_effects=pltpu.SideEffectType.DATAFLOW_SIDE_EFFECTING)


def _start_copies(make_copies, n_sems, srcs, zones, name):
    n = len(srcs)

    def body(*refs):
        for cp in make_copies(refs[:n], refs[n:2 * n], refs[2 * n], refs[2 * n + 1]):
            cp.start()
        refs[-1][...] = jnp.zeros_like(refs[-1])

    hbm = tuple(pltpu.HBM(b.shape, b.dtype) for b in list(srcs) + list(zones))
    out_shape = (pltpu.SemaphoreType.DMA((n_sems,)), pltpu.SemaphoreType.DMA((n_sems,))) + hbm + (jax.ShapeDtypeStruct((8, 128), F32),)
    operands = [pltpu.with_memory_space_constraint(b, pltpu.HBM) for b in srcs]
    operands += [pltpu.with_memory_space_constraint(lax.empty(z.shape, z.dtype), pltpu.HBM) for z in zones]
    res = _pcall(body, out_shape=out_shape, in_specs=[HBM] * (2 * n), out_specs=(SEM, SEM) + (HBM,) * (2 * n) + (VMEM,),
                 input_output_aliases={i: 2 + i for i in range(2 * n)}, name=name, compiler_params=SPLIT_COPY)(*operands)
    return (res[0], res[1], res[2:2 + n], res[2 + n:2 + 2 * n]), res[-1]


def _wait_copies(make_copies, flight, after, name):
    send_sems, recv_sems, srcs, zones = flight
    n = len(srcs)

    def body(*refs):
        for cp in make_copies(refs[:n], refs[n:2 * n], refs[2 * n], refs[2 * n + 1]):
            cp.wait_send()
            cp.wait_recv()

    hbm = tuple(pltpu.HBM(b.shape, b.dtype) for b in list(srcs) + list(zones))
    res = _pcall(body, out_shape=hbm, in_specs=[HBM] * (2 * n) + [SEM, SEM, ANY], out_specs=(HBM,) * (2 * n),
                 input_output_aliases={i: i for i in range(2 * n)}, name=name, compiler_params=SPLIT_COPY)(
        *srcs, *zones, send_sems, recv_sems, after)
    return res[:n], res[n:]


def _pair_sums(devs, grads, lands, n_steps, name):
    n = len(grads)
    rows = [l.shape[1] for l in lands]
    rbs = [r // n_steps for r in rows]

    def body(devs_ref, *refs):
        del devs_ref
        g_refs, land_refs, outs = refs[:4 * n], refs[4 * n:5 * n], refs[5 * n:]
        for a in range(n):
            outs[2 * a][...] = g_refs[4 * a][...] + land_refs[a][0]
            for m in range(1, 4):
                outs[2 * a + 1][m - 1] = (g_refs[4 * a + m][...] + land_refs[a][m]).astype(BF16)

    def block_of(m, per_dev):
        return lambda i, devs_ref: (devs_ref[m] * per_dev + i, 0)

    in_specs = [pl.BlockSpec((rb, l.shape[2]), block_of(m, n_steps)) for rb, l in zip(rbs, lands) for m in range(4)]
    in_specs += [pl.BlockSpec((4, rb, l.shape[2]), lambda i, devs_ref: (0, i, 0)) for rb, l in zip(rbs, lands)]
    out_shape, out_specs = [], []
    for rb, l in zip(rbs, lands):
        out_shape += [jax.ShapeDtypeStruct(l.shape[1:], F32), jax.ShapeDtypeStruct((3,) + l.shape[1:], BF16)]
        out_specs += [pl.BlockSpec((rb, l.shape[2]), lambda i, devs_ref: (i, 0)),
                      pl.BlockSpec((3, rb, l.shape[2]), lambda i, devs_ref: (0, i, 0))]
    grid_spec = pltpu.PrefetchScalarGridSpec(num_scalar_prefetch=1, grid=(n_steps,), in_specs=in_specs, out_specs=tuple(out_specs))
    res = _pcall(body, grid_spec=grid_spec, out_shape=tuple(out_shape), name=name,
                 compiler_params=_params(48, ("parallel",)))(devs, *[g for g in grads for _ in range(4)], *lands)
    return res[0::2], res[1::2]


def _final_sums(mine, lands, n_steps, name):
    n = len(mine)
    rbs = [o.shape[0] // n_steps for o in mine]

    def body(*refs):
        mine_refs, land_refs, outs = refs[:n], refs[n:2 * n], refs[2 * n:]
        for a in range(n):
            tot = mine_refs[a][...]
            for m in range(3):
                tot = tot + land_refs[a][m].astype(F32)
            outs[a][...] = tot

    in_specs = ([pl.BlockSpec((rb, o.shape[1]), lambda i: (i, 0)) for rb, o in zip(rbs, mine)]
                + [pl.BlockSpec((3, rb, o.shape[1]), lambda i: (0, i, 0)) for rb, o in zip(rbs, mine)])
    out_specs = tuple(pl.BlockSpec((rb, o.shape[1]), lambda i: (i, 0)) for rb, o in zip(rbs, mine))
    out_shape = tuple(jax.ShapeDtypeStruct(o.shape, F32) for o in mine)
    return _pcall(body, grid=(n_steps,), out_shape=out_shape, in_specs=in_specs, out_specs=out_specs, name=name,
                  compiler_params=_params(32, ("parallel",)))(*mine, *lands)


def _reduce_scatter_begin(big, small_after_start):
    c = lax.axis_index("c")
    devs = jnp.stack([_flat(*_chip(m), c) for m in range(4)]).astype(jnp.int32)
    flight, token = _start_copies(_sibling_copies, 4, [big], _sibling_zones([big]), "siblings_start")
    small = small_after_start(token)
    (big,), big_lands = _wait_copies(_sibling_copies, flight, small[-1], "siblings_wait")
    big_mine, big_send = _pair_sums(devs, [big], big_lands, 4, "pair_sums_w_in")
    big_flight, token = _start_copies(_chip_copies, 3, list(big_send), list(big_send), "chips_start_w_in")
    small_lands = _exchange_siblings(small, token, "exchange_siblings_rest")
    small_mine, small_send = _pair_sums(devs, small, small_lands, 1, "pair_sums_rest")
    small_flight, token = _start_copies(_chip_copies, 3 * len(small), list(small_send), list(small_send), "chips_start_rest")
    return (big_flight, small_flight, list(big_mine) + list(small_mine)), token


def _reduce_scatter_end(state, after):
    big_flight, small_flight, mine = state
    _, big_got = _wait_copies(_chip_copies, big_flight, after, "chips_wait_w_in")
    _, small_got = _wait_copies(_chip_copies, small_flight, after, "chips_wait_rest")
    small = _final_sums(mine[1:], small_got, 1, "final_sums_rest")
    return (mine[0], big_got[0]), list(small)


def _adamw(w, g, m, v):
    m_new = B1 * m + (1.0 - B1) * g
    v_new = B2 * v + (1.0 - B2) * (g * g)
    m_hat = m_new / (1.0 - B1 ** STEP)
    v_hat = v_new / (1.0 - B2 ** STEP)
    delta = -LR * (m_hat / (jnp.sqrt(v_hat) + EPS) + WD * w)
    return delta, m_new, v_new


def _final_sum_adam_rows(mine, land, w, m, v, n_steps, name):
    rows, ncol = w.shape
    blk = pl.BlockSpec((rows // n_steps, ncol), lambda i: (i, 0))

    def body(mine_ref, land_ref, w_ref, m_ref, v_ref, g_ref, d_ref, mo_ref, vo_ref):
        g = mine_ref[...]
        for k in range(3):
            g = g + land_ref[k].astype(F32)
        g_ref[...] = g
        d_ref[...], mo_ref[...], vo_ref[...] = _adamw(w_ref[...], g, m_ref[...], v_ref[...])

    shape = jax.ShapeDtypeStruct(w.shape, F32)
    return _pcall(body, grid=(n_steps,), out_shape=(shape,) * 4,
                  in_specs=[blk, pl.BlockSpec((3, rows // n_steps, ncol), lambda i: (0, i, 0)), blk, blk, blk],
                  out_specs=(blk,) * 4, name=name, compiler_params=_params(32, ("parallel",)))(mine, land, w, m, v)


def _adam_transposed(g_t, w, m, v, name):
    n, k = g_t.shape
    rb = min(k, 128)

    def body(gt_ref, w_ref, m_ref, v_ref, g_ref, d_ref, mo_ref, vo_ref):
        for src, skip, dst, size in _column_chunks(n):
            sl = pl.ds(dst, size)
            g = gt_ref[pl.ds(src, 128), :].T[:, skip:]
            delta, m_new, v_new = _adamw(w_ref[:, sl], g, m_ref[:, sl], v_ref[:, sl])
            g_ref[:, sl], d_ref[:, sl], mo_ref[:, sl], vo_ref[:, sl] = g, delta, m_new, v_new

    shape = jax.ShapeDtypeStruct(w.shape, F32)
    rows = pl.BlockSpec((rb, n), lambda i: (i, 0))
    return _pcall(body, grid=(k // rb,), out_shape=(shape,) * 4,
                  in_specs=[pl.BlockSpec((n, rb), lambda i: (0, i)), rows, rows, rows], out_specs=(rows,) * 4, name=name,
                  compiler_params=_params(32, ("parallel",)))(g_t, w, m, v)


def _adam_many(items, name):
    n = len(items)

    def body(*refs):
        ins, outs = refs[:4 * n], refs[4 * n:]
        for a in range(n):
            w_ref, g_ref, m_ref, v_ref = ins[4 * a:4 * a + 4]
            delta, m_new, v_new = _adamw(w_ref[...], g_ref[...], m_ref[...], v_ref[...])
            outs[3 * a][...], outs[3 * a + 1][...], outs[3 * a + 2][...] = delta, m_new, v_new

    out_shape = tuple(jax.ShapeDtypeStruct(it[0].shape, F32) for it in items for _ in range(3))
    flat = [arr for it in items for arr in it]
    res = _pcall(body, grid=(1,), out_shape=out_shape, in_specs=[_whole(a) for a in flat],
                 out_specs=tuple(_whole(o) for o in out_shape), name=name, compiler_params=_params(32))(*flat)
    return [tuple(res[3 * a:3 * a + 3]) for a in range(n)]


def _adam_w_ada(cact_all, dada_mine, w, m, v):
    def body(c_ref, d_ref, w_ref, m_ref, v_ref, g_ref, dl_ref, mo_ref, vo_ref):
        g = _tn(c_ref[...].astype(BF16), d_ref[...].astype(BF16))
        delta, m_new, v_new = _adamw(w_ref[...], g, m_ref[...], v_ref[...])
        g_ref[...], dl_ref[...], mo_ref[...], vo_ref[...] = g, delta, m_new, v_new

    shape = jax.ShapeDtypeStruct(w.shape, F32)
    operands = (cact_all, dada_mine, w, m, v)
    return _pcall(body, grid=(1,), out_shape=(shape,) * 4, in_specs=[_whole(a) for a in operands],
                  out_specs=(_whole(w),) * 4, name="adam_w_ada", compiler_params=_params(32))(*operands)


def kernel(x, c, w_ada, b_ada, w_in, b_in, conv_w, w_proj_attn, w_proj_conv, w_out, b_out, ln_g, ln_b, loss_target, m_w_ada, m_b_ada, m_w_in, m_b_in, m_conv_w, m_w_proj_attn, m_w_proj_conv, m_w_out, m_b_out, m_ln_g, m_ln_b, v_w_ada, v_b_ada, v_w_in, v_b_in, v_conv_w, v_w_proj_attn, v_w_proj_conv, v_w_out, v_b_out, v_ln_g, v_ln_b):
    nbat, seq, _ = x.shape
    t = nbat * seq
    me = _flat(*_my_position())
    x2, tgt2 = x.reshape(t, D), loss_target.reshape(t, D)
    sq = lambda a: a.reshape(a.shape[1:])

    tr = lambda a: a[0].T
    w_in_rows = tr(w_in)
    w_in_t_s = _cast_rows(w_in_rows, 4, "cast_w_in")
    w_pa_t_s, w_pb_s, w_out_s, cact_s, cw_s = _prep(sq(w_proj_attn), sq(w_proj_conv), sq(w_out), c, sq(conv_w))

    ncol = w_ada.shape[2]
    b_ada_mine = lax.dynamic_slice(b_ada, (0, me * ncol), (1, ncol))
    ada_slots, cact_slots, cw_slots = _ada_forward(cact_s, cw_s, sq(w_ada), b_ada_mine)
    cact_all = cact_slots[:, :nbat].reshape(N_DEV * nbat, D)
    cw = cw_slots[:, :3].transpose(1, 0, 2).reshape(3, D)
    ada_all = ada_slots[:, :, :nbat].transpose(1, 2, 0, 3).reshape(N_DEV * nbat, 3, D)
    ada = lax.dynamic_slice(ada_all, (me * nbat, 0, 0), (nbat, 3, D))

    w_in_t, qkv, rest, h, (w_pa_t, w_pb, w_o) = _project_gather(
        w_in_t_s, x2, ada, b_in.reshape(N_SLAB, 1, SLAB), [w_pa_t_s, w_pb_s, w_out_s])
    ol_tot = _attn_forward(qkv, nbat)
    (dproj, gx0, do_attn, merged, do_f, bbs, dyc, a_bf, dya, gb_rest, svec, dgate) = _mid(
        rest, ol_tot, x2, tgt2, ada, cw, b_out, ln_g, ln_b, w_pa_t, w_pb, w_o)

    gb_qkv = []
    for g in range(3):
        dproj, gb = _attn_backward(qkv, do_attn, ol_tot, dproj, g, nbat)
        gb_qkv.append(gb)
    g_w_in_t = _grad_w_in_t(dproj, h)

    def small_grads(token):
        g_w_out = _grad_rows_2d(merged, do_f, "grad_w_out", token)
        g_w_pb = _grad_rows_2d(bbs, dyc, "grad_w_proj_conv", g_w_out)
        g_w_pa_t = _grad_rows_2d(dya, a_bf, "grad_w_proj_attn", g_w_pb)
        return [g_w_out, g_w_pb, g_w_pa_t]

    rs_state, token = _reduce_scatter_begin(g_w_in_t, small_grads)
    grad_x, dss = _grad_h(dproj, w_in_t, gx0, x2, ada, token)

    rows8, tot, g_bada = _small_reduce(gb_rest, gb_qkv, svec, dgate, dss)
    (g_in_mine, g_in_got), (g_out, g_pb, g_pa_t) = _reduce_scatter_end(rs_state, tot)
    loss = tot[0, P_LOSS]
    dada_all = rows8[:, 0, P_DADA:].reshape(N_DEV * nbat, 3 * D)
    dada_mine = lax.dynamic_slice(dada_all, (0, me * ncol), (N_DEV * nbat, ncol))

    g_in_t, d_win_t, nm_win_t, nv_win_t = _final_sum_adam_rows(g_in_mine, g_in_got, w_in_rows, tr(m_w_in), tr(v_w_in), 4, "adam_w_in")
    g_win, d_win, nm_win, nv_win = g_in_t.T, d_win_t.T, nm_win_t.T, nv_win_t.T
    g_wpa, d_wpa, nm_wpa, nv_wpa = _adam_transposed(g_pa_t, sq(w_proj_attn), sq(m_w_proj_attn), sq(v_w_proj_attn), "adam_w_proj_attn")
    g_wada, d_wada, nm_wada, nv_wada = _adam_w_ada(cact_all, dada_mine, sq(w_ada), sq(m_w_ada), sq(v_w_ada))
    g_bin = tot[:, P_BIN:P_BIN + D_IN]
    g_bout = tot[:, P_BOUT:P_BOUT + D]
    g_lng = tot[:, P_LNG:P_LNG + D]
    g_lnb = tot[:, P_LNB:P_LNB + D]
    g_conv = lax.dynamic_slice(tot[:, P_CONV:P_CONV + 3 * D].reshape(3, D), (0, me * cw_s.shape[1]), (3, cw_s.shape[1]))
    upd = _adam_many([
        (sq(w_proj_conv), g_pb, sq(m_w_proj_conv), sq(v_w_proj_conv)),
        (sq(w_out), g_out, sq(m_w_out), sq(v_w_out)),
        (b_ada, g_bada, m_b_ada, v_b_ada), (b_in, g_bin, m_b_in, v_b_in), (sq(conv_w), g_conv, sq(m_conv_w), sq(v_conv_w)),
        (b_out, g_bout, m_b_out, v_b_out), (ln_g, g_lng, m_ln_g, v_ln_g), (ln_b, g_lnb, m_ln_b, v_ln_b)], "adam_rest")
    (d_wpb, nm_wpb, nv_wpb), (d_wout, nm_wout, nv_wout), (d_bada, nm_bada, nv_bada), (d_bin, nm_bin, nv_bin), \
        (d_conv, nm_conv, nv_conv), (d_bout, nm_bout, nv_bout), (d_lng, nm_lng, nv_lng), (d_lnb, nm_lnb, nv_lnb) = upd

    ex = lambda a: a.reshape((1,) + a.shape)
    grads = [ex(g_wada), g_bada, ex(g_win), g_bin, ex(g_conv), ex(g_wpa), ex(g_pb), ex(g_out), g_bout, g_lng, g_lnb]
    deltas = [ex(d_wada), d_bada, ex(d_win), d_bin, ex(d_conv), ex(d_wpa), ex(d_wpb), ex(d_wout), d_bout, d_lng, d_lnb]
    new_m = [ex(nm_wada), nm_bada, ex(nm_win), nm_bin, ex(nm_conv), ex(nm_wpa), ex(nm_wpb), ex(nm_wout), nm_bout, nm_lng, nm_lnb]
    new_v = [ex(nv_wada), nv_bada, ex(nv_win), nv_bin, ex(nv_conv), ex(nv_wpa), ex(nv_wpb), ex(nv_wout), nv_bout, nv_lng, nv_lnb]
    return (loss, grad_x.reshape(x.shape), *grads, *deltas, *new_m, *new_v)
```

```python
import jax
import jax.numpy as jnp
from jax import lax
from jax.experimental import pallas as pl
from jax.experimental.pallas import tpu as pltpu

F32, BF16 = jnp.float32, jnp.bfloat16
MESH = pl.DeviceIdType.MESH
N_DEV = 8
D = 1024
SLAB = 256
N_QKV, N_REST = 9, 25
N_SLAB = N_QKV + N_REST
D_IN = N_SLAB * SLAB
DP_SLABS = 36
BLK = 128
WAYS = 4
GROUPS = ((128, 1), (512, 4), (2048, 16))
ALPHA = 2.0 ** 0.25
LN_EPS = 1e-5
LR, B1, B2, EPS, WD, STEP = 0.001, 0.9, 0.999, 1e-08, 0.01, 10
R_ZA, R_UX, R_GB, R_GC, R_ZC, R_GA, R_GBM = 0, 1, 5, 9, 13, 17, 21
P_BIN, P_BOUT, P_LNG, P_LNB, P_CONV, P_LOSS, P_DADA = 0, 8704, 9728, 10752, 11776, 14848, 14976
MIB = 1024 * 1024


def _pcall(body, *, out_shape, out_specs=None, **kw):
    def pin_out(shape, spec):
        in_hbm = getattr(spec, "block_shape", None) is not None or getattr(spec, "memory_space", None) is pl.ANY
        return pltpu.HBM(shape.shape, shape.dtype) if in_hbm and isinstance(shape, jax.ShapeDtypeStruct) else shape

    n_scalar = 0
    if out_specs is None:
        specs = kw["grid_spec"].out_specs
        n_scalar = kw["grid_spec"].num_scalar_prefetch
    else:
        kw["out_specs"] = specs = out_specs
    if isinstance(out_shape, (tuple, list)):
        out_shape = tuple(pin_out(s, p) for s, p in zip(out_shape, specs))
    else:
        out_shape = pin_out(out_shape, specs)
    call = pl.pallas_call(body, out_shape=out_shape, **kw)

    def run(*operands):
        def pin(o):
            is_data = jnp.issubdtype(o.dtype, jnp.floating) or jnp.issubdtype(o.dtype, jnp.integer)
            return pltpu.with_memory_space_constraint(o, pltpu.HBM) if is_data else o
        return call(*operands[:n_scalar], *[pin(o) for o in operands[n_scalar:]])

    return run

ANY = pl.BlockSpec(memory_space=pl.ANY)
VMEM = pl.BlockSpec(memory_space=pltpu.VMEM)


def _whole(a):
    return pl.BlockSpec(a.shape, lambda i: (0,) * len(a.shape))


def _params(vmem_mib=None, sem=None):
    kw = {}
    if vmem_mib is not None:
        kw["vmem_limit_bytes"] = vmem_mib * MIB
    if sem is not None:
        kw["dimension_semantics"] = sem
    return pltpu.CompilerParams(**kw)


def _nn(a, b):
    return jnp.dot(a, b, preferred_element_type=F32)


def _nt(a, b):
    return lax.dot_general(a, b, (((1,), (1,)), ((), ())), preferred_element_type=F32)


def _tn(a, b):
    return lax.dot_general(a, b, (((0,), (0,)), ((), ())), preferred_element_type=F32)


def _sigmoid(v):
    return 0.5 * jnp.tanh(0.5 * v) + 0.5


def _part8(v):
    return v.reshape(v.shape[0] // 8, 8, v.shape[1]).sum(axis=0)


def _my_position():
    return lax.axis_index("x"), lax.axis_index("y"), lax.axis_index("c")


def _flat(px, py, pc):
    return 4 * px + 2 * py + pc


def _peer(mask):
    x, y, c = _my_position()
    return (x ^ ((mask >> 2) & 1), y ^ ((mask >> 1) & 1), c ^ (mask & 1))


def _column_chunks(n):
    chunks = [(128 * a, 0, 128 * a, 128) for a in range(n // 128)]
    if n % 128:
        chunks.append((n - 128, 128 - n % 128, 128 * (n // 128), n % 128))
    return chunks


def _cast_rows(w, n_steps, name):
    rows, ncol = w.shape
    blk = pl.BlockSpec((rows // n_steps, ncol), lambda i: (i, 0))

    def body(w_ref, o_ref):
        o_ref[...] = w_ref[...].astype(BF16)

    return _pcall(body, grid=(n_steps,), out_shape=jax.ShapeDtypeStruct(w.shape, BF16), in_specs=[blk], out_specs=blk,
                  name=name, compiler_params=_params(16, ("parallel",)))(w)


def _prep(w_pa, w_pb, w_out, c, conv_w):
    def body(wpa_ref, wpb_ref, wout_ref, c_ref, cw_ref, wpat_ref, wpb_o, wout_o, cact_ref, cwp_ref):
        wpat_ref[...] = wpa_ref[...].T.astype(BF16)
        wpb_o[...] = wpb_ref[...].astype(BF16)
        wout_o[...] = wout_ref[...].astype(BF16)
        cv = c_ref[...]
        cact_ref[...] = jnp.zeros_like(cact_ref)
        cact_ref[pl.ds(0, cv.shape[0]), :] = cv * _sigmoid(cv)
        cwp_ref[...] = jnp.zeros_like(cwp_ref)
        cwp_ref[pl.ds(0, 3), :] = cw_ref[...]

    out_shape = (jax.ShapeDtypeStruct((w_pa.shape[1], w_pa.shape[0]), BF16),
                 jax.ShapeDtypeStruct(w_pb.shape, BF16), jax.ShapeDtypeStruct(w_out.shape, BF16),
                 jax.ShapeDtypeStruct((8, D), F32), jax.ShapeDtypeStruct((8, conv_w.shape[1]), F32))
    operands = (w_pa, w_pb, w_out, c, conv_w)
    return _pcall(body, grid=(1,), out_shape=out_shape, in_specs=[_whole(a) for a in operands],
                  out_specs=tuple(_whole(o) for o in out_shape), name="prep", compiler_params=_params(16))(*operands)


def _exchange_slots(out_refs, send_sems, recv_sems, base=0):
    me = _flat(*_my_position())

    def copy(a, mask, slot):
        return pltpu.make_async_remote_copy(
            src_ref=out_refs[a].at[slot], dst_ref=out_refs[a].at[slot], send_sem=send_sems.at[base + 7 * a + mask - 1],
            recv_sem=recv_sems.at[base + 7 * a + mask - 1], device_id=_peer(mask), device_id_type=MESH)

    pairs = [(a, mask) for a in range(len(out_refs)) for mask in range(1, N_DEV)]
    for a, mask in pairs:
        copy(a, mask, me).start()
    for a, mask in pairs:
        copy(a, mask, _flat(*_peer(mask))).wait_recv()
    for a, mask in pairs:
        copy(a, mask, me).wait_send()


def _ada_forward(cact_mine, cw_mine, w_ada, b_ada_mine):
    ncol = w_ada.shape[1]

    def body(c_ref, cw_ref, w_ref, b_ref, out_ref, call_ref, cwall_ref, send_sems, recv_sems):
        me = _flat(*_my_position())
        call_ref[me] = c_ref[...]
        cwall_ref[me] = cw_ref[...]
        _exchange_slots([call_ref, cwall_ref], send_sems, recv_sems)
        c_all = call_ref[...].reshape(N_DEV * 8, D).astype(BF16)
        out_ref[me] = (_nn(c_all, w_ref[...].astype(BF16)) + b_ref[...]).reshape(N_DEV, 8, ncol)
        _exchange_slots([out_ref], send_sems, recv_sems, base=14)

    operands = (cact_mine, cw_mine, w_ada, b_ada_mine)
    out_shape = (jax.ShapeDtypeStruct((N_DEV, N_DEV, 8, ncol), F32), jax.ShapeDtypeStruct((N_DEV, 8, D), F32),
                 jax.ShapeDtypeStruct((N_DEV,) + cw_mine.shape, F32))
    return _pcall(body, grid=(1,), out_shape=out_shape, in_specs=[_whole(a) for a in operands], out_specs=(VMEM,) * 3,
                  scratch_shapes=[pltpu.SemaphoreType.DMA((21,)), pltpu.SemaphoreType.DMA((21,))], name="ada_forward",
                  compiler_params=_params(16))(*operands)


def _small_reduce(gb_rest, gb_qkv, svec, dgate, dss):
    nbat = dgate.shape[0]

    def body(gbr_ref, q0_ref, q1_ref, q2_ref, sv_ref, dg_ref, dss_ref, rows_ref, tot_ref, gbada_ref, send_sems, recv_sems):
        me = _flat(*_my_position())

        def put(off, v):
            rows_ref[me, :, pl.ds(off, v.shape[1])] = v

        def row(v):
            return jnp.sum(v, axis=0, keepdims=True)

        for g, q_ref in enumerate((q0_ref, q1_ref, q2_ref)):
            for which in range(3):
                put(P_BIN + SLAB * (3 * which + g), row(q_ref[which]))
        for s in range(N_REST):
            put(P_BIN + SLAB * (N_QKV + s), row(gbr_ref[s]))
        put(P_LNG, row(sv_ref[0]))
        put(P_LNB, row(sv_ref[1]))
        put(P_BOUT, row(sv_ref[2]))
        for j in range(3):
            put(P_CONV + D * j, row(sv_ref[3 + j]))
        loss = (0.5 / D) * jnp.sum(row(sv_ref[6]), axis=1, keepdims=True)
        put(P_LOSS, jnp.broadcast_to(loss, (1, 128)))
        for b in range(nbat):
            put(P_DADA + 3 * D * b, row(dss_ref[b, 0]))
            put(P_DADA + 3 * D * b + D, row(dss_ref[b, 1]))
            put(P_DADA + 3 * D * b + 2 * D, row(dg_ref[b]))
        _exchange_slots([rows_ref], send_sems, recv_sems)
        tot = rows_ref[0]
        for k in range(1, N_DEV):
            tot = tot + rows_ref[k]
        tot_ref[...] = tot
        gbada = tot[:, P_DADA:P_DADA + 3 * D]
        for b in range(1, nbat):
            gbada = gbada + tot[:, P_DADA + 3 * D * b:P_DADA + 3 * D * (b + 1)]
        gbada_ref[...] = gbada

    p_len = P_DADA + nbat * 3 * D
    out_shape = (jax.ShapeDtypeStruct((N_DEV, 1, p_len), F32), jax.ShapeDtypeStruct((1, p_len), F32),
                 jax.ShapeDtypeStruct((1, 3 * D), F32))
    operands = (gb_rest, *gb_qkv, svec, dgate, dss)
    return _pcall(body, grid=(1,), out_shape=out_shape, in_specs=[_whole(a) for a in operands],
                  out_specs=(VMEM, _whole(out_shape[1]), _whole(out_shape[2])),
                  scratch_shapes=[pltpu.SemaphoreType.DMA((7,)), pltpu.SemaphoreType.DMA((7,))], name="small_reduce",
                  compiler_params=_params(16))(*operands)


PIECE = 64
N_CHUNK = 4
ARRIVAL_RANK = (0, 1, 3, 5, 2, 4, 6, 7)
SLOT_MASK = (1, 4, 2, 6, 5, 3, 7)


def _arrival_tables(shard_rows):
    import numpy as np
    crow = shard_rows // N_CHUNK
    table = np.zeros((N_DEV, N_SLAB + 7 * N_CHUNK), np.int32)
    lo = [(SLAB * j) // crow for j in range(N_SLAB)]
    hi = [(SLAB * j + SLAB - 1) // crow for j in range(N_SLAB)]
    for k in range(N_DEV):
        def rank(chunk):
            shard_rank = ARRIVAL_RANK[(chunk // N_CHUNK) ^ k]
            return shard_rank if shard_rank < 2 else 2 + 8 * (chunk % N_CHUNK) + shard_rank
        order = sorted(range(N_SLAB), key=lambda j: (max(rank(lo[j]), rank(hi[j])), j))
        table[k, :N_SLAB] = order
        for slot, mask in enumerate(SLOT_MASK):
            for ch in range(N_CHUNK):
                chunk = (k ^ mask) * N_CHUNK + ch
                table[k, N_SLAB + slot * N_CHUNK + ch] = min(t for t, j in enumerate(order) if lo[j] <= chunk <= hi[j])
    return table


def _project_gather(shard, x, ada, b_in3, others, xt=512):
    t = x.shape[0]
    n_o = len(others)
    srows = shard.shape[0]
    crow = srows // N_CHUNK
    shards = [shard] + list(others)
    table = jnp.asarray(_arrival_tables(srows))
    seq_tiles = (t // ada.shape[0]) // xt

    def body(tbl_ref, *refs):
        srcs = [refs[0]] + list(refs[4:4 + n_o])
        x_ref, ada_ref, b_ref = refs[1], refs[2], refs[3]
        outs = [refs[4 + n_o]] + list(refs[8 + n_o:8 + 2 * n_o])
        qkv_ref, rest_ref, h_out = refs[5 + n_o], refs[6 + n_o], refs[7 + n_o]
        (wtile, obf, of32, h_ref, xbuf, send_sems, recv_sems, local_sems, tile_sems, obf_sems, of32_sems, x_sems,
         h_sems) = refs[8 + 2 * n_o:]
        w_full = outs[0]
        x, y, c = _my_position()
        k = _flat(x, y, c)
        me, sibling = (x, y, c), (x, y, 1 - c)
        chips = [(1 - x, y), (x, 1 - y), (1 - x, 1 - y)]

        def rows(a, px, py, pc, ch):
            r = shards[a].shape[0]
            if ch is None:
                return outs[a].at[pl.ds(pl.multiple_of(_flat(px, py, pc) * r, r), r), :]
            return outs[a].at[pl.ds(pl.multiple_of(_flat(px, py, pc) * r + ch * crow, crow), crow), :]

        def copy(a, slot, block, to, ch=None, src=None):
            sem = slot * N_CHUNK + ch if a == 0 else 7 * (N_CHUNK - 1 + a) + slot
            if src is not None and ch is not None:
                src = src.at[pl.ds(ch * crow, crow), :]
            return pltpu.make_async_remote_copy(
                src_ref=rows(a, *block, ch) if src is None else src, dst_ref=rows(a, *block, ch),
                send_sem=send_sems.at[sem], recv_sem=recv_sems.at[sem], device_id=to, device_id_type=MESH)

        mine = [pltpu.make_async_copy(srcs[a], rows(a, *me, None), local_sems.at[a]) for a in range(1 + n_o)]
        first = []
        for ch in range(N_CHUNK):
            first.append(copy(0, 0, me, sibling, ch, src=srcs[0]))
            first += [copy(0, 1 + j, me, (*chip, c), ch, src=srcs[0]) for j, chip in enumerate(chips)]
        for a in range(1, 1 + n_o):
            first.append(copy(a, 0, me, sibling, src=srcs[a]))
            first += [copy(a, 1 + j, me, (*chip, c), src=srcs[a]) for j, chip in enumerate(chips)]
        for cp in mine + first:
            cp.start()

        def arrive(a, slot, ch=None):
            if slot == 0:
                copy(a, 0, sibling, me, ch).wait_recv()
            elif slot < 4:
                copy(a, slot, (*chips[slot - 1], c), me, ch).wait_recv()
                copy(a, slot + 3, (*chips[slot - 1], c), sibling, ch).start()
            else:
                copy(a, slot, (*chips[slot - 4], 1 - c), me, ch).wait_recv()

        def arrive_for(step):
            for slot in range(7):
                for ch in range(N_CHUNK):
                    @pl.when(tbl_ref[k, N_SLAB + slot * N_CHUNK + ch] == step)
                    def _():
                        arrive(0, slot, ch)

        def fetch(step, buf):
            slab = tbl_ref[k, step]
            for p in range(SLAB // PIECE):
                g0 = slab * SLAB + PIECE * p
                own = (g0 >= k * srows) & (g0 < (k + 1) * srows)
                dst = wtile.at[buf, pl.ds(PIECE * p, PIECE), :]

                @pl.when(own)
                def _():
                    pltpu.make_async_copy(srcs[0].at[pl.ds(pl.multiple_of(g0 - k * srows, PIECE), PIECE), :], dst, tile_sems.at[buf]).start()

                @pl.when(jnp.logical_not(own))
                def _():
                    pltpu.make_async_copy(w_full.at[pl.ds(pl.multiple_of(g0, PIECE), PIECE), :], dst, tile_sems.at[buf]).start()

        def wait_tile(buf):
            pltpu.make_async_copy(w_full.at[pl.ds(0, SLAB), :], wtile.at[buf], tile_sems.at[buf]).wait()

        def put(buf_ref, sems, dst_ref, count, value):
            b = count % 2

            @pl.when(count >= 2)
            def _():
                pltpu.make_async_copy(buf_ref.at[b], dst_ref, sems.at[b]).wait()

            buf_ref[b] = value
            pltpu.make_async_copy(buf_ref.at[b], dst_ref, sems.at[b]).start()

        def drain(buf_ref, sems, dst_ref, count):
            for back in (1, 2):
                @pl.when(count >= back)
                def _():
                    pltpu.make_async_copy(buf_ref.at[(count - back) % 2], dst_ref, sems.at[(count - back) % 2]).wait()

        def x_copy(i):
            return pltpu.make_async_copy(x_ref.at[pl.ds(xt * i, xt), :], xbuf.at[i % 2], x_sems.at[i % 2])

        def h_copy(i):
            return pltpu.make_async_copy(h_ref.at[pl.ds(xt * i, xt), :], h_out.at[pl.ds(xt * i, xt), :], h_sems.at[i % 2])

        x_copy(0).start()
        for i in range(t // xt):
            if i + 1 < t // xt:
                x_copy(i + 1).start()
            x_copy(i).wait()
            b = i // seq_tiles
            h_ref[pl.ds(xt * i, xt), :] = (xbuf[i % 2] * (1.0 + ada_ref[b, 1:2, :]) + ada_ref[b, 0:1, :]).astype(BF16)
            if i >= 2:
                h_copy(i - 2).wait()
            h_copy(i).start()
        for i in range(max(t // xt - 2, 0), t // xt):
            h_copy(i).wait()

        arrive_for(0)
        fetch(0, 0)

        def step(s, carry):
            n_bf, n_f32 = carry
            buf = s % 2

            @pl.when(s + 1 < N_SLAB)
            def _():
                arrive_for(s + 1)
                fetch(s + 1, 1 - buf)

            wait_tile(buf)
            slab = tbl_ref[k, s]
            v = _nt(h_ref[...], wtile[buf]) + b_ref[slab]
            is_qkv = slab < N_QKV

            @pl.when(is_qkv)
            def _():
                put(obf, obf_sems, qkv_ref.at[jnp.minimum(slab, N_QKV - 1)], n_bf, v.astype(BF16))

            @pl.when(jnp.logical_not(is_qkv))
            def _():
                put(of32, of32_sems, rest_ref.at[jnp.maximum(slab - N_QKV, 0)], n_f32, v)

            return n_bf + is_qkv.astype(jnp.int32), n_f32 + 1 - is_qkv.astype(jnp.int32)

        n_bf, n_f32 = lax.fori_loop(0, N_SLAB, step, (jnp.int32(0), jnp.int32(0)))
        drain(obf, obf_sems, qkv_ref.at[0], n_bf)
        drain(of32, of32_sems, rest_ref.at[0], n_f32)

        for slots in ((1, 2, 3), (0, 4, 5, 6)):
            for a in range(1, 1 + n_o):
                for slot in slots:
                    arrive(a, slot)
        for cp in first:
            cp.wait_send()
        for j, chip in enumerate(chips):
            for ch in range(N_CHUNK):
                copy(0, 4 + j, (*chip, c), sibling, ch).wait_send()
            for a in range(1, 1 + n_o):
                copy(a, 4 + j, (*chip, c), sibling).wait_send()
        for cp in mine:
            cp.wait()

    out_shape = ((jax.ShapeDtypeStruct((N_DEV * srows, D), BF16), jax.ShapeDtypeStruct((N_QKV, t, SLAB), BF16),
                  jax.ShapeDtypeStruct((N_REST, t, SLAB), F32), jax.ShapeDtypeStruct((t, D), BF16))
                 + tuple(jax.ShapeDtypeStruct((N_DEV * o.shape[0], o.shape[1]), o.dtype) for o in others))
    n_all = 1 + n_o
    n_sems = 7 * (N_CHUNK + n_o)
    pair = pltpu.SemaphoreType.DMA((2,))
    grid_spec = pltpu.PrefetchScalarGridSpec(
        num_scalar_prefetch=1, grid=(1,),
        in_specs=[ANY, ANY, pl.BlockSpec(ada.shape, lambda i, tbl: (0, 0, 0)),
                  pl.BlockSpec((N_SLAB, 1, SLAB), lambda i, tbl: (0, 0, 0))] + [ANY] * n_o,
        out_specs=(ANY,) * (4 + n_o),
        scratch_shapes=[pltpu.VMEM((2, SLAB, D), BF16), pltpu.VMEM((2, t, SLAB), BF16), pltpu.VMEM((2, t, SLAB), F32),
                        pltpu.VMEM((t, D), BF16), pltpu.VMEM((2, xt, D), F32),
                        pltpu.SemaphoreType.DMA((n_sems,)), pltpu.SemaphoreType.DMA((n_sems,)),
                        pltpu.SemaphoreType.DMA((n_all,)), pair, pair, pair, pair, pair])
    res = _pcall(body, grid_spec=grid_spec, out_shape=out_shape, name="project_gather",
                 compiler_params=_params(48, ("arbitrary",)))(table, shard, x, ada, b_in3, *others)
    return res[0], res[1], res[2], res[3], list(res[4:])


def _bias_tables(g):
    window, dil = GROUPS[g]
    span = window // dil
    qi = jnp.arange(BLK)[:, None]
    kj = jnp.arange(2 * BLK)[None, :]
    delta = qi + BLK - kj
    valid = (delta >= 0) & (delta <= span)
    heads = jnp.arange(4, dtype=F32) + 4.0 * g
    slopes = 2.0 ** (-8.0 * (heads + 1.0) / 12.0)
    bias = -slopes[:, None, None] * (delta * dil).astype(F32)[None]
    return jnp.where(valid[None], bias, -1e30).reshape(4 * BLK, 2 * BLK)


def _head_masks(shape):
    lane = lax.broadcasted_iota(jnp.int32, shape, 1)
    return [(lane >= 64 * h) & (lane < 64 * (h + 1)) for h in range(4)]


def _stack_heads(v, masks):
    return jnp.concatenate([jnp.where(masks[h], v, jnp.zeros_like(v)) for h in range(4)], axis=0)


def _unstack_heads(v4, masks):
    out = jnp.where(masks[0], v4[0:BLK], 0.0)
    for h in range(1, 4):
        out = jnp.where(masks[h], v4[BLK * h:BLK * (h + 1)], out)
    return out


def _regroup(load_half, dst_ref, stage_ref, n, dil):
    for hlf in range(2):
        stage_ref[hlf] = load_half(hlf)

    def residue(r, carry):
        for hlf in range(2):
            dst_ref[pl.ds(pl.multiple_of(r * n, BLK), n), pl.ds(128 * hlf, 128)] = (
                stage_ref[hlf, pl.ds(r, n, stride=dil), :].astype(dst_ref.dtype))
        return carry

    lax.fori_loop(0, dil, residue, 0)


def _store_block(nat_ref, r, i, val, dil):
    for hlf in range(2):
        nat_ref[hlf, pl.ds(r + dil * BLK * i, BLK, stride=dil), :] = val[:, 128 * hlf:128 * (hlf + 1)]


def _for_blocks(block, dil, nblk, ways=WAYS):
    ways = min(ways, max(dil, nblk))
    if dil == 1:
        for i in range(ways):
            block(0, i, i == 0)

        def step(k, carry):
            for j in range(ways):
                block(0, ways * k + j, False)
            return carry

        lax.fori_loop(1, nblk // ways, step, 0)
    else:
        ways = min(ways, dil)

        def residues(k, carry):
            for j in range(ways):
                block(ways * k + j, 0, True)
            if nblk > 1:
                def loop(i, c):
                    for j in range(ways):
                        block(ways * k + j, i, False)
                    return c
                lax.fori_loop(1, nblk, loop, 0)
            return carry

        lax.fori_loop(0, dil // ways, residues, 0)


def _attn_forward(qkv, nbat):
    t = qkv.shape[1]
    seq = t // nbat
    n_grp = len(GROUPS)

    def body(qkv_ref, b0_ref, b1_ref, b2_ref, ol_ref, stage, qs_ref, ks_ref, vs_ref, *nat):
        masks = _head_masks((BLK, SLAB))
        bias_refs = (b0_ref, b1_ref, b2_ref)
        for g, (_, dil) in enumerate(GROUPS):
            n = seq // dil
            bias_ref, nat_o, nat_l = bias_refs[g], nat[2 * g], nat[2 * g + 1]
            if dil > 1:
                qd, kd, vd = qs_ref, ks_ref, vs_ref
                for which, dst in enumerate((qd, kd, vd)):
                    _regroup(lambda hlf, which=which, g=g: qkv_ref[3 * which + g, :, pl.ds(128 * hlf, 128)].astype(F32), dst, stage, n, dil)
            else:
                qd, kd, vd = qkv_ref.at[g], qkv_ref.at[3 + g], qkv_ref.at[6 + g]

            def block(r, i, first, n=n, dil=dil, qd=qd, kd=kd, vd=vd, bias_ref=bias_ref, nat_o=nat_o, nat_l=nat_l):
                base = r * n
                qs = pl.ds(pl.multiple_of(base + i * BLK, BLK), BLK)
                ks = pl.ds(pl.multiple_of(base, BLK), BLK) if first else pl.ds(pl.multiple_of(base + (i - 1) * BLK, BLK), 2 * BLK)
                q, kk, vv = qd[qs, :], kd[ks, :], vd[ks, :]
                bias = bias_ref[:, pl.ds(BLK, BLK)] if first else bias_ref[...]
                s = _nt(_stack_heads(q, masks), kk) * 0.125 + bias
                m = jnp.max(s, axis=1, keepdims=True)
                p = jnp.exp(s - m)
                den = jnp.sum(p, axis=1, keepdims=True)
                out = _unstack_heads(_nn((p * (1.0 / den)).astype(BF16), vv), masks)
                lse = _unstack_heads(jnp.broadcast_to(m + jnp.log(den), (4 * BLK, SLAB)), masks)
                _store_block(nat_o, r, i, out, dil)
                _store_block(nat_l, r, i, lse, dil)

            _for_blocks(block, dil, n // BLK, ways=2 * WAYS)

        for hlf in range(2):
            l0, l1, l2 = nat[1][hlf], nat[3][hlf], nat[5][hlf]
            mx = jnp.maximum(jnp.maximum(l0, l1), l2)
            e0, e1, e2 = jnp.exp(l0 - mx), jnp.exp(l1 - mx), jnp.exp(l2 - mx)
            den = e0 + e1 + e2
            ol_ref[0, :, pl.ds(128 * hlf, 128)] = (e0 * nat[0][hlf] + e1 * nat[2][hlf] + e2 * nat[4][hlf]) * (1.0 / den)
            ol_ref[1, :, pl.ds(128 * hlf, 128)] = mx + jnp.log(den)

    halves = pltpu.VMEM((2, seq, 128), F32)
    bias_spec = pl.BlockSpec((4 * BLK, 2 * BLK), lambda b: (0, 0))
    return _pcall(
        body, grid=(nbat,), out_shape=jax.ShapeDtypeStruct((2, t, SLAB), F32),
        in_specs=[pl.BlockSpec((N_QKV, seq, SLAB), lambda b: (0, b, 0))] + [bias_spec] * n_grp,
        out_specs=pl.BlockSpec((2, seq, SLAB), lambda b: (0, b, 0)),
        scratch_shapes=[halves] + [pltpu.VMEM((seq, SLAB), BF16)] * 3 + [halves] * (2 * n_grp),
        name="attn_forward", compiler_params=_params(56, ("parallel",)))(qkv, *[_bias_tables(g) for g in range(n_grp)])


def _attn_backward(qkv, do_attn, ol_tot, dproj, g, nbat):
    t = qkv.shape[1]
    seq = t // nbat
    dil = GROUPS[g][1]
    n = seq // dil
    nblk = n // BLK
    qkv4 = qkv.reshape(3, 3, t, SLAB)
    dp4 = dproj.reshape(DP_SLABS // 3, 3, t, SLAB)

    def body(qkv_ref, do_ref, ol_ref, bias_ref, dp_in, dp_ref, gb_ref, dk_acc, dv_acc, *scratch):
        del dp_in
        masks = _head_masks((BLK, SLAB))

        @pl.when(pl.program_id(0) == 0)
        def _():
            gb_ref[...] = jnp.zeros_like(gb_ref)

        dk_acc[...] = jnp.zeros_like(dk_acc)
        dv_acc[...] = jnp.zeros_like(dv_acc)
        if dil > 1:
            stage, qd, kd, vd, dod, prodd, lsed, nat = scratch
            lanes = lambda hlf: pl.ds(128 * hlf, 128)
            for which, dst in enumerate((qd, kd, vd)):
                _regroup(lambda hlf, which=which: qkv_ref[which, 0, :, lanes(hlf)].astype(F32), dst, stage, n, dil)
            _regroup(lambda hlf: do_ref[:, lanes(hlf)].astype(F32), dod, stage, n, dil)
            _regroup(lambda hlf: do_ref[:, lanes(hlf)].astype(F32) * ol_ref[0, :, lanes(hlf)], prodd, stage, n, dil)
            _regroup(lambda hlf: ol_ref[1, :, lanes(hlf)], lsed, stage, n, dil)
        else:
            qd, kd, vd = qkv_ref.at[0, 0], qkv_ref.at[1, 0], qkv_ref.at[2, 0]

        def block(r, i, first):
            base = r * n
            qs = pl.ds(pl.multiple_of(base + i * BLK, BLK), BLK)
            ks = pl.ds(pl.multiple_of(base, BLK), BLK) if first else pl.ds(pl.multiple_of(base + (i - 1) * BLK, BLK), 2 * BLK)
            q, kk, vv = qd[qs, :], kd[ks, :], vd[ks, :]
            if dil > 1:
                do, prod, lse = dod[qs, :], prodd[qs, :], lsed[qs, :]
            else:
                do = do_ref[qs, :]
                prod = do.astype(F32) * ol_ref[0, qs, :]
                lse = ol_ref[1, qs, :]
            q4, do4 = _stack_heads(q, masks), _stack_heads(do, masks)
            bias = bias_ref[:, pl.ds(BLK, BLK)] if first else bias_ref[...]
            lse4 = jnp.concatenate([lse[:, 64 * h:64 * h + 1] for h in range(4)], axis=0)
            delta4 = jnp.concatenate([jnp.sum(jnp.where(masks[h], prod, 0.0), axis=1, keepdims=True) for h in range(4)], axis=0)
            p = jnp.exp(_nt(q4, kk) * 0.125 + bias - lse4)
            ds = (p * (_nt(do4, vv) - delta4)).astype(BF16)
            dv_acc[ks, :] += _tn(p.astype(BF16), do4)
            dk_acc[ks, :] += _tn(ds, q4) * 0.125
            dq = _unstack_heads(_nn(ds, kk), masks) * 0.125
            if dil > 1:
                _store_block(nat, r, i, dq, dil)
            else:
                dp_ref[0, 0, qs, :] = dq.astype(BF16)
            gb_ref[0] += _part8(dq)

        _for_blocks(block, dil, nblk)
        gb_ref[1] += _part8(dk_acc[...])
        gb_ref[2] += _part8(dv_acc[...])
        if dil > 1:
            def flush(which):
                for hlf in range(2):
                    dp_ref[which, 0, :, pl.ds(128 * hlf, 128)] = nat[hlf].astype(BF16)

            def to_token_order(acc_ref):
                def residue(r, carry):
                    for hlf in range(2):
                        nat[hlf, pl.ds(r, n, stride=dil), :] = acc_ref[pl.ds(pl.multiple_of(r * n, BLK), n), pl.ds(128 * hlf, 128)]
                    return carry
                lax.fori_loop(0, dil, residue, 0)

            flush(0)
            to_token_order(dk_acc)
            flush(1)
            to_token_order(dv_acc)
            flush(2)
        else:
            dp_ref[1, 0] = dk_acc[...].astype(BF16)
            dp_ref[2, 0] = dv_acc[...].astype(BF16)

    scratch = [pltpu.VMEM((seq, SLAB), F32)] * 2
    if dil > 1:
        scratch += ([pltpu.VMEM((2, seq, 128), F32)] + [pltpu.VMEM((seq, SLAB), BF16)] * 4 + [pltpu.VMEM((seq, SLAB), F32)] * 2
                    + [pltpu.VMEM((2, seq, 128), F32)])
    dp, gb = _pcall(
        body, grid=(nbat,),
        out_shape=(jax.ShapeDtypeStruct(dp4.shape, BF16), jax.ShapeDtypeStruct((3, 8, SLAB), F32)),
        in_specs=[pl.BlockSpec((3, 1, seq, SLAB), lambda b: (0, g, b, 0)),
                  pl.BlockSpec((seq, SLAB), lambda b: (b, 0)),
                  pl.BlockSpec((2, seq, SLAB), lambda b: (0, b, 0)),
                  pl.BlockSpec((4 * BLK, 2 * BLK), lambda b: (0, 0)), ANY],
        out_specs=(pl.BlockSpec((3, 1, seq, SLAB), lambda b: (DP_SLABS // 9 - 1, g, b, 0)),
                   pl.BlockSpec((3, 8, SLAB), lambda b: (0, 0, 0))),
        scratch_shapes=scratch, input_output_aliases={4: 0}, name=f"attn_backward_{g}",
        compiler_params=_params(48, ("arbitrary",)))(qkv4, do_attn, ol_tot, _bias_tables(g), dp4)
    return dp.reshape(DP_SLABS, t, SLAB), gb


def _mid(rest, ol_tot, x, tgt, ada, cw, b_out, ln_g, ln_b, w_pa_t, w_pb, w_out, tm=256):
    t = x.shape[0]
    nbat = ada.shape[0]
    nt = t // tm
    tps = nt // nbat

    def body(rest_ref, halo_ref, ol_ref, x_ref, t_ref, ada_ref, cw_ref, bout_ref, lng_ref, lnb_ref,
             wpat_ref, wpb_ref, wout_ref,
             dp_ref, gx0_ref, doa_ref, mg_ref, dof_ref, bbs_ref, dyc_ref, a_ref, dya_ref,
             gbr_ref, sv_ref, dgate_ref, carry_ref, keep_ref):
        i = pl.program_id(0)
        ti = nt - 1 - i
        pos = ti % tps

        @pl.when(i == 0)
        def _():
            gbr_ref[...] = jnp.zeros_like(gbr_ref)
            sv_ref[...] = jnp.zeros_like(sv_ref)

        @pl.when(pos == tps - 1)
        def _():
            dgate_ref[...] = jnp.zeros_like(dgate_ref)
            carry_ref[...] = jnp.zeros_like(carry_ref)

        row = lax.broadcasted_iota(jnp.int32, (tm, SLAB), 0)
        halo_on = (pos > 0).astype(F32)

        def cols(s):
            return pl.ds(SLAB * s, SLAB)

        o_attn = ol_ref[0]
        z_a = rest_ref[R_ZA]
        sg_za = _sigmoid(z_a)
        a_ref[...] = (o_attn * z_a * sg_za).astype(BF16)
        y_attn = _nt(a_ref[...], wpat_ref[...])

        for s in range(4):
            u = rest_ref[R_GC + s] * rest_ref[R_UX + s]
            hu = halo_ref[R_GC + s] * halo_ref[R_UX + s] * halo_on
            u1 = jnp.where(row == 0, hu[7:8], pltpu.roll(u, 1, 0))
            u2 = jnp.where(row == 0, hu[6:7], jnp.where(row == 1, hu[7:8], pltpu.roll(u, 2, 0)))
            conv = cw_ref[0:1, cols(s)] * u2 + cw_ref[1:2, cols(s)] * u1 + cw_ref[2:3, cols(s)] * u
            zc = rest_ref[R_ZC + s]
            sg = _sigmoid(zc)
            keep_ref[2, :, cols(s)], keep_ref[3, :, cols(s)], keep_ref[4, :, cols(s)], keep_ref[5, :, cols(s)] = u1, u2, conv, sg
            bbs_ref[:, cols(s)] = (rest_ref[R_GB + s] * conv * (zc * sg)).astype(BF16)
        y_conv = _nn(bbs_ref[...], wpb_ref[...])

        for s in range(4):
            s_a, s_b = _sigmoid(rest_ref[R_GA + s]), _sigmoid(rest_ref[R_GBM + s])
            keep_ref[0, :, cols(s)], keep_ref[1, :, cols(s)] = s_a, s_b
            mg_ref[:, cols(s)] = (s_a * y_attn[:, SLAB * s:SLAB * (s + 1)] + s_b * y_conv[:, SLAB * s:SLAB * (s + 1)]).astype(BF16)
        o = _nn(mg_ref[...], wout_ref[...]) + bout_ref[...]
        gate = ada_ref[0, 2:3, :]
        r = ALPHA * x_ref[...] + gate * o
        mu = jnp.mean(r, axis=1, keepdims=True)
        rc = r - mu
        rstd = lax.rsqrt(jnp.mean(rc * rc, axis=1, keepdims=True) + LN_EPS)
        xhat = rc * rstd
        err = xhat * lng_ref[...] + lnb_ref[...] - t_ref[...]
        sv_ref[6] += _part8(err * err)
        dy = err * (1.0 / D)
        sv_ref[0] += _part8(dy * xhat)
        sv_ref[1] += _part8(dy)
        dxh = dy * lng_ref[...]
        dr = rstd * (dxh - jnp.mean(dxh, axis=1, keepdims=True) - xhat * jnp.mean(dxh * xhat, axis=1, keepdims=True))
        gx0_ref[...] = ALPHA * dr
        dgate_ref[0] += _part8(dr * o)
        do_ = dr * gate
        sv_ref[2] += _part8(do_)
        dof_ref[...] = do_.astype(BF16)
        dmerged = _nt(dof_ref[...], wout_ref[...])
        for s in range(4):
            s_a, s_b = keep_ref[0, :, cols(s)], keep_ref[1, :, cols(s)]
            dm = dmerged[:, SLAB * s:SLAB * (s + 1)]
            ya, yc = y_attn[:, SLAB * s:SLAB * (s + 1)], y_conv[:, SLAB * s:SLAB * (s + 1)]
            dya_ref[:, cols(s)] = (dm * s_a).astype(BF16)
            dyc_ref[:, cols(s)] = (dm * s_b).astype(BF16)
            dga = dm * ya * s_a * (1.0 - s_a)
            dgb = dm * yc * s_b * (1.0 - s_b)
            dp_ref[R_GA + s] = dga.astype(BF16)
            dp_ref[R_GBM + s] = dgb.astype(BF16)
            gbr_ref[R_GA + s] += _part8(dga)
            gbr_ref[R_GBM + s] += _part8(dgb)

        da = _nn(dya_ref[...], wpat_ref[...])
        doa_ref[...] = (da * z_a * sg_za).astype(BF16)
        dza = da * o_attn * (sg_za * (1.0 + z_a * (1.0 - sg_za)))
        dp_ref[R_ZA] = dza.astype(BF16)
        gbr_ref[R_ZA] += _part8(dza)

        dbb = _nt(dyc_ref[...], wpb_ref[...])
        for s in range(4):
            ux, gc, zc = rest_ref[R_UX + s], rest_ref[R_GC + s], rest_ref[R_ZC + s]
            u = gc * ux
            u1, u2, conv, sg = keep_ref[2, :, cols(s)], keep_ref[3, :, cols(s)], keep_ref[4, :, cols(s)], keep_ref[5, :, cols(s)]
            gb = rest_ref[R_GB + s]
            d_b = dbb[:, SLAB * s:SLAB * (s + 1)]
            szc = zc * sg
            dgb_ = d_b * conv * szc
            dconv = d_b * gb * szc
            dzc = d_b * gb * conv * (sg * (1.0 + zc * (1.0 - sg)))
            sv_ref[3, :, cols(s)] += _part8(dconv * u2)
            sv_ref[4, :, cols(s)] += _part8(dconv * u1)
            sv_ref[5, :, cols(s)] += _part8(dconv * u)
            nxt = carry_ref[:, cols(s)]
            d1 = jnp.where(row == tm - 1, nxt[0:1], pltpu.roll(dconv, tm - 1, 0))
            d2 = jnp.where(row == tm - 1, nxt[1:2], jnp.where(row == tm - 2, nxt[0:1], pltpu.roll(dconv, tm - 2, 0)))
            carry_ref[:, cols(s)] = dconv[0:8]
            du = cw_ref[2:3, cols(s)] * dconv + cw_ref[1:2, cols(s)] * d1 + cw_ref[0:1, cols(s)] * d2
            dgc, dux = du * ux, du * gc
            for slab, val in ((R_GB + s, dgb_), (R_ZC + s, dzc), (R_GC + s, dgc), (R_UX + s, dux)):
                dp_ref[slab] = val.astype(BF16)
                gbr_ref[slab] += _part8(val)

    def tile(i):
        return nt - 1 - i

    row_blk = lambda i: (tile(i), 0)
    slab_blk = lambda i: (0, tile(i), 0)
    const2 = lambda i: (0, 0)
    const3 = lambda i: (0, 0, 0)
    in_specs = [
        pl.BlockSpec((N_REST, tm, SLAB), slab_blk),
        pl.BlockSpec((N_REST, 8, SLAB), lambda i: (0, jnp.maximum(tile(i) * (tm // 8) - 1, 0), 0)),
        pl.BlockSpec((1, tm, SLAB), slab_blk),
        pl.BlockSpec((tm, D), row_blk), pl.BlockSpec((tm, D), row_blk),
        pl.BlockSpec((1, 3, D), lambda i: (tile(i) // tps, 0, 0)),
        pl.BlockSpec((3, D), const2), pl.BlockSpec((1, D), const2), pl.BlockSpec((1, D), const2), pl.BlockSpec((1, D), const2),
        pl.BlockSpec((D, SLAB), const2), pl.BlockSpec((D, D), const2), pl.BlockSpec((D, D), const2)]
    bf_rows = lambda: jax.ShapeDtypeStruct((t, D), BF16)
    out_shape = (
        jax.ShapeDtypeStruct((DP_SLABS, t, SLAB), BF16), jax.ShapeDtypeStruct((t, D), F32),
        jax.ShapeDtypeStruct((t, SLAB), BF16),
        bf_rows(), bf_rows(), bf_rows(), bf_rows(), jax.ShapeDtypeStruct((t, SLAB), BF16), bf_rows(),
        jax.ShapeDtypeStruct((N_REST, 8, SLAB), F32), jax.ShapeDtypeStruct((7, 8, D), F32),
        jax.ShapeDtypeStruct((nbat, 8, D), F32))
    out_specs = (
        pl.BlockSpec((N_REST, tm, SLAB), slab_blk), pl.BlockSpec((tm, D), row_blk),
        pl.BlockSpec((tm, SLAB), row_blk),
        pl.BlockSpec((tm, D), row_blk), pl.BlockSpec((tm, D), row_blk), pl.BlockSpec((tm, D), row_blk),
        pl.BlockSpec((tm, D), row_blk), pl.BlockSpec((tm, SLAB), row_blk), pl.BlockSpec((tm, D), row_blk),
        pl.BlockSpec((N_REST, 8, SLAB), const3), pl.BlockSpec((7, 8, D), const3),
        pl.BlockSpec((1, 8, D), lambda i: (tile(i) // tps, 0, 0)))
    return _pcall(body, grid=(nt,), out_shape=out_shape, in_specs=in_specs, out_specs=out_specs,
                  scratch_shapes=[pltpu.VMEM((8, D), F32), pltpu.VMEM((6, tm, D), F32)], name="mid",
                  compiler_params=_params(56, ("arbitrary",)))(
        rest, rest, ol_tot, x, tgt, ada, cw, b_out, ln_g, ln_b, w_pa_t, w_pb, w_out)


def _tn_matmul(lhs, rhs, lhs_spec, n_steps, out_rows, out_index, name, after):
    t, n = rhs.shape

    def body(l_ref, r_ref, after_ref, o_ref):
        del after_ref
        o_ref[...] = _tn(l_ref[0] if len(l_ref.shape) == 3 else l_ref[...], r_ref[...])

    return _pcall(body, grid=(n_steps,), out_shape=jax.ShapeDtypeStruct((out_rows, n), F32),
                  in_specs=[lhs_spec, pl.BlockSpec((t, n), lambda j: (0, 0)), ANY],
                  out_specs=pl.BlockSpec((SLAB, n), out_index), name=name,
                  compiler_params=_params(48, ("parallel",)))(lhs, rhs, after)


def _grad_rows_2d(lhs, rhs, name, after, tc=1024):
    t, k = lhs.shape
    n = rhs.shape[1]

    def body(l_ref, r_ref, after_ref, o_ref):
        del after_ref
        part = _tn(l_ref[...], r_ref[...])

        @pl.when(pl.program_id(0) == 0)
        def _():
            o_ref[...] = part

        @pl.when(pl.program_id(0) > 0)
        def _():
            o_ref[...] += part

    return _pcall(body, grid=(t // tc,), out_shape=jax.ShapeDtypeStruct((k, n), F32),
                  in_specs=[pl.BlockSpec((tc, k), lambda i: (i, 0)), pl.BlockSpec((tc, n), lambda i: (i, 0)), ANY],
                  out_specs=pl.BlockSpec((k, n), lambda i: (0, 0)), name=name,
                  compiler_params=_params(32, ("arbitrary",)))(lhs, rhs, after)


def _w_row_block(j):
    return (j + N_QKV) % N_SLAB


def _dp_slab(j):
    return jnp.where(j < N_REST, j, j + 2)


def _grad_w_in_t(dproj, h):
    t = h.shape[0]
    return _tn_matmul(dproj, h, pl.BlockSpec((1, t, SLAB), lambda j: (_dp_slab(j), 0, 0)), N_SLAB, D_IN,
                      lambda j: (_w_row_block(j), 0), "grad_w_in", h)


def _grad_h(dproj, w_in_t, gx0, x, ada, after, tm=512):
    t = x.shape[0]
    nbat = ada.shape[0]
    tps = (t // nbat) // tm

    def body(dp_ref, w_ref, gx0_ref, x_ref, ada_ref, after_ref, gx_ref, dss_ref):
        del after_ref
        i = pl.program_id(0)
        dh = None
        for j in range(N_SLAB):
            slab = j if j < N_REST else j + 2
            part = _nn(dp_ref[slab], w_ref[pl.ds(SLAB * ((j + N_QKV) % N_SLAB), SLAB), :])
            dh = part if dh is None else dh + part
        gx_ref[...] = gx0_ref[...] + dh * (1.0 + ada_ref[0, 1:2, :])

        @pl.when((i % tps) == 0)
        def _():
            dss_ref[...] = jnp.zeros_like(dss_ref)

        dss_ref[0, 0] += _part8(dh)
        dss_ref[0, 1] += _part8(dh * x_ref[...])

    return _pcall(
        body, grid=(t // tm,),
        out_shape=(jax.ShapeDtypeStruct((t, D), F32), jax.ShapeDtypeStruct((nbat, 2, 8, D), F32)),
        in_specs=[pl.BlockSpec((DP_SLABS, tm, SLAB), lambda i: (0, i, 0)),
                  pl.BlockSpec((D_IN, D), lambda i: (0, 0), pipeline_mode=pl.Buffered(1)),
                  pl.BlockSpec((tm, D), lambda i: (i, 0)), pl.BlockSpec((tm, D), lambda i: (i, 0)),
                  pl.BlockSpec((1, 3, D), lambda i: (i // tps, 0, 0)), ANY],
        out_specs=(pl.BlockSpec((tm, D), lambda i: (i, 0)),
                   pl.BlockSpec((1, 2, 8, D), lambda i: (i // tps, 0, 0, 0))),
        name="grad_h", compiler_params=_params(60, ("arbitrary",)))(dproj, w_in_t, gx0, x, ada, after)


def _chip(m):
    x, y, _ = _my_position()
    return (x ^ ((m >> 1) & 1), y ^ (m & 1))


def _exchange_siblings(grads, after, name):
    n = len(grads)

    def body(*refs):
        copies = _sibling_copies(refs[:n], refs[n + 1:2 * n + 1], refs[2 * n + 1], refs[2 * n + 2])
        for cp in copies:
            cp.start()
        for cp in copies:
            cp.wait()

    return _pcall(body, out_shape=tuple(_sibling_zones(grads)), in_specs=[ANY] * (n + 1), out_specs=(ANY,) * n,
                  name=name, scratch_shapes=[pltpu.SemaphoreType.DMA((4 * n,))] * 2)(*grads, after)


def _sibling_zones(grads):
    return [jax.ShapeDtypeStruct((4, g.shape[0] // N_DEV, g.shape[1]), g.dtype) for g in grads]


def _sibling_copies(srcs, lands, send_sems, recv_sems):
    x, y, c = _my_position()
    copies = []
    for a, (src, land) in enumerate(zip(srcs, lands)):
        rows = land.shape[1]
        for m in range(4):
            dev = _flat(*_chip(m), 1 - c)
            copies.append(pltpu.make_async_remote_copy(
                src_ref=src.at[pl.ds(pl.multiple_of(dev * rows, 8), rows), :], dst_ref=land.at[m],
                send_sem=send_sems.at[4 * a + m], recv_sem=recv_sems.at[4 * a + m], device_id=(x, y, 1 - c),
                device_id_type=MESH))
    return copies


def _chip_copies(srcs, lands, send_sems, recv_sems):
    _, _, c = _my_position()
    return [pltpu.make_async_remote_copy(
        src_ref=srcs[a].at[m - 1], dst_ref=lands[a].at[m - 1], send_sem=send_sems.at[3 * a + m - 1],
        recv_sem=recv_sems.at[3 * a + m - 1], device_id=(*_chip(m), c), device_id_type=MESH)
        for a in range(len(srcs)) for m in range(1, 4)]


HBM = pl.BlockSpec(memory_space=pltpu.HBM)
SEM = pl.BlockSpec(memory_space=pltpu.SEMAPHORE)
SPLIT_COPY = pltpu.CompilerParams(has_side_effects=pltpu.SideEffectType.DATAFLOW_SIDE_EFFECTING)


def _start_copies(make_copies, n_sems, srcs, zones, name):
    n = len(srcs)

    def body(*refs):
        for cp in make_copies(refs[:n], refs[n:2 * n], refs[2 * n], refs[2 * n + 1]):
            cp.start()
        refs[-1][...] = jnp.zeros_like(refs[-1])

    hbm = tuple(pltpu.HBM(b.shape, b.dtype) for b in list(srcs) + list(zones))
    out_shape = (pltpu.SemaphoreType.DMA((n_sems,)), pltpu.SemaphoreType.DMA((n_sems,))) + hbm + (jax.ShapeDtypeStruct((8, 128), F32),)
    operands = [pltpu.with_memory_space_constraint(b, pltpu.HBM) for b in srcs]
    operands += [pltpu.with_memory_space_constraint(lax.empty(z.shape, z.dtype), pltpu.HBM) for z in zones]
    res = _pcall(body, out_shape=out_shape, in_specs=[HBM] * (2 * n), out_specs=(SEM, SEM) + (HBM,) * (2 * n) + (VMEM,),
                 input_output_aliases={i: 2 + i for i in range(2 * n)}, name=name, compiler_params=SPLIT_COPY)(*operands)
    return (res[0], res[1], res[2:2 + n], res[2 + n:2 + 2 * n]), res[-1]


def _wait_copies(make_copies, flight, after, name):
    send_sems, recv_sems, srcs, zones = flight
    n = len(srcs)

    def body(*refs):
        for cp in make_copies(refs[:n], refs[n:2 * n], refs[2 * n], refs[2 * n + 1]):
            cp.wait_send()
            cp.wait_recv()

    hbm = tuple(pltpu.HBM(b.shape, b.dtype) for b in list(srcs) + list(zones))
    res = _pcall(body, out_shape=hbm, in_specs=[HBM] * (2 * n) + [SEM, SEM, ANY], out_specs=(HBM,) * (2 * n),
                 input_output_aliases={i: i for i in range(2 * n)}, name=name, compiler_params=SPLIT_COPY)(
        *srcs, *zones, send_sems, recv_sems, after)
    return res[:n], res[n:]


def _pair_sums(devs, grads, lands, n_steps, name):
    n = len(grads)
    rows = [l.shape[1] for l in lands]
    rbs = [r // n_steps for r in rows]

    def body(devs_ref, *refs):
        del devs_ref
        g_refs, land_refs, outs = refs[:4 * n], refs[4 * n:5 * n], refs[5 * n:]
        for a in range(n):
            outs[2 * a][...] = g_refs[4 * a][...] + land_refs[a][0]
            for m in range(1, 4):
                outs[2 * a + 1][m - 1] = (g_refs[4 * a + m][...] + land_refs[a][m]).astype(BF16)

    def block_of(m, per_dev):
        return lambda i, devs_ref: (devs_ref[m] * per_dev + i, 0)

    in_specs = [pl.BlockSpec((rb, l.shape[2]), block_of(m, n_steps)) for rb, l in zip(rbs, lands) for m in range(4)]
    in_specs += [pl.BlockSpec((4, rb, l.shape[2]), lambda i, devs_ref: (0, i, 0)) for rb, l in zip(rbs, lands)]
    out_shape, out_specs = [], []
    for rb, l in zip(rbs, lands):
        out_shape += [jax.ShapeDtypeStruct(l.shape[1:], F32), jax.ShapeDtypeStruct((3,) + l.shape[1:], BF16)]
        out_specs += [pl.BlockSpec((rb, l.shape[2]), lambda i, devs_ref: (i, 0)),
                      pl.BlockSpec((3, rb, l.shape[2]), lambda i, devs_ref: (0, i, 0))]
    grid_spec = pltpu.PrefetchScalarGridSpec(num_scalar_prefetch=1, grid=(n_steps,), in_specs=in_specs, out_specs=tuple(out_specs))
    res = _pcall(body, grid_spec=grid_spec, out_shape=tuple(out_shape), name=name,
                 compiler_params=_params(48, ("parallel",)))(devs, *[g for g in grads for _ in range(4)], *lands)
    return res[0::2], res[1::2]


def _final_sums(mine, lands, n_steps, name):
    n = len(mine)
    rbs = [o.shape[0] // n_steps for o in mine]

    def body(*refs):
        mine_refs, land_refs, outs = refs[:n], refs[n:2 * n], refs[2 * n:]
        for a in range(n):
            tot = mine_refs[a][...]
            for m in range(3):
                tot = tot + land_refs[a][m].astype(F32)
            outs[a][...] = tot

    in_specs = ([pl.BlockSpec((rb, o.shape[1]), lambda i: (i, 0)) for rb, o in zip(rbs, mine)]
                + [pl.BlockSpec((3, rb, o.shape[1]), lambda i: (0, i, 0)) for rb, o in zip(rbs, mine)])
    out_specs = tuple(pl.BlockSpec((rb, o.shape[1]), lambda i: (i, 0)) for rb, o in zip(rbs, mine))
    out_shape = tuple(jax.ShapeDtypeStruct(o.shape, F32) for o in mine)
    return _pcall(body, grid=(n_steps,), out_shape=out_shape, in_specs=in_specs, out_specs=out_specs, name=name,
                  compiler_params=_params(32, ("parallel",)))(*mine, *lands)


def _reduce_scatter_begin(big, small_after_start):
    c = lax.axis_index("c")
    devs = jnp.stack([_flat(*_chip(m), c) for m in range(4)]).astype(jnp.int32)
    flight, token = _start_copies(_sibling_copies, 4, [big], _sibling_zones([big]), "siblings_start")
    small = small_after_start(token)
    (big,), big_lands = _wait_copies(_sibling_copies, flight, small[-1], "siblings_wait")
    big_mine, big_send = _pair_sums(devs, [big], big_lands, 4, "pair_sums_w_in")
    big_flight, token = _start_copies(_chip_copies, 3, list(big_send), list(big_send), "chips_start_w_in")
    small_lands = _exchange_siblings(small, token, "exchange_siblings_rest")
    small_mine, small_send = _pair_sums(devs, small, small_lands, 1, "pair_sums_rest")
    small_flight, token = _start_copies(_chip_copies, 3 * len(small), list(small_send), list(small_send), "chips_start_rest")
    return (big_flight, small_flight, list(big_mine) + list(small_mine)), token


def _reduce_scatter_end(state, after):
    big_flight, small_flight, mine = state
    _, big_got = _wait_copies(_chip_copies, big_flight, after, "chips_wait_w_in")
    _, small_got = _wait_copies(_chip_copies, small_flight, after, "chips_wait_rest")
    small = _final_sums(mine[1:], small_got, 1, "final_sums_rest")
    return (mine[0], big_got[0]), list(small)


def _adamw(w, g, m, v):
    m_new = B1 * m + (1.0 - B1) * g
    v_new = B2 * v + (1.0 - B2) * (g * g)
    m_hat = m_new / (1.0 - B1 ** STEP)
    v_hat = v_new / (1.0 - B2 ** STEP)
    delta = -LR * (m_hat / (jnp.sqrt(v_hat) + EPS) + WD * w)
    return delta, m_new, v_new


def _final_sum_adam_rows(mine, land, w, m, v, n_steps, name):
    rows, ncol = w.shape
    blk = pl.BlockSpec((rows // n_steps, ncol), lambda i: (i, 0))

    def body(mine_ref, land_ref, w_ref, m_ref, v_ref, g_ref, d_ref, mo_ref, vo_ref):
        g = mine_ref[...]
        for k in range(3):
            g = g + land_ref[k].astype(F32)
        g_ref[...] = g
        d_ref[...], mo_ref[...], vo_ref[...] = _adamw(w_ref[...], g, m_ref[...], v_ref[...])

    shape = jax.ShapeDtypeStruct(w.shape, F32)
    return _pcall(body, grid=(n_steps,), out_shape=(shape,) * 4,
                  in_specs=[blk, pl.BlockSpec((3, rows // n_steps, ncol), lambda i: (0, i, 0)), blk, blk, blk],
                  out_specs=(blk,) * 4, name=name, compiler_params=_params(32, ("parallel",)))(mine, land, w, m, v)


def _adam_transposed(g_t, w, m, v, name):
    n, k = g_t.shape
    rb = min(k, 128)

    def body(gt_ref, w_ref, m_ref, v_ref, g_ref, d_ref, mo_ref, vo_ref):
        for src, skip, dst, size in _column_chunks(n):
            sl = pl.ds(dst, size)
            g = gt_ref[pl.ds(src, 128), :].T[:, skip:]
            delta, m_new, v_new = _adamw(w_ref[:, sl], g, m_ref[:, sl], v_ref[:, sl])
            g_ref[:, sl], d_ref[:, sl], mo_ref[:, sl], vo_ref[:, sl] = g, delta, m_new, v_new

    shape = jax.ShapeDtypeStruct(w.shape, F32)
    rows = pl.BlockSpec((rb, n), lambda i: (i, 0))
    return _pcall(body, grid=(k // rb,), out_shape=(shape,) * 4,
                  in_specs=[pl.BlockSpec((n, rb), lambda i: (0, i)), rows, rows, rows], out_specs=(rows,) * 4, name=name,
                  compiler_params=_params(32, ("parallel",)))(g_t, w, m, v)


def _adam_many(items, name):
    n = len(items)

    def body(*refs):
        ins, outs = refs[:4 * n], refs[4 * n:]
        for a in range(n):
            w_ref, g_ref, m_ref, v_ref = ins[4 * a:4 * a + 4]
            delta, m_new, v_new = _adamw(w_ref[...], g_ref[...], m_ref[...], v_ref[...])
            outs[3 * a][...], outs[3 * a + 1][...], outs[3 * a + 2][...] = delta, m_new, v_new

    out_shape = tuple(jax.ShapeDtypeStruct(it[0].shape, F32) for it in items for _ in range(3))
    flat = [arr for it in items for arr in it]
    res = _pcall(body, grid=(1,), out_shape=out_shape, in_specs=[_whole(a) for a in flat],
                 out_specs=tuple(_whole(o) for o in out_shape), name=name, compiler_params=_params(32))(*flat)
    return [tuple(res[3 * a:3 * a + 3]) for a in range(n)]


def _adam_w_ada(cact_all, dada_mine, w, m, v):
    def body(c_ref, d_ref, w_ref, m_ref, v_ref, g_ref, dl_ref, mo_ref, vo_ref):
        g = _tn(c_ref[...].astype(BF16), d_ref[...].astype(BF16))
        delta, m_new, v_new = _adamw(w_ref[...], g, m_ref[...], v_ref[...])
        g_ref[...], dl_ref[...], mo_ref[...], vo_ref[...] = g, delta, m_new, v_new

    shape = jax.ShapeDtypeStruct(w.shape, F32)
    operands = (cact_all, dada_mine, w, m, v)
    return _pcall(body, grid=(1,), out_shape=(shape,) * 4, in_specs=[_whole(a) for a in operands],
                  out_specs=(_whole(w),) * 4, name="adam_w_ada", compiler_params=_params(32))(*operands)


def kernel(x, c, w_ada, b_ada, w_in, b_in, conv_w, w_proj_attn, w_proj_conv, w_out, b_out, ln_g, ln_b, loss_target, m_w_ada, m_b_ada, m_w_in, m_b_in, m_conv_w, m_w_proj_attn, m_w_proj_conv, m_w_out, m_b_out, m_ln_g, m_ln_b, v_w_ada, v_b_ada, v_w_in, v_b_in, v_conv_w, v_w_proj_attn, v_w_proj_conv, v_w_out, v_b_out, v_ln_g, v_ln_b):
    nbat, seq, _ = x.shape
    t = nbat * seq
    me = _flat(*_my_position())
    x2, tgt2 = x.reshape(t, D), loss_target.reshape(t, D)
    sq = lambda a: a.reshape(a.shape[1:])

    tr = lambda a: a[0].T
    w_in_rows = tr(w_in)
    w_in_t_s = _cast_rows(w_in_rows, 4, "cast_w_in")
    w_pa_t_s, w_pb_s, w_out_s, cact_s, cw_s = _prep(sq(w_proj_attn), sq(w_proj_conv), sq(w_out), c, sq(conv_w))

    ncol = w_ada.shape[2]
    b_ada_mine = lax.dynamic_slice(b_ada, (0, me * ncol), (1, ncol))
    ada_slots, cact_slots, cw_slots = _ada_forward(cact_s, cw_s, sq(w_ada), b_ada_mine)
    cact_all = cact_slots[:, :nbat].reshape(N_DEV * nbat, D)
    cw = cw_slots[:, :3].transpose(1, 0, 2).reshape(3, D)
    ada_all = ada_slots[:, :, :nbat].transpose(1, 2, 0, 3).reshape(N_DEV * nbat, 3, D)
    ada = lax.dynamic_slice(ada_all, (me * nbat, 0, 0), (nbat, 3, D))

    w_in_t, qkv, rest, h, (w_pa_t, w_pb, w_o) = _project_gather(
        w_in_t_s, x2, ada, b_in.reshape(N_SLAB, 1, SLAB), [w_pa_t_s, w_pb_s, w_out_s])
    ol_tot = _attn_forward(qkv, nbat)
    (dproj, gx0, do_attn, merged, do_f, bbs, dyc, a_bf, dya, gb_rest, svec, dgate) = _mid(
        rest, ol_tot, x2, tgt2, ada, cw, b_out, ln_g, ln_b, w_pa_t, w_pb, w_o)

    gb_qkv = []
    for g in range(3):
        dproj, gb = _attn_backward(qkv, do_attn, ol_tot, dproj, g, nbat)
        gb_qkv.append(gb)
    g_w_in_t = _grad_w_in_t(dproj, h)

    def small_grads(token):
        g_w_out = _grad_rows_2d(merged, do_f, "grad_w_out", token)
        g_w_pb = _grad_rows_2d(bbs, dyc, "grad_w_proj_conv", g_w_out)
        g_w_pa_t = _grad_rows_2d(dya, a_bf, "grad_w_proj_attn", g_w_pb)
        return [g_w_out, g_w_pb, g_w_pa_t]

    rs_state, token = _reduce_scatter_begin(g_w_in_t, small_grads)
    grad_x, dss = _grad_h(dproj, w_in_t, gx0, x2, ada, token)

    rows8, tot, g_bada = _small_reduce(gb_rest, gb_qkv, svec, dgate, dss)
    (g_in_mine, g_in_got), (g_out, g_pb, g_pa_t) = _reduce_scatter_end(rs_state, tot)
    loss = tot[0, P_LOSS]
    dada_all = rows8[:, 0, P_DADA:].reshape(N_DEV * nbat, 3 * D)
    dada_mine = lax.dynamic_slice(dada_all, (0, me * ncol), (N_DEV * nbat, ncol))

    g_in_t, d_win_t, nm_win_t, nv_win_t = _final_sum_adam_rows(g_in_mine, g_in_got, w_in_rows, tr(m_w_in), tr(v_w_in), 4, "adam_w_in")
    g_win, d_win, nm_win, nv_win = g_in_t.T, d_win_t.T, nm_win_t.T, nv_win_t.T
    g_wpa, d_wpa, nm_wpa, nv_wpa = _adam_transposed(g_pa_t, sq(w_proj_attn), sq(m_w_proj_attn), sq(v_w_proj_attn), "adam_w_proj_attn")
    g_wada, d_wada, nm_wada, nv_wada = _adam_w_ada(cact_all, dada_mine, sq(w_ada), sq(m_w_ada), sq(v_w_ada))
    g_bin = tot[:, P_BIN:P_BIN + D_IN]
    g_bout = tot[:, P_BOUT:P_BOUT + D]
    g_lng = tot[:, P_LNG:P_LNG + D]
    g_lnb = tot[:, P_LNB:P_LNB + D]
    g_conv = lax.dynamic_slice(tot[:, P_CONV:P_CONV + 3 * D].reshape(3, D), (0, me * cw_s.shape[1]), (3, cw_s.shape[1]))
    upd = _adam_many([
        (sq(w_proj_conv), g_pb, sq(m_w_proj_conv), sq(v_w_proj_conv)),
        (sq(w_out), g_out, sq(m_w_out), sq(v_w_out)),
        (b_ada, g_bada, m_b_ada, v_b_ada), (b_in, g_bin, m_b_in, v_b_in), (sq(conv_w), g_conv, sq(m_conv_w), sq(v_conv_w)),
        (b_out, g_bout, m_b_out, v_b_out), (ln_g, g_lng, m_ln_g, v_ln_g), (ln_b, g_lnb, m_ln_b, v_ln_b)], "adam_rest")
    (d_wpb, nm_wpb, nv_wpb), (d_wout, nm_wout, nv_wout), (d_bada, nm_bada, nv_bada), (d_bin, nm_bin, nv_bin), \
        (d_conv, nm_conv, nv_conv), (d_bout, nm_bout, nv_bout), (d_lng, nm_lng, nv_lng), (d_lnb, nm_lnb, nv_lnb) = upd

    ex = lambda a: a.reshape((1,) + a.shape)
    grads = [ex(g_wada), g_bada, ex(g_win), g_bin, ex(g_conv), ex(g_wpa), ex(g_pb), ex(g_out), g_bout, g_lng, g_lnb]
    deltas = [ex(d_wada), d_bada, ex(d_win), d_bin, ex(d_conv), ex(d_wpa), ex(d_wpb), ex(d_wout), d_bout, d_lng, d_lnb]
    new_m = [ex(nm_wada), nm_bada, ex(nm_win), nm_bin, ex(nm_conv), ex(nm_wpa), ex(nm_wpb), ex(nm_wout), nm_bout, nm_lng, nm_lnb]
    new_v = [ex(nv_wada), nv_bada, ex(nv_win), nv_bin, ex(nv_conv), ex(nv_wpa), ex(nv_wpb), ex(nv_wout), nv_bout, nv_lng, nv_lnb]
    return (loss, grad_x.reshape(x.shape), *grads, *deltas, *new_m, *new_v)
```

```python
import jax
import jax.numpy as jnp
from jax import lax
from jax.experimental import pallas as pl
from jax.experimental.pallas import tpu as pltpu

F32, BF16 = jnp.float32, jnp.bfloat16
MESH = pl.DeviceIdType.MESH
N_DEV = 8
D = 1024
SLAB = 256
N_QKV, N_REST = 9, 25
N_SLAB = N_QKV + N_REST
D_IN = N_SLAB * SLAB
DP_SLABS = 36
BLK = 128
WAYS = 4
GROUPS = ((128, 1), (512, 4), (2048, 16))
ALPHA = 2.0 ** 0.25
LN_EPS = 1e-5
LR, B1, B2, EPS, WD, STEP = 0.001, 0.9, 0.999, 1e-08, 0.01, 10
R_ZA, R_UX, R_GB, R_GC, R_ZC, R_GA, R_GBM = 0, 1, 5, 9, 13, 17, 21
P_BIN, P_BOUT, P_LNG, P_LNB, P_CONV, P_LOSS, P_DADA = 0, 8704, 9728, 10752, 11776, 14848, 14976
MIB = 1024 * 1024


def _pcall(body, *, out_shape, out_specs=None, **kw):
    def pin_out(shape, spec):
        in_hbm = getattr(spec, "block_shape", None) is not None or getattr(spec, "memory_space", None) is pl.ANY
        return pltpu.HBM(shape.shape, shape.dtype) if in_hbm and isinstance(shape, jax.ShapeDtypeStruct) else shape

    n_scalar = 0
    if out_specs is None:
        specs = kw["grid_spec"].out_specs
        n_scalar = kw["grid_spec"].num_scalar_prefetch
    else:
        kw["out_specs"] = specs = out_specs
    if isinstance(out_shape, (tuple, list)):
        out_shape = tuple(pin_out(s, p) for s, p in zip(out_shape, specs))
    else:
        out_shape = pin_out(out_shape, specs)
    call = pl.pallas_call(body, out_shape=out_shape, **kw)

    def run(*operands):
        def pin(o):
            is_data = jnp.issubdtype(o.dtype, jnp.floating) or jnp.issubdtype(o.dtype, jnp.integer)
            return pltpu.with_memory_space_constraint(o, pltpu.HBM) if is_data else o
        return call(*operands[:n_scalar], *[pin(o) for o in operands[n_scalar:]])

    return run

ANY = pl.BlockSpec(memory_space=pl.ANY)
VMEM = pl.BlockSpec(memory_space=pltpu.VMEM)


def _whole(a):
    return pl.BlockSpec(a.shape, lambda i: (0,) * len(a.shape))


def _params(vmem_mib=None, sem=None):
    kw = {}
    if vmem_mib is not None:
        kw["vmem_limit_bytes"] = vmem_mib * MIB
    if sem is not None:
        kw["dimension_semantics"] = sem
    return pltpu.CompilerParams(**kw)


def _nn(a, b):
    return jnp.dot(a, b, preferred_element_type=F32)


def _nt(a, b):
    return lax.dot_general(a, b, (((1,), (1,)), ((), ())), preferred_element_type=F32)


def _tn(a, b):
    return lax.dot_general(a, b, (((0,), (0,)), ((), ())), preferred_element_type=F32)


def _sigmoid(v):
    return 0.5 * jnp.tanh(0.5 * v) + 0.5


def _part8(v):
    return v.reshape(v.shape[0] // 8, 8, v.shape[1]).sum(axis=0)


def _my_position():
    return lax.axis_index("x"), lax.axis_index("y"), lax.axis_index("c")


def _flat(px, py, pc):
    return 4 * px + 2 * py + pc


def _peer(mask):
    x, y, c = _my_position()
    return (x ^ ((mask >> 2) & 1), y ^ ((mask >> 1) & 1), c ^ (mask & 1))


def _column_chunks(n):
    chunks = [(128 * a, 0, 128 * a, 128) for a in range(n // 128)]
    if n % 128:
        chunks.append((n - 128, 128 - n % 128, 128 * (n // 128), n % 128))
    return chunks


def _cast_rows(w, n_steps, name):
    rows, ncol = w.shape
    blk = pl.BlockSpec((rows // n_steps, ncol), lambda i: (i, 0))

    def body(w_ref, o_ref):
        o_ref[...] = w_ref[...].astype(BF16)

    return _pcall(body, grid=(n_steps,), out_shape=jax.ShapeDtypeStruct(w.shape, BF16), in_specs=[blk], out_specs=blk,
                  name=name, compiler_params=_params(16, ("parallel",)))(w)


def _prep(w_pa, w_pb, w_out, c, conv_w):
    def body(wpa_ref, wpb_ref, wout_ref, c_ref, cw_ref, wpat_ref, wpb_o, wout_o, cact_ref, cwp_ref):
        wpat_ref[...] = wpa_ref[...].T.astype(BF16)
        wpb_o[...] = wpb_ref[...].astype(BF16)
        wout_o[...] = wout_ref[...].astype(BF16)
        cv = c_ref[...]
        cact_ref[...] = jnp.zeros_like(cact_ref)
        cact_ref[pl.ds(0, cv.shape[0]), :] = cv * _sigmoid(cv)
        cwp_ref[...] = jnp.zeros_like(cwp_ref)
        cwp_ref[pl.ds(0, 3), :] = cw_ref[...]

    out_shape = (jax.ShapeDtypeStruct((w_pa.shape[1], w_pa.shape[0]), BF16),
                 jax.ShapeDtypeStruct(w_pb.shape, BF16), jax.ShapeDtypeStruct(w_out.shape, BF16),
                 jax.ShapeDtypeStruct((8, D), F32), jax.ShapeDtypeStruct((8, conv_w.shape[1]), F32))
    operands = (w_pa, w_pb, w_out, c, conv_w)
    return _pcall(body, grid=(1,), out_shape=out_shape, in_specs=[_whole(a) for a in operands],
                  out_specs=tuple(_whole(o) for o in out_shape), name="prep", compiler_params=_params(16))(*operands)


def _exchange_slots(out_refs, send_sems, recv_sems, base=0):
    me = _flat(*_my_position())

    def copy(a, mask, slot):
        return pltpu.make_async_remote_copy(
            src_ref=out_refs[a].at[slot], dst_ref=out_refs[a].at[slot], send_sem=send_sems.at[base + 7 * a + mask - 1],
            recv_sem=recv_sems.at[base + 7 * a + mask - 1], device_id=_peer(mask), device_id_type=MESH)

    pairs = [(a, mask) for a in range(len(out_refs)) for mask in range(1, N_DEV)]
    for a, mask in pairs:
        copy(a, mask, me).start()
    for a, mask in pairs:
        copy(a, mask, _flat(*_peer(mask))).wait_recv()
    for a, mask in pairs:
        copy(a, mask, me).wait_send()


def _ada_forward(cact_mine, cw_mine, w_ada, b_ada_mine):
    ncol = w_ada.shape[1]

    def body(c_ref, cw_ref, w_ref, b_ref, out_ref, call_ref, cwall_ref, send_sems, recv_sems):
        me = _flat(*_my_position())
        call_ref[me] = c_ref[...]
        cwall_ref[me] = cw_ref[...]
        _exchange_slots([call_ref, cwall_ref], send_sems, recv_sems)
        c_all = call_ref[...].reshape(N_DEV * 8, D).astype(BF16)
        out_ref[me] = (_nn(c_all, w_ref[...].astype(BF16)) + b_ref[...]).reshape(N_DEV, 8, ncol)
        _exchange_slots([out_ref], send_sems, recv_sems, base=14)

    operands = (cact_mine, cw_mine, w_ada, b_ada_mine)
    out_shape = (jax.ShapeDtypeStruct((N_DEV, N_DEV, 8, ncol), F32), jax.ShapeDtypeStruct((N_DEV, 8, D), F32),
                 jax.ShapeDtypeStruct((N_DEV,) + cw_mine.shape, F32))
    return _pcall(body, grid=(1,), out_shape=out_shape, in_specs=[_whole(a) for a in operands], out_specs=(VMEM,) * 3,
                  scratch_shapes=[pltpu.SemaphoreType.DMA((21,)), pltpu.SemaphoreType.DMA((21,))], name="ada_forward",
                  compiler_params=_params(16))(*operands)


def _small_reduce(gb_rest, gb_qkv, svec, dgate, dss):
    nbat = dgate.shape[0]

    def body(gbr_ref, q0_ref, q1_ref, q2_ref, sv_ref, dg_ref, dss_ref, rows_ref, tot_ref, gbada_ref, send_sems, recv_sems):
        me = _flat(*_my_position())

        def put(off, v):
            rows_ref[me, :, pl.ds(off, v.shape[1])] = v

        def row(v):
            return jnp.sum(v, axis=0, keepdims=True)

        for g, q_ref in enumerate((q0_ref, q1_ref, q2_ref)):
            for which in range(3):
                put(P_BIN + SLAB * (3 * which + g), row(q_ref[which]))
        for s in range(N_REST):
            put(P_BIN + SLAB * (N_QKV + s), row(gbr_ref[s]))
        put(P_LNG, row(sv_ref[0]))
        put(P_LNB, row(sv_ref[1]))
        put(P_BOUT, row(sv_ref[2]))
        for j in range(3):
            put(P_CONV + D * j, row(sv_ref[3 + j]))
        loss = (0.5 / D) * jnp.sum(row(sv_ref[6]), axis=1, keepdims=True)
        put(P_LOSS, jnp.broadcast_to(loss, (1, 128)))
        for b in range(nbat):
            put(P_DADA + 3 * D * b, row(dss_ref[b, 0]))
            put(P_DADA + 3 * D * b + D, row(dss_ref[b, 1]))
            put(P_DADA + 3 * D * b + 2 * D, row(dg_ref[b]))
        _exchange_slots([rows_ref], send_sems, recv_sems)
        tot = rows_ref[0]
        for k in range(1, N_DEV):
            tot = tot + rows_ref[k]
        tot_ref[...] = tot
        gbada = tot[:, P_DADA:P_DADA + 3 * D]
        for b in range(1, nbat):
            gbada = gbada + tot[:, P_DADA + 3 * D * b:P_DADA + 3 * D * (b + 1)]
        gbada_ref[...] = gbada

    p_len = P_DADA + nbat * 3 * D
    out_shape = (jax.ShapeDtypeStruct((N_DEV, 1, p_len), F32), jax.ShapeDtypeStruct((1, p_len), F32),
                 jax.ShapeDtypeStruct((1, 3 * D), F32))
    operands = (gb_rest, *gb_qkv, svec, dgate, dss)
    return _pcall(body, grid=(1,), out_shape=out_shape, in_specs=[_whole(a) for a in operands],
                  out_specs=(VMEM, _whole(out_shape[1]), _whole(out_shape[2])),
                  scratch_shapes=[pltpu.SemaphoreType.DMA((7,)), pltpu.SemaphoreType.DMA((7,))], name="small_reduce",
                  compiler_params=_params(16))(*operands)


PIECE = 64
N_CHUNK = 4
ARRIVAL_RANK = (0, 1, 3, 5, 2, 4, 6, 7)
SLOT_MASK = (1, 4, 2, 6, 5, 3, 7)


def _arrival_tables(shard_rows):
    import numpy as np
    crow = shard_rows // N_CHUNK
    table = np.zeros((N_DEV, N_SLAB + 7 * N_CHUNK), np.int32)
    lo = [(SLAB * j) // crow for j in range(N_SLAB)]
    hi = [(SLAB * j + SLAB - 1) // crow for j in range(N_SLAB)]
    for k in range(N_DEV):
        def rank(chunk):
            shard_rank = ARRIVAL_RANK[(chunk // N_CHUNK) ^ k]
            return shard_rank if shard_rank < 2 else 2 + 8 * (chunk % N_CHUNK) + shard_rank
        order = sorted(range(N_SLAB), key=lambda j: (max(rank(lo[j]), rank(hi[j])), j))
        table[k, :N_SLAB] = order
        for slot, mask in enumerate(SLOT_MASK):
            for ch in range(N_CHUNK):
                chunk = (k ^ mask) * N_CHUNK + ch
                table[k, N_SLAB + slot * N_CHUNK + ch] = min(t for t, j in enumerate(order) if lo[j] <= chunk <= hi[j])
    return table


def _project_gather(shard, x, ada, b_in3, others, xt=512):
    t = x.shape[0]
    n_o = len(others)
    srows = shard.shape[0]
    crow = srows // N_CHUNK
    shards = [shard] + list(others)
    table = jnp.asarray(_arrival_tables(srows))
    seq_tiles = (t // ada.shape[0]) // xt

    def body(tbl_ref, *refs):
        srcs = [refs[0]] + list(refs[4:4 + n_o])
        x_ref, ada_ref, b_ref = refs[1], refs[2], refs[3]
        outs = [refs[4 + n_o]] + list(refs[8 + n_o:8 + 2 * n_o])
        qkv_ref, rest_ref, h_out = refs[5 + n_o], refs[6 + n_o], refs[7 + n_o]
        (wtile, obf, of32, h_ref, xbuf, send_sems, recv_sems, local_sems, tile_sems, obf_sems, of32_sems, x_sems,
         h_sems) = refs[8 + 2 * n_o:]
        w_full = outs[0]
        x, y, c = _my_position()
        k = _flat(x, y, c)
        me, sibling = (x, y, c), (x, y, 1 - c)
        chips = [(1 - x, y), (x, 1 - y), (1 - x, 1 - y)]

        def rows(a, px, py, pc, ch):
            r = shards[a].shape[0]
            if ch is None:
                return outs[a].at[pl.ds(pl.multiple_of(_flat(px, py, pc) * r, r), r), :]
            return outs[a].at[pl.ds(pl.multiple_of(_flat(px, py, pc) * r + ch * crow, crow), crow), :]

        def copy(a, slot, block, to, ch=None, src=None):
            sem = slot * N_CHUNK + ch if a == 0 else 7 * (N_CHUNK - 1 + a) + slot
            if src is not None and ch is not None:
                src = src.at[pl.ds(ch * crow, crow), :]
            return pltpu.make_async_remote_copy(
                src_ref=rows(a, *block, ch) if src is None else src, dst_ref=rows(a, *block, ch),
                send_sem=send_sems.at[sem], recv_sem=recv_sems.at[sem], device_id=to, device_id_type=MESH)

        mine = [pltpu.make_async_copy(srcs[a], rows(a, *me, None), local_sems.at[a]) for a in range(1 + n_o)]
        first = []
        for ch in range(N_CHUNK):
            first.append(copy(0, 0, me, sibling, ch, src=srcs[0]))
            first += [copy(0, 1 + j, me, (*chip, c), ch, src=srcs[0]) for j, chip in enumerate(chips)]
        for a in range(1, 1 + n_o):
            first.append(copy(a, 0, me, sibling, src=srcs[a]))
            first += [copy(a, 1 + j, me, (*chip, c), src=srcs[a]) for j, chip in enumerate(chips)]
        for cp in mine + first:
            cp.start()

        def arrive(a, slot, ch=None):
            if slot == 0:
                copy(a, 0, sibling, me, ch).wait_recv()
            elif slot < 4:
                copy(a, slot, (*chips[slot - 1], c), me, ch).wait_recv()
                copy(a, slot + 3, (*chips[slot - 1], c), sibling, ch).start()
            else:
                copy(a, slot, (*chips[slot - 4], 1 - c), me, ch).wait_recv()

        def arrive_for(step):
            for slot in range(7):
                for ch in range(N_CHUNK):
                    @pl.when(tbl_ref[k, N_SLAB + slot * N_CHUNK + ch] == step)
                    def _():
                        arrive(0, slot, ch)

        def fetch(step, buf):
            slab = tbl_ref[k, step]
            for p in range(SLAB // PIECE):
                g0 = slab * SLAB + PIECE * p
                own = (g0 >= k * srows) & (g0 < (k + 1) * srows)
                dst = wtile.at[buf, pl.ds(PIECE * p, PIECE), :]

                @pl.when(own)
                def _():
                    pltpu.make_async_copy(srcs[0].at[pl.ds(pl.multiple_of(g0 - k * srows, PIECE), PIECE), :], dst, tile_sems.at[buf]).start()

                @pl.when(jnp.logical_not(own))
                def _():
                    pltpu.make_async_copy(w_full.at[pl.ds(pl.multiple_of(g0, PIECE), PIECE), :], dst, tile_sems.at[buf]).start()

        def wait_tile(buf):
            pltpu.make_async_copy(w_full.at[pl.ds(0, SLAB), :], wtile.at[buf], tile_sems.at[buf]).wait()

        def put(buf_ref, sems, dst_ref, count, value):
            b = count % 2

            @pl.when(count >= 2)
            def _():
                pltpu.make_async_copy(buf_ref.at[b], dst_ref, sems.at[b]).wait()

            buf_ref[b] = value
            pltpu.make_async_copy(buf_ref.at[b], dst_ref, sems.at[b]).start()

        def drain(buf_ref, sems, dst_ref, count):
            for back in (1, 2):
                @pl.when(count >= back)
                def _():
                    pltpu.make_async_copy(buf_ref.at[(count - back) % 2], dst_ref, sems.at[(count - back) % 2]).wait()

        def x_copy(i):
            return pltpu.make_async_copy(x_ref.at[pl.ds(xt * i, xt), :], xbuf.at[i % 2], x_sems.at[i % 2])

        def h_copy(i):
            return pltpu.make_async_copy(h_ref.at[pl.ds(xt * i, xt), :], h_out.at[pl.ds(xt * i, xt), :], h_sems.at[i % 2])

        x_copy(0).start()
        for i in range(t // xt):
            if i + 1 < t // xt:
                x_copy(i + 1).start()
            x_copy(i).wait()
            b = i // seq_tiles
            h_ref[pl.ds(xt * i, xt), :] = (xbuf[i % 2] * (1.0 + ada_ref[b, 1:2, :]) + ada_ref[b, 0:1, :]).astype(BF16)
            if i >= 2:
                h_copy(i - 2).wait()
            h_copy(i).start()
        for i in range(max(t // xt - 2, 0), t // xt):
            h_copy(i).wait()

        arrive_for(0)
        fetch(0, 0)

        def step(s, carry):
            n_bf, n_f32 = carry
            buf = s % 2

            @pl.when(s + 1 < N_SLAB)
            def _():
                arrive_for(s + 1)
                fetch(s + 1, 1 - buf)

            wait_tile(buf)
            slab = tbl_ref[k, s]
            v = _nt(h_ref[...], wtile[buf]) + b_ref[slab]
            is_qkv = slab < N_QKV

            @pl.when(is_qkv)
            def _():
                put(obf, obf_sems, qkv_ref.at[jnp.minimum(slab, N_QKV - 1)], n_bf, v.astype(BF16))

            @pl.when(jnp.logical_not(is_qkv))
            def _():
                put(of32, of32_sems, rest_ref.at[jnp.maximum(slab - N_QKV, 0)], n_f32, v)

            return n_bf + is_qkv.astype(jnp.int32), n_f32 + 1 - is_qkv.astype(jnp.int32)

        n_bf, n_f32 = lax.fori_loop(0, N_SLAB, step, (jnp.int32(0), jnp.int32(0)))
        drain(obf, obf_sems, qkv_ref.at[0], n_bf)
        drain(of32, of32_sems, rest_ref.at[0], n_f32)

        for slots in ((1, 2, 3), (0, 4, 5, 6)):
            for a in range(1, 1 + n_o):
                for slot in slots:
                    arrive(a, slot)
        for cp in first:
            cp.wait_send()
        for j, chip in enumerate(chips):
            for ch in range(N_CHUNK):
                copy(0, 4 + j, (*chip, c), sibling, ch).wait_send()
            for a in range(1, 1 + n_o):
                copy(a, 4 + j, (*chip, c), sibling).wait_send()
        for cp in mine:
            cp.wait()

    out_shape = ((jax.ShapeDtypeStruct((N_DEV * srows, D), BF16), jax.ShapeDtypeStruct((N_QKV, t, SLAB), BF16),
                  jax.ShapeDtypeStruct((N_REST, t, SLAB), F32), jax.ShapeDtypeStruct((t, D), BF16))
                 + tuple(jax.ShapeDtypeStruct((N_DEV * o.shape[0], o.shape[1]), o.dtype) for o in others))
    n_all = 1 + n_o
    n_sems = 7 * (N_CHUNK + n_o)
    pair = pltpu.SemaphoreType.DMA((2,))
    grid_spec = pltpu.PrefetchScalarGridSpec(
        num_scalar_prefetch=1, grid=(1,),
        in_specs=[ANY, ANY, pl.BlockSpec(ada.shape, lambda i, tbl: (0, 0, 0)),
                  pl.BlockSpec((N_SLAB, 1, SLAB), lambda i, tbl: (0, 0, 0))] + [ANY] * n_o,
        out_specs=(ANY,) * (4 + n_o),
        scratch_shapes=[pltpu.VMEM((2, SLAB, D), BF16), pltpu.VMEM((2, t, SLAB), BF16), pltpu.VMEM((2, t, SLAB), F32),
                        pltpu.VMEM((t, D), BF16), pltpu.VMEM((2, xt, D), F32),
                        pltpu.SemaphoreType.DMA((n_sems,)), pltpu.SemaphoreType.DMA((n_sems,)),
                        pltpu.SemaphoreType.DMA((n_all,)), pair, pair, pair, pair, pair])
    res = _pcall(body, grid_spec=grid_spec, out_shape=out_shape, name="project_gather",
                 compiler_params=_params(48, ("arbitrary",)))(table, shard, x, ada, b_in3, *others)
    return res[0], res[1], res[2], res[3], list(res[4:])


def _bias_tables(g):
    window, dil = GROUPS[g]
    span = window // dil
    qi = jnp.arange(BLK)[:, None]
    kj = jnp.arange(2 * BLK)[None, :]
    delta = qi + BLK - kj
    valid = (delta >= 0) & (delta <= span)
    heads = jnp.arange(4, dtype=F32) + 4.0 * g
    slopes = 2.0 ** (-8.0 * (heads + 1.0) / 12.0)
    bias = -slopes[:, None, None] * (delta * dil).astype(F32)[None]
    return jnp.where(valid[None], bias, -1e30).reshape(4 * BLK, 2 * BLK)


def _head_masks(shape):
    lane = lax.broadcasted_iota(jnp.int32, shape, 1)
    return [(lane >= 64 * h) & (lane < 64 * (h + 1)) for h in range(4)]


def _stack_heads(v, masks):
    return jnp.concatenate([jnp.where(masks[h], v, jnp.zeros_like(v)) for h in range(4)], axis=0)


def _unstack_heads(v4, masks):
    out = jnp.where(masks[0], v4[0:BLK], 0.0)
    for h in range(1, 4):
        out = jnp.where(masks[h], v4[BLK * h:BLK * (h + 1)], out)
    return out


def _by_residue(v, n, dil):
    return jnp.swapaxes(v.reshape(n, dil, 128), 0, 1).reshape(n * dil, 128) if dil > 1 else v


def _by_token(v, n, dil):
    return jnp.swapaxes(v.reshape(dil, n, 128), 0, 1).reshape(n * dil, 128) if dil > 1 else v


def _regroup(load_half, dst_ref, n, dil):
    for hlf in range(2):
        dst_ref[:, pl.ds(128 * hlf, 128)] = _by_residue(load_half(hlf), n, dil).astype(dst_ref.dtype)


def _store_block(res_ref, r, i, val, n):
    for hlf in range(2):
        res_ref[hlf, pl.ds(pl.multiple_of(r * n + i * BLK, BLK), BLK), :] = val[:, 128 * hlf:128 * (hlf + 1)]


def _for_blocks(block, dil, nblk, ways=WAYS):
    ways = min(ways, max(dil, nblk))
    if dil == 1:
        for i in range(ways):
            block(0, i, i == 0)

        def step(k, carry):
            for j in range(ways):
                block(0, ways * k + j, False)
            return carry

        lax.fori_loop(1, nblk // ways, step, 0)
    else:
        ways = min(ways, dil)

        def residues(k, carry):
            for j in range(ways):
                block(ways * k + j, 0, True)
            if nblk > 1:
                def loop(i, c):
                    for j in range(ways):
                        block(ways * k + j, i, False)
                    return c
                lax.fori_loop(1, nblk, loop, 0)
            return carry

        lax.fori_loop(0, dil // ways, residues, 0)


def _attn_forward(qkv, nbat):
    t = qkv.shape[1]
    seq = t // nbat
    n_grp = len(GROUPS)

    def body(qkv_ref, b0_ref, b1_ref, b2_ref, ol_ref, qs_ref, ks_ref, vs_ref, *nat):
        masks = _head_masks((BLK, SLAB))
        bias_refs = (b0_ref, b1_ref, b2_ref)
        for g, (_, dil) in enumerate(GROUPS):
            n = seq // dil
            bias_ref, nat_o, nat_l = bias_refs[g], nat[2 * g], nat[2 * g + 1]
            if dil > 1:
                qd, kd, vd = qs_ref, ks_ref, vs_ref
                for which, dst in enumerate((qd, kd, vd)):
                    _regroup(lambda hlf, which=which, g=g: qkv_ref[3 * which + g, :, pl.ds(128 * hlf, 128)].astype(F32), dst, n, dil)
            else:
                qd, kd, vd = qkv_ref.at[g], qkv_ref.at[3 + g], qkv_ref.at[6 + g]

            def block(r, i, first, n=n, dil=dil, qd=qd, kd=kd, vd=vd, bias_ref=bias_ref, nat_o=nat_o, nat_l=nat_l):
                base = r * n
                qs = pl.ds(pl.multiple_of(base + i * BLK, BLK), BLK)
                ks = pl.ds(pl.multiple_of(base, BLK), BLK) if first else pl.ds(pl.multiple_of(base + (i - 1) * BLK, BLK), 2 * BLK)
                q, kk, vv = qd[qs, :], kd[ks, :], vd[ks, :]
                bias = bias_ref[:, pl.ds(BLK, BLK)] if first else bias_ref[...]
                s = _nt(_stack_heads(q, masks), kk) * 0.125 + bias
                m = jnp.max(s, axis=1, keepdims=True)
                p = jnp.exp(s - m)
                den = jnp.sum(p, axis=1, keepdims=True)
                out = _unstack_heads(_nn((p * (1.0 / den)).astype(BF16), vv), masks)
                lse = _unstack_heads(jnp.broadcast_to(m + jnp.log(den), (4 * BLK, SLAB)), masks)
                _store_block(nat_o, r, i, out, n)
                _store_block(nat_l, r, i, lse, n)

            _for_blocks(block, dil, n // BLK, ways=2 * WAYS)

        def tokens(k, hlf):
            dil = GROUPS[k // 2][1]
            return _by_token(nat[k][hlf], seq // dil, dil)

        for hlf in range(2):
            l0, l1, l2 = tokens(1, hlf), tokens(3, hlf), tokens(5, hlf)
            mx = jnp.maximum(jnp.maximum(l0, l1), l2)
            e0, e1, e2 = jnp.exp(l0 - mx), jnp.exp(l1 - mx), jnp.exp(l2 - mx)
            den = e0 + e1 + e2
            ol_ref[0, :, pl.ds(128 * hlf, 128)] = (e0 * tokens(0, hlf) + e1 * tokens(2, hlf) + e2 * tokens(4, hlf)) * (1.0 / den)
            ol_ref[1, :, pl.ds(128 * hlf, 128)] = mx + jnp.log(den)

    halves = pltpu.VMEM((2, seq, 128), F32)
    bias_spec = pl.BlockSpec((4 * BLK, 2 * BLK), lambda b: (0, 0))
    return _pcall(
        body, grid=(nbat,), out_shape=jax.ShapeDtypeStruct((2, t, SLAB), F32),
        in_specs=[pl.BlockSpec((N_QKV, seq, SLAB), lambda b: (0, b, 0))] + [bias_spec] * n_grp,
        out_specs=pl.BlockSpec((2, seq, SLAB), lambda b: (0, b, 0)),
        scratch_shapes=[pltpu.VMEM((seq, SLAB), BF16)] * 3 + [halves] * (2 * n_grp),
        name="attn_forward", compiler_params=_params(56, ("parallel",)))(qkv, *[_bias_tables(g) for g in range(n_grp)])


def _attn_backward(qkv, do_attn, ol_tot, dproj, g, nbat):
    t = qkv.shape[1]
    seq = t // nbat
    dil = GROUPS[g][1]
    n = seq // dil
    nblk = n // BLK
    qkv4 = qkv.reshape(3, 3, t, SLAB)
    dp4 = dproj.reshape(DP_SLABS // 3, 3, t, SLAB)

    def body(qkv_ref, do_ref, ol_ref, bias_ref, dp_in, dp_ref, gb_ref, dk_acc, dv_acc, *scratch):
        del dp_in
        masks = _head_masks((BLK, SLAB))

        @pl.when(pl.program_id(0) == 0)
        def _():
            gb_ref[...] = jnp.zeros_like(gb_ref)

        dk_acc[...] = jnp.zeros_like(dk_acc)
        dv_acc[...] = jnp.zeros_like(dv_acc)
        if dil > 1:
            qd, kd, vd, dod, prodd, lsed, dq_res = scratch
            lanes = lambda hlf: pl.ds(128 * hlf, 128)
            for which, dst in enumerate((qd, kd, vd)):
                _regroup(lambda hlf, which=which: qkv_ref[which, 0, :, lanes(hlf)].astype(F32), dst, n, dil)
            _regroup(lambda hlf: do_ref[:, lanes(hlf)].astype(F32), dod, n, dil)
            _regroup(lambda hlf: do_ref[:, lanes(hlf)].astype(F32) * ol_ref[0, :, lanes(hlf)], prodd, n, dil)
            _regroup(lambda hlf: ol_ref[1, :, lanes(hlf)], lsed, n, dil)
        else:
            qd, kd, vd = qkv_ref.at[0, 0], qkv_ref.at[1, 0], qkv_ref.at[2, 0]

        def block(r, i, first):
            base = r * n
            qs = pl.ds(pl.multiple_of(base + i * BLK, BLK), BLK)
            ks = pl.ds(pl.multiple_of(base, BLK), BLK) if first else pl.ds(pl.multiple_of(base + (i - 1) * BLK, BLK), 2 * BLK)
            q, kk, vv = qd[qs, :], kd[ks, :], vd[ks, :]
            if dil > 1:
                do, prod, lse = dod[qs, :], prodd[qs, :], lsed[qs, :]
            else:
                do = do_ref[qs, :]
                prod = do.astype(F32) * ol_ref[0, qs, :]
                lse = ol_ref[1, qs, :]
            q4, do4 = _stack_heads(q, masks), _stack_heads(do, masks)
            bias = bias_ref[:, pl.ds(BLK, BLK)] if first else bias_ref[...]
            lse4 = jnp.concatenate([lse[:, 64 * h:64 * h + 1] for h in range(4)], axis=0)
            delta4 = jnp.concatenate([jnp.sum(jnp.where(masks[h], prod, 0.0), axis=1, keepdims=True) for h in range(4)], axis=0)
            p = jnp.exp(_nt(q4, kk) * 0.125 + bias - lse4)
            ds = (p * (_nt(do4, vv) - delta4)).astype(BF16)
            dv_acc[ks, :] += _tn(p.astype(BF16), do4)
            dk_acc[ks, :] += _tn(ds, q4) * 0.125
            dq = _unstack_heads(_nn(ds, kk), masks) * 0.125
            if dil > 1:
                _store_block(dq_res, r, i, dq, n)
            else:
                dp_ref[0, 0, qs, :] = dq.astype(BF16)
            gb_ref[0] += _part8(dq)

        _for_blocks(block, dil, nblk)
        gb_ref[1] += _part8(dk_acc[...])
        gb_ref[2] += _part8(dv_acc[...])
        if dil > 1:
            for hlf in range(2):
                half = pl.ds(128 * hlf, 128)
                dp_ref[0, 0, :, half] = _by_token(dq_res[hlf], n, dil).astype(BF16)
                dp_ref[1, 0, :, half] = _by_token(dk_acc[:, half], n, dil).astype(BF16)
                dp_ref[2, 0, :, half] = _by_token(dv_acc[:, half], n, dil).astype(BF16)
        else:
            dp_ref[1, 0] = dk_acc[...].astype(BF16)
            dp_ref[2, 0] = dv_acc[...].astype(BF16)

    scratch = [pltpu.VMEM((seq, SLAB), F32)] * 2
    if dil > 1:
        scratch += [pltpu.VMEM((seq, SLAB), BF16)] * 4 + [pltpu.VMEM((seq, SLAB), F32)] * 2 + [pltpu.VMEM((2, seq, 128), F32)]
    dp, gb = _pcall(
        body, grid=(nbat,),
        out_shape=(jax.ShapeDtypeStruct(dp4.shape, BF16), jax.ShapeDtypeStruct((3, 8, SLAB), F32)),
        in_specs=[pl.BlockSpec((3, 1, seq, SLAB), lambda b: (0, g, b, 0)),
                  pl.BlockSpec((seq, SLAB), lambda b: (b, 0)),
                  pl.BlockSpec((2, seq, SLAB), lambda b: (0, b, 0)),
                  pl.BlockSpec((4 * BLK, 2 * BLK), lambda b: (0, 0)), ANY],
        out_specs=(pl.BlockSpec((3, 1, seq, SLAB), lambda b: (DP_SLABS // 9 - 1, g, b, 0)),
                   pl.BlockSpec((3, 8, SLAB), lambda b: (0, 0, 0))),
        scratch_shapes=scratch, input_output_aliases={4: 0}, name=f"attn_backward_{g}",
        compiler_params=_params(48, ("arbitrary",)))(qkv4, do_attn, ol_tot, _bias_tables(g), dp4)
    return dp.reshape(DP_SLABS, t, SLAB), gb


def _mid(rest, ol_tot, x, tgt, ada, cw, b_out, ln_g, ln_b, w_pa_t, w_pb, w_out, tm=256):
    t = x.shape[0]
    nbat = ada.shape[0]
    nt = t // tm
    tps = nt // nbat

    def body(rest_ref, halo_ref, ol_ref, x_ref, t_ref, ada_ref, cw_ref, bout_ref, lng_ref, lnb_ref,
             wpat_ref, wpb_ref, wout_ref,
             dp_ref, gx0_ref, doa_ref, mg_ref, dof_ref, bbs_ref, dyc_ref, a_ref, dya_ref,
             gbr_ref, sv_ref, dgate_ref, carry_ref, keep_ref):
        i = pl.program_id(0)
        ti = nt - 1 - i
        pos = ti % tps

        @pl.when(i == 0)
        def _():
            gbr_ref[...] = jnp.zeros_like(gbr_ref)
            sv_ref[...] = jnp.zeros_like(sv_ref)

        @pl.when(pos == tps - 1)
        def _():
            dgate_ref[...] = jnp.zeros_like(dgate_ref)
            carry_ref[...] = jnp.zeros_like(carry_ref)

        row = lax.broadcasted_iota(jnp.int32, (tm, SLAB), 0)
        halo_on = (pos > 0).astype(F32)

        def cols(s):
            return pl.ds(SLAB * s, SLAB)

        o_attn = ol_ref[0]
        z_a = rest_ref[R_ZA]
        sg_za = _sigmoid(z_a)
        a_ref[...] = (o_attn * z_a * sg_za).astype(BF16)
        y_attn = _nt(a_ref[...], wpat_ref[...])

        for s in range(4):
            u = rest_ref[R_GC + s] * rest_ref[R_UX + s]
            hu = halo_ref[R_GC + s] * halo_ref[R_UX + s] * halo_on
            u1 = jnp.where(row == 0, hu[7:8], pltpu.roll(u, 1, 0))
            u2 = jnp.where(row == 0, hu[6:7], jnp.where(row == 1, hu[7:8], pltpu.roll(u, 2, 0)))
            conv = cw_ref[0:1, cols(s)] * u2 + cw_ref[1:2, cols(s)] * u1 + cw_ref[2:3, cols(s)] * u
            zc = rest_ref[R_ZC + s]
            sg = _sigmoid(zc)
            keep_ref[2, :, cols(s)], keep_ref[3, :, cols(s)], keep_ref[4, :, cols(s)], keep_ref[5, :, cols(s)] = u1, u2, conv, sg
            bbs_ref[:, cols(s)] = (rest_ref[R_GB + s] * conv * (zc * sg)).astype(BF16)
        y_conv = _nn(bbs_ref[...], wpb_ref[...])

        for s in range(4):
            s_a, s_b = _sigmoid(rest_ref[R_GA + s]), _sigmoid(rest_ref[R_GBM + s])
            keep_ref[0, :, cols(s)], keep_ref[1, :, cols(s)] = s_a, s_b
            mg_ref[:, cols(s)] = (s_a * y_attn[:, SLAB * s:SLAB * (s + 1)] + s_b * y_conv[:, SLAB * s:SLAB * (s + 1)]).astype(BF16)
        o = _nn(mg_ref[...], wout_ref[...]) + bout_ref[...]
        gate = ada_ref[0, 2:3, :]
        r = ALPHA * x_ref[...] + gate * o
        mu = jnp.mean(r, axis=1, keepdims=True)
        rc = r - mu
        rstd = lax.rsqrt(jnp.mean(rc * rc, axis=1, keepdims=True) + LN_EPS)
        xhat = rc * rstd
        err = xhat * lng_ref[...] + lnb_ref[...] - t_ref[...]
        sv_ref[6] += _part8(err * err)
        dy = err * (1.0 / D)
        sv_ref[0] += _part8(dy * xhat)
        sv_ref[1] += _part8(dy)
        dxh = dy * lng_ref[...]
        dr = rstd * (dxh - jnp.mean(dxh, axis=1, keepdims=True) - xhat * jnp.mean(dxh * xhat, axis=1, keepdims=True))
        gx0_ref[...] = ALPHA * dr
        dgate_ref[0] += _part8(dr * o)
        do_ = dr * gate
        sv_ref[2] += _part8(do_)
        dof_ref[...] = do_.astype(BF16)
        dmerged = _nt(dof_ref[...], wout_ref[...])
        for s in range(4):
            s_a, s_b = keep_ref[0, :, cols(s)], keep_ref[1, :, cols(s)]
            dm = dmerged[:, SLAB * s:SLAB * (s + 1)]
            ya, yc = y_attn[:, SLAB * s:SLAB * (s + 1)], y_conv[:, SLAB * s:SLAB * (s + 1)]
            dya_ref[:, cols(s)] = (dm * s_a).astype(BF16)
            dyc_ref[:, cols(s)] = (dm * s_b).astype(BF16)
            dga = dm * ya * s_a * (1.0 - s_a)
            dgb = dm * yc * s_b * (1.0 - s_b)
            dp_ref[R_GA + s] = dga.astype(BF16)
            dp_ref[R_GBM + s] = dgb.astype(BF16)
            gbr_ref[R_GA + s] += _part8(dga)
            gbr_ref[R_GBM + s] += _part8(dgb)

        da = _nn(dya_ref[...], wpat_ref[...])
        doa_ref[...] = (da * z_a * sg_za).astype(BF16)
        dza = da * o_attn * (sg_za * (1.0 + z_a * (1.0 - sg_za)))
        dp_ref[R_ZA] = dza.astype(BF16)
        gbr_ref[R_ZA] += _part8(dza)

        dbb = _nt(dyc_ref[...], wpb_ref[...])
        for s in range(4):
            ux, gc, zc = rest_ref[R_UX + s], rest_ref[R_GC + s], rest_ref[R_ZC + s]
            u = gc * ux
            u1, u2, conv, sg = keep_ref[2, :, cols(s)], keep_ref[3, :, cols(s)], keep_ref[4, :, cols(s)], keep_ref[5, :, cols(s)]
            gb = rest_ref[R_GB + s]
            d_b = dbb[:, SLAB * s:SLAB * (s + 1)]
            szc = zc * sg
            dgb_ = d_b * conv * szc
            dconv = d_b * gb * szc
            dzc = d_b * gb * conv * (sg * (1.0 + zc * (1.0 - sg)))
            sv_ref[3, :, cols(s)] += _part8(dconv * u2)
            sv_ref[4, :, cols(s)] += _part8(dconv * u1)
            sv_ref[5, :, cols(s)] += _part8(dconv * u)
            nxt = carry_ref[:, cols(s)]
            d1 = jnp.where(row == tm - 1, nxt[0:1], pltpu.roll(dconv, tm - 1, 0))
            d2 = jnp.where(row == tm - 1, nxt[1:2], jnp.where(row == tm - 2, nxt[0:1], pltpu.roll(dconv, tm - 2, 0)))
            carry_ref[:, cols(s)] = dconv[0:8]
            du = cw_ref[2:3, cols(s)] * dconv + cw_ref[1:2, cols(s)] * d1 + cw_ref[0:1, cols(s)] * d2
            dgc, dux = du * ux, du * gc
            for slab, val in ((R_GB + s, dgb_), (R_ZC + s, dzc), (R_GC + s, dgc), (R_UX + s, dux)):
                dp_ref[slab] = val.astype(BF16)
                gbr_ref[slab] += _part8(val)

    def tile(i):
        return nt - 1 - i

    row_blk = lambda i: (tile(i), 0)
    slab_blk = lambda i: (0, tile(i), 0)
    const2 = lambda i: (0, 0)
    const3 = lambda i: (0, 0, 0)
    in_specs = [
        pl.BlockSpec((N_REST, tm, SLAB), slab_blk),
        pl.BlockSpec((N_REST, 8, SLAB), lambda i: (0, jnp.maximum(tile(i) * (tm // 8) - 1, 0), 0)),
        pl.BlockSpec((1, tm, SLAB), slab_blk),
        pl.BlockSpec((tm, D), row_blk), pl.BlockSpec((tm, D), row_blk),
        pl.BlockSpec((1, 3, D), lambda i: (tile(i) // tps, 0, 0)),
        pl.BlockSpec((3, D), const2), pl.BlockSpec((1, D), const2), pl.BlockSpec((1, D), const2), pl.BlockSpec((1, D), const2),
        pl.BlockSpec((D, SLAB), const2), pl.BlockSpec((D, D), const2), pl.BlockSpec((D, D), const2)]
    bf_rows = lambda: jax.ShapeDtypeStruct((t, D), BF16)
    out_shape = (
        jax.ShapeDtypeStruct((DP_SLABS, t, SLAB), BF16), jax.ShapeDtypeStruct((t, D), F32),
        jax.ShapeDtypeStruct((t, SLAB), BF16),
        bf_rows(), bf_rows(), bf_rows(), bf_rows(), jax.ShapeDtypeStruct((t, SLAB), BF16), bf_rows(),
        jax.ShapeDtypeStruct((N_REST, 8, SLAB), F32), jax.ShapeDtypeStruct((7, 8, D), F32),
        jax.ShapeDtypeStruct((nbat, 8, D), F32))
    out_specs = (
        pl.BlockSpec((N_REST, tm, SLAB), slab_blk), pl.BlockSpec((tm, D), row_blk),
        pl.BlockSpec((tm, SLAB), row_blk),
        pl.BlockSpec((tm, D), row_blk), pl.BlockSpec((tm, D), row_blk), pl.BlockSpec((tm, D), row_blk),
        pl.BlockSpec((tm, D), row_blk), pl.BlockSpec((tm, SLAB), row_blk), pl.BlockSpec((tm, D), row_blk),
        pl.BlockSpec((N_REST, 8, SLAB), const3), pl.BlockSpec((7, 8, D), const3),
        pl.BlockSpec((1, 8, D), lambda i: (tile(i) // tps, 0, 0)))
    return _pcall(body, grid=(nt,), out_shape=out_shape, in_specs=in_specs, out_specs=out_specs,
                  scratch_shapes=[pltpu.VMEM((8, D), F32), pltpu.VMEM((6, tm, D), F32)], name="mid",
                  compiler_params=_params(56, ("arbitrary",)))(
        rest, rest, ol_tot, x, tgt, ada, cw, b_out, ln_g, ln_b, w_pa_t, w_pb, w_out)


def _tn_matmul(lhs, rhs, lhs_spec, n_steps, out_rows, out_index, name, after):
    t, n = rhs.shape

    def body(l_ref, r_ref, after_ref, o_ref):
        del after_ref
        o_ref[...] = _tn(l_ref[0] if len(l_ref.shape) == 3 else l_ref[...], r_ref[...])

    return _pcall(body, grid=(n_steps,), out_shape=jax.ShapeDtypeStruct((out_rows, n), F32),
                  in_specs=[lhs_spec, pl.BlockSpec((t, n), lambda j: (0, 0)), ANY],
                  out_specs=pl.BlockSpec((SLAB, n), out_index), name=name,
                  compiler_params=_params(48, ("parallel",)))(lhs, rhs, after)


def _grad_rows_2d(lhs, rhs, name, after, tc=1024):
    t, k = lhs.shape
    n = rhs.shape[1]

    def body(l_ref, r_ref, after_ref, o_ref):
        del after_ref
        part = _tn(l_ref[...], r_ref[...])

        @pl.when(pl.program_id(0) == 0)
        def _():
            o_ref[...] = part

        @pl.when(pl.program_id(0) > 0)
        def _():
            o_ref[...] += part

    return _pcall(body, grid=(t // tc,), out_shape=jax.ShapeDtypeStruct((k, n), F32),
                  in_specs=[pl.BlockSpec((tc, k), lambda i: (i, 0)), pl.BlockSpec((tc, n), lambda i: (i, 0)), ANY],
                  out_specs=pl.BlockSpec((k, n), lambda i: (0, 0)), name=name,
                  compiler_params=_params(32, ("arbitrary",)))(lhs, rhs, after)


def _w_row_block(j):
    return (j + N_QKV) % N_SLAB


def _dp_slab(j):
    return jnp.where(j < N_REST, j, j + 2)


def _grad_w_in_t(dproj, h):
    t = h.shape[0]
    return _tn_matmul(dproj, h, pl.BlockSpec((1, t, SLAB), lambda j: (_dp_slab(j), 0, 0)), N_SLAB, D_IN,
                      lambda j: (_w_row_block(j), 0), "grad_w_in", h)


def _grad_h(dproj, w_in_t, gx0, x, ada, after, tm=512):
    t = x.shape[0]
    nbat = ada.shape[0]
    tps = (t // nbat) // tm

    def body(dp_ref, w_ref, gx0_ref, x_ref, ada_ref, after_ref, gx_ref, dss_ref):
        del after_ref
        i = pl.program_id(0)
        dh = None
        for j in range(N_SLAB):
            slab = j if j < N_REST else j + 2
            part = _nn(dp_ref[slab], w_ref[pl.ds(SLAB * ((j + N_QKV) % N_SLAB), SLAB), :])
            dh = part if dh is None else dh + part
        gx_ref[...] = gx0_ref[...] + dh * (1.0 + ada_ref[0, 1:2, :])

        @pl.when((i % tps) == 0)
        def _():
            dss_ref[...] = jnp.zeros_like(dss_ref)

        dss_ref[0, 0] += _part8(dh)
        dss_ref[0, 1] += _part8(dh * x_ref[...])

    return _pcall(
        body, grid=(t // tm,),
        out_shape=(jax.ShapeDtypeStruct((t, D), F32), jax.ShapeDtypeStruct((nbat, 2, 8, D), F32)),
        in_specs=[pl.BlockSpec((DP_SLABS, tm, SLAB), lambda i: (0, i, 0)),
                  pl.BlockSpec((D_IN, D), lambda i: (0, 0), pipeline_mode=pl.Buffered(1)),
                  pl.BlockSpec((tm, D), lambda i: (i, 0)), pl.BlockSpec((tm, D), lambda i: (i, 0)),
                  pl.BlockSpec((1, 3, D), lambda i: (i // tps, 0, 0)), ANY],
        out_specs=(pl.BlockSpec((tm, D), lambda i: (i, 0)),
                   pl.BlockSpec((1, 2, 8, D), lambda i: (i // tps, 0, 0, 0))),
        name="grad_h", compiler_params=_params(60, ("arbitrary",)))(dproj, w_in_t, gx0, x, ada, after)


def _chip(m):
    x, y, _ = _my_position()
    return (x ^ ((m >> 1) & 1), y ^ (m & 1))


def _exchange_siblings(grads, after, name):
    n = len(grads)

    def body(*refs):
        copies = _sibling_copies(refs[:n], refs[n + 1:2 * n + 1], refs[2 * n + 1], refs[2 * n + 2])
        for cp in copies:
            cp.start()
        for cp in copies:
            cp.wait()

    return _pcall(body, out_shape=tuple(_sibling_zones(grads)), in_specs=[ANY] * (n + 1), out_specs=(ANY,) * n,
                  name=name, scratch_shapes=[pltpu.SemaphoreType.DMA((4 * n,))] * 2)(*grads, after)


def _sibling_zones(grads):
    return [jax.ShapeDtypeStruct((4, g.shape[0] // N_DEV, g.shape[1]), g.dtype) for g in grads]


def _sibling_copies(srcs, lands, send_sems, recv_sems):
    x, y, c = _my_position()
    copies = []
    for a, (src, land) in enumerate(zip(srcs, lands)):
        rows = land.shape[1]
        for m in range(4):
            dev = _flat(*_chip(m), 1 - c)
            copies.append(pltpu.make_async_remote_copy(
                src_ref=src.at[pl.ds(pl.multiple_of(dev * rows, 8), rows), :], dst_ref=land.at[m],
                send_sem=send_sems.at[4 * a + m], recv_sem=recv_sems.at[4 * a + m], device_id=(x, y, 1 - c),
                device_id_type=MESH))
    return copies


def _chip_copies(srcs, lands, send_sems, recv_sems):
    _, _, c = _my_position()
    return [pltpu.make_async_remote_copy(
        src_ref=srcs[a].at[m - 1], dst_ref=lands[a].at[m - 1], send_sem=send_sems.at[3 * a + m - 1],
        recv_sem=recv_sems.at[3 * a + m - 1], device_id=(*_chip(m), c), device_id_type=MESH)
        for a in range(len(srcs)) for m in range(1, 4)]


HBM = pl.BlockSpec(memory_space=pltpu.HBM)
SEM = pl.BlockSpec(memory_space=pltpu.SEMAPHORE)
SPLIT_COPY = pltpu.CompilerParams(has_side_effects=pltpu.SideEffectType.DATAFLOW_SIDE_EFFECTING)


def _start_copies(make_copies, n_sems, srcs, zones, name):
    n = len(srcs)

    def body(*refs):
        for cp in make_copies(refs[:n], refs[n:2 * n], refs[2 * n], refs[2 * n + 1]):
            cp.start()
        refs[-1][...] = jnp.zeros_like(refs[-1])

    hbm = tuple(pltpu.HBM(b.shape, b.dtype) for b in list(srcs) + list(zones))
    out_shape = (pltpu.SemaphoreType.DMA((n_sems,)), pltpu.SemaphoreType.DMA((n_sems,))) + hbm + (jax.ShapeDtypeStruct((8, 128), F32),)
    operands = [pltpu.with_memory_space_constraint(b, pltpu.HBM) for b in srcs]
    operands += [pltpu.with_memory_space_constraint(lax.empty(z.shape, z.dtype), pltpu.HBM) for z in zones]
    res = _pcall(body, out_shape=out_shape, in_specs=[HBM] * (2 * n), out_specs=(SEM, SEM) + (HBM,) * (2 * n) + (VMEM,),
                 input_output_aliases={i: 2 + i for i in range(2 * n)}, name=name, compiler_params=SPLIT_COPY)(*operands)
    return (res[0], res[1], res[2:2 + n], res[2 + n:2 + 2 * n]), res[-1]


def _wait_copies(make_copies, flight, after, name):
    send_sems, recv_sems, srcs, zones = flight
    n = len(srcs)

    def body(*refs):
        for cp in make_copies(refs[:n], refs[n:2 * n], refs[2 * n], refs[2 * n + 1]):
            cp.wait_send()
            cp.wait_recv()

    hbm = tuple(pltpu.HBM(b.shape, b.dtype) for b in list(srcs) + list(zones))
    res = _pcall(body, out_shape=hbm, in_specs=[HBM] * (2 * n) + [SEM, SEM, ANY], out_specs=(HBM,) * (2 * n),
                 input_output_aliases={i: i for i in range(2 * n)}, name=name, compiler_params=SPLIT_COPY)(
        *srcs, *zones, send_sems, recv_sems, after)
    return res[:n], res[n:]


def _pair_sums(devs, grads, lands, n_steps, name):
    n = len(grads)
    rows = [l.shape[1] for l in lands]
    rbs = [r // n_steps for r in rows]

    def body(devs_ref, *refs):
        del devs_ref
        g_refs, land_refs, outs = refs[:4 * n], refs[4 * n:5 * n], refs[5 * n:]
        for a in range(n):
            outs[2 * a][...] = g_refs[4 * a][...] + land_refs[a][0]
            for m in range(1, 4):
                outs[2 * a + 1][m - 1] = (g_refs[4 * a + m][...] + land_refs[a][m]).astype(BF16)

    def block_of(m, per_dev):
        return lambda i, devs_ref: (devs_ref[m] * per_dev + i, 0)

    in_specs = [pl.BlockSpec((rb, l.shape[2]), block_of(m, n_steps)) for rb, l in zip(rbs, lands) for m in range(4)]
    in_specs += [pl.BlockSpec((4, rb, l.shape[2]), lambda i, devs_ref: (0, i, 0)) for rb, l in zip(rbs, lands)]
    out_shape, out_specs = [], []
    for rb, l in zip(rbs, lands):
        out_shape += [jax.ShapeDtypeStruct(l.shape[1:], F32), jax.ShapeDtypeStruct((3,) + l.shape[1:], BF16)]
        out_specs += [pl.BlockSpec((rb, l.shape[2]), lambda i, devs_ref: (i, 0)),
                      pl.BlockSpec((3, rb, l.shape[2]), lambda i, devs_ref: (0, i, 0))]
    grid_spec = pltpu.PrefetchScalarGridSpec(num_scalar_prefetch=1, grid=(n_steps,), in_specs=in_specs, out_specs=tuple(out_specs))
    res = _pcall(body, grid_spec=grid_spec, out_shape=tuple(out_shape), name=name,
                 compiler_params=_params(48, ("parallel",)))(devs, *[g for g in grads for _ in range(4)], *lands)
    return res[0::2], res[1::2]


def _final_sums(mine, lands, n_steps, name):
    n = len(mine)
    rbs = [o.shape[0] // n_steps for o in mine]

    def body(*refs):
        mine_refs, land_refs, outs = refs[:n], refs[n:2 * n], refs[2 * n:]
        for a in range(n):
            tot = mine_refs[a][...]
            for m in range(3):
                tot = tot + land_refs[a][m].astype(F32)
            outs[a][...] = tot

    in_specs = ([pl.BlockSpec((rb, o.shape[1]), lambda i: (i, 0)) for rb, o in zip(rbs, mine)]
                + [pl.BlockSpec((3, rb, o.shape[1]), lambda i: (0, i, 0)) for rb, o in zip(rbs, mine)])
    out_specs = tuple(pl.BlockSpec((rb, o.shape[1]), lambda i: (i, 0)) for rb, o in zip(rbs, mine))
    out_shape = tuple(jax.ShapeDtypeStruct(o.shape, F32) for o in mine)
    return _pcall(body, grid=(n_steps,), out_shape=out_shape, in_specs=in_specs, out_specs=out_specs, name=name,
                  compiler_params=_params(32, ("parallel",)))(*mine, *lands)


def _reduce_scatter_begin(big, small_after_start):
    c = lax.axis_index("c")
    devs = jnp.stack([_flat(*_chip(m), c) for m in range(4)]).astype(jnp.int32)
    flight, token = _start_copies(_sibling_copies, 4, [big], _sibling_zones([big]), "siblings_start")
    small = small_after_start(token)
    (big,), big_lands = _wait_copies(_sibling_copies, flight, small[-1], "siblings_wait")
    big_mine, big_send = _pair_sums(devs, [big], big_lands, 4, "pair_sums_w_in")
    big_flight, token = _start_copies(_chip_copies, 3, list(big_send), list(big_send), "chips_start_w_in")
    small_lands = _exchange_siblings(small, token, "exchange_siblings_rest")
    small_mine, small_send = _pair_sums(devs, small, small_lands, 1, "pair_sums_rest")
    small_flight, token = _start_copies(_chip_copies, 3 * len(small), list(small_send), list(small_send), "chips_start_rest")
    return (big_flight, small_flight, list(big_mine) + list(small_mine)), token


def _reduce_scatter_end(state, after):
    big_flight, small_flight, mine = state
    _, big_got = _wait_copies(_chip_copies, big_flight, after, "chips_wait_w_in")
    _, small_got = _wait_copies(_chip_copies, small_flight, after, "chips_wait_rest")
    small = _final_sums(mine[1:], small_got, 1, "final_sums_rest")
    return (mine[0], big_got[0]), list(small)


def _adamw(w, g, m, v):
    m_new = B1 * m + (1.0 - B1) * g
    v_new = B2 * v + (1.0 - B2) * (g * g)
    m_hat = m_new / (1.0 - B1 ** STEP)
    v_hat = v_new / (1.0 - B2 ** STEP)
    delta = -LR * (m_hat / (jnp.sqrt(v_hat) + EPS) + WD * w)
    return delta, m_new, v_new


def _final_sum_adam_rows(mine, land, w, m, v, n_steps, name):
    rows, ncol = w.shape
    blk = pl.BlockSpec((rows // n_steps, ncol), lambda i: (i, 0))

    def body(mine_ref, land_ref, w_ref, m_ref, v_ref, g_ref, d_ref, mo_ref, vo_ref):
        g = mine_ref[...]
        for k in range(3):
            g = g + land_ref[k].astype(F32)
        g_ref[...] = g
        d_ref[...], mo_ref[...], vo_ref[...] = _adamw(w_ref[...], g, m_ref[...], v_ref[...])

    shape = jax.ShapeDtypeStruct(w.shape, F32)
    return _pcall(body, grid=(n_steps,), out_shape=(shape,) * 4,
                  in_specs=[blk, pl.BlockSpec((3, rows // n_steps, ncol), lambda i: (0, i, 0)), blk, blk, blk],
                  out_specs=(blk,) * 4, name=name, compiler_params=_params(32, ("parallel",)))(mine, land, w, m, v)


def _adam_transposed(g_t, w, m, v, name):
    n, k = g_t.shape
    rb = min(k, 128)

    def body(gt_ref, w_ref, m_ref, v_ref, g_ref, d_ref, mo_ref, vo_ref):
        for src, skip, dst, size in _column_chunks(n):
            sl = pl.ds(dst, size)
            g = gt_ref[pl.ds(src, 128), :].T[:, skip:]
            delta, m_new, v_new = _adamw(w_ref[:, sl], g, m_ref[:, sl], v_ref[:, sl])
            g_ref[:, sl], d_ref[:, sl], mo_ref[:, sl], vo_ref[:, sl] = g, delta, m_new, v_new

    shape = jax.ShapeDtypeStruct(w.shape, F32)
    rows = pl.BlockSpec((rb, n), lambda i: (i, 0))
    return _pcall(body, grid=(k // rb,), out_shape=(shape,) * 4,
                  in_specs=[pl.BlockSpec((n, rb), lambda i: (0, i)), rows, rows, rows], out_specs=(rows,) * 4, name=name,
                  compiler_params=_params(32, ("parallel",)))(g_t, w, m, v)


def _adam_many(items, name):
    n = len(items)

    def body(*refs):
        ins, outs = refs[:4 * n], refs[4 * n:]
        for a in range(n):
            w_ref, g_ref, m_ref, v_ref = ins[4 * a:4 * a + 4]
            delta, m_new, v_new = _adamw(w_ref[...], g_ref[...], m_ref[...], v_ref[...])
            outs[3 * a][...], outs[3 * a + 1][...], outs[3 * a + 2][...] = delta, m_new, v_new

    out_shape = tuple(jax.ShapeDtypeStruct(it[0].shape, F32) for it in items for _ in range(3))
    flat = [arr for it in items for arr in it]
    res = _pcall(body, grid=(1,), out_shape=out_shape, in_specs=[_whole(a) for a in flat],
                 out_specs=tuple(_whole(o) for o in out_shape), name=name, compiler_params=_params(32))(*flat)
    return [tuple(res[3 * a:3 * a + 3]) for a in range(n)]


def _adam_w_ada(cact_all, dada_mine, w, m, v):
    def body(c_ref, d_ref, w_ref, m_ref, v_ref, g_ref, dl_ref, mo_ref, vo_ref):
        g = _tn(c_ref[...].astype(BF16), d_ref[...].astype(BF16))
        delta, m_new, v_new = _adamw(w_ref[...], g, m_ref[...], v_ref[...])
        g_ref[...], dl_ref[...], mo_ref[...], vo_ref[...] = g, delta, m_new, v_new

    shape = jax.ShapeDtypeStruct(w.shape, F32)
    operands = (cact_all, dada_mine, w, m, v)
    return _pcall(body, grid=(1,), out_shape=(shape,) * 4, in_specs=[_whole(a) for a in operands],
                  out_specs=(_whole(w),) * 4, name="adam_w_ada", compiler_params=_params(32))(*operands)


def kernel(x, c, w_ada, b_ada, w_in, b_in, conv_w, w_proj_attn, w_proj_conv, w_out, b_out, ln_g, ln_b, loss_target, m_w_ada, m_b_ada, m_w_in, m_b_in, m_conv_w, m_w_proj_attn, m_w_proj_conv, m_w_out, m_b_out, m_ln_g, m_ln_b, v_w_ada, v_b_ada, v_w_in, v_b_in, v_conv_w, v_w_proj_attn, v_w_proj_conv, v_w_out, v_b_out, v_ln_g, v_ln_b):
    nbat, seq, _ = x.shape
    t = nbat * seq
    me = _flat(*_my_position())
    x2, tgt2 = x.reshape(t, D), loss_target.reshape(t, D)
    sq = lambda a: a.reshape(a.shape[1:])

    tr = lambda a: a[0].T
    w_in_rows = tr(w_in)
    w_in_t_s = _cast_rows(w_in_rows, 4, "cast_w_in")
    w_pa_t_s, w_pb_s, w_out_s, cact_s, cw_s = _prep(sq(w_proj_attn), sq(w_proj_conv), sq(w_out), c, sq(conv_w))

    ncol = w_ada.shape[2]
    b_ada_mine = lax.dynamic_slice(b_ada, (0, me * ncol), (1, ncol))
    ada_slots, cact_slots, cw_slots = _ada_forward(cact_s, cw_s, sq(w_ada), b_ada_mine)
    cact_all = cact_slots[:, :nbat].reshape(N_DEV * nbat, D)
    cw = cw_slots[:, :3].transpose(1, 0, 2).reshape(3, D)
    ada_all = ada_slots[:, :, :nbat].transpose(1, 2, 0, 3).reshape(N_DEV * nbat, 3, D)
    ada = lax.dynamic_slice(ada_all, (me * nbat, 0, 0), (nbat, 3, D))

    w_in_t, qkv, rest, h, (w_pa_t, w_pb, w_o) = _project_gather(
        w_in_t_s, x2, ada, b_in.reshape(N_SLAB, 1, SLAB), [w_pa_t_s, w_pb_s, w_out_s])
    ol_tot = _attn_forward(qkv, nbat)
    (dproj, gx0, do_attn, merged, do_f, bbs, dyc, a_bf, dya, gb_rest, svec, dgate) = _mid(
        rest, ol_tot, x2, tgt2, ada, cw, b_out, ln_g, ln_b, w_pa_t, w_pb, w_o)

    gb_qkv = []
    for g in range(3):
        dproj, gb = _attn_backward(qkv, do_attn, ol_tot, dproj, g, nbat)
        gb_qkv.append(gb)
    g_w_in_t = _grad_w_in_t(dproj, h)

    def small_grads(token):
        g_w_out = _grad_rows_2d(merged, do_f, "grad_w_out", token)
        g_w_pb = _grad_rows_2d(bbs, dyc, "grad_w_proj_conv", g_w_out)
        g_w_pa_t = _grad_rows_2d(dya, a_bf, "grad_w_proj_attn", g_w_pb)
        return [g_w_out, g_w_pb, g_w_pa_t]

    rs_state, token = _reduce_scatter_begin(g_w_in_t, small_grads)
    grad_x, dss = _grad_h(dproj, w_in_t, gx0, x2, ada, token)

    rows8, tot, g_bada = _small_reduce(gb_rest, gb_qkv, svec, dgate, dss)
    (g_in_mine, g_in_got), (g_out, g_pb, g_pa_t) = _reduce_scatter_end(rs_state, tot)
    loss = tot[0, P_LOSS]
    dada_all = rows8[:, 0, P_DADA:].reshape(N_DEV * nbat, 3 * D)
    dada_mine = lax.dynamic_slice(dada_all, (0, me * ncol), (N_DEV * nbat, ncol))

    g_in_t, d_win_t, nm_win_t, nv_win_t = _final_sum_adam_rows(g_in_mine, g_in_got, w_in_rows, tr(m_w_in), tr(v_w_in), 4, "adam_w_in")
    g_win, d_win, nm_win, nv_win = g_in_t.T, d_win_t.T, nm_win_t.T, nv_win_t.T
    g_wpa, d_wpa, nm_wpa, nv_wpa = _adam_transposed(g_pa_t, sq(w_proj_attn), sq(m_w_proj_attn), sq(v_w_proj_attn), "adam_w_proj_attn")
    g_wada, d_wada, nm_wada, nv_wada = _adam_w_ada(cact_all, dada_mine, sq(w_ada), sq(m_w_ada), sq(v_w_ada))
    g_bin = tot[:, P_BIN:P_BIN + D_IN]
    g_bout = tot[:, P_BOUT:P_BOUT + D]
    g_lng = tot[:, P_LNG:P_LNG + D]
    g_lnb = tot[:, P_LNB:P_LNB + D]
    g_conv = lax.dynamic_slice(tot[:, P_CONV:P_CONV + 3 * D].reshape(3, D), (0, me * cw_s.shape[1]), (3, cw_s.shape[1]))
    upd = _adam_many([
        (sq(w_proj_conv), g_pb, sq(m_w_proj_conv), sq(v_w_proj_conv)),
        (sq(w_out), g_out, sq(m_w_out), sq(v_w_out)),
        (b_ada, g_bada, m_b_ada, v_b_ada), (b_in, g_bin, m_b_in, v_b_in), (sq(conv_w), g_conv, sq(m_conv_w), sq(v_conv_w)),
        (b_out, g_bout, m_b_out, v_b_out), (ln_g, g_lng, m_ln_g, v_ln_g), (ln_b, g_lnb, m_ln_b, v_ln_b)], "adam_rest")
    (d_wpb, nm_wpb, nv_wpb), (d_wout, nm_wout, nv_wout), (d_bada, nm_bada, nv_bada), (d_bin, nm_bin, nv_bin), \
        (d_conv, nm_conv, nv_conv), (d_bout, nm_bout, nv_bout), (d_lng, nm_lng, nv_lng), (d_lnb, nm_lnb, nv_lnb) = upd

    ex = lambda a: a.reshape((1,) + a.shape)
    grads = [ex(g_wada), g_bada, ex(g_win), g_bin, ex(g_conv), ex(g_wpa), ex(g_pb), ex(g_out), g_bout, g_lng, g_lnb]
    deltas = [ex(d_wada), d_bada, ex(d_win), d_bin, ex(d_conv), ex(d_wpa), ex(d_wpb), ex(d_wout), d_bout, d_lng, d_lnb]
    new_m = [ex(nm_wada), nm_bada, ex(nm_win), nm_bin, ex(nm_conv), ex(nm_wpa), ex(nm_wpb), ex(nm_wout), nm_bout, nm_lng, nm_lnb]
    new_v = [ex(nv_wada), nv_bada, ex(nv_win), nv_bin, ex(nv_conv), ex(nv_wpa), ex(nv_wpb), ex(nv_wout), nv_bout, nv_lng, nv_lnb]
    return (loss, grad_x.reshape(x.shape), *grads, *deltas, *new_m, *new_v)
```

```python
import jax
import jax.numpy as jnp
from jax import lax
from jax.experimental import pallas as pl
from jax.experimental.pallas import tpu as pltpu

F32, BF16 = jnp.float32, jnp.bfloat16
MESH = pl.DeviceIdType.MESH
N_DEV = 8
D = 1024
SLAB = 256
N_QKV, N_REST = 9, 25
N_SLAB = N_QKV + N_REST
D_IN = N_SLAB * SLAB
DP_SLABS = 36
BLK = 128
WAYS = 4
GROUPS = ((128, 1), (512, 4), (2048, 16))
ALPHA = 2.0 ** 0.25
LN_EPS = 1e-5
LR, B1, B2, EPS, WD, STEP = 0.001, 0.9, 0.999, 1e-08, 0.01, 10
R_ZA, R_UX, R_GB, R_GC, R_ZC, R_GA, R_GBM = 0, 1, 5, 9, 13, 17, 21
P_BIN, P_BOUT, P_LNG, P_LNB, P_CONV, P_LOSS, P_DADA = 0, 8704, 9728, 10752, 11776, 14848, 14976
MIB = 1024 * 1024


def _pcall(body, *, out_shape, out_specs=None, **kw):
    def pin_out(shape, spec):
        in_hbm = getattr(spec, "block_shape", None) is not None or getattr(spec, "memory_space", None) is pl.ANY
        return pltpu.HBM(shape.shape, shape.dtype) if in_hbm and isinstance(shape, jax.ShapeDtypeStruct) else shape

    n_scalar = 0
    if out_specs is None:
        specs = kw["grid_spec"].out_specs
        n_scalar = kw["grid_spec"].num_scalar_prefetch
    else:
        kw["out_specs"] = specs = out_specs
    if isinstance(out_shape, (tuple, list)):
        out_shape = tuple(pin_out(s, p) for s, p in zip(out_shape, specs))
    else:
        out_shape = pin_out(out_shape, specs)
    call = pl.pallas_call(body, out_shape=out_shape, **kw)

    def run(*operands):
        def pin(o):
            is_data = jnp.issubdtype(o.dtype, jnp.floating) or jnp.issubdtype(o.dtype, jnp.integer)
            return pltpu.with_memory_space_constraint(o, pltpu.HBM) if is_data else o
        return call(*operands[:n_scalar], *[pin(o) for o in operands[n_scalar:]])

    return run

ANY = pl.BlockSpec(memory_space=pl.ANY)
VMEM = pl.BlockSpec(memory_space=pltpu.VMEM)


def _whole(a):
    return pl.BlockSpec(a.shape, lambda i: (0,) * len(a.shape))


def _params(vmem_mib=None, sem=None):
    kw = {}
    if vmem_mib is not None:
        kw["vmem_limit_bytes"] = vmem_mib * MIB
    if sem is not None:
        kw["dimension_semantics"] = sem
    return pltpu.CompilerParams(**kw)


def _nn(a, b):
    return jnp.dot(a, b, preferred_element_type=F32)


def _nt(a, b):
    return lax.dot_general(a, b, (((1,), (1,)), ((), ())), preferred_element_type=F32)


def _tn(a, b):
    return lax.dot_general(a, b, (((0,), (0,)), ((), ())), preferred_element_type=F32)


def _sigmoid(v):
    return 0.5 * jnp.tanh(0.5 * v) + 0.5


def _part8(v):
    return v.reshape(v.shape[0] // 8, 8, v.shape[1]).sum(axis=0)


def _my_position():
    return lax.axis_index("x"), lax.axis_index("y"), lax.axis_index("c")


def _flat(px, py, pc):
    return 4 * px + 2 * py + pc


def _peer(mask):
    x, y, c = _my_position()
    return (x ^ ((mask >> 2) & 1), y ^ ((mask >> 1) & 1), c ^ (mask & 1))


def _column_chunks(n):
    chunks = [(128 * a, 0, 128 * a, 128) for a in range(n // 128)]
    if n % 128:
        chunks.append((n - 128, 128 - n % 128, 128 * (n // 128), n % 128))
    return chunks


def _cast_rows(w, n_steps, name):
    rows, ncol = w.shape
    blk = pl.BlockSpec((rows // n_steps, ncol), lambda i: (i, 0))

    def body(w_ref, o_ref):
        o_ref[...] = w_ref[...].astype(BF16)

    return _pcall(body, grid=(n_steps,), out_shape=jax.ShapeDtypeStruct(w.shape, BF16), in_specs=[blk], out_specs=blk,
                  name=name, compiler_params=_params(16, ("parallel",)))(w)


def _prep(w_pa, w_pb, w_out, c, conv_w):
    def body(wpa_ref, wpb_ref, wout_ref, c_ref, cw_ref, wpat_ref, wpb_o, wout_o, cact_ref, cwp_ref):
        wpat_ref[...] = wpa_ref[...].T.astype(BF16)
        wpb_o[...] = wpb_ref[...].astype(BF16)
        wout_o[...] = wout_ref[...].astype(BF16)
        cv = c_ref[...]
        cact_ref[...] = jnp.zeros_like(cact_ref)
        cact_ref[pl.ds(0, cv.shape[0]), :] = cv * _sigmoid(cv)
        cwp_ref[...] = jnp.zeros_like(cwp_ref)
        cwp_ref[pl.ds(0, 3), :] = cw_ref[...]

    out_shape = (jax.ShapeDtypeStruct((w_pa.shape[1], w_pa.shape[0]), BF16),
                 jax.ShapeDtypeStruct(w_pb.shape, BF16), jax.ShapeDtypeStruct(w_out.shape, BF16),
                 jax.ShapeDtypeStruct((8, D), F32), jax.ShapeDtypeStruct((8, conv_w.shape[1]), F32))
    operands = (w_pa, w_pb, w_out, c, conv_w)
    return _pcall(body, grid=(1,), out_shape=out_shape, in_specs=[_whole(a) for a in operands],
                  out_specs=tuple(_whole(o) for o in out_shape), name="prep", compiler_params=_params(16))(*operands)


def _exchange_slots(out_refs, send_sems, recv_sems, base=0):
    me = _flat(*_my_position())

    def copy(a, mask, slot):
        return pltpu.make_async_remote_copy(
            src_ref=out_refs[a].at[slot], dst_ref=out_refs[a].at[slot], send_sem=send_sems.at[base + 7 * a + mask - 1],
            recv_sem=recv_sems.at[base + 7 * a + mask - 1], device_id=_peer(mask), device_id_type=MESH)

    pairs = [(a, mask) for a in range(len(out_refs)) for mask in range(1, N_DEV)]
    for a, mask in pairs:
        copy(a, mask, me).start()
    for a, mask in pairs:
        copy(a, mask, _flat(*_peer(mask))).wait_recv()
    for a, mask in pairs:
        copy(a, mask, me).wait_send()


def _ada_forward(cact_mine, cw_mine, w_ada, b_ada_mine):
    ncol = w_ada.shape[1]

    def body(c_ref, cw_ref, w_ref, b_ref, out_ref, call_ref, cwall_ref, send_sems, recv_sems):
        me = _flat(*_my_position())
        call_ref[me] = c_ref[...]
        cwall_ref[me] = cw_ref[...]
        _exchange_slots([call_ref, cwall_ref], send_sems, recv_sems)
        c_all = call_ref[...].reshape(N_DEV * 8, D).astype(BF16)
        out_ref[me] = (_nn(c_all, w_ref[...].astype(BF16)) + b_ref[...]).reshape(N_DEV, 8, ncol)
        _exchange_slots([out_ref], send_sems, recv_sems, base=14)

    operands = (cact_mine, cw_mine, w_ada, b_ada_mine)
    out_shape = (jax.ShapeDtypeStruct((N_DEV, N_DEV, 8, ncol), F32), jax.ShapeDtypeStruct((N_DEV, 8, D), F32),
                 jax.ShapeDtypeStruct((N_DEV,) + cw_mine.shape, F32))
    return _pcall(body, grid=(1,), out_shape=out_shape, in_specs=[_whole(a) for a in operands], out_specs=(VMEM,) * 3,
                  scratch_shapes=[pltpu.SemaphoreType.DMA((21,)), pltpu.SemaphoreType.DMA((21,))], name="ada_forward",
                  compiler_params=_params(16))(*operands)


def _small_reduce(gb_rest, gb_qkv, svec, dgate, dss):
    nbat = dgate.shape[0]

    def body(gbr_ref, q0_ref, q1_ref, q2_ref, sv_ref, dg_ref, dss_ref, rows_ref, tot_ref, gbada_ref, send_sems, recv_sems):
        me = _flat(*_my_position())

        def put(off, v):
            rows_ref[me, :, pl.ds(off, v.shape[1])] = v

        def row(v):
            return jnp.sum(v, axis=0, keepdims=True)

        for g, q_ref in enumerate((q0_ref, q1_ref, q2_ref)):
            for which in range(3):
                put(P_BIN + SLAB * (3 * which + g), row(q_ref[which]))
        for s in range(N_REST):
            put(P_BIN + SLAB * (N_QKV + s), row(gbr_ref[s]))
        put(P_LNG, row(sv_ref[0]))
        put(P_LNB, row(sv_ref[1]))
        put(P_BOUT, row(sv_ref[2]))
        for j in range(3):
            put(P_CONV + D * j, row(sv_ref[3 + j]))
        loss = (0.5 / D) * jnp.sum(row(sv_ref[6]), axis=1, keepdims=True)
        put(P_LOSS, jnp.broadcast_to(loss, (1, 128)))
        for b in range(nbat):
            put(P_DADA + 3 * D * b, row(dss_ref[b, 0]))
            put(P_DADA + 3 * D * b + D, row(dss_ref[b, 1]))
            put(P_DADA + 3 * D * b + 2 * D, row(dg_ref[b]))
        _exchange_slots([rows_ref], send_sems, recv_sems)
        tot = rows_ref[0]
        for k in range(1, N_DEV):
            tot = tot + rows_ref[k]
        tot_ref[...] = tot
        gbada = tot[:, P_DADA:P_DADA + 3 * D]
        for b in range(1, nbat):
            gbada = gbada + tot[:, P_DADA + 3 * D * b:P_DADA + 3 * D * (b + 1)]
        gbada_ref[...] = gbada

    p_len = P_DADA + nbat * 3 * D
    out_shape = (jax.ShapeDtypeStruct((N_DEV, 1, p_len), F32), jax.ShapeDtypeStruct((1, p_len), F32),
                 jax.ShapeDtypeStruct((1, 3 * D), F32))
    operands = (gb_rest, *gb_qkv, svec, dgate, dss)
    return _pcall(body, grid=(1,), out_shape=out_shape, in_specs=[_whole(a) for a in operands],
                  out_specs=(VMEM, _whole(out_shape[1]), _whole(out_shape[2])),
                  scratch_shapes=[pltpu.SemaphoreType.DMA((7,)), pltpu.SemaphoreType.DMA((7,))], name="small_reduce",
                  compiler_params=_params(16))(*operands)


PIECE = 64
N_CHUNK = 4
ARRIVAL_RANK = (0, 1, 3, 5, 2, 4, 6, 7)
SLOT_MASK = (1, 4, 2, 6, 5, 3, 7)


def _arrival_tables(shard_rows):
    import numpy as np
    crow = shard_rows // N_CHUNK
    table = np.zeros((N_DEV, N_SLAB + 7 * N_CHUNK), np.int32)
    lo = [(SLAB * j) // crow for j in range(N_SLAB)]
    hi = [(SLAB * j + SLAB - 1) // crow for j in range(N_SLAB)]
    for k in range(N_DEV):
        def rank(chunk):
            shard_rank = ARRIVAL_RANK[(chunk // N_CHUNK) ^ k]
            return shard_rank if shard_rank < 2 else 2 + 8 * (chunk % N_CHUNK) + shard_rank
        order = sorted(range(N_SLAB), key=lambda j: (max(rank(lo[j]), rank(hi[j])), j))
        table[k, :N_SLAB] = order
        for slot, mask in enumerate(SLOT_MASK):
            for ch in range(N_CHUNK):
                chunk = (k ^ mask) * N_CHUNK + ch
                table[k, N_SLAB + slot * N_CHUNK + ch] = min(t for t, j in enumerate(order) if lo[j] <= chunk <= hi[j])
    return table


def _project_gather(shard, x, ada, b_in3, others, xt=512):
    t = x.shape[0]
    n_o = len(others)
    srows = shard.shape[0]
    crow = srows // N_CHUNK
    shards = [shard] + list(others)
    table = jnp.asarray(_arrival_tables(srows))
    seq_tiles = (t // ada.shape[0]) // xt

    def body(tbl_ref, *refs):
        srcs = [refs[0]] + list(refs[4:4 + n_o])
        x_ref, ada_ref, b_ref = refs[1], refs[2], refs[3]
        outs = [refs[4 + n_o]] + list(refs[8 + n_o:8 + 2 * n_o])
        qkv_ref, rest_ref, h_out = refs[5 + n_o], refs[6 + n_o], refs[7 + n_o]
        (wtile, obf, of32, h_ref, xbuf, send_sems, recv_sems, local_sems, tile_sems, obf_sems, of32_sems, x_sems,
         h_sems) = refs[8 + 2 * n_o:]
        w_full = outs[0]
        x, y, c = _my_position()
        k = _flat(x, y, c)
        me, sibling = (x, y, c), (x, y, 1 - c)
        chips = [(1 - x, y), (x, 1 - y), (1 - x, 1 - y)]

        def rows(a, px, py, pc, ch):
            r = shards[a].shape[0]
            if ch is None:
                return outs[a].at[pl.ds(pl.multiple_of(_flat(px, py, pc) * r, r), r), :]
            return outs[a].at[pl.ds(pl.multiple_of(_flat(px, py, pc) * r + ch * crow, crow), crow), :]

        def copy(a, slot, block, to, ch=None, src=None):
            sem = slot * N_CHUNK + ch if a == 0 else 7 * (N_CHUNK - 1 + a) + slot
            if src is not None and ch is not None:
                src = src.at[pl.ds(ch * crow, crow), :]
            return pltpu.make_async_remote_copy(
                src_ref=rows(a, *block, ch) if src is None else src, dst_ref=rows(a, *block, ch),
                send_sem=send_sems.at[sem], recv_sem=recv_sems.at[sem], device_id=to, device_id_type=MESH)

        mine = [pltpu.make_async_copy(srcs[a], rows(a, *me, None), local_sems.at[a]) for a in range(1 + n_o)]
        first = []
        for ch in range(N_CHUNK):
            first.append(copy(0, 0, me, sibling, ch, src=srcs[0]))
            first += [copy(0, 1 + j, me, (*chip, c), ch, src=srcs[0]) for j, chip in enumerate(chips)]
        for a in range(1, 1 + n_o):
            first.append(copy(a, 0, me, sibling, src=srcs[a]))
            first += [copy(a, 1 + j, me, (*chip, c), src=srcs[a]) for j, chip in enumerate(chips)]
        for cp in mine + first:
            cp.start()

        def arrive(a, slot, ch=None):
            if slot == 0:
                copy(a, 0, sibling, me, ch).wait_recv()
            elif slot < 4:
                copy(a, slot, (*chips[slot - 1], c), me, ch).wait_recv()
                copy(a, slot + 3, (*chips[slot - 1], c), sibling, ch).start()
            else:
                copy(a, slot, (*chips[slot - 4], 1 - c), me, ch).wait_recv()

        def arrive_for(step):
            for slot in range(7):
                for ch in range(N_CHUNK):
                    @pl.when(tbl_ref[k, N_SLAB + slot * N_CHUNK + ch] == step)
                    def _():
                        arrive(0, slot, ch)

        def fetch(step, buf):
            slab = tbl_ref[k, step]
            for p in range(SLAB // PIECE):
                g0 = slab * SLAB + PIECE * p
                own = (g0 >= k * srows) & (g0 < (k + 1) * srows)
                dst = wtile.at[buf, pl.ds(PIECE * p, PIECE), :]

                @pl.when(own)
                def _():
                    pltpu.make_async_copy(srcs[0].at[pl.ds(pl.multiple_of(g0 - k * srows, PIECE), PIECE), :], dst, tile_sems.at[buf]).start()

                @pl.when(jnp.logical_not(own))
                def _():
                    pltpu.make_async_copy(w_full.at[pl.ds(pl.multiple_of(g0, PIECE), PIECE), :], dst, tile_sems.at[buf]).start()

        def wait_tile(buf):
            pltpu.make_async_copy(w_full.at[pl.ds(0, SLAB), :], wtile.at[buf], tile_sems.at[buf]).wait()

        def put(buf_ref, sems, dst_ref, count, value):
            b = count % 2

            @pl.when(count >= 2)
            def _():
                pltpu.make_async_copy(buf_ref.at[b], dst_ref, sems.at[b]).wait()

            buf_ref[b] = value
            pltpu.make_async_copy(buf_ref.at[b], dst_ref, sems.at[b]).start()

        def drain(buf_ref, sems, dst_ref, count):
            for back in (1, 2):
                @pl.when(count >= back)
                def _():
                    pltpu.make_async_copy(buf_ref.at[(count - back) % 2], dst_ref, sems.at[(count - back) % 2]).wait()

        def x_copy(i):
            return pltpu.make_async_copy(x_ref.at[pl.ds(xt * i, xt), :], xbuf.at[i % 2], x_sems.at[i % 2])

        def h_copy(i):
            return pltpu.make_async_copy(h_ref.at[pl.ds(xt * i, xt), :], h_out.at[pl.ds(xt * i, xt), :], h_sems.at[i % 2])

        x_copy(0).start()
        for i in range(t // xt):
            if i + 1 < t // xt:
                x_copy(i + 1).start()
            x_copy(i).wait()
            b = i // seq_tiles
            h_ref[pl.ds(xt * i, xt), :] = (xbuf[i % 2] * (1.0 + ada_ref[b, 1:2, :]) + ada_ref[b, 0:1, :]).astype(BF16)
            if i >= 2:
                h_copy(i - 2).wait()
            h_copy(i).start()
        for i in range(max(t // xt - 2, 0), t // xt):
            h_copy(i).wait()

        arrive_for(0)
        fetch(0, 0)

        def step(s, carry):
            n_bf, n_f32 = carry
            buf = s % 2

            @pl.when(s + 1 < N_SLAB)
            def _():
                arrive_for(s + 1)
                fetch(s + 1, 1 - buf)

            wait_tile(buf)
            slab = tbl_ref[k, s]
            v = _nt(h_ref[...], wtile[buf]) + b_ref[slab]
            is_qkv = slab < N_QKV

            @pl.when(is_qkv)
            def _():
                put(obf, obf_sems, qkv_ref.at[jnp.minimum(slab, N_QKV - 1)], n_bf, v.astype(BF16))

            @pl.when(jnp.logical_not(is_qkv))
            def _():
                put(of32, of32_sems, rest_ref.at[jnp.maximum(slab - N_QKV, 0)], n_f32, v)

            return n_bf + is_qkv.astype(jnp.int32), n_f32 + 1 - is_qkv.astype(jnp.int32)

        n_bf, n_f32 = lax.fori_loop(0, N_SLAB, step, (jnp.int32(0), jnp.int32(0)))
        drain(obf, obf_sems, qkv_ref.at[0], n_bf)
        drain(of32, of32_sems, rest_ref.at[0], n_f32)

        for slots in ((1, 2, 3), (0, 4, 5, 6)):
            for a in range(1, 1 + n_o):
                for slot in slots:
                    arrive(a, slot)
        for cp in first:
            cp.wait_send()
        for j, chip in enumerate(chips):
            for ch in range(N_CHUNK):
                copy(0, 4 + j, (*chip, c), sibling, ch).wait_send()
            for a in range(1, 1 + n_o):
                copy(a, 4 + j, (*chip, c), sibling).wait_send()
        for cp in mine:
            cp.wait()

    out_shape = ((jax.ShapeDtypeStruct((N_DEV * srows, D), BF16), jax.ShapeDtypeStruct((N_QKV, t, SLAB), BF16),
                  jax.ShapeDtypeStruct((N_REST, t, SLAB), F32), jax.ShapeDtypeStruct((t, D), BF16))
                 + tuple(jax.ShapeDtypeStruct((N_DEV * o.shape[0], o.shape[1]), o.dtype) for o in others))
    n_all = 1 + n_o
    n_sems = 7 * (N_CHUNK + n_o)
    pair = pltpu.SemaphoreType.DMA((2,))
    grid_spec = pltpu.PrefetchScalarGridSpec(
        num_scalar_prefetch=1, grid=(1,),
        in_specs=[ANY, ANY, pl.BlockSpec(ada.shape, lambda i, tbl: (0, 0, 0)),
                  pl.BlockSpec((N_SLAB, 1, SLAB), lambda i, tbl: (0, 0, 0))] + [ANY] * n_o,
        out_specs=(ANY,) * (4 + n_o),
        scratch_shapes=[pltpu.VMEM((2, SLAB, D), BF16), pltpu.VMEM((2, t, SLAB), BF16), pltpu.VMEM((2, t, SLAB), F32),
                        pltpu.VMEM((t, D), BF16), pltpu.VMEM((2, xt, D), F32),
                        pltpu.SemaphoreType.DMA((n_sems,)), pltpu.SemaphoreType.DMA((n_sems,)),
                        pltpu.SemaphoreType.DMA((n_all,)), pair, pair, pair, pair, pair])
    res = _pcall(body, grid_spec=grid_spec, out_shape=out_shape, name="project_gather",
                 compiler_params=_params(48, ("arbitrary",)))(table, shard, x, ada, b_in3, *others)
    return res[0], res[1], res[2], res[3], list(res[4:])


def _bias_tables(g):
    window, dil = GROUPS[g]
    span = window // dil
    qi = jnp.arange(BLK)[:, None]
    kj = jnp.arange(2 * BLK)[None, :]
    delta = qi + BLK - kj
    valid = (delta >= 0) & (delta <= span)
    heads = jnp.arange(4, dtype=F32) + 4.0 * g
    slopes = 2.0 ** (-8.0 * (heads + 1.0) / 12.0)
    bias = -slopes[:, None, None] * (delta * dil).astype(F32)[None]
    return jnp.where(valid[None], bias, -1e30).reshape(4 * BLK, 2 * BLK)


def _head_masks(shape):
    lane = lax.broadcasted_iota(jnp.int32, shape, 1)
    return [(lane >= 64 * h) & (lane < 64 * (h + 1)) for h in range(4)]


def _stack_heads(v, masks):
    return jnp.concatenate([jnp.where(masks[h], v, jnp.zeros_like(v)) for h in range(4)], axis=0)


def _unstack_heads(v4, masks):
    out = jnp.where(masks[0], v4[0:BLK], 0.0)
    for h in range(1, 4):
        out = jnp.where(masks[h], v4[BLK * h:BLK * (h + 1)], out)
    return out


def _by_residue(v, n, dil):
    return jnp.swapaxes(v.reshape(n, dil, 128), 0, 1).reshape(n * dil, 128) if dil > 1 else v


def _by_token(v, n, dil):
    return jnp.swapaxes(v.reshape(dil, n, 128), 0, 1).reshape(n * dil, 128) if dil > 1 else v


def _regroup(load_half, dst_ref, stage_ref, n, dil):
    if dil >= 8:
        for hlf in range(2):
            dst_ref[:, pl.ds(128 * hlf, 128)] = _by_residue(load_half(hlf), n, dil).astype(dst_ref.dtype)
        return
    for hlf in range(2):
        stage_ref[hlf] = load_half(hlf)
    for r in range(dil):
        for hlf in range(2):
            dst_ref[pl.ds(r * n, n), pl.ds(128 * hlf, 128)] = stage_ref[hlf, pl.ds(r, n, stride=dil), :].astype(dst_ref.dtype)


def _store_block(res_ref, r, i, val, n):
    for hlf in range(2):
        res_ref[hlf, pl.ds(pl.multiple_of(r * n + i * BLK, BLK), BLK), :] = val[:, 128 * hlf:128 * (hlf + 1)]


def _for_blocks(block, dil, nblk, ways=WAYS):
    ways = min(ways, max(dil, nblk))
    if dil == 1:
        for i in range(ways):
            block(0, i, i == 0)

        def step(k, carry):
            for j in range(ways):
                block(0, ways * k + j, False)
            return carry

        lax.fori_loop(1, nblk // ways, step, 0)
    else:
        ways = min(ways, dil)

        def residues(k, carry):
            for j in range(ways):
                block(ways * k + j, 0, True)
            if nblk > 1:
                def loop(i, c):
                    for j in range(ways):
                        block(ways * k + j, i, False)
                    return c
                lax.fori_loop(1, nblk, loop, 0)
            return carry

        lax.fori_loop(0, dil // ways, residues, 0)


def _attn_forward(qkv, nbat):
    t = qkv.shape[1]
    seq = t // nbat
    n_grp = len(GROUPS)

    def body(qkv_ref, b0_ref, b1_ref, b2_ref, ol_ref, stage, qs_ref, ks_ref, vs_ref, *nat):
        masks = _head_masks((BLK, SLAB))
        bias_refs = (b0_ref, b1_ref, b2_ref)
        for g, (_, dil) in enumerate(GROUPS):
            n = seq // dil
            bias_ref, nat_o, nat_l = bias_refs[g], nat[2 * g], nat[2 * g + 1]
            if dil > 1:
                qd, kd, vd = qs_ref, ks_ref, vs_ref
                for which, dst in enumerate((qd, kd, vd)):
                    _regroup(lambda hlf, which=which, g=g: qkv_ref[3 * which + g, :, pl.ds(128 * hlf, 128)].astype(F32), dst, stage, n, dil)
            else:
                qd, kd, vd = qkv_ref.at[g], qkv_ref.at[3 + g], qkv_ref.at[6 + g]

            def block(r, i, first, n=n, dil=dil, qd=qd, kd=kd, vd=vd, bias_ref=bias_ref, nat_o=nat_o, nat_l=nat_l):
                base = r * n
                qs = pl.ds(pl.multiple_of(base + i * BLK, BLK), BLK)
                ks = pl.ds(pl.multiple_of(base, BLK), BLK) if first else pl.ds(pl.multiple_of(base + (i - 1) * BLK, BLK), 2 * BLK)
                q, kk, vv = qd[qs, :], kd[ks, :], vd[ks, :]
                bias = bias_ref[:, pl.ds(BLK, BLK)] if first else bias_ref[...]
                s = _nt(_stack_heads(q, masks), kk) * 0.125 + bias
                m = jnp.max(s, axis=1, keepdims=True)
                p = jnp.exp(s - m)
                den = jnp.sum(p, axis=1, keepdims=True)
                out = _unstack_heads(_nn((p * (1.0 / den)).astype(BF16), vv), masks)
                lse = _unstack_heads(jnp.broadcast_to(m + jnp.log(den), (4 * BLK, SLAB)), masks)
                _store_block(nat_o, r, i, out, n)
                _store_block(nat_l, r, i, lse, n)

            _for_blocks(block, dil, n // BLK, ways=2 * WAYS)

        def tokens(k, hlf):
            dil = GROUPS[k // 2][1]
            return _by_token(nat[k][hlf], seq // dil, dil)

        for hlf in range(2):
            l0, l1, l2 = tokens(1, hlf), tokens(3, hlf), tokens(5, hlf)
            mx = jnp.maximum(jnp.maximum(l0, l1), l2)
            e0, e1, e2 = jnp.exp(l0 - mx), jnp.exp(l1 - mx), jnp.exp(l2 - mx)
            den = e0 + e1 + e2
            ol_ref[0, :, pl.ds(128 * hlf, 128)] = (e0 * tokens(0, hlf) + e1 * tokens(2, hlf) + e2 * tokens(4, hlf)) * (1.0 / den)
            ol_ref[1, :, pl.ds(128 * hlf, 128)] = mx + jnp.log(den)

    halves = pltpu.VMEM((2, seq, 128), F32)
    bias_spec = pl.BlockSpec((4 * BLK, 2 * BLK), lambda b: (0, 0))
    return _pcall(
        body, grid=(nbat,), out_shape=jax.ShapeDtypeStruct((2, t, SLAB), F32),
        in_specs=[pl.BlockSpec((N_QKV, seq, SLAB), lambda b: (0, b, 0))] + [bias_spec] * n_grp,
        out_specs=pl.BlockSpec((2, seq, SLAB), lambda b: (0, b, 0)),
        scratch_shapes=[halves] + [pltpu.VMEM((seq, SLAB), BF16)] * 3 + [halves] * (2 * n_grp),
        name="attn_forward", compiler_params=_params(56, ("parallel",)))(qkv, *[_bias_tables(g) for g in range(n_grp)])


def _attn_backward(qkv, do_attn, ol_tot, dproj, g, nbat):
    t = qkv.shape[1]
    seq = t // nbat
    dil = GROUPS[g][1]
    n = seq // dil
    nblk = n // BLK
    qkv4 = qkv.reshape(3, 3, t, SLAB)
    dp4 = dproj.reshape(DP_SLABS // 3, 3, t, SLAB)

    def body(qkv_ref, do_ref, ol_ref, bias_ref, dp_in, dp_ref, gb_ref, dk_acc, dv_acc, *scratch):
        del dp_in
        masks = _head_masks((BLK, SLAB))

        @pl.when(pl.program_id(0) == 0)
        def _():
            gb_ref[...] = jnp.zeros_like(gb_ref)

        dk_acc[...] = jnp.zeros_like(dk_acc)
        dv_acc[...] = jnp.zeros_like(dv_acc)
        if dil > 1:
            stage, qd, kd, vd, dod, prodd, lsed, dq_res = scratch
            lanes = lambda hlf: pl.ds(128 * hlf, 128)
            for which, dst in enumerate((qd, kd, vd)):
                _regroup(lambda hlf, which=which: qkv_ref[which, 0, :, lanes(hlf)].astype(F32), dst, stage, n, dil)
            _regroup(lambda hlf: do_ref[:, lanes(hlf)].astype(F32), dod, stage, n, dil)
            _regroup(lambda hlf: do_ref[:, lanes(hlf)].astype(F32) * ol_ref[0, :, lanes(hlf)], prodd, stage, n, dil)
            _regroup(lambda hlf: ol_ref[1, :, lanes(hlf)], lsed, stage, n, dil)
        else:
            qd, kd, vd = qkv_ref.at[0, 0], qkv_ref.at[1, 0], qkv_ref.at[2, 0]

        def block(r, i, first):
            base = r * n
            qs = pl.ds(pl.multiple_of(base + i * BLK, BLK), BLK)
            ks = pl.ds(pl.multiple_of(base, BLK), BLK) if first else pl.ds(pl.multiple_of(base + (i - 1) * BLK, BLK), 2 * BLK)
            q, kk, vv = qd[qs, :], kd[ks, :], vd[ks, :]
            if dil > 1:
                do, prod, lse = dod[qs, :], prodd[qs, :], lsed[qs, :]
            else:
                do = do_ref[qs, :]
                prod = do.astype(F32) * ol_ref[0, qs, :]
                lse = ol_ref[1, qs, :]
            q4, do4 = _stack_heads(q, masks), _stack_heads(do, masks)
            bias = bias_ref[:, pl.ds(BLK, BLK)] if first else bias_ref[...]
            lse4 = jnp.concatenate([lse[:, 64 * h:64 * h + 1] for h in range(4)], axis=0)
            delta4 = jnp.concatenate([jnp.sum(jnp.where(masks[h], prod, 0.0), axis=1, keepdims=True) for h in range(4)], axis=0)
            p = jnp.exp(_nt(q4, kk) * 0.125 + bias - lse4)
            ds = (p * (_nt(do4, vv) - delta4)).astype(BF16)
            dv_acc[ks, :] += _tn(p.astype(BF16), do4)
            dk_acc[ks, :] += _tn(ds, q4) * 0.125
            dq = _unstack_heads(_nn(ds, kk), masks) * 0.125
            if dil > 1:
                _store_block(dq_res, r, i, dq, n)
            else:
                dp_ref[0, 0, qs, :] = dq.astype(BF16)
            gb_ref[0] += _part8(dq)

        _for_blocks(block, dil, nblk)
        gb_ref[1] += _part8(dk_acc[...])
        gb_ref[2] += _part8(dv_acc[...])
        if dil > 1:
            for hlf in range(2):
                half = pl.ds(128 * hlf, 128)
                dp_ref[0, 0, :, half] = _by_token(dq_res[hlf], n, dil).astype(BF16)
                dp_ref[1, 0, :, half] = _by_token(dk_acc[:, half], n, dil).astype(BF16)
                dp_ref[2, 0, :, half] = _by_token(dv_acc[:, half], n, dil).astype(BF16)
        else:
            dp_ref[1, 0] = dk_acc[...].astype(BF16)
            dp_ref[2, 0] = dv_acc[...].astype(BF16)

    scratch = [pltpu.VMEM((seq, SLAB), F32)] * 2
    if dil > 1:
        halves = pltpu.VMEM((2, seq, 128), F32)
        scratch += [halves] + [pltpu.VMEM((seq, SLAB), BF16)] * 4 + [pltpu.VMEM((seq, SLAB), F32)] * 2 + [halves]
    dp, gb = _pcall(
        body, grid=(nbat,),
        out_shape=(jax.ShapeDtypeStruct(dp4.shape, BF16), jax.ShapeDtypeStruct((3, 8, SLAB), F32)),
        in_specs=[pl.BlockSpec((3, 1, seq, SLAB), lambda b: (0, g, b, 0)),
                  pl.BlockSpec((seq, SLAB), lambda b: (b, 0)),
                  pl.BlockSpec((2, seq, SLAB), lambda b: (0, b, 0)),
                  pl.BlockSpec((4 * BLK, 2 * BLK), lambda b: (0, 0)), ANY],
        out_specs=(pl.BlockSpec((3, 1, seq, SLAB), lambda b: (DP_SLABS // 9 - 1, g, b, 0)),
                   pl.BlockSpec((3, 8, SLAB), lambda b: (0, 0, 0))),
        scratch_shapes=scratch, input_output_aliases={4: 0}, name=f"attn_backward_{g}",
        compiler_params=_params(48, ("arbitrary",)))(qkv4, do_attn, ol_tot, _bias_tables(g), dp4)
    return dp.reshape(DP_SLABS, t, SLAB), gb


def _mid(rest, ol_tot, x, tgt, ada, cw, b_out, ln_g, ln_b, w_pa_t, w_pb, w_out, tm=256):
    t = x.shape[0]
    nbat = ada.shape[0]
    nt = t // tm
    tps = nt // nbat

    def body(rest_ref, halo_ref, ol_ref, x_ref, t_ref, ada_ref, cw_ref, bout_ref, lng_ref, lnb_ref,
             wpat_ref, wpb_ref, wout_ref,
             dp_ref, gx0_ref, doa_ref, mg_ref, dof_ref, bbs_ref, dyc_ref, a_ref, dya_ref,
             gbr_ref, sv_ref, dgate_ref, carry_ref, keep_ref):
        i = pl.program_id(0)
        ti = nt - 1 - i
        pos = ti % tps

        @pl.when(i == 0)
        def _():
            gbr_ref[...] = jnp.zeros_like(gbr_ref)
            sv_ref[...] = jnp.zeros_like(sv_ref)

        @pl.when(pos == tps - 1)
        def _():
            dgate_ref[...] = jnp.zeros_like(dgate_ref)
            carry_ref[...] = jnp.zeros_like(carry_ref)

        row = lax.broadcasted_iota(jnp.int32, (tm, SLAB), 0)
        halo_on = (pos > 0).astype(F32)

        def cols(s):
            return pl.ds(SLAB * s, SLAB)

        o_attn = ol_ref[0]
        z_a = rest_ref[R_ZA]
        sg_za = _sigmoid(z_a)
        a_ref[...] = (o_attn * z_a * sg_za).astype(BF16)
        y_attn = _nt(a_ref[...], wpat_ref[...])

        for s in range(4):
            u = rest_ref[R_GC + s] * rest_ref[R_UX + s]
            hu = halo_ref[R_GC + s] * halo_ref[R_UX + s] * halo_on
            u1 = jnp.where(row == 0, hu[7:8], pltpu.roll(u, 1, 0))
            u2 = jnp.where(row == 0, hu[6:7], jnp.where(row == 1, hu[7:8], pltpu.roll(u, 2, 0)))
            conv = cw_ref[0:1, cols(s)] * u2 + cw_ref[1:2, cols(s)] * u1 + cw_ref[2:3, cols(s)] * u
            zc = rest_ref[R_ZC + s]
            sg = _sigmoid(zc)
            keep_ref[2, :, cols(s)], keep_ref[3, :, cols(s)], keep_ref[4, :, cols(s)], keep_ref[5, :, cols(s)] = u1, u2, conv, sg
            bbs_ref[:, cols(s)] = (rest_ref[R_GB + s] * conv * (zc * sg)).astype(BF16)
        y_conv = _nn(bbs_ref[...], wpb_ref[...])

        for s in range(4):
            s_a, s_b = _sigmoid(rest_ref[R_GA + s]), _sigmoid(rest_ref[R_GBM + s])
            keep_ref[0, :, cols(s)], keep_ref[1, :, cols(s)] = s_a, s_b
            mg_ref[:, cols(s)] = (s_a * y_attn[:, SLAB * s:SLAB * (s + 1)] + s_b * y_conv[:, SLAB * s:SLAB * (s + 1)]).astype(BF16)
        o = _nn(mg_ref[...], wout_ref[...]) + bout_ref[...]
        gate = ada_ref[0, 2:3, :]
        r = ALPHA * x_ref[...] + gate * o
        mu = jnp.mean(r, axis=1, keepdims=True)
        rc = r - mu
        rstd = lax.rsqrt(jnp.mean(rc * rc, axis=1, keepdims=True) + LN_EPS)
        xhat = rc * rstd
        err = xhat * lng_ref[...] + lnb_ref[...] - t_ref[...]
        sv_ref[6] += _part8(err * err)
        dy = err * (1.0 / D)
        sv_ref[0] += _part8(dy * xhat)
        sv_ref[1] += _part8(dy)
        dxh = dy * lng_ref[...]
        dr = rstd * (dxh - jnp.mean(dxh, axis=1, keepdims=True) - xhat * jnp.mean(dxh * xhat, axis=1, keepdims=True))
        gx0_ref[...] = ALPHA * dr
        dgate_ref[0] += _part8(dr * o)
        do_ = dr * gate
        sv_ref[2] += _part8(do_)
        dof_ref[...] = do_.astype(BF16)
        dmerged = _nt(dof_ref[...], wout_ref[...])
        for s in range(4):
            s_a, s_b = keep_ref[0, :, cols(s)], keep_ref[1, :, cols(s)]
            dm = dmerged[:, SLAB * s:SLAB * (s + 1)]
            ya, yc = y_attn[:, SLAB * s:SLAB * (s + 1)], y_conv[:, SLAB * s:SLAB * (s + 1)]
            dya_ref[:, cols(s)] = (dm * s_a).astype(BF16)
            dyc_ref[:, cols(s)] = (dm * s_b).astype(BF16)
            dga = dm * ya * s_a * (1.0 - s_a)
            dgb = dm * yc * s_b * (1.0 - s_b)
            dp_ref[R_GA + s] = dga.astype(BF16)
            dp_ref[R_GBM + s] = dgb.astype(BF16)
            gbr_ref[R_GA + s] += _part8(dga)
            gbr_ref[R_GBM + s] += _part8(dgb)

        da = _nn(dya_ref[...], wpat_ref[...])
        doa_ref[...] = (da * z_a * sg_za).astype(BF16)
        dza = da * o_attn * (sg_za * (1.0 + z_a * (1.0 - sg_za)))
        dp_ref[R_ZA] = dza.astype(BF16)
        gbr_ref[R_ZA] += _part8(dza)

        dbb = _nt(dyc_ref[...], wpb_ref[...])
        for s in range(4):
            ux, gc, zc = rest_ref[R_UX + s], rest_ref[R_GC + s], rest_ref[R_ZC + s]
            u = gc * ux
            u1, u2, conv, sg = keep_ref[2, :, cols(s)], keep_ref[3, :, cols(s)], keep_ref[4, :, cols(s)], keep_ref[5, :, cols(s)]
            gb = rest_ref[R_GB + s]
            d_b = dbb[:, SLAB * s:SLAB * (s + 1)]
            szc = zc * sg
            dgb_ = d_b * conv * szc
            dconv = d_b * gb * szc
            dzc = d_b * gb * conv * (sg * (1.0 + zc * (1.0 - sg)))
            sv_ref[3, :, cols(s)] += _part8(dconv * u2)
            sv_ref[4, :, cols(s)] += _part8(dconv * u1)
            sv_ref[5, :, cols(s)] += _part8(dconv * u)
            nxt = carry_ref[:, cols(s)]
            d1 = jnp.where(row == tm - 1, nxt[0:1], pltpu.roll(dconv, tm - 1, 0))
            d2 = jnp.where(row == tm - 1, nxt[1:2], jnp.where(row == tm - 2, nxt[0:1], pltpu.roll(dconv, tm - 2, 0)))
            carry_ref[:, cols(s)] = dconv[0:8]
            du = cw_ref[2:3, cols(s)] * dconv + cw_ref[1:2, cols(s)] * d1 + cw_ref[0:1, cols(s)] * d2
            dgc, dux = du * ux, du * gc
            for slab, val in ((R_GB + s, dgb_), (R_ZC + s, dzc), (R_GC + s, dgc), (R_UX + s, dux)):
                dp_ref[slab] = val.astype(BF16)
                gbr_ref[slab] += _part8(val)

    def tile(i):
        return nt - 1 - i

    row_blk = lambda i: (tile(i), 0)
    slab_blk = lambda i: (0, tile(i), 0)
    const2 = lambda i: (0, 0)
    const3 = lambda i: (0, 0, 0)
    in_specs = [
        pl.BlockSpec((N_REST, tm, SLAB), slab_blk),
        pl.BlockSpec((N_REST, 8, SLAB), lambda i: (0, jnp.maximum(tile(i) * (tm // 8) - 1, 0), 0)),
        pl.BlockSpec((1, tm, SLAB), slab_blk),
        pl.BlockSpec((tm, D), row_blk), pl.BlockSpec((tm, D), row_blk),
        pl.BlockSpec((1, 3, D), lambda i: (tile(i) // tps, 0, 0)),
        pl.BlockSpec((3, D), const2), pl.BlockSpec((1, D), const2), pl.BlockSpec((1, D), const2), pl.BlockSpec((1, D), const2),
        pl.BlockSpec((D, SLAB), const2), pl.BlockSpec((D, D), const2), pl.BlockSpec((D, D), const2)]
    bf_rows = lambda: jax.ShapeDtypeStruct((t, D), BF16)
    out_shape = (
        jax.ShapeDtypeStruct((DP_SLABS, t, SLAB), BF16), jax.ShapeDtypeStruct((t, D), F32),
        jax.ShapeDtypeStruct((t, SLAB), BF16),
        bf_rows(), bf_rows(), bf_rows(), bf_rows(), jax.ShapeDtypeStruct((t, SLAB), BF16), bf_rows(),
        jax.ShapeDtypeStruct((N_REST, 8, SLAB), F32), jax.ShapeDtypeStruct((7, 8, D), F32),
        jax.ShapeDtypeStruct((nbat, 8, D), F32))
    out_specs = (
        pl.BlockSpec((N_REST, tm, SLAB), slab_blk), pl.BlockSpec((tm, D), row_blk),
        pl.BlockSpec((tm, SLAB), row_blk),
        pl.BlockSpec((tm, D), row_blk), pl.BlockSpec((tm, D), row_blk), pl.BlockSpec((tm, D), row_blk),
        pl.BlockSpec((tm, D), row_blk), pl.BlockSpec((tm, SLAB), row_blk), pl.BlockSpec((tm, D), row_blk),
        pl.BlockSpec((N_REST, 8, SLAB), const3), pl.BlockSpec((7, 8, D), const3),
        pl.BlockSpec((1, 8, D), lambda i: (tile(i) // tps, 0, 0)))
    return _pcall(body, grid=(nt,), out_shape=out_shape, in_specs=in_specs, out_specs=out_specs,
                  scratch_shapes=[pltpu.VMEM((8, D), F32), pltpu.VMEM((6, tm, D), F32)], name="mid",
                  compiler_params=_params(56, ("arbitrary",)))(
        rest, rest, ol_tot, x, tgt, ada, cw, b_out, ln_g, ln_b, w_pa_t, w_pb, w_out)


def _tn_matmul(lhs, rhs, lhs_spec, n_steps, out_rows, out_index, name, after):
    t, n = rhs.shape

    def body(l_ref, r_ref, after_ref, o_ref):
        del after_ref
        o_ref[...] = _tn(l_ref[0] if len(l_ref.shape) == 3 else l_ref[...], r_ref[...])

    return _pcall(body, grid=(n_steps,), out_shape=jax.ShapeDtypeStruct((out_rows, n), F32),
                  in_specs=[lhs_spec, pl.BlockSpec((t, n), lambda j: (0, 0)), ANY],
                  out_specs=pl.BlockSpec((SLAB, n), out_index), name=name,
                  compiler_params=_params(48, ("parallel",)))(lhs, rhs, after)


def _grad_rows_2d(lhs, rhs, name, after, tc=1024):
    t, k = lhs.shape
    n = rhs.shape[1]

    def body(l_ref, r_ref, after_ref, o_ref):
        del after_ref
        part = _tn(l_ref[...], r_ref[...])

        @pl.when(pl.program_id(0) == 0)
        def _():
            o_ref[...] = part

        @pl.when(pl.program_id(0) > 0)
        def _():
            o_ref[...] += part

    return _pcall(body, grid=(t // tc,), out_shape=jax.ShapeDtypeStruct((k, n), F32),
                  in_specs=[pl.BlockSpec((tc, k), lambda i: (i, 0)), pl.BlockSpec((tc, n), lambda i: (i, 0)), ANY],
                  out_specs=pl.BlockSpec((k, n), lambda i: (0, 0)), name=name,
                  compiler_params=_params(32, ("arbitrary",)))(lhs, rhs, after)


def _w_row_block(j):
    return (j + N_QKV) % N_SLAB


def _dp_slab(j):
    return jnp.where(j < N_REST, j, j + 2)


def _grad_w_in_t(dproj, h):
    t = h.shape[0]
    return _tn_matmul(dproj, h, pl.BlockSpec((1, t, SLAB), lambda j: (_dp_slab(j), 0, 0)), N_SLAB, D_IN,
                      lambda j: (_w_row_block(j), 0), "grad_w_in", h)


def _grad_h(dproj, w_in_t, gx0, x, ada, after, tm=512):
    t = x.shape[0]
    nbat = ada.shape[0]
    tps = (t // nbat) // tm

    def body(dp_ref, w_ref, gx0_ref, x_ref, ada_ref, after_ref, gx_ref, dss_ref):
        del after_ref
        i = pl.program_id(0)
        dh = None
        for j in range(N_SLAB):
            slab = j if j < N_REST else j + 2
            part = _nn(dp_ref[slab], w_ref[pl.ds(SLAB * ((j + N_QKV) % N_SLAB), SLAB), :])
            dh = part if dh is None else dh + part
        gx_ref[...] = gx0_ref[...] + dh * (1.0 + ada_ref[0, 1:2, :])

        @pl.when((i % tps) == 0)
        def _():
            dss_ref[...] = jnp.zeros_like(dss_ref)

        dss_ref[0, 0] += _part8(dh)
        dss_ref[0, 1] += _part8(dh * x_ref[...])

    return _pcall(
        body, grid=(t // tm,),
        out_shape=(jax.ShapeDtypeStruct((t, D), F32), jax.ShapeDtypeStruct((nbat, 2, 8, D), F32)),
        in_specs=[pl.BlockSpec((DP_SLABS, tm, SLAB), lambda i: (0, i, 0)),
                  pl.BlockSpec((D_IN, D), lambda i: (0, 0), pipeline_mode=pl.Buffered(1)),
                  pl.BlockSpec((tm, D), lambda i: (i, 0)), pl.BlockSpec((tm, D), lambda i: (i, 0)),
                  pl.BlockSpec((1, 3, D), lambda i: (i // tps, 0, 0)), ANY],
        out_specs=(pl.BlockSpec((tm, D), lambda i: (i, 0)),
                   pl.BlockSpec((1, 2, 8, D), lambda i: (i // tps, 0, 0, 0))),
        name="grad_h", compiler_params=_params(60, ("arbitrary",)))(dproj, w_in_t, gx0, x, ada, after)


def _chip(m):
    x, y, _ = _my_position()
    return (x ^ ((m >> 1) & 1), y ^ (m & 1))


def _exchange_siblings(grads, after, name):
    n = len(grads)

    def body(*refs):
        copies = _sibling_copies(refs[:n], refs[n + 1:2 * n + 1], refs[2 * n + 1], refs[2 * n + 2])
        for cp in copies:
            cp.start()
        for cp in copies:
            cp.wait()

    return _pcall(body, out_shape=tuple(_sibling_zones(grads)), in_specs=[ANY] * (n + 1), out_specs=(ANY,) * n,
                  name=name, scratch_shapes=[pltpu.SemaphoreType.DMA((4 * n,))] * 2)(*grads, after)


def _sibling_zones(grads):
    return [jax.ShapeDtypeStruct((4, g.shape[0] // N_DEV, g.shape[1]), g.dtype) for g in grads]


def _sibling_copies(srcs, lands, send_sems, recv_sems):
    x, y, c = _my_position()
    copies = []
    for a, (src, land) in enumerate(zip(srcs, lands)):
        rows = land.shape[1]
        for m in range(4):
            dev = _flat(*_chip(m), 1 - c)
            copies.append(pltpu.make_async_remote_copy(
                src_ref=src.at[pl.ds(pl.multiple_of(dev * rows, 8), rows), :], dst_ref=land.at[m],
                send_sem=send_sems.at[4 * a + m], recv_sem=recv_sems.at[4 * a + m], device_id=(x, y, 1 - c),
                device_id_type=MESH))
    return copies


def _chip_copies(srcs, lands, send_sems, recv_sems):
    _, _, c = _my_position()
    return [pltpu.make_async_remote_copy(
        src_ref=srcs[a].at[m - 1], dst_ref=lands[a].at[m - 1], send_sem=send_sems.at[3 * a + m - 1],
        recv_sem=recv_sems.at[3 * a + m - 1], device_id=(*_chip(m), c), device_id_type=MESH)
        for a in range(len(srcs)) for m in range(1, 4)]


HBM = pl.BlockSpec(memory_space=pltpu.HBM)
SEM = pl.BlockSpec(memory_space=pltpu.SEMAPHORE)
SPLIT_COPY = pltpu.CompilerParams(has_side_effects=pltpu.SideEffectType.DATAFLOW_SIDE_EFFECTING)


def _start_copies(make_copies, n_sems, srcs, zones, name):
    n = len(srcs)

    def body(*refs):
        for cp in make_copies(refs[:n], refs[n:2 * n], refs[2 * n], refs[2 * n + 1]):
            cp.start()
        refs[-1][...] = jnp.zeros_like(refs[-1])

    hbm = tuple(pltpu.HBM(b.shape, b.dtype) for b in list(srcs) + list(zones))
    out_shape = (pltpu.SemaphoreType.DMA((n_sems,)), pltpu.SemaphoreType.DMA((n_sems,))) + hbm + (jax.ShapeDtypeStruct((8, 128), F32),)
    operands = [pltpu.with_memory_space_constraint(b, pltpu.HBM) for b in srcs]
    operands += [pltpu.with_memory_space_constraint(lax.empty(z.shape, z.dtype), pltpu.HBM) for z in zones]
    res = _pcall(body, out_shape=out_shape, in_specs=[HBM] * (2 * n), out_specs=(SEM, SEM) + (HBM,) * (2 * n) + (VMEM,),
                 input_output_aliases={i: 2 + i for i in range(2 * n)}, name=name, compiler_params=SPLIT_COPY)(*operands)
    return (res[0], res[1], res[2:2 + n], res[2 + n:2 + 2 * n]), res[-1]


def _wait_copies(make_copies, flight, after, name):
    send_sems, recv_sems, srcs, zones = flight
    n = len(srcs)

    def body(*refs):
        for cp in make_copies(refs[:n], refs[n:2 * n], refs[2 * n], refs[2 * n + 1]):
            cp.wait_send()
            cp.wait_recv()

    hbm = tuple(pltpu.HBM(b.shape, b.dtype) for b in list(srcs) + list(zones))
    res = _pcall(body, out_shape=hbm, in_specs=[HBM] * (2 * n) + [SEM, SEM, ANY], out_specs=(HBM,) * (2 * n),
                 input_output_aliases={i: i for i in range(2 * n)}, name=name, compiler_params=SPLIT_COPY)(
        *srcs, *zones, send_sems, recv_sems, after)
    return res[:n], res[n:]


def _pair_sums(devs, grads, lands, n_steps, name):
    n = len(grads)
    rows = [l.shape[1] for l in lands]
    rbs = [r // n_steps for r in rows]

    def body(devs_ref, *refs):
        del devs_ref
        g_refs, land_refs, outs = refs[:4 * n], refs[4 * n:5 * n], refs[5 * n:]
        for a in range(n):
            outs[2 * a][...] = g_refs[4 * a][...] + land_refs[a][0]
            for m in range(1, 4):
                outs[2 * a + 1][m - 1] = (g_refs[4 * a + m][...] + land_refs[a][m]).astype(BF16)

    def block_of(m, per_dev):
        return lambda i, devs_ref: (devs_ref[m] * per_dev + i, 0)

    in_specs = [pl.BlockSpec((rb, l.shape[2]), block_of(m, n_steps)) for rb, l in zip(rbs, lands) for m in range(4)]
    in_specs += [pl.BlockSpec((4, rb, l.shape[2]), lambda i, devs_ref: (0, i, 0)) for rb, l in zip(rbs, lands)]
    out_shape, out_specs = [], []
    for rb, l in zip(rbs, lands):
        out_shape += [jax.ShapeDtypeStruct(l.shape[1:], F32), jax.ShapeDtypeStruct((3,) + l.shape[1:], BF16)]
        out_specs += [pl.BlockSpec((rb, l.shape[2]), lambda i, devs_ref: (i, 0)),
                      pl.BlockSpec((3, rb, l.shape[2]), lambda i, devs_ref: (0, i, 0))]
    grid_spec = pltpu.PrefetchScalarGridSpec(num_scalar_prefetch=1, grid=(n_steps,), in_specs=in_specs, out_specs=tuple(out_specs))
    res = _pcall(body, grid_spec=grid_spec, out_shape=tuple(out_shape), name=name,
                 compiler_params=_params(48, ("parallel",)))(devs, *[g for g in grads for _ in range(4)], *lands)
    return res[0::2], res[1::2]


def _final_sums(mine, lands, n_steps, name):
    n = len(mine)
    rbs = [o.shape[0] // n_steps for o in mine]

    def body(*refs):
        mine_refs, land_refs, outs = refs[:n], refs[n:2 * n], refs[2 * n:]
        for a in range(n):
            tot = mine_refs[a][...]
            for m in range(3):
                tot = tot + land_refs[a][m].astype(F32)
            outs[a][...] = tot

    in_specs = ([pl.BlockSpec((rb, o.shape[1]), lambda i: (i, 0)) for rb, o in zip(rbs, mine)]
                + [pl.BlockSpec((3, rb, o.shape[1]), lambda i: (0, i, 0)) for rb, o in zip(rbs, mine)])
    out_specs = tuple(pl.BlockSpec((rb, o.shape[1]), lambda i: (i, 0)) for rb, o in zip(rbs, mine))
    out_shape = tuple(jax.ShapeDtypeStruct(o.shape, F32) for o in mine)
    return _pcall(body, grid=(n_steps,), out_shape=out_shape, in_specs=in_specs, out_specs=out_specs, name=name,
                  compiler_params=_params(32, ("parallel",)))(*mine, *lands)


def _reduce_scatter_begin(big, small_after_start):
    c = lax.axis_index("c")
    devs = jnp.stack([_flat(*_chip(m), c) for m in range(4)]).astype(jnp.int32)
    flight, token = _start_copies(_sibling_copies, 4, [big], _sibling_zones([big]), "siblings_start")
    small = small_after_start(token)
    (big,), big_lands = _wait_copies(_sibling_copies, flight, small[-1], "siblings_wait")
    big_mine, big_send = _pair_sums(devs, [big], big_lands, 4, "pair_sums_w_in")
    big_flight, token = _start_copies(_chip_copies, 3, list(big_send), list(big_send), "chips_start_w_in")
    small_lands = _exchange_siblings(small, token, "exchange_siblings_rest")
    small_mine, small_send = _pair_sums(devs, small, small_lands, 1, "pair_sums_rest")
    small_flight, token = _start_copies(_chip_copies, 3 * len(small), list(small_send), list(small_send), "chips_start_rest")
    return (big_flight, small_flight, list(big_mine) + list(small_mine)), token


def _reduce_scatter_end(state, after):
    big_flight, small_flight, mine = state
    _, big_got = _wait_copies(_chip_copies, big_flight, after, "chips_wait_w_in")
    _, small_got = _wait_copies(_chip_copies, small_flight, after, "chips_wait_rest")
    small = _final_sums(mine[1:], small_got, 1, "final_sums_rest")
    return (mine[0], big_got[0]), list(small)


def _adamw(w, g, m, v):
    m_new = B1 * m + (1.0 - B1) * g
    v_new = B2 * v + (1.0 - B2) * (g * g)
    m_hat = m_new / (1.0 - B1 ** STEP)
    v_hat = v_new / (1.0 - B2 ** STEP)
    delta = -LR * (m_hat / (jnp.sqrt(v_hat) + EPS) + WD * w)
    return delta, m_new, v_new


def _final_sum_adam_rows(mine, land, w, m, v, n_steps, name):
    rows, ncol = w.shape
    blk = pl.BlockSpec((rows // n_steps, ncol), lambda i: (i, 0))

    def body(mine_ref, land_ref, w_ref, m_ref, v_ref, g_ref, d_ref, mo_ref, vo_ref):
        g = mine_ref[...]
        for k in range(3):
            g = g + land_ref[k].astype(F32)
        g_ref[...] = g
        d_ref[...], mo_ref[...], vo_ref[...] = _adamw(w_ref[...], g, m_ref[...], v_ref[...])

    shape = jax.ShapeDtypeStruct(w.shape, F32)
    return _pcall(body, grid=(n_steps,), out_shape=(shape,) * 4,
                  in_specs=[blk, pl.BlockSpec((3, rows // n_steps, ncol), lambda i: (0, i, 0)), blk, blk, blk],
                  out_specs=(blk,) * 4, name=name, compiler_params=_params(32, ("parallel",)))(mine, land, w, m, v)


def _adam_transposed(g_t, w, m, v, name):
    n, k = g_t.shape
    rb = min(k, 128)

    def body(gt_ref, w_ref, m_ref, v_ref, g_ref, d_ref, mo_ref, vo_ref):
        for src, skip, dst, size in _column_chunks(n):
            sl = pl.ds(dst, size)
            g = gt_ref[pl.ds(src, 128), :].T[:, skip:]
            delta, m_new, v_new = _adamw(w_ref[:, sl], g, m_ref[:, sl], v_ref[:, sl])
            g_ref[:, sl], d_ref[:, sl], mo_ref[:, sl], vo_ref[:, sl] = g, delta, m_new, v_new

    shape = jax.ShapeDtypeStruct(w.shape, F32)
    rows = pl.BlockSpec((rb, n), lambda i: (i, 0))
    return _pcall(body, grid=(k // rb,), out_shape=(shape,) * 4,
                  in_specs=[pl.BlockSpec((n, rb), lambda i: (0, i)), rows, rows, rows], out_specs=(rows,) * 4, name=name,
                  compiler_params=_params(32, ("parallel",)))(g_t, w, m, v)


def _adam_many(items, name):
    n = len(items)

    def body(*refs):
        ins, outs = refs[:4 * n], refs[4 * n:]
        for a in range(n):
            w_ref, g_ref, m_ref, v_ref = ins[4 * a:4 * a + 4]
            delta, m_new, v_new = _adamw(w_ref[...], g_ref[...], m_ref[...], v_ref[...])
            outs[3 * a][...], outs[3 * a + 1][...], outs[3 * a + 2][...] = delta, m_new, v_new

    out_shape = tuple(jax.ShapeDtypeStruct(it[0].shape, F32) for it in items for _ in range(3))
    flat = [arr for it in items for arr in it]
    res = _pcall(body, grid=(1,), out_shape=out_shape, in_specs=[_whole(a) for a in flat],
                 out_specs=tuple(_whole(o) for o in out_shape), name=name, compiler_params=_params(32))(*flat)
    return [tuple(res[3 * a:3 * a + 3]) for a in range(n)]


def _adam_w_ada(cact_all, dada_mine, w, m, v):
    def body(c_ref, d_ref, w_ref, m_ref, v_ref, g_ref, dl_ref, mo_ref, vo_ref):
        g = _tn(c_ref[...].astype(BF16), d_ref[...].astype(BF16))
        delta, m_new, v_new = _adamw(w_ref[...], g, m_ref[...], v_ref[...])
        g_ref[...], dl_ref[...], mo_ref[...], vo_ref[...] = g, delta, m_new, v_new

    shape = jax.ShapeDtypeStruct(w.shape, F32)
    operands = (cact_all, dada_mine, w, m, v)
    return _pcall(body, grid=(1,), out_shape=(shape,) * 4, in_specs=[_whole(a) for a in operands],
                  out_specs=(_whole(w),) * 4, name="adam_w_ada", compiler_params=_params(32))(*operands)


def kernel(x, c, w_ada, b_ada, w_in, b_in, conv_w, w_proj_attn, w_proj_conv, w_out, b_out, ln_g, ln_b, loss_target, m_w_ada, m_b_ada, m_w_in, m_b_in, m_conv_w, m_w_proj_attn, m_w_proj_conv, m_w_out, m_b_out, m_ln_g, m_ln_b, v_w_ada, v_b_ada, v_w_in, v_b_in, v_conv_w, v_w_proj_attn, v_w_proj_conv, v_w_out, v_b_out, v_ln_g, v_ln_b):
    nbat, seq, _ = x.shape
    t = nbat * seq
    me = _flat(*_my_position())
    x2, tgt2 = x.reshape(t, D), loss_target.reshape(t, D)
    sq = lambda a: a.reshape(a.shape[1:])

    tr = lambda a: a[0].T
    w_in_rows = tr(w_in)
    w_in_t_s = _cast_rows(w_in_rows, 4, "cast_w_in")
    w_pa_t_s, w_pb_s, w_out_s, cact_s, cw_s = _prep(sq(w_proj_attn), sq(w_proj_conv), sq(w_out), c, sq(conv_w))

    ncol = w_ada.shape[2]
    b_ada_mine = lax.dynamic_slice(b_ada, (0, me * ncol), (1, ncol))
    ada_slots, cact_slots, cw_slots = _ada_forward(cact_s, cw_s, sq(w_ada), b_ada_mine)
    cact_all = cact_slots[:, :nbat].reshape(N_DEV * nbat, D)
    cw = cw_slots[:, :3].transpose(1, 0, 2).reshape(3, D)
    ada_all = ada_slots[:, :, :nbat].transpose(1, 2, 0, 3).reshape(N_DEV * nbat, 3, D)
    ada = lax.dynamic_slice(ada_all, (me * nbat, 0, 0), (nbat, 3, D))

    w_in_t, qkv, rest, h, (w_pa_t, w_pb, w_o) = _project_gather(
        w_in_t_s, x2, ada, b_in.reshape(N_SLAB, 1, SLAB), [w_pa_t_s, w_pb_s, w_out_s])
    ol_tot = _attn_forward(qkv, nbat)
    (dproj, gx0, do_attn, merged, do_f, bbs, dyc, a_bf, dya, gb_rest, svec, dgate) = _mid(
        rest, ol_tot, x2, tgt2, ada, cw, b_out, ln_g, ln_b, w_pa_t, w_pb, w_o)

    gb_qkv = []
    for g in range(3):
        dproj, gb = _attn_backward(qkv, do_attn, ol_tot, dproj, g, nbat)
        gb_qkv.append(gb)
    g_w_in_t = _grad_w_in_t(dproj, h)

    def small_grads(token):
        g_w_out = _grad_rows_2d(merged, do_f, "grad_w_out", token)
        g_w_pb = _grad_rows_2d(bbs, dyc, "grad_w_proj_conv", g_w_out)
        g_w_pa_t = _grad_rows_2d(dya, a_bf, "grad_w_proj_attn", g_w_pb)
        return [g_w_out, g_w_pb, g_w_pa_t]

    rs_state, token = _reduce_scatter_begin(g_w_in_t, small_grads)
    grad_x, dss = _grad_h(dproj, w_in_t, gx0, x2, ada, token)

    rows8, tot, g_bada = _small_reduce(gb_rest, gb_qkv, svec, dgate, dss)
    (g_in_mine, g_in_got), (g_out, g_pb, g_pa_t) = _reduce_scatter_end(rs_state, tot)
    loss = tot[0, P_LOSS]
    dada_all = rows8[:, 0, P_DADA:].reshape(N_DEV * nbat, 3 * D)
    dada_mine = lax.dynamic_slice(dada_all, (0, me * ncol), (N_DEV * nbat, ncol))

    g_in_t, d_win_t, nm_win_t, nv_win_t = _final_sum_adam_rows(g_in_mine, g_in_got, w_in_rows, tr(m_w_in), tr(v_w_in), 4, "adam_w_in")
    g_win, d_win, nm_win, nv_win = g_in_t.T, d_win_t.T, nm_win_t.T, nv_win_t.T
    g_wpa, d_wpa, nm_wpa, nv_wpa = _adam_transposed(g_pa_t, sq(w_proj_attn), sq(m_w_proj_attn), sq(v_w_proj_attn), "adam_w_proj_attn")
    g_wada, d_wada, nm_wada, nv_wada = _adam_w_ada(cact_all, dada_mine, sq(w_ada), sq(m_w_ada), sq(v_w_ada))
    g_bin = tot[:, P_BIN:P_BIN + D_IN]
    g_bout = tot[:, P_BOUT:P_BOUT + D]
    g_lng = tot[:, P_LNG:P_LNG + D]
    g_lnb = tot[:, P_LNB:P_LNB + D]
    g_conv = lax.dynamic_slice(tot[:, P_CONV:P_CONV + 3 * D].reshape(3, D), (0, me * cw_s.shape[1]), (3, cw_s.shape[1]))
    upd = _adam_many([
        (sq(w_proj_conv), g_pb, sq(m_w_proj_conv), sq(v_w_proj_conv)),
        (sq(w_out), g_out, sq(m_w_out), sq(v_w_out)),
        (b_ada, g_bada, m_b_ada, v_b_ada), (b_in, g_bin, m_b_in, v_b_in), (sq(conv_w), g_conv, sq(m_conv_w), sq(v_conv_w)),
        (b_out, g_bout, m_b_out, v_b_out), (ln_g, g_lng, m_ln_g, v_ln_g), (ln_b, g_lnb, m_ln_b, v_ln_b)], "adam_rest")
    (d_wpb, nm_wpb, nv_wpb), (d_wout, nm_wout, nv_wout), (d_bada, nm_bada, nv_bada), (d_bin, nm_bin, nv_bin), \
        (d_conv, nm_conv, nv_conv), (d_bout, nm_bout, nv_bout), (d_lng, nm_lng, nv_lng), (d_lnb, nm_lnb, nv_lnb) = upd

    ex = lambda a: a.reshape((1,) + a.shape)
    grads = [ex(g_wada), g_bada, ex(g_win), g_bin, ex(g_conv), ex(g_wpa), ex(g_pb), ex(g_out), g_bout, g_lng, g_lnb]
    deltas = [ex(d_wada), d_bada, ex(d_win), d_bin, ex(d_conv), ex(d_wpa), ex(d_wpb), ex(d_wout), d_bout, d_lng, d_lnb]
    new_m = [ex(nm_wada), nm_bada, ex(nm_win), nm_bin, ex(nm_conv), ex(nm_wpa), ex(nm_wpb), ex(nm_wout), nm_bout, nm_lng, nm_lnb]
    new_v = [ex(nv_wada), nv_bada, ex(nv_win), nv_bin, ex(nv_conv), ex(nv_wpa), ex(nv_wpb), ex(nv_wout), nv_bout, nv_lng, nv_lnb]
    return (loss, grad_x.reshape(x.shape), *grads, *deltas, *new_m, *new_v)
```

```python
import jax
import jax.numpy as jnp
from jax import lax
from jax.experimental import pallas as pl
from jax.experimental.pallas import tpu as pltpu

F32, BF16 = jnp.float32, jnp.bfloat16
MESH = pl.DeviceIdType.MESH
N_DEV = 8
D = 1024
SLAB = 256
N_QKV, N_REST = 9, 25
N_SLAB = N_QKV + N_REST
D_IN = N_SLAB * SLAB
DP_SLABS = 36
BLK = 128
WAYS = 4
GROUPS = ((128, 1), (512, 4), (2048, 16))
ALPHA = 2.0 ** 0.25
LN_EPS = 1e-5
LR, B1, B2, EPS, WD, STEP = 0.001, 0.9, 0.999, 1e-08, 0.01, 10
R_ZA, R_UX, R_GB, R_GC, R_ZC, R_GA, R_GBM = 0, 1, 5, 9, 13, 17, 21
P_BIN, P_BOUT, P_LNG, P_LNB, P_CONV, P_LOSS, P_DADA = 0, 8704, 9728, 10752, 11776, 14848, 14976
MIB = 1024 * 1024


def _pcall(body, *, out_shape, out_specs=None, **kw):
    def pin_out(shape, spec):
        in_hbm = getattr(spec, "block_shape", None) is not None or getattr(spec, "memory_space", None) is pl.ANY
        return pltpu.HBM(shape.shape, shape.dtype) if in_hbm and isinstance(shape, jax.ShapeDtypeStruct) else shape

    n_scalar = 0
    if out_specs is None:
        specs = kw["grid_spec"].out_specs
        n_scalar = kw["grid_spec"].num_scalar_prefetch
    else:
        kw["out_specs"] = specs = out_specs
    if isinstance(out_shape, (tuple, list)):
        out_shape = tuple(pin_out(s, p) for s, p in zip(out_shape, specs))
    else:
        out_shape = pin_out(out_shape, specs)
    call = pl.pallas_call(body, out_shape=out_shape, **kw)

    def run(*operands):
        def pin(o):
            is_data = jnp.issubdtype(o.dtype, jnp.floating) or jnp.issubdtype(o.dtype, jnp.integer)
            return pltpu.with_memory_space_constraint(o, pltpu.HBM) if is_data else o
        return call(*operands[:n_scalar], *[pin(o) for o in operands[n_scalar:]])

    return run

ANY = pl.BlockSpec(memory_space=pl.ANY)
VMEM = pl.BlockSpec(memory_space=pltpu.VMEM)


def _whole(a):
    return pl.BlockSpec(a.shape, lambda i: (0,) * len(a.shape))


def _params(vmem_mib=None, sem=None):
    kw = {}
    if vmem_mib is not None:
        kw["vmem_limit_bytes"] = vmem_mib * MIB
    if sem is not None:
        kw["dimension_semantics"] = sem
    return pltpu.CompilerParams(**kw)


def _nn(a, b):
    return jnp.dot(a, b, preferred_element_type=F32)


def _nt(a, b):
    return lax.dot_general(a, b, (((1,), (1,)), ((), ())), preferred_element_type=F32)


def _tn(a, b):
    return lax.dot_general(a, b, (((0,), (0,)), ((), ())), preferred_element_type=F32)


def _sigmoid(v):
    return 0.5 * jnp.tanh(0.5 * v) + 0.5


def _part8(v):
    return v.reshape(v.shape[0] // 8, 8, v.shape[1]).sum(axis=0)


def _my_position():
    return lax.axis_index("x"), lax.axis_index("y"), lax.axis_index("c")


def _flat(px, py, pc):
    return 4 * px + 2 * py + pc


def _peer(mask):
    x, y, c = _my_position()
    return (x ^ ((mask >> 2) & 1), y ^ ((mask >> 1) & 1), c ^ (mask & 1))


def _column_chunks(n):
    chunks = [(128 * a, 0, 128 * a, 128) for a in range(n // 128)]
    if n % 128:
        chunks.append((n - 128, 128 - n % 128, 128 * (n // 128), n % 128))
    return chunks


def _cast_rows(w, n_steps, name):
    rows, ncol = w.shape
    blk = pl.BlockSpec((rows // n_steps, ncol), lambda i: (i, 0))

    def body(w_ref, o_ref):
        o_ref[...] = w_ref[...].astype(BF16)

    return _pcall(body, grid=(n_steps,), out_shape=jax.ShapeDtypeStruct(w.shape, BF16), in_specs=[blk], out_specs=blk,
                  name=name, compiler_params=_params(16, ("parallel",)))(w)


def _prep(w_pa, w_pb, w_out, c, conv_w):
    def body(wpa_ref, wpb_ref, wout_ref, c_ref, cw_ref, wpat_ref, wpb_o, wout_o, cact_ref, cwp_ref):
        wpat_ref[...] = wpa_ref[...].T.astype(BF16)
        wpb_o[...] = wpb_ref[...].astype(BF16)
        wout_o[...] = wout_ref[...].astype(BF16)
        cv = c_ref[...]
        cact_ref[...] = jnp.zeros_like(cact_ref)
        cact_ref[pl.ds(0, cv.shape[0]), :] = cv * _sigmoid(cv)
        cwp_ref[...] = jnp.zeros_like(cwp_ref)
        cwp_ref[pl.ds(0, 3), :] = cw_ref[...]

    out_shape = (jax.ShapeDtypeStruct((w_pa.shape[1], w_pa.shape[0]), BF16),
                 jax.ShapeDtypeStruct(w_pb.shape, BF16), jax.ShapeDtypeStruct(w_out.shape, BF16),
                 jax.ShapeDtypeStruct((8, D), F32), jax.ShapeDtypeStruct((8, conv_w.shape[1]), F32))
    operands = (w_pa, w_pb, w_out, c, conv_w)
    return _pcall(body, grid=(1,), out_shape=out_shape, in_specs=[_whole(a) for a in operands],
                  out_specs=tuple(_whole(o) for o in out_shape), name="prep", compiler_params=_params(16))(*operands)


def _exchange_slots(out_refs, send_sems, recv_sems, base=0):
    me = _flat(*_my_position())

    def copy(a, mask, slot):
        return pltpu.make_async_remote_copy(
            src_ref=out_refs[a].at[slot], dst_ref=out_refs[a].at[slot], send_sem=send_sems.at[base + 7 * a + mask - 1],
            recv_sem=recv_sems.at[base + 7 * a + mask - 1], device_id=_peer(mask), device_id_type=MESH)

    pairs = [(a, mask) for a in range(len(out_refs)) for mask in range(1, N_DEV)]
    for a, mask in pairs:
        copy(a, mask, me).start()
    for a, mask in pairs:
        copy(a, mask, _flat(*_peer(mask))).wait_recv()
    for a, mask in pairs:
        copy(a, mask, me).wait_send()


def _ada_forward(cact_mine, cw_mine, w_ada, b_ada_mine):
    ncol = w_ada.shape[1]

    def body(c_ref, cw_ref, w_ref, b_ref, out_ref, call_ref, cwall_ref, send_sems, recv_sems):
        me = _flat(*_my_position())
        call_ref[me] = c_ref[...]
        cwall_ref[me] = cw_ref[...]
        _exchange_slots([call_ref, cwall_ref], send_sems, recv_sems)
        c_all = call_ref[...].reshape(N_DEV * 8, D).astype(BF16)
        out_ref[me] = (_nn(c_all, w_ref[...].astype(BF16)) + b_ref[...]).reshape(N_DEV, 8, ncol)
        _exchange_slots([out_ref], send_sems, recv_sems, base=14)

    operands = (cact_mine, cw_mine, w_ada, b_ada_mine)
    out_shape = (jax.ShapeDtypeStruct((N_DEV, N_DEV, 8, ncol), F32), jax.ShapeDtypeStruct((N_DEV, 8, D), F32),
                 jax.ShapeDtypeStruct((N_DEV,) + cw_mine.shape, F32))
    return _pcall(body, grid=(1,), out_shape=out_shape, in_specs=[_whole(a) for a in operands], out_specs=(VMEM,) * 3,
                  scratch_shapes=[pltpu.SemaphoreType.DMA((21,)), pltpu.SemaphoreType.DMA((21,))], name="ada_forward",
                  compiler_params=_params(16))(*operands)


def _small_reduce(gb_rest, gb_qkv, svec, dgate, dss):
    nbat = dgate.shape[0]

    def body(gbr_ref, q0_ref, q1_ref, q2_ref, sv_ref, dg_ref, dss_ref, rows_ref, tot_ref, gbada_ref, send_sems, recv_sems):
        me = _flat(*_my_position())

        def put(off, v):
            rows_ref[me, :, pl.ds(off, v.shape[1])] = v

        def row(v):
            return jnp.sum(v, axis=0, keepdims=True)

        for g, q_ref in enumerate((q0_ref, q1_ref, q2_ref)):
            for which in range(3):
                put(P_BIN + SLAB * (3 * which + g), row(q_ref[which]))
        for s in range(N_REST):
            put(P_BIN + SLAB * (N_QKV + s), row(gbr_ref[s]))
        put(P_LNG, row(sv_ref[0]))
        put(P_LNB, row(sv_ref[1]))
        put(P_BOUT, row(sv_ref[2]))
        for j in range(3):
            put(P_CONV + D * j, row(sv_ref[3 + j]))
        loss = (0.5 / D) * jnp.sum(row(sv_ref[6]), axis=1, keepdims=True)
        put(P_LOSS, jnp.broadcast_to(loss, (1, 128)))
        for b in range(nbat):
            put(P_DADA + 3 * D * b, row(dss_ref[b, 0]))
            put(P_DADA + 3 * D * b + D, row(dss_ref[b, 1]))
            put(P_DADA + 3 * D * b + 2 * D, row(dg_ref[b]))
        _exchange_slots([rows_ref], send_sems, recv_sems)
        tot = rows_ref[0]
        for k in range(1, N_DEV):
            tot = tot + rows_ref[k]
        tot_ref[...] = tot
        gbada = tot[:, P_DADA:P_DADA + 3 * D]
        for b in range(1, nbat):
            gbada = gbada + tot[:, P_DADA + 3 * D * b:P_DADA + 3 * D * (b + 1)]
        gbada_ref[...] = gbada

    p_len = P_DADA + nbat * 3 * D
    out_shape = (jax.ShapeDtypeStruct((N_DEV, 1, p_len), F32), jax.ShapeDtypeStruct((1, p_len), F32),
                 jax.ShapeDtypeStruct((1, 3 * D), F32))
    operands = (gb_rest, *gb_qkv, svec, dgate, dss)
    return _pcall(body, grid=(1,), out_shape=out_shape, in_specs=[_whole(a) for a in operands],
                  out_specs=(VMEM, _whole(out_shape[1]), _whole(out_shape[2])),
                  scratch_shapes=[pltpu.SemaphoreType.DMA((7,)), pltpu.SemaphoreType.DMA((7,))], name="small_reduce",
                  compiler_params=_params(16))(*operands)


PIECE = 64
N_CHUNK = 4
ARRIVAL_RANK = (0, 1, 3, 5, 2, 4, 6, 7)
SLOT_MASK = (1, 4, 2, 6, 5, 3, 7)


def _arrival_tables(shard_rows):
    import numpy as np
    crow = shard_rows // N_CHUNK
    table = np.zeros((N_DEV, N_SLAB + 7 * N_CHUNK), np.int32)
    lo = [(SLAB * j) // crow for j in range(N_SLAB)]
    hi = [(SLAB * j + SLAB - 1) // crow for j in range(N_SLAB)]
    for k in range(N_DEV):
        def rank(chunk):
            shard_rank = ARRIVAL_RANK[(chunk // N_CHUNK) ^ k]
            return shard_rank if shard_rank < 2 else 2 + 8 * (chunk % N_CHUNK) + shard_rank
        order = sorted(range(N_SLAB), key=lambda j: (max(rank(lo[j]), rank(hi[j])), j))
        table[k, :N_SLAB] = order
        for slot, mask in enumerate(SLOT_MASK):
            for ch in range(N_CHUNK):
                chunk = (k ^ mask) * N_CHUNK + ch
                table[k, N_SLAB + slot * N_CHUNK + ch] = min(t for t, j in enumerate(order) if lo[j] <= chunk <= hi[j])
    return table


def _project_gather(shard, x, ada, b_in3, others, xt=512):
    t = x.shape[0]
    n_o = len(others)
    srows = shard.shape[0]
    crow = srows // N_CHUNK
    shards = [shard] + list(others)
    table = jnp.asarray(_arrival_tables(srows))
    seq_tiles = (t // ada.shape[0]) // xt

    def body(tbl_ref, *refs):
        srcs = [refs[0]] + list(refs[4:4 + n_o])
        x_ref, ada_ref, b_ref = refs[1], refs[2], refs[3]
        outs = [refs[4 + n_o]] + list(refs[8 + n_o:8 + 2 * n_o])
        qkv_ref, rest_ref, h_out = refs[5 + n_o], refs[6 + n_o], refs[7 + n_o]
        (wtile, obf, of32, h_ref, xbuf, send_sems, recv_sems, local_sems, tile_sems, obf_sems, of32_sems, x_sems,
         h_sems) = refs[8 + 2 * n_o:]
        w_full = outs[0]
        x, y, c = _my_position()
        k = _flat(x, y, c)
        me, sibling = (x, y, c), (x, y, 1 - c)
        chips = [(1 - x, y), (x, 1 - y), (1 - x, 1 - y)]

        def rows(a, px, py, pc, ch):
            r = shards[a].shape[0]
            if ch is None:
                return outs[a].at[pl.ds(pl.multiple_of(_flat(px, py, pc) * r, r), r), :]
            return outs[a].at[pl.ds(pl.multiple_of(_flat(px, py, pc) * r + ch * crow, crow), crow), :]

        def copy(a, slot, block, to, ch=None, src=None):
            sem = slot * N_CHUNK + ch if a == 0 else 7 * (N_CHUNK - 1 + a) + slot
            if src is not None and ch is not None:
                src = src.at[pl.ds(ch * crow, crow), :]
            return pltpu.make_async_remote_copy(
                src_ref=rows(a, *block, ch) if src is None else src, dst_ref=rows(a, *block, ch),
                send_sem=send_sems.at[sem], recv_sem=recv_sems.at[sem], device_id=to, device_id_type=MESH)

        mine = [pltpu.make_async_copy(srcs[a], rows(a, *me, None), local_sems.at[a]) for a in range(1 + n_o)]
        first = []
        for ch in range(N_CHUNK):
            first.append(copy(0, 0, me, sibling, ch, src=srcs[0]))
            first += [copy(0, 1 + j, me, (*chip, c), ch, src=srcs[0]) for j, chip in enumerate(chips)]
        for a in range(1, 1 + n_o):
            first.append(copy(a, 0, me, sibling, src=srcs[a]))
            first += [copy(a, 1 + j, me, (*chip, c), src=srcs[a]) for j, chip in enumerate(chips)]
        for cp in mine + first:
            cp.start()

        def arrive(a, slot, ch=None):
            if slot == 0:
                copy(a, 0, sibling, me, ch).wait_recv()
            elif slot < 4:
                copy(a, slot, (*chips[slot - 1], c), me, ch).wait_recv()
                copy(a, slot + 3, (*chips[slot - 1], c), sibling, ch).start()
            else:
                copy(a, slot, (*chips[slot - 4], 1 - c), me, ch).wait_recv()

        def arrive_for(step):
            for slot in range(7):
                for ch in range(N_CHUNK):
                    @pl.when(tbl_ref[k, N_SLAB + slot * N_CHUNK + ch] == step)
                    def _():
                        arrive(0, slot, ch)

        def fetch(step, buf):
            slab = tbl_ref[k, step]
            for p in range(SLAB // PIECE):
                g0 = slab * SLAB + PIECE * p
                own = (g0 >= k * srows) & (g0 < (k + 1) * srows)
                dst = wtile.at[buf, pl.ds(PIECE * p, PIECE), :]

                @pl.when(own)
                def _():
                    pltpu.make_async_copy(srcs[0].at[pl.ds(pl.multiple_of(g0 - k * srows, PIECE), PIECE), :], dst, tile_sems.at[buf]).start()

                @pl.when(jnp.logical_not(own))
                def _():
                    pltpu.make_async_copy(w_full.at[pl.ds(pl.multiple_of(g0, PIECE), PIECE), :], dst, tile_sems.at[buf]).start()

        def wait_tile(buf):
            pltpu.make_async_copy(w_full.at[pl.ds(0, SLAB), :], wtile.at[buf], tile_sems.at[buf]).wait()

        def put(buf_ref, sems, dst_ref, count, value):
            b = count % 2

            @pl.when(count >= 2)
            def _():
                pltpu.make_async_copy(buf_ref.at[b], dst_ref, sems.at[b]).wait()

            buf_ref[b] = value
            pltpu.make_async_copy(buf_ref.at[b], dst_ref, sems.at[b]).start()

        def drain(buf_ref, sems, dst_ref, count):
            for back in (1, 2):
                @pl.when(count >= back)
                def _():
                    pltpu.make_async_copy(buf_ref.at[(count - back) % 2], dst_ref, sems.at[(count - back) % 2]).wait()

        def x_copy(i):
            return pltpu.make_async_copy(x_ref.at[pl.ds(xt * i, xt), :], xbuf.at[i % 2], x_sems.at[i % 2])

        def h_copy(i):
            return pltpu.make_async_copy(h_ref.at[pl.ds(xt * i, xt), :], h_out.at[pl.ds(xt * i, xt), :], h_sems.at[i % 2])

        x_copy(0).start()
        for i in range(t // xt):
            if i + 1 < t // xt:
                x_copy(i + 1).start()
            x_copy(i).wait()
            b = i // seq_tiles
            h_ref[pl.ds(xt * i, xt), :] = (xbuf[i % 2] * (1.0 + ada_ref[b, 1:2, :]) + ada_ref[b, 0:1, :]).astype(BF16)
            if i >= 2:
                h_copy(i - 2).wait()
            h_copy(i).start()
        for i in range(max(t // xt - 2, 0), t // xt):
            h_copy(i).wait()

        arrive_for(0)
        fetch(0, 0)

        def step(s, carry):
            n_bf, n_f32 = carry
            buf = s % 2

            @pl.when(s + 1 < N_SLAB)
            def _():
                arrive_for(s + 1)
                fetch(s + 1, 1 - buf)

            wait_tile(buf)
            slab = tbl_ref[k, s]
            v = _nt(h_ref[...], wtile[buf]) + b_ref[slab]
            is_qkv = slab < N_QKV

            @pl.when(is_qkv)
            def _():
                put(obf, obf_sems, qkv_ref.at[jnp.minimum(slab, N_QKV - 1)], n_bf, v.astype(BF16))

            @pl.when(jnp.logical_not(is_qkv))
            def _():
                put(of32, of32_sems, rest_ref.at[jnp.maximum(slab - N_QKV, 0)], n_f32, v)

            return n_bf + is_qkv.astype(jnp.int32), n_f32 + 1 - is_qkv.astype(jnp.int32)

        n_bf, n_f32 = lax.fori_loop(0, N_SLAB, step, (jnp.int32(0), jnp.int32(0)))
        drain(obf, obf_sems, qkv_ref.at[0], n_bf)
        drain(of32, of32_sems, rest_ref.at[0], n_f32)

        for slots in ((1, 2, 3), (0, 4, 5, 6)):
            for a in range(1, 1 + n_o):
                for slot in slots:
                    arrive(a, slot)
        for cp in first:
            cp.wait_send()
        for j, chip in enumerate(chips):
            for ch in range(N_CHUNK):
                copy(0, 4 + j, (*chip, c), sibling, ch).wait_send()
            for a in range(1, 1 + n_o):
                copy(a, 4 + j, (*chip, c), sibling).wait_send()
        for cp in mine:
            cp.wait()

    out_shape = ((jax.ShapeDtypeStruct((N_DEV * srows, D), BF16), jax.ShapeDtypeStruct((N_QKV, t, SLAB), BF16),
                  jax.ShapeDtypeStruct((N_REST, t, SLAB), F32), jax.ShapeDtypeStruct((t, D), BF16))
                 + tuple(jax.ShapeDtypeStruct((N_DEV * o.shape[0], o.shape[1]), o.dtype) for o in others))
    n_all = 1 + n_o
    n_sems = 7 * (N_CHUNK + n_o)
    pair = pltpu.SemaphoreType.DMA((2,))
    grid_spec = pltpu.PrefetchScalarGridSpec(
        num_scalar_prefetch=1, grid=(1,),
        in_specs=[ANY, ANY, pl.BlockSpec(ada.shape, lambda i, tbl: (0, 0, 0)),
                  pl.BlockSpec((N_SLAB, 1, SLAB), lambda i, tbl: (0, 0, 0))] + [ANY] * n_o,
        out_specs=(ANY,) * (4 + n_o),
        scratch_shapes=[pltpu.VMEM((2, SLAB, D), BF16), pltpu.VMEM((2, t, SLAB), BF16), pltpu.VMEM((2, t, SLAB), F32),
                        pltpu.VMEM((t, D), BF16), pltpu.VMEM((2, xt, D), F32),
                        pltpu.SemaphoreType.DMA((n_sems,)), pltpu.SemaphoreType.DMA((n_sems,)),
                        pltpu.SemaphoreType.DMA((n_all,)), pair, pair, pair, pair, pair])
    res = _pcall(body, grid_spec=grid_spec, out_shape=out_shape, name="project_gather",
                 compiler_params=_params(48, ("arbitrary",)))(table, shard, x, ada, b_in3, *others)
    return res[0], res[1], res[2], res[3], list(res[4:])


def _bias_tables(g):
    window, dil = GROUPS[g]
    span = window // dil
    qi = jnp.arange(BLK)[:, None]
    kj = jnp.arange(2 * BLK)[None, :]
    delta = qi + BLK - kj
    valid = (delta >= 0) & (delta <= span)
    heads = jnp.arange(4, dtype=F32) + 4.0 * g
    slopes = 2.0 ** (-8.0 * (heads + 1.0) / 12.0)
    bias = -slopes[:, None, None] * (delta * dil).astype(F32)[None]
    return jnp.where(valid[None], bias, -1e30).reshape(4 * BLK, 2 * BLK)


def _head_masks(shape):
    lane = lax.broadcasted_iota(jnp.int32, shape, 1)
    return [(lane >= 64 * h) & (lane < 64 * (h + 1)) for h in range(4)]


def _stack_heads(v, masks):
    return jnp.concatenate([jnp.where(masks[h], v, jnp.zeros_like(v)) for h in range(4)], axis=0)


def _unstack_heads(v4, masks):
    out = jnp.where(masks[0], v4[0:BLK], 0.0)
    for h in range(1, 4):
        out = jnp.where(masks[h], v4[BLK * h:BLK * (h + 1)], out)
    return out


def _by_residue(v, n, dil):
    return jnp.swapaxes(v.reshape(n, dil, 128), 0, 1).reshape(n * dil, 128) if dil > 1 else v


def _by_token(v, n, dil):
    return jnp.swapaxes(v.reshape(dil, n, 128), 0, 1).reshape(n * dil, 128) if dil > 1 else v


def _regroup(load_half, dst_ref, stage_ref, n, dil):
    if dil >= 8:
        for hlf in range(2):
            dst_ref[:, pl.ds(128 * hlf, 128)] = _by_residue(load_half(hlf), n, dil).astype(dst_ref.dtype)
        return
    for hlf in range(2):
        stage_ref[hlf] = load_half(hlf)
    for r in range(dil):
        for hlf in range(2):
            dst_ref[pl.ds(r * n, n), pl.ds(128 * hlf, 128)] = stage_ref[hlf, pl.ds(r, n, stride=dil), :].astype(dst_ref.dtype)


def _store_block(res_ref, r, i, val, n):
    for hlf in range(2):
        res_ref[hlf, pl.ds(pl.multiple_of(r * n + i * BLK, BLK), BLK), :] = val[:, 128 * hlf:128 * (hlf + 1)]


def _for_blocks(block, dil, nblk, ways=WAYS):
    ways = min(ways, max(dil, nblk))
    if dil == 1:
        for i in range(ways):
            block(0, i, i == 0)

        def step(k, carry):
            for j in range(ways):
                block(0, ways * k + j, False)
            return carry

        lax.fori_loop(1, nblk // ways, step, 0)
    else:
        ways = min(ways, dil)

        def residues(k, carry):
            for j in range(ways):
                block(ways * k + j, 0, True)
            if nblk > 1:
                def loop(i, c):
                    for j in range(ways):
                        block(ways * k + j, i, False)
                    return c
                lax.fori_loop(1, nblk, loop, 0)
            return carry

        lax.fori_loop(0, dil // ways, residues, 0)


def _attn_forward(qkv, nbat):
    t = qkv.shape[1]
    seq = t // nbat
    n_grp = len(GROUPS)

    def body(qkv_ref, b0_ref, b1_ref, b2_ref, ol_ref, stage, qs_ref, ks_ref, vs_ref, *nat):
        masks = _head_masks((BLK, SLAB))
        bias_refs = (b0_ref, b1_ref, b2_ref)
        for g, (_, dil) in enumerate(GROUPS):
            n = seq // dil
            bias_ref, nat_o, nat_l = bias_refs[g], nat[2 * g], nat[2 * g + 1]
            if dil > 1:
                qd, kd, vd = qs_ref, ks_ref, vs_ref
                for which, dst in enumerate((qd, kd, vd)):
                    _regroup(lambda hlf, which=which, g=g: qkv_ref[3 * which + g, :, pl.ds(128 * hlf, 128)].astype(F32), dst, stage, n, dil)
            else:
                qd, kd, vd = qkv_ref.at[g], qkv_ref.at[3 + g], qkv_ref.at[6 + g]

            def block(r, i, first, n=n, dil=dil, qd=qd, kd=kd, vd=vd, bias_ref=bias_ref, nat_o=nat_o, nat_l=nat_l):
                base = r * n
                qs = pl.ds(pl.multiple_of(base + i * BLK, BLK), BLK)
                ks = pl.ds(pl.multiple_of(base, BLK), BLK) if first else pl.ds(pl.multiple_of(base + (i - 1) * BLK, BLK), 2 * BLK)
                q, kk, vv = qd[qs, :], kd[ks, :], vd[ks, :]
                bias = bias_ref[:, pl.ds(BLK, BLK)] if first else bias_ref[...]
                s = _nt(_stack_heads(q, masks), kk) * 0.125 + bias
                m = jnp.max(s, axis=1, keepdims=True)
                p = jnp.exp(s - m)
                den = jnp.sum(p, axis=1, keepdims=True)
                out = _unstack_heads(_nn((p * (1.0 / den)).astype(BF16), vv), masks)
                lse = _unstack_heads(jnp.broadcast_to(m + jnp.log(den), (4 * BLK, SLAB)), masks)
                _store_block(nat_o, r, i, out, n)
                _store_block(nat_l, r, i, lse, n)

            _for_blocks(block, dil, n // BLK, ways=2 * WAYS)

        def tokens(k, hlf):
            dil = GROUPS[k // 2][1]
            return _by_token(nat[k][hlf], seq // dil, dil)

        for hlf in range(2):
            l0, l1, l2 = tokens(1, hlf), tokens(3, hlf), tokens(5, hlf)
            mx = jnp.maximum(jnp.maximum(l0, l1), l2)
            e0, e1, e2 = jnp.exp(l0 - mx), jnp.exp(l1 - mx), jnp.exp(l2 - mx)
            den = e0 + e1 + e2
            ol_ref[0, :, pl.ds(128 * hlf, 128)] = (e0 * tokens(0, hlf) + e1 * tokens(2, hlf) + e2 * tokens(4, hlf)) * (1.0 / den)
            ol_ref[1, :, pl.ds(128 * hlf, 128)] = mx + jnp.log(den)

    halves = pltpu.VMEM((2, seq, 128), F32)
    bias_spec = pl.BlockSpec((4 * BLK, 2 * BLK), lambda b: (0, 0))
    return _pcall(
        body, grid=(nbat,), out_shape=jax.ShapeDtypeStruct((2, t, SLAB), F32),
        in_specs=[pl.BlockSpec((N_QKV, seq, SLAB), lambda b: (0, b, 0))] + [bias_spec] * n_grp,
        out_specs=pl.BlockSpec((2, seq, SLAB), lambda b: (0, b, 0)),
        scratch_shapes=[halves] + [pltpu.VMEM((seq, SLAB), BF16)] * 3 + [halves] * (2 * n_grp),
        name="attn_forward", compiler_params=_params(56, ("parallel",)))(qkv, *[_bias_tables(g) for g in range(n_grp)])


def _attn_backward(qkv, do_attn, ol_tot, dproj, g, nbat):
    t = qkv.shape[1]
    seq = t // nbat
    dil = GROUPS[g][1]
    n = seq // dil
    nblk = n // BLK
    qkv4 = qkv.reshape(3, 3, t, SLAB)
    dp4 = dproj.reshape(DP_SLABS // 3, 3, t, SLAB)

    def body(qkv_ref, do_ref, ol_ref, bias_ref, dp_in, dp_ref, gb_ref, dk_acc, dv_acc, *scratch):
        del dp_in
        masks = _head_masks((BLK, SLAB))

        @pl.when(pl.program_id(0) == 0)
        def _():
            gb_ref[...] = jnp.zeros_like(gb_ref)

        dk_acc[...] = jnp.zeros_like(dk_acc)
        dv_acc[...] = jnp.zeros_like(dv_acc)
        if dil > 1:
            stage, qd, kd, vd, dod, prodd, lsed, dq_res = scratch
            lanes = lambda hlf: pl.ds(128 * hlf, 128)
            for which, dst in enumerate((qd, kd, vd)):
                _regroup(lambda hlf, which=which: qkv_ref[which, 0, :, lanes(hlf)].astype(F32), dst, stage, n, dil)
            _regroup(lambda hlf: do_ref[:, lanes(hlf)].astype(F32), dod, stage, n, dil)
            _regroup(lambda hlf: do_ref[:, lanes(hlf)].astype(F32) * ol_ref[0, :, lanes(hlf)], prodd, stage, n, dil)
            _regroup(lambda hlf: ol_ref[1, :, lanes(hlf)], lsed, stage, n, dil)
        else:
            qd, kd, vd = qkv_ref.at[0, 0], qkv_ref.at[1, 0], qkv_ref.at[2, 0]

        def block(r, i, first):
            base = r * n
            qs = pl.ds(pl.multiple_of(base + i * BLK, BLK), BLK)
            ks = pl.ds(pl.multiple_of(base, BLK), BLK) if first else pl.ds(pl.multiple_of(base + (i - 1) * BLK, BLK), 2 * BLK)
            q, kk, vv = qd[qs, :], kd[ks, :], vd[ks, :]
            if dil > 1:
                do, prod, lse = dod[qs, :], prodd[qs, :], lsed[qs, :]
            else:
                do = do_ref[qs, :]
                prod = do.astype(F32) * ol_ref[0, qs, :]
                lse = ol_ref[1, qs, :]
            q4, do4 = _stack_heads(q, masks), _stack_heads(do, masks)
            bias = bias_ref[:, pl.ds(BLK, BLK)] if first else bias_ref[...]
            lse4 = jnp.concatenate([lse[:, 64 * h:64 * h + 1] for h in range(4)], axis=0)
            delta4 = jnp.concatenate([jnp.sum(jnp.where(masks[h], prod, 0.0), axis=1, keepdims=True) for h in range(4)], axis=0)
            p = jnp.exp(_nt(q4, kk) * 0.125 + bias - lse4)
            ds = (p * (_nt(do4, vv) - delta4)).astype(BF16)
            dv_acc[ks, :] += _tn(p.astype(BF16), do4)
            dk_acc[ks, :] += _tn(ds, q4) * 0.125
            dq = _unstack_heads(_nn(ds, kk), masks) * 0.125
            if dil > 1:
                _store_block(dq_res, r, i, dq, n)
            else:
                dp_ref[0, 0, qs, :] = dq.astype(BF16)
            gb_ref[0] += _part8(dq)

        _for_blocks(block, dil, nblk, ways=2 * WAYS)
        gb_ref[1] += _part8(dk_acc[...])
        gb_ref[2] += _part8(dv_acc[...])
        if dil > 1:
            for hlf in range(2):
                half = pl.ds(128 * hlf, 128)
                dp_ref[0, 0, :, half] = _by_token(dq_res[hlf], n, dil).astype(BF16)
                dp_ref[1, 0, :, half] = _by_token(dk_acc[:, half], n, dil).astype(BF16)
                dp_ref[2, 0, :, half] = _by_token(dv_acc[:, half], n, dil).astype(BF16)
        else:
            dp_ref[1, 0] = dk_acc[...].astype(BF16)
            dp_ref[2, 0] = dv_acc[...].astype(BF16)

    scratch = [pltpu.VMEM((seq, SLAB), F32)] * 2
    if dil > 1:
        halves = pltpu.VMEM((2, seq, 128), F32)
        scratch += [halves] + [pltpu.VMEM((seq, SLAB), BF16)] * 4 + [pltpu.VMEM((seq, SLAB), F32)] * 2 + [halves]
    dp, gb = _pcall(
        body, grid=(nbat,),
        out_shape=(jax.ShapeDtypeStruct(dp4.shape, BF16), jax.ShapeDtypeStruct((3, 8, SLAB), F32)),
        in_specs=[pl.BlockSpec((3, 1, seq, SLAB), lambda b: (0, g, b, 0)),
                  pl.BlockSpec((seq, SLAB), lambda b: (b, 0)),
                  pl.BlockSpec((2, seq, SLAB), lambda b: (0, b, 0)),
                  pl.BlockSpec((4 * BLK, 2 * BLK), lambda b: (0, 0)), ANY],
        out_specs=(pl.BlockSpec((3, 1, seq, SLAB), lambda b: (DP_SLABS // 9 - 1, g, b, 0)),
                   pl.BlockSpec((3, 8, SLAB), lambda b: (0, 0, 0))),
        scratch_shapes=scratch, input_output_aliases={4: 0}, name=f"attn_backward_{g}",
        compiler_params=_params(48, ("arbitrary",)))(qkv4, do_attn, ol_tot, _bias_tables(g), dp4)
    return dp.reshape(DP_SLABS, t, SLAB), gb


def _mid(rest, ol_tot, x, tgt, ada, cw, b_out, ln_g, ln_b, w_pa_t, w_pb, w_out, tm=256):
    t = x.shape[0]
    nbat = ada.shape[0]
    nt = t // tm
    tps = nt // nbat

    def body(rest_ref, halo_ref, ol_ref, x_ref, t_ref, ada_ref, cw_ref, bout_ref, lng_ref, lnb_ref,
             wpat_ref, wpb_ref, wout_ref,
             dp_ref, gx0_ref, doa_ref, mg_ref, dof_ref, bbs_ref, dyc_ref, a_ref, dya_ref,
             gbr_ref, sv_ref, dgate_ref, carry_ref, keep_ref):
        i = pl.program_id(0)
        ti = nt - 1 - i
        pos = ti % tps

        @pl.when(i == 0)
        def _():
            gbr_ref[...] = jnp.zeros_like(gbr_ref)
            sv_ref[...] = jnp.zeros_like(sv_ref)

        @pl.when(pos == tps - 1)
        def _():
            dgate_ref[...] = jnp.zeros_like(dgate_ref)
            carry_ref[...] = jnp.zeros_like(carry_ref)

        row = lax.broadcasted_iota(jnp.int32, (tm, SLAB), 0)
        halo_on = (pos > 0).astype(F32)

        def cols(s):
            return pl.ds(SLAB * s, SLAB)

        o_attn = ol_ref[0]
        z_a = rest_ref[R_ZA]
        sg_za = _sigmoid(z_a)
        a_ref[...] = (o_attn * z_a * sg_za).astype(BF16)
        y_attn = _nt(a_ref[...], wpat_ref[...])

        for s in range(4):
            u = rest_ref[R_GC + s] * rest_ref[R_UX + s]
            hu = halo_ref[R_GC + s] * halo_ref[R_UX + s] * halo_on
            u1 = jnp.where(row == 0, hu[7:8], pltpu.roll(u, 1, 0))
            u2 = jnp.where(row == 0, hu[6:7], jnp.where(row == 1, hu[7:8], pltpu.roll(u, 2, 0)))
            conv = cw_ref[0:1, cols(s)] * u2 + cw_ref[1:2, cols(s)] * u1 + cw_ref[2:3, cols(s)] * u
            zc = rest_ref[R_ZC + s]
            sg = _sigmoid(zc)
            keep_ref[2, :, cols(s)], keep_ref[3, :, cols(s)], keep_ref[4, :, cols(s)], keep_ref[5, :, cols(s)] = u1, u2, conv, sg
            bbs_ref[:, cols(s)] = (rest_ref[R_GB + s] * conv * (zc * sg)).astype(BF16)
        y_conv = _nn(bbs_ref[...], wpb_ref[...])

        for s in range(4):
            s_a, s_b = _sigmoid(rest_ref[R_GA + s]), _sigmoid(rest_ref[R_GBM + s])
            keep_ref[0, :, cols(s)], keep_ref[1, :, cols(s)] = s_a, s_b
            mg_ref[:, cols(s)] = (s_a * y_attn[:, SLAB * s:SLAB * (s + 1)] + s_b * y_conv[:, SLAB * s:SLAB * (s + 1)]).astype(BF16)
        o = _nn(mg_ref[...], wout_ref[...]) + bout_ref[...]
        gate = ada_ref[0, 2:3, :]
        r = ALPHA * x_ref[...] + gate * o
        mu = jnp.mean(r, axis=1, keepdims=True)
        rc = r - mu
        rstd = lax.rsqrt(jnp.mean(rc * rc, axis=1, keepdims=True) + LN_EPS)
        xhat = rc * rstd
        err = xhat * lng_ref[...] + lnb_ref[...] - t_ref[...]
        sv_ref[6] += _part8(err * err)
        dy = err * (1.0 / D)
        sv_ref[0] += _part8(dy * xhat)
        sv_ref[1] += _part8(dy)
        dxh = dy * lng_ref[...]
        dr = rstd * (dxh - jnp.mean(dxh, axis=1, keepdims=True) - xhat * jnp.mean(dxh * xhat, axis=1, keepdims=True))
        gx0_ref[...] = ALPHA * dr
        dgate_ref[0] += _part8(dr * o)
        do_ = dr * gate
        sv_ref[2] += _part8(do_)
        dof_ref[...] = do_.astype(BF16)
        dmerged = _nt(dof_ref[...], wout_ref[...])
        for s in range(4):
            s_a, s_b = keep_ref[0, :, cols(s)], keep_ref[1, :, cols(s)]
            dm = dmerged[:, SLAB * s:SLAB * (s + 1)]
            ya, yc = y_attn[:, SLAB * s:SLAB * (s + 1)], y_conv[:, SLAB * s:SLAB * (s + 1)]
            dya_ref[:, cols(s)] = (dm * s_a).astype(BF16)
            dyc_ref[:, cols(s)] = (dm * s_b).astype(BF16)
            dga = dm * ya * s_a * (1.0 - s_a)
            dgb = dm * yc * s_b * (1.0 - s_b)
            dp_ref[R_GA + s] = dga.astype(BF16)
            dp_ref[R_GBM + s] = dgb.astype(BF16)
            gbr_ref[R_GA + s] += _part8(dga)
            gbr_ref[R_GBM + s] += _part8(dgb)

        da = _nn(dya_ref[...], wpat_ref[...])
        doa_ref[...] = (da * z_a * sg_za).astype(BF16)
        dza = da * o_attn * (sg_za * (1.0 + z_a * (1.0 - sg_za)))
        dp_ref[R_ZA] = dza.astype(BF16)
        gbr_ref[R_ZA] += _part8(dza)

        dbb = _nt(dyc_ref[...], wpb_ref[...])
        for s in range(4):
            ux, gc, zc = rest_ref[R_UX + s], rest_ref[R_GC + s], rest_ref[R_ZC + s]
            u = gc * ux
            u1, u2, conv, sg = keep_ref[2, :, cols(s)], keep_ref[3, :, cols(s)], keep_ref[4, :, cols(s)], keep_ref[5, :, cols(s)]
            gb = rest_ref[R_GB + s]
            d_b = dbb[:, SLAB * s:SLAB * (s + 1)]
            szc = zc * sg
            dgb_ = d_b * conv * szc
            dconv = d_b * gb * szc
            dzc = d_b * gb * conv * (sg * (1.0 + zc * (1.0 - sg)))
            sv_ref[3, :, cols(s)] += _part8(dconv * u2)
            sv_ref[4, :, cols(s)] += _part8(dconv * u1)
            sv_ref[5, :, cols(s)] += _part8(dconv * u)
            nxt = carry_ref[:, cols(s)]
            d1 = jnp.where(row == tm - 1, nxt[0:1], pltpu.roll(dconv, tm - 1, 0))
            d2 = jnp.where(row == tm - 1, nxt[1:2], jnp.where(row == tm - 2, nxt[0:1], pltpu.roll(dconv, tm - 2, 0)))
            carry_ref[:, cols(s)] = dconv[0:8]
            du = cw_ref[2:3, cols(s)] * dconv + cw_ref[1:2, cols(s)] * d1 + cw_ref[0:1, cols(s)] * d2
            dgc, dux = du * ux, du * gc
            for slab, val in ((R_GB + s, dgb_), (R_ZC + s, dzc), (R_GC + s, dgc), (R_UX + s, dux)):
                dp_ref[slab] = val.astype(BF16)
                gbr_ref[slab] += _part8(val)

    def tile(i):
        return nt - 1 - i

    row_blk = lambda i: (tile(i), 0)
    slab_blk = lambda i: (0, tile(i), 0)
    const2 = lambda i: (0, 0)
    const3 = lambda i: (0, 0, 0)
    in_specs = [
        pl.BlockSpec((N_REST, tm, SLAB), slab_blk),
        pl.BlockSpec((N_REST, 8, SLAB), lambda i: (0, jnp.maximum(tile(i) * (tm // 8) - 1, 0), 0)),
        pl.BlockSpec((1, tm, SLAB), slab_blk),
        pl.BlockSpec((tm, D), row_blk), pl.BlockSpec((tm, D), row_blk),
        pl.BlockSpec((1, 3, D), lambda i: (tile(i) // tps, 0, 0)),
        pl.BlockSpec((3, D), const2), pl.BlockSpec((1, D), const2), pl.BlockSpec((1, D), const2), pl.BlockSpec((1, D), const2),
        pl.BlockSpec((D, SLAB), const2), pl.BlockSpec((D, D), const2), pl.BlockSpec((D, D), const2)]
    bf_rows = lambda: jax.ShapeDtypeStruct((t, D), BF16)
    out_shape = (
        jax.ShapeDtypeStruct((DP_SLABS, t, SLAB), BF16), jax.ShapeDtypeStruct((t, D), F32),
        jax.ShapeDtypeStruct((t, SLAB), BF16),
        bf_rows(), bf_rows(), bf_rows(), bf_rows(), jax.ShapeDtypeStruct((t, SLAB), BF16), bf_rows(),
        jax.ShapeDtypeStruct((N_REST, 8, SLAB), F32), jax.ShapeDtypeStruct((7, 8, D), F32),
        jax.ShapeDtypeStruct((nbat, 8, D), F32))
    out_specs = (
        pl.BlockSpec((N_REST, tm, SLAB), slab_blk), pl.BlockSpec((tm, D), row_blk),
        pl.BlockSpec((tm, SLAB), row_blk),
        pl.BlockSpec((tm, D), row_blk), pl.BlockSpec((tm, D), row_blk), pl.BlockSpec((tm, D), row_blk),
        pl.BlockSpec((tm, D), row_blk), pl.BlockSpec((tm, SLAB), row_blk), pl.BlockSpec((tm, D), row_blk),
        pl.BlockSpec((N_REST, 8, SLAB), const3), pl.BlockSpec((7, 8, D), const3),
        pl.BlockSpec((1, 8, D), lambda i: (tile(i) // tps, 0, 0)))
    return _pcall(body, grid=(nt,), out_shape=out_shape, in_specs=in_specs, out_specs=out_specs,
                  scratch_shapes=[pltpu.VMEM((8, D), F32), pltpu.VMEM((6, tm, D), F32)], name="mid",
                  compiler_params=_params(56, ("arbitrary",)))(
        rest, rest, ol_tot, x, tgt, ada, cw, b_out, ln_g, ln_b, w_pa_t, w_pb, w_out)


def _tn_matmul(lhs, rhs, lhs_spec, n_steps, out_rows, out_index, name, after):
    t, n = rhs.shape

    def body(l_ref, r_ref, after_ref, o_ref):
        del after_ref
        o_ref[...] = _tn(l_ref[0] if len(l_ref.shape) == 3 else l_ref[...], r_ref[...])

    return _pcall(body, grid=(n_steps,), out_shape=jax.ShapeDtypeStruct((out_rows, n), F32),
                  in_specs=[lhs_spec, pl.BlockSpec((t, n), lambda j: (0, 0)), ANY],
                  out_specs=pl.BlockSpec((SLAB, n), out_index), name=name,
                  compiler_params=_params(48, ("parallel",)))(lhs, rhs, after)


def _grad_rows_2d(lhs, rhs, name, after, tc=1024):
    t, k = lhs.shape
    n = rhs.shape[1]

    def body(l_ref, r_ref, after_ref, o_ref):
        del after_ref
        part = _tn(l_ref[...], r_ref[...])

        @pl.when(pl.program_id(0) == 0)
        def _():
            o_ref[...] = part

        @pl.when(pl.program_id(0) > 0)
        def _():
            o_ref[...] += part

    return _pcall(body, grid=(t // tc,), out_shape=jax.ShapeDtypeStruct((k, n), F32),
                  in_specs=[pl.BlockSpec((tc, k), lambda i: (i, 0)), pl.BlockSpec((tc, n), lambda i: (i, 0)), ANY],
                  out_specs=pl.BlockSpec((k, n), lambda i: (0, 0)), name=name,
                  compiler_params=_params(32, ("arbitrary",)))(lhs, rhs, after)


def _w_row_block(j):
    return (j + N_QKV) % N_SLAB


def _dp_slab(j):
    return jnp.where(j < N_REST, j, j + 2)


def _grad_w_in_t(dproj, h):
    t = h.shape[0]
    return _tn_matmul(dproj, h, pl.BlockSpec((1, t, SLAB), lambda j: (_dp_slab(j), 0, 0)), N_SLAB, D_IN,
                      lambda j: (_w_row_block(j), 0), "grad_w_in", h)


def _grad_h(dproj, w_in_t, gx0, x, ada, after, tm=512):
    t = x.shape[0]
    nbat = ada.shape[0]
    tps = (t // nbat) // tm

    def body(dp_ref, w_ref, gx0_ref, x_ref, ada_ref, after_ref, gx_ref, dss_ref):
        del after_ref
        i = pl.program_id(0)
        dh = None
        for j in range(N_SLAB):
            slab = j if j < N_REST else j + 2
            part = _nn(dp_ref[slab], w_ref[pl.ds(SLAB * ((j + N_QKV) % N_SLAB), SLAB), :])
            dh = part if dh is None else dh + part
        gx_ref[...] = gx0_ref[...] + dh * (1.0 + ada_ref[0, 1:2, :])

        @pl.when((i % tps) == 0)
        def _():
            dss_ref[...] = jnp.zeros_like(dss_ref)

        dss_ref[0, 0] += _part8(dh)
        dss_ref[0, 1] += _part8(dh * x_ref[...])

    return _pcall(
        body, grid=(t // tm,),
        out_shape=(jax.ShapeDtypeStruct((t, D), F32), jax.ShapeDtypeStruct((nbat, 2, 8, D), F32)),
        in_specs=[pl.BlockSpec((DP_SLABS, tm, SLAB), lambda i: (0, i, 0)),
                  pl.BlockSpec((D_IN, D), lambda i: (0, 0), pipeline_mode=pl.Buffered(1)),
                  pl.BlockSpec((tm, D), lambda i: (i, 0)), pl.BlockSpec((tm, D), lambda i: (i, 0)),
                  pl.BlockSpec((1, 3, D), lambda i: (i // tps, 0, 0)), ANY],
        out_specs=(pl.BlockSpec((tm, D), lambda i: (i, 0)),
                   pl.BlockSpec((1, 2, 8, D), lambda i: (i // tps, 0, 0, 0))),
        name="grad_h", compiler_params=_params(60, ("arbitrary",)))(dproj, w_in_t, gx0, x, ada, after)


def _chip(m):
    x, y, _ = _my_position()
    return (x ^ ((m >> 1) & 1), y ^ (m & 1))


def _exchange_siblings(grads, after, name):
    n = len(grads)

    def body(*refs):
        copies = _sibling_copies(refs[:n], refs[n + 1:2 * n + 1], refs[2 * n + 1], refs[2 * n + 2])
        for cp in copies:
            cp.start()
        for cp in copies:
            cp.wait()

    return _pcall(body, out_shape=tuple(_sibling_zones(grads)), in_specs=[ANY] * (n + 1), out_specs=(ANY,) * n,
                  name=name, scratch_shapes=[pltpu.SemaphoreType.DMA((4 * n,))] * 2)(*grads, after)


def _sibling_zones(grads):
    return [jax.ShapeDtypeStruct((4, g.shape[0] // N_DEV, g.shape[1]), g.dtype) for g in grads]


def _sibling_copies(srcs, lands, send_sems, recv_sems):
    x, y, c = _my_position()
    copies = []
    for a, (src, land) in enumerate(zip(srcs, lands)):
        rows = land.shape[1]
        for m in range(4):
            dev = _flat(*_chip(m), 1 - c)
            copies.append(pltpu.make_async_remote_copy(
                src_ref=src.at[pl.ds(pl.multiple_of(dev * rows, 8), rows), :], dst_ref=land.at[m],
                send_sem=send_sems.at[4 * a + m], recv_sem=recv_sems.at[4 * a + m], device_id=(x, y, 1 - c),
                device_id_type=MESH))
    return copies


def _chip_copies(srcs, lands, send_sems, recv_sems):
    _, _, c = _my_position()
    return [pltpu.make_async_remote_copy(
        src_ref=srcs[a].at[m - 1], dst_ref=lands[a].at[m - 1], send_sem=send_sems.at[3 * a + m - 1],
        recv_sem=recv_sems.at[3 * a + m - 1], device_id=(*_chip(m), c), device_id_type=MESH)
        for a in range(len(srcs)) for m in range(1, 4)]


HBM = pl.BlockSpec(memory_space=pltpu.HBM)
SEM = pl.BlockSpec(memory_space=pltpu.SEMAPHORE)
SPLIT_COPY = pltpu.CompilerParams(has_side_effects=pltpu.SideEffectType.DATAFLOW_SIDE_EFFECTING)


def _start_copies(make_copies, n_sems, srcs, zones, name):
    n = len(srcs)

    def body(*refs):
        for cp in make_copies(refs[:n], refs[n:2 * n], refs[2 * n], refs[2 * n + 1]):
            cp.start()
        refs[-1][...] = jnp.zeros_like(refs[-1])

    hbm = tuple(pltpu.HBM(b.shape, b.dtype) for b in list(srcs) + list(zones))
    out_shape = (pltpu.SemaphoreType.DMA((n_sems,)), pltpu.SemaphoreType.DMA((n_sems,))) + hbm + (jax.ShapeDtypeStruct((8, 128), F32),)
    operands = [pltpu.with_memory_space_constraint(b, pltpu.HBM) for b in srcs]
    operands += [pltpu.with_memory_space_constraint(lax.empty(z.shape, z.dtype), pltpu.HBM) for z in zones]
    res = _pcall(body, out_shape=out_shape, in_specs=[HBM] * (2 * n), out_specs=(SEM, SEM) + (HBM,) * (2 * n) + (VMEM,),
                 input_output_aliases={i: 2 + i for i in range(2 * n)}, name=name, compiler_params=SPLIT_COPY)(*operands)
    return (res[0], res[1], res[2:2 + n], res[2 + n:2 + 2 * n]), res[-1]


def _wait_copies(make_copies, flight, after, name):
    send_sems, recv_sems, srcs, zones = flight
    n = len(srcs)

    def body(*refs):
        for cp in make_copies(refs[:n], refs[n:2 * n], refs[2 * n], refs[2 * n + 1]):
            cp.wait_send()
            cp.wait_recv()

    hbm = tuple(pltpu.HBM(b.shape, b.dtype) for b in list(srcs) + list(zones))
    res = _pcall(body, out_shape=hbm, in_specs=[HBM] * (2 * n) + [SEM, SEM, ANY], out_specs=(HBM,) * (2 * n),
                 input_output_aliases={i: i for i in range(2 * n)}, name=name, compiler_params=SPLIT_COPY)(
        *srcs, *zones, send_sems, recv_sems, after)
    return res[:n], res[n:]


def _pair_sums(devs, grads, lands, n_steps, name):
    n = len(grads)
    rows = [l.shape[1] for l in lands]
    rbs = [r // n_steps for r in rows]

    def body(devs_ref, *refs):
        del devs_ref
        g_refs, land_refs, outs = refs[:4 * n], refs[4 * n:5 * n], refs[5 * n:]
        for a in range(n):
            outs[2 * a][...] = g_refs[4 * a][...] + land_refs[a][0]
            for m in range(1, 4):
                outs[2 * a + 1][m - 1] = (g_refs[4 * a + m][...] + land_refs[a][m]).astype(BF16)

    def block_of(m, per_dev):
        return lambda i, devs_ref: (devs_ref[m] * per_dev + i, 0)

    in_specs = [pl.BlockSpec((rb, l.shape[2]), block_of(m, n_steps)) for rb, l in zip(rbs, lands) for m in range(4)]
    in_specs += [pl.BlockSpec((4, rb, l.shape[2]), lambda i, devs_ref: (0, i, 0)) for rb, l in zip(rbs, lands)]
    out_shape, out_specs = [], []
    for rb, l in zip(rbs, lands):
        out_shape += [jax.ShapeDtypeStruct(l.shape[1:], F32), jax.ShapeDtypeStruct((3,) + l.shape[1:], BF16)]
        out_specs += [pl.BlockSpec((rb, l.shape[2]), lambda i, devs_ref: (i, 0)),
                      pl.BlockSpec((3, rb, l.shape[2]), lambda i, devs_ref: (0, i, 0))]
    grid_spec = pltpu.PrefetchScalarGridSpec(num_scalar_prefetch=1, grid=(n_steps,), in_specs=in_specs, out_specs=tuple(out_specs))
    res = _pcall(body, grid_spec=grid_spec, out_shape=tuple(out_shape), name=name,
                 compiler_params=_params(48, ("parallel",)))(devs, *[g for g in grads for _ in range(4)], *lands)
    return res[0::2], res[1::2]


def _final_sums(mine, lands, n_steps, name):
    n = len(mine)
    rbs = [o.shape[0] // n_steps for o in mine]

    def body(*refs):
        mine_refs, land_refs, outs = refs[:n], refs[n:2 * n], refs[2 * n:]
        for a in range(n):
            tot = mine_refs[a][...]
            for m in range(3):
                tot = tot + land_refs[a][m].astype(F32)
            outs[a][...] = tot

    in_specs = ([pl.BlockSpec((rb, o.shape[1]), lambda i: (i, 0)) for rb, o in zip(rbs, mine)]
                + [pl.BlockSpec((3, rb, o.shape[1]), lambda i: (0, i, 0)) for rb, o in zip(rbs, mine)])
    out_specs = tuple(pl.BlockSpec((rb, o.shape[1]), lambda i: (i, 0)) for rb, o in zip(rbs, mine))
    out_shape = tuple(jax.ShapeDtypeStruct(o.shape, F32) for o in mine)
    return _pcall(body, grid=(n_steps,), out_shape=out_shape, in_specs=in_specs, out_specs=out_specs, name=name,
                  compiler_params=_params(32, ("parallel",)))(*mine, *lands)


def _reduce_scatter_begin(big, small_after_start):
    c = lax.axis_index("c")
    devs = jnp.stack([_flat(*_chip(m), c) for m in range(4)]).astype(jnp.int32)
    flight, token = _start_copies(_sibling_copies, 4, [big], _sibling_zones([big]), "siblings_start")
    small = small_after_start(token)
    (big,), big_lands = _wait_copies(_sibling_copies, flight, small[-1], "siblings_wait")
    big_mine, big_send = _pair_sums(devs, [big], big_lands, 4, "pair_sums_w_in")
    big_flight, token = _start_copies(_chip_copies, 3, list(big_send), list(big_send), "chips_start_w_in")
    small_lands = _exchange_siblings(small, token, "exchange_siblings_rest")
    small_mine, small_send = _pair_sums(devs, small, small_lands, 1, "pair_sums_rest")
    small_flight, token = _start_copies(_chip_copies, 3 * len(small), list(small_send), list(small_send), "chips_start_rest")
    return (big_flight, small_flight, list(big_mine) + list(small_mine)), token


def _reduce_scatter_end(state, after):
    big_flight, small_flight, mine = state
    _, big_got = _wait_copies(_chip_copies, big_flight, after, "chips_wait_w_in")
    _, small_got = _wait_copies(_chip_copies, small_flight, after, "chips_wait_rest")
    small = _final_sums(mine[1:], small_got, 1, "final_sums_rest")
    return (mine[0], big_got[0]), list(small)


def _adamw(w, g, m, v):
    m_new = B1 * m + (1.0 - B1) * g
    v_new = B2 * v + (1.0 - B2) * (g * g)
    m_hat = m_new / (1.0 - B1 ** STEP)
    v_hat = v_new / (1.0 - B2 ** STEP)
    delta = -LR * (m_hat / (jnp.sqrt(v_hat) + EPS) + WD * w)
    return delta, m_new, v_new


def _final_sum_adam_rows(mine, land, w, m, v, n_steps, name):
    rows, ncol = w.shape
    blk = pl.BlockSpec((rows // n_steps, ncol), lambda i: (i, 0))

    def body(mine_ref, land_ref, w_ref, m_ref, v_ref, g_ref, d_ref, mo_ref, vo_ref):
        g = mine_ref[...]
        for k in range(3):
            g = g + land_ref[k].astype(F32)
        g_ref[...] = g
        d_ref[...], mo_ref[...], vo_ref[...] = _adamw(w_ref[...], g, m_ref[...], v_ref[...])

    shape = jax.ShapeDtypeStruct(w.shape, F32)
    return _pcall(body, grid=(n_steps,), out_shape=(shape,) * 4,
                  in_specs=[blk, pl.BlockSpec((3, rows // n_steps, ncol), lambda i: (0, i, 0)), blk, blk, blk],
                  out_specs=(blk,) * 4, name=name, compiler_params=_params(32, ("parallel",)))(mine, land, w, m, v)


def _adam_transposed(g_t, w, m, v, name):
    n, k = g_t.shape
    rb = min(k, 128)

    def body(gt_ref, w_ref, m_ref, v_ref, g_ref, d_ref, mo_ref, vo_ref):
        for src, skip, dst, size in _column_chunks(n):
            sl = pl.ds(dst, size)
            g = gt_ref[pl.ds(src, 128), :].T[:, skip:]
            delta, m_new, v_new = _adamw(w_ref[:, sl], g, m_ref[:, sl], v_ref[:, sl])
            g_ref[:, sl], d_ref[:, sl], mo_ref[:, sl], vo_ref[:, sl] = g, delta, m_new, v_new

    shape = jax.ShapeDtypeStruct(w.shape, F32)
    rows = pl.BlockSpec((rb, n), lambda i: (i, 0))
    return _pcall(body, grid=(k // rb,), out_shape=(shape,) * 4,
                  in_specs=[pl.BlockSpec((n, rb), lambda i: (0, i)), rows, rows, rows], out_specs=(rows,) * 4, name=name,
                  compiler_params=_params(32, ("parallel",)))(g_t, w, m, v)


def _adam_many(items, name):
    n = len(items)

    def body(*refs):
        ins, outs = refs[:4 * n], refs[4 * n:]
        for a in range(n):
            w_ref, g_ref, m_ref, v_ref = ins[4 * a:4 * a + 4]
            delta, m_new, v_new = _adamw(w_ref[...], g_ref[...], m_ref[...], v_ref[...])
            outs[3 * a][...], outs[3 * a + 1][...], outs[3 * a + 2][...] = delta, m_new, v_new

    out_shape = tuple(jax.ShapeDtypeStruct(it[0].shape, F32) for it in items for _ in range(3))
    flat = [arr for it in items for arr in it]
    res = _pcall(body, grid=(1,), out_shape=out_shape, in_specs=[_whole(a) for a in flat],
                 out_specs=tuple(_whole(o) for o in out_shape), name=name, compiler_params=_params(32))(*flat)
    return [tuple(res[3 * a:3 * a + 3]) for a in range(n)]


def _adam_w_ada(cact_all, dada_mine, w, m, v):
    def body(c_ref, d_ref, w_ref, m_ref, v_ref, g_ref, dl_ref, mo_ref, vo_ref):
        g = _tn(c_ref[...].astype(BF16), d_ref[...].astype(BF16))
        delta, m_new, v_new = _adamw(w_ref[...], g, m_ref[...], v_ref[...])
        g_ref[...], dl_ref[...], mo_ref[...], vo_ref[...] = g, delta, m_new, v_new

    shape = jax.ShapeDtypeStruct(w.shape, F32)
    operands = (cact_all, dada_mine, w, m, v)
    return _pcall(body, grid=(1,), out_shape=(shape,) * 4, in_specs=[_whole(a) for a in operands],
                  out_specs=(_whole(w),) * 4, name="adam_w_ada", compiler_params=_params(32))(*operands)


def kernel(x, c, w_ada, b_ada, w_in, b_in, conv_w, w_proj_attn, w_proj_conv, w_out, b_out, ln_g, ln_b, loss_target, m_w_ada, m_b_ada, m_w_in, m_b_in, m_conv_w, m_w_proj_attn, m_w_proj_conv, m_w_out, m_b_out, m_ln_g, m_ln_b, v_w_ada, v_b_ada, v_w_in, v_b_in, v_conv_w, v_w_proj_attn, v_w_proj_conv, v_w_out, v_b_out, v_ln_g, v_ln_b):
    nbat, seq, _ = x.shape
    t = nbat * seq
    me = _flat(*_my_position())
    x2, tgt2 = x.reshape(t, D), loss_target.reshape(t, D)
    sq = lambda a: a.reshape(a.shape[1:])

    tr = lambda a: a[0].T
    w_in_rows = tr(w_in)
    w_in_t_s = _cast_rows(w_in_rows, 4, "cast_w_in")
    w_pa_t_s, w_pb_s, w_out_s, cact_s, cw_s = _prep(sq(w_proj_attn), sq(w_proj_conv), sq(w_out), c, sq(conv_w))

    ncol = w_ada.shape[2]
    b_ada_mine = lax.dynamic_slice(b_ada, (0, me * ncol), (1, ncol))
    ada_slots, cact_slots, cw_slots = _ada_forward(cact_s, cw_s, sq(w_ada), b_ada_mine)
    cact_all = cact_slots[:, :nbat].reshape(N_DEV * nbat, D)
    cw = cw_slots[:, :3].transpose(1, 0, 2).reshape(3, D)
    ada_all = ada_slots[:, :, :nbat].transpose(1, 2, 0, 3).reshape(N_DEV * nbat, 3, D)
    ada = lax.dynamic_slice(ada_all, (me * nbat, 0, 0), (nbat, 3, D))

    w_in_t, qkv, rest, h, (w_pa_t, w_pb, w_o) = _project_gather(
        w_in_t_s, x2, ada, b_in.reshape(N_SLAB, 1, SLAB), [w_pa_t_s, w_pb_s, w_out_s])
    ol_tot = _attn_forward(qkv, nbat)
    (dproj, gx0, do_attn, merged, do_f, bbs, dyc, a_bf, dya, gb_rest, svec, dgate) = _mid(
        rest, ol_tot, x2, tgt2, ada, cw, b_out, ln_g, ln_b, w_pa_t, w_pb, w_o)

    gb_qkv = []
    for g in range(3):
        dproj, gb = _attn_backward(qkv, do_attn, ol_tot, dproj, g, nbat)
        gb_qkv.append(gb)
    g_w_in_t = _grad_w_in_t(dproj, h)

    def small_grads(token):
        g_w_out = _grad_rows_2d(merged, do_f, "grad_w_out", token)
        g_w_pb = _grad_rows_2d(bbs, dyc, "grad_w_proj_conv", g_w_out)
        g_w_pa_t = _grad_rows_2d(dya, a_bf, "grad_w_proj_attn", g_w_pb)
        return [g_w_out, g_w_pb, g_w_pa_t]

    rs_state, token = _reduce_scatter_begin(g_w_in_t, small_grads)
    grad_x, dss = _grad_h(dproj, w_in_t, gx0, x2, ada, token)

    rows8, tot, g_bada = _small_reduce(gb_rest, gb_qkv, svec, dgate, dss)
    (g_in_mine, g_in_got), (g_out, g_pb, g_pa_t) = _reduce_scatter_end(rs_state, tot)
    loss = tot[0, P_LOSS]
    dada_all = rows8[:, 0, P_DADA:].reshape(N_DEV * nbat, 3 * D)
    dada_mine = lax.dynamic_slice(dada_all, (0, me * ncol), (N_DEV * nbat, ncol))

    g_in_t, d_win_t, nm_win_t, nv_win_t = _final_sum_adam_rows(g_in_mine, g_in_got, w_in_rows, tr(m_w_in), tr(v_w_in), 4, "adam_w_in")
    g_win, d_win, nm_win, nv_win = g_in_t.T, d_win_t.T, nm_win_t.T, nv_win_t.T
    g_wpa, d_wpa, nm_wpa, nv_wpa = _adam_transposed(g_pa_t, sq(w_proj_attn), sq(m_w_proj_attn), sq(v_w_proj_attn), "adam_w_proj_attn")
    g_wada, d_wada, nm_wada, nv_wada = _adam_w_ada(cact_all, dada_mine, sq(w_ada), sq(m_w_ada), sq(v_w_ada))
    g_bin = tot[:, P_BIN:P_BIN + D_IN]
    g_bout = tot[:, P_BOUT:P_BOUT + D]
    g_lng = tot[:, P_LNG:P_LNG + D]
    g_lnb = tot[:, P_LNB:P_LNB + D]
    g_conv = lax.dynamic_slice(tot[:, P_CONV:P_CONV + 3 * D].reshape(3, D), (0, me * cw_s.shape[1]), (3, cw_s.shape[1]))
    upd = _adam_many([
        (sq(w_proj_conv), g_pb, sq(m_w_proj_conv), sq(v_w_proj_conv)),
        (sq(w_out), g_out, sq(m_w_out), sq(v_w_out)),
        (b_ada, g_bada, m_b_ada, v_b_ada), (b_in, g_bin, m_b_in, v_b_in), (sq(conv_w), g_conv, sq(m_conv_w), sq(v_conv_w)),
        (b_out, g_bout, m_b_out, v_b_out), (ln_g, g_lng, m_ln_g, v_ln_g), (ln_b, g_lnb, m_ln_b, v_ln_b)], "adam_rest")
    (d_wpb, nm_wpb, nv_wpb), (d_wout, nm_wout, nv_wout), (d_bada, nm_bada, nv_bada), (d_bin, nm_bin, nv_bin), \
        (d_conv, nm_conv, nv_conv), (d_bout, nm_bout, nv_bout), (d_lng, nm_lng, nv_lng), (d_lnb, nm_lnb, nv_lnb) = upd

    ex = lambda a: a.reshape((1,) + a.shape)
    grads = [ex(g_wada), g_bada, ex(g_win), g_bin, ex(g_conv), ex(g_wpa), ex(g_pb), ex(g_out), g_bout, g_lng, g_lnb]
    deltas = [ex(d_wada), d_bada, ex(d_win), d_bin, ex(d_conv), ex(d_wpa), ex(d_wpb), ex(d_wout), d_bout, d_lng, d_lnb]
    new_m = [ex(nm_wada), nm_bada, ex(nm_win), nm_bin, ex(nm_conv), ex(nm_wpa), ex(nm_wpb), ex(nm_wout), nm_bout, nm_lng, nm_lnb]
    new_v = [ex(nv_wada), nv_bada, ex(nv_win), nv_bin, ex(nv_conv), ex(nv_wpa), ex(nv_wpb), ex(nv_wout), nv_bout, nv_lng, nv_lnb]
    return (loss, grad_x.reshape(x.shape), *grads, *deltas, *new_m, *new_v)
```

```python
import jax
import jax.numpy as jnp
from jax import lax
from jax.experimental import pallas as pl
from jax.experimental.pallas import tpu as pltpu

F32, BF16 = jnp.float32, jnp.bfloat16
MESH = pl.DeviceIdType.MESH
N_DEV = 8
D = 1024
SLAB = 256
N_QKV, N_REST = 9, 25
N_SLAB = N_QKV + N_REST
D_IN = N_SLAB * SLAB
DP_SLABS = 36
BLK = 128
WAYS = 4
GROUPS = ((128, 1), (512, 4), (2048, 16))
ALPHA = 2.0 ** 0.25
LN_EPS = 1e-5
LR, B1, B2, EPS, WD, STEP = 0.001, 0.9, 0.999, 1e-08, 0.01, 10
R_ZA, R_UX, R_GB, R_GC, R_ZC, R_GA, R_GBM = 0, 1, 5, 9, 13, 17, 21
P_BIN, P_BOUT, P_LNG, P_LNB, P_CONV, P_LOSS, P_DADA = 0, 8704, 9728, 10752, 11776, 14848, 14976
MIB = 1024 * 1024


def _pcall(body, *, out_shape, out_specs=None, **kw):
    def pin_out(shape, spec):
        in_hbm = getattr(spec, "block_shape", None) is not None or getattr(spec, "memory_space", None) is pl.ANY
        return pltpu.HBM(shape.shape, shape.dtype) if in_hbm and isinstance(shape, jax.ShapeDtypeStruct) else shape

    n_scalar = 0
    if out_specs is None:
        specs = kw["grid_spec"].out_specs
        n_scalar = kw["grid_spec"].num_scalar_prefetch
    else:
        kw["out_specs"] = specs = out_specs
    if isinstance(out_shape, (tuple, list)):
        out_shape = tuple(pin_out(s, p) for s, p in zip(out_shape, specs))
    else:
        out_shape = pin_out(out_shape, specs)
    call = pl.pallas_call(body, out_shape=out_shape, **kw)

    def run(*operands):
        def pin(o):
            is_data = jnp.issubdtype(o.dtype, jnp.floating) or jnp.issubdtype(o.dtype, jnp.integer)
            return pltpu.with_memory_space_constraint(o, pltpu.HBM) if is_data else o
        return call(*operands[:n_scalar], *[pin(o) for o in operands[n_scalar:]])

    return run

ANY = pl.BlockSpec(memory_space=pl.ANY)
VMEM = pl.BlockSpec(memory_space=pltpu.VMEM)


def _whole(a):
    return pl.BlockSpec(a.shape, lambda i: (0,) * len(a.shape))


def _params(vmem_mib=None, sem=None):
    kw = {}
    if vmem_mib is not None:
        kw["vmem_limit_bytes"] = vmem_mib * MIB
    if sem is not None:
        kw["dimension_semantics"] = sem
    return pltpu.CompilerParams(**kw)


def _nn(a, b):
    return jnp.dot(a, b, preferred_element_type=F32)


def _nt(a, b):
    return lax.dot_general(a, b, (((1,), (1,)), ((), ())), preferred_element_type=F32)


def _tn(a, b):
    return lax.dot_general(a, b, (((0,), (0,)), ((), ())), preferred_element_type=F32)


def _sigmoid(v):
    return 0.5 * jnp.tanh(0.5 * v) + 0.5


def _part8(v):
    return v.reshape(v.shape[0] // 8, 8, v.shape[1]).sum(axis=0)


def _my_position():
    return lax.axis_index("x"), lax.axis_index("y"), lax.axis_index("c")


def _flat(px, py, pc):
    return 4 * px + 2 * py + pc


def _peer(mask):
    x, y, c = _my_position()
    return (x ^ ((mask >> 2) & 1), y ^ ((mask >> 1) & 1), c ^ (mask & 1))


def _column_chunks(n):
    chunks = [(128 * a, 0, 128 * a, 128) for a in range(n // 128)]
    if n % 128:
        chunks.append((n - 128, 128 - n % 128, 128 * (n // 128), n % 128))
    return chunks


def _cast_rows(w, n_steps, name):
    rows, ncol = w.shape
    blk = pl.BlockSpec((rows // n_steps, ncol), lambda i: (i, 0))

    def body(w_ref, o_ref):
        o_ref[...] = w_ref[...].astype(BF16)

    return _pcall(body, grid=(n_steps,), out_shape=jax.ShapeDtypeStruct(w.shape, BF16), in_specs=[blk], out_specs=blk,
                  name=name, compiler_params=_params(16, ("parallel",)))(w)


def _prep(w_in_rows, w_pa, w_pb, w_out, c, conv_w, n_steps=4):
    rows, ncol = w_in_rows.shape
    blk = pl.BlockSpec((rows // n_steps, ncol), lambda i: (i, 0))

    def body(win_ref, wpa_ref, wpb_ref, wout_ref, c_ref, cw_ref, win_o, wpat_ref, wpb_o, wout_o, cact_ref, cwp_ref):
        win_o[...] = win_ref[...].astype(BF16)

        @pl.when(pl.program_id(0) == 0)
        def _():
            wpat_ref[...] = wpa_ref[...].T.astype(BF16)
            wpb_o[...] = wpb_ref[...].astype(BF16)
            wout_o[...] = wout_ref[...].astype(BF16)
            cv = c_ref[...]
            cact_ref[...] = jnp.zeros_like(cact_ref)
            cact_ref[pl.ds(0, cv.shape[0]), :] = cv * _sigmoid(cv)
            cwp_ref[...] = jnp.zeros_like(cwp_ref)
            cwp_ref[pl.ds(0, 3), :] = cw_ref[...]

    small_shape = (jax.ShapeDtypeStruct((w_pa.shape[1], w_pa.shape[0]), BF16),
                   jax.ShapeDtypeStruct(w_pb.shape, BF16), jax.ShapeDtypeStruct(w_out.shape, BF16),
                   jax.ShapeDtypeStruct((8, D), F32), jax.ShapeDtypeStruct((8, conv_w.shape[1]), F32))
    out_shape = (jax.ShapeDtypeStruct(w_in_rows.shape, BF16),) + small_shape
    small = (w_pa, w_pb, w_out, c, conv_w)
    return _pcall(body, grid=(n_steps,), out_shape=out_shape, in_specs=[blk] + [_whole(a) for a in small],
                  out_specs=(blk,) + tuple(_whole(o) for o in small_shape), name="prep",
                  compiler_params=_params(16, ("arbitrary",)))(w_in_rows, *small)


def _exchange_slots(out_refs, send_sems, recv_sems, base=0):
    me = _flat(*_my_position())

    def copy(a, mask, slot):
        return pltpu.make_async_remote_copy(
            src_ref=out_refs[a].at[slot], dst_ref=out_refs[a].at[slot], send_sem=send_sems.at[base + 7 * a + mask - 1],
            recv_sem=recv_sems.at[base + 7 * a + mask - 1], device_id=_peer(mask), device_id_type=MESH)

    pairs = [(a, mask) for a in range(len(out_refs)) for mask in range(1, N_DEV)]
    for a, mask in pairs:
        copy(a, mask, me).start()
    for a, mask in pairs:
        copy(a, mask, _flat(*_peer(mask))).wait_recv()
    for a, mask in pairs:
        copy(a, mask, me).wait_send()


def _ada_forward(cact_mine, cw_mine, w_ada, b_ada_mine):
    ncol = w_ada.shape[1]

    def body(c_ref, cw_ref, w_ref, b_ref, out_ref, call_ref, cwall_ref, send_sems, recv_sems):
        me = _flat(*_my_position())
        call_ref[me] = c_ref[...]
        cwall_ref[me] = cw_ref[...]
        _exchange_slots([call_ref, cwall_ref], send_sems, recv_sems)
        c_all = call_ref[...].reshape(N_DEV * 8, D).astype(BF16)
        out_ref[me] = (_nn(c_all, w_ref[...].astype(BF16)) + b_ref[...]).reshape(N_DEV, 8, ncol)
        _exchange_slots([out_ref], send_sems, recv_sems, base=14)

    operands = (cact_mine, cw_mine, w_ada, b_ada_mine)
    out_shape = (jax.ShapeDtypeStruct((N_DEV, N_DEV, 8, ncol), F32), jax.ShapeDtypeStruct((N_DEV, 8, D), F32),
                 jax.ShapeDtypeStruct((N_DEV,) + cw_mine.shape, F32))
    return _pcall(body, grid=(1,), out_shape=out_shape, in_specs=[_whole(a) for a in operands], out_specs=(VMEM,) * 3,
                  scratch_shapes=[pltpu.SemaphoreType.DMA((21,)), pltpu.SemaphoreType.DMA((21,))], name="ada_forward",
                  compiler_params=_params(16))(*operands)


def _small_reduce(gb_rest, gb_qkv, svec, dgate, dss):
    nbat = dgate.shape[0]

    def body(gbr_ref, q0_ref, q1_ref, q2_ref, sv_ref, dg_ref, dss_ref, rows_ref, tot_ref, gbada_ref, send_sems, recv_sems):
        me = _flat(*_my_position())

        def put(off, v):
            rows_ref[me, :, pl.ds(off, v.shape[1])] = v

        def row(v):
            return jnp.sum(v, axis=0, keepdims=True)

        for g, q_ref in enumerate((q0_ref, q1_ref, q2_ref)):
            for which in range(3):
                put(P_BIN + SLAB * (3 * which + g), row(q_ref[which]))
        for s in range(N_REST):
            put(P_BIN + SLAB * (N_QKV + s), row(gbr_ref[s]))
        put(P_LNG, row(sv_ref[0]))
        put(P_LNB, row(sv_ref[1]))
        put(P_BOUT, row(sv_ref[2]))
        for j in range(3):
            put(P_CONV + D * j, row(sv_ref[3 + j]))
        loss = (0.5 / D) * jnp.sum(row(sv_ref[6]), axis=1, keepdims=True)
        put(P_LOSS, jnp.broadcast_to(loss, (1, 128)))
        for b in range(nbat):
            put(P_DADA + 3 * D * b, row(dss_ref[b, 0]))
            put(P_DADA + 3 * D * b + D, row(dss_ref[b, 1]))
            put(P_DADA + 3 * D * b + 2 * D, row(dg_ref[b]))
        _exchange_slots([rows_ref], send_sems, recv_sems)
        tot = rows_ref[0]
        for k in range(1, N_DEV):
            tot = tot + rows_ref[k]
        tot_ref[...] = tot
        gbada = tot[:, P_DADA:P_DADA + 3 * D]
        for b in range(1, nbat):
            gbada = gbada + tot[:, P_DADA + 3 * D * b:P_DADA + 3 * D * (b + 1)]
        gbada_ref[...] = gbada

    p_len = P_DADA + nbat * 3 * D
    out_shape = (jax.ShapeDtypeStruct((N_DEV, 1, p_len), F32), jax.ShapeDtypeStruct((1, p_len), F32),
                 jax.ShapeDtypeStruct((1, 3 * D), F32))
    operands = (gb_rest, *gb_qkv, svec, dgate, dss)
    return _pcall(body, grid=(1,), out_shape=out_shape, in_specs=[_whole(a) for a in operands],
                  out_specs=(VMEM, _whole(out_shape[1]), _whole(out_shape[2])),
                  scratch_shapes=[pltpu.SemaphoreType.DMA((7,)), pltpu.SemaphoreType.DMA((7,))], name="small_reduce",
                  compiler_params=_params(16))(*operands)


PIECE = 64
N_CHUNK = 4
ARRIVAL_RANK = (0, 1, 3, 5, 2, 4, 6, 7)
SLOT_MASK = (1, 4, 2, 6, 5, 3, 7)


def _arrival_tables(shard_rows):
    import numpy as np
    crow = shard_rows // N_CHUNK
    table = np.zeros((N_DEV, N_SLAB + 7 * N_CHUNK), np.int32)
    lo = [(SLAB * j) // crow for j in range(N_SLAB)]
    hi = [(SLAB * j + SLAB - 1) // crow for j in range(N_SLAB)]
    for k in range(N_DEV):
        def rank(chunk):
            shard_rank = ARRIVAL_RANK[(chunk // N_CHUNK) ^ k]
            return shard_rank if shard_rank < 2 else 2 + 8 * (chunk % N_CHUNK) + shard_rank
        order = sorted(range(N_SLAB), key=lambda j: (max(rank(lo[j]), rank(hi[j])), j))
        table[k, :N_SLAB] = order
        for slot, mask in enumerate(SLOT_MASK):
            for ch in range(N_CHUNK):
                chunk = (k ^ mask) * N_CHUNK + ch
                table[k, N_SLAB + slot * N_CHUNK + ch] = min(t for t, j in enumerate(order) if lo[j] <= chunk <= hi[j])
    return table


def _project_gather(shard, x, ada, b_in3, others, xt=512):
    t = x.shape[0]
    n_o = len(others)
    srows = shard.shape[0]
    crow = srows // N_CHUNK
    shards = [shard] + list(others)
    table = jnp.asarray(_arrival_tables(srows))
    seq_tiles = (t // ada.shape[0]) // xt

    def body(tbl_ref, *refs):
        srcs = [refs[0]] + list(refs[4:4 + n_o])
        x_ref, ada_ref, b_ref = refs[1], refs[2], refs[3]
        outs = [refs[4 + n_o]] + list(refs[8 + n_o:8 + 2 * n_o])
        qkv_ref, rest_ref, h_out = refs[5 + n_o], refs[6 + n_o], refs[7 + n_o]
        (wtile, obf, of32, h_ref, xbuf, send_sems, recv_sems, local_sems, tile_sems, obf_sems, of32_sems, x_sems,
         h_sems) = refs[8 + 2 * n_o:]
        w_full = outs[0]
        x, y, c = _my_position()
        k = _flat(x, y, c)
        me, sibling = (x, y, c), (x, y, 1 - c)
        chips = [(1 - x, y), (x, 1 - y), (1 - x, 1 - y)]

        def rows(a, px, py, pc, ch):
            r = shards[a].shape[0]
            if ch is None:
                return outs[a].at[pl.ds(pl.multiple_of(_flat(px, py, pc) * r, r), r), :]
            return outs[a].at[pl.ds(pl.multiple_of(_flat(px, py, pc) * r + ch * crow, crow), crow), :]

        def copy(a, slot, block, to, ch=None, src=None):
            sem = slot * N_CHUNK + ch if a == 0 else 7 * (N_CHUNK - 1 + a) + slot
            if src is not None and ch is not None:
                src = src.at[pl.ds(ch * crow, crow), :]
            return pltpu.make_async_remote_copy(
                src_ref=rows(a, *block, ch) if src is None else src, dst_ref=rows(a, *block, ch),
                send_sem=send_sems.at[sem], recv_sem=recv_sems.at[sem], device_id=to, device_id_type=MESH)

        mine = [pltpu.make_async_copy(srcs[a], rows(a, *me, None), local_sems.at[a]) for a in range(1 + n_o)]
        first = []
        for ch in range(N_CHUNK):
            first.append(copy(0, 0, me, sibling, ch, src=srcs[0]))
            first += [copy(0, 1 + j, me, (*chip, c), ch, src=srcs[0]) for j, chip in enumerate(chips)]
        for a in range(1, 1 + n_o):
            first.append(copy(a, 0, me, sibling, src=srcs[a]))
            first += [copy(a, 1 + j, me, (*chip, c), src=srcs[a]) for j, chip in enumerate(chips)]
        for cp in mine + first:
            cp.start()

        def arrive(a, slot, ch=None):
            if slot == 0:
                copy(a, 0, sibling, me, ch).wait_recv()
            elif slot < 4:
                copy(a, slot, (*chips[slot - 1], c), me, ch).wait_recv()
                copy(a, slot + 3, (*chips[slot - 1], c), sibling, ch).start()
            else:
                copy(a, slot, (*chips[slot - 4], 1 - c), me, ch).wait_recv()

        def arrive_for(step):
            for slot in range(7):
                for ch in range(N_CHUNK):
                    @pl.when(tbl_ref[k, N_SLAB + slot * N_CHUNK + ch] == step)
                    def _():
                        arrive(0, slot, ch)

        def fetch(step, buf):
            slab = tbl_ref[k, step]
            for p in range(SLAB // PIECE):
                g0 = slab * SLAB + PIECE * p
                own = (g0 >= k * srows) & (g0 < (k + 1) * srows)
                dst = wtile.at[buf, pl.ds(PIECE * p, PIECE), :]

                @pl.when(own)
                def _():
                    pltpu.make_async_copy(srcs[0].at[pl.ds(pl.multiple_of(g0 - k * srows, PIECE), PIECE), :], dst, tile_sems.at[buf]).start()

                @pl.when(jnp.logical_not(own))
                def _():
                    pltpu.make_async_copy(w_full.at[pl.ds(pl.multiple_of(g0, PIECE), PIECE), :], dst, tile_sems.at[buf]).start()

        def wait_tile(buf):
            pltpu.make_async_copy(w_full.at[pl.ds(0, SLAB), :], wtile.at[buf], tile_sems.at[buf]).wait()

        def put(buf_ref, sems, dst_ref, count, value):
            b = count % 2

            @pl.when(count >= 2)
            def _():
                pltpu.make_async_copy(buf_ref.at[b], dst_ref, sems.at[b]).wait()

            buf_ref[b] = value
            pltpu.make_async_copy(buf_ref.at[b], dst_ref, sems.at[b]).start()

        def drain(buf_ref, sems, dst_ref, count):
            for back in (1, 2):
                @pl.when(count >= back)
                def _():
                    pltpu.make_async_copy(buf_ref.at[(count - back) % 2], dst_ref, sems.at[(count - back) % 2]).wait()

        def x_copy(i):
            return pltpu.make_async_copy(x_ref.at[pl.ds(xt * i, xt), :], xbuf.at[i % 2], x_sems.at[i % 2])

        def h_copy(i):
            return pltpu.make_async_copy(h_ref.at[pl.ds(xt * i, xt), :], h_out.at[pl.ds(xt * i, xt), :], h_sems.at[i % 2])

        x_copy(0).start()
        for i in range(t // xt):
            if i + 1 < t // xt:
                x_copy(i + 1).start()
            x_copy(i).wait()
            b = i // seq_tiles
            h_ref[pl.ds(xt * i, xt), :] = (xbuf[i % 2] * (1.0 + ada_ref[b, 1:2, :]) + ada_ref[b, 0:1, :]).astype(BF16)
            if i >= 2:
                h_copy(i - 2).wait()
            h_copy(i).start()
        for i in range(max(t // xt - 2, 0), t // xt):
            h_copy(i).wait()

        arrive_for(0)
        fetch(0, 0)

        def step(s, carry):
            n_bf, n_f32 = carry
            buf = s % 2

            @pl.when(s + 1 < N_SLAB)
            def _():
                arrive_for(s + 1)
                fetch(s + 1, 1 - buf)

            wait_tile(buf)
            slab = tbl_ref[k, s]
            v = _nt(h_ref[...], wtile[buf]) + b_ref[slab]
            is_qkv = slab < N_QKV

            @pl.when(is_qkv)
            def _():
                put(obf, obf_sems, qkv_ref.at[jnp.minimum(slab, N_QKV - 1)], n_bf, v.astype(BF16))

            @pl.when(jnp.logical_not(is_qkv))
            def _():
                put(of32, of32_sems, rest_ref.at[jnp.maximum(slab - N_QKV, 0)], n_f32, v)

            return n_bf + is_qkv.astype(jnp.int32), n_f32 + 1 - is_qkv.astype(jnp.int32)

        n_bf, n_f32 = lax.fori_loop(0, N_SLAB, step, (jnp.int32(0), jnp.int32(0)))
        drain(obf, obf_sems, qkv_ref.at[0], n_bf)
        drain(of32, of32_sems, rest_ref.at[0], n_f32)

        for slots in ((1, 2, 3), (0, 4, 5, 6)):
            for a in range(1, 1 + n_o):
                for slot in slots:
                    arrive(a, slot)
        for cp in first:
            cp.wait_send()
        for j, chip in enumerate(chips):
            for ch in range(N_CHUNK):
                copy(0, 4 + j, (*chip, c), sibling, ch).wait_send()
            for a in range(1, 1 + n_o):
                copy(a, 4 + j, (*chip, c), sibling).wait_send()
        for cp in mine:
            cp.wait()

    out_shape = ((jax.ShapeDtypeStruct((N_DEV * srows, D), BF16), jax.ShapeDtypeStruct((N_QKV, t, SLAB), BF16),
                  jax.ShapeDtypeStruct((N_REST, t, SLAB), F32), jax.ShapeDtypeStruct((t, D), BF16))
                 + tuple(jax.ShapeDtypeStruct((N_DEV * o.shape[0], o.shape[1]), o.dtype) for o in others))
    n_all = 1 + n_o
    n_sems = 7 * (N_CHUNK + n_o)
    pair = pltpu.SemaphoreType.DMA((2,))
    grid_spec = pltpu.PrefetchScalarGridSpec(
        num_scalar_prefetch=1, grid=(1,),
        in_specs=[ANY, ANY, pl.BlockSpec(ada.shape, lambda i, tbl: (0, 0, 0)),
                  pl.BlockSpec((N_SLAB, 1, SLAB), lambda i, tbl: (0, 0, 0))] + [ANY] * n_o,
        out_specs=(ANY,) * (4 + n_o),
        scratch_shapes=[pltpu.VMEM((2, SLAB, D), BF16), pltpu.VMEM((2, t, SLAB), BF16), pltpu.VMEM((2, t, SLAB), F32),
                        pltpu.VMEM((t, D), BF16), pltpu.VMEM((2, xt, D), F32),
                        pltpu.SemaphoreType.DMA((n_sems,)), pltpu.SemaphoreType.DMA((n_sems,)),
                        pltpu.SemaphoreType.DMA((n_all,)), pair, pair, pair, pair, pair])
    res = _pcall(body, grid_spec=grid_spec, out_shape=out_shape, name="project_gather",
                 compiler_params=_params(48, ("arbitrary",)))(table, shard, x, ada, b_in3, *others)
    return res[0], res[1], res[2], res[3], list(res[4:])


def _bias_tables(g):
    window, dil = GROUPS[g]
    span = window // dil
    qi = jnp.arange(BLK)[:, None]
    kj = jnp.arange(2 * BLK)[None, :]
    delta = qi + BLK - kj
    valid = (delta >= 0) & (delta <= span)
    heads = jnp.arange(4, dtype=F32) + 4.0 * g
    slopes = 2.0 ** (-8.0 * (heads + 1.0) / 12.0)
    bias = -slopes[:, None, None] * (delta * dil).astype(F32)[None]
    return jnp.where(valid[None], bias, -1e30).reshape(4 * BLK, 2 * BLK)


def _head_masks(shape):
    lane = lax.broadcasted_iota(jnp.int32, shape, 1)
    return [(lane >= 64 * h) & (lane < 64 * (h + 1)) for h in range(4)]


def _stack_heads(v, masks):
    return jnp.concatenate([jnp.where(masks[h], v, jnp.zeros_like(v)) for h in range(4)], axis=0)


def _unstack_heads(v4, masks):
    out = jnp.where(masks[0], v4[0:BLK], 0.0)
    for h in range(1, 4):
        out = jnp.where(masks[h], v4[BLK * h:BLK * (h + 1)], out)
    return out


def _by_residue(v, n, dil):
    return jnp.swapaxes(v.reshape(n, dil, 128), 0, 1).reshape(n * dil, 128) if dil > 1 else v


def _by_token(v, n, dil):
    return jnp.swapaxes(v.reshape(dil, n, 128), 0, 1).reshape(n * dil, 128) if dil > 1 else v


def _regroup(load_half, dst_ref, stage_ref, n, dil):
    if dil >= 8:
        for hlf in range(2):
            dst_ref[:, pl.ds(128 * hlf, 128)] = _by_residue(load_half(hlf), n, dil).astype(dst_ref.dtype)
        return
    for hlf in range(2):
        stage_ref[hlf] = load_half(hlf)
    for r in range(dil):
        for hlf in range(2):
            dst_ref[pl.ds(r * n, n), pl.ds(128 * hlf, 128)] = stage_ref[hlf, pl.ds(r, n, stride=dil), :].astype(dst_ref.dtype)


def _store_block(res_ref, r, i, val, n):
    for hlf in range(2):
        res_ref[hlf, pl.ds(pl.multiple_of(r * n + i * BLK, BLK), BLK), :] = val[:, 128 * hlf:128 * (hlf + 1)]


def _for_blocks(block, dil, nblk, ways=WAYS):
    ways = min(ways, max(dil, nblk))
    if dil == 1:
        for i in range(ways):
            block(0, i, i == 0)

        def step(k, carry):
            for j in range(ways):
                block(0, ways * k + j, False)
            return carry

        lax.fori_loop(1, nblk // ways, step, 0)
    else:
        ways = min(ways, dil)

        def residues(k, carry):
            for j in range(ways):
                block(ways * k + j, 0, True)
            if nblk > 1:
                def loop(i, c):
                    for j in range(ways):
                        block(ways * k + j, i, False)
                    return c
                lax.fori_loop(1, nblk, loop, 0)
            return carry

        lax.fori_loop(0, dil // ways, residues, 0)


def _attn_forward(qkv, nbat):
    t = qkv.shape[1]
    seq = t // nbat
    n_grp = len(GROUPS)

    def body(qkv_ref, b0_ref, b1_ref, b2_ref, ol_ref, stage, qs_ref, ks_ref, vs_ref, *nat):
        masks = _head_masks((BLK, SLAB))
        bias_refs = (b0_ref, b1_ref, b2_ref)
        for g, (_, dil) in enumerate(GROUPS):
            n = seq // dil
            bias_ref, nat_o, nat_l = bias_refs[g], nat[2 * g], nat[2 * g + 1]
            if dil > 1:
                qd, kd, vd = qs_ref, ks_ref, vs_ref
                for which, dst in enumerate((qd, kd, vd)):
                    _regroup(lambda hlf, which=which, g=g: qkv_ref[3 * which + g, :, pl.ds(128 * hlf, 128)].astype(F32), dst, stage, n, dil)
            else:
                qd, kd, vd = qkv_ref.at[g], qkv_ref.at[3 + g], qkv_ref.at[6 + g]

            def block(r, i, first, n=n, dil=dil, qd=qd, kd=kd, vd=vd, bias_ref=bias_ref, nat_o=nat_o, nat_l=nat_l):
                base = r * n
                qs = pl.ds(pl.multiple_of(base + i * BLK, BLK), BLK)
                ks = pl.ds(pl.multiple_of(base, BLK), BLK) if first else pl.ds(pl.multiple_of(base + (i - 1) * BLK, BLK), 2 * BLK)
                q, kk, vv = qd[qs, :], kd[ks, :], vd[ks, :]
                bias = bias_ref[:, pl.ds(BLK, BLK)] if first else bias_ref[...]
                s = _nt(_stack_heads(q, masks), kk) * 0.125 + bias
                m = jnp.max(s, axis=1, keepdims=True)
                p = jnp.exp(s - m)
                den = jnp.sum(p, axis=1, keepdims=True)
                out = _unstack_heads(_nn((p * (1.0 / den)).astype(BF16), vv), masks)
                lse = _unstack_heads(jnp.broadcast_to(m + jnp.log(den), (4 * BLK, SLAB)), masks)
                _store_block(nat_o, r, i, out, n)
                _store_block(nat_l, r, i, lse, n)

            _for_blocks(block, dil, n // BLK, ways=2 * WAYS)

        def tokens(k, hlf):
            dil = GROUPS[k // 2][1]
            return _by_token(nat[k][hlf], seq // dil, dil)

        for hlf in range(2):
            l0, l1, l2 = tokens(1, hlf), tokens(3, hlf), tokens(5, hlf)
            mx = jnp.maximum(jnp.maximum(l0, l1), l2)
            e0, e1, e2 = jnp.exp(l0 - mx), jnp.exp(l1 - mx), jnp.exp(l2 - mx)
            den = e0 + e1 + e2
            ol_ref[0, :, pl.ds(128 * hlf, 128)] = (e0 * tokens(0, hlf) + e1 * tokens(2, hlf) + e2 * tokens(4, hlf)) * (1.0 / den)
            ol_ref[1, :, pl.ds(128 * hlf, 128)] = mx + jnp.log(den)

    halves = pltpu.VMEM((2, seq, 128), F32)
    bias_spec = pl.BlockSpec((4 * BLK, 2 * BLK), lambda b: (0, 0))
    return _pcall(
        body, grid=(nbat,), out_shape=jax.ShapeDtypeStruct((2, t, SLAB), F32),
        in_specs=[pl.BlockSpec((N_QKV, seq, SLAB), lambda b: (0, b, 0))] + [bias_spec] * n_grp,
        out_specs=pl.BlockSpec((2, seq, SLAB), lambda b: (0, b, 0)),
        scratch_shapes=[halves] + [pltpu.VMEM((seq, SLAB), BF16)] * 3 + [halves] * (2 * n_grp),
        name="attn_forward", compiler_params=_params(56, ("parallel",)))(qkv, *[_bias_tables(g) for g in range(n_grp)])


def _attn_backward(qkv, do_attn, ol_tot, dproj, g, nbat):
    t = qkv.shape[1]
    seq = t // nbat
    dil = GROUPS[g][1]
    n = seq // dil
    nblk = n // BLK
    qkv4 = qkv.reshape(3, 3, t, SLAB)
    dp4 = dproj.reshape(DP_SLABS // 3, 3, t, SLAB)

    def body(qkv_ref, do_ref, ol_ref, bias_ref, dp_in, dp_ref, gb_ref, dk_acc, dv_acc, *scratch):
        del dp_in
        masks = _head_masks((BLK, SLAB))

        @pl.when(pl.program_id(0) == 0)
        def _():
            gb_ref[...] = jnp.zeros_like(gb_ref)

        dk_acc[...] = jnp.zeros_like(dk_acc)
        dv_acc[...] = jnp.zeros_like(dv_acc)
        if dil > 1:
            stage, qd, kd, vd, dod, prodd, lsed, dq_res = scratch
            lanes = lambda hlf: pl.ds(128 * hlf, 128)
            for which, dst in enumerate((qd, kd, vd)):
                _regroup(lambda hlf, which=which: qkv_ref[which, 0, :, lanes(hlf)].astype(F32), dst, stage, n, dil)
            _regroup(lambda hlf: do_ref[:, lanes(hlf)].astype(F32), dod, stage, n, dil)
            _regroup(lambda hlf: do_ref[:, lanes(hlf)].astype(F32) * ol_ref[0, :, lanes(hlf)], prodd, stage, n, dil)
            _regroup(lambda hlf: ol_ref[1, :, lanes(hlf)], lsed, stage, n, dil)
        else:
            qd, kd, vd = qkv_ref.at[0, 0], qkv_ref.at[1, 0], qkv_ref.at[2, 0]

        def block(r, i, first):
            base = r * n
            qs = pl.ds(pl.multiple_of(base + i * BLK, BLK), BLK)
            ks = pl.ds(pl.multiple_of(base, BLK), BLK) if first else pl.ds(pl.multiple_of(base + (i - 1) * BLK, BLK), 2 * BLK)
            q, kk, vv = qd[qs, :], kd[ks, :], vd[ks, :]
            if dil > 1:
                do, prod, lse = dod[qs, :], prodd[qs, :], lsed[qs, :]
            else:
                do = do_ref[qs, :]
                prod = do.astype(F32) * ol_ref[0, qs, :]
                lse = ol_ref[1, qs, :]
            q4, do4 = _stack_heads(q, masks), _stack_heads(do, masks)
            bias = bias_ref[:, pl.ds(BLK, BLK)] if first else bias_ref[...]
            lse4 = jnp.concatenate([lse[:, 64 * h:64 * h + 1] for h in range(4)], axis=0)
            delta4 = jnp.concatenate([jnp.sum(jnp.where(masks[h], prod, 0.0), axis=1, keepdims=True) for h in range(4)], axis=0)
            p = jnp.exp(_nt(q4, kk) * 0.125 + bias - lse4)
            ds = (p * (_nt(do4, vv) - delta4)).astype(BF16)
            dv_acc[ks, :] += _tn(p.astype(BF16), do4)
            dk_acc[ks, :] += _tn(ds, q4) * 0.125
            dq = _unstack_heads(_nn(ds, kk), masks) * 0.125
            if dil > 1:
                _store_block(dq_res, r, i, dq, n)
            else:
                dp_ref[0, 0, qs, :] = dq.astype(BF16)
            gb_ref[0] += _part8(dq)

        _for_blocks(block, dil, nblk, ways=2 * WAYS)
        gb_ref[1] += _part8(dk_acc[...])
        gb_ref[2] += _part8(dv_acc[...])
        if dil > 1:
            for hlf in range(2):
                half = pl.ds(128 * hlf, 128)
                dp_ref[0, 0, :, half] = _by_token(dq_res[hlf], n, dil).astype(BF16)
                dp_ref[1, 0, :, half] = _by_token(dk_acc[:, half], n, dil).astype(BF16)
                dp_ref[2, 0, :, half] = _by_token(dv_acc[:, half], n, dil).astype(BF16)
        else:
            dp_ref[1, 0] = dk_acc[...].astype(BF16)
            dp_ref[2, 0] = dv_acc[...].astype(BF16)

    scratch = [pltpu.VMEM((seq, SLAB), F32)] * 2
    if dil > 1:
        halves = pltpu.VMEM((2, seq, 128), F32)
        scratch += [halves] + [pltpu.VMEM((seq, SLAB), BF16)] * 4 + [pltpu.VMEM((seq, SLAB), F32)] * 2 + [halves]
    dp, gb = _pcall(
        body, grid=(nbat,),
        out_shape=(jax.ShapeDtypeStruct(dp4.shape, BF16), jax.ShapeDtypeStruct((3, 8, SLAB), F32)),
        in_specs=[pl.BlockSpec((3, 1, seq, SLAB), lambda b: (0, g, b, 0)),
                  pl.BlockSpec((seq, SLAB), lambda b: (b, 0)),
                  pl.BlockSpec((2, seq, SLAB), lambda b: (0, b, 0)),
                  pl.BlockSpec((4 * BLK, 2 * BLK), lambda b: (0, 0)), ANY],
        out_specs=(pl.BlockSpec((3, 1, seq, SLAB), lambda b: (DP_SLABS // 9 - 1, g, b, 0)),
                   pl.BlockSpec((3, 8, SLAB), lambda b: (0, 0, 0))),
        scratch_shapes=scratch, input_output_aliases={4: 0}, name=f"attn_backward_{g}",
        compiler_params=_params(48, ("arbitrary",)))(qkv4, do_attn, ol_tot, _bias_tables(g), dp4)
    return dp.reshape(DP_SLABS, t, SLAB), gb


def _mid(rest, ol_tot, x, tgt, ada, cw, b_out, ln_g, ln_b, w_pa_t, w_pb, w_out, tm=256):
    t = x.shape[0]
    nbat = ada.shape[0]
    nt = t // tm
    tps = nt // nbat

    def body(rest_ref, halo_ref, ol_ref, x_ref, t_ref, ada_ref, cw_ref, bout_ref, lng_ref, lnb_ref,
             wpat_ref, wpb_ref, wout_ref,
             dp_ref, gx0_ref, doa_ref, mg_ref, dof_ref, bbs_ref, dyc_ref, a_ref, dya_ref,
             gbr_ref, sv_ref, dgate_ref, carry_ref, keep_ref):
        i = pl.program_id(0)
        ti = nt - 1 - i
        pos = ti % tps

        @pl.when(i == 0)
        def _():
            gbr_ref[...] = jnp.zeros_like(gbr_ref)
            sv_ref[...] = jnp.zeros_like(sv_ref)

        @pl.when(pos == tps - 1)
        def _():
            dgate_ref[...] = jnp.zeros_like(dgate_ref)
            carry_ref[...] = jnp.zeros_like(carry_ref)

        row = lax.broadcasted_iota(jnp.int32, (tm, SLAB), 0)
        halo_on = (pos > 0).astype(F32)

        def cols(s):
            return pl.ds(SLAB * s, SLAB)

        o_attn = ol_ref[0]
        z_a = rest_ref[R_ZA]
        sg_za = _sigmoid(z_a)
        a_ref[...] = (o_attn * z_a * sg_za).astype(BF16)
        y_attn = _nt(a_ref[...], wpat_ref[...])

        for s in range(4):
            u = rest_ref[R_GC + s] * rest_ref[R_UX + s]
            hu = halo_ref[R_GC + s] * halo_ref[R_UX + s] * halo_on
            u1 = jnp.where(row == 0, hu[7:8], pltpu.roll(u, 1, 0))
            u2 = jnp.where(row == 0, hu[6:7], jnp.where(row == 1, hu[7:8], pltpu.roll(u, 2, 0)))
            conv = cw_ref[0:1, cols(s)] * u2 + cw_ref[1:2, cols(s)] * u1 + cw_ref[2:3, cols(s)] * u
            zc = rest_ref[R_ZC + s]
            sg = _sigmoid(zc)
            keep_ref[2, :, cols(s)], keep_ref[3, :, cols(s)], keep_ref[4, :, cols(s)], keep_ref[5, :, cols(s)] = u1, u2, conv, sg
            bbs_ref[:, cols(s)] = (rest_ref[R_GB + s] * conv * (zc * sg)).astype(BF16)
        y_conv = _nn(bbs_ref[...], wpb_ref[...])

        for s in range(4):
            s_a, s_b = _sigmoid(rest_ref[R_GA + s]), _sigmoid(rest_ref[R_GBM + s])
            keep_ref[0, :, cols(s)], keep_ref[1, :, cols(s)] = s_a, s_b
            mg_ref[:, cols(s)] = (s_a * y_attn[:, SLAB * s:SLAB * (s + 1)] + s_b * y_conv[:, SLAB * s:SLAB * (s + 1)]).astype(BF16)
        o = _nn(mg_ref[...], wout_ref[...]) + bout_ref[...]
        gate = ada_ref[0, 2:3, :]
        r = ALPHA * x_ref[...] + gate * o
        mu = jnp.mean(r, axis=1, keepdims=True)
        rc = r - mu
        rstd = lax.rsqrt(jnp.mean(rc * rc, axis=1, keepdims=True) + LN_EPS)
        xhat = rc * rstd
        err = xhat * lng_ref[...] + lnb_ref[...] - t_ref[...]
        sv_ref[6] += _part8(err * err)
        dy = err * (1.0 / D)
        sv_ref[0] += _part8(dy * xhat)
        sv_ref[1] += _part8(dy)
        dxh = dy * lng_ref[...]
        dr = rstd * (dxh - jnp.mean(dxh, axis=1, keepdims=True) - xhat * jnp.mean(dxh * xhat, axis=1, keepdims=True))
        gx0_ref[...] = ALPHA * dr
        dgate_ref[0] += _part8(dr * o)
        do_ = dr * gate
        sv_ref[2] += _part8(do_)
        dof_ref[...] = do_.astype(BF16)
        dmerged = _nt(dof_ref[...], wout_ref[...])
        for s in range(4):
            s_a, s_b = keep_ref[0, :, cols(s)], keep_ref[1, :, cols(s)]
            dm = dmerged[:, SLAB * s:SLAB * (s + 1)]
            ya, yc = y_attn[:, SLAB * s:SLAB * (s + 1)], y_conv[:, SLAB * s:SLAB * (s + 1)]
            dya_ref[:, cols(s)] = (dm * s_a).astype(BF16)
            dyc_ref[:, cols(s)] = (dm * s_b).astype(BF16)
            dga = dm * ya * s_a * (1.0 - s_a)
            dgb = dm * yc * s_b * (1.0 - s_b)
            dp_ref[R_GA + s] = dga.astype(BF16)
            dp_ref[R_GBM + s] = dgb.astype(BF16)
            gbr_ref[R_GA + s] += _part8(dga)
            gbr_ref[R_GBM + s] += _part8(dgb)

        da = _nn(dya_ref[...], wpat_ref[...])
        doa_ref[...] = (da * z_a * sg_za).astype(BF16)
        dza = da * o_attn * (sg_za * (1.0 + z_a * (1.0 - sg_za)))
        dp_ref[R_ZA] = dza.astype(BF16)
        gbr_ref[R_ZA] += _part8(dza)

        dbb = _nt(dyc_ref[...], wpb_ref[...])
        for s in range(4):
            ux, gc, zc = rest_ref[R_UX + s], rest_ref[R_GC + s], rest_ref[R_ZC + s]
            u = gc * ux
            u1, u2, conv, sg = keep_ref[2, :, cols(s)], keep_ref[3, :, cols(s)], keep_ref[4, :, cols(s)], keep_ref[5, :, cols(s)]
            gb = rest_ref[R_GB + s]
            d_b = dbb[:, SLAB * s:SLAB * (s + 1)]
            szc = zc * sg
            dgb_ = d_b * conv * szc
            dconv = d_b * gb * szc
            dzc = d_b * gb * conv * (sg * (1.0 + zc * (1.0 - sg)))
            sv_ref[3, :, cols(s)] += _part8(dconv * u2)
            sv_ref[4, :, cols(s)] += _part8(dconv * u1)
            sv_ref[5, :, cols(s)] += _part8(dconv * u)
            nxt = carry_ref[:, cols(s)]
            d1 = jnp.where(row == tm - 1, nxt[0:1], pltpu.roll(dconv, tm - 1, 0))
            d2 = jnp.where(row == tm - 1, nxt[1:2], jnp.where(row == tm - 2, nxt[0:1], pltpu.roll(dconv, tm - 2, 0)))
            carry_ref[:, cols(s)] = dconv[0:8]
            du = cw_ref[2:3, cols(s)] * dconv + cw_ref[1:2, cols(s)] * d1 + cw_ref[0:1, cols(s)] * d2
            dgc, dux = du * ux, du * gc
            for slab, val in ((R_GB + s, dgb_), (R_ZC + s, dzc), (R_GC + s, dgc), (R_UX + s, dux)):
                dp_ref[slab] = val.astype(BF16)
                gbr_ref[slab] += _part8(val)

    def tile(i):
        return nt - 1 - i

    row_blk = lambda i: (tile(i), 0)
    slab_blk = lambda i: (0, tile(i), 0)
    const2 = lambda i: (0, 0)
    const3 = lambda i: (0, 0, 0)
    in_specs = [
        pl.BlockSpec((N_REST, tm, SLAB), slab_blk),
        pl.BlockSpec((N_REST, 8, SLAB), lambda i: (0, jnp.maximum(tile(i) * (tm // 8) - 1, 0), 0)),
        pl.BlockSpec((1, tm, SLAB), slab_blk),
        pl.BlockSpec((tm, D), row_blk), pl.BlockSpec((tm, D), row_blk),
        pl.BlockSpec((1, 3, D), lambda i: (tile(i) // tps, 0, 0)),
        pl.BlockSpec((3, D), const2), pl.BlockSpec((1, D), const2), pl.BlockSpec((1, D), const2), pl.BlockSpec((1, D), const2),
        pl.BlockSpec((D, SLAB), const2), pl.BlockSpec((D, D), const2), pl.BlockSpec((D, D), const2)]
    bf_rows = lambda: jax.ShapeDtypeStruct((t, D), BF16)
    out_shape = (
        jax.ShapeDtypeStruct((DP_SLABS, t, SLAB), BF16), jax.ShapeDtypeStruct((t, D), F32),
        jax.ShapeDtypeStruct((t, SLAB), BF16),
        bf_rows(), bf_rows(), bf_rows(), bf_rows(), jax.ShapeDtypeStruct((t, SLAB), BF16), bf_rows(),
        jax.ShapeDtypeStruct((N_REST, 8, SLAB), F32), jax.ShapeDtypeStruct((7, 8, D), F32),
        jax.ShapeDtypeStruct((nbat, 8, D), F32))
    out_specs = (
        pl.BlockSpec((N_REST, tm, SLAB), slab_blk), pl.BlockSpec((tm, D), row_blk),
        pl.BlockSpec((tm, SLAB), row_blk),
        pl.BlockSpec((tm, D), row_blk), pl.BlockSpec((tm, D), row_blk), pl.BlockSpec((tm, D), row_blk),
        pl.BlockSpec((tm, D), row_blk), pl.BlockSpec((tm, SLAB), row_blk), pl.BlockSpec((tm, D), row_blk),
        pl.BlockSpec((N_REST, 8, SLAB), const3), pl.BlockSpec((7, 8, D), const3),
        pl.BlockSpec((1, 8, D), lambda i: (tile(i) // tps, 0, 0)))
    return _pcall(body, grid=(nt,), out_shape=out_shape, in_specs=in_specs, out_specs=out_specs,
                  scratch_shapes=[pltpu.VMEM((8, D), F32), pltpu.VMEM((6, tm, D), F32)], name="mid",
                  compiler_params=_params(56, ("arbitrary",)))(
        rest, rest, ol_tot, x, tgt, ada, cw, b_out, ln_g, ln_b, w_pa_t, w_pb, w_out)


def _tn_matmul(lhs, rhs, lhs_spec, n_steps, out_rows, out_index, name, after):
    t, n = rhs.shape

    def body(l_ref, r_ref, after_ref, o_ref):
        del after_ref
        o_ref[...] = _tn(l_ref[0] if len(l_ref.shape) == 3 else l_ref[...], r_ref[...])

    return _pcall(body, grid=(n_steps,), out_shape=jax.ShapeDtypeStruct((out_rows, n), F32),
                  in_specs=[lhs_spec, pl.BlockSpec((t, n), lambda j: (0, 0)), ANY],
                  out_specs=pl.BlockSpec((SLAB, n), out_index), name=name,
                  compiler_params=_params(48, ("parallel",)))(lhs, rhs, after)


def _grad_rows_2d(lhs, rhs, name, after, tc=1024):
    t, k = lhs.shape
    n = rhs.shape[1]

    def body(l_ref, r_ref, after_ref, o_ref):
        del after_ref
        part = _tn(l_ref[...], r_ref[...])

        @pl.when(pl.program_id(0) == 0)
        def _():
            o_ref[...] = part

        @pl.when(pl.program_id(0) > 0)
        def _():
            o_ref[...] += part

    return _pcall(body, grid=(t // tc,), out_shape=jax.ShapeDtypeStruct((k, n), F32),
                  in_specs=[pl.BlockSpec((tc, k), lambda i: (i, 0)), pl.BlockSpec((tc, n), lambda i: (i, 0)), ANY],
                  out_specs=pl.BlockSpec((k, n), lambda i: (0, 0)), name=name,
                  compiler_params=_params(32, ("arbitrary",)))(lhs, rhs, after)


def _w_row_block(j):
    return (j + N_QKV) % N_SLAB


def _dp_slab(j):
    return jnp.where(j < N_REST, j, j + 2)


def _grad_w_in_t(dproj, h):
    t = h.shape[0]
    return _tn_matmul(dproj, h, pl.BlockSpec((1, t, SLAB), lambda j: (_dp_slab(j), 0, 0)), N_SLAB, D_IN,
                      lambda j: (_w_row_block(j), 0), "grad_w_in", h)


def _grad_h(dproj, w_in_t, gx0, x, ada, after, tm=512):
    t = x.shape[0]
    nbat = ada.shape[0]
    tps = (t // nbat) // tm

    def body(dp_ref, w_ref, gx0_ref, x_ref, ada_ref, after_ref, gx_ref, dss_ref):
        del after_ref
        i = pl.program_id(0)
        dh = None
        for j in range(N_SLAB):
            slab = j if j < N_REST else j + 2
            part = _nn(dp_ref[slab], w_ref[pl.ds(SLAB * ((j + N_QKV) % N_SLAB), SLAB), :])
            dh = part if dh is None else dh + part
        gx_ref[...] = gx0_ref[...] + dh * (1.0 + ada_ref[0, 1:2, :])

        @pl.when((i % tps) == 0)
        def _():
            dss_ref[...] = jnp.zeros_like(dss_ref)

        dss_ref[0, 0] += _part8(dh)
        dss_ref[0, 1] += _part8(dh * x_ref[...])

    return _pcall(
        body, grid=(t // tm,),
        out_shape=(jax.ShapeDtypeStruct((t, D), F32), jax.ShapeDtypeStruct((nbat, 2, 8, D), F32)),
        in_specs=[pl.BlockSpec((DP_SLABS, tm, SLAB), lambda i: (0, i, 0)),
                  pl.BlockSpec((D_IN, D), lambda i: (0, 0), pipeline_mode=pl.Buffered(1)),
                  pl.BlockSpec((tm, D), lambda i: (i, 0)), pl.BlockSpec((tm, D), lambda i: (i, 0)),
                  pl.BlockSpec((1, 3, D), lambda i: (i // tps, 0, 0)), ANY],
        out_specs=(pl.BlockSpec((tm, D), lambda i: (i, 0)),
                   pl.BlockSpec((1, 2, 8, D), lambda i: (i // tps, 0, 0, 0))),
        name="grad_h", compiler_params=_params(60, ("arbitrary",)))(dproj, w_in_t, gx0, x, ada, after)


def _chip(m):
    x, y, _ = _my_position()
    return (x ^ ((m >> 1) & 1), y ^ (m & 1))


def _exchange_siblings(grads, after, name):
    n = len(grads)

    def body(*refs):
        copies = _sibling_copies(refs[:n], refs[n + 1:2 * n + 1], refs[2 * n + 1], refs[2 * n + 2])
        for cp in copies:
            cp.start()
        for cp in copies:
            cp.wait()

    return _pcall(body, out_shape=tuple(_sibling_zones(grads)), in_specs=[ANY] * (n + 1), out_specs=(ANY,) * n,
                  name=name, scratch_shapes=[pltpu.SemaphoreType.DMA((4 * n,))] * 2)(*grads, after)


def _sibling_zones(grads):
    return [jax.ShapeDtypeStruct((4, g.shape[0] // N_DEV, g.shape[1]), g.dtype) for g in grads]


def _sibling_copies(srcs, lands, send_sems, recv_sems):
    x, y, c = _my_position()
    copies = []
    for a, (src, land) in enumerate(zip(srcs, lands)):
        rows = land.shape[1]
        for m in range(4):
            dev = _flat(*_chip(m), 1 - c)
            copies.append(pltpu.make_async_remote_copy(
                src_ref=src.at[pl.ds(pl.multiple_of(dev * rows, 8), rows), :], dst_ref=land.at[m],
                send_sem=send_sems.at[4 * a + m], recv_sem=recv_sems.at[4 * a + m], device_id=(x, y, 1 - c),
                device_id_type=MESH))
    return copies


def _chip_copies(srcs, lands, send_sems, recv_sems):
    _, _, c = _my_position()
    return [pltpu.make_async_remote_copy(
        src_ref=srcs[a].at[m - 1], dst_ref=lands[a].at[m - 1], send_sem=send_sems.at[3 * a + m - 1],
        recv_sem=recv_sems.at[3 * a + m - 1], device_id=(*_chip(m), c), device_id_type=MESH)
        for a in range(len(srcs)) for m in range(1, 4)]


HBM = pl.BlockSpec(memory_space=pltpu.HBM)
SEM = pl.BlockSpec(memory_space=pltpu.SEMAPHORE)
SPLIT_COPY = pltpu.CompilerParams(has_side_effects=pltpu.SideEffectType.DATAFLOW_SIDE_EFFECTING)


def _start_copies(make_copies, n_sems, srcs, zones, name):
    n = len(srcs)

    def body(*refs):
        for cp in make_copies(refs[:n], refs[n:2 * n], refs[2 * n], refs[2 * n + 1]):
            cp.start()
        refs[-1][...] = jnp.zeros_like(refs[-1])

    hbm = tuple(pltpu.HBM(b.shape, b.dtype) for b in list(srcs) + list(zones))
    out_shape = (pltpu.SemaphoreType.DMA((n_sems,)), pltpu.SemaphoreType.DMA((n_sems,))) + hbm + (jax.ShapeDtypeStruct((8, 128), F32),)
    operands = [pltpu.with_memory_space_constraint(b, pltpu.HBM) for b in srcs]
    operands += [pltpu.with_memory_space_constraint(lax.empty(z.shape, z.dtype), pltpu.HBM) for z in zones]
    res = _pcall(body, out_shape=out_shape, in_specs=[HBM] * (2 * n), out_specs=(SEM, SEM) + (HBM,) * (2 * n) + (VMEM,),
                 input_output_aliases={i: 2 + i for i in range(2 * n)}, name=name, compiler_params=SPLIT_COPY)(*operands)
    return (res[0], res[1], res[2:2 + n], res[2 + n:2 + 2 * n]), res[-1]


def _wait_copies(make_copies, flight, after, name):
    send_sems, recv_sems, srcs, zones = flight
    n = len(srcs)

    def body(*refs):
        for cp in make_copies(refs[:n], refs[n:2 * n], refs[2 * n], refs[2 * n + 1]):
            cp.wait_send()
            cp.wait_recv()

    hbm = tuple(pltpu.HBM(b.shape, b.dtype) for b in list(srcs) + list(zones))
    res = _pcall(body, out_shape=hbm, in_specs=[HBM] * (2 * n) + [SEM, SEM, ANY], out_specs=(HBM,) * (2 * n),
                 input_output_aliases={i: i for i in range(2 * n)}, name=name, compiler_params=SPLIT_COPY)(
        *srcs, *zones, send_sems, recv_sems, after)
    return res[:n], res[n:]


def _pair_sums(devs, grads, lands, n_steps, name):
    n = len(grads)
    rows = [l.shape[1] for l in lands]
    rbs = [r // n_steps for r in rows]

    def body(devs_ref, *refs):
        del devs_ref
        g_refs, land_refs, outs = refs[:4 * n], refs[4 * n:5 * n], refs[5 * n:]
        for a in range(n):
            outs[2 * a][...] = g_refs[4 * a][...] + land_refs[a][0]
            for m in range(1, 4):
                outs[2 * a + 1][m - 1] = (g_refs[4 * a + m][...] + land_refs[a][m]).astype(BF16)

    def block_of(m, per_dev):
        return lambda i, devs_ref: (devs_ref[m] * per_dev + i, 0)

    in_specs = [pl.BlockSpec((rb, l.shape[2]), block_of(m, n_steps)) for rb, l in zip(rbs, lands) for m in range(4)]
    in_specs += [pl.BlockSpec((4, rb, l.shape[2]), lambda i, devs_ref: (0, i, 0)) for rb, l in zip(rbs, lands)]
    out_shape, out_specs = [], []
    for rb, l in zip(rbs, lands):
        out_shape += [jax.ShapeDtypeStruct(l.shape[1:], F32), jax.ShapeDtypeStruct((3,) + l.shape[1:], BF16)]
        out_specs += [pl.BlockSpec((rb, l.shape[2]), lambda i, devs_ref: (i, 0)),
                      pl.BlockSpec((3, rb, l.shape[2]), lambda i, devs_ref: (0, i, 0))]
    grid_spec = pltpu.PrefetchScalarGridSpec(num_scalar_prefetch=1, grid=(n_steps,), in_specs=in_specs, out_specs=tuple(out_specs))
    res = _pcall(body, grid_spec=grid_spec, out_shape=tuple(out_shape), name=name,
                 compiler_params=_params(48, ("parallel",)))(devs, *[g for g in grads for _ in range(4)], *lands)
    return res[0::2], res[1::2]


def _final_sums(mine, lands, n_steps, name):
    n = len(mine)
    rbs = [o.shape[0] // n_steps for o in mine]

    def body(*refs):
        mine_refs, land_refs, outs = refs[:n], refs[n:2 * n], refs[2 * n:]
        for a in range(n):
            tot = mine_refs[a][...]
            for m in range(3):
                tot = tot + land_refs[a][m].astype(F32)
            outs[a][...] = tot

    in_specs = ([pl.BlockSpec((rb, o.shape[1]), lambda i: (i, 0)) for rb, o in zip(rbs, mine)]
                + [pl.BlockSpec((3, rb, o.shape[1]), lambda i: (0, i, 0)) for rb, o in zip(rbs, mine)])
    out_specs = tuple(pl.BlockSpec((rb, o.shape[1]), lambda i: (i, 0)) for rb, o in zip(rbs, mine))
    out_shape = tuple(jax.ShapeDtypeStruct(o.shape, F32) for o in mine)
    return _pcall(body, grid=(n_steps,), out_shape=out_shape, in_specs=in_specs, out_specs=out_specs, name=name,
                  compiler_params=_params(32, ("parallel",)))(*mine, *lands)


def _reduce_scatter_begin(big, small_after_start):
    c = lax.axis_index("c")
    devs = jnp.stack([_flat(*_chip(m), c) for m in range(4)]).astype(jnp.int32)
    flight, token = _start_copies(_sibling_copies, 4, [big], _sibling_zones([big]), "siblings_start")
    small = small_after_start(token)
    (big,), big_lands = _wait_copies(_sibling_copies, flight, small[-1], "siblings_wait")
    big_mine, big_send = _pair_sums(devs, [big], big_lands, 4, "pair_sums_w_in")
    big_flight, token = _start_copies(_chip_copies, 3, list(big_send), list(big_send), "chips_start_w_in")
    small_lands = _exchange_siblings(small, token, "exchange_siblings_rest")
    small_mine, small_send = _pair_sums(devs, small, small_lands, 1, "pair_sums_rest")
    small_flight, token = _start_copies(_chip_copies, 3 * len(small), list(small_send), list(small_send), "chips_start_rest")
    return (big_flight, small_flight, list(big_mine) + list(small_mine)), token


def _reduce_scatter_end(state, after):
    big_flight, small_flight, mine = state
    _, big_got = _wait_copies(_chip_copies, big_flight, after, "chips_wait_w_in")
    _, small_got = _wait_copies(_chip_copies, small_flight, after, "chips_wait_rest")
    small = _final_sums(mine[1:], small_got, 1, "final_sums_rest")
    return (mine[0], big_got[0]), list(small)


def _adamw(w, g, m, v):
    m_new = B1 * m + (1.0 - B1) * g
    v_new = B2 * v + (1.0 - B2) * (g * g)
    m_hat = m_new / (1.0 - B1 ** STEP)
    v_hat = v_new / (1.0 - B2 ** STEP)
    delta = -LR * (m_hat / (jnp.sqrt(v_hat) + EPS) + WD * w)
    return delta, m_new, v_new


def _final_sum_adam_rows(mine, land, w, m, v, n_steps, name):
    rows, ncol = w.shape
    blk = pl.BlockSpec((rows // n_steps, ncol), lambda i: (i, 0))

    def body(mine_ref, land_ref, w_ref, m_ref, v_ref, g_ref, d_ref, mo_ref, vo_ref):
        g = mine_ref[...]
        for k in range(3):
            g = g + land_ref[k].astype(F32)
        g_ref[...] = g
        d_ref[...], mo_ref[...], vo_ref[...] = _adamw(w_ref[...], g, m_ref[...], v_ref[...])

    shape = jax.ShapeDtypeStruct(w.shape, F32)
    return _pcall(body, grid=(n_steps,), out_shape=(shape,) * 4,
                  in_specs=[blk, pl.BlockSpec((3, rows // n_steps, ncol), lambda i: (0, i, 0)), blk, blk, blk],
                  out_specs=(blk,) * 4, name=name, compiler_params=_params(32, ("parallel",)))(mine, land, w, m, v)


def _adam_transposed(g_t, w, m, v, name):
    n, k = g_t.shape
    rb = min(k, 128)

    def body(gt_ref, w_ref, m_ref, v_ref, g_ref, d_ref, mo_ref, vo_ref):
        for src, skip, dst, size in _column_chunks(n):
            sl = pl.ds(dst, size)
            g = gt_ref[pl.ds(src, 128), :].T[:, skip:]
            delta, m_new, v_new = _adamw(w_ref[:, sl], g, m_ref[:, sl], v_ref[:, sl])
            g_ref[:, sl], d_ref[:, sl], mo_ref[:, sl], vo_ref[:, sl] = g, delta, m_new, v_new

    shape = jax.ShapeDtypeStruct(w.shape, F32)
    rows = pl.BlockSpec((rb, n), lambda i: (i, 0))
    return _pcall(body, grid=(k // rb,), out_shape=(shape,) * 4,
                  in_specs=[pl.BlockSpec((n, rb), lambda i: (0, i)), rows, rows, rows], out_specs=(rows,) * 4, name=name,
                  compiler_params=_params(32, ("parallel",)))(g_t, w, m, v)


def _adam_many(items, name):
    n = len(items)

    def body(*refs):
        ins, outs = refs[:4 * n], refs[4 * n:]
        for a in range(n):
            w_ref, g_ref, m_ref, v_ref = ins[4 * a:4 * a + 4]
            delta, m_new, v_new = _adamw(w_ref[...], g_ref[...], m_ref[...], v_ref[...])
            outs[3 * a][...], outs[3 * a + 1][...], outs[3 * a + 2][...] = delta, m_new, v_new

    out_shape = tuple(jax.ShapeDtypeStruct(it[0].shape, F32) for it in items for _ in range(3))
    flat = [arr for it in items for arr in it]
    res = _pcall(body, grid=(1,), out_shape=out_shape, in_specs=[_whole(a) for a in flat],
                 out_specs=tuple(_whole(o) for o in out_shape), name=name, compiler_params=_params(32))(*flat)
    return [tuple(res[3 * a:3 * a + 3]) for a in range(n)]


def _adam_w_ada(cact_all, dada_mine, w, m, v):
    def body(c_ref, d_ref, w_ref, m_ref, v_ref, g_ref, dl_ref, mo_ref, vo_ref):
        g = _tn(c_ref[...].astype(BF16), d_ref[...].astype(BF16))
        delta, m_new, v_new = _adamw(w_ref[...], g, m_ref[...], v_ref[...])
        g_ref[...], dl_ref[...], mo_ref[...], vo_ref[...] = g, delta, m_new, v_new

    shape = jax.ShapeDtypeStruct(w.shape, F32)
    operands = (cact_all, dada_mine, w, m, v)
    return _pcall(body, grid=(1,), out_shape=(shape,) * 4, in_specs=[_whole(a) for a in operands],
                  out_specs=(_whole(w),) * 4, name="adam_w_ada", compiler_params=_params(32))(*operands)


def kernel(x, c, w_ada, b_ada, w_in, b_in, conv_w, w_proj_attn, w_proj_conv, w_out, b_out, ln_g, ln_b, loss_target, m_w_ada, m_b_ada, m_w_in, m_b_in, m_conv_w, m_w_proj_attn, m_w_proj_conv, m_w_out, m_b_out, m_ln_g, m_ln_b, v_w_ada, v_b_ada, v_w_in, v_b_in, v_conv_w, v_w_proj_attn, v_w_proj_conv, v_w_out, v_b_out, v_ln_g, v_ln_b):
    nbat, seq, _ = x.shape
    t = nbat * seq
    me = _flat(*_my_position())
    x2, tgt2 = x.reshape(t, D), loss_target.reshape(t, D)
    sq = lambda a: a.reshape(a.shape[1:])

    tr = lambda a: a[0].T
    w_in_rows = tr(w_in)
    w_in_t_s, w_pa_t_s, w_pb_s, w_out_s, cact_s, cw_s = _prep(
        w_in_rows, sq(w_proj_attn), sq(w_proj_conv), sq(w_out), c, sq(conv_w))

    ncol = w_ada.shape[2]
    b_ada_mine = lax.dynamic_slice(b_ada, (0, me * ncol), (1, ncol))
    ada_slots, cact_slots, cw_slots = _ada_forward(cact_s, cw_s, sq(w_ada), b_ada_mine)
    cact_all = cact_slots[:, :nbat].reshape(N_DEV * nbat, D)
    cw = cw_slots[:, :3].transpose(1, 0, 2).reshape(3, D)
    ada_all = ada_slots[:, :, :nbat].transpose(1, 2, 0, 3).reshape(N_DEV * nbat, 3, D)
    ada = lax.dynamic_slice(ada_all, (me * nbat, 0, 0), (nbat, 3, D))

    w_in_t, qkv, rest, h, (w_pa_t, w_pb, w_o) = _project_gather(
        w_in_t_s, x2, ada, b_in.reshape(N_SLAB, 1, SLAB), [w_pa_t_s, w_pb_s, w_out_s])
    ol_tot = _attn_forward(qkv, nbat)
    (dproj, gx0, do_attn, merged, do_f, bbs, dyc, a_bf, dya, gb_rest, svec, dgate) = _mid(
        rest, ol_tot, x2, tgt2, ada, cw, b_out, ln_g, ln_b, w_pa_t, w_pb, w_o)

    gb_qkv = []
    for g in range(3):
        dproj, gb = _attn_backward(qkv, do_attn, ol_tot, dproj, g, nbat)
        gb_qkv.append(gb)
    g_w_in_t = _grad_w_in_t(dproj, h)

    def small_grads(token):
        g_w_out = _grad_rows_2d(merged, do_f, "grad_w_out", token)
        g_w_pb = _grad_rows_2d(bbs, dyc, "grad_w_proj_conv", g_w_out)
        g_w_pa_t = _grad_rows_2d(dya, a_bf, "grad_w_proj_attn", g_w_pb)
        return [g_w_out, g_w_pb, g_w_pa_t]

    rs_state, token = _reduce_scatter_begin(g_w_in_t, small_grads)
    grad_x, dss = _grad_h(dproj, w_in_t, gx0, x2, ada, token)

    rows8, tot, g_bada = _small_reduce(gb_rest, gb_qkv, svec, dgate, dss)
    (g_in_mine, g_in_got), (g_out, g_pb, g_pa_t) = _reduce_scatter_end(rs_state, tot)
    loss = tot[0, P_LOSS]
    dada_all = rows8[:, 0, P_DADA:].reshape(N_DEV * nbat, 3 * D)
    dada_mine = lax.dynamic_slice(dada_all, (0, me * ncol), (N_DEV * nbat, ncol))

    g_in_t, d_win_t, nm_win_t, nv_win_t = _final_sum_adam_rows(g_in_mine, g_in_got, w_in_rows, tr(m_w_in), tr(v_w_in), 4, "adam_w_in")
    g_win, d_win, nm_win, nv_win = g_in_t.T, d_win_t.T, nm_win_t.T, nv_win_t.T
    g_wpa, d_wpa, nm_wpa, nv_wpa = _adam_transposed(g_pa_t, sq(w_proj_attn), sq(m_w_proj_attn), sq(v_w_proj_attn), "adam_w_proj_attn")
    g_wada, d_wada, nm_wada, nv_wada = _adam_w_ada(cact_all, dada_mine, sq(w_ada), sq(m_w_ada), sq(v_w_ada))
    g_bin = tot[:, P_BIN:P_BIN + D_IN]
    g_bout = tot[:, P_BOUT:P_BOUT + D]
    g_lng = tot[:, P_LNG:P_LNG + D]
    g_lnb = tot[:, P_LNB:P_LNB + D]
    g_conv = lax.dynamic_slice(tot[:, P_CONV:P_CONV + 3 * D].reshape(3, D), (0, me * cw_s.shape[1]), (3, cw_s.shape[1]))
    upd = _adam_many([
        (sq(w_proj_conv), g_pb, sq(m_w_proj_conv), sq(v_w_proj_conv)),
        (sq(w_out), g_out, sq(m_w_out), sq(v_w_out)),
        (b_ada, g_bada, m_b_ada, v_b_ada), (b_in, g_bin, m_b_in, v_b_in), (sq(conv_w), g_conv, sq(m_conv_w), sq(v_conv_w)),
        (b_out, g_bout, m_b_out, v_b_out), (ln_g, g_lng, m_ln_g, v_ln_g), (ln_b, g_lnb, m_ln_b, v_ln_b)], "adam_rest")
    (d_wpb, nm_wpb, nv_wpb), (d_wout, nm_wout, nv_wout), (d_bada, nm_bada, nv_bada), (d_bin, nm_bin, nv_bin), \
        (d_conv, nm_conv, nv_conv), (d_bout, nm_bout, nv_bout), (d_lng, nm_lng, nv_lng), (d_lnb, nm_lnb, nv_lnb) = upd

    ex = lambda a: a.reshape((1,) + a.shape)
    grads = [ex(g_wada), g_bada, ex(g_win), g_bin, ex(g_conv), ex(g_wpa), ex(g_pb), ex(g_out), g_bout, g_lng, g_lnb]
    deltas = [ex(d_wada), d_bada, ex(d_win), d_bin, ex(d_conv), ex(d_wpa), ex(d_wpb), ex(d_wout), d_bout, d_lng, d_lnb]
    new_m = [ex(nm_wada), nm_bada, ex(nm_win), nm_bin, ex(nm_conv), ex(nm_wpa), ex(nm_wpb), ex(nm_wout), nm_bout, nm_lng, nm_lnb]
    new_v = [ex(nv_wada), nv_bada, ex(nv_win), nv_bin, ex(nv_conv), ex(nv_wpa), ex(nv_wpb), ex(nv_wout), nv_bout, nv_lng, nv_lnb]
    return (loss, grad_x.reshape(x.shape), *grads, *deltas, *new_m, *new_v)
```

```python
import jax
import jax.numpy as jnp
from jax import lax
from jax.experimental import pallas as pl
from jax.experimental.pallas import tpu as pltpu

F32, BF16 = jnp.float32, jnp.bfloat16
MESH = pl.DeviceIdType.MESH
N_DEV = 8
D = 1024
SLAB = 256
N_QKV, N_REST = 9, 25
N_SLAB = N_QKV + N_REST
D_IN = N_SLAB * SLAB
DP_SLABS = 36
BLK = 128
WAYS = 4
GROUPS = ((128, 1), (512, 4), (2048, 16))
ALPHA = 2.0 ** 0.25
LN_EPS = 1e-5
LR, B1, B2, EPS, WD, STEP = 0.001, 0.9, 0.999, 1e-08, 0.01, 10
R_ZA, R_UX, R_GB, R_GC, R_ZC, R_GA, R_GBM = 0, 1, 5, 9, 13, 17, 21
P_BIN, P_BOUT, P_LNG, P_LNB, P_CONV, P_LOSS, P_DADA = 0, 8704, 9728, 10752, 11776, 14848, 14976
MIB = 1024 * 1024


def _pcall(body, *, out_shape, out_specs=None, **kw):
    def pin_out(shape, spec):
        in_hbm = getattr(spec, "block_shape", None) is not None or getattr(spec, "memory_space", None) is pl.ANY
        return pltpu.HBM(shape.shape, shape.dtype) if in_hbm and isinstance(shape, jax.ShapeDtypeStruct) else shape

    n_scalar = 0
    if out_specs is None:
        specs = kw["grid_spec"].out_specs
        n_scalar = kw["grid_spec"].num_scalar_prefetch
    else:
        kw["out_specs"] = specs = out_specs
    if isinstance(out_shape, (tuple, list)):
        out_shape = tuple(pin_out(s, p) for s, p in zip(out_shape, specs))
    else:
        out_shape = pin_out(out_shape, specs)
    call = pl.pallas_call(body, out_shape=out_shape, **kw)

    def run(*operands):
        def pin(o):
            is_data = jnp.issubdtype(o.dtype, jnp.floating) or jnp.issubdtype(o.dtype, jnp.integer)
            return pltpu.with_memory_space_constraint(o, pltpu.HBM) if is_data else o
        return call(*operands[:n_scalar], *[pin(o) for o in operands[n_scalar:]])

    return run

ANY = pl.BlockSpec(memory_space=pl.ANY)
VMEM = pl.BlockSpec(memory_space=pltpu.VMEM)


def _whole(a):
    return pl.BlockSpec(a.shape, lambda i: (0,) * len(a.shape))


def _params(vmem_mib=None, sem=None):
    kw = {}
    if vmem_mib is not None:
        kw["vmem_limit_bytes"] = vmem_mib * MIB
    if sem is not None:
        kw["dimension_semantics"] = sem
    return pltpu.CompilerParams(**kw)


def _nn(a, b):
    return jnp.dot(a, b, preferred_element_type=F32)


def _nt(a, b):
    return lax.dot_general(a, b, (((1,), (1,)), ((), ())), preferred_element_type=F32)


def _tn(a, b):
    return lax.dot_general(a, b, (((0,), (0,)), ((), ())), preferred_element_type=F32)


def _sigmoid(v):
    return 0.5 * jnp.tanh(0.5 * v) + 0.5


def _part8(v):
    return v.reshape(v.shape[0] // 8, 8, v.shape[1]).sum(axis=0)


def _my_position():
    return lax.axis_index("x"), lax.axis_index("y"), lax.axis_index("c")


def _flat(px, py, pc):
    return 4 * px + 2 * py + pc


def _peer(mask):
    x, y, c = _my_position()
    return (x ^ ((mask >> 2) & 1), y ^ ((mask >> 1) & 1), c ^ (mask & 1))


def _column_chunks(n):
    chunks = [(128 * a, 0, 128 * a, 128) for a in range(n // 128)]
    if n % 128:
        chunks.append((n - 128, 128 - n % 128, 128 * (n // 128), n % 128))
    return chunks


def _cast_rows(w, n_steps, name):
    rows, ncol = w.shape
    blk = pl.BlockSpec((rows // n_steps, ncol), lambda i: (i, 0))

    def body(w_ref, o_ref):
        o_ref[...] = w_ref[...].astype(BF16)

    return _pcall(body, grid=(n_steps,), out_shape=jax.ShapeDtypeStruct(w.shape, BF16), in_specs=[blk], out_specs=blk,
                  name=name, compiler_params=_params(16, ("parallel",)))(w)


def _prep(w_in_rows, w_pa, w_pb, w_out, c, conv_w, n_steps=4):
    rows, ncol = w_in_rows.shape
    blk = pl.BlockSpec((rows // n_steps, ncol), lambda i: (i, 0))

    def body(win_ref, wpa_ref, wpb_ref, wout_ref, c_ref, cw_ref, win_o, wpat_ref, wpb_o, wout_o, cact_ref, cwp_ref):
        win_o[...] = win_ref[...].astype(BF16)

        @pl.when(pl.program_id(0) == 0)
        def _():
            wpat_ref[...] = wpa_ref[...].T.astype(BF16)
            wpb_o[...] = wpb_ref[...].astype(BF16)
            wout_o[...] = wout_ref[...].astype(BF16)
            cv = c_ref[...]
            cact_ref[...] = jnp.zeros_like(cact_ref)
            cact_ref[pl.ds(0, cv.shape[0]), :] = cv * _sigmoid(cv)
            cwp_ref[...] = jnp.zeros_like(cwp_ref)
            cwp_ref[pl.ds(0, 3), :] = cw_ref[...]

    small_shape = (jax.ShapeDtypeStruct((w_pa.shape[1], w_pa.shape[0]), BF16),
                   jax.ShapeDtypeStruct(w_pb.shape, BF16), jax.ShapeDtypeStruct(w_out.shape, BF16),
                   jax.ShapeDtypeStruct((8, D), F32), jax.ShapeDtypeStruct((8, conv_w.shape[1]), F32))
    out_shape = (jax.ShapeDtypeStruct(w_in_rows.shape, BF16),) + small_shape
    small = (w_pa, w_pb, w_out, c, conv_w)
    return _pcall(body, grid=(n_steps,), out_shape=out_shape, in_specs=[blk] + [_whole(a) for a in small],
                  out_specs=(blk,) + tuple(_whole(o) for o in small_shape), name="prep",
                  compiler_params=_params(16, ("arbitrary",)))(w_in_rows, *small)


def _exchange_slots(out_refs, send_sems, recv_sems, base=0):
    me = _flat(*_my_position())

    def copy(a, mask, slot):
        return pltpu.make_async_remote_copy(
            src_ref=out_refs[a].at[slot], dst_ref=out_refs[a].at[slot], send_sem=send_sems.at[base + 7 * a + mask - 1],
            recv_sem=recv_sems.at[base + 7 * a + mask - 1], device_id=_peer(mask), device_id_type=MESH)

    pairs = [(a, mask) for a in range(len(out_refs)) for mask in range(1, N_DEV)]
    for a, mask in pairs:
        copy(a, mask, me).start()
    for a, mask in pairs:
        copy(a, mask, _flat(*_peer(mask))).wait_recv()
    for a, mask in pairs:
        copy(a, mask, me).wait_send()


def _ada_forward(cact_mine, cw_mine, w_ada, b_ada_mine):
    ncol = w_ada.shape[1]

    def body(c_ref, cw_ref, w_ref, b_ref, out_ref, call_ref, cwall_ref, send_sems, recv_sems):
        me = _flat(*_my_position())
        call_ref[me] = c_ref[...]
        cwall_ref[me] = cw_ref[...]
        _exchange_slots([call_ref, cwall_ref], send_sems, recv_sems)
        c_all = call_ref[...].reshape(N_DEV * 8, D).astype(BF16)
        out_ref[me] = (_nn(c_all, w_ref[...].astype(BF16)) + b_ref[...]).reshape(N_DEV, 8, ncol)
        _exchange_slots([out_ref], send_sems, recv_sems, base=14)

    operands = (cact_mine, cw_mine, w_ada, b_ada_mine)
    out_shape = (jax.ShapeDtypeStruct((N_DEV, N_DEV, 8, ncol), F32), jax.ShapeDtypeStruct((N_DEV, 8, D), F32),
                 jax.ShapeDtypeStruct((N_DEV,) + cw_mine.shape, F32))
    return _pcall(body, grid=(1,), out_shape=out_shape, in_specs=[_whole(a) for a in operands], out_specs=(VMEM,) * 3,
                  scratch_shapes=[pltpu.SemaphoreType.DMA((21,)), pltpu.SemaphoreType.DMA((21,))], name="ada_forward",
                  compiler_params=_params(16))(*operands)


def _small_reduce(gb_rest, gb_qkv, svec, dgate, dss):
    nbat = dgate.shape[0]

    def body(gbr_ref, q0_ref, q1_ref, q2_ref, sv_ref, dg_ref, dss_ref, rows_ref, tot_ref, gbada_ref, bin_ref, bout_ref,
             lng_ref, lnb_ref, send_sems, recv_sems):
        me = _flat(*_my_position())

        def put(off, v):
            rows_ref[me, :, pl.ds(off, v.shape[1])] = v

        def row(v):
            return jnp.sum(v, axis=0, keepdims=True)

        for g, q_ref in enumerate((q0_ref, q1_ref, q2_ref)):
            for which in range(3):
                put(P_BIN + SLAB * (3 * which + g), row(q_ref[which]))
        for s in range(N_REST):
            put(P_BIN + SLAB * (N_QKV + s), row(gbr_ref[s]))
        put(P_LNG, row(sv_ref[0]))
        put(P_LNB, row(sv_ref[1]))
        put(P_BOUT, row(sv_ref[2]))
        for j in range(3):
            put(P_CONV + D * j, row(sv_ref[3 + j]))
        loss = (0.5 / D) * jnp.sum(row(sv_ref[6]), axis=1, keepdims=True)
        put(P_LOSS, jnp.broadcast_to(loss, (1, 128)))
        for b in range(nbat):
            put(P_DADA + 3 * D * b, row(dss_ref[b, 0]))
            put(P_DADA + 3 * D * b + D, row(dss_ref[b, 1]))
            put(P_DADA + 3 * D * b + 2 * D, row(dg_ref[b]))
        _exchange_slots([rows_ref], send_sems, recv_sems)
        tot = rows_ref[0]
        for k in range(1, N_DEV):
            tot = tot + rows_ref[k]
        tot_ref[...] = tot
        gbada = tot[:, P_DADA:P_DADA + 3 * D]
        for b in range(1, nbat):
            gbada = gbada + tot[:, P_DADA + 3 * D * b:P_DADA + 3 * D * (b + 1)]
        gbada_ref[...] = gbada
        bin_ref[...] = tot[:, P_BIN:P_BIN + D_IN]
        bout_ref[...] = tot[:, P_BOUT:P_BOUT + D]
        lng_ref[...] = tot[:, P_LNG:P_LNG + D]
        lnb_ref[...] = tot[:, P_LNB:P_LNB + D]

    p_len = P_DADA + nbat * 3 * D
    out_shape = (jax.ShapeDtypeStruct((N_DEV, 1, p_len), F32), jax.ShapeDtypeStruct((1, p_len), F32),
                 jax.ShapeDtypeStruct((1, 3 * D), F32), jax.ShapeDtypeStruct((1, D_IN), F32),
                 jax.ShapeDtypeStruct((1, D), F32), jax.ShapeDtypeStruct((1, D), F32), jax.ShapeDtypeStruct((1, D), F32))
    operands = (gb_rest, *gb_qkv, svec, dgate, dss)
    return _pcall(body, grid=(1,), out_shape=out_shape, in_specs=[_whole(a) for a in operands],
                  out_specs=(VMEM,) + tuple(_whole(o) for o in out_shape[1:]),
                  scratch_shapes=[pltpu.SemaphoreType.DMA((7,)), pltpu.SemaphoreType.DMA((7,))], name="small_reduce",
                  compiler_params=_params(16))(*operands)


PIECE = 64
N_CHUNK = 4
ARRIVAL_RANK = (0, 1, 3, 5, 2, 4, 6, 7)
SLOT_MASK = (1, 4, 2, 6, 5, 3, 7)


def _arrival_tables(shard_rows):
    import numpy as np
    crow = shard_rows // N_CHUNK
    table = np.zeros((N_DEV, N_SLAB + 7 * N_CHUNK), np.int32)
    lo = [(SLAB * j) // crow for j in range(N_SLAB)]
    hi = [(SLAB * j + SLAB - 1) // crow for j in range(N_SLAB)]
    for k in range(N_DEV):
        def rank(chunk):
            shard_rank = ARRIVAL_RANK[(chunk // N_CHUNK) ^ k]
            return shard_rank if shard_rank < 2 else 2 + 8 * (chunk % N_CHUNK) + shard_rank
        order = sorted(range(N_SLAB), key=lambda j: (max(rank(lo[j]), rank(hi[j])), j))
        table[k, :N_SLAB] = order
        for slot, mask in enumerate(SLOT_MASK):
            for ch in range(N_CHUNK):
                chunk = (k ^ mask) * N_CHUNK + ch
                table[k, N_SLAB + slot * N_CHUNK + ch] = min(t for t, j in enumerate(order) if lo[j] <= chunk <= hi[j])
    return table


def _project_gather(shard, x, ada, b_in3, others, xt=512):
    t = x.shape[0]
    n_o = len(others)
    srows = shard.shape[0]
    crow = srows // N_CHUNK
    shards = [shard] + list(others)
    table = jnp.asarray(_arrival_tables(srows))
    seq_tiles = (t // ada.shape[0]) // xt

    def body(tbl_ref, *refs):
        srcs = [refs[0]] + list(refs[4:4 + n_o])
        x_ref, ada_ref, b_ref = refs[1], refs[2], refs[3]
        outs = [refs[4 + n_o]] + list(refs[8 + n_o:8 + 2 * n_o])
        qkv_ref, rest_ref, h_out = refs[5 + n_o], refs[6 + n_o], refs[7 + n_o]
        (wtile, obf, of32, h_ref, xbuf, send_sems, recv_sems, local_sems, tile_sems, obf_sems, of32_sems, x_sems,
         h_sems) = refs[8 + 2 * n_o:]
        w_full = outs[0]
        x, y, c = _my_position()
        k = _flat(x, y, c)
        me, sibling = (x, y, c), (x, y, 1 - c)
        chips = [(1 - x, y), (x, 1 - y), (1 - x, 1 - y)]

        def rows(a, px, py, pc, ch):
            r = shards[a].shape[0]
            if ch is None:
                return outs[a].at[pl.ds(pl.multiple_of(_flat(px, py, pc) * r, r), r), :]
            return outs[a].at[pl.ds(pl.multiple_of(_flat(px, py, pc) * r + ch * crow, crow), crow), :]

        def copy(a, slot, block, to, ch=None, src=None):
            sem = slot * N_CHUNK + ch if a == 0 else 7 * (N_CHUNK - 1 + a) + slot
            if src is not None and ch is not None:
                src = src.at[pl.ds(ch * crow, crow), :]
            return pltpu.make_async_remote_copy(
                src_ref=rows(a, *block, ch) if src is None else src, dst_ref=rows(a, *block, ch),
                send_sem=send_sems.at[sem], recv_sem=recv_sems.at[sem], device_id=to, device_id_type=MESH)

        mine = [pltpu.make_async_copy(srcs[a], rows(a, *me, None), local_sems.at[a]) for a in range(1 + n_o)]
        first = []
        for ch in range(N_CHUNK):
            first.append(copy(0, 0, me, sibling, ch, src=srcs[0]))
            first += [copy(0, 1 + j, me, (*chip, c), ch, src=srcs[0]) for j, chip in enumerate(chips)]
        for a in range(1, 1 + n_o):
            first.append(copy(a, 0, me, sibling, src=srcs[a]))
            first += [copy(a, 1 + j, me, (*chip, c), src=srcs[a]) for j, chip in enumerate(chips)]
        for cp in mine + first:
            cp.start()

        def arrive(a, slot, ch=None):
            if slot == 0:
                copy(a, 0, sibling, me, ch).wait_recv()
            elif slot < 4:
                copy(a, slot, (*chips[slot - 1], c), me, ch).wait_recv()
                copy(a, slot + 3, (*chips[slot - 1], c), sibling, ch).start()
            else:
                copy(a, slot, (*chips[slot - 4], 1 - c), me, ch).wait_recv()

        def arrive_for(step):
            for slot in range(7):
                for ch in range(N_CHUNK):
                    @pl.when(tbl_ref[k, N_SLAB + slot * N_CHUNK + ch] == step)
                    def _():
                        arrive(0, slot, ch)

        def fetch(step, buf):
            slab = tbl_ref[k, step]
            for p in range(SLAB // PIECE):
                g0 = slab * SLAB + PIECE * p
                own = (g0 >= k * srows) & (g0 < (k + 1) * srows)
                dst = wtile.at[buf, pl.ds(PIECE * p, PIECE), :]

                @pl.when(own)
                def _():
                    pltpu.make_async_copy(srcs[0].at[pl.ds(pl.multiple_of(g0 - k * srows, PIECE), PIECE), :], dst, tile_sems.at[buf]).start()

                @pl.when(jnp.logical_not(own))
                def _():
                    pltpu.make_async_copy(w_full.at[pl.ds(pl.multiple_of(g0, PIECE), PIECE), :], dst, tile_sems.at[buf]).start()

        def wait_tile(buf):
            pltpu.make_async_copy(w_full.at[pl.ds(0, SLAB), :], wtile.at[buf], tile_sems.at[buf]).wait()

        def put(buf_ref, sems, dst_ref, count, value):
            b = count % 2

            @pl.when(count >= 2)
            def _():
                pltpu.make_async_copy(buf_ref.at[b], dst_ref, sems.at[b]).wait()

            buf_ref[b] = value
            pltpu.make_async_copy(buf_ref.at[b], dst_ref, sems.at[b]).start()

        def drain(buf_ref, sems, dst_ref, count):
            for back in (1, 2):
                @pl.when(count >= back)
                def _():
                    pltpu.make_async_copy(buf_ref.at[(count - back) % 2], dst_ref, sems.at[(count - back) % 2]).wait()

        def x_copy(i):
            return pltpu.make_async_copy(x_ref.at[pl.ds(xt * i, xt), :], xbuf.at[i % 2], x_sems.at[i % 2])

        def h_copy(i):
            return pltpu.make_async_copy(h_ref.at[pl.ds(xt * i, xt), :], h_out.at[pl.ds(xt * i, xt), :], h_sems.at[i % 2])

        x_copy(0).start()
        for i in range(t // xt):
            if i + 1 < t // xt:
                x_copy(i + 1).start()
            x_copy(i).wait()
            b = i // seq_tiles
            h_ref[pl.ds(xt * i, xt), :] = (xbuf[i % 2] * (1.0 + ada_ref[b, 1:2, :]) + ada_ref[b, 0:1, :]).astype(BF16)
            if i >= 2:
                h_copy(i - 2).wait()
            h_copy(i).start()
        for i in range(max(t // xt - 2, 0), t // xt):
            h_copy(i).wait()

        arrive_for(0)
        fetch(0, 0)

        def step(s, carry):
            n_bf, n_f32 = carry
            buf = s % 2

            @pl.when(s + 1 < N_SLAB)
            def _():
                arrive_for(s + 1)
                fetch(s + 1, 1 - buf)

            wait_tile(buf)
            slab = tbl_ref[k, s]
            v = _nt(h_ref[...], wtile[buf]) + b_ref[slab]
            is_qkv = slab < N_QKV

            @pl.when(is_qkv)
            def _():
                put(obf, obf_sems, qkv_ref.at[jnp.minimum(slab, N_QKV - 1)], n_bf, v.astype(BF16))

            @pl.when(jnp.logical_not(is_qkv))
            def _():
                put(of32, of32_sems, rest_ref.at[jnp.maximum(slab - N_QKV, 0)], n_f32, v)

            return n_bf + is_qkv.astype(jnp.int32), n_f32 + 1 - is_qkv.astype(jnp.int32)

        n_bf, n_f32 = lax.fori_loop(0, N_SLAB, step, (jnp.int32(0), jnp.int32(0)))
        drain(obf, obf_sems, qkv_ref.at[0], n_bf)
        drain(of32, of32_sems, rest_ref.at[0], n_f32)

        for slots in ((1, 2, 3), (0, 4, 5, 6)):
            for a in range(1, 1 + n_o):
                for slot in slots:
                    arrive(a, slot)
        for cp in first:
            cp.wait_send()
        for j, chip in enumerate(chips):
            for ch in range(N_CHUNK):
                copy(0, 4 + j, (*chip, c), sibling, ch).wait_send()
            for a in range(1, 1 + n_o):
                copy(a, 4 + j, (*chip, c), sibling).wait_send()
        for cp in mine:
            cp.wait()

    out_shape = ((jax.ShapeDtypeStruct((N_DEV * srows, D), BF16), jax.ShapeDtypeStruct((N_QKV, t, SLAB), BF16),
                  jax.ShapeDtypeStruct((N_REST, t, SLAB), F32), jax.ShapeDtypeStruct((t, D), BF16))
                 + tuple(jax.ShapeDtypeStruct((N_DEV * o.shape[0], o.shape[1]), o.dtype) for o in others))
    n_all = 1 + n_o
    n_sems = 7 * (N_CHUNK + n_o)
    pair = pltpu.SemaphoreType.DMA((2,))
    grid_spec = pltpu.PrefetchScalarGridSpec(
        num_scalar_prefetch=1, grid=(1,),
        in_specs=[ANY, ANY, pl.BlockSpec(ada.shape, lambda i, tbl: (0, 0, 0)),
                  pl.BlockSpec((N_SLAB, 1, SLAB), lambda i, tbl: (0, 0, 0))] + [ANY] * n_o,
        out_specs=(ANY,) * (4 + n_o),
        scratch_shapes=[pltpu.VMEM((2, SLAB, D), BF16), pltpu.VMEM((2, t, SLAB), BF16), pltpu.VMEM((2, t, SLAB), F32),
                        pltpu.VMEM((t, D), BF16), pltpu.VMEM((2, xt, D), F32),
                        pltpu.SemaphoreType.DMA((n_sems,)), pltpu.SemaphoreType.DMA((n_sems,)),
                        pltpu.SemaphoreType.DMA((n_all,)), pair, pair, pair, pair, pair])
    res = _pcall(body, grid_spec=grid_spec, out_shape=out_shape, name="project_gather",
                 compiler_params=_params(48, ("arbitrary",)))(table, shard, x, ada, b_in3, *others)
    return res[0], res[1], res[2], res[3], list(res[4:])


def _bias_tables(g):
    window, dil = GROUPS[g]
    span = window // dil
    qi = jnp.arange(BLK)[:, None]
    kj = jnp.arange(2 * BLK)[None, :]
    delta = qi + BLK - kj
    valid = (delta >= 0) & (delta <= span)
    heads = jnp.arange(4, dtype=F32) + 4.0 * g
    slopes = 2.0 ** (-8.0 * (heads + 1.0) / 12.0)
    bias = -slopes[:, None, None] * (delta * dil).astype(F32)[None]
    return jnp.where(valid[None], bias, -1e30).reshape(4 * BLK, 2 * BLK)


def _head_masks(shape):
    lane = lax.broadcasted_iota(jnp.int32, shape, 1)
    return [(lane >= 64 * h) & (lane < 64 * (h + 1)) for h in range(4)]


def _stack_heads(v, masks):
    return jnp.concatenate([jnp.where(masks[h], v, jnp.zeros_like(v)) for h in range(4)], axis=0)


def _unstack_heads(v4, masks):
    out = jnp.where(masks[0], v4[0:BLK], 0.0)
    for h in range(1, 4):
        out = jnp.where(masks[h], v4[BLK * h:BLK * (h + 1)], out)
    return out


def _by_residue(v, n, dil):
    return jnp.swapaxes(v.reshape(n, dil, 128), 0, 1).reshape(n * dil, 128) if dil > 1 else v


def _by_token(v, n, dil):
    return jnp.swapaxes(v.reshape(dil, n, 128), 0, 1).reshape(n * dil, 128) if dil > 1 else v


def _regroup(load_half, dst_ref, stage_ref, n, dil):
    if dil >= 8:
        for hlf in range(2):
            dst_ref[:, pl.ds(128 * hlf, 128)] = _by_residue(load_half(hlf), n, dil).astype(dst_ref.dtype)
        return
    for hlf in range(2):
        stage_ref[hlf] = load_half(hlf)
    for r in range(dil):
        for hlf in range(2):
            dst_ref[pl.ds(r * n, n), pl.ds(128 * hlf, 128)] = stage_ref[hlf, pl.ds(r, n, stride=dil), :].astype(dst_ref.dtype)


def _store_block(res_ref, r, i, val, n):
    for hlf in range(2):
        res_ref[hlf, pl.ds(pl.multiple_of(r * n + i * BLK, BLK), BLK), :] = val[:, 128 * hlf:128 * (hlf + 1)]


def _for_blocks(block, dil, nblk, ways=WAYS):
    ways = min(ways, max(dil, nblk))
    if dil == 1:
        for i in range(ways):
            block(0, i, i == 0)

        def step(k, carry):
            for j in range(ways):
                block(0, ways * k + j, False)
            return carry

        lax.fori_loop(1, nblk // ways, step, 0)
    else:
        ways = min(ways, dil)

        def residues(k, carry):
            for j in range(ways):
                block(ways * k + j, 0, True)
            if nblk > 1:
                def loop(i, c):
                    for j in range(ways):
                        block(ways * k + j, i, False)
                    return c
                lax.fori_loop(1, nblk, loop, 0)
            return carry

        lax.fori_loop(0, dil // ways, residues, 0)


def _attn_forward(qkv, nbat):
    t = qkv.shape[1]
    seq = t // nbat
    n_grp = len(GROUPS)

    def body(qkv_ref, b0_ref, b1_ref, b2_ref, ol_ref, stage, qs_ref, ks_ref, vs_ref, *nat):
        masks = _head_masks((BLK, SLAB))
        bias_refs = (b0_ref, b1_ref, b2_ref)
        for g, (_, dil) in enumerate(GROUPS):
            n = seq // dil
            bias_ref, nat_o, nat_l = bias_refs[g], nat[2 * g], nat[2 * g + 1]
            if dil > 1:
                qd, kd, vd = qs_ref, ks_ref, vs_ref
                for which, dst in enumerate((qd, kd, vd)):
                    _regroup(lambda hlf, which=which, g=g: qkv_ref[3 * which + g, :, pl.ds(128 * hlf, 128)].astype(F32), dst, stage, n, dil)
            else:
                qd, kd, vd = qkv_ref.at[g], qkv_ref.at[3 + g], qkv_ref.at[6 + g]

            def block(r, i, first, n=n, dil=dil, qd=qd, kd=kd, vd=vd, bias_ref=bias_ref, nat_o=nat_o, nat_l=nat_l):
                base = r * n
                qs = pl.ds(pl.multiple_of(base + i * BLK, BLK), BLK)
                ks = pl.ds(pl.multiple_of(base, BLK), BLK) if first else pl.ds(pl.multiple_of(base + (i - 1) * BLK, BLK), 2 * BLK)
                q, kk, vv = qd[qs, :], kd[ks, :], vd[ks, :]
                bias = bias_ref[:, pl.ds(BLK, BLK)] if first else bias_ref[...]
                s = _nt(_stack_heads(q, masks), kk) * 0.125 + bias
                m = jnp.max(s, axis=1, keepdims=True)
                p = jnp.exp(s - m)
                den = jnp.sum(p, axis=1, keepdims=True)
                out = _unstack_heads(_nn((p * (1.0 / den)).astype(BF16), vv), masks)
                lse = _unstack_heads(jnp.broadcast_to(m + jnp.log(den), (4 * BLK, SLAB)), masks)
                _store_block(nat_o, r, i, out, n)
                _store_block(nat_l, r, i, lse, n)

            _for_blocks(block, dil, n // BLK, ways=2 * WAYS)

        def tokens(k, hlf):
            dil = GROUPS[k // 2][1]
            return _by_token(nat[k][hlf], seq // dil, dil)

        for hlf in range(2):
            l0, l1, l2 = tokens(1, hlf), tokens(3, hlf), tokens(5, hlf)
            mx = jnp.maximum(jnp.maximum(l0, l1), l2)
            e0, e1, e2 = jnp.exp(l0 - mx), jnp.exp(l1 - mx), jnp.exp(l2 - mx)
            den = e0 + e1 + e2
            ol_ref[0, :, pl.ds(128 * hlf, 128)] = (e0 * tokens(0, hlf) + e1 * tokens(2, hlf) + e2 * tokens(4, hlf)) * (1.0 / den)
            ol_ref[1, :, pl.ds(128 * hlf, 128)] = mx + jnp.log(den)

    halves = pltpu.VMEM((2, seq, 128), F32)
    bias_spec = pl.BlockSpec((4 * BLK, 2 * BLK), lambda b: (0, 0))
    return _pcall(
        body, grid=(nbat,), out_shape=jax.ShapeDtypeStruct((2, t, SLAB), F32),
        in_specs=[pl.BlockSpec((N_QKV, seq, SLAB), lambda b: (0, b, 0))] + [bias_spec] * n_grp,
        out_specs=pl.BlockSpec((2, seq, SLAB), lambda b: (0, b, 0)),
        scratch_shapes=[halves] + [pltpu.VMEM((seq, SLAB), BF16)] * 3 + [halves] * (2 * n_grp),
        name="attn_forward", compiler_params=_params(56, ("parallel",)))(qkv, *[_bias_tables(g) for g in range(n_grp)])


def _attn_backward(qkv, do_attn, ol_tot, dproj, g, nbat):
    t = qkv.shape[1]
    seq = t // nbat
    dil = GROUPS[g][1]
    n = seq // dil
    nblk = n // BLK
    qkv4 = qkv.reshape(3, 3, t, SLAB)
    dp4 = dproj.reshape(DP_SLABS // 3, 3, t, SLAB)

    def body(qkv_ref, do_ref, ol_ref, bias_ref, dp_in, dp_ref, gb_ref, dk_acc, dv_acc, *scratch):
        del dp_in
        masks = _head_masks((BLK, SLAB))

        @pl.when(pl.program_id(0) == 0)
        def _():
            gb_ref[...] = jnp.zeros_like(gb_ref)

        dk_acc[...] = jnp.zeros_like(dk_acc)
        dv_acc[...] = jnp.zeros_like(dv_acc)
        if dil > 1:
            stage, qd, kd, vd, dod, prodd, lsed, dq_res = scratch
            lanes = lambda hlf: pl.ds(128 * hlf, 128)
            for which, dst in enumerate((qd, kd, vd)):
                _regroup(lambda hlf, which=which: qkv_ref[which, 0, :, lanes(hlf)].astype(F32), dst, stage, n, dil)
            _regroup(lambda hlf: do_ref[:, lanes(hlf)].astype(F32), dod, stage, n, dil)
            _regroup(lambda hlf: do_ref[:, lanes(hlf)].astype(F32) * ol_ref[0, :, lanes(hlf)], prodd, stage, n, dil)
            _regroup(lambda hlf: ol_ref[1, :, lanes(hlf)], lsed, stage, n, dil)
        else:
            qd, kd, vd = qkv_ref.at[0, 0], qkv_ref.at[1, 0], qkv_ref.at[2, 0]

        def block(r, i, first):
            base = r * n
            qs = pl.ds(pl.multiple_of(base + i * BLK, BLK), BLK)
            ks = pl.ds(pl.multiple_of(base, BLK), BLK) if first else pl.ds(pl.multiple_of(base + (i - 1) * BLK, BLK), 2 * BLK)
            q, kk, vv = qd[qs, :], kd[ks, :], vd[ks, :]
            if dil > 1:
                do, prod, lse = dod[qs, :], prodd[qs, :], lsed[qs, :]
            else:
                do = do_ref[qs, :]
                prod = do.astype(F32) * ol_ref[0, qs, :]
                lse = ol_ref[1, qs, :]
            q4, do4 = _stack_heads(q, masks), _stack_heads(do, masks)
            bias = bias_ref[:, pl.ds(BLK, BLK)] if first else bias_ref[...]
            lse4 = jnp.concatenate([lse[:, 64 * h:64 * h + 1] for h in range(4)], axis=0)
            delta4 = jnp.concatenate([jnp.sum(jnp.where(masks[h], prod, 0.0), axis=1, keepdims=True) for h in range(4)], axis=0)
            p = jnp.exp(_nt(q4, kk) * 0.125 + bias - lse4)
            ds = (p * (_nt(do4, vv) - delta4)).astype(BF16)
            dv_acc[ks, :] += _tn(p.astype(BF16), do4)
            dk_acc[ks, :] += _tn(ds, q4) * 0.125
            dq = _unstack_heads(_nn(ds, kk), masks) * 0.125
            if dil > 1:
                _store_block(dq_res, r, i, dq, n)
            else:
                dp_ref[0, 0, qs, :] = dq.astype(BF16)
            gb_ref[0] += _part8(dq)

        _for_blocks(block, dil, nblk, ways=2 * WAYS)
        gb_ref[1] += _part8(dk_acc[...])
        gb_ref[2] += _part8(dv_acc[...])
        if dil > 1:
            for hlf in range(2):
                half = pl.ds(128 * hlf, 128)
                dp_ref[0, 0, :, half] = _by_token(dq_res[hlf], n, dil).astype(BF16)
                dp_ref[1, 0, :, half] = _by_token(dk_acc[:, half], n, dil).astype(BF16)
                dp_ref[2, 0, :, half] = _by_token(dv_acc[:, half], n, dil).astype(BF16)
        else:
            dp_ref[1, 0] = dk_acc[...].astype(BF16)
            dp_ref[2, 0] = dv_acc[...].astype(BF16)

    scratch = [pltpu.VMEM((seq, SLAB), F32)] * 2
    if dil > 1:
        halves = pltpu.VMEM((2, seq, 128), F32)
        scratch += [halves] + [pltpu.VMEM((seq, SLAB), BF16)] * 4 + [pltpu.VMEM((seq, SLAB), F32)] * 2 + [halves]
    dp, gb = _pcall(
        body, grid=(nbat,),
        out_shape=(jax.ShapeDtypeStruct(dp4.shape, BF16), jax.ShapeDtypeStruct((3, 8, SLAB), F32)),
        in_specs=[pl.BlockSpec((3, 1, seq, SLAB), lambda b: (0, g, b, 0)),
                  pl.BlockSpec((seq, SLAB), lambda b: (b, 0)),
                  pl.BlockSpec((2, seq, SLAB), lambda b: (0, b, 0)),
                  pl.BlockSpec((4 * BLK, 2 * BLK), lambda b: (0, 0)), ANY],
        out_specs=(pl.BlockSpec((3, 1, seq, SLAB), lambda b: (DP_SLABS // 9 - 1, g, b, 0)),
                   pl.BlockSpec((3, 8, SLAB), lambda b: (0, 0, 0))),
        scratch_shapes=scratch, input_output_aliases={4: 0}, name=f"attn_backward_{g}",
        compiler_params=_params(48, ("arbitrary",)))(qkv4, do_attn, ol_tot, _bias_tables(g), dp4)
    return dp.reshape(DP_SLABS, t, SLAB), gb


def _mid(rest, ol_tot, x, tgt, ada, cw, b_out, ln_g, ln_b, w_pa_t, w_pb, w_out, tm=256):
    t = x.shape[0]
    nbat = ada.shape[0]
    nt = t // tm
    tps = nt // nbat

    def body(rest_ref, halo_ref, ol_ref, x_ref, t_ref, ada_ref, cw_ref, bout_ref, lng_ref, lnb_ref,
             wpat_ref, wpb_ref, wout_ref,
             dp_ref, gx0_ref, doa_ref, mg_ref, dof_ref, bbs_ref, dyc_ref, a_ref, dya_ref,
             gbr_ref, sv_ref, dgate_ref, carry_ref, keep_ref):
        i = pl.program_id(0)
        ti = nt - 1 - i
        pos = ti % tps

        @pl.when(i == 0)
        def _():
            gbr_ref[...] = jnp.zeros_like(gbr_ref)
            sv_ref[...] = jnp.zeros_like(sv_ref)

        @pl.when(pos == tps - 1)
        def _():
            dgate_ref[...] = jnp.zeros_like(dgate_ref)
            carry_ref[...] = jnp.zeros_like(carry_ref)

        row = lax.broadcasted_iota(jnp.int32, (tm, SLAB), 0)
        halo_on = (pos > 0).astype(F32)

        def cols(s):
            return pl.ds(SLAB * s, SLAB)

        o_attn = ol_ref[0]
        z_a = rest_ref[R_ZA]
        sg_za = _sigmoid(z_a)
        a_ref[...] = (o_attn * z_a * sg_za).astype(BF16)
        y_attn = _nt(a_ref[...], wpat_ref[...])

        for s in range(4):
            u = rest_ref[R_GC + s] * rest_ref[R_UX + s]
            hu = halo_ref[R_GC + s] * halo_ref[R_UX + s] * halo_on
            u1 = jnp.where(row == 0, hu[7:8], pltpu.roll(u, 1, 0))
            u2 = jnp.where(row == 0, hu[6:7], jnp.where(row == 1, hu[7:8], pltpu.roll(u, 2, 0)))
            conv = cw_ref[0:1, cols(s)] * u2 + cw_ref[1:2, cols(s)] * u1 + cw_ref[2:3, cols(s)] * u
            zc = rest_ref[R_ZC + s]
            sg = _sigmoid(zc)
            keep_ref[2, :, cols(s)], keep_ref[3, :, cols(s)], keep_ref[4, :, cols(s)], keep_ref[5, :, cols(s)] = u1, u2, conv, sg
            bbs_ref[:, cols(s)] = (rest_ref[R_GB + s] * conv * (zc * sg)).astype(BF16)
        y_conv = _nn(bbs_ref[...], wpb_ref[...])

        for s in range(4):
            s_a, s_b = _sigmoid(rest_ref[R_GA + s]), _sigmoid(rest_ref[R_GBM + s])
            keep_ref[0, :, cols(s)], keep_ref[1, :, cols(s)] = s_a, s_b
            mg_ref[:, cols(s)] = (s_a * y_attn[:, SLAB * s:SLAB * (s + 1)] + s_b * y_conv[:, SLAB * s:SLAB * (s + 1)]).astype(BF16)
        o = _nn(mg_ref[...], wout_ref[...]) + bout_ref[...]
        gate = ada_ref[0, 2:3, :]
        r = ALPHA * x_ref[...] + gate * o
        mu = jnp.mean(r, axis=1, keepdims=True)
        rc = r - mu
        rstd = lax.rsqrt(jnp.mean(rc * rc, axis=1, keepdims=True) + LN_EPS)
        xhat = rc * rstd
        err = xhat * lng_ref[...] + lnb_ref[...] - t_ref[...]
        sv_ref[6] += _part8(err * err)
        dy = err * (1.0 / D)
        sv_ref[0] += _part8(dy * xhat)
        sv_ref[1] += _part8(dy)
        dxh = dy * lng_ref[...]
        dr = rstd * (dxh - jnp.mean(dxh, axis=1, keepdims=True) - xhat * jnp.mean(dxh * xhat, axis=1, keepdims=True))
        gx0_ref[...] = ALPHA * dr
        dgate_ref[0] += _part8(dr * o)
        do_ = dr * gate
        sv_ref[2] += _part8(do_)
        dof_ref[...] = do_.astype(BF16)
        dmerged = _nt(dof_ref[...], wout_ref[...])
        for s in range(4):
            s_a, s_b = keep_ref[0, :, cols(s)], keep_ref[1, :, cols(s)]
            dm = dmerged[:, SLAB * s:SLAB * (s + 1)]
            ya, yc = y_attn[:, SLAB * s:SLAB * (s + 1)], y_conv[:, SLAB * s:SLAB * (s + 1)]
            dya_ref[:, cols(s)] = (dm * s_a).astype(BF16)
            dyc_ref[:, cols(s)] = (dm * s_b).astype(BF16)
            dga = dm * ya * s_a * (1.0 - s_a)
            dgb = dm * yc * s_b * (1.0 - s_b)
            dp_ref[R_GA + s] = dga.astype(BF16)
            dp_ref[R_GBM + s] = dgb.astype(BF16)
            gbr_ref[R_GA + s] += _part8(dga)
            gbr_ref[R_GBM + s] += _part8(dgb)

        da = _nn(dya_ref[...], wpat_ref[...])
        doa_ref[...] = (da * z_a * sg_za).astype(BF16)
        dza = da * o_attn * (sg_za * (1.0 + z_a * (1.0 - sg_za)))
        dp_ref[R_ZA] = dza.astype(BF16)
        gbr_ref[R_ZA] += _part8(dza)

        dbb = _nt(dyc_ref[...], wpb_ref[...])
        for s in range(4):
            ux, gc, zc = rest_ref[R_UX + s], rest_ref[R_GC + s], rest_ref[R_ZC + s]
            u = gc * ux
            u1, u2, conv, sg = keep_ref[2, :, cols(s)], keep_ref[3, :, cols(s)], keep_ref[4, :, cols(s)], keep_ref[5, :, cols(s)]
            gb = rest_ref[R_GB + s]
            d_b = dbb[:, SLAB * s:SLAB * (s + 1)]
            szc = zc * sg
            dgb_ = d_b * conv * szc
            dconv = d_b * gb * szc
            dzc = d_b * gb * conv * (sg * (1.0 + zc * (1.0 - sg)))
            sv_ref[3, :, cols(s)] += _part8(dconv * u2)
            sv_ref[4, :, cols(s)] += _part8(dconv * u1)
            sv_ref[5, :, cols(s)] += _part8(dconv * u)
            nxt = carry_ref[:, cols(s)]
            d1 = jnp.where(row == tm - 1, nxt[0:1], pltpu.roll(dconv, tm - 1, 0))
            d2 = jnp.where(row == tm - 1, nxt[1:2], jnp.where(row == tm - 2, nxt[0:1], pltpu.roll(dconv, tm - 2, 0)))
            carry_ref[:, cols(s)] = dconv[0:8]
            du = cw_ref[2:3, cols(s)] * dconv + cw_ref[1:2, cols(s)] * d1 + cw_ref[0:1, cols(s)] * d2
            dgc, dux = du * ux, du * gc
            for slab, val in ((R_GB + s, dgb_), (R_ZC + s, dzc), (R_GC + s, dgc), (R_UX + s, dux)):
                dp_ref[slab] = val.astype(BF16)
                gbr_ref[slab] += _part8(val)

    def tile(i):
        return nt - 1 - i

    row_blk = lambda i: (tile(i), 0)
    slab_blk = lambda i: (0, tile(i), 0)
    const2 = lambda i: (0, 0)
    const3 = lambda i: (0, 0, 0)
    in_specs = [
        pl.BlockSpec((N_REST, tm, SLAB), slab_blk),
        pl.BlockSpec((N_REST, 8, SLAB), lambda i: (0, jnp.maximum(tile(i) * (tm // 8) - 1, 0), 0)),
        pl.BlockSpec((1, tm, SLAB), slab_blk),
        pl.BlockSpec((tm, D), row_blk), pl.BlockSpec((tm, D), row_blk),
        pl.BlockSpec((1, 3, D), lambda i: (tile(i) // tps, 0, 0)),
        pl.BlockSpec((3, D), const2), pl.BlockSpec((1, D), const2), pl.BlockSpec((1, D), const2), pl.BlockSpec((1, D), const2),
        pl.BlockSpec((D, SLAB), const2), pl.BlockSpec((D, D), const2), pl.BlockSpec((D, D), const2)]
    bf_rows = lambda: jax.ShapeDtypeStruct((t, D), BF16)
    out_shape = (
        jax.ShapeDtypeStruct((DP_SLABS, t, SLAB), BF16), jax.ShapeDtypeStruct((t, D), F32),
        jax.ShapeDtypeStruct((t, SLAB), BF16),
        bf_rows(), bf_rows(), bf_rows(), bf_rows(), jax.ShapeDtypeStruct((t, SLAB), BF16), bf_rows(),
        jax.ShapeDtypeStruct((N_REST, 8, SLAB), F32), jax.ShapeDtypeStruct((7, 8, D), F32),
        jax.ShapeDtypeStruct((nbat, 8, D), F32))
    out_specs = (
        pl.BlockSpec((N_REST, tm, SLAB), slab_blk), pl.BlockSpec((tm, D), row_blk),
        pl.BlockSpec((tm, SLAB), row_blk),
        pl.BlockSpec((tm, D), row_blk), pl.BlockSpec((tm, D), row_blk), pl.BlockSpec((tm, D), row_blk),
        pl.BlockSpec((tm, D), row_blk), pl.BlockSpec((tm, SLAB), row_blk), pl.BlockSpec((tm, D), row_blk),
        pl.BlockSpec((N_REST, 8, SLAB), const3), pl.BlockSpec((7, 8, D), const3),
        pl.BlockSpec((1, 8, D), lambda i: (tile(i) // tps, 0, 0)))
    return _pcall(body, grid=(nt,), out_shape=out_shape, in_specs=in_specs, out_specs=out_specs,
                  scratch_shapes=[pltpu.VMEM((8, D), F32), pltpu.VMEM((6, tm, D), F32)], name="mid",
                  compiler_params=_params(56, ("arbitrary",)))(
        rest, rest, ol_tot, x, tgt, ada, cw, b_out, ln_g, ln_b, w_pa_t, w_pb, w_out)


def _tn_matmul(lhs, rhs, lhs_spec, n_steps, out_rows, out_index, name, after):
    t, n = rhs.shape

    def body(l_ref, r_ref, after_ref, o_ref):
        del after_ref
        o_ref[...] = _tn(l_ref[0] if len(l_ref.shape) == 3 else l_ref[...], r_ref[...])

    return _pcall(body, grid=(n_steps,), out_shape=jax.ShapeDtypeStruct((out_rows, n), F32),
                  in_specs=[lhs_spec, pl.BlockSpec((t, n), lambda j: (0, 0)), ANY],
                  out_specs=pl.BlockSpec((SLAB, n), out_index), name=name,
                  compiler_params=_params(48, ("parallel",)))(lhs, rhs, after)


def _grad_rows_2d(lhs, rhs, name, after, tc=1024):
    t, k = lhs.shape
    n = rhs.shape[1]

    def body(l_ref, r_ref, after_ref, o_ref):
        del after_ref
        part = _tn(l_ref[...], r_ref[...])

        @pl.when(pl.program_id(0) == 0)
        def _():
            o_ref[...] = part

        @pl.when(pl.program_id(0) > 0)
        def _():
            o_ref[...] += part

    return _pcall(body, grid=(t // tc,), out_shape=jax.ShapeDtypeStruct((k, n), F32),
                  in_specs=[pl.BlockSpec((tc, k), lambda i: (i, 0)), pl.BlockSpec((tc, n), lambda i: (i, 0)), ANY],
                  out_specs=pl.BlockSpec((k, n), lambda i: (0, 0)), name=name,
                  compiler_params=_params(32, ("arbitrary",)))(lhs, rhs, after)


def _w_row_block(j):
    return (j + N_QKV) % N_SLAB


def _dp_slab(j):
    return jnp.where(j < N_REST, j, j + 2)


def _grad_w_in_t(dproj, h):
    t = h.shape[0]
    return _tn_matmul(dproj, h, pl.BlockSpec((1, t, SLAB), lambda j: (_dp_slab(j), 0, 0)), N_SLAB, D_IN,
                      lambda j: (_w_row_block(j), 0), "grad_w_in", h)


def _grad_h(dproj, w_in_t, gx0, x, ada, after, tm=512):
    t = x.shape[0]
    nbat = ada.shape[0]
    tps = (t // nbat) // tm

    def body(dp_ref, w_ref, gx0_ref, x_ref, ada_ref, after_ref, gx_ref, dss_ref):
        del after_ref
        i = pl.program_id(0)
        dh = None
        for j in range(N_SLAB):
            slab = j if j < N_REST else j + 2
            part = _nn(dp_ref[slab], w_ref[pl.ds(SLAB * ((j + N_QKV) % N_SLAB), SLAB), :])
            dh = part if dh is None else dh + part
        gx_ref[...] = gx0_ref[...] + dh * (1.0 + ada_ref[0, 1:2, :])

        @pl.when((i % tps) == 0)
        def _():
            dss_ref[...] = jnp.zeros_like(dss_ref)

        dss_ref[0, 0] += _part8(dh)
        dss_ref[0, 1] += _part8(dh * x_ref[...])

    return _pcall(
        body, grid=(t // tm,),
        out_shape=(jax.ShapeDtypeStruct((t, D), F32), jax.ShapeDtypeStruct((nbat, 2, 8, D), F32)),
        in_specs=[pl.BlockSpec((DP_SLABS, tm, SLAB), lambda i: (0, i, 0)),
                  pl.BlockSpec((D_IN, D), lambda i: (0, 0), pipeline_mode=pl.Buffered(1)),
                  pl.BlockSpec((tm, D), lambda i: (i, 0)), pl.BlockSpec((tm, D), lambda i: (i, 0)),
                  pl.BlockSpec((1, 3, D), lambda i: (i // tps, 0, 0)), ANY],
        out_specs=(pl.BlockSpec((tm, D), lambda i: (i, 0)),
                   pl.BlockSpec((1, 2, 8, D), lambda i: (i // tps, 0, 0, 0))),
        name="grad_h", compiler_params=_params(60, ("arbitrary",)))(dproj, w_in_t, gx0, x, ada, after)


def _chip(m):
    x, y, _ = _my_position()
    return (x ^ ((m >> 1) & 1), y ^ (m & 1))


def _exchange_siblings(grads, after, name):
    n = len(grads)

    def body(*refs):
        copies = _sibling_copies(refs[:n], refs[n + 1:2 * n + 1], refs[2 * n + 1], refs[2 * n + 2])
        for cp in copies:
            cp.start()
        for cp in copies:
            cp.wait()

    return _pcall(body, out_shape=tuple(_sibling_zones(grads)), in_specs=[ANY] * (n + 1), out_specs=(ANY,) * n,
                  name=name, scratch_shapes=[pltpu.SemaphoreType.DMA((4 * n,))] * 2)(*grads, after)


def _sibling_zones(grads):
    return [jax.ShapeDtypeStruct((4, g.shape[0] // N_DEV, g.shape[1]), g.dtype) for g in grads]


def _sibling_copies(srcs, lands, send_sems, recv_sems):
    x, y, c = _my_position()
    copies = []
    for a, (src, land) in enumerate(zip(srcs, lands)):
        rows = land.shape[1]
        for m in range(4):
            dev = _flat(*_chip(m), 1 - c)
            copies.append(pltpu.make_async_remote_copy(
                src_ref=src.at[pl.ds(pl.multiple_of(dev * rows, 8), rows), :], dst_ref=land.at[m],
                send_sem=send_sems.at[4 * a + m], recv_sem=recv_sems.at[4 * a + m], device_id=(x, y, 1 - c),
                device_id_type=MESH))
    return copies


def _chip_copies(srcs, lands, send_sems, recv_sems):
    _, _, c = _my_position()
    return [pltpu.make_async_remote_copy(
        src_ref=srcs[a].at[m - 1], dst_ref=lands[a].at[m - 1], send_sem=send_sems.at[3 * a + m - 1],
        recv_sem=recv_sems.at[3 * a + m - 1], device_id=(*_chip(m), c), device_id_type=MESH)
        for a in range(len(srcs)) for m in range(1, 4)]


HBM = pl.BlockSpec(memory_space=pltpu.HBM)
SEM = pl.BlockSpec(memory_space=pltpu.SEMAPHORE)
SPLIT_COPY = pltpu.CompilerParams(has_side_effects=pltpu.SideEffectType.DATAFLOW_SIDE_EFFECTING)


def _start_copies(make_copies, n_sems, srcs, zones, name):
    n = len(srcs)

    def body(*refs):
        for cp in make_copies(refs[:n], refs[n:2 * n], refs[2 * n], refs[2 * n + 1]):
            cp.start()
        refs[-1][...] = jnp.zeros_like(refs[-1])

    hbm = tuple(pltpu.HBM(b.shape, b.dtype) for b in list(srcs) + list(zones))
    out_shape = (pltpu.SemaphoreType.DMA((n_sems,)), pltpu.SemaphoreType.DMA((n_sems,))) + hbm + (jax.ShapeDtypeStruct((8, 128), F32),)
    operands = [pltpu.with_memory_space_constraint(b, pltpu.HBM) for b in srcs]
    operands += [pltpu.with_memory_space_constraint(lax.empty(z.shape, z.dtype), pltpu.HBM) for z in zones]
    res = _pcall(body, out_shape=out_shape, in_specs=[HBM] * (2 * n), out_specs=(SEM, SEM) + (HBM,) * (2 * n) + (VMEM,),
                 input_output_aliases={i: 2 + i for i in range(2 * n)}, name=name, compiler_params=SPLIT_COPY)(*operands)
    return (res[0], res[1], res[2:2 + n], res[2 + n:2 + 2 * n]), res[-1]


def _wait_copies(make_copies, flight, after, name):
    send_sems, recv_sems, srcs, zones = flight
    n = len(srcs)

    def body(*refs):
        for cp in make_copies(refs[:n], refs[n:2 * n], refs[2 * n], refs[2 * n + 1]):
            cp.wait_send()
            cp.wait_recv()

    hbm = tuple(pltpu.HBM(b.shape, b.dtype) for b in list(srcs) + list(zones))
    res = _pcall(body, out_shape=hbm, in_specs=[HBM] * (2 * n) + [SEM, SEM, ANY], out_specs=(HBM,) * (2 * n),
                 input_output_aliases={i: i for i in range(2 * n)}, name=name, compiler_params=SPLIT_COPY)(
        *srcs, *zones, send_sems, recv_sems, after)
    return res[:n], res[n:]


def _pair_sums(devs, grads, lands, n_steps, name):
    n = len(grads)
    rows = [l.shape[1] for l in lands]
    rbs = [r // n_steps for r in rows]

    def body(devs_ref, *refs):
        del devs_ref
        g_refs, land_refs, outs = refs[:4 * n], refs[4 * n:5 * n], refs[5 * n:]
        for a in range(n):
            outs[2 * a][...] = g_refs[4 * a][...] + land_refs[a][0]
            for m in range(1, 4):
                outs[2 * a + 1][m - 1] = (g_refs[4 * a + m][...] + land_refs[a][m]).astype(BF16)

    def block_of(m, per_dev):
        return lambda i, devs_ref: (devs_ref[m] * per_dev + i, 0)

    in_specs = [pl.BlockSpec((rb, l.shape[2]), block_of(m, n_steps)) for rb, l in zip(rbs, lands) for m in range(4)]
    in_specs += [pl.BlockSpec((4, rb, l.shape[2]), lambda i, devs_ref: (0, i, 0)) for rb, l in zip(rbs, lands)]
    out_shape, out_specs = [], []
    for rb, l in zip(rbs, lands):
        out_shape += [jax.ShapeDtypeStruct(l.shape[1:], F32), jax.ShapeDtypeStruct((3,) + l.shape[1:], BF16)]
        out_specs += [pl.BlockSpec((rb, l.shape[2]), lambda i, devs_ref: (i, 0)),
                      pl.BlockSpec((3, rb, l.shape[2]), lambda i, devs_ref: (0, i, 0))]
    grid_spec = pltpu.PrefetchScalarGridSpec(num_scalar_prefetch=1, grid=(n_steps,), in_specs=in_specs, out_specs=tuple(out_specs))
    res = _pcall(body, grid_spec=grid_spec, out_shape=tuple(out_shape), name=name,
                 compiler_params=_params(48, ("parallel",)))(devs, *[g for g in grads for _ in range(4)], *lands)
    return res[0::2], res[1::2]


def _final_sums(mine, lands, n_steps, name):
    n = len(mine)
    rbs = [o.shape[0] // n_steps for o in mine]

    def body(*refs):
        mine_refs, land_refs, outs = refs[:n], refs[n:2 * n], refs[2 * n:]
        for a in range(n):
            tot = mine_refs[a][...]
            for m in range(3):
                tot = tot + land_refs[a][m].astype(F32)
            outs[a][...] = tot

    in_specs = ([pl.BlockSpec((rb, o.shape[1]), lambda i: (i, 0)) for rb, o in zip(rbs, mine)]
                + [pl.BlockSpec((3, rb, o.shape[1]), lambda i: (0, i, 0)) for rb, o in zip(rbs, mine)])
    out_specs = tuple(pl.BlockSpec((rb, o.shape[1]), lambda i: (i, 0)) for rb, o in zip(rbs, mine))
    out_shape = tuple(jax.ShapeDtypeStruct(o.shape, F32) for o in mine)
    return _pcall(body, grid=(n_steps,), out_shape=out_shape, in_specs=in_specs, out_specs=out_specs, name=name,
                  compiler_params=_params(32, ("parallel",)))(*mine, *lands)


def _reduce_scatter_begin(big, small_after_start):
    c = lax.axis_index("c")
    devs = jnp.stack([_flat(*_chip(m), c) for m in range(4)]).astype(jnp.int32)
    flight, token = _start_copies(_sibling_copies, 4, [big], _sibling_zones([big]), "siblings_start")
    small = small_after_start(token)
    (big,), big_lands = _wait_copies(_sibling_copies, flight, small[-1], "siblings_wait")
    big_mine, big_send = _pair_sums(devs, [big], big_lands, 4, "pair_sums_w_in")
    big_flight, token = _start_copies(_chip_copies, 3, list(big_send), list(big_send), "chips_start_w_in")
    small_lands = _exchange_siblings(small, token, "exchange_siblings_rest")
    small_mine, small_send = _pair_sums(devs, small, small_lands, 1, "pair_sums_rest")
    small_flight, token = _start_copies(_chip_copies, 3 * len(small), list(small_send), list(small_send), "chips_start_rest")
    return (big_flight, small_flight, list(big_mine) + list(small_mine)), token


def _reduce_scatter_end(state, after):
    big_flight, small_flight, mine = state
    _, big_got = _wait_copies(_chip_copies, big_flight, after, "chips_wait_w_in")
    _, small_got = _wait_copies(_chip_copies, small_flight, after, "chips_wait_rest")
    small = _final_sums(mine[1:], small_got, 1, "final_sums_rest")
    return (mine[0], big_got[0]), list(small)


def _adamw(w, g, m, v):
    m_new = B1 * m + (1.0 - B1) * g
    v_new = B2 * v + (1.0 - B2) * (g * g)
    m_hat = m_new / (1.0 - B1 ** STEP)
    v_hat = v_new / (1.0 - B2 ** STEP)
    delta = -LR * (m_hat / (jnp.sqrt(v_hat) + EPS) + WD * w)
    return delta, m_new, v_new


def _final_sum_adam_rows(mine, land, w, m, v, n_steps, name):
    rows, ncol = w.shape
    blk = pl.BlockSpec((rows // n_steps, ncol), lambda i: (i, 0))

    def body(mine_ref, land_ref, w_ref, m_ref, v_ref, g_ref, d_ref, mo_ref, vo_ref):
        g = mine_ref[...]
        for k in range(3):
            g = g + land_ref[k].astype(F32)
        g_ref[...] = g
        d_ref[...], mo_ref[...], vo_ref[...] = _adamw(w_ref[...], g, m_ref[...], v_ref[...])

    shape = jax.ShapeDtypeStruct(w.shape, F32)
    return _pcall(body, grid=(n_steps,), out_shape=(shape,) * 4,
                  in_specs=[blk, pl.BlockSpec((3, rows // n_steps, ncol), lambda i: (0, i, 0)), blk, blk, blk],
                  out_specs=(blk,) * 4, name=name, compiler_params=_params(32, ("parallel",)))(mine, land, w, m, v)


def _adam_transposed(g_t, w, m, v, name):
    n, k = g_t.shape
    rb = min(k, 128)

    def body(gt_ref, w_ref, m_ref, v_ref, g_ref, d_ref, mo_ref, vo_ref):
        for src, skip, dst, size in _column_chunks(n):
            sl = pl.ds(dst, size)
            g = gt_ref[pl.ds(src, 128), :].T[:, skip:]
            delta, m_new, v_new = _adamw(w_ref[:, sl], g, m_ref[:, sl], v_ref[:, sl])
            g_ref[:, sl], d_ref[:, sl], mo_ref[:, sl], vo_ref[:, sl] = g, delta, m_new, v_new

    shape = jax.ShapeDtypeStruct(w.shape, F32)
    rows = pl.BlockSpec((rb, n), lambda i: (i, 0))
    return _pcall(body, grid=(k // rb,), out_shape=(shape,) * 4,
                  in_specs=[pl.BlockSpec((n, rb), lambda i: (0, i)), rows, rows, rows], out_specs=(rows,) * 4, name=name,
                  compiler_params=_params(32, ("parallel",)))(g_t, w, m, v)


def _adam_many(items, name):
    n = len(items)

    def body(*refs):
        ins, outs = refs[:4 * n], refs[4 * n:]
        for a in range(n):
            w_ref, g_ref, m_ref, v_ref = ins[4 * a:4 * a + 4]
            delta, m_new, v_new = _adamw(w_ref[...], g_ref[...], m_ref[...], v_ref[...])
            outs[3 * a][...], outs[3 * a + 1][...], outs[3 * a + 2][...] = delta, m_new, v_new

    out_shape = tuple(jax.ShapeDtypeStruct(it[0].shape, F32) for it in items for _ in range(3))
    flat = [arr for it in items for arr in it]
    res = _pcall(body, grid=(1,), out_shape=out_shape, in_specs=[_whole(a) for a in flat],
                 out_specs=tuple(_whole(o) for o in out_shape), name=name, compiler_params=_params(32))(*flat)
    return [tuple(res[3 * a:3 * a + 3]) for a in range(n)]


def _adam_w_ada(cact_all, dada_mine, w, m, v):
    def body(c_ref, d_ref, w_ref, m_ref, v_ref, g_ref, dl_ref, mo_ref, vo_ref):
        g = _tn(c_ref[...].astype(BF16), d_ref[...].astype(BF16))
        delta, m_new, v_new = _adamw(w_ref[...], g, m_ref[...], v_ref[...])
        g_ref[...], dl_ref[...], mo_ref[...], vo_ref[...] = g, delta, m_new, v_new

    shape = jax.ShapeDtypeStruct(w.shape, F32)
    operands = (cact_all, dada_mine, w, m, v)
    return _pcall(body, grid=(1,), out_shape=(shape,) * 4, in_specs=[_whole(a) for a in operands],
                  out_specs=(_whole(w),) * 4, name="adam_w_ada", compiler_params=_params(32))(*operands)


def kernel(x, c, w_ada, b_ada, w_in, b_in, conv_w, w_proj_attn, w_proj_conv, w_out, b_out, ln_g, ln_b, loss_target, m_w_ada, m_b_ada, m_w_in, m_b_in, m_conv_w, m_w_proj_attn, m_w_proj_conv, m_w_out, m_b_out, m_ln_g, m_ln_b, v_w_ada, v_b_ada, v_w_in, v_b_in, v_conv_w, v_w_proj_attn, v_w_proj_conv, v_w_out, v_b_out, v_ln_g, v_ln_b):
    nbat, seq, _ = x.shape
    t = nbat * seq
    me = _flat(*_my_position())
    x2, tgt2 = x.reshape(t, D), loss_target.reshape(t, D)
    sq = lambda a: a.reshape(a.shape[1:])

    tr = lambda a: a[0].T
    w_in_rows = tr(w_in)
    w_in_t_s, w_pa_t_s, w_pb_s, w_out_s, cact_s, cw_s = _prep(
        w_in_rows, sq(w_proj_attn), sq(w_proj_conv), sq(w_out), c, sq(conv_w))

    ncol = w_ada.shape[2]
    b_ada_mine = lax.dynamic_slice(b_ada, (0, me * ncol), (1, ncol))
    ada_slots, cact_slots, cw_slots = _ada_forward(cact_s, cw_s, sq(w_ada), b_ada_mine)
    cact_all = cact_slots[:, :nbat].reshape(N_DEV * nbat, D)
    cw = cw_slots[:, :3].transpose(1, 0, 2).reshape(3, D)
    ada_all = ada_slots[:, :, :nbat].transpose(1, 2, 0, 3).reshape(N_DEV * nbat, 3, D)
    ada = lax.dynamic_slice(ada_all, (me * nbat, 0, 0), (nbat, 3, D))

    w_in_t, qkv, rest, h, (w_pa_t, w_pb, w_o) = _project_gather(
        w_in_t_s, x2, ada, b_in.reshape(N_SLAB, 1, SLAB), [w_pa_t_s, w_pb_s, w_out_s])
    ol_tot = _attn_forward(qkv, nbat)
    (dproj, gx0, do_attn, merged, do_f, bbs, dyc, a_bf, dya, gb_rest, svec, dgate) = _mid(
        rest, ol_tot, x2, tgt2, ada, cw, b_out, ln_g, ln_b, w_pa_t, w_pb, w_o)

    gb_qkv = []
    for g in range(3):
        dproj, gb = _attn_backward(qkv, do_attn, ol_tot, dproj, g, nbat)
        gb_qkv.append(gb)
    g_w_in_t = _grad_w_in_t(dproj, h)

    def small_grads(token):
        g_w_out = _grad_rows_2d(merged, do_f, "grad_w_out", token)
        g_w_pb = _grad_rows_2d(bbs, dyc, "grad_w_proj_conv", g_w_out)
        g_w_pa_t = _grad_rows_2d(dya, a_bf, "grad_w_proj_attn", g_w_pb)
        return [g_w_out, g_w_pb, g_w_pa_t]

    rs_state, token = _reduce_scatter_begin(g_w_in_t, small_grads)
    grad_x, dss = _grad_h(dproj, w_in_t, gx0, x2, ada, token)

    rows8, tot, g_bada, g_bin, g_bout, g_lng, g_lnb = _small_reduce(gb_rest, gb_qkv, svec, dgate, dss)
    (g_in_mine, g_in_got), (g_out, g_pb, g_pa_t) = _reduce_scatter_end(rs_state, tot)
    loss = tot[0, P_LOSS]
    dada_all = rows8[:, 0, P_DADA:].reshape(N_DEV * nbat, 3 * D)
    dada_mine = lax.dynamic_slice(dada_all, (0, me * ncol), (N_DEV * nbat, ncol))

    g_in_t, d_win_t, nm_win_t, nv_win_t = _final_sum_adam_rows(g_in_mine, g_in_got, w_in_rows, tr(m_w_in), tr(v_w_in), 4, "adam_w_in")
    g_win, d_win, nm_win, nv_win = g_in_t.T, d_win_t.T, nm_win_t.T, nv_win_t.T
    g_wpa, d_wpa, nm_wpa, nv_wpa = _adam_transposed(g_pa_t, sq(w_proj_attn), sq(m_w_proj_attn), sq(v_w_proj_attn), "adam_w_proj_attn")
    g_wada, d_wada, nm_wada, nv_wada = _adam_w_ada(cact_all, dada_mine, sq(w_ada), sq(m_w_ada), sq(v_w_ada))
    g_conv = lax.dynamic_slice(tot[:, P_CONV:P_CONV + 3 * D].reshape(3, D), (0, me * cw_s.shape[1]), (3, cw_s.shape[1]))
    upd = _adam_many([
        (sq(w_proj_conv), g_pb, sq(m_w_proj_conv), sq(v_w_proj_conv)),
        (sq(w_out), g_out, sq(m_w_out), sq(v_w_out)),
        (b_ada, g_bada, m_b_ada, v_b_ada), (b_in, g_bin, m_b_in, v_b_in), (sq(conv_w), g_conv, sq(m_conv_w), sq(v_conv_w)),
        (b_out, g_bout, m_b_out, v_b_out), (ln_g, g_lng, m_ln_g, v_ln_g), (ln_b, g_lnb, m_ln_b, v_ln_b)], "adam_rest")
    (d_wpb, nm_wpb, nv_wpb), (d_wout, nm_wout, nv_wout), (d_bada, nm_bada, nv_bada), (d_bin, nm_bin, nv_bin), \
        (d_conv, nm_conv, nv_conv), (d_bout, nm_bout, nv_bout), (d_lng, nm_lng, nv_lng), (d_lnb, nm_lnb, nv_lnb) = upd

    ex = lambda a: a.reshape((1,) + a.shape)
    grads = [ex(g_wada), g_bada, ex(g_win), g_bin, ex(g_conv), ex(g_wpa), ex(g_pb), ex(g_out), g_bout, g_lng, g_lnb]
    deltas = [ex(d_wada), d_bada, ex(d_win), d_bin, ex(d_conv), ex(d_wpa), ex(d_wpb), ex(d_wout), d_bout, d_lng, d_lnb]
    new_m = [ex(nm_wada), nm_bada, ex(nm_win), nm_bin, ex(nm_conv), ex(nm_wpa), ex(nm_wpb), ex(nm_wout), nm_bout, nm_lng, nm_lnb]
    new_v = [ex(nv_wada), nv_bada, ex(nv_win), nv_bin, ex(nv_conv), ex(nv_wpa), ex(nv_wpb), ex(nv_wout), nv_bout, nv_lng, nv_lnb]
    return (loss, grad_x.reshape(x.shape), *grads, *deltas, *new_m, *new_v)
```
